```python
import math
import jax, jax.numpy as jnp
from jax import lax
import numpy as np

D_MODEL = 1024
BATCH = 8
SEQ = 4096
DEPTH = 1

EXPAND = 2
D_MIX = EXPAND * D_MODEL
D_CONV = D_MIX // 2
D_LRU = D_MIX - D_CONV
N_CONV_HEADS = 8
N_LRU_HEADS = 16
LRU_HEAD_DIM = D_LRU // N_LRU_HEADS
SHORT_CONV_WIDTH = 3
LRU_CONV_WIDTH = 4
RG_LRU_C = 8.0
RMS_EPS = 1e-6
IN_COLS = 4 * D_CONV + 2 * D_LRU

kernel_name = "hymba_shortconv_rglru_hybrid"


def rms_norm(x, g):
    xf = x.astype(jnp.float32)
    xf = xf * lax.rsqrt(jnp.mean(xf * xf, axis=-1, keepdims=True) + RMS_EPS)
    return xf.astype(x.dtype) * g


def headwise_rms_norm(y, n_heads, g):
    b, s, d = y.shape
    yh = y.reshape(b, s, n_heads, d // n_heads).astype(jnp.float32)
    yh = yh * lax.rsqrt(jnp.mean(yh * yh, axis=-1, keepdims=True) + RMS_EPS)
    return yh.reshape(b, s, d).astype(y.dtype) * g


def causal_depthwise_conv(u, w):
    k_width = w.shape[0]
    s = u.shape[1]
    up = jnp.pad(u, ((0, 0), (k_width - 1, 0), (0, 0)))
    out = up[:, 0:s, :] * w[0]
    for k in range(1, k_width):
        out = out + up[:, k:k + s, :] * w[k]
    return out


def short_conv_mixer(b_gate, c_gate, x_in, conv_w):
    return b_gate * causal_depthwise_conv(c_gate * x_in, conv_w)


def _lru_combine(left, right):
    a1, b1 = left
    a2, b2 = right
    return a1 * a2, a2 * b1 + b2


def rg_lru_mixer(x_in, conv_w, conv_b, w_a, b_a, w_i, b_i, lam):
    bsz, s, d = x_in.shape
    u = causal_depthwise_conv(x_in, conv_w) + conv_b
    uh = u.reshape(bsz, s, N_LRU_HEADS, LRU_HEAD_DIM)
    r = jax.nn.sigmoid(jnp.einsum('bshd,hde->bshe', uh, w_a).reshape(bsz, s, d) + b_a)
    i = jax.nn.sigmoid(jnp.einsum('bshd,hde->bshe', uh, w_i).reshape(bsz, s, d) + b_i)
    log_a = RG_LRU_C * r.astype(jnp.float32) * jax.nn.log_sigmoid(lam.astype(jnp.float32))
    a = jnp.exp(log_a)
    mult = jnp.sqrt(-jnp.expm1(2.0 * log_a))
    drive = mult * (i * u).astype(jnp.float32)
    _, h = lax.associative_scan(_lru_combine, (a, drive), axis=1)
    return h.astype(x_in.dtype)


def _fwd_setup_inputs(seed: int = 0) -> dict:
    key = jax.random.key(seed)
    ks = jax.random.split(key, 16)
    f32 = jnp.float32
    x = jax.random.normal(ks[0], (BATCH, SEQ, D_MODEL), f32)
    ln_g = 1.0 + 0.02 * jax.random.normal(ks[1], (D_MODEL,), f32)
    w_in = jax.random.normal(ks[2], (D_MODEL, IN_COLS), f32) * D_MODEL ** -0.5
    conv_w = jax.random.normal(ks[3], (SHORT_CONV_WIDTH, D_CONV), f32) * SHORT_CONV_WIDTH ** -0.5
    lru_conv_w = jax.random.normal(ks[4], (LRU_CONV_WIDTH, D_LRU), f32) * LRU_CONV_WIDTH ** -0.5
    lru_conv_b = 0.02 * jax.random.normal(ks[5], (D_LRU,), f32)
    w_a = jax.random.normal(ks[6], (N_LRU_HEADS, LRU_HEAD_DIM, LRU_HEAD_DIM), f32) * LRU_HEAD_DIM ** -0.5
    b_a = 0.02 * jax.random.normal(ks[7], (D_LRU,), f32)
    w_i = jax.random.normal(ks[8], (N_LRU_HEADS, LRU_HEAD_DIM, LRU_HEAD_DIM), f32) * LRU_HEAD_DIM ** -0.5
    b_i = 0.02 * jax.random.normal(ks[9], (D_LRU,), f32)
    a_init = jax.random.uniform(ks[10], (D_LRU,), f32, minval=0.9, maxval=0.999)
    lam = jnp.log(a_init) - jnp.log1p(-a_init)
    conv_out_g = 1.0 + 0.02 * jax.random.normal(ks[11], (D_CONV,), f32)
    lru_out_g = 1.0 + 0.02 * jax.random.normal(ks[12], (D_LRU,), f32)
    w_out = jax.random.normal(ks[13], (D_MIX, D_MODEL), f32) * D_MIX ** -0.5
    final_g = 1.0 + 0.02 * jax.random.normal(ks[14], (D_MODEL,), f32)
    return {"x": x, "ln_g": ln_g, "w_in": w_in, "conv_w": conv_w,
            "lru_conv_w": lru_conv_w, "lru_conv_b": lru_conv_b,
            "w_a": w_a, "b_a": b_a, "w_i": w_i, "b_i": b_i, "lam": lam,
            "conv_out_g": conv_out_g, "lru_out_g": lru_out_g,
            "w_out": w_out, "final_g": final_g}


def _fwd_reference(x, ln_g, w_in, conv_w, lru_conv_w, lru_conv_b, w_a, b_a, w_i, b_i,
              lam, conv_out_g, lru_out_g, w_out, final_g):
    h = x
    for _ in range(DEPTH):
        xn = rms_norm(h, ln_g)
        proj = jnp.einsum('bsd,de->bse', xn, w_in)
        splits = [D_CONV, 2 * D_CONV, 3 * D_CONV, 4 * D_CONV, 4 * D_CONV + D_LRU]
        b_gate, c_gate, x_conv, g_conv, x_lru, g_lru = jnp.split(proj, splits, axis=-1)
        y_conv = short_conv_mixer(b_gate, c_gate, x_conv, conv_w)
        y_conv = headwise_rms_norm(y_conv, N_CONV_HEADS, conv_out_g) * jax.nn.silu(g_conv)
        y_lru = rg_lru_mixer(x_lru, lru_conv_w, lru_conv_b, w_a, b_a, w_i, b_i, lam)
        y_lru = headwise_rms_norm(y_lru, N_LRU_HEADS, lru_out_g) * jax.nn.silu(g_lru)
        y = jnp.concatenate([y_conv, y_lru], axis=-1)
        h = h + jnp.einsum('bse,ed->bsd', y, w_out)
    return rms_norm(h, final_g)


import jax as _jax
import jax.numpy as _jnp

TWIN_FORMAT = 'train_step'
FWD_PARAMS = ['x', 'ln_g', 'w_in', 'conv_w', 'lru_conv_w', 'lru_conv_b', 'w_a', 'b_a', 'w_i', 'b_i', 'lam', 'conv_out_g', 'lru_out_g', 'w_out', 'final_g']
TWIN_WEIGHTS = ['ln_g', 'w_in', 'conv_w', 'lru_conv_w', 'lru_conv_b', 'w_a', 'b_a', 'w_i', 'b_i', 'lam', 'conv_out_g', 'lru_out_g', 'w_out', 'final_g']
TWIN_DIFF_INPUT = 'x'
TWIN_INPUTS = ['x', 'ln_g', 'w_in', 'conv_w', 'lru_conv_w', 'lru_conv_b', 'w_a', 'b_a', 'w_i', 'b_i', 'lam', 'conv_out_g', 'lru_out_g', 'w_out', 'final_g', 'loss_target', 'm_ln_g', 'm_w_in', 'm_conv_w', 'm_lru_conv_w', 'm_lru_conv_b', 'm_w_a', 'm_b_a', 'm_w_i', 'm_b_i', 'm_lam', 'm_conv_out_g', 'm_lru_out_g', 'm_w_out', 'm_final_g', 'v_ln_g', 'v_w_in', 'v_conv_w', 'v_lru_conv_w', 'v_lru_conv_b', 'v_w_a', 'v_b_a', 'v_w_i', 'v_b_i', 'v_lam', 'v_conv_out_g', 'v_lru_out_g', 'v_w_out', 'v_final_g']
TWIN_OUTPUTS = ['loss', 'grad_x', 'grad_ln_g', 'grad_w_in', 'grad_conv_w', 'grad_lru_conv_w', 'grad_lru_conv_b', 'grad_w_a', 'grad_b_a', 'grad_w_i', 'grad_b_i', 'grad_lam', 'grad_conv_out_g', 'grad_lru_out_g', 'grad_w_out', 'grad_final_g', 'delta_ln_g', 'delta_w_in', 'delta_conv_w', 'delta_lru_conv_w', 'delta_lru_conv_b', 'delta_w_a', 'delta_b_a', 'delta_w_i', 'delta_b_i', 'delta_lam', 'delta_conv_out_g', 'delta_lru_out_g', 'delta_w_out', 'delta_final_g', 'new_m_ln_g', 'new_m_w_in', 'new_m_conv_w', 'new_m_lru_conv_w', 'new_m_lru_conv_b', 'new_m_w_a', 'new_m_b_a', 'new_m_w_i', 'new_m_b_i', 'new_m_lam', 'new_m_conv_out_g', 'new_m_lru_out_g', 'new_m_w_out', 'new_m_final_g', 'new_v_ln_g', 'new_v_w_in', 'new_v_conv_w', 'new_v_lru_conv_w', 'new_v_lru_conv_b', 'new_v_w_a', 'new_v_b_a', 'new_v_w_i', 'new_v_b_i', 'new_v_lam', 'new_v_conv_out_g', 'new_v_lru_out_g', 'new_v_w_out', 'new_v_final_g']
TWIN_LEAF_KINDS = {'loss': 'loss', 'grad_x': 'grad_x', 'grad_ln_g': 'grad_w', 'grad_w_in': 'grad_w', 'grad_conv_w': 'grad_w', 'grad_lru_conv_w': 'grad_w', 'grad_lru_conv_b': 'grad_w', 'grad_w_a': 'grad_w', 'grad_b_a': 'grad_w', 'grad_w_i': 'grad_w', 'grad_b_i': 'grad_w', 'grad_lam': 'grad_w', 'grad_conv_out_g': 'grad_w', 'grad_lru_out_g': 'grad_w', 'grad_w_out': 'grad_w', 'grad_final_g': 'grad_w', 'delta_ln_g': 'delta_w', 'delta_w_in': 'delta_w', 'delta_conv_w': 'delta_w', 'delta_lru_conv_w': 'delta_w', 'delta_lru_conv_b': 'delta_w', 'delta_w_a': 'delta_w', 'delta_b_a': 'delta_w', 'delta_w_i': 'delta_w', 'delta_b_i': 'delta_w', 'delta_lam': 'delta_w', 'delta_conv_out_g': 'delta_w', 'delta_lru_out_g': 'delta_w', 'delta_w_out': 'delta_w', 'delta_final_g': 'delta_w', 'new_m_ln_g': 'new_m', 'new_m_w_in': 'new_m', 'new_m_conv_w': 'new_m', 'new_m_lru_conv_w': 'new_m', 'new_m_lru_conv_b': 'new_m', 'new_m_w_a': 'new_m', 'new_m_b_a': 'new_m', 'new_m_w_i': 'new_m', 'new_m_b_i': 'new_m', 'new_m_lam': 'new_m', 'new_m_conv_out_g': 'new_m', 'new_m_lru_out_g': 'new_m', 'new_m_w_out': 'new_m', 'new_m_final_g': 'new_m', 'new_v_ln_g': 'new_v', 'new_v_w_in': 'new_v', 'new_v_conv_w': 'new_v', 'new_v_lru_conv_w': 'new_v', 'new_v_lru_conv_b': 'new_v', 'new_v_w_a': 'new_v', 'new_v_b_a': 'new_v', 'new_v_w_i': 'new_v', 'new_v_b_i': 'new_v', 'new_v_lam': 'new_v', 'new_v_conv_out_g': 'new_v', 'new_v_lru_out_g': 'new_v', 'new_v_w_out': 'new_v', 'new_v_final_g': 'new_v'}


def _forward(args):
    return _fwd_reference(*[args[k] for k in FWD_PARAMS])


def _output_shape():
    out = _jax.eval_shape(lambda: _forward(_fwd_setup_inputs(0)))
    return out.shape, out.dtype

N_MICROBATCH = 1
ADAM_LR = 0.001
ADAM_B1 = 0.9
ADAM_B2 = 0.999
ADAM_EPS = 1e-08
ADAM_WD = 0.01
ADAM_STEP = 10
PER_EXAMPLE_BATCH_AXIS = {'x': 0, 'loss_target': 0}
SHARED_INPUTS = []
_WEIGHT_DTYPES = {'ln_g': _jnp.float32, 'w_in': _jnp.float32, 'conv_w': _jnp.float32, 'lru_conv_w': _jnp.float32, 'lru_conv_b': _jnp.float32, 'w_a': _jnp.float32, 'b_a': _jnp.float32, 'w_i': _jnp.float32, 'b_i': _jnp.float32, 'lam': _jnp.float32, 'conv_out_g': _jnp.float32, 'lru_out_g': _jnp.float32, 'w_out': _jnp.float32, 'final_g': _jnp.float32}
MOMENT_SCALE = {'ln_g': 1.628388e-01, 'w_in': 6.660486e-02, 'conv_w': 7.140444e-02, 'lru_conv_w': 7.176444e-02, 'lru_conv_b': 3.528719e-01, 'w_a': 1.780541e-02, 'b_a': 1.587273e-02, 'w_i': 3.128715e-02, 'b_i': 2.511032e-02, 'lam': 3.163907e-02, 'conv_out_g': 6.528589e-02, 'lru_out_g': 7.419389e-02, 'w_out': 9.048590e-02, 'final_g': 3.196254e+01}


def _to_microbatches(a, axis):
    t = _jnp.moveaxis(a, axis, 0)
    t = t.reshape((N_MICROBATCH, t.shape[0] // N_MICROBATCH) + t.shape[1:])
    return _jnp.moveaxis(t, 1, axis + 1)


def setup_inputs(seed: int = 0) -> dict:
    inp = _fwd_setup_inputs(seed)
    key = _jax.random.fold_in(_jax.random.key(seed), 7919)
    shape, _ = _output_shape()
    out = dict(inp)
    out["loss_target"] = _jax.random.normal(_jax.random.fold_in(key, 0), shape, _jnp.float32)
    for i, name in enumerate(TWIN_WEIGHTS):
        w = inp[name].astype(_jnp.float32)
        if MOMENT_SCALE is None:
            s = _jnp.sqrt(_jnp.mean(_jnp.square(w)) + 1e-30)
        else:
            s = MOMENT_SCALE[name]
        km, kv = _jax.random.split(_jax.random.fold_in(key, i + 1))
        out[name] = w
        out["m_" + name] = s * _jax.random.normal(km, w.shape, _jnp.float32)
        out["v_" + name] = (s * s) * _jax.random.uniform(kv, w.shape, _jnp.float32, 0.5, 1.5)
    if N_MICROBATCH > 1:
        for name, axis in PER_EXAMPLE_BATCH_AXIS.items():
            out[name] = _to_microbatches(out[name], axis)
    return {'x': out['x'], 'ln_g': out['ln_g'], 'w_in': out['w_in'], 'conv_w': out['conv_w'], 'lru_conv_w': out['lru_conv_w'], 'lru_conv_b': out['lru_conv_b'], 'w_a': out['w_a'], 'b_a': out['b_a'], 'w_i': out['w_i'], 'b_i': out['b_i'], 'lam': out['lam'], 'conv_out_g': out['conv_out_g'], 'lru_out_g': out['lru_out_g'], 'w_out': out['w_out'], 'final_g': out['final_g'], 'loss_target': out['loss_target'], 'm_ln_g': out['m_ln_g'], 'm_w_in': out['m_w_in'], 'm_conv_w': out['m_conv_w'], 'm_lru_conv_w': out['m_lru_conv_w'], 'm_lru_conv_b': out['m_lru_conv_b'], 'm_w_a': out['m_w_a'], 'm_b_a': out['m_b_a'], 'm_w_i': out['m_w_i'], 'm_b_i': out['m_b_i'], 'm_lam': out['m_lam'], 'm_conv_out_g': out['m_conv_out_g'], 'm_lru_out_g': out['m_lru_out_g'], 'm_w_out': out['m_w_out'], 'm_final_g': out['m_final_g'], 'v_ln_g': out['v_ln_g'], 'v_w_in': out['v_w_in'], 'v_conv_w': out['v_conv_w'], 'v_lru_conv_w': out['v_lru_conv_w'], 'v_lru_conv_b': out['v_lru_conv_b'], 'v_w_a': out['v_w_a'], 'v_b_a': out['v_b_a'], 'v_w_i': out['v_w_i'], 'v_b_i': out['v_b_i'], 'v_lam': out['v_lam'], 'v_conv_out_g': out['v_conv_out_g'], 'v_lru_out_g': out['v_lru_out_g'], 'v_w_out': out['v_w_out'], 'v_final_g': out['v_final_g']}


def _loss(weights, diff, rest, loss_target):
    with _jax.named_scope("forward"):
        args = {**rest, TWIN_DIFF_INPUT: diff, **{k: w.astype(_WEIGHT_DTYPES[k]) for k, w in weights.items()}}
        y = _forward(args)
    with _jax.named_scope("loss_head"):
        err = _jnp.square(y.astype(_jnp.float32) - loss_target)
        return 0.5 * _jnp.sum(_jnp.mean(err, axis=-1)) if err.ndim else 0.5 * err


def _adamw(w, g, m, v):
    m = ADAM_B1 * m + (1.0 - ADAM_B1) * g
    v = ADAM_B2 * v + (1.0 - ADAM_B2) * _jnp.square(g)
    m_hat = m / (1.0 - ADAM_B1 ** ADAM_STEP)
    v_hat = v / (1.0 - ADAM_B2 ** ADAM_STEP)
    delta = -ADAM_LR * (m_hat / (_jnp.sqrt(v_hat) + ADAM_EPS) + ADAM_WD * w)
    return delta, m, v


def reference(x, ln_g, w_in, conv_w, lru_conv_w, lru_conv_b, w_a, b_a, w_i, b_i, lam, conv_out_g, lru_out_g, w_out, final_g, loss_target, m_ln_g, m_w_in, m_conv_w, m_lru_conv_w, m_lru_conv_b, m_w_a, m_b_a, m_w_i, m_b_i, m_lam, m_conv_out_g, m_lru_out_g, m_w_out, m_final_g, v_ln_g, v_w_in, v_conv_w, v_lru_conv_w, v_lru_conv_b, v_w_a, v_b_a, v_w_i, v_b_i, v_lam, v_conv_out_g, v_lru_out_g, v_w_out, v_final_g):
    given = dict(x=x, ln_g=ln_g, w_in=w_in, conv_w=conv_w, lru_conv_w=lru_conv_w, lru_conv_b=lru_conv_b, w_a=w_a, b_a=b_a, w_i=w_i, b_i=b_i, lam=lam, conv_out_g=conv_out_g, lru_out_g=lru_out_g, w_out=w_out, final_g=final_g, loss_target=loss_target, m_ln_g=m_ln_g, m_w_in=m_w_in, m_conv_w=m_conv_w, m_lru_conv_w=m_lru_conv_w, m_lru_conv_b=m_lru_conv_b, m_w_a=m_w_a, m_b_a=m_b_a, m_w_i=m_w_i, m_b_i=m_b_i, m_lam=m_lam, m_conv_out_g=m_conv_out_g, m_lru_out_g=m_lru_out_g, m_w_out=m_w_out, m_final_g=m_final_g, v_ln_g=v_ln_g, v_w_in=v_w_in, v_conv_w=v_conv_w, v_lru_conv_w=v_lru_conv_w, v_lru_conv_b=v_lru_conv_b, v_w_a=v_w_a, v_b_a=v_b_a, v_w_i=v_w_i, v_b_i=v_b_i, v_lam=v_lam, v_conv_out_g=v_conv_out_g, v_lru_out_g=v_lru_out_g, v_w_out=v_w_out, v_final_g=v_final_g)
    weights = {n: given[n] for n in TWIN_WEIGHTS}
    shared = {n: given[n] for n in SHARED_INPUTS}
    per_example = {n: given[n] for n in ['x']}
    grad_fn = _jax.value_and_grad(_loss, argnums=(0, 1))

    def one_microbatch(ex, loss_target):
        ex = dict(ex)
        diff = ex.pop(TWIN_DIFF_INPUT)
        return grad_fn(weights, diff, {**shared, **ex}, loss_target)

    if N_MICROBATCH == 1:
        loss, (grad_w, grad_x) = one_microbatch(per_example, given["loss_target"])
    else:
        def body(carry, xs):
            loss_sum, grad_sum = carry
            l_k, (gw_k, gx_k) = one_microbatch(xs[0], xs[1])
            with _jax.named_scope("update"):
                return (loss_sum + l_k, _jax.tree.map(_jnp.add, grad_sum, gw_k)), gx_k

        init = (_jnp.zeros((), _jnp.float32), _jax.tree.map(_jnp.zeros_like, weights))
        (loss, grad_w), grad_x = _jax.lax.scan(body, init, (per_example, given["loss_target"]))
    with _jax.named_scope("update"):
        delta_w, new_m, new_v = {}, {}, {}
        for n in TWIN_WEIGHTS:
            delta_w[n], new_m[n], new_v[n] = _adamw(weights[n], grad_w[n], given["m_" + n], given["v_" + n])
    return (loss, grad_x, *[grad_w[n] for n in TWIN_WEIGHTS], *[delta_w[n] for n in TWIN_WEIGHTS],
            *[new_m[n] for n in TWIN_WEIGHTS], *[new_v[n] for n in TWIN_WEIGHTS])
```

```python
import functools
import math

import jax
import jax.numpy as jnp
from jax import lax
from jax.experimental import pallas as pl
from jax.experimental.pallas import tpu as pltpu

F32 = jnp.float32
MM = jnp.bfloat16
MESH = pl.DeviceIdType.MESH

N_DEV = 8
N_CONV_HEADS = 8
N_LRU_HEADS = 16
RG_LRU_C = 8.0
RMS_EPS = 1e-6
ADAM_LR, ADAM_B1, ADAM_B2, ADAM_EPS, ADAM_WD, ADAM_STEP = 0.001, 0.9, 0.999, 1e-08, 0.01, 10
ADAM_BC1 = 1.0 - ADAM_B1 ** ADAM_STEP
ADAM_BC2 = 1.0 - ADAM_B2 ** ADAM_STEP

SUBLANES = 8
LANES = 128
MXU_TILE = 256
VMEM_LIMIT = 56 * 1024 * 1024

SP_LN_G, SP_LRU_B, SP_B_A, SP_B_I, SP_LAM, SP_CONV_G, SP_LRU_G, SP_FINAL_G, SP_CONV_W, SP_LRU_W = 0, 1, 2, 3, 4, 5, 6, 7, 8, 11
SP_ROWS = 16
A_CONV_G, A_LRU_G, A_LAM, A_B_A, A_B_I, A_CONV_W, A_LRU_W, A_LRU_B = 0, 1, 2, 3, 4, 5, 8, 12
A_GROUPS = 13
SL_LOSS, SL_LN_G, SL_LRU_B, SL_B_A, SL_B_I, SL_LAM, SL_CONV_G, SL_LRU_G, SL_FINAL_G, SL_CONV_W, SL_LRU_W, SL_W_A = 0, 1, 2, 3, 4, 5, 6, 7, 8, 9, 12, 16


def _params(vmem=True, **kw):
    if vmem:
        kw["vmem_limit_bytes"] = VMEM_LIMIT
    return pltpu.CompilerParams(**kw)


def _dot(a, b):
    return jnp.dot(a, b, preferred_element_type=F32)


def _dot_nt(a, b):
    return lax.dot_general(a, b, (((1,), (1,)), ((), ())), preferred_element_type=F32)


def _split3(v):
    hi = v.astype(MM)
    r1 = v - hi.astype(F32)
    mid = r1.astype(MM)
    lo = (r1 - mid.astype(F32)).astype(MM)
    return hi, mid, lo


def _dot3(v, e):
    hi, mid, lo = _split3(v)
    return _dot(hi, e) + _dot(mid, e) + _dot(lo, e)


def _head_rstd(v, e, et, head_dim):
    ms = _dot3(v * v, e) * (1.0 / head_dim)
    return _dot3(lax.rsqrt(ms + RMS_EPS), et)


def _head_mean(v, e, et, head_dim):
    return _dot3(_dot3(v, e) * (1.0 / head_dim), et)


def _neg_expm1(x):
    series = -x * (1.0 + x * (0.5 + x * (1.0 / 6.0 + x * (1.0 / 24.0 + x * (1.0 / 120.0)))))
    return jnp.where(x > -0.03, series, 1.0 - jnp.exp(x))


def _log_sigmoid(x):
    z = jnp.exp(-jnp.abs(x))
    u = 1.0 + z
    log1p_z = jnp.where(u == 1.0, z, jnp.log(u) * (z / (u - 1.0)))
    return jnp.minimum(x, 0.0) - log1p_z


def _row_iota(d):
    return lax.broadcasted_iota(jnp.int32, (SUBLANES, d), 0)


def _shift_down(cur, prev, s, row):
    return jnp.where(row >= s, pltpu.roll(cur, s, axis=0), pltpu.roll(prev, s, axis=0))


def _shift_up(cur, nxt, s, row):
    k = SUBLANES - s
    return jnp.where(row < k, pltpu.roll(cur, k, axis=0), pltpu.roll(nxt, k, axis=0))


def _scan_fwd(a, b, h_prev, row):
    for s in (1, 2, 4):
        a_s = jnp.where(row >= s, pltpu.roll(a, s, axis=0), 1.0)
        b_s = jnp.where(row >= s, pltpu.roll(b, s, axis=0), 0.0)
        b = a * b_s + b
        a = a * a_s
    return a * h_prev + b


def _scan_bwd(a_next, b, g_next, row):
    a = a_next
    for s in (1, 2, 4):
        k = SUBLANES - s
        a_s = jnp.where(row < k, pltpu.roll(a, k, axis=0), 1.0)
        b_s = jnp.where(row < k, pltpu.roll(b, k, axis=0), 0.0)
        b = a * b_s + b
        a = a * a_s
    return a * g_next + b


def _bcast_row(v, r):
    return jnp.broadcast_to(v[r:r + 1, :], v.shape)


def _chunks(n_rows, rc, body, init, reverse=False):
    n = n_rows // rc

    def step(i, carry):
        j = (n - 1 - i) if reverse else i
        return body(pl.multiple_of(j * rc, rc), carry)

    return lax.fori_loop(0, n, step, init)


def _adamw(w, g, m, v):
    m = ADAM_B1 * m + (1.0 - ADAM_B1) * g
    v = ADAM_B2 * v + (1.0 - ADAM_B2) * (g * g)
    m_hat = m / ADAM_BC1
    v_hat = v / ADAM_BC2
    delta = -ADAM_LR * (m_hat / (jnp.sqrt(v_hat) + ADAM_EPS) + ADAM_WD * w)
    return delta, m, v


def _mesh_pos():
    return lax.axis_index("x"), lax.axis_index("y"), lax.axis_index("c")


def _two_level_gather(blocks_of, n_arrays, send_sems, recv_sems):
    x, y, c = _mesh_pos()
    me, sibling = (x, y, c), (x, y, 1 - c)
    chips = [(1 - x, y), (x, 1 - y), (1 - x, 1 - y)]

    def copy(a, k, block, to):
        return pltpu.make_async_remote_copy(
            src_ref=blocks_of(a, *block), dst_ref=blocks_of(a, *block),
            send_sem=send_sems.at[a * 7 + k], recv_sem=recv_sems.at[a * 7 + k],
            device_id=to, device_id_type=MESH)

    started = []
    for a in range(n_arrays):
        started.append(copy(a, 0, me, sibling))
        started += [copy(a, 1 + j, me, (*chip, c)) for j, chip in enumerate(chips)]
    for cp in started:
        cp.start()
    for j, chip in enumerate(chips):
        for a in range(n_arrays):
            copy(a, 1 + j, (*chip, c), me).wait_recv()
            passed = copy(a, 4 + j, (*chip, c), sibling)
            passed.start()
            started.append(passed)
    for a in range(n_arrays):
        copy(a, 0, sibling, me).wait_recv()
        for j, chip in enumerate(chips):
            copy(a, 4 + j, (*chip, 1 - c), me).wait_recv()
    for cp in started:
        cp.wait_send()


def _gather_weights(w_in, w_out, conv_pack):
    srcs = (w_in, w_out, conv_pack)
    dts = (MM, MM, F32)

    def body(win_ref, wout_ref, cp_ref, win_all, wout_all, cp_all, send_sems, recv_sems):
        x, y, c = _mesh_pos()
        me = 4 * x + 2 * y + c
        outs = (win_all, wout_all, cp_all)
        for src, dst in zip((win_ref, wout_ref, cp_ref), outs):
            rows = src.shape[0]
            rc = min(rows, 32)

            def cast(r, carry, src=src, dst=dst, rc=rc):
                dst[me, pl.ds(r, rc), :] = src[pl.ds(r, rc), :].astype(dst.dtype)
                return carry

            _chunks(rows, rc, cast, 0)

        def blocks_of(a, px, py, pc):
            return outs[a].at[4 * px + 2 * py + pc]

        _two_level_gather(blocks_of, 3, send_sems, recv_sems)

    vm = pl.BlockSpec(memory_space=pltpu.VMEM)
    return pl.pallas_call(
        body, name="gather_weights",
        out_shape=tuple(jax.ShapeDtypeStruct((N_DEV,) + s.shape, dt) for s, dt in zip(srcs, dts)),
        in_specs=[vm, vm, vm], out_specs=(vm, vm, vm),
        scratch_shapes=[pltpu.SemaphoreType.DMA((21,)), pltpu.SemaphoreType.DMA((21,))],
        compiler_params=_params(),
    )(*srcs)


def _forward(x, tgt, win_all, wout, wa_t, wi_t, sp, e_c, et_c, e_l, et_l, tb):
    t_len, d = x.shape
    nb = t_len // tb
    cols = win_all.shape[2]
    pw = math.gcd(d, cols)
    ppb = cols // pw
    n_pieces = 6 * d // pw
    n_tiles, tw = wa_t.shape[0], wa_t.shape[1]
    hd_c, hd_l = d // N_CONV_HEADS, d // N_LRU_HEADS

    def body(x_ref, tgt_ref, win_ref, wout_ref, wa_ref, wi_ref, sp_ref, ec_ref, etc_ref, el_ref, etl_ref,
             p_ref, h_ref, dh_ref, dhb_ref, xnt_ref, acc_ref,
             pf, xnb, yc, u, pa, pi, rcf, rlf, ybuf, tail_z, tail_xl, hcar):
        i = pl.program_id(0)
        row = _row_iota(d)

        @pl.when(i == 0)
        def _():
            tail_z[...] = jnp.zeros_like(tail_z)
            tail_xl[...] = jnp.zeros_like(tail_xl)
            hcar[...] = jnp.zeros_like(hcar)
            acc_ref[...] = jnp.zeros_like(acc_ref)

        def spr(r):
            return sp_ref[r:r + 1, :]

        xv = x_ref[...]
        r0 = lax.rsqrt(jnp.mean(xv * xv, axis=-1, keepdims=True) + RMS_EPS)
        xn = xv * r0 * spr(SP_LN_G)
        xnb[...] = xn.astype(MM)
        xnt_ref[...] = xn.T.astype(MM)

        for p in range(n_pieces):
            w = win_ref[p // ppb, :, (p % ppb) * pw:(p % ppb + 1) * pw]
            pb = _dot(xnb[...], w).astype(MM)
            p_ref[:, p * pw:(p + 1) * pw] = pb
            pf[:, p * pw:(p + 1) * pw] = pb.astype(F32)

        w0, w1, w2 = spr(SP_CONV_W), spr(SP_CONV_W + 1), spr(SP_CONV_W + 2)
        l0, l1, l2, l3 = spr(SP_LRU_W), spr(SP_LRU_W + 1), spr(SP_LRU_W + 2), spr(SP_LRU_W + 3)
        lb = spr(SP_LRU_B)

        def convs(r, carry):
            zp, xp = carry
            rows = pl.ds(r, SUBLANES)
            z = pf[rows, d:2 * d] * pf[rows, 2 * d:3 * d]
            cz = w0 * _shift_down(z, zp, 2, row) + w1 * _shift_down(z, zp, 1, row) + w2 * z
            yc[rows, :] = pf[rows, 0:d] * cz
            xl = pf[rows, 4 * d:5 * d]
            u[rows, :] = (l0 * _shift_down(xl, xp, 3, row) + l1 * _shift_down(xl, xp, 2, row)
                          + l2 * _shift_down(xl, xp, 1, row) + l3 * xl + lb)
            return z, xl

        z_last, xl_last = _chunks(tb, SUBLANES, convs, (tail_z[...], tail_xl[...]))
        tail_z[...] = z_last
        tail_xl[...] = xl_last

        ub = u[...].astype(MM)
        for k in range(n_tiles):
            sl = slice(k * tw, (k + 1) * tw)
            pa[:, sl] = _dot(ub[:, sl], wa_ref[k])
            pi[:, sl] = _dot(ub[:, sl], wi_ref[k])
        rcf[...] = _head_rstd(yc[...], ec_ref[...], etc_ref[...], hd_c)

        c8 = RG_LRU_C * _log_sigmoid(spr(SP_LAM))
        b_a, b_i = spr(SP_B_A), spr(SP_B_I)

        def lru(r, hp):
            rows = pl.ds(r, SUBLANES)
            ra = jax.nn.sigmoid(pa[rows, :] + b_a)
            ii = jax.nn.sigmoid(pi[rows, :] + b_i)
            la = ra * c8
            a = jnp.exp(la)
            mult = jnp.sqrt(_neg_expm1(2.0 * la))
            h = _scan_fwd(a, mult * (ii * u[rows, :]), hp, row)
            h_ref[rows, :] = h
            return _bcast_row(h, SUBLANES - 1)

        hcar[...] = _chunks(tb, SUBLANES, lru, hcar[...])
        rlf[...] = _head_rstd(h_ref[...], el_ref[...], etl_ref[...], hd_l)

        g_c, g_l = spr(SP_CONV_G), spr(SP_LRU_G)

        def gate(r, carry):
            rows = pl.ds(r, SUBLANES)
            gc = pf[rows, 3 * d:4 * d]
            ybuf[rows, 0:d] = yc[rows, :] * rcf[rows, :] * g_c * (gc * jax.nn.sigmoid(gc))
            gl = pf[rows, 5 * d:6 * d]
            ybuf[rows, d:2 * d] = h_ref[rows, :] * rlf[rows, :] * g_l * (gl * jax.nn.sigmoid(gl))
            return carry

        _chunks(tb, SUBLANES, gate, 0)

        hres = x_ref[...] + _dot(ybuf[...].astype(MM), wout_ref[...])
        rf = lax.rsqrt(jnp.mean(hres * hres, axis=-1, keepdims=True) + RMS_EPS)
        hn = hres * rf
        fg = spr(SP_FINAL_G)
        err = hn * fg - tgt_ref[...]
        dout = err * (1.0 / d)
        acc_ref[0:SUBLANES, :] += (err * err).reshape(tb // SUBLANES, SUBLANES, d).sum(axis=0)
        acc_ref[SUBLANES:2 * SUBLANES, :] += (dout * hn).reshape(tb // SUBLANES, SUBLANES, d).sum(axis=0)
        gd = dout * fg
        dhres = rf * (gd - hn * jnp.mean(gd * hn, axis=-1, keepdims=True))
        dh_ref[...] = dhres
        dhb_ref[...] = dhres.astype(MM)

    vm = pl.BlockSpec(memory_space=pltpu.VMEM)
    blk = lambda w: pl.BlockSpec((tb, w), lambda i: (i, 0))
    buf = lambda w: pltpu.VMEM((tb, w), F32)
    return pl.pallas_call(
        body, name="forward", grid=(nb,),
        in_specs=[blk(d), blk(d), vm, vm, vm, vm, vm, vm, vm, vm, vm],
        out_specs=(blk(6 * d), blk(d), blk(d), blk(d),
                   pl.BlockSpec((d, tb), lambda i: (0, i)),
                   pl.BlockSpec((2 * SUBLANES, d), lambda i: (0, 0))),
        out_shape=(jax.ShapeDtypeStruct((t_len, 6 * d), MM),
                   jax.ShapeDtypeStruct((t_len, d), F32),
                   jax.ShapeDtypeStruct((t_len, d), F32),
                   jax.ShapeDtypeStruct((t_len, d), MM),
                   jax.ShapeDtypeStruct((d, t_len), MM),
                   jax.ShapeDtypeStruct((2 * SUBLANES, d), F32)),
        scratch_shapes=[buf(6 * d), pltpu.VMEM((tb, d), MM), buf(d), buf(d), buf(d), buf(d), buf(d), buf(d), buf(2 * d),
                        pltpu.VMEM((SUBLANES, d), F32), pltpu.VMEM((SUBLANES, d), F32), pltpu.VMEM((SUBLANES, d), F32)],
        compiler_params=_params(dimension_semantics=("arbitrary",)),
    )(x, tgt, win_all, wout, wa_t, wi_t, sp, e_c, et_c, e_l, et_l)


def _backward(p, h, dh, wout, wa_t, wi_t, sp, e_c, et_c, e_l, et_l, tb):
    t_len, d = h.shape
    nb = t_len // tb
    n_tiles, tw = wa_t.shape[0], wa_t.shape[1]
    hd_c, hd_l = d // N_CONV_HEADS, d // N_LRU_HEADS
    ph = 16
    s8 = SUBLANES

    def body(p_ref, phalo_ref, h_ref, hhalo_ref, dh_ref, wout_ref, wa_ref, wi_ref, sp_ref,
             ec_ref, etc_ref, el_ref, etl_ref,
             dp_ref, yt_ref, gwa_ref, gwi_ref, acc_ref,
             pf, hh, dy, ybuf, yc, czs, u, pa, pi, rcf, rlf, qc, ql, dyc_hat, dyl_hat, dpa, dpi, du, dpf,
             car_dcz, car_a, car_g, car_du):
        i = pl.program_id(0)
        blk_idx = nb - 1 - i
        row = _row_iota(d)

        @pl.when(i == 0)
        def _():
            for ref in (car_dcz, car_a, car_g, car_du, gwa_ref, gwi_ref, acc_ref):
                ref[...] = jnp.zeros_like(ref)

        def spr(r):
            return sp_ref[r:r + 1, :]

        def acc_add(group, val):
            acc_ref[group * s8:(group + 1) * s8, :] += val

        live = jnp.where(blk_idx > 0, 1.0, 0.0).astype(F32)
        pf[0:ph, :] = phalo_ref[...].astype(F32) * live
        hh[0:s8, :] = hhalo_ref[...] * live
        hh[s8:, :] = h_ref[...]

        def widen(r, carry):
            pf[pl.ds(ph + r, 16), :] = p_ref[pl.ds(r, 16), :].astype(F32)
            return carry

        _chunks(tb, 16, widen, 0)

        dy[...] = _dot_nt(dh_ref[...].astype(MM), wout_ref[...])

        w0, w1, w2 = spr(SP_CONV_W), spr(SP_CONV_W + 1), spr(SP_CONV_W + 2)
        l0, l1, l2, l3 = spr(SP_LRU_W), spr(SP_LRU_W + 1), spr(SP_LRU_W + 2), spr(SP_LRU_W + 3)
        lb = spr(SP_LRU_B)

        def zx(r):
            rows = pl.ds(ph + r, s8)
            return pf[rows, d:2 * d] * pf[rows, 2 * d:3 * d], pf[rows, 4 * d:5 * d]

        def recompute(r, carry):
            rows = pl.ds(r, s8)
            z, xl = zx(r)
            zp, xp = zx(r - s8)
            cz = w0 * _shift_down(z, zp, 2, row) + w1 * _shift_down(z, zp, 1, row) + w2 * z
            czs[rows, :] = cz
            yc[rows, :] = pf[pl.ds(ph + r, s8), 0:d] * cz
            u[rows, :] = (l0 * _shift_down(xl, xp, 3, row) + l1 * _shift_down(xl, xp, 2, row)
                          + l2 * _shift_down(xl, xp, 1, row) + l3 * xl + lb)
            return carry

        _chunks(tb, s8, recompute, 0)

        ub = u[...].astype(MM)
        for k in range(n_tiles):
            sl = slice(k * tw, (k + 1) * tw)
            pa[:, sl] = _dot(ub[:, sl], wa_ref[k])
            pi[:, sl] = _dot(ub[:, sl], wi_ref[k])
        rcf[...] = _head_rstd(yc[...], ec_ref[...], etc_ref[...], hd_c)
        rlf[...] = _head_rstd(h_ref[...], el_ref[...], etl_ref[...], hd_l)

        g_c, g_l = spr(SP_CONV_G), spr(SP_LRU_G)

        def gates(r, carry):
            rows = pl.ds(r, s8)
            prow = pl.ds(ph + r, s8)
            for (off_g, off_y, src, rstd, gain, q, dhat, grp) in (
                    (3 * d, 0, yc, rcf, g_c, qc, dyc_hat, A_CONV_G),
                    (5 * d, d, h_ref, rlf, g_l, ql, dyl_hat, A_LRU_G)):
                gt = pf[prow, off_g:off_g + d]
                sg = jax.nn.sigmoid(gt)
                silu = gt * sg
                yhat = src[rows, :] * rstd[rows, :]
                nrm = yhat * gain
                ybuf[rows, off_y:off_y + d] = nrm * silu
                dout = dy[rows, off_y:off_y + d]
                dnrm = dout * silu
                dpf[rows, off_g:off_g + d] = dout * nrm * (sg * (1.0 + gt * (1.0 - sg)))
                acc_add(grp, dnrm * yhat)
                dh_ = dnrm * gain
                dhat[rows, :] = dh_
                q[rows, :] = dh_ * yhat
            return carry

        _chunks(tb, s8, gates, 0)

        qc[...] = _head_mean(qc[...], ec_ref[...], etc_ref[...], hd_c)
        ql[...] = _head_mean(ql[...], el_ref[...], etl_ref[...], hd_l)
        yt_ref[...] = ybuf[...].T.astype(MM)

        c8 = RG_LRU_C * _log_sigmoid(spr(SP_LAM))
        b_a, b_i = spr(SP_B_A), spr(SP_B_I)

        def mixers(r, carry):
            dcz_n, a_n, g_n = carry
            rows = pl.ds(r, s8)
            prow = pl.ds(ph + r, s8)
            rstd = rcf[rows, :]
            yhat = yc[rows, :] * rstd
            dyc = rstd * (dyc_hat[rows, :] - yhat * qc[rows, :])
            dpf[rows, 0:d] = dyc * czs[rows, :]
            dcz = dyc * pf[prow, 0:d]
            dz = w2 * dcz + w1 * _shift_up(dcz, dcz_n, 1, row) + w0 * _shift_up(dcz, dcz_n, 2, row)
            z, _ = zx(r)
            zp, _ = zx(r - s8)
            dpf[rows, d:2 * d] = dz * pf[prow, 2 * d:3 * d]
            dpf[rows, 2 * d:3 * d] = dz * pf[prow, d:2 * d]
            acc_add(A_CONV_W, dcz * _shift_down(z, zp, 2, row))
            acc_add(A_CONV_W + 1, dcz * _shift_down(z, zp, 1, row))
            acc_add(A_CONV_W + 2, dcz * z)
            rstd = rlf[rows, :]
            hcur = hh[pl.ds(s8 + r, s8), :]
            hhat = hcur * rstd
            dh_out = rstd * (dyl_hat[rows, :] - hhat * ql[rows, :])
            ra = jax.nn.sigmoid(pa[rows, :] + b_a)
            la = ra * c8
            a = jnp.exp(la)
            g = _scan_bwd(_shift_up(a, a_n, 1, row), dh_out, g_n, row)
            da = g * _shift_down(hcur, hh[pl.ds(r, s8), :], 1, row)
            ii = jax.nn.sigmoid(pi[rows, :] + b_i)
            uu = u[rows, :]
            mult = jnp.sqrt(_neg_expm1(2.0 * la))
            dmult = g * (ii * uu)
            ds = g * mult
            dla = a * (da - dmult * a / mult)
            acc_add(A_LAM, dla * ra)
            dpa_ = dla * c8 * ra * (1.0 - ra)
            dpi_ = ds * uu * ii * (1.0 - ii)
            acc_add(A_B_A, dpa_)
            acc_add(A_B_I, dpi_)
            dpa[rows, :] = dpa_
            dpi[rows, :] = dpi_
            du[rows, :] = ds * ii
            return dcz, a, _bcast_row(g, 0)

        dcz_f, a_f, g_f = _chunks(tb, s8, mixers, (car_dcz[...], car_a[...], car_g[...]), reverse=True)
        car_dcz[...] = dcz_f
        car_a[...] = a_f
        car_g[...] = g_f

        dpab = dpa[...].astype(MM)
        dpib = dpi[...].astype(MM)
        for k in range(n_tiles):
            sl = slice(k * tw, (k + 1) * tw)
            du[:, sl] += _dot_nt(dpab[:, sl], wa_ref[k]) + _dot_nt(dpib[:, sl], wi_ref[k])
            ut = u[:, sl].T.astype(MM)
            gwa_ref[k] += _dot(ut, dpab[:, sl])
            gwi_ref[k] += _dot(ut, dpib[:, sl])

        def lru_conv(r, du_n):
            rows = pl.ds(r, s8)
            dut = du[rows, :]
            dpf[rows, 4 * d:5 * d] = (l3 * dut + l2 * _shift_up(dut, du_n, 1, row)
                                      + l1 * _shift_up(dut, du_n, 2, row) + l0 * _shift_up(dut, du_n, 3, row))
            _, xl = zx(r)
            _, xp = zx(r - s8)
            acc_add(A_LRU_W, dut * _shift_down(xl, xp, 3, row))
            acc_add(A_LRU_W + 1, dut * _shift_down(xl, xp, 2, row))
            acc_add(A_LRU_W + 2, dut * _shift_down(xl, xp, 1, row))
            acc_add(A_LRU_W + 3, dut * xl)
            acc_add(A_LRU_B, dut)
            return dut

        car_du[...] = _chunks(tb, s8, lru_conv, car_du[...], reverse=True)

        def narrow(r, carry):
            dp_ref[pl.ds(r, 16), :] = dpf[pl.ds(r, 16), :].astype(MM)
            return carry

        _chunks(tb, 16, narrow, 0)

    vm = pl.BlockSpec(memory_space=pltpu.VMEM)
    rev = lambda w: pl.BlockSpec((tb, w), lambda i: (nb - 1 - i, 0))
    halo = lambda rows, w: pl.BlockSpec((rows, w), lambda i: (jnp.maximum((nb - 1 - i) * (tb // rows) - 1, 0), 0))
    const = lambda shape: pl.BlockSpec(shape, lambda i: (0,) * len(shape))
    buf = lambda w: pltpu.VMEM((tb, w), F32)
    car = pltpu.VMEM((SUBLANES, d), F32)
    return pl.pallas_call(
        body, name="backward", grid=(nb,),
        in_specs=[rev(6 * d), halo(ph, 6 * d), rev(d), halo(SUBLANES, d), rev(d), vm, vm, vm, vm, vm, vm, vm, vm],
        out_specs=(rev(6 * d), pl.BlockSpec((2 * d, tb), lambda i: (0, nb - 1 - i)),
                   const((n_tiles, tw, tw)), const((n_tiles, tw, tw)), const((A_GROUPS * SUBLANES, d))),
        out_shape=(jax.ShapeDtypeStruct((t_len, 6 * d), MM),
                   jax.ShapeDtypeStruct((2 * d, t_len), MM),
                   jax.ShapeDtypeStruct((n_tiles, tw, tw), F32),
                   jax.ShapeDtypeStruct((n_tiles, tw, tw), F32),
                   jax.ShapeDtypeStruct((A_GROUPS * SUBLANES, d), F32)),
        scratch_shapes=[pltpu.VMEM((ph + tb, 6 * d), F32), pltpu.VMEM((SUBLANES + tb, d), F32), buf(2 * d), buf(2 * d)]
                       + [buf(d)] * 14 + [buf(6 * d), car, car, car, car],
        compiler_params=_params(dimension_semantics=("arbitrary",)),
    )(p, p, h, h, dh, wout, wa_t, wi_t, sp, e_c, et_c, e_l, et_l)


def _input_grad(dp, win_all, x, dh, sp, tb):
    t_len, d = x.shape
    nb = t_len // tb
    cols = win_all.shape[2]

    def body(dp_ref, win_ref, x_ref, dh_ref, sp_ref, gx_ref, acc_ref):
        @pl.when(pl.program_id(0) == 0)
        def _():
            acc_ref[...] = jnp.zeros_like(acc_ref)

        dxn = _dot_nt(dp_ref[:, 0:cols], win_ref[0])
        for j in range(1, N_DEV):
            dxn += _dot_nt(dp_ref[:, j * cols:(j + 1) * cols], win_ref[j])
        xv = x_ref[...]
        r0 = lax.rsqrt(jnp.mean(xv * xv, axis=-1, keepdims=True) + RMS_EPS)
        xhat = xv * r0
        acc_ref[...] += (dxn * xhat).reshape(tb // SUBLANES, SUBLANES, d).sum(axis=0)
        dxh = dxn * sp_ref[SP_LN_G:SP_LN_G + 1, :]
        gx_ref[...] = dh_ref[...] + r0 * (dxh - xhat * jnp.mean(dxh * xhat, axis=-1, keepdims=True))

    vm = pl.BlockSpec(memory_space=pltpu.VMEM)
    blk = lambda w: pl.BlockSpec((tb, w), lambda i: (i, 0))
    return pl.pallas_call(
        body, name="input_grad", grid=(nb,),
        in_specs=[blk(6 * d), vm, blk(d), blk(d), vm],
        out_specs=(blk(d), pl.BlockSpec((SUBLANES, d), lambda i: (0, 0))),
        out_shape=(jax.ShapeDtypeStruct((t_len, d), F32), jax.ShapeDtypeStruct((SUBLANES, d), F32)),
        compiler_params=_params(dimension_semantics=("arbitrary",)),
    )(dp, win_all, x, dh, sp)


def _weight_grad_in(xnt, dp):
    d, t_len = xnt.shape
    cols = dp.shape[1] // N_DEV
    half = d // 2

    def body(xnt_ref, dp_ref, o_ref):
        for s in range(2):
            o_ref[0, s * half:(s + 1) * half, :] = _dot(xnt_ref[s * half:(s + 1) * half, :], dp_ref[...]).astype(MM)

    return pl.pallas_call(
        body, name="weight_grad_in", grid=(N_DEV,),
        in_specs=[pl.BlockSpec(memory_space=pltpu.VMEM), pl.BlockSpec((t_len, cols), lambda j: (0, j))],
        out_specs=pl.BlockSpec((1, d, cols), lambda j: (j, 0, 0)),
        out_shape=jax.ShapeDtypeStruct((N_DEV, d, cols), MM),
        compiler_params=_params(dimension_semantics=("arbitrary",)),
    )(xnt, dp)


def _weight_grad_out(yt, dhb):
    d2, t_len = yt.shape
    d = dhb.shape[1]
    rows = d2 // N_DEV

    def body(yt_ref, dhb_ref, o_ref):
        o_ref[0] = _dot(yt_ref[...], dhb_ref[...]).astype(MM)

    return pl.pallas_call(
        body, name="weight_grad_out", grid=(N_DEV,),
        in_specs=[pl.BlockSpec((rows, t_len), lambda j: (j, 0)), pl.BlockSpec(memory_space=pltpu.VMEM)],
        out_specs=pl.BlockSpec((1, rows, d), lambda j: (j, 0, 0)),
        out_shape=jax.ShapeDtypeStruct((N_DEV, rows, d), MM),
        compiler_params=_params(dimension_semantics=("arbitrary",)),
    )(yt, dhb)


def _reduce_scatter_adamw(g_all, w, m, v, name):
    n_rows, n_cols = w.shape
    rc = 32

    def body(g_ref, w_ref, m_ref, v_ref, grad_ref, delta_ref, mo_ref, vo_ref,
             from_sib, to_chip, from_chip, send1, recv1, send2, recv2):
        x, y, c = _mesh_pos()
        sibling = (x, y, 1 - c)
        rel = [(0, 0), (1, 0), (0, 1), (1, 1)]

        def block(k, core):
            fx, fy = rel[k]
            return 4 * (x ^ fx) + 2 * (y ^ fy) + core

        def chip(k):
            fx, fy = rel[k]
            return (x ^ fx, y ^ fy, c)

        stage1 = [pltpu.make_async_remote_copy(
            src_ref=g_ref.at[block(k, 1 - c)], dst_ref=from_sib.at[k], send_sem=send1.at[k], recv_sem=recv1.at[k],
            device_id=sibling, device_id_type=MESH) for k in range(4)]
        for cp in stage1:
            cp.start()
        stage2 = []
        for k in (1, 2, 3):
            stage1[k].wait_recv()
            bk = block(k, c)

            def partial(r, carry, k=k, bk=bk):
                rows = pl.ds(r, rc)
                to_chip[k - 1, rows, :] = (g_ref[bk, rows, :].astype(F32) + from_sib[k, rows, :].astype(F32)).astype(MM)
                return carry

            _chunks(n_rows, rc, partial, 0)
            cp = pltpu.make_async_remote_copy(
                src_ref=to_chip.at[k - 1], dst_ref=from_chip.at[k - 1], send_sem=send2.at[k - 1], recv_sem=recv2.at[k - 1],
                device_id=chip(k), device_id_type=MESH)
            cp.start()
            stage2.append(cp)
        stage1[0].wait_recv()
        for cp in stage2:
            cp.wait_recv()
        b0 = block(0, c)

        def update(r, carry):
            rows = pl.ds(r, rc)
            g = g_ref[b0, rows, :].astype(F32) + from_sib[0, rows, :].astype(F32)
            for k in range(3):
                g = g + from_chip[k, rows, :].astype(F32)
            delta, m_new, v_new = _adamw(w_ref[rows, :], g, m_ref[rows, :], v_ref[rows, :])
            grad_ref[rows, :] = g
            delta_ref[rows, :] = delta
            mo_ref[rows, :] = m_new
            vo_ref[rows, :] = v_new
            return carry

        _chunks(n_rows, rc, update, 0)
        for cp in stage1 + stage2:
            cp.wait_send()

    vm = pl.BlockSpec(memory_space=pltpu.VMEM)
    out = jax.ShapeDtypeStruct((n_rows, n_cols), F32)
    return pl.pallas_call(
        body, name=name,
        in_specs=[vm, vm, vm, vm], out_specs=(vm, vm, vm, vm), out_shape=(out, out, out, out),
        scratch_shapes=[pltpu.VMEM((4, n_rows, n_cols), MM), pltpu.VMEM((3, n_rows, n_cols), MM),
                        pltpu.VMEM((3, n_rows, n_cols), MM),
                        pltpu.SemaphoreType.DMA((4,)), pltpu.SemaphoreType.DMA((4,)),
                        pltpu.SemaphoreType.DMA((3,)), pltpu.SemaphoreType.DMA((3,))],
        compiler_params=_params(),
    )(g_all, w, m, v)


def _small_update(facc, xacc, bacc, gwa, gwi, lam, wsl, msl, vsl, cw, cm, cv):
    n_rows, d = wsl.shape
    s8 = SUBLANES
    cc = cw.shape[1]

    def body(facc_ref, xacc_ref, bacc_ref, gwa_ref, gwi_ref, lam_ref, w_ref, m_ref, v_ref, cw_ref, cm_ref, cv_ref,
             g_o, d_o, m_o, v_o, cg_o, cd_o, cm_o, cv_o, gat, send_sems, recv_sems):
        x, y, c = _mesh_pos()
        me = 4 * x + 2 * y + c

        def rowsum(ref, group):
            return jnp.sum(ref[group * s8:(group + 1) * s8, :], axis=0, keepdims=True)

        mine = gat.at[me]
        loss = jnp.sum(rowsum(facc_ref, 0), axis=1, keepdims=True) * (0.5 / d)
        mine[SL_LOSS:SL_LOSS + 1, :] = jnp.broadcast_to(loss, (1, d))
        mine[SL_LN_G:SL_LN_G + 1, :] = rowsum(xacc_ref, 0)
        mine[SL_LRU_B:SL_LRU_B + 1, :] = rowsum(bacc_ref, A_LRU_B)
        mine[SL_B_A:SL_B_A + 1, :] = rowsum(bacc_ref, A_B_A)
        mine[SL_B_I:SL_B_I + 1, :] = rowsum(bacc_ref, A_B_I)
        mine[SL_LAM:SL_LAM + 1, :] = rowsum(bacc_ref, A_LAM) * (RG_LRU_C * jax.nn.sigmoid(-lam_ref[...]))
        mine[SL_CONV_G:SL_CONV_G + 1, :] = rowsum(bacc_ref, A_CONV_G)
        mine[SL_LRU_G:SL_LRU_G + 1, :] = rowsum(bacc_ref, A_LRU_G)
        mine[SL_FINAL_G:SL_FINAL_G + 1, :] = rowsum(facc_ref, 1)
        for k in range(3):
            mine[SL_CONV_W + k:SL_CONV_W + k + 1, :] = rowsum(bacc_ref, A_CONV_W + k)
        for k in range(4):
            mine[SL_LRU_W + k:SL_LRU_W + k + 1, :] = rowsum(bacc_ref, A_LRU_W + k)
        na = gwa_ref.shape[0]
        mine[SL_W_A:SL_W_A + na, :] = gwa_ref[...]
        mine[SL_W_A + na:SL_W_A + 2 * na, :] = gwi_ref[...]

        _two_level_gather(lambda a, px, py, pc: gat.at[4 * px + 2 * py + pc], 1, send_sems, recv_sems)

        def update(r, carry):
            rows = pl.ds(r, s8)
            g = gat[0, rows, :]
            for b in range(1, N_DEV):
                g = g + gat[b, rows, :]
            delta, m_new, v_new = _adamw(w_ref[rows, :], g, m_ref[rows, :], v_ref[rows, :])
            g_o[rows, :] = g
            d_o[rows, :] = delta
            m_o[rows, :] = m_new
            v_o[rows, :] = v_new
            return carry

        _chunks(n_rows, s8, update, 0)
        gc = g_o[s8:2 * s8, pl.ds(pl.multiple_of(me * cc, cc), cc)]
        delta, m_new, v_new = _adamw(cw_ref[...], gc, cm_ref[...], cv_ref[...])
        cg_o[...] = gc
        cd_o[...] = delta
        cm_o[...] = m_new
        cv_o[...] = v_new

    vm = pl.BlockSpec(memory_space=pltpu.VMEM)
    big = jax.ShapeDtypeStruct((n_rows, d), F32)
    small = jax.ShapeDtypeStruct(cw.shape, F32)
    return pl.pallas_call(
        body, name="small_update",
        in_specs=[vm] * 12, out_specs=(vm,) * 8, out_shape=(big, big, big, big, small, small, small, small),
        scratch_shapes=[pltpu.VMEM((N_DEV, n_rows, d), F32), pltpu.SemaphoreType.DMA((7,)), pltpu.SemaphoreType.DMA((7,))],
        compiler_params=_params(),
    )(facc, xacc, bacc, gwa, gwi, lam, wsl, msl, vsl, cw, cm, cv)


def _head_selectors(d, n_heads):
    lane = jnp.arange(d)[:, None] // (d // n_heads)
    e = (lane == jnp.arange(LANES)[None, :]).astype(MM)
    return e, e.T


def _gate_tiles(w, tw):
    n_heads, hd, _ = w.shape
    per = tw // hd
    w4 = w.reshape(n_heads // per, per, hd, hd)
    eye = jnp.eye(per, dtype=w.dtype)
    return (w4[:, :, :, None, :] * eye[None, :, None, :, None]).reshape(n_heads // per, tw, tw)


def _gate_blocks(tiles, n_heads, hd):
    n_tiles, tw, _ = tiles.shape
    per = tw // hd
    t5 = tiles.reshape(n_tiles, per, hd, per, hd)
    diag = jnp.stack([t5[:, a, :, a, :] for a in range(per)], axis=1)
    return diag.reshape(hd, n_heads * hd)


def kernel(x, ln_g, w_in, conv_w, lru_conv_w, lru_conv_b, w_a, b_a, w_i, b_i, lam, conv_out_g, lru_out_g, w_out, final_g, loss_target, m_ln_g, m_w_in, m_conv_w, m_lru_conv_w, m_lru_conv_b, m_w_a, m_b_a, m_w_i, m_b_i, m_lam, m_conv_out_g, m_lru_out_g, m_w_out, m_final_g, v_ln_g, v_w_in, v_conv_w, v_lru_conv_w, v_lru_conv_b, v_w_a, v_b_a, v_w_i, v_b_i, v_lam, v_conv_out_g, v_lru_out_g, v_w_out, v_final_g):
    _, t_len, d = x.shape
    hd_l = d // N_LRU_HEADS
    tw = min(MXU_TILE, d)
    cc = conv_w.shape[1]
    tb = 128
    x2, tgt2 = x[0], loss_target[0]

    def conv_rows(cw3, lw4):
        return jnp.concatenate([jnp.zeros((1, cc), F32), cw3, lw4], axis=0)

    win_all, wout_all, conv_all = _gather_weights(w_in, w_out, conv_rows(conv_w, lru_conv_w))
    wout_full = wout_all.reshape(N_DEV * w_out.shape[0], d)
    conv_full = conv_all.transpose(1, 0, 2).reshape(SUBLANES, d)
    small = [ln_g, lru_conv_b, b_a, b_i, lam, conv_out_g, lru_out_g, final_g]
    sp = jnp.concatenate([jnp.stack(small), conv_full[1:], jnp.zeros((1, d), F32)], axis=0)
    wa_t, wi_t = _gate_tiles(w_a, tw).astype(MM), _gate_tiles(w_i, tw).astype(MM)
    e_c, et_c = _head_selectors(d, N_CONV_HEADS)
    e_l, et_l = _head_selectors(d, N_LRU_HEADS)

    p, h, dh, dhb, xnt, facc = _forward(x2, tgt2, win_all, wout_full, wa_t, wi_t, sp, e_c, et_c, e_l, et_l, tb)
    dp, yt, gwa_t, gwi_t, bacc = _backward(p, h, dh, wout_full, wa_t, wi_t, sp, e_c, et_c, e_l, et_l, tb)
    grad_x, xacc = _input_grad(dp, win_all, x2, dh, sp, tb)
    g_in = _weight_grad_in(xnt, dp)
    g_out = _weight_grad_out(yt, dhb)
    gw_in, dw_in, mw_in, vw_in = _reduce_scatter_adamw(g_in, w_in, m_w_in, v_w_in, "reduce_w_in")
    gw_out, dw_out, mw_out, vw_out = _reduce_scatter_adamw(g_out, w_out, m_w_out, v_w_out, "reduce_w_out")

    def slab(parts, wa_, wi_):
        return jnp.concatenate([jnp.zeros((1, d), F32), jnp.stack(parts), jnp.zeros((SL_W_A - SL_CONV_W, d), F32),
                                wa_.reshape(-1, d), wi_.reshape(-1, d)], axis=0)

    wsl = slab(small, w_a, w_i)
    msl = slab([m_ln_g, m_lru_conv_b, m_b_a, m_b_i, m_lam, m_conv_out_g, m_lru_out_g, m_final_g], m_w_a, m_w_i)
    vsl = slab([v_ln_g, v_lru_conv_b, v_b_a, v_b_i, v_lam, v_conv_out_g, v_lru_out_g, v_final_g], v_w_a, v_w_i)
    outs = _small_update(
        facc, xacc, bacc, _gate_blocks(gwa_t, N_LRU_HEADS, hd_l), _gate_blocks(gwi_t, N_LRU_HEADS, hd_l),
        lam.reshape(1, d), wsl, msl, vsl,
        conv_rows(conv_w, lru_conv_w), conv_rows(m_conv_w, m_lru_conv_w), conv_rows(v_conv_w, v_lru_conv_w))
    sl_g, sl_d, sl_m, sl_v, c_g, c_d, c_m, c_v = outs
    loss = sl_g[SL_LOSS, 0]
    na = w_a.size // d

    def unpack(sl, cv, big_in, big_out):
        one = lambda r: sl[r]
        return [one(SL_LN_G), big_in, cv[1:4], cv[4:8], one(SL_LRU_B),
                sl[SL_W_A:SL_W_A + na].reshape(w_a.shape), one(SL_B_A),
                sl[SL_W_A + na:SL_W_A + 2 * na].reshape(w_i.shape), one(SL_B_I), one(SL_LAM),
                one(SL_CONV_G), one(SL_LRU_G), big_out, one(SL_FINAL_G)]

    return (loss, grad_x[None], *unpack(sl_g, c_g, gw_in, gw_out), *unpack(sl_d, c_d, dw_in, dw_out),
            *unpack(sl_m, c_m, mw_in, mw_out), *unpack(sl_v, c_v, vw_in, vw_out))
```

```python
import functools
import math

import jax
import jax.numpy as jnp
from jax import lax
from jax.experimental import pallas as pl
from jax.experimental.pallas import tpu as pltpu

F32 = jnp.float32
MM = jnp.bfloat16
MESH = pl.DeviceIdType.MESH

N_DEV = 8
N_CONV_HEADS = 8
N_LRU_HEADS = 16
RG_LRU_C = 8.0
RMS_EPS = 1e-6
ADAM_LR, ADAM_B1, ADAM_B2, ADAM_EPS, ADAM_WD, ADAM_STEP = 0.001, 0.9, 0.999, 1e-08, 0.01, 10
ADAM_BC1 = 1.0 - ADAM_B1 ** ADAM_STEP
ADAM_BC2 = 1.0 - ADAM_B2 ** ADAM_STEP

SUBLANES = 8
LANES = 128
MXU_TILE = 256
VMEM_LIMIT = 56 * 1024 * 1024

SP_LN_G, SP_LRU_B, SP_B_A, SP_B_I, SP_LAM, SP_CONV_G, SP_LRU_G, SP_FINAL_G, SP_CONV_W, SP_LRU_W = 0, 1, 2, 3, 4, 5, 6, 7, 8, 11
SP_ROWS = 16
A_CONV_G, A_LRU_G, A_LAM, A_B_A, A_B_I, A_CONV_W, A_LRU_W, A_LRU_B = 0, 1, 2, 3, 4, 5, 8, 12
A_GROUPS = 13
SL_LOSS, SL_LN_G, SL_LRU_B, SL_B_A, SL_B_I, SL_LAM, SL_CONV_G, SL_LRU_G, SL_FINAL_G, SL_CONV_W, SL_LRU_W, SL_W_A = 0, 1, 2, 3, 4, 5, 6, 7, 8, 9, 12, 16


def _params(vmem=True, **kw):
    if vmem:
        kw["vmem_limit_bytes"] = VMEM_LIMIT
    return pltpu.CompilerParams(**kw)


def _dot(a, b):
    return jnp.dot(a, b, preferred_element_type=F32)


def _dot_nt(a, b):
    return lax.dot_general(a, b, (((1,), (1,)), ((), ())), preferred_element_type=F32)


def _split3(v):
    hi = v.astype(MM)
    r1 = v - hi.astype(F32)
    mid = r1.astype(MM)
    lo = (r1 - mid.astype(F32)).astype(MM)
    return hi, mid, lo


def _dot3(v, e):
    hi, mid, lo = _split3(v)
    return _dot(hi, e) + _dot(mid, e) + _dot(lo, e)


def _head_rstd(v, e, et, head_dim):
    ms = _dot3(v * v, e) * (1.0 / head_dim)
    return _dot3(lax.rsqrt(ms + RMS_EPS), et)


def _head_mean(v, e, et, head_dim):
    return _dot3(_dot3(v, e) * (1.0 / head_dim), et)


def _neg_expm1(x):
    series = -x * (1.0 + x * (0.5 + x * (1.0 / 6.0 + x * (1.0 / 24.0 + x * (1.0 / 120.0)))))
    return jnp.where(x > -0.03, series, 1.0 - jnp.exp(x))


def _log_sigmoid(x):
    z = jnp.exp(-jnp.abs(x))
    u = 1.0 + z
    log1p_z = jnp.where(u == 1.0, z, jnp.log(u) * (z / (u - 1.0)))
    return jnp.minimum(x, 0.0) - log1p_z


def _row_iota(d):
    return lax.broadcasted_iota(jnp.int32, (SUBLANES, d), 0)


def _shift_down(cur, prev, s, row):
    return jnp.where(row >= s, pltpu.roll(cur, s, axis=0), pltpu.roll(prev, s, axis=0))


def _shift_up(cur, nxt, s, row):
    k = SUBLANES - s
    return jnp.where(row < k, pltpu.roll(cur, k, axis=0), pltpu.roll(nxt, k, axis=0))


def _scan_fwd(a, b, h_prev, row):
    for s in (1, 2, 4):
        a_s = jnp.where(row >= s, pltpu.roll(a, s, axis=0), 1.0)
        b_s = jnp.where(row >= s, pltpu.roll(b, s, axis=0), 0.0)
        b = a * b_s + b
        a = a * a_s
    return a * h_prev + b


def _scan_bwd(a_next, b, g_next, row):
    a = a_next
    for s in (1, 2, 4):
        k = SUBLANES - s
        a_s = jnp.where(row < k, pltpu.roll(a, k, axis=0), 1.0)
        b_s = jnp.where(row < k, pltpu.roll(b, k, axis=0), 0.0)
        b = a * b_s + b
        a = a * a_s
    return a * g_next + b


def _bcast_row(v, r):
    return jnp.broadcast_to(v[r:r + 1, :], v.shape)


def _chunks(n_rows, rc, body, init, reverse=False):
    n = n_rows // rc

    def step(i, carry):
        j = (n - 1 - i) if reverse else i
        return body(pl.multiple_of(j * rc, rc), carry)

    return lax.fori_loop(0, n, step, init)


def _adamw(w, g, m, v):
    m = ADAM_B1 * m + (1.0 - ADAM_B1) * g
    v = ADAM_B2 * v + (1.0 - ADAM_B2) * (g * g)
    m_hat = m / ADAM_BC1
    v_hat = v / ADAM_BC2
    delta = -ADAM_LR * (m_hat / (jnp.sqrt(v_hat) + ADAM_EPS) + ADAM_WD * w)
    return delta, m, v


def _mesh_pos():
    return lax.axis_index("x"), lax.axis_index("y"), lax.axis_index("c")


def _two_level_gather(blocks_of, n_arrays, send_sems, recv_sems):
    x, y, c = _mesh_pos()
    me, sibling = (x, y, c), (x, y, 1 - c)
    chips = [(1 - x, y), (x, 1 - y), (1 - x, 1 - y)]

    def copy(a, k, block, to):
        return pltpu.make_async_remote_copy(
            src_ref=blocks_of(a, *block), dst_ref=blocks_of(a, *block),
            send_sem=send_sems.at[a * 7 + k], recv_sem=recv_sems.at[a * 7 + k],
            device_id=to, device_id_type=MESH)

    started = []
    for a in range(n_arrays):
        started.append(copy(a, 0, me, sibling))
        started += [copy(a, 1 + j, me, (*chip, c)) for j, chip in enumerate(chips)]
    for cp in started:
        cp.start()
    for j, chip in enumerate(chips):
        for a in range(n_arrays):
            copy(a, 1 + j, (*chip, c), me).wait_recv()
            passed = copy(a, 4 + j, (*chip, c), sibling)
            passed.start()
            started.append(passed)
    for a in range(n_arrays):
        copy(a, 0, sibling, me).wait_recv()
        for j, chip in enumerate(chips):
            copy(a, 4 + j, (*chip, 1 - c), me).wait_recv()
    for cp in started:
        cp.wait_send()


def _gather_weights(w_in, w_out, conv_pack):
    srcs = (w_in, w_out, conv_pack)
    dts = (MM, MM, F32)

    def body(win_ref, wout_ref, cp_ref, win_all, wout_all, cp_all, send_sems, recv_sems):
        x, y, c = _mesh_pos()
        me = 4 * x + 2 * y + c
        outs = (win_all, wout_all, cp_all)
        for src, dst in zip((win_ref, wout_ref, cp_ref), outs):
            rows = src.shape[0]
            rc = min(rows, 32)

            def cast(r, carry, src=src, dst=dst, rc=rc):
                dst[me, pl.ds(r, rc), :] = src[pl.ds(r, rc), :].astype(dst.dtype)
                return carry

            _chunks(rows, rc, cast, 0)

        def blocks_of(a, px, py, pc):
            return outs[a].at[4 * px + 2 * py + pc]

        _two_level_gather(blocks_of, 3, send_sems, recv_sems)

    vm = pl.BlockSpec(memory_space=pltpu.VMEM)
    return pl.pallas_call(
        body, name="gather_weights",
        out_shape=tuple(jax.ShapeDtypeStruct((N_DEV,) + s.shape, dt) for s, dt in zip(srcs, dts)),
        in_specs=[vm, vm, vm], out_specs=(vm, vm, vm),
        scratch_shapes=[pltpu.SemaphoreType.DMA((21,)), pltpu.SemaphoreType.DMA((21,))],
        compiler_params=_params(),
    )(*srcs)


def _forward(x, tgt, win_all, wout, wa_t, wi_t, sp, e_c, et_c, e_l, et_l, tb):
    t_len, d = x.shape
    nb = t_len // tb
    cols = win_all.shape[2]
    pw = math.gcd(d, cols)
    ppb = cols // pw
    n_pieces = 6 * d // pw
    n_tiles, tw = wa_t.shape[0], wa_t.shape[1]
    hd_c, hd_l = d // N_CONV_HEADS, d // N_LRU_HEADS

    def body(x_ref, tgt_ref, win_ref, wout_ref, wa_ref, wi_ref, sp_ref, ec_ref, etc_ref, el_ref, etl_ref,
             p_ref, h_ref, dh_ref, dhb_ref, xnt_ref, acc_ref,
             pf, xnb, yc, u, pa, pi, rcf, rlf, ybuf, tail_z, tail_xl, hcar):
        i = pl.program_id(0)
        row = _row_iota(d)

        @pl.when(i == 0)
        def _():
            tail_z[...] = jnp.zeros_like(tail_z)
            tail_xl[...] = jnp.zeros_like(tail_xl)
            hcar[...] = jnp.zeros_like(hcar)
            acc_ref[...] = jnp.zeros_like(acc_ref)

        def spr(r):
            return sp_ref[r:r + 1, :]

        xv = x_ref[...]
        r0 = lax.rsqrt(jnp.mean(xv * xv, axis=-1, keepdims=True) + RMS_EPS)
        xn = xv * r0 * spr(SP_LN_G)
        xnb[...] = xn.astype(MM)
        xnt_ref[...] = xn.T.astype(MM)

        for p in range(n_pieces):
            w = win_ref[p // ppb, :, (p % ppb) * pw:(p % ppb + 1) * pw]
            pb = _dot(xnb[...], w).astype(MM)
            p_ref[:, p * pw:(p + 1) * pw] = pb
            pf[:, p * pw:(p + 1) * pw] = pb.astype(F32)

        w0, w1, w2 = spr(SP_CONV_W), spr(SP_CONV_W + 1), spr(SP_CONV_W + 2)
        l0, l1, l2, l3 = spr(SP_LRU_W), spr(SP_LRU_W + 1), spr(SP_LRU_W + 2), spr(SP_LRU_W + 3)
        lb = spr(SP_LRU_B)

        def convs(r, carry):
            zp, xp = carry
            rows = pl.ds(r, SUBLANES)
            z = pf[rows, d:2 * d] * pf[rows, 2 * d:3 * d]
            cz = w0 * _shift_down(z, zp, 2, row) + w1 * _shift_down(z, zp, 1, row) + w2 * z
            yc[rows, :] = pf[rows, 0:d] * cz
            xl = pf[rows, 4 * d:5 * d]
            u[rows, :] = (l0 * _shift_down(xl, xp, 3, row) + l1 * _shift_down(xl, xp, 2, row)
                          + l2 * _shift_down(xl, xp, 1, row) + l3 * xl + lb)
            return z, xl

        z_last, xl_last = _chunks(tb, SUBLANES, convs, (tail_z[...], tail_xl[...]))
        tail_z[...] = z_last
        tail_xl[...] = xl_last

        ub = u[...].astype(MM)
        for k in range(n_tiles):
            sl = slice(k * tw, (k + 1) * tw)
            pa[:, sl] = _dot(ub[:, sl], wa_ref[k])
            pi[:, sl] = _dot(ub[:, sl], wi_ref[k])
        rcf[...] = _head_rstd(yc[...], ec_ref[...], etc_ref[...], hd_c)

        c8 = RG_LRU_C * _log_sigmoid(spr(SP_LAM))
        b_a, b_i = spr(SP_B_A), spr(SP_B_I)

        def lru(r, hp):
            rows = pl.ds(r, SUBLANES)
            ra = jax.nn.sigmoid(pa[rows, :] + b_a)
            ii = jax.nn.sigmoid(pi[rows, :] + b_i)
            la = ra * c8
            a = jnp.exp(la)
            mult = jnp.sqrt(_neg_expm1(2.0 * la))
            h = _scan_fwd(a, mult * (ii * u[rows, :]), hp, row)
            h_ref[rows, :] = h
            return _bcast_row(h, SUBLANES - 1)

        hcar[...] = _chunks(tb, SUBLANES, lru, hcar[...])
        rlf[...] = _head_rstd(h_ref[...], el_ref[...], etl_ref[...], hd_l)

        g_c, g_l = spr(SP_CONV_G), spr(SP_LRU_G)

        def gate(r, carry):
            rows = pl.ds(r, SUBLANES)
            gc = pf[rows, 3 * d:4 * d]
            ybuf[rows, 0:d] = yc[rows, :] * rcf[rows, :] * g_c * (gc * jax.nn.sigmoid(gc))
            gl = pf[rows, 5 * d:6 * d]
            ybuf[rows, d:2 * d] = h_ref[rows, :] * rlf[rows, :] * g_l * (gl * jax.nn.sigmoid(gl))
            return carry

        _chunks(tb, SUBLANES, gate, 0)

        hres = x_ref[...] + _dot(ybuf[...].astype(MM), wout_ref[...])
        rf = lax.rsqrt(jnp.mean(hres * hres, axis=-1, keepdims=True) + RMS_EPS)
        hn = hres * rf
        fg = spr(SP_FINAL_G)
        err = hn * fg - tgt_ref[...]
        dout = err * (1.0 / d)
        acc_ref[0:SUBLANES, :] += (err * err).reshape(tb // SUBLANES, SUBLANES, d).sum(axis=0)
        acc_ref[SUBLANES:2 * SUBLANES, :] += (dout * hn).reshape(tb // SUBLANES, SUBLANES, d).sum(axis=0)
        gd = dout * fg
        dhres = rf * (gd - hn * jnp.mean(gd * hn, axis=-1, keepdims=True))
        dh_ref[...] = dhres
        dhb_ref[...] = dhres.astype(MM)

    vm = pl.BlockSpec(memory_space=pltpu.VMEM)
    blk = lambda w: pl.BlockSpec((tb, w), lambda i: (i, 0))
    buf = lambda w: pltpu.VMEM((tb, w), F32)
    return pl.pallas_call(
        body, name="forward", grid=(nb,),
        in_specs=[blk(d), blk(d), vm, vm, vm, vm, vm, vm, vm, vm, vm],
        out_specs=(blk(6 * d), blk(d), blk(d), blk(d),
                   pl.BlockSpec((d, tb), lambda i: (0, i)),
                   pl.BlockSpec((2 * SUBLANES, d), lambda i: (0, 0))),
        out_shape=(jax.ShapeDtypeStruct((t_len, 6 * d), MM),
                   jax.ShapeDtypeStruct((t_len, d), F32),
                   jax.ShapeDtypeStruct((t_len, d), F32),
                   jax.ShapeDtypeStruct((t_len, d), MM),
                   jax.ShapeDtypeStruct((d, t_len), MM),
                   jax.ShapeDtypeStruct((2 * SUBLANES, d), F32)),
        scratch_shapes=[buf(6 * d), pltpu.VMEM((tb, d), MM), buf(d), buf(d), buf(d), buf(d), buf(d), buf(d), buf(2 * d),
                        pltpu.VMEM((SUBLANES, d), F32), pltpu.VMEM((SUBLANES, d), F32), pltpu.VMEM((SUBLANES, d), F32)],
        compiler_params=_params(dimension_semantics=("arbitrary",)),
    )(x, tgt, win_all, wout, wa_t, wi_t, sp, e_c, et_c, e_l, et_l)


def _backward(p, h, dh, wout, wa_t, wi_t, sp, e_c, et_c, e_l, et_l, tb):
    t_len, d = h.shape
    nb = t_len // tb
    n_tiles, tw = wa_t.shape[0], wa_t.shape[1]
    hd_c, hd_l = d // N_CONV_HEADS, d // N_LRU_HEADS
    ph = 16
    s8 = SUBLANES

    def body(p_ref, phalo_ref, h_ref, hhalo_ref, dh_ref, wout_ref, wa_ref, wi_ref, sp_ref,
             ec_ref, etc_ref, el_ref, etl_ref,
             dp_ref, yt_ref, gwa_ref, gwi_ref, acc_ref,
             pf, hh, dy, ybuf, yc, czs, u, pa, pi, rcf, rlf, qc, ql, dyc_hat, dyl_hat, dpa, dpi, du, dpf,
             car_dcz, car_a, car_g, car_du):
        i = pl.program_id(0)
        blk_idx = nb - 1 - i
        row = _row_iota(d)

        @pl.when(i == 0)
        def _():
            for ref in (car_dcz, car_a, car_g, car_du, gwa_ref, gwi_ref, acc_ref):
                ref[...] = jnp.zeros_like(ref)

        def spr(r):
            return sp_ref[r:r + 1, :]

        def acc_add(group, val):
            acc_ref[group * s8:(group + 1) * s8, :] += val

        live = jnp.where(blk_idx > 0, 1.0, 0.0).astype(F32)
        pf[0:ph, :] = phalo_ref[...].astype(F32) * live
        hh[0:s8, :] = hhalo_ref[...] * live
        hh[s8:, :] = h_ref[...]

        def widen(r, carry):
            pf[pl.ds(ph + r, 16), :] = p_ref[pl.ds(r, 16), :].astype(F32)
            return carry

        _chunks(tb, 16, widen, 0)

        dy[...] = _dot_nt(dh_ref[...].astype(MM), wout_ref[...])

        w0, w1, w2 = spr(SP_CONV_W), spr(SP_CONV_W + 1), spr(SP_CONV_W + 2)
        l0, l1, l2, l3 = spr(SP_LRU_W), spr(SP_LRU_W + 1), spr(SP_LRU_W + 2), spr(SP_LRU_W + 3)
        lb = spr(SP_LRU_B)

        def zx(r):
            rows = pl.ds(ph + r, s8)
            return pf[rows, d:2 * d] * pf[rows, 2 * d:3 * d], pf[rows, 4 * d:5 * d]

        def recompute(r, carry):
            rows = pl.ds(r, s8)
            z, xl = zx(r)
            zp, xp = zx(r - s8)
            cz = w0 * _shift_down(z, zp, 2, row) + w1 * _shift_down(z, zp, 1, row) + w2 * z
            czs[rows, :] = cz
            yc[rows, :] = pf[pl.ds(ph + r, s8), 0:d] * cz
            u[rows, :] = (l0 * _shift_down(xl, xp, 3, row) + l1 * _shift_down(xl, xp, 2, row)
                          + l2 * _shift_down(xl, xp, 1, row) + l3 * xl + lb)
            return carry

        _chunks(tb, s8, recompute, 0)

        ub = u[...].astype(MM)
        for k in range(n_tiles):
            sl = slice(k * tw, (k + 1) * tw)
            pa[:, sl] = _dot(ub[:, sl], wa_ref[k])
            pi[:, sl] = _dot(ub[:, sl], wi_ref[k])
        rcf[...] = _head_rstd(yc[...], ec_ref[...], etc_ref[...], hd_c)
        rlf[...] = _head_rstd(h_ref[...], el_ref[...], etl_ref[...], hd_l)

        g_c, g_l = spr(SP_CONV_G), spr(SP_LRU_G)

        def gates(r, carry):
            rows = pl.ds(r, s8)
            prow = pl.ds(ph + r, s8)
            for (off_g, off_y, src, rstd, gain, q, dhat, grp) in (
                    (3 * d, 0, yc, rcf, g_c, qc, dyc_hat, A_CONV_G),
                    (5 * d, d, h_ref, rlf, g_l, ql, dyl_hat, A_LRU_G)):
                gt = pf[prow, off_g:off_g + d]
                sg = jax.nn.sigmoid(gt)
                silu = gt * sg
                yhat = src[rows, :] * rstd[rows, :]
                nrm = yhat * gain
                ybuf[rows, off_y:off_y + d] = nrm * silu
                dout = dy[rows, off_y:off_y + d]
                dnrm = dout * silu
                dpf[rows, off_g:off_g + d] = dout * nrm * (sg * (1.0 + gt * (1.0 - sg)))
                acc_add(grp, dnrm * yhat)
                dh_ = dnrm * gain
                dhat[rows, :] = dh_
                q[rows, :] = dh_ * yhat
            return carry

        _chunks(tb, s8, gates, 0)

        qc[...] = _head_mean(qc[...], ec_ref[...], etc_ref[...], hd_c)
        ql[...] = _head_mean(ql[...], el_ref[...], etl_ref[...], hd_l)
        yt_ref[...] = ybuf[...].T.astype(MM)

        c8 = RG_LRU_C * _log_sigmoid(spr(SP_LAM))
        b_a, b_i = spr(SP_B_A), spr(SP_B_I)

        def mixers(r, carry):
            dcz_n, a_n, g_n = carry
            rows = pl.ds(r, s8)
            prow = pl.ds(ph + r, s8)
            rstd = rcf[rows, :]
            yhat = yc[rows, :] * rstd
            dyc = rstd * (dyc_hat[rows, :] - yhat * qc[rows, :])
            dpf[rows, 0:d] = dyc * czs[rows, :]
            dcz = dyc * pf[prow, 0:d]
            dz = w2 * dcz + w1 * _shift_up(dcz, dcz_n, 1, row) + w0 * _shift_up(dcz, dcz_n, 2, row)
            z, _ = zx(r)
            zp, _ = zx(r - s8)
            dpf[rows, d:2 * d] = dz * pf[prow, 2 * d:3 * d]
            dpf[rows, 2 * d:3 * d] = dz * pf[prow, d:2 * d]
            acc_add(A_CONV_W, dcz * _shift_down(z, zp, 2, row))
            acc_add(A_CONV_W + 1, dcz * _shift_down(z, zp, 1, row))
            acc_add(A_CONV_W + 2, dcz * z)
            rstd = rlf[rows, :]
            hcur = hh[pl.ds(s8 + r, s8), :]
            hhat = hcur * rstd
            dh_out = rstd * (dyl_hat[rows, :] - hhat * ql[rows, :])
            ra = jax.nn.sigmoid(pa[rows, :] + b_a)
            la = ra * c8
            a = jnp.exp(la)
            g = _scan_bwd(_shift_up(a, a_n, 1, row), dh_out, g_n, row)
            da = g * _shift_down(hcur, hh[pl.ds(r, s8), :], 1, row)
            ii = jax.nn.sigmoid(pi[rows, :] + b_i)
            uu = u[rows, :]
            mult = jnp.sqrt(_neg_expm1(2.0 * la))
            dmult = g * (ii * uu)
            ds = g * mult
            dla = a * (da - dmult * a / mult)
            acc_add(A_LAM, dla * ra)
            dpa_ = dla * c8 * ra * (1.0 - ra)
            dpi_ = ds * uu * ii * (1.0 - ii)
            acc_add(A_B_A, dpa_)
            acc_add(A_B_I, dpi_)
            dpa[rows, :] = dpa_
            dpi[rows, :] = dpi_
            du[rows, :] = ds * ii
            return dcz, a, _bcast_row(g, 0)

        dcz_f, a_f, g_f = _chunks(tb, s8, mixers, (car_dcz[...], car_a[...], car_g[...]), reverse=True)
        car_dcz[...] = dcz_f
        car_a[...] = a_f
        car_g[...] = g_f

        dpab = dpa[...].astype(MM)
        dpib = dpi[...].astype(MM)
        for k in range(n_tiles):
            sl = slice(k * tw, (k + 1) * tw)
            du[:, sl] += _dot_nt(dpab[:, sl], wa_ref[k]) + _dot_nt(dpib[:, sl], wi_ref[k])
            ut = u[:, sl].T.astype(MM)
            gwa_ref[k] += _dot(ut, dpab[:, sl])
            gwi_ref[k] += _dot(ut, dpib[:, sl])

        def lru_conv(r, du_n):
            rows = pl.ds(r, s8)
            dut = du[rows, :]
            dpf[rows, 4 * d:5 * d] = (l3 * dut + l2 * _shift_up(dut, du_n, 1, row)
                                      + l1 * _shift_up(dut, du_n, 2, row) + l0 * _shift_up(dut, du_n, 3, row))
            _, xl = zx(r)
            _, xp = zx(r - s8)
            acc_add(A_LRU_W, dut * _shift_down(xl, xp, 3, row))
            acc_add(A_LRU_W + 1, dut * _shift_down(xl, xp, 2, row))
            acc_add(A_LRU_W + 2, dut * _shift_down(xl, xp, 1, row))
            acc_add(A_LRU_W + 3, dut * xl)
            acc_add(A_LRU_B, dut)
            return dut

        car_du[...] = _chunks(tb, s8, lru_conv, car_du[...], reverse=True)

        def narrow(r, carry):
            dp_ref[pl.ds(r, 16), :] = dpf[pl.ds(r, 16), :].astype(MM)
            return carry

        _chunks(tb, 16, narrow, 0)

    vm = pl.BlockSpec(memory_space=pltpu.VMEM)
    rev = lambda w: pl.BlockSpec((tb, w), lambda i: (nb - 1 - i, 0))
    halo = lambda rows, w: pl.BlockSpec((rows, w), lambda i: (jnp.maximum((nb - 1 - i) * (tb // rows) - 1, 0), 0))
    const = lambda shape: pl.BlockSpec(shape, lambda i: (0,) * len(shape))
    buf = lambda w: pltpu.VMEM((tb, w), F32)
    car = pltpu.VMEM((SUBLANES, d), F32)
    return pl.pallas_call(
        body, name="backward", grid=(nb,),
        in_specs=[rev(6 * d), halo(ph, 6 * d), rev(d), halo(SUBLANES, d), rev(d), vm, vm, vm, vm, vm, vm, vm, vm],
        out_specs=(rev(6 * d), pl.BlockSpec((2 * d, tb), lambda i: (0, nb - 1 - i)),
                   const((n_tiles, tw, tw)), const((n_tiles, tw, tw)), const((A_GROUPS * SUBLANES, d))),
        out_shape=(jax.ShapeDtypeStruct((t_len, 6 * d), MM),
                   jax.ShapeDtypeStruct((2 * d, t_len), MM),
                   jax.ShapeDtypeStruct((n_tiles, tw, tw), F32),
                   jax.ShapeDtypeStruct((n_tiles, tw, tw), F32),
                   jax.ShapeDtypeStruct((A_GROUPS * SUBLANES, d), F32)),
        scratch_shapes=[pltpu.VMEM((ph + tb, 6 * d), F32), pltpu.VMEM((SUBLANES + tb, d), F32), buf(2 * d), buf(2 * d)]
                       + [buf(d)] * 14 + [buf(6 * d), car, car, car, car],
        compiler_params=_params(dimension_semantics=("arbitrary",)),
    )(p, p, h, h, dh, wout, wa_t, wi_t, sp, e_c, et_c, e_l, et_l)


def _input_grad(dp, win_all, x, dh, sp, tb):
    t_len, d = x.shape
    nb = t_len // tb
    cols = win_all.shape[2]

    def body(dp_ref, win_ref, x_ref, dh_ref, sp_ref, gx_ref, acc_ref):
        @pl.when(pl.program_id(0) == 0)
        def _():
            acc_ref[...] = jnp.zeros_like(acc_ref)

        dxn = _dot_nt(dp_ref[:, 0:cols], win_ref[0])
        for j in range(1, N_DEV):
            dxn += _dot_nt(dp_ref[:, j * cols:(j + 1) * cols], win_ref[j])
        xv = x_ref[...]
        r0 = lax.rsqrt(jnp.mean(xv * xv, axis=-1, keepdims=True) + RMS_EPS)
        xhat = xv * r0
        acc_ref[...] += (dxn * xhat).reshape(tb // SUBLANES, SUBLANES, d).sum(axis=0)
        dxh = dxn * sp_ref[SP_LN_G:SP_LN_G + 1, :]
        gx_ref[...] = dh_ref[...] + r0 * (dxh - xhat * jnp.mean(dxh * xhat, axis=-1, keepdims=True))

    vm = pl.BlockSpec(memory_space=pltpu.VMEM)
    blk = lambda w: pl.BlockSpec((tb, w), lambda i: (i, 0))
    return pl.pallas_call(
        body, name="input_grad", grid=(nb,),
        in_specs=[blk(6 * d), vm, blk(d), blk(d), vm],
        out_specs=(blk(d), pl.BlockSpec((SUBLANES, d), lambda i: (0, 0))),
        out_shape=(jax.ShapeDtypeStruct((t_len, d), F32), jax.ShapeDtypeStruct((SUBLANES, d), F32)),
        compiler_params=_params(dimension_semantics=("arbitrary",)),
    )(dp, win_all, x, dh, sp)


def _weight_grad_in(xnt, dp):
    d, t_len = xnt.shape
    cols = dp.shape[1] // N_DEV
    half = d // 2

    def body(xnt_ref, dp_ref, o_ref):
        for s in range(2):
            o_ref[0, s * half:(s + 1) * half, :] = _dot(xnt_ref[s * half:(s + 1) * half, :], dp_ref[...]).astype(MM)

    return pl.pallas_call(
        body, name="weight_grad_in", grid=(N_DEV,),
        in_specs=[pl.BlockSpec(memory_space=pltpu.VMEM), pl.BlockSpec((t_len, cols), lambda j: (0, j))],
        out_specs=pl.BlockSpec((1, d, cols), lambda j: (j, 0, 0)),
        out_shape=jax.ShapeDtypeStruct((N_DEV, d, cols), MM),
        compiler_params=_params(dimension_semantics=("arbitrary",)),
    )(xnt, dp)


def _weight_grad_out(yt, dhb):
    d2, t_len = yt.shape
    d = dhb.shape[1]
    rows = d2 // N_DEV

    def body(yt_ref, dhb_ref, o_ref):
        o_ref[0] = _dot(yt_ref[...], dhb_ref[...]).astype(MM)

    return pl.pallas_call(
        body, name="weight_grad_out", grid=(N_DEV,),
        in_specs=[pl.BlockSpec((rows, t_len), lambda j: (j, 0)), pl.BlockSpec(memory_space=pltpu.VMEM)],
        out_specs=pl.BlockSpec((1, rows, d), lambda j: (j, 0, 0)),
        out_shape=jax.ShapeDtypeStruct((N_DEV, rows, d), MM),
        compiler_params=_params(dimension_semantics=("arbitrary",)),
    )(yt, dhb)


def _reduce_scatter_adamw(g_all, w, m, v, name):
    n_rows, n_cols = w.shape
    rc = 32

    def body(g_ref, w_ref, m_ref, v_ref, grad_ref, delta_ref, mo_ref, vo_ref,
             from_sib, to_chip, from_chip, send1, recv1, send2, recv2):
        x, y, c = _mesh_pos()
        sibling = (x, y, 1 - c)
        rel = [(0, 0), (1, 0), (0, 1), (1, 1)]

        def block(k, core):
            fx, fy = rel[k]
            return 4 * (x ^ fx) + 2 * (y ^ fy) + core

        def chip(k):
            fx, fy = rel[k]
            return (x ^ fx, y ^ fy, c)

        stage1 = [pltpu.make_async_remote_copy(
            src_ref=g_ref.at[block(k, 1 - c)], dst_ref=from_sib.at[k], send_sem=send1.at[k], recv_sem=recv1.at[k],
            device_id=sibling, device_id_type=MESH) for k in range(4)]
        for cp in stage1:
            cp.start()
        stage2 = []
        for k in (1, 2, 3):
            stage1[k].wait_recv()
            bk = block(k, c)

            def partial(r, carry, k=k, bk=bk):
                rows = pl.ds(r, rc)
                to_chip[k - 1, rows, :] = (g_ref[bk, rows, :].astype(F32) + from_sib[k, rows, :].astype(F32)).astype(MM)
                return carry

            _chunks(n_rows, rc, partial, 0)
            cp = pltpu.make_async_remote_copy(
                src_ref=to_chip.at[k - 1], dst_ref=from_chip.at[k - 1], send_sem=send2.at[k - 1], recv_sem=recv2.at[k - 1],
                device_id=chip(k), device_id_type=MESH)
            cp.start()
            stage2.append(cp)
        stage1[0].wait_recv()
        for cp in stage2:
            cp.wait_recv()
        b0 = block(0, c)

        def update(r, carry):
            rows = pl.ds(r, rc)
            g = g_ref[b0, rows, :].astype(F32) + from_sib[0, rows, :].astype(F32)
            for k in range(3):
                g = g + from_chip[k, rows, :].astype(F32)
            delta, m_new, v_new = _adamw(w_ref[rows, :], g, m_ref[rows, :], v_ref[rows, :])
            grad_ref[rows, :] = g
            delta_ref[rows, :] = delta
            mo_ref[rows, :] = m_new
            vo_ref[rows, :] = v_new
            return carry

        _chunks(n_rows, rc, update, 0)
        for cp in stage1 + stage2:
            cp.wait_send()

    vm = pl.BlockSpec(memory_space=pltpu.VMEM)
    out = jax.ShapeDtypeStruct((n_rows, n_cols), F32)
    return pl.pallas_call(
        body, name=name,
        in_specs=[vm, vm, vm, vm], out_specs=(vm, vm, vm, vm), out_shape=(out, out, out, out),
        scratch_shapes=[pltpu.VMEM((4, n_rows, n_cols), MM), pltpu.VMEM((3, n_rows, n_cols), MM),
                        pltpu.VMEM((3, n_rows, n_cols), MM),
                        pltpu.SemaphoreType.DMA((4,)), pltpu.SemaphoreType.DMA((4,)),
                        pltpu.SemaphoreType.DMA((3,)), pltpu.SemaphoreType.DMA((3,))],
        compiler_params=_params(),
    )(g_all, w, m, v)


def _small_update(facc, xacc, bacc, gwa, gwi, lam, wsl, msl, vsl, cw, cm, cv):
    n_rows, d = wsl.shape
    s8 = SUBLANES
    cc = cw.shape[1]

    def body(facc_ref, xacc_ref, bacc_ref, gwa_ref, gwi_ref, lam_ref, w_ref, m_ref, v_ref, cw_ref, cm_ref, cv_ref,
             g_o, d_o, m_o, v_o, cg_o, cd_o, cm_o, cv_o, gat, send_sems, recv_sems):
        x, y, c = _mesh_pos()
        me = 4 * x + 2 * y + c

        def rowsum(ref, group):
            return jnp.sum(ref[group * s8:(group + 1) * s8, :], axis=0, keepdims=True)

        mine = gat.at[me]
        loss = jnp.sum(rowsum(facc_ref, 0), axis=1, keepdims=True) * (0.5 / d)
        mine[SL_LOSS:SL_LOSS + 1, :] = jnp.broadcast_to(loss, (1, d))
        mine[SL_LN_G:SL_LN_G + 1, :] = rowsum(xacc_ref, 0)
        mine[SL_LRU_B:SL_LRU_B + 1, :] = rowsum(bacc_ref, A_LRU_B)
        mine[SL_B_A:SL_B_A + 1, :] = rowsum(bacc_ref, A_B_A)
        mine[SL_B_I:SL_B_I + 1, :] = rowsum(bacc_ref, A_B_I)
        mine[SL_LAM:SL_LAM + 1, :] = rowsum(bacc_ref, A_LAM) * (RG_LRU_C * jax.nn.sigmoid(-lam_ref[...]))
        mine[SL_CONV_G:SL_CONV_G + 1, :] = rowsum(bacc_ref, A_CONV_G)
        mine[SL_LRU_G:SL_LRU_G + 1, :] = rowsum(bacc_ref, A_LRU_G)
        mine[SL_FINAL_G:SL_FINAL_G + 1, :] = rowsum(facc_ref, 1)
        for k in range(3):
            mine[SL_CONV_W + k:SL_CONV_W + k + 1, :] = rowsum(bacc_ref, A_CONV_W + k)
        for k in range(4):
            mine[SL_LRU_W + k:SL_LRU_W + k + 1, :] = rowsum(bacc_ref, A_LRU_W + k)
        na = gwa_ref.shape[0]
        mine[SL_W_A:SL_W_A + na, :] = gwa_ref[...]
        mine[SL_W_A + na:SL_W_A + 2 * na, :] = gwi_ref[...]

        _two_level_gather(lambda a, px, py, pc: gat.at[4 * px + 2 * py + pc], 1, send_sems, recv_sems)

        def update(r, carry):
            rows = pl.ds(r, s8)
            g = gat[0, rows, :]
            for b in range(1, N_DEV):
                g = g + gat[b, rows, :]
            delta, m_new, v_new = _adamw(w_ref[rows, :], g, m_ref[rows, :], v_ref[rows, :])
            g_o[rows, :] = g
            d_o[rows, :] = delta
            m_o[rows, :] = m_new
            v_o[rows, :] = v_new
            return carry

        _chunks(n_rows, s8, update, 0)
        gc = g_o[s8:2 * s8, pl.ds(pl.multiple_of(me * cc, cc), cc)]
        delta, m_new, v_new = _adamw(cw_ref[...], gc, cm_ref[...], cv_ref[...])
        cg_o[...] = gc
        cd_o[...] = delta
        cm_o[...] = m_new
        cv_o[...] = v_new

    vm = pl.BlockSpec(memory_space=pltpu.VMEM)
    big = jax.ShapeDtypeStruct((n_rows, d), F32)
    small = jax.ShapeDtypeStruct(cw.shape, F32)
    return pl.pallas_call(
        body, name="small_update",
        in_specs=[vm] * 12, out_specs=(vm,) * 8, out_shape=(big, big, big, big, small, small, small, small),
        scratch_shapes=[pltpu.VMEM((N_DEV, n_rows, d), F32), pltpu.SemaphoreType.DMA((7,)), pltpu.SemaphoreType.DMA((7,))],
        compiler_params=_params(),
    )(facc, xacc, bacc, gwa, gwi, lam, wsl, msl, vsl, cw, cm, cv)


def _head_selectors(d, n_heads):
    lane = jnp.arange(d)[:, None] // (d // n_heads)
    e = (lane == jnp.arange(LANES)[None, :]).astype(MM)
    return e, e.T


def _gate_tiles(w, tw):
    n_heads, hd, _ = w.shape
    per = tw // hd
    w4 = w.reshape(n_heads // per, per, hd, hd)
    eye = jnp.eye(per, dtype=w.dtype)
    return (w4[:, :, :, None, :] * eye[None, :, None, :, None]).reshape(n_heads // per, tw, tw)


def _gate_blocks(tiles, n_heads, hd):
    n_tiles, tw, _ = tiles.shape
    per = tw // hd
    t5 = tiles.reshape(n_tiles, per, hd, per, hd)
    diag = jnp.stack([t5[:, a, :, a, :] for a in range(per)], axis=1)
    return diag.reshape(hd, n_heads * hd)


def kernel(x, ln_g, w_in, conv_w, lru_conv_w, lru_conv_b, w_a, b_a, w_i, b_i, lam, conv_out_g, lru_out_g, w_out, final_g, loss_target, m_ln_g, m_w_in, m_conv_w, m_lru_conv_w, m_lru_conv_b, m_w_a, m_b_a, m_w_i, m_b_i, m_lam, m_conv_out_g, m_lru_out_g, m_w_out, m_final_g, v_ln_g, v_w_in, v_conv_w, v_lru_conv_w, v_lru_conv_b, v_w_a, v_b_a, v_w_i, v_b_i, v_lam, v_conv_out_g, v_lru_out_g, v_w_out, v_final_g):
    _, t_len, d = x.shape
    hd_l = d // N_LRU_HEADS
    tw = min(MXU_TILE, d)
    cc = conv_w.shape[1]
    tb = 128
    x2, tgt2 = x[0], loss_target[0]

    def conv_rows(cw3, lw4):
        return jnp.concatenate([jnp.zeros((1, cc), F32), cw3, lw4], axis=0)

    win_all, wout_all, conv_all = _gather_weights(w_in, w_out, conv_rows(conv_w, lru_conv_w))
    wout_full = wout_all.reshape(N_DEV * w_out.shape[0], d)
    conv_full = conv_all.transpose(1, 0, 2).reshape(SUBLANES, d)
    small = [ln_g, lru_conv_b, b_a, b_i, lam, conv_out_g, lru_out_g, final_g]
    sp = jnp.concatenate([jnp.stack(small), conv_full[1:], jnp.zeros((1, d), F32)], axis=0)
    wa_t, wi_t = _gate_tiles(w_a, tw).astype(MM), _gate_tiles(w_i, tw).astype(MM)
    e_c, et_c = _head_selectors(d, N_CONV_HEADS)
    e_l, et_l = _head_selectors(d, N_LRU_HEADS)

    p, h, dh, dhb, xnt, facc = _forward(x2, tgt2, win_all, wout_full, wa_t, wi_t, sp, e_c, et_c, e_l, et_l, tb)
    dp, yt, gwa_t, gwi_t, bacc = _backward(p, h, dh, wout_full, wa_t, wi_t, sp, e_c, et_c, e_l, et_l, tb)
    grad_x, xacc = _input_grad(dp, win_all, x2, dh, sp, min(512, t_len))
    g_in = _weight_grad_in(xnt, dp)
    g_out = _weight_grad_out(yt, dhb)
    gw_in, dw_in, mw_in, vw_in = _reduce_scatter_adamw(g_in, w_in, m_w_in, v_w_in, "reduce_w_in")
    gw_out, dw_out, mw_out, vw_out = _reduce_scatter_adamw(g_out, w_out, m_w_out, v_w_out, "reduce_w_out")

    def slab(parts, wa_, wi_):
        return jnp.concatenate([jnp.zeros((1, d), F32), jnp.stack(parts), jnp.zeros((SL_W_A - SL_CONV_W, d), F32),
                                wa_.reshape(-1, d), wi_.reshape(-1, d)], axis=0)

    wsl = slab(small, w_a, w_i)
    msl = slab([m_ln_g, m_lru_conv_b, m_b_a, m_b_i, m_lam, m_conv_out_g, m_lru_out_g, m_final_g], m_w_a, m_w_i)
    vsl = slab([v_ln_g, v_lru_conv_b, v_b_a, v_b_i, v_lam, v_conv_out_g, v_lru_out_g, v_final_g], v_w_a, v_w_i)
    outs = _small_update(
        facc, xacc, bacc, _gate_blocks(gwa_t, N_LRU_HEADS, hd_l), _gate_blocks(gwi_t, N_LRU_HEADS, hd_l),
        lam.reshape(1, d), wsl, msl, vsl,
        conv_rows(conv_w, lru_conv_w), conv_rows(m_conv_w, m_lru_conv_w), conv_rows(v_conv_w, v_lru_conv_w))
    sl_g, sl_d, sl_m, sl_v, c_g, c_d, c_m, c_v = outs
    loss = sl_g[SL_LOSS, 0]
    na = w_a.size // d

    def unpack(sl, cv, big_in, big_out):
        one = lambda r: sl[r]
        return [one(SL_LN_G), big_in, cv[1:4], cv[4:8], one(SL_LRU_B),
                sl[SL_W_A:SL_W_A + na].reshape(w_a.shape), one(SL_B_A),
                sl[SL_W_A + na:SL_W_A + 2 * na].reshape(w_i.shape), one(SL_B_I), one(SL_LAM),
                one(SL_CONV_G), one(SL_LRU_G), big_out, one(SL_FINAL_G)]

    return (loss, grad_x[None], *unpack(sl_g, c_g, gw_in, gw_out), *unpack(sl_d, c_d, dw_in, dw_out),
            *unpack(sl_m, c_m, mw_in, mw_out), *unpack(sl_v, c_v, vw_in, vw_out))
```

```python
import functools

import jax
import jax.numpy as jnp
from jax import lax
from jax.experimental import pallas as pl
from jax.experimental.pallas import tpu as pltpu

F32 = jnp.float32
MM = jnp.bfloat16
MESH = pl.DeviceIdType.MESH

N_DEV = 8
N_CONV_HEADS = 8
N_LRU_HEADS = 16
RG_LRU_C = 8.0
RMS_EPS = 1e-6
ADAM_LR, ADAM_B1, ADAM_B2, ADAM_EPS, ADAM_WD, ADAM_STEP = 0.001, 0.9, 0.999, 1e-08, 0.01, 10
ADAM_BC1 = 1.0 - ADAM_B1 ** ADAM_STEP
ADAM_BC2 = 1.0 - ADAM_B2 ** ADAM_STEP

SUBLANES = 8
LANES = 128
MXU_TILE = 256
VMEM_LIMIT = 56 * 1024 * 1024

SP_LN_G, SP_LRU_B, SP_B_A, SP_B_I, SP_LAM, SP_CONV_G, SP_LRU_G, SP_FINAL_G, SP_CONV_W, SP_LRU_W = 0, 1, 2, 3, 4, 5, 6, 7, 8, 11
SP_ROWS = 16
A_CONV_G, A_LRU_G, A_LAM, A_B_A, A_B_I, A_CONV_W, A_LRU_W, A_LRU_B = 0, 1, 2, 3, 4, 5, 8, 12
A_GROUPS = 13
SL_LOSS, SL_LN_G, SL_LRU_B, SL_B_A, SL_B_I, SL_LAM, SL_CONV_G, SL_LRU_G, SL_FINAL_G, SL_CONV_W, SL_LRU_W, SL_W_A = 0, 1, 2, 3, 4, 5, 6, 7, 8, 9, 12, 16


def _params(vmem=True, **kw):
    if vmem:
        kw["vmem_limit_bytes"] = VMEM_LIMIT
    return pltpu.CompilerParams(**kw)


def _dot(a, b):
    return jnp.dot(a, b, preferred_element_type=F32)


def _dot_nt(a, b):
    return lax.dot_general(a, b, (((1,), (1,)), ((), ())), preferred_element_type=F32)


def _split3(v):
    hi = v.astype(MM)
    r1 = v - hi.astype(F32)
    mid = r1.astype(MM)
    lo = (r1 - mid.astype(F32)).astype(MM)
    return hi, mid, lo


def _dot3(v, e):
    hi, mid, lo = _split3(v)
    return _dot(hi, e) + _dot(mid, e) + _dot(lo, e)


def _head_rstd(v, e, et, head_dim):
    ms = _dot3(v * v, e) * (1.0 / head_dim)
    return _dot3(lax.rsqrt(ms + RMS_EPS), et)


def _head_mean(v, e, et, head_dim):
    return _dot3(_dot3(v, e) * (1.0 / head_dim), et)


def _neg_expm1(x):
    series = -x * (1.0 + x * (0.5 + x * (1.0 / 6.0 + x * (1.0 / 24.0 + x * (1.0 / 120.0)))))
    return jnp.where(x > -0.03, series, 1.0 - jnp.exp(x))


def _log_sigmoid(x):
    z = jnp.exp(-jnp.abs(x))
    u = 1.0 + z
    log1p_z = jnp.where(u == 1.0, z, jnp.log(u) * (z / (u - 1.0)))
    return jnp.minimum(x, 0.0) - log1p_z


def _row_iota(d):
    return lax.broadcasted_iota(jnp.int32, (SUBLANES, d), 0)


def _shift_down(cur, prev, s, row):
    return jnp.where(row >= s, pltpu.roll(cur, s, axis=0), pltpu.roll(prev, s, axis=0))


def _shift_up(cur, nxt, s, row):
    k = SUBLANES - s
    return jnp.where(row < k, pltpu.roll(cur, k, axis=0), pltpu.roll(nxt, k, axis=0))


def _scan_fwd(a, b, h_prev, row):
    for s in (1, 2, 4):
        a_s = jnp.where(row >= s, pltpu.roll(a, s, axis=0), 1.0)
        b_s = jnp.where(row >= s, pltpu.roll(b, s, axis=0), 0.0)
        b = a * b_s + b
        a = a * a_s
    return a * h_prev + b


def _scan_bwd(a_next, b, g_next, row):
    a = a_next
    for s in (1, 2, 4):
        k = SUBLANES - s
        a_s = jnp.where(row < k, pltpu.roll(a, k, axis=0), 1.0)
        b_s = jnp.where(row < k, pltpu.roll(b, k, axis=0), 0.0)
        b = a * b_s + b
        a = a * a_s
    return a * g_next + b


def _bcast_row(v, r):
    return jnp.broadcast_to(v[r:r + 1, :], v.shape)


def _chunks(n_rows, rc, body, init, reverse=False):
    n = n_rows // rc

    def step(i, carry):
        j = (n - 1 - i) if reverse else i
        return body(pl.multiple_of(j * rc, rc), carry)

    return lax.fori_loop(0, n, step, init)


def _adamw(w, g, m, v):
    m = ADAM_B1 * m + (1.0 - ADAM_B1) * g
    v = ADAM_B2 * v + (1.0 - ADAM_B2) * (g * g)
    m_hat = m / ADAM_BC1
    v_hat = v / ADAM_BC2
    delta = -ADAM_LR * (m_hat / (jnp.sqrt(v_hat) + ADAM_EPS) + ADAM_WD * w)
    return delta, m, v


def _mesh_pos():
    return lax.axis_index("x"), lax.axis_index("y"), lax.axis_index("c")


class _Gather:
    def __init__(self, blocks_of, send_sems, recv_sems, own_src=None):
        x, y, c = _mesh_pos()
        self.c = c
        self.me, self.sibling = (x, y, c), (x, y, 1 - c)
        self.chips = [(1 - x, y), (x, 1 - y), (1 - x, 1 - y)]
        self.blocks_of, self.send_sems, self.recv_sems = blocks_of, send_sems, recv_sems
        self.own_src = own_src

    def copy(self, a, k, block, to):
        src = self.blocks_of(a, *block)
        if block is self.me and self.own_src is not None:
            src = self.own_src[a]
        return pltpu.make_async_remote_copy(
            src_ref=src, dst_ref=self.blocks_of(a, *block),
            send_sem=self.send_sems.at[a * 7 + k], recv_sem=self.recv_sems.at[a * 7 + k],
            device_id=to, device_id_type=MESH)

    def start_own(self, a):
        self.copy(a, 0, self.me, self.sibling).start()
        for j, chip in enumerate(self.chips):
            self.copy(a, 1 + j, self.me, (*chip, self.c)).start()

    def wait_sibling(self, a):
        self.copy(a, 0, self.sibling, self.me).wait_recv()

    def wait_chip_and_pass_on(self, a, j):
        block = (*self.chips[j], self.c)
        self.copy(a, 1 + j, block, self.me).wait_recv()
        self.copy(a, 4 + j, block, self.sibling).start()

    def wait_passed_on(self, a, j):
        self.copy(a, 4 + j, (*self.chips[j], 1 - self.c), self.me).wait_recv()

    def wait_sends(self, a):
        self.copy(a, 0, self.me, self.sibling).wait_send()
        for j, chip in enumerate(self.chips):
            self.copy(a, 1 + j, self.me, (*chip, self.c)).wait_send()
            self.copy(a, 4 + j, (*chip, self.c), self.sibling).wait_send()

    def finish(self, a):
        for j in range(3):
            self.wait_chip_and_pass_on(a, j)
        self.wait_sibling(a)
        for j in range(3):
            self.wait_passed_on(a, j)
        self.wait_sends(a)


def _block_order():
    x, y, c = _mesh_pos()
    chips = [(1 - x, y), (x, 1 - y), (1 - x, 1 - y)]
    idx = lambda px, py, pc: 4 * px + 2 * py + pc
    order = [idx(x, y, c), idx(x, y, 1 - c)] + [idx(*ch, c) for ch in chips] + [idx(*ch, 1 - c) for ch in chips]
    return jnp.stack(order).astype(jnp.int32)


def _gather_project(x, w_in, w_out, conv_pack, ln_g, tb):
    t_len, d = x.shape
    nb = t_len // tb
    cols = w_in.shape[1]
    mc = min(512, t_len)
    srcs = (w_in, w_out, conv_pack)
    dts = (MM, MM, F32)

    def body(order_ref, x_ref, win_ref, wout_ref, cp_ref, lng_ref, p_ref, xnt_ref, win_all, wout_all, cp_all,
             xnb, st_in, st_out, st_cp, wbuf, send_sems, recv_sems, local_sems):
        i = pl.program_id(0)
        x_, y_, c_ = _mesh_pos()
        me = 4 * x_ + 2 * y_ + c_
        outs = (win_all, wout_all, cp_all)
        stages = (st_in, st_out, st_cp)
        gather = _Gather(lambda a, px, py, pc: outs[a].at[4 * px + 2 * py + pc], send_sems, recv_sems, own_src=stages)
        keep_own = [pltpu.make_async_copy(stages[a], outs[a].at[me], local_sems.at[a]) for a in range(3)]

        @pl.when(i == 0)
        def _():
            for a, (src, dst) in enumerate(zip((win_ref, wout_ref, cp_ref), stages)):
                rows = src.shape[0]
                rc = min(rows, 32)

                def cast(r, carry, src=src, dst=dst, rc=rc):
                    dst[pl.ds(r, rc), :] = src[pl.ds(r, rc), :].astype(dst.dtype)
                    return carry

                _chunks(rows, rc, cast, 0)
                gather.start_own(a)
                keep_own[a].start()

        @pl.when(i < nb)
        def _():
            xv = x_ref[...]
            r0 = lax.rsqrt(jnp.mean(xv * xv, axis=-1, keepdims=True) + RMS_EPS)
            xn = xv * r0 * lng_ref[...]
            xnb[pl.ds(pl.multiple_of(i * tb, tb), tb), :] = xn.astype(MM)
            xnt_ref[...] = xn.T.astype(MM)

        for k in range(N_DEV):
            @pl.when(i == nb + k)
            def _(k=k):
                if k == 1:
                    gather.wait_sibling(0)
                elif 2 <= k <= 4:
                    gather.wait_chip_and_pass_on(0, k - 2)
                elif k >= 5:
                    gather.wait_passed_on(0, k - 5)
                if k == 0:
                    w_blk = st_in
                else:
                    load = pltpu.make_async_copy(win_all.at[order_ref[k]], wbuf, local_sems.at[3])
                    load.start()
                    load.wait()
                    w_blk = wbuf

                def project(r, carry):
                    rows = pl.ds(r, mc)
                    p_ref[rows, :] = _dot(xnb[rows, :], w_blk[...]).astype(MM)
                    return carry

                _chunks(t_len, mc, project, 0)
                if k == N_DEV - 1:
                    gather.wait_sends(0)
                    gather.finish(1)
                    gather.finish(2)
                    for cp in keep_own:
                        cp.wait()

    vm = pl.BlockSpec(memory_space=pltpu.VMEM)
    hbm = pl.BlockSpec(memory_space=pl.ANY)
    grid_spec = pltpu.PrefetchScalarGridSpec(
        num_scalar_prefetch=1, grid=(nb + N_DEV,),
        in_specs=[pl.BlockSpec((tb, d), lambda i, o: (jnp.minimum(i, nb - 1), 0)), vm, vm, vm, vm],
        out_specs=(pl.BlockSpec((t_len, cols), lambda i, o: (0, o[jnp.maximum(i - nb, 0)])),
                   pl.BlockSpec((d, tb), lambda i, o: (0, jnp.minimum(i, nb - 1))), hbm, hbm, hbm),
        scratch_shapes=[pltpu.VMEM((t_len, d), MM)] + [pltpu.VMEM(s.shape, dt) for s, dt in zip(srcs, dts)]
                       + [pltpu.VMEM(w_in.shape, MM),
                          pltpu.SemaphoreType.DMA((21,)), pltpu.SemaphoreType.DMA((21,)), pltpu.SemaphoreType.DMA((4,))])
    return pl.pallas_call(
        body, name="gather_project", grid_spec=grid_spec,
        out_shape=(jax.ShapeDtypeStruct((t_len, N_DEV * cols), MM),
                   jax.ShapeDtypeStruct((d, t_len), MM))
                  + tuple(jax.ShapeDtypeStruct((N_DEV,) + s.shape, dt) for s, dt in zip(srcs, dts)),
        compiler_params=_params(dimension_semantics=("arbitrary",)),
    )(_block_order(), x, w_in, w_out, conv_pack, ln_g)


def _forward(x, tgt, p, wout, wa_t, wi_t, sp, e_c, et_c, e_l, et_l, tb):
    t_len, d = x.shape
    nb = t_len // tb
    n_tiles, tw = wa_t.shape[0], wa_t.shape[1]
    hd_c, hd_l = d // N_CONV_HEADS, d // N_LRU_HEADS

    def body(x_ref, tgt_ref, p_ref, wout_ref, wa_ref, wi_ref, sp_ref, ec_ref, etc_ref, el_ref, etl_ref,
             h_ref, dh_ref, dhb_ref, acc_ref,
             pf, yc, u, pa, pi, rcf, rlf, ybuf, tail_z, tail_xl, hcar):
        i = pl.program_id(0)
        row = _row_iota(d)

        @pl.when(i == 0)
        def _():
            tail_z[...] = jnp.zeros_like(tail_z)
            tail_xl[...] = jnp.zeros_like(tail_xl)
            hcar[...] = jnp.zeros_like(hcar)
            acc_ref[...] = jnp.zeros_like(acc_ref)

        def spr(r):
            return sp_ref[r:r + 1, :]

        def widen(r, carry):
            pf[pl.ds(r, 16), :] = p_ref[pl.ds(r, 16), :].astype(F32)
            return carry

        _chunks(tb, 16, widen, 0)

        w0, w1, w2 = spr(SP_CONV_W), spr(SP_CONV_W + 1), spr(SP_CONV_W + 2)
        l0, l1, l2, l3 = spr(SP_LRU_W), spr(SP_LRU_W + 1), spr(SP_LRU_W + 2), spr(SP_LRU_W + 3)
        lb = spr(SP_LRU_B)

        def convs(r, carry):
            zp, xp = carry
            rows = pl.ds(r, SUBLANES)
            z = pf[rows, d:2 * d] * pf[rows, 2 * d:3 * d]
            cz = w0 * _shift_down(z, zp, 2, row) + w1 * _shift_down(z, zp, 1, row) + w2 * z
            yc[rows, :] = pf[rows, 0:d] * cz
            xl = pf[rows, 4 * d:5 * d]
            u[rows, :] = (l0 * _shift_down(xl, xp, 3, row) + l1 * _shift_down(xl, xp, 2, row)
                          + l2 * _shift_down(xl, xp, 1, row) + l3 * xl + lb)
            return z, xl

        z_last, xl_last = _chunks(tb, SUBLANES, convs, (tail_z[...], tail_xl[...]))
        tail_z[...] = z_last
        tail_xl[...] = xl_last

        ub = u[...].astype(MM)
        for k in range(n_tiles):
            sl = slice(k * tw, (k + 1) * tw)
            pa[:, sl] = _dot(ub[:, sl], wa_ref[k])
            pi[:, sl] = _dot(ub[:, sl], wi_ref[k])
        rcf[...] = _head_rstd(yc[...], ec_ref[...], etc_ref[...], hd_c)

        c8 = RG_LRU_C * _log_sigmoid(spr(SP_LAM))
        b_a, b_i = spr(SP_B_A), spr(SP_B_I)

        def lru(r, hp):
            rows = pl.ds(r, SUBLANES)
            ra = jax.nn.sigmoid(pa[rows, :] + b_a)
            ii = jax.nn.sigmoid(pi[rows, :] + b_i)
            la = ra * c8
            a = jnp.exp(la)
            mult = jnp.sqrt(_neg_expm1(2.0 * la))
            h = _scan_fwd(a, mult * (ii * u[rows, :]), hp, row)
            h_ref[rows, :] = h
            return _bcast_row(h, SUBLANES - 1)

        hcar[...] = _chunks(tb, SUBLANES, lru, hcar[...])
        rlf[...] = _head_rstd(h_ref[...], el_ref[...], etl_ref[...], hd_l)

        g_c, g_l = spr(SP_CONV_G), spr(SP_LRU_G)

        def gate(r, carry):
            rows = pl.ds(r, SUBLANES)
            gc = pf[rows, 3 * d:4 * d]
            ybuf[rows, 0:d] = yc[rows, :] * rcf[rows, :] * g_c * (gc * jax.nn.sigmoid(gc))
            gl = pf[rows, 5 * d:6 * d]
            ybuf[rows, d:2 * d] = h_ref[rows, :] * rlf[rows, :] * g_l * (gl * jax.nn.sigmoid(gl))
            return carry

        _chunks(tb, SUBLANES, gate, 0)

        hres = x_ref[...] + _dot(ybuf[...].astype(MM), wout_ref[...])
        rf = lax.rsqrt(jnp.mean(hres * hres, axis=-1, keepdims=True) + RMS_EPS)
        hn = hres * rf
        fg = spr(SP_FINAL_G)
        err = hn * fg - tgt_ref[...]
        dout = err * (1.0 / d)
        acc_ref[0:SUBLANES, :] += (err * err).reshape(tb // SUBLANES, SUBLANES, d).sum(axis=0)
        acc_ref[SUBLANES:2 * SUBLANES, :] += (dout * hn).reshape(tb // SUBLANES, SUBLANES, d).sum(axis=0)
        gd = dout * fg
        dhres = rf * (gd - hn * jnp.mean(gd * hn, axis=-1, keepdims=True))
        dh_ref[...] = dhres
        dhb_ref[...] = dhres.astype(MM)

    vm = pl.BlockSpec(memory_space=pltpu.VMEM)
    blk = lambda w: pl.BlockSpec((tb, w), lambda i: (i, 0))
    buf = lambda w: pltpu.VMEM((tb, w), F32)
    return pl.pallas_call(
        body, name="forward", grid=(nb,),
        in_specs=[blk(d), blk(d), blk(6 * d), vm, vm, vm, vm, vm, vm, vm, vm],
        out_specs=(blk(d), blk(d), blk(d), pl.BlockSpec((2 * SUBLANES, d), lambda i: (0, 0))),
        out_shape=(jax.ShapeDtypeStruct((t_len, d), F32),
                   jax.ShapeDtypeStruct((t_len, d), F32),
                   jax.ShapeDtypeStruct((t_len, d), MM),
                   jax.ShapeDtypeStruct((2 * SUBLANES, d), F32)),
        scratch_shapes=[buf(6 * d), buf(d), buf(d), buf(d), buf(d), buf(d), buf(d), buf(2 * d),
                        pltpu.VMEM((SUBLANES, d), F32), pltpu.VMEM((SUBLANES, d), F32), pltpu.VMEM((SUBLANES, d), F32)],
        compiler_params=_params(dimension_semantics=("arbitrary",)),
    )(x, tgt, p, wout, wa_t, wi_t, sp, e_c, et_c, e_l, et_l)


def _backward(p, h, dh, wout, wa_t, wi_t, sp, e_c, et_c, e_l, et_l, tb):
    t_len, d = h.shape
    nb = t_len // tb
    n_tiles, tw = wa_t.shape[0], wa_t.shape[1]
    hd_c, hd_l = d // N_CONV_HEADS, d // N_LRU_HEADS
    ph = 16
    s8 = SUBLANES

    def body(p_ref, phalo_ref, h_ref, hhalo_ref, dh_ref, wout_ref, wa_ref, wi_ref, sp_ref,
             ec_ref, etc_ref, el_ref, etl_ref,
             dp_ref, yt_ref, gwa_ref, gwi_ref, acc_ref,
             pf, hh, dy, ybuf, yc, czs, u, pa, pi, rcf, rlf, qc, ql, dyc_hat, dyl_hat, dpa, dpi, du, dpf,
             car_dcz, car_a, car_g, car_du):
        i = pl.program_id(0)
        blk_idx = nb - 1 - i
        row = _row_iota(d)

        @pl.when(i == 0)
        def _():
            for ref in (car_dcz, car_a, car_g, car_du, gwa_ref, gwi_ref, acc_ref):
                ref[...] = jnp.zeros_like(ref)

        def spr(r):
            return sp_ref[r:r + 1, :]

        def acc_add(group, val):
            acc_ref[group * s8:(group + 1) * s8, :] += val

        live = jnp.where(blk_idx > 0, 1.0, 0.0).astype(F32)
        pf[0:ph, :] = phalo_ref[...].astype(F32) * live
        hh[0:s8, :] = hhalo_ref[...] * live
        hh[s8:, :] = h_ref[...]

        def widen(r, carry):
            pf[pl.ds(ph + r, 16), :] = p_ref[pl.ds(r, 16), :].astype(F32)
            return carry

        _chunks(tb, 16, widen, 0)

        dy[...] = _dot_nt(dh_ref[...].astype(MM), wout_ref[...])

        w0, w1, w2 = spr(SP_CONV_W), spr(SP_CONV_W + 1), spr(SP_CONV_W + 2)
        l0, l1, l2, l3 = spr(SP_LRU_W), spr(SP_LRU_W + 1), spr(SP_LRU_W + 2), spr(SP_LRU_W + 3)
        lb = spr(SP_LRU_B)

        def zx(r):
            rows = pl.ds(ph + r, s8)
            return pf[rows, d:2 * d] * pf[rows, 2 * d:3 * d], pf[rows, 4 * d:5 * d]

        def recompute(r, carry):
            rows = pl.ds(r, s8)
            z, xl = zx(r)
            zp, xp = zx(r - s8)
            cz = w0 * _shift_down(z, zp, 2, row) + w1 * _shift_down(z, zp, 1, row) + w2 * z
            czs[rows, :] = cz
            yc[rows, :] = pf[pl.ds(ph + r, s8), 0:d] * cz
            u[rows, :] = (l0 * _shift_down(xl, xp, 3, row) + l1 * _shift_down(xl, xp, 2, row)
                          + l2 * _shift_down(xl, xp, 1, row) + l3 * xl + lb)
            return carry

        _chunks(tb, s8, recompute, 0)

        ub = u[...].astype(MM)
        for k in range(n_tiles):
            sl = slice(k * tw, (k + 1) * tw)
            pa[:, sl] = _dot(ub[:, sl], wa_ref[k])
            pi[:, sl] = _dot(ub[:, sl], wi_ref[k])
        rcf[...] = _head_rstd(yc[...], ec_ref[...], etc_ref[...], hd_c)
        rlf[...] = _head_rstd(h_ref[...], el_ref[...], etl_ref[...], hd_l)

        g_c, g_l = spr(SP_CONV_G), spr(SP_LRU_G)

        def gates(r, carry):
            rows = pl.ds(r, s8)
            prow = pl.ds(ph + r, s8)
            for (off_g, off_y, src, rstd, gain, q, dhat, grp) in (
                    (3 * d, 0, yc, rcf, g_c, qc, dyc_hat, A_CONV_G),
                    (5 * d, d, h_ref, rlf, g_l, ql, dyl_hat, A_LRU_G)):
                gt = pf[prow, off_g:off_g + d]
                sg = jax.nn.sigmoid(gt)
                silu = gt * sg
                yhat = src[rows, :] * rstd[rows, :]
                nrm = yhat * gain
                ybuf[rows, off_y:off_y + d] = nrm * silu
                dout = dy[rows, off_y:off_y + d]
                dnrm = dout * silu
                dpf[rows, off_g:off_g + d] = dout * nrm * (sg * (1.0 + gt * (1.0 - sg)))
                acc_add(grp, dnrm * yhat)
                dh_ = dnrm * gain
                dhat[rows, :] = dh_
                q[rows, :] = dh_ * yhat
            return carry

        _chunks(tb, s8, gates, 0)

        qc[...] = _head_mean(qc[...], ec_ref[...], etc_ref[...], hd_c)
        ql[...] = _head_mean(ql[...], el_ref[...], etl_ref[...], hd_l)
        yt_ref[...] = ybuf[...].T.astype(MM)

        c8 = RG_LRU_C * _log_sigmoid(spr(SP_LAM))
        b_a, b_i = spr(SP_B_A), spr(SP_B_I)

        def mixers(r, carry):
            dcz_n, a_n, g_n = carry
            rows = pl.ds(r, s8)
            prow = pl.ds(ph + r, s8)
            rstd = rcf[rows, :]
            yhat = yc[rows, :] * rstd
            dyc = rstd * (dyc_hat[rows, :] - yhat * qc[rows, :])
            dpf[rows, 0:d] = dyc * czs[rows, :]
            dcz = dyc * pf[prow, 0:d]
            dz = w2 * dcz + w1 * _shift_up(dcz, dcz_n, 1, row) + w0 * _shift_up(dcz, dcz_n, 2, row)
            z, _ = zx(r)
            zp, _ = zx(r - s8)
            dpf[rows, d:2 * d] = dz * pf[prow, 2 * d:3 * d]
            dpf[rows, 2 * d:3 * d] = dz * pf[prow, d:2 * d]
            acc_add(A_CONV_W, dcz * _shift_down(z, zp, 2, row))
            acc_add(A_CONV_W + 1, dcz * _shift_down(z, zp, 1, row))
            acc_add(A_CONV_W + 2, dcz * z)
            rstd = rlf[rows, :]
            hcur = hh[pl.ds(s8 + r, s8), :]
            hhat = hcur * rstd
            dh_out = rstd * (dyl_hat[rows, :] - hhat * ql[rows, :])
            ra = jax.nn.sigmoid(pa[rows, :] + b_a)
            la = ra * c8
            a = jnp.exp(la)
            g = _scan_bwd(_shift_up(a, a_n, 1, row), dh_out, g_n, row)
            da = g * _shift_down(hcur, hh[pl.ds(r, s8), :], 1, row)
            ii = jax.nn.sigmoid(pi[rows, :] + b_i)
            uu = u[rows, :]
            mult = jnp.sqrt(_neg_expm1(2.0 * la))
            dmult = g * (ii * uu)
            ds = g * mult
            dla = a * (da - dmult * a / mult)
            acc_add(A_LAM, dla * ra)
            dpa_ = dla * c8 * ra * (1.0 - ra)
            dpi_ = ds * uu * ii * (1.0 - ii)
            acc_add(A_B_A, dpa_)
            acc_add(A_B_I, dpi_)
            dpa[rows, :] = dpa_
            dpi[rows, :] = dpi_
            du[rows, :] = ds * ii
            return dcz, a, _bcast_row(g, 0)

        dcz_f, a_f, g_f = _chunks(tb, s8, mixers, (car_dcz[...], car_a[...], car_g[...]), reverse=True)
        car_dcz[...] = dcz_f
        car_a[...] = a_f
        car_g[...] = g_f

        dpab = dpa[...].astype(MM)
        dpib = dpi[...].astype(MM)
        for k in range(n_tiles):
            sl = slice(k * tw, (k + 1) * tw)
            du[:, sl] += _dot_nt(dpab[:, sl], wa_ref[k]) + _dot_nt(dpib[:, sl], wi_ref[k])
            ut = u[:, sl].T.astype(MM)
            gwa_ref[k] += _dot(ut, dpab[:, sl])
            gwi_ref[k] += _dot(ut, dpib[:, sl])

        def lru_conv(r, du_n):
            rows = pl.ds(r, s8)
            dut = du[rows, :]
            dpf[rows, 4 * d:5 * d] = (l3 * dut + l2 * _shift_up(dut, du_n, 1, row)
                                      + l1 * _shift_up(dut, du_n, 2, row) + l0 * _shift_up(dut, du_n, 3, row))
            _, xl = zx(r)
            _, xp = zx(r - s8)
            acc_add(A_LRU_W, dut * _shift_down(xl, xp, 3, row))
            acc_add(A_LRU_W + 1, dut * _shift_down(xl, xp, 2, row))
            acc_add(A_LRU_W + 2, dut * _shift_down(xl, xp, 1, row))
            acc_add(A_LRU_W + 3, dut * xl)
            acc_add(A_LRU_B, dut)
            return dut

        car_du[...] = _chunks(tb, s8, lru_conv, car_du[...], reverse=True)

        def narrow(r, carry):
            dp_ref[pl.ds(r, 16), :] = dpf[pl.ds(r, 16), :].astype(MM)
            return carry

        _chunks(tb, 16, narrow, 0)

    vm = pl.BlockSpec(memory_space=pltpu.VMEM)
    rev = lambda w: pl.BlockSpec((tb, w), lambda i: (nb - 1 - i, 0))
    halo = lambda rows, w: pl.BlockSpec((rows, w), lambda i: (jnp.maximum((nb - 1 - i) * (tb // rows) - 1, 0), 0))
    const = lambda shape: pl.BlockSpec(shape, lambda i: (0,) * len(shape))
    buf = lambda w: pltpu.VMEM((tb, w), F32)
    car = pltpu.VMEM((SUBLANES, d), F32)
    return pl.pallas_call(
        body, name="backward", grid=(nb,),
        in_specs=[rev(6 * d), halo(ph, 6 * d), rev(d), halo(SUBLANES, d), rev(d), vm, vm, vm, vm, vm, vm, vm, vm],
        out_specs=(rev(6 * d), pl.BlockSpec((2 * d, tb), lambda i: (0, nb - 1 - i)),
                   const((n_tiles, tw, tw)), const((n_tiles, tw, tw)), const((A_GROUPS * SUBLANES, d))),
        out_shape=(jax.ShapeDtypeStruct((t_len, 6 * d), MM),
                   jax.ShapeDtypeStruct((2 * d, t_len), MM),
                   jax.ShapeDtypeStruct((n_tiles, tw, tw), F32),
                   jax.ShapeDtypeStruct((n_tiles, tw, tw), F32),
                   jax.ShapeDtypeStruct((A_GROUPS * SUBLANES, d), F32)),
        scratch_shapes=[pltpu.VMEM((ph + tb, 6 * d), F32), pltpu.VMEM((SUBLANES + tb, d), F32), buf(2 * d), buf(2 * d)]
                       + [buf(d)] * 14 + [buf(6 * d), car, car, car, car],
        compiler_params=_params(dimension_semantics=("arbitrary",)),
    )(p, p, h, h, dh, wout, wa_t, wi_t, sp, e_c, et_c, e_l, et_l)


def _input_grad(dp, win_all, x, dh, sp, tb):
    t_len, d = x.shape
    nb = t_len // tb
    cols = win_all.shape[2]

    def body(dp_ref, win_ref, x_ref, dh_ref, sp_ref, gx_ref, acc_ref):
        @pl.when(pl.program_id(0) == 0)
        def _():
            acc_ref[...] = jnp.zeros_like(acc_ref)

        dxn = _dot_nt(dp_ref[:, 0:cols], win_ref[0])
        for j in range(1, N_DEV):
            dxn += _dot_nt(dp_ref[:, j * cols:(j + 1) * cols], win_ref[j])
        xv = x_ref[...]
        r0 = lax.rsqrt(jnp.mean(xv * xv, axis=-1, keepdims=True) + RMS_EPS)
        xhat = xv * r0
        acc_ref[...] += (dxn * xhat).reshape(tb // SUBLANES, SUBLANES, d).sum(axis=0)
        dxh = dxn * sp_ref[SP_LN_G:SP_LN_G + 1, :]
        gx_ref[...] = dh_ref[...] + r0 * (dxh - xhat * jnp.mean(dxh * xhat, axis=-1, keepdims=True))

    vm = pl.BlockSpec(memory_space=pltpu.VMEM)
    blk = lambda w: pl.BlockSpec((tb, w), lambda i: (i, 0))
    return pl.pallas_call(
        body, name="input_grad", grid=(nb,),
        in_specs=[blk(6 * d), vm, blk(d), blk(d), vm],
        out_specs=(blk(d), pl.BlockSpec((SUBLANES, d), lambda i: (0, 0))),
        out_shape=(jax.ShapeDtypeStruct((t_len, d), F32), jax.ShapeDtypeStruct((SUBLANES, d), F32)),
        compiler_params=_params(dimension_semantics=("arbitrary",)),
    )(dp, win_all, x, dh, sp)


def _weight_grad_in(xnt, dp):
    d, t_len = xnt.shape
    cols = dp.shape[1] // N_DEV
    half = d // 2

    def body(xnt_ref, dp_ref, o_ref):
        for s in range(2):
            o_ref[0, s * half:(s + 1) * half, :] = _dot(xnt_ref[s * half:(s + 1) * half, :], dp_ref[...]).astype(MM)

    return pl.pallas_call(
        body, name="weight_grad_in", grid=(N_DEV,),
        in_specs=[pl.BlockSpec(memory_space=pltpu.VMEM), pl.BlockSpec((t_len, cols), lambda j: (0, j))],
        out_specs=pl.BlockSpec((1, d, cols), lambda j: (j, 0, 0)),
        out_shape=jax.ShapeDtypeStruct((N_DEV, d, cols), MM),
        compiler_params=_params(dimension_semantics=("arbitrary",)),
    )(xnt, dp)


def _weight_grad_out(yt, dhb):
    d2, t_len = yt.shape
    d = dhb.shape[1]
    rows = d2 // N_DEV

    def body(yt_ref, dhb_ref, o_ref):
        o_ref[0] = _dot(yt_ref[...], dhb_ref[...]).astype(MM)

    return pl.pallas_call(
        body, name="weight_grad_out", grid=(N_DEV,),
        in_specs=[pl.BlockSpec((rows, t_len), lambda j: (j, 0)), pl.BlockSpec(memory_space=pltpu.VMEM)],
        out_specs=pl.BlockSpec((1, rows, d), lambda j: (j, 0, 0)),
        out_shape=jax.ShapeDtypeStruct((N_DEV, rows, d), MM),
        compiler_params=_params(dimension_semantics=("arbitrary",)),
    )(yt, dhb)


def _reduce_scatter_adamw(g_all, w, m, v, name):
    n_rows, n_cols = w.shape
    rc = 32

    def body(g_ref, w_ref, m_ref, v_ref, grad_ref, delta_ref, mo_ref, vo_ref,
             from_sib, to_chip, from_chip, send1, recv1, send2, recv2):
        x, y, c = _mesh_pos()
        sibling = (x, y, 1 - c)
        rel = [(0, 0), (1, 0), (0, 1), (1, 1)]

        def block(k, core):
            fx, fy = rel[k]
            return 4 * (x ^ fx) + 2 * (y ^ fy) + core

        def chip(k):
            fx, fy = rel[k]
            return (x ^ fx, y ^ fy, c)

        stage1 = [pltpu.make_async_remote_copy(
            src_ref=g_ref.at[block(k, 1 - c)], dst_ref=from_sib.at[k], send_sem=send1.at[k], recv_sem=recv1.at[k],
            device_id=sibling, device_id_type=MESH) for k in range(4)]
        for cp in stage1:
            cp.start()
        stage2 = []
        for k in (1, 2, 3):
            stage1[k].wait_recv()
            bk = block(k, c)

            def partial(r, carry, k=k, bk=bk):
                rows = pl.ds(r, rc)
                to_chip[k - 1, rows, :] = (g_ref[bk, rows, :].astype(F32) + from_sib[k, rows, :].astype(F32)).astype(MM)
                return carry

            _chunks(n_rows, rc, partial, 0)
            cp = pltpu.make_async_remote_copy(
                src_ref=to_chip.at[k - 1], dst_ref=from_chip.at[k - 1], send_sem=send2.at[k - 1], recv_sem=recv2.at[k - 1],
                device_id=chip(k), device_id_type=MESH)
            cp.start()
            stage2.append(cp)
        stage1[0].wait_recv()
        for cp in stage2:
            cp.wait_recv()
        b0 = block(0, c)

        def update(r, carry):
            rows = pl.ds(r, rc)
            g = g_ref[b0, rows, :].astype(F32) + from_sib[0, rows, :].astype(F32)
            for k in range(3):
                g = g + from_chip[k, rows, :].astype(F32)
            delta, m_new, v_new = _adamw(w_ref[rows, :], g, m_ref[rows, :], v_ref[rows, :])
            grad_ref[rows, :] = g
            delta_ref[rows, :] = delta
            mo_ref[rows, :] = m_new
            vo_ref[rows, :] = v_new
            return carry

        _chunks(n_rows, rc, update, 0)
        for cp in stage1 + stage2:
            cp.wait_send()

    vm = pl.BlockSpec(memory_space=pltpu.VMEM)
    out = jax.ShapeDtypeStruct((n_rows, n_cols), F32)
    return pl.pallas_call(
        body, name=name,
        in_specs=[vm, vm, vm, vm], out_specs=(vm, vm, vm, vm), out_shape=(out, out, out, out),
        scratch_shapes=[pltpu.VMEM((4, n_rows, n_cols), MM), pltpu.VMEM((3, n_rows, n_cols), MM),
                        pltpu.VMEM((3, n_rows, n_cols), MM),
                        pltpu.SemaphoreType.DMA((4,)), pltpu.SemaphoreType.DMA((4,)),
                        pltpu.SemaphoreType.DMA((3,)), pltpu.SemaphoreType.DMA((3,))],
        compiler_params=_params(),
    )(g_all, w, m, v)


def _small_update(facc, xacc, bacc, gwa, gwi, lam, wsl, msl, vsl, cw, cm, cv):
    n_rows, d = wsl.shape
    s8 = SUBLANES
    cc = cw.shape[1]

    def body(facc_ref, xacc_ref, bacc_ref, gwa_ref, gwi_ref, lam_ref, w_ref, m_ref, v_ref, cw_ref, cm_ref, cv_ref,
             g_o, d_o, m_o, v_o, cg_o, cd_o, cm_o, cv_o, gat, send_sems, recv_sems):
        x, y, c = _mesh_pos()
        me = 4 * x + 2 * y + c

        def rowsum(ref, group):
            return jnp.sum(ref[group * s8:(group + 1) * s8, :], axis=0, keepdims=True)

        mine = gat.at[me]
        loss = jnp.sum(rowsum(facc_ref, 0), axis=1, keepdims=True) * (0.5 / d)
        mine[SL_LOSS:SL_LOSS + 1, :] = jnp.broadcast_to(loss, (1, d))
        mine[SL_LN_G:SL_LN_G + 1, :] = rowsum(xacc_ref, 0)
        mine[SL_LRU_B:SL_LRU_B + 1, :] = rowsum(bacc_ref, A_LRU_B)
        mine[SL_B_A:SL_B_A + 1, :] = rowsum(bacc_ref, A_B_A)
        mine[SL_B_I:SL_B_I + 1, :] = rowsum(bacc_ref, A_B_I)
        mine[SL_LAM:SL_LAM + 1, :] = rowsum(bacc_ref, A_LAM) * (RG_LRU_C * jax.nn.sigmoid(-lam_ref[...]))
        mine[SL_CONV_G:SL_CONV_G + 1, :] = rowsum(bacc_ref, A_CONV_G)
        mine[SL_LRU_G:SL_LRU_G + 1, :] = rowsum(bacc_ref, A_LRU_G)
        mine[SL_FINAL_G:SL_FINAL_G + 1, :] = rowsum(facc_ref, 1)
        for k in range(3):
            mine[SL_CONV_W + k:SL_CONV_W + k + 1, :] = rowsum(bacc_ref, A_CONV_W + k)
        for k in range(4):
            mine[SL_LRU_W + k:SL_LRU_W + k + 1, :] = rowsum(bacc_ref, A_LRU_W + k)
        na = gwa_ref.shape[0]
        mine[SL_W_A:SL_W_A + na, :] = gwa_ref[...]
        mine[SL_W_A + na:SL_W_A + 2 * na, :] = gwi_ref[...]

        gather = _Gather(lambda a, px, py, pc: gat.at[4 * px + 2 * py + pc], send_sems, recv_sems)
        gather.start_own(0)
        gather.finish(0)

        def update(r, carry):
            rows = pl.ds(r, s8)
            g = gat[0, rows, :]
            for b in range(1, N_DEV):
                g = g + gat[b, rows, :]
            delta, m_new, v_new = _adamw(w_ref[rows, :], g, m_ref[rows, :], v_ref[rows, :])
            g_o[rows, :] = g
            d_o[rows, :] = delta
            m_o[rows, :] = m_new
            v_o[rows, :] = v_new
            return carry

        _chunks(n_rows, s8, update, 0)
        gc = g_o[s8:2 * s8, pl.ds(pl.multiple_of(me * cc, cc), cc)]
        delta, m_new, v_new = _adamw(cw_ref[...], gc, cm_ref[...], cv_ref[...])
        cg_o[...] = gc
        cd_o[...] = delta
        cm_o[...] = m_new
        cv_o[...] = v_new

    vm = pl.BlockSpec(memory_space=pltpu.VMEM)
    big = jax.ShapeDtypeStruct((n_rows, d), F32)
    small = jax.ShapeDtypeStruct(cw.shape, F32)
    return pl.pallas_call(
        body, name="small_update",
        in_specs=[vm] * 12, out_specs=(vm,) * 8, out_shape=(big, big, big, big, small, small, small, small),
        scratch_shapes=[pltpu.VMEM((N_DEV, n_rows, d), F32), pltpu.SemaphoreType.DMA((7,)), pltpu.SemaphoreType.DMA((7,))],
        compiler_params=_params(),
    )(facc, xacc, bacc, gwa, gwi, lam, wsl, msl, vsl, cw, cm, cv)


def _head_selectors(d, n_heads):
    lane = jnp.arange(d)[:, None] // (d // n_heads)
    e = (lane == jnp.arange(LANES)[None, :]).astype(MM)
    return e, e.T


def _gate_tiles(w, tw):
    n_heads, hd, _ = w.shape
    per = tw // hd
    w4 = w.reshape(n_heads // per, per, hd, hd)
    eye = jnp.eye(per, dtype=w.dtype)
    return (w4[:, :, :, None, :] * eye[None, :, None, :, None]).reshape(n_heads // per, tw, tw)


def _gate_blocks(tiles, n_heads, hd):
    n_tiles, tw, _ = tiles.shape
    per = tw // hd
    t5 = tiles.reshape(n_tiles, per, hd, per, hd)
    diag = jnp.stack([t5[:, a, :, a, :] for a in range(per)], axis=1)
    return diag.reshape(hd, n_heads * hd)


def kernel(x, ln_g, w_in, conv_w, lru_conv_w, lru_conv_b, w_a, b_a, w_i, b_i, lam, conv_out_g, lru_out_g, w_out, final_g, loss_target, m_ln_g, m_w_in, m_conv_w, m_lru_conv_w, m_lru_conv_b, m_w_a, m_b_a, m_w_i, m_b_i, m_lam, m_conv_out_g, m_lru_out_g, m_w_out, m_final_g, v_ln_g, v_w_in, v_conv_w, v_lru_conv_w, v_lru_conv_b, v_w_a, v_b_a, v_w_i, v_b_i, v_lam, v_conv_out_g, v_lru_out_g, v_w_out, v_final_g):
    _, t_len, d = x.shape
    hd_l = d // N_LRU_HEADS
    tw = min(MXU_TILE, d)
    cc = conv_w.shape[1]
    tb = 128
    x2, tgt2 = x[0], loss_target[0]

    def conv_rows(cw3, lw4):
        return jnp.concatenate([jnp.zeros((1, cc), F32), cw3, lw4], axis=0)

    p, xnt, win_all, wout_all, conv_all = _gather_project(
        x2, w_in, w_out, conv_rows(conv_w, lru_conv_w), ln_g.reshape(1, d), min(256, t_len))
    wout_full = wout_all.reshape(N_DEV * w_out.shape[0], d)
    conv_full = conv_all.transpose(1, 0, 2).reshape(SUBLANES, d)
    small = [ln_g, lru_conv_b, b_a, b_i, lam, conv_out_g, lru_out_g, final_g]
    sp = jnp.concatenate([jnp.stack(small), conv_full[1:], jnp.zeros((1, d), F32)], axis=0)
    wa_t, wi_t = _gate_tiles(w_a, tw).astype(MM), _gate_tiles(w_i, tw).astype(MM)
    e_c, et_c = _head_selectors(d, N_CONV_HEADS)
    e_l, et_l = _head_selectors(d, N_LRU_HEADS)

    h, dh, dhb, facc = _forward(x2, tgt2, p, wout_full, wa_t, wi_t, sp, e_c, et_c, e_l, et_l, min(256, t_len))
    dp, yt, gwa_t, gwi_t, bacc = _backward(p, h, dh, wout_full, wa_t, wi_t, sp, e_c, et_c, e_l, et_l, tb)
    grad_x, xacc = _input_grad(dp, win_all, x2, dh, sp, min(512, t_len))
    g_in = _weight_grad_in(xnt, dp)
    g_out = _weight_grad_out(yt, dhb)
    gw_in, dw_in, mw_in, vw_in = _reduce_scatter_adamw(g_in, w_in, m_w_in, v_w_in, "reduce_w_in")
    gw_out, dw_out, mw_out, vw_out = _reduce_scatter_adamw(g_out, w_out, m_w_out, v_w_out, "reduce_w_out")

    def slab(parts, wa_, wi_):
        return jnp.concatenate([jnp.zeros((1, d), F32), jnp.stack(parts), jnp.zeros((SL_W_A - SL_CONV_W, d), F32),
                                wa_.reshape(-1, d), wi_.reshape(-1, d)], axis=0)

    wsl = slab(small, w_a, w_i)
    msl = slab([m_ln_g, m_lru_conv_b, m_b_a, m_b_i, m_lam, m_conv_out_g, m_lru_out_g, m_final_g], m_w_a, m_w_i)
    vsl = slab([v_ln_g, v_lru_conv_b, v_b_a, v_b_i, v_lam, v_conv_out_g, v_lru_out_g, v_final_g], v_w_a, v_w_i)
    outs = _small_update(
        facc, xacc, bacc, _gate_blocks(gwa_t, N_LRU_HEADS, hd_l), _gate_blocks(gwi_t, N_LRU_HEADS, hd_l),
        lam.reshape(1, d), wsl, msl, vsl,
        conv_rows(conv_w, lru_conv_w), conv_rows(m_conv_w, m_lru_conv_w), conv_rows(v_conv_w, v_lru_conv_w))
    sl_g, sl_d, sl_m, sl_v, c_g, c_d, c_m, c_v = outs
    loss = sl_g[SL_LOSS, 0]
    na = w_a.size // d

    def unpack(sl, cv, big_in, big_out):
        one = lambda r: sl[r]
        return [one(SL_LN_G), big_in, cv[1:4], cv[4:8], one(SL_LRU_B),
                sl[SL_W_A:SL_W_A + na].reshape(w_a.shape), one(SL_B_A),
                sl[SL_W_A + na:SL_W_A + 2 * na].reshape(w_i.shape), one(SL_B_I), one(SL_LAM),
                one(SL_CONV_G), one(SL_LRU_G), big_out, one(SL_FINAL_G)]

    return (loss, grad_x[None], *unpack(sl_g, c_g, gw_in, gw_out), *unpack(sl_d, c_d, dw_in, dw_out),
            *unpack(sl_m, c_m, mw_in, mw_out), *unpack(sl_v, c_v, vw_in, vw_out))
```

```python
import functools

import jax
import jax.numpy as jnp
from jax import lax
from jax.experimental import pallas as pl
from jax.experimental.pallas import tpu as pltpu

F32 = jnp.float32
MM = jnp.bfloat16
MESH = pl.DeviceIdType.MESH

N_DEV = 8
N_CONV_HEADS = 8
N_LRU_HEADS = 16
RG_LRU_C = 8.0
RMS_EPS = 1e-6
ADAM_LR, ADAM_B1, ADAM_B2, ADAM_EPS, ADAM_WD, ADAM_STEP = 0.001, 0.9, 0.999, 1e-08, 0.01, 10
ADAM_BC1 = 1.0 - ADAM_B1 ** ADAM_STEP
ADAM_BC2 = 1.0 - ADAM_B2 ** ADAM_STEP

SUBLANES = 8
LANES = 128
MXU_TILE = 256
VMEM_LIMIT = 56 * 1024 * 1024

SP_LN_G, SP_LRU_B, SP_B_A, SP_B_I, SP_LAM, SP_CONV_G, SP_LRU_G, SP_FINAL_G, SP_CONV_W, SP_LRU_W = 0, 1, 2, 3, 4, 5, 6, 7, 8, 11
SP_ROWS = 16
A_CONV_G, A_LRU_G, A_LAM, A_B_A, A_B_I, A_CONV_W, A_LRU_W, A_LRU_B = 0, 1, 2, 3, 4, 5, 8, 12
A_GROUPS = 13
SL_LOSS, SL_LN_G, SL_LRU_B, SL_B_A, SL_B_I, SL_LAM, SL_CONV_G, SL_LRU_G, SL_FINAL_G, SL_CONV_W, SL_LRU_W, SL_W_A = 0, 1, 2, 3, 4, 5, 6, 7, 8, 9, 12, 16


def _params(vmem=True, **kw):
    if vmem:
        kw["vmem_limit_bytes"] = VMEM_LIMIT
    return pltpu.CompilerParams(**kw)


def _dot(a, b):
    return jnp.dot(a, b, preferred_element_type=F32)


def _dot_nt(a, b):
    return lax.dot_general(a, b, (((1,), (1,)), ((), ())), preferred_element_type=F32)


def _split3(v):
    hi = v.astype(MM)
    r1 = v - hi.astype(F32)
    mid = r1.astype(MM)
    lo = (r1 - mid.astype(F32)).astype(MM)
    return hi, mid, lo


def _dot3(v, e):
    hi, mid, lo = _split3(v)
    return _dot(hi, e) + _dot(mid, e) + _dot(lo, e)


def _head_rstd(v, e, et, head_dim):
    ms = _dot3(v * v, e) * (1.0 / head_dim)
    return _dot3(lax.rsqrt(ms + RMS_EPS), et)


def _head_mean(v, e, et, head_dim):
    return _dot3(_dot3(v, e) * (1.0 / head_dim), et)


def _neg_expm1(x):
    series = -x * (1.0 + x * (0.5 + x * (1.0 / 6.0 + x * (1.0 / 24.0 + x * (1.0 / 120.0)))))
    return jnp.where(x > -0.03, series, 1.0 - jnp.exp(x))


def _log_sigmoid(x):
    z = jnp.exp(-jnp.abs(x))
    u = 1.0 + z
    log1p_z = jnp.where(u == 1.0, z, jnp.log(u) * (z / (u - 1.0)))
    return jnp.minimum(x, 0.0) - log1p_z


def _row_iota(d):
    return lax.broadcasted_iota(jnp.int32, (SUBLANES, d), 0)


def _shift_down(cur, prev, s, row):
    return jnp.where(row >= s, pltpu.roll(cur, s, axis=0), pltpu.roll(prev, s, axis=0))


def _shift_up(cur, nxt, s, row):
    k = SUBLANES - s
    return jnp.where(row < k, pltpu.roll(cur, k, axis=0), pltpu.roll(nxt, k, axis=0))


def _scan_fwd(a, b, h_prev, row):
    for s in (1, 2, 4):
        a_s = jnp.where(row >= s, pltpu.roll(a, s, axis=0), 1.0)
        b_s = jnp.where(row >= s, pltpu.roll(b, s, axis=0), 0.0)
        b = a * b_s + b
        a = a * a_s
    return a * h_prev + b


def _scan_bwd(a_next, b, g_next, row):
    a = a_next
    for s in (1, 2, 4):
        k = SUBLANES - s
        a_s = jnp.where(row < k, pltpu.roll(a, k, axis=0), 1.0)
        b_s = jnp.where(row < k, pltpu.roll(b, k, axis=0), 0.0)
        b = a * b_s + b
        a = a * a_s
    return a * g_next + b


def _bcast_row(v, r):
    return jnp.broadcast_to(v[r:r + 1, :], v.shape)


def _chunks(n_rows, rc, body, init, reverse=False):
    n = n_rows // rc

    def step(i, carry):
        j = (n - 1 - i) if reverse else i
        return body(pl.multiple_of(j * rc, rc), carry)

    return lax.fori_loop(0, n, step, init)


def _adamw(w, g, m, v):
    m = ADAM_B1 * m + (1.0 - ADAM_B1) * g
    v = ADAM_B2 * v + (1.0 - ADAM_B2) * (g * g)
    m_hat = m / ADAM_BC1
    v_hat = v / ADAM_BC2
    delta = -ADAM_LR * (m_hat / (jnp.sqrt(v_hat) + ADAM_EPS) + ADAM_WD * w)
    return delta, m, v


def _mesh_pos():
    return lax.axis_index("x"), lax.axis_index("y"), lax.axis_index("c")


class _Gather:
    def __init__(self, blocks_of, send_sems, recv_sems, own_src=None):
        x, y, c = _mesh_pos()
        self.c = c
        self.me, self.sibling = (x, y, c), (x, y, 1 - c)
        self.chips = [(1 - x, y), (x, 1 - y), (1 - x, 1 - y)]
        self.blocks_of, self.send_sems, self.recv_sems = blocks_of, send_sems, recv_sems
        self.own_src = own_src

    def copy(self, a, k, block, to):
        src = self.blocks_of(a, *block)
        if block is self.me and self.own_src is not None:
            src = self.own_src[a]
        return pltpu.make_async_remote_copy(
            src_ref=src, dst_ref=self.blocks_of(a, *block),
            send_sem=self.send_sems.at[a * 7 + k], recv_sem=self.recv_sems.at[a * 7 + k],
            device_id=to, device_id_type=MESH)

    def start_own(self, a):
        self.copy(a, 0, self.me, self.sibling).start()
        for j, chip in enumerate(self.chips):
            self.copy(a, 1 + j, self.me, (*chip, self.c)).start()

    def wait_sibling(self, a):
        self.copy(a, 0, self.sibling, self.me).wait_recv()

    def wait_chip_and_pass_on(self, a, j):
        block = (*self.chips[j], self.c)
        self.copy(a, 1 + j, block, self.me).wait_recv()
        self.copy(a, 4 + j, block, self.sibling).start()

    def wait_passed_on(self, a, j):
        self.copy(a, 4 + j, (*self.chips[j], 1 - self.c), self.me).wait_recv()

    def wait_sends(self, a):
        self.copy(a, 0, self.me, self.sibling).wait_send()
        for j, chip in enumerate(self.chips):
            self.copy(a, 1 + j, self.me, (*chip, self.c)).wait_send()
            self.copy(a, 4 + j, (*chip, self.c), self.sibling).wait_send()

    def finish(self, a):
        for j in range(3):
            self.wait_chip_and_pass_on(a, j)
        self.wait_sibling(a)
        for j in range(3):
            self.wait_passed_on(a, j)
        self.wait_sends(a)


def _block_order():
    x, y, c = _mesh_pos()
    chips = [(1 - x, y), (x, 1 - y), (1 - x, 1 - y)]
    idx = lambda px, py, pc: 4 * px + 2 * py + pc
    order = [idx(x, y, c), idx(x, y, 1 - c)] + [idx(*ch, c) for ch in chips] + [idx(*ch, 1 - c) for ch in chips]
    return jnp.stack(order).astype(jnp.int32)


def _gather_project(x, w_in, w_out, conv_pack, ln_g, tb):
    t_len, d = x.shape
    nb = t_len // tb
    cols = w_in.shape[1]
    mc = min(512, t_len)
    srcs = (w_in, w_out, conv_pack)
    dts = (MM, MM, F32)

    def body(order_ref, x_ref, win_ref, wout_ref, cp_ref, lng_ref, p_ref, xnt_ref, win_all, wout_all, cp_all,
             xnb, st_in, st_out, st_cp, wbuf, send_sems, recv_sems, local_sems):
        i = pl.program_id(0)
        x_, y_, c_ = _mesh_pos()
        me = 4 * x_ + 2 * y_ + c_
        outs = (win_all, wout_all, cp_all)
        stages = (st_in, st_out, st_cp)
        gather = _Gather(lambda a, px, py, pc: outs[a].at[4 * px + 2 * py + pc], send_sems, recv_sems, own_src=stages)
        keep_own = [pltpu.make_async_copy(stages[a], outs[a].at[me], local_sems.at[a]) for a in range(3)]

        @pl.when(i == 0)
        def _():
            for a, (src, dst) in enumerate(zip((win_ref, wout_ref, cp_ref), stages)):
                rows = src.shape[0]
                rc = min(rows, 32)

                def cast(r, carry, src=src, dst=dst, rc=rc):
                    dst[pl.ds(r, rc), :] = src[pl.ds(r, rc), :].astype(dst.dtype)
                    return carry

                _chunks(rows, rc, cast, 0)
                gather.start_own(a)
                keep_own[a].start()

        @pl.when(i < nb)
        def _():
            xv = x_ref[...]
            r0 = lax.rsqrt(jnp.mean(xv * xv, axis=-1, keepdims=True) + RMS_EPS)
            xn = xv * r0 * lng_ref[...]
            xnb[pl.ds(pl.multiple_of(i * tb, tb), tb), :] = xn.astype(MM)
            xnt_ref[...] = xn.T.astype(MM)

        for k in range(N_DEV):
            @pl.when(i == nb + k)
            def _(k=k):
                if k == 1:
                    gather.wait_sibling(0)
                elif 2 <= k <= 4:
                    gather.wait_chip_and_pass_on(0, k - 2)
                elif k >= 5:
                    gather.wait_passed_on(0, k - 5)
                if k == 0:
                    w_blk = st_in
                else:
                    load = pltpu.make_async_copy(win_all.at[order_ref[k]], wbuf, local_sems.at[3])
                    load.start()
                    load.wait()
                    w_blk = wbuf

                def project(r, carry):
                    rows = pl.ds(r, mc)
                    p_ref[rows, :] = _dot(xnb[rows, :], w_blk[...]).astype(MM)
                    return carry

                _chunks(t_len, mc, project, 0)
                if k == N_DEV - 1:
                    gather.wait_sends(0)
                    gather.finish(1)
                    gather.finish(2)
                    for cp in keep_own:
                        cp.wait()

    vm = pl.BlockSpec(memory_space=pltpu.VMEM)
    hbm = pl.BlockSpec(memory_space=pl.ANY)
    grid_spec = pltpu.PrefetchScalarGridSpec(
        num_scalar_prefetch=1, grid=(nb + N_DEV,),
        in_specs=[pl.BlockSpec((tb, d), lambda i, o: (jnp.minimum(i, nb - 1), 0)), vm, vm, vm, vm],
        out_specs=(pl.BlockSpec((t_len, cols), lambda i, o: (0, o[jnp.maximum(i - nb, 0)])),
                   pl.BlockSpec((d, tb), lambda i, o: (0, jnp.minimum(i, nb - 1))), hbm, hbm, hbm),
        scratch_shapes=[pltpu.VMEM((t_len, d), MM)] + [pltpu.VMEM(s.shape, dt) for s, dt in zip(srcs, dts)]
                       + [pltpu.VMEM(w_in.shape, MM),
                          pltpu.SemaphoreType.DMA((21,)), pltpu.SemaphoreType.DMA((21,)), pltpu.SemaphoreType.DMA((4,))])
    return pl.pallas_call(
        body, name="gather_project", grid_spec=grid_spec,
        out_shape=(jax.ShapeDtypeStruct((t_len, N_DEV * cols), MM),
                   jax.ShapeDtypeStruct((d, t_len), MM))
                  + tuple(jax.ShapeDtypeStruct((N_DEV,) + s.shape, dt) for s, dt in zip(srcs, dts)),
        compiler_params=_params(dimension_semantics=("arbitrary",)),
    )(_block_order(), x, w_in, w_out, conv_pack, ln_g)


def _forward(x, tgt, p, wout, wa_t, wi_t, sp, e_c, et_c, e_l, et_l, tb):
    t_len, d = x.shape
    nb = t_len // tb
    n_tiles, tw = wa_t.shape[0], wa_t.shape[1]
    hd_c, hd_l = d // N_CONV_HEADS, d // N_LRU_HEADS

    def body(x_ref, tgt_ref, p_ref, wout_ref, wa_ref, wi_ref, sp_ref, ec_ref, etc_ref, el_ref, etl_ref,
             h_ref, dh_ref, dhb_ref, acc_ref,
             pf, yc, u, pa, pi, rcf, rlf, ybuf, tail_z, tail_xl, hcar):
        i = pl.program_id(0)
        row = _row_iota(d)

        @pl.when(i == 0)
        def _():
            tail_z[...] = jnp.zeros_like(tail_z)
            tail_xl[...] = jnp.zeros_like(tail_xl)
            hcar[...] = jnp.zeros_like(hcar)
            acc_ref[...] = jnp.zeros_like(acc_ref)

        def spr(r):
            return sp_ref[r:r + 1, :]

        def widen(r, carry):
            pf[pl.ds(r, 16), :] = p_ref[pl.ds(r, 16), :].astype(F32)
            return carry

        _chunks(tb, 16, widen, 0)

        w0, w1, w2 = spr(SP_CONV_W), spr(SP_CONV_W + 1), spr(SP_CONV_W + 2)
        l0, l1, l2, l3 = spr(SP_LRU_W), spr(SP_LRU_W + 1), spr(SP_LRU_W + 2), spr(SP_LRU_W + 3)
        lb = spr(SP_LRU_B)

        def convs(r, carry):
            zp, xp = carry
            rows = pl.ds(r, SUBLANES)
            z = pf[rows, d:2 * d] * pf[rows, 2 * d:3 * d]
            cz = w0 * _shift_down(z, zp, 2, row) + w1 * _shift_down(z, zp, 1, row) + w2 * z
            yc[rows, :] = pf[rows, 0:d] * cz
            xl = pf[rows, 4 * d:5 * d]
            u[rows, :] = (l0 * _shift_down(xl, xp, 3, row) + l1 * _shift_down(xl, xp, 2, row)
                          + l2 * _shift_down(xl, xp, 1, row) + l3 * xl + lb)
            return z, xl

        z_last, xl_last = _chunks(tb, SUBLANES, convs, (tail_z[...], tail_xl[...]))
        tail_z[...] = z_last
        tail_xl[...] = xl_last

        ub = u[...].astype(MM)
        for k in range(n_tiles):
            sl = slice(k * tw, (k + 1) * tw)
            pa[:, sl] = _dot(ub[:, sl], wa_ref[k])
            pi[:, sl] = _dot(ub[:, sl], wi_ref[k])
        rcf[...] = _head_rstd(yc[...], ec_ref[...], etc_ref[...], hd_c)

        c8 = RG_LRU_C * _log_sigmoid(spr(SP_LAM))
        b_a, b_i = spr(SP_B_A), spr(SP_B_I)

        def lru(r, hp):
            rows = pl.ds(r, SUBLANES)
            ra = jax.nn.sigmoid(pa[rows, :] + b_a)
            ii = jax.nn.sigmoid(pi[rows, :] + b_i)
            la = ra * c8
            a = jnp.exp(la)
            mult = jnp.sqrt(_neg_expm1(2.0 * la))
            h = _scan_fwd(a, mult * (ii * u[rows, :]), hp, row)
            h_ref[rows, :] = h
            return _bcast_row(h, SUBLANES - 1)

        hcar[...] = _chunks(tb, SUBLANES, lru, hcar[...])
        rlf[...] = _head_rstd(h_ref[...], el_ref[...], etl_ref[...], hd_l)

        g_c, g_l = spr(SP_CONV_G), spr(SP_LRU_G)

        def gate(r, carry):
            rows = pl.ds(r, SUBLANES)
            gc = pf[rows, 3 * d:4 * d]
            ybuf[rows, 0:d] = yc[rows, :] * rcf[rows, :] * g_c * (gc * jax.nn.sigmoid(gc))
            gl = pf[rows, 5 * d:6 * d]
            ybuf[rows, d:2 * d] = h_ref[rows, :] * rlf[rows, :] * g_l * (gl * jax.nn.sigmoid(gl))
            return carry

        _chunks(tb, SUBLANES, gate, 0)

        hres = x_ref[...] + _dot(ybuf[...].astype(MM), wout_ref[...])
        rf = lax.rsqrt(jnp.mean(hres * hres, axis=-1, keepdims=True) + RMS_EPS)
        hn = hres * rf
        fg = spr(SP_FINAL_G)
        err = hn * fg - tgt_ref[...]
        dout = err * (1.0 / d)
        acc_ref[0:SUBLANES, :] += (err * err).reshape(tb // SUBLANES, SUBLANES, d).sum(axis=0)
        acc_ref[SUBLANES:2 * SUBLANES, :] += (dout * hn).reshape(tb // SUBLANES, SUBLANES, d).sum(axis=0)
        gd = dout * fg
        dhres = rf * (gd - hn * jnp.mean(gd * hn, axis=-1, keepdims=True))
        dh_ref[...] = dhres
        dhb_ref[...] = dhres.astype(MM)

    vm = pl.BlockSpec(memory_space=pltpu.VMEM)
    blk = lambda w: pl.BlockSpec((tb, w), lambda i: (i, 0))
    buf = lambda w: pltpu.VMEM((tb, w), F32)
    return pl.pallas_call(
        body, name="forward", grid=(nb,),
        in_specs=[blk(d), blk(d), blk(6 * d), vm, vm, vm, vm, vm, vm, vm, vm],
        out_specs=(blk(d), blk(d), blk(d), pl.BlockSpec((2 * SUBLANES, d), lambda i: (0, 0))),
        out_shape=(jax.ShapeDtypeStruct((t_len, d), F32),
                   jax.ShapeDtypeStruct((t_len, d), F32),
                   jax.ShapeDtypeStruct((t_len, d), MM),
                   jax.ShapeDtypeStruct((2 * SUBLANES, d), F32)),
        scratch_shapes=[buf(6 * d), buf(d), buf(d), buf(d), buf(d), buf(d), buf(d), buf(2 * d),
                        pltpu.VMEM((SUBLANES, d), F32), pltpu.VMEM((SUBLANES, d), F32), pltpu.VMEM((SUBLANES, d), F32)],
        compiler_params=_params(dimension_semantics=("arbitrary",)),
    )(x, tgt, p, wout, wa_t, wi_t, sp, e_c, et_c, e_l, et_l)


def _backward(p, h, dh, wout, wa_t, wi_t, sp, e_c, et_c, e_l, et_l, tb):
    t_len, d = h.shape
    nb = t_len // tb
    n_tiles, tw = wa_t.shape[0], wa_t.shape[1]
    hd_c, hd_l = d // N_CONV_HEADS, d // N_LRU_HEADS
    ph = 16
    s8 = SUBLANES

    def body(p_ref, phalo_ref, h_ref, hhalo_ref, dh_ref, wout_ref, wa_ref, wi_ref, sp_ref,
             ec_ref, etc_ref, el_ref, etl_ref,
             dp_ref, yt_ref, gwa_ref, gwi_ref, acc_ref,
             pf, hh, dy, ybuf, yc, czs, u, pa, pi, rcf, rlf, qc, ql, dyc_hat, dyl_hat, dpa, dpi, du, dpf,
             car_dcz, car_a, car_g, car_du):
        i = pl.program_id(0)
        blk_idx = nb - 1 - i
        row = _row_iota(d)

        @pl.when(i == 0)
        def _():
            for ref in (car_dcz, car_a, car_g, car_du, gwa_ref, gwi_ref, acc_ref):
                ref[...] = jnp.zeros_like(ref)

        def spr(r):
            return sp_ref[r:r + 1, :]

        def acc_add(group, val):
            acc_ref[group * s8:(group + 1) * s8, :] += val

        live = jnp.where(blk_idx > 0, 1.0, 0.0).astype(F32)
        pf[0:ph, :] = phalo_ref[...].astype(F32) * live
        hh[0:s8, :] = hhalo_ref[...] * live
        hh[s8:, :] = h_ref[...]

        def widen(r, carry):
            pf[pl.ds(ph + r, 16), :] = p_ref[pl.ds(r, 16), :].astype(F32)
            return carry

        _chunks(tb, 16, widen, 0)

        dy[...] = _dot_nt(dh_ref[...].astype(MM), wout_ref[...])

        w0, w1, w2 = spr(SP_CONV_W), spr(SP_CONV_W + 1), spr(SP_CONV_W + 2)
        l0, l1, l2, l3 = spr(SP_LRU_W), spr(SP_LRU_W + 1), spr(SP_LRU_W + 2), spr(SP_LRU_W + 3)
        lb = spr(SP_LRU_B)

        def zx(r):
            rows = pl.ds(ph + r, s8)
            return pf[rows, d:2 * d] * pf[rows, 2 * d:3 * d], pf[rows, 4 * d:5 * d]

        def recompute(r, carry):
            rows = pl.ds(r, s8)
            z, xl = zx(r)
            zp, xp = zx(r - s8)
            cz = w0 * _shift_down(z, zp, 2, row) + w1 * _shift_down(z, zp, 1, row) + w2 * z
            czs[rows, :] = cz
            yc[rows, :] = pf[pl.ds(ph + r, s8), 0:d] * cz
            u[rows, :] = (l0 * _shift_down(xl, xp, 3, row) + l1 * _shift_down(xl, xp, 2, row)
                          + l2 * _shift_down(xl, xp, 1, row) + l3 * xl + lb)
            return carry

        _chunks(tb, s8, recompute, 0)

        ub = u[...].astype(MM)
        for k in range(n_tiles):
            sl = slice(k * tw, (k + 1) * tw)
            pa[:, sl] = _dot(ub[:, sl], wa_ref[k])
            pi[:, sl] = _dot(ub[:, sl], wi_ref[k])
        rcf[...] = _head_rstd(yc[...], ec_ref[...], etc_ref[...], hd_c)
        rlf[...] = _head_rstd(h_ref[...], el_ref[...], etl_ref[...], hd_l)

        g_c, g_l = spr(SP_CONV_G), spr(SP_LRU_G)

        def gates(r, carry):
            rows = pl.ds(r, s8)
            prow = pl.ds(ph + r, s8)
            for (off_g, off_y, src, rstd, gain, q, dhat, grp) in (
                    (3 * d, 0, yc, rcf, g_c, qc, dyc_hat, A_CONV_G),
                    (5 * d, d, h_ref, rlf, g_l, ql, dyl_hat, A_LRU_G)):
                gt = pf[prow, off_g:off_g + d]
                sg = jax.nn.sigmoid(gt)
                silu = gt * sg
                yhat = src[rows, :] * rstd[rows, :]
                nrm = yhat * gain
                ybuf[rows, off_y:off_y + d] = nrm * silu
                dout = dy[rows, off_y:off_y + d]
                dnrm = dout * silu
                dpf[rows, off_g:off_g + d] = dout * nrm * (sg * (1.0 + gt * (1.0 - sg)))
                acc_add(grp, dnrm * yhat)
                dh_ = dnrm * gain
                dhat[rows, :] = dh_
                q[rows, :] = dh_ * yhat
            return carry

        _chunks(tb, s8, gates, 0)

        qc[...] = _head_mean(qc[...], ec_ref[...], etc_ref[...], hd_c)
        ql[...] = _head_mean(ql[...], el_ref[...], etl_ref[...], hd_l)
        yt_ref[...] = ybuf[...].T.astype(MM)

        c8 = RG_LRU_C * _log_sigmoid(spr(SP_LAM))
        b_a, b_i = spr(SP_B_A), spr(SP_B_I)

        def mixers(r, carry):
            dcz_n, a_n, g_n = carry
            rows = pl.ds(r, s8)
            prow = pl.ds(ph + r, s8)
            rstd = rcf[rows, :]
            yhat = yc[rows, :] * rstd
            dyc = rstd * (dyc_hat[rows, :] - yhat * qc[rows, :])
            dpf[rows, 0:d] = dyc * czs[rows, :]
            dcz = dyc * pf[prow, 0:d]
            dz = w2 * dcz + w1 * _shift_up(dcz, dcz_n, 1, row) + w0 * _shift_up(dcz, dcz_n, 2, row)
            z, _ = zx(r)
            zp, _ = zx(r - s8)
            dpf[rows, d:2 * d] = dz * pf[prow, 2 * d:3 * d]
            dpf[rows, 2 * d:3 * d] = dz * pf[prow, d:2 * d]
            acc_add(A_CONV_W, dcz * _shift_down(z, zp, 2, row))
            acc_add(A_CONV_W + 1, dcz * _shift_down(z, zp, 1, row))
            acc_add(A_CONV_W + 2, dcz * z)
            rstd = rlf[rows, :]
            hcur = hh[pl.ds(s8 + r, s8), :]
            hhat = hcur * rstd
            dh_out = rstd * (dyl_hat[rows, :] - hhat * ql[rows, :])
            ra = jax.nn.sigmoid(pa[rows, :] + b_a)
            la = ra * c8
            a = jnp.exp(la)
            g = _scan_bwd(_shift_up(a, a_n, 1, row), dh_out, g_n, row)
            da = g * _shift_down(hcur, hh[pl.ds(r, s8), :], 1, row)
            ii = jax.nn.sigmoid(pi[rows, :] + b_i)
            uu = u[rows, :]
            mult = jnp.sqrt(_neg_expm1(2.0 * la))
            dmult = g * (ii * uu)
            ds = g * mult
            dla = a * (da - dmult * a / mult)
            acc_add(A_LAM, dla * ra)
            dpa_ = dla * c8 * ra * (1.0 - ra)
            dpi_ = ds * uu * ii * (1.0 - ii)
            acc_add(A_B_A, dpa_)
            acc_add(A_B_I, dpi_)
            dpa[rows, :] = dpa_
            dpi[rows, :] = dpi_
            du[rows, :] = ds * ii
            return dcz, a, _bcast_row(g, 0)

        dcz_f, a_f, g_f = _chunks(tb, s8, mixers, (car_dcz[...], car_a[...], car_g[...]), reverse=True)
        car_dcz[...] = dcz_f
        car_a[...] = a_f
        car_g[...] = g_f

        dpab = dpa[...].astype(MM)
        dpib = dpi[...].astype(MM)
        for k in range(n_tiles):
            sl = slice(k * tw, (k + 1) * tw)
            du[:, sl] += _dot_nt(dpab[:, sl], wa_ref[k]) + _dot_nt(dpib[:, sl], wi_ref[k])
            ut = u[:, sl].T.astype(MM)
            gwa_ref[k] += _dot(ut, dpab[:, sl])
            gwi_ref[k] += _dot(ut, dpib[:, sl])

        def lru_conv(r, du_n):
            rows = pl.ds(r, s8)
            dut = du[rows, :]
            dpf[rows, 4 * d:5 * d] = (l3 * dut + l2 * _shift_up(dut, du_n, 1, row)
                                      + l1 * _shift_up(dut, du_n, 2, row) + l0 * _shift_up(dut, du_n, 3, row))
            _, xl = zx(r)
            _, xp = zx(r - s8)
            acc_add(A_LRU_W, dut * _shift_down(xl, xp, 3, row))
            acc_add(A_LRU_W + 1, dut * _shift_down(xl, xp, 2, row))
            acc_add(A_LRU_W + 2, dut * _shift_down(xl, xp, 1, row))
            acc_add(A_LRU_W + 3, dut * xl)
            acc_add(A_LRU_B, dut)
            return dut

        car_du[...] = _chunks(tb, s8, lru_conv, car_du[...], reverse=True)

        def narrow(r, carry):
            dp_ref[pl.ds(r, 16), :] = dpf[pl.ds(r, 16), :].astype(MM)
            return carry

        _chunks(tb, 16, narrow, 0)

    vm = pl.BlockSpec(memory_space=pltpu.VMEM)
    rev = lambda w: pl.BlockSpec((tb, w), lambda i: (nb - 1 - i, 0))
    halo = lambda rows, w: pl.BlockSpec((rows, w), lambda i: (jnp.maximum((nb - 1 - i) * (tb // rows) - 1, 0), 0))
    const = lambda shape: pl.BlockSpec(shape, lambda i: (0,) * len(shape))
    buf = lambda w: pltpu.VMEM((tb, w), F32)
    car = pltpu.VMEM((SUBLANES, d), F32)
    return pl.pallas_call(
        body, name="backward", grid=(nb,),
        in_specs=[rev(6 * d), halo(ph, 6 * d), rev(d), halo(SUBLANES, d), rev(d), vm, vm, vm, vm, vm, vm, vm, vm],
        out_specs=(rev(6 * d), pl.BlockSpec((2 * d, tb), lambda i: (0, nb - 1 - i)),
                   const((n_tiles, tw, tw)), const((n_tiles, tw, tw)), const((A_GROUPS * SUBLANES, d))),
        out_shape=(jax.ShapeDtypeStruct((t_len, 6 * d), MM),
                   jax.ShapeDtypeStruct((2 * d, t_len), MM),
                   jax.ShapeDtypeStruct((n_tiles, tw, tw), F32),
                   jax.ShapeDtypeStruct((n_tiles, tw, tw), F32),
                   jax.ShapeDtypeStruct((A_GROUPS * SUBLANES, d), F32)),
        scratch_shapes=[pltpu.VMEM((ph + tb, 6 * d), F32), pltpu.VMEM((SUBLANES + tb, d), F32), buf(2 * d), buf(2 * d)]
                       + [buf(d)] * 14 + [buf(6 * d), car, car, car, car],
        compiler_params=_params(dimension_semantics=("arbitrary",)),
    )(p, p, h, h, dh, wout, wa_t, wi_t, sp, e_c, et_c, e_l, et_l)


def _input_grad(dp, win_all, x, dh, sp, parts, tb):
    t_len, d = x.shape
    nb = t_len // tb
    cols = win_all.shape[2]
    n_parts = len(parts)

    def body(dp_ref, win_ref, x_ref, dh_ref, sp_ref, *refs):
        part_refs = refs[:n_parts]
        gx_ref, acc_ref = refs[n_parts:n_parts + 2]
        land_refs = refs[n_parts + 2:2 * n_parts + 2]
        send_sems, recv_sems = refs[2 * n_parts + 2:]
        i = pl.program_id(0)
        x_, y_, c_ = _mesh_pos()

        def to_chip(a, k):
            fx, fy = _CHIP_RELATIONS[k]
            return pltpu.make_async_remote_copy(
                src_ref=part_refs[a].at[k - 1], dst_ref=land_refs[a].at[k - 1],
                send_sem=send_sems.at[3 * a + k - 1], recv_sem=recv_sems.at[3 * a + k - 1],
                device_id=(x_ ^ fx, y_ ^ fy, c_), device_id_type=MESH)

        @pl.when(i == 0)
        def _():
            acc_ref[...] = jnp.zeros_like(acc_ref)
            for a in range(n_parts):
                for k in (1, 2, 3):
                    to_chip(a, k).start()

        dxn = _dot_nt(dp_ref[:, 0:cols], win_ref[0])
        for j in range(1, N_DEV):
            dxn += _dot_nt(dp_ref[:, j * cols:(j + 1) * cols], win_ref[j])
        xv = x_ref[...]
        r0 = lax.rsqrt(jnp.mean(xv * xv, axis=-1, keepdims=True) + RMS_EPS)
        xhat = xv * r0
        acc_ref[...] += (dxn * xhat).reshape(tb // SUBLANES, SUBLANES, d).sum(axis=0)
        dxh = dxn * sp_ref[SP_LN_G:SP_LN_G + 1, :]
        gx_ref[...] = dh_ref[...] + r0 * (dxh - xhat * jnp.mean(dxh * xhat, axis=-1, keepdims=True))

        @pl.when(i == nb - 1)
        def _():
            for a in range(n_parts):
                for k in (1, 2, 3):
                    to_chip(a, k).wait_recv()
            for a in range(n_parts):
                for k in (1, 2, 3):
                    to_chip(a, k).wait_send()

    vm = pl.BlockSpec(memory_space=pltpu.VMEM)
    hbm = pl.BlockSpec(memory_space=pl.ANY)
    blk = lambda w: pl.BlockSpec((tb, w), lambda i: (i, 0))
    outs = pl.pallas_call(
        body, name="input_grad", grid=(nb,),
        in_specs=[blk(6 * d), vm, blk(d), blk(d), vm] + [hbm] * n_parts,
        out_specs=(blk(d), pl.BlockSpec((SUBLANES, d), lambda i: (0, 0))) + (hbm,) * n_parts,
        out_shape=(jax.ShapeDtypeStruct((t_len, d), F32), jax.ShapeDtypeStruct((SUBLANES, d), F32))
                  + tuple(jax.ShapeDtypeStruct(p.shape, p.dtype) for p in parts),
        scratch_shapes=[pltpu.SemaphoreType.DMA((3 * n_parts,)), pltpu.SemaphoreType.DMA((3 * n_parts,))],
        compiler_params=_params(dimension_semantics=("arbitrary",)),
    )(dp, win_all, x, dh, sp, *parts)
    return outs[0], outs[1], outs[2:]


_CHIP_RELATIONS = [(0, 0), (1, 0), (0, 1), (1, 1)]


def _related_block(k, core):
    x, y, _ = _mesh_pos()
    fx, fy = _CHIP_RELATIONS[k]
    return 4 * (x ^ fx) + 2 * (y ^ fy) + core


def _weight_grad_stage1(name, blk_shape, n_split, operands, in_specs, product):
    n_rows, n_cols = blk_shape
    rs = n_rows // n_split
    _, _, c = _mesh_pos()
    order = jnp.stack([_related_block(k, 1 - c) for k in range(4)]
                      + [_related_block(k, c) for k in (1, 2, 3, 0)]).astype(jnp.int32)

    def body(order_ref, *refs):
        n_in = len(operands)
        ins = refs[:n_in]
        part_ref, own_ref, from_sib, sendbuf, tmp, send_sems, recv_sems, local_sems = refs[n_in:]
        s = pl.program_id(0)
        x, y, c = _mesh_pos()

        def to_sibling(k):
            return pltpu.make_async_remote_copy(
                src_ref=sendbuf.at[k], dst_ref=from_sib.at[k], send_sem=send_sems.at[k], recv_sem=recv_sems.at[k],
                device_id=(x, y, 1 - c), device_id_type=MESH)

        for k in range(4):
            @pl.when(s == k)
            def _(k=k):
                for h in range(n_split):
                    sendbuf[k, h * rs:(h + 1) * rs, :] = product(ins, h).astype(MM)
                to_sibling(k).start()

        for step, k in ((4, 1), (5, 2), (6, 3), (7, 0)):
            @pl.when(s == step)
            def _(k=k):
                to_sibling(k).wait_recv()
                load = pltpu.make_async_copy(from_sib.at[k], tmp, local_sems.at[0])
                load.start()
                load.wait()
                for h in range(n_split):
                    total = product(ins, h) + tmp[h * rs:(h + 1) * rs, :].astype(F32)
                    if k:
                        part_ref[0, h * rs:(h + 1) * rs, :] = total.astype(MM)
                    else:
                        own_ref[h * rs:(h + 1) * rs, :] = total
                if k == 0:
                    for kk in range(4):
                        to_sibling(kk).wait_send()

    grid_spec = pltpu.PrefetchScalarGridSpec(
        num_scalar_prefetch=1, grid=(N_DEV,), in_specs=in_specs,
        out_specs=(pl.BlockSpec((1, n_rows, n_cols), lambda s, o: (jnp.clip(s - 4, 0, 2), 0, 0)),
                   pl.BlockSpec((n_rows, n_cols), lambda s, o: (0, 0)),
                   pl.BlockSpec(memory_space=pl.ANY)),
        scratch_shapes=[pltpu.VMEM((4, n_rows, n_cols), MM), pltpu.VMEM((n_rows, n_cols), MM),
                        pltpu.SemaphoreType.DMA((4,)), pltpu.SemaphoreType.DMA((4,)), pltpu.SemaphoreType.DMA((1,))])
    part, own, _ = pl.pallas_call(
        body, name=name, grid_spec=grid_spec,
        out_shape=(jax.ShapeDtypeStruct((3, n_rows, n_cols), MM), jax.ShapeDtypeStruct((n_rows, n_cols), F32),
                   jax.ShapeDtypeStruct((4, n_rows, n_cols), MM)),
        compiler_params=_params(dimension_semantics=("arbitrary",)),
    )(order, *operands)
    return part, own


def _weight_grad_in(xnt, dp):
    d, t_len = xnt.shape
    cols = dp.shape[1] // N_DEV
    half = d // 2
    return _weight_grad_stage1(
        "weight_grad_in", (d, cols), 2, (xnt, dp),
        [pl.BlockSpec(memory_space=pltpu.VMEM), pl.BlockSpec((t_len, cols), lambda s, o: (0, o[s]))],
        lambda refs, h: _dot(refs[0][h * half:(h + 1) * half, :], refs[1][...]))


def _weight_grad_out(yt, dhb):
    d2, t_len = yt.shape
    d = dhb.shape[1]
    rows = d2 // N_DEV
    return _weight_grad_stage1(
        "weight_grad_out", (rows, d), 1, (yt, dhb),
        [pl.BlockSpec((rows, t_len), lambda s, o: (o[s], 0)), pl.BlockSpec(memory_space=pltpu.VMEM)],
        lambda refs, h: _dot(refs[0][...], refs[1][...]))


def _update_shard(own, from_chips, w, m, v, name):
    n_rows, n_cols = w.shape
    rb = min(256, n_rows)

    def body(own_ref, fc_ref, w_ref, m_ref, v_ref, grad_ref, delta_ref, mo_ref, vo_ref):
        g = own_ref[...]
        for k in range(3):
            g = g + fc_ref[k].astype(F32)
        delta, m_new, v_new = _adamw(w_ref[...], g, m_ref[...], v_ref[...])
        grad_ref[...] = g
        delta_ref[...] = delta
        mo_ref[...] = m_new
        vo_ref[...] = v_new

    blk = pl.BlockSpec((rb, n_cols), lambda i: (i, 0))
    out = jax.ShapeDtypeStruct((n_rows, n_cols), F32)
    return pl.pallas_call(
        body, name=name, grid=(n_rows // rb,),
        in_specs=[blk, pl.BlockSpec((3, rb, n_cols), lambda i: (0, i, 0)), blk, blk, blk],
        out_specs=(blk, blk, blk, blk), out_shape=(out, out, out, out),
        compiler_params=_params(dimension_semantics=("arbitrary",)),
    )(own, from_chips, w, m, v)


def _small_update(facc, xacc, bacc, gwa, gwi, lam, wsl, msl, vsl, cw, cm, cv):
    n_rows, d = wsl.shape
    s8 = SUBLANES
    cc = cw.shape[1]

    def body(facc_ref, xacc_ref, bacc_ref, gwa_ref, gwi_ref, lam_ref, w_ref, m_ref, v_ref, cw_ref, cm_ref, cv_ref,
             g_o, d_o, m_o, v_o, cg_o, cd_o, cm_o, cv_o, gat, send_sems, recv_sems):
        x, y, c = _mesh_pos()
        me = 4 * x + 2 * y + c

        def rowsum(ref, group):
            return jnp.sum(ref[group * s8:(group + 1) * s8, :], axis=0, keepdims=True)

        mine = gat.at[me]
        loss = jnp.sum(rowsum(facc_ref, 0), axis=1, keepdims=True) * (0.5 / d)
        mine[SL_LOSS:SL_LOSS + 1, :] = jnp.broadcast_to(loss, (1, d))
        mine[SL_LN_G:SL_LN_G + 1, :] = rowsum(xacc_ref, 0)
        mine[SL_LRU_B:SL_LRU_B + 1, :] = rowsum(bacc_ref, A_LRU_B)
        mine[SL_B_A:SL_B_A + 1, :] = rowsum(bacc_ref, A_B_A)
        mine[SL_B_I:SL_B_I + 1, :] = rowsum(bacc_ref, A_B_I)
        mine[SL_LAM:SL_LAM + 1, :] = rowsum(bacc_ref, A_LAM) * (RG_LRU_C * jax.nn.sigmoid(-lam_ref[...]))
        mine[SL_CONV_G:SL_CONV_G + 1, :] = rowsum(bacc_ref, A_CONV_G)
        mine[SL_LRU_G:SL_LRU_G + 1, :] = rowsum(bacc_ref, A_LRU_G)
        mine[SL_FINAL_G:SL_FINAL_G + 1, :] = rowsum(facc_ref, 1)
        for k in range(3):
            mine[SL_CONV_W + k:SL_CONV_W + k + 1, :] = rowsum(bacc_ref, A_CONV_W + k)
        for k in range(4):
            mine[SL_LRU_W + k:SL_LRU_W + k + 1, :] = rowsum(bacc_ref, A_LRU_W + k)
        na = gwa_ref.shape[0]
        mine[SL_W_A:SL_W_A + na, :] = gwa_ref[...]
        mine[SL_W_A + na:SL_W_A + 2 * na, :] = gwi_ref[...]

        gather = _Gather(lambda a, px, py, pc: gat.at[4 * px + 2 * py + pc], send_sems, recv_sems)
        gather.start_own(0)
        gather.finish(0)

        def update(r, carry):
            rows = pl.ds(r, s8)
            g = gat[0, rows, :]
            for b in range(1, N_DEV):
                g = g + gat[b, rows, :]
            delta, m_new, v_new = _adamw(w_ref[rows, :], g, m_ref[rows, :], v_ref[rows, :])
            g_o[rows, :] = g
            d_o[rows, :] = delta
            m_o[rows, :] = m_new
            v_o[rows, :] = v_new
            return carry

        _chunks(n_rows, s8, update, 0)
        gc = g_o[s8:2 * s8, pl.ds(pl.multiple_of(me * cc, cc), cc)]
        delta, m_new, v_new = _adamw(cw_ref[...], gc, cm_ref[...], cv_ref[...])
        cg_o[...] = gc
        cd_o[...] = delta
        cm_o[...] = m_new
        cv_o[...] = v_new

    vm = pl.BlockSpec(memory_space=pltpu.VMEM)
    big = jax.ShapeDtypeStruct((n_rows, d), F32)
    small = jax.ShapeDtypeStruct(cw.shape, F32)
    return pl.pallas_call(
        body, name="small_update",
        in_specs=[vm] * 12, out_specs=(vm,) * 8, out_shape=(big, big, big, big, small, small, small, small),
        scratch_shapes=[pltpu.VMEM((N_DEV, n_rows, d), F32), pltpu.SemaphoreType.DMA((7,)), pltpu.SemaphoreType.DMA((7,))],
        compiler_params=_params(),
    )(facc, xacc, bacc, gwa, gwi, lam, wsl, msl, vsl, cw, cm, cv)


def _head_selectors(d, n_heads):
    lane = jnp.arange(d)[:, None] // (d // n_heads)
    e = (lane == jnp.arange(LANES)[None, :]).astype(MM)
    return e, e.T


def _gate_tiles(w, tw):
    n_heads, hd, _ = w.shape
    per = tw // hd
    w4 = w.reshape(n_heads // per, per, hd, hd)
    eye = jnp.eye(per, dtype=w.dtype)
    return (w4[:, :, :, None, :] * eye[None, :, None, :, None]).reshape(n_heads // per, tw, tw)


def _gate_blocks(tiles, n_heads, hd):
    n_tiles, tw, _ = tiles.shape
    per = tw // hd
    t5 = tiles.reshape(n_tiles, per, hd, per, hd)
    diag = jnp.stack([t5[:, a, :, a, :] for a in range(per)], axis=1)
    return diag.reshape(hd, n_heads * hd)


def kernel(x, ln_g, w_in, conv_w, lru_conv_w, lru_conv_b, w_a, b_a, w_i, b_i, lam, conv_out_g, lru_out_g, w_out, final_g, loss_target, m_ln_g, m_w_in, m_conv_w, m_lru_conv_w, m_lru_conv_b, m_w_a, m_b_a, m_w_i, m_b_i, m_lam, m_conv_out_g, m_lru_out_g, m_w_out, m_final_g, v_ln_g, v_w_in, v_conv_w, v_lru_conv_w, v_lru_conv_b, v_w_a, v_b_a, v_w_i, v_b_i, v_lam, v_conv_out_g, v_lru_out_g, v_w_out, v_final_g):
    _, t_len, d = x.shape
    hd_l = d // N_LRU_HEADS
    tw = min(MXU_TILE, d)
    cc = conv_w.shape[1]
    tb = 128
    x2, tgt2 = x[0], loss_target[0]

    def conv_rows(cw3, lw4):
        return jnp.concatenate([jnp.zeros((1, cc), F32), cw3, lw4], axis=0)

    p, xnt, win_all, wout_all, conv_all = _gather_project(
        x2, w_in, w_out, conv_rows(conv_w, lru_conv_w), ln_g.reshape(1, d), min(256, t_len))
    wout_full = wout_all.reshape(N_DEV * w_out.shape[0], d)
    conv_full = conv_all.transpose(1, 0, 2).reshape(SUBLANES, d)
    small = [ln_g, lru_conv_b, b_a, b_i, lam, conv_out_g, lru_out_g, final_g]
    sp = jnp.concatenate([jnp.stack(small), conv_full[1:], jnp.zeros((1, d), F32)], axis=0)
    wa_t, wi_t = _gate_tiles(w_a, tw).astype(MM), _gate_tiles(w_i, tw).astype(MM)
    e_c, et_c = _head_selectors(d, N_CONV_HEADS)
    e_l, et_l = _head_selectors(d, N_LRU_HEADS)

    h, dh, dhb, facc = _forward(x2, tgt2, p, wout_full, wa_t, wi_t, sp, e_c, et_c, e_l, et_l, min(256, t_len))
    dp, yt, gwa_t, gwi_t, bacc = _backward(p, h, dh, wout_full, wa_t, wi_t, sp, e_c, et_c, e_l, et_l, tb)
    part_out, own_out = _weight_grad_out(yt, dhb)
    part_in, own_in = _weight_grad_in(xnt, dp)
    grad_x, xacc, (chips_in, chips_out) = _input_grad(dp, win_all, x2, dh, sp, (part_in, part_out), min(512, t_len))
    gw_in, dw_in, mw_in, vw_in = _update_shard(own_in, chips_in, w_in, m_w_in, v_w_in, "update_w_in")
    gw_out, dw_out, mw_out, vw_out = _update_shard(own_out, chips_out, w_out, m_w_out, v_w_out, "update_w_out")

    def slab(parts, wa_, wi_):
        return jnp.concatenate([jnp.zeros((1, d), F32), jnp.stack(parts), jnp.zeros((SL_W_A - SL_CONV_W, d), F32),
                                wa_.reshape(-1, d), wi_.reshape(-1, d)], axis=0)

    wsl = slab(small, w_a, w_i)
    msl = slab([m_ln_g, m_lru_conv_b, m_b_a, m_b_i, m_lam, m_conv_out_g, m_lru_out_g, m_final_g], m_w_a, m_w_i)
    vsl = slab([v_ln_g, v_lru_conv_b, v_b_a, v_b_i, v_lam, v_conv_out_g, v_lru_out_g, v_final_g], v_w_a, v_w_i)
    outs = _small_update(
        facc, xacc, bacc, _gate_blocks(gwa_t, N_LRU_HEADS, hd_l), _gate_blocks(gwi_t, N_LRU_HEADS, hd_l),
        lam.reshape(1, d), wsl, msl, vsl,
        conv_rows(conv_w, lru_conv_w), conv_rows(m_conv_w, m_lru_conv_w), conv_rows(v_conv_w, v_lru_conv_w))
    sl_g, sl_d, sl_m, sl_v, c_g, c_d, c_m, c_v = outs
    loss = sl_g[SL_LOSS, 0]
    na = w_a.size // d

    def unpack(sl, cv, big_in, big_out):
        one = lambda r: sl[r]
        return [one(SL_LN_G), big_in, cv[1:4], cv[4:8], one(SL_LRU_B),
                sl[SL_W_A:SL_W_A + na].reshape(w_a.shape), one(SL_B_A),
                sl[SL_W_A + na:SL_W_A + 2 * na].reshape(w_i.shape), one(SL_B_I), one(SL_LAM),
                one(SL_CONV_G), one(SL_LRU_G), big_out, one(SL_FINAL_G)]

    return (loss, grad_x[None], *unpack(sl_g, c_g, gw_in, gw_out), *unpack(sl_d, c_d, dw_in, dw_out),
            *unpack(sl_m, c_m, mw_in, mw_out), *unpack(sl_v, c_v, vw_in, vw_out))
```

```python
import functools

import jax
import jax.numpy as jnp
from jax import lax
from jax.experimental import pallas as pl
from jax.experimental.pallas import tpu as pltpu

F32 = jnp.float32
MM = jnp.bfloat16
MESH = pl.DeviceIdType.MESH

N_DEV = 8
N_CONV_HEADS = 8
N_LRU_HEADS = 16
RG_LRU_C = 8.0
RMS_EPS = 1e-6
ADAM_LR, ADAM_B1, ADAM_B2, ADAM_EPS, ADAM_WD, ADAM_STEP = 0.001, 0.9, 0.999, 1e-08, 0.01, 10
ADAM_BC1 = 1.0 - ADAM_B1 ** ADAM_STEP
ADAM_BC2 = 1.0 - ADAM_B2 ** ADAM_STEP

SUBLANES = 8
LANES = 128
MXU_TILE = 256
VMEM_LIMIT = 56 * 1024 * 1024

SP_LN_G, SP_LRU_B, SP_B_A, SP_B_I, SP_LAM, SP_CONV_G, SP_LRU_G, SP_FINAL_G, SP_CONV_W, SP_LRU_W = 0, 1, 2, 3, 4, 5, 6, 7, 8, 11
SP_ROWS = 16
A_CONV_G, A_LRU_G, A_LAM, A_B_A, A_B_I, A_CONV_W, A_LRU_W, A_LRU_B = 0, 1, 2, 3, 4, 5, 8, 12
A_GROUPS = 13
SL_LOSS, SL_LN_G, SL_LRU_B, SL_B_A, SL_B_I, SL_LAM, SL_CONV_G, SL_LRU_G, SL_FINAL_G, SL_CONV_W, SL_LRU_W, SL_W_A = 0, 1, 2, 3, 4, 5, 6, 7, 8, 9, 12, 16


def _params(vmem=True, **kw):
    if vmem:
        kw["vmem_limit_bytes"] = VMEM_LIMIT
    return pltpu.CompilerParams(**kw)


def _dot(a, b):
    return jnp.dot(a, b, preferred_element_type=F32)


def _dot_nt(a, b):
    return lax.dot_general(a, b, (((1,), (1,)), ((), ())), preferred_element_type=F32)


def _split3(v):
    hi = v.astype(MM)
    r1 = v - hi.astype(F32)
    mid = r1.astype(MM)
    lo = (r1 - mid.astype(F32)).astype(MM)
    return hi, mid, lo


def _dot3(v, e):
    hi, mid, lo = _split3(v)
    return _dot(hi, e) + _dot(mid, e) + _dot(lo, e)


def _head_rstd(v, e, et, head_dim):
    ms = _dot3(v * v, e) * (1.0 / head_dim)
    return _dot3(lax.rsqrt(ms + RMS_EPS), et)


def _head_mean(v, e, et, head_dim):
    return _dot3(_dot3(v, e) * (1.0 / head_dim), et)


def _neg_expm1(x):
    series = -x * (1.0 + x * (0.5 + x * (1.0 / 6.0 + x * (1.0 / 24.0 + x * (1.0 / 120.0)))))
    return jnp.where(x > -0.03, series, 1.0 - jnp.exp(x))


def _log_sigmoid(x):
    z = jnp.exp(-jnp.abs(x))
    u = 1.0 + z
    log1p_z = jnp.where(u == 1.0, z, jnp.log(u) * (z / (u - 1.0)))
    return jnp.minimum(x, 0.0) - log1p_z


def _row_iota(d):
    return lax.broadcasted_iota(jnp.int32, (SUBLANES, d), 0)


def _shift_down(cur, prev, s, row):
    return jnp.where(row >= s, pltpu.roll(cur, s, axis=0), pltpu.roll(prev, s, axis=0))


def _shift_up(cur, nxt, s, row):
    k = SUBLANES - s
    return jnp.where(row < k, pltpu.roll(cur, k, axis=0), pltpu.roll(nxt, k, axis=0))


def _scan_fwd(a, b, h_prev, row):
    for s in (1, 2, 4):
        a_s = jnp.where(row >= s, pltpu.roll(a, s, axis=0), 1.0)
        b_s = jnp.where(row >= s, pltpu.roll(b, s, axis=0), 0.0)
        b = a * b_s + b
        a = a * a_s
    return a * h_prev + b


def _scan_bwd(a_next, b, g_next, row):
    a = a_next
    for s in (1, 2, 4):
        k = SUBLANES - s
        a_s = jnp.where(row < k, pltpu.roll(a, k, axis=0), 1.0)
        b_s = jnp.where(row < k, pltpu.roll(b, k, axis=0), 0.0)
        b = a * b_s + b
        a = a * a_s
    return a * g_next + b


def _bcast_row(v, r):
    return jnp.broadcast_to(v[r:r + 1, :], v.shape)


def _chunks(n_rows, rc, body, init, reverse=False):
    n = n_rows // rc

    def step(i, carry):
        j = (n - 1 - i) if reverse else i
        return body(pl.multiple_of(j * rc, rc), carry)

    return lax.fori_loop(0, n, step, init)


def _adamw(w, g, m, v):
    m = ADAM_B1 * m + (1.0 - ADAM_B1) * g
    v = ADAM_B2 * v + (1.0 - ADAM_B2) * (g * g)
    m_hat = m / ADAM_BC1
    v_hat = v / ADAM_BC2
    delta = -ADAM_LR * (m_hat / (jnp.sqrt(v_hat) + ADAM_EPS) + ADAM_WD * w)
    return delta, m, v


def _mesh_pos():
    return lax.axis_index("x"), lax.axis_index("y"), lax.axis_index("c")


class _Gather:
    def __init__(self, blocks_of, send_sems, recv_sems, own_src=None):
        x, y, c = _mesh_pos()
        self.c = c
        self.me, self.sibling = (x, y, c), (x, y, 1 - c)
        self.chips = [(1 - x, y), (x, 1 - y), (1 - x, 1 - y)]
        self.blocks_of, self.send_sems, self.recv_sems = blocks_of, send_sems, recv_sems
        self.own_src = own_src

    def copy(self, a, k, block, to):
        src = self.blocks_of(a, *block)
        if block is self.me and self.own_src is not None:
            src = self.own_src[a]
        return pltpu.make_async_remote_copy(
            src_ref=src, dst_ref=self.blocks_of(a, *block),
            send_sem=self.send_sems.at[a * 7 + k], recv_sem=self.recv_sems.at[a * 7 + k],
            device_id=to, device_id_type=MESH)

    def start_own(self, a):
        self.copy(a, 0, self.me, self.sibling).start()
        for j, chip in enumerate(self.chips):
            self.copy(a, 1 + j, self.me, (*chip, self.c)).start()

    def wait_sibling(self, a):
        self.copy(a, 0, self.sibling, self.me).wait_recv()

    def wait_chip_and_pass_on(self, a, j):
        block = (*self.chips[j], self.c)
        self.copy(a, 1 + j, block, self.me).wait_recv()
        self.copy(a, 4 + j, block, self.sibling).start()

    def wait_passed_on(self, a, j):
        self.copy(a, 4 + j, (*self.chips[j], 1 - self.c), self.me).wait_recv()

    def wait_sends(self, a):
        self.copy(a, 0, self.me, self.sibling).wait_send()
        for j, chip in enumerate(self.chips):
            self.copy(a, 1 + j, self.me, (*chip, self.c)).wait_send()
            self.copy(a, 4 + j, (*chip, self.c), self.sibling).wait_send()

    def finish(self, a):
        for j in range(3):
            self.wait_chip_and_pass_on(a, j)
        self.wait_sibling(a)
        for j in range(3):
            self.wait_passed_on(a, j)
        self.wait_sends(a)


def _block_order():
    x, y, c = _mesh_pos()
    chips = [(1 - x, y), (x, 1 - y), (1 - x, 1 - y)]
    idx = lambda px, py, pc: 4 * px + 2 * py + pc
    order = [idx(x, y, c), idx(x, y, 1 - c)] + [idx(*ch, c) for ch in chips] + [idx(*ch, 1 - c) for ch in chips]
    return jnp.stack(order).astype(jnp.int32)


def _gather_project(x, w_in, w_out, conv_pack, ln_g, tb):
    t_len, d = x.shape
    nb = t_len // tb
    cols = w_in.shape[1]
    mc = min(512, t_len)
    srcs = (w_in, w_out, conv_pack)
    dts = (MM, MM, F32)

    def body(order_ref, x_ref, win_ref, wout_ref, cp_ref, lng_ref, p_ref, xnt_ref, win_all, wout_all, cp_all,
             xnb, st_in, st_out, st_cp, wbuf, send_sems, recv_sems, local_sems):
        i = pl.program_id(0)
        x_, y_, c_ = _mesh_pos()
        me = 4 * x_ + 2 * y_ + c_
        outs = (win_all, wout_all, cp_all)
        stages = (st_in, st_out, st_cp)
        gather = _Gather(lambda a, px, py, pc: outs[a].at[4 * px + 2 * py + pc], send_sems, recv_sems, own_src=stages)
        keep_own = [pltpu.make_async_copy(stages[a], outs[a].at[me], local_sems.at[a]) for a in range(3)]

        @pl.when(i == 0)
        def _():
            for a, (src, dst) in enumerate(zip((win_ref, wout_ref, cp_ref), stages)):
                rows = src.shape[0]
                rc = min(rows, 32)

                def cast(r, carry, src=src, dst=dst, rc=rc):
                    dst[pl.ds(r, rc), :] = src[pl.ds(r, rc), :].astype(dst.dtype)
                    return carry

                _chunks(rows, rc, cast, 0)
                gather.start_own(a)
                keep_own[a].start()

        @pl.when(i < nb)
        def _():
            xv = x_ref[...]
            r0 = lax.rsqrt(jnp.mean(xv * xv, axis=-1, keepdims=True) + RMS_EPS)
            xn = xv * r0 * lng_ref[...]
            xnb[pl.ds(pl.multiple_of(i * tb, tb), tb), :] = xn.astype(MM)
            xnt_ref[...] = xn.T.astype(MM)

        for k in range(N_DEV):
            @pl.when(i == nb + k)
            def _(k=k):
                if k == 1:
                    gather.wait_sibling(0)
                elif 2 <= k <= 4:
                    gather.wait_chip_and_pass_on(0, k - 2)
                elif k >= 5:
                    gather.wait_passed_on(0, k - 5)
                if k == 0:
                    w_blk = st_in
                else:
                    load = pltpu.make_async_copy(win_all.at[order_ref[k]], wbuf, local_sems.at[3])
                    load.start()
                    load.wait()
                    w_blk = wbuf

                def project(r, carry):
                    rows = pl.ds(r, mc)
                    p_ref[rows, :] = _dot(xnb[rows, :], w_blk[...]).astype(MM)
                    return carry

                _chunks(t_len, mc, project, 0)
                if k == N_DEV - 1:
                    gather.wait_sends(0)
                    gather.finish(1)
                    gather.finish(2)
                    for cp in keep_own:
                        cp.wait()

    vm = pl.BlockSpec(memory_space=pltpu.VMEM)
    hbm = pl.BlockSpec(memory_space=pl.ANY)
    grid_spec = pltpu.PrefetchScalarGridSpec(
        num_scalar_prefetch=1, grid=(nb + N_DEV,),
        in_specs=[pl.BlockSpec((tb, d), lambda i, o: (jnp.minimum(i, nb - 1), 0)), vm, vm, vm, vm],
        out_specs=(pl.BlockSpec((t_len, cols), lambda i, o: (0, o[jnp.maximum(i - nb, 0)])),
                   pl.BlockSpec((d, tb), lambda i, o: (0, jnp.minimum(i, nb - 1))), hbm, hbm, hbm),
        scratch_shapes=[pltpu.VMEM((t_len, d), MM)] + [pltpu.VMEM(s.shape, dt) for s, dt in zip(srcs, dts)]
                       + [pltpu.VMEM(w_in.shape, MM),
                          pltpu.SemaphoreType.DMA((21,)), pltpu.SemaphoreType.DMA((21,)), pltpu.SemaphoreType.DMA((4,))])
    return pl.pallas_call(
        body, name="gather_project", grid_spec=grid_spec,
        out_shape=(jax.ShapeDtypeStruct((t_len, N_DEV * cols), MM),
                   jax.ShapeDtypeStruct((d, t_len), MM))
                  + tuple(jax.ShapeDtypeStruct((N_DEV,) + s.shape, dt) for s, dt in zip(srcs, dts)),
        compiler_params=_params(dimension_semantics=("arbitrary",)),
    )(_block_order(), x, w_in, w_out, conv_pack, ln_g)


def _forward(x, tgt, p, wout, wa_t, wi_t, sp, e_c, et_c, e_l, et_l, tb):
    t_len, d = x.shape
    nb = t_len // tb
    n_tiles, tw = wa_t.shape[0], wa_t.shape[1]
    hd_c, hd_l = d // N_CONV_HEADS, d // N_LRU_HEADS

    def body(x_ref, tgt_ref, p_ref, wout_ref, wa_ref, wi_ref, sp_ref, ec_ref, etc_ref, el_ref, etl_ref,
             h_ref, dh_ref, dhb_ref, acc_ref,
             pf, yc, u, pa, pi, rcf, rlf, ybuf, tail_z, tail_xl, hcar):
        i = pl.program_id(0)
        row = _row_iota(d)

        @pl.when(i == 0)
        def _():
            tail_z[...] = jnp.zeros_like(tail_z)
            tail_xl[...] = jnp.zeros_like(tail_xl)
            hcar[...] = jnp.zeros_like(hcar)
            acc_ref[...] = jnp.zeros_like(acc_ref)

        def spr(r):
            return sp_ref[r:r + 1, :]

        def widen(r, carry):
            pf[pl.ds(r, 16), :] = p_ref[pl.ds(r, 16), :].astype(F32)
            return carry

        _chunks(tb, 16, widen, 0)

        w0, w1, w2 = spr(SP_CONV_W), spr(SP_CONV_W + 1), spr(SP_CONV_W + 2)
        l0, l1, l2, l3 = spr(SP_LRU_W), spr(SP_LRU_W + 1), spr(SP_LRU_W + 2), spr(SP_LRU_W + 3)
        lb = spr(SP_LRU_B)

        def convs(r, carry):
            zp, xp = carry
            rows = pl.ds(r, SUBLANES)
            z = pf[rows, d:2 * d] * pf[rows, 2 * d:3 * d]
            cz = w0 * _shift_down(z, zp, 2, row) + w1 * _shift_down(z, zp, 1, row) + w2 * z
            yc[rows, :] = pf[rows, 0:d] * cz
            xl = pf[rows, 4 * d:5 * d]
            u[rows, :] = (l0 * _shift_down(xl, xp, 3, row) + l1 * _shift_down(xl, xp, 2, row)
                          + l2 * _shift_down(xl, xp, 1, row) + l3 * xl + lb)
            return z, xl

        z_last, xl_last = _chunks(tb, SUBLANES, convs, (tail_z[...], tail_xl[...]))
        tail_z[...] = z_last
        tail_xl[...] = xl_last

        ub = u[...].astype(MM)
        for k in range(n_tiles):
            sl = slice(k * tw, (k + 1) * tw)
            pa[:, sl] = _dot(ub[:, sl], wa_ref[k])
            pi[:, sl] = _dot(ub[:, sl], wi_ref[k])
        rcf[...] = _head_rstd(yc[...], ec_ref[...], etc_ref[...], hd_c)

        c8 = RG_LRU_C * _log_sigmoid(spr(SP_LAM))
        b_a, b_i = spr(SP_B_A), spr(SP_B_I)

        def lru(r, hp):
            rows = pl.ds(r, SUBLANES)
            ra = jax.nn.sigmoid(pa[rows, :] + b_a)
            ii = jax.nn.sigmoid(pi[rows, :] + b_i)
            la = ra * c8
            a = jnp.exp(la)
            mult = jnp.sqrt(_neg_expm1(2.0 * la))
            h = _scan_fwd(a, mult * (ii * u[rows, :]), hp, row)
            h_ref[rows, :] = h
            return _bcast_row(h, SUBLANES - 1)

        hcar[...] = _chunks(tb, SUBLANES, lru, hcar[...])
        rlf[...] = _head_rstd(h_ref[...], el_ref[...], etl_ref[...], hd_l)

        g_c, g_l = spr(SP_CONV_G), spr(SP_LRU_G)

        def gate(r, carry):
            rows = pl.ds(r, SUBLANES)
            gc = pf[rows, 3 * d:4 * d]
            ybuf[rows, 0:d] = yc[rows, :] * rcf[rows, :] * g_c * (gc * jax.nn.sigmoid(gc))
            gl = pf[rows, 5 * d:6 * d]
            ybuf[rows, d:2 * d] = h_ref[rows, :] * rlf[rows, :] * g_l * (gl * jax.nn.sigmoid(gl))
            return carry

        _chunks(tb, SUBLANES, gate, 0)

        hres = x_ref[...] + _dot(ybuf[...].astype(MM), wout_ref[...])
        rf = lax.rsqrt(jnp.mean(hres * hres, axis=-1, keepdims=True) + RMS_EPS)
        hn = hres * rf
        fg = spr(SP_FINAL_G)
        err = hn * fg - tgt_ref[...]
        dout = err * (1.0 / d)
        acc_ref[0:SUBLANES, :] += (err * err).reshape(tb // SUBLANES, SUBLANES, d).sum(axis=0)
        acc_ref[SUBLANES:2 * SUBLANES, :] += (dout * hn).reshape(tb // SUBLANES, SUBLANES, d).sum(axis=0)
        gd = dout * fg
        dhres = rf * (gd - hn * jnp.mean(gd * hn, axis=-1, keepdims=True))
        dh_ref[...] = dhres
        dhb_ref[...] = dhres.astype(MM)

    vm = pl.BlockSpec(memory_space=pltpu.VMEM)
    blk = lambda w: pl.BlockSpec((tb, w), lambda i: (i, 0))
    buf = lambda w: pltpu.VMEM((tb, w), F32)
    return pl.pallas_call(
        body, name="forward", grid=(nb,),
        in_specs=[blk(d), blk(d), blk(6 * d), vm, vm, vm, vm, vm, vm, vm, vm],
        out_specs=(blk(d), blk(d), blk(d), pl.BlockSpec((2 * SUBLANES, d), lambda i: (0, 0))),
        out_shape=(jax.ShapeDtypeStruct((t_len, d), F32),
                   jax.ShapeDtypeStruct((t_len, d), F32),
                   jax.ShapeDtypeStruct((t_len, d), MM),
                   jax.ShapeDtypeStruct((2 * SUBLANES, d), F32)),
        scratch_shapes=[buf(6 * d), buf(d), buf(d), buf(d), buf(d), buf(d), buf(d), buf(2 * d),
                        pltpu.VMEM((SUBLANES, d), F32), pltpu.VMEM((SUBLANES, d), F32), pltpu.VMEM((SUBLANES, d), F32)],
        compiler_params=_params(dimension_semantics=("arbitrary",)),
    )(x, tgt, p, wout, wa_t, wi_t, sp, e_c, et_c, e_l, et_l)


def _backward(p, h, dh, wout, wa_t, wi_t, sp, e_c, et_c, e_l, et_l, tb):
    t_len, d = h.shape
    nb = t_len // tb
    n_tiles, tw = wa_t.shape[0], wa_t.shape[1]
    hd_c, hd_l = d // N_CONV_HEADS, d // N_LRU_HEADS
    ph = 16
    s8 = SUBLANES

    def body(p_ref, phalo_ref, h_ref, hhalo_ref, dh_ref, wout_ref, wa_ref, wi_ref, sp_ref,
             ec_ref, etc_ref, el_ref, etl_ref,
             dp_ref, yt_ref, gwa_ref, gwi_ref, acc_ref,
             pf, hh, dy, ybuf, yc, czs, u, pa, pi, rcf, rlf, qc, ql, dyc_hat, dyl_hat, dpa, dpi, du, dpf,
             car_dcz, car_a, car_g, car_du):
        i = pl.program_id(0)
        blk_idx = nb - 1 - i
        row = _row_iota(d)

        @pl.when(i == 0)
        def _():
            for ref in (car_dcz, car_a, car_g, car_du, gwa_ref, gwi_ref, acc_ref):
                ref[...] = jnp.zeros_like(ref)

        def spr(r):
            return sp_ref[r:r + 1, :]

        def acc_add(group, val):
            acc_ref[group * s8:(group + 1) * s8, :] += val

        live = jnp.where(blk_idx > 0, 1.0, 0.0).astype(F32)
        pf[0:ph, :] = phalo_ref[...].astype(F32) * live
        hh[0:s8, :] = hhalo_ref[...] * live
        hh[s8:, :] = h_ref[...]

        def widen(r, carry):
            pf[pl.ds(ph + r, 16), :] = p_ref[pl.ds(r, 16), :].astype(F32)
            return carry

        _chunks(tb, 16, widen, 0)

        dy[...] = _dot_nt(dh_ref[...].astype(MM), wout_ref[...])

        w0, w1, w2 = spr(SP_CONV_W), spr(SP_CONV_W + 1), spr(SP_CONV_W + 2)
        l0, l1, l2, l3 = spr(SP_LRU_W), spr(SP_LRU_W + 1), spr(SP_LRU_W + 2), spr(SP_LRU_W + 3)
        lb = spr(SP_LRU_B)

        def zx(r):
            rows = pl.ds(ph + r, s8)
            return pf[rows, d:2 * d] * pf[rows, 2 * d:3 * d], pf[rows, 4 * d:5 * d]

        def recompute(r, carry):
            rows = pl.ds(r, s8)
            z, xl = zx(r)
            zp, xp = zx(r - s8)
            cz = w0 * _shift_down(z, zp, 2, row) + w1 * _shift_down(z, zp, 1, row) + w2 * z
            czs[rows, :] = cz
            yc[rows, :] = pf[pl.ds(ph + r, s8), 0:d] * cz
            u[rows, :] = (l0 * _shift_down(xl, xp, 3, row) + l1 * _shift_down(xl, xp, 2, row)
                          + l2 * _shift_down(xl, xp, 1, row) + l3 * xl + lb)
            return carry

        _chunks(tb, s8, recompute, 0)

        ub = u[...].astype(MM)
        for k in range(n_tiles):
            sl = slice(k * tw, (k + 1) * tw)
            pa[:, sl] = _dot(ub[:, sl], wa_ref[k])
            pi[:, sl] = _dot(ub[:, sl], wi_ref[k])
        rcf[...] = _head_rstd(yc[...], ec_ref[...], etc_ref[...], hd_c)
        rlf[...] = _head_rstd(h_ref[...], el_ref[...], etl_ref[...], hd_l)

        g_c, g_l = spr(SP_CONV_G), spr(SP_LRU_G)

        def gates(r, carry):
            rows = pl.ds(r, s8)
            prow = pl.ds(ph + r, s8)
            for (off_g, off_y, src, rstd, gain, q, dhat, grp) in (
                    (3 * d, 0, yc, rcf, g_c, qc, dyc_hat, A_CONV_G),
                    (5 * d, d, h_ref, rlf, g_l, ql, dyl_hat, A_LRU_G)):
                gt = pf[prow, off_g:off_g + d]
                sg = jax.nn.sigmoid(gt)
                silu = gt * sg
                yhat = src[rows, :] * rstd[rows, :]
                nrm = yhat * gain
                ybuf[rows, off_y:off_y + d] = nrm * silu
                dout = dy[rows, off_y:off_y + d]
                dnrm = dout * silu
                dpf[rows, off_g:off_g + d] = dout * nrm * (sg * (1.0 + gt * (1.0 - sg)))
                acc_add(grp, dnrm * yhat)
                dh_ = dnrm * gain
                dhat[rows, :] = dh_
                q[rows, :] = dh_ * yhat
            return carry

        _chunks(tb, s8, gates, 0)

        qc[...] = _head_mean(qc[...], ec_ref[...], etc_ref[...], hd_c)
        ql[...] = _head_mean(ql[...], el_ref[...], etl_ref[...], hd_l)
        yt_ref[...] = ybuf[...].T.astype(MM)

        c8 = RG_LRU_C * _log_sigmoid(spr(SP_LAM))
        b_a, b_i = spr(SP_B_A), spr(SP_B_I)

        def mixers(r, carry):
            dcz_n, a_n, g_n = carry
            rows = pl.ds(r, s8)
            prow = pl.ds(ph + r, s8)
            rstd = rcf[rows, :]
            yhat = yc[rows, :] * rstd
            dyc = rstd * (dyc_hat[rows, :] - yhat * qc[rows, :])
            dpf[rows, 0:d] = dyc * czs[rows, :]
            dcz = dyc * pf[prow, 0:d]
            dz = w2 * dcz + w1 * _shift_up(dcz, dcz_n, 1, row) + w0 * _shift_up(dcz, dcz_n, 2, row)
            z, _ = zx(r)
            zp, _ = zx(r - s8)
            dpf[rows, d:2 * d] = dz * pf[prow, 2 * d:3 * d]
            dpf[rows, 2 * d:3 * d] = dz * pf[prow, d:2 * d]
            acc_add(A_CONV_W, dcz * _shift_down(z, zp, 2, row))
            acc_add(A_CONV_W + 1, dcz * _shift_down(z, zp, 1, row))
            acc_add(A_CONV_W + 2, dcz * z)
            rstd = rlf[rows, :]
            hcur = hh[pl.ds(s8 + r, s8), :]
            hhat = hcur * rstd
            dh_out = rstd * (dyl_hat[rows, :] - hhat * ql[rows, :])
            ra = jax.nn.sigmoid(pa[rows, :] + b_a)
            la = ra * c8
            a = jnp.exp(la)
            g = _scan_bwd(_shift_up(a, a_n, 1, row), dh_out, g_n, row)
            da = g * _shift_down(hcur, hh[pl.ds(r, s8), :], 1, row)
            ii = jax.nn.sigmoid(pi[rows, :] + b_i)
            uu = u[rows, :]
            mult = jnp.sqrt(_neg_expm1(2.0 * la))
            dmult = g * (ii * uu)
            ds = g * mult
            dla = a * (da - dmult * a / mult)
            acc_add(A_LAM, dla * ra)
            dpa_ = dla * c8 * ra * (1.0 - ra)
            dpi_ = ds * uu * ii * (1.0 - ii)
            acc_add(A_B_A, dpa_)
            acc_add(A_B_I, dpi_)
            dpa[rows, :] = dpa_
            dpi[rows, :] = dpi_
            du[rows, :] = ds * ii
            return dcz, a, _bcast_row(g, 0)

        dcz_f, a_f, g_f = _chunks(tb, s8, mixers, (car_dcz[...], car_a[...], car_g[...]), reverse=True)
        car_dcz[...] = dcz_f
        car_a[...] = a_f
        car_g[...] = g_f

        dpab = dpa[...].astype(MM)
        dpib = dpi[...].astype(MM)
        for k in range(n_tiles):
            sl = slice(k * tw, (k + 1) * tw)
            du[:, sl] += _dot_nt(dpab[:, sl], wa_ref[k]) + _dot_nt(dpib[:, sl], wi_ref[k])
            ut = u[:, sl].T.astype(MM)
            gwa_ref[k] += _dot(ut, dpab[:, sl])
            gwi_ref[k] += _dot(ut, dpib[:, sl])

        def lru_conv(r, du_n):
            rows = pl.ds(r, s8)
            dut = du[rows, :]
            dpf[rows, 4 * d:5 * d] = (l3 * dut + l2 * _shift_up(dut, du_n, 1, row)
                                      + l1 * _shift_up(dut, du_n, 2, row) + l0 * _shift_up(dut, du_n, 3, row))
            _, xl = zx(r)
            _, xp = zx(r - s8)
            acc_add(A_LRU_W, dut * _shift_down(xl, xp, 3, row))
            acc_add(A_LRU_W + 1, dut * _shift_down(xl, xp, 2, row))
            acc_add(A_LRU_W + 2, dut * _shift_down(xl, xp, 1, row))
            acc_add(A_LRU_W + 3, dut * xl)
            acc_add(A_LRU_B, dut)
            return dut

        car_du[...] = _chunks(tb, s8, lru_conv, car_du[...], reverse=True)

        def narrow(r, carry):
            dp_ref[pl.ds(r, 16), :] = dpf[pl.ds(r, 16), :].astype(MM)
            return carry

        _chunks(tb, 16, narrow, 0)

    vm = pl.BlockSpec(memory_space=pltpu.VMEM)
    rev = lambda w: pl.BlockSpec((tb, w), lambda i: (nb - 1 - i, 0))
    halo = lambda rows, w: pl.BlockSpec((rows, w), lambda i: (jnp.maximum((nb - 1 - i) * (tb // rows) - 1, 0), 0))
    const = lambda shape: pl.BlockSpec(shape, lambda i: (0,) * len(shape))
    buf = lambda w: pltpu.VMEM((tb, w), F32)
    car = pltpu.VMEM((SUBLANES, d), F32)
    return pl.pallas_call(
        body, name="backward", grid=(nb,),
        in_specs=[rev(6 * d), halo(ph, 6 * d), rev(d), halo(SUBLANES, d), rev(d), vm, vm, vm, vm, vm, vm, vm, vm],
        out_specs=(rev(6 * d), pl.BlockSpec((2 * d, tb), lambda i: (0, nb - 1 - i)),
                   const((n_tiles, tw, tw)), const((n_tiles, tw, tw)), const((A_GROUPS * SUBLANES, d))),
        out_shape=(jax.ShapeDtypeStruct((t_len, 6 * d), MM),
                   jax.ShapeDtypeStruct((2 * d, t_len), MM),
                   jax.ShapeDtypeStruct((n_tiles, tw, tw), F32),
                   jax.ShapeDtypeStruct((n_tiles, tw, tw), F32),
                   jax.ShapeDtypeStruct((A_GROUPS * SUBLANES, d), F32)),
        scratch_shapes=[pltpu.VMEM((ph + tb, 6 * d), F32), pltpu.VMEM((SUBLANES + tb, d), F32), buf(2 * d), buf(2 * d)]
                       + [buf(d)] * 14 + [buf(6 * d), car, car, car, car],
        compiler_params=_params(dimension_semantics=("arbitrary",)),
    )(p, p, h, h, dh, wout, wa_t, wi_t, sp, e_c, et_c, e_l, et_l)


def _input_grad(dp, win_all, x, dh, sp, parts, tb):
    t_len, d = x.shape
    nb = t_len // tb
    cols = win_all.shape[2]
    n_parts = len(parts)

    def body(dp_ref, win_ref, x_ref, dh_ref, sp_ref, *refs):
        gx_ref, acc_ref = refs[n_parts:n_parts + 2]
        exchange = _ChipExchange(refs[:n_parts], refs[n_parts + 2:2 * n_parts + 2], *refs[2 * n_parts + 2:])
        i = pl.program_id(0)

        @pl.when(i == 0)
        def _():
            acc_ref[...] = jnp.zeros_like(acc_ref)
            exchange.start()

        dxn = _dot_nt(dp_ref[:, 0:cols], win_ref[0])
        for j in range(1, N_DEV):
            dxn += _dot_nt(dp_ref[:, j * cols:(j + 1) * cols], win_ref[j])
        xv = x_ref[...]
        r0 = lax.rsqrt(jnp.mean(xv * xv, axis=-1, keepdims=True) + RMS_EPS)
        xhat = xv * r0
        acc_ref[...] += (dxn * xhat).reshape(tb // SUBLANES, SUBLANES, d).sum(axis=0)
        dxh = dxn * sp_ref[SP_LN_G:SP_LN_G + 1, :]
        gx_ref[...] = dh_ref[...] + r0 * (dxh - xhat * jnp.mean(dxh * xhat, axis=-1, keepdims=True))

        @pl.when(i == nb - 1)
        def _():
            exchange.finish()

    vm = pl.BlockSpec(memory_space=pltpu.VMEM)
    hbm = pl.BlockSpec(memory_space=pl.ANY)
    blk = lambda w: pl.BlockSpec((tb, w), lambda i: (i, 0))
    outs = pl.pallas_call(
        body, name="input_grad", grid=(nb,),
        in_specs=[blk(6 * d), vm, blk(d), blk(d), vm] + [hbm] * n_parts,
        out_specs=(blk(d), pl.BlockSpec((SUBLANES, d), lambda i: (0, 0))) + (hbm,) * n_parts,
        out_shape=(jax.ShapeDtypeStruct((t_len, d), F32), jax.ShapeDtypeStruct((SUBLANES, d), F32))
                  + tuple(jax.ShapeDtypeStruct(p.shape, p.dtype) for p in parts),
        scratch_shapes=[pltpu.SemaphoreType.DMA((3 * n_parts,)), pltpu.SemaphoreType.DMA((3 * n_parts,))],
        compiler_params=_params(dimension_semantics=("arbitrary",)),
    )(dp, win_all, x, dh, sp, *parts)
    return outs[0], outs[1], outs[2:]


_CHIP_RELATIONS = [(0, 0), (1, 0), (0, 1), (1, 1)]


def _related_block(k, core):
    x, y, _ = _mesh_pos()
    fx, fy = _CHIP_RELATIONS[k]
    return 4 * (x ^ fx) + 2 * (y ^ fy) + core


class _ChipExchange:
    def __init__(self, part_refs, land_refs, send_sems, recv_sems):
        self.part_refs, self.land_refs, self.send_sems, self.recv_sems = part_refs, land_refs, send_sems, recv_sems

    def copies(self):
        x, y, c = _mesh_pos()
        for a in range(len(self.part_refs)):
            for k in (1, 2, 3):
                fx, fy = _CHIP_RELATIONS[k]
                yield pltpu.make_async_remote_copy(
                    src_ref=self.part_refs[a].at[k - 1], dst_ref=self.land_refs[a].at[k - 1],
                    send_sem=self.send_sems.at[3 * a + k - 1], recv_sem=self.recv_sems.at[3 * a + k - 1],
                    device_id=(x ^ fx, y ^ fy, c), device_id_type=MESH)

    def start(self):
        for cp in self.copies():
            cp.start()

    def finish(self):
        for cp in self.copies():
            cp.wait_recv()
        for cp in self.copies():
            cp.wait_send()


def _weight_grad_stage1(name, blk_shape, n_split, operands, in_specs, product, riders=()):
    n_rows, n_cols = blk_shape
    rs = n_rows // n_split
    rc = 32
    n_in, n_ride = len(operands), len(riders)
    _, _, c = _mesh_pos()
    order = jnp.stack([_related_block(k, 1 - c) for k in range(4)]
                      + [_related_block(k, c) for k in (1, 2, 3, 0)]).astype(jnp.int32)

    def body(order_ref, *refs):
        ins = refs[:n_in]
        ride_in = refs[n_in:n_in + n_ride]
        part_ref, own_ref, from_sib = refs[n_in + n_ride:n_in + n_ride + 3]
        ride_out = refs[n_in + n_ride + 3:n_in + 2 * n_ride + 3]
        gbuf, sendbuf, tmp, send_sems, recv_sems, local_sems, ride_send, ride_recv = refs[n_in + 2 * n_ride + 3:]
        exchange = _ChipExchange(ride_in, ride_out, ride_send, ride_recv)
        s = pl.program_id(0)
        x, y, c = _mesh_pos()

        def to_sibling(k):
            return pltpu.make_async_remote_copy(
                src_ref=sendbuf.at[k], dst_ref=from_sib.at[k], send_sem=send_sems.at[k], recv_sem=recv_sems.at[k],
                device_id=(x, y, 1 - c), device_id_type=MESH)

        if n_ride:
            @pl.when(s == 0)
            def _():
                exchange.start()

        for h in range(n_split):
            gbuf[h * rs:(h + 1) * rs, :] = product(ins, h)

        @pl.when(s < 4)
        def _():
            def narrow(r, carry):
                sendbuf[s, pl.ds(r, rc), :] = gbuf[pl.ds(r, rc), :].astype(MM)
                return carry

            _chunks(n_rows, rc, narrow, 0)
            to_sibling(s).start()

        @pl.when(s >= 4)
        def _():
            k = jnp.where(s == 7, 0, s - 3)
            to_sibling(k).wait_recv()
            load = pltpu.make_async_copy(from_sib.at[k], tmp, local_sems.at[0])
            load.start()
            load.wait()

            @pl.when(s < 7)
            def _():
                def add(r, carry):
                    rows = pl.ds(r, rc)
                    part_ref[0, rows, :] = (gbuf[rows, :] + tmp[rows, :].astype(F32)).astype(MM)
                    return carry

                _chunks(n_rows, rc, add, 0)

            @pl.when(s == 7)
            def _():
                def add(r, carry):
                    rows = pl.ds(r, rc)
                    own_ref[rows, :] = gbuf[rows, :] + tmp[rows, :].astype(F32)
                    return carry

                _chunks(n_rows, rc, add, 0)
                for kk in range(4):
                    to_sibling(kk).wait_send()
                if n_ride:
                    exchange.finish()

    hbm = pl.BlockSpec(memory_space=pl.ANY)
    grid_spec = pltpu.PrefetchScalarGridSpec(
        num_scalar_prefetch=1, grid=(N_DEV,), in_specs=list(in_specs) + [hbm] * n_ride,
        out_specs=(pl.BlockSpec((1, n_rows, n_cols), lambda s, o: (jnp.clip(s - 4, 0, 2), 0, 0)),
                   pl.BlockSpec((n_rows, n_cols), lambda s, o: (0, 0)), hbm) + (hbm,) * n_ride,
        scratch_shapes=[pltpu.VMEM((n_rows, n_cols), F32), pltpu.VMEM((4, n_rows, n_cols), MM),
                        pltpu.VMEM((n_rows, n_cols), MM),
                        pltpu.SemaphoreType.DMA((4,)), pltpu.SemaphoreType.DMA((4,)), pltpu.SemaphoreType.DMA((1,)),
                        pltpu.SemaphoreType.DMA((max(3 * n_ride, 1),)), pltpu.SemaphoreType.DMA((max(3 * n_ride, 1),))])
    outs = pl.pallas_call(
        body, name=name, grid_spec=grid_spec,
        out_shape=(jax.ShapeDtypeStruct((3, n_rows, n_cols), MM), jax.ShapeDtypeStruct((n_rows, n_cols), F32),
                   jax.ShapeDtypeStruct((4, n_rows, n_cols), MM))
                  + tuple(jax.ShapeDtypeStruct(p.shape, p.dtype) for p in riders),
        compiler_params=_params(dimension_semantics=("arbitrary",)),
    )(order, *operands, *riders)
    return outs[0], outs[1], outs[3:]


def _weight_grad_in(xnt, dp, riders):
    d, t_len = xnt.shape
    cols = dp.shape[1] // N_DEV
    half = d // 2
    return _weight_grad_stage1(
        "weight_grad_in", (d, cols), 2, (xnt, dp),
        [pl.BlockSpec(memory_space=pltpu.VMEM), pl.BlockSpec((t_len, cols), lambda s, o: (0, o[s]))],
        lambda refs, h: _dot(refs[0][h * half:(h + 1) * half, :], refs[1][...]), riders)


def _weight_grad_out(yt, dhb):
    d2, t_len = yt.shape
    d = dhb.shape[1]
    rows = d2 // N_DEV
    return _weight_grad_stage1(
        "weight_grad_out", (rows, d), 1, (yt, dhb),
        [pl.BlockSpec((rows, t_len), lambda s, o: (o[s], 0)), pl.BlockSpec(memory_space=pltpu.VMEM)],
        lambda refs, h: _dot(refs[0][...], refs[1][...]))


def _update_shard(own, from_chips, w, m, v, name):
    n_rows, n_cols = w.shape
    rb = min(256, n_rows)

    def body(own_ref, fc_ref, w_ref, m_ref, v_ref, grad_ref, delta_ref, mo_ref, vo_ref):
        g = own_ref[...]
        for k in range(3):
            g = g + fc_ref[k].astype(F32)
        delta, m_new, v_new = _adamw(w_ref[...], g, m_ref[...], v_ref[...])
        grad_ref[...] = g
        delta_ref[...] = delta
        mo_ref[...] = m_new
        vo_ref[...] = v_new

    blk = pl.BlockSpec((rb, n_cols), lambda i: (i, 0))
    out = jax.ShapeDtypeStruct((n_rows, n_cols), F32)
    return pl.pallas_call(
        body, name=name, grid=(n_rows // rb,),
        in_specs=[blk, pl.BlockSpec((3, rb, n_cols), lambda i: (0, i, 0)), blk, blk, blk],
        out_specs=(blk, blk, blk, blk), out_shape=(out, out, out, out),
        compiler_params=_params(dimension_semantics=("arbitrary",)),
    )(own, from_chips, w, m, v)


def _small_update(facc, xacc, bacc, gwa, gwi, lam, wsl, msl, vsl, cw, cm, cv):
    n_rows, d = wsl.shape
    s8 = SUBLANES
    cc = cw.shape[1]

    def body(facc_ref, xacc_ref, bacc_ref, gwa_ref, gwi_ref, lam_ref, w_ref, m_ref, v_ref, cw_ref, cm_ref, cv_ref,
             g_o, d_o, m_o, v_o, cg_o, cd_o, cm_o, cv_o, gat, send_sems, recv_sems):
        x, y, c = _mesh_pos()
        me = 4 * x + 2 * y + c

        def rowsum(ref, group):
            return jnp.sum(ref[group * s8:(group + 1) * s8, :], axis=0, keepdims=True)

        mine = gat.at[me]
        loss = jnp.sum(rowsum(facc_ref, 0), axis=1, keepdims=True) * (0.5 / d)
        mine[SL_LOSS:SL_LOSS + 1, :] = jnp.broadcast_to(loss, (1, d))
        mine[SL_LN_G:SL_LN_G + 1, :] = rowsum(xacc_ref, 0)
        mine[SL_LRU_B:SL_LRU_B + 1, :] = rowsum(bacc_ref, A_LRU_B)
        mine[SL_B_A:SL_B_A + 1, :] = rowsum(bacc_ref, A_B_A)
        mine[SL_B_I:SL_B_I + 1, :] = rowsum(bacc_ref, A_B_I)
        mine[SL_LAM:SL_LAM + 1, :] = rowsum(bacc_ref, A_LAM) * (RG_LRU_C * jax.nn.sigmoid(-lam_ref[...]))
        mine[SL_CONV_G:SL_CONV_G + 1, :] = rowsum(bacc_ref, A_CONV_G)
        mine[SL_LRU_G:SL_LRU_G + 1, :] = rowsum(bacc_ref, A_LRU_G)
        mine[SL_FINAL_G:SL_FINAL_G + 1, :] = rowsum(facc_ref, 1)
        for k in range(3):
            mine[SL_CONV_W + k:SL_CONV_W + k + 1, :] = rowsum(bacc_ref, A_CONV_W + k)
        for k in range(4):
            mine[SL_LRU_W + k:SL_LRU_W + k + 1, :] = rowsum(bacc_ref, A_LRU_W + k)
        na = gwa_ref.shape[0]
        mine[SL_W_A:SL_W_A + na, :] = gwa_ref[...]
        mine[SL_W_A + na:SL_W_A + 2 * na, :] = gwi_ref[...]

        gather = _Gather(lambda a, px, py, pc: gat.at[4 * px + 2 * py + pc], send_sems, recv_sems)
        gather.start_own(0)
        gather.finish(0)

        def update(r, carry):
            rows = pl.ds(r, s8)
            g = gat[0, rows, :]
            for b in range(1, N_DEV):
                g = g + gat[b, rows, :]
            delta, m_new, v_new = _adamw(w_ref[rows, :], g, m_ref[rows, :], v_ref[rows, :])
            g_o[rows, :] = g
            d_o[rows, :] = delta
            m_o[rows, :] = m_new
            v_o[rows, :] = v_new
            return carry

        _chunks(n_rows, s8, update, 0)
        gc = g_o[s8:2 * s8, pl.ds(pl.multiple_of(me * cc, cc), cc)]
        delta, m_new, v_new = _adamw(cw_ref[...], gc, cm_ref[...], cv_ref[...])
        cg_o[...] = gc
        cd_o[...] = delta
        cm_o[...] = m_new
        cv_o[...] = v_new

    vm = pl.BlockSpec(memory_space=pltpu.VMEM)
    big = jax.ShapeDtypeStruct((n_rows, d), F32)
    small = jax.ShapeDtypeStruct(cw.shape, F32)
    return pl.pallas_call(
        body, name="small_update",
        in_specs=[vm] * 12, out_specs=(vm,) * 8, out_shape=(big, big, big, big, small, small, small, small),
        scratch_shapes=[pltpu.VMEM((N_DEV, n_rows, d), F32), pltpu.SemaphoreType.DMA((7,)), pltpu.SemaphoreType.DMA((7,))],
        compiler_params=_params(),
    )(facc, xacc, bacc, gwa, gwi, lam, wsl, msl, vsl, cw, cm, cv)


def _head_selectors(d, n_heads):
    lane = jnp.arange(d)[:, None] // (d // n_heads)
    e = (lane == jnp.arange(LANES)[None, :]).astype(MM)
    return e, e.T


def _gate_tiles(w, tw):
    n_heads, hd, _ = w.shape
    per = tw // hd
    w4 = w.reshape(n_heads // per, per, hd, hd)
    eye = jnp.eye(per, dtype=w.dtype)
    return (w4[:, :, :, None, :] * eye[None, :, None, :, None]).reshape(n_heads // per, tw, tw)


def _gate_blocks(tiles, n_heads, hd):
    n_tiles, tw, _ = tiles.shape
    per = tw // hd
    t5 = tiles.reshape(n_tiles, per, hd, per, hd)
    diag = jnp.stack([t5[:, a, :, a, :] for a in range(per)], axis=1)
    return diag.reshape(hd, n_heads * hd)


def kernel(x, ln_g, w_in, conv_w, lru_conv_w, lru_conv_b, w_a, b_a, w_i, b_i, lam, conv_out_g, lru_out_g, w_out, final_g, loss_target, m_ln_g, m_w_in, m_conv_w, m_lru_conv_w, m_lru_conv_b, m_w_a, m_b_a, m_w_i, m_b_i, m_lam, m_conv_out_g, m_lru_out_g, m_w_out, m_final_g, v_ln_g, v_w_in, v_conv_w, v_lru_conv_w, v_lru_conv_b, v_w_a, v_b_a, v_w_i, v_b_i, v_lam, v_conv_out_g, v_lru_out_g, v_w_out, v_final_g):
    _, t_len, d = x.shape
    hd_l = d // N_LRU_HEADS
    tw = min(MXU_TILE, d)
    cc = conv_w.shape[1]
    tb = 128
    x2, tgt2 = x[0], loss_target[0]

    def conv_rows(cw3, lw4):
        return jnp.concatenate([jnp.zeros((1, cc), F32), cw3, lw4], axis=0)

    p, xnt, win_all, wout_all, conv_all = _gather_project(
        x2, w_in, w_out, conv_rows(conv_w, lru_conv_w), ln_g.reshape(1, d), min(256, t_len))
    wout_full = wout_all.reshape(N_DEV * w_out.shape[0], d)
    conv_full = conv_all.transpose(1, 0, 2).reshape(SUBLANES, d)
    small = [ln_g, lru_conv_b, b_a, b_i, lam, conv_out_g, lru_out_g, final_g]
    sp = jnp.concatenate([jnp.stack(small), conv_full[1:], jnp.zeros((1, d), F32)], axis=0)
    wa_t, wi_t = _gate_tiles(w_a, tw).astype(MM), _gate_tiles(w_i, tw).astype(MM)
    e_c, et_c = _head_selectors(d, N_CONV_HEADS)
    e_l, et_l = _head_selectors(d, N_LRU_HEADS)

    h, dh, dhb, facc = _forward(x2, tgt2, p, wout_full, wa_t, wi_t, sp, e_c, et_c, e_l, et_l, min(256, t_len))
    dp, yt, gwa_t, gwi_t, bacc = _backward(p, h, dh, wout_full, wa_t, wi_t, sp, e_c, et_c, e_l, et_l, tb)
    part_out, own_out, _ = _weight_grad_out(yt, dhb)
    part_in, own_in, (chips_out,) = _weight_grad_in(xnt, dp, (part_out,))
    grad_x, xacc, (chips_in,) = _input_grad(dp, win_all, x2, dh, sp, (part_in,), min(512, t_len))
    gw_in, dw_in, mw_in, vw_in = _update_shard(own_in, chips_in, w_in, m_w_in, v_w_in, "update_w_in")
    gw_out, dw_out, mw_out, vw_out = _update_shard(own_out, chips_out, w_out, m_w_out, v_w_out, "update_w_out")

    def slab(parts, wa_, wi_):
        return jnp.concatenate([jnp.zeros((1, d), F32), jnp.stack(parts), jnp.zeros((SL_W_A - SL_CONV_W, d), F32),
                                wa_.reshape(-1, d), wi_.reshape(-1, d)], axis=0)

    wsl = slab(small, w_a, w_i)
    msl = slab([m_ln_g, m_lru_conv_b, m_b_a, m_b_i, m_lam, m_conv_out_g, m_lru_out_g, m_final_g], m_w_a, m_w_i)
    vsl = slab([v_ln_g, v_lru_conv_b, v_b_a, v_b_i, v_lam, v_conv_out_g, v_lru_out_g, v_final_g], v_w_a, v_w_i)
    outs = _small_update(
        facc, xacc, bacc, _gate_blocks(gwa_t, N_LRU_HEADS, hd_l), _gate_blocks(gwi_t, N_LRU_HEADS, hd_l),
        lam.reshape(1, d), wsl, msl, vsl,
        conv_rows(conv_w, lru_conv_w), conv_rows(m_conv_w, m_lru_conv_w), conv_rows(v_conv_w, v_lru_conv_w))
    sl_g, sl_d, sl_m, sl_v, c_g, c_d, c_m, c_v = outs
    loss = sl_g[SL_LOSS, 0]
    na = w_a.size // d

    def unpack(sl, cv, big_in, big_out):
        one = lambda r: sl[r]
        return [one(SL_LN_G), big_in, cv[1:4], cv[4:8], one(SL_LRU_B),
                sl[SL_W_A:SL_W_A + na].reshape(w_a.shape), one(SL_B_A),
                sl[SL_W_A + na:SL_W_A + 2 * na].reshape(w_i.shape), one(SL_B_I), one(SL_LAM),
                one(SL_CONV_G), one(SL_LRU_G), big_out, one(SL_FINAL_G)]

    return (loss, grad_x[None], *unpack(sl_g, c_g, gw_in, gw_out), *unpack(sl_d, c_d, dw_in, dw_out),
            *unpack(sl_m, c_m, mw_in, mw_out), *unpack(sl_v, c_v, vw_in, vw_out))
```

```python
import functools

import jax
import jax.numpy as jnp
from jax import lax
from jax.experimental import pallas as pl
from jax.experimental.pallas import tpu as pltpu

F32 = jnp.float32
MM = jnp.bfloat16
MESH = pl.DeviceIdType.MESH

N_DEV = 8
N_CONV_HEADS = 8
N_LRU_HEADS = 16
RG_LRU_C = 8.0
RMS_EPS = 1e-6
ADAM_LR, ADAM_B1, ADAM_B2, ADAM_EPS, ADAM_WD, ADAM_STEP = 0.001, 0.9, 0.999, 1e-08, 0.01, 10
ADAM_BC1 = 1.0 - ADAM_B1 ** ADAM_STEP
ADAM_BC2 = 1.0 - ADAM_B2 ** ADAM_STEP

SUBLANES = 8
LANES = 128
MXU_TILE = 256
VMEM_LIMIT = 56 * 1024 * 1024

SP_LN_G, SP_LRU_B, SP_B_A, SP_B_I, SP_LAM, SP_CONV_G, SP_LRU_G, SP_FINAL_G, SP_CONV_W, SP_LRU_W = 0, 1, 2, 3, 4, 5, 6, 7, 8, 11
SP_ROWS = 16
P_B, P_C, P_XC, P_GC, P_XL, P_GL = 0, 1, 2, 3, 4, 5
A_CONV_G, A_LRU_G, A_LAM, A_B_A, A_B_I, A_CONV_W, A_LRU_W, A_LRU_B = 0, 1, 2, 3, 4, 5, 8, 12
A_GROUPS = 13
SL_LOSS, SL_LN_G, SL_LRU_B, SL_B_A, SL_B_I, SL_LAM, SL_CONV_G, SL_LRU_G, SL_FINAL_G, SL_CONV_W, SL_LRU_W, SL_W_A = 0, 1, 2, 3, 4, 5, 6, 7, 8, 9, 12, 16


def _params(vmem=True, **kw):
    if vmem:
        kw["vmem_limit_bytes"] = VMEM_LIMIT
    return pltpu.CompilerParams(**kw)


def _dot(a, b):
    return jnp.dot(a, b, preferred_element_type=F32)


def _dot_nt(a, b):
    return lax.dot_general(a, b, (((1,), (1,)), ((), ())), preferred_element_type=F32)


def _head_sums(v, ones_tile):
    tw = ones_tile.shape[0]
    hi = v.astype(MM)
    lo = (v - hi.astype(F32)).astype(MM)
    return jnp.concatenate(
        [_dot(hi[:, k:k + tw], ones_tile) + _dot(lo[:, k:k + tw], ones_tile) for k in range(0, v.shape[1], tw)], axis=1)


def _head_rstd(v, ones_tile, head_dim):
    return lax.rsqrt(_head_sums(v * v, ones_tile) * (1.0 / head_dim) + RMS_EPS)


def _neg_expm1(x):
    series = -x * (1.0 + x * (0.5 + x * (1.0 / 6.0 + x * (1.0 / 24.0 + x * (1.0 / 120.0)))))
    return jnp.where(x > -0.03, series, 1.0 - jnp.exp(x))


def _log_sigmoid(x):
    z = jnp.exp(-jnp.abs(x))
    u = 1.0 + z
    log1p_z = jnp.where(u == 1.0, z, jnp.log(u) * (z / (u - 1.0)))
    return jnp.minimum(x, 0.0) - log1p_z


def _row_iota(d):
    return lax.broadcasted_iota(jnp.int32, (SUBLANES, d), 0)


def _shift_down(cur, prev, s, row):
    return jnp.where(row >= s, pltpu.roll(cur, s, axis=0), pltpu.roll(prev, s, axis=0))


def _shift_up(cur, nxt, s, row):
    k = SUBLANES - s
    return jnp.where(row < k, pltpu.roll(cur, k, axis=0), pltpu.roll(nxt, k, axis=0))


def _scan_fwd(a, b, h_prev, row):
    for s in (1, 2, 4):
        a_s = jnp.where(row >= s, pltpu.roll(a, s, axis=0), 1.0)
        b_s = jnp.where(row >= s, pltpu.roll(b, s, axis=0), 0.0)
        b = a * b_s + b
        a = a * a_s
    return a * h_prev + b


def _scan_bwd(a_next, b, g_next, row):
    a = a_next
    for s in (1, 2, 4):
        k = SUBLANES - s
        a_s = jnp.where(row < k, pltpu.roll(a, k, axis=0), 1.0)
        b_s = jnp.where(row < k, pltpu.roll(b, k, axis=0), 0.0)
        b = a * b_s + b
        a = a * a_s
    return a * g_next + b


def _bcast_row(v, r):
    return jnp.broadcast_to(v[r:r + 1, :], v.shape)


def _chunks(n_rows, rc, body, init, reverse=False):
    n = n_rows // rc

    def step(i, carry):
        j = (n - 1 - i) if reverse else i
        return body(pl.multiple_of(j * rc, rc), carry)

    return lax.fori_loop(0, n, step, init)


def _adamw(w, g, m, v):
    m = ADAM_B1 * m + (1.0 - ADAM_B1) * g
    v = ADAM_B2 * v + (1.0 - ADAM_B2) * (g * g)
    m_hat = m / ADAM_BC1
    v_hat = v / ADAM_BC2
    delta = -ADAM_LR * (m_hat / (jnp.sqrt(v_hat) + ADAM_EPS) + ADAM_WD * w)
    return delta, m, v


def _mesh_pos():
    return lax.axis_index("x"), lax.axis_index("y"), lax.axis_index("c")


class _Gather:
    def __init__(self, blocks_of, send_sems, recv_sems, own_src=None):
        x, y, c = _mesh_pos()
        self.c = c
        self.me, self.sibling = (x, y, c), (x, y, 1 - c)
        self.chips = [(1 - x, y), (x, 1 - y), (1 - x, 1 - y)]
        self.blocks_of, self.send_sems, self.recv_sems = blocks_of, send_sems, recv_sems
        self.own_src = own_src

    def copy(self, a, k, block, to):
        src = self.blocks_of(a, *block)
        if block is self.me and self.own_src is not None:
            src = self.own_src[a]
        return pltpu.make_async_remote_copy(
            src_ref=src, dst_ref=self.blocks_of(a, *block),
            send_sem=self.send_sems.at[a * 7 + k], recv_sem=self.recv_sems.at[a * 7 + k],
            device_id=to, device_id_type=MESH)

    def start_own(self, a):
        self.copy(a, 0, self.me, self.sibling).start()
        for j, chip in enumerate(self.chips):
            self.copy(a, 1 + j, self.me, (*chip, self.c)).start()

    def wait_sibling(self, a):
        self.copy(a, 0, self.sibling, self.me).wait_recv()

    def wait_chip_and_pass_on(self, a, j):
        block = (*self.chips[j], self.c)
        self.copy(a, 1 + j, block, self.me).wait_recv()
        self.copy(a, 4 + j, block, self.sibling).start()

    def wait_passed_on(self, a, j):
        self.copy(a, 4 + j, (*self.chips[j], 1 - self.c), self.me).wait_recv()

    def wait_sends(self, a):
        self.copy(a, 0, self.me, self.sibling).wait_send()
        for j, chip in enumerate(self.chips):
            self.copy(a, 1 + j, self.me, (*chip, self.c)).wait_send()
            self.copy(a, 4 + j, (*chip, self.c), self.sibling).wait_send()

    def finish(self, a):
        for j in range(3):
            self.wait_chip_and_pass_on(a, j)
        self.wait_sibling(a)
        for j in range(3):
            self.wait_passed_on(a, j)
        self.wait_sends(a)


def _block_order():
    x, y, c = _mesh_pos()
    chips = [(1 - x, y), (x, 1 - y), (1 - x, 1 - y)]
    idx = lambda px, py, pc: 4 * px + 2 * py + pc
    order = [idx(x, y, c), idx(x, y, 1 - c)] + [idx(*ch, c) for ch in chips] + [idx(*ch, 1 - c) for ch in chips]
    return jnp.stack(order).astype(jnp.int32)


def _gather_project(x, w_in, w_out, conv_pack, ln_g, tb):
    t_len, d = x.shape
    nb = t_len // tb
    cols = w_in.shape[1]
    mc = min(512, t_len)
    srcs = (w_in, w_out, conv_pack)
    dts = (MM, MM, F32)

    def body(order_ref, x_ref, win_ref, wout_ref, cp_ref, lng_ref, p_ref, xnt_ref, win_all, wout_all, cp_all,
             xnb, st_in, st_out, st_cp, wbuf, send_sems, recv_sems, local_sems):
        i = pl.program_id(0)
        x_, y_, c_ = _mesh_pos()
        me = 4 * x_ + 2 * y_ + c_
        outs = (win_all, wout_all, cp_all)
        stages = (st_in, st_out, st_cp)
        gather = _Gather(lambda a, px, py, pc: outs[a].at[4 * px + 2 * py + pc], send_sems, recv_sems, own_src=stages)
        keep_own = [pltpu.make_async_copy(stages[a], outs[a].at[me], local_sems.at[a]) for a in range(3)]

        @pl.when(i == 0)
        def _():
            for a, (src, dst) in enumerate(zip((win_ref, wout_ref, cp_ref), stages)):
                rows = src.shape[0]
                rc = min(rows, 32)

                def cast(r, carry, src=src, dst=dst, rc=rc):
                    dst[pl.ds(r, rc), :] = src[pl.ds(r, rc), :].astype(dst.dtype)
                    return carry

                _chunks(rows, rc, cast, 0)
                gather.start_own(a)
                keep_own[a].start()

        @pl.when(i < nb)
        def _():
            xv = x_ref[...]
            r0 = lax.rsqrt(jnp.mean(xv * xv, axis=-1, keepdims=True) + RMS_EPS)
            xn = xv * r0 * lng_ref[...]
            xnb[pl.ds(pl.multiple_of(i * tb, tb), tb), :] = xn.astype(MM)
            xnt_ref[...] = xn.T.astype(MM)

        for k in range(N_DEV):
            @pl.when(i == nb + k)
            def _(k=k):
                if k == 1:
                    gather.wait_sibling(0)
                elif 2 <= k <= 4:
                    gather.wait_chip_and_pass_on(0, k - 2)
                elif k >= 5:
                    gather.wait_passed_on(0, k - 5)
                if k == 0:
                    w_blk = st_in
                else:
                    load = pltpu.make_async_copy(win_all.at[order_ref[k]], wbuf, local_sems.at[3])
                    load.start()
                    load.wait()
                    w_blk = wbuf

                def project(r, carry):
                    rows = pl.ds(r, mc)
                    p_ref[rows, :] = _dot(xnb[rows, :], w_blk[...]).astype(MM)
                    return carry

                _chunks(t_len, mc, project, 0)
                if k == N_DEV - 1:
                    gather.wait_sends(0)
                    gather.finish(1)
                    gather.finish(2)
                    for cp in keep_own:
                        cp.wait()

    vm = pl.BlockSpec(memory_space=pltpu.VMEM)
    hbm = pl.BlockSpec(memory_space=pl.ANY)
    grid_spec = pltpu.PrefetchScalarGridSpec(
        num_scalar_prefetch=1, grid=(nb + N_DEV,),
        in_specs=[pl.BlockSpec((tb, d), lambda i, o: (jnp.minimum(i, nb - 1), 0)), vm, vm, vm, vm],
        out_specs=(pl.BlockSpec((t_len, cols), lambda i, o: (0, o[jnp.maximum(i - nb, 0)])),
                   pl.BlockSpec((d, tb), lambda i, o: (0, jnp.minimum(i, nb - 1))), hbm, hbm, hbm),
        scratch_shapes=[pltpu.VMEM((t_len, d), MM)] + [pltpu.VMEM(s.shape, dt) for s, dt in zip(srcs, dts)]
                       + [pltpu.VMEM(w_in.shape, MM),
                          pltpu.SemaphoreType.DMA((21,)), pltpu.SemaphoreType.DMA((21,)), pltpu.SemaphoreType.DMA((4,))])
    return pl.pallas_call(
        body, name="gather_project", grid_spec=grid_spec,
        out_shape=(jax.ShapeDtypeStruct((t_len, N_DEV * cols), MM),
                   jax.ShapeDtypeStruct((d, t_len), MM))
                  + tuple(jax.ShapeDtypeStruct((N_DEV,) + s.shape, dt) for s, dt in zip(srcs, dts)),
        compiler_params=_params(dimension_semantics=("arbitrary",)),
    )(_block_order(), x, w_in, w_out, conv_pack, ln_g)


def _forward(x, tgt, p, wout, wa_t, wi_t, sp, ones_c, ones_l, tb):
    t_len, d = x.shape
    nb = t_len // tb
    n_tiles, tw = wa_t.shape[0], wa_t.shape[1]
    hd_c, hd_l = d // N_CONV_HEADS, d // N_LRU_HEADS
    s8 = SUBLANES

    def body(x_ref, tgt_ref, p_ref, wout_ref, wa_ref, wi_ref, sp_ref, oc_ref, ol_ref,
             h_ref, dh_ref, dhb_ref, acc_ref,
             yc, u, pa, pi, rcf, rlf, ybuf, tail_z, tail_xl, hcar):
        i = pl.program_id(0)
        row = _row_iota(d)

        @pl.when(i == 0)
        def _():
            tail_z[...] = jnp.zeros_like(tail_z)
            tail_xl[...] = jnp.zeros_like(tail_xl)
            hcar[...] = jnp.zeros_like(hcar)
            acc_ref[...] = jnp.zeros_like(acc_ref)

        def spr(r):
            return sp_ref[r:r + 1, :]

        def proj(rows, seg):
            return p_ref[rows, seg * d:(seg + 1) * d].astype(F32)

        w0, w1, w2 = spr(SP_CONV_W), spr(SP_CONV_W + 1), spr(SP_CONV_W + 2)
        l0, l1, l2, l3 = spr(SP_LRU_W), spr(SP_LRU_W + 1), spr(SP_LRU_W + 2), spr(SP_LRU_W + 3)
        lb = spr(SP_LRU_B)

        def convs(r, carry):
            zp, xp = carry
            rows16 = pl.ds(r, 2 * s8)
            bg16, xl16 = proj(rows16, P_B), proj(rows16, P_XL)
            z16 = proj(rows16, P_C) * proj(rows16, P_XC)
            for j in range(2):
                rows, sub = pl.ds(r + j * s8, s8), slice(j * s8, (j + 1) * s8)
                z, xl = z16[sub], xl16[sub]
                cz = w0 * _shift_down(z, zp, 2, row) + w1 * _shift_down(z, zp, 1, row) + w2 * z
                yc[rows, :] = bg16[sub] * cz
                u[rows, :] = (l0 * _shift_down(xl, xp, 3, row) + l1 * _shift_down(xl, xp, 2, row)
                              + l2 * _shift_down(xl, xp, 1, row) + l3 * xl + lb)
                zp, xp = z, xl
            return zp, xp

        z_last, xl_last = _chunks(tb, 2 * s8, convs, (tail_z[...], tail_xl[...]))
        tail_z[...] = z_last
        tail_xl[...] = xl_last

        ub = u[...].astype(MM)
        for k in range(n_tiles):
            sl = slice(k * tw, (k + 1) * tw)
            pa[:, sl] = _dot(ub[:, sl], wa_ref[k])
            pi[:, sl] = _dot(ub[:, sl], wi_ref[k])
        rcf[...] = _head_rstd(yc[...], oc_ref[...], hd_c)

        c8 = RG_LRU_C * _log_sigmoid(spr(SP_LAM))
        b_a, b_i = spr(SP_B_A), spr(SP_B_I)

        def lru(r, hp):
            rows = pl.ds(r, SUBLANES)
            ra = jax.nn.sigmoid(pa[rows, :] + b_a)
            ii = jax.nn.sigmoid(pi[rows, :] + b_i)
            la = ra * c8
            a = jnp.exp(la)
            mult = jnp.sqrt(_neg_expm1(2.0 * la))
            h = _scan_fwd(a, mult * (ii * u[rows, :]), hp, row)
            h_ref[rows, :] = h
            return _bcast_row(h, SUBLANES - 1)

        hcar[...] = _chunks(tb, SUBLANES, lru, hcar[...])
        rlf[...] = _head_rstd(h_ref[...], ol_ref[...], hd_l)

        g_c, g_l = spr(SP_CONV_G), spr(SP_LRU_G)

        def gate(r, carry):
            rows = pl.ds(r, 2 * s8)
            gc, gl = proj(rows, P_GC), proj(rows, P_GL)
            ybuf[rows, 0:d] = (yc[rows, :] * rcf[rows, :] * g_c * (gc * jax.nn.sigmoid(gc))).astype(MM)
            ybuf[rows, d:2 * d] = (h_ref[rows, :] * rlf[rows, :] * g_l * (gl * jax.nn.sigmoid(gl))).astype(MM)
            return carry

        _chunks(tb, 2 * s8, gate, 0)

        hres = x_ref[...] + _dot(ybuf[...], wout_ref[...])
        rf = lax.rsqrt(jnp.mean(hres * hres, axis=-1, keepdims=True) + RMS_EPS)
        hn = hres * rf
        fg = spr(SP_FINAL_G)
        err = hn * fg - tgt_ref[...]
        dout = err * (1.0 / d)
        acc_ref[0:SUBLANES, :] += (err * err).reshape(tb // SUBLANES, SUBLANES, d).sum(axis=0)
        acc_ref[SUBLANES:2 * SUBLANES, :] += (dout * hn).reshape(tb // SUBLANES, SUBLANES, d).sum(axis=0)
        gd = dout * fg
        dhres = rf * (gd - hn * jnp.mean(gd * hn, axis=-1, keepdims=True))
        dh_ref[...] = dhres
        dhb_ref[...] = dhres.astype(MM)

    vm = pl.BlockSpec(memory_space=pltpu.VMEM)
    blk = lambda w: pl.BlockSpec((tb, w), lambda i: (i, 0))
    buf = pltpu.VMEM((tb, d), F32)
    car = pltpu.VMEM((SUBLANES, d), F32)
    return pl.pallas_call(
        body, name="forward", grid=(nb,),
        in_specs=[blk(d), blk(d), blk(6 * d), vm, vm, vm, vm, vm, vm],
        out_specs=(blk(d), blk(d), blk(d), pl.BlockSpec((2 * SUBLANES, d), lambda i: (0, 0))),
        out_shape=(jax.ShapeDtypeStruct((t_len, d), F32),
                   jax.ShapeDtypeStruct((t_len, d), F32),
                   jax.ShapeDtypeStruct((t_len, d), MM),
                   jax.ShapeDtypeStruct((2 * SUBLANES, d), F32)),
        scratch_shapes=[buf] * 6 + [pltpu.VMEM((tb, 2 * d), MM), car, car, car],
        compiler_params=_params(dimension_semantics=("arbitrary",)),
    )(x, tgt, p, wout, wa_t, wi_t, sp, ones_c, ones_l)


def _backward(p, h, dh, wout, wa_t, wi_t, sp, ones_c, ones_l, tb):
    t_len, d = h.shape
    nb = t_len // tb
    n_tiles, tw = wa_t.shape[0], wa_t.shape[1]
    hd_c, hd_l = d // N_CONV_HEADS, d // N_LRU_HEADS
    s8 = SUBLANES
    ph = 2 * s8

    def body(p_ref, phalo_ref, h_ref, hhalo_ref, dh_ref, wout_ref, wa_ref, wi_ref, sp_ref, oc_ref, ol_ref,
             dp_ref, yt_ref, gwa_ref, gwi_ref, acc_ref,
             hh, dy, ybuf, yc, czs, u, pa, pi, rcf, rlf, qc, ql, dyc_hat, dyl_hat, dpa, dpi, du,
             car_dcz, car_a, car_g, car_du):
        i = pl.program_id(0)
        blk_idx = nb - 1 - i
        row = _row_iota(d)

        @pl.when(i == 0)
        def _():
            for ref in (car_dcz, car_a, car_g, car_du, gwa_ref, gwi_ref, acc_ref):
                ref[...] = jnp.zeros_like(ref)

        def spr(r):
            return sp_ref[r:r + 1, :]

        def proj(rows, seg):
            return p_ref[rows, seg * d:(seg + 1) * d].astype(F32)

        def put(rows, seg, halves):
            dp_ref[rows, seg * d:(seg + 1) * d] = jnp.concatenate(halves, axis=0).astype(MM)

        def acc_add(group, val):
            acc_ref[group * s8:(group + 1) * s8, :] += val

        live = jnp.where(blk_idx > 0, 1.0, 0.0).astype(F32)
        hh[0:s8, :] = hhalo_ref[...] * live
        hh[s8:, :] = h_ref[...]

        dy[...] = _dot_nt(dh_ref[...].astype(MM), wout_ref[...])

        w0, w1, w2 = spr(SP_CONV_W), spr(SP_CONV_W + 1), spr(SP_CONV_W + 2)
        l0, l1, l2, l3 = spr(SP_LRU_W), spr(SP_LRU_W + 1), spr(SP_LRU_W + 2), spr(SP_LRU_W + 3)
        lb = spr(SP_LRU_B)

        def recompute(r, carry):
            zp, xp = carry
            rows16 = pl.ds(r, 2 * s8)
            bg16, xl16 = proj(rows16, P_B), proj(rows16, P_XL)
            z16 = proj(rows16, P_C) * proj(rows16, P_XC)
            for j in range(2):
                rows, sub = pl.ds(r + j * s8, s8), slice(j * s8, (j + 1) * s8)
                z, xl = z16[sub], xl16[sub]
                cz = w0 * _shift_down(z, zp, 2, row) + w1 * _shift_down(z, zp, 1, row) + w2 * z
                czs[rows, :] = cz
                yc[rows, :] = bg16[sub] * cz
                u[rows, :] = (l0 * _shift_down(xl, xp, 3, row) + l1 * _shift_down(xl, xp, 2, row)
                              + l2 * _shift_down(xl, xp, 1, row) + l3 * xl + lb)
                zp, xp = z, xl
            return zp, xp

        halo = lambda seg: phalo_ref[:, seg * d:(seg + 1) * d].astype(F32)[s8:, :] * live
        _chunks(tb, 2 * s8, recompute, (halo(P_C) * halo(P_XC), halo(P_XL)))

        ub = u[...].astype(MM)
        for k in range(n_tiles):
            sl = slice(k * tw, (k + 1) * tw)
            pa[:, sl] = _dot(ub[:, sl], wa_ref[k])
            pi[:, sl] = _dot(ub[:, sl], wi_ref[k])
        rcf[...] = _head_rstd(yc[...], oc_ref[...], hd_c)
        rlf[...] = _head_rstd(h_ref[...], ol_ref[...], hd_l)

        g_c, g_l = spr(SP_CONV_G), spr(SP_LRU_G)

        def gates(r, carry):
            rows = pl.ds(r, 2 * s8)
            for (seg, off_y, src, rstd, gain, q, dhat, grp) in (
                    (P_GC, 0, yc, rcf, g_c, qc, dyc_hat, A_CONV_G),
                    (P_GL, d, h_ref, rlf, g_l, ql, dyl_hat, A_LRU_G)):
                gt = proj(rows, seg)
                sg = jax.nn.sigmoid(gt)
                silu = gt * sg
                yhat = src[rows, :] * rstd[rows, :]
                nrm = yhat * gain
                ybuf[rows, off_y:off_y + d] = nrm * silu
                dout = dy[rows, off_y:off_y + d]
                dnrm = dout * silu
                dp_ref[rows, seg * d:(seg + 1) * d] = (dout * nrm * (sg * (1.0 + gt * (1.0 - sg)))).astype(MM)
                dg = dnrm * yhat
                acc_add(grp, dg[0:s8] + dg[s8:])
                dh_ = dnrm * gain
                dhat[rows, :] = dh_
                q[rows, :] = dh_ * yhat
            return carry

        _chunks(tb, 2 * s8, gates, 0)

        qc[...] = _head_sums(qc[...], oc_ref[...]) * (1.0 / hd_c)
        ql[...] = _head_sums(ql[...], ol_ref[...]) * (1.0 / hd_l)
        yt_ref[...] = ybuf[...].T.astype(MM)

        c8 = RG_LRU_C * _log_sigmoid(spr(SP_LAM))
        b_a, b_i = spr(SP_B_A), spr(SP_B_I)

        def mixers(r, carry):
            dcz_n, a_n, g_n = carry
            rows16 = pl.ds(r, 2 * s8)
            bg16, cg16, xc16 = proj(rows16, P_B), proj(rows16, P_C), proj(rows16, P_XC)
            z16 = cg16 * xc16
            d_b, d_c, d_x = [None, None], [None, None], [None, None]
            for j in (1, 0):
                rows, sub = pl.ds(r + j * s8, s8), slice(j * s8, (j + 1) * s8)
                rstd = rcf[rows, :]
                yhat = yc[rows, :] * rstd
                dyc = rstd * (dyc_hat[rows, :] - yhat * qc[rows, :])
                d_b[j] = dyc * czs[rows, :]
                dcz = dyc * bg16[sub]
                up1, up2 = _shift_up(dcz, dcz_n, 1, row), _shift_up(dcz, dcz_n, 2, row)
                dz = w2 * dcz + w1 * up1 + w0 * up2
                d_c[j] = dz * xc16[sub]
                d_x[j] = dz * cg16[sub]
                z = z16[sub]
                acc_add(A_CONV_W, up2 * z)
                acc_add(A_CONV_W + 1, up1 * z)
                acc_add(A_CONV_W + 2, dcz * z)
                rstd = rlf[rows, :]
                hcur = hh[pl.ds(r + (j + 1) * s8, s8), :]
                hhat = hcur * rstd
                dh_out = rstd * (dyl_hat[rows, :] - hhat * ql[rows, :])
                ra = jax.nn.sigmoid(pa[rows, :] + b_a)
                la = ra * c8
                a = jnp.exp(la)
                g = _scan_bwd(_shift_up(a, a_n, 1, row), dh_out, g_n, row)
                da = g * _shift_down(hcur, hh[pl.ds(r + j * s8, s8), :], 1, row)
                ii = jax.nn.sigmoid(pi[rows, :] + b_i)
                uu = u[rows, :]
                mult = jnp.sqrt(_neg_expm1(2.0 * la))
                dmult = g * (ii * uu)
                ds = g * mult
                dla = a * (da - dmult * a / mult)
                acc_add(A_LAM, dla * ra)
                dpa_ = dla * c8 * ra * (1.0 - ra)
                dpi_ = ds * uu * ii * (1.0 - ii)
                acc_add(A_B_A, dpa_)
                acc_add(A_B_I, dpi_)
                dpa[rows, :] = dpa_
                dpi[rows, :] = dpi_
                du[rows, :] = ds * ii
                dcz_n, a_n, g_n = dcz, a, _bcast_row(g, 0)
            put(rows16, P_B, d_b)
            put(rows16, P_C, d_c)
            put(rows16, P_XC, d_x)
            return dcz_n, a_n, g_n

        dcz_f, a_f, g_f = _chunks(tb, 2 * s8, mixers, (car_dcz[...], car_a[...], car_g[...]), reverse=True)
        car_dcz[...] = dcz_f
        car_a[...] = a_f
        car_g[...] = g_f

        dpab = dpa[...].astype(MM)
        dpib = dpi[...].astype(MM)
        for k in range(n_tiles):
            sl = slice(k * tw, (k + 1) * tw)
            du[:, sl] += _dot_nt(dpab[:, sl], wa_ref[k]) + _dot_nt(dpib[:, sl], wi_ref[k])
            ut = u[:, sl].T.astype(MM)
            gwa_ref[k] += _dot(ut, dpab[:, sl])
            gwi_ref[k] += _dot(ut, dpib[:, sl])

        def lru_conv(r, du_n):
            rows16 = pl.ds(r, 2 * s8)
            xl16 = proj(rows16, P_XL)
            d_xl = [None, None]
            for j in (1, 0):
                rows, sub = pl.ds(r + j * s8, s8), slice(j * s8, (j + 1) * s8)
                dut = du[rows, :]
                up1, up2, up3 = (_shift_up(dut, du_n, s, row) for s in (1, 2, 3))
                d_xl[j] = l3 * dut + l2 * up1 + l1 * up2 + l0 * up3
                xl = xl16[sub]
                acc_add(A_LRU_W, up3 * xl)
                acc_add(A_LRU_W + 1, up2 * xl)
                acc_add(A_LRU_W + 2, up1 * xl)
                acc_add(A_LRU_W + 3, dut * xl)
                acc_add(A_LRU_B, dut)
                du_n = dut
            put(rows16, P_XL, d_xl)
            return du_n

        car_du[...] = _chunks(tb, 2 * s8, lru_conv, car_du[...], reverse=True)

    vm = pl.BlockSpec(memory_space=pltpu.VMEM)
    rev = lambda w: pl.BlockSpec((tb, w), lambda i: (nb - 1 - i, 0))
    halo = lambda rows, w: pl.BlockSpec((rows, w), lambda i: (jnp.maximum((nb - 1 - i) * (tb // rows) - 1, 0), 0))
    const = lambda shape: pl.BlockSpec(shape, lambda i: (0,) * len(shape))
    buf = lambda w: pltpu.VMEM((tb, w), F32)
    car = pltpu.VMEM((SUBLANES, d), F32)
    return pl.pallas_call(
        body, name="backward", grid=(nb,),
        in_specs=[rev(6 * d), halo(ph, 6 * d), rev(d), halo(SUBLANES, d), rev(d), vm, vm, vm, vm, vm, vm],
        out_specs=(rev(6 * d), pl.BlockSpec((2 * d, tb), lambda i: (0, nb - 1 - i)),
                   const((n_tiles, tw, tw)), const((n_tiles, tw, tw)), const((A_GROUPS * SUBLANES, d))),
        out_shape=(jax.ShapeDtypeStruct((t_len, 6 * d), MM),
                   jax.ShapeDtypeStruct((2 * d, t_len), MM),
                   jax.ShapeDtypeStruct((n_tiles, tw, tw), F32),
                   jax.ShapeDtypeStruct((n_tiles, tw, tw), F32),
                   jax.ShapeDtypeStruct((A_GROUPS * SUBLANES, d), F32)),
        scratch_shapes=[pltpu.VMEM((SUBLANES + tb, d), F32), buf(2 * d), buf(2 * d)] + [buf(d)] * 14 + [car, car, car, car],
        compiler_params=_params(dimension_semantics=("arbitrary",)),
    )(p, p, h, h, dh, wout, wa_t, wi_t, sp, ones_c, ones_l)


def _input_grad(dp, win_all, x, dh, sp, parts, tb):
    t_len, d = x.shape
    nb = t_len // tb
    cols = win_all.shape[2]
    n_parts = len(parts)

    def body(dp_ref, win_ref, x_ref, dh_ref, sp_ref, *refs):
        gx_ref, acc_ref = refs[n_parts:n_parts + 2]
        exchange = _ChipExchange(refs[:n_parts], refs[n_parts + 2:2 * n_parts + 2], *refs[2 * n_parts + 2:])
        i = pl.program_id(0)

        @pl.when(i == 0)
        def _():
            acc_ref[...] = jnp.zeros_like(acc_ref)
            exchange.start()

        dxn = _dot_nt(dp_ref[:, 0:cols], win_ref[0])
        for j in range(1, N_DEV):
            dxn += _dot_nt(dp_ref[:, j * cols:(j + 1) * cols], win_ref[j])
        xv = x_ref[...]
        r0 = lax.rsqrt(jnp.mean(xv * xv, axis=-1, keepdims=True) + RMS_EPS)
        xhat = xv * r0
        acc_ref[...] += (dxn * xhat).reshape(tb // SUBLANES, SUBLANES, d).sum(axis=0)
        dxh = dxn * sp_ref[SP_LN_G:SP_LN_G + 1, :]
        gx_ref[...] = dh_ref[...] + r0 * (dxh - xhat * jnp.mean(dxh * xhat, axis=-1, keepdims=True))

        @pl.when(i == nb - 1)
        def _():
            exchange.finish()

    vm = pl.BlockSpec(memory_space=pltpu.VMEM)
    hbm = pl.BlockSpec(memory_space=pl.ANY)
    blk = lambda w: pl.BlockSpec((tb, w), lambda i: (i, 0))
    outs = pl.pallas_call(
        body, name="input_grad", grid=(nb,),
        in_specs=[blk(6 * d), vm, blk(d), blk(d), vm] + [hbm] * n_parts,
        out_specs=(blk(d), pl.BlockSpec((SUBLANES, d), lambda i: (0, 0))) + (hbm,) * n_parts,
        out_shape=(jax.ShapeDtypeStruct((t_len, d), F32), jax.ShapeDtypeStruct((SUBLANES, d), F32))
                  + tuple(jax.ShapeDtypeStruct(p.shape, p.dtype) for p in parts),
        scratch_shapes=[pltpu.SemaphoreType.DMA((3 * n_parts,)), pltpu.SemaphoreType.DMA((3 * n_parts,))],
        compiler_params=_params(dimension_semantics=("arbitrary",)),
    )(dp, win_all, x, dh, sp, *parts)
    return outs[0], outs[1], outs[2:]


_CHIP_RELATIONS = [(0, 0), (1, 0), (0, 1), (1, 1)]


def _related_block(k, core):
    x, y, _ = _mesh_pos()
    fx, fy = _CHIP_RELATIONS[k]
    return 4 * (x ^ fx) + 2 * (y ^ fy) + core


class _ChipExchange:
    def __init__(self, part_refs, land_refs, send_sems, recv_sems):
        self.part_refs, self.land_refs, self.send_sems, self.recv_sems = part_refs, land_refs, send_sems, recv_sems

    def copies(self):
        x, y, c = _mesh_pos()
        for a in range(len(self.part_refs)):
            for k in (1, 2, 3):
                fx, fy = _CHIP_RELATIONS[k]
                yield pltpu.make_async_remote_copy(
                    src_ref=self.part_refs[a].at[k - 1], dst_ref=self.land_refs[a].at[k - 1],
                    send_sem=self.send_sems.at[3 * a + k - 1], recv_sem=self.recv_sems.at[3 * a + k - 1],
                    device_id=(x ^ fx, y ^ fy, c), device_id_type=MESH)

    def start(self):
        for cp in self.copies():
            cp.start()

    def finish(self):
        for cp in self.copies():
            cp.wait_recv()
        for cp in self.copies():
            cp.wait_send()


def _weight_grad_stage1(name, blk_shape, n_split, operands, in_specs, product, riders=()):
    n_rows, n_cols = blk_shape
    rs = n_rows // n_split
    rc = 32
    n_in, n_ride = len(operands), len(riders)
    _, _, c = _mesh_pos()
    order = jnp.stack([_related_block(k, 1 - c) for k in range(4)]
                      + [_related_block(k, c) for k in (1, 2, 3, 0)]).astype(jnp.int32)

    def body(order_ref, *refs):
        ins = refs[:n_in]
        ride_in = refs[n_in:n_in + n_ride]
        part_ref, own_ref, from_sib = refs[n_in + n_ride:n_in + n_ride + 3]
        ride_out = refs[n_in + n_ride + 3:n_in + 2 * n_ride + 3]
        gbuf, sendbuf, tmp, send_sems, recv_sems, local_sems, ride_send, ride_recv = refs[n_in + 2 * n_ride + 3:]
        exchange = _ChipExchange(ride_in, ride_out, ride_send, ride_recv)
        s = pl.program_id(0)
        x, y, c = _mesh_pos()

        def to_sibling(k):
            return pltpu.make_async_remote_copy(
                src_ref=sendbuf.at[k], dst_ref=from_sib.at[k], send_sem=send_sems.at[k], recv_sem=recv_sems.at[k],
                device_id=(x, y, 1 - c), device_id_type=MESH)

        if n_ride:
            @pl.when(s == 0)
            def _():
                exchange.start()

        for h in range(n_split):
            gbuf[h * rs:(h + 1) * rs, :] = product(ins, h)

        @pl.when(s < 4)
        def _():
            def narrow(r, carry):
                sendbuf[s, pl.ds(r, rc), :] = gbuf[pl.ds(r, rc), :].astype(MM)
                return carry

            _chunks(n_rows, rc, narrow, 0)
            to_sibling(s).start()

        @pl.when(s >= 4)
        def _():
            k = jnp.where(s == 7, 0, s - 3)
            to_sibling(k).wait_recv()
            load = pltpu.make_async_copy(from_sib.at[k], tmp, local_sems.at[0])
            load.start()
            load.wait()

            @pl.when(s < 7)
            def _():
                def add(r, carry):
                    rows = pl.ds(r, rc)
                    part_ref[0, rows, :] = (gbuf[rows, :] + tmp[rows, :].astype(F32)).astype(MM)
                    return carry

                _chunks(n_rows, rc, add, 0)

            @pl.when(s == 7)
            def _():
                def add(r, carry):
                    rows = pl.ds(r, rc)
                    own_ref[rows, :] = gbuf[rows, :] + tmp[rows, :].astype(F32)
                    return carry

                _chunks(n_rows, rc, add, 0)
                for kk in range(4):
                    to_sibling(kk).wait_send()
                if n_ride:
                    exchange.finish()

    hbm = pl.BlockSpec(memory_space=pl.ANY)
    grid_spec = pltpu.PrefetchScalarGridSpec(
        num_scalar_prefetch=1, grid=(N_DEV,), in_specs=list(in_specs) + [hbm] * n_ride,
        out_specs=(pl.BlockSpec((1, n_rows, n_cols), lambda s, o: (jnp.clip(s - 4, 0, 2), 0, 0)),
                   pl.BlockSpec((n_rows, n_cols), lambda s, o: (0, 0)), hbm) + (hbm,) * n_ride,
        scratch_shapes=[pltpu.VMEM((n_rows, n_cols), F32), pltpu.VMEM((4, n_rows, n_cols), MM),
                        pltpu.VMEM((n_rows, n_cols), MM),
                        pltpu.SemaphoreType.DMA((4,)), pltpu.SemaphoreType.DMA((4,)), pltpu.SemaphoreType.DMA((1,)),
                        pltpu.SemaphoreType.DMA((max(3 * n_ride, 1),)), pltpu.SemaphoreType.DMA((max(3 * n_ride, 1),))])
    outs = pl.pallas_call(
        body, name=name, grid_spec=grid_spec,
        out_shape=(jax.ShapeDtypeStruct((3, n_rows, n_cols), MM), jax.ShapeDtypeStruct((n_rows, n_cols), F32),
                   jax.ShapeDtypeStruct((4, n_rows, n_cols), MM))
                  + tuple(jax.ShapeDtypeStruct(p.shape, p.dtype) for p in riders),
        compiler_params=_params(dimension_semantics=("arbitrary",)),
    )(order, *operands, *riders)
    return outs[0], outs[1], outs[3:]


def _weight_grad_in(xnt, dp, riders):
    d, t_len = xnt.shape
    cols = dp.shape[1] // N_DEV
    half = d // 2
    return _weight_grad_stage1(
        "weight_grad_in", (d, cols), 2, (xnt, dp),
        [pl.BlockSpec(memory_space=pltpu.VMEM), pl.BlockSpec((t_len, cols), lambda s, o: (0, o[s]))],
        lambda refs, h: _dot(refs[0][h * half:(h + 1) * half, :], refs[1][...]), riders)


def _weight_grad_out(yt, dhb):
    d2, t_len = yt.shape
    d = dhb.shape[1]
    rows = d2 // N_DEV
    return _weight_grad_stage1(
        "weight_grad_out", (rows, d), 1, (yt, dhb),
        [pl.BlockSpec((rows, t_len), lambda s, o: (o[s], 0)), pl.BlockSpec(memory_space=pltpu.VMEM)],
        lambda refs, h: _dot(refs[0][...], refs[1][...]))


def _update_shard(own, from_chips, w, m, v, name):
    n_rows, n_cols = w.shape
    rb = min(256, n_rows)

    def body(own_ref, fc_ref, w_ref, m_ref, v_ref, grad_ref, delta_ref, mo_ref, vo_ref):
        g = own_ref[...]
        for k in range(3):
            g = g + fc_ref[k].astype(F32)
        delta, m_new, v_new = _adamw(w_ref[...], g, m_ref[...], v_ref[...])
        grad_ref[...] = g
        delta_ref[...] = delta
        mo_ref[...] = m_new
        vo_ref[...] = v_new

    blk = pl.BlockSpec((rb, n_cols), lambda i: (i, 0))
    out = jax.ShapeDtypeStruct((n_rows, n_cols), F32)
    return pl.pallas_call(
        body, name=name, grid=(n_rows // rb,),
        in_specs=[blk, pl.BlockSpec((3, rb, n_cols), lambda i: (0, i, 0)), blk, blk, blk],
        out_specs=(blk, blk, blk, blk), out_shape=(out, out, out, out),
        compiler_params=_params(dimension_semantics=("arbitrary",)),
    )(own, from_chips, w, m, v)


def _small_update(facc, xacc, bacc, gwa, gwi, lam, wsl, msl, vsl, cw, cm, cv):
    n_rows, d = wsl.shape
    s8 = SUBLANES
    cc = cw.shape[1]

    def body(facc_ref, xacc_ref, bacc_ref, gwa_ref, gwi_ref, lam_ref, w_ref, m_ref, v_ref, cw_ref, cm_ref, cv_ref,
             g_o, d_o, m_o, v_o, cg_o, cd_o, cm_o, cv_o, gat, send_sems, recv_sems):
        x, y, c = _mesh_pos()
        me = 4 * x + 2 * y + c

        def rowsum(ref, group):
            return jnp.sum(ref[group * s8:(group + 1) * s8, :], axis=0, keepdims=True)

        mine = gat.at[me]
        loss = jnp.sum(rowsum(facc_ref, 0), axis=1, keepdims=True) * (0.5 / d)
        mine[SL_LOSS:SL_LOSS + 1, :] = jnp.broadcast_to(loss, (1, d))
        mine[SL_LN_G:SL_LN_G + 1, :] = rowsum(xacc_ref, 0)
        mine[SL_LRU_B:SL_LRU_B + 1, :] = rowsum(bacc_ref, A_LRU_B)
        mine[SL_B_A:SL_B_A + 1, :] = rowsum(bacc_ref, A_B_A)
        mine[SL_B_I:SL_B_I + 1, :] = rowsum(bacc_ref, A_B_I)
        mine[SL_LAM:SL_LAM + 1, :] = rowsum(bacc_ref, A_LAM) * (RG_LRU_C * jax.nn.sigmoid(-lam_ref[...]))
        mine[SL_CONV_G:SL_CONV_G + 1, :] = rowsum(bacc_ref, A_CONV_G)
        mine[SL_LRU_G:SL_LRU_G + 1, :] = rowsum(bacc_ref, A_LRU_G)
        mine[SL_FINAL_G:SL_FINAL_G + 1, :] = rowsum(facc_ref, 1)
        for k in range(3):
            mine[SL_CONV_W + k:SL_CONV_W + k + 1, :] = rowsum(bacc_ref, A_CONV_W + k)
        for k in range(4):
            mine[SL_LRU_W + k:SL_LRU_W + k + 1, :] = rowsum(bacc_ref, A_LRU_W + k)
        na = gwa_ref.shape[0]
        mine[SL_W_A:SL_W_A + na, :] = gwa_ref[...]
        mine[SL_W_A + na:SL_W_A + 2 * na, :] = gwi_ref[...]

        gather = _Gather(lambda a, px, py, pc: gat.at[4 * px + 2 * py + pc], send_sems, recv_sems)
        gather.start_own(0)
        gather.finish(0)

        def update(r, carry):
            rows = pl.ds(r, s8)
            g = gat[0, rows, :]
            for b in range(1, N_DEV):
                g = g + gat[b, rows, :]
            delta, m_new, v_new = _adamw(w_ref[rows, :], g, m_ref[rows, :], v_ref[rows, :])
            g_o[rows, :] = g
            d_o[rows, :] = delta
            m_o[rows, :] = m_new
            v_o[rows, :] = v_new
            return carry

        _chunks(n_rows, s8, update, 0)
        gc = g_o[s8:2 * s8, pl.ds(pl.multiple_of(me * cc, cc), cc)]
        delta, m_new, v_new = _adamw(cw_ref[...], gc, cm_ref[...], cv_ref[...])
        cg_o[...] = gc
        cd_o[...] = delta
        cm_o[...] = m_new
        cv_o[...] = v_new

    vm = pl.BlockSpec(memory_space=pltpu.VMEM)
    big = jax.ShapeDtypeStruct((n_rows, d), F32)
    small = jax.ShapeDtypeStruct(cw.shape, F32)
    return pl.pallas_call(
        body, name="small_update",
        in_specs=[vm] * 12, out_specs=(vm,) * 8, out_shape=(big, big, big, big, small, small, small, small),
        scratch_shapes=[pltpu.VMEM((N_DEV, n_rows, d), F32), pltpu.SemaphoreType.DMA((7,)), pltpu.SemaphoreType.DMA((7,))],
        compiler_params=_params(),
    )(facc, xacc, bacc, gwa, gwi, lam, wsl, msl, vsl, cw, cm, cv)


def _head_ones(head_dim, tw):
    lane = jnp.arange(tw) // head_dim
    return (lane[:, None] == lane[None, :]).astype(MM)


def _gate_tiles(w, tw):
    n_heads, hd, _ = w.shape
    per = tw // hd
    w4 = w.reshape(n_heads // per, per, hd, hd)
    eye = jnp.eye(per, dtype=w.dtype)
    return (w4[:, :, :, None, :] * eye[None, :, None, :, None]).reshape(n_heads // per, tw, tw)


def _gate_blocks(tiles, n_heads, hd):
    n_tiles, tw, _ = tiles.shape
    per = tw // hd
    t5 = tiles.reshape(n_tiles, per, hd, per, hd)
    diag = jnp.stack([t5[:, a, :, a, :] for a in range(per)], axis=1)
    return diag.reshape(hd, n_heads * hd)


def kernel(x, ln_g, w_in, conv_w, lru_conv_w, lru_conv_b, w_a, b_a, w_i, b_i, lam, conv_out_g, lru_out_g, w_out, final_g, loss_target, m_ln_g, m_w_in, m_conv_w, m_lru_conv_w, m_lru_conv_b, m_w_a, m_b_a, m_w_i, m_b_i, m_lam, m_conv_out_g, m_lru_out_g, m_w_out, m_final_g, v_ln_g, v_w_in, v_conv_w, v_lru_conv_w, v_lru_conv_b, v_w_a, v_b_a, v_w_i, v_b_i, v_lam, v_conv_out_g, v_lru_out_g, v_w_out, v_final_g):
    _, t_len, d = x.shape
    hd_l = d // N_LRU_HEADS
    tw = min(MXU_TILE, d)
    cc = conv_w.shape[1]
    x2, tgt2 = x[0], loss_target[0]

    def conv_rows(cw3, lw4):
        return jnp.concatenate([jnp.zeros((1, cc), F32), cw3, lw4], axis=0)

    p, xnt, win_all, wout_all, conv_all = _gather_project(
        x2, w_in, w_out, conv_rows(conv_w, lru_conv_w), ln_g.reshape(1, d), min(256, t_len))
    wout_full = wout_all.reshape(N_DEV * w_out.shape[0], d)
    conv_full = conv_all.transpose(1, 0, 2).reshape(SUBLANES, d)
    small = [ln_g, lru_conv_b, b_a, b_i, lam, conv_out_g, lru_out_g, final_g]
    sp = jnp.concatenate([jnp.stack(small), conv_full[1:], jnp.zeros((1, d), F32)], axis=0)
    wa_t, wi_t = _gate_tiles(w_a, tw).astype(MM), _gate_tiles(w_i, tw).astype(MM)
    ones_c, ones_l = _head_ones(d // N_CONV_HEADS, tw), _head_ones(hd_l, tw)

    h, dh, dhb, facc = _forward(x2, tgt2, p, wout_full, wa_t, wi_t, sp, ones_c, ones_l, min(256, t_len))
    dp, yt, gwa_t, gwi_t, bacc = _backward(p, h, dh, wout_full, wa_t, wi_t, sp, ones_c, ones_l, min(256, t_len))
    part_out, own_out, _ = _weight_grad_out(yt, dhb)
    part_in, own_in, (chips_out,) = _weight_grad_in(xnt, dp, (part_out,))
    grad_x, xacc, (chips_in,) = _input_grad(dp, win_all, x2, dh, sp, (part_in,), min(512, t_len))
    gw_in, dw_in, mw_in, vw_in = _update_shard(own_in, chips_in, w_in, m_w_in, v_w_in, "update_w_in")
    gw_out, dw_out, mw_out, vw_out = _update_shard(own_out, chips_out, w_out, m_w_out, v_w_out, "update_w_out")

    def slab(parts, wa_, wi_):
        return jnp.concatenate([jnp.zeros((1, d), F32), jnp.stack(parts), jnp.zeros((SL_W_A - SL_CONV_W, d), F32),
                                wa_.reshape(-1, d), wi_.reshape(-1, d)], axis=0)

    wsl = slab(small, w_a, w_i)
    msl = slab([m_ln_g, m_lru_conv_b, m_b_a, m_b_i, m_lam, m_conv_out_g, m_lru_out_g, m_final_g], m_w_a, m_w_i)
    vsl = slab([v_ln_g, v_lru_conv_b, v_b_a, v_b_i, v_lam, v_conv_out_g, v_lru_out_g, v_final_g], v_w_a, v_w_i)
    outs = _small_update(
        facc, xacc, bacc, _gate_blocks(gwa_t, N_LRU_HEADS, hd_l), _gate_blocks(gwi_t, N_LRU_HEADS, hd_l),
        lam.reshape(1, d), wsl, msl, vsl,
        conv_rows(conv_w, lru_conv_w), conv_rows(m_conv_w, m_lru_conv_w), conv_rows(v_conv_w, v_lru_conv_w))
    sl_g, sl_d, sl_m, sl_v, c_g, c_d, c_m, c_v = outs
    loss = sl_g[SL_LOSS, 0]
    na = w_a.size // d

    def unpack(sl, cv, big_in, big_out):
        one = lambda r: sl[r]
        return [one(SL_LN_G), big_in, cv[1:4], cv[4:8], one(SL_LRU_B),
                sl[SL_W_A:SL_W_A + na].reshape(w_a.shape), one(SL_B_A),
                sl[SL_W_A + na:SL_W_A + 2 * na].reshape(w_i.shape), one(SL_B_I), one(SL_LAM),
                one(SL_CONV_G), one(SL_LRU_G), big_out, one(SL_FINAL_G)]

    return (loss, grad_x[None], *unpack(sl_g, c_g, gw_in, gw_out), *unpack(sl_d, c_d, dw_in, dw_out),
            *unpack(sl_m, c_m, mw_in, mw_out), *unpack(sl_v, c_v, vw_in, vw_out))
```

```python
import functools

import jax
import jax.numpy as jnp
from jax import lax
from jax.experimental import pallas as pl
from jax.experimental.pallas import tpu as pltpu

F32 = jnp.float32
MM = jnp.bfloat16
MESH = pl.DeviceIdType.MESH

N_DEV = 8
N_CONV_HEADS = 8
N_LRU_HEADS = 16
RG_LRU_C = 8.0
RMS_EPS = 1e-6
ADAM_LR, ADAM_B1, ADAM_B2, ADAM_EPS, ADAM_WD, ADAM_STEP = 0.001, 0.9, 0.999, 1e-08, 0.01, 10
ADAM_BC1 = 1.0 - ADAM_B1 ** ADAM_STEP
ADAM_BC2 = 1.0 - ADAM_B2 ** ADAM_STEP

SUBLANES = 8
LANES = 128
MXU_TILE = 256
VMEM_LIMIT = 56 * 1024 * 1024

SP_LN_G, SP_LRU_B, SP_B_A, SP_B_I, SP_LAM, SP_CONV_G, SP_LRU_G, SP_FINAL_G, SP_CONV_W, SP_LRU_W = 0, 1, 2, 3, 4, 5, 6, 7, 8, 11
SP_ROWS = 16
P_B, P_C, P_XC, P_GC, P_XL, P_GL = 0, 1, 2, 3, 4, 5
A_CONV_G, A_LRU_G, A_LAM, A_B_A, A_B_I, A_CONV_W, A_LRU_W, A_LRU_B = 0, 1, 2, 3, 4, 5, 8, 12
A_GROUPS = 13
SL_LOSS, SL_LN_G, SL_LRU_B, SL_B_A, SL_B_I, SL_LAM, SL_CONV_G, SL_LRU_G, SL_FINAL_G, SL_CONV_W, SL_LRU_W, SL_W_A = 0, 1, 2, 3, 4, 5, 6, 7, 8, 9, 12, 16


def _params(vmem=True, **kw):
    if vmem:
        kw["vmem_limit_bytes"] = VMEM_LIMIT
    return pltpu.CompilerParams(**kw)


def _dot(a, b):
    return jnp.dot(a, b, preferred_element_type=F32)


def _dot_nt(a, b):
    return lax.dot_general(a, b, (((1,), (1,)), ((), ())), preferred_element_type=F32)


def _head_sums(v, ones_tile):
    tw = ones_tile.shape[0]
    hi = v.astype(MM)
    lo = (v - hi.astype(F32)).astype(MM)
    return jnp.concatenate(
        [_dot(hi[:, k:k + tw], ones_tile) + _dot(lo[:, k:k + tw], ones_tile) for k in range(0, v.shape[1], tw)], axis=1)


def _head_rstd(v, ones_tile, head_dim):
    return lax.rsqrt(_head_sums(v * v, ones_tile) * (1.0 / head_dim) + RMS_EPS)


def _neg_expm1(x):
    series = -x * (1.0 + x * (0.5 + x * (1.0 / 6.0 + x * (1.0 / 24.0 + x * (1.0 / 120.0)))))
    return jnp.where(x > -0.03, series, 1.0 - jnp.exp(x))


def _log_sigmoid(x):
    z = jnp.exp(-jnp.abs(x))
    u = 1.0 + z
    log1p_z = jnp.where(u == 1.0, z, jnp.log(u) * (z / (u - 1.0)))
    return jnp.minimum(x, 0.0) - log1p_z


def _row_iota(d):
    return lax.broadcasted_iota(jnp.int32, (SUBLANES, d), 0)


def _shift_down(cur, prev, s, row):
    return jnp.where(row >= s, pltpu.roll(cur, s, axis=0), pltpu.roll(prev, s, axis=0))


def _shift_up(cur, nxt, s, row):
    k = SUBLANES - s
    return jnp.where(row < k, pltpu.roll(cur, k, axis=0), pltpu.roll(nxt, k, axis=0))


def _scan_fwd(a, b, h_prev, row):
    for s in (1, 2, 4):
        a_s = jnp.where(row >= s, pltpu.roll(a, s, axis=0), 1.0)
        b_s = jnp.where(row >= s, pltpu.roll(b, s, axis=0), 0.0)
        b = a * b_s + b
        a = a * a_s
    return a * h_prev + b


def _scan_bwd(a_next, b, g_next, row):
    a = a_next
    for s in (1, 2, 4):
        k = SUBLANES - s
        a_s = jnp.where(row < k, pltpu.roll(a, k, axis=0), 1.0)
        b_s = jnp.where(row < k, pltpu.roll(b, k, axis=0), 0.0)
        b = a * b_s + b
        a = a * a_s
    return a * g_next + b


def _bcast_row(v, r):
    return jnp.broadcast_to(v[r:r + 1, :], v.shape)


def _chunks(n_rows, rc, body, init, reverse=False):
    n = n_rows // rc

    def step(i, carry):
        j = (n - 1 - i) if reverse else i
        return body(pl.multiple_of(j * rc, rc), carry)

    return lax.fori_loop(0, n, step, init)


def _adamw(w, g, m, v):
    m = ADAM_B1 * m + (1.0 - ADAM_B1) * g
    v = ADAM_B2 * v + (1.0 - ADAM_B2) * (g * g)
    m_hat = m / ADAM_BC1
    v_hat = v / ADAM_BC2
    delta = -ADAM_LR * (m_hat / (jnp.sqrt(v_hat) + ADAM_EPS) + ADAM_WD * w)
    return delta, m, v


def _mesh_pos():
    return lax.axis_index("x"), lax.axis_index("y"), lax.axis_index("c")


class _Gather:
    def __init__(self, blocks_of, send_sems, recv_sems, own_src=None):
        x, y, c = _mesh_pos()
        self.c = c
        self.me, self.sibling = (x, y, c), (x, y, 1 - c)
        self.chips = [(1 - x, y), (x, 1 - y), (1 - x, 1 - y)]
        self.blocks_of, self.send_sems, self.recv_sems = blocks_of, send_sems, recv_sems
        self.own_src = own_src

    def copy(self, a, k, block, to):
        src = self.blocks_of(a, *block)
        if block is self.me and self.own_src is not None:
            src = self.own_src[a]
        return pltpu.make_async_remote_copy(
            src_ref=src, dst_ref=self.blocks_of(a, *block),
            send_sem=self.send_sems.at[a * 7 + k], recv_sem=self.recv_sems.at[a * 7 + k],
            device_id=to, device_id_type=MESH)

    def start_own(self, a):
        self.copy(a, 0, self.me, self.sibling).start()
        for j, chip in enumerate(self.chips):
            self.copy(a, 1 + j, self.me, (*chip, self.c)).start()

    def wait_sibling(self, a):
        self.copy(a, 0, self.sibling, self.me).wait_recv()

    def wait_chip_and_pass_on(self, a, j):
        block = (*self.chips[j], self.c)
        self.copy(a, 1 + j, block, self.me).wait_recv()
        self.copy(a, 4 + j, block, self.sibling).start()

    def wait_passed_on(self, a, j):
        self.copy(a, 4 + j, (*self.chips[j], 1 - self.c), self.me).wait_recv()

    def wait_sends(self, a):
        self.copy(a, 0, self.me, self.sibling).wait_send()
        for j, chip in enumerate(self.chips):
            self.copy(a, 1 + j, self.me, (*chip, self.c)).wait_send()
            self.copy(a, 4 + j, (*chip, self.c), self.sibling).wait_send()

    def finish(self, a):
        for j in range(3):
            self.wait_chip_and_pass_on(a, j)
        self.wait_sibling(a)
        for j in range(3):
            self.wait_passed_on(a, j)
        self.wait_sends(a)


class _BalancedGather:
    def __init__(self, slot, send_sems, recv_sems, own_src):
        x, y, c = _mesh_pos()
        self.c = c
        self.me, self.sibling = (x, y, c), (x, y, 1 - c)
        self.chips = [(1 - x, y), (x, 1 - y), (1 - x, 1 - y)]
        self.slot, self.send_sems, self.recv_sems, self.own_src = slot, send_sems, recv_sems, own_src

    def half(self, a, block, which):
        ref = self.slot(a, *block)
        n = ref.shape[0] // 2
        return ref.at[pl.ds(which * n, n)]

    def copy(self, a, k, src, dst, to):
        return pltpu.make_async_remote_copy(
            src_ref=src, dst_ref=dst, send_sem=self.send_sems.at[a * 8 + k], recv_sem=self.recv_sems.at[a * 8 + k],
            device_id=to, device_id_type=MESH)

    def whole(self, a, k, block, to):
        src = self.own_src[a] if block is self.me else self.slot(a, *block)
        return self.copy(a, k, src, self.slot(a, *block), to)

    def halved(self, a, k, block, which, to):
        return self.copy(a, k, self.half(a, block, which), self.half(a, block, which), to)

    def on(self, chip):
        return (*self.chips[chip], self.c)

    def start_own(self, a):
        self.whole(a, 0, self.me, self.sibling).start()
        self.whole(a, 1, self.me, self.on(0)).start()
        self.whole(a, 2, self.me, self.on(1)).start()

    def wait_sibling(self, a):
        self.whole(a, 0, self.sibling, self.me).wait_recv()

    def on_neighbour(self, a, j):
        self.whole(a, 1 + j, self.on(j), self.me).wait_recv()
        self.halved(a, 3 + j, self.on(j), j, self.on(1 - j)).start()
        self.whole(a, 5 + j, self.on(j), self.sibling).start()

    def on_diagonal(self, a):
        self.halved(a, 3, self.on(2), 0, self.me).wait_recv()
        self.halved(a, 4, self.on(2), 1, self.me).wait_recv()
        self.whole(a, 7, self.on(2), self.sibling).start()

    def wait_passed_on(self, a, j):
        self.whole(a, 5 + j, (*self.chips[j], 1 - self.c), self.me).wait_recv()

    def wait_sends(self, a):
        self.whole(a, 0, self.me, self.sibling).wait_send()
        for j in range(2):
            self.whole(a, 1 + j, self.me, self.on(j)).wait_send()
            self.halved(a, 3 + j, self.on(j), j, self.on(1 - j)).wait_send()
        for j in range(3):
            self.whole(a, 5 + j, self.on(j), self.sibling).wait_send()


def _block_order():
    x, y, c = _mesh_pos()
    chips = [(1 - x, y), (x, 1 - y), (1 - x, 1 - y)]
    idx = lambda px, py, pc: 4 * px + 2 * py + pc
    order = [idx(x, y, c), idx(x, y, 1 - c)] + [idx(*ch, c) for ch in chips] + [idx(*ch, 1 - c) for ch in chips]
    return jnp.stack(order).astype(jnp.int32)


def _gather_project(x, w_in, w_out, conv_pack, ln_g, tb):
    t_len, d = x.shape
    nb = t_len // tb
    cols = w_in.shape[1]
    mc = min(512, t_len)
    srcs = (w_in, w_out, conv_pack)
    dts = (MM, MM, F32)

    def body(order_ref, x_ref, win_ref, wout_ref, cp_ref, lng_ref, p_ref, xnt_ref, win_all, wout_all, cp_all,
             xnb, st_in, st_out, st_cp, wbuf, send_sems, recv_sems, cp_send, cp_recv, local_sems):
        i = pl.program_id(0)
        x_, y_, c_ = _mesh_pos()
        me = 4 * x_ + 2 * y_ + c_
        outs = (win_all, wout_all, cp_all)
        stages = (st_in, st_out, st_cp)
        gather = _BalancedGather(lambda a, px, py, pc: outs[a].at[4 * px + 2 * py + pc], send_sems, recv_sems, stages)
        small = _Gather(lambda a, px, py, pc: cp_all.at[4 * px + 2 * py + pc], cp_send, cp_recv, own_src=[st_cp])
        keep_own = [pltpu.make_async_copy(stages[a], outs[a].at[me], local_sems.at[a]) for a in range(3)]

        @pl.when(i == 0)
        def _():
            for a, (src, dst) in enumerate(zip((win_ref, wout_ref, cp_ref), stages)):
                rows = src.shape[0]
                rc = min(rows, 32)

                def cast(r, carry, src=src, dst=dst, rc=rc):
                    dst[pl.ds(r, rc), :] = src[pl.ds(r, rc), :].astype(dst.dtype)
                    return carry

                _chunks(rows, rc, cast, 0)
                if a < 2:
                    gather.start_own(a)
                else:
                    small.start_own(0)
                keep_own[a].start()

        @pl.when(i < nb)
        def _():
            xv = x_ref[...]
            r0 = lax.rsqrt(jnp.mean(xv * xv, axis=-1, keepdims=True) + RMS_EPS)
            xn = xv * r0 * lng_ref[...]
            xnb[pl.ds(pl.multiple_of(i * tb, tb), tb), :] = xn.astype(MM)
            xnt_ref[...] = xn.T.astype(MM)

        for k in range(N_DEV):
            @pl.when(i == nb + k)
            def _(k=k):
                if k == 1:
                    gather.wait_sibling(0)
                elif k == 2:
                    gather.on_neighbour(0, 0)
                elif k == 3:
                    gather.on_neighbour(0, 1)
                    gather.on_neighbour(1, 0)
                elif k == 4:
                    gather.on_diagonal(0)
                    gather.on_neighbour(1, 1)
                elif k >= 5:
                    gather.wait_passed_on(0, k - 5)
                    if k == 5:
                        gather.on_diagonal(1)
                if k == 0:
                    w_blk = st_in
                else:
                    load = pltpu.make_async_copy(win_all.at[order_ref[k]], wbuf, local_sems.at[3])
                    load.start()
                    load.wait()
                    w_blk = wbuf

                def project(r, carry):
                    rows = pl.ds(r, mc)
                    p_ref[rows, :] = _dot(xnb[rows, :], w_blk[...]).astype(MM)
                    return carry

                _chunks(t_len, mc, project, 0)
                if k == N_DEV - 1:
                    gather.wait_sends(0)
                    gather.wait_sibling(1)
                    for j in range(3):
                        gather.wait_passed_on(1, j)
                    gather.wait_sends(1)
                    small.finish(0)
                    for cp in keep_own:
                        cp.wait()

    vm = pl.BlockSpec(memory_space=pltpu.VMEM)
    hbm = pl.BlockSpec(memory_space=pl.ANY)
    grid_spec = pltpu.PrefetchScalarGridSpec(
        num_scalar_prefetch=1, grid=(nb + N_DEV,),
        in_specs=[pl.BlockSpec((tb, d), lambda i, o: (jnp.minimum(i, nb - 1), 0)), vm, vm, vm, vm],
        out_specs=(pl.BlockSpec((t_len, cols), lambda i, o: (0, o[jnp.maximum(i - nb, 0)])),
                   pl.BlockSpec((d, tb), lambda i, o: (0, jnp.minimum(i, nb - 1))), hbm, hbm, hbm),
        scratch_shapes=[pltpu.VMEM((t_len, d), MM)] + [pltpu.VMEM(s.shape, dt) for s, dt in zip(srcs, dts)]
                       + [pltpu.VMEM(w_in.shape, MM),
                          pltpu.SemaphoreType.DMA((16,)), pltpu.SemaphoreType.DMA((16,)),
                          pltpu.SemaphoreType.DMA((7,)), pltpu.SemaphoreType.DMA((7,)), pltpu.SemaphoreType.DMA((4,))])
    return pl.pallas_call(
        body, name="gather_project", grid_spec=grid_spec,
        out_shape=(jax.ShapeDtypeStruct((t_len, N_DEV * cols), MM),
                   jax.ShapeDtypeStruct((d, t_len), MM))
                  + tuple(jax.ShapeDtypeStruct((N_DEV,) + s.shape, dt) for s, dt in zip(srcs, dts)),
        compiler_params=_params(dimension_semantics=("arbitrary",)),
    )(_block_order(), x, w_in, w_out, conv_pack, ln_g)


def _forward(x, tgt, p, wout, wa_t, wi_t, sp, ones_c, ones_l, tb):
    t_len, d = x.shape
    nb = t_len // tb
    n_tiles, tw = wa_t.shape[0], wa_t.shape[1]
    hd_c, hd_l = d // N_CONV_HEADS, d // N_LRU_HEADS
    s8 = SUBLANES

    def body(x_ref, tgt_ref, p_ref, wout_ref, wa_ref, wi_ref, sp_ref, oc_ref, ol_ref,
             h_ref, dh_ref, dhb_ref, acc_ref,
             yc, u, pa, pi, rcf, rlf, ybuf, tail_z, tail_xl, hcar):
        i = pl.program_id(0)
        row = _row_iota(d)

        @pl.when(i == 0)
        def _():
            tail_z[...] = jnp.zeros_like(tail_z)
            tail_xl[...] = jnp.zeros_like(tail_xl)
            hcar[...] = jnp.zeros_like(hcar)
            acc_ref[...] = jnp.zeros_like(acc_ref)

        def spr(r):
            return sp_ref[r:r + 1, :]

        def proj(rows, seg):
            return p_ref[rows, seg * d:(seg + 1) * d].astype(F32)

        w0, w1, w2 = spr(SP_CONV_W), spr(SP_CONV_W + 1), spr(SP_CONV_W + 2)
        l0, l1, l2, l3 = spr(SP_LRU_W), spr(SP_LRU_W + 1), spr(SP_LRU_W + 2), spr(SP_LRU_W + 3)
        lb = spr(SP_LRU_B)

        def convs(r, carry):
            zp, xp = carry
            rows16 = pl.ds(r, 2 * s8)
            bg16, xl16 = proj(rows16, P_B), proj(rows16, P_XL)
            z16 = proj(rows16, P_C) * proj(rows16, P_XC)
            for j in range(2):
                rows, sub = pl.ds(r + j * s8, s8), slice(j * s8, (j + 1) * s8)
                z, xl = z16[sub], xl16[sub]
                cz = w0 * _shift_down(z, zp, 2, row) + w1 * _shift_down(z, zp, 1, row) + w2 * z
                yc[rows, :] = bg16[sub] * cz
                u[rows, :] = (l0 * _shift_down(xl, xp, 3, row) + l1 * _shift_down(xl, xp, 2, row)
                              + l2 * _shift_down(xl, xp, 1, row) + l3 * xl + lb)
                zp, xp = z, xl
            return zp, xp

        z_last, xl_last = _chunks(tb, 2 * s8, convs, (tail_z[...], tail_xl[...]))
        tail_z[...] = z_last
        tail_xl[...] = xl_last

        ub = u[...].astype(MM)
        for k in range(n_tiles):
            sl = slice(k * tw, (k + 1) * tw)
            pa[:, sl] = _dot(ub[:, sl], wa_ref[k])
            pi[:, sl] = _dot(ub[:, sl], wi_ref[k])
        rcf[...] = _head_rstd(yc[...], oc_ref[...], hd_c)

        c8 = RG_LRU_C * _log_sigmoid(spr(SP_LAM))
        b_a, b_i = spr(SP_B_A), spr(SP_B_I)

        def lru(r, hp):
            rows = pl.ds(r, SUBLANES)
            ra = jax.nn.sigmoid(pa[rows, :] + b_a)
            ii = jax.nn.sigmoid(pi[rows, :] + b_i)
            la = ra * c8
            a = jnp.exp(la)
            mult = jnp.sqrt(_neg_expm1(2.0 * la))
            h = _scan_fwd(a, mult * (ii * u[rows, :]), hp, row)
            h_ref[rows, :] = h
            return _bcast_row(h, SUBLANES - 1)

        hcar[...] = _chunks(tb, SUBLANES, lru, hcar[...])
        rlf[...] = _head_rstd(h_ref[...], ol_ref[...], hd_l)

        g_c, g_l = spr(SP_CONV_G), spr(SP_LRU_G)

        def gate(r, carry):
            rows = pl.ds(r, 2 * s8)
            gc, gl = proj(rows, P_GC), proj(rows, P_GL)
            ybuf[rows, 0:d] = (yc[rows, :] * rcf[rows, :] * g_c * (gc * jax.nn.sigmoid(gc))).astype(MM)
            ybuf[rows, d:2 * d] = (h_ref[rows, :] * rlf[rows, :] * g_l * (gl * jax.nn.sigmoid(gl))).astype(MM)
            return carry

        _chunks(tb, 2 * s8, gate, 0)

        hres = x_ref[...] + _dot(ybuf[...], wout_ref[...])
        rf = lax.rsqrt(jnp.mean(hres * hres, axis=-1, keepdims=True) + RMS_EPS)
        hn = hres * rf
        fg = spr(SP_FINAL_G)
        err = hn * fg - tgt_ref[...]
        dout = err * (1.0 / d)
        acc_ref[0:SUBLANES, :] += (err * err).reshape(tb // SUBLANES, SUBLANES, d).sum(axis=0)
        acc_ref[SUBLANES:2 * SUBLANES, :] += (dout * hn).reshape(tb // SUBLANES, SUBLANES, d).sum(axis=0)
        gd = dout * fg
        dhres = rf * (gd - hn * jnp.mean(gd * hn, axis=-1, keepdims=True))
        dh_ref[...] = dhres
        dhb_ref[...] = dhres.astype(MM)

    vm = pl.BlockSpec(memory_space=pltpu.VMEM)
    blk = lambda w: pl.BlockSpec((tb, w), lambda i: (i, 0))
    buf = pltpu.VMEM((tb, d), F32)
    car = pltpu.VMEM((SUBLANES, d), F32)
    return pl.pallas_call(
        body, name="forward", grid=(nb,),
        in_specs=[blk(d), blk(d), blk(6 * d), vm, vm, vm, vm, vm, vm],
        out_specs=(blk(d), blk(d), blk(d), pl.BlockSpec((2 * SUBLANES, d), lambda i: (0, 0))),
        out_shape=(jax.ShapeDtypeStruct((t_len, d), F32),
                   jax.ShapeDtypeStruct((t_len, d), F32),
                   jax.ShapeDtypeStruct((t_len, d), MM),
                   jax.ShapeDtypeStruct((2 * SUBLANES, d), F32)),
        scratch_shapes=[buf] * 6 + [pltpu.VMEM((tb, 2 * d), MM), car, car, car],
        compiler_params=_params(dimension_semantics=("arbitrary",)),
    )(x, tgt, p, wout, wa_t, wi_t, sp, ones_c, ones_l)


def _backward(p, h, dh, wout, wa_t, wi_t, sp, ones_c, ones_l, tb):
    t_len, d = h.shape
    nb = t_len // tb
    n_tiles, tw = wa_t.shape[0], wa_t.shape[1]
    hd_c, hd_l = d // N_CONV_HEADS, d // N_LRU_HEADS
    s8 = SUBLANES
    ph = 2 * s8

    def body(p_ref, phalo_ref, h_ref, hhalo_ref, dh_ref, wout_ref, wa_ref, wi_ref, sp_ref, oc_ref, ol_ref,
             dp_ref, yt_ref, gwa_ref, gwi_ref, acc_ref,
             hh, dy, ybuf, yc, czs, u, pa, pi, rcf, rlf, qc, ql, dyc_hat, dyl_hat, dpa, dpi, du,
             car_dcz, car_a, car_g, car_du):
        i = pl.program_id(0)
        blk_idx = nb - 1 - i
        row = _row_iota(d)

        @pl.when(i == 0)
        def _():
            for ref in (car_dcz, car_a, car_g, car_du, gwa_ref, gwi_ref, acc_ref):
                ref[...] = jnp.zeros_like(ref)

        def spr(r):
            return sp_ref[r:r + 1, :]

        def proj(rows, seg):
            return p_ref[rows, seg * d:(seg + 1) * d].astype(F32)

        def put(rows, seg, halves):
            dp_ref[rows, seg * d:(seg + 1) * d] = jnp.concatenate(halves, axis=0).astype(MM)

        def acc_add(group, val):
            acc_ref[group * s8:(group + 1) * s8, :] += val

        live = jnp.where(blk_idx > 0, 1.0, 0.0).astype(F32)
        hh[0:s8, :] = hhalo_ref[...] * live
        hh[s8:, :] = h_ref[...]

        dy[...] = _dot_nt(dh_ref[...].astype(MM), wout_ref[...])

        w0, w1, w2 = spr(SP_CONV_W), spr(SP_CONV_W + 1), spr(SP_CONV_W + 2)
        l0, l1, l2, l3 = spr(SP_LRU_W), spr(SP_LRU_W + 1), spr(SP_LRU_W + 2), spr(SP_LRU_W + 3)
        lb = spr(SP_LRU_B)

        def recompute(r, carry):
            zp, xp = carry
            rows16 = pl.ds(r, 2 * s8)
            bg16, xl16 = proj(rows16, P_B), proj(rows16, P_XL)
            z16 = proj(rows16, P_C) * proj(rows16, P_XC)
            for j in range(2):
                rows, sub = pl.ds(r + j * s8, s8), slice(j * s8, (j + 1) * s8)
                z, xl = z16[sub], xl16[sub]
                cz = w0 * _shift_down(z, zp, 2, row) + w1 * _shift_down(z, zp, 1, row) + w2 * z
                czs[rows, :] = cz
                yc[rows, :] = bg16[sub] * cz
                u[rows, :] = (l0 * _shift_down(xl, xp, 3, row) + l1 * _shift_down(xl, xp, 2, row)
                              + l2 * _shift_down(xl, xp, 1, row) + l3 * xl + lb)
                zp, xp = z, xl
            return zp, xp

        halo = lambda seg: phalo_ref[:, seg * d:(seg + 1) * d].astype(F32)[s8:, :] * live
        _chunks(tb, 2 * s8, recompute, (halo(P_C) * halo(P_XC), halo(P_XL)))

        ub = u[...].astype(MM)
        for k in range(n_tiles):
            sl = slice(k * tw, (k + 1) * tw)
            pa[:, sl] = _dot(ub[:, sl], wa_ref[k])
            pi[:, sl] = _dot(ub[:, sl], wi_ref[k])
        rcf[...] = _head_rstd(yc[...], oc_ref[...], hd_c)
        rlf[...] = _head_rstd(h_ref[...], ol_ref[...], hd_l)

        g_c, g_l = spr(SP_CONV_G), spr(SP_LRU_G)

        def gates(r, carry):
            rows = pl.ds(r, 2 * s8)
            for (seg, off_y, src, rstd, gain, q, dhat, grp) in (
                    (P_GC, 0, yc, rcf, g_c, qc, dyc_hat, A_CONV_G),
                    (P_GL, d, h_ref, rlf, g_l, ql, dyl_hat, A_LRU_G)):
                gt = proj(rows, seg)
                sg = jax.nn.sigmoid(gt)
                silu = gt * sg
                yhat = src[rows, :] * rstd[rows, :]
                nrm = yhat * gain
                ybuf[rows, off_y:off_y + d] = nrm * silu
                dout = dy[rows, off_y:off_y + d]
                dnrm = dout * silu
                dp_ref[rows, seg * d:(seg + 1) * d] = (dout * nrm * (sg * (1.0 + gt * (1.0 - sg)))).astype(MM)
                dg = dnrm * yhat
                acc_add(grp, dg[0:s8] + dg[s8:])
                dh_ = dnrm * gain
                dhat[rows, :] = dh_
                q[rows, :] = dh_ * yhat
            return carry

        _chunks(tb, 2 * s8, gates, 0)

        qc[...] = _head_sums(qc[...], oc_ref[...]) * (1.0 / hd_c)
        ql[...] = _head_sums(ql[...], ol_ref[...]) * (1.0 / hd_l)
        yt_ref[...] = ybuf[...].T.astype(MM)

        c8 = RG_LRU_C * _log_sigmoid(spr(SP_LAM))
        b_a, b_i = spr(SP_B_A), spr(SP_B_I)

        def mixers(r, carry):
            dcz_n, a_n, g_n = carry
            rows16 = pl.ds(r, 2 * s8)
            bg16, cg16, xc16 = proj(rows16, P_B), proj(rows16, P_C), proj(rows16, P_XC)
            z16 = cg16 * xc16
            d_b, d_c, d_x = [None, None], [None, None], [None, None]
            for j in (1, 0):
                rows, sub = pl.ds(r + j * s8, s8), slice(j * s8, (j + 1) * s8)
                rstd = rcf[rows, :]
                yhat = yc[rows, :] * rstd
                dyc = rstd * (dyc_hat[rows, :] - yhat * qc[rows, :])
                d_b[j] = dyc * czs[rows, :]
                dcz = dyc * bg16[sub]
                up1, up2 = _shift_up(dcz, dcz_n, 1, row), _shift_up(dcz, dcz_n, 2, row)
                dz = w2 * dcz + w1 * up1 + w0 * up2
                d_c[j] = dz * xc16[sub]
                d_x[j] = dz * cg16[sub]
                z = z16[sub]
                acc_add(A_CONV_W, up2 * z)
                acc_add(A_CONV_W + 1, up1 * z)
                acc_add(A_CONV_W + 2, dcz * z)
                rstd = rlf[rows, :]
                hcur = hh[pl.ds(r + (j + 1) * s8, s8), :]
                hhat = hcur * rstd
                dh_out = rstd * (dyl_hat[rows, :] - hhat * ql[rows, :])
                ra = jax.nn.sigmoid(pa[rows, :] + b_a)
                la = ra * c8
                a = jnp.exp(la)
                g = _scan_bwd(_shift_up(a, a_n, 1, row), dh_out, g_n, row)
                da = g * _shift_down(hcur, hh[pl.ds(r + j * s8, s8), :], 1, row)
                ii = jax.nn.sigmoid(pi[rows, :] + b_i)
                uu = u[rows, :]
                mult = jnp.sqrt(_neg_expm1(2.0 * la))
                dmult = g * (ii * uu)
                ds = g * mult
                dla = a * (da - dmult * a / mult)
                acc_add(A_LAM, dla * ra)
                dpa_ = dla * c8 * ra * (1.0 - ra)
                dpi_ = ds * uu * ii * (1.0 - ii)
                acc_add(A_B_A, dpa_)
                acc_add(A_B_I, dpi_)
                dpa[rows, :] = dpa_
                dpi[rows, :] = dpi_
                du[rows, :] = ds * ii
                dcz_n, a_n, g_n = dcz, a, _bcast_row(g, 0)
            put(rows16, P_B, d_b)
            put(rows16, P_C, d_c)
            put(rows16, P_XC, d_x)
            return dcz_n, a_n, g_n

        dcz_f, a_f, g_f = _chunks(tb, 2 * s8, mixers, (car_dcz[...], car_a[...], car_g[...]), reverse=True)
        car_dcz[...] = dcz_f
        car_a[...] = a_f
        car_g[...] = g_f

        dpab = dpa[...].astype(MM)
        dpib = dpi[...].astype(MM)
        for k in range(n_tiles):
            sl = slice(k * tw, (k + 1) * tw)
            du[:, sl] += _dot_nt(dpab[:, sl], wa_ref[k]) + _dot_nt(dpib[:, sl], wi_ref[k])
            ut = u[:, sl].T.astype(MM)
            gwa_ref[k] += _dot(ut, dpab[:, sl])
            gwi_ref[k] += _dot(ut, dpib[:, sl])

        def lru_conv(r, du_n):
            rows16 = pl.ds(r, 2 * s8)
            xl16 = proj(rows16, P_XL)
            d_xl = [None, None]
            for j in (1, 0):
                rows, sub = pl.ds(r + j * s8, s8), slice(j * s8, (j + 1) * s8)
                dut = du[rows, :]
                up1, up2, up3 = (_shift_up(dut, du_n, s, row) for s in (1, 2, 3))
                d_xl[j] = l3 * dut + l2 * up1 + l1 * up2 + l0 * up3
                xl = xl16[sub]
                acc_add(A_LRU_W, up3 * xl)
                acc_add(A_LRU_W + 1, up2 * xl)
                acc_add(A_LRU_W + 2, up1 * xl)
                acc_add(A_LRU_W + 3, dut * xl)
                acc_add(A_LRU_B, dut)
                du_n = dut
            put(rows16, P_XL, d_xl)
            return du_n

        car_du[...] = _chunks(tb, 2 * s8, lru_conv, car_du[...], reverse=True)

    vm = pl.BlockSpec(memory_space=pltpu.VMEM)
    rev = lambda w: pl.BlockSpec((tb, w), lambda i: (nb - 1 - i, 0))
    halo = lambda rows, w: pl.BlockSpec((rows, w), lambda i: (jnp.maximum((nb - 1 - i) * (tb // rows) - 1, 0), 0))
    const = lambda shape: pl.BlockSpec(shape, lambda i: (0,) * len(shape))
    buf = lambda w: pltpu.VMEM((tb, w), F32)
    car = pltpu.VMEM((SUBLANES, d), F32)
    return pl.pallas_call(
        body, name="backward", grid=(nb,),
        in_specs=[rev(6 * d), halo(ph, 6 * d), rev(d), halo(SUBLANES, d), rev(d), vm, vm, vm, vm, vm, vm],
        out_specs=(rev(6 * d), pl.BlockSpec((2 * d, tb), lambda i: (0, nb - 1 - i)),
                   const((n_tiles, tw, tw)), const((n_tiles, tw, tw)), const((A_GROUPS * SUBLANES, d))),
        out_shape=(jax.ShapeDtypeStruct((t_len, 6 * d), MM),
                   jax.ShapeDtypeStruct((2 * d, t_len), MM),
                   jax.ShapeDtypeStruct((n_tiles, tw, tw), F32),
                   jax.ShapeDtypeStruct((n_tiles, tw, tw), F32),
                   jax.ShapeDtypeStruct((A_GROUPS * SUBLANES, d), F32)),
        scratch_shapes=[pltpu.VMEM((SUBLANES + tb, d), F32), buf(2 * d), buf(2 * d)] + [buf(d)] * 14 + [car, car, car, car],
        compiler_params=_params(dimension_semantics=("arbitrary",)),
    )(p, p, h, h, dh, wout, wa_t, wi_t, sp, ones_c, ones_l)


def _input_grad(dp, win_all, x, dh, sp, parts, tb):
    t_len, d = x.shape
    nb = t_len // tb
    cols = win_all.shape[2]
    n_parts = len(parts)

    def body(dp_ref, win_ref, x_ref, dh_ref, sp_ref, *refs):
        gx_ref, acc_ref = refs[n_parts:n_parts + 2]
        exchange = _ChipExchange(refs[:n_parts], refs[n_parts + 2:2 * n_parts + 2], *refs[2 * n_parts + 2:])
        i = pl.program_id(0)

        @pl.when(i == 0)
        def _():
            acc_ref[...] = jnp.zeros_like(acc_ref)
            exchange.start()

        dxn = _dot_nt(dp_ref[:, 0:cols], win_ref[0])
        for j in range(1, N_DEV):
            dxn += _dot_nt(dp_ref[:, j * cols:(j + 1) * cols], win_ref[j])
        xv = x_ref[...]
        r0 = lax.rsqrt(jnp.mean(xv * xv, axis=-1, keepdims=True) + RMS_EPS)
        xhat = xv * r0
        acc_ref[...] += (dxn * xhat).reshape(tb // SUBLANES, SUBLANES, d).sum(axis=0)
        dxh = dxn * sp_ref[SP_LN_G:SP_LN_G + 1, :]
        gx_ref[...] = dh_ref[...] + r0 * (dxh - xhat * jnp.mean(dxh * xhat, axis=-1, keepdims=True))

        @pl.when(i == nb - 1)
        def _():
            exchange.finish()

    vm = pl.BlockSpec(memory_space=pltpu.VMEM)
    hbm = pl.BlockSpec(memory_space=pl.ANY)
    blk = lambda w: pl.BlockSpec((tb, w), lambda i: (i, 0))
    outs = pl.pallas_call(
        body, name="input_grad", grid=(nb,),
        in_specs=[blk(6 * d), vm, blk(d), blk(d), vm] + [hbm] * n_parts,
        out_specs=(blk(d), pl.BlockSpec((SUBLANES, d), lambda i: (0, 0))) + (hbm,) * n_parts,
        out_shape=(jax.ShapeDtypeStruct((t_len, d), F32), jax.ShapeDtypeStruct((SUBLANES, d), F32))
                  + tuple(jax.ShapeDtypeStruct(p.shape, p.dtype) for p in parts),
        scratch_shapes=[pltpu.SemaphoreType.DMA((3 * n_parts,)), pltpu.SemaphoreType.DMA((3 * n_parts,))],
        compiler_params=_params(dimension_semantics=("arbitrary",)),
    )(dp, win_all, x, dh, sp, *parts)
    return outs[0], outs[1], outs[2:]


_CHIP_RELATIONS = [(0, 0), (1, 0), (0, 1), (1, 1)]


def _related_block(k, core):
    x, y, _ = _mesh_pos()
    fx, fy = _CHIP_RELATIONS[k]
    return 4 * (x ^ fx) + 2 * (y ^ fy) + core


class _ChipExchange:
    def __init__(self, part_refs, land_refs, send_sems, recv_sems):
        self.part_refs, self.land_refs, self.send_sems, self.recv_sems = part_refs, land_refs, send_sems, recv_sems

    def copies(self):
        x, y, c = _mesh_pos()
        for a in range(len(self.part_refs)):
            for k in (1, 2, 3):
                fx, fy = _CHIP_RELATIONS[k]
                yield pltpu.make_async_remote_copy(
                    src_ref=self.part_refs[a].at[k - 1], dst_ref=self.land_refs[a].at[k - 1],
                    send_sem=self.send_sems.at[3 * a + k - 1], recv_sem=self.recv_sems.at[3 * a + k - 1],
                    device_id=(x ^ fx, y ^ fy, c), device_id_type=MESH)

    def start(self):
        for cp in self.copies():
            cp.start()

    def finish(self):
        for cp in self.copies():
            cp.wait_recv()
        for cp in self.copies():
            cp.wait_send()


def _weight_grad_stage1(name, blk_shape, n_split, operands, in_specs, product, riders=()):
    n_rows, n_cols = blk_shape
    rs = n_rows // n_split
    rc = 32
    n_in, n_ride = len(operands), len(riders)
    _, _, c = _mesh_pos()
    order = jnp.stack([_related_block(k, 1 - c) for k in range(4)]
                      + [_related_block(k, c) for k in (1, 2, 3, 0)]).astype(jnp.int32)

    def body(order_ref, *refs):
        ins = refs[:n_in]
        ride_in = refs[n_in:n_in + n_ride]
        part_ref, own_ref, from_sib = refs[n_in + n_ride:n_in + n_ride + 3]
        ride_out = refs[n_in + n_ride + 3:n_in + 2 * n_ride + 3]
        gbuf, sendbuf, tmp, send_sems, recv_sems, local_sems, ride_send, ride_recv = refs[n_in + 2 * n_ride + 3:]
        exchange = _ChipExchange(ride_in, ride_out, ride_send, ride_recv)
        s = pl.program_id(0)
        x, y, c = _mesh_pos()

        def to_sibling(k):
            return pltpu.make_async_remote_copy(
                src_ref=sendbuf.at[k], dst_ref=from_sib.at[k], send_sem=send_sems.at[k], recv_sem=recv_sems.at[k],
                device_id=(x, y, 1 - c), device_id_type=MESH)

        if n_ride:
            @pl.when(s == 0)
            def _():
                exchange.start()

        for h in range(n_split):
            gbuf[h * rs:(h + 1) * rs, :] = product(ins, h)

        @pl.when(s < 4)
        def _():
            def narrow(r, carry):
                sendbuf[s, pl.ds(r, rc), :] = gbuf[pl.ds(r, rc), :].astype(MM)
                return carry

            _chunks(n_rows, rc, narrow, 0)
            to_sibling(s).start()

        @pl.when(s >= 4)
        def _():
            k = jnp.where(s == 7, 0, s - 3)
            to_sibling(k).wait_recv()
            load = pltpu.make_async_copy(from_sib.at[k], tmp, local_sems.at[0])
            load.start()
            load.wait()

            @pl.when(s < 7)
            def _():
                def add(r, carry):
                    rows = pl.ds(r, rc)
                    part_ref[0, rows, :] = (gbuf[rows, :] + tmp[rows, :].astype(F32)).astype(MM)
                    return carry

                _chunks(n_rows, rc, add, 0)

            @pl.when(s == 7)
            def _():
                def add(r, carry):
                    rows = pl.ds(r, rc)
                    own_ref[rows, :] = gbuf[rows, :] + tmp[rows, :].astype(F32)
                    return carry

                _chunks(n_rows, rc, add, 0)
                for kk in range(4):
                    to_sibling(kk).wait_send()
                if n_ride:
                    exchange.finish()

    hbm = pl.BlockSpec(memory_space=pl.ANY)
    grid_spec = pltpu.PrefetchScalarGridSpec(
        num_scalar_prefetch=1, grid=(N_DEV,), in_specs=list(in_specs) + [hbm] * n_ride,
        out_specs=(pl.BlockSpec((1, n_rows, n_cols), lambda s, o: (jnp.clip(s - 4, 0, 2), 0, 0)),
                   pl.BlockSpec((n_rows, n_cols), lambda s, o: (0, 0)), hbm) + (hbm,) * n_ride,
        scratch_shapes=[pltpu.VMEM((n_rows, n_cols), F32), pltpu.VMEM((4, n_rows, n_cols), MM),
                        pltpu.VMEM((n_rows, n_cols), MM),
                        pltpu.SemaphoreType.DMA((4,)), pltpu.SemaphoreType.DMA((4,)), pltpu.SemaphoreType.DMA((1,)),
                        pltpu.SemaphoreType.DMA((max(3 * n_ride, 1),)), pltpu.SemaphoreType.DMA((max(3 * n_ride, 1),))])
    outs = pl.pallas_call(
        body, name=name, grid_spec=grid_spec,
        out_shape=(jax.ShapeDtypeStruct((3, n_rows, n_cols), MM), jax.ShapeDtypeStruct((n_rows, n_cols), F32),
                   jax.ShapeDtypeStruct((4, n_rows, n_cols), MM))
                  + tuple(jax.ShapeDtypeStruct(p.shape, p.dtype) for p in riders),
        compiler_params=_params(dimension_semantics=("arbitrary",)),
    )(order, *operands, *riders)
    return outs[0], outs[1], outs[3:]


def _weight_grad_in(xnt, dp, riders):
    d, t_len = xnt.shape
    cols = dp.shape[1] // N_DEV
    half = d // 2
    return _weight_grad_stage1(
        "weight_grad_in", (d, cols), 2, (xnt, dp),
        [pl.BlockSpec(memory_space=pltpu.VMEM), pl.BlockSpec((t_len, cols), lambda s, o: (0, o[s]))],
        lambda refs, h: _dot(refs[0][h * half:(h + 1) * half, :], refs[1][...]), riders)


def _weight_grad_out(yt, dhb):
    d2, t_len = yt.shape
    d = dhb.shape[1]
    rows = d2 // N_DEV
    return _weight_grad_stage1(
        "weight_grad_out", (rows, d), 1, (yt, dhb),
        [pl.BlockSpec((rows, t_len), lambda s, o: (o[s], 0)), pl.BlockSpec(memory_space=pltpu.VMEM)],
        lambda refs, h: _dot(refs[0][...], refs[1][...]))


def _update_shard(own, from_chips, w, m, v, name):
    n_rows, n_cols = w.shape
    rb = min(256, n_rows)

    def body(own_ref, fc_ref, w_ref, m_ref, v_ref, grad_ref, delta_ref, mo_ref, vo_ref):
        g = own_ref[...]
        for k in range(3):
            g = g + fc_ref[k].astype(F32)
        delta, m_new, v_new = _adamw(w_ref[...], g, m_ref[...], v_ref[...])
        grad_ref[...] = g
        delta_ref[...] = delta
        mo_ref[...] = m_new
        vo_ref[...] = v_new

    blk = pl.BlockSpec((rb, n_cols), lambda i: (i, 0))
    out = jax.ShapeDtypeStruct((n_rows, n_cols), F32)
    return pl.pallas_call(
        body, name=name, grid=(n_rows // rb,),
        in_specs=[blk, pl.BlockSpec((3, rb, n_cols), lambda i: (0, i, 0)), blk, blk, blk],
        out_specs=(blk, blk, blk, blk), out_shape=(out, out, out, out),
        compiler_params=_params(dimension_semantics=("arbitrary",)),
    )(own, from_chips, w, m, v)


def _small_update(facc, xacc, bacc, gwa, gwi, lam, wsl, msl, vsl, cw, cm, cv):
    n_rows, d = wsl.shape
    s8 = SUBLANES
    cc = cw.shape[1]

    def body(facc_ref, xacc_ref, bacc_ref, gwa_ref, gwi_ref, lam_ref, w_ref, m_ref, v_ref, cw_ref, cm_ref, cv_ref,
             g_o, d_o, m_o, v_o, cg_o, cd_o, cm_o, cv_o, gat, send_sems, recv_sems):
        x, y, c = _mesh_pos()
        me = 4 * x + 2 * y + c

        def rowsum(ref, group):
            return jnp.sum(ref[group * s8:(group + 1) * s8, :], axis=0, keepdims=True)

        mine = gat.at[me]
        loss = jnp.sum(rowsum(facc_ref, 0), axis=1, keepdims=True) * (0.5 / d)
        mine[SL_LOSS:SL_LOSS + 1, :] = jnp.broadcast_to(loss, (1, d))
        mine[SL_LN_G:SL_LN_G + 1, :] = rowsum(xacc_ref, 0)
        mine[SL_LRU_B:SL_LRU_B + 1, :] = rowsum(bacc_ref, A_LRU_B)
        mine[SL_B_A:SL_B_A + 1, :] = rowsum(bacc_ref, A_B_A)
        mine[SL_B_I:SL_B_I + 1, :] = rowsum(bacc_ref, A_B_I)
        mine[SL_LAM:SL_LAM + 1, :] = rowsum(bacc_ref, A_LAM) * (RG_LRU_C * jax.nn.sigmoid(-lam_ref[...]))
        mine[SL_CONV_G:SL_CONV_G + 1, :] = rowsum(bacc_ref, A_CONV_G)
        mine[SL_LRU_G:SL_LRU_G + 1, :] = rowsum(bacc_ref, A_LRU_G)
        mine[SL_FINAL_G:SL_FINAL_G + 1, :] = rowsum(facc_ref, 1)
        for k in range(3):
            mine[SL_CONV_W + k:SL_CONV_W + k + 1, :] = rowsum(bacc_ref, A_CONV_W + k)
        for k in range(4):
            mine[SL_LRU_W + k:SL_LRU_W + k + 1, :] = rowsum(bacc_ref, A_LRU_W + k)
        na = gwa_ref.shape[0]
        mine[SL_W_A:SL_W_A + na, :] = gwa_ref[...]
        mine[SL_W_A + na:SL_W_A + 2 * na, :] = gwi_ref[...]

        gather = _Gather(lambda a, px, py, pc: gat.at[4 * px + 2 * py + pc], send_sems, recv_sems)
        gather.start_own(0)
        gather.finish(0)

        def update(r, carry):
            rows = pl.ds(r, s8)
            g = gat[0, rows, :]
            for b in range(1, N_DEV):
                g = g + gat[b, rows, :]
            delta, m_new, v_new = _adamw(w_ref[rows, :], g, m_ref[rows, :], v_ref[rows, :])
            g_o[rows, :] = g
            d_o[rows, :] = delta
            m_o[rows, :] = m_new
            v_o[rows, :] = v_new
            return carry

        _chunks(n_rows, s8, update, 0)
        gc = g_o[s8:2 * s8, pl.ds(pl.multiple_of(me * cc, cc), cc)]
        delta, m_new, v_new = _adamw(cw_ref[...], gc, cm_ref[...], cv_ref[...])
        cg_o[...] = gc
        cd_o[...] = delta
        cm_o[...] = m_new
        cv_o[...] = v_new

    vm = pl.BlockSpec(memory_space=pltpu.VMEM)
    big = jax.ShapeDtypeStruct((n_rows, d), F32)
    small = jax.ShapeDtypeStruct(cw.shape, F32)
    return pl.pallas_call(
        body, name="small_update",
        in_specs=[vm] * 12, out_specs=(vm,) * 8, out_shape=(big, big, big, big, small, small, small, small),
        scratch_shapes=[pltpu.VMEM((N_DEV, n_rows, d), F32), pltpu.SemaphoreType.DMA((7,)), pltpu.SemaphoreType.DMA((7,))],
        compiler_params=_params(),
    )(facc, xacc, bacc, gwa, gwi, lam, wsl, msl, vsl, cw, cm, cv)


def _head_ones(head_dim, tw):
    lane = jnp.arange(tw) // head_dim
    return (lane[:, None] == lane[None, :]).astype(MM)


def _gate_tiles(w, tw):
    n_heads, hd, _ = w.shape
    per = tw // hd
    w4 = w.reshape(n_heads // per, per, hd, hd)
    eye = jnp.eye(per, dtype=w.dtype)
    return (w4[:, :, :, None, :] * eye[None, :, None, :, None]).reshape(n_heads // per, tw, tw)


def _gate_blocks(tiles, n_heads, hd):
    n_tiles, tw, _ = tiles.shape
    per = tw // hd
    t5 = tiles.reshape(n_tiles, per, hd, per, hd)
    diag = jnp.stack([t5[:, a, :, a, :] for a in range(per)], axis=1)
    return diag.reshape(hd, n_heads * hd)


def kernel(x, ln_g, w_in, conv_w, lru_conv_w, lru_conv_b, w_a, b_a, w_i, b_i, lam, conv_out_g, lru_out_g, w_out, final_g, loss_target, m_ln_g, m_w_in, m_conv_w, m_lru_conv_w, m_lru_conv_b, m_w_a, m_b_a, m_w_i, m_b_i, m_lam, m_conv_out_g, m_lru_out_g, m_w_out, m_final_g, v_ln_g, v_w_in, v_conv_w, v_lru_conv_w, v_lru_conv_b, v_w_a, v_b_a, v_w_i, v_b_i, v_lam, v_conv_out_g, v_lru_out_g, v_w_out, v_final_g):
    _, t_len, d = x.shape
    hd_l = d // N_LRU_HEADS
    tw = min(MXU_TILE, d)
    cc = conv_w.shape[1]
    x2, tgt2 = x[0], loss_target[0]

    def conv_rows(cw3, lw4):
        return jnp.concatenate([jnp.zeros((1, cc), F32), cw3, lw4], axis=0)

    p, xnt, win_all, wout_all, conv_all = _gather_project(
        x2, w_in, w_out, conv_rows(conv_w, lru_conv_w), ln_g.reshape(1, d), min(256, t_len))
    wout_full = wout_all.reshape(N_DEV * w_out.shape[0], d)
    conv_full = conv_all.transpose(1, 0, 2).reshape(SUBLANES, d)
    small = [ln_g, lru_conv_b, b_a, b_i, lam, conv_out_g, lru_out_g, final_g]
    sp = jnp.concatenate([jnp.stack(small), conv_full[1:], jnp.zeros((1, d), F32)], axis=0)
    wa_t, wi_t = _gate_tiles(w_a, tw).astype(MM), _gate_tiles(w_i, tw).astype(MM)
    ones_c, ones_l = _head_ones(d // N_CONV_HEADS, tw), _head_ones(hd_l, tw)

    h, dh, dhb, facc = _forward(x2, tgt2, p, wout_full, wa_t, wi_t, sp, ones_c, ones_l, min(256, t_len))
    dp, yt, gwa_t, gwi_t, bacc = _backward(p, h, dh, wout_full, wa_t, wi_t, sp, ones_c, ones_l, min(256, t_len))
    part_out, own_out, _ = _weight_grad_out(yt, dhb)
    part_in, own_in, (chips_out,) = _weight_grad_in(xnt, dp, (part_out,))
    grad_x, xacc, (chips_in,) = _input_grad(dp, win_all, x2, dh, sp, (part_in,), min(512, t_len))
    gw_in, dw_in, mw_in, vw_in = _update_shard(own_in, chips_in, w_in, m_w_in, v_w_in, "update_w_in")
    gw_out, dw_out, mw_out, vw_out = _update_shard(own_out, chips_out, w_out, m_w_out, v_w_out, "update_w_out")

    def slab(parts, wa_, wi_):
        return jnp.concatenate([jnp.zeros((1, d), F32), jnp.stack(parts), jnp.zeros((SL_W_A - SL_CONV_W, d), F32),
                                wa_.reshape(-1, d), wi_.reshape(-1, d)], axis=0)

    wsl = slab(small, w_a, w_i)
    msl = slab([m_ln_g, m_lru_conv_b, m_b_a, m_b_i, m_lam, m_conv_out_g, m_lru_out_g, m_final_g], m_w_a, m_w_i)
    vsl = slab([v_ln_g, v_lru_conv_b, v_b_a, v_b_i, v_lam, v_conv_out_g, v_lru_out_g, v_final_g], v_w_a, v_w_i)
    outs = _small_update(
        facc, xacc, bacc, _gate_blocks(gwa_t, N_LRU_HEADS, hd_l), _gate_blocks(gwi_t, N_LRU_HEADS, hd_l),
        lam.reshape(1, d), wsl, msl, vsl,
        conv_rows(conv_w, lru_conv_w), conv_rows(m_conv_w, m_lru_conv_w), conv_rows(v_conv_w, v_lru_conv_w))
    sl_g, sl_d, sl_m, sl_v, c_g, c_d, c_m, c_v = outs
    loss = sl_g[SL_LOSS, 0]
    na = w_a.size // d

    def unpack(sl, cv, big_in, big_out):
        one = lambda r: sl[r]
        return [one(SL_LN_G), big_in, cv[1:4], cv[4:8], one(SL_LRU_B),
                sl[SL_W_A:SL_W_A + na].reshape(w_a.shape), one(SL_B_A),
                sl[SL_W_A + na:SL_W_A + 2 * na].reshape(w_i.shape), one(SL_B_I), one(SL_LAM),
                one(SL_CONV_G), one(SL_LRU_G), big_out, one(SL_FINAL_G)]

    return (loss, grad_x[None], *unpack(sl_g, c_g, gw_in, gw_out), *unpack(sl_d, c_d, dw_in, dw_out),
            *unpack(sl_m, c_m, mw_in, mw_out), *unpack(sl_v, c_v, vw_in, vw_out))
```

```python
import functools

import jax
import jax.numpy as jnp
from jax import lax
from jax.experimental import pallas as pl
from jax.experimental.pallas import tpu as pltpu

F32 = jnp.float32
MM = jnp.bfloat16
MESH = pl.DeviceIdType.MESH

N_DEV = 8
N_CONV_HEADS = 8
N_LRU_HEADS = 16
RG_LRU_C = 8.0
RMS_EPS = 1e-6
ADAM_LR, ADAM_B1, ADAM_B2, ADAM_EPS, ADAM_WD, ADAM_STEP = 0.001, 0.9, 0.999, 1e-08, 0.01, 10
ADAM_BC1 = 1.0 - ADAM_B1 ** ADAM_STEP
ADAM_BC2 = 1.0 - ADAM_B2 ** ADAM_STEP

SUBLANES = 8
LANES = 128
MXU_TILE = 256
VMEM_LIMIT = 56 * 1024 * 1024

SP_LN_G, SP_LRU_B, SP_B_A, SP_B_I, SP_LAM, SP_CONV_G, SP_LRU_G, SP_FINAL_G, SP_CONV_W, SP_LRU_W = 0, 1, 2, 3, 4, 5, 6, 7, 8, 11
SP_ROWS = 16
P_B, P_C, P_XC, P_GC, P_XL, P_GL = 0, 1, 2, 3, 4, 5
A_CONV_G, A_LRU_G, A_LAM, A_B_A, A_B_I, A_CONV_W, A_LRU_W, A_LRU_B = 0, 1, 2, 3, 4, 5, 8, 12
A_GROUPS = 13
SL_LOSS, SL_LN_G, SL_LRU_B, SL_B_A, SL_B_I, SL_LAM, SL_CONV_G, SL_LRU_G, SL_FINAL_G, SL_CONV_W, SL_LRU_W, SL_W_A = 0, 1, 2, 3, 4, 5, 6, 7, 8, 9, 12, 16


def _params(vmem=True, **kw):
    if vmem:
        kw["vmem_limit_bytes"] = VMEM_LIMIT
    return pltpu.CompilerParams(**kw)


def _dot(a, b):
    return jnp.dot(a, b, preferred_element_type=F32)


def _dot_nt(a, b):
    return lax.dot_general(a, b, (((1,), (1,)), ((), ())), preferred_element_type=F32)


def _head_sums(v, ones_tile):
    tw = ones_tile.shape[0]
    hi = v.astype(MM)
    lo = (v - hi.astype(F32)).astype(MM)
    return jnp.concatenate(
        [_dot(hi[:, k:k + tw], ones_tile) + _dot(lo[:, k:k + tw], ones_tile) for k in range(0, v.shape[1], tw)], axis=1)


def _head_rstd(v, ones_tile, head_dim):
    return lax.rsqrt(_head_sums(v * v, ones_tile) * (1.0 / head_dim) + RMS_EPS)


def _sigmoid(x):
    return 0.5 * jnp.tanh(0.5 * x) + 0.5


def _lru_input_scale(log_a, a):
    return jnp.sqrt(-jnp.tanh(log_a) * (1.0 + a * a))


def _log_sigmoid(x):
    z = jnp.exp(-jnp.abs(x))
    u = 1.0 + z
    log1p_z = jnp.where(u == 1.0, z, jnp.log(u) * (z / (u - 1.0)))
    return jnp.minimum(x, 0.0) - log1p_z


def _row_iota(d):
    return lax.broadcasted_iota(jnp.int32, (SUBLANES, d), 0)


def _shift_down(cur, prev, s, row):
    return jnp.where(row >= s, pltpu.roll(cur, s, axis=0), pltpu.roll(prev, s, axis=0))


def _shift_up(cur, nxt, s, row):
    k = SUBLANES - s
    return jnp.where(row < k, pltpu.roll(cur, k, axis=0), pltpu.roll(nxt, k, axis=0))


def _scan_fwd(a, b, h_prev, row):
    for s in (1, 2, 4):
        a_s = jnp.where(row >= s, pltpu.roll(a, s, axis=0), 1.0)
        b_s = jnp.where(row >= s, pltpu.roll(b, s, axis=0), 0.0)
        b = a * b_s + b
        a = a * a_s
    return a * h_prev + b


def _scan_bwd(a_next, b, g_next, row):
    a = a_next
    for s in (1, 2, 4):
        k = SUBLANES - s
        a_s = jnp.where(row < k, pltpu.roll(a, k, axis=0), 1.0)
        b_s = jnp.where(row < k, pltpu.roll(b, k, axis=0), 0.0)
        b = a * b_s + b
        a = a * a_s
    return a * g_next + b


def _bcast_row(v, r):
    return jnp.broadcast_to(v[r:r + 1, :], v.shape)


def _chunks(n_rows, rc, body, init, reverse=False):
    n = n_rows // rc

    def step(i, carry):
        j = (n - 1 - i) if reverse else i
        return body(pl.multiple_of(j * rc, rc), carry)

    return lax.fori_loop(0, n, step, init)


def _adamw(w, g, m, v):
    m = ADAM_B1 * m + (1.0 - ADAM_B1) * g
    v = ADAM_B2 * v + (1.0 - ADAM_B2) * (g * g)
    m_hat = m / ADAM_BC1
    v_hat = v / ADAM_BC2
    delta = -ADAM_LR * (m_hat / (jnp.sqrt(v_hat) + ADAM_EPS) + ADAM_WD * w)
    return delta, m, v


def _mesh_pos():
    return lax.axis_index("x"), lax.axis_index("y"), lax.axis_index("c")


class _Gather:
    def __init__(self, blocks_of, send_sems, recv_sems, own_src=None):
        x, y, c = _mesh_pos()
        self.c = c
        self.me, self.sibling = (x, y, c), (x, y, 1 - c)
        self.chips = [(1 - x, y), (x, 1 - y), (1 - x, 1 - y)]
        self.blocks_of, self.send_sems, self.recv_sems = blocks_of, send_sems, recv_sems
        self.own_src = own_src

    def copy(self, a, k, block, to):
        src = self.blocks_of(a, *block)
        if block is self.me and self.own_src is not None:
            src = self.own_src[a]
        return pltpu.make_async_remote_copy(
            src_ref=src, dst_ref=self.blocks_of(a, *block),
            send_sem=self.send_sems.at[a * 7 + k], recv_sem=self.recv_sems.at[a * 7 + k],
            device_id=to, device_id_type=MESH)

    def start_own(self, a):
        self.copy(a, 0, self.me, self.sibling).start()
        for j, chip in enumerate(self.chips):
            self.copy(a, 1 + j, self.me, (*chip, self.c)).start()

    def wait_sibling(self, a):
        self.copy(a, 0, self.sibling, self.me).wait_recv()

    def wait_chip_and_pass_on(self, a, j):
        block = (*self.chips[j], self.c)
        self.copy(a, 1 + j, block, self.me).wait_recv()
        self.copy(a, 4 + j, block, self.sibling).start()

    def wait_passed_on(self, a, j):
        self.copy(a, 4 + j, (*self.chips[j], 1 - self.c), self.me).wait_recv()

    def wait_sends(self, a):
        self.copy(a, 0, self.me, self.sibling).wait_send()
        for j, chip in enumerate(self.chips):
            self.copy(a, 1 + j, self.me, (*chip, self.c)).wait_send()
            self.copy(a, 4 + j, (*chip, self.c), self.sibling).wait_send()

    def finish(self, a):
        for j in range(3):
            self.wait_chip_and_pass_on(a, j)
        self.wait_sibling(a)
        for j in range(3):
            self.wait_passed_on(a, j)
        self.wait_sends(a)


class _BalancedGather:
    def __init__(self, slot, send_sems, recv_sems, own_src):
        x, y, c = _mesh_pos()
        self.c = c
        self.me, self.sibling = (x, y, c), (x, y, 1 - c)
        self.chips = [(1 - x, y), (x, 1 - y), (1 - x, 1 - y)]
        self.slot, self.send_sems, self.recv_sems, self.own_src = slot, send_sems, recv_sems, own_src

    def half(self, a, block, which):
        ref = self.slot(a, *block)
        n = ref.shape[0] // 2
        return ref.at[pl.ds(which * n, n)]

    def copy(self, a, k, src, dst, to):
        return pltpu.make_async_remote_copy(
            src_ref=src, dst_ref=dst, send_sem=self.send_sems.at[a * 8 + k], recv_sem=self.recv_sems.at[a * 8 + k],
            device_id=to, device_id_type=MESH)

    def whole(self, a, k, block, to):
        src = self.own_src[a] if block is self.me else self.slot(a, *block)
        return self.copy(a, k, src, self.slot(a, *block), to)

    def halved(self, a, k, block, which, to):
        return self.copy(a, k, self.half(a, block, which), self.half(a, block, which), to)

    def on(self, chip):
        return (*self.chips[chip], self.c)

    def start_own(self, a):
        self.whole(a, 0, self.me, self.sibling).start()
        self.whole(a, 1, self.me, self.on(0)).start()
        self.whole(a, 2, self.me, self.on(1)).start()

    def wait_sibling(self, a):
        self.whole(a, 0, self.sibling, self.me).wait_recv()

    def on_neighbour(self, a, j):
        self.whole(a, 1 + j, self.on(j), self.me).wait_recv()
        self.halved(a, 3 + j, self.on(j), j, self.on(1 - j)).start()
        self.whole(a, 5 + j, self.on(j), self.sibling).start()

    def on_diagonal(self, a):
        self.halved(a, 3, self.on(2), 0, self.me).wait_recv()
        self.halved(a, 4, self.on(2), 1, self.me).wait_recv()
        self.whole(a, 7, self.on(2), self.sibling).start()

    def wait_passed_on(self, a, j):
        self.whole(a, 5 + j, (*self.chips[j], 1 - self.c), self.me).wait_recv()

    def wait_sends(self, a):
        self.whole(a, 0, self.me, self.sibling).wait_send()
        for j in range(2):
            self.whole(a, 1 + j, self.me, self.on(j)).wait_send()
            self.halved(a, 3 + j, self.on(j), j, self.on(1 - j)).wait_send()
        for j in range(3):
            self.whole(a, 5 + j, self.on(j), self.sibling).wait_send()


def _block_order():
    x, y, c = _mesh_pos()
    chips = [(1 - x, y), (x, 1 - y), (1 - x, 1 - y)]
    idx = lambda px, py, pc: 4 * px + 2 * py + pc
    order = [idx(x, y, c), idx(x, y, 1 - c)] + [idx(*ch, c) for ch in chips] + [idx(*ch, 1 - c) for ch in chips]
    return jnp.stack(order).astype(jnp.int32)


def _gather_project(x, w_in, w_out, conv_pack, ln_g, tb):
    t_len, d = x.shape
    nb = t_len // tb
    cols = w_in.shape[1]
    mc = min(512, t_len)
    srcs = (w_in, w_out, conv_pack)
    dts = (MM, MM, F32)

    def body(order_ref, x_ref, win_ref, wout_ref, cp_ref, lng_ref, p_ref, xnt_ref, win_all, wout_all, cp_all,
             xnb, st_in, st_out, st_cp, wbuf, send_sems, recv_sems, cp_send, cp_recv, local_sems):
        i = pl.program_id(0)
        x_, y_, c_ = _mesh_pos()
        me = 4 * x_ + 2 * y_ + c_
        outs = (win_all, wout_all, cp_all)
        stages = (st_in, st_out, st_cp)
        gather = _BalancedGather(lambda a, px, py, pc: outs[a].at[4 * px + 2 * py + pc], send_sems, recv_sems, stages)
        small = _Gather(lambda a, px, py, pc: cp_all.at[4 * px + 2 * py + pc], cp_send, cp_recv, own_src=[st_cp])
        keep_own = [pltpu.make_async_copy(stages[a], outs[a].at[me], local_sems.at[a]) for a in range(3)]

        @pl.when(i == 0)
        def _():
            for a, (src, dst) in enumerate(zip((win_ref, wout_ref, cp_ref), stages)):
                rows = src.shape[0]
                rc = min(rows, 32)

                def cast(r, carry, src=src, dst=dst, rc=rc):
                    dst[pl.ds(r, rc), :] = src[pl.ds(r, rc), :].astype(dst.dtype)
                    return carry

                _chunks(rows, rc, cast, 0)
                if a < 2:
                    gather.start_own(a)
                else:
                    small.start_own(0)
                keep_own[a].start()

        @pl.when(i < nb)
        def _():
            xv = x_ref[...]
            r0 = lax.rsqrt(jnp.mean(xv * xv, axis=-1, keepdims=True) + RMS_EPS)
            xn = xv * r0 * lng_ref[...]
            xnb[pl.ds(pl.multiple_of(i * tb, tb), tb), :] = xn.astype(MM)
            xnt_ref[...] = xn.T.astype(MM)

        for k in range(N_DEV):
            @pl.when(i == nb + k)
            def _(k=k):
                if k == 1:
                    gather.wait_sibling(0)
                elif k == 2:
                    gather.on_neighbour(0, 0)
                elif k == 3:
                    gather.on_neighbour(0, 1)
                    gather.on_neighbour(1, 0)
                elif k == 4:
                    gather.on_diagonal(0)
                    gather.on_neighbour(1, 1)
                elif k >= 5:
                    gather.wait_passed_on(0, k - 5)
                    if k == 5:
                        gather.on_diagonal(1)
                if k == 0:
                    w_blk = st_in
                else:
                    load = pltpu.make_async_copy(win_all.at[order_ref[k]], wbuf, local_sems.at[3])
                    load.start()
                    load.wait()
                    w_blk = wbuf

                def project(r, carry):
                    rows = pl.ds(r, mc)
                    p_ref[rows, :] = _dot(xnb[rows, :], w_blk[...]).astype(MM)
                    return carry

                _chunks(t_len, mc, project, 0)
                if k == N_DEV - 1:
                    gather.wait_sends(0)
                    gather.wait_sibling(1)
                    for j in range(3):
                        gather.wait_passed_on(1, j)
                    gather.wait_sends(1)
                    small.finish(0)
                    for cp in keep_own:
                        cp.wait()

    vm = pl.BlockSpec(memory_space=pltpu.VMEM)
    hbm = pl.BlockSpec(memory_space=pl.ANY)
    grid_spec = pltpu.PrefetchScalarGridSpec(
        num_scalar_prefetch=1, grid=(nb + N_DEV,),
        in_specs=[pl.BlockSpec((tb, d), lambda i, o: (jnp.minimum(i, nb - 1), 0)), vm, vm, vm, vm],
        out_specs=(pl.BlockSpec((t_len, cols), lambda i, o: (0, o[jnp.maximum(i - nb, 0)])),
                   pl.BlockSpec((d, tb), lambda i, o: (0, jnp.minimum(i, nb - 1))), hbm, hbm, hbm),
        scratch_shapes=[pltpu.VMEM((t_len, d), MM)] + [pltpu.VMEM(s.shape, dt) for s, dt in zip(srcs, dts)]
                       + [pltpu.VMEM(w_in.shape, MM),
                          pltpu.SemaphoreType.DMA((16,)), pltpu.SemaphoreType.DMA((16,)),
                          pltpu.SemaphoreType.DMA((7,)), pltpu.SemaphoreType.DMA((7,)), pltpu.SemaphoreType.DMA((4,))])
    return pl.pallas_call(
        body, name="gather_project", grid_spec=grid_spec,
        out_shape=(jax.ShapeDtypeStruct((t_len, N_DEV * cols), MM),
                   jax.ShapeDtypeStruct((d, t_len), MM))
                  + tuple(jax.ShapeDtypeStruct((N_DEV,) + s.shape, dt) for s, dt in zip(srcs, dts)),
        compiler_params=_params(dimension_semantics=("arbitrary",)),
    )(_block_order(), x, w_in, w_out, conv_pack, ln_g)


def _forward(x, tgt, p, wout, wa_t, wi_t, sp, ones_c, ones_l, tb):
    t_len, d = x.shape
    nb = t_len // tb
    n_tiles, tw = wa_t.shape[0], wa_t.shape[1]
    hd_c, hd_l = d // N_CONV_HEADS, d // N_LRU_HEADS
    s8 = SUBLANES

    def body(x_ref, tgt_ref, p_ref, wout_ref, wa_ref, wi_ref, sp_ref, oc_ref, ol_ref,
             h_ref, dh_ref, dhb_ref, acc_ref,
             yc, u, pa, pi, rcf, rlf, ybuf, tail_z, tail_xl, hcar):
        i = pl.program_id(0)
        row = _row_iota(d)

        @pl.when(i == 0)
        def _():
            tail_z[...] = jnp.zeros_like(tail_z)
            tail_xl[...] = jnp.zeros_like(tail_xl)
            hcar[...] = jnp.zeros_like(hcar)
            acc_ref[...] = jnp.zeros_like(acc_ref)

        def spr(r):
            return sp_ref[r:r + 1, :]

        def proj(rows, seg):
            return p_ref[rows, seg * d:(seg + 1) * d].astype(F32)

        w0, w1, w2 = spr(SP_CONV_W), spr(SP_CONV_W + 1), spr(SP_CONV_W + 2)
        l0, l1, l2, l3 = spr(SP_LRU_W), spr(SP_LRU_W + 1), spr(SP_LRU_W + 2), spr(SP_LRU_W + 3)
        lb = spr(SP_LRU_B)

        def convs(r, carry):
            zp, xp = carry
            rows16 = pl.ds(r, 2 * s8)
            bg16, xl16 = proj(rows16, P_B), proj(rows16, P_XL)
            z16 = proj(rows16, P_C) * proj(rows16, P_XC)
            for j in range(2):
                rows, sub = pl.ds(r + j * s8, s8), slice(j * s8, (j + 1) * s8)
                z, xl = z16[sub], xl16[sub]
                cz = w0 * _shift_down(z, zp, 2, row) + w1 * _shift_down(z, zp, 1, row) + w2 * z
                yc[rows, :] = bg16[sub] * cz
                u[rows, :] = (l0 * _shift_down(xl, xp, 3, row) + l1 * _shift_down(xl, xp, 2, row)
                              + l2 * _shift_down(xl, xp, 1, row) + l3 * xl + lb)
                zp, xp = z, xl
            return zp, xp

        z_last, xl_last = _chunks(tb, 2 * s8, convs, (tail_z[...], tail_xl[...]))
        tail_z[...] = z_last
        tail_xl[...] = xl_last

        ub = u[...].astype(MM)
        for k in range(n_tiles):
            sl = slice(k * tw, (k + 1) * tw)
            pa[:, sl] = _dot(ub[:, sl], wa_ref[k])
            pi[:, sl] = _dot(ub[:, sl], wi_ref[k])
        rcf[...] = _head_rstd(yc[...], oc_ref[...], hd_c)

        c8 = RG_LRU_C * _log_sigmoid(spr(SP_LAM))
        b_a, b_i = spr(SP_B_A), spr(SP_B_I)

        def lru(r, hp):
            rows = pl.ds(r, SUBLANES)
            ra = _sigmoid(pa[rows, :] + b_a)
            ii = _sigmoid(pi[rows, :] + b_i)
            la = ra * c8
            a = jnp.exp(la)
            mult = _lru_input_scale(la, a)
            h = _scan_fwd(a, mult * (ii * u[rows, :]), hp, row)
            h_ref[rows, :] = h
            return _bcast_row(h, SUBLANES - 1)

        hcar[...] = _chunks(tb, SUBLANES, lru, hcar[...])
        rlf[...] = _head_rstd(h_ref[...], ol_ref[...], hd_l)

        g_c, g_l = spr(SP_CONV_G), spr(SP_LRU_G)

        def gate(r, carry):
            rows = pl.ds(r, 2 * s8)
            gc, gl = proj(rows, P_GC), proj(rows, P_GL)
            ybuf[rows, 0:d] = (yc[rows, :] * rcf[rows, :] * g_c * (gc * _sigmoid(gc))).astype(MM)
            ybuf[rows, d:2 * d] = (h_ref[rows, :] * rlf[rows, :] * g_l * (gl * _sigmoid(gl))).astype(MM)
            return carry

        _chunks(tb, 2 * s8, gate, 0)

        hres = x_ref[...] + _dot(ybuf[...], wout_ref[...])
        rf = lax.rsqrt(jnp.mean(hres * hres, axis=-1, keepdims=True) + RMS_EPS)
        hn = hres * rf
        fg = spr(SP_FINAL_G)
        err = hn * fg - tgt_ref[...]
        dout = err * (1.0 / d)
        acc_ref[0:SUBLANES, :] += (err * err).reshape(tb // SUBLANES, SUBLANES, d).sum(axis=0)
        acc_ref[SUBLANES:2 * SUBLANES, :] += (dout * hn).reshape(tb // SUBLANES, SUBLANES, d).sum(axis=0)
        gd = dout * fg
        dhres = rf * (gd - hn * jnp.mean(gd * hn, axis=-1, keepdims=True))
        dh_ref[...] = dhres
        dhb_ref[...] = dhres.astype(MM)

    vm = pl.BlockSpec(memory_space=pltpu.VMEM)
    blk = lambda w: pl.BlockSpec((tb, w), lambda i: (i, 0))
    buf = pltpu.VMEM((tb, d), F32)
    car = pltpu.VMEM((SUBLANES, d), F32)
    return pl.pallas_call(
        body, name="forward", grid=(nb,),
        in_specs=[blk(d), blk(d), blk(6 * d), vm, vm, vm, vm, vm, vm],
        out_specs=(blk(d), blk(d), blk(d), pl.BlockSpec((2 * SUBLANES, d), lambda i: (0, 0))),
        out_shape=(jax.ShapeDtypeStruct((t_len, d), F32),
                   jax.ShapeDtypeStruct((t_len, d), F32),
                   jax.ShapeDtypeStruct((t_len, d), MM),
                   jax.ShapeDtypeStruct((2 * SUBLANES, d), F32)),
        scratch_shapes=[buf] * 6 + [pltpu.VMEM((tb, 2 * d), MM), car, car, car],
        compiler_params=_params(dimension_semantics=("arbitrary",)),
    )(x, tgt, p, wout, wa_t, wi_t, sp, ones_c, ones_l)


def _backward(p, h, dh, wout, wa_t, wi_t, sp, ones_c, ones_l, tb):
    t_len, d = h.shape
    nb = t_len // tb
    n_tiles, tw = wa_t.shape[0], wa_t.shape[1]
    hd_c, hd_l = d // N_CONV_HEADS, d // N_LRU_HEADS
    s8 = SUBLANES
    ph = 2 * s8

    def body(p_ref, phalo_ref, h_ref, hhalo_ref, dh_ref, wout_ref, wa_ref, wi_ref, sp_ref, oc_ref, ol_ref,
             dp_ref, yt_ref, gwa_ref, gwi_ref, acc_ref,
             hh, dy, ybuf, yc, czs, u, pa, pi, rcf, rlf, qc, ql, dyc_hat, dyl_hat, dpa, dpi, du,
             car_dcz, car_a, car_g, car_du):
        i = pl.program_id(0)
        blk_idx = nb - 1 - i
        row = _row_iota(d)

        @pl.when(i == 0)
        def _():
            for ref in (car_dcz, car_a, car_g, car_du, gwa_ref, gwi_ref, acc_ref):
                ref[...] = jnp.zeros_like(ref)

        def spr(r):
            return sp_ref[r:r + 1, :]

        def proj(rows, seg):
            return p_ref[rows, seg * d:(seg + 1) * d].astype(F32)

        def put(rows, seg, halves):
            dp_ref[rows, seg * d:(seg + 1) * d] = jnp.concatenate(halves, axis=0).astype(MM)

        def acc_add(group, val):
            acc_ref[group * s8:(group + 1) * s8, :] += val

        live = jnp.where(blk_idx > 0, 1.0, 0.0).astype(F32)
        hh[0:s8, :] = hhalo_ref[...] * live
        hh[s8:, :] = h_ref[...]

        dy[...] = _dot_nt(dh_ref[...].astype(MM), wout_ref[...])

        w0, w1, w2 = spr(SP_CONV_W), spr(SP_CONV_W + 1), spr(SP_CONV_W + 2)
        l0, l1, l2, l3 = spr(SP_LRU_W), spr(SP_LRU_W + 1), spr(SP_LRU_W + 2), spr(SP_LRU_W + 3)
        lb = spr(SP_LRU_B)

        def recompute(r, carry):
            zp, xp = carry
            rows16 = pl.ds(r, 2 * s8)
            bg16, xl16 = proj(rows16, P_B), proj(rows16, P_XL)
            z16 = proj(rows16, P_C) * proj(rows16, P_XC)
            for j in range(2):
                rows, sub = pl.ds(r + j * s8, s8), slice(j * s8, (j + 1) * s8)
                z, xl = z16[sub], xl16[sub]
                cz = w0 * _shift_down(z, zp, 2, row) + w1 * _shift_down(z, zp, 1, row) + w2 * z
                czs[rows, :] = cz
                yc[rows, :] = bg16[sub] * cz
                u[rows, :] = (l0 * _shift_down(xl, xp, 3, row) + l1 * _shift_down(xl, xp, 2, row)
                              + l2 * _shift_down(xl, xp, 1, row) + l3 * xl + lb)
                zp, xp = z, xl
            return zp, xp

        halo = lambda seg: phalo_ref[:, seg * d:(seg + 1) * d].astype(F32)[s8:, :] * live
        _chunks(tb, 2 * s8, recompute, (halo(P_C) * halo(P_XC), halo(P_XL)))

        ub = u[...].astype(MM)
        for k in range(n_tiles):
            sl = slice(k * tw, (k + 1) * tw)
            pa[:, sl] = _dot(ub[:, sl], wa_ref[k])
            pi[:, sl] = _dot(ub[:, sl], wi_ref[k])
        rcf[...] = _head_rstd(yc[...], oc_ref[...], hd_c)
        rlf[...] = _head_rstd(h_ref[...], ol_ref[...], hd_l)

        g_c, g_l = spr(SP_CONV_G), spr(SP_LRU_G)

        def gates(r, carry):
            rows = pl.ds(r, 2 * s8)
            for (seg, off_y, src, rstd, gain, q, dhat, grp) in (
                    (P_GC, 0, yc, rcf, g_c, qc, dyc_hat, A_CONV_G),
                    (P_GL, d, h_ref, rlf, g_l, ql, dyl_hat, A_LRU_G)):
                gt = proj(rows, seg)
                sg = _sigmoid(gt)
                silu = gt * sg
                yhat = src[rows, :] * rstd[rows, :]
                nrm = yhat * gain
                ybuf[rows, off_y:off_y + d] = nrm * silu
                dout = dy[rows, off_y:off_y + d]
                dnrm = dout * silu
                dp_ref[rows, seg * d:(seg + 1) * d] = (dout * nrm * (sg * (1.0 + gt * (1.0 - sg)))).astype(MM)
                dg = dnrm * yhat
                acc_add(grp, dg[0:s8] + dg[s8:])
                dh_ = dnrm * gain
                dhat[rows, :] = dh_
                q[rows, :] = dh_ * yhat
            return carry

        _chunks(tb, 2 * s8, gates, 0)

        qc[...] = _head_sums(qc[...], oc_ref[...]) * (1.0 / hd_c)
        ql[...] = _head_sums(ql[...], ol_ref[...]) * (1.0 / hd_l)
        yt_ref[...] = ybuf[...].T.astype(MM)

        c8 = RG_LRU_C * _log_sigmoid(spr(SP_LAM))
        b_a, b_i = spr(SP_B_A), spr(SP_B_I)

        def mixers(r, carry):
            dcz_n, a_n, g_n = carry
            rows16 = pl.ds(r, 2 * s8)
            bg16, cg16, xc16 = proj(rows16, P_B), proj(rows16, P_C), proj(rows16, P_XC)
            z16 = cg16 * xc16
            d_b, d_c, d_x = [None, None], [None, None], [None, None]
            for j in (1, 0):
                rows, sub = pl.ds(r + j * s8, s8), slice(j * s8, (j + 1) * s8)
                rstd = rcf[rows, :]
                yhat = yc[rows, :] * rstd
                dyc = rstd * (dyc_hat[rows, :] - yhat * qc[rows, :])
                d_b[j] = dyc * czs[rows, :]
                dcz = dyc * bg16[sub]
                up1, up2 = _shift_up(dcz, dcz_n, 1, row), _shift_up(dcz, dcz_n, 2, row)
                dz = w2 * dcz + w1 * up1 + w0 * up2
                d_c[j] = dz * xc16[sub]
                d_x[j] = dz * cg16[sub]
                z = z16[sub]
                acc_add(A_CONV_W, up2 * z)
                acc_add(A_CONV_W + 1, up1 * z)
                acc_add(A_CONV_W + 2, dcz * z)
                rstd = rlf[rows, :]
                hcur = hh[pl.ds(r + (j + 1) * s8, s8), :]
                hhat = hcur * rstd
                dh_out = rstd * (dyl_hat[rows, :] - hhat * ql[rows, :])
                ra = _sigmoid(pa[rows, :] + b_a)
                la = ra * c8
                a = jnp.exp(la)
                g = _scan_bwd(_shift_up(a, a_n, 1, row), dh_out, g_n, row)
                da = g * _shift_down(hcur, hh[pl.ds(r + j * s8, s8), :], 1, row)
                ii = _sigmoid(pi[rows, :] + b_i)
                uu = u[rows, :]
                mult = _lru_input_scale(la, a)
                dmult = g * (ii * uu)
                ds = g * mult
                dla = a * (da - dmult * a / mult)
                acc_add(A_LAM, dla * ra)
                dpa_ = dla * c8 * ra * (1.0 - ra)
                dpi_ = ds * uu * ii * (1.0 - ii)
                acc_add(A_B_A, dpa_)
                acc_add(A_B_I, dpi_)
                dpa[rows, :] = dpa_
                dpi[rows, :] = dpi_
                du[rows, :] = ds * ii
                dcz_n, a_n, g_n = dcz, a, _bcast_row(g, 0)
            put(rows16, P_B, d_b)
            put(rows16, P_C, d_c)
            put(rows16, P_XC, d_x)
            return dcz_n, a_n, g_n

        dcz_f, a_f, g_f = _chunks(tb, 2 * s8, mixers, (car_dcz[...], car_a[...], car_g[...]), reverse=True)
        car_dcz[...] = dcz_f
        car_a[...] = a_f
        car_g[...] = g_f

        dpab = dpa[...].astype(MM)
        dpib = dpi[...].astype(MM)
        for k in range(n_tiles):
            sl = slice(k * tw, (k + 1) * tw)
            du[:, sl] += _dot_nt(dpab[:, sl], wa_ref[k]) + _dot_nt(dpib[:, sl], wi_ref[k])
            ut = u[:, sl].T.astype(MM)
            gwa_ref[k] += _dot(ut, dpab[:, sl])
            gwi_ref[k] += _dot(ut, dpib[:, sl])

        def lru_conv(r, du_n):
            rows16 = pl.ds(r, 2 * s8)
            xl16 = proj(rows16, P_XL)
            d_xl = [None, None]
            for j in (1, 0):
                rows, sub = pl.ds(r + j * s8, s8), slice(j * s8, (j + 1) * s8)
                dut = du[rows, :]
                up1, up2, up3 = (_shift_up(dut, du_n, s, row) for s in (1, 2, 3))
                d_xl[j] = l3 * dut + l2 * up1 + l1 * up2 + l0 * up3
                xl = xl16[sub]
                acc_add(A_LRU_W, up3 * xl)
                acc_add(A_LRU_W + 1, up2 * xl)
                acc_add(A_LRU_W + 2, up1 * xl)
                acc_add(A_LRU_W + 3, dut * xl)
                acc_add(A_LRU_B, dut)
                du_n = dut
            put(rows16, P_XL, d_xl)
            return du_n

        car_du[...] = _chunks(tb, 2 * s8, lru_conv, car_du[...], reverse=True)

    vm = pl.BlockSpec(memory_space=pltpu.VMEM)
    rev = lambda w: pl.BlockSpec((tb, w), lambda i: (nb - 1 - i, 0))
    halo = lambda rows, w: pl.BlockSpec((rows, w), lambda i: (jnp.maximum((nb - 1 - i) * (tb // rows) - 1, 0), 0))
    const = lambda shape: pl.BlockSpec(shape, lambda i: (0,) * len(shape))
    buf = lambda w: pltpu.VMEM((tb, w), F32)
    car = pltpu.VMEM((SUBLANES, d), F32)
    return pl.pallas_call(
        body, name="backward", grid=(nb,),
        in_specs=[rev(6 * d), halo(ph, 6 * d), rev(d), halo(SUBLANES, d), rev(d), vm, vm, vm, vm, vm, vm],
        out_specs=(rev(6 * d), pl.BlockSpec((2 * d, tb), lambda i: (0, nb - 1 - i)),
                   const((n_tiles, tw, tw)), const((n_tiles, tw, tw)), const((A_GROUPS * SUBLANES, d))),
        out_shape=(jax.ShapeDtypeStruct((t_len, 6 * d), MM),
                   jax.ShapeDtypeStruct((2 * d, t_len), MM),
                   jax.ShapeDtypeStruct((n_tiles, tw, tw), F32),
                   jax.ShapeDtypeStruct((n_tiles, tw, tw), F32),
                   jax.ShapeDtypeStruct((A_GROUPS * SUBLANES, d), F32)),
        scratch_shapes=[pltpu.VMEM((SUBLANES + tb, d), F32), buf(2 * d), buf(2 * d)] + [buf(d)] * 14 + [car, car, car, car],
        compiler_params=_params(dimension_semantics=("arbitrary",)),
    )(p, p, h, h, dh, wout, wa_t, wi_t, sp, ones_c, ones_l)


def _input_grad(dp, win_all, x, dh, sp, parts, tb):
    t_len, d = x.shape
    nb = t_len // tb
    cols = win_all.shape[2]
    n_parts = len(parts)

    def body(dp_ref, win_ref, x_ref, dh_ref, sp_ref, *refs):
        gx_ref, acc_ref = refs[n_parts:n_parts + 2]
        exchange = _ChipExchange(refs[:n_parts], refs[n_parts + 2:2 * n_parts + 2], *refs[2 * n_parts + 2:])
        i = pl.program_id(0)

        @pl.when(i == 0)
        def _():
            acc_ref[...] = jnp.zeros_like(acc_ref)
            exchange.start()

        dxn = _dot_nt(dp_ref[:, 0:cols], win_ref[0])
        for j in range(1, N_DEV):
            dxn += _dot_nt(dp_ref[:, j * cols:(j + 1) * cols], win_ref[j])
        xv = x_ref[...]
        r0 = lax.rsqrt(jnp.mean(xv * xv, axis=-1, keepdims=True) + RMS_EPS)
        xhat = xv * r0
        acc_ref[...] += (dxn * xhat).reshape(tb // SUBLANES, SUBLANES, d).sum(axis=0)
        dxh = dxn * sp_ref[SP_LN_G:SP_LN_G + 1, :]
        gx_ref[...] = dh_ref[...] + r0 * (dxh - xhat * jnp.mean(dxh * xhat, axis=-1, keepdims=True))

        @pl.when(i == nb - 1)
        def _():
            exchange.finish()

    vm = pl.BlockSpec(memory_space=pltpu.VMEM)
    hbm = pl.BlockSpec(memory_space=pl.ANY)
    blk = lambda w: pl.BlockSpec((tb, w), lambda i: (i, 0))
    outs = pl.pallas_call(
        body, name="input_grad", grid=(nb,),
        in_specs=[blk(6 * d), vm, blk(d), blk(d), vm] + [hbm] * n_parts,
        out_specs=(blk(d), pl.BlockSpec((SUBLANES, d), lambda i: (0, 0))) + (hbm,) * n_parts,
        out_shape=(jax.ShapeDtypeStruct((t_len, d), F32), jax.ShapeDtypeStruct((SUBLANES, d), F32))
                  + tuple(jax.ShapeDtypeStruct(p.shape, p.dtype) for p in parts),
        scratch_shapes=[pltpu.SemaphoreType.DMA((3 * n_parts,)), pltpu.SemaphoreType.DMA((3 * n_parts,))],
        compiler_params=_params(dimension_semantics=("arbitrary",)),
    )(dp, win_all, x, dh, sp, *parts)
    return outs[0], outs[1], outs[2:]


_CHIP_RELATIONS = [(0, 0), (1, 0), (0, 1), (1, 1)]


def _related_block(k, core):
    x, y, _ = _mesh_pos()
    fx, fy = _CHIP_RELATIONS[k]
    return 4 * (x ^ fx) + 2 * (y ^ fy) + core


class _ChipExchange:
    def __init__(self, part_refs, land_refs, send_sems, recv_sems):
        self.part_refs, self.land_refs, self.send_sems, self.recv_sems = part_refs, land_refs, send_sems, recv_sems

    def copies(self):
        x, y, c = _mesh_pos()
        for a in range(len(self.part_refs)):
            for k in (1, 2, 3):
                fx, fy = _CHIP_RELATIONS[k]
                yield pltpu.make_async_remote_copy(
                    src_ref=self.part_refs[a].at[k - 1], dst_ref=self.land_refs[a].at[k - 1],
                    send_sem=self.send_sems.at[3 * a + k - 1], recv_sem=self.recv_sems.at[3 * a + k - 1],
                    device_id=(x ^ fx, y ^ fy, c), device_id_type=MESH)

    def start(self):
        for cp in self.copies():
            cp.start()

    def finish(self):
        for cp in self.copies():
            cp.wait_recv()
        for cp in self.copies():
            cp.wait_send()


def _weight_grad_stage1(name, blk_shape, n_split, operands, in_specs, product, riders=()):
    n_rows, n_cols = blk_shape
    rs = n_rows // n_split
    rc = 32
    n_in, n_ride = len(operands), len(riders)
    _, _, c = _mesh_pos()
    order = jnp.stack([_related_block(k, 1 - c) for k in range(4)]
                      + [_related_block(k, c) for k in (1, 2, 3, 0)]).astype(jnp.int32)

    def body(order_ref, *refs):
        ins = refs[:n_in]
        ride_in = refs[n_in:n_in + n_ride]
        part_ref, own_ref, from_sib = refs[n_in + n_ride:n_in + n_ride + 3]
        ride_out = refs[n_in + n_ride + 3:n_in + 2 * n_ride + 3]
        gbuf, sendbuf, tmp, send_sems, recv_sems, local_sems, ride_send, ride_recv = refs[n_in + 2 * n_ride + 3:]
        exchange = _ChipExchange(ride_in, ride_out, ride_send, ride_recv)
        s = pl.program_id(0)
        x, y, c = _mesh_pos()

        def to_sibling(k):
            return pltpu.make_async_remote_copy(
                src_ref=sendbuf.at[k], dst_ref=from_sib.at[k], send_sem=send_sems.at[k], recv_sem=recv_sems.at[k],
                device_id=(x, y, 1 - c), device_id_type=MESH)

        if n_ride:
            @pl.when(s == 0)
            def _():
                exchange.start()

        for h in range(n_split):
            gbuf[h * rs:(h + 1) * rs, :] = product(ins, h)

        @pl.when(s < 4)
        def _():
            def narrow(r, carry):
                sendbuf[s, pl.ds(r, rc), :] = gbuf[pl.ds(r, rc), :].astype(MM)
                return carry

            _chunks(n_rows, rc, narrow, 0)
            to_sibling(s).start()

        @pl.when(s >= 4)
        def _():
            k = jnp.where(s == 7, 0, s - 3)
            to_sibling(k).wait_recv()
            load = pltpu.make_async_copy(from_sib.at[k], tmp, local_sems.at[0])
            load.start()
            load.wait()

            @pl.when(s < 7)
            def _():
                def add(r, carry):
                    rows = pl.ds(r, rc)
                    part_ref[0, rows, :] = (gbuf[rows, :] + tmp[rows, :].astype(F32)).astype(MM)
                    return carry

                _chunks(n_rows, rc, add, 0)

            @pl.when(s == 7)
            def _():
                def add(r, carry):
                    rows = pl.ds(r, rc)
                    own_ref[rows, :] = gbuf[rows, :] + tmp[rows, :].astype(F32)
                    return carry

                _chunks(n_rows, rc, add, 0)
                for kk in range(4):
                    to_sibling(kk).wait_send()
                if n_ride:
                    exchange.finish()

    hbm = pl.BlockSpec(memory_space=pl.ANY)
    grid_spec = pltpu.PrefetchScalarGridSpec(
        num_scalar_prefetch=1, grid=(N_DEV,), in_specs=list(in_specs) + [hbm] * n_ride,
        out_specs=(pl.BlockSpec((1, n_rows, n_cols), lambda s, o: (jnp.clip(s - 4, 0, 2), 0, 0)),
                   pl.BlockSpec((n_rows, n_cols), lambda s, o: (0, 0)), hbm) + (hbm,) * n_ride,
        scratch_shapes=[pltpu.VMEM((n_rows, n_cols), F32), pltpu.VMEM((4, n_rows, n_cols), MM),
                        pltpu.VMEM((n_rows, n_cols), MM),
                        pltpu.SemaphoreType.DMA((4,)), pltpu.SemaphoreType.DMA((4,)), pltpu.SemaphoreType.DMA((1,)),
                        pltpu.SemaphoreType.DMA((max(3 * n_ride, 1),)), pltpu.SemaphoreType.DMA((max(3 * n_ride, 1),))])
    outs = pl.pallas_call(
        body, name=name, grid_spec=grid_spec,
        out_shape=(jax.ShapeDtypeStruct((3, n_rows, n_cols), MM), jax.ShapeDtypeStruct((n_rows, n_cols), F32),
                   jax.ShapeDtypeStruct((4, n_rows, n_cols), MM))
                  + tuple(jax.ShapeDtypeStruct(p.shape, p.dtype) for p in riders),
        compiler_params=_params(dimension_semantics=("arbitrary",)),
    )(order, *operands, *riders)
    return outs[0], outs[1], outs[3:]


def _weight_grad_in(xnt, dp, riders):
    d, t_len = xnt.shape
    cols = dp.shape[1] // N_DEV
    half = d // 2
    return _weight_grad_stage1(
        "weight_grad_in", (d, cols), 2, (xnt, dp),
        [pl.BlockSpec(memory_space=pltpu.VMEM), pl.BlockSpec((t_len, cols), lambda s, o: (0, o[s]))],
        lambda refs, h: _dot(refs[0][h * half:(h + 1) * half, :], refs[1][...]), riders)


def _weight_grad_out(yt, dhb):
    d2, t_len = yt.shape
    d = dhb.shape[1]
    rows = d2 // N_DEV
    return _weight_grad_stage1(
        "weight_grad_out", (rows, d), 1, (yt, dhb),
        [pl.BlockSpec((rows, t_len), lambda s, o: (o[s], 0)), pl.BlockSpec(memory_space=pltpu.VMEM)],
        lambda refs, h: _dot(refs[0][...], refs[1][...]))


def _update_shard(own, from_chips, w, m, v, name):
    n_rows, n_cols = w.shape
    rb = min(256, n_rows)

    def body(own_ref, fc_ref, w_ref, m_ref, v_ref, grad_ref, delta_ref, mo_ref, vo_ref):
        g = own_ref[...]
        for k in range(3):
            g = g + fc_ref[k].astype(F32)
        delta, m_new, v_new = _adamw(w_ref[...], g, m_ref[...], v_ref[...])
        grad_ref[...] = g
        delta_ref[...] = delta
        mo_ref[...] = m_new
        vo_ref[...] = v_new

    blk = pl.BlockSpec((rb, n_cols), lambda i: (i, 0))
    out = jax.ShapeDtypeStruct((n_rows, n_cols), F32)
    return pl.pallas_call(
        body, name=name, grid=(n_rows // rb,),
        in_specs=[blk, pl.BlockSpec((3, rb, n_cols), lambda i: (0, i, 0)), blk, blk, blk],
        out_specs=(blk, blk, blk, blk), out_shape=(out, out, out, out),
        compiler_params=_params(dimension_semantics=("arbitrary",)),
    )(own, from_chips, w, m, v)


def _small_update(facc, xacc, bacc, gwa, gwi, lam, wsl, msl, vsl, cw, cm, cv):
    n_rows, d = wsl.shape
    s8 = SUBLANES
    cc = cw.shape[1]

    def body(facc_ref, xacc_ref, bacc_ref, gwa_ref, gwi_ref, lam_ref, w_ref, m_ref, v_ref, cw_ref, cm_ref, cv_ref,
             g_o, d_o, m_o, v_o, cg_o, cd_o, cm_o, cv_o, gat, send_sems, recv_sems):
        x, y, c = _mesh_pos()
        me = 4 * x + 2 * y + c

        def rowsum(ref, group):
            return jnp.sum(ref[group * s8:(group + 1) * s8, :], axis=0, keepdims=True)

        mine = gat.at[me]
        loss = jnp.sum(rowsum(facc_ref, 0), axis=1, keepdims=True) * (0.5 / d)
        mine[SL_LOSS:SL_LOSS + 1, :] = jnp.broadcast_to(loss, (1, d))
        mine[SL_LN_G:SL_LN_G + 1, :] = rowsum(xacc_ref, 0)
        mine[SL_LRU_B:SL_LRU_B + 1, :] = rowsum(bacc_ref, A_LRU_B)
        mine[SL_B_A:SL_B_A + 1, :] = rowsum(bacc_ref, A_B_A)
        mine[SL_B_I:SL_B_I + 1, :] = rowsum(bacc_ref, A_B_I)
        mine[SL_LAM:SL_LAM + 1, :] = rowsum(bacc_ref, A_LAM) * (RG_LRU_C * jax.nn.sigmoid(-lam_ref[...]))
        mine[SL_CONV_G:SL_CONV_G + 1, :] = rowsum(bacc_ref, A_CONV_G)
        mine[SL_LRU_G:SL_LRU_G + 1, :] = rowsum(bacc_ref, A_LRU_G)
        mine[SL_FINAL_G:SL_FINAL_G + 1, :] = rowsum(facc_ref, 1)
        for k in range(3):
            mine[SL_CONV_W + k:SL_CONV_W + k + 1, :] = rowsum(bacc_ref, A_CONV_W + k)
        for k in range(4):
            mine[SL_LRU_W + k:SL_LRU_W + k + 1, :] = rowsum(bacc_ref, A_LRU_W + k)
        na = gwa_ref.shape[0]
        mine[SL_W_A:SL_W_A + na, :] = gwa_ref[...]
        mine[SL_W_A + na:SL_W_A + 2 * na, :] = gwi_ref[...]

        gather = _Gather(lambda a, px, py, pc: gat.at[4 * px + 2 * py + pc], send_sems, recv_sems)
        gather.start_own(0)
        gather.finish(0)

        def update(r, carry):
            rows = pl.ds(r, s8)
            g = gat[0, rows, :]
            for b in range(1, N_DEV):
                g = g + gat[b, rows, :]
            delta, m_new, v_new = _adamw(w_ref[rows, :], g, m_ref[rows, :], v_ref[rows, :])
            g_o[rows, :] = g
            d_o[rows, :] = delta
            m_o[rows, :] = m_new
            v_o[rows, :] = v_new
            return carry

        _chunks(n_rows, s8, update, 0)
        gc = g_o[s8:2 * s8, pl.ds(pl.multiple_of(me * cc, cc), cc)]
        delta, m_new, v_new = _adamw(cw_ref[...], gc, cm_ref[...], cv_ref[...])
        cg_o[...] = gc
        cd_o[...] = delta
        cm_o[...] = m_new
        cv_o[...] = v_new

    vm = pl.BlockSpec(memory_space=pltpu.VMEM)
    big = jax.ShapeDtypeStruct((n_rows, d), F32)
    small = jax.ShapeDtypeStruct(cw.shape, F32)
    return pl.pallas_call(
        body, name="small_update",
        in_specs=[vm] * 12, out_specs=(vm,) * 8, out_shape=(big, big, big, big, small, small, small, small),
        scratch_shapes=[pltpu.VMEM((N_DEV, n_rows, d), F32), pltpu.SemaphoreType.DMA((7,)), pltpu.SemaphoreType.DMA((7,))],
        compiler_params=_params(),
    )(facc, xacc, bacc, gwa, gwi, lam, wsl, msl, vsl, cw, cm, cv)


def _head_ones(head_dim, tw):
    lane = jnp.arange(tw) // head_dim
    return (lane[:, None] == lane[None, :]).astype(MM)


def _gate_tiles(w, tw):
    n_heads, hd, _ = w.shape
    per = tw // hd
    w4 = w.reshape(n_heads // per, per, hd, hd)
    eye = jnp.eye(per, dtype=w.dtype)
    return (w4[:, :, :, None, :] * eye[None, :, None, :, None]).reshape(n_heads // per, tw, tw)


def _gate_blocks(tiles, n_heads, hd):
    n_tiles, tw, _ = tiles.shape
    per = tw // hd
    t5 = tiles.reshape(n_tiles, per, hd, per, hd)
    diag = jnp.stack([t5[:, a, :, a, :] for a in range(per)], axis=1)
    return diag.reshape(hd, n_heads * hd)


def kernel(x, ln_g, w_in, conv_w, lru_conv_w, lru_conv_b, w_a, b_a, w_i, b_i, lam, conv_out_g, lru_out_g, w_out, final_g, loss_target, m_ln_g, m_w_in, m_conv_w, m_lru_conv_w, m_lru_conv_b, m_w_a, m_b_a, m_w_i, m_b_i, m_lam, m_conv_out_g, m_lru_out_g, m_w_out, m_final_g, v_ln_g, v_w_in, v_conv_w, v_lru_conv_w, v_lru_conv_b, v_w_a, v_b_a, v_w_i, v_b_i, v_lam, v_conv_out_g, v_lru_out_g, v_w_out, v_final_g):
    _, t_len, d = x.shape
    hd_l = d // N_LRU_HEADS
    tw = min(MXU_TILE, d)
    cc = conv_w.shape[1]
    x2, tgt2 = x[0], loss_target[0]

    def conv_rows(cw3, lw4):
        return jnp.concatenate([jnp.zeros((1, cc), F32), cw3, lw4], axis=0)

    p, xnt, win_all, wout_all, conv_all = _gather_project(
        x2, w_in, w_out, conv_rows(conv_w, lru_conv_w), ln_g.reshape(1, d), min(256, t_len))
    wout_full = wout_all.reshape(N_DEV * w_out.shape[0], d)
    conv_full = conv_all.transpose(1, 0, 2).reshape(SUBLANES, d)
    small = [ln_g, lru_conv_b, b_a, b_i, lam, conv_out_g, lru_out_g, final_g]
    sp = jnp.concatenate([jnp.stack(small), conv_full[1:], jnp.zeros((1, d), F32)], axis=0)
    wa_t, wi_t = _gate_tiles(w_a, tw).astype(MM), _gate_tiles(w_i, tw).astype(MM)
    ones_c, ones_l = _head_ones(d // N_CONV_HEADS, tw), _head_ones(hd_l, tw)

    h, dh, dhb, facc = _forward(x2, tgt2, p, wout_full, wa_t, wi_t, sp, ones_c, ones_l, min(256, t_len))
    dp, yt, gwa_t, gwi_t, bacc = _backward(p, h, dh, wout_full, wa_t, wi_t, sp, ones_c, ones_l, min(256, t_len))
    part_out, own_out, _ = _weight_grad_out(yt, dhb)
    part_in, own_in, (chips_out,) = _weight_grad_in(xnt, dp, (part_out,))
    grad_x, xacc, (chips_in,) = _input_grad(dp, win_all, x2, dh, sp, (part_in,), min(512, t_len))
    gw_in, dw_in, mw_in, vw_in = _update_shard(own_in, chips_in, w_in, m_w_in, v_w_in, "update_w_in")
    gw_out, dw_out, mw_out, vw_out = _update_shard(own_out, chips_out, w_out, m_w_out, v_w_out, "update_w_out")

    def slab(parts, wa_, wi_):
        return jnp.concatenate([jnp.zeros((1, d), F32), jnp.stack(parts), jnp.zeros((SL_W_A - SL_CONV_W, d), F32),
                                wa_.reshape(-1, d), wi_.reshape(-1, d)], axis=0)

    wsl = slab(small, w_a, w_i)
    msl = slab([m_ln_g, m_lru_conv_b, m_b_a, m_b_i, m_lam, m_conv_out_g, m_lru_out_g, m_final_g], m_w_a, m_w_i)
    vsl = slab([v_ln_g, v_lru_conv_b, v_b_a, v_b_i, v_lam, v_conv_out_g, v_lru_out_g, v_final_g], v_w_a, v_w_i)
    outs = _small_update(
        facc, xacc, bacc, _gate_blocks(gwa_t, N_LRU_HEADS, hd_l), _gate_blocks(gwi_t, N_LRU_HEADS, hd_l),
        lam.reshape(1, d), wsl, msl, vsl,
        conv_rows(conv_w, lru_conv_w), conv_rows(m_conv_w, m_lru_conv_w), conv_rows(v_conv_w, v_lru_conv_w))
    sl_g, sl_d, sl_m, sl_v, c_g, c_d, c_m, c_v = outs
    loss = sl_g[SL_LOSS, 0]
    na = w_a.size // d

    def unpack(sl, cv, big_in, big_out):
        one = lambda r: sl[r]
        return [one(SL_LN_G), big_in, cv[1:4], cv[4:8], one(SL_LRU_B),
                sl[SL_W_A:SL_W_A + na].reshape(w_a.shape), one(SL_B_A),
                sl[SL_W_A + na:SL_W_A + 2 * na].reshape(w_i.shape), one(SL_B_I), one(SL_LAM),
                one(SL_CONV_G), one(SL_LRU_G), big_out, one(SL_FINAL_G)]

    return (loss, grad_x[None], *unpack(sl_g, c_g, gw_in, gw_out), *unpack(sl_d, c_d, dw_in, dw_out),
            *unpack(sl_m, c_m, mw_in, mw_out), *unpack(sl_v, c_v, vw_in, vw_out))
```

```python
import functools

import jax
import jax.numpy as jnp
from jax import lax
from jax.experimental import pallas as pl
from jax.experimental.pallas import tpu as pltpu

F32 = jnp.float32
MM = jnp.bfloat16
MESH = pl.DeviceIdType.MESH

N_DEV = 8
N_CONV_HEADS = 8
N_LRU_HEADS = 16
RG_LRU_C = 8.0
RMS_EPS = 1e-6
ADAM_LR, ADAM_B1, ADAM_B2, ADAM_EPS, ADAM_WD, ADAM_STEP = 0.001, 0.9, 0.999, 1e-08, 0.01, 10
ADAM_BC1 = 1.0 - ADAM_B1 ** ADAM_STEP
ADAM_BC2 = 1.0 - ADAM_B2 ** ADAM_STEP

SUBLANES = 8
LANES = 128
MXU_TILE = 256
VMEM_LIMIT = 56 * 1024 * 1024

SP_LN_G, SP_LRU_B, SP_B_A, SP_B_I, SP_LAM, SP_CONV_G, SP_LRU_G, SP_FINAL_G, SP_CONV_W, SP_LRU_W = 0, 1, 2, 3, 4, 5, 6, 7, 8, 11
SP_ROWS = 16
P_B, P_C, P_XC, P_GC, P_XL, P_GL = 0, 1, 2, 3, 4, 5
A_CONV_G, A_LRU_G, A_LAM, A_B_A, A_B_I, A_CONV_W, A_LRU_W, A_LRU_B = 0, 1, 2, 3, 4, 5, 8, 12
A_GROUPS = 13
SL_LOSS, SL_LN_G, SL_LRU_B, SL_B_A, SL_B_I, SL_LAM, SL_CONV_G, SL_LRU_G, SL_FINAL_G, SL_CONV_W, SL_LRU_W, SL_W_A = 0, 1, 2, 3, 4, 5, 6, 7, 8, 9, 12, 16


def _params(vmem=True, **kw):
    if vmem:
        kw["vmem_limit_bytes"] = VMEM_LIMIT
    return pltpu.CompilerParams(**kw)


def _dot(a, b):
    return jnp.dot(a, b, preferred_element_type=F32)


def _dot_nt(a, b):
    return lax.dot_general(a, b, (((1,), (1,)), ((), ())), preferred_element_type=F32)


def _head_sums(v, ones_tile):
    tw = ones_tile.shape[0]
    hi = v.astype(MM)
    lo = (v - hi.astype(F32)).astype(MM)
    return jnp.concatenate(
        [_dot(hi[:, k:k + tw], ones_tile) + _dot(lo[:, k:k + tw], ones_tile) for k in range(0, v.shape[1], tw)], axis=1)


def _head_rstd(v, ones_tile, head_dim):
    return lax.rsqrt(_head_sums(v * v, ones_tile) * (1.0 / head_dim) + RMS_EPS)


def _sigmoid(x):
    return 0.5 * jnp.tanh(0.5 * x) + 0.5


def _lru_input_scale(log_a, a):
    return jnp.sqrt(-jnp.tanh(log_a) * (1.0 + a * a))


def _log_sigmoid(x):
    z = jnp.exp(-jnp.abs(x))
    u = 1.0 + z
    log1p_z = jnp.where(u == 1.0, z, jnp.log(u) * (z / (u - 1.0)))
    return jnp.minimum(x, 0.0) - log1p_z


def _row_iota(d):
    return lax.broadcasted_iota(jnp.int32, (SUBLANES, d), 0)


def _shift_down(cur, prev, s, row):
    return jnp.where(row >= s, pltpu.roll(cur, s, axis=0), pltpu.roll(prev, s, axis=0))


def _shift_up(cur, nxt, s, row):
    k = SUBLANES - s
    return jnp.where(row < k, pltpu.roll(cur, k, axis=0), pltpu.roll(nxt, k, axis=0))


def _scan_fwd(a, b, h_prev, row):
    for s in (1, 2, 4):
        a_s = jnp.where(row >= s, pltpu.roll(a, s, axis=0), 1.0)
        b_s = jnp.where(row >= s, pltpu.roll(b, s, axis=0), 0.0)
        b = a * b_s + b
        a = a * a_s
    return a * h_prev + b


def _scan_bwd(a_next, b, g_next, row):
    a = a_next
    for s in (1, 2, 4):
        k = SUBLANES - s
        a_s = jnp.where(row < k, pltpu.roll(a, k, axis=0), 1.0)
        b_s = jnp.where(row < k, pltpu.roll(b, k, axis=0), 0.0)
        b = a * b_s + b
        a = a * a_s
    return a * g_next + b


def _bcast_row(v, r):
    return jnp.broadcast_to(v[r:r + 1, :], v.shape)


def _chunks(n_rows, rc, body, init, reverse=False):
    n = n_rows // rc

    def step(i, carry):
        j = (n - 1 - i) if reverse else i
        return body(pl.multiple_of(j * rc, rc), carry)

    return lax.fori_loop(0, n, step, init)


def _adamw(w, g, m, v):
    m = ADAM_B1 * m + (1.0 - ADAM_B1) * g
    v = ADAM_B2 * v + (1.0 - ADAM_B2) * (g * g)
    m_hat = m / ADAM_BC1
    v_hat = v / ADAM_BC2
    delta = -ADAM_LR * (m_hat / (jnp.sqrt(v_hat) + ADAM_EPS) + ADAM_WD * w)
    return delta, m, v


def _mesh_pos():
    return lax.axis_index("x"), lax.axis_index("y"), lax.axis_index("c")


class _Gather:
    def __init__(self, blocks_of, send_sems, recv_sems, own_src=None):
        x, y, c = _mesh_pos()
        self.c = c
        self.me, self.sibling = (x, y, c), (x, y, 1 - c)
        self.chips = [(1 - x, y), (x, 1 - y), (1 - x, 1 - y)]
        self.blocks_of, self.send_sems, self.recv_sems = blocks_of, send_sems, recv_sems
        self.own_src = own_src

    def copy(self, a, k, block, to):
        src = self.blocks_of(a, *block)
        if block is self.me and self.own_src is not None:
            src = self.own_src[a]
        return pltpu.make_async_remote_copy(
            src_ref=src, dst_ref=self.blocks_of(a, *block),
            send_sem=self.send_sems.at[a * 7 + k], recv_sem=self.recv_sems.at[a * 7 + k],
            device_id=to, device_id_type=MESH)

    def start_own(self, a):
        self.copy(a, 0, self.me, self.sibling).start()
        for j, chip in enumerate(self.chips):
            self.copy(a, 1 + j, self.me, (*chip, self.c)).start()

    def wait_sibling(self, a):
        self.copy(a, 0, self.sibling, self.me).wait_recv()

    def wait_chip_and_pass_on(self, a, j):
        block = (*self.chips[j], self.c)
        self.copy(a, 1 + j, block, self.me).wait_recv()
        self.copy(a, 4 + j, block, self.sibling).start()

    def wait_passed_on(self, a, j):
        self.copy(a, 4 + j, (*self.chips[j], 1 - self.c), self.me).wait_recv()

    def wait_sends(self, a):
        self.copy(a, 0, self.me, self.sibling).wait_send()
        for j, chip in enumerate(self.chips):
            self.copy(a, 1 + j, self.me, (*chip, self.c)).wait_send()
            self.copy(a, 4 + j, (*chip, self.c), self.sibling).wait_send()

    def finish(self, a):
        for j in range(3):
            self.wait_chip_and_pass_on(a, j)
        self.wait_sibling(a)
        for j in range(3):
            self.wait_passed_on(a, j)
        self.wait_sends(a)


class _BalancedGather:
    def __init__(self, slot, send_sems, recv_sems, own_src):
        x, y, c = _mesh_pos()
        self.c = c
        self.me, self.sibling = (x, y, c), (x, y, 1 - c)
        self.chips = [(1 - x, y), (x, 1 - y), (1 - x, 1 - y)]
        self.slot, self.send_sems, self.recv_sems, self.own_src = slot, send_sems, recv_sems, own_src

    def half(self, a, block, which):
        ref = self.slot(a, *block)
        n = ref.shape[0] // 2
        return ref.at[pl.ds(which * n, n)]

    def copy(self, a, k, src, dst, to):
        return pltpu.make_async_remote_copy(
            src_ref=src, dst_ref=dst, send_sem=self.send_sems.at[a * 8 + k], recv_sem=self.recv_sems.at[a * 8 + k],
            device_id=to, device_id_type=MESH)

    def whole(self, a, k, block, to):
        src = self.own_src[a] if block is self.me else self.slot(a, *block)
        return self.copy(a, k, src, self.slot(a, *block), to)

    def halved(self, a, k, block, which, to):
        return self.copy(a, k, self.half(a, block, which), self.half(a, block, which), to)

    def on(self, chip):
        return (*self.chips[chip], self.c)

    def start_own(self, a):
        self.whole(a, 0, self.me, self.sibling).start()
        self.whole(a, 1, self.me, self.on(0)).start()
        self.whole(a, 2, self.me, self.on(1)).start()

    def wait_sibling(self, a):
        self.whole(a, 0, self.sibling, self.me).wait_recv()

    def on_neighbour(self, a, j):
        self.whole(a, 1 + j, self.on(j), self.me).wait_recv()
        self.halved(a, 3 + j, self.on(j), j, self.on(1 - j)).start()
        self.whole(a, 5 + j, self.on(j), self.sibling).start()

    def on_diagonal(self, a):
        self.halved(a, 3, self.on(2), 0, self.me).wait_recv()
        self.halved(a, 4, self.on(2), 1, self.me).wait_recv()
        self.whole(a, 7, self.on(2), self.sibling).start()

    def wait_passed_on(self, a, j):
        self.whole(a, 5 + j, (*self.chips[j], 1 - self.c), self.me).wait_recv()

    def wait_sends(self, a):
        self.whole(a, 0, self.me, self.sibling).wait_send()
        for j in range(2):
            self.whole(a, 1 + j, self.me, self.on(j)).wait_send()
            self.halved(a, 3 + j, self.on(j), j, self.on(1 - j)).wait_send()
        for j in range(3):
            self.whole(a, 5 + j, self.on(j), self.sibling).wait_send()


def _block_order():
    x, y, c = _mesh_pos()
    chips = [(1 - x, y), (x, 1 - y), (1 - x, 1 - y)]
    idx = lambda px, py, pc: 4 * px + 2 * py + pc
    order = [idx(x, y, c), idx(x, y, 1 - c)] + [idx(*ch, c) for ch in chips] + [idx(*ch, 1 - c) for ch in chips]
    return jnp.stack(order).astype(jnp.int32)


def _gather_project(x, w_in, w_out, conv_pack, ln_g, tb):
    t_len, d = x.shape
    nb = t_len // tb
    cols = w_in.shape[1]
    mc = min(512, t_len)
    srcs = (w_in, w_out, conv_pack)
    dts = (MM, MM, F32)

    def body(order_ref, x_ref, win_ref, wout_ref, cp_ref, lng_ref, p_ref, xnt_ref, win_all, wout_all, cp_all,
             xnb, st_in, st_out, st_cp, wbuf, send_sems, recv_sems, cp_send, cp_recv, local_sems):
        i = pl.program_id(0)
        x_, y_, c_ = _mesh_pos()
        me = 4 * x_ + 2 * y_ + c_
        outs = (win_all, wout_all, cp_all)
        stages = (st_in, st_out, st_cp)
        gather = _BalancedGather(lambda a, px, py, pc: outs[a].at[4 * px + 2 * py + pc], send_sems, recv_sems, stages)
        small = _Gather(lambda a, px, py, pc: cp_all.at[4 * px + 2 * py + pc], cp_send, cp_recv, own_src=[st_cp])
        keep_own = [pltpu.make_async_copy(stages[a], outs[a].at[me], local_sems.at[a]) for a in range(3)]

        @pl.when(i == 0)
        def _():
            for a, (src, dst) in enumerate(zip((win_ref, wout_ref, cp_ref), stages)):
                rows = src.shape[0]
                rc = min(rows, 32)

                def cast(r, carry, src=src, dst=dst, rc=rc):
                    dst[pl.ds(r, rc), :] = src[pl.ds(r, rc), :].astype(dst.dtype)
                    return carry

                _chunks(rows, rc, cast, 0)
                if a < 2:
                    gather.start_own(a)
                else:
                    small.start_own(0)
                keep_own[a].start()

        @pl.when(i < nb)
        def _():
            xv = x_ref[...]
            r0 = lax.rsqrt(jnp.mean(xv * xv, axis=-1, keepdims=True) + RMS_EPS)
            xn = xv * r0 * lng_ref[...]
            xnb[pl.ds(pl.multiple_of(i * tb, tb), tb), :] = xn.astype(MM)
            xnt_ref[...] = xn.T.astype(MM)

        for k in range(N_DEV):
            @pl.when(i == nb + k)
            def _(k=k):
                if k == 1:
                    gather.wait_sibling(0)
                elif k == 2:
                    gather.on_neighbour(0, 0)
                elif k == 3:
                    gather.on_neighbour(0, 1)
                    gather.on_neighbour(1, 0)
                elif k == 4:
                    gather.on_diagonal(0)
                    gather.on_neighbour(1, 1)
                elif k >= 5:
                    gather.wait_passed_on(0, k - 5)
                    if k == 5:
                        gather.on_diagonal(1)
                if k == 0:
                    w_blk = st_in
                else:
                    load = pltpu.make_async_copy(win_all.at[order_ref[k]], wbuf, local_sems.at[3])
                    load.start()
                    load.wait()
                    w_blk = wbuf

                def project(r, carry):
                    rows = pl.ds(r, mc)
                    p_ref[rows, :] = _dot(xnb[rows, :], w_blk[...]).astype(MM)
                    return carry

                _chunks(t_len, mc, project, 0)
                if k == N_DEV - 1:
                    gather.wait_sends(0)
                    gather.wait_sibling(1)
                    for j in range(3):
                        gather.wait_passed_on(1, j)
                    gather.wait_sends(1)
                    small.finish(0)
                    for cp in keep_own:
                        cp.wait()

    vm = pl.BlockSpec(memory_space=pltpu.VMEM)
    hbm = pl.BlockSpec(memory_space=pl.ANY)
    grid_spec = pltpu.PrefetchScalarGridSpec(
        num_scalar_prefetch=1, grid=(nb + N_DEV,),
        in_specs=[pl.BlockSpec((tb, d), lambda i, o: (jnp.minimum(i, nb - 1), 0)), vm, vm, vm, vm],
        out_specs=(pl.BlockSpec((t_len, cols), lambda i, o: (0, o[jnp.maximum(i - nb, 0)])),
                   pl.BlockSpec((d, tb), lambda i, o: (0, jnp.minimum(i, nb - 1))), hbm, hbm, hbm),
        scratch_shapes=[pltpu.VMEM((t_len, d), MM)] + [pltpu.VMEM(s.shape, dt) for s, dt in zip(srcs, dts)]
                       + [pltpu.VMEM(w_in.shape, MM),
                          pltpu.SemaphoreType.DMA((16,)), pltpu.SemaphoreType.DMA((16,)),
                          pltpu.SemaphoreType.DMA((7,)), pltpu.SemaphoreType.DMA((7,)), pltpu.SemaphoreType.DMA((4,))])
    return pl.pallas_call(
        body, name="gather_project", grid_spec=grid_spec,
        out_shape=(jax.ShapeDtypeStruct((t_len, N_DEV * cols), MM),
                   jax.ShapeDtypeStruct((d, t_len), MM))
                  + tuple(jax.ShapeDtypeStruct((N_DEV,) + s.shape, dt) for s, dt in zip(srcs, dts)),
        compiler_params=_params(dimension_semantics=("arbitrary",)),
    )(_block_order(), x, w_in, w_out, conv_pack, ln_g)


def _forward(x, tgt, p, wout, wa_t, wi_t, sp, ones_c, ones_l, tb):
    t_len, d = x.shape
    nb = t_len // tb
    n_tiles, tw = wa_t.shape[0], wa_t.shape[1]
    hd_c, hd_l = d // N_CONV_HEADS, d // N_LRU_HEADS
    s8 = SUBLANES

    def body(x_ref, tgt_ref, p_ref, wout_ref, wa_ref, wi_ref, sp_ref, oc_ref, ol_ref,
             h_ref, dh_ref, dhb_ref, acc_ref, yc, czs, u, pa, pi,
             rcf, rlf, ybuf, tail_z, tail_xl, hcar):
        i = pl.program_id(0)
        row = _row_iota(d)

        @pl.when(i == 0)
        def _():
            tail_z[...] = jnp.zeros_like(tail_z)
            tail_xl[...] = jnp.zeros_like(tail_xl)
            hcar[...] = jnp.zeros_like(hcar)
            acc_ref[...] = jnp.zeros_like(acc_ref)

        def spr(r):
            return sp_ref[r:r + 1, :]

        def proj(rows, seg):
            return p_ref[rows, seg * d:(seg + 1) * d].astype(F32)

        w0, w1, w2 = spr(SP_CONV_W), spr(SP_CONV_W + 1), spr(SP_CONV_W + 2)
        l0, l1, l2, l3 = spr(SP_LRU_W), spr(SP_LRU_W + 1), spr(SP_LRU_W + 2), spr(SP_LRU_W + 3)
        lb = spr(SP_LRU_B)

        def convs(r, carry):
            zp, xp = carry
            rows16 = pl.ds(r, 2 * s8)
            bg16, xl16 = proj(rows16, P_B), proj(rows16, P_XL)
            z16 = proj(rows16, P_C) * proj(rows16, P_XC)
            for j in range(2):
                rows, sub = pl.ds(r + j * s8, s8), slice(j * s8, (j + 1) * s8)
                z, xl = z16[sub], xl16[sub]
                cz = w0 * _shift_down(z, zp, 2, row) + w1 * _shift_down(z, zp, 1, row) + w2 * z
                czs[rows, :] = cz
                yc[rows, :] = bg16[sub] * cz
                u[rows, :] = (l0 * _shift_down(xl, xp, 3, row) + l1 * _shift_down(xl, xp, 2, row)
                              + l2 * _shift_down(xl, xp, 1, row) + l3 * xl + lb)
                zp, xp = z, xl
            return zp, xp

        z_last, xl_last = _chunks(tb, 2 * s8, convs, (tail_z[...], tail_xl[...]))
        tail_z[...] = z_last
        tail_xl[...] = xl_last

        ub = u[...].astype(MM)
        for k in range(n_tiles):
            sl = slice(k * tw, (k + 1) * tw)
            pa[:, sl] = _dot(ub[:, sl], wa_ref[k])
            pi[:, sl] = _dot(ub[:, sl], wi_ref[k])
        rcf[...] = _head_rstd(yc[...], oc_ref[...], hd_c)

        c8 = RG_LRU_C * _log_sigmoid(spr(SP_LAM))
        b_a, b_i = spr(SP_B_A), spr(SP_B_I)

        def lru(r, hp):
            rows = pl.ds(r, SUBLANES)
            ra = _sigmoid(pa[rows, :] + b_a)
            ii = _sigmoid(pi[rows, :] + b_i)
            pa[rows, :] = ra
            pi[rows, :] = ii
            la = ra * c8
            a = jnp.exp(la)
            mult = _lru_input_scale(la, a)
            h = _scan_fwd(a, mult * (ii * u[rows, :]), hp, row)
            h_ref[rows, :] = h
            return _bcast_row(h, SUBLANES - 1)

        hcar[...] = _chunks(tb, SUBLANES, lru, hcar[...])
        rlf[...] = _head_rstd(h_ref[...], ol_ref[...], hd_l)

        g_c, g_l = spr(SP_CONV_G), spr(SP_LRU_G)

        def gate(r, carry):
            rows = pl.ds(r, 2 * s8)
            gc, gl = proj(rows, P_GC), proj(rows, P_GL)
            ybuf[rows, 0:d] = (yc[rows, :] * rcf[rows, :] * g_c * (gc * _sigmoid(gc))).astype(MM)
            ybuf[rows, d:2 * d] = (h_ref[rows, :] * rlf[rows, :] * g_l * (gl * _sigmoid(gl))).astype(MM)
            return carry

        _chunks(tb, 2 * s8, gate, 0)

        hres = x_ref[...] + _dot(ybuf[...], wout_ref[...])
        rf = lax.rsqrt(jnp.mean(hres * hres, axis=-1, keepdims=True) + RMS_EPS)
        hn = hres * rf
        fg = spr(SP_FINAL_G)
        err = hn * fg - tgt_ref[...]
        dout = err * (1.0 / d)
        acc_ref[0:SUBLANES, :] += (err * err).reshape(tb // SUBLANES, SUBLANES, d).sum(axis=0)
        acc_ref[SUBLANES:2 * SUBLANES, :] += (dout * hn).reshape(tb // SUBLANES, SUBLANES, d).sum(axis=0)
        gd = dout * fg
        dhres = rf * (gd - hn * jnp.mean(gd * hn, axis=-1, keepdims=True))
        dh_ref[...] = dhres
        dhb_ref[...] = dhres.astype(MM)

    vm = pl.BlockSpec(memory_space=pltpu.VMEM)
    blk = lambda w: pl.BlockSpec((tb, w), lambda i: (i, 0))
    buf = pltpu.VMEM((tb, d), F32)
    car = pltpu.VMEM((SUBLANES, d), F32)
    return pl.pallas_call(
        body, name="forward", grid=(nb,),
        in_specs=[blk(d), blk(d), blk(6 * d), vm, vm, vm, vm, vm, vm],
        out_specs=(blk(d), blk(d), blk(d), pl.BlockSpec((2 * SUBLANES, d), lambda i: (0, 0))) + (blk(d),) * 5,
        out_shape=(jax.ShapeDtypeStruct((t_len, d), F32),
                   jax.ShapeDtypeStruct((t_len, d), F32),
                   jax.ShapeDtypeStruct((t_len, d), MM),
                   jax.ShapeDtypeStruct((2 * SUBLANES, d), F32))
                  + (jax.ShapeDtypeStruct((t_len, d), F32),) * 5,
        scratch_shapes=[buf] * 2 + [pltpu.VMEM((tb, 2 * d), MM), car, car, car],
        compiler_params=_params(dimension_semantics=("arbitrary",)),
    )(x, tgt, p, wout, wa_t, wi_t, sp, ones_c, ones_l)


def _backward(p, h, dh, saved, wout, wa_t, wi_t, sp, ones_c, ones_l, tb):
    t_len, d = h.shape
    nb = t_len // tb
    n_tiles, tw = wa_t.shape[0], wa_t.shape[1]
    hd_c, hd_l = d // N_CONV_HEADS, d // N_LRU_HEADS
    s8 = SUBLANES

    def body(p_ref, h_ref, hhalo_ref, dh_ref, yc, czs, u, ra_ref, ii_ref,
             wout_ref, wa_ref, wi_ref, sp_ref, oc_ref, ol_ref,
             dp_ref, yt_ref, gwa_ref, gwi_ref, acc_ref,
             hh, dy, ybuf, rcf, rlf, qc, ql, dyc_hat, dyl_hat, dpa, dpi, du,
             car_dcz, car_a, car_g, car_du):
        i = pl.program_id(0)
        blk_idx = nb - 1 - i
        row = _row_iota(d)

        @pl.when(i == 0)
        def _():
            for ref in (car_dcz, car_a, car_g, car_du, gwa_ref, gwi_ref, acc_ref):
                ref[...] = jnp.zeros_like(ref)

        def spr(r):
            return sp_ref[r:r + 1, :]

        def proj(rows, seg):
            return p_ref[rows, seg * d:(seg + 1) * d].astype(F32)

        def put(rows, seg, halves):
            dp_ref[rows, seg * d:(seg + 1) * d] = jnp.concatenate(halves, axis=0).astype(MM)

        def acc_add(group, val):
            acc_ref[group * s8:(group + 1) * s8, :] += val

        live = jnp.where(blk_idx > 0, 1.0, 0.0).astype(F32)
        hh[0:s8, :] = hhalo_ref[...] * live
        hh[s8:, :] = h_ref[...]

        dy[...] = _dot_nt(dh_ref[...].astype(MM), wout_ref[...])

        w0, w1, w2 = spr(SP_CONV_W), spr(SP_CONV_W + 1), spr(SP_CONV_W + 2)
        l0, l1, l2, l3 = spr(SP_LRU_W), spr(SP_LRU_W + 1), spr(SP_LRU_W + 2), spr(SP_LRU_W + 3)

        rcf[...] = _head_rstd(yc[...], oc_ref[...], hd_c)
        rlf[...] = _head_rstd(h_ref[...], ol_ref[...], hd_l)

        g_c, g_l = spr(SP_CONV_G), spr(SP_LRU_G)

        def gates(r, carry):
            rows = pl.ds(r, 2 * s8)
            for (seg, off_y, src, rstd, gain, q, dhat, grp) in (
                    (P_GC, 0, yc, rcf, g_c, qc, dyc_hat, A_CONV_G),
                    (P_GL, d, h_ref, rlf, g_l, ql, dyl_hat, A_LRU_G)):
                gt = proj(rows, seg)
                sg = _sigmoid(gt)
                silu = gt * sg
                yhat = src[rows, :] * rstd[rows, :]
                nrm = yhat * gain
                ybuf[rows, off_y:off_y + d] = nrm * silu
                dout = dy[rows, off_y:off_y + d]
                dnrm = dout * silu
                dp_ref[rows, seg * d:(seg + 1) * d] = (dout * nrm * (sg * (1.0 + gt * (1.0 - sg)))).astype(MM)
                dg = dnrm * yhat
                acc_add(grp, dg[0:s8] + dg[s8:])
                dh_ = dnrm * gain
                dhat[rows, :] = dh_
                q[rows, :] = dh_ * yhat
            return carry

        _chunks(tb, 2 * s8, gates, 0)

        qc[...] = _head_sums(qc[...], oc_ref[...]) * (1.0 / hd_c)
        ql[...] = _head_sums(ql[...], ol_ref[...]) * (1.0 / hd_l)
        yt_ref[...] = ybuf[...].T.astype(MM)

        c8 = RG_LRU_C * _log_sigmoid(spr(SP_LAM))

        def mixers(r, carry):
            dcz_n, a_n, g_n = carry
            rows16 = pl.ds(r, 2 * s8)
            bg16, cg16, xc16 = proj(rows16, P_B), proj(rows16, P_C), proj(rows16, P_XC)
            z16 = cg16 * xc16
            d_b, d_c, d_x = [None, None], [None, None], [None, None]
            for j in (1, 0):
                rows, sub = pl.ds(r + j * s8, s8), slice(j * s8, (j + 1) * s8)
                rstd = rcf[rows, :]
                yhat = yc[rows, :] * rstd
                dyc = rstd * (dyc_hat[rows, :] - yhat * qc[rows, :])
                d_b[j] = dyc * czs[rows, :]
                dcz = dyc * bg16[sub]
                up1, up2 = _shift_up(dcz, dcz_n, 1, row), _shift_up(dcz, dcz_n, 2, row)
                dz = w2 * dcz + w1 * up1 + w0 * up2
                d_c[j] = dz * xc16[sub]
                d_x[j] = dz * cg16[sub]
                z = z16[sub]
                acc_add(A_CONV_W, up2 * z)
                acc_add(A_CONV_W + 1, up1 * z)
                acc_add(A_CONV_W + 2, dcz * z)
                rstd = rlf[rows, :]
                hcur = hh[pl.ds(r + (j + 1) * s8, s8), :]
                hhat = hcur * rstd
                dh_out = rstd * (dyl_hat[rows, :] - hhat * ql[rows, :])
                ra = ra_ref[rows, :]
                la = ra * c8
                a = jnp.exp(la)
                g = _scan_bwd(_shift_up(a, a_n, 1, row), dh_out, g_n, row)
                da = g * _shift_down(hcur, hh[pl.ds(r + j * s8, s8), :], 1, row)
                ii = ii_ref[rows, :]
                uu = u[rows, :]
                mult = _lru_input_scale(la, a)
                dmult = g * (ii * uu)
                ds = g * mult
                dla = a * (da - dmult * a / mult)
                acc_add(A_LAM, dla * ra)
                dpa_ = dla * c8 * ra * (1.0 - ra)
                dpi_ = ds * uu * ii * (1.0 - ii)
                acc_add(A_B_A, dpa_)
                acc_add(A_B_I, dpi_)
                dpa[rows, :] = dpa_
                dpi[rows, :] = dpi_
                du[rows, :] = ds * ii
                dcz_n, a_n, g_n = dcz, a, _bcast_row(g, 0)
            put(rows16, P_B, d_b)
            put(rows16, P_C, d_c)
            put(rows16, P_XC, d_x)
            return dcz_n, a_n, g_n

        dcz_f, a_f, g_f = _chunks(tb, 2 * s8, mixers, (car_dcz[...], car_a[...], car_g[...]), reverse=True)
        car_dcz[...] = dcz_f
        car_a[...] = a_f
        car_g[...] = g_f

        dpab = dpa[...].astype(MM)
        dpib = dpi[...].astype(MM)
        for k in range(n_tiles):
            sl = slice(k * tw, (k + 1) * tw)
            du[:, sl] += _dot_nt(dpab[:, sl], wa_ref[k]) + _dot_nt(dpib[:, sl], wi_ref[k])
            ut = u[:, sl].T.astype(MM)
            gwa_ref[k] += _dot(ut, dpab[:, sl])
            gwi_ref[k] += _dot(ut, dpib[:, sl])

        def lru_conv(r, du_n):
            rows16 = pl.ds(r, 2 * s8)
            xl16 = proj(rows16, P_XL)
            d_xl = [None, None]
            for j in (1, 0):
                rows, sub = pl.ds(r + j * s8, s8), slice(j * s8, (j + 1) * s8)
                dut = du[rows, :]
                up1, up2, up3 = (_shift_up(dut, du_n, s, row) for s in (1, 2, 3))
                d_xl[j] = l3 * dut + l2 * up1 + l1 * up2 + l0 * up3
                xl = xl16[sub]
                acc_add(A_LRU_W, up3 * xl)
                acc_add(A_LRU_W + 1, up2 * xl)
                acc_add(A_LRU_W + 2, up1 * xl)
                acc_add(A_LRU_W + 3, dut * xl)
                acc_add(A_LRU_B, dut)
                du_n = dut
            put(rows16, P_XL, d_xl)
            return du_n

        car_du[...] = _chunks(tb, 2 * s8, lru_conv, car_du[...], reverse=True)

    vm = pl.BlockSpec(memory_space=pltpu.VMEM)
    rev = lambda w: pl.BlockSpec((tb, w), lambda i: (nb - 1 - i, 0))
    halo = lambda rows, w: pl.BlockSpec((rows, w), lambda i: (jnp.maximum((nb - 1 - i) * (tb // rows) - 1, 0), 0))
    const = lambda shape: pl.BlockSpec(shape, lambda i: (0,) * len(shape))
    buf = lambda w: pltpu.VMEM((tb, w), F32)
    car = pltpu.VMEM((SUBLANES, d), F32)
    return pl.pallas_call(
        body, name="backward", grid=(nb,),
        in_specs=[rev(6 * d), rev(d), halo(SUBLANES, d), rev(d)] + [rev(d)] * 5 + [vm, vm, vm, vm, vm, vm],
        out_specs=(rev(6 * d), pl.BlockSpec((2 * d, tb), lambda i: (0, nb - 1 - i)),
                   const((n_tiles, tw, tw)), const((n_tiles, tw, tw)), const((A_GROUPS * SUBLANES, d))),
        out_shape=(jax.ShapeDtypeStruct((t_len, 6 * d), MM),
                   jax.ShapeDtypeStruct((2 * d, t_len), MM),
                   jax.ShapeDtypeStruct((n_tiles, tw, tw), F32),
                   jax.ShapeDtypeStruct((n_tiles, tw, tw), F32),
                   jax.ShapeDtypeStruct((A_GROUPS * SUBLANES, d), F32)),
        scratch_shapes=[pltpu.VMEM((SUBLANES + tb, d), F32), buf(2 * d), buf(2 * d)] + [buf(d)] * 9 + [car, car, car, car],
        compiler_params=_params(dimension_semantics=("arbitrary",)),
    )(p, h, h, dh, *saved, wout, wa_t, wi_t, sp, ones_c, ones_l)


def _input_grad(dp, win_all, x, dh, sp, parts, tb):
    t_len, d = x.shape
    nb = t_len // tb
    cols = win_all.shape[2]
    n_parts = len(parts)

    def body(dp_ref, win_ref, x_ref, dh_ref, sp_ref, *refs):
        gx_ref, acc_ref = refs[n_parts:n_parts + 2]
        exchange = _ChipExchange(refs[:n_parts], refs[n_parts + 2:2 * n_parts + 2], *refs[2 * n_parts + 2:])
        i = pl.program_id(0)

        @pl.when(i == 0)
        def _():
            acc_ref[...] = jnp.zeros_like(acc_ref)
            exchange.start()

        dxn = _dot_nt(dp_ref[:, 0:cols], win_ref[0])
        for j in range(1, N_DEV):
            dxn += _dot_nt(dp_ref[:, j * cols:(j + 1) * cols], win_ref[j])
        xv = x_ref[...]
        r0 = lax.rsqrt(jnp.mean(xv * xv, axis=-1, keepdims=True) + RMS_EPS)
        xhat = xv * r0
        acc_ref[...] += (dxn * xhat).reshape(tb // SUBLANES, SUBLANES, d).sum(axis=0)
        dxh = dxn * sp_ref[SP_LN_G:SP_LN_G + 1, :]
        gx_ref[...] = dh_ref[...] + r0 * (dxh - xhat * jnp.mean(dxh * xhat, axis=-1, keepdims=True))

        @pl.when(i == nb - 1)
        def _():
            exchange.finish()

    vm = pl.BlockSpec(memory_space=pltpu.VMEM)
    hbm = pl.BlockSpec(memory_space=pl.ANY)
    blk = lambda w: pl.BlockSpec((tb, w), lambda i: (i, 0))
    outs = pl.pallas_call(
        body, name="input_grad", grid=(nb,),
        in_specs=[blk(6 * d), vm, blk(d), blk(d), vm] + [hbm] * n_parts,
        out_specs=(blk(d), pl.BlockSpec((SUBLANES, d), lambda i: (0, 0))) + (hbm,) * n_parts,
        out_shape=(jax.ShapeDtypeStruct((t_len, d), F32), jax.ShapeDtypeStruct((SUBLANES, d), F32))
                  + tuple(jax.ShapeDtypeStruct(p.shape, p.dtype) for p in parts),
        scratch_shapes=[pltpu.SemaphoreType.DMA((3 * n_parts,)), pltpu.SemaphoreType.DMA((3 * n_parts,))],
        compiler_params=_params(dimension_semantics=("arbitrary",)),
    )(dp, win_all, x, dh, sp, *parts)
    return outs[0], outs[1], outs[2:]


_CHIP_RELATIONS = [(0, 0), (1, 0), (0, 1), (1, 1)]


def _related_block(k, core):
    x, y, _ = _mesh_pos()
    fx, fy = _CHIP_RELATIONS[k]
    return 4 * (x ^ fx) + 2 * (y ^ fy) + core


class _ChipExchange:
    def __init__(self, part_refs, land_refs, send_sems, recv_sems):
        self.part_refs, self.land_refs, self.send_sems, self.recv_sems = part_refs, land_refs, send_sems, recv_sems

    def copies(self):
        x, y, c = _mesh_pos()
        for a in range(len(self.part_refs)):
            for k in (1, 2, 3):
                fx, fy = _CHIP_RELATIONS[k]
                yield pltpu.make_async_remote_copy(
                    src_ref=self.part_refs[a].at[k - 1], dst_ref=self.land_refs[a].at[k - 1],
                    send_sem=self.send_sems.at[3 * a + k - 1], recv_sem=self.recv_sems.at[3 * a + k - 1],
                    device_id=(x ^ fx, y ^ fy, c), device_id_type=MESH)

    def start(self):
        for cp in self.copies():
            cp.start()

    def finish(self):
        for cp in self.copies():
            cp.wait_recv()
        for cp in self.copies():
            cp.wait_send()


def _weight_grad_stage1(name, blk_shape, n_split, operands, in_specs, product, riders=()):
    n_rows, n_cols = blk_shape
    rs = n_rows // n_split
    rc = 32
    n_in, n_ride = len(operands), len(riders)
    _, _, c = _mesh_pos()
    order = jnp.stack([_related_block(k, 1 - c) for k in range(4)]
                      + [_related_block(k, c) for k in (1, 2, 3, 0)]).astype(jnp.int32)

    def body(order_ref, *refs):
        ins = refs[:n_in]
        ride_in = refs[n_in:n_in + n_ride]
        part_ref, own_ref, from_sib = refs[n_in + n_ride:n_in + n_ride + 3]
        ride_out = refs[n_in + n_ride + 3:n_in + 2 * n_ride + 3]
        gbuf, sendbuf, tmp, send_sems, recv_sems, local_sems, ride_send, ride_recv = refs[n_in + 2 * n_ride + 3:]
        exchange = _ChipExchange(ride_in, ride_out, ride_send, ride_recv)
        s = pl.program_id(0)
        x, y, c = _mesh_pos()

        def to_sibling(k):
            return pltpu.make_async_remote_copy(
                src_ref=sendbuf.at[k], dst_ref=from_sib.at[k], send_sem=send_sems.at[k], recv_sem=recv_sems.at[k],
                device_id=(x, y, 1 - c), device_id_type=MESH)

        if n_ride:
            @pl.when(s == 0)
            def _():
                exchange.start()

        for h in range(n_split):
            gbuf[h * rs:(h + 1) * rs, :] = product(ins, h)

        @pl.when(s < 4)
        def _():
            def narrow(r, carry):
                sendbuf[s, pl.ds(r, rc), :] = gbuf[pl.ds(r, rc), :].astype(MM)
                return carry

            _chunks(n_rows, rc, narrow, 0)
            to_sibling(s).start()

        @pl.when(s >= 4)
        def _():
            k = jnp.where(s == 7, 0, s - 3)
            to_sibling(k).wait_recv()
            load = pltpu.make_async_copy(from_sib.at[k], tmp, local_sems.at[0])
            load.start()
            load.wait()

            @pl.when(s < 7)
            def _():
                def add(r, carry):
                    rows = pl.ds(r, rc)
                    part_ref[0, rows, :] = (gbuf[rows, :] + tmp[rows, :].astype(F32)).astype(MM)
                    return carry

                _chunks(n_rows, rc, add, 0)

            @pl.when(s == 7)
            def _():
                def add(r, carry):
                    rows = pl.ds(r, rc)
                    own_ref[rows, :] = gbuf[rows, :] + tmp[rows, :].astype(F32)
                    return carry

                _chunks(n_rows, rc, add, 0)
                for kk in range(4):
                    to_sibling(kk).wait_send()
                if n_ride:
                    exchange.finish()

    hbm = pl.BlockSpec(memory_space=pl.ANY)
    grid_spec = pltpu.PrefetchScalarGridSpec(
        num_scalar_prefetch=1, grid=(N_DEV,), in_specs=list(in_specs) + [hbm] * n_ride,
        out_specs=(pl.BlockSpec((1, n_rows, n_cols), lambda s, o: (jnp.clip(s - 4, 0, 2), 0, 0)),
                   pl.BlockSpec((n_rows, n_cols), lambda s, o: (0, 0)), hbm) + (hbm,) * n_ride,
        scratch_shapes=[pltpu.VMEM((n_rows, n_cols), F32), pltpu.VMEM((4, n_rows, n_cols), MM),
                        pltpu.VMEM((n_rows, n_cols), MM),
                        pltpu.SemaphoreType.DMA((4,)), pltpu.SemaphoreType.DMA((4,)), pltpu.SemaphoreType.DMA((1,)),
                        pltpu.SemaphoreType.DMA((max(3 * n_ride, 1),)), pltpu.SemaphoreType.DMA((max(3 * n_ride, 1),))])
    outs = pl.pallas_call(
        body, name=name, grid_spec=grid_spec,
        out_shape=(jax.ShapeDtypeStruct((3, n_rows, n_cols), MM), jax.ShapeDtypeStruct((n_rows, n_cols), F32),
                   jax.ShapeDtypeStruct((4, n_rows, n_cols), MM))
                  + tuple(jax.ShapeDtypeStruct(p.shape, p.dtype) for p in riders),
        compiler_params=_params(dimension_semantics=("arbitrary",)),
    )(order, *operands, *riders)
    return outs[0], outs[1], outs[3:]


def _weight_grad_in(xnt, dp, riders):
    d, t_len = xnt.shape
    cols = dp.shape[1] // N_DEV
    half = d // 2
    return _weight_grad_stage1(
        "weight_grad_in", (d, cols), 2, (xnt, dp),
        [pl.BlockSpec(memory_space=pltpu.VMEM), pl.BlockSpec((t_len, cols), lambda s, o: (0, o[s]))],
        lambda refs, h: _dot(refs[0][h * half:(h + 1) * half, :], refs[1][...]), riders)


def _weight_grad_out(yt, dhb):
    d2, t_len = yt.shape
    d = dhb.shape[1]
    rows = d2 // N_DEV
    return _weight_grad_stage1(
        "weight_grad_out", (rows, d), 1, (yt, dhb),
        [pl.BlockSpec((rows, t_len), lambda s, o: (o[s], 0)), pl.BlockSpec(memory_space=pltpu.VMEM)],
        lambda refs, h: _dot(refs[0][...], refs[1][...]))


def _update_shard(own, from_chips, w, m, v, name):
    n_rows, n_cols = w.shape
    rb = min(256, n_rows)

    def body(own_ref, fc_ref, w_ref, m_ref, v_ref, grad_ref, delta_ref, mo_ref, vo_ref):
        g = own_ref[...]
        for k in range(3):
            g = g + fc_ref[k].astype(F32)
        delta, m_new, v_new = _adamw(w_ref[...], g, m_ref[...], v_ref[...])
        grad_ref[...] = g
        delta_ref[...] = delta
        mo_ref[...] = m_new
        vo_ref[...] = v_new

    blk = pl.BlockSpec((rb, n_cols), lambda i: (i, 0))
    out = jax.ShapeDtypeStruct((n_rows, n_cols), F32)
    return pl.pallas_call(
        body, name=name, grid=(n_rows // rb,),
        in_specs=[blk, pl.BlockSpec((3, rb, n_cols), lambda i: (0, i, 0)), blk, blk, blk],
        out_specs=(blk, blk, blk, blk), out_shape=(out, out, out, out),
        compiler_params=_params(dimension_semantics=("arbitrary",)),
    )(own, from_chips, w, m, v)


def _small_update(facc, xacc, bacc, gwa, gwi, lam, wsl, msl, vsl, cw, cm, cv):
    n_rows, d = wsl.shape
    s8 = SUBLANES
    cc = cw.shape[1]

    def body(facc_ref, xacc_ref, bacc_ref, gwa_ref, gwi_ref, lam_ref, w_ref, m_ref, v_ref, cw_ref, cm_ref, cv_ref,
             g_o, d_o, m_o, v_o, cg_o, cd_o, cm_o, cv_o, gat, send_sems, recv_sems):
        x, y, c = _mesh_pos()
        me = 4 * x + 2 * y + c

        def rowsum(ref, group):
            return jnp.sum(ref[group * s8:(group + 1) * s8, :], axis=0, keepdims=True)

        mine = gat.at[me]
        loss = jnp.sum(rowsum(facc_ref, 0), axis=1, keepdims=True) * (0.5 / d)
        mine[SL_LOSS:SL_LOSS + 1, :] = jnp.broadcast_to(loss, (1, d))
        mine[SL_LN_G:SL_LN_G + 1, :] = rowsum(xacc_ref, 0)
        mine[SL_LRU_B:SL_LRU_B + 1, :] = rowsum(bacc_ref, A_LRU_B)
        mine[SL_B_A:SL_B_A + 1, :] = rowsum(bacc_ref, A_B_A)
        mine[SL_B_I:SL_B_I + 1, :] = rowsum(bacc_ref, A_B_I)
        mine[SL_LAM:SL_LAM + 1, :] = rowsum(bacc_ref, A_LAM) * (RG_LRU_C * jax.nn.sigmoid(-lam_ref[...]))
        mine[SL_CONV_G:SL_CONV_G + 1, :] = rowsum(bacc_ref, A_CONV_G)
        mine[SL_LRU_G:SL_LRU_G + 1, :] = rowsum(bacc_ref, A_LRU_G)
        mine[SL_FINAL_G:SL_FINAL_G + 1, :] = rowsum(facc_ref, 1)
        for k in range(3):
            mine[SL_CONV_W + k:SL_CONV_W + k + 1, :] = rowsum(bacc_ref, A_CONV_W + k)
        for k in range(4):
            mine[SL_LRU_W + k:SL_LRU_W + k + 1, :] = rowsum(bacc_ref, A_LRU_W + k)
        na = gwa_ref.shape[0]
        mine[SL_W_A:SL_W_A + na, :] = gwa_ref[...]
        mine[SL_W_A + na:SL_W_A + 2 * na, :] = gwi_ref[...]

        gather = _Gather(lambda a, px, py, pc: gat.at[4 * px + 2 * py + pc], send_sems, recv_sems)
        gather.start_own(0)
        gather.finish(0)

        def update(r, carry):
            rows = pl.ds(r, s8)
            g = gat[0, rows, :]
            for b in range(1, N_DEV):
                g = g + gat[b, rows, :]
            delta, m_new, v_new = _adamw(w_ref[rows, :], g, m_ref[rows, :], v_ref[rows, :])
            g_o[rows, :] = g
            d_o[rows, :] = delta
            m_o[rows, :] = m_new
            v_o[rows, :] = v_new
            return carry

        _chunks(n_rows, s8, update, 0)
        gc = g_o[s8:2 * s8, pl.ds(pl.multiple_of(me * cc, cc), cc)]
        delta, m_new, v_new = _adamw(cw_ref[...], gc, cm_ref[...], cv_ref[...])
        cg_o[...] = gc
        cd_o[...] = delta
        cm_o[...] = m_new
        cv_o[...] = v_new

    vm = pl.BlockSpec(memory_space=pltpu.VMEM)
    big = jax.ShapeDtypeStruct((n_rows, d), F32)
    small = jax.ShapeDtypeStruct(cw.shape, F32)
    return pl.pallas_call(
        body, name="small_update",
        in_specs=[vm] * 12, out_specs=(vm,) * 8, out_shape=(big, big, big, big, small, small, small, small),
        scratch_shapes=[pltpu.VMEM((N_DEV, n_rows, d), F32), pltpu.SemaphoreType.DMA((7,)), pltpu.SemaphoreType.DMA((7,))],
        compiler_params=_params(),
    )(facc, xacc, bacc, gwa, gwi, lam, wsl, msl, vsl, cw, cm, cv)


def _head_ones(head_dim, tw):
    lane = jnp.arange(tw) // head_dim
    return (lane[:, None] == lane[None, :]).astype(MM)


def _gate_tiles(w, tw):
    n_heads, hd, _ = w.shape
    per = tw // hd
    w4 = w.reshape(n_heads // per, per, hd, hd)
    eye = jnp.eye(per, dtype=w.dtype)
    return (w4[:, :, :, None, :] * eye[None, :, None, :, None]).reshape(n_heads // per, tw, tw)


def _gate_blocks(tiles, n_heads, hd):
    n_tiles, tw, _ = tiles.shape
    per = tw // hd
    t5 = tiles.reshape(n_tiles, per, hd, per, hd)
    diag = jnp.stack([t5[:, a, :, a, :] for a in range(per)], axis=1)
    return diag.reshape(hd, n_heads * hd)


def kernel(x, ln_g, w_in, conv_w, lru_conv_w, lru_conv_b, w_a, b_a, w_i, b_i, lam, conv_out_g, lru_out_g, w_out, final_g, loss_target, m_ln_g, m_w_in, m_conv_w, m_lru_conv_w, m_lru_conv_b, m_w_a, m_b_a, m_w_i, m_b_i, m_lam, m_conv_out_g, m_lru_out_g, m_w_out, m_final_g, v_ln_g, v_w_in, v_conv_w, v_lru_conv_w, v_lru_conv_b, v_w_a, v_b_a, v_w_i, v_b_i, v_lam, v_conv_out_g, v_lru_out_g, v_w_out, v_final_g):
    _, t_len, d = x.shape
    hd_l = d // N_LRU_HEADS
    tw = min(MXU_TILE, d)
    cc = conv_w.shape[1]
    x2, tgt2 = x[0], loss_target[0]

    def conv_rows(cw3, lw4):
        return jnp.concatenate([jnp.zeros((1, cc), F32), cw3, lw4], axis=0)

    p, xnt, win_all, wout_all, conv_all = _gather_project(
        x2, w_in, w_out, conv_rows(conv_w, lru_conv_w), ln_g.reshape(1, d), min(256, t_len))
    wout_full = wout_all.reshape(N_DEV * w_out.shape[0], d)
    conv_full = conv_all.transpose(1, 0, 2).reshape(SUBLANES, d)
    small = [ln_g, lru_conv_b, b_a, b_i, lam, conv_out_g, lru_out_g, final_g]
    sp = jnp.concatenate([jnp.stack(small), conv_full[1:], jnp.zeros((1, d), F32)], axis=0)
    wa_t, wi_t = _gate_tiles(w_a, tw).astype(MM), _gate_tiles(w_i, tw).astype(MM)
    ones_c, ones_l = _head_ones(d // N_CONV_HEADS, tw), _head_ones(hd_l, tw)

    h, dh, dhb, facc, *saved = _forward(x2, tgt2, p, wout_full, wa_t, wi_t, sp, ones_c, ones_l, min(256, t_len))
    dp, yt, gwa_t, gwi_t, bacc = _backward(p, h, dh, saved, wout_full, wa_t, wi_t, sp, ones_c, ones_l, min(256, t_len))
    part_out, own_out, _ = _weight_grad_out(yt, dhb)
    part_in, own_in, (chips_out,) = _weight_grad_in(xnt, dp, (part_out,))
    grad_x, xacc, (chips_in,) = _input_grad(dp, win_all, x2, dh, sp, (part_in,), min(512, t_len))
    gw_in, dw_in, mw_in, vw_in = _update_shard(own_in, chips_in, w_in, m_w_in, v_w_in, "update_w_in")
    gw_out, dw_out, mw_out, vw_out = _update_shard(own_out, chips_out, w_out, m_w_out, v_w_out, "update_w_out")

    def slab(parts, wa_, wi_):
        return jnp.concatenate([jnp.zeros((1, d), F32), jnp.stack(parts), jnp.zeros((SL_W_A - SL_CONV_W, d), F32),
                                wa_.reshape(-1, d), wi_.reshape(-1, d)], axis=0)

    wsl = slab(small, w_a, w_i)
    msl = slab([m_ln_g, m_lru_conv_b, m_b_a, m_b_i, m_lam, m_conv_out_g, m_lru_out_g, m_final_g], m_w_a, m_w_i)
    vsl = slab([v_ln_g, v_lru_conv_b, v_b_a, v_b_i, v_lam, v_conv_out_g, v_lru_out_g, v_final_g], v_w_a, v_w_i)
    outs = _small_update(
        facc, xacc, bacc, _gate_blocks(gwa_t, N_LRU_HEADS, hd_l), _gate_blocks(gwi_t, N_LRU_HEADS, hd_l),
        lam.reshape(1, d), wsl, msl, vsl,
        conv_rows(conv_w, lru_conv_w), conv_rows(m_conv_w, m_lru_conv_w), conv_rows(v_conv_w, v_lru_conv_w))
    sl_g, sl_d, sl_m, sl_v, c_g, c_d, c_m, c_v = outs
    loss = sl_g[SL_LOSS, 0]
    na = w_a.size // d

    def unpack(sl, cv, big_in, big_out):
        one = lambda r: sl[r]
        return [one(SL_LN_G), big_in, cv[1:4], cv[4:8], one(SL_LRU_B),
                sl[SL_W_A:SL_W_A + na].reshape(w_a.shape), one(SL_B_A),
                sl[SL_W_A + na:SL_W_A + 2 * na].reshape(w_i.shape), one(SL_B_I), one(SL_LAM),
                one(SL_CONV_G), one(SL_LRU_G), big_out, one(SL_FINAL_G)]

    return (loss, grad_x[None], *unpack(sl_g, c_g, gw_in, gw_out), *unpack(sl_d, c_d, dw_in, dw_out),
            *unpack(sl_m, c_m, mw_in, mw_out), *unpack(sl_v, c_v, vw_in, vw_out))
```

```python
import functools

import jax
import jax.numpy as jnp
from jax import lax
from jax.experimental import pallas as pl
from jax.experimental.pallas import tpu as pltpu

F32 = jnp.float32
MM = jnp.bfloat16
MESH = pl.DeviceIdType.MESH

N_DEV = 8
N_CONV_HEADS = 8
N_LRU_HEADS = 16
RG_LRU_C = 8.0
RMS_EPS = 1e-6
ADAM_LR, ADAM_B1, ADAM_B2, ADAM_EPS, ADAM_WD, ADAM_STEP = 0.001, 0.9, 0.999, 1e-08, 0.01, 10
ADAM_BC1 = 1.0 - ADAM_B1 ** ADAM_STEP
ADAM_BC2 = 1.0 - ADAM_B2 ** ADAM_STEP

SUBLANES = 8
LANES = 128
MXU_TILE = 256
VMEM_LIMIT = 56 * 1024 * 1024

SP_LN_G, SP_LRU_B, SP_B_A, SP_B_I, SP_LAM, SP_CONV_G, SP_LRU_G, SP_FINAL_G, SP_CONV_W, SP_LRU_W = 0, 1, 2, 3, 4, 5, 6, 7, 8, 11
SP_ROWS = 16
P_B, P_C, P_XC, P_GC, P_XL, P_GL = 0, 1, 2, 3, 4, 5
A_CONV_G, A_LRU_G, A_LAM, A_B_A, A_B_I, A_CONV_W, A_LRU_W, A_LRU_B = 0, 1, 2, 3, 4, 5, 8, 12
A_GROUPS = 13
SL_LOSS, SL_LN_G, SL_LRU_B, SL_B_A, SL_B_I, SL_LAM, SL_CONV_G, SL_LRU_G, SL_FINAL_G, SL_CONV_W, SL_LRU_W, SL_W_A = 0, 1, 2, 3, 4, 5, 6, 7, 8, 9, 12, 16


def _params(vmem=True, **kw):
    if vmem:
        kw["vmem_limit_bytes"] = VMEM_LIMIT
    return pltpu.CompilerParams(**kw)


def _dot(a, b):
    return jnp.dot(a, b, preferred_element_type=F32)


def _dot_nt(a, b):
    return lax.dot_general(a, b, (((1,), (1,)), ((), ())), preferred_element_type=F32)


def _head_sums(v, ones_tile):
    tw = ones_tile.shape[0]
    vb = v.astype(MM)
    return jnp.concatenate([_dot(vb[:, k:k + tw], ones_tile) for k in range(0, v.shape[1], tw)], axis=1)


def _head_rstd(v, ones_tile, head_dim):
    return lax.rsqrt(_head_sums(v * v, ones_tile) * (1.0 / head_dim) + RMS_EPS)


def _sigmoid(x):
    return 0.5 * jnp.tanh(0.5 * x) + 0.5


def _lru_input_scale(log_a, a):
    return jnp.sqrt(-jnp.tanh(log_a) * (1.0 + a * a))


def _log_sigmoid(x):
    z = jnp.exp(-jnp.abs(x))
    u = 1.0 + z
    log1p_z = jnp.where(u == 1.0, z, jnp.log(u) * (z / (u - 1.0)))
    return jnp.minimum(x, 0.0) - log1p_z


def _row_iota(d):
    return lax.broadcasted_iota(jnp.int32, (SUBLANES, d), 0)


def _shift_down(cur, prev, s, row):
    return jnp.where(row >= s, pltpu.roll(cur, s, axis=0), pltpu.roll(prev, s, axis=0))


def _shift_up(cur, nxt, s, row):
    k = SUBLANES - s
    return jnp.where(row < k, pltpu.roll(cur, k, axis=0), pltpu.roll(nxt, k, axis=0))


def _scan_fwd(a, b, h_prev, row):
    for s in (1, 2, 4):
        a_s = jnp.where(row >= s, pltpu.roll(a, s, axis=0), 1.0)
        b_s = jnp.where(row >= s, pltpu.roll(b, s, axis=0), 0.0)
        b = a * b_s + b
        a = a * a_s
    return a * h_prev + b


def _scan_bwd(a_next, b, g_next, row):
    a = a_next
    for s in (1, 2, 4):
        k = SUBLANES - s
        a_s = jnp.where(row < k, pltpu.roll(a, k, axis=0), 1.0)
        b_s = jnp.where(row < k, pltpu.roll(b, k, axis=0), 0.0)
        b = a * b_s + b
        a = a * a_s
    return a * g_next + b


def _bcast_row(v, r):
    return jnp.broadcast_to(v[r:r + 1, :], v.shape)


def _chunks(n_rows, rc, body, init, reverse=False):
    n = n_rows // rc

    def step(i, carry):
        j = (n - 1 - i) if reverse else i
        return body(pl.multiple_of(j * rc, rc), carry)

    return lax.fori_loop(0, n, step, init)


def _adamw(w, g, m, v):
    m = ADAM_B1 * m + (1.0 - ADAM_B1) * g
    v = ADAM_B2 * v + (1.0 - ADAM_B2) * (g * g)
    m_hat = m / ADAM_BC1
    v_hat = v / ADAM_BC2
    delta = -ADAM_LR * (m_hat / (jnp.sqrt(v_hat) + ADAM_EPS) + ADAM_WD * w)
    return delta, m, v


def _mesh_pos():
    return lax.axis_index("x"), lax.axis_index("y"), lax.axis_index("c")


class _Gather:
    def __init__(self, blocks_of, send_sems, recv_sems, own_src=None):
        x, y, c = _mesh_pos()
        self.c = c
        self.me, self.sibling = (x, y, c), (x, y, 1 - c)
        self.chips = [(1 - x, y), (x, 1 - y), (1 - x, 1 - y)]
        self.blocks_of, self.send_sems, self.recv_sems = blocks_of, send_sems, recv_sems
        self.own_src = own_src

    def copy(self, a, k, block, to):
        src = self.blocks_of(a, *block)
        if block is self.me and self.own_src is not None:
            src = self.own_src[a]
        return pltpu.make_async_remote_copy(
            src_ref=src, dst_ref=self.blocks_of(a, *block),
            send_sem=self.send_sems.at[a * 7 + k], recv_sem=self.recv_sems.at[a * 7 + k],
            device_id=to, device_id_type=MESH)

    def start_own(self, a):
        self.copy(a, 0, self.me, self.sibling).start()
        for j, chip in enumerate(self.chips):
            self.copy(a, 1 + j, self.me, (*chip, self.c)).start()

    def wait_sibling(self, a):
        self.copy(a, 0, self.sibling, self.me).wait_recv()

    def wait_chip_and_pass_on(self, a, j):
        block = (*self.chips[j], self.c)
        self.copy(a, 1 + j, block, self.me).wait_recv()
        self.copy(a, 4 + j, block, self.sibling).start()

    def wait_passed_on(self, a, j):
        self.copy(a, 4 + j, (*self.chips[j], 1 - self.c), self.me).wait_recv()

    def wait_sends(self, a):
        self.copy(a, 0, self.me, self.sibling).wait_send()
        for j, chip in enumerate(self.chips):
            self.copy(a, 1 + j, self.me, (*chip, self.c)).wait_send()
            self.copy(a, 4 + j, (*chip, self.c), self.sibling).wait_send()

    def finish(self, a):
        for j in range(3):
            self.wait_chip_and_pass_on(a, j)
        self.wait_sibling(a)
        for j in range(3):
            self.wait_passed_on(a, j)
        self.wait_sends(a)


class _BalancedGather:
    def __init__(self, slot, send_sems, recv_sems, own_src):
        x, y, c = _mesh_pos()
        self.c = c
        self.me, self.sibling = (x, y, c), (x, y, 1 - c)
        self.chips = [(1 - x, y), (x, 1 - y), (1 - x, 1 - y)]
        self.slot, self.send_sems, self.recv_sems, self.own_src = slot, send_sems, recv_sems, own_src

    def half(self, a, block, which):
        ref = self.slot(a, *block)
        n = ref.shape[0] // 2
        return ref.at[pl.ds(which * n, n)]

    def copy(self, a, k, src, dst, to):
        return pltpu.make_async_remote_copy(
            src_ref=src, dst_ref=dst, send_sem=self.send_sems.at[a * 8 + k], recv_sem=self.recv_sems.at[a * 8 + k],
            device_id=to, device_id_type=MESH)

    def whole(self, a, k, block, to):
        src = self.own_src[a] if block is self.me else self.slot(a, *block)
        return self.copy(a, k, src, self.slot(a, *block), to)

    def halved(self, a, k, block, which, to):
        return self.copy(a, k, self.half(a, block, which), self.half(a, block, which), to)

    def on(self, chip):
        return (*self.chips[chip], self.c)

    def start_own(self, a):
        self.whole(a, 0, self.me, self.sibling).start()
        self.whole(a, 1, self.me, self.on(0)).start()
        self.whole(a, 2, self.me, self.on(1)).start()

    def wait_sibling(self, a):
        self.whole(a, 0, self.sibling, self.me).wait_recv()

    def on_neighbour(self, a, j):
        self.whole(a, 1 + j, self.on(j), self.me).wait_recv()
        self.halved(a, 3 + j, self.on(j), j, self.on(1 - j)).start()
        self.whole(a, 5 + j, self.on(j), self.sibling).start()

    def on_diagonal(self, a):
        self.halved(a, 3, self.on(2), 0, self.me).wait_recv()
        self.halved(a, 4, self.on(2), 1, self.me).wait_recv()
        self.whole(a, 7, self.on(2), self.sibling).start()

    def wait_passed_on(self, a, j):
        self.whole(a, 5 + j, (*self.chips[j], 1 - self.c), self.me).wait_recv()

    def wait_sends(self, a):
        self.whole(a, 0, self.me, self.sibling).wait_send()
        for j in range(2):
            self.whole(a, 1 + j, self.me, self.on(j)).wait_send()
            self.halved(a, 3 + j, self.on(j), j, self.on(1 - j)).wait_send()
        for j in range(3):
            self.whole(a, 5 + j, self.on(j), self.sibling).wait_send()


def _block_order():
    x, y, c = _mesh_pos()
    chips = [(1 - x, y), (x, 1 - y), (1 - x, 1 - y)]
    idx = lambda px, py, pc: 4 * px + 2 * py + pc
    order = [idx(x, y, c), idx(x, y, 1 - c)] + [idx(*ch, c) for ch in chips] + [idx(*ch, 1 - c) for ch in chips]
    return jnp.stack(order).astype(jnp.int32)


def _gather_project(x, w_in, w_out, conv_pack, ln_g, tb):
    t_len, d = x.shape
    nb = t_len // tb
    cols = w_in.shape[1]
    mc = min(512, t_len)
    srcs = (w_in, w_out, conv_pack)
    dts = (MM, MM, F32)

    def body(order_ref, x_ref, win_ref, wout_ref, cp_ref, lng_ref, p_ref, xnt_ref, win_all, wout_all, cp_all,
             xnb, st_in, st_out, st_cp, wbuf, send_sems, recv_sems, cp_send, cp_recv, local_sems):
        i = pl.program_id(0)
        x_, y_, c_ = _mesh_pos()
        me = 4 * x_ + 2 * y_ + c_
        outs = (win_all, wout_all, cp_all)
        stages = (st_in, st_out, st_cp)
        gather = _BalancedGather(lambda a, px, py, pc: outs[a].at[4 * px + 2 * py + pc], send_sems, recv_sems, stages)
        small = _Gather(lambda a, px, py, pc: cp_all.at[4 * px + 2 * py + pc], cp_send, cp_recv, own_src=[st_cp])
        keep_own = [pltpu.make_async_copy(stages[a], outs[a].at[me], local_sems.at[a]) for a in range(3)]

        @pl.when(i == 0)
        def _():
            for a, (src, dst) in enumerate(zip((win_ref, wout_ref, cp_ref), stages)):
                rows = src.shape[0]
                rc = min(rows, 32)

                def cast(r, carry, src=src, dst=dst, rc=rc):
                    dst[pl.ds(r, rc), :] = src[pl.ds(r, rc), :].astype(dst.dtype)
                    return carry

                _chunks(rows, rc, cast, 0)
                if a < 2:
                    gather.start_own(a)
                else:
                    small.start_own(0)
                keep_own[a].start()

        @pl.when(i < nb)
        def _():
            xv = x_ref[...]
            r0 = lax.rsqrt(jnp.mean(xv * xv, axis=-1, keepdims=True) + RMS_EPS)
            xn = xv * r0 * lng_ref[...]
            xnb[pl.ds(pl.multiple_of(i * tb, tb), tb), :] = xn.astype(MM)
            xnt_ref[...] = xn.T.astype(MM)

        for k in range(N_DEV):
            @pl.when(i == nb + k)
            def _(k=k):
                if k == 1:
                    gather.wait_sibling(0)
                elif k == 2:
                    gather.on_neighbour(0, 0)
                elif k == 3:
                    gather.on_neighbour(0, 1)
                    gather.on_neighbour(1, 0)
                elif k == 4:
                    gather.on_diagonal(0)
                    gather.on_neighbour(1, 1)
                elif k >= 5:
                    gather.wait_passed_on(0, k - 5)
                    if k == 5:
                        gather.on_diagonal(1)
                if k == 0:
                    w_blk = st_in
                else:
                    load = pltpu.make_async_copy(win_all.at[order_ref[k]], wbuf, local_sems.at[3])
                    load.start()
                    load.wait()
                    w_blk = wbuf

                def project(r, carry):
                    rows = pl.ds(r, mc)
                    p_ref[rows, :] = _dot(xnb[rows, :], w_blk[...]).astype(MM)
                    return carry

                _chunks(t_len, mc, project, 0)
                if k == N_DEV - 1:
                    gather.wait_sends(0)
                    gather.wait_sibling(1)
                    for j in range(3):
                        gather.wait_passed_on(1, j)
                    gather.wait_sends(1)
                    small.finish(0)
                    for cp in keep_own:
                        cp.wait()

    vm = pl.BlockSpec(memory_space=pltpu.VMEM)
    hbm = pl.BlockSpec(memory_space=pl.ANY)
    grid_spec = pltpu.PrefetchScalarGridSpec(
        num_scalar_prefetch=1, grid=(nb + N_DEV,),
        in_specs=[pl.BlockSpec((tb, d), lambda i, o: (jnp.minimum(i, nb - 1), 0)), vm, vm, vm, vm],
        out_specs=(pl.BlockSpec((t_len, cols), lambda i, o: (0, o[jnp.maximum(i - nb, 0)])),
                   pl.BlockSpec((d, tb), lambda i, o: (0, jnp.minimum(i, nb - 1))), hbm, hbm, hbm),
        scratch_shapes=[pltpu.VMEM((t_len, d), MM)] + [pltpu.VMEM(s.shape, dt) for s, dt in zip(srcs, dts)]
                       + [pltpu.VMEM(w_in.shape, MM),
                          pltpu.SemaphoreType.DMA((16,)), pltpu.SemaphoreType.DMA((16,)),
                          pltpu.SemaphoreType.DMA((7,)), pltpu.SemaphoreType.DMA((7,)), pltpu.SemaphoreType.DMA((4,))])
    return pl.pallas_call(
        body, name="gather_project", grid_spec=grid_spec,
        out_shape=(jax.ShapeDtypeStruct((t_len, N_DEV * cols), MM),
                   jax.ShapeDtypeStruct((d, t_len), MM))
                  + tuple(jax.ShapeDtypeStruct((N_DEV,) + s.shape, dt) for s, dt in zip(srcs, dts)),
        compiler_params=_params(dimension_semantics=("arbitrary",)),
    )(_block_order(), x, w_in, w_out, conv_pack, ln_g)


def _forward(x, tgt, p, wout, wa_t, wi_t, sp, ones_c, ones_l, tb):
    t_len, d = x.shape
    nb = t_len // tb
    n_tiles, tw = wa_t.shape[0], wa_t.shape[1]
    hd_c, hd_l = d // N_CONV_HEADS, d // N_LRU_HEADS
    s8 = SUBLANES

    def body(x_ref, tgt_ref, p_ref, wout_ref, wa_ref, wi_ref, sp_ref, oc_ref, ol_ref,
             h_ref, dh_ref, dhb_ref, acc_ref, yc, czs, u, pa, pi,
             rcf, rlf, ybuf, tail_z, tail_xl, hcar):
        i = pl.program_id(0)
        row = _row_iota(d)

        @pl.when(i == 0)
        def _():
            tail_z[...] = jnp.zeros_like(tail_z)
            tail_xl[...] = jnp.zeros_like(tail_xl)
            hcar[...] = jnp.zeros_like(hcar)
            acc_ref[...] = jnp.zeros_like(acc_ref)

        def spr(r):
            return sp_ref[r:r + 1, :]

        def proj(rows, seg):
            return p_ref[rows, seg * d:(seg + 1) * d].astype(F32)

        w0, w1, w2 = spr(SP_CONV_W), spr(SP_CONV_W + 1), spr(SP_CONV_W + 2)
        l0, l1, l2, l3 = spr(SP_LRU_W), spr(SP_LRU_W + 1), spr(SP_LRU_W + 2), spr(SP_LRU_W + 3)
        lb = spr(SP_LRU_B)

        def convs(r, carry):
            zp, xp = carry
            rows16 = pl.ds(r, 2 * s8)
            bg16, xl16 = proj(rows16, P_B), proj(rows16, P_XL)
            z16 = proj(rows16, P_C) * proj(rows16, P_XC)
            for j in range(2):
                rows, sub = pl.ds(r + j * s8, s8), slice(j * s8, (j + 1) * s8)
                z, xl = z16[sub], xl16[sub]
                cz = w0 * _shift_down(z, zp, 2, row) + w1 * _shift_down(z, zp, 1, row) + w2 * z
                czs[rows, :] = cz
                yc[rows, :] = bg16[sub] * cz
                u[rows, :] = (l0 * _shift_down(xl, xp, 3, row) + l1 * _shift_down(xl, xp, 2, row)
                              + l2 * _shift_down(xl, xp, 1, row) + l3 * xl + lb)
                zp, xp = z, xl
            return zp, xp

        z_last, xl_last = _chunks(tb, 2 * s8, convs, (tail_z[...], tail_xl[...]))
        tail_z[...] = z_last
        tail_xl[...] = xl_last

        ub = u[...].astype(MM)
        for k in range(n_tiles):
            sl = slice(k * tw, (k + 1) * tw)
            pa[:, sl] = _dot(ub[:, sl], wa_ref[k])
            pi[:, sl] = _dot(ub[:, sl], wi_ref[k])
        rcf[...] = _head_rstd(yc[...], oc_ref[...], hd_c)

        c8 = RG_LRU_C * _log_sigmoid(spr(SP_LAM))
        b_a, b_i = spr(SP_B_A), spr(SP_B_I)

        def lru(r, hp):
            rows = pl.ds(r, SUBLANES)
            ra = _sigmoid(pa[rows, :] + b_a)
            ii = _sigmoid(pi[rows, :] + b_i)
            pa[rows, :] = ra
            pi[rows, :] = ii
            la = ra * c8
            a = jnp.exp(la)
            mult = _lru_input_scale(la, a)
            h = _scan_fwd(a, mult * (ii * u[rows, :]), hp, row)
            h_ref[rows, :] = h
            return _bcast_row(h, SUBLANES - 1)

        hcar[...] = _chunks(tb, SUBLANES, lru, hcar[...])
        rlf[...] = _head_rstd(h_ref[...], ol_ref[...], hd_l)

        g_c, g_l = spr(SP_CONV_G), spr(SP_LRU_G)

        def gate(r, carry):
            rows = pl.ds(r, 2 * s8)
            gc, gl = proj(rows, P_GC), proj(rows, P_GL)
            ybuf[rows, 0:d] = (yc[rows, :] * rcf[rows, :] * g_c * (gc * _sigmoid(gc))).astype(MM)
            ybuf[rows, d:2 * d] = (h_ref[rows, :] * rlf[rows, :] * g_l * (gl * _sigmoid(gl))).astype(MM)
            return carry

        _chunks(tb, 2 * s8, gate, 0)

        hres = x_ref[...] + _dot(ybuf[...], wout_ref[...])
        rf = lax.rsqrt(jnp.mean(hres * hres, axis=-1, keepdims=True) + RMS_EPS)
        hn = hres * rf
        fg = spr(SP_FINAL_G)
        err = hn * fg - tgt_ref[...]
        dout = err * (1.0 / d)
        acc_ref[0:SUBLANES, :] += (err * err).reshape(tb // SUBLANES, SUBLANES, d).sum(axis=0)
        acc_ref[SUBLANES:2 * SUBLANES, :] += (dout * hn).reshape(tb // SUBLANES, SUBLANES, d).sum(axis=0)
        gd = dout * fg
        dhres = rf * (gd - hn * jnp.mean(gd * hn, axis=-1, keepdims=True))
        dh_ref[...] = dhres
        dhb_ref[...] = dhres.astype(MM)

    vm = pl.BlockSpec(memory_space=pltpu.VMEM)
    blk = lambda w: pl.BlockSpec((tb, w), lambda i: (i, 0))
    buf = pltpu.VMEM((tb, d), F32)
    car = pltpu.VMEM((SUBLANES, d), F32)
    return pl.pallas_call(
        body, name="forward", grid=(nb,),
        in_specs=[blk(d), blk(d), blk(6 * d), vm, vm, vm, vm, vm, vm],
        out_specs=(blk(d), blk(d), blk(d), pl.BlockSpec((2 * SUBLANES, d), lambda i: (0, 0))) + (blk(d),) * 5,
        out_shape=(jax.ShapeDtypeStruct((t_len, d), F32),
                   jax.ShapeDtypeStruct((t_len, d), F32),
                   jax.ShapeDtypeStruct((t_len, d), MM),
                   jax.ShapeDtypeStruct((2 * SUBLANES, d), F32))
                  + (jax.ShapeDtypeStruct((t_len, d), F32),) * 5,
        scratch_shapes=[buf] * 2 + [pltpu.VMEM((tb, 2 * d), MM), car, car, car],
        compiler_params=_params(dimension_semantics=("arbitrary",)),
    )(x, tgt, p, wout, wa_t, wi_t, sp, ones_c, ones_l)


def _backward(p, h, dh, saved, wout, wa_t, wi_t, sp, ones_c, ones_l, tb):
    t_len, d = h.shape
    nb = t_len // tb
    n_tiles, tw = wa_t.shape[0], wa_t.shape[1]
    hd_c, hd_l = d // N_CONV_HEADS, d // N_LRU_HEADS
    s8 = SUBLANES

    def body(p_ref, h_ref, hhalo_ref, dh_ref, yc, czs, u, ra_ref, ii_ref,
             wout_ref, wa_ref, wi_ref, sp_ref, oc_ref, ol_ref,
             dp_ref, yt_ref, gwa_ref, gwi_ref, acc_ref,
             hh, dy, ybuf, rcf, rlf, qc, ql, dyc_hat, dyl_hat, dpa, dpi, du,
             car_dcz, car_a, car_g, car_du):
        i = pl.program_id(0)
        blk_idx = nb - 1 - i
        row = _row_iota(d)

        @pl.when(i == 0)
        def _():
            for ref in (car_dcz, car_a, car_g, car_du, gwa_ref, gwi_ref, acc_ref):
                ref[...] = jnp.zeros_like(ref)

        def spr(r):
            return sp_ref[r:r + 1, :]

        def proj(rows, seg):
            return p_ref[rows, seg * d:(seg + 1) * d].astype(F32)

        def put(rows, seg, halves):
            dp_ref[rows, seg * d:(seg + 1) * d] = jnp.concatenate(halves, axis=0).astype(MM)

        def acc_add(group, val):
            acc_ref[group * s8:(group + 1) * s8, :] += val

        live = jnp.where(blk_idx > 0, 1.0, 0.0).astype(F32)
        hh[0:s8, :] = hhalo_ref[...] * live
        hh[s8:, :] = h_ref[...]

        dy[...] = _dot_nt(dh_ref[...].astype(MM), wout_ref[...])

        w0, w1, w2 = spr(SP_CONV_W), spr(SP_CONV_W + 1), spr(SP_CONV_W + 2)
        l0, l1, l2, l3 = spr(SP_LRU_W), spr(SP_LRU_W + 1), spr(SP_LRU_W + 2), spr(SP_LRU_W + 3)

        rcf[...] = _head_rstd(yc[...], oc_ref[...], hd_c)
        rlf[...] = _head_rstd(h_ref[...], ol_ref[...], hd_l)

        g_c, g_l = spr(SP_CONV_G), spr(SP_LRU_G)

        def gates(r, carry):
            rows = pl.ds(r, 2 * s8)
            for (seg, off_y, src, rstd, gain, q, dhat, grp) in (
                    (P_GC, 0, yc, rcf, g_c, qc, dyc_hat, A_CONV_G),
                    (P_GL, d, h_ref, rlf, g_l, ql, dyl_hat, A_LRU_G)):
                gt = proj(rows, seg)
                sg = _sigmoid(gt)
                silu = gt * sg
                yhat = src[rows, :] * rstd[rows, :]
                nrm = yhat * gain
                ybuf[rows, off_y:off_y + d] = nrm * silu
                dout = dy[rows, off_y:off_y + d]
                dnrm = dout * silu
                dp_ref[rows, seg * d:(seg + 1) * d] = (dout * nrm * (sg * (1.0 + gt * (1.0 - sg)))).astype(MM)
                dg = dnrm * yhat
                acc_add(grp, dg[0:s8] + dg[s8:])
                dh_ = dnrm * gain
                dhat[rows, :] = dh_
                q[rows, :] = dh_ * yhat
            return carry

        _chunks(tb, 2 * s8, gates, 0)

        qc[...] = _head_sums(qc[...], oc_ref[...]) * (1.0 / hd_c)
        ql[...] = _head_sums(ql[...], ol_ref[...]) * (1.0 / hd_l)
        yt_ref[...] = ybuf[...].T.astype(MM)

        c8 = RG_LRU_C * _log_sigmoid(spr(SP_LAM))

        def conv_mixer(r, dcz_n):
            rows16 = pl.ds(r, 2 * s8)
            bg16, cg16, xc16 = proj(rows16, P_B), proj(rows16, P_C), proj(rows16, P_XC)
            z16 = cg16 * xc16
            d_b, d_c, d_x = [None, None], [None, None], [None, None]
            for j in (1, 0):
                rows, sub = pl.ds(r + j * s8, s8), slice(j * s8, (j + 1) * s8)
                rstd = rcf[rows, :]
                yhat = yc[rows, :] * rstd
                dyc = rstd * (dyc_hat[rows, :] - yhat * qc[rows, :])
                d_b[j] = dyc * czs[rows, :]
                dcz = dyc * bg16[sub]
                up1, up2 = _shift_up(dcz, dcz_n, 1, row), _shift_up(dcz, dcz_n, 2, row)
                dz = w2 * dcz + w1 * up1 + w0 * up2
                d_c[j] = dz * xc16[sub]
                d_x[j] = dz * cg16[sub]
                z = z16[sub]
                acc_add(A_CONV_W, up2 * z)
                acc_add(A_CONV_W + 1, up1 * z)
                acc_add(A_CONV_W + 2, dcz * z)
                dcz_n = dcz
            put(rows16, P_B, d_b)
            put(rows16, P_C, d_c)
            put(rows16, P_XC, d_x)
            return dcz_n

        car_dcz[...] = _chunks(tb, 2 * s8, conv_mixer, car_dcz[...], reverse=True)

        def lru_mixer(r, carry):
            a_n, g_n = carry
            for j in (1, 0):
                rows = pl.ds(r + j * s8, s8)
                rstd = rlf[rows, :]
                hcur = hh[pl.ds(r + (j + 1) * s8, s8), :]
                hhat = hcur * rstd
                dh_out = rstd * (dyl_hat[rows, :] - hhat * ql[rows, :])
                ra = ra_ref[rows, :]
                la = ra * c8
                a = jnp.exp(la)
                g = _scan_bwd(_shift_up(a, a_n, 1, row), dh_out, g_n, row)
                da = g * _shift_down(hcur, hh[pl.ds(r + j * s8, s8), :], 1, row)
                ii = ii_ref[rows, :]
                uu = u[rows, :]
                mult = _lru_input_scale(la, a)
                dmult = g * (ii * uu)
                ds = g * mult
                dla = a * (da - dmult * a / mult)
                acc_add(A_LAM, dla * ra)
                dpa_ = dla * c8 * ra * (1.0 - ra)
                dpi_ = ds * uu * ii * (1.0 - ii)
                acc_add(A_B_A, dpa_)
                acc_add(A_B_I, dpi_)
                dpa[rows, :] = dpa_
                dpi[rows, :] = dpi_
                du[rows, :] = ds * ii
                a_n, g_n = a, _bcast_row(g, 0)
            return a_n, g_n

        a_f, g_f = _chunks(tb, 2 * s8, lru_mixer, (car_a[...], car_g[...]), reverse=True)
        car_a[...] = a_f
        car_g[...] = g_f

        dpab = dpa[...].astype(MM)
        dpib = dpi[...].astype(MM)
        for k in range(n_tiles):
            sl = slice(k * tw, (k + 1) * tw)
            du[:, sl] += _dot_nt(dpab[:, sl], wa_ref[k]) + _dot_nt(dpib[:, sl], wi_ref[k])
            ut = u[:, sl].T.astype(MM)
            gwa_ref[k] += _dot(ut, dpab[:, sl])
            gwi_ref[k] += _dot(ut, dpib[:, sl])

        def lru_conv(r, du_n):
            rows16 = pl.ds(r, 2 * s8)
            xl16 = proj(rows16, P_XL)
            d_xl = [None, None]
            for j in (1, 0):
                rows, sub = pl.ds(r + j * s8, s8), slice(j * s8, (j + 1) * s8)
                dut = du[rows, :]
                up1, up2, up3 = (_shift_up(dut, du_n, s, row) for s in (1, 2, 3))
                d_xl[j] = l3 * dut + l2 * up1 + l1 * up2 + l0 * up3
                xl = xl16[sub]
                acc_add(A_LRU_W, up3 * xl)
                acc_add(A_LRU_W + 1, up2 * xl)
                acc_add(A_LRU_W + 2, up1 * xl)
                acc_add(A_LRU_W + 3, dut * xl)
                acc_add(A_LRU_B, dut)
                du_n = dut
            put(rows16, P_XL, d_xl)
            return du_n

        car_du[...] = _chunks(tb, 2 * s8, lru_conv, car_du[...], reverse=True)

    vm = pl.BlockSpec(memory_space=pltpu.VMEM)
    rev = lambda w: pl.BlockSpec((tb, w), lambda i: (nb - 1 - i, 0))
    halo = lambda rows, w: pl.BlockSpec((rows, w), lambda i: (jnp.maximum((nb - 1 - i) * (tb // rows) - 1, 0), 0))
    const = lambda shape: pl.BlockSpec(shape, lambda i: (0,) * len(shape))
    buf = lambda w: pltpu.VMEM((tb, w), F32)
    car = pltpu.VMEM((SUBLANES, d), F32)
    return pl.pallas_call(
        body, name="backward", grid=(nb,),
        in_specs=[rev(6 * d), rev(d), halo(SUBLANES, d), rev(d)] + [rev(d)] * 5 + [vm, vm, vm, vm, vm, vm],
        out_specs=(rev(6 * d), pl.BlockSpec((2 * d, tb), lambda i: (0, nb - 1 - i)),
                   const((n_tiles, tw, tw)), const((n_tiles, tw, tw)), const((A_GROUPS * SUBLANES, d))),
        out_shape=(jax.ShapeDtypeStruct((t_len, 6 * d), MM),
                   jax.ShapeDtypeStruct((2 * d, t_len), MM),
                   jax.ShapeDtypeStruct((n_tiles, tw, tw), F32),
                   jax.ShapeDtypeStruct((n_tiles, tw, tw), F32),
                   jax.ShapeDtypeStruct((A_GROUPS * SUBLANES, d), F32)),
        scratch_shapes=[pltpu.VMEM((SUBLANES + tb, d), F32), buf(2 * d), buf(2 * d)] + [buf(d)] * 9 + [car, car, car, car],
        compiler_params=_params(dimension_semantics=("arbitrary",)),
    )(p, h, h, dh, *saved, wout, wa_t, wi_t, sp, ones_c, ones_l)


def _input_grad(dp, win_all, x, dh, sp, parts, tb):
    t_len, d = x.shape
    nb = t_len // tb
    cols = win_all.shape[2]
    n_parts = len(parts)

    def body(dp_ref, win_ref, x_ref, dh_ref, sp_ref, *refs):
        gx_ref, acc_ref = refs[n_parts:n_parts + 2]
        exchange = _ChipExchange(refs[:n_parts], refs[n_parts + 2:2 * n_parts + 2], *refs[2 * n_parts + 2:])
        i = pl.program_id(0)

        @pl.when(i == 0)
        def _():
            acc_ref[...] = jnp.zeros_like(acc_ref)
            exchange.start()

        dxn = _dot_nt(dp_ref[:, 0:cols], win_ref[0])
        for j in range(1, N_DEV):
            dxn += _dot_nt(dp_ref[:, j * cols:(j + 1) * cols], win_ref[j])
        xv = x_ref[...]
        r0 = lax.rsqrt(jnp.mean(xv * xv, axis=-1, keepdims=True) + RMS_EPS)
        xhat = xv * r0
        acc_ref[...] += (dxn * xhat).reshape(tb // SUBLANES, SUBLANES, d).sum(axis=0)
        dxh = dxn * sp_ref[SP_LN_G:SP_LN_G + 1, :]
        gx_ref[...] = dh_ref[...] + r0 * (dxh - xhat * jnp.mean(dxh * xhat, axis=-1, keepdims=True))

        @pl.when(i == nb - 1)
        def _():
            exchange.finish()

    vm = pl.BlockSpec(memory_space=pltpu.VMEM)
    hbm = pl.BlockSpec(memory_space=pl.ANY)
    blk = lambda w: pl.BlockSpec((tb, w), lambda i: (i, 0))
    outs = pl.pallas_call(
        body, name="input_grad", grid=(nb,),
        in_specs=[blk(6 * d), vm, blk(d), blk(d), vm] + [hbm] * n_parts,
        out_specs=(blk(d), pl.BlockSpec((SUBLANES, d), lambda i: (0, 0))) + (hbm,) * n_parts,
        out_shape=(jax.ShapeDtypeStruct((t_len, d), F32), jax.ShapeDtypeStruct((SUBLANES, d), F32))
                  + tuple(jax.ShapeDtypeStruct(p.shape, p.dtype) for p in parts),
        scratch_shapes=[pltpu.SemaphoreType.DMA((3 * n_parts,)), pltpu.SemaphoreType.DMA((3 * n_parts,))],
        compiler_params=_params(dimension_semantics=("arbitrary",)),
    )(dp, win_all, x, dh, sp, *parts)
    return outs[0], outs[1], outs[2:]


_CHIP_RELATIONS = [(0, 0), (1, 0), (0, 1), (1, 1)]


def _related_block(k, core):
    x, y, _ = _mesh_pos()
    fx, fy = _CHIP_RELATIONS[k]
    return 4 * (x ^ fx) + 2 * (y ^ fy) + core


class _ChipExchange:
    def __init__(self, part_refs, land_refs, send_sems, recv_sems):
        self.part_refs, self.land_refs, self.send_sems, self.recv_sems = part_refs, land_refs, send_sems, recv_sems

    def copies(self):
        x, y, c = _mesh_pos()
        for a in range(len(self.part_refs)):
            for k in (1, 2, 3):
                fx, fy = _CHIP_RELATIONS[k]
                yield pltpu.make_async_remote_copy(
                    src_ref=self.part_refs[a].at[k - 1], dst_ref=self.land_refs[a].at[k - 1],
                    send_sem=self.send_sems.at[3 * a + k - 1], recv_sem=self.recv_sems.at[3 * a + k - 1],
                    device_id=(x ^ fx, y ^ fy, c), device_id_type=MESH)

    def start(self):
        for cp in self.copies():
            cp.start()

    def finish(self):
        for cp in self.copies():
            cp.wait_recv()
        for cp in self.copies():
            cp.wait_send()


def _weight_grad_stage1(name, blk_shape, n_split, operands, in_specs, product, riders=()):
    n_rows, n_cols = blk_shape
    rs = n_rows // n_split
    rc = 32
    n_in, n_ride = len(operands), len(riders)
    _, _, c = _mesh_pos()
    order = jnp.stack([_related_block(k, 1 - c) for k in range(4)]
                      + [_related_block(k, c) for k in (1, 2, 3, 0)]).astype(jnp.int32)

    def body(order_ref, *refs):
        ins = refs[:n_in]
        ride_in = refs[n_in:n_in + n_ride]
        part_ref, own_ref, from_sib = refs[n_in + n_ride:n_in + n_ride + 3]
        ride_out = refs[n_in + n_ride + 3:n_in + 2 * n_ride + 3]
        gbuf, sendbuf, tmp, send_sems, recv_sems, local_sems, ride_send, ride_recv = refs[n_in + 2 * n_ride + 3:]
        exchange = _ChipExchange(ride_in, ride_out, ride_send, ride_recv)
        s = pl.program_id(0)
        x, y, c = _mesh_pos()

        def to_sibling(k):
            return pltpu.make_async_remote_copy(
                src_ref=sendbuf.at[k], dst_ref=from_sib.at[k], send_sem=send_sems.at[k], recv_sem=recv_sems.at[k],
                device_id=(x, y, 1 - c), device_id_type=MESH)

        if n_ride:
            @pl.when(s == 0)
            def _():
                exchange.start()

        for h in range(n_split):
            gbuf[h * rs:(h + 1) * rs, :] = product(ins, h)

        @pl.when(s < 4)
        def _():
            def narrow(r, carry):
                sendbuf[s, pl.ds(r, rc), :] = gbuf[pl.ds(r, rc), :].astype(MM)
                return carry

            _chunks(n_rows, rc, narrow, 0)
            to_sibling(s).start()

        @pl.when(s >= 4)
        def _():
            k = jnp.where(s == 7, 0, s - 3)
            to_sibling(k).wait_recv()
            load = pltpu.make_async_copy(from_sib.at[k], tmp, local_sems.at[0])
            load.start()
            load.wait()

            @pl.when(s < 7)
            def _():
                def add(r, carry):
                    rows = pl.ds(r, rc)
                    part_ref[0, rows, :] = (gbuf[rows, :] + tmp[rows, :].astype(F32)).astype(MM)
                    return carry

                _chunks(n_rows, rc, add, 0)

            @pl.when(s == 7)
            def _():
                def add(r, carry):
                    rows = pl.ds(r, rc)
                    own_ref[rows, :] = gbuf[rows, :] + tmp[rows, :].astype(F32)
                    return carry

                _chunks(n_rows, rc, add, 0)
                for kk in range(4):
                    to_sibling(kk).wait_send()
                if n_ride:
                    exchange.finish()

    hbm = pl.BlockSpec(memory_space=pl.ANY)
    grid_spec = pltpu.PrefetchScalarGridSpec(
        num_scalar_prefetch=1, grid=(N_DEV,), in_specs=list(in_specs) + [hbm] * n_ride,
        out_specs=(pl.BlockSpec((1, n_rows, n_cols), lambda s, o: (jnp.clip(s - 4, 0, 2), 0, 0)),
                   pl.BlockSpec((n_rows, n_cols), lambda s, o: (0, 0)), hbm) + (hbm,) * n_ride,
        scratch_shapes=[pltpu.VMEM((n_rows, n_cols), F32), pltpu.VMEM((4, n_rows, n_cols), MM),
                        pltpu.VMEM((n_rows, n_cols), MM),
                        pltpu.SemaphoreType.DMA((4,)), pltpu.SemaphoreType.DMA((4,)), pltpu.SemaphoreType.DMA((1,)),
                        pltpu.SemaphoreType.DMA((max(3 * n_ride, 1),)), pltpu.SemaphoreType.DMA((max(3 * n_ride, 1),))])
    outs = pl.pallas_call(
        body, name=name, grid_spec=grid_spec,
        out_shape=(jax.ShapeDtypeStruct((3, n_rows, n_cols), MM), jax.ShapeDtypeStruct((n_rows, n_cols), F32),
                   jax.ShapeDtypeStruct((4, n_rows, n_cols), MM))
                  + tuple(jax.ShapeDtypeStruct(p.shape, p.dtype) for p in riders),
        compiler_params=_params(dimension_semantics=("arbitrary",)),
    )(order, *operands, *riders)
    return outs[0], outs[1], outs[3:]


def _weight_grad_in(xnt, dp, riders):
    d, t_len = xnt.shape
    cols = dp.shape[1] // N_DEV
    half = d // 2
    return _weight_grad_stage1(
        "weight_grad_in", (d, cols), 2, (xnt, dp),
        [pl.BlockSpec(memory_space=pltpu.VMEM), pl.BlockSpec((t_len, cols), lambda s, o: (0, o[s]))],
        lambda refs, h: _dot(refs[0][h * half:(h + 1) * half, :], refs[1][...]), riders)


def _weight_grad_out(yt, dhb):
    d2, t_len = yt.shape
    d = dhb.shape[1]
    rows = d2 // N_DEV
    return _weight_grad_stage1(
        "weight_grad_out", (rows, d), 1, (yt, dhb),
        [pl.BlockSpec((rows, t_len), lambda s, o: (o[s], 0)), pl.BlockSpec(memory_space=pltpu.VMEM)],
        lambda refs, h: _dot(refs[0][...], refs[1][...]))


def _update_shard(own, from_chips, w, m, v, name):
    n_rows, n_cols = w.shape
    rb = min(256, n_rows)

    def body(own_ref, fc_ref, w_ref, m_ref, v_ref, grad_ref, delta_ref, mo_ref, vo_ref):
        g = own_ref[...]
        for k in range(3):
            g = g + fc_ref[k].astype(F32)
        delta, m_new, v_new = _adamw(w_ref[...], g, m_ref[...], v_ref[...])
        grad_ref[...] = g
        delta_ref[...] = delta
        mo_ref[...] = m_new
        vo_ref[...] = v_new

    blk = pl.BlockSpec((rb, n_cols), lambda i: (i, 0))
    out = jax.ShapeDtypeStruct((n_rows, n_cols), F32)
    return pl.pallas_call(
        body, name=name, grid=(n_rows // rb,),
        in_specs=[blk, pl.BlockSpec((3, rb, n_cols), lambda i: (0, i, 0)), blk, blk, blk],
        out_specs=(blk, blk, blk, blk), out_shape=(out, out, out, out),
        compiler_params=_params(dimension_semantics=("arbitrary",)),
    )(own, from_chips, w, m, v)


def _small_update(facc, xacc, bacc, gwa, gwi, lam, wsl, msl, vsl, cw, cm, cv):
    n_rows, d = wsl.shape
    s8 = SUBLANES
    cc = cw.shape[1]

    def body(facc_ref, xacc_ref, bacc_ref, gwa_ref, gwi_ref, lam_ref, w_ref, m_ref, v_ref, cw_ref, cm_ref, cv_ref,
             g_o, d_o, m_o, v_o, cg_o, cd_o, cm_o, cv_o, gat, send_sems, recv_sems):
        x, y, c = _mesh_pos()
        me = 4 * x + 2 * y + c

        def rowsum(ref, group):
            return jnp.sum(ref[group * s8:(group + 1) * s8, :], axis=0, keepdims=True)

        mine = gat.at[me]
        loss = jnp.sum(rowsum(facc_ref, 0), axis=1, keepdims=True) * (0.5 / d)
        mine[SL_LOSS:SL_LOSS + 1, :] = jnp.broadcast_to(loss, (1, d))
        mine[SL_LN_G:SL_LN_G + 1, :] = rowsum(xacc_ref, 0)
        mine[SL_LRU_B:SL_LRU_B + 1, :] = rowsum(bacc_ref, A_LRU_B)
        mine[SL_B_A:SL_B_A + 1, :] = rowsum(bacc_ref, A_B_A)
        mine[SL_B_I:SL_B_I + 1, :] = rowsum(bacc_ref, A_B_I)
        mine[SL_LAM:SL_LAM + 1, :] = rowsum(bacc_ref, A_LAM) * (RG_LRU_C * jax.nn.sigmoid(-lam_ref[...]))
        mine[SL_CONV_G:SL_CONV_G + 1, :] = rowsum(bacc_ref, A_CONV_G)
        mine[SL_LRU_G:SL_LRU_G + 1, :] = rowsum(bacc_ref, A_LRU_G)
        mine[SL_FINAL_G:SL_FINAL_G + 1, :] = rowsum(facc_ref, 1)
        for k in range(3):
            mine[SL_CONV_W + k:SL_CONV_W + k + 1, :] = rowsum(bacc_ref, A_CONV_W + k)
        for k in range(4):
            mine[SL_LRU_W + k:SL_LRU_W + k + 1, :] = rowsum(bacc_ref, A_LRU_W + k)
        na = gwa_ref.shape[0]
        mine[SL_W_A:SL_W_A + na, :] = gwa_ref[...]
        mine[SL_W_A + na:SL_W_A + 2 * na, :] = gwi_ref[...]

        gather = _Gather(lambda a, px, py, pc: gat.at[4 * px + 2 * py + pc], send_sems, recv_sems)
        gather.start_own(0)
        gather.finish(0)

        def update(r, carry):
            rows = pl.ds(r, s8)
            g = gat[0, rows, :]
            for b in range(1, N_DEV):
                g = g + gat[b, rows, :]
            delta, m_new, v_new = _adamw(w_ref[rows, :], g, m_ref[rows, :], v_ref[rows, :])
            g_o[rows, :] = g
            d_o[rows, :] = delta
            m_o[rows, :] = m_new
            v_o[rows, :] = v_new
            return carry

        _chunks(n_rows, s8, update, 0)
        gc = g_o[s8:2 * s8, pl.ds(pl.multiple_of(me * cc, cc), cc)]
        delta, m_new, v_new = _adamw(cw_ref[...], gc, cm_ref[...], cv_ref[...])
        cg_o[...] = gc
        cd_o[...] = delta
        cm_o[...] = m_new
        cv_o[...] = v_new

    vm = pl.BlockSpec(memory_space=pltpu.VMEM)
    big = jax.ShapeDtypeStruct((n_rows, d), F32)
    small = jax.ShapeDtypeStruct(cw.shape, F32)
    return pl.pallas_call(
        body, name="small_update",
        in_specs=[vm] * 12, out_specs=(vm,) * 8, out_shape=(big, big, big, big, small, small, small, small),
        scratch_shapes=[pltpu.VMEM((N_DEV, n_rows, d), F32), pltpu.SemaphoreType.DMA((7,)), pltpu.SemaphoreType.DMA((7,))],
        compiler_params=_params(),
    )(facc, xacc, bacc, gwa, gwi, lam, wsl, msl, vsl, cw, cm, cv)


def _head_ones(head_dim, tw):
    lane = jnp.arange(tw) // head_dim
    return (lane[:, None] == lane[None, :]).astype(MM)


def _gate_tiles(w, tw):
    n_heads, hd, _ = w.shape
    per = tw // hd
    w4 = w.reshape(n_heads // per, per, hd, hd)
    eye = jnp.eye(per, dtype=w.dtype)
    return (w4[:, :, :, None, :] * eye[None, :, None, :, None]).reshape(n_heads // per, tw, tw)


def _gate_blocks(tiles, n_heads, hd):
    n_tiles, tw, _ = tiles.shape
    per = tw // hd
    t5 = tiles.reshape(n_tiles, per, hd, per, hd)
    diag = jnp.stack([t5[:, a, :, a, :] for a in range(per)], axis=1)
    return diag.reshape(hd, n_heads * hd)


def kernel(x, ln_g, w_in, conv_w, lru_conv_w, lru_conv_b, w_a, b_a, w_i, b_i, lam, conv_out_g, lru_out_g, w_out, final_g, loss_target, m_ln_g, m_w_in, m_conv_w, m_lru_conv_w, m_lru_conv_b, m_w_a, m_b_a, m_w_i, m_b_i, m_lam, m_conv_out_g, m_lru_out_g, m_w_out, m_final_g, v_ln_g, v_w_in, v_conv_w, v_lru_conv_w, v_lru_conv_b, v_w_a, v_b_a, v_w_i, v_b_i, v_lam, v_conv_out_g, v_lru_out_g, v_w_out, v_final_g):
    _, t_len, d = x.shape
    hd_l = d // N_LRU_HEADS
    tw = min(MXU_TILE, d)
    cc = conv_w.shape[1]
    x2, tgt2 = x[0], loss_target[0]

    def conv_rows(cw3, lw4):
        return jnp.concatenate([jnp.zeros((1, cc), F32), cw3, lw4], axis=0)

    p, xnt, win_all, wout_all, conv_all = _gather_project(
        x2, w_in, w_out, conv_rows(conv_w, lru_conv_w), ln_g.reshape(1, d), min(256, t_len))
    wout_full = wout_all.reshape(N_DEV * w_out.shape[0], d)
    conv_full = conv_all.transpose(1, 0, 2).reshape(SUBLANES, d)
    small = [ln_g, lru_conv_b, b_a, b_i, lam, conv_out_g, lru_out_g, final_g]
    sp = jnp.concatenate([jnp.stack(small), conv_full[1:], jnp.zeros((1, d), F32)], axis=0)
    wa_t, wi_t = _gate_tiles(w_a, tw).astype(MM), _gate_tiles(w_i, tw).astype(MM)
    ones_c, ones_l = _head_ones(d // N_CONV_HEADS, tw), _head_ones(hd_l, tw)

    h, dh, dhb, facc, *saved = _forward(x2, tgt2, p, wout_full, wa_t, wi_t, sp, ones_c, ones_l, min(256, t_len))
    dp, yt, gwa_t, gwi_t, bacc = _backward(p, h, dh, saved, wout_full, wa_t, wi_t, sp, ones_c, ones_l, min(256, t_len))
    part_out, own_out, _ = _weight_grad_out(yt, dhb)
    part_in, own_in, (chips_out,) = _weight_grad_in(xnt, dp, (part_out,))
    grad_x, xacc, (chips_in,) = _input_grad(dp, win_all, x2, dh, sp, (part_in,), min(512, t_len))
    gw_in, dw_in, mw_in, vw_in = _update_shard(own_in, chips_in, w_in, m_w_in, v_w_in, "update_w_in")
    gw_out, dw_out, mw_out, vw_out = _update_shard(own_out, chips_out, w_out, m_w_out, v_w_out, "update_w_out")

    def slab(parts, wa_, wi_):
        return jnp.concatenate([jnp.zeros((1, d), F32), jnp.stack(parts), jnp.zeros((SL_W_A - SL_CONV_W, d), F32),
                                wa_.reshape(-1, d), wi_.reshape(-1, d)], axis=0)

    wsl = slab(small, w_a, w_i)
    msl = slab([m_ln_g, m_lru_conv_b, m_b_a, m_b_i, m_lam, m_conv_out_g, m_lru_out_g, m_final_g], m_w_a, m_w_i)
    vsl = slab([v_ln_g, v_lru_conv_b, v_b_a, v_b_i, v_lam, v_conv_out_g, v_lru_out_g, v_final_g], v_w_a, v_w_i)
    outs = _small_update(
        facc, xacc, bacc, _gate_blocks(gwa_t, N_LRU_HEADS, hd_l), _gate_blocks(gwi_t, N_LRU_HEADS, hd_l),
        lam.reshape(1, d), wsl, msl, vsl,
        conv_rows(conv_w, lru_conv_w), conv_rows(m_conv_w, m_lru_conv_w), conv_rows(v_conv_w, v_lru_conv_w))
    sl_g, sl_d, sl_m, sl_v, c_g, c_d, c_m, c_v = outs
    loss = sl_g[SL_LOSS, 0]
    na = w_a.size // d

    def unpack(sl, cv, big_in, big_out):
        one = lambda r: sl[r]
        return [one(SL_LN_G), big_in, cv[1:4], cv[4:8], one(SL_LRU_B),
                sl[SL_W_A:SL_W_A + na].reshape(w_a.shape), one(SL_B_A),
                sl[SL_W_A + na:SL_W_A + 2 * na].reshape(w_i.shape), one(SL_B_I), one(SL_LAM),
                one(SL_CONV_G), one(SL_LRU_G), big_out, one(SL_FINAL_G)]

    return (loss, grad_x[None], *unpack(sl_g, c_g, gw_in, gw_out), *unpack(sl_d, c_d, dw_in, dw_out),
            *unpack(sl_m, c_m, mw_in, mw_out), *unpack(sl_v, c_v, vw_in, vw_out))
```

```python
import functools

import jax
import jax.numpy as jnp
from jax import lax
from jax.experimental import pallas as pl
from jax.experimental.pallas import tpu as pltpu

F32 = jnp.float32
MM = jnp.bfloat16
MESH = pl.DeviceIdType.MESH

N_DEV = 8
N_CONV_HEADS = 8
N_LRU_HEADS = 16
RG_LRU_C = 8.0
RMS_EPS = 1e-6
ADAM_LR, ADAM_B1, ADAM_B2, ADAM_EPS, ADAM_WD, ADAM_STEP = 0.001, 0.9, 0.999, 1e-08, 0.01, 10
ADAM_BC1 = 1.0 - ADAM_B1 ** ADAM_STEP
ADAM_BC2 = 1.0 - ADAM_B2 ** ADAM_STEP

SUBLANES = 8
LANES = 128
MXU_TILE = 256
VMEM_LIMIT = 56 * 1024 * 1024

SP_LN_G, SP_LRU_B, SP_B_A, SP_B_I, SP_LAM, SP_CONV_G, SP_LRU_G, SP_FINAL_G, SP_CONV_W, SP_LRU_W = 0, 1, 2, 3, 4, 5, 6, 7, 8, 11
SP_ROWS = 16
P_B, P_C, P_XC, P_GC, P_XL, P_GL = 0, 1, 2, 3, 4, 5
A_CONV_G, A_LRU_G, A_LAM, A_B_A, A_B_I, A_CONV_W, A_LRU_W, A_LRU_B = 0, 1, 2, 3, 4, 5, 8, 12
A_GROUPS = 13
SL_LOSS, SL_LN_G, SL_LRU_B, SL_B_A, SL_B_I, SL_LAM, SL_CONV_G, SL_LRU_G, SL_FINAL_G, SL_CONV_W, SL_LRU_W, SL_W_A = 0, 1, 2, 3, 4, 5, 6, 7, 8, 9, 12, 16


def _params(vmem=True, **kw):
    if vmem:
        kw["vmem_limit_bytes"] = VMEM_LIMIT
    return pltpu.CompilerParams(**kw)


def _dot(a, b):
    return jnp.dot(a, b, preferred_element_type=F32)


def _dot_nt(a, b):
    return lax.dot_general(a, b, (((1,), (1,)), ((), ())), preferred_element_type=F32)


def _head_sums(v, ones_tile):
    tw = ones_tile.shape[0]
    vb = v.astype(MM)
    return jnp.concatenate([_dot(vb[:, k:k + tw], ones_tile) for k in range(0, v.shape[1], tw)], axis=1)


def _head_rstd(v, ones_tile, head_dim):
    return lax.rsqrt(_head_sums(v * v, ones_tile) * (1.0 / head_dim) + RMS_EPS)


def _sigmoid(x):
    return 0.5 * jnp.tanh(0.5 * x) + 0.5


def _lru_input_scale(log_a, a):
    return jnp.sqrt(-jnp.tanh(log_a) * (1.0 + a * a))


def _log_sigmoid(x):
    z = jnp.exp(-jnp.abs(x))
    u = 1.0 + z
    log1p_z = jnp.where(u == 1.0, z, jnp.log(u) * (z / (u - 1.0)))
    return jnp.minimum(x, 0.0) - log1p_z


def _row_iota(d):
    return lax.broadcasted_iota(jnp.int32, (SUBLANES, d), 0)


def _shift_down(cur, prev, s, row):
    return jnp.where(row >= s, pltpu.roll(cur, s, axis=0), pltpu.roll(prev, s, axis=0))


def _shift_up(cur, nxt, s, row):
    k = SUBLANES - s
    return jnp.where(row < k, pltpu.roll(cur, k, axis=0), pltpu.roll(nxt, k, axis=0))


def _scan_fwd(a, b, h_prev, row):
    for s in (1, 2, 4):
        a_s = jnp.where(row >= s, pltpu.roll(a, s, axis=0), 1.0)
        b_s = jnp.where(row >= s, pltpu.roll(b, s, axis=0), 0.0)
        b = a * b_s + b
        a = a * a_s
    return a * h_prev + b


def _scan_bwd(a_next, b, g_next, row):
    a = a_next
    for s in (1, 2, 4):
        k = SUBLANES - s
        a_s = jnp.where(row < k, pltpu.roll(a, k, axis=0), 1.0)
        b_s = jnp.where(row < k, pltpu.roll(b, k, axis=0), 0.0)
        b = a * b_s + b
        a = a * a_s
    return a * g_next + b


def _bcast_row(v, r):
    return jnp.broadcast_to(v[r:r + 1, :], v.shape)


def _chunks(n_rows, rc, body, init, reverse=False):
    n = n_rows // rc

    def step(i, carry):
        j = (n - 1 - i) if reverse else i
        return body(pl.multiple_of(j * rc, rc), carry)

    return lax.fori_loop(0, n, step, init)


def _adamw(w, g, m, v):
    m = ADAM_B1 * m + (1.0 - ADAM_B1) * g
    v = ADAM_B2 * v + (1.0 - ADAM_B2) * (g * g)
    m_hat = m / ADAM_BC1
    v_hat = v / ADAM_BC2
    delta = -ADAM_LR * (m_hat / (jnp.sqrt(v_hat) + ADAM_EPS) + ADAM_WD * w)
    return delta, m, v


def _mesh_pos():
    return lax.axis_index("x"), lax.axis_index("y"), lax.axis_index("c")


class _Gather:
    def __init__(self, blocks_of, send_sems, recv_sems, own_src=None):
        x, y, c = _mesh_pos()
        self.c = c
        self.me, self.sibling = (x, y, c), (x, y, 1 - c)
        self.chips = [(1 - x, y), (x, 1 - y), (1 - x, 1 - y)]
        self.blocks_of, self.send_sems, self.recv_sems = blocks_of, send_sems, recv_sems
        self.own_src = own_src

    def copy(self, a, k, block, to):
        src = self.blocks_of(a, *block)
        if block is self.me and self.own_src is not None:
            src = self.own_src[a]
        return pltpu.make_async_remote_copy(
            src_ref=src, dst_ref=self.blocks_of(a, *block),
            send_sem=self.send_sems.at[a * 7 + k], recv_sem=self.recv_sems.at[a * 7 + k],
            device_id=to, device_id_type=MESH)

    def start_own(self, a):
        self.copy(a, 0, self.me, self.sibling).start()
        for j, chip in enumerate(self.chips):
            self.copy(a, 1 + j, self.me, (*chip, self.c)).start()

    def wait_sibling(self, a):
        self.copy(a, 0, self.sibling, self.me).wait_recv()

    def wait_chip_and_pass_on(self, a, j):
        block = (*self.chips[j], self.c)
        self.copy(a, 1 + j, block, self.me).wait_recv()
        self.copy(a, 4 + j, block, self.sibling).start()

    def wait_passed_on(self, a, j):
        self.copy(a, 4 + j, (*self.chips[j], 1 - self.c), self.me).wait_recv()

    def wait_sends(self, a):
        self.copy(a, 0, self.me, self.sibling).wait_send()
        for j, chip in enumerate(self.chips):
            self.copy(a, 1 + j, self.me, (*chip, self.c)).wait_send()
            self.copy(a, 4 + j, (*chip, self.c), self.sibling).wait_send()

    def finish(self, a):
        for j in range(3):
            self.wait_chip_and_pass_on(a, j)
        self.wait_sibling(a)
        for j in range(3):
            self.wait_passed_on(a, j)
        self.wait_sends(a)


class _BalancedGather:
    def __init__(self, slot, send_sems, recv_sems, own_src):
        x, y, c = _mesh_pos()
        self.c = c
        self.me, self.sibling = (x, y, c), (x, y, 1 - c)
        self.chips = [(1 - x, y), (x, 1 - y), (1 - x, 1 - y)]
        self.slot, self.send_sems, self.recv_sems, self.own_src = slot, send_sems, recv_sems, own_src

    def half(self, a, block, which):
        ref = self.slot(a, *block)
        n = ref.shape[0] // 2
        return ref.at[pl.ds(which * n, n)]

    def copy(self, a, k, src, dst, to):
        return pltpu.make_async_remote_copy(
            src_ref=src, dst_ref=dst, send_sem=self.send_sems.at[a * 8 + k], recv_sem=self.recv_sems.at[a * 8 + k],
            device_id=to, device_id_type=MESH)

    def whole(self, a, k, block, to):
        src = self.own_src[a] if block is self.me else self.slot(a, *block)
        return self.copy(a, k, src, self.slot(a, *block), to)

    def halved(self, a, k, block, which, to):
        return self.copy(a, k, self.half(a, block, which), self.half(a, block, which), to)

    def on(self, chip):
        return (*self.chips[chip], self.c)

    def start_own(self, a):
        self.whole(a, 0, self.me, self.sibling).start()
        self.whole(a, 1, self.me, self.on(0)).start()
        self.whole(a, 2, self.me, self.on(1)).start()

    def wait_sibling(self, a):
        self.whole(a, 0, self.sibling, self.me).wait_recv()

    def on_neighbour(self, a, j):
        self.whole(a, 1 + j, self.on(j), self.me).wait_recv()
        self.halved(a, 3 + j, self.on(j), j, self.on(1 - j)).start()
        self.whole(a, 5 + j, self.on(j), self.sibling).start()

    def on_diagonal(self, a):
        self.halved(a, 3, self.on(2), 0, self.me).wait_recv()
        self.halved(a, 4, self.on(2), 1, self.me).wait_recv()
        self.whole(a, 7, self.on(2), self.sibling).start()

    def wait_passed_on(self, a, j):
        self.whole(a, 5 + j, (*self.chips[j], 1 - self.c), self.me).wait_recv()

    def wait_sends(self, a):
        self.whole(a, 0, self.me, self.sibling).wait_send()
        for j in range(2):
            self.whole(a, 1 + j, self.me, self.on(j)).wait_send()
            self.halved(a, 3 + j, self.on(j), j, self.on(1 - j)).wait_send()
        for j in range(3):
            self.whole(a, 5 + j, self.on(j), self.sibling).wait_send()


def _block_order():
    x, y, c = _mesh_pos()
    chips = [(x, y), (1 - x, y), (x, 1 - y), (1 - x, 1 - y)]
    return jnp.stack([4 * px + 2 * py + pc for px, py in chips for pc in (c, 1 - c)]).astype(jnp.int32)


def _gather_project(x, w_in, w_out, conv_pack, ln_g, tb):
    t_len, d = x.shape
    nb = t_len // tb
    cols = w_in.shape[1]
    mc = min(512, t_len)
    srcs = (w_in, w_out, conv_pack)
    dts = (MM, MM, F32)

    def body(order_ref, x_ref, win_ref, wout_ref, cp_ref, lng_ref, p_ref, xnt_ref, win_all, wout_all, cp_all,
             xnb, st_in, st_out, st_cp, wbuf, send_sems, recv_sems, cp_send, cp_recv, local_sems):
        i = pl.program_id(0)
        x_, y_, c_ = _mesh_pos()
        me = 4 * x_ + 2 * y_ + c_
        outs = (win_all, wout_all, cp_all)
        stages = (st_in, st_out, st_cp)
        gather = _BalancedGather(lambda a, px, py, pc: outs[a].at[4 * px + 2 * py + pc], send_sems, recv_sems, stages)
        small = _Gather(lambda a, px, py, pc: cp_all.at[4 * px + 2 * py + pc], cp_send, cp_recv, own_src=[st_cp])
        keep_own = [pltpu.make_async_copy(stages[a], outs[a].at[me], local_sems.at[a]) for a in range(3)]

        @pl.when(i == 0)
        def _():
            for a, (src, dst) in enumerate(zip((win_ref, wout_ref, cp_ref), stages)):
                rows = src.shape[0]
                rc = min(rows, 32)

                def cast(r, carry, src=src, dst=dst, rc=rc):
                    dst[pl.ds(r, rc), :] = src[pl.ds(r, rc), :].astype(dst.dtype)
                    return carry

                _chunks(rows, rc, cast, 0)
                if a < 2:
                    gather.start_own(a)
                else:
                    small.start_own(0)
                keep_own[a].start()

        @pl.when(i < nb)
        def _():
            xv = x_ref[...]
            r0 = lax.rsqrt(jnp.mean(xv * xv, axis=-1, keepdims=True) + RMS_EPS)
            xn = xv * r0 * lng_ref[...]
            xnb[pl.ds(pl.multiple_of(i * tb, tb), tb), :] = xn.astype(MM)
            xnt_ref[...] = xn.T.astype(MM)

        for k in range(N_DEV):
            @pl.when(i == nb + k)
            def _(k=k):
                if k == 1:
                    gather.wait_sibling(0)
                elif k == 2:
                    gather.on_neighbour(0, 0)
                    gather.on_neighbour(0, 1)
                elif k in (3, 5, 7):
                    gather.wait_passed_on(0, (k - 3) // 2)
                    if k == 3:
                        gather.on_neighbour(1, 0)
                        gather.on_neighbour(1, 1)
                    if k == 7:
                        gather.on_diagonal(1)
                elif k == 6:
                    gather.on_diagonal(0)
                if k == 0:
                    w_blk = st_in
                else:
                    load = pltpu.make_async_copy(win_all.at[order_ref[k]], wbuf, local_sems.at[3])
                    load.start()
                    load.wait()
                    w_blk = wbuf

                def project(r, carry):
                    rows = pl.ds(r, mc)
                    p_ref[rows, :] = _dot(xnb[rows, :], w_blk[...]).astype(MM)
                    return carry

                _chunks(t_len, mc, project, 0)
                if k == N_DEV - 1:
                    gather.wait_sends(0)
                    gather.wait_sibling(1)
                    for j in range(3):
                        gather.wait_passed_on(1, j)
                    gather.wait_sends(1)
                    small.finish(0)
                    for cp in keep_own:
                        cp.wait()

    vm = pl.BlockSpec(memory_space=pltpu.VMEM)
    hbm = pl.BlockSpec(memory_space=pl.ANY)
    grid_spec = pltpu.PrefetchScalarGridSpec(
        num_scalar_prefetch=1, grid=(nb + N_DEV,),
        in_specs=[pl.BlockSpec((tb, d), lambda i, o: (jnp.minimum(i, nb - 1), 0)), vm, vm, vm, vm],
        out_specs=(pl.BlockSpec((t_len, cols), lambda i, o: (0, o[jnp.maximum(i - nb, 0)])),
                   pl.BlockSpec((d, tb), lambda i, o: (0, jnp.minimum(i, nb - 1))), hbm, hbm, hbm),
        scratch_shapes=[pltpu.VMEM((t_len, d), MM)] + [pltpu.VMEM(s.shape, dt) for s, dt in zip(srcs, dts)]
                       + [pltpu.VMEM(w_in.shape, MM),
                          pltpu.SemaphoreType.DMA((16,)), pltpu.SemaphoreType.DMA((16,)),
                          pltpu.SemaphoreType.DMA((7,)), pltpu.SemaphoreType.DMA((7,)), pltpu.SemaphoreType.DMA((4,))])
    return pl.pallas_call(
        body, name="gather_project", grid_spec=grid_spec,
        out_shape=(jax.ShapeDtypeStruct((t_len, N_DEV * cols), MM),
                   jax.ShapeDtypeStruct((d, t_len), MM))
                  + tuple(jax.ShapeDtypeStruct((N_DEV,) + s.shape, dt) for s, dt in zip(srcs, dts)),
        compiler_params=_params(dimension_semantics=("arbitrary",)),
    )(_block_order(), x, w_in, w_out, conv_pack, ln_g)


def _forward(x, tgt, p, wout, wa_t, wi_t, sp, ones_c, ones_l, tb):
    t_len, d = x.shape
    nb = t_len // tb
    n_tiles, tw = wa_t.shape[0], wa_t.shape[1]
    hd_c, hd_l = d // N_CONV_HEADS, d // N_LRU_HEADS
    s8 = SUBLANES

    def body(x_ref, tgt_ref, p_ref, wout_ref, wa_ref, wi_ref, sp_ref, oc_ref, ol_ref,
             h_ref, dh_ref, dhb_ref, acc_ref, yc, czs, u, pa, pi,
             rcf, rlf, ybuf, tail_z, tail_xl, hcar):
        i = pl.program_id(0)
        row = _row_iota(d)

        @pl.when(i == 0)
        def _():
            tail_z[...] = jnp.zeros_like(tail_z)
            tail_xl[...] = jnp.zeros_like(tail_xl)
            hcar[...] = jnp.zeros_like(hcar)
            acc_ref[...] = jnp.zeros_like(acc_ref)

        def spr(r):
            return sp_ref[r:r + 1, :]

        def proj(rows, seg):
            return p_ref[rows, seg * d:(seg + 1) * d].astype(F32)

        w0, w1, w2 = spr(SP_CONV_W), spr(SP_CONV_W + 1), spr(SP_CONV_W + 2)
        l0, l1, l2, l3 = spr(SP_LRU_W), spr(SP_LRU_W + 1), spr(SP_LRU_W + 2), spr(SP_LRU_W + 3)
        lb = spr(SP_LRU_B)

        def convs(r, carry):
            zp, xp = carry
            rows16 = pl.ds(r, 2 * s8)
            bg16, xl16 = proj(rows16, P_B), proj(rows16, P_XL)
            z16 = proj(rows16, P_C) * proj(rows16, P_XC)
            for j in range(2):
                rows, sub = pl.ds(r + j * s8, s8), slice(j * s8, (j + 1) * s8)
                z, xl = z16[sub], xl16[sub]
                cz = w0 * _shift_down(z, zp, 2, row) + w1 * _shift_down(z, zp, 1, row) + w2 * z
                czs[rows, :] = cz
                yc[rows, :] = bg16[sub] * cz
                u[rows, :] = (l0 * _shift_down(xl, xp, 3, row) + l1 * _shift_down(xl, xp, 2, row)
                              + l2 * _shift_down(xl, xp, 1, row) + l3 * xl + lb)
                zp, xp = z, xl
            return zp, xp

        z_last, xl_last = _chunks(tb, 2 * s8, convs, (tail_z[...], tail_xl[...]))
        tail_z[...] = z_last
        tail_xl[...] = xl_last

        ub = u[...].astype(MM)
        for k in range(n_tiles):
            sl = slice(k * tw, (k + 1) * tw)
            pa[:, sl] = _dot(ub[:, sl], wa_ref[k])
            pi[:, sl] = _dot(ub[:, sl], wi_ref[k])
        rcf[...] = _head_rstd(yc[...], oc_ref[...], hd_c)

        c8 = RG_LRU_C * _log_sigmoid(spr(SP_LAM))
        b_a, b_i = spr(SP_B_A), spr(SP_B_I)

        def lru(r, hp):
            rows = pl.ds(r, SUBLANES)
            ra = _sigmoid(pa[rows, :] + b_a)
            ii = _sigmoid(pi[rows, :] + b_i)
            pa[rows, :] = ra
            pi[rows, :] = ii
            la = ra * c8
            a = jnp.exp(la)
            mult = _lru_input_scale(la, a)
            h = _scan_fwd(a, mult * (ii * u[rows, :]), hp, row)
            h_ref[rows, :] = h
            return _bcast_row(h, SUBLANES - 1)

        hcar[...] = _chunks(tb, SUBLANES, lru, hcar[...])
        rlf[...] = _head_rstd(h_ref[...], ol_ref[...], hd_l)

        g_c, g_l = spr(SP_CONV_G), spr(SP_LRU_G)

        def gate(r, carry):
            rows = pl.ds(r, 2 * s8)
            gc, gl = proj(rows, P_GC), proj(rows, P_GL)
            ybuf[rows, 0:d] = (yc[rows, :] * rcf[rows, :] * g_c * (gc * _sigmoid(gc))).astype(MM)
            ybuf[rows, d:2 * d] = (h_ref[rows, :] * rlf[rows, :] * g_l * (gl * _sigmoid(gl))).astype(MM)
            return carry

        _chunks(tb, 2 * s8, gate, 0)

        hres = x_ref[...] + _dot(ybuf[...], wout_ref[...])
        rf = lax.rsqrt(jnp.mean(hres * hres, axis=-1, keepdims=True) + RMS_EPS)
        hn = hres * rf
        fg = spr(SP_FINAL_G)
        err = hn * fg - tgt_ref[...]
        dout = err * (1.0 / d)
        acc_ref[0:SUBLANES, :] += (err * err).reshape(tb // SUBLANES, SUBLANES, d).sum(axis=0)
        acc_ref[SUBLANES:2 * SUBLANES, :] += (dout * hn).reshape(tb // SUBLANES, SUBLANES, d).sum(axis=0)
        gd = dout * fg
        dhres = rf * (gd - hn * jnp.mean(gd * hn, axis=-1, keepdims=True))
        dh_ref[...] = dhres
        dhb_ref[...] = dhres.astype(MM)

    vm = pl.BlockSpec(memory_space=pltpu.VMEM)
    blk = lambda w: pl.BlockSpec((tb, w), lambda i: (i, 0))
    buf = pltpu.VMEM((tb, d), F32)
    car = pltpu.VMEM((SUBLANES, d), F32)
    return pl.pallas_call(
        body, name="forward", grid=(nb,),
        in_specs=[blk(d), blk(d), blk(6 * d), vm, vm, vm, vm, vm, vm],
        out_specs=(blk(d), blk(d), blk(d), pl.BlockSpec((2 * SUBLANES, d), lambda i: (0, 0))) + (blk(d),) * 5,
        out_shape=(jax.ShapeDtypeStruct((t_len, d), F32),
                   jax.ShapeDtypeStruct((t_len, d), F32),
                   jax.ShapeDtypeStruct((t_len, d), MM),
                   jax.ShapeDtypeStruct((2 * SUBLANES, d), F32))
                  + (jax.ShapeDtypeStruct((t_len, d), F32),) * 5,
        scratch_shapes=[buf] * 2 + [pltpu.VMEM((tb, 2 * d), MM), car, car, car],
        compiler_params=_params(dimension_semantics=("arbitrary",)),
    )(x, tgt, p, wout, wa_t, wi_t, sp, ones_c, ones_l)


def _backward(p, h, dh, saved, wout, wa_t, wi_t, sp, ones_c, ones_l, tb):
    t_len, d = h.shape
    nb = t_len // tb
    n_tiles, tw = wa_t.shape[0], wa_t.shape[1]
    hd_c, hd_l = d // N_CONV_HEADS, d // N_LRU_HEADS
    s8 = SUBLANES

    def body(p_ref, h_ref, hhalo_ref, dh_ref, yc, czs, u, ra_ref, ii_ref,
             wout_ref, wa_ref, wi_ref, sp_ref, oc_ref, ol_ref,
             dp_ref, yt_ref, gwa_ref, gwi_ref, acc_ref,
             hh, dy, ybuf, rcf, rlf, qc, ql, dyc_hat, dyl_hat, dpa, dpi, du,
             car_dcz, car_a, car_g, car_du):
        i = pl.program_id(0)
        blk_idx = nb - 1 - i
        row = _row_iota(d)

        @pl.when(i == 0)
        def _():
            for ref in (car_dcz, car_a, car_g, car_du, gwa_ref, gwi_ref, acc_ref):
                ref[...] = jnp.zeros_like(ref)

        def spr(r):
            return sp_ref[r:r + 1, :]

        def proj(rows, seg):
            return p_ref[rows, seg * d:(seg + 1) * d].astype(F32)

        def put(rows, seg, halves):
            dp_ref[rows, seg * d:(seg + 1) * d] = jnp.concatenate(halves, axis=0).astype(MM)

        def acc_add(group, val):
            acc_ref[group * s8:(group + 1) * s8, :] += val

        live = jnp.where(blk_idx > 0, 1.0, 0.0).astype(F32)
        hh[0:s8, :] = hhalo_ref[...] * live
        hh[s8:, :] = h_ref[...]

        dy[...] = _dot_nt(dh_ref[...].astype(MM), wout_ref[...])

        w0, w1, w2 = spr(SP_CONV_W), spr(SP_CONV_W + 1), spr(SP_CONV_W + 2)
        l0, l1, l2, l3 = spr(SP_LRU_W), spr(SP_LRU_W + 1), spr(SP_LRU_W + 2), spr(SP_LRU_W + 3)

        rcf[...] = _head_rstd(yc[...], oc_ref[...], hd_c)
        rlf[...] = _head_rstd(h_ref[...], ol_ref[...], hd_l)

        g_c, g_l = spr(SP_CONV_G), spr(SP_LRU_G)

        def gates(r, carry):
            rows = pl.ds(r, 2 * s8)
            for (seg, off_y, src, rstd, gain, q, dhat, grp) in (
                    (P_GC, 0, yc, rcf, g_c, qc, dyc_hat, A_CONV_G),
                    (P_GL, d, h_ref, rlf, g_l, ql, dyl_hat, A_LRU_G)):
                gt = proj(rows, seg)
                sg = _sigmoid(gt)
                silu = gt * sg
                yhat = src[rows, :] * rstd[rows, :]
                nrm = yhat * gain
                ybuf[rows, off_y:off_y + d] = nrm * silu
                dout = dy[rows, off_y:off_y + d]
                dnrm = dout * silu
                dp_ref[rows, seg * d:(seg + 1) * d] = (dout * nrm * (sg * (1.0 + gt * (1.0 - sg)))).astype(MM)
                dg = dnrm * yhat
                acc_add(grp, dg[0:s8] + dg[s8:])
                dh_ = dnrm * gain
                dhat[rows, :] = dh_
                q[rows, :] = dh_ * yhat
            return carry

        _chunks(tb, 2 * s8, gates, 0)

        qc[...] = _head_sums(qc[...], oc_ref[...]) * (1.0 / hd_c)
        ql[...] = _head_sums(ql[...], ol_ref[...]) * (1.0 / hd_l)
        yt_ref[...] = ybuf[...].T.astype(MM)

        c8 = RG_LRU_C * _log_sigmoid(spr(SP_LAM))

        def conv_mixer(r, dcz_n):
            rows16 = pl.ds(r, 2 * s8)
            bg16, cg16, xc16 = proj(rows16, P_B), proj(rows16, P_C), proj(rows16, P_XC)
            z16 = cg16 * xc16
            d_b, d_c, d_x = [None, None], [None, None], [None, None]
            for j in (1, 0):
                rows, sub = pl.ds(r + j * s8, s8), slice(j * s8, (j + 1) * s8)
                rstd = rcf[rows, :]
                yhat = yc[rows, :] * rstd
                dyc = rstd * (dyc_hat[rows, :] - yhat * qc[rows, :])
                d_b[j] = dyc * czs[rows, :]
                dcz = dyc * bg16[sub]
                up1, up2 = _shift_up(dcz, dcz_n, 1, row), _shift_up(dcz, dcz_n, 2, row)
                dz = w2 * dcz + w1 * up1 + w0 * up2
                d_c[j] = dz * xc16[sub]
                d_x[j] = dz * cg16[sub]
                z = z16[sub]
                acc_add(A_CONV_W, up2 * z)
                acc_add(A_CONV_W + 1, up1 * z)
                acc_add(A_CONV_W + 2, dcz * z)
                dcz_n = dcz
            put(rows16, P_B, d_b)
            put(rows16, P_C, d_c)
            put(rows16, P_XC, d_x)
            return dcz_n

        car_dcz[...] = _chunks(tb, 2 * s8, conv_mixer, car_dcz[...], reverse=True)

        def lru_mixer(r, carry):
            a_n, g_n = carry
            for j in (1, 0):
                rows = pl.ds(r + j * s8, s8)
                rstd = rlf[rows, :]
                hcur = hh[pl.ds(r + (j + 1) * s8, s8), :]
                hhat = hcur * rstd
                dh_out = rstd * (dyl_hat[rows, :] - hhat * ql[rows, :])
                ra = ra_ref[rows, :]
                la = ra * c8
                a = jnp.exp(la)
                g = _scan_bwd(_shift_up(a, a_n, 1, row), dh_out, g_n, row)
                da = g * _shift_down(hcur, hh[pl.ds(r + j * s8, s8), :], 1, row)
                ii = ii_ref[rows, :]
                uu = u[rows, :]
                mult = _lru_input_scale(la, a)
                dmult = g * (ii * uu)
                ds = g * mult
                dla = a * (da - dmult * a / mult)
                acc_add(A_LAM, dla * ra)
                dpa_ = dla * c8 * ra * (1.0 - ra)
                dpi_ = ds * uu * ii * (1.0 - ii)
                acc_add(A_B_A, dpa_)
                acc_add(A_B_I, dpi_)
                dpa[rows, :] = dpa_
                dpi[rows, :] = dpi_
                du[rows, :] = ds * ii
                a_n, g_n = a, _bcast_row(g, 0)
            return a_n, g_n

        a_f, g_f = _chunks(tb, 2 * s8, lru_mixer, (car_a[...], car_g[...]), reverse=True)
        car_a[...] = a_f
        car_g[...] = g_f

        dpab = dpa[...].astype(MM)
        dpib = dpi[...].astype(MM)
        for k in range(n_tiles):
            sl = slice(k * tw, (k + 1) * tw)
            du[:, sl] += _dot_nt(dpab[:, sl], wa_ref[k]) + _dot_nt(dpib[:, sl], wi_ref[k])
            ut = u[:, sl].T.astype(MM)
            gwa_ref[k] += _dot(ut, dpab[:, sl])
            gwi_ref[k] += _dot(ut, dpib[:, sl])

        def lru_conv(r, du_n):
            rows16 = pl.ds(r, 2 * s8)
            xl16 = proj(rows16, P_XL)
            d_xl = [None, None]
            for j in (1, 0):
                rows, sub = pl.ds(r + j * s8, s8), slice(j * s8, (j + 1) * s8)
                dut = du[rows, :]
                up1, up2, up3 = (_shift_up(dut, du_n, s, row) for s in (1, 2, 3))
                d_xl[j] = l3 * dut + l2 * up1 + l1 * up2 + l0 * up3
                xl = xl16[sub]
                acc_add(A_LRU_W, up3 * xl)
                acc_add(A_LRU_W + 1, up2 * xl)
                acc_add(A_LRU_W + 2, up1 * xl)
                acc_add(A_LRU_W + 3, dut * xl)
                acc_add(A_LRU_B, dut)
                du_n = dut
            put(rows16, P_XL, d_xl)
            return du_n

        car_du[...] = _chunks(tb, 2 * s8, lru_conv, car_du[...], reverse=True)

    vm = pl.BlockSpec(memory_space=pltpu.VMEM)
    rev = lambda w: pl.BlockSpec((tb, w), lambda i: (nb - 1 - i, 0))
    halo = lambda rows, w: pl.BlockSpec((rows, w), lambda i: (jnp.maximum((nb - 1 - i) * (tb // rows) - 1, 0), 0))
    const = lambda shape: pl.BlockSpec(shape, lambda i: (0,) * len(shape))
    buf = lambda w: pltpu.VMEM((tb, w), F32)
    car = pltpu.VMEM((SUBLANES, d), F32)
    return pl.pallas_call(
        body, name="backward", grid=(nb,),
        in_specs=[rev(6 * d), rev(d), halo(SUBLANES, d), rev(d)] + [rev(d)] * 5 + [vm, vm, vm, vm, vm, vm],
        out_specs=(rev(6 * d), pl.BlockSpec((2 * d, tb), lambda i: (0, nb - 1 - i)),
                   const((n_tiles, tw, tw)), const((n_tiles, tw, tw)), const((A_GROUPS * SUBLANES, d))),
        out_shape=(jax.ShapeDtypeStruct((t_len, 6 * d), MM),
                   jax.ShapeDtypeStruct((2 * d, t_len), MM),
                   jax.ShapeDtypeStruct((n_tiles, tw, tw), F32),
                   jax.ShapeDtypeStruct((n_tiles, tw, tw), F32),
                   jax.ShapeDtypeStruct((A_GROUPS * SUBLANES, d), F32)),
        scratch_shapes=[pltpu.VMEM((SUBLANES + tb, d), F32), buf(2 * d), buf(2 * d)] + [buf(d)] * 9 + [car, car, car, car],
        compiler_params=_params(dimension_semantics=("arbitrary",)),
    )(p, h, h, dh, *saved, wout, wa_t, wi_t, sp, ones_c, ones_l)


def _input_grad(dp, win_all, x, dh, sp, parts, tb):
    t_len, d = x.shape
    nb = t_len // tb
    cols = win_all.shape[2]
    n_parts = len(parts)

    def body(dp_ref, win_ref, x_ref, dh_ref, sp_ref, *refs):
        gx_ref, acc_ref = refs[n_parts:n_parts + 2]
        exchange = _ChipExchange(refs[:n_parts], refs[n_parts + 2:2 * n_parts + 2], *refs[2 * n_parts + 2:])
        i = pl.program_id(0)

        @pl.when(i == 0)
        def _():
            acc_ref[...] = jnp.zeros_like(acc_ref)
            exchange.start()

        dxn = _dot_nt(dp_ref[:, 0:cols], win_ref[0])
        for j in range(1, N_DEV):
            dxn += _dot_nt(dp_ref[:, j * cols:(j + 1) * cols], win_ref[j])
        xv = x_ref[...]
        r0 = lax.rsqrt(jnp.mean(xv * xv, axis=-1, keepdims=True) + RMS_EPS)
        xhat = xv * r0
        acc_ref[...] += (dxn * xhat).reshape(tb // SUBLANES, SUBLANES, d).sum(axis=0)
        dxh = dxn * sp_ref[SP_LN_G:SP_LN_G + 1, :]
        gx_ref[...] = dh_ref[...] + r0 * (dxh - xhat * jnp.mean(dxh * xhat, axis=-1, keepdims=True))

        @pl.when(i == nb - 1)
        def _():
            exchange.finish()

    vm = pl.BlockSpec(memory_space=pltpu.VMEM)
    hbm = pl.BlockSpec(memory_space=pl.ANY)
    blk = lambda w: pl.BlockSpec((tb, w), lambda i: (i, 0))
    outs = pl.pallas_call(
        body, name="input_grad", grid=(nb,),
        in_specs=[blk(6 * d), vm, blk(d), blk(d), vm] + [hbm] * n_parts,
        out_specs=(blk(d), pl.BlockSpec((SUBLANES, d), lambda i: (0, 0))) + (hbm,) * n_parts,
        out_shape=(jax.ShapeDtypeStruct((t_len, d), F32), jax.ShapeDtypeStruct((SUBLANES, d), F32))
                  + tuple(jax.ShapeDtypeStruct(p.shape, p.dtype) for p in parts),
        scratch_shapes=[pltpu.SemaphoreType.DMA((3 * n_parts,)), pltpu.SemaphoreType.DMA((3 * n_parts,))],
        compiler_params=_params(dimension_semantics=("arbitrary",)),
    )(dp, win_all, x, dh, sp, *parts)
    return outs[0], outs[1], outs[2:]


_CHIP_RELATIONS = [(0, 0), (1, 0), (0, 1), (1, 1)]


def _related_block(k, core):
    x, y, _ = _mesh_pos()
    fx, fy = _CHIP_RELATIONS[k]
    return 4 * (x ^ fx) + 2 * (y ^ fy) + core


class _ChipExchange:
    def __init__(self, part_refs, land_refs, send_sems, recv_sems):
        self.part_refs, self.land_refs, self.send_sems, self.recv_sems = part_refs, land_refs, send_sems, recv_sems

    def copies(self):
        x, y, c = _mesh_pos()
        for a in range(len(self.part_refs)):
            for k in (1, 2, 3):
                fx, fy = _CHIP_RELATIONS[k]
                yield pltpu.make_async_remote_copy(
                    src_ref=self.part_refs[a].at[k - 1], dst_ref=self.land_refs[a].at[k - 1],
                    send_sem=self.send_sems.at[3 * a + k - 1], recv_sem=self.recv_sems.at[3 * a + k - 1],
                    device_id=(x ^ fx, y ^ fy, c), device_id_type=MESH)

    def start(self):
        for cp in self.copies():
            cp.start()

    def finish(self):
        for cp in self.copies():
            cp.wait_recv()
        for cp in self.copies():
            cp.wait_send()


def _weight_grad_stage1(name, blk_shape, n_split, operands, in_specs, product, riders=()):
    n_rows, n_cols = blk_shape
    rs = n_rows // n_split
    rc = 32
    n_in, n_ride = len(operands), len(riders)
    _, _, c = _mesh_pos()
    order = jnp.stack([_related_block(k, 1 - c) for k in range(4)]
                      + [_related_block(k, c) for k in (1, 2, 3, 0)]).astype(jnp.int32)

    def body(order_ref, *refs):
        ins = refs[:n_in]
        ride_in = refs[n_in:n_in + n_ride]
        part_ref, own_ref, from_sib = refs[n_in + n_ride:n_in + n_ride + 3]
        ride_out = refs[n_in + n_ride + 3:n_in + 2 * n_ride + 3]
        gbuf, sendbuf, tmp, send_sems, recv_sems, local_sems, ride_send, ride_recv = refs[n_in + 2 * n_ride + 3:]
        exchange = _ChipExchange(ride_in, ride_out, ride_send, ride_recv)
        s = pl.program_id(0)
        x, y, c = _mesh_pos()

        def to_sibling(k):
            return pltpu.make_async_remote_copy(
                src_ref=sendbuf.at[k], dst_ref=from_sib.at[k], send_sem=send_sems.at[k], recv_sem=recv_sems.at[k],
                device_id=(x, y, 1 - c), device_id_type=MESH)

        if n_ride:
            @pl.when(s == 0)
            def _():
                exchange.start()

        for h in range(n_split):
            gbuf[h * rs:(h + 1) * rs, :] = product(ins, h)

        @pl.when(s < 4)
        def _():
            def narrow(r, carry):
                sendbuf[s, pl.ds(r, rc), :] = gbuf[pl.ds(r, rc), :].astype(MM)
                return carry

            _chunks(n_rows, rc, narrow, 0)
            to_sibling(s).start()

        @pl.when(s >= 4)
        def _():
            k = jnp.where(s == 7, 0, s - 3)
            to_sibling(k).wait_recv()
            load = pltpu.make_async_copy(from_sib.at[k], tmp, local_sems.at[0])
            load.start()
            load.wait()

            @pl.when(s < 7)
            def _():
                def add(r, carry):
                    rows = pl.ds(r, rc)
                    part_ref[0, rows, :] = (gbuf[rows, :] + tmp[rows, :].astype(F32)).astype(MM)
                    return carry

                _chunks(n_rows, rc, add, 0)

            @pl.when(s == 7)
            def _():
                def add(r, carry):
                    rows = pl.ds(r, rc)
                    own_ref[rows, :] = gbuf[rows, :] + tmp[rows, :].astype(F32)
                    return carry

                _chunks(n_rows, rc, add, 0)
                for kk in range(4):
                    to_sibling(kk).wait_send()
                if n_ride:
                    exchange.finish()

    hbm = pl.BlockSpec(memory_space=pl.ANY)
    grid_spec = pltpu.PrefetchScalarGridSpec(
        num_scalar_prefetch=1, grid=(N_DEV,), in_specs=list(in_specs) + [hbm] * n_ride,
        out_specs=(pl.BlockSpec((1, n_rows, n_cols), lambda s, o: (jnp.clip(s - 4, 0, 2), 0, 0)),
                   pl.BlockSpec((n_rows, n_cols), lambda s, o: (0, 0)), hbm) + (hbm,) * n_ride,
        scratch_shapes=[pltpu.VMEM((n_rows, n_cols), F32), pltpu.VMEM((4, n_rows, n_cols), MM),
                        pltpu.VMEM((n_rows, n_cols), MM),
                        pltpu.SemaphoreType.DMA((4,)), pltpu.SemaphoreType.DMA((4,)), pltpu.SemaphoreType.DMA((1,)),
                        pltpu.SemaphoreType.DMA((max(3 * n_ride, 1),)), pltpu.SemaphoreType.DMA((max(3 * n_ride, 1),))])
    outs = pl.pallas_call(
        body, name=name, grid_spec=grid_spec,
        out_shape=(jax.ShapeDtypeStruct((3, n_rows, n_cols), MM), jax.ShapeDtypeStruct((n_rows, n_cols), F32),
                   jax.ShapeDtypeStruct((4, n_rows, n_cols), MM))
                  + tuple(jax.ShapeDtypeStruct(p.shape, p.dtype) for p in riders),
        compiler_params=_params(dimension_semantics=("arbitrary",)),
    )(order, *operands, *riders)
    return outs[0], outs[1], outs[3:]


def _weight_grad_in(xnt, dp, riders):
    d, t_len = xnt.shape
    cols = dp.shape[1] // N_DEV
    half = d // 2
    return _weight_grad_stage1(
        "weight_grad_in", (d, cols), 2, (xnt, dp),
        [pl.BlockSpec(memory_space=pltpu.VMEM), pl.BlockSpec((t_len, cols), lambda s, o: (0, o[s]))],
        lambda refs, h: _dot(refs[0][h * half:(h + 1) * half, :], refs[1][...]), riders)


def _weight_grad_out(yt, dhb):
    d2, t_len = yt.shape
    d = dhb.shape[1]
    rows = d2 // N_DEV
    return _weight_grad_stage1(
        "weight_grad_out", (rows, d), 1, (yt, dhb),
        [pl.BlockSpec((rows, t_len), lambda s, o: (o[s], 0)), pl.BlockSpec(memory_space=pltpu.VMEM)],
        lambda refs, h: _dot(refs[0][...], refs[1][...]))


def _update_shard(own, from_chips, w, m, v, name):
    n_rows, n_cols = w.shape
    rb = min(256, n_rows)

    def body(own_ref, fc_ref, w_ref, m_ref, v_ref, grad_ref, delta_ref, mo_ref, vo_ref):
        g = own_ref[...]
        for k in range(3):
            g = g + fc_ref[k].astype(F32)
        delta, m_new, v_new = _adamw(w_ref[...], g, m_ref[...], v_ref[...])
        grad_ref[...] = g
        delta_ref[...] = delta
        mo_ref[...] = m_new
        vo_ref[...] = v_new

    blk = pl.BlockSpec((rb, n_cols), lambda i: (i, 0))
    out = jax.ShapeDtypeStruct((n_rows, n_cols), F32)
    return pl.pallas_call(
        body, name=name, grid=(n_rows // rb,),
        in_specs=[blk, pl.BlockSpec((3, rb, n_cols), lambda i: (0, i, 0)), blk, blk, blk],
        out_specs=(blk, blk, blk, blk), out_shape=(out, out, out, out),
        compiler_params=_params(dimension_semantics=("arbitrary",)),
    )(own, from_chips, w, m, v)


def _small_update(facc, xacc, bacc, gwa, gwi, lam, wsl, msl, vsl, cw, cm, cv):
    n_rows, d = wsl.shape
    s8 = SUBLANES
    cc = cw.shape[1]

    def body(facc_ref, xacc_ref, bacc_ref, gwa_ref, gwi_ref, lam_ref, w_ref, m_ref, v_ref, cw_ref, cm_ref, cv_ref,
             g_o, d_o, m_o, v_o, cg_o, cd_o, cm_o, cv_o, gat, send_sems, recv_sems):
        x, y, c = _mesh_pos()
        me = 4 * x + 2 * y + c

        def rowsum(ref, group):
            return jnp.sum(ref[group * s8:(group + 1) * s8, :], axis=0, keepdims=True)

        mine = gat.at[me]
        loss = jnp.sum(rowsum(facc_ref, 0), axis=1, keepdims=True) * (0.5 / d)
        mine[SL_LOSS:SL_LOSS + 1, :] = jnp.broadcast_to(loss, (1, d))
        mine[SL_LN_G:SL_LN_G + 1, :] = rowsum(xacc_ref, 0)
        mine[SL_LRU_B:SL_LRU_B + 1, :] = rowsum(bacc_ref, A_LRU_B)
        mine[SL_B_A:SL_B_A + 1, :] = rowsum(bacc_ref, A_B_A)
        mine[SL_B_I:SL_B_I + 1, :] = rowsum(bacc_ref, A_B_I)
        mine[SL_LAM:SL_LAM + 1, :] = rowsum(bacc_ref, A_LAM) * (RG_LRU_C * jax.nn.sigmoid(-lam_ref[...]))
        mine[SL_CONV_G:SL_CONV_G + 1, :] = rowsum(bacc_ref, A_CONV_G)
        mine[SL_LRU_G:SL_LRU_G + 1, :] = rowsum(bacc_ref, A_LRU_G)
        mine[SL_FINAL_G:SL_FINAL_G + 1, :] = rowsum(facc_ref, 1)
        for k in range(3):
            mine[SL_CONV_W + k:SL_CONV_W + k + 1, :] = rowsum(bacc_ref, A_CONV_W + k)
        for k in range(4):
            mine[SL_LRU_W + k:SL_LRU_W + k + 1, :] = rowsum(bacc_ref, A_LRU_W + k)
        na = gwa_ref.shape[0]
        mine[SL_W_A:SL_W_A + na, :] = gwa_ref[...]
        mine[SL_W_A + na:SL_W_A + 2 * na, :] = gwi_ref[...]

        gather = _Gather(lambda a, px, py, pc: gat.at[4 * px + 2 * py + pc], send_sems, recv_sems)
        gather.start_own(0)
        gather.finish(0)

        def update(r, carry):
            rows = pl.ds(r, s8)
            g = gat[0, rows, :]
            for b in range(1, N_DEV):
                g = g + gat[b, rows, :]
            delta, m_new, v_new = _adamw(w_ref[rows, :], g, m_ref[rows, :], v_ref[rows, :])
            g_o[rows, :] = g
            d_o[rows, :] = delta
            m_o[rows, :] = m_new
            v_o[rows, :] = v_new
            return carry

        _chunks(n_rows, s8, update, 0)
        gc = g_o[s8:2 * s8, pl.ds(pl.multiple_of(me * cc, cc), cc)]
        delta, m_new, v_new = _adamw(cw_ref[...], gc, cm_ref[...], cv_ref[...])
        cg_o[...] = gc
        cd_o[...] = delta
        cm_o[...] = m_new
        cv_o[...] = v_new

    vm = pl.BlockSpec(memory_space=pltpu.VMEM)
    big = jax.ShapeDtypeStruct((n_rows, d), F32)
    small = jax.ShapeDtypeStruct(cw.shape, F32)
    return pl.pallas_call(
        body, name="small_update",
        in_specs=[vm] * 12, out_specs=(vm,) * 8, out_shape=(big, big, big, big, small, small, small, small),
        scratch_shapes=[pltpu.VMEM((N_DEV, n_rows, d), F32), pltpu.SemaphoreType.DMA((7,)), pltpu.SemaphoreType.DMA((7,))],
        compiler_params=_params(),
    )(facc, xacc, bacc, gwa, gwi, lam, wsl, msl, vsl, cw, cm, cv)


def _head_ones(head_dim, tw):
    lane = jnp.arange(tw) // head_dim
    return (lane[:, None] == lane[None, :]).astype(MM)


def _gate_tiles(w, tw):
    n_heads, hd, _ = w.shape
    per = tw // hd
    w4 = w.reshape(n_heads // per, per, hd, hd)
    eye = jnp.eye(per, dtype=w.dtype)
    return (w4[:, :, :, None, :] * eye[None, :, None, :, None]).reshape(n_heads // per, tw, tw)


def _gate_blocks(tiles, n_heads, hd):
    n_tiles, tw, _ = tiles.shape
    per = tw // hd
    t5 = tiles.reshape(n_tiles, per, hd, per, hd)
    diag = jnp.stack([t5[:, a, :, a, :] for a in range(per)], axis=1)
    return diag.reshape(hd, n_heads * hd)


def kernel(x, ln_g, w_in, conv_w, lru_conv_w, lru_conv_b, w_a, b_a, w_i, b_i, lam, conv_out_g, lru_out_g, w_out, final_g, loss_target, m_ln_g, m_w_in, m_conv_w, m_lru_conv_w, m_lru_conv_b, m_w_a, m_b_a, m_w_i, m_b_i, m_lam, m_conv_out_g, m_lru_out_g, m_w_out, m_final_g, v_ln_g, v_w_in, v_conv_w, v_lru_conv_w, v_lru_conv_b, v_w_a, v_b_a, v_w_i, v_b_i, v_lam, v_conv_out_g, v_lru_out_g, v_w_out, v_final_g):
    _, t_len, d = x.shape
    hd_l = d // N_LRU_HEADS
    tw = min(MXU_TILE, d)
    cc = conv_w.shape[1]
    x2, tgt2 = x[0], loss_target[0]

    def conv_rows(cw3, lw4):
        return jnp.concatenate([jnp.zeros((1, cc), F32), cw3, lw4], axis=0)

    p, xnt, win_all, wout_all, conv_all = _gather_project(
        x2, w_in, w_out, conv_rows(conv_w, lru_conv_w), ln_g.reshape(1, d), min(256, t_len))
    wout_full = wout_all.reshape(N_DEV * w_out.shape[0], d)
    conv_full = conv_all.transpose(1, 0, 2).reshape(SUBLANES, d)
    small = [ln_g, lru_conv_b, b_a, b_i, lam, conv_out_g, lru_out_g, final_g]
    sp = jnp.concatenate([jnp.stack(small), conv_full[1:], jnp.zeros((1, d), F32)], axis=0)
    wa_t, wi_t = _gate_tiles(w_a, tw).astype(MM), _gate_tiles(w_i, tw).astype(MM)
    ones_c, ones_l = _head_ones(d // N_CONV_HEADS, tw), _head_ones(hd_l, tw)

    h, dh, dhb, facc, *saved = _forward(x2, tgt2, p, wout_full, wa_t, wi_t, sp, ones_c, ones_l, min(256, t_len))
    dp, yt, gwa_t, gwi_t, bacc = _backward(p, h, dh, saved, wout_full, wa_t, wi_t, sp, ones_c, ones_l, min(256, t_len))
    part_out, own_out, _ = _weight_grad_out(yt, dhb)
    part_in, own_in, (chips_out,) = _weight_grad_in(xnt, dp, (part_out,))
    grad_x, xacc, (chips_in,) = _input_grad(dp, win_all, x2, dh, sp, (part_in,), min(512, t_len))
    gw_in, dw_in, mw_in, vw_in = _update_shard(own_in, chips_in, w_in, m_w_in, v_w_in, "update_w_in")
    gw_out, dw_out, mw_out, vw_out = _update_shard(own_out, chips_out, w_out, m_w_out, v_w_out, "update_w_out")

    def slab(parts, wa_, wi_):
        return jnp.concatenate([jnp.zeros((1, d), F32), jnp.stack(parts), jnp.zeros((SL_W_A - SL_CONV_W, d), F32),
                                wa_.reshape(-1, d), wi_.reshape(-1, d)], axis=0)

    wsl = slab(small, w_a, w_i)
    msl = slab([m_ln_g, m_lru_conv_b, m_b_a, m_b_i, m_lam, m_conv_out_g, m_lru_out_g, m_final_g], m_w_a, m_w_i)
    vsl = slab([v_ln_g, v_lru_conv_b, v_b_a, v_b_i, v_lam, v_conv_out_g, v_lru_out_g, v_final_g], v_w_a, v_w_i)
    outs = _small_update(
        facc, xacc, bacc, _gate_blocks(gwa_t, N_LRU_HEADS, hd_l), _gate_blocks(gwi_t, N_LRU_HEADS, hd_l),
        lam.reshape(1, d), wsl, msl, vsl,
        conv_rows(conv_w, lru_conv_w), conv_rows(m_conv_w, m_lru_conv_w), conv_rows(v_conv_w, v_lru_conv_w))
    sl_g, sl_d, sl_m, sl_v, c_g, c_d, c_m, c_v = outs
    loss = sl_g[SL_LOSS, 0]
    na = w_a.size // d

    def unpack(sl, cv, big_in, big_out):
        one = lambda r: sl[r]
        return [one(SL_LN_G), big_in, cv[1:4], cv[4:8], one(SL_LRU_B),
                sl[SL_W_A:SL_W_A + na].reshape(w_a.shape), one(SL_B_A),
                sl[SL_W_A + na:SL_W_A + 2 * na].reshape(w_i.shape), one(SL_B_I), one(SL_LAM),
                one(SL_CONV_G), one(SL_LRU_G), big_out, one(SL_FINAL_G)]

    return (loss, grad_x[None], *unpack(sl_g, c_g, gw_in, gw_out), *unpack(sl_d, c_d, dw_in, dw_out),
            *unpack(sl_m, c_m, mw_in, mw_out), *unpack(sl_v, c_v, vw_in, vw_out))
```

```python
import functools

import jax
import jax.numpy as jnp
from jax import lax
from jax.experimental import pallas as pl
from jax.experimental.pallas import tpu as pltpu

F32 = jnp.float32
MM = jnp.bfloat16
MESH = pl.DeviceIdType.MESH

N_DEV = 8
N_CONV_HEADS = 8
N_LRU_HEADS = 16
RG_LRU_C = 8.0
RMS_EPS = 1e-6
ADAM_LR, ADAM_B1, ADAM_B2, ADAM_EPS, ADAM_WD, ADAM_STEP = 0.001, 0.9, 0.999, 1e-08, 0.01, 10
ADAM_BC1 = 1.0 - ADAM_B1 ** ADAM_STEP
ADAM_BC2 = 1.0 - ADAM_B2 ** ADAM_STEP

SUBLANES = 8
LANES = 128
MXU_TILE = 256
VMEM_LIMIT = 56 * 1024 * 1024

SP_LN_G, SP_LRU_B, SP_B_A, SP_B_I, SP_LAM, SP_CONV_G, SP_LRU_G, SP_FINAL_G, SP_CONV_W, SP_LRU_W = 0, 1, 2, 3, 4, 5, 6, 7, 8, 11
SP_ROWS = 16
P_B, P_C, P_XC, P_GC, P_XL, P_GL = 0, 1, 2, 3, 4, 5
A_CONV_G, A_LRU_G, A_LAM, A_B_A, A_B_I, A_CONV_W, A_LRU_W, A_LRU_B = 0, 1, 2, 3, 4, 5, 8, 12
A_GROUPS = 13
SL_LOSS, SL_LN_G, SL_LRU_B, SL_B_A, SL_B_I, SL_LAM, SL_CONV_G, SL_LRU_G, SL_FINAL_G, SL_CONV_W, SL_LRU_W, SL_W_A = 0, 1, 2, 3, 4, 5, 6, 7, 8, 9, 12, 16


def _params(vmem=True, **kw):
    if vmem:
        kw["vmem_limit_bytes"] = VMEM_LIMIT
    return pltpu.CompilerParams(**kw)


def _dot(a, b):
    return jnp.dot(a, b, preferred_element_type=F32)


def _dot_nt(a, b):
    return lax.dot_general(a, b, (((1,), (1,)), ((), ())), preferred_element_type=F32)


def _head_sums(v, ones_tile):
    tw = ones_tile.shape[0]
    vb = v.astype(MM)
    return jnp.concatenate([_dot(vb[:, k:k + tw], ones_tile) for k in range(0, v.shape[1], tw)], axis=1)


def _head_rstd(v, ones_tile, head_dim):
    return lax.rsqrt(_head_sums(v * v, ones_tile) * (1.0 / head_dim) + RMS_EPS)


def _sigmoid(x):
    return 0.5 * jnp.tanh(0.5 * x) + 0.5


def _lru_input_scale(log_a, a):
    return jnp.sqrt(-jnp.tanh(log_a) * (1.0 + a * a))


def _log_sigmoid(x):
    z = jnp.exp(-jnp.abs(x))
    u = 1.0 + z
    log1p_z = jnp.where(u == 1.0, z, jnp.log(u) * (z / (u - 1.0)))
    return jnp.minimum(x, 0.0) - log1p_z


def _row_iota(d):
    return lax.broadcasted_iota(jnp.int32, (SUBLANES, d), 0)


def _shift_down(cur, prev, s, row):
    return jnp.where(row >= s, pltpu.roll(cur, s, axis=0), pltpu.roll(prev, s, axis=0))


def _shift_up(cur, nxt, s, row):
    k = SUBLANES - s
    return jnp.where(row < k, pltpu.roll(cur, k, axis=0), pltpu.roll(nxt, k, axis=0))


def _scan_fwd(a, b, h_prev, row):
    for s in (1, 2, 4):
        a_s = jnp.where(row >= s, pltpu.roll(a, s, axis=0), 1.0)
        b_s = jnp.where(row >= s, pltpu.roll(b, s, axis=0), 0.0)
        b = a * b_s + b
        a = a * a_s
    return a * h_prev + b


def _scan_bwd(a_next, b, g_next, row):
    a = a_next
    for s in (1, 2, 4):
        k = SUBLANES - s
        a_s = jnp.where(row < k, pltpu.roll(a, k, axis=0), 1.0)
        b_s = jnp.where(row < k, pltpu.roll(b, k, axis=0), 0.0)
        b = a * b_s + b
        a = a * a_s
    return a * g_next + b


def _bcast_row(v, r):
    return jnp.broadcast_to(v[r:r + 1, :], v.shape)


def _chunks(n_rows, rc, body, init, reverse=False):
    n = n_rows // rc

    def step(i, carry):
        j = (n - 1 - i) if reverse else i
        return body(pl.multiple_of(j * rc, rc), carry)

    return lax.fori_loop(0, n, step, init)


def _adamw(w, g, m, v):
    m = ADAM_B1 * m + (1.0 - ADAM_B1) * g
    v = ADAM_B2 * v + (1.0 - ADAM_B2) * (g * g)
    m_hat = m / ADAM_BC1
    v_hat = v / ADAM_BC2
    delta = -ADAM_LR * (m_hat / (jnp.sqrt(v_hat) + ADAM_EPS) + ADAM_WD * w)
    return delta, m, v


def _mesh_pos():
    return lax.axis_index("x"), lax.axis_index("y"), lax.axis_index("c")


class _Gather:
    def __init__(self, blocks_of, send_sems, recv_sems, own_src=None):
        x, y, c = _mesh_pos()
        self.c = c
        self.me, self.sibling = (x, y, c), (x, y, 1 - c)
        self.chips = [(1 - x, y), (x, 1 - y), (1 - x, 1 - y)]
        self.blocks_of, self.send_sems, self.recv_sems = blocks_of, send_sems, recv_sems
        self.own_src = own_src

    def copy(self, a, k, block, to):
        src = self.blocks_of(a, *block)
        if block is self.me and self.own_src is not None:
            src = self.own_src[a]
        return pltpu.make_async_remote_copy(
            src_ref=src, dst_ref=self.blocks_of(a, *block),
            send_sem=self.send_sems.at[a * 7 + k], recv_sem=self.recv_sems.at[a * 7 + k],
            device_id=to, device_id_type=MESH)

    def start_own(self, a):
        self.copy(a, 0, self.me, self.sibling).start()
        for j, chip in enumerate(self.chips):
            self.copy(a, 1 + j, self.me, (*chip, self.c)).start()

    def wait_sibling(self, a):
        self.copy(a, 0, self.sibling, self.me).wait_recv()

    def wait_chip_and_pass_on(self, a, j):
        block = (*self.chips[j], self.c)
        self.copy(a, 1 + j, block, self.me).wait_recv()
        self.copy(a, 4 + j, block, self.sibling).start()

    def wait_passed_on(self, a, j):
        self.copy(a, 4 + j, (*self.chips[j], 1 - self.c), self.me).wait_recv()

    def wait_sends(self, a):
        self.copy(a, 0, self.me, self.sibling).wait_send()
        for j, chip in enumerate(self.chips):
            self.copy(a, 1 + j, self.me, (*chip, self.c)).wait_send()
            self.copy(a, 4 + j, (*chip, self.c), self.sibling).wait_send()

    def finish(self, a):
        for j in range(3):
            self.wait_chip_and_pass_on(a, j)
        self.wait_sibling(a)
        for j in range(3):
            self.wait_passed_on(a, j)
        self.wait_sends(a)


class _BalancedGather:
    def __init__(self, slot, send_sems, recv_sems, own_src):
        x, y, c = _mesh_pos()
        self.c = c
        self.me, self.sibling = (x, y, c), (x, y, 1 - c)
        self.chips = [(1 - x, y), (x, 1 - y), (1 - x, 1 - y)]
        self.slot, self.send_sems, self.recv_sems, self.own_src = slot, send_sems, recv_sems, own_src

    def half(self, a, block, which):
        ref = self.slot(a, *block)
        n = ref.shape[0] // 2
        return ref.at[pl.ds(which * n, n)]

    def copy(self, a, k, src, dst, to):
        return pltpu.make_async_remote_copy(
            src_ref=src, dst_ref=dst, send_sem=self.send_sems.at[a * 8 + k], recv_sem=self.recv_sems.at[a * 8 + k],
            device_id=to, device_id_type=MESH)

    def whole(self, a, k, block, to):
        src = self.own_src[a] if block is self.me else self.slot(a, *block)
        return self.copy(a, k, src, self.slot(a, *block), to)

    def halved(self, a, k, block, which, to):
        return self.copy(a, k, self.half(a, block, which), self.half(a, block, which), to)

    def on(self, chip):
        return (*self.chips[chip], self.c)

    def start_own(self, a):
        self.whole(a, 0, self.me, self.sibling).start()
        self.whole(a, 1, self.me, self.on(0)).start()
        self.whole(a, 2, self.me, self.on(1)).start()

    def wait_sibling(self, a):
        self.whole(a, 0, self.sibling, self.me).wait_recv()

    def on_neighbour(self, a, j):
        self.whole(a, 1 + j, self.on(j), self.me).wait_recv()
        self.halved(a, 3 + j, self.on(j), j, self.on(1 - j)).start()
        self.whole(a, 5 + j, self.on(j), self.sibling).start()

    def on_diagonal(self, a):
        self.halved(a, 3, self.on(2), 0, self.me).wait_recv()
        self.halved(a, 4, self.on(2), 1, self.me).wait_recv()
        self.whole(a, 7, self.on(2), self.sibling).start()

    def wait_passed_on(self, a, j):
        self.whole(a, 5 + j, (*self.chips[j], 1 - self.c), self.me).wait_recv()

    def wait_sends(self, a):
        self.whole(a, 0, self.me, self.sibling).wait_send()
        for j in range(2):
            self.whole(a, 1 + j, self.me, self.on(j)).wait_send()
            self.halved(a, 3 + j, self.on(j), j, self.on(1 - j)).wait_send()
        for j in range(3):
            self.whole(a, 5 + j, self.on(j), self.sibling).wait_send()


def _block_order():
    x, y, c = _mesh_pos()
    chips = [(x, y), (1 - x, y), (x, 1 - y), (1 - x, 1 - y)]
    return jnp.stack([4 * px + 2 * py + pc for px, py in chips for pc in (c, 1 - c)]).astype(jnp.int32)


def _gather_project(x, w_in, w_out, conv_pack, ln_g, tb):
    t_len, d = x.shape
    nb = t_len // tb
    cols = w_in.shape[1]
    mc = min(512, t_len)
    srcs = (w_in, w_out, conv_pack)
    dts = (MM, MM, F32)

    def body(order_ref, x_ref, win_ref, wout_ref, cp_ref, lng_ref, p_ref, xnt_ref, win_all, wout_all, cp_all,
             xnb, wall, st_out, st_cp, send_sems, recv_sems, cp_send, cp_recv, local_sems):
        i = pl.program_id(0)
        x_, y_, c_ = _mesh_pos()
        me = 4 * x_ + 2 * y_ + c_
        outs = (win_all, wout_all, cp_all)
        lands = (wall, wout_all, cp_all)
        stages = (wall.at[me], st_out, st_cp)
        gather = _BalancedGather(lambda a, px, py, pc: lands[a].at[4 * px + 2 * py + pc], send_sems, recv_sems, stages)
        small = _Gather(lambda a, px, py, pc: cp_all.at[4 * px + 2 * py + pc], cp_send, cp_recv, own_src=[st_cp])
        keep_own = [pltpu.make_async_copy(stages[a], outs[a].at[me], local_sems.at[a]) for a in range(3)]

        def keep(k):
            blk = order_ref[k]
            return pltpu.make_async_copy(wall.at[blk], win_all.at[blk], local_sems.at[2 + k])

        @pl.when(i == 0)
        def _():
            for a, (src, dst) in enumerate(zip((win_ref, wout_ref, cp_ref), stages)):
                rows = src.shape[0]
                rc = min(rows, 32)

                def cast(r, carry, src=src, dst=dst, rc=rc):
                    dst[pl.ds(r, rc), :] = src[pl.ds(r, rc), :].astype(dst.dtype)
                    return carry

                _chunks(rows, rc, cast, 0)
                if a < 2:
                    gather.start_own(a)
                else:
                    small.start_own(0)
                keep_own[a].start()

        @pl.when(i < nb)
        def _():
            xv = x_ref[...]
            r0 = lax.rsqrt(jnp.mean(xv * xv, axis=-1, keepdims=True) + RMS_EPS)
            xn = xv * r0 * lng_ref[...]
            xnb[pl.ds(pl.multiple_of(i * tb, tb), tb), :] = xn.astype(MM)
            xnt_ref[...] = xn.T.astype(MM)

        for k in range(N_DEV):
            @pl.when(i == nb + k)
            def _(k=k):
                if k == 1:
                    gather.wait_sibling(0)
                elif k == 2:
                    gather.on_neighbour(0, 0)
                    gather.on_neighbour(0, 1)
                elif k in (3, 5, 7):
                    gather.wait_passed_on(0, (k - 3) // 2)
                    if k == 3:
                        gather.on_neighbour(1, 0)
                        gather.on_neighbour(1, 1)
                    if k == 7:
                        gather.on_diagonal(1)
                elif k == 6:
                    gather.on_diagonal(0)
                blk = order_ref[k]
                if k:
                    keep(k).start()

                def project(r, carry):
                    rows = pl.ds(r, mc)
                    p_ref[rows, :] = _dot(xnb[rows, :], wall[blk]).astype(MM)
                    return carry

                _chunks(t_len, mc, project, 0)
                if k == N_DEV - 1:
                    gather.wait_sends(0)
                    gather.wait_sibling(1)
                    for j in range(3):
                        gather.wait_passed_on(1, j)
                    gather.wait_sends(1)
                    small.finish(0)
                    for cp in keep_own + [keep(kk) for kk in range(1, N_DEV)]:
                        cp.wait()

    vm = pl.BlockSpec(memory_space=pltpu.VMEM)
    hbm = pl.BlockSpec(memory_space=pl.ANY)
    grid_spec = pltpu.PrefetchScalarGridSpec(
        num_scalar_prefetch=1, grid=(nb + N_DEV,),
        in_specs=[pl.BlockSpec((tb, d), lambda i, o: (jnp.minimum(i, nb - 1), 0)), vm, vm, vm, vm],
        out_specs=(pl.BlockSpec((t_len, cols), lambda i, o: (0, o[jnp.maximum(i - nb, 0)])),
                   pl.BlockSpec((d, tb), lambda i, o: (0, jnp.minimum(i, nb - 1))), hbm, hbm, hbm),
        scratch_shapes=[pltpu.VMEM((t_len, d), MM), pltpu.VMEM((N_DEV,) + w_in.shape, MM),
                        pltpu.VMEM(w_out.shape, MM), pltpu.VMEM(conv_pack.shape, F32),
                        pltpu.SemaphoreType.DMA((16,)), pltpu.SemaphoreType.DMA((16,)),
                        pltpu.SemaphoreType.DMA((7,)), pltpu.SemaphoreType.DMA((7,)), pltpu.SemaphoreType.DMA((10,))])
    return pl.pallas_call(
        body, name="gather_project", grid_spec=grid_spec,
        out_shape=(jax.ShapeDtypeStruct((t_len, N_DEV * cols), MM),
                   jax.ShapeDtypeStruct((d, t_len), MM))
                  + tuple(jax.ShapeDtypeStruct((N_DEV,) + s.shape, dt) for s, dt in zip(srcs, dts)),
        compiler_params=_params(dimension_semantics=("arbitrary",)),
    )(_block_order(), x, w_in, w_out, conv_pack, ln_g)


def _forward(x, tgt, p, wout, wa_t, wi_t, sp, ones_c, ones_l, tb):
    t_len, d = x.shape
    nb = t_len // tb
    n_tiles, tw = wa_t.shape[0], wa_t.shape[1]
    hd_c, hd_l = d // N_CONV_HEADS, d // N_LRU_HEADS
    s8 = SUBLANES

    def body(x_ref, tgt_ref, p_ref, wout_ref, wa_ref, wi_ref, sp_ref, oc_ref, ol_ref,
             h_ref, dh_ref, dhb_ref, acc_ref, yc, czs, u, pa, pi,
             rcf, rlf, ybuf, tail_z, tail_xl, hcar):
        i = pl.program_id(0)
        row = _row_iota(d)

        @pl.when(i == 0)
        def _():
            tail_z[...] = jnp.zeros_like(tail_z)
            tail_xl[...] = jnp.zeros_like(tail_xl)
            hcar[...] = jnp.zeros_like(hcar)
            acc_ref[...] = jnp.zeros_like(acc_ref)

        def spr(r):
            return sp_ref[r:r + 1, :]

        def proj(rows, seg):
            return p_ref[rows, seg * d:(seg + 1) * d].astype(F32)

        w0, w1, w2 = spr(SP_CONV_W), spr(SP_CONV_W + 1), spr(SP_CONV_W + 2)
        l0, l1, l2, l3 = spr(SP_LRU_W), spr(SP_LRU_W + 1), spr(SP_LRU_W + 2), spr(SP_LRU_W + 3)
        lb = spr(SP_LRU_B)

        def convs(r, carry):
            zp, xp = carry
            rows16 = pl.ds(r, 2 * s8)
            bg16, xl16 = proj(rows16, P_B), proj(rows16, P_XL)
            z16 = proj(rows16, P_C) * proj(rows16, P_XC)
            for j in range(2):
                rows, sub = pl.ds(r + j * s8, s8), slice(j * s8, (j + 1) * s8)
                z, xl = z16[sub], xl16[sub]
                cz = w0 * _shift_down(z, zp, 2, row) + w1 * _shift_down(z, zp, 1, row) + w2 * z
                czs[rows, :] = cz
                yc[rows, :] = bg16[sub] * cz
                u[rows, :] = (l0 * _shift_down(xl, xp, 3, row) + l1 * _shift_down(xl, xp, 2, row)
                              + l2 * _shift_down(xl, xp, 1, row) + l3 * xl + lb)
                zp, xp = z, xl
            return zp, xp

        z_last, xl_last = _chunks(tb, 2 * s8, convs, (tail_z[...], tail_xl[...]))
        tail_z[...] = z_last
        tail_xl[...] = xl_last

        ub = u[...].astype(MM)
        for k in range(n_tiles):
            sl = slice(k * tw, (k + 1) * tw)
            pa[:, sl] = _dot(ub[:, sl], wa_ref[k])
            pi[:, sl] = _dot(ub[:, sl], wi_ref[k])
        rcf[...] = _head_rstd(yc[...], oc_ref[...], hd_c)

        c8 = RG_LRU_C * _log_sigmoid(spr(SP_LAM))
        b_a, b_i = spr(SP_B_A), spr(SP_B_I)

        def lru(r, hp):
            rows = pl.ds(r, SUBLANES)
            ra = _sigmoid(pa[rows, :] + b_a)
            ii = _sigmoid(pi[rows, :] + b_i)
            pa[rows, :] = ra
            pi[rows, :] = ii
            la = ra * c8
            a = jnp.exp(la)
            mult = _lru_input_scale(la, a)
            h = _scan_fwd(a, mult * (ii * u[rows, :]), hp, row)
            h_ref[rows, :] = h
            return _bcast_row(h, SUBLANES - 1)

        hcar[...] = _chunks(tb, SUBLANES, lru, hcar[...])
        rlf[...] = _head_rstd(h_ref[...], ol_ref[...], hd_l)

        g_c, g_l = spr(SP_CONV_G), spr(SP_LRU_G)

        def gate(r, carry):
            rows = pl.ds(r, 2 * s8)
            gc, gl = proj(rows, P_GC), proj(rows, P_GL)
            ybuf[rows, 0:d] = (yc[rows, :] * rcf[rows, :] * g_c * (gc * _sigmoid(gc))).astype(MM)
            ybuf[rows, d:2 * d] = (h_ref[rows, :] * rlf[rows, :] * g_l * (gl * _sigmoid(gl))).astype(MM)
            return carry

        _chunks(tb, 2 * s8, gate, 0)

        hres = x_ref[...] + _dot(ybuf[...], wout_ref[...])
        rf = lax.rsqrt(jnp.mean(hres * hres, axis=-1, keepdims=True) + RMS_EPS)
        hn = hres * rf
        fg = spr(SP_FINAL_G)
        err = hn * fg - tgt_ref[...]
        dout = err * (1.0 / d)
        acc_ref[0:SUBLANES, :] += (err * err).reshape(tb // SUBLANES, SUBLANES, d).sum(axis=0)
        acc_ref[SUBLANES:2 * SUBLANES, :] += (dout * hn).reshape(tb // SUBLANES, SUBLANES, d).sum(axis=0)
        gd = dout * fg
        dhres = rf * (gd - hn * jnp.mean(gd * hn, axis=-1, keepdims=True))
        dh_ref[...] = dhres
        dhb_ref[...] = dhres.astype(MM)

    vm = pl.BlockSpec(memory_space=pltpu.VMEM)
    blk = lambda w: pl.BlockSpec((tb, w), lambda i: (i, 0))
    buf = pltpu.VMEM((tb, d), F32)
    car = pltpu.VMEM((SUBLANES, d), F32)
    return pl.pallas_call(
        body, name="forward", grid=(nb,),
        in_specs=[blk(d), blk(d), blk(6 * d), vm, vm, vm, vm, vm, vm],
        out_specs=(blk(d), blk(d), blk(d), pl.BlockSpec((2 * SUBLANES, d), lambda i: (0, 0))) + (blk(d),) * 5,
        out_shape=(jax.ShapeDtypeStruct((t_len, d), F32),
                   jax.ShapeDtypeStruct((t_len, d), F32),
                   jax.ShapeDtypeStruct((t_len, d), MM),
                   jax.ShapeDtypeStruct((2 * SUBLANES, d), F32))
                  + (jax.ShapeDtypeStruct((t_len, d), F32),) * 5,
        scratch_shapes=[buf] * 2 + [pltpu.VMEM((tb, 2 * d), MM), car, car, car],
        compiler_params=_params(dimension_semantics=("arbitrary",)),
    )(x, tgt, p, wout, wa_t, wi_t, sp, ones_c, ones_l)


def _backward(p, h, dh, saved, wout, wa_t, wi_t, sp, ones_c, ones_l, tb):
    t_len, d = h.shape
    nb = t_len // tb
    n_tiles, tw = wa_t.shape[0], wa_t.shape[1]
    hd_c, hd_l = d // N_CONV_HEADS, d // N_LRU_HEADS
    s8 = SUBLANES

    def body(p_ref, h_ref, hhalo_ref, dh_ref, yc, czs, u, ra_ref, ii_ref,
             wout_ref, wa_ref, wi_ref, sp_ref, oc_ref, ol_ref,
             dp_ref, yt_ref, gwa_ref, gwi_ref, acc_ref,
             hh, dy, ybuf, rcf, rlf, qc, ql, dyc_hat, dyl_hat, dpa, dpi, du,
             car_dcz, car_a, car_g, car_du):
        i = pl.program_id(0)
        blk_idx = nb - 1 - i
        row = _row_iota(d)

        @pl.when(i == 0)
        def _():
            for ref in (car_dcz, car_a, car_g, car_du, gwa_ref, gwi_ref, acc_ref):
                ref[...] = jnp.zeros_like(ref)

        def spr(r):
            return sp_ref[r:r + 1, :]

        def proj(rows, seg):
            return p_ref[rows, seg * d:(seg + 1) * d].astype(F32)

        def put(rows, seg, halves):
            dp_ref[rows, seg * d:(seg + 1) * d] = jnp.concatenate(halves, axis=0).astype(MM)

        def acc_add(group, val):
            acc_ref[group * s8:(group + 1) * s8, :] += val

        live = jnp.where(blk_idx > 0, 1.0, 0.0).astype(F32)
        hh[0:s8, :] = hhalo_ref[...] * live
        hh[s8:, :] = h_ref[...]

        dy[...] = _dot_nt(dh_ref[...].astype(MM), wout_ref[...])

        w0, w1, w2 = spr(SP_CONV_W), spr(SP_CONV_W + 1), spr(SP_CONV_W + 2)
        l0, l1, l2, l3 = spr(SP_LRU_W), spr(SP_LRU_W + 1), spr(SP_LRU_W + 2), spr(SP_LRU_W + 3)

        rcf[...] = _head_rstd(yc[...], oc_ref[...], hd_c)
        rlf[...] = _head_rstd(h_ref[...], ol_ref[...], hd_l)

        g_c, g_l = spr(SP_CONV_G), spr(SP_LRU_G)

        def gates(r, carry):
            rows = pl.ds(r, 2 * s8)
            for (seg, off_y, src, rstd, gain, q, dhat, grp) in (
                    (P_GC, 0, yc, rcf, g_c, qc, dyc_hat, A_CONV_G),
                    (P_GL, d, h_ref, rlf, g_l, ql, dyl_hat, A_LRU_G)):
                gt = proj(rows, seg)
                sg = _sigmoid(gt)
                silu = gt * sg
                yhat = src[rows, :] * rstd[rows, :]
                nrm = yhat * gain
                ybuf[rows, off_y:off_y + d] = nrm * silu
                dout = dy[rows, off_y:off_y + d]
                dnrm = dout * silu
                dp_ref[rows, seg * d:(seg + 1) * d] = (dout * nrm * (sg * (1.0 + gt * (1.0 - sg)))).astype(MM)
                dg = dnrm * yhat
                acc_add(grp, dg[0:s8] + dg[s8:])
                dh_ = dnrm * gain
                dhat[rows, :] = dh_
                q[rows, :] = dh_ * yhat
            return carry

        _chunks(tb, 2 * s8, gates, 0)

        qc[...] = _head_sums(qc[...], oc_ref[...]) * (1.0 / hd_c)
        ql[...] = _head_sums(ql[...], ol_ref[...]) * (1.0 / hd_l)
        yt_ref[...] = ybuf[...].T.astype(MM)

        c8 = RG_LRU_C * _log_sigmoid(spr(SP_LAM))

        def conv_mixer(r, dcz_n):
            rows16 = pl.ds(r, 2 * s8)
            bg16, cg16, xc16 = proj(rows16, P_B), proj(rows16, P_C), proj(rows16, P_XC)
            z16 = cg16 * xc16
            d_b, d_c, d_x = [None, None], [None, None], [None, None]
            for j in (1, 0):
                rows, sub = pl.ds(r + j * s8, s8), slice(j * s8, (j + 1) * s8)
                rstd = rcf[rows, :]
                yhat = yc[rows, :] * rstd
                dyc = rstd * (dyc_hat[rows, :] - yhat * qc[rows, :])
                d_b[j] = dyc * czs[rows, :]
                dcz = dyc * bg16[sub]
                up1, up2 = _shift_up(dcz, dcz_n, 1, row), _shift_up(dcz, dcz_n, 2, row)
                dz = w2 * dcz + w1 * up1 + w0 * up2
                d_c[j] = dz * xc16[sub]
                d_x[j] = dz * cg16[sub]
                z = z16[sub]
                acc_add(A_CONV_W, up2 * z)
                acc_add(A_CONV_W + 1, up1 * z)
                acc_add(A_CONV_W + 2, dcz * z)
                dcz_n = dcz
            put(rows16, P_B, d_b)
            put(rows16, P_C, d_c)
            put(rows16, P_XC, d_x)
            return dcz_n

        car_dcz[...] = _chunks(tb, 2 * s8, conv_mixer, car_dcz[...], reverse=True)

        def lru_mixer(r, carry):
            a_n, g_n = carry
            for j in (1, 0):
                rows = pl.ds(r + j * s8, s8)
                rstd = rlf[rows, :]
                hcur = hh[pl.ds(r + (j + 1) * s8, s8), :]
                hhat = hcur * rstd
                dh_out = rstd * (dyl_hat[rows, :] - hhat * ql[rows, :])
                ra = ra_ref[rows, :]
                la = ra * c8
                a = jnp.exp(la)
                g = _scan_bwd(_shift_up(a, a_n, 1, row), dh_out, g_n, row)
                da = g * _shift_down(hcur, hh[pl.ds(r + j * s8, s8), :], 1, row)
                ii = ii_ref[rows, :]
                uu = u[rows, :]
                mult = _lru_input_scale(la, a)
                dmult = g * (ii * uu)
                ds = g * mult
                dla = a * (da - dmult * a / mult)
                acc_add(A_LAM, dla * ra)
                dpa_ = dla * c8 * ra * (1.0 - ra)
                dpi_ = ds * uu * ii * (1.0 - ii)
                acc_add(A_B_A, dpa_)
                acc_add(A_B_I, dpi_)
                dpa[rows, :] = dpa_
                dpi[rows, :] = dpi_
                du[rows, :] = ds * ii
                a_n, g_n = a, _bcast_row(g, 0)
            return a_n, g_n

        a_f, g_f = _chunks(tb, 2 * s8, lru_mixer, (car_a[...], car_g[...]), reverse=True)
        car_a[...] = a_f
        car_g[...] = g_f

        dpab = dpa[...].astype(MM)
        dpib = dpi[...].astype(MM)
        for k in range(n_tiles):
            sl = slice(k * tw, (k + 1) * tw)
            du[:, sl] += _dot_nt(dpab[:, sl], wa_ref[k]) + _dot_nt(dpib[:, sl], wi_ref[k])
            ut = u[:, sl].T.astype(MM)
            gwa_ref[k] += _dot(ut, dpab[:, sl])
            gwi_ref[k] += _dot(ut, dpib[:, sl])

        def lru_conv(r, du_n):
            rows16 = pl.ds(r, 2 * s8)
            xl16 = proj(rows16, P_XL)
            d_xl = [None, None]
            for j in (1, 0):
                rows, sub = pl.ds(r + j * s8, s8), slice(j * s8, (j + 1) * s8)
                dut = du[rows, :]
                up1, up2, up3 = (_shift_up(dut, du_n, s, row) for s in (1, 2, 3))
                d_xl[j] = l3 * dut + l2 * up1 + l1 * up2 + l0 * up3
                xl = xl16[sub]
                acc_add(A_LRU_W, up3 * xl)
                acc_add(A_LRU_W + 1, up2 * xl)
                acc_add(A_LRU_W + 2, up1 * xl)
                acc_add(A_LRU_W + 3, dut * xl)
                acc_add(A_LRU_B, dut)
                du_n = dut
            put(rows16, P_XL, d_xl)
            return du_n

        car_du[...] = _chunks(tb, 2 * s8, lru_conv, car_du[...], reverse=True)

    vm = pl.BlockSpec(memory_space=pltpu.VMEM)
    rev = lambda w: pl.BlockSpec((tb, w), lambda i: (nb - 1 - i, 0))
    halo = lambda rows, w: pl.BlockSpec((rows, w), lambda i: (jnp.maximum((nb - 1 - i) * (tb // rows) - 1, 0), 0))
    const = lambda shape: pl.BlockSpec(shape, lambda i: (0,) * len(shape))
    buf = lambda w: pltpu.VMEM((tb, w), F32)
    car = pltpu.VMEM((SUBLANES, d), F32)
    return pl.pallas_call(
        body, name="backward", grid=(nb,),
        in_specs=[rev(6 * d), rev(d), halo(SUBLANES, d), rev(d)] + [rev(d)] * 5 + [vm, vm, vm, vm, vm, vm],
        out_specs=(rev(6 * d), pl.BlockSpec((2 * d, tb), lambda i: (0, nb - 1 - i)),
                   const((n_tiles, tw, tw)), const((n_tiles, tw, tw)), const((A_GROUPS * SUBLANES, d))),
        out_shape=(jax.ShapeDtypeStruct((t_len, 6 * d), MM),
                   jax.ShapeDtypeStruct((2 * d, t_len), MM),
                   jax.ShapeDtypeStruct((n_tiles, tw, tw), F32),
                   jax.ShapeDtypeStruct((n_tiles, tw, tw), F32),
                   jax.ShapeDtypeStruct((A_GROUPS * SUBLANES, d), F32)),
        scratch_shapes=[pltpu.VMEM((SUBLANES + tb, d), F32), buf(2 * d), buf(2 * d)] + [buf(d)] * 9 + [car, car, car, car],
        compiler_params=_params(dimension_semantics=("arbitrary",)),
    )(p, h, h, dh, *saved, wout, wa_t, wi_t, sp, ones_c, ones_l)


def _input_grad(dp, win_all, x, dh, sp, parts, tb):
    t_len, d = x.shape
    nb = t_len // tb
    cols = win_all.shape[2]
    n_parts = len(parts)

    def body(dp_ref, win_ref, x_ref, dh_ref, sp_ref, *refs):
        gx_ref, acc_ref = refs[n_parts:n_parts + 2]
        exchange = _ChipExchange(refs[:n_parts], refs[n_parts + 2:2 * n_parts + 2], *refs[2 * n_parts + 2:])
        i = pl.program_id(0)

        @pl.when(i == 0)
        def _():
            acc_ref[...] = jnp.zeros_like(acc_ref)
            exchange.start()

        dxn = _dot_nt(dp_ref[:, 0:cols], win_ref[0])
        for j in range(1, N_DEV):
            dxn += _dot_nt(dp_ref[:, j * cols:(j + 1) * cols], win_ref[j])
        xv = x_ref[...]
        r0 = lax.rsqrt(jnp.mean(xv * xv, axis=-1, keepdims=True) + RMS_EPS)
        xhat = xv * r0
        acc_ref[...] += (dxn * xhat).reshape(tb // SUBLANES, SUBLANES, d).sum(axis=0)
        dxh = dxn * sp_ref[SP_LN_G:SP_LN_G + 1, :]
        gx_ref[...] = dh_ref[...] + r0 * (dxh - xhat * jnp.mean(dxh * xhat, axis=-1, keepdims=True))

        @pl.when(i == nb - 1)
        def _():
            exchange.finish()

    vm = pl.BlockSpec(memory_space=pltpu.VMEM)
    hbm = pl.BlockSpec(memory_space=pl.ANY)
    blk = lambda w: pl.BlockSpec((tb, w), lambda i: (i, 0))
    outs = pl.pallas_call(
        body, name="input_grad", grid=(nb,),
        in_specs=[blk(6 * d), vm, blk(d), blk(d), vm] + [hbm] * n_parts,
        out_specs=(blk(d), pl.BlockSpec((SUBLANES, d), lambda i: (0, 0))) + (hbm,) * n_parts,
        out_shape=(jax.ShapeDtypeStruct((t_len, d), F32), jax.ShapeDtypeStruct((SUBLANES, d), F32))
                  + tuple(jax.ShapeDtypeStruct(p.shape, p.dtype) for p in parts),
        scratch_shapes=[pltpu.SemaphoreType.DMA((3 * n_parts,)), pltpu.SemaphoreType.DMA((3 * n_parts,))],
        compiler_params=_params(dimension_semantics=("arbitrary",)),
    )(dp, win_all, x, dh, sp, *parts)
    return outs[0], outs[1], outs[2:]


_CHIP_RELATIONS = [(0, 0), (1, 0), (0, 1), (1, 1)]


def _related_block(k, core):
    x, y, _ = _mesh_pos()
    fx, fy = _CHIP_RELATIONS[k]
    return 4 * (x ^ fx) + 2 * (y ^ fy) + core


class _ChipExchange:
    def __init__(self, part_refs, land_refs, send_sems, recv_sems):
        self.part_refs, self.land_refs, self.send_sems, self.recv_sems = part_refs, land_refs, send_sems, recv_sems

    def copies(self):
        x, y, c = _mesh_pos()
        for a in range(len(self.part_refs)):
            for k in (1, 2, 3):
                fx, fy = _CHIP_RELATIONS[k]
                yield pltpu.make_async_remote_copy(
                    src_ref=self.part_refs[a].at[k - 1], dst_ref=self.land_refs[a].at[k - 1],
                    send_sem=self.send_sems.at[3 * a + k - 1], recv_sem=self.recv_sems.at[3 * a + k - 1],
                    device_id=(x ^ fx, y ^ fy, c), device_id_type=MESH)

    def start(self):
        for cp in self.copies():
            cp.start()

    def finish(self):
        for cp in self.copies():
            cp.wait_recv()
        for cp in self.copies():
            cp.wait_send()


def _weight_grad_stage1(name, blk_shape, n_split, operands, in_specs, product, riders=()):
    n_rows, n_cols = blk_shape
    rs = n_rows // n_split
    rc = 32
    n_in, n_ride = len(operands), len(riders)
    _, _, c = _mesh_pos()
    order = jnp.stack([_related_block(k, 1 - c) for k in range(4)]
                      + [_related_block(k, c) for k in (1, 2, 3, 0)]).astype(jnp.int32)

    def body(order_ref, *refs):
        ins = refs[:n_in]
        ride_in = refs[n_in:n_in + n_ride]
        part_ref, own_ref = refs[n_in + n_ride:n_in + n_ride + 2]
        ride_out = refs[n_in + n_ride + 2:n_in + 2 * n_ride + 2]
        gbuf, sendbuf, from_sib, send_sems, recv_sems, ride_send, ride_recv = refs[n_in + 2 * n_ride + 2:]
        exchange = _ChipExchange(ride_in, ride_out, ride_send, ride_recv)
        s = pl.program_id(0)
        x, y, c = _mesh_pos()

        def to_sibling(k):
            return pltpu.make_async_remote_copy(
                src_ref=sendbuf.at[k], dst_ref=from_sib.at[k], send_sem=send_sems.at[k], recv_sem=recv_sems.at[k],
                device_id=(x, y, 1 - c), device_id_type=MESH)

        if n_ride:
            @pl.when(s == 0)
            def _():
                exchange.start()

        for h in range(n_split):
            gbuf[h * rs:(h + 1) * rs, :] = product(ins, h)

        @pl.when(s < 4)
        def _():
            def narrow(r, carry):
                sendbuf[s, pl.ds(r, rc), :] = gbuf[pl.ds(r, rc), :].astype(MM)
                return carry

            _chunks(n_rows, rc, narrow, 0)
            to_sibling(s).start()

        @pl.when(s >= 4)
        def _():
            k = jnp.where(s == 7, 0, s - 3)
            to_sibling(k).wait_recv()

            @pl.when(s < 7)
            def _():
                def add(r, carry):
                    rows = pl.ds(r, rc)
                    part_ref[0, rows, :] = (gbuf[rows, :] + from_sib[k, rows, :].astype(F32)).astype(MM)
                    return carry

                _chunks(n_rows, rc, add, 0)

            @pl.when(s == 7)
            def _():
                def add(r, carry):
                    rows = pl.ds(r, rc)
                    own_ref[rows, :] = gbuf[rows, :] + from_sib[0, rows, :].astype(F32)
                    return carry

                _chunks(n_rows, rc, add, 0)
                for kk in range(4):
                    to_sibling(kk).wait_send()
                if n_ride:
                    exchange.finish()

    hbm = pl.BlockSpec(memory_space=pl.ANY)
    grid_spec = pltpu.PrefetchScalarGridSpec(
        num_scalar_prefetch=1, grid=(N_DEV,), in_specs=list(in_specs) + [hbm] * n_ride,
        out_specs=(pl.BlockSpec((1, n_rows, n_cols), lambda s, o: (jnp.clip(s - 4, 0, 2), 0, 0)),
                   pl.BlockSpec((n_rows, n_cols), lambda s, o: (0, 0))) + (hbm,) * n_ride,
        scratch_shapes=[pltpu.VMEM((n_rows, n_cols), F32), pltpu.VMEM((4, n_rows, n_cols), MM),
                        pltpu.VMEM((4, n_rows, n_cols), MM),
                        pltpu.SemaphoreType.DMA((4,)), pltpu.SemaphoreType.DMA((4,)),
                        pltpu.SemaphoreType.DMA((max(3 * n_ride, 1),)), pltpu.SemaphoreType.DMA((max(3 * n_ride, 1),))])
    outs = pl.pallas_call(
        body, name=name, grid_spec=grid_spec,
        out_shape=(jax.ShapeDtypeStruct((3, n_rows, n_cols), MM), jax.ShapeDtypeStruct((n_rows, n_cols), F32))
                  + tuple(jax.ShapeDtypeStruct(p.shape, p.dtype) for p in riders),
        compiler_params=_params(dimension_semantics=("arbitrary",)),
    )(order, *operands, *riders)
    return outs[0], outs[1], outs[2:]


def _weight_grad_in(xnt, dp, riders):
    d, t_len = xnt.shape
    cols = dp.shape[1] // N_DEV
    half = d // 2
    return _weight_grad_stage1(
        "weight_grad_in", (d, cols), 2, (xnt, dp),
        [pl.BlockSpec(memory_space=pltpu.VMEM), pl.BlockSpec((t_len, cols), lambda s, o: (0, o[s]))],
        lambda refs, h: _dot(refs[0][h * half:(h + 1) * half, :], refs[1][...]), riders)


def _weight_grad_out(yt, dhb):
    d2, t_len = yt.shape
    d = dhb.shape[1]
    rows = d2 // N_DEV
    return _weight_grad_stage1(
        "weight_grad_out", (rows, d), 1, (yt, dhb),
        [pl.BlockSpec((rows, t_len), lambda s, o: (o[s], 0)), pl.BlockSpec(memory_space=pltpu.VMEM)],
        lambda refs, h: _dot(refs[0][...], refs[1][...]))


def _update_shard(own, from_chips, w, m, v, name):
    n_rows, n_cols = w.shape
    rb = min(256, n_rows)

    def body(own_ref, fc_ref, w_ref, m_ref, v_ref, grad_ref, delta_ref, mo_ref, vo_ref):
        g = own_ref[...]
        for k in range(3):
            g = g + fc_ref[k].astype(F32)
        delta, m_new, v_new = _adamw(w_ref[...], g, m_ref[...], v_ref[...])
        grad_ref[...] = g
        delta_ref[...] = delta
        mo_ref[...] = m_new
        vo_ref[...] = v_new

    blk = pl.BlockSpec((rb, n_cols), lambda i: (i, 0))
    out = jax.ShapeDtypeStruct((n_rows, n_cols), F32)
    return pl.pallas_call(
        body, name=name, grid=(n_rows // rb,),
        in_specs=[blk, pl.BlockSpec((3, rb, n_cols), lambda i: (0, i, 0)), blk, blk, blk],
        out_specs=(blk, blk, blk, blk), out_shape=(out, out, out, out),
        compiler_params=_params(dimension_semantics=("arbitrary",)),
    )(own, from_chips, w, m, v)


def _small_update(facc, xacc, bacc, gwa, gwi, lam, wsl, msl, vsl, cw, cm, cv):
    n_rows, d = wsl.shape
    s8 = SUBLANES
    cc = cw.shape[1]

    def body(facc_ref, xacc_ref, bacc_ref, gwa_ref, gwi_ref, lam_ref, w_ref, m_ref, v_ref, cw_ref, cm_ref, cv_ref,
             g_o, d_o, m_o, v_o, cg_o, cd_o, cm_o, cv_o, gat, send_sems, recv_sems):
        x, y, c = _mesh_pos()
        me = 4 * x + 2 * y + c

        def rowsum(ref, group):
            return jnp.sum(ref[group * s8:(group + 1) * s8, :], axis=0, keepdims=True)

        mine = gat.at[me]
        loss = jnp.sum(rowsum(facc_ref, 0), axis=1, keepdims=True) * (0.5 / d)
        mine[SL_LOSS:SL_LOSS + 1, :] = jnp.broadcast_to(loss, (1, d))
        mine[SL_LN_G:SL_LN_G + 1, :] = rowsum(xacc_ref, 0)
        mine[SL_LRU_B:SL_LRU_B + 1, :] = rowsum(bacc_ref, A_LRU_B)
        mine[SL_B_A:SL_B_A + 1, :] = rowsum(bacc_ref, A_B_A)
        mine[SL_B_I:SL_B_I + 1, :] = rowsum(bacc_ref, A_B_I)
        mine[SL_LAM:SL_LAM + 1, :] = rowsum(bacc_ref, A_LAM) * (RG_LRU_C * jax.nn.sigmoid(-lam_ref[...]))
        mine[SL_CONV_G:SL_CONV_G + 1, :] = rowsum(bacc_ref, A_CONV_G)
        mine[SL_LRU_G:SL_LRU_G + 1, :] = rowsum(bacc_ref, A_LRU_G)
        mine[SL_FINAL_G:SL_FINAL_G + 1, :] = rowsum(facc_ref, 1)
        for k in range(3):
            mine[SL_CONV_W + k:SL_CONV_W + k + 1, :] = rowsum(bacc_ref, A_CONV_W + k)
        for k in range(4):
            mine[SL_LRU_W + k:SL_LRU_W + k + 1, :] = rowsum(bacc_ref, A_LRU_W + k)
        na = gwa_ref.shape[0]
        mine[SL_W_A:SL_W_A + na, :] = gwa_ref[...]
        mine[SL_W_A + na:SL_W_A + 2 * na, :] = gwi_ref[...]

        gather = _Gather(lambda a, px, py, pc: gat.at[4 * px + 2 * py + pc], send_sems, recv_sems)
        gather.start_own(0)
        gather.finish(0)

        def update(r, carry):
            rows = pl.ds(r, s8)
            g = gat[0, rows, :]
            for b in range(1, N_DEV):
                g = g + gat[b, rows, :]
            delta, m_new, v_new = _adamw(w_ref[rows, :], g, m_ref[rows, :], v_ref[rows, :])
            g_o[rows, :] = g
            d_o[rows, :] = delta
            m_o[rows, :] = m_new
            v_o[rows, :] = v_new
            return carry

        _chunks(n_rows, s8, update, 0)
        gc = g_o[s8:2 * s8, pl.ds(pl.multiple_of(me * cc, cc), cc)]
        delta, m_new, v_new = _adamw(cw_ref[...], gc, cm_ref[...], cv_ref[...])
        cg_o[...] = gc
        cd_o[...] = delta
        cm_o[...] = m_new
        cv_o[...] = v_new

    vm = pl.BlockSpec(memory_space=pltpu.VMEM)
    big = jax.ShapeDtypeStruct((n_rows, d), F32)
    small = jax.ShapeDtypeStruct(cw.shape, F32)
    return pl.pallas_call(
        body, name="small_update",
        in_specs=[vm] * 12, out_specs=(vm,) * 8, out_shape=(big, big, big, big, small, small, small, small),
        scratch_shapes=[pltpu.VMEM((N_DEV, n_rows, d), F32), pltpu.SemaphoreType.DMA((7,)), pltpu.SemaphoreType.DMA((7,))],
        compiler_params=_params(),
    )(facc, xacc, bacc, gwa, gwi, lam, wsl, msl, vsl, cw, cm, cv)


def _head_ones(head_dim, tw):
    lane = jnp.arange(tw) // head_dim
    return (lane[:, None] == lane[None, :]).astype(MM)


def _gate_tiles(w, tw):
    n_heads, hd, _ = w.shape
    per = tw // hd
    w4 = w.reshape(n_heads // per, per, hd, hd)
    eye = jnp.eye(per, dtype=w.dtype)
    return (w4[:, :, :, None, :] * eye[None, :, None, :, None]).reshape(n_heads // per, tw, tw)


def _gate_blocks(tiles, n_heads, hd):
    n_tiles, tw, _ = tiles.shape
    per = tw // hd
    t5 = tiles.reshape(n_tiles, per, hd, per, hd)
    diag = jnp.stack([t5[:, a, :, a, :] for a in range(per)], axis=1)
    return diag.reshape(hd, n_heads * hd)


def kernel(x, ln_g, w_in, conv_w, lru_conv_w, lru_conv_b, w_a, b_a, w_i, b_i, lam, conv_out_g, lru_out_g, w_out, final_g, loss_target, m_ln_g, m_w_in, m_conv_w, m_lru_conv_w, m_lru_conv_b, m_w_a, m_b_a, m_w_i, m_b_i, m_lam, m_conv_out_g, m_lru_out_g, m_w_out, m_final_g, v_ln_g, v_w_in, v_conv_w, v_lru_conv_w, v_lru_conv_b, v_w_a, v_b_a, v_w_i, v_b_i, v_lam, v_conv_out_g, v_lru_out_g, v_w_out, v_final_g):
    _, t_len, d = x.shape
    hd_l = d // N_LRU_HEADS
    tw = min(MXU_TILE, d)
    cc = conv_w.shape[1]
    x2, tgt2 = x[0], loss_target[0]

    def conv_rows(cw3, lw4):
        return jnp.concatenate([jnp.zeros((1, cc), F32), cw3, lw4], axis=0)

    p, xnt, win_all, wout_all, conv_all = _gather_project(
        x2, w_in, w_out, conv_rows(conv_w, lru_conv_w), ln_g.reshape(1, d), min(256, t_len))
    wout_full = wout_all.reshape(N_DEV * w_out.shape[0], d)
    conv_full = conv_all.transpose(1, 0, 2).reshape(SUBLANES, d)
    small = [ln_g, lru_conv_b, b_a, b_i, lam, conv_out_g, lru_out_g, final_g]
    sp = jnp.concatenate([jnp.stack(small), conv_full[1:], jnp.zeros((1, d), F32)], axis=0)
    wa_t, wi_t = _gate_tiles(w_a, tw).astype(MM), _gate_tiles(w_i, tw).astype(MM)
    ones_c, ones_l = _head_ones(d // N_CONV_HEADS, tw), _head_ones(hd_l, tw)

    h, dh, dhb, facc, *saved = _forward(x2, tgt2, p, wout_full, wa_t, wi_t, sp, ones_c, ones_l, min(256, t_len))
    dp, yt, gwa_t, gwi_t, bacc = _backward(p, h, dh, saved, wout_full, wa_t, wi_t, sp, ones_c, ones_l, min(256, t_len))
    part_out, own_out, _ = _weight_grad_out(yt, dhb)
    part_in, own_in, (chips_out,) = _weight_grad_in(xnt, dp, (part_out,))
    grad_x, xacc, (chips_in,) = _input_grad(dp, win_all, x2, dh, sp, (part_in,), min(512, t_len))
    gw_in, dw_in, mw_in, vw_in = _update_shard(own_in, chips_in, w_in, m_w_in, v_w_in, "update_w_in")
    gw_out, dw_out, mw_out, vw_out = _update_shard(own_out, chips_out, w_out, m_w_out, v_w_out, "update_w_out")

    def slab(parts, wa_, wi_):
        return jnp.concatenate([jnp.zeros((1, d), F32), jnp.stack(parts), jnp.zeros((SL_W_A - SL_CONV_W, d), F32),
                                wa_.reshape(-1, d), wi_.reshape(-1, d)], axis=0)

    wsl = slab(small, w_a, w_i)
    msl = slab([m_ln_g, m_lru_conv_b, m_b_a, m_b_i, m_lam, m_conv_out_g, m_lru_out_g, m_final_g], m_w_a, m_w_i)
    vsl = slab([v_ln_g, v_lru_conv_b, v_b_a, v_b_i, v_lam, v_conv_out_g, v_lru_out_g, v_final_g], v_w_a, v_w_i)
    outs = _small_update(
        facc, xacc, bacc, _gate_blocks(gwa_t, N_LRU_HEADS, hd_l), _gate_blocks(gwi_t, N_LRU_HEADS, hd_l),
        lam.reshape(1, d), wsl, msl, vsl,
        conv_rows(conv_w, lru_conv_w), conv_rows(m_conv_w, m_lru_conv_w), conv_rows(v_conv_w, v_lru_conv_w))
    sl_g, sl_d, sl_m, sl_v, c_g, c_d, c_m, c_v = outs
    loss = sl_g[SL_LOSS, 0]
    na = w_a.size // d

    def unpack(sl, cv, big_in, big_out):
        one = lambda r: sl[r]
        return [one(SL_LN_G), big_in, cv[1:4], cv[4:8], one(SL_LRU_B),
                sl[SL_W_A:SL_W_A + na].reshape(w_a.shape), one(SL_B_A),
                sl[SL_W_A + na:SL_W_A + 2 * na].reshape(w_i.shape), one(SL_B_I), one(SL_LAM),
                one(SL_CONV_G), one(SL_LRU_G), big_out, one(SL_FINAL_G)]

    return (loss, grad_x[None], *unpack(sl_g, c_g, gw_in, gw_out), *unpack(sl_d, c_d, dw_in, dw_out),
            *unpack(sl_m, c_m, mw_in, mw_out), *unpack(sl_v, c_v, vw_in, vw_out))
```

```python
import functools

import jax
import jax.numpy as jnp
from jax import lax
from jax.experimental import pallas as pl
from jax.experimental.pallas import tpu as pltpu

F32 = jnp.float32
MM = jnp.bfloat16
MESH = pl.DeviceIdType.MESH

N_DEV = 8
N_CONV_HEADS = 8
N_LRU_HEADS = 16
RG_LRU_C = 8.0
RMS_EPS = 1e-6
ADAM_LR, ADAM_B1, ADAM_B2, ADAM_EPS, ADAM_WD, ADAM_STEP = 0.001, 0.9, 0.999, 1e-08, 0.01, 10
ADAM_BC1 = 1.0 - ADAM_B1 ** ADAM_STEP
ADAM_BC2 = 1.0 - ADAM_B2 ** ADAM_STEP

SUBLANES = 8
LANES = 128
MXU_TILE = 256
VMEM_LIMIT = 56 * 1024 * 1024

SP_LN_G, SP_LRU_B, SP_B_A, SP_B_I, SP_LAM, SP_CONV_G, SP_LRU_G, SP_FINAL_G, SP_CONV_W, SP_LRU_W = 0, 1, 2, 3, 4, 5, 6, 7, 8, 11
SP_ROWS = 16
P_B, P_C, P_XC, P_GC, P_XL, P_GL = 0, 1, 2, 3, 4, 5
A_CONV_G, A_LRU_G, A_LAM, A_B_A, A_B_I, A_CONV_W, A_LRU_W, A_LRU_B = 0, 1, 2, 3, 4, 5, 8, 12
A_GROUPS = 13
SL_LOSS, SL_LN_G, SL_LRU_B, SL_B_A, SL_B_I, SL_LAM, SL_CONV_G, SL_LRU_G, SL_FINAL_G, SL_CONV_W, SL_LRU_W = 0, 1, 2, 3, 4, 5, 6, 7, 8, 16, 24
SL_ROWS = 32


def _params(vmem=True, **kw):
    if vmem:
        kw["vmem_limit_bytes"] = VMEM_LIMIT
    return pltpu.CompilerParams(**kw)


def _dot(a, b):
    return jnp.dot(a, b, preferred_element_type=F32)


def _dot_nt(a, b):
    return lax.dot_general(a, b, (((1,), (1,)), ((), ())), preferred_element_type=F32)


def _head_sums(v, ones_tile):
    tw = ones_tile.shape[0]
    vb = v.astype(MM)
    return jnp.concatenate([_dot(vb[:, k:k + tw], ones_tile) for k in range(0, v.shape[1], tw)], axis=1)


def _head_rstd(v, ones_tile, head_dim):
    return lax.rsqrt(_head_sums(v * v, ones_tile) * (1.0 / head_dim) + RMS_EPS)


def _sigmoid(x):
    return 0.5 * jnp.tanh(0.5 * x) + 0.5


def _lru_input_scale(log_a, a):
    return jnp.sqrt(-jnp.tanh(log_a) * (1.0 + a * a))


def _log_sigmoid(x):
    z = jnp.exp(-jnp.abs(x))
    u = 1.0 + z
    log1p_z = jnp.where(u == 1.0, z, jnp.log(u) * (z / (u - 1.0)))
    return jnp.minimum(x, 0.0) - log1p_z


def _row_iota(d):
    return lax.broadcasted_iota(jnp.int32, (SUBLANES, d), 0)


def _shift_down(cur, prev, s, row):
    return jnp.where(row >= s, pltpu.roll(cur, s, axis=0), pltpu.roll(prev, s, axis=0))


def _shift_up(cur, nxt, s, row):
    k = SUBLANES - s
    return jnp.where(row < k, pltpu.roll(cur, k, axis=0), pltpu.roll(nxt, k, axis=0))


def _scan_fwd(a, b, h_prev, row):
    for s in (1, 2, 4):
        a_s = jnp.where(row >= s, pltpu.roll(a, s, axis=0), 1.0)
        b_s = jnp.where(row >= s, pltpu.roll(b, s, axis=0), 0.0)
        b = a * b_s + b
        a = a * a_s
    return a * h_prev + b


def _scan_bwd(a_next, b, g_next, row):
    a = a_next
    for s in (1, 2, 4):
        k = SUBLANES - s
        a_s = jnp.where(row < k, pltpu.roll(a, k, axis=0), 1.0)
        b_s = jnp.where(row < k, pltpu.roll(b, k, axis=0), 0.0)
        b = a * b_s + b
        a = a * a_s
    return a * g_next + b


def _bcast_row(v, r):
    return jnp.broadcast_to(v[r:r + 1, :], v.shape)


def _chunks(n_rows, rc, body, init, reverse=False):
    n = n_rows // rc

    def step(i, carry):
        j = (n - 1 - i) if reverse else i
        return body(pl.multiple_of(j * rc, rc), carry)

    return lax.fori_loop(0, n, step, init)


def _adamw(w, g, m, v):
    m = ADAM_B1 * m + (1.0 - ADAM_B1) * g
    v = ADAM_B2 * v + (1.0 - ADAM_B2) * (g * g)
    m_hat = m / ADAM_BC1
    v_hat = v / ADAM_BC2
    delta = -ADAM_LR * (m_hat / (jnp.sqrt(v_hat) + ADAM_EPS) + ADAM_WD * w)
    return delta, m, v


def _mesh_pos():
    return lax.axis_index("x"), lax.axis_index("y"), lax.axis_index("c")


class _Gather:
    def __init__(self, blocks_of, send_sems, recv_sems, own_src=None):
        x, y, c = _mesh_pos()
        self.c = c
        self.me, self.sibling = (x, y, c), (x, y, 1 - c)
        self.chips = [(1 - x, y), (x, 1 - y), (1 - x, 1 - y)]
        self.blocks_of, self.send_sems, self.recv_sems = blocks_of, send_sems, recv_sems
        self.own_src = own_src

    def copy(self, a, k, block, to):
        src = self.blocks_of(a, *block)
        if block is self.me and self.own_src is not None:
            src = self.own_src[a]
        return pltpu.make_async_remote_copy(
            src_ref=src, dst_ref=self.blocks_of(a, *block),
            send_sem=self.send_sems.at[a * 7 + k], recv_sem=self.recv_sems.at[a * 7 + k],
            device_id=to, device_id_type=MESH)

    def start_own(self, a):
        self.copy(a, 0, self.me, self.sibling).start()
        for j, chip in enumerate(self.chips):
            self.copy(a, 1 + j, self.me, (*chip, self.c)).start()

    def wait_sibling(self, a):
        self.copy(a, 0, self.sibling, self.me).wait_recv()

    def wait_chip_and_pass_on(self, a, j):
        block = (*self.chips[j], self.c)
        self.copy(a, 1 + j, block, self.me).wait_recv()
        self.copy(a, 4 + j, block, self.sibling).start()

    def wait_passed_on(self, a, j):
        self.copy(a, 4 + j, (*self.chips[j], 1 - self.c), self.me).wait_recv()

    def wait_sends(self, a):
        self.copy(a, 0, self.me, self.sibling).wait_send()
        for j, chip in enumerate(self.chips):
            self.copy(a, 1 + j, self.me, (*chip, self.c)).wait_send()
            self.copy(a, 4 + j, (*chip, self.c), self.sibling).wait_send()

    def finish(self, a):
        for j in range(3):
            self.wait_chip_and_pass_on(a, j)
        self.wait_sibling(a)
        for j in range(3):
            self.wait_passed_on(a, j)
        self.wait_sends(a)


class _BalancedGather:
    def __init__(self, slot, send_sems, recv_sems, own_src):
        x, y, c = _mesh_pos()
        self.c = c
        self.me, self.sibling = (x, y, c), (x, y, 1 - c)
        self.chips = [(1 - x, y), (x, 1 - y), (1 - x, 1 - y)]
        self.slot, self.send_sems, self.recv_sems, self.own_src = slot, send_sems, recv_sems, own_src

    def half(self, a, block, which):
        ref = self.slot(a, *block)
        n = ref.shape[0] // 2
        return ref.at[pl.ds(which * n, n)]

    def copy(self, a, k, src, dst, to):
        return pltpu.make_async_remote_copy(
            src_ref=src, dst_ref=dst, send_sem=self.send_sems.at[a * 8 + k], recv_sem=self.recv_sems.at[a * 8 + k],
            device_id=to, device_id_type=MESH)

    def whole(self, a, k, block, to):
        src = self.own_src[a] if block is self.me else self.slot(a, *block)
        return self.copy(a, k, src, self.slot(a, *block), to)

    def halved(self, a, k, block, which, to):
        return self.copy(a, k, self.half(a, block, which), self.half(a, block, which), to)

    def on(self, chip):
        return (*self.chips[chip], self.c)

    def start_own(self, a):
        self.whole(a, 0, self.me, self.sibling).start()
        self.whole(a, 1, self.me, self.on(0)).start()
        self.whole(a, 2, self.me, self.on(1)).start()

    def wait_sibling(self, a):
        self.whole(a, 0, self.sibling, self.me).wait_recv()

    def on_neighbour(self, a, j):
        self.whole(a, 1 + j, self.on(j), self.me).wait_recv()
        self.halved(a, 3 + j, self.on(j), j, self.on(1 - j)).start()
        self.whole(a, 5 + j, self.on(j), self.sibling).start()

    def on_diagonal(self, a):
        self.halved(a, 3, self.on(2), 0, self.me).wait_recv()
        self.halved(a, 4, self.on(2), 1, self.me).wait_recv()
        self.whole(a, 7, self.on(2), self.sibling).start()

    def wait_passed_on(self, a, j):
        self.whole(a, 5 + j, (*self.chips[j], 1 - self.c), self.me).wait_recv()

    def wait_sends(self, a):
        self.whole(a, 0, self.me, self.sibling).wait_send()
        for j in range(2):
            self.whole(a, 1 + j, self.me, self.on(j)).wait_send()
            self.halved(a, 3 + j, self.on(j), j, self.on(1 - j)).wait_send()
        for j in range(3):
            self.whole(a, 5 + j, self.on(j), self.sibling).wait_send()


def _block_order():
    x, y, c = _mesh_pos()
    chips = [(x, y), (1 - x, y), (x, 1 - y), (1 - x, 1 - y)]
    return jnp.stack([4 * px + 2 * py + pc for px, py in chips for pc in (c, 1 - c)]).astype(jnp.int32)


def _gather_project(x, w_in, w_out, conv_pack, ln_g, tb):
    t_len, d = x.shape
    nb = t_len // tb
    cols = w_in.shape[1]
    mc = min(512, t_len)
    srcs = (w_in, w_out, conv_pack)
    dts = (MM, MM, F32)

    def body(order_ref, x_ref, win_ref, wout_ref, cp_ref, lng_ref, p_ref, xnt_ref, win_all, wout_all, cp_all,
             xnb, wall, st_out, st_cp, send_sems, recv_sems, cp_send, cp_recv, local_sems):
        i = pl.program_id(0)
        x_, y_, c_ = _mesh_pos()
        me = 4 * x_ + 2 * y_ + c_
        outs = (win_all, wout_all, cp_all)
        lands = (wall, wout_all, cp_all)
        stages = (wall.at[me], st_out, st_cp)
        gather = _BalancedGather(lambda a, px, py, pc: lands[a].at[4 * px + 2 * py + pc], send_sems, recv_sems, stages)
        small = _Gather(lambda a, px, py, pc: cp_all.at[4 * px + 2 * py + pc], cp_send, cp_recv, own_src=[st_cp])
        keep_own = [pltpu.make_async_copy(stages[a], outs[a].at[me], local_sems.at[a]) for a in range(3)]

        def keep(k):
            blk = order_ref[k]
            return pltpu.make_async_copy(wall.at[blk], win_all.at[blk], local_sems.at[2 + k])

        @pl.when(i == 0)
        def _():
            for a, (src, dst) in enumerate(zip((win_ref, wout_ref, cp_ref), stages)):
                rows = src.shape[0]
                rc = min(rows, 32)

                def cast(r, carry, src=src, dst=dst, rc=rc):
                    dst[pl.ds(r, rc), :] = src[pl.ds(r, rc), :].astype(dst.dtype)
                    return carry

                _chunks(rows, rc, cast, 0)
                if a < 2:
                    gather.start_own(a)
                else:
                    small.start_own(0)
                keep_own[a].start()

        @pl.when(i < nb)
        def _():
            xv = x_ref[...]
            r0 = lax.rsqrt(jnp.mean(xv * xv, axis=-1, keepdims=True) + RMS_EPS)
            xn = xv * r0 * lng_ref[...]
            xnb[pl.ds(pl.multiple_of(i * tb, tb), tb), :] = xn.astype(MM)
            xnt_ref[...] = xn.T.astype(MM)

        for k in range(N_DEV):
            @pl.when(i == nb + k)
            def _(k=k):
                if k == 1:
                    gather.wait_sibling(0)
                elif k == 2:
                    gather.on_neighbour(0, 0)
                    gather.on_neighbour(0, 1)
                elif k in (3, 5, 7):
                    gather.wait_passed_on(0, (k - 3) // 2)
                    if k == 3:
                        gather.on_neighbour(1, 0)
                        gather.on_neighbour(1, 1)
                    if k == 7:
                        gather.on_diagonal(1)
                elif k == 6:
                    gather.on_diagonal(0)
                blk = order_ref[k]
                if k:
                    keep(k).start()

                def project(r, carry):
                    rows = pl.ds(r, mc)
                    p_ref[rows, :] = _dot(xnb[rows, :], wall[blk]).astype(MM)
                    return carry

                _chunks(t_len, mc, project, 0)
                if k == N_DEV - 1:
                    gather.wait_sends(0)
                    gather.wait_sibling(1)
                    for j in range(3):
                        gather.wait_passed_on(1, j)
                    gather.wait_sends(1)
                    small.finish(0)
                    for cp in keep_own + [keep(kk) for kk in range(1, N_DEV)]:
                        cp.wait()

    vm = pl.BlockSpec(memory_space=pltpu.VMEM)
    hbm = pl.BlockSpec(memory_space=pl.ANY)
    grid_spec = pltpu.PrefetchScalarGridSpec(
        num_scalar_prefetch=1, grid=(nb + N_DEV,),
        in_specs=[pl.BlockSpec((tb, d), lambda i, o: (jnp.minimum(i, nb - 1), 0)), vm, vm, vm, vm],
        out_specs=(pl.BlockSpec((t_len, cols), lambda i, o: (0, o[jnp.maximum(i - nb, 0)])),
                   pl.BlockSpec((d, tb), lambda i, o: (0, jnp.minimum(i, nb - 1))), hbm, hbm, hbm),
        scratch_shapes=[pltpu.VMEM((t_len, d), MM), pltpu.VMEM((N_DEV,) + w_in.shape, MM),
                        pltpu.VMEM(w_out.shape, MM), pltpu.VMEM(conv_pack.shape, F32),
                        pltpu.SemaphoreType.DMA((16,)), pltpu.SemaphoreType.DMA((16,)),
                        pltpu.SemaphoreType.DMA((7,)), pltpu.SemaphoreType.DMA((7,)), pltpu.SemaphoreType.DMA((10,))])
    return pl.pallas_call(
        body, name="gather_project", grid_spec=grid_spec,
        out_shape=(jax.ShapeDtypeStruct((t_len, N_DEV * cols), MM),
                   jax.ShapeDtypeStruct((d, t_len), MM))
                  + tuple(jax.ShapeDtypeStruct((N_DEV,) + s.shape, dt) for s, dt in zip(srcs, dts)),
        compiler_params=_params(dimension_semantics=("arbitrary",)),
    )(_block_order(), x, w_in, w_out, conv_pack, ln_g)


def _forward(x, tgt, p, wout, wa_t, wi_t, sp, ones_c, ones_l, tb):
    t_len, d = x.shape
    nb = t_len // tb
    n_tiles, tw = wa_t.shape[0], wa_t.shape[1]
    hd_c, hd_l = d // N_CONV_HEADS, d // N_LRU_HEADS
    s8 = SUBLANES

    def body(x_ref, tgt_ref, p_ref, wout_ref, wa_ref, wi_ref, sp_ref, oc_ref, ol_ref,
             h_ref, dh_ref, dhb_ref, acc_ref, yc, czs, u, pa, pi,
             rcf, rlf, ybuf, tail_z, tail_xl, hcar):
        i = pl.program_id(0)
        row = _row_iota(d)

        @pl.when(i == 0)
        def _():
            tail_z[...] = jnp.zeros_like(tail_z)
            tail_xl[...] = jnp.zeros_like(tail_xl)
            hcar[...] = jnp.zeros_like(hcar)
            acc_ref[...] = jnp.zeros_like(acc_ref)

        def spr(r):
            return sp_ref[r:r + 1, :]

        def proj(rows, seg):
            return p_ref[rows, seg * d:(seg + 1) * d].astype(F32)

        w0, w1, w2 = spr(SP_CONV_W), spr(SP_CONV_W + 1), spr(SP_CONV_W + 2)
        l0, l1, l2, l3 = spr(SP_LRU_W), spr(SP_LRU_W + 1), spr(SP_LRU_W + 2), spr(SP_LRU_W + 3)
        lb = spr(SP_LRU_B)

        def convs(r, carry):
            zp, xp = carry
            rows16 = pl.ds(r, 2 * s8)
            bg16, xl16 = proj(rows16, P_B), proj(rows16, P_XL)
            z16 = proj(rows16, P_C) * proj(rows16, P_XC)
            for j in range(2):
                rows, sub = pl.ds(r + j * s8, s8), slice(j * s8, (j + 1) * s8)
                z, xl = z16[sub], xl16[sub]
                cz = w0 * _shift_down(z, zp, 2, row) + w1 * _shift_down(z, zp, 1, row) + w2 * z
                czs[rows, :] = cz
                yc[rows, :] = bg16[sub] * cz
                u[rows, :] = (l0 * _shift_down(xl, xp, 3, row) + l1 * _shift_down(xl, xp, 2, row)
                              + l2 * _shift_down(xl, xp, 1, row) + l3 * xl + lb)
                zp, xp = z, xl
            return zp, xp

        z_last, xl_last = _chunks(tb, 2 * s8, convs, (tail_z[...], tail_xl[...]))
        tail_z[...] = z_last
        tail_xl[...] = xl_last

        ub = u[...].astype(MM)
        for k in range(n_tiles):
            sl = slice(k * tw, (k + 1) * tw)
            pa[:, sl] = _dot(ub[:, sl], wa_ref[k])
            pi[:, sl] = _dot(ub[:, sl], wi_ref[k])
        rcf[...] = _head_rstd(yc[...], oc_ref[...], hd_c)

        c8 = RG_LRU_C * _log_sigmoid(spr(SP_LAM))
        b_a, b_i = spr(SP_B_A), spr(SP_B_I)

        def lru(r, hp):
            rows = pl.ds(r, SUBLANES)
            ra = _sigmoid(pa[rows, :] + b_a)
            ii = _sigmoid(pi[rows, :] + b_i)
            pa[rows, :] = ra
            pi[rows, :] = ii
            la = ra * c8
            a = jnp.exp(la)
            mult = _lru_input_scale(la, a)
            h = _scan_fwd(a, mult * (ii * u[rows, :]), hp, row)
            h_ref[rows, :] = h
            return _bcast_row(h, SUBLANES - 1)

        hcar[...] = _chunks(tb, SUBLANES, lru, hcar[...])
        rlf[...] = _head_rstd(h_ref[...], ol_ref[...], hd_l)

        g_c, g_l = spr(SP_CONV_G), spr(SP_LRU_G)

        def gate(r, carry):
            rows = pl.ds(r, 2 * s8)
            gc, gl = proj(rows, P_GC), proj(rows, P_GL)
            ybuf[rows, 0:d] = (yc[rows, :] * rcf[rows, :] * g_c * (gc * _sigmoid(gc))).astype(MM)
            ybuf[rows, d:2 * d] = (h_ref[rows, :] * rlf[rows, :] * g_l * (gl * _sigmoid(gl))).astype(MM)
            return carry

        _chunks(tb, 2 * s8, gate, 0)

        hres = x_ref[...] + _dot(ybuf[...], wout_ref[...])
        rf = lax.rsqrt(jnp.mean(hres * hres, axis=-1, keepdims=True) + RMS_EPS)
        hn = hres * rf
        fg = spr(SP_FINAL_G)
        err = hn * fg - tgt_ref[...]
        dout = err * (1.0 / d)
        acc_ref[0:SUBLANES, :] += (err * err).reshape(tb // SUBLANES, SUBLANES, d).sum(axis=0)
        acc_ref[SUBLANES:2 * SUBLANES, :] += (dout * hn).reshape(tb // SUBLANES, SUBLANES, d).sum(axis=0)
        gd = dout * fg
        dhres = rf * (gd - hn * jnp.mean(gd * hn, axis=-1, keepdims=True))
        dh_ref[...] = dhres
        dhb_ref[...] = dhres.astype(MM)

    vm = pl.BlockSpec(memory_space=pltpu.VMEM)
    blk = lambda w: pl.BlockSpec((tb, w), lambda i: (i, 0))
    buf = pltpu.VMEM((tb, d), F32)
    car = pltpu.VMEM((SUBLANES, d), F32)
    return pl.pallas_call(
        body, name="forward", grid=(nb,),
        in_specs=[blk(d), blk(d), blk(6 * d), vm, vm, vm, vm, vm, vm],
        out_specs=(blk(d), blk(d), blk(d), pl.BlockSpec((2 * SUBLANES, d), lambda i: (0, 0))) + (blk(d),) * 5,
        out_shape=(jax.ShapeDtypeStruct((t_len, d), F32),
                   jax.ShapeDtypeStruct((t_len, d), F32),
                   jax.ShapeDtypeStruct((t_len, d), MM),
                   jax.ShapeDtypeStruct((2 * SUBLANES, d), F32))
                  + (jax.ShapeDtypeStruct((t_len, d), F32),) * 5,
        scratch_shapes=[buf] * 2 + [pltpu.VMEM((tb, 2 * d), MM), car, car, car],
        compiler_params=_params(dimension_semantics=("arbitrary",)),
    )(x, tgt, p, wout, wa_t, wi_t, sp, ones_c, ones_l)


def _backward(p, h, dh, saved, wout, wa_t, wi_t, sp, ones_c, ones_l, tb):
    t_len, d = h.shape
    nb = t_len // tb
    n_tiles, tw = wa_t.shape[0], wa_t.shape[1]
    hd_c, hd_l = d // N_CONV_HEADS, d // N_LRU_HEADS
    s8 = SUBLANES

    def body(p_ref, h_ref, hhalo_ref, dh_ref, yc, czs, u, ra_ref, ii_ref,
             wout_ref, wa_ref, wi_ref, sp_ref, oc_ref, ol_ref,
             dp_ref, yt_ref, gwa_ref, gwi_ref, acc_ref,
             hh, dy, ybuf, rcf, rlf, qc, ql, dyc_hat, dyl_hat, dpa, dpi, du,
             car_dcz, car_a, car_g, car_du):
        i = pl.program_id(0)
        blk_idx = nb - 1 - i
        row = _row_iota(d)

        @pl.when(i == 0)
        def _():
            for ref in (car_dcz, car_a, car_g, car_du, gwa_ref, gwi_ref, acc_ref):
                ref[...] = jnp.zeros_like(ref)

        def spr(r):
            return sp_ref[r:r + 1, :]

        def proj(rows, seg):
            return p_ref[rows, seg * d:(seg + 1) * d].astype(F32)

        def put(rows, seg, halves):
            dp_ref[rows, seg * d:(seg + 1) * d] = jnp.concatenate(halves, axis=0).astype(MM)

        def acc_add(group, val):
            acc_ref[group * s8:(group + 1) * s8, :] += val

        live = jnp.where(blk_idx > 0, 1.0, 0.0).astype(F32)
        hh[0:s8, :] = hhalo_ref[...] * live
        hh[s8:, :] = h_ref[...]

        dy[...] = _dot_nt(dh_ref[...].astype(MM), wout_ref[...])

        w0, w1, w2 = spr(SP_CONV_W), spr(SP_CONV_W + 1), spr(SP_CONV_W + 2)
        l0, l1, l2, l3 = spr(SP_LRU_W), spr(SP_LRU_W + 1), spr(SP_LRU_W + 2), spr(SP_LRU_W + 3)

        rcf[...] = _head_rstd(yc[...], oc_ref[...], hd_c)
        rlf[...] = _head_rstd(h_ref[...], ol_ref[...], hd_l)

        g_c, g_l = spr(SP_CONV_G), spr(SP_LRU_G)

        def gates(r, carry):
            rows = pl.ds(r, 2 * s8)
            for (seg, off_y, src, rstd, gain, q, dhat, grp) in (
                    (P_GC, 0, yc, rcf, g_c, qc, dyc_hat, A_CONV_G),
                    (P_GL, d, h_ref, rlf, g_l, ql, dyl_hat, A_LRU_G)):
                gt = proj(rows, seg)
                sg = _sigmoid(gt)
                silu = gt * sg
                yhat = src[rows, :] * rstd[rows, :]
                nrm = yhat * gain
                ybuf[rows, off_y:off_y + d] = nrm * silu
                dout = dy[rows, off_y:off_y + d]
                dnrm = dout * silu
                dp_ref[rows, seg * d:(seg + 1) * d] = (dout * nrm * (sg * (1.0 + gt * (1.0 - sg)))).astype(MM)
                dg = dnrm * yhat
                acc_add(grp, dg[0:s8] + dg[s8:])
                dh_ = dnrm * gain
                dhat[rows, :] = dh_
                q[rows, :] = dh_ * yhat
            return carry

        _chunks(tb, 2 * s8, gates, 0)

        qc[...] = _head_sums(qc[...], oc_ref[...]) * (1.0 / hd_c)
        ql[...] = _head_sums(ql[...], ol_ref[...]) * (1.0 / hd_l)
        yt_ref[...] = ybuf[...].T.astype(MM)

        c8 = RG_LRU_C * _log_sigmoid(spr(SP_LAM))

        def conv_mixer(r, dcz_n):
            rows16 = pl.ds(r, 2 * s8)
            bg16, cg16, xc16 = proj(rows16, P_B), proj(rows16, P_C), proj(rows16, P_XC)
            z16 = cg16 * xc16
            d_b, d_c, d_x = [None, None], [None, None], [None, None]
            for j in (1, 0):
                rows, sub = pl.ds(r + j * s8, s8), slice(j * s8, (j + 1) * s8)
                rstd = rcf[rows, :]
                yhat = yc[rows, :] * rstd
                dyc = rstd * (dyc_hat[rows, :] - yhat * qc[rows, :])
                d_b[j] = dyc * czs[rows, :]
                dcz = dyc * bg16[sub]
                up1, up2 = _shift_up(dcz, dcz_n, 1, row), _shift_up(dcz, dcz_n, 2, row)
                dz = w2 * dcz + w1 * up1 + w0 * up2
                d_c[j] = dz * xc16[sub]
                d_x[j] = dz * cg16[sub]
                z = z16[sub]
                acc_add(A_CONV_W, up2 * z)
                acc_add(A_CONV_W + 1, up1 * z)
                acc_add(A_CONV_W + 2, dcz * z)
                dcz_n = dcz
            put(rows16, P_B, d_b)
            put(rows16, P_C, d_c)
            put(rows16, P_XC, d_x)
            return dcz_n

        car_dcz[...] = _chunks(tb, 2 * s8, conv_mixer, car_dcz[...], reverse=True)

        def lru_mixer(r, carry):
            a_n, g_n = carry
            for j in (1, 0):
                rows = pl.ds(r + j * s8, s8)
                rstd = rlf[rows, :]
                hcur = hh[pl.ds(r + (j + 1) * s8, s8), :]
                hhat = hcur * rstd
                dh_out = rstd * (dyl_hat[rows, :] - hhat * ql[rows, :])
                ra = ra_ref[rows, :]
                la = ra * c8
                a = jnp.exp(la)
                g = _scan_bwd(_shift_up(a, a_n, 1, row), dh_out, g_n, row)
                da = g * _shift_down(hcur, hh[pl.ds(r + j * s8, s8), :], 1, row)
                ii = ii_ref[rows, :]
                uu = u[rows, :]
                mult = _lru_input_scale(la, a)
                dmult = g * (ii * uu)
                ds = g * mult
                dla = a * (da - dmult * a / mult)
                acc_add(A_LAM, dla * ra)
                dpa_ = dla * c8 * ra * (1.0 - ra)
                dpi_ = ds * uu * ii * (1.0 - ii)
                acc_add(A_B_A, dpa_)
                acc_add(A_B_I, dpi_)
                dpa[rows, :] = dpa_
                dpi[rows, :] = dpi_
                du[rows, :] = ds * ii
                a_n, g_n = a, _bcast_row(g, 0)
            return a_n, g_n

        a_f, g_f = _chunks(tb, 2 * s8, lru_mixer, (car_a[...], car_g[...]), reverse=True)
        car_a[...] = a_f
        car_g[...] = g_f

        dpab = dpa[...].astype(MM)
        dpib = dpi[...].astype(MM)
        for k in range(n_tiles):
            sl = slice(k * tw, (k + 1) * tw)
            du[:, sl] += _dot_nt(dpab[:, sl], wa_ref[k]) + _dot_nt(dpib[:, sl], wi_ref[k])
            ut = u[:, sl].T.astype(MM)
            gwa_ref[k] += _dot(ut, dpab[:, sl])
            gwi_ref[k] += _dot(ut, dpib[:, sl])

        def lru_conv(r, du_n):
            rows16 = pl.ds(r, 2 * s8)
            xl16 = proj(rows16, P_XL)
            d_xl = [None, None]
            for j in (1, 0):
                rows, sub = pl.ds(r + j * s8, s8), slice(j * s8, (j + 1) * s8)
                dut = du[rows, :]
                up1, up2, up3 = (_shift_up(dut, du_n, s, row) for s in (1, 2, 3))
                d_xl[j] = l3 * dut + l2 * up1 + l1 * up2 + l0 * up3
                xl = xl16[sub]
                acc_add(A_LRU_W, up3 * xl)
                acc_add(A_LRU_W + 1, up2 * xl)
                acc_add(A_LRU_W + 2, up1 * xl)
                acc_add(A_LRU_W + 3, dut * xl)
                acc_add(A_LRU_B, dut)
                du_n = dut
            put(rows16, P_XL, d_xl)
            return du_n

        car_du[...] = _chunks(tb, 2 * s8, lru_conv, car_du[...], reverse=True)

    vm = pl.BlockSpec(memory_space=pltpu.VMEM)
    rev = lambda w: pl.BlockSpec((tb, w), lambda i: (nb - 1 - i, 0))
    halo = lambda rows, w: pl.BlockSpec((rows, w), lambda i: (jnp.maximum((nb - 1 - i) * (tb // rows) - 1, 0), 0))
    const = lambda shape: pl.BlockSpec(shape, lambda i: (0,) * len(shape))
    buf = lambda w: pltpu.VMEM((tb, w), F32)
    car = pltpu.VMEM((SUBLANES, d), F32)
    return pl.pallas_call(
        body, name="backward", grid=(nb,),
        in_specs=[rev(6 * d), rev(d), halo(SUBLANES, d), rev(d)] + [rev(d)] * 5 + [vm, vm, vm, vm, vm, vm],
        out_specs=(rev(6 * d), pl.BlockSpec((2 * d, tb), lambda i: (0, nb - 1 - i)),
                   const((n_tiles, tw, tw)), const((n_tiles, tw, tw)), const((A_GROUPS * SUBLANES, d))),
        out_shape=(jax.ShapeDtypeStruct((t_len, 6 * d), MM),
                   jax.ShapeDtypeStruct((2 * d, t_len), MM),
                   jax.ShapeDtypeStruct((n_tiles, tw, tw), F32),
                   jax.ShapeDtypeStruct((n_tiles, tw, tw), F32),
                   jax.ShapeDtypeStruct((A_GROUPS * SUBLANES, d), F32)),
        scratch_shapes=[pltpu.VMEM((SUBLANES + tb, d), F32), buf(2 * d), buf(2 * d)] + [buf(d)] * 9 + [car, car, car, car],
        compiler_params=_params(dimension_semantics=("arbitrary",)),
    )(p, h, h, dh, *saved, wout, wa_t, wi_t, sp, ones_c, ones_l)


def _input_grad(dp, win_all, x, dh, sp, parts, tb):
    t_len, d = x.shape
    nb = t_len // tb
    cols = win_all.shape[2]
    n_parts = len(parts)

    def body(dp_ref, win_ref, x_ref, dh_ref, sp_ref, *refs):
        gx_ref, acc_ref = refs[n_parts:n_parts + 2]
        exchange = _ChipExchange(refs[:n_parts], refs[n_parts + 2:2 * n_parts + 2], *refs[2 * n_parts + 2:])
        i = pl.program_id(0)

        @pl.when(i == 0)
        def _():
            acc_ref[...] = jnp.zeros_like(acc_ref)
            exchange.start()

        dxn = _dot_nt(dp_ref[:, 0:cols], win_ref[0])
        for j in range(1, N_DEV):
            dxn += _dot_nt(dp_ref[:, j * cols:(j + 1) * cols], win_ref[j])
        xv = x_ref[...]
        r0 = lax.rsqrt(jnp.mean(xv * xv, axis=-1, keepdims=True) + RMS_EPS)
        xhat = xv * r0
        acc_ref[...] += (dxn * xhat).reshape(tb // SUBLANES, SUBLANES, d).sum(axis=0)
        dxh = dxn * sp_ref[SP_LN_G:SP_LN_G + 1, :]
        gx_ref[...] = dh_ref[...] + r0 * (dxh - xhat * jnp.mean(dxh * xhat, axis=-1, keepdims=True))

        @pl.when(i == nb - 1)
        def _():
            exchange.finish()

    vm = pl.BlockSpec(memory_space=pltpu.VMEM)
    hbm = pl.BlockSpec(memory_space=pl.ANY)
    blk = lambda w: pl.BlockSpec((tb, w), lambda i: (i, 0))
    outs = pl.pallas_call(
        body, name="input_grad", grid=(nb,),
        in_specs=[blk(6 * d), vm, blk(d), blk(d), vm] + [hbm] * n_parts,
        out_specs=(blk(d), pl.BlockSpec((SUBLANES, d), lambda i: (0, 0))) + (hbm,) * n_parts,
        out_shape=(jax.ShapeDtypeStruct((t_len, d), F32), jax.ShapeDtypeStruct((SUBLANES, d), F32))
                  + tuple(jax.ShapeDtypeStruct(p.shape, p.dtype) for p in parts),
        scratch_shapes=[pltpu.SemaphoreType.DMA((3 * n_parts,)), pltpu.SemaphoreType.DMA((3 * n_parts,))],
        compiler_params=_params(dimension_semantics=("arbitrary",)),
    )(dp, win_all, x, dh, sp, *parts)
    return outs[0], outs[1], outs[2:]


_CHIP_RELATIONS = [(0, 0), (1, 0), (0, 1), (1, 1)]


def _related_block(k, core):
    x, y, _ = _mesh_pos()
    fx, fy = _CHIP_RELATIONS[k]
    return 4 * (x ^ fx) + 2 * (y ^ fy) + core


class _ChipExchange:
    def __init__(self, part_refs, land_refs, send_sems, recv_sems):
        self.part_refs, self.land_refs, self.send_sems, self.recv_sems = part_refs, land_refs, send_sems, recv_sems

    def copies(self):
        x, y, c = _mesh_pos()
        for a in range(len(self.part_refs)):
            for k in (1, 2, 3):
                fx, fy = _CHIP_RELATIONS[k]
                yield pltpu.make_async_remote_copy(
                    src_ref=self.part_refs[a].at[k - 1], dst_ref=self.land_refs[a].at[k - 1],
                    send_sem=self.send_sems.at[3 * a + k - 1], recv_sem=self.recv_sems.at[3 * a + k - 1],
                    device_id=(x ^ fx, y ^ fy, c), device_id_type=MESH)

    def start(self):
        for cp in self.copies():
            cp.start()

    def finish(self):
        for cp in self.copies():
            cp.wait_recv()
        for cp in self.copies():
            cp.wait_send()


def _weight_grad_stage1(name, blk_shape, n_split, operands, in_specs, product, riders=()):
    n_rows, n_cols = blk_shape
    rs = n_rows // n_split
    rc = 32
    n_in, n_ride = len(operands), len(riders)
    _, _, c = _mesh_pos()
    order = jnp.stack([_related_block(k, 1 - c) for k in range(4)]
                      + [_related_block(k, c) for k in (1, 2, 3, 0)]).astype(jnp.int32)

    def body(order_ref, *refs):
        ins = refs[:n_in]
        ride_in = refs[n_in:n_in + n_ride]
        part_ref, own_ref = refs[n_in + n_ride:n_in + n_ride + 2]
        ride_out = refs[n_in + n_ride + 2:n_in + 2 * n_ride + 2]
        gbuf, sendbuf, from_sib, send_sems, recv_sems, ride_send, ride_recv = refs[n_in + 2 * n_ride + 2:]
        exchange = _ChipExchange(ride_in, ride_out, ride_send, ride_recv)
        s = pl.program_id(0)
        x, y, c = _mesh_pos()

        def to_sibling(k):
            return pltpu.make_async_remote_copy(
                src_ref=sendbuf.at[k], dst_ref=from_sib.at[k], send_sem=send_sems.at[k], recv_sem=recv_sems.at[k],
                device_id=(x, y, 1 - c), device_id_type=MESH)

        if n_ride:
            @pl.when(s == 0)
            def _():
                exchange.start()

        for h in range(n_split):
            gbuf[h * rs:(h + 1) * rs, :] = product(ins, h)

        @pl.when(s < 4)
        def _():
            def narrow(r, carry):
                sendbuf[s, pl.ds(r, rc), :] = gbuf[pl.ds(r, rc), :].astype(MM)
                return carry

            _chunks(n_rows, rc, narrow, 0)
            to_sibling(s).start()

        @pl.when(s >= 4)
        def _():
            k = jnp.where(s == 7, 0, s - 3)
            to_sibling(k).wait_recv()

            @pl.when(s < 7)
            def _():
                def add(r, carry):
                    rows = pl.ds(r, rc)
                    part_ref[0, rows, :] = (gbuf[rows, :] + from_sib[k, rows, :].astype(F32)).astype(MM)
                    return carry

                _chunks(n_rows, rc, add, 0)

            @pl.when(s == 7)
            def _():
                def add(r, carry):
                    rows = pl.ds(r, rc)
                    own_ref[rows, :] = gbuf[rows, :] + from_sib[0, rows, :].astype(F32)
                    return carry

                _chunks(n_rows, rc, add, 0)
                for kk in range(4):
                    to_sibling(kk).wait_send()
                if n_ride:
                    exchange.finish()

    hbm = pl.BlockSpec(memory_space=pl.ANY)
    grid_spec = pltpu.PrefetchScalarGridSpec(
        num_scalar_prefetch=1, grid=(N_DEV,), in_specs=list(in_specs) + [hbm] * n_ride,
        out_specs=(pl.BlockSpec((1, n_rows, n_cols), lambda s, o: (jnp.clip(s - 4, 0, 2), 0, 0)),
                   pl.BlockSpec((n_rows, n_cols), lambda s, o: (0, 0))) + (hbm,) * n_ride,
        scratch_shapes=[pltpu.VMEM((n_rows, n_cols), F32), pltpu.VMEM((4, n_rows, n_cols), MM),
                        pltpu.VMEM((4, n_rows, n_cols), MM),
                        pltpu.SemaphoreType.DMA((4,)), pltpu.SemaphoreType.DMA((4,)),
                        pltpu.SemaphoreType.DMA((max(3 * n_ride, 1),)), pltpu.SemaphoreType.DMA((max(3 * n_ride, 1),))])
    outs = pl.pallas_call(
        body, name=name, grid_spec=grid_spec,
        out_shape=(jax.ShapeDtypeStruct((3, n_rows, n_cols), MM), jax.ShapeDtypeStruct((n_rows, n_cols), F32))
                  + tuple(jax.ShapeDtypeStruct(p.shape, p.dtype) for p in riders),
        compiler_params=_params(dimension_semantics=("arbitrary",)),
    )(order, *operands, *riders)
    return outs[0], outs[1], outs[2:]


def _weight_grad_in(xnt, dp, riders):
    d, t_len = xnt.shape
    cols = dp.shape[1] // N_DEV
    half = d // 2
    return _weight_grad_stage1(
        "weight_grad_in", (d, cols), 2, (xnt, dp),
        [pl.BlockSpec(memory_space=pltpu.VMEM), pl.BlockSpec((t_len, cols), lambda s, o: (0, o[s]))],
        lambda refs, h: _dot(refs[0][h * half:(h + 1) * half, :], refs[1][...]), riders)


def _weight_grad_out(yt, dhb):
    d2, t_len = yt.shape
    d = dhb.shape[1]
    rows = d2 // N_DEV
    return _weight_grad_stage1(
        "weight_grad_out", (rows, d), 1, (yt, dhb),
        [pl.BlockSpec((rows, t_len), lambda s, o: (o[s], 0)), pl.BlockSpec(memory_space=pltpu.VMEM)],
        lambda refs, h: _dot(refs[0][...], refs[1][...]))


def _update_shard(own, from_chips, w, m, v, name):
    n_rows, n_cols = w.shape
    rb = min(256, n_rows)

    def body(own_ref, fc_ref, w_ref, m_ref, v_ref, grad_ref, delta_ref, mo_ref, vo_ref):
        g = own_ref[...]
        for k in range(3):
            g = g + fc_ref[k].astype(F32)
        delta, m_new, v_new = _adamw(w_ref[...], g, m_ref[...], v_ref[...])
        grad_ref[...] = g
        delta_ref[...] = delta
        mo_ref[...] = m_new
        vo_ref[...] = v_new

    blk = pl.BlockSpec((rb, n_cols), lambda i: (i, 0))
    out = jax.ShapeDtypeStruct((n_rows, n_cols), F32)
    return pl.pallas_call(
        body, name=name, grid=(n_rows // rb,),
        in_specs=[blk, pl.BlockSpec((3, rb, n_cols), lambda i: (0, i, 0)), blk, blk, blk],
        out_specs=(blk, blk, blk, blk), out_shape=(out, out, out, out),
        compiler_params=_params(dimension_semantics=("arbitrary",)),
    )(own, from_chips, w, m, v)


def _small_update(facc, xacc, bacc, gwa_t, gwi_t, vec_w, vec_m, vec_v, gates, convs):
    d = vec_w[0].shape[0]
    s8 = SUBLANES
    n_vec = len(vec_w)
    n_heads, hd, _ = gates[0].shape
    n_tiles, tw, _ = gwa_t.shape
    per = tw // hd
    cc = convs[0].shape[1]
    n_in = 5 + 3 * n_vec + 12

    def body(*refs):
        facc_ref, xacc_ref, bacc_ref, gwa_ref, gwi_ref = refs[:5]
        w_refs, m_refs, v_refs = (refs[5 + j * n_vec:5 + (j + 1) * n_vec] for j in range(3))
        gate_refs = refs[5 + 3 * n_vec:5 + 3 * n_vec + 6]
        conv_refs = refs[5 + 3 * n_vec + 6:n_in]
        outs = refs[n_in:n_in + 1 + 4 * (n_vec + 4)]
        gat_v, gat_g, tv, send_sems, recv_sems = refs[n_in + 1 + 4 * (n_vec + 4):]
        loss_o = outs[0]
        kinds = [outs[1 + j * (n_vec + 4):1 + (j + 1) * (n_vec + 4)] for j in range(4)]
        x, y, c = _mesh_pos()
        me = 4 * x + 2 * y + c

        def rowsum(ref, group):
            return jnp.sum(ref[group * s8:(group + 1) * s8, :], axis=0, keepdims=True)

        def emit(k_out, w, g, m, v):
            delta, m_new, v_new = _adamw(w, g, m, v)
            for ref, val in zip(k_out, (g, delta, m_new, v_new)):
                ref[...] = val

        mine = gat_v.at[me]
        mine[...] = jnp.zeros_like(mine)
        loss = jnp.sum(rowsum(facc_ref, 0), axis=1, keepdims=True) * (0.5 / d)
        mine[SL_LOSS:SL_LOSS + 1, :] = jnp.broadcast_to(loss, (1, d))
        rows = {SL_LN_G: rowsum(xacc_ref, 0), SL_LRU_B: rowsum(bacc_ref, A_LRU_B), SL_B_A: rowsum(bacc_ref, A_B_A),
                SL_B_I: rowsum(bacc_ref, A_B_I), SL_LAM: rowsum(bacc_ref, A_LAM),
                SL_CONV_G: rowsum(bacc_ref, A_CONV_G), SL_LRU_G: rowsum(bacc_ref, A_LRU_G), SL_FINAL_G: rowsum(facc_ref, 1)}
        for k in range(3):
            rows[SL_CONV_W + k] = rowsum(bacc_ref, A_CONV_W + k)
        for k in range(4):
            rows[SL_LRU_W + k] = rowsum(bacc_ref, A_LRU_W + k)
        for r, val in rows.items():
            mine[r:r + 1, :] = val
        head_of_lane = lax.broadcasted_iota(jnp.int32, (hd, tw), 1) // hd
        for mtx, g_ref in enumerate((gwa_ref, gwi_ref)):
            for k in range(n_tiles):
                packed = jnp.zeros((hd, tw), F32)
                for a in range(per):
                    packed = jnp.where(head_of_lane == a, g_ref[k, a * hd:(a + 1) * hd, :], packed)
                gat_g[me, (mtx * n_tiles + k) * hd:(mtx * n_tiles + k + 1) * hd, :] = packed

        gather = _Gather(lambda a, px, py, pc: (gat_v, gat_g)[a].at[4 * px + 2 * py + pc], send_sems, recv_sems)
        gather.start_own(0)
        gather.start_own(1)
        gather.finish(0)
        gather.finish(1)

        total = gat_v[0]
        for b in range(1, N_DEV):
            total = total + gat_v[b]
        tv[...] = total
        loss_o[...] = jnp.broadcast_to(tv[SL_LOSS:SL_LOSS + 1, 0:LANES], loss_o.shape)
        for p in range(n_vec):
            w, g = w_refs[p][...], tv[SL_LN_G + p, :]
            if SL_LN_G + p == SL_LAM:
                g = g * (RG_LRU_C * jax.nn.sigmoid(-w))
            emit([k_out[p] for k_out in kinds], w, g, m_refs[p][...], v_refs[p][...])
        lanes = pl.ds(pl.multiple_of(me * cc, cc), cc)
        for j, (row0, n) in enumerate(((SL_CONV_W, 3), (SL_LRU_W, 4))):
            w_ref, m_ref, v_ref = conv_refs[3 * j:3 * j + 3]
            emit([k_out[n_vec + 2 + j] for k_out in kinds], w_ref[...], tv[row0:row0 + n, lanes], m_ref[...], v_ref[...])
        for mtx in range(2):
            w_ref, m_ref, v_ref = gate_refs[3 * mtx:3 * mtx + 3]
            for k in range(n_tiles):
                r0 = (mtx * n_tiles + k) * hd
                tile = gat_g[0, r0:r0 + hd, :]
                for b in range(1, N_DEV):
                    tile = tile + gat_g[b, r0:r0 + hd, :]
                for a in range(per):
                    head = k * per + a
                    delta, m_new, v_new = _adamw(w_ref[head], tile[:, a * hd:(a + 1) * hd], m_ref[head], v_ref[head])
                    for k_out, val in zip(kinds, (tile[:, a * hd:(a + 1) * hd], delta, m_new, v_new)):
                        k_out[n_vec + mtx][head] = val

    vm = pl.BlockSpec(memory_space=pltpu.VMEM)
    like = lambda a: jax.ShapeDtypeStruct(a.shape, F32)
    per_kind = tuple(like(a) for a in vec_w) + (like(gates[0]), like(gates[3]), like(convs[0]), like(convs[3]))
    n_out = 1 + 4 * len(per_kind)
    outs = pl.pallas_call(
        body, name="small_update",
        in_specs=[vm] * n_in, out_specs=(vm,) * n_out,
        out_shape=(jax.ShapeDtypeStruct((SUBLANES, LANES), F32),) + per_kind * 4,
        scratch_shapes=[pltpu.VMEM((N_DEV, SL_ROWS, d), F32), pltpu.VMEM((N_DEV, 2 * n_tiles * hd, tw), F32),
                        pltpu.VMEM((SL_ROWS, d), F32), pltpu.SemaphoreType.DMA((14,)), pltpu.SemaphoreType.DMA((14,))],
        compiler_params=_params(),
    )(facc, xacc, bacc, gwa_t, gwi_t, *vec_w, *vec_m, *vec_v, *gates, *convs)
    return outs[0], [outs[1 + j * len(per_kind):1 + (j + 1) * len(per_kind)] for j in range(4)]


def _head_ones(head_dim, tw):
    lane = jnp.arange(tw) // head_dim
    return (lane[:, None] == lane[None, :]).astype(MM)


def _gate_tiles(w, tw):
    n_heads, hd, _ = w.shape
    per = tw // hd
    w4 = w.reshape(n_heads // per, per, hd, hd)
    eye = jnp.eye(per, dtype=w.dtype)
    return (w4[:, :, :, None, :] * eye[None, :, None, :, None]).reshape(n_heads // per, tw, tw)


def kernel(x, ln_g, w_in, conv_w, lru_conv_w, lru_conv_b, w_a, b_a, w_i, b_i, lam, conv_out_g, lru_out_g, w_out, final_g, loss_target, m_ln_g, m_w_in, m_conv_w, m_lru_conv_w, m_lru_conv_b, m_w_a, m_b_a, m_w_i, m_b_i, m_lam, m_conv_out_g, m_lru_out_g, m_w_out, m_final_g, v_ln_g, v_w_in, v_conv_w, v_lru_conv_w, v_lru_conv_b, v_w_a, v_b_a, v_w_i, v_b_i, v_lam, v_conv_out_g, v_lru_out_g, v_w_out, v_final_g):
    _, t_len, d = x.shape
    hd_l = d // N_LRU_HEADS
    tw = min(MXU_TILE, d)
    cc = conv_w.shape[1]
    x2, tgt2 = x[0], loss_target[0]

    def conv_rows(cw3, lw4):
        return jnp.concatenate([jnp.zeros((1, cc), F32), cw3, lw4], axis=0)

    p, xnt, win_all, wout_all, conv_all = _gather_project(
        x2, w_in, w_out, conv_rows(conv_w, lru_conv_w), ln_g.reshape(1, d), min(256, t_len))
    wout_full = wout_all.reshape(N_DEV * w_out.shape[0], d)
    conv_full = conv_all.transpose(1, 0, 2).reshape(SUBLANES, d)
    small = [ln_g, lru_conv_b, b_a, b_i, lam, conv_out_g, lru_out_g, final_g]
    sp = jnp.concatenate([jnp.stack(small), conv_full[1:], jnp.zeros((1, d), F32)], axis=0)
    wa_t, wi_t = _gate_tiles(w_a, tw).astype(MM), _gate_tiles(w_i, tw).astype(MM)
    ones_c, ones_l = _head_ones(d // N_CONV_HEADS, tw), _head_ones(hd_l, tw)

    h, dh, dhb, facc, *saved = _forward(x2, tgt2, p, wout_full, wa_t, wi_t, sp, ones_c, ones_l, min(256, t_len))
    dp, yt, gwa_t, gwi_t, bacc = _backward(p, h, dh, saved, wout_full, wa_t, wi_t, sp, ones_c, ones_l, min(256, t_len))
    part_out, own_out, _ = _weight_grad_out(yt, dhb)
    part_in, own_in, (chips_out,) = _weight_grad_in(xnt, dp, (part_out,))
    grad_x, xacc, (chips_in,) = _input_grad(dp, win_all, x2, dh, sp, (part_in,), min(512, t_len))
    gw_in, dw_in, mw_in, vw_in = _update_shard(own_in, chips_in, w_in, m_w_in, v_w_in, "update_w_in")
    gw_out, dw_out, mw_out, vw_out = _update_shard(own_out, chips_out, w_out, m_w_out, v_w_out, "update_w_out")

    loss_tile, kinds = _small_update(
        facc, xacc, bacc, gwa_t, gwi_t, small,
        [m_ln_g, m_lru_conv_b, m_b_a, m_b_i, m_lam, m_conv_out_g, m_lru_out_g, m_final_g],
        [v_ln_g, v_lru_conv_b, v_b_a, v_b_i, v_lam, v_conv_out_g, v_lru_out_g, v_final_g],
        (w_a, m_w_a, v_w_a, w_i, m_w_i, v_w_i), (conv_w, m_conv_w, v_conv_w, lru_conv_w, m_lru_conv_w, v_lru_conv_w))

    def unpack(kind, big_in, big_out):
        vec, (wa_, wi_, cw_, lw_) = kind[:len(small)], kind[len(small):]
        return [vec[0], big_in, cw_, lw_, vec[1], wa_, vec[2], wi_, vec[3], vec[4], vec[5], vec[6], big_out, vec[7]]

    return (loss_tile[0, 0], grad_x[None], *unpack(kinds[0], gw_in, gw_out), *unpack(kinds[1], dw_in, dw_out),
            *unpack(kinds[2], mw_in, mw_out), *unpack(kinds[3], vw_in, vw_out))
```

```python
import functools

import jax
import jax.numpy as jnp
from jax import lax
from jax.experimental import pallas as pl
from jax.experimental.pallas import tpu as pltpu

F32 = jnp.float32
MM = jnp.bfloat16
MESH = pl.DeviceIdType.MESH

N_DEV = 8
N_CONV_HEADS = 8
N_LRU_HEADS = 16
RG_LRU_C = 8.0
RMS_EPS = 1e-6
ADAM_LR, ADAM_B1, ADAM_B2, ADAM_EPS, ADAM_WD, ADAM_STEP = 0.001, 0.9, 0.999, 1e-08, 0.01, 10
ADAM_BC1 = 1.0 - ADAM_B1 ** ADAM_STEP
ADAM_BC2 = 1.0 - ADAM_B2 ** ADAM_STEP

SUBLANES = 8
LANES = 128
MXU_TILE = 256
VMEM_LIMIT = 56 * 1024 * 1024

SP_LN_G, SP_LRU_B, SP_B_A, SP_B_I, SP_LAM, SP_CONV_G, SP_LRU_G, SP_FINAL_G, SP_CONV_W, SP_LRU_W = 0, 1, 2, 3, 4, 5, 6, 7, 8, 11
SP_ROWS = 16
P_B, P_C, P_XC, P_GC, P_XL, P_GL = 0, 1, 2, 3, 4, 5
A_CONV_G, A_LRU_G, A_LAM, A_B_A, A_B_I, A_CONV_W, A_LRU_W, A_LRU_B = 0, 1, 2, 3, 4, 5, 8, 12
A_GROUPS = 13
SL_LOSS, SL_LN_G, SL_LRU_B, SL_B_A, SL_B_I, SL_LAM, SL_CONV_G, SL_LRU_G, SL_FINAL_G, SL_CONV_W, SL_LRU_W = 0, 1, 2, 3, 4, 5, 6, 7, 8, 16, 24
SL_ROWS = 32


def _params(vmem=True, **kw):
    if vmem:
        kw["vmem_limit_bytes"] = VMEM_LIMIT
    return pltpu.CompilerParams(**kw)


def _dot(a, b):
    return jnp.dot(a, b, preferred_element_type=F32)


def _dot_nt(a, b):
    return lax.dot_general(a, b, (((1,), (1,)), ((), ())), preferred_element_type=F32)


def _head_sums(v, ones_tile):
    tw = ones_tile.shape[0]
    vb = v.astype(MM)
    return jnp.concatenate([_dot(vb[:, k:k + tw], ones_tile) for k in range(0, v.shape[1], tw)], axis=1)


def _head_rstd(v, ones_tile, head_dim):
    return lax.rsqrt(_head_sums(v * v, ones_tile) * (1.0 / head_dim) + RMS_EPS)


def _sigmoid(x):
    return 0.5 * jnp.tanh(0.5 * x) + 0.5


def _lru_input_scale(log_a, a):
    return jnp.sqrt(-jnp.tanh(log_a) * (1.0 + a * a))


def _log_sigmoid(x):
    z = jnp.exp(-jnp.abs(x))
    u = 1.0 + z
    log1p_z = jnp.where(u == 1.0, z, jnp.log(u) * (z / (u - 1.0)))
    return jnp.minimum(x, 0.0) - log1p_z


def _row_iota(d):
    return lax.broadcasted_iota(jnp.int32, (SUBLANES, d), 0)


def _shift_down(cur, prev, s, row):
    return jnp.where(row >= s, pltpu.roll(cur, s, axis=0), pltpu.roll(prev, s, axis=0))


def _shift_up(cur, nxt, s, row):
    k = SUBLANES - s
    return jnp.where(row < k, pltpu.roll(cur, k, axis=0), pltpu.roll(nxt, k, axis=0))


def _scan_fwd(a, b, h_prev, row):
    for s in (1, 2, 4):
        a_s = jnp.where(row >= s, pltpu.roll(a, s, axis=0), 1.0)
        b_s = jnp.where(row >= s, pltpu.roll(b, s, axis=0), 0.0)
        b = a * b_s + b
        a = a * a_s
    return a * h_prev + b


def _scan_bwd(a_next, b, g_next, row):
    a = a_next
    for s in (1, 2, 4):
        k = SUBLANES - s
        a_s = jnp.where(row < k, pltpu.roll(a, k, axis=0), 1.0)
        b_s = jnp.where(row < k, pltpu.roll(b, k, axis=0), 0.0)
        b = a * b_s + b
        a = a * a_s
    return a * g_next + b


def _bcast_row(v, r):
    return jnp.broadcast_to(v[r:r + 1, :], v.shape)


def _chunks(n_rows, rc, body, init, reverse=False):
    n = n_rows // rc

    def step(i, carry):
        j = (n - 1 - i) if reverse else i
        return body(pl.multiple_of(j * rc, rc), carry)

    return lax.fori_loop(0, n, step, init)


def _adamw(w, g, m, v):
    m = ADAM_B1 * m + (1.0 - ADAM_B1) * g
    v = ADAM_B2 * v + (1.0 - ADAM_B2) * (g * g)
    m_hat = m / ADAM_BC1
    v_hat = v / ADAM_BC2
    delta = -ADAM_LR * (m_hat / (jnp.sqrt(v_hat) + ADAM_EPS) + ADAM_WD * w)
    return delta, m, v


def _mesh_pos():
    return lax.axis_index("x"), lax.axis_index("y"), lax.axis_index("c")


class _Gather:
    def __init__(self, blocks_of, send_sems, recv_sems, own_src=None):
        x, y, c = _mesh_pos()
        self.c = c
        self.me, self.sibling = (x, y, c), (x, y, 1 - c)
        self.chips = [(1 - x, y), (x, 1 - y), (1 - x, 1 - y)]
        self.blocks_of, self.send_sems, self.recv_sems = blocks_of, send_sems, recv_sems
        self.own_src = own_src

    def copy(self, a, k, block, to):
        src = self.blocks_of(a, *block)
        if block is self.me and self.own_src is not None:
            src = self.own_src[a]
        return pltpu.make_async_remote_copy(
            src_ref=src, dst_ref=self.blocks_of(a, *block),
            send_sem=self.send_sems.at[a * 7 + k], recv_sem=self.recv_sems.at[a * 7 + k],
            device_id=to, device_id_type=MESH)

    def start_own(self, a):
        self.copy(a, 0, self.me, self.sibling).start()
        for j, chip in enumerate(self.chips):
            self.copy(a, 1 + j, self.me, (*chip, self.c)).start()

    def wait_sibling(self, a):
        self.copy(a, 0, self.sibling, self.me).wait_recv()

    def wait_chip_and_pass_on(self, a, j):
        block = (*self.chips[j], self.c)
        self.copy(a, 1 + j, block, self.me).wait_recv()
        self.copy(a, 4 + j, block, self.sibling).start()

    def wait_passed_on(self, a, j):
        self.copy(a, 4 + j, (*self.chips[j], 1 - self.c), self.me).wait_recv()

    def wait_sends(self, a):
        self.copy(a, 0, self.me, self.sibling).wait_send()
        for j, chip in enumerate(self.chips):
            self.copy(a, 1 + j, self.me, (*chip, self.c)).wait_send()
            self.copy(a, 4 + j, (*chip, self.c), self.sibling).wait_send()

    def finish(self, a):
        for j in range(3):
            self.wait_chip_and_pass_on(a, j)
        self.wait_sibling(a)
        for j in range(3):
            self.wait_passed_on(a, j)
        self.wait_sends(a)


class _BalancedGather:
    def __init__(self, slot, send_sems, recv_sems, own_src):
        x, y, c = _mesh_pos()
        self.c = c
        self.me, self.sibling = (x, y, c), (x, y, 1 - c)
        self.chips = [(1 - x, y), (x, 1 - y), (1 - x, 1 - y)]
        self.slot, self.send_sems, self.recv_sems, self.own_src = slot, send_sems, recv_sems, own_src

    def half(self, a, block, which):
        ref = self.slot(a, *block)
        n = ref.shape[0] // 2
        return ref.at[pl.ds(which * n, n)]

    def copy(self, a, k, src, dst, to):
        return pltpu.make_async_remote_copy(
            src_ref=src, dst_ref=dst, send_sem=self.send_sems.at[a * 8 + k], recv_sem=self.recv_sems.at[a * 8 + k],
            device_id=to, device_id_type=MESH)

    def whole(self, a, k, block, to):
        src = self.own_src[a] if block is self.me else self.slot(a, *block)
        return self.copy(a, k, src, self.slot(a, *block), to)

    def halved(self, a, k, block, which, to):
        return self.copy(a, k, self.half(a, block, which), self.half(a, block, which), to)

    def on(self, chip):
        return (*self.chips[chip], self.c)

    def start_own(self, a):
        self.whole(a, 0, self.me, self.sibling).start()
        self.whole(a, 1, self.me, self.on(0)).start()
        self.whole(a, 2, self.me, self.on(1)).start()

    def wait_sibling(self, a):
        self.whole(a, 0, self.sibling, self.me).wait_recv()

    def on_neighbour(self, a, j):
        self.whole(a, 1 + j, self.on(j), self.me).wait_recv()
        self.halved(a, 3 + j, self.on(j), j, self.on(1 - j)).start()
        self.whole(a, 5 + j, self.on(j), self.sibling).start()

    def on_diagonal(self, a):
        self.halved(a, 3, self.on(2), 0, self.me).wait_recv()
        self.halved(a, 4, self.on(2), 1, self.me).wait_recv()
        self.whole(a, 7, self.on(2), self.sibling).start()

    def wait_passed_on(self, a, j):
        self.whole(a, 5 + j, (*self.chips[j], 1 - self.c), self.me).wait_recv()

    def wait_sends(self, a):
        self.whole(a, 0, self.me, self.sibling).wait_send()
        for j in range(2):
            self.whole(a, 1 + j, self.me, self.on(j)).wait_send()
            self.halved(a, 3 + j, self.on(j), j, self.on(1 - j)).wait_send()
        for j in range(3):
            self.whole(a, 5 + j, self.on(j), self.sibling).wait_send()


def _block_order():
    x, y, c = _mesh_pos()
    chips = [(x, y), (1 - x, y), (x, 1 - y), (1 - x, 1 - y)]
    return jnp.stack([4 * px + 2 * py + pc for px, py in chips for pc in (c, 1 - c)]).astype(jnp.int32)


def _gather_project(x, w_in, w_out, conv_pack, ln_g, tb):
    t_len, d = x.shape
    nb = t_len // tb
    cols = w_in.shape[1]
    mc = min(512, t_len)
    srcs = (w_in, w_out, conv_pack)
    dts = (MM, MM, F32)

    def body(order_ref, x_ref, win_ref, wout_ref, cp_ref, lng_ref, p_ref, xnt_ref, win_all, wout_all, cp_all,
             xnb, wall, st_out, st_cp, send_sems, recv_sems, cp_send, cp_recv, local_sems):
        i = pl.program_id(0)
        x_, y_, c_ = _mesh_pos()
        me = 4 * x_ + 2 * y_ + c_
        outs = (win_all, wout_all, cp_all)
        lands = (wall, wout_all, cp_all)
        stages = (wall.at[me], st_out, st_cp)
        gather = _BalancedGather(lambda a, px, py, pc: lands[a].at[4 * px + 2 * py + pc], send_sems, recv_sems, stages)
        small = _Gather(lambda a, px, py, pc: cp_all.at[4 * px + 2 * py + pc], cp_send, cp_recv, own_src=[st_cp])
        keep_own = [pltpu.make_async_copy(stages[a], outs[a].at[me], local_sems.at[a]) for a in range(3)]

        def keep(k):
            blk = order_ref[k]
            return pltpu.make_async_copy(wall.at[blk], win_all.at[blk], local_sems.at[2 + k])

        @pl.when(i == 0)
        def _():
            for a, (src, dst) in enumerate(zip((win_ref, wout_ref, cp_ref), stages)):
                rows = src.shape[0]
                rc = min(rows, 32)

                def cast(r, carry, src=src, dst=dst, rc=rc):
                    dst[pl.ds(r, rc), :] = src[pl.ds(r, rc), :].astype(dst.dtype)
                    return carry

                _chunks(rows, rc, cast, 0)
                if a < 2:
                    gather.start_own(a)
                else:
                    small.start_own(0)
                keep_own[a].start()

        @pl.when(i < nb)
        def _():
            xv = x_ref[...]
            r0 = lax.rsqrt(jnp.mean(xv * xv, axis=-1, keepdims=True) + RMS_EPS)
            xn = xv * r0 * lng_ref[...]
            xnb[pl.ds(pl.multiple_of(i * tb, tb), tb), :] = xn.astype(MM)
            xnt_ref[...] = xn.T.astype(MM)

        for k in range(N_DEV):
            @pl.when(i == nb + k)
            def _(k=k):
                if k == 1:
                    gather.wait_sibling(0)
                elif k == 2:
                    gather.on_neighbour(0, 0)
                    gather.on_neighbour(0, 1)
                elif k in (3, 5, 7):
                    gather.wait_passed_on(0, (k - 3) // 2)
                    if k == 3:
                        gather.on_neighbour(1, 0)
                        gather.on_neighbour(1, 1)
                    if k == 7:
                        gather.on_diagonal(1)
                elif k == 6:
                    gather.on_diagonal(0)
                blk = order_ref[k]
                if k:
                    keep(k).start()

                def project(r, carry):
                    rows = pl.ds(r, mc)
                    p_ref[rows, :] = _dot(xnb[rows, :], wall[blk]).astype(MM)
                    return carry

                _chunks(t_len, mc, project, 0)
                if k == N_DEV - 1:
                    gather.wait_sends(0)
                    gather.wait_sibling(1)
                    for j in range(3):
                        gather.wait_passed_on(1, j)
                    gather.wait_sends(1)
                    small.finish(0)
                    for cp in keep_own + [keep(kk) for kk in range(1, N_DEV)]:
                        cp.wait()

    vm = pl.BlockSpec(memory_space=pltpu.VMEM)
    hbm = pl.BlockSpec(memory_space=pl.ANY)
    grid_spec = pltpu.PrefetchScalarGridSpec(
        num_scalar_prefetch=1, grid=(nb + N_DEV,),
        in_specs=[pl.BlockSpec((tb, d), lambda i, o: (jnp.minimum(i, nb - 1), 0)), vm, vm, vm, vm],
        out_specs=(pl.BlockSpec((t_len, cols), lambda i, o: (0, o[jnp.maximum(i - nb, 0)])),
                   pl.BlockSpec((d, tb), lambda i, o: (0, jnp.minimum(i, nb - 1))), hbm, hbm, hbm),
        scratch_shapes=[pltpu.VMEM((t_len, d), MM), pltpu.VMEM((N_DEV,) + w_in.shape, MM),
                        pltpu.VMEM(w_out.shape, MM), pltpu.VMEM(conv_pack.shape, F32),
                        pltpu.SemaphoreType.DMA((16,)), pltpu.SemaphoreType.DMA((16,)),
                        pltpu.SemaphoreType.DMA((7,)), pltpu.SemaphoreType.DMA((7,)), pltpu.SemaphoreType.DMA((10,))])
    return pl.pallas_call(
        body, name="gather_project", grid_spec=grid_spec,
        out_shape=(jax.ShapeDtypeStruct((t_len, N_DEV * cols), MM),
                   jax.ShapeDtypeStruct((d, t_len), MM))
                  + tuple(jax.ShapeDtypeStruct((N_DEV,) + s.shape, dt) for s, dt in zip(srcs, dts)),
        compiler_params=_params(dimension_semantics=("arbitrary",)),
    )(_block_order(), x, w_in, w_out, conv_pack, ln_g)


def _forward(x, tgt, p, wout, wa_t, wi_t, sp, ones_c, ones_l, tb):
    t_len, d = x.shape
    nb = t_len // tb
    n_tiles, tw = wa_t.shape[0], wa_t.shape[1]
    hd_c, hd_l = d // N_CONV_HEADS, d // N_LRU_HEADS
    s8 = SUBLANES

    def body(x_ref, tgt_ref, p_ref, wout_ref, wa_ref, wi_ref, sp_ref, oc_ref, ol_ref,
             h_ref, dh_ref, dhb_ref, acc_ref, yc, czs, u, pa, pi,
             rcf, rlf, ybuf, tail_z, tail_xl, hcar):
        i = pl.program_id(0)
        row = _row_iota(d)

        @pl.when(i == 0)
        def _():
            tail_z[...] = jnp.zeros_like(tail_z)
            tail_xl[...] = jnp.zeros_like(tail_xl)
            hcar[...] = jnp.zeros_like(hcar)
            acc_ref[...] = jnp.zeros_like(acc_ref)

        def spr(r):
            return sp_ref[r:r + 1, :]

        def proj(rows, seg):
            return p_ref[rows, seg * d:(seg + 1) * d].astype(F32)

        w0, w1, w2 = spr(SP_CONV_W), spr(SP_CONV_W + 1), spr(SP_CONV_W + 2)
        l0, l1, l2, l3 = spr(SP_LRU_W), spr(SP_LRU_W + 1), spr(SP_LRU_W + 2), spr(SP_LRU_W + 3)
        lb = spr(SP_LRU_B)

        def convs(r, carry):
            zp, xp = carry
            rows16 = pl.ds(r, 2 * s8)
            bg16, xl16 = proj(rows16, P_B), proj(rows16, P_XL)
            z16 = proj(rows16, P_C) * proj(rows16, P_XC)
            for j in range(2):
                rows, sub = pl.ds(r + j * s8, s8), slice(j * s8, (j + 1) * s8)
                z, xl = z16[sub], xl16[sub]
                cz = w0 * _shift_down(z, zp, 2, row) + w1 * _shift_down(z, zp, 1, row) + w2 * z
                czs[rows, :] = cz
                yc[rows, :] = bg16[sub] * cz
                u[rows, :] = (l0 * _shift_down(xl, xp, 3, row) + l1 * _shift_down(xl, xp, 2, row)
                              + l2 * _shift_down(xl, xp, 1, row) + l3 * xl + lb)
                zp, xp = z, xl
            return zp, xp

        z_last, xl_last = _chunks(tb, 2 * s8, convs, (tail_z[...], tail_xl[...]))
        tail_z[...] = z_last
        tail_xl[...] = xl_last

        ub = u[...].astype(MM)
        for k in range(n_tiles):
            sl = slice(k * tw, (k + 1) * tw)
            pa[:, sl] = _dot(ub[:, sl], wa_ref[k])
            pi[:, sl] = _dot(ub[:, sl], wi_ref[k])
        rcf[...] = _head_rstd(yc[...], oc_ref[...], hd_c)

        c8 = RG_LRU_C * _log_sigmoid(spr(SP_LAM))
        b_a, b_i = spr(SP_B_A), spr(SP_B_I)

        def lru(r, hp):
            rows = pl.ds(r, SUBLANES)
            ra = _sigmoid(pa[rows, :] + b_a)
            ii = _sigmoid(pi[rows, :] + b_i)
            pa[rows, :] = ra
            pi[rows, :] = ii
            la = ra * c8
            a = jnp.exp(la)
            mult = _lru_input_scale(la, a)
            h = _scan_fwd(a, mult * (ii * u[rows, :]), hp, row)
            h_ref[rows, :] = h
            return _bcast_row(h, SUBLANES - 1)

        hcar[...] = _chunks(tb, SUBLANES, lru, hcar[...])
        rlf[...] = _head_rstd(h_ref[...], ol_ref[...], hd_l)

        g_c, g_l = spr(SP_CONV_G), spr(SP_LRU_G)

        def gate(r, carry):
            rows = pl.ds(r, 2 * s8)
            gc, gl = proj(rows, P_GC), proj(rows, P_GL)
            ybuf[rows, 0:d] = (yc[rows, :] * rcf[rows, :] * g_c * (gc * _sigmoid(gc))).astype(MM)
            ybuf[rows, d:2 * d] = (h_ref[rows, :] * rlf[rows, :] * g_l * (gl * _sigmoid(gl))).astype(MM)
            return carry

        _chunks(tb, 2 * s8, gate, 0)

        hres = x_ref[...] + _dot(ybuf[...], wout_ref[...])
        rf = lax.rsqrt(jnp.mean(hres * hres, axis=-1, keepdims=True) + RMS_EPS)
        hn = hres * rf
        fg = spr(SP_FINAL_G)
        err = hn * fg - tgt_ref[...]
        dout = err * (1.0 / d)
        acc_ref[0:SUBLANES, :] += (err * err).reshape(tb // SUBLANES, SUBLANES, d).sum(axis=0)
        acc_ref[SUBLANES:2 * SUBLANES, :] += (dout * hn).reshape(tb // SUBLANES, SUBLANES, d).sum(axis=0)
        gd = dout * fg
        dhres = rf * (gd - hn * jnp.mean(gd * hn, axis=-1, keepdims=True))
        dh_ref[...] = dhres
        dhb_ref[...] = dhres.astype(MM)

    vm = pl.BlockSpec(memory_space=pltpu.VMEM)
    blk = lambda w: pl.BlockSpec((tb, w), lambda i: (i, 0))
    buf = pltpu.VMEM((tb, d), F32)
    car = pltpu.VMEM((SUBLANES, d), F32)
    return pl.pallas_call(
        body, name="forward", grid=(nb,),
        in_specs=[blk(d), blk(d), blk(6 * d), vm, vm, vm, vm, vm, vm],
        out_specs=(blk(d), blk(d), blk(d), pl.BlockSpec((2 * SUBLANES, d), lambda i: (0, 0))) + (blk(d),) * 5,
        out_shape=(jax.ShapeDtypeStruct((t_len, d), F32),
                   jax.ShapeDtypeStruct((t_len, d), F32),
                   jax.ShapeDtypeStruct((t_len, d), MM),
                   jax.ShapeDtypeStruct((2 * SUBLANES, d), F32))
                  + (jax.ShapeDtypeStruct((t_len, d), F32),) * 5,
        scratch_shapes=[buf] * 2 + [pltpu.VMEM((tb, 2 * d), MM), car, car, car],
        compiler_params=_params(dimension_semantics=("arbitrary",)),
    )(x, tgt, p, wout, wa_t, wi_t, sp, ones_c, ones_l)


def _backward(p, h, dh, saved, wout, wa_t, wi_t, sp, ones_c, ones_l, tb):
    t_len, d = h.shape
    nb = t_len // tb
    n_tiles, tw = wa_t.shape[0], wa_t.shape[1]
    hd_c, hd_l = d // N_CONV_HEADS, d // N_LRU_HEADS
    s8 = SUBLANES

    def body(p_ref, h_ref, hhalo_ref, dh_ref, yc, czs, u, ra_ref, ii_ref,
             wout_ref, wa_ref, wi_ref, sp_ref, oc_ref, ol_ref,
             dp_ref, yt_ref, gwa_ref, gwi_ref, acc_ref,
             hh, dy, ybuf, rcf, rlf, qc, ql, dyc_hat, dyl_hat, dpa, dpi, du,
             car_dcz, car_a, car_g, car_du):
        i = pl.program_id(0)
        blk_idx = nb - 1 - i
        row = _row_iota(d)

        @pl.when(i == 0)
        def _():
            for ref in (car_dcz, car_a, car_g, car_du, gwa_ref, gwi_ref, acc_ref):
                ref[...] = jnp.zeros_like(ref)

        def spr(r):
            return sp_ref[r:r + 1, :]

        def proj(rows, seg):
            return p_ref[rows, seg * d:(seg + 1) * d].astype(F32)

        def put(rows, seg, halves):
            dp_ref[rows, seg * d:(seg + 1) * d] = jnp.concatenate(halves, axis=0).astype(MM)

        def acc_add(group, val):
            acc_ref[group * s8:(group + 1) * s8, :] += val

        live = jnp.where(blk_idx > 0, 1.0, 0.0).astype(F32)
        hh[0:s8, :] = hhalo_ref[...] * live
        hh[s8:, :] = h_ref[...]

        dy[...] = _dot_nt(dh_ref[...].astype(MM), wout_ref[...])

        w0, w1, w2 = spr(SP_CONV_W), spr(SP_CONV_W + 1), spr(SP_CONV_W + 2)
        l0, l1, l2, l3 = spr(SP_LRU_W), spr(SP_LRU_W + 1), spr(SP_LRU_W + 2), spr(SP_LRU_W + 3)

        rcf[...] = _head_rstd(yc[...], oc_ref[...], hd_c)
        rlf[...] = _head_rstd(h_ref[...], ol_ref[...], hd_l)

        g_c, g_l = spr(SP_CONV_G), spr(SP_LRU_G)

        def gates(r, carry):
            rows = pl.ds(r, 2 * s8)
            for (seg, off_y, src, rstd, gain, q, dhat, grp) in (
                    (P_GC, 0, yc, rcf, g_c, qc, dyc_hat, A_CONV_G),
                    (P_GL, d, h_ref, rlf, g_l, ql, dyl_hat, A_LRU_G)):
                gt = proj(rows, seg)
                sg = _sigmoid(gt)
                silu = gt * sg
                yhat = src[rows, :] * rstd[rows, :]
                nrm = yhat * gain
                ybuf[rows, off_y:off_y + d] = nrm * silu
                dout = dy[rows, off_y:off_y + d]
                dnrm = dout * silu
                dp_ref[rows, seg * d:(seg + 1) * d] = (dout * nrm * (sg * (1.0 + gt * (1.0 - sg)))).astype(MM)
                dg = dnrm * yhat
                acc_add(grp, dg[0:s8] + dg[s8:])
                dh_ = dnrm * gain
                dhat[rows, :] = dh_
                q[rows, :] = dh_ * yhat
            return carry

        _chunks(tb, 2 * s8, gates, 0)

        qc[...] = _head_sums(qc[...], oc_ref[...]) * (1.0 / hd_c)
        ql[...] = _head_sums(ql[...], ol_ref[...]) * (1.0 / hd_l)
        yt_ref[...] = ybuf[...].T.astype(MM)

        c8 = RG_LRU_C * _log_sigmoid(spr(SP_LAM))

        def conv_mixer(r, dcz_n):
            rows16 = pl.ds(r, 2 * s8)
            bg16, cg16, xc16 = proj(rows16, P_B), proj(rows16, P_C), proj(rows16, P_XC)
            z16 = cg16 * xc16
            d_b, d_c, d_x = [None, None], [None, None], [None, None]
            for j in (1, 0):
                rows, sub = pl.ds(r + j * s8, s8), slice(j * s8, (j + 1) * s8)
                rstd = rcf[rows, :]
                yhat = yc[rows, :] * rstd
                dyc = rstd * (dyc_hat[rows, :] - yhat * qc[rows, :])
                d_b[j] = dyc * czs[rows, :]
                dcz = dyc * bg16[sub]
                up1, up2 = _shift_up(dcz, dcz_n, 1, row), _shift_up(dcz, dcz_n, 2, row)
                dz = w2 * dcz + w1 * up1 + w0 * up2
                d_c[j] = dz * xc16[sub]
                d_x[j] = dz * cg16[sub]
                z = z16[sub]
                acc_add(A_CONV_W, up2 * z)
                acc_add(A_CONV_W + 1, up1 * z)
                acc_add(A_CONV_W + 2, dcz * z)
                dcz_n = dcz
            put(rows16, P_B, d_b)
            put(rows16, P_C, d_c)
            put(rows16, P_XC, d_x)
            return dcz_n

        car_dcz[...] = _chunks(tb, 2 * s8, conv_mixer, car_dcz[...], reverse=True)

        def lru_mixer(r, carry):
            a_n, g_n = carry
            for j in (1, 0):
                rows = pl.ds(r + j * s8, s8)
                rstd = rlf[rows, :]
                hcur = hh[pl.ds(r + (j + 1) * s8, s8), :]
                hhat = hcur * rstd
                dh_out = rstd * (dyl_hat[rows, :] - hhat * ql[rows, :])
                ra = ra_ref[rows, :]
                la = ra * c8
                a = jnp.exp(la)
                g = _scan_bwd(_shift_up(a, a_n, 1, row), dh_out, g_n, row)
                da = g * _shift_down(hcur, hh[pl.ds(r + j * s8, s8), :], 1, row)
                ii = ii_ref[rows, :]
                uu = u[rows, :]
                mult = _lru_input_scale(la, a)
                dmult = g * (ii * uu)
                ds = g * mult
                dla = a * (da - dmult * a / mult)
                acc_add(A_LAM, dla * ra)
                dpa_ = dla * c8 * ra * (1.0 - ra)
                dpi_ = ds * uu * ii * (1.0 - ii)
                acc_add(A_B_A, dpa_)
                acc_add(A_B_I, dpi_)
                dpa[rows, :] = dpa_
                dpi[rows, :] = dpi_
                du[rows, :] = ds * ii
                a_n, g_n = a, _bcast_row(g, 0)
            return a_n, g_n

        a_f, g_f = _chunks(tb, 2 * s8, lru_mixer, (car_a[...], car_g[...]), reverse=True)
        car_a[...] = a_f
        car_g[...] = g_f

        dpab = dpa[...].astype(MM)
        dpib = dpi[...].astype(MM)
        for k in range(n_tiles):
            sl = slice(k * tw, (k + 1) * tw)
            du[:, sl] += _dot_nt(dpab[:, sl], wa_ref[k]) + _dot_nt(dpib[:, sl], wi_ref[k])
            ut = u[:, sl].T.astype(MM)
            gwa_ref[k] += _dot(ut, dpab[:, sl])
            gwi_ref[k] += _dot(ut, dpib[:, sl])

        def lru_conv(r, du_n):
            rows16 = pl.ds(r, 2 * s8)
            xl16 = proj(rows16, P_XL)
            d_xl = [None, None]
            for j in (1, 0):
                rows, sub = pl.ds(r + j * s8, s8), slice(j * s8, (j + 1) * s8)
                dut = du[rows, :]
                up1, up2, up3 = (_shift_up(dut, du_n, s, row) for s in (1, 2, 3))
                d_xl[j] = l3 * dut + l2 * up1 + l1 * up2 + l0 * up3
                xl = xl16[sub]
                acc_add(A_LRU_W, up3 * xl)
                acc_add(A_LRU_W + 1, up2 * xl)
                acc_add(A_LRU_W + 2, up1 * xl)
                acc_add(A_LRU_W + 3, dut * xl)
                acc_add(A_LRU_B, dut)
                du_n = dut
            put(rows16, P_XL, d_xl)
            return du_n

        car_du[...] = _chunks(tb, 2 * s8, lru_conv, car_du[...], reverse=True)

    vm = pl.BlockSpec(memory_space=pltpu.VMEM)
    rev = lambda w: pl.BlockSpec((tb, w), lambda i: (nb - 1 - i, 0))
    halo = lambda rows, w: pl.BlockSpec((rows, w), lambda i: (jnp.maximum((nb - 1 - i) * (tb // rows) - 1, 0), 0))
    const = lambda shape: pl.BlockSpec(shape, lambda i: (0,) * len(shape))
    buf = lambda w: pltpu.VMEM((tb, w), F32)
    car = pltpu.VMEM((SUBLANES, d), F32)
    return pl.pallas_call(
        body, name="backward", grid=(nb,),
        in_specs=[rev(6 * d), rev(d), halo(SUBLANES, d), rev(d)] + [rev(d)] * 5 + [vm, vm, vm, vm, vm, vm],
        out_specs=(rev(6 * d), pl.BlockSpec((2 * d, tb), lambda i: (0, nb - 1 - i)),
                   const((n_tiles, tw, tw)), const((n_tiles, tw, tw)), const((A_GROUPS * SUBLANES, d))),
        out_shape=(jax.ShapeDtypeStruct((t_len, 6 * d), MM),
                   jax.ShapeDtypeStruct((2 * d, t_len), MM),
                   jax.ShapeDtypeStruct((n_tiles, tw, tw), F32),
                   jax.ShapeDtypeStruct((n_tiles, tw, tw), F32),
                   jax.ShapeDtypeStruct((A_GROUPS * SUBLANES, d), F32)),
        scratch_shapes=[pltpu.VMEM((SUBLANES + tb, d), F32), buf(2 * d), buf(2 * d)] + [buf(d)] * 9 + [car, car, car, car],
        compiler_params=_params(dimension_semantics=("arbitrary",)),
    )(p, h, h, dh, *saved, wout, wa_t, wi_t, sp, ones_c, ones_l)


def _input_grad(dp, win_all, x, dh, sp, parts, tb):
    t_len, d = x.shape
    nb = t_len // tb
    cols = win_all.shape[2]
    n_parts = len(parts)

    def body(dp_ref, win_ref, x_ref, dh_ref, sp_ref, *refs):
        gx_ref, acc_ref = refs[n_parts:n_parts + 2]
        exchange = _ChipExchange(refs[:n_parts], refs[n_parts + 2:2 * n_parts + 2], *refs[2 * n_parts + 2:])
        i = pl.program_id(0)

        @pl.when(i == 0)
        def _():
            acc_ref[...] = jnp.zeros_like(acc_ref)
            exchange.start()

        dxn = _dot_nt(dp_ref[:, 0:cols], win_ref[0])
        for j in range(1, N_DEV):
            dxn += _dot_nt(dp_ref[:, j * cols:(j + 1) * cols], win_ref[j])
        xv = x_ref[...]
        r0 = lax.rsqrt(jnp.mean(xv * xv, axis=-1, keepdims=True) + RMS_EPS)
        xhat = xv * r0
        acc_ref[...] += (dxn * xhat).reshape(tb // SUBLANES, SUBLANES, d).sum(axis=0)
        dxh = dxn * sp_ref[SP_LN_G:SP_LN_G + 1, :]
        gx_ref[...] = dh_ref[...] + r0 * (dxh - xhat * jnp.mean(dxh * xhat, axis=-1, keepdims=True))

        @pl.when(i == nb - 1)
        def _():
            exchange.finish()

    vm = pl.BlockSpec(memory_space=pltpu.VMEM)
    hbm = pl.BlockSpec(memory_space=pl.ANY)
    blk = lambda w: pl.BlockSpec((tb, w), lambda i: (i, 0))
    outs = pl.pallas_call(
        body, name="input_grad", grid=(nb,),
        in_specs=[blk(6 * d), vm, blk(d), blk(d), vm] + [hbm] * n_parts,
        out_specs=(blk(d), pl.BlockSpec((SUBLANES, d), lambda i: (0, 0))) + (hbm,) * n_parts,
        out_shape=(jax.ShapeDtypeStruct((t_len, d), F32), jax.ShapeDtypeStruct((SUBLANES, d), F32))
                  + tuple(jax.ShapeDtypeStruct(p.shape, p.dtype) for p in parts),
        scratch_shapes=[pltpu.SemaphoreType.DMA((3 * n_parts,)), pltpu.SemaphoreType.DMA((3 * n_parts,))],
        compiler_params=_params(dimension_semantics=("arbitrary",)),
    )(dp, win_all, x, dh, sp, *parts)
    return outs[0], outs[1], outs[2:]


_CHIP_RELATIONS = [(0, 0), (1, 0), (0, 1), (1, 1)]


def _related_block(k, core):
    x, y, _ = _mesh_pos()
    fx, fy = _CHIP_RELATIONS[k]
    return 4 * (x ^ fx) + 2 * (y ^ fy) + core


class _ChipExchange:
    def __init__(self, part_refs, land_refs, send_sems, recv_sems):
        self.part_refs, self.land_refs, self.send_sems, self.recv_sems = part_refs, land_refs, send_sems, recv_sems

    def copies(self):
        x, y, c = _mesh_pos()
        for a in range(len(self.part_refs)):
            for k in (1, 2, 3):
                fx, fy = _CHIP_RELATIONS[k]
                yield pltpu.make_async_remote_copy(
                    src_ref=self.part_refs[a].at[k - 1], dst_ref=self.land_refs[a].at[k - 1],
                    send_sem=self.send_sems.at[3 * a + k - 1], recv_sem=self.recv_sems.at[3 * a + k - 1],
                    device_id=(x ^ fx, y ^ fy, c), device_id_type=MESH)

    def start(self):
        for cp in self.copies():
            cp.start()

    def finish(self):
        for cp in self.copies():
            cp.wait_recv()
        for cp in self.copies():
            cp.wait_send()


def _weight_grad_stage1(name, blk_shape, n_split, operands, in_specs, product, riders=()):
    n_rows, n_cols = blk_shape
    rs = n_rows // n_split
    rc = 32
    n_in, n_ride = len(operands), len(riders)
    _, _, c = _mesh_pos()
    order = jnp.stack([_related_block(k, 1 - c) for k in range(4)]
                      + [_related_block(k, c) for k in (1, 2, 3, 0)]).astype(jnp.int32)

    def body(order_ref, *refs):
        ins = refs[:n_in]
        ride_in = refs[n_in:n_in + n_ride]
        part_ref, own_ref = refs[n_in + n_ride:n_in + n_ride + 2]
        ride_out = refs[n_in + n_ride + 2:n_in + 2 * n_ride + 2]
        gbuf, sendbuf, from_sib, send_sems, recv_sems, ride_send, ride_recv = refs[n_in + 2 * n_ride + 2:]
        exchange = _ChipExchange(ride_in, ride_out, ride_send, ride_recv)
        s = pl.program_id(0)
        x, y, c = _mesh_pos()

        def to_sibling(k):
            return pltpu.make_async_remote_copy(
                src_ref=sendbuf.at[k], dst_ref=from_sib.at[k], send_sem=send_sems.at[k], recv_sem=recv_sems.at[k],
                device_id=(x, y, 1 - c), device_id_type=MESH)

        if n_ride:
            @pl.when(s == 0)
            def _():
                exchange.start()

        for h in range(n_split):
            gbuf[h * rs:(h + 1) * rs, :] = product(ins, h)

        @pl.when(s < 4)
        def _():
            def narrow(r, carry):
                sendbuf[s, pl.ds(r, rc), :] = gbuf[pl.ds(r, rc), :].astype(MM)
                return carry

            _chunks(n_rows, rc, narrow, 0)
            to_sibling(s).start()

        @pl.when(s >= 4)
        def _():
            k = jnp.where(s == 7, 0, s - 3)
            to_sibling(k).wait_recv()

            @pl.when(s < 7)
            def _():
                def add(r, carry):
                    rows = pl.ds(r, rc)
                    part_ref[0, rows, :] = (gbuf[rows, :] + from_sib[k, rows, :].astype(F32)).astype(MM)
                    return carry

                _chunks(n_rows, rc, add, 0)

            @pl.when(s == 7)
            def _():
                def add(r, carry):
                    rows = pl.ds(r, rc)
                    own_ref[rows, :] = gbuf[rows, :] + from_sib[0, rows, :].astype(F32)
                    return carry

                _chunks(n_rows, rc, add, 0)
                for kk in range(4):
                    to_sibling(kk).wait_send()
                if n_ride:
                    exchange.finish()

    hbm = pl.BlockSpec(memory_space=pl.ANY)
    grid_spec = pltpu.PrefetchScalarGridSpec(
        num_scalar_prefetch=1, grid=(N_DEV,), in_specs=list(in_specs) + [hbm] * n_ride,
        out_specs=(pl.BlockSpec((1, n_rows, n_cols), lambda s, o: (jnp.clip(s - 4, 0, 2), 0, 0)),
                   pl.BlockSpec((n_rows, n_cols), lambda s, o: (0, 0))) + (hbm,) * n_ride,
        scratch_shapes=[pltpu.VMEM((n_rows, n_cols), F32), pltpu.VMEM((4, n_rows, n_cols), MM),
                        pltpu.VMEM((4, n_rows, n_cols), MM),
                        pltpu.SemaphoreType.DMA((4,)), pltpu.SemaphoreType.DMA((4,)),
                        pltpu.SemaphoreType.DMA((max(3 * n_ride, 1),)), pltpu.SemaphoreType.DMA((max(3 * n_ride, 1),))])
    outs = pl.pallas_call(
        body, name=name, grid_spec=grid_spec,
        out_shape=(jax.ShapeDtypeStruct((3, n_rows, n_cols), MM), jax.ShapeDtypeStruct((n_rows, n_cols), F32))
                  + tuple(jax.ShapeDtypeStruct(p.shape, p.dtype) for p in riders),
        compiler_params=_params(dimension_semantics=("arbitrary",)),
    )(order, *operands, *riders)
    return outs[0], outs[1], outs[2:]


def _weight_grad_in(xnt, dp, riders):
    d, t_len = xnt.shape
    cols = dp.shape[1] // N_DEV
    half = d // 2
    return _weight_grad_stage1(
        "weight_grad_in", (d, cols), 2, (xnt, dp),
        [pl.BlockSpec(memory_space=pltpu.VMEM), pl.BlockSpec((t_len, cols), lambda s, o: (0, o[s]))],
        lambda refs, h: _dot(refs[0][h * half:(h + 1) * half, :], refs[1][...]), riders)


def _weight_grad_out(yt, dhb):
    d2, t_len = yt.shape
    d = dhb.shape[1]
    rows = d2 // N_DEV
    return _weight_grad_stage1(
        "weight_grad_out", (rows, d), 1, (yt, dhb),
        [pl.BlockSpec((rows, t_len), lambda s, o: (o[s], 0)), pl.BlockSpec(memory_space=pltpu.VMEM)],
        lambda refs, h: _dot(refs[0][...], refs[1][...]))


def _update_shard(own, from_chips, w, m, v, name):
    n_rows, n_cols = w.shape
    rb = min(256, n_rows)

    def body(own_ref, fc_ref, w_ref, m_ref, v_ref, grad_ref, delta_ref, mo_ref, vo_ref):
        g = own_ref[...]
        for k in range(3):
            g = g + fc_ref[k].astype(F32)
        delta, m_new, v_new = _adamw(w_ref[...], g, m_ref[...], v_ref[...])
        grad_ref[...] = g
        delta_ref[...] = delta
        mo_ref[...] = m_new
        vo_ref[...] = v_new

    blk = pl.BlockSpec((rb, n_cols), lambda i: (i, 0))
    out = jax.ShapeDtypeStruct((n_rows, n_cols), F32)
    return pl.pallas_call(
        body, name=name, grid=(n_rows // rb,),
        in_specs=[blk, pl.BlockSpec((3, rb, n_cols), lambda i: (0, i, 0)), blk, blk, blk],
        out_specs=(blk, blk, blk, blk), out_shape=(out, out, out, out),
        compiler_params=_params(dimension_semantics=("arbitrary",)),
    )(own, from_chips, w, m, v)


def _small_reduce(facc, xacc, bacc, gwa_t, gwi_t, hd):
    d = facc.shape[1]
    s8 = SUBLANES
    n_tiles, tw, _ = gwa_t.shape
    per = tw // hd

    def body(facc_ref, xacc_ref, bacc_ref, gwa_ref, gwi_ref, tv, tg, gat_v, gat_g, send_sems, recv_sems):
        x, y, c = _mesh_pos()
        me = 4 * x + 2 * y + c

        def rowsum(ref, group):
            return jnp.sum(ref[group * s8:(group + 1) * s8, :], axis=0, keepdims=True)

        mine = gat_v.at[me]
        mine[...] = jnp.zeros_like(mine)
        loss = jnp.sum(rowsum(facc_ref, 0), axis=1, keepdims=True) * (0.5 / d)
        mine[SL_LOSS:SL_LOSS + 1, :] = jnp.broadcast_to(loss, (1, d))
        rows = {SL_LN_G: rowsum(xacc_ref, 0), SL_LRU_B: rowsum(bacc_ref, A_LRU_B), SL_B_A: rowsum(bacc_ref, A_B_A),
                SL_B_I: rowsum(bacc_ref, A_B_I), SL_LAM: rowsum(bacc_ref, A_LAM),
                SL_CONV_G: rowsum(bacc_ref, A_CONV_G), SL_LRU_G: rowsum(bacc_ref, A_LRU_G), SL_FINAL_G: rowsum(facc_ref, 1)}
        for k in range(3):
            rows[SL_CONV_W + k] = rowsum(bacc_ref, A_CONV_W + k)
        for k in range(4):
            rows[SL_LRU_W + k] = rowsum(bacc_ref, A_LRU_W + k)
        for r, val in rows.items():
            mine[r:r + 1, :] = val
        head_of_lane = lax.broadcasted_iota(jnp.int32, (hd, tw), 1) // hd
        for mtx, g_ref in enumerate((gwa_ref, gwi_ref)):
            for k in range(n_tiles):
                packed = jnp.zeros((hd, tw), F32)
                for a in range(per):
                    packed = jnp.where(head_of_lane == a, g_ref[k, a * hd:(a + 1) * hd, :], packed)
                gat_g[me, (mtx * n_tiles + k) * hd:(mtx * n_tiles + k + 1) * hd, :] = packed

        gather = _Gather(lambda a, px, py, pc: (gat_v, gat_g)[a].at[4 * px + 2 * py + pc], send_sems, recv_sems)
        gather.start_own(0)
        gather.start_own(1)
        gather.finish(0)
        gather.finish(1)

        total = gat_v[0]
        for b in range(1, N_DEV):
            total = total + gat_v[b]
        tv[...] = total

        def sum_gates(r, carry):
            rows8 = pl.ds(r, s8)
            part = gat_g[0, rows8, :]
            for b in range(1, N_DEV):
                part = part + gat_g[b, rows8, :]
            tg[rows8, :] = part
            return carry

        _chunks(tg.shape[0], s8, sum_gates, 0)

    vm = pl.BlockSpec(memory_space=pltpu.VMEM)
    g_rows = 2 * n_tiles * hd
    return pl.pallas_call(
        body, name="small_reduce",
        in_specs=[vm] * 5, out_specs=(vm, vm),
        out_shape=(jax.ShapeDtypeStruct((SL_ROWS, d), F32), jax.ShapeDtypeStruct((g_rows, tw), F32)),
        scratch_shapes=[pltpu.VMEM((N_DEV, SL_ROWS, d), F32), pltpu.VMEM((N_DEV, g_rows, tw), F32),
                        pltpu.SemaphoreType.DMA((14,)), pltpu.SemaphoreType.DMA((14,))],
        compiler_params=_params(),
    )(facc, xacc, bacc, gwa_t, gwi_t)


def _small_update(tot_v, tot_g, vec_w, vec_m, vec_v, gates, convs):
    n_vec = len(vec_w)
    n_heads, hd, _ = gates[0].shape
    tw = tot_g.shape[1]
    per = tw // hd
    n_tiles = n_heads // per
    cc = convs[0].shape[1]
    n_in = 2 + 3 * n_vec + 12

    def body(*refs):
        tv, tg = refs[:2]
        w_refs, m_refs, v_refs = (refs[2 + j * n_vec:2 + (j + 1) * n_vec] for j in range(3))
        gate_refs = refs[2 + 3 * n_vec:2 + 3 * n_vec + 6]
        conv_refs = refs[2 + 3 * n_vec + 6:n_in]
        loss_o = refs[n_in]
        kinds = [refs[n_in + 1 + j * (n_vec + 4):n_in + 1 + (j + 1) * (n_vec + 4)] for j in range(4)]
        x, y, c = _mesh_pos()
        me = 4 * x + 2 * y + c

        def emit(k_out, w, g, m, v):
            delta, m_new, v_new = _adamw(w, g, m, v)
            for ref, val in zip(k_out, (g, delta, m_new, v_new)):
                ref[...] = val

        loss_o[...] = jnp.broadcast_to(tv[SL_LOSS:SL_LOSS + 1, 0:LANES], loss_o.shape)
        for p in range(n_vec):
            w, g = w_refs[p][...], tv[SL_LN_G + p, :]
            if SL_LN_G + p == SL_LAM:
                g = g * (RG_LRU_C * jax.nn.sigmoid(-w))
            emit([k_out[p] for k_out in kinds], w, g, m_refs[p][...], v_refs[p][...])
        lanes = pl.ds(pl.multiple_of(me * cc, cc), cc)
        for j, (row0, n) in enumerate(((SL_CONV_W, 3), (SL_LRU_W, 4))):
            w_ref, m_ref, v_ref = conv_refs[3 * j:3 * j + 3]
            emit([k_out[n_vec + 2 + j] for k_out in kinds], w_ref[...], tv[row0:row0 + n, lanes], m_ref[...], v_ref[...])
        for mtx in range(2):
            w_ref, m_ref, v_ref = gate_refs[3 * mtx:3 * mtx + 3]
            for k in range(n_tiles):
                tile = tg[(mtx * n_tiles + k) * hd:(mtx * n_tiles + k + 1) * hd, :]
                for a in range(per):
                    head = k * per + a
                    g = tile[:, a * hd:(a + 1) * hd]
                    delta, m_new, v_new = _adamw(w_ref[head], g, m_ref[head], v_ref[head])
                    for k_out, val in zip(kinds, (g, delta, m_new, v_new)):
                        k_out[n_vec + mtx][head] = val

    vm = pl.BlockSpec(memory_space=pltpu.VMEM)
    like = lambda a: jax.ShapeDtypeStruct(a.shape, F32)
    per_kind = tuple(like(a) for a in vec_w) + (like(gates[0]), like(gates[3]), like(convs[0]), like(convs[3]))
    n_out = 1 + 4 * len(per_kind)
    outs = pl.pallas_call(
        body, name="small_update",
        in_specs=[vm] * n_in, out_specs=(vm,) * n_out,
        out_shape=(jax.ShapeDtypeStruct((SUBLANES, LANES), F32),) + per_kind * 4,
        compiler_params=_params(),
    )(tot_v, tot_g, *vec_w, *vec_m, *vec_v, *gates, *convs)
    return outs[0], [outs[1 + j * len(per_kind):1 + (j + 1) * len(per_kind)] for j in range(4)]


def _head_ones(head_dim, tw):
    lane = jnp.arange(tw) // head_dim
    return (lane[:, None] == lane[None, :]).astype(MM)


def _gate_tiles(w, tw):
    n_heads, hd, _ = w.shape
    per = tw // hd
    w4 = w.reshape(n_heads // per, per, hd, hd)
    eye = jnp.eye(per, dtype=w.dtype)
    return (w4[:, :, :, None, :] * eye[None, :, None, :, None]).reshape(n_heads // per, tw, tw)


def kernel(x, ln_g, w_in, conv_w, lru_conv_w, lru_conv_b, w_a, b_a, w_i, b_i, lam, conv_out_g, lru_out_g, w_out, final_g, loss_target, m_ln_g, m_w_in, m_conv_w, m_lru_conv_w, m_lru_conv_b, m_w_a, m_b_a, m_w_i, m_b_i, m_lam, m_conv_out_g, m_lru_out_g, m_w_out, m_final_g, v_ln_g, v_w_in, v_conv_w, v_lru_conv_w, v_lru_conv_b, v_w_a, v_b_a, v_w_i, v_b_i, v_lam, v_conv_out_g, v_lru_out_g, v_w_out, v_final_g):
    _, t_len, d = x.shape
    hd_l = d // N_LRU_HEADS
    tw = min(MXU_TILE, d)
    cc = conv_w.shape[1]
    x2, tgt2 = x[0], loss_target[0]

    def conv_rows(cw3, lw4):
        return jnp.concatenate([jnp.zeros((1, cc), F32), cw3, lw4], axis=0)

    p, xnt, win_all, wout_all, conv_all = _gather_project(
        x2, w_in, w_out, conv_rows(conv_w, lru_conv_w), ln_g.reshape(1, d), min(256, t_len))
    wout_full = wout_all.reshape(N_DEV * w_out.shape[0], d)
    conv_full = conv_all.transpose(1, 0, 2).reshape(SUBLANES, d)
    small = [ln_g, lru_conv_b, b_a, b_i, lam, conv_out_g, lru_out_g, final_g]
    sp = jnp.concatenate([jnp.stack(small), conv_full[1:], jnp.zeros((1, d), F32)], axis=0)
    wa_t, wi_t = _gate_tiles(w_a, tw).astype(MM), _gate_tiles(w_i, tw).astype(MM)
    ones_c, ones_l = _head_ones(d // N_CONV_HEADS, tw), _head_ones(hd_l, tw)

    h, dh, dhb, facc, *saved = _forward(x2, tgt2, p, wout_full, wa_t, wi_t, sp, ones_c, ones_l, min(256, t_len))
    dp, yt, gwa_t, gwi_t, bacc = _backward(p, h, dh, saved, wout_full, wa_t, wi_t, sp, ones_c, ones_l, min(256, t_len))
    part_out, own_out, _ = _weight_grad_out(yt, dhb)
    part_in, own_in, (chips_out,) = _weight_grad_in(xnt, dp, (part_out,))
    grad_x, xacc, (chips_in,) = _input_grad(dp, win_all, x2, dh, sp, (part_in,), min(512, t_len))
    gw_in, dw_in, mw_in, vw_in = _update_shard(own_in, chips_in, w_in, m_w_in, v_w_in, "update_w_in")
    gw_out, dw_out, mw_out, vw_out = _update_shard(own_out, chips_out, w_out, m_w_out, v_w_out, "update_w_out")

    tot_v, tot_g = _small_reduce(facc, xacc, bacc, gwa_t, gwi_t, hd_l)
    loss_tile, kinds = _small_update(
        tot_v, tot_g, small,
        [m_ln_g, m_lru_conv_b, m_b_a, m_b_i, m_lam, m_conv_out_g, m_lru_out_g, m_final_g],
        [v_ln_g, v_lru_conv_b, v_b_a, v_b_i, v_lam, v_conv_out_g, v_lru_out_g, v_final_g],
        (w_a, m_w_a, v_w_a, w_i, m_w_i, v_w_i), (conv_w, m_conv_w, v_conv_w, lru_conv_w, m_lru_conv_w, v_lru_conv_w))

    def unpack(kind, big_in, big_out):
        vec, (wa_, wi_, cw_, lw_) = kind[:len(small)], kind[len(small):]
        return [vec[0], big_in, cw_, lw_, vec[1], wa_, vec[2], wi_, vec[3], vec[4], vec[5], vec[6], big_out, vec[7]]

    return (loss_tile[0, 0], grad_x[None], *unpack(kinds[0], gw_in, gw_out), *unpack(kinds[1], dw_in, dw_out),
            *unpack(kinds[2], mw_in, mw_out), *unpack(kinds[3], vw_in, vw_out))
```

```python
import functools

import jax
import jax.numpy as jnp
from jax import lax
from jax.experimental import pallas as pl
from jax.experimental.pallas import tpu as pltpu

F32 = jnp.float32
MM = jnp.bfloat16
MESH = pl.DeviceIdType.MESH

N_DEV = 8
N_CONV_HEADS = 8
N_LRU_HEADS = 16
RG_LRU_C = 8.0
RMS_EPS = 1e-6
ADAM_LR, ADAM_B1, ADAM_B2, ADAM_EPS, ADAM_WD, ADAM_STEP = 0.001, 0.9, 0.999, 1e-08, 0.01, 10
ADAM_BC1 = 1.0 - ADAM_B1 ** ADAM_STEP
ADAM_BC2 = 1.0 - ADAM_B2 ** ADAM_STEP

SUBLANES = 8
LANES = 128
MXU_TILE = 256
VMEM_LIMIT = 56 * 1024 * 1024

SP_LN_G, SP_LRU_B, SP_B_A, SP_B_I, SP_LAM, SP_CONV_G, SP_LRU_G, SP_FINAL_G, SP_CONV_W, SP_LRU_W = 0, 1, 2, 3, 4, 5, 6, 7, 8, 11
SP_ROWS = 16
P_B, P_C, P_XC, P_GC, P_XL, P_GL = 0, 1, 2, 3, 4, 5
A_CONV_G, A_LRU_G, A_LAM, A_B_A, A_B_I, A_CONV_W, A_LRU_W, A_LRU_B = 0, 1, 2, 3, 4, 5, 8, 12
A_GROUPS = 13
SL_LOSS, SL_LN_G, SL_LRU_B, SL_B_A, SL_B_I, SL_LAM, SL_CONV_G, SL_LRU_G, SL_FINAL_G, SL_CONV_W, SL_LRU_W = 0, 1, 2, 3, 4, 5, 6, 7, 8, 16, 24
SL_ROWS = 32


def _params(vmem=True, **kw):
    if vmem:
        kw["vmem_limit_bytes"] = VMEM_LIMIT
    return pltpu.CompilerParams(**kw)


def _dot(a, b):
    return jnp.dot(a, b, preferred_element_type=F32)


def _dot_nt(a, b):
    return lax.dot_general(a, b, (((1,), (1,)), ((), ())), preferred_element_type=F32)


def _head_sums(v, ones_tile):
    tw = ones_tile.shape[0]
    vb = v.astype(MM)
    return jnp.concatenate([_dot(vb[:, k:k + tw], ones_tile) for k in range(0, v.shape[1], tw)], axis=1)


def _head_rstd(v, ones_tile, head_dim):
    return lax.rsqrt(_head_sums(v * v, ones_tile) * (1.0 / head_dim) + RMS_EPS)


def _sigmoid(x):
    return 0.5 * jnp.tanh(0.5 * x) + 0.5


def _lru_input_scale(log_a, a):
    return jnp.sqrt(-jnp.tanh(log_a) * (1.0 + a * a))


def _log_sigmoid(x):
    z = jnp.exp(-jnp.abs(x))
    u = 1.0 + z
    log1p_z = jnp.where(u == 1.0, z, jnp.log(u) * (z / (u - 1.0)))
    return jnp.minimum(x, 0.0) - log1p_z


def _row_iota(d):
    return lax.broadcasted_iota(jnp.int32, (SUBLANES, d), 0)


def _shift_down(cur, prev, s, row):
    return jnp.where(row >= s, pltpu.roll(cur, s, axis=0), pltpu.roll(prev, s, axis=0))


def _shift_up(cur, nxt, s, row):
    k = SUBLANES - s
    return jnp.where(row < k, pltpu.roll(cur, k, axis=0), pltpu.roll(nxt, k, axis=0))


def _scan_fwd(a, b, h_prev, row):
    for s in (1, 2, 4):
        a_s = jnp.where(row >= s, pltpu.roll(a, s, axis=0), 1.0)
        b_s = jnp.where(row >= s, pltpu.roll(b, s, axis=0), 0.0)
        b = a * b_s + b
        a = a * a_s
    return a * h_prev + b


def _scan_bwd(a_next, b, g_next, row):
    a = a_next
    for s in (1, 2, 4):
        k = SUBLANES - s
        a_s = jnp.where(row < k, pltpu.roll(a, k, axis=0), 1.0)
        b_s = jnp.where(row < k, pltpu.roll(b, k, axis=0), 0.0)
        b = a * b_s + b
        a = a * a_s
    return a * g_next + b


def _bcast_row(v, r):
    return jnp.broadcast_to(v[r:r + 1, :], v.shape)


def _chunks(n_rows, rc, body, init, reverse=False):
    n = n_rows // rc

    def step(i, carry):
        j = (n - 1 - i) if reverse else i
        return body(pl.multiple_of(j * rc, rc), carry)

    return lax.fori_loop(0, n, step, init)


def _adamw(w, g, m, v):
    m = ADAM_B1 * m + (1.0 - ADAM_B1) * g
    v = ADAM_B2 * v + (1.0 - ADAM_B2) * (g * g)
    m_hat = m / ADAM_BC1
    v_hat = v / ADAM_BC2
    delta = -ADAM_LR * (m_hat / (jnp.sqrt(v_hat) + ADAM_EPS) + ADAM_WD * w)
    return delta, m, v


def _mesh_pos():
    return lax.axis_index("x"), lax.axis_index("y"), lax.axis_index("c")


class _Gather:
    def __init__(self, blocks_of, send_sems, recv_sems, own_src=None):
        x, y, c = _mesh_pos()
        self.c = c
        self.me, self.sibling = (x, y, c), (x, y, 1 - c)
        self.chips = [(1 - x, y), (x, 1 - y), (1 - x, 1 - y)]
        self.blocks_of, self.send_sems, self.recv_sems = blocks_of, send_sems, recv_sems
        self.own_src = own_src

    def copy(self, a, k, block, to):
        src = self.blocks_of(a, *block)
        if block is self.me and self.own_src is not None:
            src = self.own_src[a]
        return pltpu.make_async_remote_copy(
            src_ref=src, dst_ref=self.blocks_of(a, *block),
            send_sem=self.send_sems.at[a * 7 + k], recv_sem=self.recv_sems.at[a * 7 + k],
            device_id=to, device_id_type=MESH)

    def start_own(self, a):
        self.copy(a, 0, self.me, self.sibling).start()
        for j, chip in enumerate(self.chips):
            self.copy(a, 1 + j, self.me, (*chip, self.c)).start()

    def wait_sibling(self, a):
        self.copy(a, 0, self.sibling, self.me).wait_recv()

    def wait_chip_and_pass_on(self, a, j):
        block = (*self.chips[j], self.c)
        self.copy(a, 1 + j, block, self.me).wait_recv()
        self.copy(a, 4 + j, block, self.sibling).start()

    def wait_passed_on(self, a, j):
        self.copy(a, 4 + j, (*self.chips[j], 1 - self.c), self.me).wait_recv()

    def wait_sends(self, a):
        self.copy(a, 0, self.me, self.sibling).wait_send()
        for j, chip in enumerate(self.chips):
            self.copy(a, 1 + j, self.me, (*chip, self.c)).wait_send()
            self.copy(a, 4 + j, (*chip, self.c), self.sibling).wait_send()

    def finish(self, a):
        for j in range(3):
            self.wait_chip_and_pass_on(a, j)
        self.wait_sibling(a)
        for j in range(3):
            self.wait_passed_on(a, j)
        self.wait_sends(a)


class _BalancedGather:
    def __init__(self, slot, send_sems, recv_sems, own_src):
        x, y, c = _mesh_pos()
        self.c = c
        self.me, self.sibling = (x, y, c), (x, y, 1 - c)
        self.chips = [(1 - x, y), (x, 1 - y), (1 - x, 1 - y)]
        self.slot, self.send_sems, self.recv_sems, self.own_src = slot, send_sems, recv_sems, own_src

    def half(self, a, block, which):
        ref = self.slot(a, *block)
        n = ref.shape[0] // 2
        return ref.at[pl.ds(which * n, n)]

    def copy(self, a, k, src, dst, to):
        return pltpu.make_async_remote_copy(
            src_ref=src, dst_ref=dst, send_sem=self.send_sems.at[a * 8 + k], recv_sem=self.recv_sems.at[a * 8 + k],
            device_id=to, device_id_type=MESH)

    def whole(self, a, k, block, to):
        src = self.own_src[a] if block is self.me else self.slot(a, *block)
        return self.copy(a, k, src, self.slot(a, *block), to)

    def halved(self, a, k, block, which, to):
        return self.copy(a, k, self.half(a, block, which), self.half(a, block, which), to)

    def on(self, chip):
        return (*self.chips[chip], self.c)

    def start_own(self, a):
        self.whole(a, 0, self.me, self.sibling).start()
        self.whole(a, 1, self.me, self.on(0)).start()
        self.whole(a, 2, self.me, self.on(1)).start()

    def wait_sibling(self, a):
        self.whole(a, 0, self.sibling, self.me).wait_recv()

    def on_neighbour(self, a, j):
        self.whole(a, 1 + j, self.on(j), self.me).wait_recv()
        self.halved(a, 3 + j, self.on(j), j, self.on(1 - j)).start()
        self.whole(a, 5 + j, self.on(j), self.sibling).start()

    def on_diagonal(self, a):
        self.halved(a, 3, self.on(2), 0, self.me).wait_recv()
        self.halved(a, 4, self.on(2), 1, self.me).wait_recv()
        self.whole(a, 7, self.on(2), self.sibling).start()

    def wait_passed_on(self, a, j):
        self.whole(a, 5 + j, (*self.chips[j], 1 - self.c), self.me).wait_recv()

    def wait_sends(self, a):
        self.whole(a, 0, self.me, self.sibling).wait_send()
        for j in range(2):
            self.whole(a, 1 + j, self.me, self.on(j)).wait_send()
            self.halved(a, 3 + j, self.on(j), j, self.on(1 - j)).wait_send()
        for j in range(3):
            self.whole(a, 5 + j, self.on(j), self.sibling).wait_send()


def _block_order():
    x, y, c = _mesh_pos()
    chips = [(x, y), (1 - x, y), (x, 1 - y), (1 - x, 1 - y)]
    return jnp.stack([4 * px + 2 * py + pc for px, py in chips for pc in (c, 1 - c)]).astype(jnp.int32)


def _gather_project(x, w_in, w_out, conv_pack, ln_g, tb):
    t_len, d = x.shape
    nb = t_len // tb
    cols = w_in.shape[1]
    mc = min(512, t_len)
    srcs = (w_in, w_out, conv_pack)
    dts = (MM, MM, F32)

    def body(order_ref, x_ref, win_ref, wout_ref, cp_ref, lng_ref, p_ref, xnt_ref, win_all, wout_all, cp_all,
             xnb, wall, st_out, st_cp, send_sems, recv_sems, cp_send, cp_recv, local_sems):
        i = pl.program_id(0)
        x_, y_, c_ = _mesh_pos()
        me = 4 * x_ + 2 * y_ + c_
        outs = (win_all, wout_all, cp_all)
        lands = (wall, wout_all, cp_all)
        stages = (wall.at[me], st_out, st_cp)
        gather = _BalancedGather(lambda a, px, py, pc: lands[a].at[4 * px + 2 * py + pc], send_sems, recv_sems, stages)
        small = _Gather(lambda a, px, py, pc: cp_all.at[4 * px + 2 * py + pc], cp_send, cp_recv, own_src=[st_cp])
        keep_own = [pltpu.make_async_copy(stages[a], outs[a].at[me], local_sems.at[a]) for a in range(3)]

        def keep(k):
            blk = order_ref[k]
            return pltpu.make_async_copy(wall.at[blk], win_all.at[blk], local_sems.at[2 + k])

        @pl.when(i == 0)
        def _():
            for a, (src, dst) in enumerate(zip((win_ref, wout_ref, cp_ref), stages)):
                rows = src.shape[0]
                rc = min(rows, 32)

                def cast(r, carry, src=src, dst=dst, rc=rc):
                    dst[pl.ds(r, rc), :] = src[pl.ds(r, rc), :].astype(dst.dtype)
                    return carry

                _chunks(rows, rc, cast, 0)
                if a < 2:
                    gather.start_own(a)
                else:
                    small.start_own(0)
                keep_own[a].start()

        @pl.when(i < nb)
        def _():
            xv = x_ref[...]
            r0 = lax.rsqrt(jnp.mean(xv * xv, axis=-1, keepdims=True) + RMS_EPS)
            xn = xv * r0 * lng_ref[...]
            xnb[pl.ds(pl.multiple_of(i * tb, tb), tb), :] = xn.astype(MM)
            xnt_ref[...] = xn.T.astype(MM)

        for k in range(N_DEV):
            @pl.when(i == nb + k)
            def _(k=k):
                if k == 1:
                    gather.wait_sibling(0)
                elif k == 2:
                    gather.on_neighbour(0, 0)
                    gather.on_neighbour(0, 1)
                elif k in (3, 5, 7):
                    gather.wait_passed_on(0, (k - 3) // 2)
                    if k == 3:
                        gather.on_neighbour(1, 0)
                        gather.on_neighbour(1, 1)
                    if k == 7:
                        gather.on_diagonal(1)
                elif k == 6:
                    gather.on_diagonal(0)
                blk = order_ref[k]
                if k:
                    keep(k).start()

                def project(r, carry):
                    rows = pl.ds(r, mc)
                    p_ref[rows, :] = _dot(xnb[rows, :], wall[blk]).astype(MM)
                    return carry

                _chunks(t_len, mc, project, 0)
                if k == N_DEV - 1:
                    gather.wait_sends(0)
                    gather.wait_sibling(1)
                    for j in range(3):
                        gather.wait_passed_on(1, j)
                    gather.wait_sends(1)
                    small.finish(0)
                    for cp in keep_own + [keep(kk) for kk in range(1, N_DEV)]:
                        cp.wait()

    vm = pl.BlockSpec(memory_space=pltpu.VMEM)
    hbm = pl.BlockSpec(memory_space=pl.ANY)
    grid_spec = pltpu.PrefetchScalarGridSpec(
        num_scalar_prefetch=1, grid=(nb + N_DEV,),
        in_specs=[pl.BlockSpec((tb, d), lambda i, o: (jnp.minimum(i, nb - 1), 0)), vm, vm, vm, vm],
        out_specs=(pl.BlockSpec((t_len, cols), lambda i, o: (0, o[jnp.maximum(i - nb, 0)])),
                   pl.BlockSpec((d, tb), lambda i, o: (0, jnp.minimum(i, nb - 1))), hbm, hbm, hbm),
        scratch_shapes=[pltpu.VMEM((t_len, d), MM), pltpu.VMEM((N_DEV,) + w_in.shape, MM),
                        pltpu.VMEM(w_out.shape, MM), pltpu.VMEM(conv_pack.shape, F32),
                        pltpu.SemaphoreType.DMA((16,)), pltpu.SemaphoreType.DMA((16,)),
                        pltpu.SemaphoreType.DMA((7,)), pltpu.SemaphoreType.DMA((7,)), pltpu.SemaphoreType.DMA((10,))])
    return pl.pallas_call(
        body, name="gather_project", grid_spec=grid_spec,
        out_shape=(jax.ShapeDtypeStruct((t_len, N_DEV * cols), MM),
                   jax.ShapeDtypeStruct((d, t_len), MM))
                  + tuple(jax.ShapeDtypeStruct((N_DEV,) + s.shape, dt) for s, dt in zip(srcs, dts)),
        compiler_params=_params(dimension_semantics=("arbitrary",)),
    )(_block_order(), x, w_in, w_out, conv_pack, ln_g)


def _forward(x, tgt, p, wout, wa_t, wi_t, sp, ones_c, ones_l, tb):
    t_len, d = x.shape
    nb = t_len // tb
    n_tiles, tw = wa_t.shape[0], wa_t.shape[1]
    hd_c, hd_l = d // N_CONV_HEADS, d // N_LRU_HEADS
    s8 = SUBLANES

    def body(x_ref, tgt_ref, p_ref, wout_ref, wa_ref, wi_ref, sp_ref, oc_ref, ol_ref,
             h_ref, dh_ref, dhb_ref, acc_ref, yc, czs, u, pa, pi,
             rcf, rlf, ybuf, tail_z, tail_xl, hcar):
        i = pl.program_id(0)
        row = _row_iota(d)

        @pl.when(i == 0)
        def _():
            tail_z[...] = jnp.zeros_like(tail_z)
            tail_xl[...] = jnp.zeros_like(tail_xl)
            hcar[...] = jnp.zeros_like(hcar)
            acc_ref[...] = jnp.zeros_like(acc_ref)

        def spr(r):
            return sp_ref[r:r + 1, :]

        def proj(rows, seg):
            return p_ref[rows, seg * d:(seg + 1) * d].astype(F32)

        w0, w1, w2 = spr(SP_CONV_W), spr(SP_CONV_W + 1), spr(SP_CONV_W + 2)
        l0, l1, l2, l3 = spr(SP_LRU_W), spr(SP_LRU_W + 1), spr(SP_LRU_W + 2), spr(SP_LRU_W + 3)
        lb = spr(SP_LRU_B)

        def convs(r, carry):
            zp, xp = carry
            rows16 = pl.ds(r, 2 * s8)
            bg16, xl16 = proj(rows16, P_B), proj(rows16, P_XL)
            z16 = proj(rows16, P_C) * proj(rows16, P_XC)
            for j in range(2):
                rows, sub = pl.ds(r + j * s8, s8), slice(j * s8, (j + 1) * s8)
                z, xl = z16[sub], xl16[sub]
                cz = w0 * _shift_down(z, zp, 2, row) + w1 * _shift_down(z, zp, 1, row) + w2 * z
                czs[rows, :] = cz
                yc[rows, :] = bg16[sub] * cz
                u[rows, :] = (l0 * _shift_down(xl, xp, 3, row) + l1 * _shift_down(xl, xp, 2, row)
                              + l2 * _shift_down(xl, xp, 1, row) + l3 * xl + lb)
                zp, xp = z, xl
            return zp, xp

        z_last, xl_last = _chunks(tb, 2 * s8, convs, (tail_z[...], tail_xl[...]))
        tail_z[...] = z_last
        tail_xl[...] = xl_last

        ub = u[...].astype(MM)
        for k in range(n_tiles):
            sl = slice(k * tw, (k + 1) * tw)
            pa[:, sl] = _dot(ub[:, sl], wa_ref[k])
            pi[:, sl] = _dot(ub[:, sl], wi_ref[k])
        rcf[...] = _head_rstd(yc[...], oc_ref[...], hd_c)

        c8 = RG_LRU_C * _log_sigmoid(spr(SP_LAM))
        b_a, b_i = spr(SP_B_A), spr(SP_B_I)

        def lru(r, hp):
            rows = pl.ds(r, SUBLANES)
            ra = _sigmoid(pa[rows, :] + b_a)
            ii = _sigmoid(pi[rows, :] + b_i)
            pa[rows, :] = ra
            pi[rows, :] = ii
            la = ra * c8
            a = jnp.exp(la)
            mult = _lru_input_scale(la, a)
            h = _scan_fwd(a, mult * (ii * u[rows, :]), hp, row)
            h_ref[rows, :] = h
            return _bcast_row(h, SUBLANES - 1)

        hcar[...] = _chunks(tb, SUBLANES, lru, hcar[...])
        rlf[...] = _head_rstd(h_ref[...], ol_ref[...], hd_l)

        g_c, g_l = spr(SP_CONV_G), spr(SP_LRU_G)

        def gate(r, carry):
            rows = pl.ds(r, 2 * s8)
            gc, gl = proj(rows, P_GC), proj(rows, P_GL)
            ybuf[rows, 0:d] = (yc[rows, :] * rcf[rows, :] * g_c * (gc * _sigmoid(gc))).astype(MM)
            ybuf[rows, d:2 * d] = (h_ref[rows, :] * rlf[rows, :] * g_l * (gl * _sigmoid(gl))).astype(MM)
            return carry

        _chunks(tb, 2 * s8, gate, 0)

        hres = x_ref[...] + _dot(ybuf[...], wout_ref[...])
        rf = lax.rsqrt(jnp.mean(hres * hres, axis=-1, keepdims=True) + RMS_EPS)
        hn = hres * rf
        fg = spr(SP_FINAL_G)
        err = hn * fg - tgt_ref[...]
        dout = err * (1.0 / d)
        acc_ref[0:SUBLANES, :] += (err * err).reshape(tb // SUBLANES, SUBLANES, d).sum(axis=0)
        acc_ref[SUBLANES:2 * SUBLANES, :] += (dout * hn).reshape(tb // SUBLANES, SUBLANES, d).sum(axis=0)
        gd = dout * fg
        dhres = rf * (gd - hn * jnp.mean(gd * hn, axis=-1, keepdims=True))
        dh_ref[...] = dhres
        dhb_ref[...] = dhres.astype(MM)

    vm = pl.BlockSpec(memory_space=pltpu.VMEM)
    blk = lambda w: pl.BlockSpec((tb, w), lambda i: (i, 0))
    buf = pltpu.VMEM((tb, d), F32)
    car = pltpu.VMEM((SUBLANES, d), F32)
    return pl.pallas_call(
        body, name="forward", grid=(nb,),
        in_specs=[blk(d), blk(d), blk(6 * d), vm, vm, vm, vm, vm, vm],
        out_specs=(blk(d), blk(d), blk(d), pl.BlockSpec((2 * SUBLANES, d), lambda i: (0, 0))) + (blk(d),) * 5,
        out_shape=(jax.ShapeDtypeStruct((t_len, d), F32),
                   jax.ShapeDtypeStruct((t_len, d), F32),
                   jax.ShapeDtypeStruct((t_len, d), MM),
                   jax.ShapeDtypeStruct((2 * SUBLANES, d), F32))
                  + (jax.ShapeDtypeStruct((t_len, d), F32),) * 5,
        scratch_shapes=[buf] * 2 + [pltpu.VMEM((tb, 2 * d), MM), car, car, car],
        compiler_params=_params(dimension_semantics=("arbitrary",)),
    )(x, tgt, p, wout, wa_t, wi_t, sp, ones_c, ones_l)


def _backward(p, h, dh, saved, wout, wa_t, wi_t, sp, ones_c, ones_l, tb):
    t_len, d = h.shape
    nb = t_len // tb
    n_tiles, tw = wa_t.shape[0], wa_t.shape[1]
    hd_c, hd_l = d // N_CONV_HEADS, d // N_LRU_HEADS
    s8 = SUBLANES

    def body(p_ref, h_ref, hhalo_ref, dh_ref, yc, czs, u, ra_ref, ii_ref,
             wout_ref, wa_ref, wi_ref, sp_ref, oc_ref, ol_ref,
             dp_ref, yt_ref, gwa_ref, gwi_ref, acc_ref,
             hh, dy, ybuf, rcf, rlf, qc, ql, dyc_hat, dyl_hat, dpa, dpi, du,
             car_dcz, car_a, car_g, car_du):
        i = pl.program_id(0)
        blk_idx = nb - 1 - i
        row = _row_iota(d)

        @pl.when(i == 0)
        def _():
            for ref in (car_dcz, car_a, car_g, car_du, gwa_ref, gwi_ref, acc_ref):
                ref[...] = jnp.zeros_like(ref)

        def spr(r):
            return sp_ref[r:r + 1, :]

        def proj(rows, seg):
            return p_ref[rows, seg * d:(seg + 1) * d].astype(F32)

        def put(rows, seg, halves):
            dp_ref[rows, seg * d:(seg + 1) * d] = jnp.concatenate(halves, axis=0).astype(MM)

        def acc_add(group, val):
            acc_ref[group * s8:(group + 1) * s8, :] += val

        live = jnp.where(blk_idx > 0, 1.0, 0.0).astype(F32)
        hh[0:s8, :] = hhalo_ref[...] * live
        hh[s8:, :] = h_ref[...]

        dy[...] = _dot_nt(dh_ref[...].astype(MM), wout_ref[...])

        w0, w1, w2 = spr(SP_CONV_W), spr(SP_CONV_W + 1), spr(SP_CONV_W + 2)
        l0, l1, l2, l3 = spr(SP_LRU_W), spr(SP_LRU_W + 1), spr(SP_LRU_W + 2), spr(SP_LRU_W + 3)

        rcf[...] = _head_rstd(yc[...], oc_ref[...], hd_c)
        rlf[...] = _head_rstd(h_ref[...], ol_ref[...], hd_l)

        g_c, g_l = spr(SP_CONV_G), spr(SP_LRU_G)

        def gates(r, carry):
            rows = pl.ds(r, 2 * s8)
            for (seg, off_y, src, rstd, gain, q, dhat, grp) in (
                    (P_GC, 0, yc, rcf, g_c, qc, dyc_hat, A_CONV_G),
                    (P_GL, d, h_ref, rlf, g_l, ql, dyl_hat, A_LRU_G)):
                gt = proj(rows, seg)
                sg = _sigmoid(gt)
                silu = gt * sg
                yhat = src[rows, :] * rstd[rows, :]
                nrm = yhat * gain
                ybuf[rows, off_y:off_y + d] = nrm * silu
                dout = dy[rows, off_y:off_y + d]
                dnrm = dout * silu
                dp_ref[rows, seg * d:(seg + 1) * d] = (dout * nrm * (sg * (1.0 + gt * (1.0 - sg)))).astype(MM)
                dg = dnrm * yhat
                acc_add(grp, dg[0:s8] + dg[s8:])
                dh_ = dnrm * gain
                dhat[rows, :] = dh_
                q[rows, :] = dh_ * yhat
            return carry

        _chunks(tb, 2 * s8, gates, 0)

        qc[...] = _head_sums(qc[...], oc_ref[...]) * (1.0 / hd_c)
        ql[...] = _head_sums(ql[...], ol_ref[...]) * (1.0 / hd_l)
        yt_ref[...] = ybuf[...].T.astype(MM)

        c8 = RG_LRU_C * _log_sigmoid(spr(SP_LAM))

        def conv_mixer(r, dcz_n):
            rows16 = pl.ds(r, 2 * s8)
            bg16, cg16, xc16 = proj(rows16, P_B), proj(rows16, P_C), proj(rows16, P_XC)
            z16 = cg16 * xc16
            d_b, d_c, d_x = [None, None], [None, None], [None, None]
            for j in (1, 0):
                rows, sub = pl.ds(r + j * s8, s8), slice(j * s8, (j + 1) * s8)
                rstd = rcf[rows, :]
                yhat = yc[rows, :] * rstd
                dyc = rstd * (dyc_hat[rows, :] - yhat * qc[rows, :])
                d_b[j] = dyc * czs[rows, :]
                dcz = dyc * bg16[sub]
                up1, up2 = _shift_up(dcz, dcz_n, 1, row), _shift_up(dcz, dcz_n, 2, row)
                dz = w2 * dcz + w1 * up1 + w0 * up2
                d_c[j] = dz * xc16[sub]
                d_x[j] = dz * cg16[sub]
                z = z16[sub]
                acc_add(A_CONV_W, up2 * z)
                acc_add(A_CONV_W + 1, up1 * z)
                acc_add(A_CONV_W + 2, dcz * z)
                dcz_n = dcz
            put(rows16, P_B, d_b)
            put(rows16, P_C, d_c)
            put(rows16, P_XC, d_x)
            return dcz_n

        car_dcz[...] = _chunks(tb, 2 * s8, conv_mixer, car_dcz[...], reverse=True)

        def lru_mixer(r, carry):
            a_n, g_n = carry
            for j in (1, 0):
                rows = pl.ds(r + j * s8, s8)
                rstd = rlf[rows, :]
                hcur = hh[pl.ds(r + (j + 1) * s8, s8), :]
                hhat = hcur * rstd
                dh_out = rstd * (dyl_hat[rows, :] - hhat * ql[rows, :])
                ra = ra_ref[rows, :]
                la = ra * c8
                a = jnp.exp(la)
                g = _scan_bwd(_shift_up(a, a_n, 1, row), dh_out, g_n, row)
                da = g * _shift_down(hcur, hh[pl.ds(r + j * s8, s8), :], 1, row)
                ii = ii_ref[rows, :]
                uu = u[rows, :]
                mult = _lru_input_scale(la, a)
                dmult = g * (ii * uu)
                ds = g * mult
                dla = a * (da - dmult * a / mult)
                acc_add(A_LAM, dla * ra)
                dpa_ = dla * c8 * ra * (1.0 - ra)
                dpi_ = ds * uu * ii * (1.0 - ii)
                acc_add(A_B_A, dpa_)
                acc_add(A_B_I, dpi_)
                dpa[rows, :] = dpa_
                dpi[rows, :] = dpi_
                du[rows, :] = ds * ii
                a_n, g_n = a, _bcast_row(g, 0)
            return a_n, g_n

        a_f, g_f = _chunks(tb, 2 * s8, lru_mixer, (car_a[...], car_g[...]), reverse=True)
        car_a[...] = a_f
        car_g[...] = g_f

        dpab = dpa[...].astype(MM)
        dpib = dpi[...].astype(MM)
        for k in range(n_tiles):
            sl = slice(k * tw, (k + 1) * tw)
            du[:, sl] += _dot_nt(dpab[:, sl], wa_ref[k]) + _dot_nt(dpib[:, sl], wi_ref[k])
            ut = u[:, sl].T.astype(MM)
            gwa_ref[k] += _dot(ut, dpab[:, sl])
            gwi_ref[k] += _dot(ut, dpib[:, sl])

        def lru_conv(r, du_n):
            rows16 = pl.ds(r, 2 * s8)
            xl16 = proj(rows16, P_XL)
            d_xl = [None, None]
            for j in (1, 0):
                rows, sub = pl.ds(r + j * s8, s8), slice(j * s8, (j + 1) * s8)
                dut = du[rows, :]
                up1, up2, up3 = (_shift_up(dut, du_n, s, row) for s in (1, 2, 3))
                d_xl[j] = l3 * dut + l2 * up1 + l1 * up2 + l0 * up3
                xl = xl16[sub]
                acc_add(A_LRU_W, up3 * xl)
                acc_add(A_LRU_W + 1, up2 * xl)
                acc_add(A_LRU_W + 2, up1 * xl)
                acc_add(A_LRU_W + 3, dut * xl)
                acc_add(A_LRU_B, dut)
                du_n = dut
            put(rows16, P_XL, d_xl)
            return du_n

        car_du[...] = _chunks(tb, 2 * s8, lru_conv, car_du[...], reverse=True)

    vm = pl.BlockSpec(memory_space=pltpu.VMEM)
    rev = lambda w: pl.BlockSpec((tb, w), lambda i: (nb - 1 - i, 0))
    halo = lambda rows, w: pl.BlockSpec((rows, w), lambda i: (jnp.maximum((nb - 1 - i) * (tb // rows) - 1, 0), 0))
    const = lambda shape: pl.BlockSpec(shape, lambda i: (0,) * len(shape))
    buf = lambda w: pltpu.VMEM((tb, w), F32)
    car = pltpu.VMEM((SUBLANES, d), F32)
    return pl.pallas_call(
        body, name="backward", grid=(nb,),
        in_specs=[rev(6 * d), rev(d), halo(SUBLANES, d), rev(d)] + [rev(d)] * 5 + [vm, vm, vm, vm, vm, vm],
        out_specs=(rev(6 * d), pl.BlockSpec((2 * d, tb), lambda i: (0, nb - 1 - i)),
                   const((n_tiles, tw, tw)), const((n_tiles, tw, tw)), const((A_GROUPS * SUBLANES, d))),
        out_shape=(jax.ShapeDtypeStruct((t_len, 6 * d), MM),
                   jax.ShapeDtypeStruct((2 * d, t_len), MM),
                   jax.ShapeDtypeStruct((n_tiles, tw, tw), F32),
                   jax.ShapeDtypeStruct((n_tiles, tw, tw), F32),
                   jax.ShapeDtypeStruct((A_GROUPS * SUBLANES, d), F32)),
        scratch_shapes=[pltpu.VMEM((SUBLANES + tb, d), F32), buf(2 * d), buf(2 * d)] + [buf(d)] * 9 + [car, car, car, car],
        compiler_params=_params(dimension_semantics=("arbitrary",)),
    )(p, h, h, dh, *saved, wout, wa_t, wi_t, sp, ones_c, ones_l)


def _input_grad(dp, win_all, x, dh, sp, parts, tb):
    t_len, d = x.shape
    nb = t_len // tb
    cols = win_all.shape[2]
    n_parts = len(parts)

    def body(dp_ref, win_ref, x_ref, dh_ref, sp_ref, *refs):
        gx_ref, acc_ref = refs[n_parts:n_parts + 2]
        exchange = _ChipExchange(refs[:n_parts], refs[n_parts + 2:2 * n_parts + 2], *refs[2 * n_parts + 2:])
        i = pl.program_id(0)

        @pl.when(i == 0)
        def _():
            acc_ref[...] = jnp.zeros_like(acc_ref)
            exchange.start()

        dxn = _dot_nt(dp_ref[:, 0:cols], win_ref[0])
        for j in range(1, N_DEV):
            dxn += _dot_nt(dp_ref[:, j * cols:(j + 1) * cols], win_ref[j])
        xv = x_ref[...]
        r0 = lax.rsqrt(jnp.mean(xv * xv, axis=-1, keepdims=True) + RMS_EPS)
        xhat = xv * r0
        acc_ref[...] += (dxn * xhat).reshape(tb // SUBLANES, SUBLANES, d).sum(axis=0)
        dxh = dxn * sp_ref[SP_LN_G:SP_LN_G + 1, :]
        gx_ref[...] = dh_ref[...] + r0 * (dxh - xhat * jnp.mean(dxh * xhat, axis=-1, keepdims=True))

        @pl.when(i == nb - 1)
        def _():
            exchange.finish()

    vm = pl.BlockSpec(memory_space=pltpu.VMEM)
    hbm = pl.BlockSpec(memory_space=pl.ANY)
    blk = lambda w: pl.BlockSpec((tb, w), lambda i: (i, 0))
    outs = pl.pallas_call(
        body, name="input_grad", grid=(nb,),
        in_specs=[blk(6 * d), vm, blk(d), blk(d), vm] + [hbm] * n_parts,
        out_specs=(blk(d), pl.BlockSpec((SUBLANES, d), lambda i: (0, 0))) + (hbm,) * n_parts,
        out_shape=(jax.ShapeDtypeStruct((t_len, d), F32), jax.ShapeDtypeStruct((SUBLANES, d), F32))
                  + tuple(jax.ShapeDtypeStruct(p.shape, p.dtype) for p in parts),
        scratch_shapes=[pltpu.SemaphoreType.DMA((3 * n_parts,)), pltpu.SemaphoreType.DMA((3 * n_parts,))],
        compiler_params=_params(dimension_semantics=("arbitrary",)),
    )(dp, win_all, x, dh, sp, *parts)
    return outs[0], outs[1], outs[2:]


_CHIP_RELATIONS = [(0, 0), (1, 0), (0, 1), (1, 1)]


def _related_block(k, core):
    x, y, _ = _mesh_pos()
    fx, fy = _CHIP_RELATIONS[k]
    return 4 * (x ^ fx) + 2 * (y ^ fy) + core


class _ChipExchange:
    def __init__(self, part_refs, land_refs, send_sems, recv_sems):
        self.part_refs, self.land_refs, self.send_sems, self.recv_sems = part_refs, land_refs, send_sems, recv_sems

    def copies(self):
        x, y, c = _mesh_pos()
        for a in range(len(self.part_refs)):
            for k in (1, 2, 3):
                fx, fy = _CHIP_RELATIONS[k]
                yield pltpu.make_async_remote_copy(
                    src_ref=self.part_refs[a].at[k - 1], dst_ref=self.land_refs[a].at[k - 1],
                    send_sem=self.send_sems.at[3 * a + k - 1], recv_sem=self.recv_sems.at[3 * a + k - 1],
                    device_id=(x ^ fx, y ^ fy, c), device_id_type=MESH)

    def start(self):
        for cp in self.copies():
            cp.start()

    def finish(self):
        for cp in self.copies():
            cp.wait_recv()
        for cp in self.copies():
            cp.wait_send()


def _weight_grad_stage1(name, blk_shape, n_split, operands, in_specs, product, riders=()):
    n_rows, n_cols = blk_shape
    rs = n_rows // n_split
    rc = 32
    n_in, n_ride = len(operands), len(riders)
    _, _, c = _mesh_pos()
    order = jnp.stack([_related_block(k, 1 - c) for k in range(4)]
                      + [_related_block(k, c) for k in (1, 2, 3, 0)]).astype(jnp.int32)

    def body(order_ref, *refs):
        ins = refs[:n_in]
        ride_in = refs[n_in:n_in + n_ride]
        part_ref, own_ref = refs[n_in + n_ride:n_in + n_ride + 2]
        ride_out = refs[n_in + n_ride + 2:n_in + 2 * n_ride + 2]
        gbuf, sendbuf, from_sib, send_sems, recv_sems, ride_send, ride_recv = refs[n_in + 2 * n_ride + 2:]
        exchange = _ChipExchange(ride_in, ride_out, ride_send, ride_recv)
        s = pl.program_id(0)
        x, y, c = _mesh_pos()

        def to_sibling(k):
            return pltpu.make_async_remote_copy(
                src_ref=sendbuf.at[k], dst_ref=from_sib.at[k], send_sem=send_sems.at[k], recv_sem=recv_sems.at[k],
                device_id=(x, y, 1 - c), device_id_type=MESH)

        if n_ride:
            @pl.when(s == 0)
            def _():
                exchange.start()

        for h in range(n_split):
            gbuf[h * rs:(h + 1) * rs, :] = product(ins, h)

        @pl.when(s < 4)
        def _():
            def narrow(r, carry):
                sendbuf[s, pl.ds(r, rc), :] = gbuf[pl.ds(r, rc), :].astype(MM)
                return carry

            _chunks(n_rows, rc, narrow, 0)
            to_sibling(s).start()

        @pl.when(s >= 4)
        def _():
            k = jnp.where(s == 7, 0, s - 3)
            to_sibling(k).wait_recv()

            @pl.when(s < 7)
            def _():
                def add(r, carry):
                    rows = pl.ds(r, rc)
                    part_ref[0, rows, :] = (gbuf[rows, :] + from_sib[k, rows, :].astype(F32)).astype(MM)
                    return carry

                _chunks(n_rows, rc, add, 0)

            @pl.when(s == 7)
            def _():
                def add(r, carry):
                    rows = pl.ds(r, rc)
                    own_ref[rows, :] = gbuf[rows, :] + from_sib[0, rows, :].astype(F32)
                    return carry

                _chunks(n_rows, rc, add, 0)
                for kk in range(4):
                    to_sibling(kk).wait_send()
                if n_ride:
                    exchange.finish()

    hbm = pl.BlockSpec(memory_space=pl.ANY)
    grid_spec = pltpu.PrefetchScalarGridSpec(
        num_scalar_prefetch=1, grid=(N_DEV,), in_specs=list(in_specs) + [hbm] * n_ride,
        out_specs=(pl.BlockSpec((1, n_rows, n_cols), lambda s, o: (jnp.clip(s - 4, 0, 2), 0, 0)),
                   pl.BlockSpec((n_rows, n_cols), lambda s, o: (0, 0))) + (hbm,) * n_ride,
        scratch_shapes=[pltpu.VMEM((n_rows, n_cols), F32), pltpu.VMEM((4, n_rows, n_cols), MM),
                        pltpu.VMEM((4, n_rows, n_cols), MM),
                        pltpu.SemaphoreType.DMA((4,)), pltpu.SemaphoreType.DMA((4,)),
                        pltpu.SemaphoreType.DMA((max(3 * n_ride, 1),)), pltpu.SemaphoreType.DMA((max(3 * n_ride, 1),))])
    outs = pl.pallas_call(
        body, name=name, grid_spec=grid_spec,
        out_shape=(jax.ShapeDtypeStruct((3, n_rows, n_cols), MM), jax.ShapeDtypeStruct((n_rows, n_cols), F32))
                  + tuple(jax.ShapeDtypeStruct(p.shape, p.dtype) for p in riders),
        compiler_params=_params(dimension_semantics=("arbitrary",)),
    )(order, *operands, *riders)
    return outs[0], outs[1], outs[2:]


def _weight_grad_in(xnt, dp, riders):
    d, t_len = xnt.shape
    cols = dp.shape[1] // N_DEV
    half = d // 2
    return _weight_grad_stage1(
        "weight_grad_in", (d, cols), 2, (xnt, dp),
        [pl.BlockSpec(memory_space=pltpu.VMEM), pl.BlockSpec((t_len, cols), lambda s, o: (0, o[s]))],
        lambda refs, h: _dot(refs[0][h * half:(h + 1) * half, :], refs[1][...]), riders)


def _weight_grad_out(yt, dhb):
    d2, t_len = yt.shape
    d = dhb.shape[1]
    rows = d2 // N_DEV
    return _weight_grad_stage1(
        "weight_grad_out", (rows, d), 1, (yt, dhb),
        [pl.BlockSpec((rows, t_len), lambda s, o: (o[s], 0)), pl.BlockSpec(memory_space=pltpu.VMEM)],
        lambda refs, h: _dot(refs[0][...], refs[1][...]))


def _update_shard(own, from_chips, w, m, v, name):
    n_rows, n_cols = w.shape
    rb = min(256, n_rows)

    def body(own_ref, fc_ref, w_ref, m_ref, v_ref, grad_ref, delta_ref, mo_ref, vo_ref):
        g = own_ref[...]
        for k in range(3):
            g = g + fc_ref[k].astype(F32)
        delta, m_new, v_new = _adamw(w_ref[...], g, m_ref[...], v_ref[...])
        grad_ref[...] = g
        delta_ref[...] = delta
        mo_ref[...] = m_new
        vo_ref[...] = v_new

    blk = pl.BlockSpec((rb, n_cols), lambda i: (i, 0))
    out = jax.ShapeDtypeStruct((n_rows, n_cols), F32)
    return pl.pallas_call(
        body, name=name, grid=(n_rows // rb,),
        in_specs=[blk, pl.BlockSpec((3, rb, n_cols), lambda i: (0, i, 0)), blk, blk, blk],
        out_specs=(blk, blk, blk, blk), out_shape=(out, out, out, out),
        compiler_params=_params(dimension_semantics=("arbitrary",)),
    )(own, from_chips, w, m, v)


def _small_reduce(facc, xacc, bacc, gwa_t, gwi_t, hd):
    d = facc.shape[1]
    s8 = SUBLANES
    n_tiles, tw, _ = gwa_t.shape
    per = tw // hd

    def body(facc_ref, xacc_ref, bacc_ref, gwa_ref, gwi_ref, tv, tg, gat_v, gat_g, send_sems, recv_sems):
        x, y, c = _mesh_pos()
        me = 4 * x + 2 * y + c

        def rowsum(ref, group):
            return jnp.sum(ref[group * s8:(group + 1) * s8, :], axis=0, keepdims=True)

        mine = gat_v.at[me]
        mine[...] = jnp.zeros_like(mine)
        loss = jnp.sum(rowsum(facc_ref, 0), axis=1, keepdims=True) * (0.5 / d)
        mine[SL_LOSS:SL_LOSS + 1, :] = jnp.broadcast_to(loss, (1, d))
        rows = {SL_LN_G: rowsum(xacc_ref, 0), SL_LRU_B: rowsum(bacc_ref, A_LRU_B), SL_B_A: rowsum(bacc_ref, A_B_A),
                SL_B_I: rowsum(bacc_ref, A_B_I), SL_LAM: rowsum(bacc_ref, A_LAM),
                SL_CONV_G: rowsum(bacc_ref, A_CONV_G), SL_LRU_G: rowsum(bacc_ref, A_LRU_G), SL_FINAL_G: rowsum(facc_ref, 1)}
        for k in range(3):
            rows[SL_CONV_W + k] = rowsum(bacc_ref, A_CONV_W + k)
        for k in range(4):
            rows[SL_LRU_W + k] = rowsum(bacc_ref, A_LRU_W + k)
        for r, val in rows.items():
            mine[r:r + 1, :] = val
        head_of_lane = lax.broadcasted_iota(jnp.int32, (hd, tw), 1) // hd
        for mtx, g_ref in enumerate((gwa_ref, gwi_ref)):
            for k in range(n_tiles):
                packed = jnp.zeros((hd, tw), F32)
                for a in range(per):
                    packed = jnp.where(head_of_lane == a, g_ref[k, a * hd:(a + 1) * hd, :], packed)
                gat_g[me, (mtx * n_tiles + k) * hd:(mtx * n_tiles + k + 1) * hd, :] = packed.astype(MM)

        gats = (gat_v, gat_g)
        gather = _BalancedGather(lambda a, px, py, pc: gats[a].at[4 * px + 2 * py + pc], send_sems, recv_sems,
                                 [gat_v.at[me], gat_g.at[me]])
        for a in range(2):
            gather.start_own(a)
        for step in (lambda a: gather.on_neighbour(a, 0), lambda a: gather.on_neighbour(a, 1), gather.on_diagonal,
                     gather.wait_sibling):
            for a in range(2):
                step(a)
        for a in range(2):
            for j in range(3):
                gather.wait_passed_on(a, j)
            gather.wait_sends(a)

        total = gat_v[0]
        for b in range(1, N_DEV):
            total = total + gat_v[b]
        tv[...] = total

        def sum_gates(r, carry):
            rows = pl.ds(r, 2 * s8)
            part = gat_g[0, rows, :].astype(F32)
            for b in range(1, N_DEV):
                part = part + gat_g[b, rows, :].astype(F32)
            tg[rows, :] = part
            return carry

        _chunks(tg.shape[0], 2 * s8, sum_gates, 0)

    vm = pl.BlockSpec(memory_space=pltpu.VMEM)
    g_rows = 2 * n_tiles * hd
    return pl.pallas_call(
        body, name="small_reduce",
        in_specs=[vm] * 5, out_specs=(vm, vm),
        out_shape=(jax.ShapeDtypeStruct((SL_ROWS, d), F32), jax.ShapeDtypeStruct((g_rows, tw), F32)),
        scratch_shapes=[pltpu.VMEM((N_DEV, SL_ROWS, d), F32), pltpu.VMEM((N_DEV, g_rows, tw), MM),
                        pltpu.SemaphoreType.DMA((16,)), pltpu.SemaphoreType.DMA((16,))],
        compiler_params=_params(),
    )(facc, xacc, bacc, gwa_t, gwi_t)


def _small_update(tot_v, tot_g, vec_w, vec_m, vec_v, gates, convs):
    n_vec = len(vec_w)
    n_heads, hd, _ = gates[0].shape
    tw = tot_g.shape[1]
    per = tw // hd
    n_tiles = n_heads // per
    cc = convs[0].shape[1]
    n_in = 2 + 3 * n_vec + 12

    def body(*refs):
        tv, tg = refs[:2]
        w_refs, m_refs, v_refs = (refs[2 + j * n_vec:2 + (j + 1) * n_vec] for j in range(3))
        gate_refs = refs[2 + 3 * n_vec:2 + 3 * n_vec + 6]
        conv_refs = refs[2 + 3 * n_vec + 6:n_in]
        loss_o = refs[n_in]
        kinds = [refs[n_in + 1 + j * (n_vec + 4):n_in + 1 + (j + 1) * (n_vec + 4)] for j in range(4)]
        x, y, c = _mesh_pos()
        me = 4 * x + 2 * y + c

        def emit(k_out, w, g, m, v):
            delta, m_new, v_new = _adamw(w, g, m, v)
            for ref, val in zip(k_out, (g, delta, m_new, v_new)):
                ref[...] = val

        loss_o[...] = jnp.broadcast_to(tv[SL_LOSS:SL_LOSS + 1, 0:LANES], loss_o.shape)
        for p in range(n_vec):
            w, g = w_refs[p][...], tv[SL_LN_G + p, :]
            if SL_LN_G + p == SL_LAM:
                g = g * (RG_LRU_C * jax.nn.sigmoid(-w))
            emit([k_out[p] for k_out in kinds], w, g, m_refs[p][...], v_refs[p][...])
        lanes = pl.ds(pl.multiple_of(me * cc, cc), cc)
        for j, (row0, n) in enumerate(((SL_CONV_W, 3), (SL_LRU_W, 4))):
            w_ref, m_ref, v_ref = conv_refs[3 * j:3 * j + 3]
            emit([k_out[n_vec + 2 + j] for k_out in kinds], w_ref[...], tv[row0:row0 + n, lanes], m_ref[...], v_ref[...])
        for mtx in range(2):
            w_ref, m_ref, v_ref = gate_refs[3 * mtx:3 * mtx + 3]
            for k in range(n_tiles):
                tile = tg[(mtx * n_tiles + k) * hd:(mtx * n_tiles + k + 1) * hd, :]
                for a in range(per):
                    head = k * per + a
                    g = tile[:, a * hd:(a + 1) * hd]
                    delta, m_new, v_new = _adamw(w_ref[head], g, m_ref[head], v_ref[head])
                    for k_out, val in zip(kinds, (g, delta, m_new, v_new)):
                        k_out[n_vec + mtx][head] = val

    vm = pl.BlockSpec(memory_space=pltpu.VMEM)
    like = lambda a: jax.ShapeDtypeStruct(a.shape, F32)
    per_kind = tuple(like(a) for a in vec_w) + (like(gates[0]), like(gates[3]), like(convs[0]), like(convs[3]))
    n_out = 1 + 4 * len(per_kind)
    outs = pl.pallas_call(
        body, name="small_update",
        in_specs=[vm] * n_in, out_specs=(vm,) * n_out,
        out_shape=(jax.ShapeDtypeStruct((SUBLANES, LANES), F32),) + per_kind * 4,
        compiler_params=_params(),
    )(tot_v, tot_g, *vec_w, *vec_m, *vec_v, *gates, *convs)
    return outs[0], [outs[1 + j * len(per_kind):1 + (j + 1) * len(per_kind)] for j in range(4)]


def _head_ones(head_dim, tw):
    lane = jnp.arange(tw) // head_dim
    return (lane[:, None] == lane[None, :]).astype(MM)


def _gate_tiles(w, tw):
    n_heads, hd, _ = w.shape
    per = tw // hd
    w4 = w.reshape(n_heads // per, per, hd, hd)
    eye = jnp.eye(per, dtype=w.dtype)
    return (w4[:, :, :, None, :] * eye[None, :, None, :, None]).reshape(n_heads // per, tw, tw)


def kernel(x, ln_g, w_in, conv_w, lru_conv_w, lru_conv_b, w_a, b_a, w_i, b_i, lam, conv_out_g, lru_out_g, w_out, final_g, loss_target, m_ln_g, m_w_in, m_conv_w, m_lru_conv_w, m_lru_conv_b, m_w_a, m_b_a, m_w_i, m_b_i, m_lam, m_conv_out_g, m_lru_out_g, m_w_out, m_final_g, v_ln_g, v_w_in, v_conv_w, v_lru_conv_w, v_lru_conv_b, v_w_a, v_b_a, v_w_i, v_b_i, v_lam, v_conv_out_g, v_lru_out_g, v_w_out, v_final_g):
    _, t_len, d = x.shape
    hd_l = d // N_LRU_HEADS
    tw = min(MXU_TILE, d)
    cc = conv_w.shape[1]
    x2, tgt2 = x[0], loss_target[0]

    def conv_rows(cw3, lw4):
        return jnp.concatenate([jnp.zeros((1, cc), F32), cw3, lw4], axis=0)

    p, xnt, win_all, wout_all, conv_all = _gather_project(
        x2, w_in, w_out, conv_rows(conv_w, lru_conv_w), ln_g.reshape(1, d), min(256, t_len))
    wout_full = wout_all.reshape(N_DEV * w_out.shape[0], d)
    conv_full = conv_all.transpose(1, 0, 2).reshape(SUBLANES, d)
    small = [ln_g, lru_conv_b, b_a, b_i, lam, conv_out_g, lru_out_g, final_g]
    sp = jnp.concatenate([jnp.stack(small), conv_full[1:], jnp.zeros((1, d), F32)], axis=0)
    wa_t, wi_t = _gate_tiles(w_a, tw).astype(MM), _gate_tiles(w_i, tw).astype(MM)
    ones_c, ones_l = _head_ones(d // N_CONV_HEADS, tw), _head_ones(hd_l, tw)

    h, dh, dhb, facc, *saved = _forward(x2, tgt2, p, wout_full, wa_t, wi_t, sp, ones_c, ones_l, min(256, t_len))
    dp, yt, gwa_t, gwi_t, bacc = _backward(p, h, dh, saved, wout_full, wa_t, wi_t, sp, ones_c, ones_l, min(256, t_len))
    part_out, own_out, _ = _weight_grad_out(yt, dhb)
    part_in, own_in, (chips_out,) = _weight_grad_in(xnt, dp, (part_out,))
    grad_x, xacc, (chips_in,) = _input_grad(dp, win_all, x2, dh, sp, (part_in,), min(512, t_len))
    gw_in, dw_in, mw_in, vw_in = _update_shard(own_in, chips_in, w_in, m_w_in, v_w_in, "update_w_in")
    gw_out, dw_out, mw_out, vw_out = _update_shard(own_out, chips_out, w_out, m_w_out, v_w_out, "update_w_out")

    tot_v, tot_g = _small_reduce(facc, xacc, bacc, gwa_t, gwi_t, hd_l)
    loss_tile, kinds = _small_update(
        tot_v, tot_g, small,
        [m_ln_g, m_lru_conv_b, m_b_a, m_b_i, m_lam, m_conv_out_g, m_lru_out_g, m_final_g],
        [v_ln_g, v_lru_conv_b, v_b_a, v_b_i, v_lam, v_conv_out_g, v_lru_out_g, v_final_g],
        (w_a, m_w_a, v_w_a, w_i, m_w_i, v_w_i), (conv_w, m_conv_w, v_conv_w, lru_conv_w, m_lru_conv_w, v_lru_conv_w))

    def unpack(kind, big_in, big_out):
        vec, (wa_, wi_, cw_, lw_) = kind[:len(small)], kind[len(small):]
        return [vec[0], big_in, cw_, lw_, vec[1], wa_, vec[2], wi_, vec[3], vec[4], vec[5], vec[6], big_out, vec[7]]

    return (loss_tile[0, 0], grad_x[None], *unpack(kinds[0], gw_in, gw_out), *unpack(kinds[1], dw_in, dw_out),
            *unpack(kinds[2], mw_in, mw_out), *unpack(kinds[3], vw_in, vw_out))
```

```python
import functools

import jax
import jax.numpy as jnp
from jax import lax
from jax.experimental import pallas as pl
from jax.experimental.pallas import tpu as pltpu

F32 = jnp.float32
MM = jnp.bfloat16
MESH = pl.DeviceIdType.MESH

N_DEV = 8
N_CONV_HEADS = 8
N_LRU_HEADS = 16
RG_LRU_C = 8.0
RMS_EPS = 1e-6
ADAM_LR, ADAM_B1, ADAM_B2, ADAM_EPS, ADAM_WD, ADAM_STEP = 0.001, 0.9, 0.999, 1e-08, 0.01, 10
ADAM_BC1 = 1.0 - ADAM_B1 ** ADAM_STEP
ADAM_BC2 = 1.0 - ADAM_B2 ** ADAM_STEP

SUBLANES = 8
LANES = 128
MXU_TILE = 256
VMEM_LIMIT = 56 * 1024 * 1024

SP_LN_G, SP_LRU_B, SP_B_A, SP_B_I, SP_LAM, SP_CONV_G, SP_LRU_G, SP_FINAL_G, SP_CONV_W, SP_LRU_W = 0, 1, 2, 3, 4, 5, 6, 7, 8, 11
SP_ROWS = 16
P_B, P_C, P_XC, P_GC, P_XL, P_GL = 0, 1, 2, 3, 4, 5
A_CONV_G, A_LRU_G, A_LAM, A_B_A, A_B_I, A_CONV_W, A_LRU_W, A_LRU_B = 0, 1, 2, 3, 4, 5, 8, 12
A_GROUPS = 13
SL_LOSS, SL_LN_G, SL_LRU_B, SL_B_A, SL_B_I, SL_LAM, SL_CONV_G, SL_LRU_G, SL_FINAL_G, SL_CONV_W, SL_LRU_W = 0, 1, 2, 3, 4, 5, 6, 7, 8, 16, 24
SL_ROWS = 32


def _params(vmem=True, **kw):
    if vmem:
        kw["vmem_limit_bytes"] = VMEM_LIMIT
    return pltpu.CompilerParams(**kw)


def _dot(a, b):
    return jnp.dot(a, b, preferred_element_type=F32)


def _dot_nt(a, b):
    return lax.dot_general(a, b, (((1,), (1,)), ((), ())), preferred_element_type=F32)


def _head_sums(v, ones_tile):
    tw = ones_tile.shape[0]
    vb = v.astype(MM)
    return jnp.concatenate([_dot(vb[:, k:k + tw], ones_tile) for k in range(0, v.shape[1], tw)], axis=1)


def _head_rstd(v, ones_tile, head_dim):
    return lax.rsqrt(_head_sums(v * v, ones_tile) * (1.0 / head_dim) + RMS_EPS)


def _sigmoid(x):
    return 0.5 * jnp.tanh(0.5 * x) + 0.5


def _lru_input_scale(log_a, a):
    return jnp.sqrt(-jnp.tanh(log_a) * (1.0 + a * a))


def _log_sigmoid(x):
    z = jnp.exp(-jnp.abs(x))
    u = 1.0 + z
    log1p_z = jnp.where(u == 1.0, z, jnp.log(u) * (z / (u - 1.0)))
    return jnp.minimum(x, 0.0) - log1p_z


def _row_iota(d):
    return lax.broadcasted_iota(jnp.int32, (SUBLANES, d), 0)


def _shift_down(cur, prev, s, row):
    return jnp.where(row >= s, pltpu.roll(cur, s, axis=0), pltpu.roll(prev, s, axis=0))


def _shift_up(cur, nxt, s, row):
    k = SUBLANES - s
    return jnp.where(row < k, pltpu.roll(cur, k, axis=0), pltpu.roll(nxt, k, axis=0))


def _scan_fwd(a, b, h_prev, row):
    for s in (1, 2, 4):
        a_s = jnp.where(row >= s, pltpu.roll(a, s, axis=0), 1.0)
        b_s = jnp.where(row >= s, pltpu.roll(b, s, axis=0), 0.0)
        b = a * b_s + b
        a = a * a_s
    return a * h_prev + b


def _scan_bwd(a_next, b, g_next, row):
    a = a_next
    for s in (1, 2, 4):
        k = SUBLANES - s
        a_s = jnp.where(row < k, pltpu.roll(a, k, axis=0), 1.0)
        b_s = jnp.where(row < k, pltpu.roll(b, k, axis=0), 0.0)
        b = a * b_s + b
        a = a * a_s
    return a * g_next + b


def _bcast_row(v, r):
    return jnp.broadcast_to(v[r:r + 1, :], v.shape)


def _chunks(n_rows, rc, body, init, reverse=False):
    n = n_rows // rc

    def step(i, carry):
        j = (n - 1 - i) if reverse else i
        return body(pl.multiple_of(j * rc, rc), carry)

    return lax.fori_loop(0, n, step, init)


def _adamw(w, g, m, v):
    m = ADAM_B1 * m + (1.0 - ADAM_B1) * g
    v = ADAM_B2 * v + (1.0 - ADAM_B2) * (g * g)
    m_hat = m / ADAM_BC1
    v_hat = v / ADAM_BC2
    delta = -ADAM_LR * (m_hat / (jnp.sqrt(v_hat) + ADAM_EPS) + ADAM_WD * w)
    return delta, m, v


def _mesh_pos():
    return lax.axis_index("x"), lax.axis_index("y"), lax.axis_index("c")


class _Gather:
    def __init__(self, blocks_of, send_sems, recv_sems, own_src=None):
        x, y, c = _mesh_pos()
        self.c = c
        self.me, self.sibling = (x, y, c), (x, y, 1 - c)
        self.chips = [(1 - x, y), (x, 1 - y), (1 - x, 1 - y)]
        self.blocks_of, self.send_sems, self.recv_sems = blocks_of, send_sems, recv_sems
        self.own_src = own_src

    def copy(self, a, k, block, to):
        src = self.blocks_of(a, *block)
        if block is self.me and self.own_src is not None:
            src = self.own_src[a]
        return pltpu.make_async_remote_copy(
            src_ref=src, dst_ref=self.blocks_of(a, *block),
            send_sem=self.send_sems.at[a * 7 + k], recv_sem=self.recv_sems.at[a * 7 + k],
            device_id=to, device_id_type=MESH)

    def start_own(self, a):
        self.copy(a, 0, self.me, self.sibling).start()
        for j, chip in enumerate(self.chips):
            self.copy(a, 1 + j, self.me, (*chip, self.c)).start()

    def wait_sibling(self, a):
        self.copy(a, 0, self.sibling, self.me).wait_recv()

    def wait_chip_and_pass_on(self, a, j):
        block = (*self.chips[j], self.c)
        self.copy(a, 1 + j, block, self.me).wait_recv()
        self.copy(a, 4 + j, block, self.sibling).start()

    def wait_passed_on(self, a, j):
        self.copy(a, 4 + j, (*self.chips[j], 1 - self.c), self.me).wait_recv()

    def wait_sends(self, a):
        self.copy(a, 0, self.me, self.sibling).wait_send()
        for j, chip in enumerate(self.chips):
            self.copy(a, 1 + j, self.me, (*chip, self.c)).wait_send()
            self.copy(a, 4 + j, (*chip, self.c), self.sibling).wait_send()

    def finish(self, a):
        for j in range(3):
            self.wait_chip_and_pass_on(a, j)
        self.wait_sibling(a)
        for j in range(3):
            self.wait_passed_on(a, j)
        self.wait_sends(a)


class _BalancedGather:
    def __init__(self, slot, send_sems, recv_sems, own_src):
        x, y, c = _mesh_pos()
        self.c = c
        self.me, self.sibling = (x, y, c), (x, y, 1 - c)
        self.chips = [(1 - x, y), (x, 1 - y), (1 - x, 1 - y)]
        self.slot, self.send_sems, self.recv_sems, self.own_src = slot, send_sems, recv_sems, own_src

    def half(self, a, block, which):
        ref = self.slot(a, *block)
        n = ref.shape[0] // 2
        return ref.at[pl.ds(which * n, n)]

    def copy(self, a, k, src, dst, to):
        return pltpu.make_async_remote_copy(
            src_ref=src, dst_ref=dst, send_sem=self.send_sems.at[a * 8 + k], recv_sem=self.recv_sems.at[a * 8 + k],
            device_id=to, device_id_type=MESH)

    def whole(self, a, k, block, to):
        src = self.own_src[a] if block is self.me else self.slot(a, *block)
        return self.copy(a, k, src, self.slot(a, *block), to)

    def halved(self, a, k, block, which, to):
        return self.copy(a, k, self.half(a, block, which), self.half(a, block, which), to)

    def on(self, chip):
        return (*self.chips[chip], self.c)

    def start_own(self, a):
        self.whole(a, 0, self.me, self.sibling).start()
        self.whole(a, 1, self.me, self.on(0)).start()
        self.whole(a, 2, self.me, self.on(1)).start()

    def wait_sibling(self, a):
        self.whole(a, 0, self.sibling, self.me).wait_recv()

    def on_neighbour(self, a, j):
        self.whole(a, 1 + j, self.on(j), self.me).wait_recv()
        self.halved(a, 3 + j, self.on(j), j, self.on(1 - j)).start()
        self.whole(a, 5 + j, self.on(j), self.sibling).start()

    def on_diagonal(self, a):
        self.halved(a, 3, self.on(2), 0, self.me).wait_recv()
        self.halved(a, 4, self.on(2), 1, self.me).wait_recv()
        self.whole(a, 7, self.on(2), self.sibling).start()

    def wait_passed_on(self, a, j):
        self.whole(a, 5 + j, (*self.chips[j], 1 - self.c), self.me).wait_recv()

    def wait_sends(self, a):
        self.whole(a, 0, self.me, self.sibling).wait_send()
        for j in range(2):
            self.whole(a, 1 + j, self.me, self.on(j)).wait_send()
            self.halved(a, 3 + j, self.on(j), j, self.on(1 - j)).wait_send()
        for j in range(3):
            self.whole(a, 5 + j, self.on(j), self.sibling).wait_send()


def _block_order():
    x, y, c = _mesh_pos()
    chips = [(x, y), (1 - x, y), (x, 1 - y), (1 - x, 1 - y)]
    return jnp.stack([4 * px + 2 * py + pc for px, py in chips for pc in (c, 1 - c)]).astype(jnp.int32)


def _gather_project(x, w_in, w_out, conv_pack, ln_g, tb):
    t_len, d = x.shape
    nb = t_len // tb
    cols = w_in.shape[1]
    mc = min(512, t_len)
    srcs = (w_in, w_out, conv_pack)
    dts = (MM, MM, F32)

    def body(order_ref, x_ref, win_ref, wout_ref, cp_ref, lng_ref, p_ref, xnt_ref, win_all, wout_all, cp_all,
             xnb, wall, st_out, st_cp, send_sems, recv_sems, cp_send, cp_recv, local_sems):
        i = pl.program_id(0)
        x_, y_, c_ = _mesh_pos()
        me = 4 * x_ + 2 * y_ + c_
        outs = (win_all, wout_all, cp_all)
        lands = (wall, wout_all, cp_all)
        stages = (wall.at[me], st_out, st_cp)
        gather = _BalancedGather(lambda a, px, py, pc: lands[a].at[4 * px + 2 * py + pc], send_sems, recv_sems, stages)
        small = _Gather(lambda a, px, py, pc: cp_all.at[4 * px + 2 * py + pc], cp_send, cp_recv, own_src=[st_cp])
        keep_own = [pltpu.make_async_copy(stages[a], outs[a].at[me], local_sems.at[a]) for a in range(3)]

        def keep(k):
            blk = order_ref[k]
            return pltpu.make_async_copy(wall.at[blk], win_all.at[blk], local_sems.at[2 + k])

        @pl.when(i == 0)
        def _():
            for a, (src, dst) in enumerate(zip((win_ref, wout_ref, cp_ref), stages)):
                rows = src.shape[0]
                rc = min(rows, 32)

                def cast(r, carry, src=src, dst=dst, rc=rc):
                    dst[pl.ds(r, rc), :] = src[pl.ds(r, rc), :].astype(dst.dtype)
                    return carry

                _chunks(rows, rc, cast, 0)
                if a < 2:
                    gather.start_own(a)
                else:
                    small.start_own(0)
                keep_own[a].start()

        @pl.when(i < nb)
        def _():
            xv = x_ref[...]
            r0 = lax.rsqrt(jnp.mean(xv * xv, axis=-1, keepdims=True) + RMS_EPS)
            xn = xv * r0 * lng_ref[...]
            xnb[pl.ds(pl.multiple_of(i * tb, tb), tb), :] = xn.astype(MM)
            xnt_ref[...] = xn.T.astype(MM)

        for k in range(N_DEV):
            @pl.when(i == nb + k)
            def _(k=k):
                if k == 1:
                    gather.wait_sibling(0)
                elif k == 2:
                    gather.on_neighbour(0, 0)
                    gather.on_neighbour(0, 1)
                elif k in (3, 5, 7):
                    gather.wait_passed_on(0, (k - 3) // 2)
                    if k == 3:
                        gather.on_neighbour(1, 0)
                        gather.on_neighbour(1, 1)
                    if k == 7:
                        gather.on_diagonal(1)
                elif k == 6:
                    gather.on_diagonal(0)
                blk = order_ref[k]
                if k:
                    keep(k).start()

                def project(r, carry):
                    rows = pl.ds(r, mc)
                    p_ref[rows, :] = _dot(xnb[rows, :], wall[blk]).astype(MM)
                    return carry

                _chunks(t_len, mc, project, 0)
                if k == N_DEV - 1:
                    gather.wait_sends(0)
                    gather.wait_sibling(1)
                    for j in range(3):
                        gather.wait_passed_on(1, j)
                    gather.wait_sends(1)
                    small.finish(0)
                    for cp in keep_own + [keep(kk) for kk in range(1, N_DEV)]:
                        cp.wait()

    vm = pl.BlockSpec(memory_space=pltpu.VMEM)
    hbm = pl.BlockSpec(memory_space=pl.ANY)
    grid_spec = pltpu.PrefetchScalarGridSpec(
        num_scalar_prefetch=1, grid=(nb + N_DEV,),
        in_specs=[pl.BlockSpec((tb, d), lambda i, o: (jnp.minimum(i, nb - 1), 0)), vm, vm, vm, vm],
        out_specs=(pl.BlockSpec((t_len, cols), lambda i, o: (0, o[jnp.maximum(i - nb, 0)])),
                   pl.BlockSpec((d, tb), lambda i, o: (0, jnp.minimum(i, nb - 1))), hbm, hbm, hbm),
        scratch_shapes=[pltpu.VMEM((t_len, d), MM), pltpu.VMEM((N_DEV,) + w_in.shape, MM),
                        pltpu.VMEM(w_out.shape, MM), pltpu.VMEM(conv_pack.shape, F32),
                        pltpu.SemaphoreType.DMA((16,)), pltpu.SemaphoreType.DMA((16,)),
                        pltpu.SemaphoreType.DMA((7,)), pltpu.SemaphoreType.DMA((7,)), pltpu.SemaphoreType.DMA((10,))])
    return pl.pallas_call(
        body, name="gather_project", grid_spec=grid_spec,
        out_shape=(jax.ShapeDtypeStruct((t_len, N_DEV * cols), MM),
                   jax.ShapeDtypeStruct((d, t_len), MM))
                  + tuple(jax.ShapeDtypeStruct((N_DEV,) + s.shape, dt) for s, dt in zip(srcs, dts)),
        compiler_params=_params(dimension_semantics=("arbitrary",)),
    )(_block_order(), x, w_in, w_out, conv_pack, ln_g)


def _forward(x, tgt, p, wout, wa_t, wi_t, sp, ones_c, ones_l, tb):
    t_len, d = x.shape
    nb = t_len // tb
    n_tiles, tw = wa_t.shape[0], wa_t.shape[1]
    hd_c, hd_l = d // N_CONV_HEADS, d // N_LRU_HEADS
    s8 = SUBLANES

    def body(x_ref, tgt_ref, p_ref, wout_ref, wa_ref, wi_ref, sp_ref, oc_ref, ol_ref,
             h_ref, dh_ref, dhb_ref, acc_ref, yc, czs, u, pa, pi,
             rcf, rlf, ybuf, tail_z, tail_xl, hcar):
        i = pl.program_id(0)
        row = _row_iota(d)

        @pl.when(i == 0)
        def _():
            tail_z[...] = jnp.zeros_like(tail_z)
            tail_xl[...] = jnp.zeros_like(tail_xl)
            hcar[...] = jnp.zeros_like(hcar)
            acc_ref[...] = jnp.zeros_like(acc_ref)

        def spr(r):
            return sp_ref[r:r + 1, :]

        def proj(rows, seg):
            return p_ref[rows, seg * d:(seg + 1) * d].astype(F32)

        w0, w1, w2 = spr(SP_CONV_W), spr(SP_CONV_W + 1), spr(SP_CONV_W + 2)
        l0, l1, l2, l3 = spr(SP_LRU_W), spr(SP_LRU_W + 1), spr(SP_LRU_W + 2), spr(SP_LRU_W + 3)
        lb = spr(SP_LRU_B)

        def convs(r, carry):
            zp, xp = carry
            rows16 = pl.ds(r, 2 * s8)
            bg16, xl16 = proj(rows16, P_B), proj(rows16, P_XL)
            z16 = proj(rows16, P_C) * proj(rows16, P_XC)
            for j in range(2):
                rows, sub = pl.ds(r + j * s8, s8), slice(j * s8, (j + 1) * s8)
                z, xl = z16[sub], xl16[sub]
                cz = w0 * _shift_down(z, zp, 2, row) + w1 * _shift_down(z, zp, 1, row) + w2 * z
                czs[rows, :] = cz
                yc[rows, :] = bg16[sub] * cz
                u[rows, :] = (l0 * _shift_down(xl, xp, 3, row) + l1 * _shift_down(xl, xp, 2, row)
                              + l2 * _shift_down(xl, xp, 1, row) + l3 * xl + lb)
                zp, xp = z, xl
            return zp, xp

        z_last, xl_last = _chunks(tb, 2 * s8, convs, (tail_z[...], tail_xl[...]))
        tail_z[...] = z_last
        tail_xl[...] = xl_last

        ub = u[...].astype(MM)
        for k in range(n_tiles):
            sl = slice(k * tw, (k + 1) * tw)
            pa[:, sl] = _dot(ub[:, sl], wa_ref[k])
            pi[:, sl] = _dot(ub[:, sl], wi_ref[k])
        rcf[...] = _head_rstd(yc[...], oc_ref[...], hd_c)

        c8 = RG_LRU_C * _log_sigmoid(spr(SP_LAM))
        b_a, b_i = spr(SP_B_A), spr(SP_B_I)

        def lru(r, hp):
            rows = pl.ds(r, SUBLANES)
            ra = _sigmoid(pa[rows, :] + b_a)
            ii = _sigmoid(pi[rows, :] + b_i)
            pa[rows, :] = ra
            pi[rows, :] = ii
            la = ra * c8
            a = jnp.exp(la)
            mult = _lru_input_scale(la, a)
            h = _scan_fwd(a, mult * (ii * u[rows, :]), hp, row)
            h_ref[rows, :] = h
            return _bcast_row(h, SUBLANES - 1)

        hcar[...] = _chunks(tb, SUBLANES, lru, hcar[...])
        rlf[...] = _head_rstd(h_ref[...], ol_ref[...], hd_l)

        g_c, g_l = spr(SP_CONV_G), spr(SP_LRU_G)

        def gate(r, carry):
            rows = pl.ds(r, 2 * s8)
            gc, gl = proj(rows, P_GC), proj(rows, P_GL)
            ybuf[rows, 0:d] = (yc[rows, :] * rcf[rows, :] * g_c * (gc * _sigmoid(gc))).astype(MM)
            ybuf[rows, d:2 * d] = (h_ref[rows, :] * rlf[rows, :] * g_l * (gl * _sigmoid(gl))).astype(MM)
            return carry

        _chunks(tb, 2 * s8, gate, 0)

        hres = x_ref[...] + _dot(ybuf[...], wout_ref[...])
        rf = lax.rsqrt(jnp.mean(hres * hres, axis=-1, keepdims=True) + RMS_EPS)
        hn = hres * rf
        fg = spr(SP_FINAL_G)
        err = hn * fg - tgt_ref[...]
        dout = err * (1.0 / d)
        acc_ref[0:SUBLANES, :] += (err * err).reshape(tb // SUBLANES, SUBLANES, d).sum(axis=0)
        acc_ref[SUBLANES:2 * SUBLANES, :] += (dout * hn).reshape(tb // SUBLANES, SUBLANES, d).sum(axis=0)
        gd = dout * fg
        dhres = rf * (gd - hn * jnp.mean(gd * hn, axis=-1, keepdims=True))
        dh_ref[...] = dhres
        dhb_ref[...] = dhres.astype(MM)

    vm = pl.BlockSpec(memory_space=pltpu.VMEM)
    blk = lambda w: pl.BlockSpec((tb, w), lambda i: (i, 0))
    buf = pltpu.VMEM((tb, d), F32)
    car = pltpu.VMEM((SUBLANES, d), F32)
    return pl.pallas_call(
        body, name="forward", grid=(nb,),
        in_specs=[blk(d), blk(d), blk(6 * d), vm, vm, vm, vm, vm, vm],
        out_specs=(blk(d), blk(d), blk(d), pl.BlockSpec((2 * SUBLANES, d), lambda i: (0, 0))) + (blk(d),) * 5,
        out_shape=(jax.ShapeDtypeStruct((t_len, d), F32),
                   jax.ShapeDtypeStruct((t_len, d), F32),
                   jax.ShapeDtypeStruct((t_len, d), MM),
                   jax.ShapeDtypeStruct((2 * SUBLANES, d), F32))
                  + (jax.ShapeDtypeStruct((t_len, d), F32),) * 5,
        scratch_shapes=[buf] * 2 + [pltpu.VMEM((tb, 2 * d), MM), car, car, car],
        compiler_params=_params(dimension_semantics=("arbitrary",)),
    )(x, tgt, p, wout, wa_t, wi_t, sp, ones_c, ones_l)


def _backward(p, h, dh, saved, facc, wout, wa_t, wi_t, sp, ones_c, ones_l, tb):
    t_len, d = h.shape
    nb = t_len // tb
    n_tiles, tw = wa_t.shape[0], wa_t.shape[1]
    hd_c, hd_l = d // N_CONV_HEADS, d // N_LRU_HEADS
    g_rows = 2 * n_tiles * hd_l
    s8 = SUBLANES

    def body(p_ref, h_ref, hhalo_ref, dh_ref, yc, czs, u, ra_ref, ii_ref, facc_ref,
             wout_ref, wa_ref, wi_ref, sp_ref, oc_ref, ol_ref,
             dp_ref, yt_ref, slab_v, slab_g,
             hh, dy, ybuf, rcf, rlf, qc, ql, dyc_hat, dyl_hat, dpa, dpi, du, gwa_ref, gwi_ref, acc_ref,
             car_dcz, car_a, car_g, car_du):
        i = pl.program_id(0)
        blk_idx = nb - 1 - i
        row = _row_iota(d)

        @pl.when(i == 0)
        def _():
            for ref in (car_dcz, car_a, car_g, car_du, gwa_ref, gwi_ref, acc_ref):
                ref[...] = jnp.zeros_like(ref)

        def spr(r):
            return sp_ref[r:r + 1, :]

        def proj(rows, seg):
            return p_ref[rows, seg * d:(seg + 1) * d].astype(F32)

        def put(rows, seg, halves):
            dp_ref[rows, seg * d:(seg + 1) * d] = jnp.concatenate(halves, axis=0).astype(MM)

        def acc_add(group, val):
            acc_ref[group * s8:(group + 1) * s8, :] += val

        live = jnp.where(blk_idx > 0, 1.0, 0.0).astype(F32)
        hh[0:s8, :] = hhalo_ref[...] * live
        hh[s8:, :] = h_ref[...]

        dy[...] = _dot_nt(dh_ref[...].astype(MM), wout_ref[...])

        w0, w1, w2 = spr(SP_CONV_W), spr(SP_CONV_W + 1), spr(SP_CONV_W + 2)
        l0, l1, l2, l3 = spr(SP_LRU_W), spr(SP_LRU_W + 1), spr(SP_LRU_W + 2), spr(SP_LRU_W + 3)

        rcf[...] = _head_rstd(yc[...], oc_ref[...], hd_c)
        rlf[...] = _head_rstd(h_ref[...], ol_ref[...], hd_l)

        g_c, g_l = spr(SP_CONV_G), spr(SP_LRU_G)

        def gates(r, carry):
            rows = pl.ds(r, 2 * s8)
            for (seg, off_y, src, rstd, gain, q, dhat, grp) in (
                    (P_GC, 0, yc, rcf, g_c, qc, dyc_hat, A_CONV_G),
                    (P_GL, d, h_ref, rlf, g_l, ql, dyl_hat, A_LRU_G)):
                gt = proj(rows, seg)
                sg = _sigmoid(gt)
                silu = gt * sg
                yhat = src[rows, :] * rstd[rows, :]
                nrm = yhat * gain
                ybuf[rows, off_y:off_y + d] = nrm * silu
                dout = dy[rows, off_y:off_y + d]
                dnrm = dout * silu
                dp_ref[rows, seg * d:(seg + 1) * d] = (dout * nrm * (sg * (1.0 + gt * (1.0 - sg)))).astype(MM)
                dg = dnrm * yhat
                acc_add(grp, dg[0:s8] + dg[s8:])
                dh_ = dnrm * gain
                dhat[rows, :] = dh_
                q[rows, :] = dh_ * yhat
            return carry

        _chunks(tb, 2 * s8, gates, 0)

        qc[...] = _head_sums(qc[...], oc_ref[...]) * (1.0 / hd_c)
        ql[...] = _head_sums(ql[...], ol_ref[...]) * (1.0 / hd_l)
        yt_ref[...] = ybuf[...].T.astype(MM)

        c8 = RG_LRU_C * _log_sigmoid(spr(SP_LAM))

        def conv_mixer(r, dcz_n):
            rows16 = pl.ds(r, 2 * s8)
            bg16, cg16, xc16 = proj(rows16, P_B), proj(rows16, P_C), proj(rows16, P_XC)
            z16 = cg16 * xc16
            d_b, d_c, d_x = [None, None], [None, None], [None, None]
            for j in (1, 0):
                rows, sub = pl.ds(r + j * s8, s8), slice(j * s8, (j + 1) * s8)
                rstd = rcf[rows, :]
                yhat = yc[rows, :] * rstd
                dyc = rstd * (dyc_hat[rows, :] - yhat * qc[rows, :])
                d_b[j] = dyc * czs[rows, :]
                dcz = dyc * bg16[sub]
                up1, up2 = _shift_up(dcz, dcz_n, 1, row), _shift_up(dcz, dcz_n, 2, row)
                dz = w2 * dcz + w1 * up1 + w0 * up2
                d_c[j] = dz * xc16[sub]
                d_x[j] = dz * cg16[sub]
                z = z16[sub]
                acc_add(A_CONV_W, up2 * z)
                acc_add(A_CONV_W + 1, up1 * z)
                acc_add(A_CONV_W + 2, dcz * z)
                dcz_n = dcz
            put(rows16, P_B, d_b)
            put(rows16, P_C, d_c)
            put(rows16, P_XC, d_x)
            return dcz_n

        car_dcz[...] = _chunks(tb, 2 * s8, conv_mixer, car_dcz[...], reverse=True)

        def lru_mixer(r, carry):
            a_n, g_n = carry
            for j in (1, 0):
                rows = pl.ds(r + j * s8, s8)
                rstd = rlf[rows, :]
                hcur = hh[pl.ds(r + (j + 1) * s8, s8), :]
                hhat = hcur * rstd
                dh_out = rstd * (dyl_hat[rows, :] - hhat * ql[rows, :])
                ra = ra_ref[rows, :]
                la = ra * c8
                a = jnp.exp(la)
                g = _scan_bwd(_shift_up(a, a_n, 1, row), dh_out, g_n, row)
                da = g * _shift_down(hcur, hh[pl.ds(r + j * s8, s8), :], 1, row)
                ii = ii_ref[rows, :]
                uu = u[rows, :]
                mult = _lru_input_scale(la, a)
                dmult = g * (ii * uu)
                ds = g * mult
                dla = a * (da - dmult * a / mult)
                acc_add(A_LAM, dla * ra)
                dpa_ = dla * c8 * ra * (1.0 - ra)
                dpi_ = ds * uu * ii * (1.0 - ii)
                acc_add(A_B_A, dpa_)
                acc_add(A_B_I, dpi_)
                dpa[rows, :] = dpa_
                dpi[rows, :] = dpi_
                du[rows, :] = ds * ii
                a_n, g_n = a, _bcast_row(g, 0)
            return a_n, g_n

        a_f, g_f = _chunks(tb, 2 * s8, lru_mixer, (car_a[...], car_g[...]), reverse=True)
        car_a[...] = a_f
        car_g[...] = g_f

        dpab = dpa[...].astype(MM)
        dpib = dpi[...].astype(MM)
        for k in range(n_tiles):
            sl = slice(k * tw, (k + 1) * tw)
            du[:, sl] += _dot_nt(dpab[:, sl], wa_ref[k]) + _dot_nt(dpib[:, sl], wi_ref[k])
            ut = u[:, sl].T.astype(MM)
            gwa_ref[k] += _dot(ut, dpab[:, sl])
            gwi_ref[k] += _dot(ut, dpib[:, sl])

        def lru_conv(r, du_n):
            rows16 = pl.ds(r, 2 * s8)
            xl16 = proj(rows16, P_XL)
            d_xl = [None, None]
            for j in (1, 0):
                rows, sub = pl.ds(r + j * s8, s8), slice(j * s8, (j + 1) * s8)
                dut = du[rows, :]
                up1, up2, up3 = (_shift_up(dut, du_n, s, row) for s in (1, 2, 3))
                d_xl[j] = l3 * dut + l2 * up1 + l1 * up2 + l0 * up3
                xl = xl16[sub]
                acc_add(A_LRU_W, up3 * xl)
                acc_add(A_LRU_W + 1, up2 * xl)
                acc_add(A_LRU_W + 2, up1 * xl)
                acc_add(A_LRU_W + 3, dut * xl)
                acc_add(A_LRU_B, dut)
                du_n = dut
            put(rows16, P_XL, d_xl)
            return du_n

        car_du[...] = _chunks(tb, 2 * s8, lru_conv, car_du[...], reverse=True)

        @pl.when(i == nb - 1)
        def _():
            def rowsum(ref, group):
                return jnp.sum(ref[group * s8:(group + 1) * s8, :], axis=0, keepdims=True)

            slab_v[...] = jnp.zeros_like(slab_v)
            loss = jnp.sum(rowsum(facc_ref, 0), axis=1, keepdims=True) * (0.5 / d)
            rows = {SL_LOSS: jnp.broadcast_to(loss, (1, d)), SL_FINAL_G: rowsum(facc_ref, 1),
                    SL_LRU_B: rowsum(acc_ref, A_LRU_B), SL_B_A: rowsum(acc_ref, A_B_A), SL_B_I: rowsum(acc_ref, A_B_I),
                    SL_LAM: rowsum(acc_ref, A_LAM), SL_CONV_G: rowsum(acc_ref, A_CONV_G), SL_LRU_G: rowsum(acc_ref, A_LRU_G)}
            for k in range(3):
                rows[SL_CONV_W + k] = rowsum(acc_ref, A_CONV_W + k)
            for k in range(4):
                rows[SL_LRU_W + k] = rowsum(acc_ref, A_LRU_W + k)
            for r, val in rows.items():
                slab_v[r:r + 1, :] = val
            head_of_lane = lax.broadcasted_iota(jnp.int32, (hd_l, tw), 1) // hd_l
            for mtx, g_ref in enumerate((gwa_ref, gwi_ref)):
                for k in range(n_tiles):
                    packed = jnp.zeros((hd_l, tw), F32)
                    for a in range(tw // hd_l):
                        packed = jnp.where(head_of_lane == a, g_ref[k, a * hd_l:(a + 1) * hd_l, :], packed)
                    slab_g[(mtx * n_tiles + k) * hd_l:(mtx * n_tiles + k + 1) * hd_l, :] = packed.astype(MM)

    vm = pl.BlockSpec(memory_space=pltpu.VMEM)
    rev = lambda w: pl.BlockSpec((tb, w), lambda i: (nb - 1 - i, 0))
    halo = lambda rows, w: pl.BlockSpec((rows, w), lambda i: (jnp.maximum((nb - 1 - i) * (tb // rows) - 1, 0), 0))
    const = lambda shape: pl.BlockSpec(shape, lambda i: (0,) * len(shape))
    buf = lambda w: pltpu.VMEM((tb, w), F32)
    car = pltpu.VMEM((SUBLANES, d), F32)
    return pl.pallas_call(
        body, name="backward", grid=(nb,),
        in_specs=[rev(6 * d), rev(d), halo(SUBLANES, d), rev(d)] + [rev(d)] * 5 + [vm, vm, vm, vm, vm, vm, vm],
        out_specs=(rev(6 * d), pl.BlockSpec((2 * d, tb), lambda i: (0, nb - 1 - i)),
                   const((SL_ROWS, d)), const((g_rows, tw))),
        out_shape=(jax.ShapeDtypeStruct((t_len, 6 * d), MM),
                   jax.ShapeDtypeStruct((2 * d, t_len), MM),
                   jax.ShapeDtypeStruct((SL_ROWS, d), F32),
                   jax.ShapeDtypeStruct((g_rows, tw), MM)),
        scratch_shapes=[pltpu.VMEM((SUBLANES + tb, d), F32), buf(2 * d), buf(2 * d)] + [buf(d)] * 9
                       + [pltpu.VMEM((n_tiles, tw, tw), F32), pltpu.VMEM((n_tiles, tw, tw), F32),
                          pltpu.VMEM((A_GROUPS * SUBLANES, d), F32), car, car, car, car],
        compiler_params=_params(dimension_semantics=("arbitrary",)),
    )(p, h, h, dh, *saved, facc, wout, wa_t, wi_t, sp, ones_c, ones_l)


def _input_grad(dp, win_all, x, dh, sp, parts, tb):
    t_len, d = x.shape
    nb = t_len // tb
    cols = win_all.shape[2]
    n_parts = len(parts)

    def body(dp_ref, win_ref, x_ref, dh_ref, sp_ref, *refs):
        gx_ref, ln_ref = refs[n_parts:n_parts + 2]
        send_sems, recv_sems, acc_ref, ln_all, ln_send, ln_recv = refs[2 * n_parts + 2:]
        exchange = _ChipExchange(refs[:n_parts], refs[n_parts + 2:2 * n_parts + 2], send_sems, recv_sems)
        i = pl.program_id(0)

        @pl.when(i == 0)
        def _():
            acc_ref[...] = jnp.zeros_like(acc_ref)
            exchange.start()

        dxn = _dot_nt(dp_ref[:, 0:cols], win_ref[0])
        for j in range(1, N_DEV):
            dxn += _dot_nt(dp_ref[:, j * cols:(j + 1) * cols], win_ref[j])
        xv = x_ref[...]
        r0 = lax.rsqrt(jnp.mean(xv * xv, axis=-1, keepdims=True) + RMS_EPS)
        xhat = xv * r0
        acc_ref[...] += (dxn * xhat).reshape(tb // SUBLANES, SUBLANES, d).sum(axis=0)
        dxh = dxn * sp_ref[SP_LN_G:SP_LN_G + 1, :]
        gx_ref[...] = dh_ref[...] + r0 * (dxh - xhat * jnp.mean(dxh * xhat, axis=-1, keepdims=True))

        @pl.when(i == nb - 1)
        def _():
            exchange.finish()
            x_, y_, c_ = _mesh_pos()
            ln_all[4 * x_ + 2 * y_ + c_] = jnp.broadcast_to(jnp.sum(acc_ref[...], axis=0, keepdims=True), acc_ref.shape)
            gather = _Gather(lambda a, px, py, pc: ln_all.at[4 * px + 2 * py + pc], ln_send, ln_recv)
            gather.start_own(0)
            gather.finish(0)
            total = ln_all[0]
            for dev in range(1, N_DEV):
                total = total + ln_all[dev]
            ln_ref[...] = total

    vm = pl.BlockSpec(memory_space=pltpu.VMEM)
    hbm = pl.BlockSpec(memory_space=pl.ANY)
    blk = lambda w: pl.BlockSpec((tb, w), lambda i: (i, 0))
    outs = pl.pallas_call(
        body, name="input_grad", grid=(nb,),
        in_specs=[blk(6 * d), vm, blk(d), blk(d), vm] + [hbm] * n_parts,
        out_specs=(blk(d), pl.BlockSpec((SUBLANES, d), lambda i: (0, 0))) + (hbm,) * n_parts,
        out_shape=(jax.ShapeDtypeStruct((t_len, d), F32), jax.ShapeDtypeStruct((SUBLANES, d), F32))
                  + tuple(jax.ShapeDtypeStruct(p.shape, p.dtype) for p in parts),
        scratch_shapes=[pltpu.SemaphoreType.DMA((3 * n_parts,)), pltpu.SemaphoreType.DMA((3 * n_parts,)),
                        pltpu.VMEM((SUBLANES, d), F32), pltpu.VMEM((N_DEV, SUBLANES, d), F32),
                        pltpu.SemaphoreType.DMA((7,)), pltpu.SemaphoreType.DMA((7,))],
        compiler_params=_params(dimension_semantics=("arbitrary",)),
    )(dp, win_all, x, dh, sp, *parts)
    return outs[0], outs[1], outs[2:]


_CHIP_RELATIONS = [(0, 0), (1, 0), (0, 1), (1, 1)]


def _related_block(k, core):
    x, y, _ = _mesh_pos()
    fx, fy = _CHIP_RELATIONS[k]
    return 4 * (x ^ fx) + 2 * (y ^ fy) + core


class _ChipExchange:
    def __init__(self, part_refs, land_refs, send_sems, recv_sems):
        self.part_refs, self.land_refs, self.send_sems, self.recv_sems = part_refs, land_refs, send_sems, recv_sems

    def copies(self):
        x, y, c = _mesh_pos()
        for a in range(len(self.part_refs)):
            for k in (1, 2, 3):
                fx, fy = _CHIP_RELATIONS[k]
                yield pltpu.make_async_remote_copy(
                    src_ref=self.part_refs[a].at[k - 1], dst_ref=self.land_refs[a].at[k - 1],
                    send_sem=self.send_sems.at[3 * a + k - 1], recv_sem=self.recv_sems.at[3 * a + k - 1],
                    device_id=(x ^ fx, y ^ fy, c), device_id_type=MESH)

    def start(self):
        for cp in self.copies():
            cp.start()

    def finish(self):
        for cp in self.copies():
            cp.wait_recv()
        for cp in self.copies():
            cp.wait_send()


def _weight_grad_stage1(name, blk_shape, n_split, operands, in_specs, product, riders=(), slabs=()):
    n_rows, n_cols = blk_shape
    rs = n_rows // n_split
    rc = 32
    n_in, n_ride, n_slab = len(operands), len(riders), len(slabs)
    _, _, c = _mesh_pos()
    order = jnp.stack([_related_block(k, 1 - c) for k in range(4)]
                      + [_related_block(k, c) for k in (1, 2, 3, 0)]).astype(jnp.int32)

    def body(order_ref, *refs):
        ins = refs[:n_in]
        ride_in = refs[n_in:n_in + n_ride]
        slab_in = refs[n_in + n_ride:n_in + n_ride + n_slab]
        n_op = n_in + n_ride + n_slab
        part_ref, own_ref = refs[n_op:n_op + 2]
        ride_out = refs[n_op + 2:n_op + 2 + n_ride]
        gathered = refs[n_op + 2 + n_ride:n_op + 2 + n_ride + n_slab]
        (gbuf, sendbuf, from_sib, send_sems, recv_sems, ride_send, ride_recv,
         slab_send, slab_recv, slab_local) = refs[n_op + 2 + n_ride + n_slab:]
        exchange = _ChipExchange(ride_in, ride_out, ride_send, ride_recv)
        s = pl.program_id(0)
        x, y, c = _mesh_pos()
        me = 4 * x + 2 * y + c
        gather = _BalancedGather(lambda a, px, py, pc: gathered[a].at[4 * px + 2 * py + pc], slab_send, slab_recv, slab_in)
        keep_own = [pltpu.make_async_copy(slab_in[a], gathered[a].at[me], slab_local.at[a]) for a in range(n_slab)]

        def to_sibling(k):
            return pltpu.make_async_remote_copy(
                src_ref=sendbuf.at[k], dst_ref=from_sib.at[k], send_sem=send_sems.at[k], recv_sem=recv_sems.at[k],
                device_id=(x, y, 1 - c), device_id_type=MESH)

        @pl.when(s == 0)
        def _():
            exchange.start()
            for a in range(n_slab):
                gather.start_own(a)
                keep_own[a].start()

        @pl.when(s == 3)
        def _():
            for a in range(n_slab):
                gather.on_neighbour(a, 0)
                gather.on_neighbour(a, 1)

        @pl.when(s == 6)
        def _():
            for a in range(n_slab):
                gather.on_diagonal(a)

        for h in range(n_split):
            gbuf[h * rs:(h + 1) * rs, :] = product(ins, h)

        @pl.when(s < 4)
        def _():
            def narrow(r, carry):
                sendbuf[s, pl.ds(r, rc), :] = gbuf[pl.ds(r, rc), :].astype(MM)
                return carry

            _chunks(n_rows, rc, narrow, 0)
            to_sibling(s).start()

        @pl.when(s >= 4)
        def _():
            k = jnp.where(s == 7, 0, s - 3)
            to_sibling(k).wait_recv()

            @pl.when(s < 7)
            def _():
                def add(r, carry):
                    rows = pl.ds(r, rc)
                    part_ref[0, rows, :] = (gbuf[rows, :] + from_sib[k, rows, :].astype(F32)).astype(MM)
                    return carry

                _chunks(n_rows, rc, add, 0)

            @pl.when(s == 7)
            def _():
                def add(r, carry):
                    rows = pl.ds(r, rc)
                    own_ref[rows, :] = gbuf[rows, :] + from_sib[0, rows, :].astype(F32)
                    return carry

                _chunks(n_rows, rc, add, 0)
                for kk in range(4):
                    to_sibling(kk).wait_send()
                exchange.finish()
                for a in range(n_slab):
                    gather.wait_sibling(a)
                    for j in range(3):
                        gather.wait_passed_on(a, j)
                    gather.wait_sends(a)
                    keep_own[a].wait()

    hbm = pl.BlockSpec(memory_space=pl.ANY)
    grid_spec = pltpu.PrefetchScalarGridSpec(
        num_scalar_prefetch=1, grid=(N_DEV,), in_specs=list(in_specs) + [hbm] * (n_ride + n_slab),
        out_specs=(pl.BlockSpec((1, n_rows, n_cols), lambda s, o: (jnp.clip(s - 4, 0, 2), 0, 0)),
                   pl.BlockSpec((n_rows, n_cols), lambda s, o: (0, 0))) + (hbm,) * (n_ride + n_slab),
        scratch_shapes=[pltpu.VMEM((n_rows, n_cols), F32), pltpu.VMEM((4, n_rows, n_cols), MM),
                        pltpu.VMEM((4, n_rows, n_cols), MM),
                        pltpu.SemaphoreType.DMA((4,)), pltpu.SemaphoreType.DMA((4,)),
                        pltpu.SemaphoreType.DMA((max(3 * n_ride, 1),)), pltpu.SemaphoreType.DMA((max(3 * n_ride, 1),)),
                        pltpu.SemaphoreType.DMA((max(8 * n_slab, 1),)), pltpu.SemaphoreType.DMA((max(8 * n_slab, 1),)),
                        pltpu.SemaphoreType.DMA((max(n_slab, 1),))])
    outs = pl.pallas_call(
        body, name=name, grid_spec=grid_spec,
        out_shape=(jax.ShapeDtypeStruct((3, n_rows, n_cols), MM), jax.ShapeDtypeStruct((n_rows, n_cols), F32))
                  + tuple(jax.ShapeDtypeStruct(p.shape, p.dtype) for p in riders)
                  + tuple(jax.ShapeDtypeStruct((N_DEV,) + a.shape, a.dtype) for a in slabs),
        compiler_params=_params(dimension_semantics=("arbitrary",)),
    )(order, *operands, *riders, *slabs)
    return outs[0], outs[1], outs[2:2 + n_ride], outs[2 + n_ride:]


def _weight_grad_in(xnt, dp, riders, slabs):
    d, t_len = xnt.shape
    cols = dp.shape[1] // N_DEV
    half = d // 2
    return _weight_grad_stage1(
        "weight_grad_in", (d, cols), 2, (xnt, dp),
        [pl.BlockSpec(memory_space=pltpu.VMEM), pl.BlockSpec((t_len, cols), lambda s, o: (0, o[s]))],
        lambda refs, h: _dot(refs[0][h * half:(h + 1) * half, :], refs[1][...]), riders, slabs)


def _weight_grad_out(yt, dhb):
    d2, t_len = yt.shape
    d = dhb.shape[1]
    rows = d2 // N_DEV
    return _weight_grad_stage1(
        "weight_grad_out", (rows, d), 1, (yt, dhb),
        [pl.BlockSpec((rows, t_len), lambda s, o: (o[s], 0)), pl.BlockSpec(memory_space=pltpu.VMEM)],
        lambda refs, h: _dot(refs[0][...], refs[1][...]))


def _update_shard(own, from_chips, w, m, v, name):
    n_rows, n_cols = w.shape
    rb = min(256, n_rows)

    def body(own_ref, fc_ref, w_ref, m_ref, v_ref, grad_ref, delta_ref, mo_ref, vo_ref):
        g = own_ref[...]
        for k in range(3):
            g = g + fc_ref[k].astype(F32)
        delta, m_new, v_new = _adamw(w_ref[...], g, m_ref[...], v_ref[...])
        grad_ref[...] = g
        delta_ref[...] = delta
        mo_ref[...] = m_new
        vo_ref[...] = v_new

    blk = pl.BlockSpec((rb, n_cols), lambda i: (i, 0))
    out = jax.ShapeDtypeStruct((n_rows, n_cols), F32)
    return pl.pallas_call(
        body, name=name, grid=(n_rows // rb,),
        in_specs=[blk, pl.BlockSpec((3, rb, n_cols), lambda i: (0, i, 0)), blk, blk, blk],
        out_specs=(blk, blk, blk, blk), out_shape=(out, out, out, out),
        compiler_params=_params(dimension_semantics=("arbitrary",)),
    )(own, from_chips, w, m, v)


def _small_update(gat_v, gat_g, ln_tot, vec_w, vec_m, vec_v, gates, convs):
    n_vec = len(vec_w)
    n_heads, hd, _ = gates[0].shape
    tw = gat_g.shape[2]
    s8 = SUBLANES
    per = tw // hd
    n_tiles = n_heads // per
    cc = convs[0].shape[1]
    n_in = 3 + 3 * n_vec + 12

    def body(*refs):
        gv_ref, gg_ref, ln_ref = refs[:3]
        w_refs, m_refs, v_refs = (refs[3 + j * n_vec:3 + (j + 1) * n_vec] for j in range(3))
        gate_refs = refs[3 + 3 * n_vec:3 + 3 * n_vec + 6]
        conv_refs = refs[3 + 3 * n_vec + 6:n_in]
        loss_o = refs[n_in]
        kinds = [refs[n_in + 1 + j * (n_vec + 4):n_in + 1 + (j + 1) * (n_vec + 4)] for j in range(4)]
        tv, tg = refs[n_in + 1 + 4 * (n_vec + 4):]
        x, y, c = _mesh_pos()
        me = 4 * x + 2 * y + c

        def emit(k_out, w, g, m, v):
            delta, m_new, v_new = _adamw(w, g, m, v)
            for ref, val in zip(k_out, (g, delta, m_new, v_new)):
                ref[...] = val

        total = gv_ref[0]
        for dev in range(1, N_DEV):
            total = total + gv_ref[dev]
        tv[...] = total
        tv[SL_LN_G:SL_LN_G + 1, :] = ln_ref[0:1, :]

        def sum_gates(r, carry):
            rows = pl.ds(r, 2 * s8)
            part = gg_ref[0, rows, :].astype(F32)
            for dev in range(1, N_DEV):
                part = part + gg_ref[dev, rows, :].astype(F32)
            tg[rows, :] = part
            return carry

        _chunks(tg.shape[0], 2 * s8, sum_gates, 0)
        loss_o[...] = jnp.broadcast_to(tv[SL_LOSS:SL_LOSS + 1, 0:LANES], loss_o.shape)
        for p in range(n_vec):
            w, g = w_refs[p][...], tv[SL_LN_G + p, :]
            if SL_LN_G + p == SL_LAM:
                g = g * (RG_LRU_C * jax.nn.sigmoid(-w))
            emit([k_out[p] for k_out in kinds], w, g, m_refs[p][...], v_refs[p][...])
        lanes = pl.ds(pl.multiple_of(me * cc, cc), cc)
        for j, (row0, n) in enumerate(((SL_CONV_W, 3), (SL_LRU_W, 4))):
            w_ref, m_ref, v_ref = conv_refs[3 * j:3 * j + 3]
            emit([k_out[n_vec + 2 + j] for k_out in kinds], w_ref[...], tv[row0:row0 + n, lanes], m_ref[...], v_ref[...])
        for mtx in range(2):
            w_ref, m_ref, v_ref = gate_refs[3 * mtx:3 * mtx + 3]
            for k in range(n_tiles):
                tile = tg[(mtx * n_tiles + k) * hd:(mtx * n_tiles + k + 1) * hd, :]
                for a in range(per):
                    head = k * per + a
                    g = tile[:, a * hd:(a + 1) * hd]
                    delta, m_new, v_new = _adamw(w_ref[head], g, m_ref[head], v_ref[head])
                    for k_out, val in zip(kinds, (g, delta, m_new, v_new)):
                        k_out[n_vec + mtx][head] = val

    vm = pl.BlockSpec(memory_space=pltpu.VMEM)
    like = lambda a: jax.ShapeDtypeStruct(a.shape, F32)
    per_kind = tuple(like(a) for a in vec_w) + (like(gates[0]), like(gates[3]), like(convs[0]), like(convs[3]))
    n_out = 1 + 4 * len(per_kind)
    outs = pl.pallas_call(
        body, name="small_update",
        in_specs=[vm] * n_in, out_specs=(vm,) * n_out,
        out_shape=(jax.ShapeDtypeStruct((SUBLANES, LANES), F32),) + per_kind * 4,
        scratch_shapes=[pltpu.VMEM(gat_v.shape[1:], F32), pltpu.VMEM(gat_g.shape[1:], F32)],
        compiler_params=_params(),
    )(gat_v, gat_g, ln_tot, *vec_w, *vec_m, *vec_v, *gates, *convs)
    return outs[0], [outs[1 + j * len(per_kind):1 + (j + 1) * len(per_kind)] for j in range(4)]


def _head_ones(head_dim, tw):
    lane = jnp.arange(tw) // head_dim
    return (lane[:, None] == lane[None, :]).astype(MM)


def _gate_tiles(w, tw):
    n_heads, hd, _ = w.shape
    per = tw // hd
    w4 = w.reshape(n_heads // per, per, hd, hd)
    eye = jnp.eye(per, dtype=w.dtype)
    return (w4[:, :, :, None, :] * eye[None, :, None, :, None]).reshape(n_heads // per, tw, tw)


def kernel(x, ln_g, w_in, conv_w, lru_conv_w, lru_conv_b, w_a, b_a, w_i, b_i, lam, conv_out_g, lru_out_g, w_out, final_g, loss_target, m_ln_g, m_w_in, m_conv_w, m_lru_conv_w, m_lru_conv_b, m_w_a, m_b_a, m_w_i, m_b_i, m_lam, m_conv_out_g, m_lru_out_g, m_w_out, m_final_g, v_ln_g, v_w_in, v_conv_w, v_lru_conv_w, v_lru_conv_b, v_w_a, v_b_a, v_w_i, v_b_i, v_lam, v_conv_out_g, v_lru_out_g, v_w_out, v_final_g):
    _, t_len, d = x.shape
    hd_l = d // N_LRU_HEADS
    tw = min(MXU_TILE, d)
    cc = conv_w.shape[1]
    x2, tgt2 = x[0], loss_target[0]

    def conv_rows(cw3, lw4):
        return jnp.concatenate([jnp.zeros((1, cc), F32), cw3, lw4], axis=0)

    p, xnt, win_all, wout_all, conv_all = _gather_project(
        x2, w_in, w_out, conv_rows(conv_w, lru_conv_w), ln_g.reshape(1, d), min(256, t_len))
    wout_full = wout_all.reshape(N_DEV * w_out.shape[0], d)
    conv_full = conv_all.transpose(1, 0, 2).reshape(SUBLANES, d)
    small = [ln_g, lru_conv_b, b_a, b_i, lam, conv_out_g, lru_out_g, final_g]
    sp = jnp.concatenate([jnp.stack(small), conv_full[1:], jnp.zeros((1, d), F32)], axis=0)
    wa_t, wi_t = _gate_tiles(w_a, tw).astype(MM), _gate_tiles(w_i, tw).astype(MM)
    ones_c, ones_l = _head_ones(d // N_CONV_HEADS, tw), _head_ones(hd_l, tw)

    h, dh, dhb, facc, *saved = _forward(x2, tgt2, p, wout_full, wa_t, wi_t, sp, ones_c, ones_l, min(256, t_len))
    dp, yt, slab_v, slab_g = _backward(p, h, dh, saved, facc, wout_full, wa_t, wi_t, sp, ones_c, ones_l, min(256, t_len))
    part_out, own_out, _, _ = _weight_grad_out(yt, dhb)
    part_in, own_in, (chips_out,), (gat_v, gat_g) = _weight_grad_in(xnt, dp, (part_out,), (slab_v, slab_g))
    grad_x, ln_tot, (chips_in,) = _input_grad(dp, win_all, x2, dh, sp, (part_in,), min(512, t_len))
    gw_in, dw_in, mw_in, vw_in = _update_shard(own_in, chips_in, w_in, m_w_in, v_w_in, "update_w_in")
    gw_out, dw_out, mw_out, vw_out = _update_shard(own_out, chips_out, w_out, m_w_out, v_w_out, "update_w_out")

    loss_tile, kinds = _small_update(
        gat_v, gat_g, ln_tot, small,
        [m_ln_g, m_lru_conv_b, m_b_a, m_b_i, m_lam, m_conv_out_g, m_lru_out_g, m_final_g],
        [v_ln_g, v_lru_conv_b, v_b_a, v_b_i, v_lam, v_conv_out_g, v_lru_out_g, v_final_g],
        (w_a, m_w_a, v_w_a, w_i, m_w_i, v_w_i), (conv_w, m_conv_w, v_conv_w, lru_conv_w, m_lru_conv_w, v_lru_conv_w))

    def unpack(kind, big_in, big_out):
        vec, (wa_, wi_, cw_, lw_) = kind[:len(small)], kind[len(small):]
        return [vec[0], big_in, cw_, lw_, vec[1], wa_, vec[2], wi_, vec[3], vec[4], vec[5], vec[6], big_out, vec[7]]

    return (loss_tile[0, 0], grad_x[None], *unpack(kinds[0], gw_in, gw_out), *unpack(kinds[1], dw_in, dw_out),
            *unpack(kinds[2], mw_in, mw_out), *unpack(kinds[3], vw_in, vw_out))
```

```python
import functools

import jax
import jax.numpy as jnp
from jax import lax
from jax.experimental import pallas as pl
from jax.experimental.pallas import tpu as pltpu

F32 = jnp.float32
MM = jnp.bfloat16
MESH = pl.DeviceIdType.MESH

N_DEV = 8
N_CONV_HEADS = 8
N_LRU_HEADS = 16
RG_LRU_C = 8.0
RMS_EPS = 1e-6
ADAM_LR, ADAM_B1, ADAM_B2, ADAM_EPS, ADAM_WD, ADAM_STEP = 0.001, 0.9, 0.999, 1e-08, 0.01, 10
ADAM_BC1 = 1.0 - ADAM_B1 ** ADAM_STEP
ADAM_BC2 = 1.0 - ADAM_B2 ** ADAM_STEP

SUBLANES = 8
LANES = 128
MXU_TILE = 256
VMEM_LIMIT = 56 * 1024 * 1024

SP_LN_G, SP_LRU_B, SP_B_A, SP_B_I, SP_LAM, SP_CONV_G, SP_LRU_G, SP_FINAL_G, SP_CONV_W, SP_LRU_W = 0, 1, 2, 3, 4, 5, 6, 7, 8, 11
SP_ROWS = 16
P_B, P_C, P_XC, P_GC, P_XL, P_GL = 0, 1, 2, 3, 4, 5
A_CONV_G, A_LRU_G, A_LAM, A_B_A, A_B_I, A_CONV_W, A_LRU_W, A_LRU_B = 0, 1, 2, 3, 4, 5, 8, 12
A_GROUPS = 13
SL_LOSS, SL_LN_G, SL_LRU_B, SL_B_A, SL_B_I, SL_LAM, SL_CONV_G, SL_LRU_G, SL_FINAL_G, SL_CONV_W, SL_LRU_W = 0, 1, 2, 3, 4, 5, 6, 7, 8, 16, 24
SL_ROWS = 32


def _params(vmem=True, **kw):
    if vmem:
        kw["vmem_limit_bytes"] = VMEM_LIMIT
    return pltpu.CompilerParams(**kw)


def _dot(a, b):
    return jnp.dot(a, b, preferred_element_type=F32)


def _dot_nt(a, b):
    return lax.dot_general(a, b, (((1,), (1,)), ((), ())), preferred_element_type=F32)


def _head_sums(v, ones_tile):
    tw = ones_tile.shape[0]
    vb = v.astype(MM)
    return jnp.concatenate([_dot(vb[:, k:k + tw], ones_tile) for k in range(0, v.shape[1], tw)], axis=1)


def _head_rstd(v, ones_tile, head_dim):
    return lax.rsqrt(_head_sums(v * v, ones_tile) * (1.0 / head_dim) + RMS_EPS)


def _sigmoid(x):
    return 0.5 * jnp.tanh(0.5 * x) + 0.5


def _lru_input_scale(log_a, a):
    return jnp.sqrt(-jnp.tanh(log_a) * (1.0 + a * a))


def _log_sigmoid(x):
    z = jnp.exp(-jnp.abs(x))
    u = 1.0 + z
    log1p_z = jnp.where(u == 1.0, z, jnp.log(u) * (z / (u - 1.0)))
    return jnp.minimum(x, 0.0) - log1p_z


def _row_iota(d):
    return lax.broadcasted_iota(jnp.int32, (SUBLANES, d), 0)


def _shift_down(cur, prev, s, row):
    return jnp.where(row >= s, pltpu.roll(cur, s, axis=0), pltpu.roll(prev, s, axis=0))


def _shift_up(cur, nxt, s, row):
    k = SUBLANES - s
    return jnp.where(row < k, pltpu.roll(cur, k, axis=0), pltpu.roll(nxt, k, axis=0))


def _scan_fwd(a, b, h_prev, row):
    for s in (1, 2, 4):
        a_s = jnp.where(row >= s, pltpu.roll(a, s, axis=0), 1.0)
        b_s = jnp.where(row >= s, pltpu.roll(b, s, axis=0), 0.0)
        b = a * b_s + b
        a = a * a_s
    return a * h_prev + b


def _scan_bwd(a_next, b, g_next, row):
    a = a_next
    for s in (1, 2, 4):
        k = SUBLANES - s
        a_s = jnp.where(row < k, pltpu.roll(a, k, axis=0), 1.0)
        b_s = jnp.where(row < k, pltpu.roll(b, k, axis=0), 0.0)
        b = a * b_s + b
        a = a * a_s
    return a * g_next + b


def _bcast_row(v, r):
    return jnp.broadcast_to(v[r:r + 1, :], v.shape)


def _chunks(n_rows, rc, body, init, reverse=False):
    n = n_rows // rc

    def step(i, carry):
        j = (n - 1 - i) if reverse else i
        return body(pl.multiple_of(j * rc, rc), carry)

    return lax.fori_loop(0, n, step, init)


def _adamw(w, g, m, v):
    m = ADAM_B1 * m + (1.0 - ADAM_B1) * g
    v = ADAM_B2 * v + (1.0 - ADAM_B2) * (g * g)
    m_hat = m / ADAM_BC1
    v_hat = v / ADAM_BC2
    delta = -ADAM_LR * (m_hat / (jnp.sqrt(v_hat) + ADAM_EPS) + ADAM_WD * w)
    return delta, m, v


def _mesh_pos():
    return lax.axis_index("x"), lax.axis_index("y"), lax.axis_index("c")


class _Gather:
    def __init__(self, blocks_of, send_sems, recv_sems, own_src=None):
        x, y, c = _mesh_pos()
        self.c = c
        self.me, self.sibling = (x, y, c), (x, y, 1 - c)
        self.chips = [(1 - x, y), (x, 1 - y), (1 - x, 1 - y)]
        self.blocks_of, self.send_sems, self.recv_sems = blocks_of, send_sems, recv_sems
        self.own_src = own_src

    def copy(self, a, k, block, to):
        src = self.blocks_of(a, *block)
        if block is self.me and self.own_src is not None:
            src = self.own_src[a]
        return pltpu.make_async_remote_copy(
            src_ref=src, dst_ref=self.blocks_of(a, *block),
            send_sem=self.send_sems.at[a * 7 + k], recv_sem=self.recv_sems.at[a * 7 + k],
            device_id=to, device_id_type=MESH)

    def start_own(self, a):
        self.copy(a, 0, self.me, self.sibling).start()
        for j, chip in enumerate(self.chips):
            self.copy(a, 1 + j, self.me, (*chip, self.c)).start()

    def wait_sibling(self, a):
        self.copy(a, 0, self.sibling, self.me).wait_recv()

    def wait_chip_and_pass_on(self, a, j):
        block = (*self.chips[j], self.c)
        self.copy(a, 1 + j, block, self.me).wait_recv()
        self.copy(a, 4 + j, block, self.sibling).start()

    def wait_passed_on(self, a, j):
        self.copy(a, 4 + j, (*self.chips[j], 1 - self.c), self.me).wait_recv()

    def wait_sends(self, a):
        self.copy(a, 0, self.me, self.sibling).wait_send()
        for j, chip in enumerate(self.chips):
            self.copy(a, 1 + j, self.me, (*chip, self.c)).wait_send()
            self.copy(a, 4 + j, (*chip, self.c), self.sibling).wait_send()

    def finish(self, a):
        for j in range(3):
            self.wait_chip_and_pass_on(a, j)
        self.wait_sibling(a)
        for j in range(3):
            self.wait_passed_on(a, j)
        self.wait_sends(a)


class _BalancedGather:
    def __init__(self, slot, send_sems, recv_sems, own_src):
        x, y, c = _mesh_pos()
        self.c = c
        self.me, self.sibling = (x, y, c), (x, y, 1 - c)
        self.chips = [(1 - x, y), (x, 1 - y), (1 - x, 1 - y)]
        self.slot, self.send_sems, self.recv_sems, self.own_src = slot, send_sems, recv_sems, own_src

    def half(self, a, block, which):
        ref = self.slot(a, *block)
        n = ref.shape[0] // 2
        return ref.at[pl.ds(which * n, n)]

    def copy(self, a, k, src, dst, to):
        return pltpu.make_async_remote_copy(
            src_ref=src, dst_ref=dst, send_sem=self.send_sems.at[a * 8 + k], recv_sem=self.recv_sems.at[a * 8 + k],
            device_id=to, device_id_type=MESH)

    def whole(self, a, k, block, to):
        src = self.own_src[a] if block is self.me else self.slot(a, *block)
        return self.copy(a, k, src, self.slot(a, *block), to)

    def halved(self, a, k, block, which, to):
        return self.copy(a, k, self.half(a, block, which), self.half(a, block, which), to)

    def on(self, chip):
        return (*self.chips[chip], self.c)

    def start_own(self, a):
        self.whole(a, 0, self.me, self.sibling).start()
        self.whole(a, 1, self.me, self.on(0)).start()
        self.whole(a, 2, self.me, self.on(1)).start()

    def wait_sibling(self, a):
        self.whole(a, 0, self.sibling, self.me).wait_recv()

    def on_neighbour(self, a, j):
        self.whole(a, 1 + j, self.on(j), self.me).wait_recv()
        self.halved(a, 3 + j, self.on(j), j, self.on(1 - j)).start()
        self.whole(a, 5 + j, self.on(j), self.sibling).start()

    def on_diagonal(self, a):
        self.halved(a, 3, self.on(2), 0, self.me).wait_recv()
        self.halved(a, 4, self.on(2), 1, self.me).wait_recv()
        self.whole(a, 7, self.on(2), self.sibling).start()

    def wait_passed_on(self, a, j):
        self.whole(a, 5 + j, (*self.chips[j], 1 - self.c), self.me).wait_recv()

    def wait_sends(self, a):
        self.whole(a, 0, self.me, self.sibling).wait_send()
        for j in range(2):
            self.whole(a, 1 + j, self.me, self.on(j)).wait_send()
            self.halved(a, 3 + j, self.on(j), j, self.on(1 - j)).wait_send()
        for j in range(3):
            self.whole(a, 5 + j, self.on(j), self.sibling).wait_send()


def _block_order():
    x, y, c = _mesh_pos()
    chips = [(x, y), (1 - x, y), (x, 1 - y), (1 - x, 1 - y)]
    return jnp.stack([4 * px + 2 * py + pc for px, py in chips for pc in (c, 1 - c)]).astype(jnp.int32)


def _gather_project(x, w_in, w_out, conv_pack, ln_g, tb):
    t_len, d = x.shape
    nb = t_len // tb
    cols = w_in.shape[1]
    mc = min(512, t_len)
    srcs = (w_in, w_out, conv_pack)
    dts = (MM, MM, F32)

    def body(order_ref, x_ref, win_ref, wout_ref, cp_ref, lng_ref, p_ref, xnt_ref, win_all, wout_all, cp_all,
             xnb, wall, st_out, st_cp, send_sems, recv_sems, cp_send, cp_recv, local_sems):
        i = pl.program_id(0)
        x_, y_, c_ = _mesh_pos()
        me = 4 * x_ + 2 * y_ + c_
        outs = (win_all, wout_all, cp_all)
        lands = (wall, wout_all, cp_all)
        stages = (wall.at[me], st_out, st_cp)
        gather = _BalancedGather(lambda a, px, py, pc: lands[a].at[4 * px + 2 * py + pc], send_sems, recv_sems, stages)
        small = _Gather(lambda a, px, py, pc: cp_all.at[4 * px + 2 * py + pc], cp_send, cp_recv, own_src=[st_cp])
        keep_own = [pltpu.make_async_copy(stages[a], outs[a].at[me], local_sems.at[a]) for a in range(3)]

        def keep(k):
            blk = order_ref[k]
            return pltpu.make_async_copy(wall.at[blk], win_all.at[blk], local_sems.at[2 + k])

        @pl.when(i == 0)
        def _():
            for a, (src, dst) in enumerate(zip((win_ref, wout_ref, cp_ref), stages)):
                rows = src.shape[0]
                rc = min(rows, 32)

                def cast(r, carry, src=src, dst=dst, rc=rc):
                    dst[pl.ds(r, rc), :] = src[pl.ds(r, rc), :].astype(dst.dtype)
                    return carry

                _chunks(rows, rc, cast, 0)
                if a < 2:
                    gather.start_own(a)
                else:
                    small.start_own(0)
                keep_own[a].start()

        @pl.when(i < nb)
        def _():
            xv = x_ref[...]
            r0 = lax.rsqrt(jnp.mean(xv * xv, axis=-1, keepdims=True) + RMS_EPS)
            xn = xv * r0 * lng_ref[...]
            xnb[pl.ds(pl.multiple_of(i * tb, tb), tb), :] = xn.astype(MM)
            xnt_ref[...] = xn.T.astype(MM)

        for k in range(N_DEV):
            @pl.when(i == nb + k)
            def _(k=k):
                if k == 1:
                    gather.wait_sibling(0)
                elif k == 2:
                    gather.on_neighbour(0, 0)
                    gather.on_neighbour(0, 1)
                elif k in (3, 5, 7):
                    gather.wait_passed_on(0, (k - 3) // 2)
                    if k == 3:
                        gather.on_neighbour(1, 0)
                        gather.on_neighbour(1, 1)
                    if k == 7:
                        gather.on_diagonal(1)
                elif k == 6:
                    gather.on_diagonal(0)
                blk = order_ref[k]
                if k:
                    keep(k).start()

                def project(r, carry):
                    rows = pl.ds(r, mc)
                    p_ref[rows, :] = _dot(xnb[rows, :], wall[blk]).astype(MM)
                    return carry

                _chunks(t_len, mc, project, 0)
                if k == N_DEV - 1:
                    gather.wait_sends(0)
                    gather.wait_sibling(1)
                    for j in range(3):
                        gather.wait_passed_on(1, j)
                    gather.wait_sends(1)
                    small.finish(0)
                    for cp in keep_own + [keep(kk) for kk in range(1, N_DEV)]:
                        cp.wait()

    vm = pl.BlockSpec(memory_space=pltpu.VMEM)
    hbm = pl.BlockSpec(memory_space=pl.ANY)
    grid_spec = pltpu.PrefetchScalarGridSpec(
        num_scalar_prefetch=1, grid=(nb + N_DEV,),
        in_specs=[pl.BlockSpec((tb, d), lambda i, o: (jnp.minimum(i, nb - 1), 0)), vm, vm, vm, vm],
        out_specs=(pl.BlockSpec((t_len, cols), lambda i, o: (0, o[jnp.maximum(i - nb, 0)])),
                   pl.BlockSpec((d, tb), lambda i, o: (0, jnp.minimum(i, nb - 1))), hbm, hbm, hbm),
        scratch_shapes=[pltpu.VMEM((t_len, d), MM), pltpu.VMEM((N_DEV,) + w_in.shape, MM),
                        pltpu.VMEM(w_out.shape, MM), pltpu.VMEM(conv_pack.shape, F32),
                        pltpu.SemaphoreType.DMA((16,)), pltpu.SemaphoreType.DMA((16,)),
                        pltpu.SemaphoreType.DMA((7,)), pltpu.SemaphoreType.DMA((7,)), pltpu.SemaphoreType.DMA((10,))])
    return pl.pallas_call(
        body, name="gather_project", grid_spec=grid_spec,
        out_shape=(jax.ShapeDtypeStruct((t_len, N_DEV * cols), MM),
                   jax.ShapeDtypeStruct((d, t_len), MM))
                  + tuple(jax.ShapeDtypeStruct((N_DEV,) + s.shape, dt) for s, dt in zip(srcs, dts)),
        compiler_params=_params(dimension_semantics=("arbitrary",)),
    )(_block_order(), x, w_in, w_out, conv_pack, ln_g)


def _forward(x, tgt, p, wout, wa_t, wi_t, sp, ones_c, ones_l, tb):
    t_len, d = x.shape
    nb = t_len // tb
    n_tiles, tw = wa_t.shape[0], wa_t.shape[1]
    hd_c, hd_l = d // N_CONV_HEADS, d // N_LRU_HEADS
    s8 = SUBLANES

    def body(x_ref, tgt_ref, p_ref, wout_ref, wa_ref, wi_ref, sp_ref, oc_ref, ol_ref,
             h_ref, dh_ref, dhb_ref, acc_ref, yc, czs, u, pa, pi,
             rcf, rlf, ybuf, tail_z, tail_xl, hcar):
        i = pl.program_id(0)
        row = _row_iota(d)

        @pl.when(i == 0)
        def _():
            tail_z[...] = jnp.zeros_like(tail_z)
            tail_xl[...] = jnp.zeros_like(tail_xl)
            hcar[...] = jnp.zeros_like(hcar)
            acc_ref[...] = jnp.zeros_like(acc_ref)

        def spr(r):
            return sp_ref[r:r + 1, :]

        def proj(rows, seg):
            return p_ref[rows, seg * d:(seg + 1) * d].astype(F32)

        w0, w1, w2 = spr(SP_CONV_W), spr(SP_CONV_W + 1), spr(SP_CONV_W + 2)
        l0, l1, l2, l3 = spr(SP_LRU_W), spr(SP_LRU_W + 1), spr(SP_LRU_W + 2), spr(SP_LRU_W + 3)
        lb = spr(SP_LRU_B)

        def convs(r, carry):
            zp, xp = carry
            rows16 = pl.ds(r, 2 * s8)
            bg16, xl16 = proj(rows16, P_B), proj(rows16, P_XL)
            z16 = proj(rows16, P_C) * proj(rows16, P_XC)
            for j in range(2):
                rows, sub = pl.ds(r + j * s8, s8), slice(j * s8, (j + 1) * s8)
                z, xl = z16[sub], xl16[sub]
                cz = w0 * _shift_down(z, zp, 2, row) + w1 * _shift_down(z, zp, 1, row) + w2 * z
                czs[rows, :] = cz
                yc[rows, :] = bg16[sub] * cz
                u[rows, :] = (l0 * _shift_down(xl, xp, 3, row) + l1 * _shift_down(xl, xp, 2, row)
                              + l2 * _shift_down(xl, xp, 1, row) + l3 * xl + lb)
                zp, xp = z, xl
            return zp, xp

        z_last, xl_last = _chunks(tb, 2 * s8, convs, (tail_z[...], tail_xl[...]))
        tail_z[...] = z_last
        tail_xl[...] = xl_last

        ub = u[...].astype(MM)
        for k in range(n_tiles):
            sl = slice(k * tw, (k + 1) * tw)
            pa[:, sl] = _dot(ub[:, sl], wa_ref[k])
            pi[:, sl] = _dot(ub[:, sl], wi_ref[k])
        rcf[...] = _head_rstd(yc[...], oc_ref[...], hd_c)

        c8 = RG_LRU_C * _log_sigmoid(spr(SP_LAM))
        b_a, b_i = spr(SP_B_A), spr(SP_B_I)

        def lru(r, hp):
            rows = pl.ds(r, SUBLANES)
            ra = _sigmoid(pa[rows, :] + b_a)
            ii = _sigmoid(pi[rows, :] + b_i)
            pa[rows, :] = ra
            pi[rows, :] = ii
            la = ra * c8
            a = jnp.exp(la)
            mult = _lru_input_scale(la, a)
            h = _scan_fwd(a, mult * (ii * u[rows, :]), hp, row)
            h_ref[rows, :] = h
            return _bcast_row(h, SUBLANES - 1)

        hcar[...] = _chunks(tb, SUBLANES, lru, hcar[...])
        rlf[...] = _head_rstd(h_ref[...], ol_ref[...], hd_l)

        g_c, g_l = spr(SP_CONV_G), spr(SP_LRU_G)

        def gate(r, carry):
            rows = pl.ds(r, 2 * s8)
            gc, gl = proj(rows, P_GC), proj(rows, P_GL)
            ybuf[rows, 0:d] = (yc[rows, :] * rcf[rows, :] * g_c * (gc * _sigmoid(gc))).astype(MM)
            ybuf[rows, d:2 * d] = (h_ref[rows, :] * rlf[rows, :] * g_l * (gl * _sigmoid(gl))).astype(MM)
            return carry

        _chunks(tb, 2 * s8, gate, 0)

        hres = x_ref[...] + _dot(ybuf[...], wout_ref[...])
        rf = lax.rsqrt(jnp.mean(hres * hres, axis=-1, keepdims=True) + RMS_EPS)
        hn = hres * rf
        fg = spr(SP_FINAL_G)
        err = hn * fg - tgt_ref[...]
        dout = err * (1.0 / d)
        acc_ref[0:SUBLANES, :] += (err * err).reshape(tb // SUBLANES, SUBLANES, d).sum(axis=0)
        acc_ref[SUBLANES:2 * SUBLANES, :] += (dout * hn).reshape(tb // SUBLANES, SUBLANES, d).sum(axis=0)
        gd = dout * fg
        dhres = rf * (gd - hn * jnp.mean(gd * hn, axis=-1, keepdims=True))
        dh_ref[...] = dhres
        dhb_ref[...] = dhres.astype(MM)

    vm = pl.BlockSpec(memory_space=pltpu.VMEM)
    blk = lambda w: pl.BlockSpec((tb, w), lambda i: (i, 0))
    buf = pltpu.VMEM((tb, d), F32)
    car = pltpu.VMEM((SUBLANES, d), F32)
    return pl.pallas_call(
        body, name="forward", grid=(nb,),
        in_specs=[blk(d), blk(d), blk(6 * d), vm, vm, vm, vm, vm, vm],
        out_specs=(blk(d), blk(d), blk(d), pl.BlockSpec((2 * SUBLANES, d), lambda i: (0, 0))) + (blk(d),) * 5,
        out_shape=(jax.ShapeDtypeStruct((t_len, d), F32),
                   jax.ShapeDtypeStruct((t_len, d), F32),
                   jax.ShapeDtypeStruct((t_len, d), MM),
                   jax.ShapeDtypeStruct((2 * SUBLANES, d), F32))
                  + (jax.ShapeDtypeStruct((t_len, d), F32),) * 5,
        scratch_shapes=[buf] * 2 + [pltpu.VMEM((tb, 2 * d), MM), car, car, car],
        compiler_params=_params(dimension_semantics=("arbitrary",)),
    )(x, tgt, p, wout, wa_t, wi_t, sp, ones_c, ones_l)


def _backward(p, h, dh, saved, facc, wout, wa_t, wi_t, sp, ones_c, ones_l, tb):
    t_len, d = h.shape
    nb = t_len // tb
    n_tiles, tw = wa_t.shape[0], wa_t.shape[1]
    hd_c, hd_l = d // N_CONV_HEADS, d // N_LRU_HEADS
    g_rows = 2 * n_tiles * hd_l
    s8 = SUBLANES

    def body(p_ref, h_ref, hhalo_ref, dh_ref, yc, czs, u, ra_ref, ii_ref, facc_ref,
             wout_ref, wa_ref, wi_ref, sp_ref, oc_ref, ol_ref,
             dp_ref, yt_ref, slab_v, slab_g,
             hh, dy, ybuf, rcf, rlf, qc, ql, dyc_hat, dyl_hat, dpa, dpi, du, gwa_ref, gwi_ref, acc_ref,
             car_dcz, car_a, car_g, car_du):
        i = pl.program_id(0)
        blk_idx = nb - 1 - i
        row = _row_iota(d)

        @pl.when(i == 0)
        def _():
            for ref in (car_dcz, car_a, car_g, car_du, gwa_ref, gwi_ref, acc_ref):
                ref[...] = jnp.zeros_like(ref)

        def spr(r):
            return sp_ref[r:r + 1, :]

        def proj(rows, seg):
            return p_ref[rows, seg * d:(seg + 1) * d].astype(F32)

        def put(rows, seg, halves):
            dp_ref[rows, seg * d:(seg + 1) * d] = jnp.concatenate(halves, axis=0).astype(MM)

        def acc_add(group, val):
            acc_ref[group * s8:(group + 1) * s8, :] += val

        live = jnp.where(blk_idx > 0, 1.0, 0.0).astype(F32)
        hh[0:s8, :] = hhalo_ref[...] * live
        hh[s8:, :] = h_ref[...]

        dy[...] = _dot_nt(dh_ref[...].astype(MM), wout_ref[...])

        w0, w1, w2 = spr(SP_CONV_W), spr(SP_CONV_W + 1), spr(SP_CONV_W + 2)
        l0, l1, l2, l3 = spr(SP_LRU_W), spr(SP_LRU_W + 1), spr(SP_LRU_W + 2), spr(SP_LRU_W + 3)

        rcf[...] = _head_rstd(yc[...], oc_ref[...], hd_c)
        rlf[...] = _head_rstd(h_ref[...], ol_ref[...], hd_l)

        g_c, g_l = spr(SP_CONV_G), spr(SP_LRU_G)

        def gates(r, carry):
            rows = pl.ds(r, 2 * s8)
            for (seg, off_y, src, rstd, gain, q, dhat, grp) in (
                    (P_GC, 0, yc, rcf, g_c, qc, dyc_hat, A_CONV_G),
                    (P_GL, d, h_ref, rlf, g_l, ql, dyl_hat, A_LRU_G)):
                gt = proj(rows, seg)
                sg = _sigmoid(gt)
                silu = gt * sg
                yhat = src[rows, :] * rstd[rows, :]
                nrm = yhat * gain
                ybuf[rows, off_y:off_y + d] = nrm * silu
                dout = dy[rows, off_y:off_y + d]
                dnrm = dout * silu
                dp_ref[rows, seg * d:(seg + 1) * d] = (dout * nrm * (sg * (1.0 + gt * (1.0 - sg)))).astype(MM)
                dg = dnrm * yhat
                acc_add(grp, dg[0:s8] + dg[s8:])
                dh_ = dnrm * gain
                dhat[rows, :] = dh_
                q[rows, :] = dh_ * yhat
            return carry

        _chunks(tb, 2 * s8, gates, 0)

        qc[...] = _head_sums(qc[...], oc_ref[...]) * (1.0 / hd_c)
        ql[...] = _head_sums(ql[...], ol_ref[...]) * (1.0 / hd_l)
        yt_ref[...] = ybuf[...].T.astype(MM)

        c8 = RG_LRU_C * _log_sigmoid(spr(SP_LAM))

        def conv_mixer(r, dcz_n):
            rows16 = pl.ds(r, 2 * s8)
            bg16, cg16, xc16 = proj(rows16, P_B), proj(rows16, P_C), proj(rows16, P_XC)
            z16 = cg16 * xc16
            d_b, d_c, d_x = [None, None], [None, None], [None, None]
            for j in (1, 0):
                rows, sub = pl.ds(r + j * s8, s8), slice(j * s8, (j + 1) * s8)
                rstd = rcf[rows, :]
                yhat = yc[rows, :] * rstd
                dyc = rstd * (dyc_hat[rows, :] - yhat * qc[rows, :])
                d_b[j] = dyc * czs[rows, :]
                dcz = dyc * bg16[sub]
                up1, up2 = _shift_up(dcz, dcz_n, 1, row), _shift_up(dcz, dcz_n, 2, row)
                dz = w2 * dcz + w1 * up1 + w0 * up2
                d_c[j] = dz * xc16[sub]
                d_x[j] = dz * cg16[sub]
                z = z16[sub]
                acc_add(A_CONV_W, up2 * z)
                acc_add(A_CONV_W + 1, up1 * z)
                acc_add(A_CONV_W + 2, dcz * z)
                dcz_n = dcz
            put(rows16, P_B, d_b)
            put(rows16, P_C, d_c)
            put(rows16, P_XC, d_x)
            return dcz_n

        car_dcz[...] = _chunks(tb, 2 * s8, conv_mixer, car_dcz[...], reverse=True)

        def lru_mixer(r, carry):
            a_n, g_n = carry
            for j in (1, 0):
                rows = pl.ds(r + j * s8, s8)
                rstd = rlf[rows, :]
                hcur = hh[pl.ds(r + (j + 1) * s8, s8), :]
                hhat = hcur * rstd
                dh_out = rstd * (dyl_hat[rows, :] - hhat * ql[rows, :])
                ra = ra_ref[rows, :]
                la = ra * c8
                a = jnp.exp(la)
                g = _scan_bwd(_shift_up(a, a_n, 1, row), dh_out, g_n, row)
                da = g * _shift_down(hcur, hh[pl.ds(r + j * s8, s8), :], 1, row)
                ii = ii_ref[rows, :]
                uu = u[rows, :]
                mult = _lru_input_scale(la, a)
                dmult = g * (ii * uu)
                ds = g * mult
                dla = a * (da - dmult * a / mult)
                acc_add(A_LAM, dla * ra)
                dpa_ = dla * c8 * ra * (1.0 - ra)
                dpi_ = ds * uu * ii * (1.0 - ii)
                acc_add(A_B_A, dpa_)
                acc_add(A_B_I, dpi_)
                dpa[rows, :] = dpa_
                dpi[rows, :] = dpi_
                du[rows, :] = ds * ii
                a_n, g_n = a, _bcast_row(g, 0)
            return a_n, g_n

        a_f, g_f = _chunks(tb, 2 * s8, lru_mixer, (car_a[...], car_g[...]), reverse=True)
        car_a[...] = a_f
        car_g[...] = g_f

        dpab = dpa[...].astype(MM)
        dpib = dpi[...].astype(MM)
        for k in range(n_tiles):
            sl = slice(k * tw, (k + 1) * tw)
            du[:, sl] += _dot_nt(dpab[:, sl], wa_ref[k]) + _dot_nt(dpib[:, sl], wi_ref[k])
            ut = u[:, sl].T.astype(MM)
            gwa_ref[k] += _dot(ut, dpab[:, sl])
            gwi_ref[k] += _dot(ut, dpib[:, sl])

        def lru_conv(r, du_n):
            rows16 = pl.ds(r, 2 * s8)
            xl16 = proj(rows16, P_XL)
            d_xl = [None, None]
            for j in (1, 0):
                rows, sub = pl.ds(r + j * s8, s8), slice(j * s8, (j + 1) * s8)
                dut = du[rows, :]
                up1, up2, up3 = (_shift_up(dut, du_n, s, row) for s in (1, 2, 3))
                d_xl[j] = l3 * dut + l2 * up1 + l1 * up2 + l0 * up3
                xl = xl16[sub]
                acc_add(A_LRU_W, up3 * xl)
                acc_add(A_LRU_W + 1, up2 * xl)
                acc_add(A_LRU_W + 2, up1 * xl)
                acc_add(A_LRU_W + 3, dut * xl)
                acc_add(A_LRU_B, dut)
                du_n = dut
            put(rows16, P_XL, d_xl)
            return du_n

        car_du[...] = _chunks(tb, 2 * s8, lru_conv, car_du[...], reverse=True)

        @pl.when(i == nb - 1)
        def _():
            def rowsum(ref, group):
                return jnp.sum(ref[group * s8:(group + 1) * s8, :], axis=0, keepdims=True)

            slab_v[...] = jnp.zeros_like(slab_v)
            loss = jnp.sum(rowsum(facc_ref, 0), axis=1, keepdims=True) * (0.5 / d)
            rows = {SL_LOSS: jnp.broadcast_to(loss, (1, d)), SL_FINAL_G: rowsum(facc_ref, 1),
                    SL_LRU_B: rowsum(acc_ref, A_LRU_B), SL_B_A: rowsum(acc_ref, A_B_A), SL_B_I: rowsum(acc_ref, A_B_I),
                    SL_LAM: rowsum(acc_ref, A_LAM), SL_CONV_G: rowsum(acc_ref, A_CONV_G), SL_LRU_G: rowsum(acc_ref, A_LRU_G)}
            for k in range(3):
                rows[SL_CONV_W + k] = rowsum(acc_ref, A_CONV_W + k)
            for k in range(4):
                rows[SL_LRU_W + k] = rowsum(acc_ref, A_LRU_W + k)
            for r, val in rows.items():
                slab_v[r:r + 1, :] = val
            head_of_lane = lax.broadcasted_iota(jnp.int32, (hd_l, tw), 1) // hd_l
            for mtx, g_ref in enumerate((gwa_ref, gwi_ref)):
                for k in range(n_tiles):
                    packed = jnp.zeros((hd_l, tw), F32)
                    for a in range(tw // hd_l):
                        packed = jnp.where(head_of_lane == a, g_ref[k, a * hd_l:(a + 1) * hd_l, :], packed)
                    slab_g[(mtx * n_tiles + k) * hd_l:(mtx * n_tiles + k + 1) * hd_l, :] = packed.astype(MM)

    vm = pl.BlockSpec(memory_space=pltpu.VMEM)
    rev = lambda w: pl.BlockSpec((tb, w), lambda i: (nb - 1 - i, 0))
    halo = lambda rows, w: pl.BlockSpec((rows, w), lambda i: (jnp.maximum((nb - 1 - i) * (tb // rows) - 1, 0), 0))
    const = lambda shape: pl.BlockSpec(shape, lambda i: (0,) * len(shape))
    buf = lambda w: pltpu.VMEM((tb, w), F32)
    car = pltpu.VMEM((SUBLANES, d), F32)
    return pl.pallas_call(
        body, name="backward", grid=(nb,),
        in_specs=[rev(6 * d), rev(d), halo(SUBLANES, d), rev(d)] + [rev(d)] * 5 + [vm, vm, vm, vm, vm, vm, vm],
        out_specs=(rev(6 * d), pl.BlockSpec((2 * d, tb), lambda i: (0, nb - 1 - i)),
                   const((SL_ROWS, d)), const((g_rows, tw))),
        out_shape=(jax.ShapeDtypeStruct((t_len, 6 * d), MM),
                   jax.ShapeDtypeStruct((2 * d, t_len), MM),
                   jax.ShapeDtypeStruct((SL_ROWS, d), F32),
                   jax.ShapeDtypeStruct((g_rows, tw), MM)),
        scratch_shapes=[pltpu.VMEM((SUBLANES + tb, d), F32), buf(2 * d), buf(2 * d)] + [buf(d)] * 9
                       + [pltpu.VMEM((n_tiles, tw, tw), F32), pltpu.VMEM((n_tiles, tw, tw), F32),
                          pltpu.VMEM((A_GROUPS * SUBLANES, d), F32), car, car, car, car],
        compiler_params=_params(dimension_semantics=("arbitrary",)),
    )(p, h, h, dh, *saved, facc, wout, wa_t, wi_t, sp, ones_c, ones_l)


def _input_grad(dp, win_all, x, dh, sp, parts, tb):
    t_len, d = x.shape
    nb = t_len // tb
    cols = win_all.shape[2]
    n_parts = len(parts)

    def body(dp_ref, win_ref, x_ref, dh_ref, sp_ref, *refs):
        gx_ref, ln_ref = refs[n_parts:n_parts + 2]
        send_sems, recv_sems, acc_ref, ln_all, ln_send, ln_recv = refs[2 * n_parts + 2:]
        exchange = _ChipExchange(refs[:n_parts], refs[n_parts + 2:2 * n_parts + 2], send_sems, recv_sems)
        i = pl.program_id(0)

        @pl.when(i == 0)
        def _():
            acc_ref[...] = jnp.zeros_like(acc_ref)
            exchange.start()

        dxn = _dot_nt(dp_ref[:, 0:cols], win_ref[0])
        for j in range(1, N_DEV):
            dxn += _dot_nt(dp_ref[:, j * cols:(j + 1) * cols], win_ref[j])
        xv = x_ref[...]
        r0 = lax.rsqrt(jnp.mean(xv * xv, axis=-1, keepdims=True) + RMS_EPS)
        xhat = xv * r0
        acc_ref[...] += (dxn * xhat).reshape(tb // SUBLANES, SUBLANES, d).sum(axis=0)
        dxh = dxn * sp_ref[SP_LN_G:SP_LN_G + 1, :]
        gx_ref[...] = dh_ref[...] + r0 * (dxh - xhat * jnp.mean(dxh * xhat, axis=-1, keepdims=True))

        @pl.when(i == nb - 1)
        def _():
            exchange.finish()
            x_, y_, c_ = _mesh_pos()
            ln_all[4 * x_ + 2 * y_ + c_] = jnp.broadcast_to(jnp.sum(acc_ref[...], axis=0, keepdims=True), acc_ref.shape)
            gather = _Gather(lambda a, px, py, pc: ln_all.at[4 * px + 2 * py + pc], ln_send, ln_recv)
            gather.start_own(0)
            gather.finish(0)
            total = ln_all[0]
            for dev in range(1, N_DEV):
                total = total + ln_all[dev]
            ln_ref[...] = total

    vm = pl.BlockSpec(memory_space=pltpu.VMEM)
    hbm = pl.BlockSpec(memory_space=pl.ANY)
    blk = lambda w: pl.BlockSpec((tb, w), lambda i: (i, 0))
    outs = pl.pallas_call(
        body, name="input_grad", grid=(nb,),
        in_specs=[blk(6 * d), vm, blk(d), blk(d), vm] + [hbm] * n_parts,
        out_specs=(blk(d), pl.BlockSpec((SUBLANES, d), lambda i: (0, 0))) + (hbm,) * n_parts,
        out_shape=(jax.ShapeDtypeStruct((t_len, d), F32), jax.ShapeDtypeStruct((SUBLANES, d), F32))
                  + tuple(jax.ShapeDtypeStruct(p.shape, p.dtype) for p in parts),
        scratch_shapes=[pltpu.SemaphoreType.DMA((3 * n_parts,)), pltpu.SemaphoreType.DMA((3 * n_parts,)),
                        pltpu.VMEM((SUBLANES, d), F32), pltpu.VMEM((N_DEV, SUBLANES, d), F32),
                        pltpu.SemaphoreType.DMA((7,)), pltpu.SemaphoreType.DMA((7,))],
        compiler_params=_params(dimension_semantics=("arbitrary",)),
    )(dp, win_all, x, dh, sp, *parts)
    return outs[0], outs[1], outs[2:]


_CHIP_RELATIONS = [(0, 0), (1, 0), (0, 1), (1, 1)]


def _related_block(k, core):
    x, y, _ = _mesh_pos()
    fx, fy = _CHIP_RELATIONS[k]
    return 4 * (x ^ fx) + 2 * (y ^ fy) + core


class _ChipExchange:
    def __init__(self, part_refs, land_refs, send_sems, recv_sems):
        self.part_refs, self.land_refs, self.send_sems, self.recv_sems = part_refs, land_refs, send_sems, recv_sems

    def copies(self):
        x, y, c = _mesh_pos()
        for a in range(len(self.part_refs)):
            for k in (1, 2, 3):
                fx, fy = _CHIP_RELATIONS[k]
                yield pltpu.make_async_remote_copy(
                    src_ref=self.part_refs[a].at[k - 1], dst_ref=self.land_refs[a].at[k - 1],
                    send_sem=self.send_sems.at[3 * a + k - 1], recv_sem=self.recv_sems.at[3 * a + k - 1],
                    device_id=(x ^ fx, y ^ fy, c), device_id_type=MESH)

    def start(self):
        for cp in self.copies():
            cp.start()

    def finish(self):
        for cp in self.copies():
            cp.wait_recv()
        for cp in self.copies():
            cp.wait_send()


def _weight_grad_stage1(name, blk_shape, n_split, operands, in_specs, product, riders=(), slabs=()):
    n_rows, n_cols = blk_shape
    rs = n_rows // n_split
    rc = 32
    n_in, n_ride, n_slab = len(operands), len(riders), len(slabs)
    _, _, c = _mesh_pos()
    order = jnp.stack([_related_block(k, 1 - c) for k in range(4)]
                      + [_related_block(k, c) for k in (1, 2, 3, 0)]).astype(jnp.int32)

    def body(order_ref, *refs):
        ins = refs[:n_in]
        ride_in = refs[n_in:n_in + n_ride]
        slab_in = refs[n_in + n_ride:n_in + n_ride + n_slab]
        n_op = n_in + n_ride + n_slab
        part_ref, own_ref = refs[n_op:n_op + 2]
        ride_out = refs[n_op + 2:n_op + 2 + n_ride]
        gathered = refs[n_op + 2 + n_ride:n_op + 2 + n_ride + n_slab]
        (gbuf, sendbuf, from_sib, send_sems, recv_sems, ride_send, ride_recv,
         slab_send, slab_recv, slab_local) = refs[n_op + 2 + n_ride + n_slab:]
        exchange = _ChipExchange(ride_in, ride_out, ride_send, ride_recv)
        s = pl.program_id(0)
        x, y, c = _mesh_pos()
        me = 4 * x + 2 * y + c
        gather = _BalancedGather(lambda a, px, py, pc: gathered[a].at[4 * px + 2 * py + pc], slab_send, slab_recv, slab_in)
        keep_own = [pltpu.make_async_copy(slab_in[a], gathered[a].at[me], slab_local.at[a]) for a in range(n_slab)]

        def to_sibling(k):
            return pltpu.make_async_remote_copy(
                src_ref=sendbuf.at[k], dst_ref=from_sib.at[k], send_sem=send_sems.at[k], recv_sem=recv_sems.at[k],
                device_id=(x, y, 1 - c), device_id_type=MESH)

        @pl.when(s == 0)
        def _():
            exchange.start()
            for a in range(n_slab):
                gather.start_own(a)
                keep_own[a].start()

        @pl.when(s == 5)
        def _():
            for a in range(n_slab):
                gather.on_neighbour(a, 0)
                gather.on_neighbour(a, 1)

        @pl.when(s == 7)
        def _():
            for a in range(n_slab):
                gather.on_diagonal(a)

        for h in range(n_split):
            gbuf[h * rs:(h + 1) * rs, :] = product(ins, h)

        @pl.when(s < 4)
        def _():
            def narrow(r, carry):
                sendbuf[s, pl.ds(r, rc), :] = gbuf[pl.ds(r, rc), :].astype(MM)
                return carry

            _chunks(n_rows, rc, narrow, 0)
            to_sibling(s).start()

        @pl.when(s >= 4)
        def _():
            k = jnp.where(s == 7, 0, s - 3)
            to_sibling(k).wait_recv()

            @pl.when(s < 7)
            def _():
                def add(r, carry):
                    rows = pl.ds(r, rc)
                    part_ref[0, rows, :] = (gbuf[rows, :] + from_sib[k, rows, :].astype(F32)).astype(MM)
                    return carry

                _chunks(n_rows, rc, add, 0)

            @pl.when(s == 7)
            def _():
                def add(r, carry):
                    rows = pl.ds(r, rc)
                    own_ref[rows, :] = gbuf[rows, :] + from_sib[0, rows, :].astype(F32)
                    return carry

                _chunks(n_rows, rc, add, 0)
                for kk in range(4):
                    to_sibling(kk).wait_send()
                exchange.finish()
                for a in range(n_slab):
                    gather.wait_sibling(a)
                    for j in range(3):
                        gather.wait_passed_on(a, j)
                    gather.wait_sends(a)
                    keep_own[a].wait()

    hbm = pl.BlockSpec(memory_space=pl.ANY)
    grid_spec = pltpu.PrefetchScalarGridSpec(
        num_scalar_prefetch=1, grid=(N_DEV,), in_specs=list(in_specs) + [hbm] * (n_ride + n_slab),
        out_specs=(pl.BlockSpec((1, n_rows, n_cols), lambda s, o: (jnp.clip(s - 4, 0, 2), 0, 0)),
                   pl.BlockSpec((n_rows, n_cols), lambda s, o: (0, 0))) + (hbm,) * (n_ride + n_slab),
        scratch_shapes=[pltpu.VMEM((n_rows, n_cols), F32), pltpu.VMEM((4, n_rows, n_cols), MM),
                        pltpu.VMEM((4, n_rows, n_cols), MM),
                        pltpu.SemaphoreType.DMA((4,)), pltpu.SemaphoreType.DMA((4,)),
                        pltpu.SemaphoreType.DMA((max(3 * n_ride, 1),)), pltpu.SemaphoreType.DMA((max(3 * n_ride, 1),)),
                        pltpu.SemaphoreType.DMA((max(8 * n_slab, 1),)), pltpu.SemaphoreType.DMA((max(8 * n_slab, 1),)),
                        pltpu.SemaphoreType.DMA((max(n_slab, 1),))])
    outs = pl.pallas_call(
        body, name=name, grid_spec=grid_spec,
        out_shape=(jax.ShapeDtypeStruct((3, n_rows, n_cols), MM), jax.ShapeDtypeStruct((n_rows, n_cols), F32))
                  + tuple(jax.ShapeDtypeStruct(p.shape, p.dtype) for p in riders)
                  + tuple(jax.ShapeDtypeStruct((N_DEV,) + a.shape, a.dtype) for a in slabs),
        compiler_params=_params(dimension_semantics=("arbitrary",)),
    )(order, *operands, *riders, *slabs)
    return outs[0], outs[1], outs[2:2 + n_ride], outs[2 + n_ride:]


def _weight_grad_in(xnt, dp, riders, slabs):
    d, t_len = xnt.shape
    cols = dp.shape[1] // N_DEV
    half = d // 2
    return _weight_grad_stage1(
        "weight_grad_in", (d, cols), 2, (xnt, dp),
        [pl.BlockSpec(memory_space=pltpu.VMEM), pl.BlockSpec((t_len, cols), lambda s, o: (0, o[s]))],
        lambda refs, h: _dot(refs[0][h * half:(h + 1) * half, :], refs[1][...]), riders, slabs)


def _weight_grad_out(yt, dhb):
    d2, t_len = yt.shape
    d = dhb.shape[1]
    rows = d2 // N_DEV
    return _weight_grad_stage1(
        "weight_grad_out", (rows, d), 1, (yt, dhb),
        [pl.BlockSpec((rows, t_len), lambda s, o: (o[s], 0)), pl.BlockSpec(memory_space=pltpu.VMEM)],
        lambda refs, h: _dot(refs[0][...], refs[1][...]))


def _update_shard(own, from_chips, w, m, v, name):
    n_rows, n_cols = w.shape
    rb = min(256, n_rows)

    def body(own_ref, fc_ref, w_ref, m_ref, v_ref, grad_ref, delta_ref, mo_ref, vo_ref):
        g = own_ref[...]
        for k in range(3):
            g = g + fc_ref[k].astype(F32)
        delta, m_new, v_new = _adamw(w_ref[...], g, m_ref[...], v_ref[...])
        grad_ref[...] = g
        delta_ref[...] = delta
        mo_ref[...] = m_new
        vo_ref[...] = v_new

    blk = pl.BlockSpec((rb, n_cols), lambda i: (i, 0))
    out = jax.ShapeDtypeStruct((n_rows, n_cols), F32)
    return pl.pallas_call(
        body, name=name, grid=(n_rows // rb,),
        in_specs=[blk, pl.BlockSpec((3, rb, n_cols), lambda i: (0, i, 0)), blk, blk, blk],
        out_specs=(blk, blk, blk, blk), out_shape=(out, out, out, out),
        compiler_params=_params(dimension_semantics=("arbitrary",)),
    )(own, from_chips, w, m, v)


def _small_update(gat_v, gat_g, ln_tot, vec_w, vec_m, vec_v, gates, convs):
    n_vec = len(vec_w)
    n_heads, hd, _ = gates[0].shape
    tw = gat_g.shape[2]
    s8 = SUBLANES
    per = tw // hd
    n_tiles = n_heads // per
    cc = convs[0].shape[1]
    n_in = 3 + 3 * n_vec + 12

    def body(*refs):
        gv_ref, gg_ref, ln_ref = refs[:3]
        w_refs, m_refs, v_refs = (refs[3 + j * n_vec:3 + (j + 1) * n_vec] for j in range(3))
        gate_refs = refs[3 + 3 * n_vec:3 + 3 * n_vec + 6]
        conv_refs = refs[3 + 3 * n_vec + 6:n_in]
        loss_o = refs[n_in]
        kinds = [refs[n_in + 1 + j * (n_vec + 4):n_in + 1 + (j + 1) * (n_vec + 4)] for j in range(4)]
        tv, tg = refs[n_in + 1 + 4 * (n_vec + 4):]
        x, y, c = _mesh_pos()
        me = 4 * x + 2 * y + c

        def emit(k_out, w, g, m, v):
            delta, m_new, v_new = _adamw(w, g, m, v)
            for ref, val in zip(k_out, (g, delta, m_new, v_new)):
                ref[...] = val

        total = gv_ref[0]
        for dev in range(1, N_DEV):
            total = total + gv_ref[dev]
        tv[...] = total
        tv[SL_LN_G:SL_LN_G + 1, :] = ln_ref[0:1, :]

        def sum_gates(r, carry):
            rows = pl.ds(r, 2 * s8)
            part = gg_ref[0, rows, :].astype(F32)
            for dev in range(1, N_DEV):
                part = part + gg_ref[dev, rows, :].astype(F32)
            tg[rows, :] = part
            return carry

        _chunks(tg.shape[0], 2 * s8, sum_gates, 0)
        loss_o[...] = jnp.broadcast_to(tv[SL_LOSS:SL_LOSS + 1, 0:LANES], loss_o.shape)
        for p in range(n_vec):
            w, g = w_refs[p][...], tv[SL_LN_G + p, :]
            if SL_LN_G + p == SL_LAM:
                g = g * (RG_LRU_C * jax.nn.sigmoid(-w))
            emit([k_out[p] for k_out in kinds], w, g, m_refs[p][...], v_refs[p][...])
        lanes = pl.ds(pl.multiple_of(me * cc, cc), cc)
        for j, (row0, n) in enumerate(((SL_CONV_W, 3), (SL_LRU_W, 4))):
            w_ref, m_ref, v_ref = conv_refs[3 * j:3 * j + 3]
            emit([k_out[n_vec + 2 + j] for k_out in kinds], w_ref[...], tv[row0:row0 + n, lanes], m_ref[...], v_ref[...])
        for mtx in range(2):
            w_ref, m_ref, v_ref = gate_refs[3 * mtx:3 * mtx + 3]
            for k in range(n_tiles):
                tile = tg[(mtx * n_tiles + k) * hd:(mtx * n_tiles + k + 1) * hd, :]
                for a in range(per):
                    head = k * per + a
                    g = tile[:, a * hd:(a + 1) * hd]
                    delta, m_new, v_new = _adamw(w_ref[head], g, m_ref[head], v_ref[head])
                    for k_out, val in zip(kinds, (g, delta, m_new, v_new)):
                        k_out[n_vec + mtx][head] = val

    vm = pl.BlockSpec(memory_space=pltpu.VMEM)
    like = lambda a: jax.ShapeDtypeStruct(a.shape, F32)
    per_kind = tuple(like(a) for a in vec_w) + (like(gates[0]), like(gates[3]), like(convs[0]), like(convs[3]))
    n_out = 1 + 4 * len(per_kind)
    outs = pl.pallas_call(
        body, name="small_update",
        in_specs=[vm] * n_in, out_specs=(vm,) * n_out,
        out_shape=(jax.ShapeDtypeStruct((SUBLANES, LANES), F32),) + per_kind * 4,
        scratch_shapes=[pltpu.VMEM(gat_v.shape[1:], F32), pltpu.VMEM(gat_g.shape[1:], F32)],
        compiler_params=_params(),
    )(gat_v, gat_g, ln_tot, *vec_w, *vec_m, *vec_v, *gates, *convs)
    return outs[0], [outs[1 + j * len(per_kind):1 + (j + 1) * len(per_kind)] for j in range(4)]


def _head_ones(head_dim, tw):
    lane = jnp.arange(tw) // head_dim
    return (lane[:, None] == lane[None, :]).astype(MM)


def _gate_tiles(w, tw):
    n_heads, hd, _ = w.shape
    per = tw // hd
    w4 = w.reshape(n_heads // per, per, hd, hd)
    eye = jnp.eye(per, dtype=w.dtype)
    return (w4[:, :, :, None, :] * eye[None, :, None, :, None]).reshape(n_heads // per, tw, tw)


def kernel(x, ln_g, w_in, conv_w, lru_conv_w, lru_conv_b, w_a, b_a, w_i, b_i, lam, conv_out_g, lru_out_g, w_out, final_g, loss_target, m_ln_g, m_w_in, m_conv_w, m_lru_conv_w, m_lru_conv_b, m_w_a, m_b_a, m_w_i, m_b_i, m_lam, m_conv_out_g, m_lru_out_g, m_w_out, m_final_g, v_ln_g, v_w_in, v_conv_w, v_lru_conv_w, v_lru_conv_b, v_w_a, v_b_a, v_w_i, v_b_i, v_lam, v_conv_out_g, v_lru_out_g, v_w_out, v_final_g):
    _, t_len, d = x.shape
    hd_l = d // N_LRU_HEADS
    tw = min(MXU_TILE, d)
    cc = conv_w.shape[1]
    x2, tgt2 = x[0], loss_target[0]

    def conv_rows(cw3, lw4):
        return jnp.concatenate([jnp.zeros((1, cc), F32), cw3, lw4], axis=0)

    p, xnt, win_all, wout_all, conv_all = _gather_project(
        x2, w_in, w_out, conv_rows(conv_w, lru_conv_w), ln_g.reshape(1, d), min(256, t_len))
    wout_full = wout_all.reshape(N_DEV * w_out.shape[0], d)
    conv_full = conv_all.transpose(1, 0, 2).reshape(SUBLANES, d)
    small = [ln_g, lru_conv_b, b_a, b_i, lam, conv_out_g, lru_out_g, final_g]
    sp = jnp.concatenate([jnp.stack(small), conv_full[1:], jnp.zeros((1, d), F32)], axis=0)
    wa_t, wi_t = _gate_tiles(w_a, tw).astype(MM), _gate_tiles(w_i, tw).astype(MM)
    ones_c, ones_l = _head_ones(d // N_CONV_HEADS, tw), _head_ones(hd_l, tw)

    h, dh, dhb, facc, *saved = _forward(x2, tgt2, p, wout_full, wa_t, wi_t, sp, ones_c, ones_l, min(256, t_len))
    dp, yt, slab_v, slab_g = _backward(p, h, dh, saved, facc, wout_full, wa_t, wi_t, sp, ones_c, ones_l, min(256, t_len))
    part_out, own_out, _, _ = _weight_grad_out(yt, dhb)
    part_in, own_in, (chips_out,), (gat_v, gat_g) = _weight_grad_in(xnt, dp, (part_out,), (slab_v, slab_g))
    grad_x, ln_tot, (chips_in,) = _input_grad(dp, win_all, x2, dh, sp, (part_in,), min(512, t_len))
    gw_in, dw_in, mw_in, vw_in = _update_shard(own_in, chips_in, w_in, m_w_in, v_w_in, "update_w_in")
    gw_out, dw_out, mw_out, vw_out = _update_shard(own_out, chips_out, w_out, m_w_out, v_w_out, "update_w_out")

    loss_tile, kinds = _small_update(
        gat_v, gat_g, ln_tot, small,
        [m_ln_g, m_lru_conv_b, m_b_a, m_b_i, m_lam, m_conv_out_g, m_lru_out_g, m_final_g],
        [v_ln_g, v_lru_conv_b, v_b_a, v_b_i, v_lam, v_conv_out_g, v_lru_out_g, v_final_g],
        (w_a, m_w_a, v_w_a, w_i, m_w_i, v_w_i), (conv_w, m_conv_w, v_conv_w, lru_conv_w, m_lru_conv_w, v_lru_conv_w))

    def unpack(kind, big_in, big_out):
        vec, (wa_, wi_, cw_, lw_) = kind[:len(small)], kind[len(small):]
        return [vec[0], big_in, cw_, lw_, vec[1], wa_, vec[2], wi_, vec[3], vec[4], vec[5], vec[6], big_out, vec[7]]

    return (loss_tile[0, 0], grad_x[None], *unpack(kinds[0], gw_in, gw_out), *unpack(kinds[1], dw_in, dw_out),
            *unpack(kinds[2], mw_in, mw_out), *unpack(kinds[3], vw_in, vw_out))
```

```python
import functools

import jax
import jax.numpy as jnp
from jax import lax
from jax.experimental import pallas as pl
from jax.experimental.pallas import tpu as pltpu

F32 = jnp.float32
MM = jnp.bfloat16
MESH = pl.DeviceIdType.MESH

N_DEV = 8
N_CONV_HEADS = 8
N_LRU_HEADS = 16
RG_LRU_C = 8.0
RMS_EPS = 1e-6
ADAM_LR, ADAM_B1, ADAM_B2, ADAM_EPS, ADAM_WD, ADAM_STEP = 0.001, 0.9, 0.999, 1e-08, 0.01, 10
ADAM_BC1 = 1.0 - ADAM_B1 ** ADAM_STEP
ADAM_BC2 = 1.0 - ADAM_B2 ** ADAM_STEP

SUBLANES = 8
LANES = 128
MXU_TILE = 256
VMEM_LIMIT = 56 * 1024 * 1024

SP_LN_G, SP_LRU_B, SP_B_A, SP_B_I, SP_LAM, SP_CONV_G, SP_LRU_G, SP_FINAL_G, SP_CONV_W, SP_LRU_W = 0, 1, 2, 3, 4, 5, 6, 7, 8, 11
SP_ROWS = 16
P_B, P_C, P_XC, P_GC, P_XL, P_GL = 0, 1, 2, 3, 4, 5
A_CONV_G, A_LRU_G, A_LAM, A_B_A, A_B_I, A_CONV_W, A_LRU_W, A_LRU_B = 0, 1, 2, 3, 4, 5, 8, 12
A_GROUPS = 13
SL_LOSS, SL_LN_G, SL_LRU_B, SL_B_A, SL_B_I, SL_LAM, SL_CONV_G, SL_LRU_G, SL_FINAL_G, SL_CONV_W, SL_LRU_W = 0, 1, 2, 3, 4, 5, 6, 7, 8, 16, 24
SL_ROWS = 32


def _params(vmem=True, **kw):
    if vmem:
        kw["vmem_limit_bytes"] = VMEM_LIMIT
    return pltpu.CompilerParams(**kw)


def _dot(a, b):
    return jnp.dot(a, b, preferred_element_type=F32)


def _dot_nt(a, b):
    return lax.dot_general(a, b, (((1,), (1,)), ((), ())), preferred_element_type=F32)


def _head_sums(v, ones_tile):
    tw = ones_tile.shape[0]
    vb = v.astype(MM)
    return jnp.concatenate([_dot(vb[:, k:k + tw], ones_tile) for k in range(0, v.shape[1], tw)], axis=1)


def _head_rstd(v, ones_tile, head_dim):
    return lax.rsqrt(_head_sums(v * v, ones_tile) * (1.0 / head_dim) + RMS_EPS)


def _sigmoid(x):
    return 0.5 * jnp.tanh(0.5 * x) + 0.5


def _lru_input_scale(log_a, a):
    return jnp.sqrt(-jnp.tanh(log_a) * (1.0 + a * a))


def _log_sigmoid(x):
    z = jnp.exp(-jnp.abs(x))
    u = 1.0 + z
    log1p_z = jnp.where(u == 1.0, z, jnp.log(u) * (z / (u - 1.0)))
    return jnp.minimum(x, 0.0) - log1p_z


def _row_iota(d):
    return lax.broadcasted_iota(jnp.int32, (SUBLANES, d), 0)


def _shift_down(cur, prev, s, row):
    return jnp.where(row >= s, pltpu.roll(cur, s, axis=0), pltpu.roll(prev, s, axis=0))


def _shift_up(cur, nxt, s, row):
    k = SUBLANES - s
    return jnp.where(row < k, pltpu.roll(cur, k, axis=0), pltpu.roll(nxt, k, axis=0))


def _scan_fwd(a, b, h_prev, row):
    for s in (1, 2, 4):
        a_s = jnp.where(row >= s, pltpu.roll(a, s, axis=0), 1.0)
        b_s = jnp.where(row >= s, pltpu.roll(b, s, axis=0), 0.0)
        b = a * b_s + b
        a = a * a_s
    return a * h_prev + b


def _scan_bwd(a_next, b, g_next, row):
    a = a_next
    for s in (1, 2, 4):
        k = SUBLANES - s
        a_s = jnp.where(row < k, pltpu.roll(a, k, axis=0), 1.0)
        b_s = jnp.where(row < k, pltpu.roll(b, k, axis=0), 0.0)
        b = a * b_s + b
        a = a * a_s
    return a * g_next + b


def _bcast_row(v, r):
    return jnp.broadcast_to(v[r:r + 1, :], v.shape)


def _chunks(n_rows, rc, body, init, reverse=False):
    n = n_rows // rc

    def step(i, carry):
        j = (n - 1 - i) if reverse else i
        return body(pl.multiple_of(j * rc, rc), carry)

    return lax.fori_loop(0, n, step, init)


def _adamw(w, g, m, v):
    m = ADAM_B1 * m + (1.0 - ADAM_B1) * g
    v = ADAM_B2 * v + (1.0 - ADAM_B2) * (g * g)
    m_hat = m / ADAM_BC1
    v_hat = v / ADAM_BC2
    delta = -ADAM_LR * (m_hat / (jnp.sqrt(v_hat) + ADAM_EPS) + ADAM_WD * w)
    return delta, m, v


def _mesh_pos():
    return lax.axis_index("x"), lax.axis_index("y"), lax.axis_index("c")


class _Gather:
    def __init__(self, blocks_of, send_sems, recv_sems, own_src=None):
        x, y, c = _mesh_pos()
        self.c = c
        self.me, self.sibling = (x, y, c), (x, y, 1 - c)
        self.chips = [(1 - x, y), (x, 1 - y), (1 - x, 1 - y)]
        self.blocks_of, self.send_sems, self.recv_sems = blocks_of, send_sems, recv_sems
        self.own_src = own_src

    def copy(self, a, k, block, to):
        src = self.blocks_of(a, *block)
        if block is self.me and self.own_src is not None:
            src = self.own_src[a]
        return pltpu.make_async_remote_copy(
            src_ref=src, dst_ref=self.blocks_of(a, *block),
            send_sem=self.send_sems.at[a * 7 + k], recv_sem=self.recv_sems.at[a * 7 + k],
            device_id=to, device_id_type=MESH)

    def start_own(self, a):
        self.copy(a, 0, self.me, self.sibling).start()
        for j, chip in enumerate(self.chips):
            self.copy(a, 1 + j, self.me, (*chip, self.c)).start()

    def wait_sibling(self, a):
        self.copy(a, 0, self.sibling, self.me).wait_recv()

    def wait_chip_and_pass_on(self, a, j):
        block = (*self.chips[j], self.c)
        self.copy(a, 1 + j, block, self.me).wait_recv()
        self.copy(a, 4 + j, block, self.sibling).start()

    def wait_passed_on(self, a, j):
        self.copy(a, 4 + j, (*self.chips[j], 1 - self.c), self.me).wait_recv()

    def wait_sends(self, a):
        self.copy(a, 0, self.me, self.sibling).wait_send()
        for j, chip in enumerate(self.chips):
            self.copy(a, 1 + j, self.me, (*chip, self.c)).wait_send()
            self.copy(a, 4 + j, (*chip, self.c), self.sibling).wait_send()

    def finish(self, a):
        for j in range(3):
            self.wait_chip_and_pass_on(a, j)
        self.wait_sibling(a)
        for j in range(3):
            self.wait_passed_on(a, j)
        self.wait_sends(a)


class _BalancedGather:
    def __init__(self, slot, send_sems, recv_sems, own_src):
        x, y, c = _mesh_pos()
        self.c = c
        self.me, self.sibling = (x, y, c), (x, y, 1 - c)
        self.chips = [(1 - x, y), (x, 1 - y), (1 - x, 1 - y)]
        self.slot, self.send_sems, self.recv_sems, self.own_src = slot, send_sems, recv_sems, own_src

    def half(self, a, block, which):
        ref = self.slot(a, *block)
        n = ref.shape[0] // 2
        return ref.at[pl.ds(which * n, n)]

    def copy(self, a, k, src, dst, to):
        return pltpu.make_async_remote_copy(
            src_ref=src, dst_ref=dst, send_sem=self.send_sems.at[a * 8 + k], recv_sem=self.recv_sems.at[a * 8 + k],
            device_id=to, device_id_type=MESH)

    def whole(self, a, k, block, to):
        src = self.own_src[a] if block is self.me else self.slot(a, *block)
        return self.copy(a, k, src, self.slot(a, *block), to)

    def halved(self, a, k, block, which, to):
        return self.copy(a, k, self.half(a, block, which), self.half(a, block, which), to)

    def on(self, chip):
        return (*self.chips[chip], self.c)

    def start_own(self, a):
        self.whole(a, 0, self.me, self.sibling).start()
        self.whole(a, 1, self.me, self.on(0)).start()
        self.whole(a, 2, self.me, self.on(1)).start()

    def wait_sibling(self, a):
        self.whole(a, 0, self.sibling, self.me).wait_recv()

    def on_neighbour(self, a, j):
        self.whole(a, 1 + j, self.on(j), self.me).wait_recv()
        self.halved(a, 3 + j, self.on(j), j, self.on(1 - j)).start()
        self.whole(a, 5 + j, self.on(j), self.sibling).start()

    def on_diagonal(self, a):
        self.halved(a, 3, self.on(2), 0, self.me).wait_recv()
        self.halved(a, 4, self.on(2), 1, self.me).wait_recv()
        self.whole(a, 7, self.on(2), self.sibling).start()

    def wait_passed_on(self, a, j):
        self.whole(a, 5 + j, (*self.chips[j], 1 - self.c), self.me).wait_recv()

    def wait_sends(self, a):
        self.whole(a, 0, self.me, self.sibling).wait_send()
        for j in range(2):
            self.whole(a, 1 + j, self.me, self.on(j)).wait_send()
            self.halved(a, 3 + j, self.on(j), j, self.on(1 - j)).wait_send()
        for j in range(3):
            self.whole(a, 5 + j, self.on(j), self.sibling).wait_send()


def _block_order():
    x, y, c = _mesh_pos()
    chips = [(x, y), (1 - x, y), (x, 1 - y), (1 - x, 1 - y)]
    return jnp.stack([4 * px + 2 * py + pc for px, py in chips for pc in (c, 1 - c)]).astype(jnp.int32)


def _gather_project(x, w_in, w_out, conv_pack, ln_g, tb):
    t_len, d = x.shape
    nb = t_len // tb
    cols = w_in.shape[1]
    mc = min(512, t_len)
    srcs = (w_in, w_out, conv_pack)
    dts = (MM, MM, F32)

    def body(order_ref, x_ref, win_ref, wout_ref, cp_ref, lng_ref, p_ref, xnt_ref, win_all, wout_all, cp_all,
             xnb, wall, st_out, st_cp, send_sems, recv_sems, cp_send, cp_recv, local_sems):
        i = pl.program_id(0)
        x_, y_, c_ = _mesh_pos()
        me = 4 * x_ + 2 * y_ + c_
        outs = (win_all, wout_all, cp_all)
        lands = (wall, wout_all, cp_all)
        stages = (wall.at[me], st_out, st_cp)
        gather = _BalancedGather(lambda a, px, py, pc: lands[a].at[4 * px + 2 * py + pc], send_sems, recv_sems, stages)
        small = _Gather(lambda a, px, py, pc: cp_all.at[4 * px + 2 * py + pc], cp_send, cp_recv, own_src=[st_cp])
        keep_own = [pltpu.make_async_copy(stages[a], outs[a].at[me], local_sems.at[a]) for a in range(3)]

        def keep(k):
            blk = order_ref[k]
            return pltpu.make_async_copy(wall.at[blk], win_all.at[blk], local_sems.at[2 + k])

        @pl.when(i == 0)
        def _():
            for a, (src, dst) in enumerate(zip((win_ref, wout_ref, cp_ref), stages)):
                rows = src.shape[0]
                rc = min(rows, 32)

                def cast(r, carry, src=src, dst=dst, rc=rc):
                    dst[pl.ds(r, rc), :] = src[pl.ds(r, rc), :].astype(dst.dtype)
                    return carry

                _chunks(rows, rc, cast, 0)
                if a < 2:
                    gather.start_own(a)
                else:
                    small.start_own(0)
                keep_own[a].start()

        @pl.when(i < nb)
        def _():
            xv = x_ref[...]
            r0 = lax.rsqrt(jnp.mean(xv * xv, axis=-1, keepdims=True) + RMS_EPS)
            xn = xv * r0 * lng_ref[...]
            xnb[pl.ds(pl.multiple_of(i * tb, tb), tb), :] = xn.astype(MM)
            xnt_ref[...] = xn.T.astype(MM)

        for k in range(N_DEV):
            @pl.when(i == nb + k)
            def _(k=k):
                if k == 1:
                    gather.wait_sibling(0)
                elif k == 2:
                    gather.on_neighbour(0, 0)
                    gather.on_neighbour(0, 1)
                elif k in (3, 5, 7):
                    gather.wait_passed_on(0, (k - 3) // 2)
                    if k == 3:
                        gather.on_neighbour(1, 0)
                        gather.on_neighbour(1, 1)
                    if k == 7:
                        gather.on_diagonal(1)
                elif k == 6:
                    gather.on_diagonal(0)
                blk = order_ref[k]
                if k:
                    keep(k).start()

                def project(r, carry):
                    rows = pl.ds(r, mc)
                    p_ref[rows, :] = _dot(xnb[rows, :], wall[blk]).astype(MM)
                    return carry

                _chunks(t_len, mc, project, 0)
                if k == N_DEV - 1:
                    gather.wait_sends(0)
                    gather.wait_sibling(1)
                    for j in range(3):
                        gather.wait_passed_on(1, j)
                    gather.wait_sends(1)
                    small.finish(0)
                    for cp in keep_own + [keep(kk) for kk in range(1, N_DEV)]:
                        cp.wait()

    vm = pl.BlockSpec(memory_space=pltpu.VMEM)
    hbm = pl.BlockSpec(memory_space=pl.ANY)
    grid_spec = pltpu.PrefetchScalarGridSpec(
        num_scalar_prefetch=1, grid=(nb + N_DEV,),
        in_specs=[pl.BlockSpec((tb, d), lambda i, o: (jnp.minimum(i, nb - 1), 0)), vm, vm, vm, vm],
        out_specs=(pl.BlockSpec((t_len, cols), lambda i, o: (0, o[jnp.maximum(i - nb, 0)])),
                   pl.BlockSpec((d, tb), lambda i, o: (0, jnp.minimum(i, nb - 1))), hbm, hbm, hbm),
        scratch_shapes=[pltpu.VMEM((t_len, d), MM), pltpu.VMEM((N_DEV,) + w_in.shape, MM),
                        pltpu.VMEM(w_out.shape, MM), pltpu.VMEM(conv_pack.shape, F32),
                        pltpu.SemaphoreType.DMA((16,)), pltpu.SemaphoreType.DMA((16,)),
                        pltpu.SemaphoreType.DMA((7,)), pltpu.SemaphoreType.DMA((7,)), pltpu.SemaphoreType.DMA((10,))])
    return pl.pallas_call(
        body, name="gather_project", grid_spec=grid_spec,
        out_shape=(jax.ShapeDtypeStruct((t_len, N_DEV * cols), MM),
                   jax.ShapeDtypeStruct((d, t_len), MM))
                  + tuple(jax.ShapeDtypeStruct((N_DEV,) + s.shape, dt) for s, dt in zip(srcs, dts)),
        compiler_params=_params(dimension_semantics=("arbitrary",)),
    )(_block_order(), x, w_in, w_out, conv_pack, ln_g)


def _forward(x, tgt, p, wout, wa_t, wi_t, sp, ones_c, ones_l, tb):
    t_len, d = x.shape
    nb = t_len // tb
    n_tiles, tw = wa_t.shape[0], wa_t.shape[1]
    hd_c, hd_l = d // N_CONV_HEADS, d // N_LRU_HEADS
    s8 = SUBLANES

    def body(x_ref, tgt_ref, p_ref, wout_ref, wa_ref, wi_ref, sp_ref, oc_ref, ol_ref,
             h_ref, dh_ref, dhb_ref, acc_ref, yc, czs, u, pa, pi,
             rcf, rlf, ybuf, tail_z, tail_xl, hcar):
        i = pl.program_id(0)
        row = _row_iota(d)

        @pl.when(i == 0)
        def _():
            tail_z[...] = jnp.zeros_like(tail_z)
            tail_xl[...] = jnp.zeros_like(tail_xl)
            hcar[...] = jnp.zeros_like(hcar)
            acc_ref[...] = jnp.zeros_like(acc_ref)

        def spr(r):
            return sp_ref[r:r + 1, :]

        def proj(rows, seg):
            return p_ref[rows, seg * d:(seg + 1) * d].astype(F32)

        w0, w1, w2 = spr(SP_CONV_W), spr(SP_CONV_W + 1), spr(SP_CONV_W + 2)
        l0, l1, l2, l3 = spr(SP_LRU_W), spr(SP_LRU_W + 1), spr(SP_LRU_W + 2), spr(SP_LRU_W + 3)
        lb = spr(SP_LRU_B)

        def convs(r, carry):
            zp, xp = carry
            rows16 = pl.ds(r, 2 * s8)
            bg16, xl16 = proj(rows16, P_B), proj(rows16, P_XL)
            z16 = proj(rows16, P_C) * proj(rows16, P_XC)
            for j in range(2):
                rows, sub = pl.ds(r + j * s8, s8), slice(j * s8, (j + 1) * s8)
                z, xl = z16[sub], xl16[sub]
                cz = w0 * _shift_down(z, zp, 2, row) + w1 * _shift_down(z, zp, 1, row) + w2 * z
                czs[rows, :] = cz
                yc[rows, :] = bg16[sub] * cz
                u[rows, :] = (l0 * _shift_down(xl, xp, 3, row) + l1 * _shift_down(xl, xp, 2, row)
                              + l2 * _shift_down(xl, xp, 1, row) + l3 * xl + lb)
                zp, xp = z, xl
            return zp, xp

        z_last, xl_last = _chunks(tb, 2 * s8, convs, (tail_z[...], tail_xl[...]))
        tail_z[...] = z_last
        tail_xl[...] = xl_last

        ub = u[...].astype(MM)
        for k in range(n_tiles):
            sl = slice(k * tw, (k + 1) * tw)
            pa[:, sl] = _dot(ub[:, sl], wa_ref[k])
            pi[:, sl] = _dot(ub[:, sl], wi_ref[k])
        rcf[...] = _head_rstd(yc[...], oc_ref[...], hd_c)

        c8 = RG_LRU_C * _log_sigmoid(spr(SP_LAM))
        b_a, b_i = spr(SP_B_A), spr(SP_B_I)

        def lru(r, hp):
            rows = pl.ds(r, SUBLANES)
            ra = _sigmoid(pa[rows, :] + b_a)
            ii = _sigmoid(pi[rows, :] + b_i)
            pa[rows, :] = ra
            pi[rows, :] = ii
            la = ra * c8
            a = jnp.exp(la)
            mult = _lru_input_scale(la, a)
            h = _scan_fwd(a, mult * (ii * u[rows, :]), hp, row)
            h_ref[rows, :] = h
            return _bcast_row(h, SUBLANES - 1)

        hcar[...] = _chunks(tb, SUBLANES, lru, hcar[...])
        rlf[...] = _head_rstd(h_ref[...], ol_ref[...], hd_l)

        g_c, g_l = spr(SP_CONV_G), spr(SP_LRU_G)

        def gate(r, carry):
            rows = pl.ds(r, 2 * s8)
            gc, gl = proj(rows, P_GC), proj(rows, P_GL)
            ybuf[rows, 0:d] = (yc[rows, :] * rcf[rows, :] * g_c * (gc * _sigmoid(gc))).astype(MM)
            ybuf[rows, d:2 * d] = (h_ref[rows, :] * rlf[rows, :] * g_l * (gl * _sigmoid(gl))).astype(MM)
            return carry

        _chunks(tb, 2 * s8, gate, 0)

        hres = x_ref[...] + _dot(ybuf[...], wout_ref[...])
        rf = lax.rsqrt(jnp.mean(hres * hres, axis=-1, keepdims=True) + RMS_EPS)
        hn = hres * rf
        fg = spr(SP_FINAL_G)
        err = hn * fg - tgt_ref[...]
        dout = err * (1.0 / d)
        acc_ref[0:SUBLANES, :] += (err * err).reshape(tb // SUBLANES, SUBLANES, d).sum(axis=0)
        acc_ref[SUBLANES:2 * SUBLANES, :] += (dout * hn).reshape(tb // SUBLANES, SUBLANES, d).sum(axis=0)
        gd = dout * fg
        dhres = rf * (gd - hn * jnp.mean(gd * hn, axis=-1, keepdims=True))
        dh_ref[...] = dhres
        dhb_ref[...] = dhres.astype(MM)

    vm = pl.BlockSpec(memory_space=pltpu.VMEM)
    blk = lambda w: pl.BlockSpec((tb, w), lambda i: (i, 0))
    buf = pltpu.VMEM((tb, d), F32)
    car = pltpu.VMEM((SUBLANES, d), F32)
    return pl.pallas_call(
        body, name="forward", grid=(nb,),
        in_specs=[blk(d), blk(d), blk(6 * d), vm, vm, vm, vm, vm, vm],
        out_specs=(blk(d), blk(d), blk(d), pl.BlockSpec((2 * SUBLANES, d), lambda i: (0, 0))) + (blk(d),) * 5,
        out_shape=(jax.ShapeDtypeStruct((t_len, d), F32),
                   jax.ShapeDtypeStruct((t_len, d), F32),
                   jax.ShapeDtypeStruct((t_len, d), MM),
                   jax.ShapeDtypeStruct((2 * SUBLANES, d), F32))
                  + (jax.ShapeDtypeStruct((t_len, d), F32),) * 5,
        scratch_shapes=[buf] * 2 + [pltpu.VMEM((tb, 2 * d), MM), car, car, car],
        compiler_params=_params(dimension_semantics=("arbitrary",)),
    )(x, tgt, p, wout, wa_t, wi_t, sp, ones_c, ones_l)


def _backward(p, h, dh, saved, facc, wout, wa_t, wi_t, sp, ones_c, ones_l, tb):
    t_len, d = h.shape
    nb = t_len // tb
    n_tiles, tw = wa_t.shape[0], wa_t.shape[1]
    hd_c, hd_l = d // N_CONV_HEADS, d // N_LRU_HEADS
    g_rows = 2 * n_tiles * hd_l
    s8 = SUBLANES

    def body(p_ref, h_ref, hhalo_ref, dh_ref, yc, czs, u, ra_ref, ii_ref, facc_ref,
             wout_ref, wa_ref, wi_ref, sp_ref, oc_ref, ol_ref,
             dp_ref, yt_ref, slab_v, slab_g,
             hh, dy, ybuf, rcf, rlf, qc, ql, dyc_hat, dyl_hat, dpa, dpi, du, gwa_ref, gwi_ref, acc_ref,
             car_dcz, car_a, car_g, car_du):
        i = pl.program_id(0)
        blk_idx = nb - 1 - i
        row = _row_iota(d)

        @pl.when(i == 0)
        def _():
            for ref in (car_dcz, car_a, car_g, car_du, gwa_ref, gwi_ref, acc_ref):
                ref[...] = jnp.zeros_like(ref)

        def spr(r):
            return sp_ref[r:r + 1, :]

        def proj(rows, seg):
            return p_ref[rows, seg * d:(seg + 1) * d].astype(F32)

        def put(rows, seg, halves):
            dp_ref[rows, seg * d:(seg + 1) * d] = jnp.concatenate(halves, axis=0).astype(MM)

        def acc_add(group, val):
            acc_ref[group * s8:(group + 1) * s8, :] += val

        live = jnp.where(blk_idx > 0, 1.0, 0.0).astype(F32)
        hh[0:s8, :] = hhalo_ref[...] * live
        hh[s8:, :] = h_ref[...]

        dy[...] = _dot_nt(dh_ref[...].astype(MM), wout_ref[...])

        w0, w1, w2 = spr(SP_CONV_W), spr(SP_CONV_W + 1), spr(SP_CONV_W + 2)
        l0, l1, l2, l3 = spr(SP_LRU_W), spr(SP_LRU_W + 1), spr(SP_LRU_W + 2), spr(SP_LRU_W + 3)

        rcf[...] = _head_rstd(yc[...], oc_ref[...], hd_c)
        rlf[...] = _head_rstd(h_ref[...], ol_ref[...], hd_l)

        g_c, g_l = spr(SP_CONV_G), spr(SP_LRU_G)

        def gates(r, carry):
            rows = pl.ds(r, 2 * s8)
            for (seg, off_y, src, rstd, gain, q, dhat, grp) in (
                    (P_GC, 0, yc, rcf, g_c, qc, dyc_hat, A_CONV_G),
                    (P_GL, d, h_ref, rlf, g_l, ql, dyl_hat, A_LRU_G)):
                gt = proj(rows, seg)
                sg = _sigmoid(gt)
                silu = gt * sg
                yhat = src[rows, :] * rstd[rows, :]
                nrm = yhat * gain
                ybuf[rows, off_y:off_y + d] = nrm * silu
                dout = dy[rows, off_y:off_y + d]
                dnrm = dout * silu
                dp_ref[rows, seg * d:(seg + 1) * d] = (dout * nrm * (sg * (1.0 + gt * (1.0 - sg)))).astype(MM)
                dg = dnrm * yhat
                acc_add(grp, dg[0:s8] + dg[s8:])
                dh_ = dnrm * gain
                dhat[rows, :] = dh_
                q[rows, :] = dh_ * yhat
            return carry

        _chunks(tb, 2 * s8, gates, 0)

        qc[...] = _head_sums(qc[...], oc_ref[...]) * (1.0 / hd_c)
        ql[...] = _head_sums(ql[...], ol_ref[...]) * (1.0 / hd_l)
        yt_ref[...] = ybuf[...].T.astype(MM)

        c8 = RG_LRU_C * _log_sigmoid(spr(SP_LAM))

        def conv_mixer(r, dcz_n):
            rows16 = pl.ds(r, 2 * s8)
            bg16, cg16, xc16 = proj(rows16, P_B), proj(rows16, P_C), proj(rows16, P_XC)
            z16 = cg16 * xc16
            d_b, d_c, d_x = [None, None], [None, None], [None, None]
            for j in (1, 0):
                rows, sub = pl.ds(r + j * s8, s8), slice(j * s8, (j + 1) * s8)
                rstd = rcf[rows, :]
                yhat = yc[rows, :] * rstd
                dyc = rstd * (dyc_hat[rows, :] - yhat * qc[rows, :])
                d_b[j] = dyc * czs[rows, :]
                dcz = dyc * bg16[sub]
                up1, up2 = _shift_up(dcz, dcz_n, 1, row), _shift_up(dcz, dcz_n, 2, row)
                dz = w2 * dcz + w1 * up1 + w0 * up2
                d_c[j] = dz * xc16[sub]
                d_x[j] = dz * cg16[sub]
                z = z16[sub]
                acc_add(A_CONV_W, up2 * z)
                acc_add(A_CONV_W + 1, up1 * z)
                acc_add(A_CONV_W + 2, dcz * z)
                dcz_n = dcz
            put(rows16, P_B, d_b)
            put(rows16, P_C, d_c)
            put(rows16, P_XC, d_x)
            return dcz_n

        car_dcz[...] = _chunks(tb, 2 * s8, conv_mixer, car_dcz[...], reverse=True)

        def lru_mixer(r, carry):
            a_n, g_n = carry
            for j in (1, 0):
                rows = pl.ds(r + j * s8, s8)
                rstd = rlf[rows, :]
                hcur = hh[pl.ds(r + (j + 1) * s8, s8), :]
                hhat = hcur * rstd
                dh_out = rstd * (dyl_hat[rows, :] - hhat * ql[rows, :])
                ra = ra_ref[rows, :]
                la = ra * c8
                a = jnp.exp(la)
                g = _scan_bwd(_shift_up(a, a_n, 1, row), dh_out, g_n, row)
                da = g * _shift_down(hcur, hh[pl.ds(r + j * s8, s8), :], 1, row)
                ii = ii_ref[rows, :]
                uu = u[rows, :]
                mult = _lru_input_scale(la, a)
                dmult = g * (ii * uu)
                ds = g * mult
                dla = a * (da - dmult * a / mult)
                acc_add(A_LAM, dla * ra)
                dpa_ = dla * c8 * ra * (1.0 - ra)
                dpi_ = ds * uu * ii * (1.0 - ii)
                acc_add(A_B_A, dpa_)
                acc_add(A_B_I, dpi_)
                dpa[rows, :] = dpa_
                dpi[rows, :] = dpi_
                du[rows, :] = ds * ii
                a_n, g_n = a, _bcast_row(g, 0)
            return a_n, g_n

        a_f, g_f = _chunks(tb, 2 * s8, lru_mixer, (car_a[...], car_g[...]), reverse=True)
        car_a[...] = a_f
        car_g[...] = g_f

        dpab = dpa[...].astype(MM)
        dpib = dpi[...].astype(MM)
        for k in range(n_tiles):
            sl = slice(k * tw, (k + 1) * tw)
            du[:, sl] += _dot_nt(dpab[:, sl], wa_ref[k]) + _dot_nt(dpib[:, sl], wi_ref[k])
            ut = u[:, sl].T.astype(MM)
            gwa_ref[k] += _dot(ut, dpab[:, sl])
            gwi_ref[k] += _dot(ut, dpib[:, sl])

        def lru_conv(r, du_n):
            rows16 = pl.ds(r, 2 * s8)
            xl16 = proj(rows16, P_XL)
            d_xl = [None, None]
            for j in (1, 0):
                rows, sub = pl.ds(r + j * s8, s8), slice(j * s8, (j + 1) * s8)
                dut = du[rows, :]
                up1, up2, up3 = (_shift_up(dut, du_n, s, row) for s in (1, 2, 3))
                d_xl[j] = l3 * dut + l2 * up1 + l1 * up2 + l0 * up3
                xl = xl16[sub]
                acc_add(A_LRU_W, up3 * xl)
                acc_add(A_LRU_W + 1, up2 * xl)
                acc_add(A_LRU_W + 2, up1 * xl)
                acc_add(A_LRU_W + 3, dut * xl)
                acc_add(A_LRU_B, dut)
                du_n = dut
            put(rows16, P_XL, d_xl)
            return du_n

        car_du[...] = _chunks(tb, 2 * s8, lru_conv, car_du[...], reverse=True)

        @pl.when(i == nb - 1)
        def _():
            def rowsum(ref, group):
                return jnp.sum(ref[group * s8:(group + 1) * s8, :], axis=0, keepdims=True)

            slab_v[...] = jnp.zeros_like(slab_v)
            loss = jnp.sum(rowsum(facc_ref, 0), axis=1, keepdims=True) * (0.5 / d)
            rows = {SL_LOSS: jnp.broadcast_to(loss, (1, d)), SL_FINAL_G: rowsum(facc_ref, 1),
                    SL_LRU_B: rowsum(acc_ref, A_LRU_B), SL_B_A: rowsum(acc_ref, A_B_A), SL_B_I: rowsum(acc_ref, A_B_I),
                    SL_LAM: rowsum(acc_ref, A_LAM), SL_CONV_G: rowsum(acc_ref, A_CONV_G), SL_LRU_G: rowsum(acc_ref, A_LRU_G)}
            for k in range(3):
                rows[SL_CONV_W + k] = rowsum(acc_ref, A_CONV_W + k)
            for k in range(4):
                rows[SL_LRU_W + k] = rowsum(acc_ref, A_LRU_W + k)
            for r, val in rows.items():
                slab_v[r:r + 1, :] = val
            head_of_lane = lax.broadcasted_iota(jnp.int32, (hd_l, tw), 1) // hd_l
            for mtx, g_ref in enumerate((gwa_ref, gwi_ref)):
                for k in range(n_tiles):
                    packed = jnp.zeros((hd_l, tw), F32)
                    for a in range(tw // hd_l):
                        packed = jnp.where(head_of_lane == a, g_ref[k, a * hd_l:(a + 1) * hd_l, :], packed)
                    slab_g[(mtx * n_tiles + k) * hd_l:(mtx * n_tiles + k + 1) * hd_l, :] = packed.astype(MM)

    vm = pl.BlockSpec(memory_space=pltpu.VMEM)
    rev = lambda w: pl.BlockSpec((tb, w), lambda i: (nb - 1 - i, 0))
    halo = lambda rows, w: pl.BlockSpec((rows, w), lambda i: (jnp.maximum((nb - 1 - i) * (tb // rows) - 1, 0), 0))
    const = lambda shape: pl.BlockSpec(shape, lambda i: (0,) * len(shape))
    buf = lambda w: pltpu.VMEM((tb, w), F32)
    car = pltpu.VMEM((SUBLANES, d), F32)
    return pl.pallas_call(
        body, name="backward", grid=(nb,),
        in_specs=[rev(6 * d), rev(d), halo(SUBLANES, d), rev(d)] + [rev(d)] * 5 + [vm, vm, vm, vm, vm, vm, vm],
        out_specs=(rev(6 * d), pl.BlockSpec((2 * d, tb), lambda i: (0, nb - 1 - i)),
                   const((SL_ROWS, d)), const((g_rows, tw))),
        out_shape=(jax.ShapeDtypeStruct((t_len, 6 * d), MM),
                   jax.ShapeDtypeStruct((2 * d, t_len), MM),
                   jax.ShapeDtypeStruct((SL_ROWS, d), F32),
                   jax.ShapeDtypeStruct((g_rows, tw), MM)),
        scratch_shapes=[pltpu.VMEM((SUBLANES + tb, d), F32), buf(2 * d), buf(2 * d)] + [buf(d)] * 9
                       + [pltpu.VMEM((n_tiles, tw, tw), F32), pltpu.VMEM((n_tiles, tw, tw), F32),
                          pltpu.VMEM((A_GROUPS * SUBLANES, d), F32), car, car, car, car],
        compiler_params=_params(dimension_semantics=("arbitrary",)),
    )(p, h, h, dh, *saved, facc, wout, wa_t, wi_t, sp, ones_c, ones_l)


def _input_grad(dp, win_all, x, dh, sp, part, tb):
    t_len, d = x.shape
    nb = t_len // tb
    cols = win_all.shape[2]
    mid = min(nb - 1, (5 * nb) // 8)
    rc = 32

    def body(dp_ref, win_ref, x_ref, dh_ref, sp_ref, part_ref, gx_ref, ln_ref, direct, passing, relayed,
             send_sems, recv_sems, local_sems, acc_ref, ln_all, ln_send, ln_recv, mine, theirs):
        i = pl.program_id(0)
        x_, y_, c_ = _mesh_pos()
        first, second = 1 - c_, c_
        nbr1 = (x_ ^ c_, y_ ^ (1 - c_), c_)
        nbr2 = (x_ ^ (1 - c_), y_ ^ c_, c_)

        def remote(src, dst, k, to):
            return pltpu.make_async_remote_copy(src_ref=src, dst_ref=dst, send_sem=send_sems.at[k], recv_sem=recv_sems.at[k],
                                                device_id=to, device_id_type=MESH)

        to_first = [remote(part_ref.at[first], direct, 0, nbr1), remote(part_ref.at[2], passing, 1, nbr1)]
        to_second = remote(theirs, relayed, 2, nbr2)

        @pl.when(i == 0)
        def _():
            acc_ref[...] = jnp.zeros_like(acc_ref)
            for cp in to_first:
                cp.start()

        dxn = _dot_nt(dp_ref[:, 0:cols], win_ref[0])
        for j in range(1, N_DEV):
            dxn += _dot_nt(dp_ref[:, j * cols:(j + 1) * cols], win_ref[j])
        xv = x_ref[...]
        r0 = lax.rsqrt(jnp.mean(xv * xv, axis=-1, keepdims=True) + RMS_EPS)
        xhat = xv * r0
        acc_ref[...] += (dxn * xhat).reshape(tb // SUBLANES, SUBLANES, d).sum(axis=0)
        dxh = dxn * sp_ref[SP_LN_G:SP_LN_G + 1, :]
        gx_ref[...] = dh_ref[...] + r0 * (dxh - xhat * jnp.mean(dxh * xhat, axis=-1, keepdims=True))

        @pl.when(i == mid)
        def _():
            to_first[1].wait_recv()
            loads = [pltpu.make_async_copy(part_ref.at[second], mine, local_sems.at[0]),
                     pltpu.make_async_copy(passing, theirs, local_sems.at[1])]
            for cp in loads:
                cp.start()
            for cp in loads:
                cp.wait()

            def add(r, carry):
                rows = pl.ds(r, rc)
                theirs[rows, :] = (mine[rows, :].astype(F32) + theirs[rows, :].astype(F32)).astype(MM)
                return carry

            _chunks(mine.shape[0], rc, add, 0)
            to_second.start()

        @pl.when(i == nb - 1)
        def _():
            to_first[0].wait_recv()
            to_second.wait_recv()
            for cp in to_first + [to_second]:
                cp.wait_send()
            ln_all[4 * x_ + 2 * y_ + c_] = jnp.broadcast_to(jnp.sum(acc_ref[...], axis=0, keepdims=True), acc_ref.shape)
            gather = _Gather(lambda a, px, py, pc: ln_all.at[4 * px + 2 * py + pc], ln_send, ln_recv)
            gather.start_own(0)
            gather.finish(0)
            total = ln_all[0]
            for dev in range(1, N_DEV):
                total = total + ln_all[dev]
            ln_ref[...] = total

    vm = pl.BlockSpec(memory_space=pltpu.VMEM)
    hbm = pl.BlockSpec(memory_space=pl.ANY)
    blk = lambda w: pl.BlockSpec((tb, w), lambda i: (i, 0))
    landed = jax.ShapeDtypeStruct(part.shape[1:], part.dtype)
    outs = pl.pallas_call(
        body, name="input_grad", grid=(nb,),
        in_specs=[blk(6 * d), vm, blk(d), blk(d), vm, hbm],
        out_specs=(blk(d), pl.BlockSpec((SUBLANES, d), lambda i: (0, 0)), hbm, hbm, hbm),
        out_shape=(jax.ShapeDtypeStruct((t_len, d), F32), jax.ShapeDtypeStruct((SUBLANES, d), F32), landed, landed, landed),
        scratch_shapes=[pltpu.SemaphoreType.DMA((3,)), pltpu.SemaphoreType.DMA((3,)), pltpu.SemaphoreType.DMA((2,)),
                        pltpu.VMEM((SUBLANES, d), F32), pltpu.VMEM((N_DEV, SUBLANES, d), F32),
                        pltpu.SemaphoreType.DMA((7,)), pltpu.SemaphoreType.DMA((7,)),
                        pltpu.VMEM(part.shape[1:], MM), pltpu.VMEM(part.shape[1:], MM)],
        compiler_params=_params(dimension_semantics=("arbitrary",)),
    )(dp, win_all, x, dh, sp, part)
    return outs[0], outs[1], (outs[2], outs[4])


_CHIP_RELATIONS = [(0, 0), (1, 0), (0, 1), (1, 1)]


def _related_block(k, core):
    x, y, _ = _mesh_pos()
    fx, fy = _CHIP_RELATIONS[k]
    return 4 * (x ^ fx) + 2 * (y ^ fy) + core


class _ChipExchange:
    def __init__(self, part_refs, land_refs, send_sems, recv_sems):
        self.part_refs, self.land_refs, self.send_sems, self.recv_sems = part_refs, land_refs, send_sems, recv_sems

    def copies(self):
        x, y, c = _mesh_pos()
        for a in range(len(self.part_refs)):
            for k in (1, 2, 3):
                fx, fy = _CHIP_RELATIONS[k]
                yield pltpu.make_async_remote_copy(
                    src_ref=self.part_refs[a].at[k - 1], dst_ref=self.land_refs[a].at[k - 1],
                    send_sem=self.send_sems.at[3 * a + k - 1], recv_sem=self.recv_sems.at[3 * a + k - 1],
                    device_id=(x ^ fx, y ^ fy, c), device_id_type=MESH)

    def start(self):
        for cp in self.copies():
            cp.start()

    def finish(self):
        for cp in self.copies():
            cp.wait_recv()
        for cp in self.copies():
            cp.wait_send()


def _weight_grad_stage1(name, blk_shape, n_split, operands, in_specs, product, riders=(), slabs=()):
    n_rows, n_cols = blk_shape
    rs = n_rows // n_split
    rc = 32
    n_in, n_ride, n_slab = len(operands), len(riders), len(slabs)
    _, _, c = _mesh_pos()
    order = jnp.stack([_related_block(k, 1 - c) for k in range(4)]
                      + [_related_block(k, c) for k in (1, 2, 3, 0)]).astype(jnp.int32)

    def body(order_ref, *refs):
        ins = refs[:n_in]
        ride_in = refs[n_in:n_in + n_ride]
        slab_in = refs[n_in + n_ride:n_in + n_ride + n_slab]
        n_op = n_in + n_ride + n_slab
        part_ref, own_ref = refs[n_op:n_op + 2]
        ride_out = refs[n_op + 2:n_op + 2 + n_ride]
        gathered = refs[n_op + 2 + n_ride:n_op + 2 + n_ride + n_slab]
        (gbuf, sendbuf, from_sib, send_sems, recv_sems, ride_send, ride_recv,
         slab_send, slab_recv, slab_local) = refs[n_op + 2 + n_ride + n_slab:]
        exchange = _ChipExchange(ride_in, ride_out, ride_send, ride_recv)
        s = pl.program_id(0)
        x, y, c = _mesh_pos()
        me = 4 * x + 2 * y + c
        gather = _BalancedGather(lambda a, px, py, pc: gathered[a].at[4 * px + 2 * py + pc], slab_send, slab_recv, slab_in)
        keep_own = [pltpu.make_async_copy(slab_in[a], gathered[a].at[me], slab_local.at[a]) for a in range(n_slab)]

        def to_sibling(k):
            return pltpu.make_async_remote_copy(
                src_ref=sendbuf.at[k], dst_ref=from_sib.at[k], send_sem=send_sems.at[k], recv_sem=recv_sems.at[k],
                device_id=(x, y, 1 - c), device_id_type=MESH)

        @pl.when(s == 0)
        def _():
            exchange.start()
            for a in range(n_slab):
                gather.start_own(a)
                keep_own[a].start()

        @pl.when(s == 5)
        def _():
            for a in range(n_slab):
                gather.on_neighbour(a, 0)
                gather.on_neighbour(a, 1)

        @pl.when(s == 7)
        def _():
            for a in range(n_slab):
                gather.on_diagonal(a)

        for h in range(n_split):
            gbuf[h * rs:(h + 1) * rs, :] = product(ins, h)

        @pl.when(s < 4)
        def _():
            def narrow(r, carry):
                sendbuf[s, pl.ds(r, rc), :] = gbuf[pl.ds(r, rc), :].astype(MM)
                return carry

            _chunks(n_rows, rc, narrow, 0)
            to_sibling(s).start()

        @pl.when(s >= 4)
        def _():
            k = jnp.where(s == 7, 0, s - 3)
            to_sibling(k).wait_recv()

            @pl.when(s < 7)
            def _():
                def add(r, carry):
                    rows = pl.ds(r, rc)
                    part_ref[0, rows, :] = (gbuf[rows, :] + from_sib[k, rows, :].astype(F32)).astype(MM)
                    return carry

                _chunks(n_rows, rc, add, 0)

            @pl.when(s == 7)
            def _():
                def add(r, carry):
                    rows = pl.ds(r, rc)
                    own_ref[rows, :] = gbuf[rows, :] + from_sib[0, rows, :].astype(F32)
                    return carry

                _chunks(n_rows, rc, add, 0)
                for kk in range(4):
                    to_sibling(kk).wait_send()
                exchange.finish()
                for a in range(n_slab):
                    gather.wait_sibling(a)
                    for j in range(3):
                        gather.wait_passed_on(a, j)
                    gather.wait_sends(a)
                    keep_own[a].wait()

    hbm = pl.BlockSpec(memory_space=pl.ANY)
    grid_spec = pltpu.PrefetchScalarGridSpec(
        num_scalar_prefetch=1, grid=(N_DEV,), in_specs=list(in_specs) + [hbm] * (n_ride + n_slab),
        out_specs=(pl.BlockSpec((1, n_rows, n_cols), lambda s, o: (jnp.clip(s - 4, 0, 2), 0, 0)),
                   pl.BlockSpec((n_rows, n_cols), lambda s, o: (0, 0))) + (hbm,) * (n_ride + n_slab),
        scratch_shapes=[pltpu.VMEM((n_rows, n_cols), F32), pltpu.VMEM((4, n_rows, n_cols), MM),
                        pltpu.VMEM((4, n_rows, n_cols), MM),
                        pltpu.SemaphoreType.DMA((4,)), pltpu.SemaphoreType.DMA((4,)),
                        pltpu.SemaphoreType.DMA((max(3 * n_ride, 1),)), pltpu.SemaphoreType.DMA((max(3 * n_ride, 1),)),
                        pltpu.SemaphoreType.DMA((max(8 * n_slab, 1),)), pltpu.SemaphoreType.DMA((max(8 * n_slab, 1),)),
                        pltpu.SemaphoreType.DMA((max(n_slab, 1),))])
    outs = pl.pallas_call(
        body, name=name, grid_spec=grid_spec,
        out_shape=(jax.ShapeDtypeStruct((3, n_rows, n_cols), MM), jax.ShapeDtypeStruct((n_rows, n_cols), F32))
                  + tuple(jax.ShapeDtypeStruct(p.shape, p.dtype) for p in riders)
                  + tuple(jax.ShapeDtypeStruct((N_DEV,) + a.shape, a.dtype) for a in slabs),
        compiler_params=_params(dimension_semantics=("arbitrary",)),
    )(order, *operands, *riders, *slabs)
    return outs[0], outs[1], outs[2:2 + n_ride], outs[2 + n_ride:]


def _weight_grad_in(xnt, dp, riders, slabs):
    d, t_len = xnt.shape
    cols = dp.shape[1] // N_DEV
    half = d // 2
    return _weight_grad_stage1(
        "weight_grad_in", (d, cols), 2, (xnt, dp),
        [pl.BlockSpec(memory_space=pltpu.VMEM), pl.BlockSpec((t_len, cols), lambda s, o: (0, o[s]))],
        lambda refs, h: _dot(refs[0][h * half:(h + 1) * half, :], refs[1][...]), riders, slabs)


def _weight_grad_out(yt, dhb):
    d2, t_len = yt.shape
    d = dhb.shape[1]
    rows = d2 // N_DEV
    return _weight_grad_stage1(
        "weight_grad_out", (rows, d), 1, (yt, dhb),
        [pl.BlockSpec((rows, t_len), lambda s, o: (o[s], 0)), pl.BlockSpec(memory_space=pltpu.VMEM)],
        lambda refs, h: _dot(refs[0][...], refs[1][...]))


def _update_shard(own, others, w, m, v, name):
    n_rows, n_cols = w.shape
    rb = min(256, n_rows)
    n_other = len(others)

    def body(own_ref, *refs):
        other_refs = refs[:n_other]
        w_ref, m_ref, v_ref, grad_ref, delta_ref, mo_ref, vo_ref = refs[n_other:]
        g = own_ref[...]
        for ref in other_refs:
            for k in range(ref.shape[0] if len(ref.shape) == 3 else 1):
                g = g + (ref[k] if len(ref.shape) == 3 else ref[...]).astype(F32)
        delta, m_new, v_new = _adamw(w_ref[...], g, m_ref[...], v_ref[...])
        grad_ref[...] = g
        delta_ref[...] = delta
        mo_ref[...] = m_new
        vo_ref[...] = v_new

    blk = pl.BlockSpec((rb, n_cols), lambda i: (i, 0))
    stacked = lambda n: pl.BlockSpec((n, rb, n_cols), lambda i: (0, i, 0))
    out = jax.ShapeDtypeStruct((n_rows, n_cols), F32)
    return pl.pallas_call(
        body, name=name, grid=(n_rows // rb,),
        in_specs=[blk] + [stacked(o.shape[0]) if o.ndim == 3 else blk for o in others] + [blk, blk, blk],
        out_specs=(blk, blk, blk, blk), out_shape=(out, out, out, out),
        compiler_params=_params(dimension_semantics=("arbitrary",)),
    )(own, *others, w, m, v)


def _small_update(gat_v, gat_g, ln_tot, vec_w, vec_m, vec_v, gates, convs):
    n_vec = len(vec_w)
    n_heads, hd, _ = gates[0].shape
    tw = gat_g.shape[2]
    s8 = SUBLANES
    per = tw // hd
    n_tiles = n_heads // per
    cc = convs[0].shape[1]
    n_in = 3 + 3 * n_vec + 12

    def body(*refs):
        gv_ref, gg_ref, ln_ref = refs[:3]
        w_refs, m_refs, v_refs = (refs[3 + j * n_vec:3 + (j + 1) * n_vec] for j in range(3))
        gate_refs = refs[3 + 3 * n_vec:3 + 3 * n_vec + 6]
        conv_refs = refs[3 + 3 * n_vec + 6:n_in]
        loss_o = refs[n_in]
        kinds = [refs[n_in + 1 + j * (n_vec + 4):n_in + 1 + (j + 1) * (n_vec + 4)] for j in range(4)]
        tv, tg = refs[n_in + 1 + 4 * (n_vec + 4):]
        x, y, c = _mesh_pos()
        me = 4 * x + 2 * y + c

        def emit(k_out, w, g, m, v):
            delta, m_new, v_new = _adamw(w, g, m, v)
            for ref, val in zip(k_out, (g, delta, m_new, v_new)):
                ref[...] = val

        total = gv_ref[0]
        for dev in range(1, N_DEV):
            total = total + gv_ref[dev]
        tv[...] = total
        tv[SL_LN_G:SL_LN_G + 1, :] = ln_ref[0:1, :]

        def sum_gates(r, carry):
            rows = pl.ds(r, 2 * s8)
            part = gg_ref[0, rows, :].astype(F32)
            for dev in range(1, N_DEV):
                part = part + gg_ref[dev, rows, :].astype(F32)
            tg[rows, :] = part
            return carry

        _chunks(tg.shape[0], 2 * s8, sum_gates, 0)
        loss_o[...] = jnp.broadcast_to(tv[SL_LOSS:SL_LOSS + 1, 0:LANES], loss_o.shape)
        for p in range(n_vec):
            w, g = w_refs[p][...], tv[SL_LN_G + p, :]
            if SL_LN_G + p == SL_LAM:
                g = g * (RG_LRU_C * jax.nn.sigmoid(-w))
            emit([k_out[p] for k_out in kinds], w, g, m_refs[p][...], v_refs[p][...])
        lanes = pl.ds(pl.multiple_of(me * cc, cc), cc)
        for j, (row0, n) in enumerate(((SL_CONV_W, 3), (SL_LRU_W, 4))):
            w_ref, m_ref, v_ref = conv_refs[3 * j:3 * j + 3]
            emit([k_out[n_vec + 2 + j] for k_out in kinds], w_ref[...], tv[row0:row0 + n, lanes], m_ref[...], v_ref[...])
        for mtx in range(2):
            w_ref, m_ref, v_ref = gate_refs[3 * mtx:3 * mtx + 3]
            for k in range(n_tiles):
                tile = tg[(mtx * n_tiles + k) * hd:(mtx * n_tiles + k + 1) * hd, :]
                for a in range(per):
                    head = k * per + a
                    g = tile[:, a * hd:(a + 1) * hd]
                    delta, m_new, v_new = _adamw(w_ref[head], g, m_ref[head], v_ref[head])
                    for k_out, val in zip(kinds, (g, delta, m_new, v_new)):
                        k_out[n_vec + mtx][head] = val

    vm = pl.BlockSpec(memory_space=pltpu.VMEM)
    like = lambda a: jax.ShapeDtypeStruct(a.shape, F32)
    per_kind = tuple(like(a) for a in vec_w) + (like(gates[0]), like(gates[3]), like(convs[0]), like(convs[3]))
    n_out = 1 + 4 * len(per_kind)
    outs = pl.pallas_call(
        body, name="small_update",
        in_specs=[vm] * n_in, out_specs=(vm,) * n_out,
        out_shape=(jax.ShapeDtypeStruct((SUBLANES, LANES), F32),) + per_kind * 4,
        scratch_shapes=[pltpu.VMEM(gat_v.shape[1:], F32), pltpu.VMEM(gat_g.shape[1:], F32)],
        compiler_params=_params(),
    )(gat_v, gat_g, ln_tot, *vec_w, *vec_m, *vec_v, *gates, *convs)
    return outs[0], [outs[1 + j * len(per_kind):1 + (j + 1) * len(per_kind)] for j in range(4)]


def _head_ones(head_dim, tw):
    lane = jnp.arange(tw) // head_dim
    return (lane[:, None] == lane[None, :]).astype(MM)


def _gate_tiles(w, tw):
    n_heads, hd, _ = w.shape
    per = tw // hd
    w4 = w.reshape(n_heads // per, per, hd, hd)
    eye = jnp.eye(per, dtype=w.dtype)
    return (w4[:, :, :, None, :] * eye[None, :, None, :, None]).reshape(n_heads // per, tw, tw)


def kernel(x, ln_g, w_in, conv_w, lru_conv_w, lru_conv_b, w_a, b_a, w_i, b_i, lam, conv_out_g, lru_out_g, w_out, final_g, loss_target, m_ln_g, m_w_in, m_conv_w, m_lru_conv_w, m_lru_conv_b, m_w_a, m_b_a, m_w_i, m_b_i, m_lam, m_conv_out_g, m_lru_out_g, m_w_out, m_final_g, v_ln_g, v_w_in, v_conv_w, v_lru_conv_w, v_lru_conv_b, v_w_a, v_b_a, v_w_i, v_b_i, v_lam, v_conv_out_g, v_lru_out_g, v_w_out, v_final_g):
    _, t_len, d = x.shape
    hd_l = d // N_LRU_HEADS
    tw = min(MXU_TILE, d)
    cc = conv_w.shape[1]
    x2, tgt2 = x[0], loss_target[0]

    def conv_rows(cw3, lw4):
        return jnp.concatenate([jnp.zeros((1, cc), F32), cw3, lw4], axis=0)

    p, xnt, win_all, wout_all, conv_all = _gather_project(
        x2, w_in, w_out, conv_rows(conv_w, lru_conv_w), ln_g.reshape(1, d), min(256, t_len))
    wout_full = wout_all.reshape(N_DEV * w_out.shape[0], d)
    conv_full = conv_all.transpose(1, 0, 2).reshape(SUBLANES, d)
    small = [ln_g, lru_conv_b, b_a, b_i, lam, conv_out_g, lru_out_g, final_g]
    sp = jnp.concatenate([jnp.stack(small), conv_full[1:], jnp.zeros((1, d), F32)], axis=0)
    wa_t, wi_t = _gate_tiles(w_a, tw).astype(MM), _gate_tiles(w_i, tw).astype(MM)
    ones_c, ones_l = _head_ones(d // N_CONV_HEADS, tw), _head_ones(hd_l, tw)

    h, dh, dhb, facc, *saved = _forward(x2, tgt2, p, wout_full, wa_t, wi_t, sp, ones_c, ones_l, min(256, t_len))
    dp, yt, slab_v, slab_g = _backward(p, h, dh, saved, facc, wout_full, wa_t, wi_t, sp, ones_c, ones_l, min(256, t_len))
    part_out, own_out, _, _ = _weight_grad_out(yt, dhb)
    part_in, own_in, (chips_out,), (gat_v, gat_g) = _weight_grad_in(xnt, dp, (part_out,), (slab_v, slab_g))
    grad_x, ln_tot, sums_in = _input_grad(dp, win_all, x2, dh, sp, part_in, min(512, t_len))
    gw_in, dw_in, mw_in, vw_in = _update_shard(own_in, sums_in, w_in, m_w_in, v_w_in, "update_w_in")
    gw_out, dw_out, mw_out, vw_out = _update_shard(own_out, (chips_out,), w_out, m_w_out, v_w_out, "update_w_out")

    loss_tile, kinds = _small_update(
        gat_v, gat_g, ln_tot, small,
        [m_ln_g, m_lru_conv_b, m_b_a, m_b_i, m_lam, m_conv_out_g, m_lru_out_g, m_final_g],
        [v_ln_g, v_lru_conv_b, v_b_a, v_b_i, v_lam, v_conv_out_g, v_lru_out_g, v_final_g],
        (w_a, m_w_a, v_w_a, w_i, m_w_i, v_w_i), (conv_w, m_conv_w, v_conv_w, lru_conv_w, m_lru_conv_w, v_lru_conv_w))

    def unpack(kind, big_in, big_out):
        vec, (wa_, wi_, cw_, lw_) = kind[:len(small)], kind[len(small):]
        return [vec[0], big_in, cw_, lw_, vec[1], wa_, vec[2], wi_, vec[3], vec[4], vec[5], vec[6], big_out, vec[7]]

    return (loss_tile[0, 0], grad_x[None], *unpack(kinds[0], gw_in, gw_out), *unpack(kinds[1], dw_in, dw_out),
            *unpack(kinds[2], mw_in, mw_out), *unpack(kinds[3], vw_in, vw_out))
```

```python
import functools

import jax
import jax.numpy as jnp
from jax import lax
from jax.experimental import pallas as pl
from jax.experimental.pallas import tpu as pltpu

F32 = jnp.float32
MM = jnp.bfloat16
MESH = pl.DeviceIdType.MESH

N_DEV = 8
N_CONV_HEADS = 8
N_LRU_HEADS = 16
RG_LRU_C = 8.0
RMS_EPS = 1e-6
ADAM_LR, ADAM_B1, ADAM_B2, ADAM_EPS, ADAM_WD, ADAM_STEP = 0.001, 0.9, 0.999, 1e-08, 0.01, 10
ADAM_BC1 = 1.0 - ADAM_B1 ** ADAM_STEP
ADAM_BC2 = 1.0 - ADAM_B2 ** ADAM_STEP

SUBLANES = 8
LANES = 128
MXU_TILE = 256
VMEM_LIMIT = 56 * 1024 * 1024

SP_LN_G, SP_LRU_B, SP_B_A, SP_B_I, SP_LAM, SP_CONV_G, SP_LRU_G, SP_FINAL_G, SP_CONV_W, SP_LRU_W = 0, 1, 2, 3, 4, 5, 6, 7, 8, 11
SP_ROWS = 16
P_B, P_C, P_XC, P_GC, P_XL, P_GL = 0, 1, 2, 3, 4, 5
A_CONV_G, A_LRU_G, A_LAM, A_B_A, A_B_I, A_CONV_W, A_LRU_W, A_LRU_B = 0, 1, 2, 3, 4, 5, 8, 12
A_GROUPS = 13
SL_LOSS, SL_LN_G, SL_LRU_B, SL_B_A, SL_B_I, SL_LAM, SL_CONV_G, SL_LRU_G, SL_FINAL_G, SL_CONV_W, SL_LRU_W = 0, 1, 2, 3, 4, 5, 6, 7, 8, 16, 24
SL_ROWS = 32


def _params(vmem=True, **kw):
    if vmem:
        kw["vmem_limit_bytes"] = VMEM_LIMIT
    return pltpu.CompilerParams(**kw)


def _dot(a, b):
    return jnp.dot(a, b, preferred_element_type=F32)


def _dot_nt(a, b):
    return lax.dot_general(a, b, (((1,), (1,)), ((), ())), preferred_element_type=F32)


def _head_sums(v, ones_tile):
    tw = ones_tile.shape[0]
    vb = v.astype(MM)
    return jnp.concatenate([_dot(vb[:, k:k + tw], ones_tile) for k in range(0, v.shape[1], tw)], axis=1)


def _head_rstd(v, ones_tile, head_dim):
    return lax.rsqrt(_head_sums(v * v, ones_tile) * (1.0 / head_dim) + RMS_EPS)


def _sigmoid(x):
    return 0.5 * jnp.tanh(0.5 * x) + 0.5


def _lru_input_scale(log_a, a):
    return jnp.sqrt(-jnp.tanh(log_a) * (1.0 + a * a))


def _log_sigmoid(x):
    z = jnp.exp(-jnp.abs(x))
    u = 1.0 + z
    log1p_z = jnp.where(u == 1.0, z, jnp.log(u) * (z / (u - 1.0)))
    return jnp.minimum(x, 0.0) - log1p_z


def _row_iota(d):
    return lax.broadcasted_iota(jnp.int32, (SUBLANES, d), 0)


def _shift_down(cur, prev, s, row):
    return jnp.where(row >= s, pltpu.roll(cur, s, axis=0), pltpu.roll(prev, s, axis=0))


def _shift_up(cur, nxt, s, row):
    k = SUBLANES - s
    return jnp.where(row < k, pltpu.roll(cur, k, axis=0), pltpu.roll(nxt, k, axis=0))


def _scan_fwd(a, b, h_prev, row):
    for s in (1, 2, 4):
        a_s = jnp.where(row >= s, pltpu.roll(a, s, axis=0), 1.0)
        b_s = jnp.where(row >= s, pltpu.roll(b, s, axis=0), 0.0)
        b = a * b_s + b
        a = a * a_s
    return a * h_prev + b


def _scan_bwd(a_next, b, g_next, row):
    a = a_next
    for s in (1, 2, 4):
        k = SUBLANES - s
        a_s = jnp.where(row < k, pltpu.roll(a, k, axis=0), 1.0)
        b_s = jnp.where(row < k, pltpu.roll(b, k, axis=0), 0.0)
        b = a * b_s + b
        a = a * a_s
    return a * g_next + b


def _bcast_row(v, r):
    return jnp.broadcast_to(v[r:r + 1, :], v.shape)


def _chunks(n_rows, rc, body, init, reverse=False):
    n = n_rows // rc

    def step(i, carry):
        j = (n - 1 - i) if reverse else i
        return body(pl.multiple_of(j * rc, rc), carry)

    return lax.fori_loop(0, n, step, init)


def _adamw(w, g, m, v):
    m = ADAM_B1 * m + (1.0 - ADAM_B1) * g
    v = ADAM_B2 * v + (1.0 - ADAM_B2) * (g * g)
    m_hat = m / ADAM_BC1
    v_hat = v / ADAM_BC2
    delta = -ADAM_LR * (m_hat / (jnp.sqrt(v_hat) + ADAM_EPS) + ADAM_WD * w)
    return delta, m, v


def _mesh_pos():
    return lax.axis_index("x"), lax.axis_index("y"), lax.axis_index("c")


class _Gather:
    def __init__(self, blocks_of, send_sems, recv_sems, own_src=None):
        x, y, c = _mesh_pos()
        self.c = c
        self.me, self.sibling = (x, y, c), (x, y, 1 - c)
        self.chips = [(1 - x, y), (x, 1 - y), (1 - x, 1 - y)]
        self.blocks_of, self.send_sems, self.recv_sems = blocks_of, send_sems, recv_sems
        self.own_src = own_src

    def copy(self, a, k, block, to):
        src = self.blocks_of(a, *block)
        if block is self.me and self.own_src is not None:
            src = self.own_src[a]
        return pltpu.make_async_remote_copy(
            src_ref=src, dst_ref=self.blocks_of(a, *block),
            send_sem=self.send_sems.at[a * 7 + k], recv_sem=self.recv_sems.at[a * 7 + k],
            device_id=to, device_id_type=MESH)

    def start_own(self, a):
        self.copy(a, 0, self.me, self.sibling).start()
        for j, chip in enumerate(self.chips):
            self.copy(a, 1 + j, self.me, (*chip, self.c)).start()

    def wait_sibling(self, a):
        self.copy(a, 0, self.sibling, self.me).wait_recv()

    def wait_chip_and_pass_on(self, a, j):
        block = (*self.chips[j], self.c)
        self.copy(a, 1 + j, block, self.me).wait_recv()
        self.copy(a, 4 + j, block, self.sibling).start()

    def wait_passed_on(self, a, j):
        self.copy(a, 4 + j, (*self.chips[j], 1 - self.c), self.me).wait_recv()

    def wait_sends(self, a):
        self.copy(a, 0, self.me, self.sibling).wait_send()
        for j, chip in enumerate(self.chips):
            self.copy(a, 1 + j, self.me, (*chip, self.c)).wait_send()
            self.copy(a, 4 + j, (*chip, self.c), self.sibling).wait_send()

    def finish(self, a):
        for j in range(3):
            self.wait_chip_and_pass_on(a, j)
        self.wait_sibling(a)
        for j in range(3):
            self.wait_passed_on(a, j)
        self.wait_sends(a)


class _BalancedGather:
    def __init__(self, slot, send_sems, recv_sems, own_src):
        x, y, c = _mesh_pos()
        self.c = c
        self.me, self.sibling = (x, y, c), (x, y, 1 - c)
        self.chips = [(1 - x, y), (x, 1 - y), (1 - x, 1 - y)]
        self.slot, self.send_sems, self.recv_sems, self.own_src = slot, send_sems, recv_sems, own_src

    def half(self, a, block, which):
        ref = self.slot(a, *block)
        n = ref.shape[0] // 2
        return ref.at[pl.ds(which * n, n)]

    def copy(self, a, k, src, dst, to):
        return pltpu.make_async_remote_copy(
            src_ref=src, dst_ref=dst, send_sem=self.send_sems.at[a * 8 + k], recv_sem=self.recv_sems.at[a * 8 + k],
            device_id=to, device_id_type=MESH)

    def whole(self, a, k, block, to):
        src = self.own_src[a] if block is self.me else self.slot(a, *block)
        return self.copy(a, k, src, self.slot(a, *block), to)

    def halved(self, a, k, block, which, to):
        return self.copy(a, k, self.half(a, block, which), self.half(a, block, which), to)

    def on(self, chip):
        return (*self.chips[chip], self.c)

    def start_own(self, a):
        self.whole(a, 0, self.me, self.sibling).start()
        self.whole(a, 1, self.me, self.on(0)).start()
        self.whole(a, 2, self.me, self.on(1)).start()

    def wait_sibling(self, a):
        self.whole(a, 0, self.sibling, self.me).wait_recv()

    def on_neighbour(self, a, j):
        self.whole(a, 1 + j, self.on(j), self.me).wait_recv()
        self.halved(a, 3 + j, self.on(j), j, self.on(1 - j)).start()
        self.whole(a, 5 + j, self.on(j), self.sibling).start()

    def on_diagonal(self, a):
        self.halved(a, 3, self.on(2), 0, self.me).wait_recv()
        self.halved(a, 4, self.on(2), 1, self.me).wait_recv()
        self.whole(a, 7, self.on(2), self.sibling).start()

    def wait_passed_on(self, a, j):
        self.whole(a, 5 + j, (*self.chips[j], 1 - self.c), self.me).wait_recv()

    def wait_sends(self, a):
        self.whole(a, 0, self.me, self.sibling).wait_send()
        for j in range(2):
            self.whole(a, 1 + j, self.me, self.on(j)).wait_send()
            self.halved(a, 3 + j, self.on(j), j, self.on(1 - j)).wait_send()
        for j in range(3):
            self.whole(a, 5 + j, self.on(j), self.sibling).wait_send()


def _block_order():
    x, y, c = _mesh_pos()
    chips = [(x, y), (1 - x, y), (x, 1 - y), (1 - x, 1 - y)]
    return jnp.stack([4 * px + 2 * py + pc for px, py in chips for pc in (c, 1 - c)]).astype(jnp.int32)


def _gather_project(x, w_in, w_out, conv_w, lru_conv_w, ln_g, w_a, w_i, vecs, tb, tw):
    t_len, d = x.shape
    nb = t_len // tb
    cols = w_in.shape[1]
    mc = min(512, t_len)
    conv_pack = jax.ShapeDtypeStruct((SUBLANES, conv_w.shape[1]), F32)
    srcs = (w_in, w_out, conv_pack)
    dts = (MM, MM, F32)
    n_vec = len(vecs)
    n_heads, hd, _ = w_a.shape
    per = tw // hd

    def body(order_ref, x_ref, win_ref, wout_ref, cw_ref, lw_ref, lng_ref, wa_ref, wi_ref, *refs):
        vec_refs = refs[:n_vec]
        (p_ref, xnt_ref, win_all, wout_all, cp_all, sp_ref, wat_ref, wit_ref,
         xnb, wall, st_out, st_cp, cp_vm, send_sems, recv_sems, cp_send, cp_recv, local_sems) = refs[n_vec:]
        i = pl.program_id(0)
        x_, y_, c_ = _mesh_pos()
        me = 4 * x_ + 2 * y_ + c_
        outs = (win_all, wout_all, cp_all)
        lands = (wall, wout_all, cp_all)
        stages = (wall.at[me], st_out, st_cp)
        gather = _BalancedGather(lambda a, px, py, pc: lands[a].at[4 * px + 2 * py + pc], send_sems, recv_sems, stages)
        small = _Gather(lambda a, px, py, pc: cp_all.at[4 * px + 2 * py + pc], cp_send, cp_recv, own_src=[st_cp])
        keep_own = [pltpu.make_async_copy(stages[a], outs[a].at[me], local_sems.at[a]) for a in range(3)]

        def keep(k):
            blk = order_ref[k]
            return pltpu.make_async_copy(wall.at[blk], win_all.at[blk], local_sems.at[2 + k])

        @pl.when(i == 0)
        def _():
            for a, src in enumerate((win_ref, wout_ref)):
                dst, rc = stages[a], 32

                def cast(r, carry, src=src, dst=dst):
                    dst[pl.ds(r, rc), :] = src[pl.ds(r, rc), :].astype(dst.dtype)
                    return carry

                _chunks(src.shape[0], rc, cast, 0)
                gather.start_own(a)
                keep_own[a].start()
            n_cw, n_lw = cw_ref.shape[0], lw_ref.shape[0]
            st_cp[...] = jnp.zeros_like(st_cp)
            st_cp[0:n_cw, :] = cw_ref[...]
            st_cp[n_cw:n_cw + n_lw, :] = lw_ref[...]
            small.start_own(0)
            keep_own[2].start()

        @pl.when(i < nb)
        def _():
            xv = x_ref[...]
            r0 = lax.rsqrt(jnp.mean(xv * xv, axis=-1, keepdims=True) + RMS_EPS)
            xn = xv * r0 * lng_ref[...]
            xnb[pl.ds(pl.multiple_of(i * tb, tb), tb), :] = xn.astype(MM)
            xnt_ref[...] = xn.T.astype(MM)

        for k in range(N_DEV):
            @pl.when(i == nb + k)
            def _(k=k):
                if k == 1:
                    gather.wait_sibling(0)
                elif k == 2:
                    gather.on_neighbour(0, 0)
                    gather.on_neighbour(0, 1)
                elif k in (3, 5, 7):
                    gather.wait_passed_on(0, (k - 3) // 2)
                    if k == 3:
                        gather.on_neighbour(1, 0)
                        gather.on_neighbour(1, 1)
                    if k == 7:
                        gather.on_diagonal(1)
                elif k == 6:
                    gather.on_diagonal(0)
                blk = order_ref[k]
                if k:
                    keep(k).start()

                def project(r, carry):
                    rows = pl.ds(r, mc)
                    p_ref[rows, :] = _dot(xnb[rows, :], wall[blk]).astype(MM)
                    return carry

                _chunks(t_len, mc, project, 0)
                if k == N_DEV - 1:
                    gather.wait_sends(0)
                    gather.wait_sibling(1)
                    for j in range(3):
                        gather.wait_passed_on(1, j)
                    gather.wait_sends(1)
                    small.finish(0)
                    for cp in keep_own + [keep(kk) for kk in range(1, N_DEV)]:
                        cp.wait()
                    load = pltpu.make_async_copy(cp_all, cp_vm, local_sems.at[N_DEV + 2])
                    load.start()
                    load.wait()
                    for r, ref in enumerate(vec_refs):
                        sp_ref[r, :] = ref[...]
                    sp_ref[n_vec:n_vec + SUBLANES, :] = jnp.concatenate([cp_vm[dev] for dev in range(N_DEV)], axis=1)
                    for src, dst in ((wa_ref, wat_ref), (wi_ref, wit_ref)):
                        dst[...] = jnp.zeros_like(dst)
                        for head in range(n_heads):
                            lo = (head % per) * hd
                            dst[head // per, lo:lo + hd, lo:lo + hd] = src[head].astype(MM)

    vm = pl.BlockSpec(memory_space=pltpu.VMEM)
    hbm = pl.BlockSpec(memory_space=pl.ANY)
    grid_spec = pltpu.PrefetchScalarGridSpec(
        num_scalar_prefetch=1, grid=(nb + N_DEV,),
        in_specs=[pl.BlockSpec((tb, d), lambda i, o: (jnp.minimum(i, nb - 1), 0))] + [vm] * (7 + n_vec),
        out_specs=(pl.BlockSpec((t_len, cols), lambda i, o: (0, o[jnp.maximum(i - nb, 0)])),
                   pl.BlockSpec((d, tb), lambda i, o: (0, jnp.minimum(i, nb - 1))), hbm, hbm, hbm,
                   pl.BlockSpec((SP_ROWS, d), lambda i, o: (0, 0)),
                   pl.BlockSpec((n_heads // per, tw, tw), lambda i, o: (0, 0, 0)),
                   pl.BlockSpec((n_heads // per, tw, tw), lambda i, o: (0, 0, 0))),
        scratch_shapes=[pltpu.VMEM((t_len, d), MM), pltpu.VMEM((N_DEV,) + w_in.shape, MM),
                        pltpu.VMEM(w_out.shape, MM), pltpu.VMEM(conv_pack.shape, F32),
                        pltpu.VMEM((N_DEV,) + conv_pack.shape, F32),
                        pltpu.SemaphoreType.DMA((16,)), pltpu.SemaphoreType.DMA((16,)),
                        pltpu.SemaphoreType.DMA((7,)), pltpu.SemaphoreType.DMA((7,)), pltpu.SemaphoreType.DMA((N_DEV + 3,))])
    return pl.pallas_call(
        body, name="gather_project", grid_spec=grid_spec,
        out_shape=(jax.ShapeDtypeStruct((t_len, N_DEV * cols), MM),
                   jax.ShapeDtypeStruct((d, t_len), MM))
                  + tuple(jax.ShapeDtypeStruct((N_DEV,) + s.shape, dt) for s, dt in zip(srcs, dts))
                  + (jax.ShapeDtypeStruct((SP_ROWS, d), F32),)
                  + (jax.ShapeDtypeStruct((n_heads // per, tw, tw), MM),) * 2,
        compiler_params=_params(dimension_semantics=("arbitrary",)),
    )(_block_order(), x, w_in, w_out, conv_w, lru_conv_w, ln_g, w_a, w_i, *vecs)


def _forward(x, tgt, p, wout, wa_t, wi_t, sp, ones_c, ones_l, tb):
    t_len, d = x.shape
    nb = t_len // tb
    n_tiles, tw = wa_t.shape[0], wa_t.shape[1]
    hd_c, hd_l = d // N_CONV_HEADS, d // N_LRU_HEADS
    s8 = SUBLANES

    def body(x_ref, tgt_ref, p_ref, wout_ref, wa_ref, wi_ref, sp_ref, oc_ref, ol_ref,
             h_ref, dh_ref, dhb_ref, acc_ref, yc, czs, u, pa, pi,
             rcf, rlf, ybuf, tail_z, tail_xl, hcar):
        i = pl.program_id(0)
        row = _row_iota(d)

        @pl.when(i == 0)
        def _():
            tail_z[...] = jnp.zeros_like(tail_z)
            tail_xl[...] = jnp.zeros_like(tail_xl)
            hcar[...] = jnp.zeros_like(hcar)
            acc_ref[...] = jnp.zeros_like(acc_ref)

        def spr(r):
            return sp_ref[r:r + 1, :]

        def proj(rows, seg):
            return p_ref[rows, seg * d:(seg + 1) * d].astype(F32)

        w0, w1, w2 = spr(SP_CONV_W), spr(SP_CONV_W + 1), spr(SP_CONV_W + 2)
        l0, l1, l2, l3 = spr(SP_LRU_W), spr(SP_LRU_W + 1), spr(SP_LRU_W + 2), spr(SP_LRU_W + 3)
        lb = spr(SP_LRU_B)

        def convs(r, carry):
            zp, xp = carry
            rows16 = pl.ds(r, 2 * s8)
            bg16, xl16 = proj(rows16, P_B), proj(rows16, P_XL)
            z16 = proj(rows16, P_C) * proj(rows16, P_XC)
            for j in range(2):
                rows, sub = pl.ds(r + j * s8, s8), slice(j * s8, (j + 1) * s8)
                z, xl = z16[sub], xl16[sub]
                cz = w0 * _shift_down(z, zp, 2, row) + w1 * _shift_down(z, zp, 1, row) + w2 * z
                czs[rows, :] = cz
                yc[rows, :] = bg16[sub] * cz
                u[rows, :] = (l0 * _shift_down(xl, xp, 3, row) + l1 * _shift_down(xl, xp, 2, row)
                              + l2 * _shift_down(xl, xp, 1, row) + l3 * xl + lb)
                zp, xp = z, xl
            return zp, xp

        z_last, xl_last = _chunks(tb, 2 * s8, convs, (tail_z[...], tail_xl[...]))
        tail_z[...] = z_last
        tail_xl[...] = xl_last

        ub = u[...].astype(MM)
        for k in range(n_tiles):
            sl = slice(k * tw, (k + 1) * tw)
            pa[:, sl] = _dot(ub[:, sl], wa_ref[k])
            pi[:, sl] = _dot(ub[:, sl], wi_ref[k])
        rcf[...] = _head_rstd(yc[...], oc_ref[...], hd_c)

        c8 = RG_LRU_C * _log_sigmoid(spr(SP_LAM))
        b_a, b_i = spr(SP_B_A), spr(SP_B_I)

        def lru(r, hp):
            rows = pl.ds(r, SUBLANES)
            ra = _sigmoid(pa[rows, :] + b_a)
            ii = _sigmoid(pi[rows, :] + b_i)
            pa[rows, :] = ra
            pi[rows, :] = ii
            la = ra * c8
            a = jnp.exp(la)
            mult = _lru_input_scale(la, a)
            h = _scan_fwd(a, mult * (ii * u[rows, :]), hp, row)
            h_ref[rows, :] = h
            return _bcast_row(h, SUBLANES - 1)

        hcar[...] = _chunks(tb, SUBLANES, lru, hcar[...])
        rlf[...] = _head_rstd(h_ref[...], ol_ref[...], hd_l)

        g_c, g_l = spr(SP_CONV_G), spr(SP_LRU_G)

        def gate(r, carry):
            rows = pl.ds(r, 2 * s8)
            gc, gl = proj(rows, P_GC), proj(rows, P_GL)
            ybuf[rows, 0:d] = (yc[rows, :] * rcf[rows, :] * g_c * (gc * _sigmoid(gc))).astype(MM)
            ybuf[rows, d:2 * d] = (h_ref[rows, :] * rlf[rows, :] * g_l * (gl * _sigmoid(gl))).astype(MM)
            return carry

        _chunks(tb, 2 * s8, gate, 0)

        hres = x_ref[...] + _dot(ybuf[...], wout_ref[...])
        rf = lax.rsqrt(jnp.mean(hres * hres, axis=-1, keepdims=True) + RMS_EPS)
        hn = hres * rf
        fg = spr(SP_FINAL_G)
        err = hn * fg - tgt_ref[...]
        dout = err * (1.0 / d)
        acc_ref[0:SUBLANES, :] += (err * err).reshape(tb // SUBLANES, SUBLANES, d).sum(axis=0)
        acc_ref[SUBLANES:2 * SUBLANES, :] += (dout * hn).reshape(tb // SUBLANES, SUBLANES, d).sum(axis=0)
        gd = dout * fg
        dhres = rf * (gd - hn * jnp.mean(gd * hn, axis=-1, keepdims=True))
        dh_ref[...] = dhres
        dhb_ref[...] = dhres.astype(MM)

    vm = pl.BlockSpec(memory_space=pltpu.VMEM)
    blk = lambda w: pl.BlockSpec((tb, w), lambda i: (i, 0))
    buf = pltpu.VMEM((tb, d), F32)
    car = pltpu.VMEM((SUBLANES, d), F32)
    return pl.pallas_call(
        body, name="forward", grid=(nb,),
        in_specs=[blk(d), blk(d), blk(6 * d), vm, vm, vm, vm, vm, vm],
        out_specs=(blk(d), blk(d), blk(d), pl.BlockSpec((2 * SUBLANES, d), lambda i: (0, 0))) + (blk(d),) * 5,
        out_shape=(jax.ShapeDtypeStruct((t_len, d), F32),
                   jax.ShapeDtypeStruct((t_len, d), F32),
                   jax.ShapeDtypeStruct((t_len, d), MM),
                   jax.ShapeDtypeStruct((2 * SUBLANES, d), F32))
                  + (jax.ShapeDtypeStruct((t_len, d), F32),) * 5,
        scratch_shapes=[buf] * 2 + [pltpu.VMEM((tb, 2 * d), MM), car, car, car],
        compiler_params=_params(dimension_semantics=("arbitrary",)),
    )(x, tgt, p, wout, wa_t, wi_t, sp, ones_c, ones_l)


def _backward(p, h, dh, saved, facc, wout, wa_t, wi_t, sp, ones_c, ones_l, tb):
    t_len, d = h.shape
    nb = t_len // tb
    n_tiles, tw = wa_t.shape[0], wa_t.shape[1]
    hd_c, hd_l = d // N_CONV_HEADS, d // N_LRU_HEADS
    g_rows = 2 * n_tiles * hd_l
    s8 = SUBLANES

    def body(p_ref, h_ref, hhalo_ref, dh_ref, yc, czs, u, ra_ref, ii_ref, facc_ref,
             wout_ref, wa_ref, wi_ref, sp_ref, oc_ref, ol_ref,
             dp_ref, yt_ref, slab_v, slab_g,
             hh, dy, ybuf, rcf, rlf, qc, ql, dyc_hat, dyl_hat, dpa, dpi, du, gwa_ref, gwi_ref, acc_ref,
             car_dcz, car_a, car_g, car_du):
        i = pl.program_id(0)
        blk_idx = nb - 1 - i
        row = _row_iota(d)

        @pl.when(i == 0)
        def _():
            for ref in (car_dcz, car_a, car_g, car_du, gwa_ref, gwi_ref, acc_ref):
                ref[...] = jnp.zeros_like(ref)

        def spr(r):
            return sp_ref[r:r + 1, :]

        def proj(rows, seg):
            return p_ref[rows, seg * d:(seg + 1) * d].astype(F32)

        def put(rows, seg, halves):
            dp_ref[rows, seg * d:(seg + 1) * d] = jnp.concatenate(halves, axis=0).astype(MM)

        def acc_add(group, val):
            acc_ref[group * s8:(group + 1) * s8, :] += val

        live = jnp.where(blk_idx > 0, 1.0, 0.0).astype(F32)
        hh[0:s8, :] = hhalo_ref[...] * live
        hh[s8:, :] = h_ref[...]

        dy[...] = _dot_nt(dh_ref[...].astype(MM), wout_ref[...])

        w0, w1, w2 = spr(SP_CONV_W), spr(SP_CONV_W + 1), spr(SP_CONV_W + 2)
        l0, l1, l2, l3 = spr(SP_LRU_W), spr(SP_LRU_W + 1), spr(SP_LRU_W + 2), spr(SP_LRU_W + 3)

        rcf[...] = _head_rstd(yc[...], oc_ref[...], hd_c)
        rlf[...] = _head_rstd(h_ref[...], ol_ref[...], hd_l)

        g_c, g_l = spr(SP_CONV_G), spr(SP_LRU_G)

        def gates(r, carry):
            rows = pl.ds(r, 2 * s8)
            for (seg, off_y, src, rstd, gain, q, dhat, grp) in (
                    (P_GC, 0, yc, rcf, g_c, qc, dyc_hat, A_CONV_G),
                    (P_GL, d, h_ref, rlf, g_l, ql, dyl_hat, A_LRU_G)):
                gt = proj(rows, seg)
                sg = _sigmoid(gt)
                silu = gt * sg
                yhat = src[rows, :] * rstd[rows, :]
                nrm = yhat * gain
                ybuf[rows, off_y:off_y + d] = nrm * silu
                dout = dy[rows, off_y:off_y + d]
                dnrm = dout * silu
                dp_ref[rows, seg * d:(seg + 1) * d] = (dout * nrm * (sg * (1.0 + gt * (1.0 - sg)))).astype(MM)
                dg = dnrm * yhat
                acc_add(grp, dg[0:s8] + dg[s8:])
                dh_ = dnrm * gain
                dhat[rows, :] = dh_
                q[rows, :] = dh_ * yhat
            return carry

        _chunks(tb, 2 * s8, gates, 0)

        qc[...] = _head_sums(qc[...], oc_ref[...]) * (1.0 / hd_c)
        ql[...] = _head_sums(ql[...], ol_ref[...]) * (1.0 / hd_l)
        yt_ref[...] = ybuf[...].T.astype(MM)

        c8 = RG_LRU_C * _log_sigmoid(spr(SP_LAM))

        def conv_mixer(r, dcz_n):
            rows16 = pl.ds(r, 2 * s8)
            bg16, cg16, xc16 = proj(rows16, P_B), proj(rows16, P_C), proj(rows16, P_XC)
            z16 = cg16 * xc16
            d_b, d_c, d_x = [None, None], [None, None], [None, None]
            for j in (1, 0):
                rows, sub = pl.ds(r + j * s8, s8), slice(j * s8, (j + 1) * s8)
                rstd = rcf[rows, :]
                yhat = yc[rows, :] * rstd
                dyc = rstd * (dyc_hat[rows, :] - yhat * qc[rows, :])
                d_b[j] = dyc * czs[rows, :]
                dcz = dyc * bg16[sub]
                up1, up2 = _shift_up(dcz, dcz_n, 1, row), _shift_up(dcz, dcz_n, 2, row)
                dz = w2 * dcz + w1 * up1 + w0 * up2
                d_c[j] = dz * xc16[sub]
                d_x[j] = dz * cg16[sub]
                z = z16[sub]
                acc_add(A_CONV_W, up2 * z)
                acc_add(A_CONV_W + 1, up1 * z)
                acc_add(A_CONV_W + 2, dcz * z)
                dcz_n = dcz
            put(rows16, P_B, d_b)
            put(rows16, P_C, d_c)
            put(rows16, P_XC, d_x)
            return dcz_n

        car_dcz[...] = _chunks(tb, 2 * s8, conv_mixer, car_dcz[...], reverse=True)

        def lru_mixer(r, carry):
            a_n, g_n = carry
            for j in (1, 0):
                rows = pl.ds(r + j * s8, s8)
                rstd = rlf[rows, :]
                hcur = hh[pl.ds(r + (j + 1) * s8, s8), :]
                hhat = hcur * rstd
                dh_out = rstd * (dyl_hat[rows, :] - hhat * ql[rows, :])
                ra = ra_ref[rows, :]
                la = ra * c8
                a = jnp.exp(la)
                g = _scan_bwd(_shift_up(a, a_n, 1, row), dh_out, g_n, row)
                da = g * _shift_down(hcur, hh[pl.ds(r + j * s8, s8), :], 1, row)
                ii = ii_ref[rows, :]
                uu = u[rows, :]
                mult = _lru_input_scale(la, a)
                dmult = g * (ii * uu)
                ds = g * mult
                dla = a * (da - dmult * a / mult)
                acc_add(A_LAM, dla * ra)
                dpa_ = dla * c8 * ra * (1.0 - ra)
                dpi_ = ds * uu * ii * (1.0 - ii)
                acc_add(A_B_A, dpa_)
                acc_add(A_B_I, dpi_)
                dpa[rows, :] = dpa_
                dpi[rows, :] = dpi_
                du[rows, :] = ds * ii
                a_n, g_n = a, _bcast_row(g, 0)
            return a_n, g_n

        a_f, g_f = _chunks(tb, 2 * s8, lru_mixer, (car_a[...], car_g[...]), reverse=True)
        car_a[...] = a_f
        car_g[...] = g_f

        dpab = dpa[...].astype(MM)
        dpib = dpi[...].astype(MM)
        for k in range(n_tiles):
            sl = slice(k * tw, (k + 1) * tw)
            du[:, sl] += _dot_nt(dpab[:, sl], wa_ref[k]) + _dot_nt(dpib[:, sl], wi_ref[k])
            ut = u[:, sl].T.astype(MM)
            gwa_ref[k] += _dot(ut, dpab[:, sl])
            gwi_ref[k] += _dot(ut, dpib[:, sl])

        def lru_conv(r, du_n):
            rows16 = pl.ds(r, 2 * s8)
            xl16 = proj(rows16, P_XL)
            d_xl = [None, None]
            for j in (1, 0):
                rows, sub = pl.ds(r + j * s8, s8), slice(j * s8, (j + 1) * s8)
                dut = du[rows, :]
                up1, up2, up3 = (_shift_up(dut, du_n, s, row) for s in (1, 2, 3))
                d_xl[j] = l3 * dut + l2 * up1 + l1 * up2 + l0 * up3
                xl = xl16[sub]
                acc_add(A_LRU_W, up3 * xl)
                acc_add(A_LRU_W + 1, up2 * xl)
                acc_add(A_LRU_W + 2, up1 * xl)
                acc_add(A_LRU_W + 3, dut * xl)
                acc_add(A_LRU_B, dut)
                du_n = dut
            put(rows16, P_XL, d_xl)
            return du_n

        car_du[...] = _chunks(tb, 2 * s8, lru_conv, car_du[...], reverse=True)

        @pl.when(i == nb - 1)
        def _():
            def rowsum(ref, group):
                return jnp.sum(ref[group * s8:(group + 1) * s8, :], axis=0, keepdims=True)

            slab_v[...] = jnp.zeros_like(slab_v)
            loss = jnp.sum(rowsum(facc_ref, 0), axis=1, keepdims=True) * (0.5 / d)
            rows = {SL_LOSS: jnp.broadcast_to(loss, (1, d)), SL_FINAL_G: rowsum(facc_ref, 1),
                    SL_LRU_B: rowsum(acc_ref, A_LRU_B), SL_B_A: rowsum(acc_ref, A_B_A), SL_B_I: rowsum(acc_ref, A_B_I),
                    SL_LAM: rowsum(acc_ref, A_LAM), SL_CONV_G: rowsum(acc_ref, A_CONV_G), SL_LRU_G: rowsum(acc_ref, A_LRU_G)}
            for k in range(3):
                rows[SL_CONV_W + k] = rowsum(acc_ref, A_CONV_W + k)
            for k in range(4):
                rows[SL_LRU_W + k] = rowsum(acc_ref, A_LRU_W + k)
            for r, val in rows.items():
                slab_v[r:r + 1, :] = val
            head_of_lane = lax.broadcasted_iota(jnp.int32, (hd_l, tw), 1) // hd_l
            for mtx, g_ref in enumerate((gwa_ref, gwi_ref)):
                for k in range(n_tiles):
                    packed = jnp.zeros((hd_l, tw), F32)
                    for a in range(tw // hd_l):
                        packed = jnp.where(head_of_lane == a, g_ref[k, a * hd_l:(a + 1) * hd_l, :], packed)
                    slab_g[(mtx * n_tiles + k) * hd_l:(mtx * n_tiles + k + 1) * hd_l, :] = packed.astype(MM)

    vm = pl.BlockSpec(memory_space=pltpu.VMEM)
    rev = lambda w: pl.BlockSpec((tb, w), lambda i: (nb - 1 - i, 0))
    halo = lambda rows, w: pl.BlockSpec((rows, w), lambda i: (jnp.maximum((nb - 1 - i) * (tb // rows) - 1, 0), 0))
    const = lambda shape: pl.BlockSpec(shape, lambda i: (0,) * len(shape))
    buf = lambda w: pltpu.VMEM((tb, w), F32)
    car = pltpu.VMEM((SUBLANES, d), F32)
    return pl.pallas_call(
        body, name="backward", grid=(nb,),
        in_specs=[rev(6 * d), rev(d), halo(SUBLANES, d), rev(d)] + [rev(d)] * 5 + [vm, vm, vm, vm, vm, vm, vm],
        out_specs=(rev(6 * d), pl.BlockSpec((2 * d, tb), lambda i: (0, nb - 1 - i)),
                   const((SL_ROWS, d)), const((g_rows, tw))),
        out_shape=(jax.ShapeDtypeStruct((t_len, 6 * d), MM),
                   jax.ShapeDtypeStruct((2 * d, t_len), MM),
                   jax.ShapeDtypeStruct((SL_ROWS, d), F32),
                   jax.ShapeDtypeStruct((g_rows, tw), MM)),
        scratch_shapes=[pltpu.VMEM((SUBLANES + tb, d), F32), buf(2 * d), buf(2 * d)] + [buf(d)] * 9
                       + [pltpu.VMEM((n_tiles, tw, tw), F32), pltpu.VMEM((n_tiles, tw, tw), F32),
                          pltpu.VMEM((A_GROUPS * SUBLANES, d), F32), car, car, car, car],
        compiler_params=_params(dimension_semantics=("arbitrary",)),
    )(p, h, h, dh, *saved, facc, wout, wa_t, wi_t, sp, ones_c, ones_l)


def _input_grad(dp, win_all, x, dh, sp, part, tb):
    t_len, d = x.shape
    nb = t_len // tb
    cols = win_all.shape[2]
    mid = min(nb - 1, (5 * nb) // 8)
    rc = 32

    def body(dp_ref, win_ref, x_ref, dh_ref, sp_ref, part_ref, gx_ref, ln_ref, direct, passing, relayed,
             send_sems, recv_sems, local_sems, acc_ref, ln_all, ln_send, ln_recv, mine, theirs):
        i = pl.program_id(0)
        x_, y_, c_ = _mesh_pos()
        first, second = 1 - c_, c_
        nbr1 = (x_ ^ c_, y_ ^ (1 - c_), c_)
        nbr2 = (x_ ^ (1 - c_), y_ ^ c_, c_)

        def remote(src, dst, k, to):
            return pltpu.make_async_remote_copy(src_ref=src, dst_ref=dst, send_sem=send_sems.at[k], recv_sem=recv_sems.at[k],
                                                device_id=to, device_id_type=MESH)

        to_first = [remote(part_ref.at[first], direct, 0, nbr1), remote(part_ref.at[2], passing, 1, nbr1)]
        to_second = remote(theirs, relayed, 2, nbr2)

        @pl.when(i == 0)
        def _():
            acc_ref[...] = jnp.zeros_like(acc_ref)
            for cp in to_first:
                cp.start()

        dxn = _dot_nt(dp_ref[:, 0:cols], win_ref[0])
        for j in range(1, N_DEV):
            dxn += _dot_nt(dp_ref[:, j * cols:(j + 1) * cols], win_ref[j])
        xv = x_ref[...]
        r0 = lax.rsqrt(jnp.mean(xv * xv, axis=-1, keepdims=True) + RMS_EPS)
        xhat = xv * r0
        acc_ref[...] += (dxn * xhat).reshape(tb // SUBLANES, SUBLANES, d).sum(axis=0)
        dxh = dxn * sp_ref[SP_LN_G:SP_LN_G + 1, :]
        gx_ref[...] = dh_ref[...] + r0 * (dxh - xhat * jnp.mean(dxh * xhat, axis=-1, keepdims=True))

        @pl.when(i == mid)
        def _():
            to_first[1].wait_recv()
            loads = [pltpu.make_async_copy(part_ref.at[second], mine, local_sems.at[0]),
                     pltpu.make_async_copy(passing, theirs, local_sems.at[1])]
            for cp in loads:
                cp.start()
            for cp in loads:
                cp.wait()

            def add(r, carry):
                rows = pl.ds(r, rc)
                theirs[rows, :] = (mine[rows, :].astype(F32) + theirs[rows, :].astype(F32)).astype(MM)
                return carry

            _chunks(mine.shape[0], rc, add, 0)
            to_second.start()

        @pl.when(i == nb - 1)
        def _():
            to_first[0].wait_recv()
            to_second.wait_recv()
            for cp in to_first + [to_second]:
                cp.wait_send()
            ln_all[4 * x_ + 2 * y_ + c_] = jnp.broadcast_to(jnp.sum(acc_ref[...], axis=0, keepdims=True), acc_ref.shape)
            gather = _Gather(lambda a, px, py, pc: ln_all.at[4 * px + 2 * py + pc], ln_send, ln_recv)
            gather.start_own(0)
            gather.finish(0)
            total = ln_all[0]
            for dev in range(1, N_DEV):
                total = total + ln_all[dev]
            ln_ref[...] = total

    vm = pl.BlockSpec(memory_space=pltpu.VMEM)
    hbm = pl.BlockSpec(memory_space=pl.ANY)
    blk = lambda w: pl.BlockSpec((tb, w), lambda i: (i, 0))
    landed = jax.ShapeDtypeStruct(part.shape[1:], part.dtype)
    outs = pl.pallas_call(
        body, name="input_grad", grid=(nb,),
        in_specs=[blk(6 * d), vm, blk(d), blk(d), vm, hbm],
        out_specs=(blk(d), pl.BlockSpec((SUBLANES, d), lambda i: (0, 0)), hbm, hbm, hbm),
        out_shape=(jax.ShapeDtypeStruct((t_len, d), F32), jax.ShapeDtypeStruct((SUBLANES, d), F32), landed, landed, landed),
        scratch_shapes=[pltpu.SemaphoreType.DMA((3,)), pltpu.SemaphoreType.DMA((3,)), pltpu.SemaphoreType.DMA((2,)),
                        pltpu.VMEM((SUBLANES, d), F32), pltpu.VMEM((N_DEV, SUBLANES, d), F32),
                        pltpu.SemaphoreType.DMA((7,)), pltpu.SemaphoreType.DMA((7,)),
                        pltpu.VMEM(part.shape[1:], MM), pltpu.VMEM(part.shape[1:], MM)],
        compiler_params=_params(dimension_semantics=("arbitrary",)),
    )(dp, win_all, x, dh, sp, part)
    return outs[0], outs[1], (outs[2], outs[4])


_CHIP_RELATIONS = [(0, 0), (1, 0), (0, 1), (1, 1)]


def _related_block(k, core):
    x, y, _ = _mesh_pos()
    fx, fy = _CHIP_RELATIONS[k]
    return 4 * (x ^ fx) + 2 * (y ^ fy) + core


class _ChipExchange:
    def __init__(self, part_refs, land_refs, send_sems, recv_sems):
        self.part_refs, self.land_refs, self.send_sems, self.recv_sems = part_refs, land_refs, send_sems, recv_sems

    def copies(self):
        x, y, c = _mesh_pos()
        for a in range(len(self.part_refs)):
            for k in (1, 2, 3):
                fx, fy = _CHIP_RELATIONS[k]
                yield pltpu.make_async_remote_copy(
                    src_ref=self.part_refs[a].at[k - 1], dst_ref=self.land_refs[a].at[k - 1],
                    send_sem=self.send_sems.at[3 * a + k - 1], recv_sem=self.recv_sems.at[3 * a + k - 1],
                    device_id=(x ^ fx, y ^ fy, c), device_id_type=MESH)

    def start(self):
        for cp in self.copies():
            cp.start()

    def finish(self):
        for cp in self.copies():
            cp.wait_recv()
        for cp in self.copies():
            cp.wait_send()


def _weight_grad_stage1(name, blk_shape, n_split, operands, in_specs, product, riders=(), slabs=()):
    n_rows, n_cols = blk_shape
    rs = n_rows // n_split
    rc = 32
    n_in, n_ride, n_slab = len(operands), len(riders), len(slabs)
    _, _, c = _mesh_pos()
    order = jnp.stack([_related_block(k, 1 - c) for k in range(4)]
                      + [_related_block(k, c) for k in (1, 2, 3, 0)]).astype(jnp.int32)

    def body(order_ref, *refs):
        ins = refs[:n_in]
        ride_in = refs[n_in:n_in + n_ride]
        slab_in = refs[n_in + n_ride:n_in + n_ride + n_slab]
        n_op = n_in + n_ride + n_slab
        part_ref, own_ref = refs[n_op:n_op + 2]
        ride_out = refs[n_op + 2:n_op + 2 + n_ride]
        gathered = refs[n_op + 2 + n_ride:n_op + 2 + n_ride + n_slab]
        (gbuf, sendbuf, from_sib, send_sems, recv_sems, ride_send, ride_recv,
         slab_send, slab_recv, slab_local) = refs[n_op + 2 + n_ride + n_slab:]
        exchange = _ChipExchange(ride_in, ride_out, ride_send, ride_recv)
        s = pl.program_id(0)
        x, y, c = _mesh_pos()
        me = 4 * x + 2 * y + c
        gather = _BalancedGather(lambda a, px, py, pc: gathered[a].at[4 * px + 2 * py + pc], slab_send, slab_recv, slab_in)
        keep_own = [pltpu.make_async_copy(slab_in[a], gathered[a].at[me], slab_local.at[a]) for a in range(n_slab)]

        def to_sibling(k):
            return pltpu.make_async_remote_copy(
                src_ref=sendbuf.at[k], dst_ref=from_sib.at[k], send_sem=send_sems.at[k], recv_sem=recv_sems.at[k],
                device_id=(x, y, 1 - c), device_id_type=MESH)

        @pl.when(s == 0)
        def _():
            exchange.start()
            for a in range(n_slab):
                gather.start_own(a)
                keep_own[a].start()

        @pl.when(s == 5)
        def _():
            for a in range(n_slab):
                gather.on_neighbour(a, 0)
                gather.on_neighbour(a, 1)

        @pl.when(s == 7)
        def _():
            for a in range(n_slab):
                gather.on_diagonal(a)

        for h in range(n_split):
            gbuf[h * rs:(h + 1) * rs, :] = product(ins, h)

        @pl.when(s < 4)
        def _():
            def narrow(r, carry):
                sendbuf[s, pl.ds(r, rc), :] = gbuf[pl.ds(r, rc), :].astype(MM)
                return carry

            _chunks(n_rows, rc, narrow, 0)
            to_sibling(s).start()

        @pl.when(s >= 4)
        def _():
            k = jnp.where(s == 7, 0, s - 3)
            to_sibling(k).wait_recv()

            @pl.when(s < 7)
            def _():
                def add(r, carry):
                    rows = pl.ds(r, rc)
                    part_ref[0, rows, :] = (gbuf[rows, :] + from_sib[k, rows, :].astype(F32)).astype(MM)
                    return carry

                _chunks(n_rows, rc, add, 0)

            @pl.when(s == 7)
            def _():
                def add(r, carry):
                    rows = pl.ds(r, rc)
                    own_ref[rows, :] = gbuf[rows, :] + from_sib[0, rows, :].astype(F32)
                    return carry

                _chunks(n_rows, rc, add, 0)
                for kk in range(4):
                    to_sibling(kk).wait_send()
                exchange.finish()
                for a in range(n_slab):
                    gather.wait_sibling(a)
                    for j in range(3):
                        gather.wait_passed_on(a, j)
                    gather.wait_sends(a)
                    keep_own[a].wait()

    hbm = pl.BlockSpec(memory_space=pl.ANY)
    grid_spec = pltpu.PrefetchScalarGridSpec(
        num_scalar_prefetch=1, grid=(N_DEV,), in_specs=list(in_specs) + [hbm] * (n_ride + n_slab),
        out_specs=(pl.BlockSpec((1, n_rows, n_cols), lambda s, o: (jnp.clip(s - 4, 0, 2), 0, 0)),
                   pl.BlockSpec((n_rows, n_cols), lambda s, o: (0, 0))) + (hbm,) * (n_ride + n_slab),
        scratch_shapes=[pltpu.VMEM((n_rows, n_cols), F32), pltpu.VMEM((4, n_rows, n_cols), MM),
                        pltpu.VMEM((4, n_rows, n_cols), MM),
                        pltpu.SemaphoreType.DMA((4,)), pltpu.SemaphoreType.DMA((4,)),
                        pltpu.SemaphoreType.DMA((max(3 * n_ride, 1),)), pltpu.SemaphoreType.DMA((max(3 * n_ride, 1),)),
                        pltpu.SemaphoreType.DMA((max(8 * n_slab, 1),)), pltpu.SemaphoreType.DMA((max(8 * n_slab, 1),)),
                        pltpu.SemaphoreType.DMA((max(n_slab, 1),))])
    outs = pl.pallas_call(
        body, name=name, grid_spec=grid_spec,
        out_shape=(jax.ShapeDtypeStruct((3, n_rows, n_cols), MM), jax.ShapeDtypeStruct((n_rows, n_cols), F32))
                  + tuple(jax.ShapeDtypeStruct(p.shape, p.dtype) for p in riders)
                  + tuple(jax.ShapeDtypeStruct((N_DEV,) + a.shape, a.dtype) for a in slabs),
        compiler_params=_params(dimension_semantics=("arbitrary",)),
    )(order, *operands, *riders, *slabs)
    return outs[0], outs[1], outs[2:2 + n_ride], outs[2 + n_ride:]


def _weight_grad_in(xnt, dp, riders, slabs):
    d, t_len = xnt.shape
    cols = dp.shape[1] // N_DEV
    half = d // 2
    return _weight_grad_stage1(
        "weight_grad_in", (d, cols), 2, (xnt, dp),
        [pl.BlockSpec(memory_space=pltpu.VMEM), pl.BlockSpec((t_len, cols), lambda s, o: (0, o[s]))],
        lambda refs, h: _dot(refs[0][h * half:(h + 1) * half, :], refs[1][...]), riders, slabs)


def _weight_grad_out(yt, dhb):
    d2, t_len = yt.shape
    d = dhb.shape[1]
    rows = d2 // N_DEV
    return _weight_grad_stage1(
        "weight_grad_out", (rows, d), 1, (yt, dhb),
        [pl.BlockSpec((rows, t_len), lambda s, o: (o[s], 0)), pl.BlockSpec(memory_space=pltpu.VMEM)],
        lambda refs, h: _dot(refs[0][...], refs[1][...]))


def _update_shard(own, others, w, m, v, name):
    n_rows, n_cols = w.shape
    rb = min(256, n_rows)
    n_other = len(others)

    def body(own_ref, *refs):
        other_refs = refs[:n_other]
        w_ref, m_ref, v_ref, grad_ref, delta_ref, mo_ref, vo_ref = refs[n_other:]
        g = own_ref[...]
        for ref in other_refs:
            for k in range(ref.shape[0] if len(ref.shape) == 3 else 1):
                g = g + (ref[k] if len(ref.shape) == 3 else ref[...]).astype(F32)
        delta, m_new, v_new = _adamw(w_ref[...], g, m_ref[...], v_ref[...])
        grad_ref[...] = g
        delta_ref[...] = delta
        mo_ref[...] = m_new
        vo_ref[...] = v_new

    blk = pl.BlockSpec((rb, n_cols), lambda i: (i, 0))
    stacked = lambda n: pl.BlockSpec((n, rb, n_cols), lambda i: (0, i, 0))
    out = jax.ShapeDtypeStruct((n_rows, n_cols), F32)
    return pl.pallas_call(
        body, name=name, grid=(n_rows // rb,),
        in_specs=[blk] + [stacked(o.shape[0]) if o.ndim == 3 else blk for o in others] + [blk, blk, blk],
        out_specs=(blk, blk, blk, blk), out_shape=(out, out, out, out),
        compiler_params=_params(dimension_semantics=("arbitrary",)),
    )(own, *others, w, m, v)


def _small_update(gat_v, gat_g, ln_tot, vec_w, vec_m, vec_v, gates, convs):
    n_vec = len(vec_w)
    n_heads, hd, _ = gates[0].shape
    tw = gat_g.shape[2]
    s8 = SUBLANES
    per = tw // hd
    n_tiles = n_heads // per
    cc = convs[0].shape[1]
    n_in = 3 + 3 * n_vec + 12

    def body(*refs):
        gv_ref, gg_ref, ln_ref = refs[:3]
        w_refs, m_refs, v_refs = (refs[3 + j * n_vec:3 + (j + 1) * n_vec] for j in range(3))
        gate_refs = refs[3 + 3 * n_vec:3 + 3 * n_vec + 6]
        conv_refs = refs[3 + 3 * n_vec + 6:n_in]
        loss_o = refs[n_in]
        kinds = [refs[n_in + 1 + j * (n_vec + 4):n_in + 1 + (j + 1) * (n_vec + 4)] for j in range(4)]
        tv, tg = refs[n_in + 1 + 4 * (n_vec + 4):]
        x, y, c = _mesh_pos()
        me = 4 * x + 2 * y + c

        def emit(k_out, w, g, m, v):
            delta, m_new, v_new = _adamw(w, g, m, v)
            for ref, val in zip(k_out, (g, delta, m_new, v_new)):
                ref[...] = val

        total = gv_ref[0]
        for dev in range(1, N_DEV):
            total = total + gv_ref[dev]
        tv[...] = total
        tv[SL_LN_G:SL_LN_G + 1, :] = ln_ref[0:1, :]

        def sum_gates(r, carry):
            rows = pl.ds(r, 2 * s8)
            part = gg_ref[0, rows, :].astype(F32)
            for dev in range(1, N_DEV):
                part = part + gg_ref[dev, rows, :].astype(F32)
            tg[rows, :] = part
            return carry

        _chunks(tg.shape[0], 2 * s8, sum_gates, 0)
        loss_o[...] = jnp.broadcast_to(tv[SL_LOSS:SL_LOSS + 1, 0:LANES], loss_o.shape)
        for p in range(n_vec):
            w, g = w_refs[p][...], tv[SL_LN_G + p, :]
            if SL_LN_G + p == SL_LAM:
                g = g * (RG_LRU_C * jax.nn.sigmoid(-w))
            emit([k_out[p] for k_out in kinds], w, g, m_refs[p][...], v_refs[p][...])
        lanes = pl.ds(pl.multiple_of(me * cc, cc), cc)
        for j, (row0, n) in enumerate(((SL_CONV_W, 3), (SL_LRU_W, 4))):
            w_ref, m_ref, v_ref = conv_refs[3 * j:3 * j + 3]
            emit([k_out[n_vec + 2 + j] for k_out in kinds], w_ref[...], tv[row0:row0 + n, lanes], m_ref[...], v_ref[...])
        for mtx in range(2):
            w_ref, m_ref, v_ref = gate_refs[3 * mtx:3 * mtx + 3]
            for k in range(n_tiles):
                tile = tg[(mtx * n_tiles + k) * hd:(mtx * n_tiles + k + 1) * hd, :]
                for a in range(per):
                    head = k * per + a
                    g = tile[:, a * hd:(a + 1) * hd]
                    delta, m_new, v_new = _adamw(w_ref[head], g, m_ref[head], v_ref[head])
                    for k_out, val in zip(kinds, (g, delta, m_new, v_new)):
                        k_out[n_vec + mtx][head] = val

    vm = pl.BlockSpec(memory_space=pltpu.VMEM)
    like = lambda a: jax.ShapeDtypeStruct(a.shape, F32)
    per_kind = tuple(like(a) for a in vec_w) + (like(gates[0]), like(gates[3]), like(convs[0]), like(convs[3]))
    n_out = 1 + 4 * len(per_kind)
    outs = pl.pallas_call(
        body, name="small_update",
        in_specs=[vm] * n_in, out_specs=(vm,) * n_out,
        out_shape=(jax.ShapeDtypeStruct((SUBLANES, LANES), F32),) + per_kind * 4,
        scratch_shapes=[pltpu.VMEM(gat_v.shape[1:], F32), pltpu.VMEM(gat_g.shape[1:], F32)],
        compiler_params=_params(),
    )(gat_v, gat_g, ln_tot, *vec_w, *vec_m, *vec_v, *gates, *convs)
    return outs[0], [outs[1 + j * len(per_kind):1 + (j + 1) * len(per_kind)] for j in range(4)]


def _head_ones(head_dim, tw):
    lane = jnp.arange(tw) // head_dim
    return (lane[:, None] == lane[None, :]).astype(MM)


def kernel(x, ln_g, w_in, conv_w, lru_conv_w, lru_conv_b, w_a, b_a, w_i, b_i, lam, conv_out_g, lru_out_g, w_out, final_g, loss_target, m_ln_g, m_w_in, m_conv_w, m_lru_conv_w, m_lru_conv_b, m_w_a, m_b_a, m_w_i, m_b_i, m_lam, m_conv_out_g, m_lru_out_g, m_w_out, m_final_g, v_ln_g, v_w_in, v_conv_w, v_lru_conv_w, v_lru_conv_b, v_w_a, v_b_a, v_w_i, v_b_i, v_lam, v_conv_out_g, v_lru_out_g, v_w_out, v_final_g):
    _, t_len, d = x.shape
    hd_l = d // N_LRU_HEADS
    tw = min(MXU_TILE, d)
    x2, tgt2 = x[0], loss_target[0]

    small = [ln_g, lru_conv_b, b_a, b_i, lam, conv_out_g, lru_out_g, final_g]
    p, xnt, win_all, wout_all, _, sp, wa_t, wi_t = _gather_project(
        x2, w_in, w_out, conv_w, lru_conv_w, ln_g.reshape(1, d), w_a, w_i, small, min(256, t_len), tw)
    wout_full = wout_all.reshape(N_DEV * w_out.shape[0], d)
    ones_c, ones_l = _head_ones(d // N_CONV_HEADS, tw), _head_ones(hd_l, tw)

    h, dh, dhb, facc, *saved = _forward(x2, tgt2, p, wout_full, wa_t, wi_t, sp, ones_c, ones_l, min(256, t_len))
    dp, yt, slab_v, slab_g = _backward(p, h, dh, saved, facc, wout_full, wa_t, wi_t, sp, ones_c, ones_l, min(256, t_len))
    part_out, own_out, _, _ = _weight_grad_out(yt, dhb)
    part_in, own_in, (chips_out,), (gat_v, gat_g) = _weight_grad_in(xnt, dp, (part_out,), (slab_v, slab_g))
    grad_x, ln_tot, sums_in = _input_grad(dp, win_all, x2, dh, sp, part_in, min(512, t_len))
    gw_in, dw_in, mw_in, vw_in = _update_shard(own_in, sums_in, w_in, m_w_in, v_w_in, "update_w_in")
    gw_out, dw_out, mw_out, vw_out = _update_shard(own_out, (chips_out,), w_out, m_w_out, v_w_out, "update_w_out")

    loss_tile, kinds = _small_update(
        gat_v, gat_g, ln_tot, small,
        [m_ln_g, m_lru_conv_b, m_b_a, m_b_i, m_lam, m_conv_out_g, m_lru_out_g, m_final_g],
        [v_ln_g, v_lru_conv_b, v_b_a, v_b_i, v_lam, v_conv_out_g, v_lru_out_g, v_final_g],
        (w_a, m_w_a, v_w_a, w_i, m_w_i, v_w_i), (conv_w, m_conv_w, v_conv_w, lru_conv_w, m_lru_conv_w, v_lru_conv_w))

    def unpack(kind, big_in, big_out):
        vec, (wa_, wi_, cw_, lw_) = kind[:len(small)], kind[len(small):]
        return [vec[0], big_in, cw_, lw_, vec[1], wa_, vec[2], wi_, vec[3], vec[4], vec[5], vec[6], big_out, vec[7]]

    return (loss_tile[0, 0], grad_x[None], *unpack(kinds[0], gw_in, gw_out), *unpack(kinds[1], dw_in, dw_out),
            *unpack(kinds[2], mw_in, mw_out), *unpack(kinds[3], vw_in, vw_out))
```

```python
import functools

import jax
import jax.numpy as jnp
from jax import lax
from jax.experimental import pallas as pl
from jax.experimental.pallas import tpu as pltpu

F32 = jnp.float32
MM = jnp.bfloat16
MESH = pl.DeviceIdType.MESH

N_DEV = 8
N_CONV_HEADS = 8
N_LRU_HEADS = 16
RG_LRU_C = 8.0
RMS_EPS = 1e-6
ADAM_LR, ADAM_B1, ADAM_B2, ADAM_EPS, ADAM_WD, ADAM_STEP = 0.001, 0.9, 0.999, 1e-08, 0.01, 10
ADAM_BC1 = 1.0 - ADAM_B1 ** ADAM_STEP
ADAM_BC2 = 1.0 - ADAM_B2 ** ADAM_STEP

SUBLANES = 8
LANES = 128
MXU_TILE = 256
VMEM_LIMIT = 56 * 1024 * 1024

SP_LN_G, SP_LRU_B, SP_B_A, SP_B_I, SP_LAM, SP_CONV_G, SP_LRU_G, SP_FINAL_G, SP_CONV_W, SP_LRU_W = 0, 1, 2, 3, 4, 5, 6, 7, 8, 11
SP_ROWS = 16
P_B, P_C, P_XC, P_GC, P_XL, P_GL = 0, 1, 2, 3, 4, 5
A_CONV_G, A_LRU_G, A_LAM, A_B_A, A_B_I, A_CONV_W, A_LRU_W, A_LRU_B = 0, 1, 2, 3, 4, 5, 8, 12
A_GROUPS = 13
SL_LOSS, SL_LN_G, SL_LRU_B, SL_B_A, SL_B_I, SL_LAM, SL_CONV_G, SL_LRU_G, SL_FINAL_G, SL_CONV_W, SL_LRU_W = 0, 1, 2, 3, 4, 5, 6, 7, 8, 16, 24
SL_ROWS = 32


def _params(vmem=True, **kw):
    if vmem:
        kw["vmem_limit_bytes"] = VMEM_LIMIT
    return pltpu.CompilerParams(**kw)


def _dot(a, b):
    return jnp.dot(a, b, preferred_element_type=F32)


def _dot_nt(a, b):
    return lax.dot_general(a, b, (((1,), (1,)), ((), ())), preferred_element_type=F32)


def _head_sums(v, ones_tile):
    tw = ones_tile.shape[0]
    vb = v.astype(MM)
    return jnp.concatenate([_dot(vb[:, k:k + tw], ones_tile) for k in range(0, v.shape[1], tw)], axis=1)


def _head_rstd(v, ones_tile, head_dim):
    return lax.rsqrt(_head_sums(v * v, ones_tile) * (1.0 / head_dim) + RMS_EPS)


def _sigmoid(x):
    return 0.5 * jnp.tanh(0.5 * x) + 0.5


def _lru_input_scale_sq(log_a, a):
    return -jnp.tanh(log_a) * (1.0 + a * a)


def _log_sigmoid(x):
    z = jnp.exp(-jnp.abs(x))
    u = 1.0 + z
    log1p_z = jnp.where(u == 1.0, z, jnp.log(u) * (z / (u - 1.0)))
    return jnp.minimum(x, 0.0) - log1p_z


def _row_iota(d):
    return lax.broadcasted_iota(jnp.int32, (SUBLANES, d), 0)


def _shift_down(cur, prev, s, row):
    return jnp.where(row >= s, pltpu.roll(cur, s, axis=0), pltpu.roll(prev, s, axis=0))


def _shift_up(cur, nxt, s, row):
    k = SUBLANES - s
    return jnp.where(row < k, pltpu.roll(cur, k, axis=0), pltpu.roll(nxt, k, axis=0))


def _scan_fwd(a, b, h_prev, row):
    for s in (1, 2, 4):
        a_s = jnp.where(row >= s, pltpu.roll(a, s, axis=0), 1.0)
        b_s = jnp.where(row >= s, pltpu.roll(b, s, axis=0), 0.0)
        b = a * b_s + b
        a = a * a_s
    return a * h_prev + b


def _scan_bwd(a_next, b, g_next, row):
    a = a_next
    for s in (1, 2, 4):
        k = SUBLANES - s
        a_s = jnp.where(row < k, pltpu.roll(a, k, axis=0), 1.0)
        b_s = jnp.where(row < k, pltpu.roll(b, k, axis=0), 0.0)
        b = a * b_s + b
        a = a * a_s
    return a * g_next + b


def _bcast_row(v, r):
    return jnp.broadcast_to(v[r:r + 1, :], v.shape)


def _chunks(n_rows, rc, body, init, reverse=False):
    n = n_rows // rc

    def step(i, carry):
        j = (n - 1 - i) if reverse else i
        return body(pl.multiple_of(j * rc, rc), carry)

    return lax.fori_loop(0, n, step, init)


def _adamw(w, g, m, v):
    m = ADAM_B1 * m + (1.0 - ADAM_B1) * g
    v = ADAM_B2 * v + (1.0 - ADAM_B2) * (g * g)
    m_hat = m / ADAM_BC1
    v_hat = v / ADAM_BC2
    delta = -ADAM_LR * (m_hat / (jnp.sqrt(v_hat) + ADAM_EPS) + ADAM_WD * w)
    return delta, m, v


def _mesh_pos():
    return lax.axis_index("x"), lax.axis_index("y"), lax.axis_index("c")


class _Gather:
    def __init__(self, blocks_of, send_sems, recv_sems, own_src=None):
        x, y, c = _mesh_pos()
        self.c = c
        self.me, self.sibling = (x, y, c), (x, y, 1 - c)
        self.chips = [(1 - x, y), (x, 1 - y), (1 - x, 1 - y)]
        self.blocks_of, self.send_sems, self.recv_sems = blocks_of, send_sems, recv_sems
        self.own_src = own_src

    def copy(self, a, k, block, to):
        src = self.blocks_of(a, *block)
        if block is self.me and self.own_src is not None:
            src = self.own_src[a]
        return pltpu.make_async_remote_copy(
            src_ref=src, dst_ref=self.blocks_of(a, *block),
            send_sem=self.send_sems.at[a * 7 + k], recv_sem=self.recv_sems.at[a * 7 + k],
            device_id=to, device_id_type=MESH)

    def start_own(self, a):
        self.copy(a, 0, self.me, self.sibling).start()
        for j, chip in enumerate(self.chips):
            self.copy(a, 1 + j, self.me, (*chip, self.c)).start()

    def wait_sibling(self, a):
        self.copy(a, 0, self.sibling, self.me).wait_recv()

    def wait_chip_and_pass_on(self, a, j):
        block = (*self.chips[j], self.c)
        self.copy(a, 1 + j, block, self.me).wait_recv()
        self.copy(a, 4 + j, block, self.sibling).start()

    def wait_passed_on(self, a, j):
        self.copy(a, 4 + j, (*self.chips[j], 1 - self.c), self.me).wait_recv()

    def wait_sends(self, a):
        self.copy(a, 0, self.me, self.sibling).wait_send()
        for j, chip in enumerate(self.chips):
            self.copy(a, 1 + j, self.me, (*chip, self.c)).wait_send()
            self.copy(a, 4 + j, (*chip, self.c), self.sibling).wait_send()

    def finish(self, a):
        for j in range(3):
            self.wait_chip_and_pass_on(a, j)
        self.wait_sibling(a)
        for j in range(3):
            self.wait_passed_on(a, j)
        self.wait_sends(a)


class _BalancedGather:
    def __init__(self, slot, send_sems, recv_sems, own_src):
        x, y, c = _mesh_pos()
        self.c = c
        self.me, self.sibling = (x, y, c), (x, y, 1 - c)
        self.chips = [(1 - x, y), (x, 1 - y), (1 - x, 1 - y)]
        self.slot, self.send_sems, self.recv_sems, self.own_src = slot, send_sems, recv_sems, own_src

    def half(self, a, block, which):
        ref = self.slot(a, *block)
        n = ref.shape[0] // 2
        return ref.at[pl.ds(which * n, n)]

    def copy(self, a, k, src, dst, to):
        return pltpu.make_async_remote_copy(
            src_ref=src, dst_ref=dst, send_sem=self.send_sems.at[a * 8 + k], recv_sem=self.recv_sems.at[a * 8 + k],
            device_id=to, device_id_type=MESH)

    def whole(self, a, k, block, to):
        src = self.own_src[a] if block is self.me else self.slot(a, *block)
        return self.copy(a, k, src, self.slot(a, *block), to)

    def halved(self, a, k, block, which, to):
        return self.copy(a, k, self.half(a, block, which), self.half(a, block, which), to)

    def on(self, chip):
        return (*self.chips[chip], self.c)

    def start_own(self, a):
        self.whole(a, 0, self.me, self.sibling).start()
        self.whole(a, 1, self.me, self.on(0)).start()
        self.whole(a, 2, self.me, self.on(1)).start()

    def wait_sibling(self, a):
        self.whole(a, 0, self.sibling, self.me).wait_recv()

    def on_neighbour(self, a, j):
        self.whole(a, 1 + j, self.on(j), self.me).wait_recv()
        self.halved(a, 3 + j, self.on(j), j, self.on(1 - j)).start()
        self.whole(a, 5 + j, self.on(j), self.sibling).start()

    def on_diagonal(self, a):
        self.halved(a, 3, self.on(2), 0, self.me).wait_recv()
        self.halved(a, 4, self.on(2), 1, self.me).wait_recv()
        self.whole(a, 7, self.on(2), self.sibling).start()

    def wait_passed_on(self, a, j):
        self.whole(a, 5 + j, (*self.chips[j], 1 - self.c), self.me).wait_recv()

    def wait_sends(self, a):
        self.whole(a, 0, self.me, self.sibling).wait_send()
        for j in range(2):
            self.whole(a, 1 + j, self.me, self.on(j)).wait_send()
            self.halved(a, 3 + j, self.on(j), j, self.on(1 - j)).wait_send()
        for j in range(3):
            self.whole(a, 5 + j, self.on(j), self.sibling).wait_send()


def _block_order():
    x, y, c = _mesh_pos()
    chips = [(x, y), (1 - x, y), (x, 1 - y), (1 - x, 1 - y)]
    return jnp.stack([4 * px + 2 * py + pc for px, py in chips for pc in (c, 1 - c)]).astype(jnp.int32)


def _gather_project(x, w_in, w_out, conv_w, lru_conv_w, ln_g, w_a, w_i, vecs, tb, tw):
    t_len, d = x.shape
    nb = t_len // tb
    cols = w_in.shape[1]
    mc = min(512, t_len)
    conv_pack = jax.ShapeDtypeStruct((SUBLANES, conv_w.shape[1]), F32)
    srcs = (w_in, w_out, conv_pack)
    dts = (MM, MM, F32)
    n_vec = len(vecs)
    n_heads, hd, _ = w_a.shape
    per = tw // hd

    def body(order_ref, x_ref, win_ref, wout_ref, cw_ref, lw_ref, lng_ref, wa_ref, wi_ref, *refs):
        vec_refs = refs[:n_vec]
        (p_ref, xnt_ref, win_all, wout_all, cp_all, sp_ref, wat_ref, wit_ref,
         xnb, wall, st_out, st_cp, cp_vm, send_sems, recv_sems, cp_send, cp_recv, local_sems) = refs[n_vec:]
        i = pl.program_id(0)
        x_, y_, c_ = _mesh_pos()
        me = 4 * x_ + 2 * y_ + c_
        outs = (win_all, wout_all, cp_all)
        lands = (wall, wout_all, cp_all)
        stages = (wall.at[me], st_out, st_cp)
        gather = _BalancedGather(lambda a, px, py, pc: lands[a].at[4 * px + 2 * py + pc], send_sems, recv_sems, stages)
        small = _Gather(lambda a, px, py, pc: cp_all.at[4 * px + 2 * py + pc], cp_send, cp_recv, own_src=[st_cp])
        keep_own = [pltpu.make_async_copy(stages[a], outs[a].at[me], local_sems.at[a]) for a in range(3)]

        def keep(k):
            blk = order_ref[k]
            return pltpu.make_async_copy(wall.at[blk], win_all.at[blk], local_sems.at[2 + k])

        @pl.when(i == 0)
        def _():
            for a, src in enumerate((win_ref, wout_ref)):
                dst, rc = stages[a], 32

                def cast(r, carry, src=src, dst=dst):
                    dst[pl.ds(r, rc), :] = src[pl.ds(r, rc), :].astype(dst.dtype)
                    return carry

                _chunks(src.shape[0], rc, cast, 0)
                keep_own[a].start()
            gather.start_own(0)
            n_cw, n_lw = cw_ref.shape[0], lw_ref.shape[0]
            st_cp[...] = jnp.zeros_like(st_cp)
            st_cp[0:n_cw, :] = cw_ref[...]
            st_cp[n_cw:n_cw + n_lw, :] = lw_ref[...]
            keep_own[2].start()

        @pl.when(i < nb)
        def _():
            xv = x_ref[...]
            r0 = lax.rsqrt(jnp.mean(xv * xv, axis=-1, keepdims=True) + RMS_EPS)
            xn = xv * r0 * lng_ref[...]
            xnb[pl.ds(pl.multiple_of(i * tb, tb), tb), :] = xn.astype(MM)
            xnt_ref[...] = xn.T.astype(MM)

        for k in range(N_DEV):
            @pl.when(i == nb + k)
            def _(k=k):
                if k == 1:
                    gather.wait_sibling(0)
                elif k == 2:
                    gather.on_neighbour(0, 0)
                    gather.on_neighbour(0, 1)
                    gather.start_own(1)
                    small.start_own(0)
                elif k in (3, 5, 7):
                    gather.wait_passed_on(0, (k - 3) // 2)
                    if k == 5:
                        gather.on_neighbour(1, 0)
                        gather.on_neighbour(1, 1)
                    if k == 7:
                        gather.on_diagonal(1)
                elif k == 6:
                    gather.on_diagonal(0)
                blk = order_ref[k]
                if k:
                    keep(k).start()

                def project(r, carry):
                    rows = pl.ds(r, mc)
                    p_ref[rows, :] = _dot(xnb[rows, :], wall[blk]).astype(MM)
                    return carry

                _chunks(t_len, mc, project, 0)
                if k == N_DEV - 1:
                    gather.wait_sends(0)
                    gather.wait_sibling(1)
                    for j in range(3):
                        gather.wait_passed_on(1, j)
                    gather.wait_sends(1)
                    small.finish(0)
                    for cp in keep_own + [keep(kk) for kk in range(1, N_DEV)]:
                        cp.wait()
                    load = pltpu.make_async_copy(cp_all, cp_vm, local_sems.at[N_DEV + 2])
                    load.start()
                    load.wait()
                    for r, ref in enumerate(vec_refs):
                        sp_ref[r, :] = ref[...]
                    sp_ref[n_vec:n_vec + SUBLANES, :] = jnp.concatenate([cp_vm[dev] for dev in range(N_DEV)], axis=1)
                    for src, dst in ((wa_ref, wat_ref), (wi_ref, wit_ref)):
                        dst[...] = jnp.zeros_like(dst)
                        for head in range(n_heads):
                            lo = (head % per) * hd
                            dst[head // per, lo:lo + hd, lo:lo + hd] = src[head].astype(MM)

    vm = pl.BlockSpec(memory_space=pltpu.VMEM)
    hbm = pl.BlockSpec(memory_space=pl.ANY)
    grid_spec = pltpu.PrefetchScalarGridSpec(
        num_scalar_prefetch=1, grid=(nb + N_DEV,),
        in_specs=[pl.BlockSpec((tb, d), lambda i, o: (jnp.minimum(i, nb - 1), 0))] + [vm] * (7 + n_vec),
        out_specs=(pl.BlockSpec((t_len, cols), lambda i, o: (0, o[jnp.maximum(i - nb, 0)])),
                   pl.BlockSpec((d, tb), lambda i, o: (0, jnp.minimum(i, nb - 1))), hbm, hbm, hbm,
                   pl.BlockSpec((SP_ROWS, d), lambda i, o: (0, 0)),
                   pl.BlockSpec((n_heads // per, tw, tw), lambda i, o: (0, 0, 0)),
                   pl.BlockSpec((n_heads // per, tw, tw), lambda i, o: (0, 0, 0))),
        scratch_shapes=[pltpu.VMEM((t_len, d), MM), pltpu.VMEM((N_DEV,) + w_in.shape, MM),
                        pltpu.VMEM(w_out.shape, MM), pltpu.VMEM(conv_pack.shape, F32),
                        pltpu.VMEM((N_DEV,) + conv_pack.shape, F32),
                        pltpu.SemaphoreType.DMA((16,)), pltpu.SemaphoreType.DMA((16,)),
                        pltpu.SemaphoreType.DMA((7,)), pltpu.SemaphoreType.DMA((7,)), pltpu.SemaphoreType.DMA((N_DEV + 3,))])
    return pl.pallas_call(
        body, name="gather_project", grid_spec=grid_spec,
        out_shape=(jax.ShapeDtypeStruct((t_len, N_DEV * cols), MM),
                   jax.ShapeDtypeStruct((d, t_len), MM))
                  + tuple(jax.ShapeDtypeStruct((N_DEV,) + s.shape, dt) for s, dt in zip(srcs, dts))
                  + (jax.ShapeDtypeStruct((SP_ROWS, d), F32),)
                  + (jax.ShapeDtypeStruct((n_heads // per, tw, tw), MM),) * 2,
        compiler_params=_params(dimension_semantics=("arbitrary",)),
    )(_block_order(), x, w_in, w_out, conv_w, lru_conv_w, ln_g, w_a, w_i, *vecs)


def _forward(x, tgt, p, wout, wa_t, wi_t, sp, ones_c, ones_l, tb):
    t_len, d = x.shape
    nb = t_len // tb
    n_tiles, tw = wa_t.shape[0], wa_t.shape[1]
    hd_c, hd_l = d // N_CONV_HEADS, d // N_LRU_HEADS
    s8 = SUBLANES

    def body(x_ref, tgt_ref, p_ref, wout_ref, wa_ref, wi_ref, sp_ref, oc_ref, ol_ref,
             h_ref, dh_ref, dhb_ref, acc_ref, yc, czs, u, pa, pi,
             rcf, rlf, ybuf, tail_z, tail_xl, hcar):
        i = pl.program_id(0)
        row = _row_iota(d)

        @pl.when(i == 0)
        def _():
            tail_z[...] = jnp.zeros_like(tail_z)
            tail_xl[...] = jnp.zeros_like(tail_xl)
            hcar[...] = jnp.zeros_like(hcar)
            acc_ref[...] = jnp.zeros_like(acc_ref)

        def spr(r):
            return sp_ref[r:r + 1, :]

        def proj(rows, seg):
            return p_ref[rows, seg * d:(seg + 1) * d].astype(F32)

        w0, w1, w2 = spr(SP_CONV_W), spr(SP_CONV_W + 1), spr(SP_CONV_W + 2)
        l0, l1, l2, l3 = spr(SP_LRU_W), spr(SP_LRU_W + 1), spr(SP_LRU_W + 2), spr(SP_LRU_W + 3)
        lb = spr(SP_LRU_B)

        def convs(r, carry):
            zp, xp = carry
            rows16 = pl.ds(r, 2 * s8)
            bg16, xl16 = proj(rows16, P_B), proj(rows16, P_XL)
            z16 = proj(rows16, P_C) * proj(rows16, P_XC)
            for j in range(2):
                rows, sub = pl.ds(r + j * s8, s8), slice(j * s8, (j + 1) * s8)
                z, xl = z16[sub], xl16[sub]
                cz = w0 * _shift_down(z, zp, 2, row) + w1 * _shift_down(z, zp, 1, row) + w2 * z
                czs[rows, :] = cz
                yc[rows, :] = bg16[sub] * cz
                u[rows, :] = (l0 * _shift_down(xl, xp, 3, row) + l1 * _shift_down(xl, xp, 2, row)
                              + l2 * _shift_down(xl, xp, 1, row) + l3 * xl + lb)
                zp, xp = z, xl
            return zp, xp

        z_last, xl_last = _chunks(tb, 2 * s8, convs, (tail_z[...], tail_xl[...]))
        tail_z[...] = z_last
        tail_xl[...] = xl_last

        ub = u[...].astype(MM)
        for k in range(n_tiles):
            sl = slice(k * tw, (k + 1) * tw)
            pa[:, sl] = _dot(ub[:, sl], wa_ref[k])
            pi[:, sl] = _dot(ub[:, sl], wi_ref[k])
        rcf[...] = _head_rstd(yc[...], oc_ref[...], hd_c)

        c8 = RG_LRU_C * _log_sigmoid(spr(SP_LAM))
        b_a, b_i = spr(SP_B_A), spr(SP_B_I)

        def lru(r, hp):
            rows = pl.ds(r, SUBLANES)
            ra = _sigmoid(pa[rows, :] + b_a)
            ii = _sigmoid(pi[rows, :] + b_i)
            pa[rows, :] = ra
            pi[rows, :] = ii
            la = ra * c8
            a = jnp.exp(la)
            mult = jnp.sqrt(_lru_input_scale_sq(la, a))
            h = _scan_fwd(a, mult * (ii * u[rows, :]), hp, row)
            h_ref[rows, :] = h
            return _bcast_row(h, SUBLANES - 1)

        hcar[...] = _chunks(tb, SUBLANES, lru, hcar[...])
        rlf[...] = _head_rstd(h_ref[...], ol_ref[...], hd_l)

        g_c, g_l = spr(SP_CONV_G), spr(SP_LRU_G)

        def gate(r, carry):
            rows = pl.ds(r, 2 * s8)
            gc, gl = proj(rows, P_GC), proj(rows, P_GL)
            ybuf[rows, 0:d] = (yc[rows, :] * rcf[rows, :] * g_c * (gc * _sigmoid(gc))).astype(MM)
            ybuf[rows, d:2 * d] = (h_ref[rows, :] * rlf[rows, :] * g_l * (gl * _sigmoid(gl))).astype(MM)
            return carry

        _chunks(tb, 2 * s8, gate, 0)

        hres = x_ref[...] + _dot(ybuf[...], wout_ref[...])
        rf = lax.rsqrt(jnp.mean(hres * hres, axis=-1, keepdims=True) + RMS_EPS)
        hn = hres * rf
        fg = spr(SP_FINAL_G)
        err = hn * fg - tgt_ref[...]
        dout = err * (1.0 / d)
        acc_ref[0:SUBLANES, :] += (err * err).reshape(tb // SUBLANES, SUBLANES, d).sum(axis=0)
        acc_ref[SUBLANES:2 * SUBLANES, :] += (dout * hn).reshape(tb // SUBLANES, SUBLANES, d).sum(axis=0)
        gd = dout * fg
        dhres = rf * (gd - hn * jnp.mean(gd * hn, axis=-1, keepdims=True))
        dh_ref[...] = dhres
        dhb_ref[...] = dhres.astype(MM)

    vm = pl.BlockSpec(memory_space=pltpu.VMEM)
    blk = lambda w: pl.BlockSpec((tb, w), lambda i: (i, 0))
    buf = pltpu.VMEM((tb, d), F32)
    car = pltpu.VMEM((SUBLANES, d), F32)
    return pl.pallas_call(
        body, name="forward", grid=(nb,),
        in_specs=[blk(d), blk(d), blk(6 * d), vm, vm, vm, vm, vm, vm],
        out_specs=(blk(d), blk(d), blk(d), pl.BlockSpec((2 * SUBLANES, d), lambda i: (0, 0))) + (blk(d),) * 5,
        out_shape=(jax.ShapeDtypeStruct((t_len, d), F32),
                   jax.ShapeDtypeStruct((t_len, d), F32),
                   jax.ShapeDtypeStruct((t_len, d), MM),
                   jax.ShapeDtypeStruct((2 * SUBLANES, d), F32))
                  + (jax.ShapeDtypeStruct((t_len, d), F32),) * 5,
        scratch_shapes=[buf] * 2 + [pltpu.VMEM((tb, 2 * d), MM), car, car, car],
        compiler_params=_params(dimension_semantics=("arbitrary",)),
    )(x, tgt, p, wout, wa_t, wi_t, sp, ones_c, ones_l)


def _backward(p, h, dh, saved, facc, wout, wa_t, wi_t, sp, ones_c, ones_l, tb):
    t_len, d = h.shape
    nb = t_len // tb
    n_tiles, tw = wa_t.shape[0], wa_t.shape[1]
    hd_c, hd_l = d // N_CONV_HEADS, d // N_LRU_HEADS
    g_rows = 2 * n_tiles * hd_l
    s8 = SUBLANES

    def body(p_ref, h_ref, hhalo_ref, dh_ref, yc, czs, u, ra_ref, ii_ref, facc_ref,
             wout_ref, wa_ref, wi_ref, sp_ref, oc_ref, ol_ref,
             dp_ref, yt_ref, slab_v, slab_g,
             hh, dy, ybuf, rcf, rlf, qc, ql, dyc_hat, dyl_hat, dpa, dpi, du, gwa_ref, gwi_ref, acc_ref,
             car_dcz, car_a, car_g, car_du):
        i = pl.program_id(0)
        blk_idx = nb - 1 - i
        row = _row_iota(d)

        @pl.when(i == 0)
        def _():
            for ref in (car_dcz, car_a, car_g, car_du, gwa_ref, gwi_ref, acc_ref):
                ref[...] = jnp.zeros_like(ref)

        def spr(r):
            return sp_ref[r:r + 1, :]

        def proj(rows, seg):
            return p_ref[rows, seg * d:(seg + 1) * d].astype(F32)

        def put(rows, seg, halves):
            dp_ref[rows, seg * d:(seg + 1) * d] = jnp.concatenate(halves, axis=0).astype(MM)

        def acc_add(group, val):
            acc_ref[group * s8:(group + 1) * s8, :] += val

        live = jnp.where(blk_idx > 0, 1.0, 0.0).astype(F32)
        hh[0:s8, :] = hhalo_ref[...] * live
        hh[s8:, :] = h_ref[...]

        dy[...] = _dot_nt(dh_ref[...].astype(MM), wout_ref[...])

        w0, w1, w2 = spr(SP_CONV_W), spr(SP_CONV_W + 1), spr(SP_CONV_W + 2)
        l0, l1, l2, l3 = spr(SP_LRU_W), spr(SP_LRU_W + 1), spr(SP_LRU_W + 2), spr(SP_LRU_W + 3)

        rcf[...] = _head_rstd(yc[...], oc_ref[...], hd_c)
        rlf[...] = _head_rstd(h_ref[...], ol_ref[...], hd_l)

        g_c, g_l = spr(SP_CONV_G), spr(SP_LRU_G)

        def gates(r, carry):
            rows = pl.ds(r, 2 * s8)
            for (seg, off_y, src, rstd, gain, q, dhat, grp) in (
                    (P_GC, 0, yc, rcf, g_c, qc, dyc_hat, A_CONV_G),
                    (P_GL, d, h_ref, rlf, g_l, ql, dyl_hat, A_LRU_G)):
                gt = proj(rows, seg)
                sg = _sigmoid(gt)
                silu = gt * sg
                yhat = src[rows, :] * rstd[rows, :]
                nrm = yhat * gain
                ybuf[rows, off_y:off_y + d] = nrm * silu
                dout = dy[rows, off_y:off_y + d]
                dnrm = dout * silu
                dp_ref[rows, seg * d:(seg + 1) * d] = (dout * nrm * (sg * (1.0 + gt * (1.0 - sg)))).astype(MM)
                dg = dnrm * yhat
                acc_add(grp, dg[0:s8] + dg[s8:])
                dh_ = dnrm * gain
                dhat[rows, :] = dh_
                q[rows, :] = dh_ * yhat
            return carry

        _chunks(tb, 2 * s8, gates, 0)

        qc[...] = _head_sums(qc[...], oc_ref[...]) * (1.0 / hd_c)
        ql[...] = _head_sums(ql[...], ol_ref[...]) * (1.0 / hd_l)
        yt_ref[...] = ybuf[...].T.astype(MM)

        c8 = RG_LRU_C * _log_sigmoid(spr(SP_LAM))

        def conv_mixer(r, dcz_n):
            rows16 = pl.ds(r, 2 * s8)
            bg16, cg16, xc16 = proj(rows16, P_B), proj(rows16, P_C), proj(rows16, P_XC)
            z16 = cg16 * xc16
            d_b, d_c, d_x = [None, None], [None, None], [None, None]
            for j in (1, 0):
                rows, sub = pl.ds(r + j * s8, s8), slice(j * s8, (j + 1) * s8)
                rstd = rcf[rows, :]
                yhat = yc[rows, :] * rstd
                dyc = rstd * (dyc_hat[rows, :] - yhat * qc[rows, :])
                d_b[j] = dyc * czs[rows, :]
                dcz = dyc * bg16[sub]
                up1, up2 = _shift_up(dcz, dcz_n, 1, row), _shift_up(dcz, dcz_n, 2, row)
                dz = w2 * dcz + w1 * up1 + w0 * up2
                d_c[j] = dz * xc16[sub]
                d_x[j] = dz * cg16[sub]
                z = z16[sub]
                acc_add(A_CONV_W, up2 * z)
                acc_add(A_CONV_W + 1, up1 * z)
                acc_add(A_CONV_W + 2, dcz * z)
                dcz_n = dcz
            put(rows16, P_B, d_b)
            put(rows16, P_C, d_c)
            put(rows16, P_XC, d_x)
            return dcz_n

        car_dcz[...] = _chunks(tb, 2 * s8, conv_mixer, car_dcz[...], reverse=True)

        def lru_mixer(r, carry):
            a_n, g_n = carry
            for j in (1, 0):
                rows = pl.ds(r + j * s8, s8)
                rstd = rlf[rows, :]
                hcur = hh[pl.ds(r + (j + 1) * s8, s8), :]
                hhat = hcur * rstd
                dh_out = rstd * (dyl_hat[rows, :] - hhat * ql[rows, :])
                ra = ra_ref[rows, :]
                la = ra * c8
                a = jnp.exp(la)
                g = _scan_bwd(_shift_up(a, a_n, 1, row), dh_out, g_n, row)
                da = g * _shift_down(hcur, hh[pl.ds(r + j * s8, s8), :], 1, row)
                ii = ii_ref[rows, :]
                uu = u[rows, :]
                mult_sq = _lru_input_scale_sq(la, a)
                inv_mult = lax.rsqrt(mult_sq)
                dmult = g * (ii * uu)
                ds = g * (mult_sq * inv_mult)
                dla = a * (da - dmult * a * inv_mult)
                acc_add(A_LAM, dla * ra)
                dpa_ = dla * c8 * ra * (1.0 - ra)
                dpi_ = ds * uu * ii * (1.0 - ii)
                acc_add(A_B_A, dpa_)
                acc_add(A_B_I, dpi_)
                dpa[rows, :] = dpa_
                dpi[rows, :] = dpi_
                du[rows, :] = ds * ii
                a_n, g_n = a, _bcast_row(g, 0)
            return a_n, g_n

        a_f, g_f = _chunks(tb, 2 * s8, lru_mixer, (car_a[...], car_g[...]), reverse=True)
        car_a[...] = a_f
        car_g[...] = g_f

        dpab = dpa[...].astype(MM)
        dpib = dpi[...].astype(MM)
        for k in range(n_tiles):
            sl = slice(k * tw, (k + 1) * tw)
            du[:, sl] += _dot_nt(dpab[:, sl], wa_ref[k]) + _dot_nt(dpib[:, sl], wi_ref[k])
            ut = u[:, sl].T.astype(MM)
            gwa_ref[k] += _dot(ut, dpab[:, sl])
            gwi_ref[k] += _dot(ut, dpib[:, sl])

        def lru_conv(r, du_n):
            rows16 = pl.ds(r, 2 * s8)
            xl16 = proj(rows16, P_XL)
            d_xl = [None, None]
            for j in (1, 0):
                rows, sub = pl.ds(r + j * s8, s8), slice(j * s8, (j + 1) * s8)
                dut = du[rows, :]
                up1, up2, up3 = (_shift_up(dut, du_n, s, row) for s in (1, 2, 3))
                d_xl[j] = l3 * dut + l2 * up1 + l1 * up2 + l0 * up3
                xl = xl16[sub]
                acc_add(A_LRU_W, up3 * xl)
                acc_add(A_LRU_W + 1, up2 * xl)
                acc_add(A_LRU_W + 2, up1 * xl)
                acc_add(A_LRU_W + 3, dut * xl)
                acc_add(A_LRU_B, dut)
                du_n = dut
            put(rows16, P_XL, d_xl)
            return du_n

        car_du[...] = _chunks(tb, 2 * s8, lru_conv, car_du[...], reverse=True)

        @pl.when(i == nb - 1)
        def _():
            def rowsum(ref, group):
                return jnp.sum(ref[group * s8:(group + 1) * s8, :], axis=0, keepdims=True)

            slab_v[...] = jnp.zeros_like(slab_v)
            loss = jnp.sum(rowsum(facc_ref, 0), axis=1, keepdims=True) * (0.5 / d)
            rows = {SL_LOSS: jnp.broadcast_to(loss, (1, d)), SL_FINAL_G: rowsum(facc_ref, 1),
                    SL_LRU_B: rowsum(acc_ref, A_LRU_B), SL_B_A: rowsum(acc_ref, A_B_A), SL_B_I: rowsum(acc_ref, A_B_I),
                    SL_LAM: rowsum(acc_ref, A_LAM), SL_CONV_G: rowsum(acc_ref, A_CONV_G), SL_LRU_G: rowsum(acc_ref, A_LRU_G)}
            for k in range(3):
                rows[SL_CONV_W + k] = rowsum(acc_ref, A_CONV_W + k)
            for k in range(4):
                rows[SL_LRU_W + k] = rowsum(acc_ref, A_LRU_W + k)
            for r, val in rows.items():
                slab_v[r:r + 1, :] = val
            head_of_lane = lax.broadcasted_iota(jnp.int32, (hd_l, tw), 1) // hd_l
            for mtx, g_ref in enumerate((gwa_ref, gwi_ref)):
                for k in range(n_tiles):
                    packed = jnp.zeros((hd_l, tw), F32)
                    for a in range(tw // hd_l):
                        packed = jnp.where(head_of_lane == a, g_ref[k, a * hd_l:(a + 1) * hd_l, :], packed)
                    slab_g[(mtx * n_tiles + k) * hd_l:(mtx * n_tiles + k + 1) * hd_l, :] = packed.astype(MM)

    vm = pl.BlockSpec(memory_space=pltpu.VMEM)
    rev = lambda w: pl.BlockSpec((tb, w), lambda i: (nb - 1 - i, 0))
    halo = lambda rows, w: pl.BlockSpec((rows, w), lambda i: (jnp.maximum((nb - 1 - i) * (tb // rows) - 1, 0), 0))
    const = lambda shape: pl.BlockSpec(shape, lambda i: (0,) * len(shape))
    buf = lambda w: pltpu.VMEM((tb, w), F32)
    car = pltpu.VMEM((SUBLANES, d), F32)
    return pl.pallas_call(
        body, name="backward", grid=(nb,),
        in_specs=[rev(6 * d), rev(d), halo(SUBLANES, d), rev(d)] + [rev(d)] * 5 + [vm, vm, vm, vm, vm, vm, vm],
        out_specs=(rev(6 * d), pl.BlockSpec((2 * d, tb), lambda i: (0, nb - 1 - i)),
                   const((SL_ROWS, d)), const((g_rows, tw))),
        out_shape=(jax.ShapeDtypeStruct((t_len, 6 * d), MM),
                   jax.ShapeDtypeStruct((2 * d, t_len), MM),
                   jax.ShapeDtypeStruct((SL_ROWS, d), F32),
                   jax.ShapeDtypeStruct((g_rows, tw), MM)),
        scratch_shapes=[pltpu.VMEM((SUBLANES + tb, d), F32), buf(2 * d), buf(2 * d)] + [buf(d)] * 9
                       + [pltpu.VMEM((n_tiles, tw, tw), F32), pltpu.VMEM((n_tiles, tw, tw), F32),
                          pltpu.VMEM((A_GROUPS * SUBLANES, d), F32), car, car, car, car],
        compiler_params=_params(dimension_semantics=("arbitrary",)),
    )(p, h, h, dh, *saved, facc, wout, wa_t, wi_t, sp, ones_c, ones_l)


def _input_grad(dp, win_all, x, dh, sp, part, tb):
    t_len, d = x.shape
    nb = t_len // tb
    cols = win_all.shape[2]
    mid = min(nb - 1, (5 * nb) // 8)
    rc = 32

    def body(dp_ref, win_ref, x_ref, dh_ref, sp_ref, part_ref, gx_ref, ln_ref, direct, passing, relayed,
             send_sems, recv_sems, local_sems, acc_ref, ln_all, ln_send, ln_recv, mine, theirs):
        i = pl.program_id(0)
        x_, y_, c_ = _mesh_pos()
        first, second = 1 - c_, c_
        nbr1 = (x_ ^ c_, y_ ^ (1 - c_), c_)
        nbr2 = (x_ ^ (1 - c_), y_ ^ c_, c_)

        def remote(src, dst, k, to):
            return pltpu.make_async_remote_copy(src_ref=src, dst_ref=dst, send_sem=send_sems.at[k], recv_sem=recv_sems.at[k],
                                                device_id=to, device_id_type=MESH)

        to_first = [remote(part_ref.at[first], direct, 0, nbr1), remote(part_ref.at[2], passing, 1, nbr1)]
        to_second = remote(theirs, relayed, 2, nbr2)

        @pl.when(i == 0)
        def _():
            acc_ref[...] = jnp.zeros_like(acc_ref)
            for cp in to_first:
                cp.start()

        dxn = _dot_nt(dp_ref[:, 0:cols], win_ref[0])
        for j in range(1, N_DEV):
            dxn += _dot_nt(dp_ref[:, j * cols:(j + 1) * cols], win_ref[j])
        xv = x_ref[...]
        r0 = lax.rsqrt(jnp.mean(xv * xv, axis=-1, keepdims=True) + RMS_EPS)
        xhat = xv * r0
        acc_ref[...] += (dxn * xhat).reshape(tb // SUBLANES, SUBLANES, d).sum(axis=0)
        dxh = dxn * sp_ref[SP_LN_G:SP_LN_G + 1, :]
        gx_ref[...] = dh_ref[...] + r0 * (dxh - xhat * jnp.mean(dxh * xhat, axis=-1, keepdims=True))

        @pl.when(i == mid)
        def _():
            to_first[1].wait_recv()
            loads = [pltpu.make_async_copy(part_ref.at[second], mine, local_sems.at[0]),
                     pltpu.make_async_copy(passing, theirs, local_sems.at[1])]
            for cp in loads:
                cp.start()
            for cp in loads:
                cp.wait()

            def add(r, carry):
                rows = pl.ds(r, rc)
                theirs[rows, :] = (mine[rows, :].astype(F32) + theirs[rows, :].astype(F32)).astype(MM)
                return carry

            _chunks(mine.shape[0], rc, add, 0)
            to_second.start()

        @pl.when(i == nb - 1)
        def _():
            to_first[0].wait_recv()
            to_second.wait_recv()
            for cp in to_first + [to_second]:
                cp.wait_send()
            ln_all[4 * x_ + 2 * y_ + c_] = jnp.broadcast_to(jnp.sum(acc_ref[...], axis=0, keepdims=True), acc_ref.shape)
            gather = _Gather(lambda a, px, py, pc: ln_all.at[4 * px + 2 * py + pc], ln_send, ln_recv)
            gather.start_own(0)
            gather.finish(0)
            total = ln_all[0]
            for dev in range(1, N_DEV):
                total = total + ln_all[dev]
            ln_ref[...] = total

    vm = pl.BlockSpec(memory_space=pltpu.VMEM)
    hbm = pl.BlockSpec(memory_space=pl.ANY)
    blk = lambda w: pl.BlockSpec((tb, w), lambda i: (i, 0))
    landed = jax.ShapeDtypeStruct(part.shape[1:], part.dtype)
    outs = pl.pallas_call(
        body, name="input_grad", grid=(nb,),
        in_specs=[blk(6 * d), vm, blk(d), blk(d), vm, hbm],
        out_specs=(blk(d), pl.BlockSpec((SUBLANES, d), lambda i: (0, 0)), hbm, hbm, hbm),
        out_shape=(jax.ShapeDtypeStruct((t_len, d), F32), jax.ShapeDtypeStruct((SUBLANES, d), F32), landed, landed, landed),
        scratch_shapes=[pltpu.SemaphoreType.DMA((3,)), pltpu.SemaphoreType.DMA((3,)), pltpu.SemaphoreType.DMA((2,)),
                        pltpu.VMEM((SUBLANES, d), F32), pltpu.VMEM((N_DEV, SUBLANES, d), F32),
                        pltpu.SemaphoreType.DMA((7,)), pltpu.SemaphoreType.DMA((7,)),
                        pltpu.VMEM(part.shape[1:], MM), pltpu.VMEM(part.shape[1:], MM)],
        compiler_params=_params(dimension_semantics=("arbitrary",)),
    )(dp, win_all, x, dh, sp, part)
    return outs[0], outs[1], (outs[2], outs[4])


_CHIP_RELATIONS = [(0, 0), (1, 0), (0, 1), (1, 1)]


def _related_block(k, core):
    x, y, _ = _mesh_pos()
    fx, fy = _CHIP_RELATIONS[k]
    return 4 * (x ^ fx) + 2 * (y ^ fy) + core


class _ChipExchange:
    def __init__(self, part_refs, land_refs, send_sems, recv_sems):
        self.part_refs, self.land_refs, self.send_sems, self.recv_sems = part_refs, land_refs, send_sems, recv_sems

    def copies(self):
        x, y, c = _mesh_pos()
        for a in range(len(self.part_refs)):
            for k in (1, 2, 3):
                fx, fy = _CHIP_RELATIONS[k]
                yield pltpu.make_async_remote_copy(
                    src_ref=self.part_refs[a].at[k - 1], dst_ref=self.land_refs[a].at[k - 1],
                    send_sem=self.send_sems.at[3 * a + k - 1], recv_sem=self.recv_sems.at[3 * a + k - 1],
                    device_id=(x ^ fx, y ^ fy, c), device_id_type=MESH)

    def start(self):
        for cp in self.copies():
            cp.start()

    def finish(self):
        for cp in self.copies():
            cp.wait_recv()
        for cp in self.copies():
            cp.wait_send()


def _weight_grad_stage1(name, blk_shape, n_split, operands, in_specs, product, riders=(), slabs=()):
    n_rows, n_cols = blk_shape
    rs = n_rows // n_split
    rc = 32
    n_in, n_ride, n_slab = len(operands), len(riders), len(slabs)
    _, _, c = _mesh_pos()
    order = jnp.stack([_related_block(k, 1 - c) for k in range(4)]
                      + [_related_block(k, c) for k in (1, 2, 3, 0)]).astype(jnp.int32)

    def body(order_ref, *refs):
        ins = refs[:n_in]
        ride_in = refs[n_in:n_in + n_ride]
        slab_in = refs[n_in + n_ride:n_in + n_ride + n_slab]
        n_op = n_in + n_ride + n_slab
        part_ref, own_ref = refs[n_op:n_op + 2]
        ride_out = refs[n_op + 2:n_op + 2 + n_ride]
        gathered = refs[n_op + 2 + n_ride:n_op + 2 + n_ride + n_slab]
        (gbuf, sendbuf, from_sib, send_sems, recv_sems, ride_send, ride_recv,
         slab_send, slab_recv, slab_local) = refs[n_op + 2 + n_ride + n_slab:]
        exchange = _ChipExchange(ride_in, ride_out, ride_send, ride_recv)
        s = pl.program_id(0)
        x, y, c = _mesh_pos()
        me = 4 * x + 2 * y + c
        gather = _BalancedGather(lambda a, px, py, pc: gathered[a].at[4 * px + 2 * py + pc], slab_send, slab_recv, slab_in)
        keep_own = [pltpu.make_async_copy(slab_in[a], gathered[a].at[me], slab_local.at[a]) for a in range(n_slab)]

        def to_sibling(k):
            return pltpu.make_async_remote_copy(
                src_ref=sendbuf.at[k], dst_ref=from_sib.at[k], send_sem=send_sems.at[k], recv_sem=recv_sems.at[k],
                device_id=(x, y, 1 - c), device_id_type=MESH)

        @pl.when(s == 0)
        def _():
            exchange.start()
            for a in range(n_slab):
                gather.start_own(a)
                keep_own[a].start()

        @pl.when(s == 5)
        def _():
            for a in range(n_slab):
                gather.on_neighbour(a, 0)
                gather.on_neighbour(a, 1)

        @pl.when(s == 7)
        def _():
            for a in range(n_slab):
                gather.on_diagonal(a)

        for h in range(n_split):
            gbuf[h * rs:(h + 1) * rs, :] = product(ins, h)

        @pl.when(s < 4)
        def _():
            def narrow(r, carry):
                sendbuf[s, pl.ds(r, rc), :] = gbuf[pl.ds(r, rc), :].astype(MM)
                return carry

            _chunks(n_rows, rc, narrow, 0)
            to_sibling(s).start()

        @pl.when(s >= 4)
        def _():
            k = jnp.where(s == 7, 0, s - 3)
            to_sibling(k).wait_recv()

            @pl.when(s < 7)
            def _():
                def add(r, carry):
                    rows = pl.ds(r, rc)
                    part_ref[0, rows, :] = (gbuf[rows, :] + from_sib[k, rows, :].astype(F32)).astype(MM)
                    return carry

                _chunks(n_rows, rc, add, 0)

            @pl.when(s == 7)
            def _():
                def add(r, carry):
                    rows = pl.ds(r, rc)
                    own_ref[rows, :] = gbuf[rows, :] + from_sib[0, rows, :].astype(F32)
                    return carry

                _chunks(n_rows, rc, add, 0)
                for kk in range(4):
                    to_sibling(kk).wait_send()
                exchange.finish()
                for a in range(n_slab):
                    gather.wait_sibling(a)
                    for j in range(3):
                        gather.wait_passed_on(a, j)
                    gather.wait_sends(a)
                    keep_own[a].wait()

    hbm = pl.BlockSpec(memory_space=pl.ANY)
    grid_spec = pltpu.PrefetchScalarGridSpec(
        num_scalar_prefetch=1, grid=(N_DEV,), in_specs=list(in_specs) + [hbm] * (n_ride + n_slab),
        out_specs=(pl.BlockSpec((1, n_rows, n_cols), lambda s, o: (jnp.clip(s - 4, 0, 2), 0, 0)),
                   pl.BlockSpec((n_rows, n_cols), lambda s, o: (0, 0))) + (hbm,) * (n_ride + n_slab),
        scratch_shapes=[pltpu.VMEM((n_rows, n_cols), F32), pltpu.VMEM((4, n_rows, n_cols), MM),
                        pltpu.VMEM((4, n_rows, n_cols), MM),
                        pltpu.SemaphoreType.DMA((4,)), pltpu.SemaphoreType.DMA((4,)),
                        pltpu.SemaphoreType.DMA((max(3 * n_ride, 1),)), pltpu.SemaphoreType.DMA((max(3 * n_ride, 1),)),
                        pltpu.SemaphoreType.DMA((max(8 * n_slab, 1),)), pltpu.SemaphoreType.DMA((max(8 * n_slab, 1),)),
                        pltpu.SemaphoreType.DMA((max(n_slab, 1),))])
    outs = pl.pallas_call(
        body, name=name, grid_spec=grid_spec,
        out_shape=(jax.ShapeDtypeStruct((3, n_rows, n_cols), MM), jax.ShapeDtypeStruct((n_rows, n_cols), F32))
                  + tuple(jax.ShapeDtypeStruct(p.shape, p.dtype) for p in riders)
                  + tuple(jax.ShapeDtypeStruct((N_DEV,) + a.shape, a.dtype) for a in slabs),
        compiler_params=_params(dimension_semantics=("arbitrary",)),
    )(order, *operands, *riders, *slabs)
    return outs[0], outs[1], outs[2:2 + n_ride], outs[2 + n_ride:]


def _weight_grad_in(xnt, dp, riders, slabs):
    d, t_len = xnt.shape
    cols = dp.shape[1] // N_DEV
    half = d // 2
    return _weight_grad_stage1(
        "weight_grad_in", (d, cols), 2, (xnt, dp),
        [pl.BlockSpec(memory_space=pltpu.VMEM), pl.BlockSpec((t_len, cols), lambda s, o: (0, o[s]))],
        lambda refs, h: _dot(refs[0][h * half:(h + 1) * half, :], refs[1][...]), riders, slabs)


def _weight_grad_out(yt, dhb):
    d2, t_len = yt.shape
    d = dhb.shape[1]
    rows = d2 // N_DEV
    return _weight_grad_stage1(
        "weight_grad_out", (rows, d), 1, (yt, dhb),
        [pl.BlockSpec((rows, t_len), lambda s, o: (o[s], 0)), pl.BlockSpec(memory_space=pltpu.VMEM)],
        lambda refs, h: _dot(refs[0][...], refs[1][...]))


def _update_shard(own, others, w, m, v, name):
    n_rows, n_cols = w.shape
    rb = min(256, n_rows)
    n_other = len(others)

    def body(own_ref, *refs):
        other_refs = refs[:n_other]
        w_ref, m_ref, v_ref, grad_ref, delta_ref, mo_ref, vo_ref = refs[n_other:]
        g = own_ref[...]
        for ref in other_refs:
            for k in range(ref.shape[0] if len(ref.shape) == 3 else 1):
                g = g + (ref[k] if len(ref.shape) == 3 else ref[...]).astype(F32)
        delta, m_new, v_new = _adamw(w_ref[...], g, m_ref[...], v_ref[...])
        grad_ref[...] = g
        delta_ref[...] = delta
        mo_ref[...] = m_new
        vo_ref[...] = v_new

    blk = pl.BlockSpec((rb, n_cols), lambda i: (i, 0))
    stacked = lambda n: pl.BlockSpec((n, rb, n_cols), lambda i: (0, i, 0))
    out = jax.ShapeDtypeStruct((n_rows, n_cols), F32)
    return pl.pallas_call(
        body, name=name, grid=(n_rows // rb,),
        in_specs=[blk] + [stacked(o.shape[0]) if o.ndim == 3 else blk for o in others] + [blk, blk, blk],
        out_specs=(blk, blk, blk, blk), out_shape=(out, out, out, out),
        compiler_params=_params(dimension_semantics=("arbitrary",)),
    )(own, *others, w, m, v)


def _small_update(gat_v, gat_g, ln_tot, vec_w, vec_m, vec_v, gates, convs):
    n_vec = len(vec_w)
    n_heads, hd, _ = gates[0].shape
    tw = gat_g.shape[2]
    s8 = SUBLANES
    per = tw // hd
    n_tiles = n_heads // per
    cc = convs[0].shape[1]
    n_in = 3 + 3 * n_vec + 12

    def body(*refs):
        gv_ref, gg_ref, ln_ref = refs[:3]
        w_refs, m_refs, v_refs = (refs[3 + j * n_vec:3 + (j + 1) * n_vec] for j in range(3))
        gate_refs = refs[3 + 3 * n_vec:3 + 3 * n_vec + 6]
        conv_refs = refs[3 + 3 * n_vec + 6:n_in]
        loss_o = refs[n_in]
        kinds = [refs[n_in + 1 + j * (n_vec + 4):n_in + 1 + (j + 1) * (n_vec + 4)] for j in range(4)]
        tv, tg = refs[n_in + 1 + 4 * (n_vec + 4):]
        x, y, c = _mesh_pos()
        me = 4 * x + 2 * y + c

        def emit(k_out, w, g, m, v):
            delta, m_new, v_new = _adamw(w, g, m, v)
            for ref, val in zip(k_out, (g, delta, m_new, v_new)):
                ref[...] = val

        total = gv_ref[0]
        for dev in range(1, N_DEV):
            total = total + gv_ref[dev]
        tv[...] = total
        tv[SL_LN_G:SL_LN_G + 1, :] = ln_ref[0:1, :]

        def sum_gates(r, carry):
            rows = pl.ds(r, 2 * s8)
            part = gg_ref[0, rows, :].astype(F32)
            for dev in range(1, N_DEV):
                part = part + gg_ref[dev, rows, :].astype(F32)
            tg[rows, :] = part
            return carry

        _chunks(tg.shape[0], 2 * s8, sum_gates, 0)
        loss_o[...] = jnp.broadcast_to(tv[SL_LOSS:SL_LOSS + 1, 0:LANES], loss_o.shape)
        for p in range(n_vec):
            w, g = w_refs[p][...], tv[SL_LN_G + p, :]
            if SL_LN_G + p == SL_LAM:
                g = g * (RG_LRU_C * jax.nn.sigmoid(-w))
            emit([k_out[p] for k_out in kinds], w, g, m_refs[p][...], v_refs[p][...])
        lanes = pl.ds(pl.multiple_of(me * cc, cc), cc)
        for j, (row0, n) in enumerate(((SL_CONV_W, 3), (SL_LRU_W, 4))):
            w_ref, m_ref, v_ref = conv_refs[3 * j:3 * j + 3]
            emit([k_out[n_vec + 2 + j] for k_out in kinds], w_ref[...], tv[row0:row0 + n, lanes], m_ref[...], v_ref[...])
        for mtx in range(2):
            w_ref, m_ref, v_ref = gate_refs[3 * mtx:3 * mtx + 3]
            for k in range(n_tiles):
                tile = tg[(mtx * n_tiles + k) * hd:(mtx * n_tiles + k + 1) * hd, :]
                for a in range(per):
                    head = k * per + a
                    g = tile[:, a * hd:(a + 1) * hd]
                    delta, m_new, v_new = _adamw(w_ref[head], g, m_ref[head], v_ref[head])
                    for k_out, val in zip(kinds, (g, delta, m_new, v_new)):
                        k_out[n_vec + mtx][head] = val

    vm = pl.BlockSpec(memory_space=pltpu.VMEM)
    like = lambda a: jax.ShapeDtypeStruct(a.shape, F32)
    per_kind = tuple(like(a) for a in vec_w) + (like(gates[0]), like(gates[3]), like(convs[0]), like(convs[3]))
    n_out = 1 + 4 * len(per_kind)
    outs = pl.pallas_call(
        body, name="small_update",
        in_specs=[vm] * n_in, out_specs=(vm,) * n_out,
        out_shape=(jax.ShapeDtypeStruct((SUBLANES, LANES), F32),) + per_kind * 4,
        scratch_shapes=[pltpu.VMEM(gat_v.shape[1:], F32), pltpu.VMEM(gat_g.shape[1:], F32)],
        compiler_params=_params(),
    )(gat_v, gat_g, ln_tot, *vec_w, *vec_m, *vec_v, *gates, *convs)
    return outs[0], [outs[1 + j * len(per_kind):1 + (j + 1) * len(per_kind)] for j in range(4)]


def _head_ones(head_dim, tw):
    lane = jnp.arange(tw) // head_dim
    return (lane[:, None] == lane[None, :]).astype(MM)


def kernel(x, ln_g, w_in, conv_w, lru_conv_w, lru_conv_b, w_a, b_a, w_i, b_i, lam, conv_out_g, lru_out_g, w_out, final_g, loss_target, m_ln_g, m_w_in, m_conv_w, m_lru_conv_w, m_lru_conv_b, m_w_a, m_b_a, m_w_i, m_b_i, m_lam, m_conv_out_g, m_lru_out_g, m_w_out, m_final_g, v_ln_g, v_w_in, v_conv_w, v_lru_conv_w, v_lru_conv_b, v_w_a, v_b_a, v_w_i, v_b_i, v_lam, v_conv_out_g, v_lru_out_g, v_w_out, v_final_g):
    _, t_len, d = x.shape
    hd_l = d // N_LRU_HEADS
    tw = min(MXU_TILE, d)
    x2, tgt2 = x[0], loss_target[0]

    small = [ln_g, lru_conv_b, b_a, b_i, lam, conv_out_g, lru_out_g, final_g]
    p, xnt, win_all, wout_all, _, sp, wa_t, wi_t = _gather_project(
        x2, w_in, w_out, conv_w, lru_conv_w, ln_g.reshape(1, d), w_a, w_i, small, min(256, t_len), tw)
    wout_full = wout_all.reshape(N_DEV * w_out.shape[0], d)
    ones_c, ones_l = _head_ones(d // N_CONV_HEADS, tw), _head_ones(hd_l, tw)

    h, dh, dhb, facc, *saved = _forward(x2, tgt2, p, wout_full, wa_t, wi_t, sp, ones_c, ones_l, min(256, t_len))
    dp, yt, slab_v, slab_g = _backward(p, h, dh, saved, facc, wout_full, wa_t, wi_t, sp, ones_c, ones_l, min(256, t_len))
    part_out, own_out, _, _ = _weight_grad_out(yt, dhb)
    part_in, own_in, (chips_out,), (gat_v, gat_g) = _weight_grad_in(xnt, dp, (part_out,), (slab_v, slab_g))
    grad_x, ln_tot, sums_in = _input_grad(dp, win_all, x2, dh, sp, part_in, min(512, t_len))
    gw_in, dw_in, mw_in, vw_in = _update_shard(own_in, sums_in, w_in, m_w_in, v_w_in, "update_w_in")
    gw_out, dw_out, mw_out, vw_out = _update_shard(own_out, (chips_out,), w_out, m_w_out, v_w_out, "update_w_out")

    loss_tile, kinds = _small_update(
        gat_v, gat_g, ln_tot, small,
        [m_ln_g, m_lru_conv_b, m_b_a, m_b_i, m_lam, m_conv_out_g, m_lru_out_g, m_final_g],
        [v_ln_g, v_lru_conv_b, v_b_a, v_b_i, v_lam, v_conv_out_g, v_lru_out_g, v_final_g],
        (w_a, m_w_a, v_w_a, w_i, m_w_i, v_w_i), (conv_w, m_conv_w, v_conv_w, lru_conv_w, m_lru_conv_w, v_lru_conv_w))

    def unpack(kind, big_in, big_out):
        vec, (wa_, wi_, cw_, lw_) = kind[:len(small)], kind[len(small):]
        return [vec[0], big_in, cw_, lw_, vec[1], wa_, vec[2], wi_, vec[3], vec[4], vec[5], vec[6], big_out, vec[7]]

    return (loss_tile[0, 0], grad_x[None], *unpack(kinds[0], gw_in, gw_out), *unpack(kinds[1], dw_in, dw_out),
            *unpack(kinds[2], mw_in, mw_out), *unpack(kinds[3], vw_in, vw_out))
```

```python
import functools

import jax
import jax.numpy as jnp
from jax import lax
from jax.experimental import pallas as pl
from jax.experimental.pallas import tpu as pltpu

F32 = jnp.float32
MM = jnp.bfloat16
MESH = pl.DeviceIdType.MESH

N_DEV = 8
N_CONV_HEADS = 8
N_LRU_HEADS = 16
RG_LRU_C = 8.0
RMS_EPS = 1e-6
ADAM_LR, ADAM_B1, ADAM_B2, ADAM_EPS, ADAM_WD, ADAM_STEP = 0.001, 0.9, 0.999, 1e-08, 0.01, 10
ADAM_BC1 = 1.0 - ADAM_B1 ** ADAM_STEP
ADAM_BC2 = 1.0 - ADAM_B2 ** ADAM_STEP

SUBLANES = 8
LANES = 128
MXU_TILE = 256
VMEM_LIMIT = 56 * 1024 * 1024

SP_LN_G, SP_LRU_B, SP_B_A, SP_B_I, SP_LAM, SP_CONV_G, SP_LRU_G, SP_FINAL_G, SP_CONV_W, SP_LRU_W = 0, 1, 2, 3, 4, 5, 6, 7, 8, 11
SP_ROWS = 16
P_B, P_C, P_XC, P_GC, P_XL, P_GL = 0, 1, 2, 3, 4, 5
A_CONV_G, A_LRU_G, A_LAM, A_B_A, A_B_I, A_CONV_W, A_LRU_W, A_LRU_B = 0, 1, 2, 3, 4, 5, 8, 12
A_GROUPS = 13
SL_LOSS, SL_LN_G, SL_LRU_B, SL_B_A, SL_B_I, SL_LAM, SL_CONV_G, SL_LRU_G, SL_FINAL_G, SL_CONV_W, SL_LRU_W = 0, 1, 2, 3, 4, 5, 6, 7, 8, 16, 24
SL_ROWS = 32


def _params(vmem=True, **kw):
    if vmem:
        kw["vmem_limit_bytes"] = VMEM_LIMIT
    return pltpu.CompilerParams(**kw)


def _dot(a, b):
    return jnp.dot(a, b, preferred_element_type=F32)


def _dot_nt(a, b):
    return lax.dot_general(a, b, (((1,), (1,)), ((), ())), preferred_element_type=F32)


def _head_sums(v, ones_tile):
    tw = ones_tile.shape[0]
    vb = v.astype(MM)
    return jnp.concatenate([_dot(vb[:, k:k + tw], ones_tile) for k in range(0, v.shape[1], tw)], axis=1)


def _head_rstd(v, ones_tile, head_dim):
    return lax.rsqrt(_head_sums(v * v, ones_tile) * (1.0 / head_dim) + RMS_EPS)


def _sigmoid(x):
    return 0.5 * jnp.tanh(0.5 * x) + 0.5


def _lru_input_scale_sq(log_a, a):
    return -jnp.tanh(log_a) * (1.0 + a * a)


def _log_sigmoid(x):
    z = jnp.exp(-jnp.abs(x))
    u = 1.0 + z
    log1p_z = jnp.where(u == 1.0, z, jnp.log(u) * (z / (u - 1.0)))
    return jnp.minimum(x, 0.0) - log1p_z


def _row_iota(d):
    return lax.broadcasted_iota(jnp.int32, (SUBLANES, d), 0)


def _shift_down(cur, prev, s, row):
    return jnp.where(row >= s, pltpu.roll(cur, s, axis=0), pltpu.roll(prev, s, axis=0))


def _shift_up(cur, nxt, s, row):
    k = SUBLANES - s
    return jnp.where(row < k, pltpu.roll(cur, k, axis=0), pltpu.roll(nxt, k, axis=0))


def _scan_fwd(a, b, h_prev, row):
    for s in (1, 2, 4):
        a_s = jnp.where(row >= s, pltpu.roll(a, s, axis=0), 1.0)
        b_s = jnp.where(row >= s, pltpu.roll(b, s, axis=0), 0.0)
        b = a * b_s + b
        a = a * a_s
    return a * h_prev + b


def _scan_bwd(a_next, b, g_next, row):
    a = a_next
    for s in (1, 2, 4):
        k = SUBLANES - s
        a_s = jnp.where(row < k, pltpu.roll(a, k, axis=0), 1.0)
        b_s = jnp.where(row < k, pltpu.roll(b, k, axis=0), 0.0)
        b = a * b_s + b
        a = a * a_s
    return a * g_next + b


def _bcast_row(v, r):
    return jnp.broadcast_to(v[r:r + 1, :], v.shape)


def _chunks(n_rows, rc, body, init, reverse=False):
    n = n_rows // rc

    def step(i, carry):
        j = (n - 1 - i) if reverse else i
        return body(pl.multiple_of(j * rc, rc), carry)

    return lax.fori_loop(0, n, step, init)


def _adamw(w, g, m, v):
    m = ADAM_B1 * m + (1.0 - ADAM_B1) * g
    v = ADAM_B2 * v + (1.0 - ADAM_B2) * (g * g)
    m_hat = m / ADAM_BC1
    v_hat = v / ADAM_BC2
    delta = -ADAM_LR * (m_hat / (jnp.sqrt(v_hat) + ADAM_EPS) + ADAM_WD * w)
    return delta, m, v


def _mesh_pos():
    return lax.axis_index("x"), lax.axis_index("y"), lax.axis_index("c")


class _Gather:
    def __init__(self, blocks_of, send_sems, recv_sems, own_src=None):
        x, y, c = _mesh_pos()
        self.c = c
        self.me, self.sibling = (x, y, c), (x, y, 1 - c)
        self.chips = [(1 - x, y), (x, 1 - y), (1 - x, 1 - y)]
        self.blocks_of, self.send_sems, self.recv_sems = blocks_of, send_sems, recv_sems
        self.own_src = own_src

    def copy(self, a, k, block, to):
        src = self.blocks_of(a, *block)
        if block is self.me and self.own_src is not None:
            src = self.own_src[a]
        return pltpu.make_async_remote_copy(
            src_ref=src, dst_ref=self.blocks_of(a, *block),
            send_sem=self.send_sems.at[a * 7 + k], recv_sem=self.recv_sems.at[a * 7 + k],
            device_id=to, device_id_type=MESH)

    def start_own(self, a):
        self.copy(a, 0, self.me, self.sibling).start()
        for j, chip in enumerate(self.chips):
            self.copy(a, 1 + j, self.me, (*chip, self.c)).start()

    def wait_sibling(self, a):
        self.copy(a, 0, self.sibling, self.me).wait_recv()

    def wait_chip_and_pass_on(self, a, j):
        block = (*self.chips[j], self.c)
        self.copy(a, 1 + j, block, self.me).wait_recv()
        self.copy(a, 4 + j, block, self.sibling).start()

    def wait_passed_on(self, a, j):
        self.copy(a, 4 + j, (*self.chips[j], 1 - self.c), self.me).wait_recv()

    def wait_sends(self, a):
        self.copy(a, 0, self.me, self.sibling).wait_send()
        for j, chip in enumerate(self.chips):
            self.copy(a, 1 + j, self.me, (*chip, self.c)).wait_send()
            self.copy(a, 4 + j, (*chip, self.c), self.sibling).wait_send()

    def finish(self, a):
        for j in range(3):
            self.wait_chip_and_pass_on(a, j)
        self.wait_sibling(a)
        for j in range(3):
            self.wait_passed_on(a, j)
        self.wait_sends(a)


class _BalancedGather:
    def __init__(self, slot, send_sems, recv_sems, own_src):
        x, y, c = _mesh_pos()
        self.c = c
        self.me, self.sibling = (x, y, c), (x, y, 1 - c)
        self.chips = [(1 - x, y), (x, 1 - y), (1 - x, 1 - y)]
        self.slot, self.send_sems, self.recv_sems, self.own_src = slot, send_sems, recv_sems, own_src

    def half(self, a, block, which):
        ref = self.slot(a, *block)
        n = ref.shape[0] // 2
        return ref.at[pl.ds(which * n, n)]

    def copy(self, a, k, src, dst, to):
        return pltpu.make_async_remote_copy(
            src_ref=src, dst_ref=dst, send_sem=self.send_sems.at[a * 8 + k], recv_sem=self.recv_sems.at[a * 8 + k],
            device_id=to, device_id_type=MESH)

    def whole(self, a, k, block, to):
        src = self.own_src[a] if block is self.me else self.slot(a, *block)
        return self.copy(a, k, src, self.slot(a, *block), to)

    def halved(self, a, k, block, which, to):
        return self.copy(a, k, self.half(a, block, which), self.half(a, block, which), to)

    def on(self, chip):
        return (*self.chips[chip], self.c)

    def start_own(self, a):
        self.whole(a, 0, self.me, self.sibling).start()
        self.whole(a, 1, self.me, self.on(0)).start()
        self.whole(a, 2, self.me, self.on(1)).start()

    def wait_sibling(self, a):
        self.whole(a, 0, self.sibling, self.me).wait_recv()

    def on_neighbour(self, a, j):
        self.whole(a, 1 + j, self.on(j), self.me).wait_recv()
        self.halved(a, 3 + j, self.on(j), j, self.on(1 - j)).start()
        self.whole(a, 5 + j, self.on(j), self.sibling).start()

    def on_diagonal(self, a):
        self.halved(a, 3, self.on(2), 0, self.me).wait_recv()
        self.halved(a, 4, self.on(2), 1, self.me).wait_recv()
        self.whole(a, 7, self.on(2), self.sibling).start()

    def wait_passed_on(self, a, j):
        self.whole(a, 5 + j, (*self.chips[j], 1 - self.c), self.me).wait_recv()

    def wait_sends(self, a):
        self.whole(a, 0, self.me, self.sibling).wait_send()
        for j in range(2):
            self.whole(a, 1 + j, self.me, self.on(j)).wait_send()
            self.halved(a, 3 + j, self.on(j), j, self.on(1 - j)).wait_send()
        for j in range(3):
            self.whole(a, 5 + j, self.on(j), self.sibling).wait_send()


def _block_order():
    x, y, c = _mesh_pos()
    idx = lambda chip, core: 4 * chip[0] + 2 * chip[1] + core
    own, xn, yn, dg = (x, y), (1 - x, y), (x, 1 - y), (1 - x, 1 - y)
    order = [(idx(own, c), 0), (idx(own, c), 1), (idx(own, 1 - c), 0), (idx(own, 1 - c), 1)]
    for h in (0, 1):
        order += [(idx(xn, c), h), (idx(xn, 1 - c), h), (idx(yn, c), h), (idx(yn, 1 - c), h)]
    order += [(idx(dg, c), 0), (idx(dg, 1 - c), 0), (idx(dg, c), 1), (idx(dg, 1 - c), 1)]
    return jnp.stack([2 * b + h for b, h in order]).astype(jnp.int32)


def _gather_project(x, w_in, w_out, conv_w, lru_conv_w, ln_g, w_a, w_i, vecs, tb, tw):
    t_len, d = x.shape
    nb = t_len // tb
    cols = w_in.shape[1]
    hc = cols // 2
    n_steps = 2 * N_DEV
    mc = min(512, t_len)
    conv_pack = jax.ShapeDtypeStruct((SUBLANES, conv_w.shape[1]), F32)
    srcs = (w_in, w_out, conv_pack)
    dts = (MM, MM, F32)
    n_vec = len(vecs)
    n_heads, hd, _ = w_a.shape
    per = tw // hd

    def body(order_ref, x_ref, win_ref, wout_ref, cw_ref, lw_ref, lng_ref, wa_ref, wi_ref, *refs):
        vec_refs = refs[:n_vec]
        (p_ref, xnt_ref, win_all, wout_all, cp_all, sp_ref, wat_ref, wit_ref,
         xnb, wall, st_out, st_cp, cp_vm, send_sems, recv_sems, cp_send, cp_recv, local_sems) = refs[n_vec:]
        i = pl.program_id(0)
        x_, y_, c_ = _mesh_pos()
        me = 4 * x_ + 2 * y_ + c_

        def slot(a, px, py, pc):
            dev = 4 * px + 2 * py + pc
            return wall.at[a, dev] if a < 2 else wout_all.at[dev]

        stages = (wall.at[0, me], wall.at[1, me], st_out)
        gather = _BalancedGather(slot, send_sems, recv_sems, stages)
        small = _Gather(lambda a, px, py, pc: cp_all.at[4 * px + 2 * py + pc], cp_send, cp_recv, own_src=[st_cp])
        keep_own = [pltpu.make_async_copy(wall.at[h, me], win_all.at[me, :, h * hc:(h + 1) * hc], local_sems.at[h])
                    for h in range(2)]
        keep_own += [pltpu.make_async_copy(st_out, wout_all.at[me], local_sems.at[2]),
                     pltpu.make_async_copy(st_cp, cp_all.at[me], local_sems.at[3])]

        def keep(k):
            blk, h = order_ref[k] // 2, order_ref[k] % 2
            return pltpu.make_async_copy(
                wall.at[h, blk], win_all.at[blk, :, pl.ds(pl.multiple_of(h * hc, hc), hc)], local_sems.at[2 + k])

        @pl.when(i == 0)
        def _():
            rc = 32

            def cast(r, carry):
                rows = pl.ds(r, rc)
                for h in range(2):
                    wall[h, me, rows, :] = win_ref[rows, h * hc:(h + 1) * hc].astype(MM)
                return carry

            _chunks(d, rc, cast, 0)

            def cast_out(r, carry):
                st_out[pl.ds(r, rc), :] = wout_ref[pl.ds(r, rc), :].astype(MM)
                return carry

            _chunks(wout_ref.shape[0], rc, cast_out, 0)
            n_cw, n_lw = cw_ref.shape[0], lw_ref.shape[0]
            st_cp[...] = jnp.zeros_like(st_cp)
            st_cp[0:n_cw, :] = cw_ref[...]
            st_cp[n_cw:n_cw + n_lw, :] = lw_ref[...]
            gather.start_own(0)
            gather.start_own(1)
            for cp in keep_own:
                cp.start()

        @pl.when(i < nb)
        def _():
            xv = x_ref[...]
            r0 = lax.rsqrt(jnp.mean(xv * xv, axis=-1, keepdims=True) + RMS_EPS)
            xn = xv * r0 * lng_ref[...]
            xnb[pl.ds(pl.multiple_of(i * tb, tb), tb), :] = xn.astype(MM)
            xnt_ref[...] = xn.T.astype(MM)

        before = {2: [lambda: gather.wait_sibling(0)], 3: [lambda: gather.wait_sibling(1)],
                  4: [lambda: gather.on_neighbour(0, 0), lambda: gather.on_neighbour(0, 1)],
                  5: [lambda: gather.wait_passed_on(0, 0)], 7: [lambda: gather.wait_passed_on(0, 1)],
                  8: [lambda: gather.on_neighbour(1, 0), lambda: gather.on_neighbour(1, 1),
                      lambda: gather.start_own(2), lambda: small.start_own(0)],
                  9: [lambda: gather.wait_passed_on(1, 0)], 11: [lambda: gather.wait_passed_on(1, 1)],
                  12: [lambda: gather.on_diagonal(0), lambda: gather.on_neighbour(2, 0), lambda: gather.on_neighbour(2, 1)],
                  13: [lambda: gather.wait_passed_on(0, 2)], 14: [lambda: gather.on_diagonal(1)],
                  15: [lambda: gather.wait_passed_on(1, 2), lambda: gather.on_diagonal(2)]}

        for k in range(n_steps):
            @pl.when(i == nb + k)
            def _(k=k):
                for action in before.get(k, []):
                    action()
                blk, h = order_ref[k] // 2, order_ref[k] % 2
                if k >= 2:
                    keep(k).start()

                def project(r, carry):
                    rows = pl.ds(r, mc)
                    p_ref[rows, :] = _dot(xnb[rows, :], wall[h, blk]).astype(MM)
                    return carry

                _chunks(t_len, mc, project, 0)
                if k == n_steps - 1:
                    gather.wait_sends(0)
                    gather.wait_sends(1)
                    gather.wait_sibling(2)
                    for j in range(3):
                        gather.wait_passed_on(2, j)
                    gather.wait_sends(2)
                    small.finish(0)
                    for cp in keep_own + [keep(kk) for kk in range(2, n_steps)]:
                        cp.wait()
                    load = pltpu.make_async_copy(cp_all, cp_vm, local_sems.at[n_steps + 2])
                    load.start()
                    load.wait()
                    for r, ref in enumerate(vec_refs):
                        sp_ref[r, :] = ref[...]
                    sp_ref[n_vec:n_vec + SUBLANES, :] = jnp.concatenate([cp_vm[dev] for dev in range(N_DEV)], axis=1)
                    for src, dst in ((wa_ref, wat_ref), (wi_ref, wit_ref)):
                        dst[...] = jnp.zeros_like(dst)
                        for head in range(n_heads):
                            lo = (head % per) * hd
                            dst[head // per, lo:lo + hd, lo:lo + hd] = src[head].astype(MM)

    vm = pl.BlockSpec(memory_space=pltpu.VMEM)
    hbm = pl.BlockSpec(memory_space=pl.ANY)
    grid_spec = pltpu.PrefetchScalarGridSpec(
        num_scalar_prefetch=1, grid=(nb + n_steps,),
        in_specs=[pl.BlockSpec((tb, d), lambda i, o: (jnp.minimum(i, nb - 1), 0))] + [vm] * (7 + n_vec),
        out_specs=(pl.BlockSpec((t_len, hc), lambda i, o: (0, o[jnp.maximum(i - nb, 0)])),
                   pl.BlockSpec((d, tb), lambda i, o: (0, jnp.minimum(i, nb - 1))), hbm, hbm, hbm,
                   pl.BlockSpec((SP_ROWS, d), lambda i, o: (0, 0)),
                   pl.BlockSpec((n_heads // per, tw, tw), lambda i, o: (0, 0, 0)),
                   pl.BlockSpec((n_heads // per, tw, tw), lambda i, o: (0, 0, 0))),
        scratch_shapes=[pltpu.VMEM((t_len, d), MM), pltpu.VMEM((2, N_DEV, d, hc), MM),
                        pltpu.VMEM(w_out.shape, MM), pltpu.VMEM(conv_pack.shape, F32),
                        pltpu.VMEM((N_DEV,) + conv_pack.shape, F32),
                        pltpu.SemaphoreType.DMA((24,)), pltpu.SemaphoreType.DMA((24,)),
                        pltpu.SemaphoreType.DMA((7,)), pltpu.SemaphoreType.DMA((7,)), pltpu.SemaphoreType.DMA((n_steps + 3,))])
    return pl.pallas_call(
        body, name="gather_project", grid_spec=grid_spec,
        out_shape=(jax.ShapeDtypeStruct((t_len, N_DEV * cols), MM),
                   jax.ShapeDtypeStruct((d, t_len), MM))
                  + tuple(jax.ShapeDtypeStruct((N_DEV,) + s.shape, dt) for s, dt in zip(srcs, dts))
                  + (jax.ShapeDtypeStruct((SP_ROWS, d), F32),)
                  + (jax.ShapeDtypeStruct((n_heads // per, tw, tw), MM),) * 2,
        compiler_params=_params(dimension_semantics=("arbitrary",)),
    )(_block_order(), x, w_in, w_out, conv_w, lru_conv_w, ln_g, w_a, w_i, *vecs)


def _forward(x, tgt, p, wout, wa_t, wi_t, sp, ones_c, ones_l, tb):
    t_len, d = x.shape
    nb = t_len // tb
    n_tiles, tw = wa_t.shape[0], wa_t.shape[1]
    hd_c, hd_l = d // N_CONV_HEADS, d // N_LRU_HEADS
    s8 = SUBLANES

    def body(x_ref, tgt_ref, p_ref, wout_ref, wa_ref, wi_ref, sp_ref, oc_ref, ol_ref,
             h_ref, dh_ref, dhb_ref, acc_ref, yc, czs, u, pa, pi,
             rcf, rlf, ybuf, tail_z, tail_xl, hcar):
        i = pl.program_id(0)
        row = _row_iota(d)

        @pl.when(i == 0)
        def _():
            tail_z[...] = jnp.zeros_like(tail_z)
            tail_xl[...] = jnp.zeros_like(tail_xl)
            hcar[...] = jnp.zeros_like(hcar)
            acc_ref[...] = jnp.zeros_like(acc_ref)

        def spr(r):
            return sp_ref[r:r + 1, :]

        def proj(rows, seg):
            return p_ref[rows, seg * d:(seg + 1) * d].astype(F32)

        w0, w1, w2 = spr(SP_CONV_W), spr(SP_CONV_W + 1), spr(SP_CONV_W + 2)
        l0, l1, l2, l3 = spr(SP_LRU_W), spr(SP_LRU_W + 1), spr(SP_LRU_W + 2), spr(SP_LRU_W + 3)
        lb = spr(SP_LRU_B)

        def convs(r, carry):
            zp, xp = carry
            rows16 = pl.ds(r, 2 * s8)
            bg16, xl16 = proj(rows16, P_B), proj(rows16, P_XL)
            z16 = proj(rows16, P_C) * proj(rows16, P_XC)
            for j in range(2):
                rows, sub = pl.ds(r + j * s8, s8), slice(j * s8, (j + 1) * s8)
                z, xl = z16[sub], xl16[sub]
                cz = w0 * _shift_down(z, zp, 2, row) + w1 * _shift_down(z, zp, 1, row) + w2 * z
                czs[rows, :] = cz
                yc[rows, :] = bg16[sub] * cz
                u[rows, :] = (l0 * _shift_down(xl, xp, 3, row) + l1 * _shift_down(xl, xp, 2, row)
                              + l2 * _shift_down(xl, xp, 1, row) + l3 * xl + lb)
                zp, xp = z, xl
            return zp, xp

        z_last, xl_last = _chunks(tb, 2 * s8, convs, (tail_z[...], tail_xl[...]))
        tail_z[...] = z_last
        tail_xl[...] = xl_last

        ub = u[...].astype(MM)
        for k in range(n_tiles):
            sl = slice(k * tw, (k + 1) * tw)
            pa[:, sl] = _dot(ub[:, sl], wa_ref[k])
            pi[:, sl] = _dot(ub[:, sl], wi_ref[k])
        rcf[...] = _head_rstd(yc[...], oc_ref[...], hd_c)

        c8 = RG_LRU_C * _log_sigmoid(spr(SP_LAM))
        b_a, b_i = spr(SP_B_A), spr(SP_B_I)

        def lru(r, hp):
            rows = pl.ds(r, SUBLANES)
            ra = _sigmoid(pa[rows, :] + b_a)
            ii = _sigmoid(pi[rows, :] + b_i)
            pa[rows, :] = ra
            pi[rows, :] = ii
            la = ra * c8
            a = jnp.exp(la)
            mult = jnp.sqrt(_lru_input_scale_sq(la, a))
            h = _scan_fwd(a, mult * (ii * u[rows, :]), hp, row)
            h_ref[rows, :] = h
            return _bcast_row(h, SUBLANES - 1)

        hcar[...] = _chunks(tb, SUBLANES, lru, hcar[...])
        rlf[...] = _head_rstd(h_ref[...], ol_ref[...], hd_l)

        g_c, g_l = spr(SP_CONV_G), spr(SP_LRU_G)

        def gate(r, carry):
            rows = pl.ds(r, 2 * s8)
            gc, gl = proj(rows, P_GC), proj(rows, P_GL)
            ybuf[rows, 0:d] = (yc[rows, :] * rcf[rows, :] * g_c * (gc * _sigmoid(gc))).astype(MM)
            ybuf[rows, d:2 * d] = (h_ref[rows, :] * rlf[rows, :] * g_l * (gl * _sigmoid(gl))).astype(MM)
            return carry

        _chunks(tb, 2 * s8, gate, 0)

        hres = x_ref[...] + _dot(ybuf[...], wout_ref[...])
        rf = lax.rsqrt(jnp.mean(hres * hres, axis=-1, keepdims=True) + RMS_EPS)
        hn = hres * rf
        fg = spr(SP_FINAL_G)
        err = hn * fg - tgt_ref[...]
        dout = err * (1.0 / d)
        acc_ref[0:SUBLANES, :] += (err * err).reshape(tb // SUBLANES, SUBLANES, d).sum(axis=0)
        acc_ref[SUBLANES:2 * SUBLANES, :] += (dout * hn).reshape(tb // SUBLANES, SUBLANES, d).sum(axis=0)
        gd = dout * fg
        dhres = rf * (gd - hn * jnp.mean(gd * hn, axis=-1, keepdims=True))
        dh_ref[...] = dhres
        dhb_ref[...] = dhres.astype(MM)

    vm = pl.BlockSpec(memory_space=pltpu.VMEM)
    blk = lambda w: pl.BlockSpec((tb, w), lambda i: (i, 0))
    buf = pltpu.VMEM((tb, d), F32)
    car = pltpu.VMEM((SUBLANES, d), F32)
    return pl.pallas_call(
        body, name="forward", grid=(nb,),
        in_specs=[blk(d), blk(d), blk(6 * d), vm, vm, vm, vm, vm, vm],
        out_specs=(blk(d), blk(d), blk(d), pl.BlockSpec((2 * SUBLANES, d), lambda i: (0, 0))) + (blk(d),) * 5,
        out_shape=(jax.ShapeDtypeStruct((t_len, d), F32),
                   jax.ShapeDtypeStruct((t_len, d), F32),
                   jax.ShapeDtypeStruct((t_len, d), MM),
                   jax.ShapeDtypeStruct((2 * SUBLANES, d), F32))
                  + (jax.ShapeDtypeStruct((t_len, d), F32),) * 5,
        scratch_shapes=[buf] * 2 + [pltpu.VMEM((tb, 2 * d), MM), car, car, car],
        compiler_params=_params(dimension_semantics=("arbitrary",)),
    )(x, tgt, p, wout, wa_t, wi_t, sp, ones_c, ones_l)


def _backward(p, h, dh, saved, facc, wout, wa_t, wi_t, sp, ones_c, ones_l, tb):
    t_len, d = h.shape
    nb = t_len // tb
    n_tiles, tw = wa_t.shape[0], wa_t.shape[1]
    hd_c, hd_l = d // N_CONV_HEADS, d // N_LRU_HEADS
    g_rows = 2 * n_tiles * hd_l
    s8 = SUBLANES

    def body(p_ref, h_ref, hhalo_ref, dh_ref, yc, czs, u, ra_ref, ii_ref, facc_ref,
             wout_ref, wa_ref, wi_ref, sp_ref, oc_ref, ol_ref,
             dp_ref, yt_ref, slab_v, slab_g,
             hh, dy, ybuf, rcf, rlf, qc, ql, dyc_hat, dyl_hat, dpa, dpi, du, gwa_ref, gwi_ref, acc_ref,
             car_dcz, car_a, car_g, car_du):
        i = pl.program_id(0)
        blk_idx = nb - 1 - i
        row = _row_iota(d)

        @pl.when(i == 0)
        def _():
            for ref in (car_dcz, car_a, car_g, car_du, gwa_ref, gwi_ref, acc_ref):
                ref[...] = jnp.zeros_like(ref)

        def spr(r):
            return sp_ref[r:r + 1, :]

        def proj(rows, seg):
            return p_ref[rows, seg * d:(seg + 1) * d].astype(F32)

        def put(rows, seg, halves):
            dp_ref[rows, seg * d:(seg + 1) * d] = jnp.concatenate(halves, axis=0).astype(MM)

        def acc_add(group, val):
            acc_ref[group * s8:(group + 1) * s8, :] += val

        live = jnp.where(blk_idx > 0, 1.0, 0.0).astype(F32)
        hh[0:s8, :] = hhalo_ref[...] * live
        hh[s8:, :] = h_ref[...]

        dy[...] = _dot_nt(dh_ref[...].astype(MM), wout_ref[...])

        w0, w1, w2 = spr(SP_CONV_W), spr(SP_CONV_W + 1), spr(SP_CONV_W + 2)
        l0, l1, l2, l3 = spr(SP_LRU_W), spr(SP_LRU_W + 1), spr(SP_LRU_W + 2), spr(SP_LRU_W + 3)

        rcf[...] = _head_rstd(yc[...], oc_ref[...], hd_c)
        rlf[...] = _head_rstd(h_ref[...], ol_ref[...], hd_l)

        g_c, g_l = spr(SP_CONV_G), spr(SP_LRU_G)

        def gates(r, carry):
            rows = pl.ds(r, 2 * s8)
            for (seg, off_y, src, rstd, gain, q, dhat, grp) in (
                    (P_GC, 0, yc, rcf, g_c, qc, dyc_hat, A_CONV_G),
                    (P_GL, d, h_ref, rlf, g_l, ql, dyl_hat, A_LRU_G)):
                gt = proj(rows, seg)
                sg = _sigmoid(gt)
                silu = gt * sg
                yhat = src[rows, :] * rstd[rows, :]
                nrm = yhat * gain
                ybuf[rows, off_y:off_y + d] = nrm * silu
                dout = dy[rows, off_y:off_y + d]
                dnrm = dout * silu
                dp_ref[rows, seg * d:(seg + 1) * d] = (dout * nrm * (sg * (1.0 + gt * (1.0 - sg)))).astype(MM)
                dg = dnrm * yhat
                acc_add(grp, dg[0:s8] + dg[s8:])
                dh_ = dnrm * gain
                dhat[rows, :] = dh_
                q[rows, :] = dh_ * yhat
            return carry

        _chunks(tb, 2 * s8, gates, 0)

        qc[...] = _head_sums(qc[...], oc_ref[...]) * (1.0 / hd_c)
        ql[...] = _head_sums(ql[...], ol_ref[...]) * (1.0 / hd_l)
        yt_ref[...] = ybuf[...].T.astype(MM)

        c8 = RG_LRU_C * _log_sigmoid(spr(SP_LAM))

        def conv_mixer(r, dcz_n):
            rows16 = pl.ds(r, 2 * s8)
            bg16, cg16, xc16 = proj(rows16, P_B), proj(rows16, P_C), proj(rows16, P_XC)
            z16 = cg16 * xc16
            d_b, d_c, d_x = [None, None], [None, None], [None, None]
            for j in (1, 0):
                rows, sub = pl.ds(r + j * s8, s8), slice(j * s8, (j + 1) * s8)
                rstd = rcf[rows, :]
                yhat = yc[rows, :] * rstd
                dyc = rstd * (dyc_hat[rows, :] - yhat * qc[rows, :])
                d_b[j] = dyc * czs[rows, :]
                dcz = dyc * bg16[sub]
                up1, up2 = _shift_up(dcz, dcz_n, 1, row), _shift_up(dcz, dcz_n, 2, row)
                dz = w2 * dcz + w1 * up1 + w0 * up2
                d_c[j] = dz * xc16[sub]
                d_x[j] = dz * cg16[sub]
                z = z16[sub]
                acc_add(A_CONV_W, up2 * z)
                acc_add(A_CONV_W + 1, up1 * z)
                acc_add(A_CONV_W + 2, dcz * z)
                dcz_n = dcz
            put(rows16, P_B, d_b)
            put(rows16, P_C, d_c)
            put(rows16, P_XC, d_x)
            return dcz_n

        car_dcz[...] = _chunks(tb, 2 * s8, conv_mixer, car_dcz[...], reverse=True)

        def lru_mixer(r, carry):
            a_n, g_n = carry
            for j in (1, 0):
                rows = pl.ds(r + j * s8, s8)
                rstd = rlf[rows, :]
                hcur = hh[pl.ds(r + (j + 1) * s8, s8), :]
                hhat = hcur * rstd
                dh_out = rstd * (dyl_hat[rows, :] - hhat * ql[rows, :])
                ra = ra_ref[rows, :]
                la = ra * c8
                a = jnp.exp(la)
                g = _scan_bwd(_shift_up(a, a_n, 1, row), dh_out, g_n, row)
                da = g * _shift_down(hcur, hh[pl.ds(r + j * s8, s8), :], 1, row)
                ii = ii_ref[rows, :]
                uu = u[rows, :]
                mult_sq = _lru_input_scale_sq(la, a)
                inv_mult = lax.rsqrt(mult_sq)
                dmult = g * (ii * uu)
                ds = g * (mult_sq * inv_mult)
                dla = a * (da - dmult * a * inv_mult)
                acc_add(A_LAM, dla * ra)
                dpa_ = dla * c8 * ra * (1.0 - ra)
                dpi_ = ds * uu * ii * (1.0 - ii)
                acc_add(A_B_A, dpa_)
                acc_add(A_B_I, dpi_)
                dpa[rows, :] = dpa_
                dpi[rows, :] = dpi_
                du[rows, :] = ds * ii
                a_n, g_n = a, _bcast_row(g, 0)
            return a_n, g_n

        a_f, g_f = _chunks(tb, 2 * s8, lru_mixer, (car_a[...], car_g[...]), reverse=True)
        car_a[...] = a_f
        car_g[...] = g_f

        dpab = dpa[...].astype(MM)
        dpib = dpi[...].astype(MM)
        for k in range(n_tiles):
            sl = slice(k * tw, (k + 1) * tw)
            du[:, sl] += _dot_nt(dpab[:, sl], wa_ref[k]) + _dot_nt(dpib[:, sl], wi_ref[k])
            ut = u[:, sl].T.astype(MM)
            gwa_ref[k] += _dot(ut, dpab[:, sl])
            gwi_ref[k] += _dot(ut, dpib[:, sl])

        def lru_conv(r, du_n):
            rows16 = pl.ds(r, 2 * s8)
            xl16 = proj(rows16, P_XL)
            d_xl = [None, None]
            for j in (1, 0):
                rows, sub = pl.ds(r + j * s8, s8), slice(j * s8, (j + 1) * s8)
                dut = du[rows, :]
                up1, up2, up3 = (_shift_up(dut, du_n, s, row) for s in (1, 2, 3))
                d_xl[j] = l3 * dut + l2 * up1 + l1 * up2 + l0 * up3
                xl = xl16[sub]
                acc_add(A_LRU_W, up3 * xl)
                acc_add(A_LRU_W + 1, up2 * xl)
                acc_add(A_LRU_W + 2, up1 * xl)
                acc_add(A_LRU_W + 3, dut * xl)
                acc_add(A_LRU_B, dut)
                du_n = dut
            put(rows16, P_XL, d_xl)
            return du_n

        car_du[...] = _chunks(tb, 2 * s8, lru_conv, car_du[...], reverse=True)

        @pl.when(i == nb - 1)
        def _():
            def rowsum(ref, group):
                return jnp.sum(ref[group * s8:(group + 1) * s8, :], axis=0, keepdims=True)

            slab_v[...] = jnp.zeros_like(slab_v)
            loss = jnp.sum(rowsum(facc_ref, 0), axis=1, keepdims=True) * (0.5 / d)
            rows = {SL_LOSS: jnp.broadcast_to(loss, (1, d)), SL_FINAL_G: rowsum(facc_ref, 1),
                    SL_LRU_B: rowsum(acc_ref, A_LRU_B), SL_B_A: rowsum(acc_ref, A_B_A), SL_B_I: rowsum(acc_ref, A_B_I),
                    SL_LAM: rowsum(acc_ref, A_LAM), SL_CONV_G: rowsum(acc_ref, A_CONV_G), SL_LRU_G: rowsum(acc_ref, A_LRU_G)}
            for k in range(3):
                rows[SL_CONV_W + k] = rowsum(acc_ref, A_CONV_W + k)
            for k in range(4):
                rows[SL_LRU_W + k] = rowsum(acc_ref, A_LRU_W + k)
            for r, val in rows.items():
                slab_v[r:r + 1, :] = val
            head_of_lane = lax.broadcasted_iota(jnp.int32, (hd_l, tw), 1) // hd_l
            for mtx, g_ref in enumerate((gwa_ref, gwi_ref)):
                for k in range(n_tiles):
                    packed = jnp.zeros((hd_l, tw), F32)
                    for a in range(tw // hd_l):
                        packed = jnp.where(head_of_lane == a, g_ref[k, a * hd_l:(a + 1) * hd_l, :], packed)
                    slab_g[(mtx * n_tiles + k) * hd_l:(mtx * n_tiles + k + 1) * hd_l, :] = packed.astype(MM)

    vm = pl.BlockSpec(memory_space=pltpu.VMEM)
    rev = lambda w: pl.BlockSpec((tb, w), lambda i: (nb - 1 - i, 0))
    halo = lambda rows, w: pl.BlockSpec((rows, w), lambda i: (jnp.maximum((nb - 1 - i) * (tb // rows) - 1, 0), 0))
    const = lambda shape: pl.BlockSpec(shape, lambda i: (0,) * len(shape))
    buf = lambda w: pltpu.VMEM((tb, w), F32)
    car = pltpu.VMEM((SUBLANES, d), F32)
    return pl.pallas_call(
        body, name="backward", grid=(nb,),
        in_specs=[rev(6 * d), rev(d), halo(SUBLANES, d), rev(d)] + [rev(d)] * 5 + [vm, vm, vm, vm, vm, vm, vm],
        out_specs=(rev(6 * d), pl.BlockSpec((2 * d, tb), lambda i: (0, nb - 1 - i)),
                   const((SL_ROWS, d)), const((g_rows, tw))),
        out_shape=(jax.ShapeDtypeStruct((t_len, 6 * d), MM),
                   jax.ShapeDtypeStruct((2 * d, t_len), MM),
                   jax.ShapeDtypeStruct((SL_ROWS, d), F32),
                   jax.ShapeDtypeStruct((g_rows, tw), MM)),
        scratch_shapes=[pltpu.VMEM((SUBLANES + tb, d), F32), buf(2 * d), buf(2 * d)] + [buf(d)] * 9
                       + [pltpu.VMEM((n_tiles, tw, tw), F32), pltpu.VMEM((n_tiles, tw, tw), F32),
                          pltpu.VMEM((A_GROUPS * SUBLANES, d), F32), car, car, car, car],
        compiler_params=_params(dimension_semantics=("arbitrary",)),
    )(p, h, h, dh, *saved, facc, wout, wa_t, wi_t, sp, ones_c, ones_l)


def _input_grad(dp, win_all, x, dh, sp, part, tb):
    t_len, d = x.shape
    nb = t_len // tb
    cols = win_all.shape[2]
    mid = min(nb - 1, (5 * nb) // 8)
    rc = 32

    def body(dp_ref, win_ref, x_ref, dh_ref, sp_ref, part_ref, gx_ref, ln_ref, direct, passing, relayed,
             send_sems, recv_sems, local_sems, acc_ref, ln_all, ln_send, ln_recv, mine, theirs):
        i = pl.program_id(0)
        x_, y_, c_ = _mesh_pos()
        first, second = 1 - c_, c_
        nbr1 = (x_ ^ c_, y_ ^ (1 - c_), c_)
        nbr2 = (x_ ^ (1 - c_), y_ ^ c_, c_)

        def remote(src, dst, k, to):
            return pltpu.make_async_remote_copy(src_ref=src, dst_ref=dst, send_sem=send_sems.at[k], recv_sem=recv_sems.at[k],
                                                device_id=to, device_id_type=MESH)

        to_first = [remote(part_ref.at[first], direct, 0, nbr1), remote(part_ref.at[2], passing, 1, nbr1)]
        to_second = remote(theirs, relayed, 2, nbr2)

        @pl.when(i == 0)
        def _():
            acc_ref[...] = jnp.zeros_like(acc_ref)
            for cp in to_first:
                cp.start()

        dxn = _dot_nt(dp_ref[:, 0:cols], win_ref[0])
        for j in range(1, N_DEV):
            dxn += _dot_nt(dp_ref[:, j * cols:(j + 1) * cols], win_ref[j])
        xv = x_ref[...]
        r0 = lax.rsqrt(jnp.mean(xv * xv, axis=-1, keepdims=True) + RMS_EPS)
        xhat = xv * r0
        acc_ref[...] += (dxn * xhat).reshape(tb // SUBLANES, SUBLANES, d).sum(axis=0)
        dxh = dxn * sp_ref[SP_LN_G:SP_LN_G + 1, :]
        gx_ref[...] = dh_ref[...] + r0 * (dxh - xhat * jnp.mean(dxh * xhat, axis=-1, keepdims=True))

        @pl.when(i == mid)
        def _():
            to_first[1].wait_recv()
            loads = [pltpu.make_async_copy(part_ref.at[second], mine, local_sems.at[0]),
                     pltpu.make_async_copy(passing, theirs, local_sems.at[1])]
            for cp in loads:
                cp.start()
            for cp in loads:
                cp.wait()

            def add(r, carry):
                rows = pl.ds(r, rc)
                theirs[rows, :] = (mine[rows, :].astype(F32) + theirs[rows, :].astype(F32)).astype(MM)
                return carry

            _chunks(mine.shape[0], rc, add, 0)
            to_second.start()

        @pl.when(i == nb - 1)
        def _():
            to_first[0].wait_recv()
            to_second.wait_recv()
            for cp in to_first + [to_second]:
                cp.wait_send()
            ln_all[4 * x_ + 2 * y_ + c_] = jnp.broadcast_to(jnp.sum(acc_ref[...], axis=0, keepdims=True), acc_ref.shape)
            gather = _Gather(lambda a, px, py, pc: ln_all.at[4 * px + 2 * py + pc], ln_send, ln_recv)
            gather.start_own(0)
            gather.finish(0)
            total = ln_all[0]
            for dev in range(1, N_DEV):
                total = total + ln_all[dev]
            ln_ref[...] = total

    vm = pl.BlockSpec(memory_space=pltpu.VMEM)
    hbm = pl.BlockSpec(memory_space=pl.ANY)
    blk = lambda w: pl.BlockSpec((tb, w), lambda i: (i, 0))
    landed = jax.ShapeDtypeStruct(part.shape[1:], part.dtype)
    outs = pl.pallas_call(
        body, name="input_grad", grid=(nb,),
        in_specs=[blk(6 * d), vm, blk(d), blk(d), vm, hbm],
        out_specs=(blk(d), pl.BlockSpec((SUBLANES, d), lambda i: (0, 0)), hbm, hbm, hbm),
        out_shape=(jax.ShapeDtypeStruct((t_len, d), F32), jax.ShapeDtypeStruct((SUBLANES, d), F32), landed, landed, landed),
        scratch_shapes=[pltpu.SemaphoreType.DMA((3,)), pltpu.SemaphoreType.DMA((3,)), pltpu.SemaphoreType.DMA((2,)),
                        pltpu.VMEM((SUBLANES, d), F32), pltpu.VMEM((N_DEV, SUBLANES, d), F32),
                        pltpu.SemaphoreType.DMA((7,)), pltpu.SemaphoreType.DMA((7,)),
                        pltpu.VMEM(part.shape[1:], MM), pltpu.VMEM(part.shape[1:], MM)],
        compiler_params=_params(dimension_semantics=("arbitrary",)),
    )(dp, win_all, x, dh, sp, part)
    return outs[0], outs[1], (outs[2], outs[4])


_CHIP_RELATIONS = [(0, 0), (1, 0), (0, 1), (1, 1)]


def _related_block(k, core):
    x, y, _ = _mesh_pos()
    fx, fy = _CHIP_RELATIONS[k]
    return 4 * (x ^ fx) + 2 * (y ^ fy) + core


class _ChipExchange:
    def __init__(self, part_refs, land_refs, send_sems, recv_sems):
        self.part_refs, self.land_refs, self.send_sems, self.recv_sems = part_refs, land_refs, send_sems, recv_sems

    def copies(self):
        x, y, c = _mesh_pos()
        for a in range(len(self.part_refs)):
            for k in (1, 2, 3):
                fx, fy = _CHIP_RELATIONS[k]
                yield pltpu.make_async_remote_copy(
                    src_ref=self.part_refs[a].at[k - 1], dst_ref=self.land_refs[a].at[k - 1],
                    send_sem=self.send_sems.at[3 * a + k - 1], recv_sem=self.recv_sems.at[3 * a + k - 1],
                    device_id=(x ^ fx, y ^ fy, c), device_id_type=MESH)

    def start(self):
        for cp in self.copies():
            cp.start()

    def finish(self):
        for cp in self.copies():
            cp.wait_recv()
        for cp in self.copies():
            cp.wait_send()


def _weight_grad_stage1(name, blk_shape, n_split, operands, in_specs, product, riders=(), slabs=()):
    n_rows, n_cols = blk_shape
    rs = n_rows // n_split
    rc = 32
    n_in, n_ride, n_slab = len(operands), len(riders), len(slabs)
    _, _, c = _mesh_pos()
    order = jnp.stack([_related_block(k, 1 - c) for k in range(4)]
                      + [_related_block(k, c) for k in (1, 2, 3, 0)]).astype(jnp.int32)

    def body(order_ref, *refs):
        ins = refs[:n_in]
        ride_in = refs[n_in:n_in + n_ride]
        slab_in = refs[n_in + n_ride:n_in + n_ride + n_slab]
        n_op = n_in + n_ride + n_slab
        part_ref, own_ref = refs[n_op:n_op + 2]
        ride_out = refs[n_op + 2:n_op + 2 + n_ride]
        gathered = refs[n_op + 2 + n_ride:n_op + 2 + n_ride + n_slab]
        (gbuf, sendbuf, from_sib, send_sems, recv_sems, ride_send, ride_recv,
         slab_send, slab_recv, slab_local) = refs[n_op + 2 + n_ride + n_slab:]
        exchange = _ChipExchange(ride_in, ride_out, ride_send, ride_recv)
        s = pl.program_id(0)
        x, y, c = _mesh_pos()
        me = 4 * x + 2 * y + c
        gather = _BalancedGather(lambda a, px, py, pc: gathered[a].at[4 * px + 2 * py + pc], slab_send, slab_recv, slab_in)
        keep_own = [pltpu.make_async_copy(slab_in[a], gathered[a].at[me], slab_local.at[a]) for a in range(n_slab)]

        def to_sibling(k):
            return pltpu.make_async_remote_copy(
                src_ref=sendbuf.at[k], dst_ref=from_sib.at[k], send_sem=send_sems.at[k], recv_sem=recv_sems.at[k],
                device_id=(x, y, 1 - c), device_id_type=MESH)

        @pl.when(s == 0)
        def _():
            exchange.start()
            for a in range(n_slab):
                gather.start_own(a)
                keep_own[a].start()

        @pl.when(s == 5)
        def _():
            for a in range(n_slab):
                gather.on_neighbour(a, 0)
                gather.on_neighbour(a, 1)

        @pl.when(s == 7)
        def _():
            for a in range(n_slab):
                gather.on_diagonal(a)

        for h in range(n_split):
            gbuf[h * rs:(h + 1) * rs, :] = product(ins, h)

        @pl.when(s < 4)
        def _():
            def narrow(r, carry):
                sendbuf[s, pl.ds(r, rc), :] = gbuf[pl.ds(r, rc), :].astype(MM)
                return carry

            _chunks(n_rows, rc, narrow, 0)
            to_sibling(s).start()

        @pl.when(s >= 4)
        def _():
            k = jnp.where(s == 7, 0, s - 3)
            to_sibling(k).wait_recv()

            @pl.when(s < 7)
            def _():
                def add(r, carry):
                    rows = pl.ds(r, rc)
                    part_ref[0, rows, :] = (gbuf[rows, :] + from_sib[k, rows, :].astype(F32)).astype(MM)
                    return carry

                _chunks(n_rows, rc, add, 0)

            @pl.when(s == 7)
            def _():
                def add(r, carry):
                    rows = pl.ds(r, rc)
                    own_ref[rows, :] = gbuf[rows, :] + from_sib[0, rows, :].astype(F32)
                    return carry

                _chunks(n_rows, rc, add, 0)
                for kk in range(4):
                    to_sibling(kk).wait_send()
                exchange.finish()
                for a in range(n_slab):
                    gather.wait_sibling(a)
                    for j in range(3):
                        gather.wait_passed_on(a, j)
                    gather.wait_sends(a)
                    keep_own[a].wait()

    hbm = pl.BlockSpec(memory_space=pl.ANY)
    grid_spec = pltpu.PrefetchScalarGridSpec(
        num_scalar_prefetch=1, grid=(N_DEV,), in_specs=list(in_specs) + [hbm] * (n_ride + n_slab),
        out_specs=(pl.BlockSpec((1, n_rows, n_cols), lambda s, o: (jnp.clip(s - 4, 0, 2), 0, 0)),
                   pl.BlockSpec((n_rows, n_cols), lambda s, o: (0, 0))) + (hbm,) * (n_ride + n_slab),
        scratch_shapes=[pltpu.VMEM((n_rows, n_cols), F32), pltpu.VMEM((4, n_rows, n_cols), MM),
                        pltpu.VMEM((4, n_rows, n_cols), MM),
                        pltpu.SemaphoreType.DMA((4,)), pltpu.SemaphoreType.DMA((4,)),
                        pltpu.SemaphoreType.DMA((max(3 * n_ride, 1),)), pltpu.SemaphoreType.DMA((max(3 * n_ride, 1),)),
                        pltpu.SemaphoreType.DMA((max(8 * n_slab, 1),)), pltpu.SemaphoreType.DMA((max(8 * n_slab, 1),)),
                        pltpu.SemaphoreType.DMA((max(n_slab, 1),))])
    outs = pl.pallas_call(
        body, name=name, grid_spec=grid_spec,
        out_shape=(jax.ShapeDtypeStruct((3, n_rows, n_cols), MM), jax.ShapeDtypeStruct((n_rows, n_cols), F32))
                  + tuple(jax.ShapeDtypeStruct(p.shape, p.dtype) for p in riders)
                  + tuple(jax.ShapeDtypeStruct((N_DEV,) + a.shape, a.dtype) for a in slabs),
        compiler_params=_params(dimension_semantics=("arbitrary",)),
    )(order, *operands, *riders, *slabs)
    return outs[0], outs[1], outs[2:2 + n_ride], outs[2 + n_ride:]


def _weight_grad_in(xnt, dp, riders, slabs):
    d, t_len = xnt.shape
    cols = dp.shape[1] // N_DEV
    half = d // 2
    return _weight_grad_stage1(
        "weight_grad_in", (d, cols), 2, (xnt, dp),
        [pl.BlockSpec(memory_space=pltpu.VMEM), pl.BlockSpec((t_len, cols), lambda s, o: (0, o[s]))],
        lambda refs, h: _dot(refs[0][h * half:(h + 1) * half, :], refs[1][...]), riders, slabs)


def _weight_grad_out(yt, dhb):
    d2, t_len = yt.shape
    d = dhb.shape[1]
    rows = d2 // N_DEV
    return _weight_grad_stage1(
        "weight_grad_out", (rows, d), 1, (yt, dhb),
        [pl.BlockSpec((rows, t_len), lambda s, o: (o[s], 0)), pl.BlockSpec(memory_space=pltpu.VMEM)],
        lambda refs, h: _dot(refs[0][...], refs[1][...]))


def _update_shard(own, others, w, m, v, name):
    n_rows, n_cols = w.shape
    rb = min(256, n_rows)
    n_other = len(others)

    def body(own_ref, *refs):
        other_refs = refs[:n_other]
        w_ref, m_ref, v_ref, grad_ref, delta_ref, mo_ref, vo_ref = refs[n_other:]
        g = own_ref[...]
        for ref in other_refs:
            for k in range(ref.shape[0] if len(ref.shape) == 3 else 1):
                g = g + (ref[k] if len(ref.shape) == 3 else ref[...]).astype(F32)
        delta, m_new, v_new = _adamw(w_ref[...], g, m_ref[...], v_ref[...])
        grad_ref[...] = g
        delta_ref[...] = delta
        mo_ref[...] = m_new
        vo_ref[...] = v_new

    blk = pl.BlockSpec((rb, n_cols), lambda i: (i, 0))
    stacked = lambda n: pl.BlockSpec((n, rb, n_cols), lambda i: (0, i, 0))
    out = jax.ShapeDtypeStruct((n_rows, n_cols), F32)
    return pl.pallas_call(
        body, name=name, grid=(n_rows // rb,),
        in_specs=[blk] + [stacked(o.shape[0]) if o.ndim == 3 else blk for o in others] + [blk, blk, blk],
        out_specs=(blk, blk, blk, blk), out_shape=(out, out, out, out),
        compiler_params=_params(dimension_semantics=("arbitrary",)),
    )(own, *others, w, m, v)


def _small_update(gat_v, gat_g, ln_tot, vec_w, vec_m, vec_v, gates, convs):
    n_vec = len(vec_w)
    n_heads, hd, _ = gates[0].shape
    tw = gat_g.shape[2]
    s8 = SUBLANES
    per = tw // hd
    n_tiles = n_heads // per
    cc = convs[0].shape[1]
    n_in = 3 + 3 * n_vec + 12

    def body(*refs):
        gv_ref, gg_ref, ln_ref = refs[:3]
        w_refs, m_refs, v_refs = (refs[3 + j * n_vec:3 + (j + 1) * n_vec] for j in range(3))
        gate_refs = refs[3 + 3 * n_vec:3 + 3 * n_vec + 6]
        conv_refs = refs[3 + 3 * n_vec + 6:n_in]
        loss_o = refs[n_in]
        kinds = [refs[n_in + 1 + j * (n_vec + 4):n_in + 1 + (j + 1) * (n_vec + 4)] for j in range(4)]
        tv, tg = refs[n_in + 1 + 4 * (n_vec + 4):]
        x, y, c = _mesh_pos()
        me = 4 * x + 2 * y + c

        def emit(k_out, w, g, m, v):
            delta, m_new, v_new = _adamw(w, g, m, v)
            for ref, val in zip(k_out, (g, delta, m_new, v_new)):
                ref[...] = val

        total = gv_ref[0]
        for dev in range(1, N_DEV):
            total = total + gv_ref[dev]
        tv[...] = total
        tv[SL_LN_G:SL_LN_G + 1, :] = ln_ref[0:1, :]

        def sum_gates(r, carry):
            rows = pl.ds(r, 2 * s8)
            part = gg_ref[0, rows, :].astype(F32)
            for dev in range(1, N_DEV):
                part = part + gg_ref[dev, rows, :].astype(F32)
            tg[rows, :] = part
            return carry

        _chunks(tg.shape[0], 2 * s8, sum_gates, 0)
        loss_o[...] = jnp.broadcast_to(tv[SL_LOSS:SL_LOSS + 1, 0:LANES], loss_o.shape)
        for p in range(n_vec):
            w, g = w_refs[p][...], tv[SL_LN_G + p, :]
            if SL_LN_G + p == SL_LAM:
                g = g * (RG_LRU_C * jax.nn.sigmoid(-w))
            emit([k_out[p] for k_out in kinds], w, g, m_refs[p][...], v_refs[p][...])
        lanes = pl.ds(pl.multiple_of(me * cc, cc), cc)
        for j, (row0, n) in enumerate(((SL_CONV_W, 3), (SL_LRU_W, 4))):
            w_ref, m_ref, v_ref = conv_refs[3 * j:3 * j + 3]
            emit([k_out[n_vec + 2 + j] for k_out in kinds], w_ref[...], tv[row0:row0 + n, lanes], m_ref[...], v_ref[...])
        for mtx in range(2):
            w_ref, m_ref, v_ref = gate_refs[3 * mtx:3 * mtx + 3]
            for k in range(n_tiles):
                tile = tg[(mtx * n_tiles + k) * hd:(mtx * n_tiles + k + 1) * hd, :]
                for a in range(per):
                    head = k * per + a
                    g = tile[:, a * hd:(a + 1) * hd]
                    delta, m_new, v_new = _adamw(w_ref[head], g, m_ref[head], v_ref[head])
                    for k_out, val in zip(kinds, (g, delta, m_new, v_new)):
                        k_out[n_vec + mtx][head] = val

    vm = pl.BlockSpec(memory_space=pltpu.VMEM)
    like = lambda a: jax.ShapeDtypeStruct(a.shape, F32)
    per_kind = tuple(like(a) for a in vec_w) + (like(gates[0]), like(gates[3]), like(convs[0]), like(convs[3]))
    n_out = 1 + 4 * len(per_kind)
    outs = pl.pallas_call(
        body, name="small_update",
        in_specs=[vm] * n_in, out_specs=(vm,) * n_out,
        out_shape=(jax.ShapeDtypeStruct((SUBLANES, LANES), F32),) + per_kind * 4,
        scratch_shapes=[pltpu.VMEM(gat_v.shape[1:], F32), pltpu.VMEM(gat_g.shape[1:], F32)],
        compiler_params=_params(),
    )(gat_v, gat_g, ln_tot, *vec_w, *vec_m, *vec_v, *gates, *convs)
    return outs[0], [outs[1 + j * len(per_kind):1 + (j + 1) * len(per_kind)] for j in range(4)]


def _head_ones(head_dim, tw):
    lane = jnp.arange(tw) // head_dim
    return (lane[:, None] == lane[None, :]).astype(MM)


def kernel(x, ln_g, w_in, conv_w, lru_conv_w, lru_conv_b, w_a, b_a, w_i, b_i, lam, conv_out_g, lru_out_g, w_out, final_g, loss_target, m_ln_g, m_w_in, m_conv_w, m_lru_conv_w, m_lru_conv_b, m_w_a, m_b_a, m_w_i, m_b_i, m_lam, m_conv_out_g, m_lru_out_g, m_w_out, m_final_g, v_ln_g, v_w_in, v_conv_w, v_lru_conv_w, v_lru_conv_b, v_w_a, v_b_a, v_w_i, v_b_i, v_lam, v_conv_out_g, v_lru_out_g, v_w_out, v_final_g):
    _, t_len, d = x.shape
    hd_l = d // N_LRU_HEADS
    tw = min(MXU_TILE, d)
    x2, tgt2 = x[0], loss_target[0]

    small = [ln_g, lru_conv_b, b_a, b_i, lam, conv_out_g, lru_out_g, final_g]
    p, xnt, win_all, wout_all, _, sp, wa_t, wi_t = _gather_project(
        x2, w_in, w_out, conv_w, lru_conv_w, ln_g.reshape(1, d), w_a, w_i, small, min(256, t_len), tw)
    wout_full = wout_all.reshape(N_DEV * w_out.shape[0], d)
    ones_c, ones_l = _head_ones(d // N_CONV_HEADS, tw), _head_ones(hd_l, tw)

    h, dh, dhb, facc, *saved = _forward(x2, tgt2, p, wout_full, wa_t, wi_t, sp, ones_c, ones_l, min(256, t_len))
    dp, yt, slab_v, slab_g = _backward(p, h, dh, saved, facc, wout_full, wa_t, wi_t, sp, ones_c, ones_l, min(256, t_len))
    part_out, own_out, _, _ = _weight_grad_out(yt, dhb)
    part_in, own_in, (chips_out,), (gat_v, gat_g) = _weight_grad_in(xnt, dp, (part_out,), (slab_v, slab_g))
    grad_x, ln_tot, sums_in = _input_grad(dp, win_all, x2, dh, sp, part_in, min(512, t_len))
    gw_in, dw_in, mw_in, vw_in = _update_shard(own_in, sums_in, w_in, m_w_in, v_w_in, "update_w_in")
    gw_out, dw_out, mw_out, vw_out = _update_shard(own_out, (chips_out,), w_out, m_w_out, v_w_out, "update_w_out")

    loss_tile, kinds = _small_update(
        gat_v, gat_g, ln_tot, small,
        [m_ln_g, m_lru_conv_b, m_b_a, m_b_i, m_lam, m_conv_out_g, m_lru_out_g, m_final_g],
        [v_ln_g, v_lru_conv_b, v_b_a, v_b_i, v_lam, v_conv_out_g, v_lru_out_g, v_final_g],
        (w_a, m_w_a, v_w_a, w_i, m_w_i, v_w_i), (conv_w, m_conv_w, v_conv_w, lru_conv_w, m_lru_conv_w, v_lru_conv_w))

    def unpack(kind, big_in, big_out):
        vec, (wa_, wi_, cw_, lw_) = kind[:len(small)], kind[len(small):]
        return [vec[0], big_in, cw_, lw_, vec[1], wa_, vec[2], wi_, vec[3], vec[4], vec[5], vec[6], big_out, vec[7]]

    return (loss_tile[0, 0], grad_x[None], *unpack(kinds[0], gw_in, gw_out), *unpack(kinds[1], dw_in, dw_out),
            *unpack(kinds[2], mw_in, mw_out), *unpack(kinds[3], vw_in, vw_out))
```

```python
import functools

import jax
import jax.numpy as jnp
from jax import lax
from jax.experimental import pallas as pl
from jax.experimental.pallas import tpu as pltpu

F32 = jnp.float32
MM = jnp.bfloat16
MESH = pl.DeviceIdType.MESH

N_DEV = 8
N_CONV_HEADS = 8
N_LRU_HEADS = 16
RG_LRU_C = 8.0
RMS_EPS = 1e-6
ADAM_LR, ADAM_B1, ADAM_B2, ADAM_EPS, ADAM_WD, ADAM_STEP = 0.001, 0.9, 0.999, 1e-08, 0.01, 10
ADAM_BC1 = 1.0 - ADAM_B1 ** ADAM_STEP
ADAM_BC2 = 1.0 - ADAM_B2 ** ADAM_STEP

SUBLANES = 8
LANES = 128
MXU_TILE = 256
VMEM_LIMIT = 56 * 1024 * 1024

SP_LN_G, SP_LRU_B, SP_B_A, SP_B_I, SP_LAM, SP_CONV_G, SP_LRU_G, SP_FINAL_G, SP_CONV_W, SP_LRU_W = 0, 1, 2, 3, 4, 5, 6, 7, 8, 11
SP_ROWS = 16
P_B, P_C, P_XC, P_GC, P_XL, P_GL = 0, 1, 2, 3, 4, 5
A_CONV_G, A_LRU_G, A_LAM, A_B_A, A_B_I, A_CONV_W, A_LRU_W, A_LRU_B = 0, 1, 2, 3, 4, 5, 8, 12
A_GROUPS = 13
SL_LOSS, SL_LN_G, SL_LRU_B, SL_B_A, SL_B_I, SL_LAM, SL_CONV_G, SL_LRU_G, SL_FINAL_G, SL_CONV_W, SL_LRU_W = 0, 1, 2, 3, 4, 5, 6, 7, 8, 16, 24
SL_ROWS = 32


def _params(vmem=True, **kw):
    if vmem:
        kw["vmem_limit_bytes"] = VMEM_LIMIT
    return pltpu.CompilerParams(**kw)


def _dot(a, b):
    return jnp.dot(a, b, preferred_element_type=F32)


def _dot_nt(a, b):
    return lax.dot_general(a, b, (((1,), (1,)), ((), ())), preferred_element_type=F32)


def _head_sums(v, ones_tile):
    tw = ones_tile.shape[0]
    vb = v.astype(MM)
    return jnp.concatenate([_dot(vb[:, k:k + tw], ones_tile) for k in range(0, v.shape[1], tw)], axis=1)


def _head_rstd(v, ones_tile, head_dim):
    return lax.rsqrt(_head_sums(v * v, ones_tile) * (1.0 / head_dim) + RMS_EPS)


def _sigmoid(x):
    return 0.5 * jnp.tanh(0.5 * x) + 0.5


def _lru_input_scale_sq(log_a, a):
    return -jnp.tanh(log_a) * (1.0 + a * a)


def _log_sigmoid(x):
    z = jnp.exp(-jnp.abs(x))
    u = 1.0 + z
    log1p_z = jnp.where(u == 1.0, z, jnp.log(u) * (z / (u - 1.0)))
    return jnp.minimum(x, 0.0) - log1p_z


def _row_iota(d):
    return lax.broadcasted_iota(jnp.int32, (SUBLANES, d), 0)


def _shift_down(cur, prev, s, row):
    return jnp.where(row >= s, pltpu.roll(cur, s, axis=0), pltpu.roll(prev, s, axis=0))


def _shift_up(cur, nxt, s, row):
    k = SUBLANES - s
    return jnp.where(row < k, pltpu.roll(cur, k, axis=0), pltpu.roll(nxt, k, axis=0))


def _scan_fwd(a, b, h_prev, row):
    for s in (1, 2, 4):
        a_s = jnp.where(row >= s, pltpu.roll(a, s, axis=0), 1.0)
        b_s = jnp.where(row >= s, pltpu.roll(b, s, axis=0), 0.0)
        b = a * b_s + b
        a = a * a_s
    return a * h_prev + b


def _scan_bwd(a_next, b, g_next, row):
    a = a_next
    for s in (1, 2, 4):
        k = SUBLANES - s
        a_s = jnp.where(row < k, pltpu.roll(a, k, axis=0), 1.0)
        b_s = jnp.where(row < k, pltpu.roll(b, k, axis=0), 0.0)
        b = a * b_s + b
        a = a * a_s
    return a * g_next + b


def _bcast_row(v, r):
    return jnp.broadcast_to(v[r:r + 1, :], v.shape)


def _chunks(n_rows, rc, body, init, reverse=False):
    n = n_rows // rc

    def step(i, carry):
        j = (n - 1 - i) if reverse else i
        return body(pl.multiple_of(j * rc, rc), carry)

    return lax.fori_loop(0, n, step, init)


def _adamw(w, g, m, v):
    m = ADAM_B1 * m + (1.0 - ADAM_B1) * g
    v = ADAM_B2 * v + (1.0 - ADAM_B2) * (g * g)
    m_hat = m / ADAM_BC1
    v_hat = v / ADAM_BC2
    delta = -ADAM_LR * (m_hat / (jnp.sqrt(v_hat) + ADAM_EPS) + ADAM_WD * w)
    return delta, m, v


def _mesh_pos():
    return lax.axis_index("x"), lax.axis_index("y"), lax.axis_index("c")


class _Gather:
    def __init__(self, blocks_of, send_sems, recv_sems, own_src=None):
        x, y, c = _mesh_pos()
        self.c = c
        self.me, self.sibling = (x, y, c), (x, y, 1 - c)
        self.chips = [(1 - x, y), (x, 1 - y), (1 - x, 1 - y)]
        self.blocks_of, self.send_sems, self.recv_sems = blocks_of, send_sems, recv_sems
        self.own_src = own_src

    def copy(self, a, k, block, to):
        src = self.blocks_of(a, *block)
        if block is self.me and self.own_src is not None:
            src = self.own_src[a]
        return pltpu.make_async_remote_copy(
            src_ref=src, dst_ref=self.blocks_of(a, *block),
            send_sem=self.send_sems.at[a * 7 + k], recv_sem=self.recv_sems.at[a * 7 + k],
            device_id=to, device_id_type=MESH)

    def start_own(self, a):
        self.copy(a, 0, self.me, self.sibling).start()
        for j, chip in enumerate(self.chips):
            self.copy(a, 1 + j, self.me, (*chip, self.c)).start()

    def wait_sibling(self, a):
        self.copy(a, 0, self.sibling, self.me).wait_recv()

    def wait_chip_and_pass_on(self, a, j):
        block = (*self.chips[j], self.c)
        self.copy(a, 1 + j, block, self.me).wait_recv()
        self.copy(a, 4 + j, block, self.sibling).start()

    def wait_passed_on(self, a, j):
        self.copy(a, 4 + j, (*self.chips[j], 1 - self.c), self.me).wait_recv()

    def wait_sends(self, a):
        self.copy(a, 0, self.me, self.sibling).wait_send()
        for j, chip in enumerate(self.chips):
            self.copy(a, 1 + j, self.me, (*chip, self.c)).wait_send()
            self.copy(a, 4 + j, (*chip, self.c), self.sibling).wait_send()

    def finish(self, a):
        for j in range(3):
            self.wait_chip_and_pass_on(a, j)
        self.wait_sibling(a)
        for j in range(3):
            self.wait_passed_on(a, j)
        self.wait_sends(a)


class _BalancedGather:
    def __init__(self, slot, send_sems, recv_sems, own_src):
        x, y, c = _mesh_pos()
        self.c = c
        self.me, self.sibling = (x, y, c), (x, y, 1 - c)
        self.chips = [(1 - x, y), (x, 1 - y), (1 - x, 1 - y)]
        self.slot, self.send_sems, self.recv_sems, self.own_src = slot, send_sems, recv_sems, own_src

    def half(self, a, block, which):
        ref = self.slot(a, *block)
        n = ref.shape[0] // 2
        return ref.at[pl.ds(which * n, n)]

    def copy(self, a, k, src, dst, to):
        return pltpu.make_async_remote_copy(
            src_ref=src, dst_ref=dst, send_sem=self.send_sems.at[a * 8 + k], recv_sem=self.recv_sems.at[a * 8 + k],
            device_id=to, device_id_type=MESH)

    def whole(self, a, k, block, to):
        src = self.own_src[a] if block is self.me else self.slot(a, *block)
        return self.copy(a, k, src, self.slot(a, *block), to)

    def halved(self, a, k, block, which, to):
        return self.copy(a, k, self.half(a, block, which), self.half(a, block, which), to)

    def on(self, chip):
        return (*self.chips[chip], self.c)

    def start_own(self, a):
        self.whole(a, 0, self.me, self.sibling).start()
        self.whole(a, 1, self.me, self.on(0)).start()
        self.whole(a, 2, self.me, self.on(1)).start()

    def wait_sibling(self, a):
        self.whole(a, 0, self.sibling, self.me).wait_recv()

    def on_neighbour(self, a, j):
        self.whole(a, 1 + j, self.on(j), self.me).wait_recv()
        self.halved(a, 3 + j, self.on(j), j, self.on(1 - j)).start()
        self.whole(a, 5 + j, self.on(j), self.sibling).start()

    def on_diagonal(self, a):
        self.halved(a, 3, self.on(2), 0, self.me).wait_recv()
        self.halved(a, 4, self.on(2), 1, self.me).wait_recv()
        self.whole(a, 7, self.on(2), self.sibling).start()

    def wait_passed_on(self, a, j):
        self.whole(a, 5 + j, (*self.chips[j], 1 - self.c), self.me).wait_recv()

    def wait_sends(self, a):
        self.whole(a, 0, self.me, self.sibling).wait_send()
        for j in range(2):
            self.whole(a, 1 + j, self.me, self.on(j)).wait_send()
            self.halved(a, 3 + j, self.on(j), j, self.on(1 - j)).wait_send()
        for j in range(3):
            self.whole(a, 5 + j, self.on(j), self.sibling).wait_send()


def _block_order(n_split):
    x, y, c = _mesh_pos()
    idx = lambda chip, core: 4 * chip[0] + 2 * chip[1] + core
    own, xn, yn, dg = (x, y), (1 - x, y), (x, 1 - y), (1 - x, 1 - y)
    parts = range(n_split)
    order = [(idx(own, c), t) for t in parts] + [(idx(own, 1 - c), t) for t in parts]
    for t in parts:
        order += [(idx(xn, c), t), (idx(xn, 1 - c), t), (idx(yn, c), t), (idx(yn, 1 - c), t)]
    for t in parts:
        order += [(idx(dg, c), t), (idx(dg, 1 - c), t)]
    return jnp.stack([n_split * b + t for b, t in order]).astype(jnp.int32)


def _gather_project(x, w_in, w_out, conv_w, lru_conv_w, ln_g, w_a, w_i, vecs, tb, tw):
    t_len, d = x.shape
    nb = t_len // tb
    cols = w_in.shape[1]
    n_split = cols // tw if cols % tw == 0 else 1
    hc = cols // n_split
    n_steps = n_split * N_DEV
    a_out = n_split
    mc = min(512, t_len)
    conv_pack = jax.ShapeDtypeStruct((SUBLANES, conv_w.shape[1]), F32)
    srcs = (w_in, w_out, conv_pack)
    dts = (MM, MM, F32)
    n_vec = len(vecs)
    n_heads, hd, _ = w_a.shape
    per = tw // hd

    def body(order_ref, x_ref, win_ref, wout_ref, cw_ref, lw_ref, lng_ref, wa_ref, wi_ref, *refs):
        vec_refs = refs[:n_vec]
        (p_ref, xnt_ref, win_all, wout_all, cp_all, sp_ref, wat_ref, wit_ref,
         xnb, wall, st_out, st_cp, cp_vm, send_sems, recv_sems, cp_send, cp_recv, local_sems) = refs[n_vec:]
        i = pl.program_id(0)
        x_, y_, c_ = _mesh_pos()
        me = 4 * x_ + 2 * y_ + c_

        def slot(a, px, py, pc):
            dev = 4 * px + 2 * py + pc
            return wall.at[a, dev] if a < n_split else wout_all.at[dev]

        stages = [wall.at[t, me] for t in range(n_split)] + [st_out]
        gather = _BalancedGather(slot, send_sems, recv_sems, stages)
        small = _Gather(lambda a, px, py, pc: cp_all.at[4 * px + 2 * py + pc], cp_send, cp_recv, own_src=[st_cp])
        keep_own = [pltpu.make_async_copy(wall.at[t, me], win_all.at[me, :, t * hc:(t + 1) * hc], local_sems.at[t])
                    for t in range(n_split)]
        keep_own += [pltpu.make_async_copy(st_out, wout_all.at[me], local_sems.at[n_split]),
                     pltpu.make_async_copy(st_cp, cp_all.at[me], local_sems.at[n_split + 1])]

        def keep(k):
            blk, t = order_ref[k] // n_split, order_ref[k] % n_split
            return pltpu.make_async_copy(
                wall.at[t, blk], win_all.at[blk, :, pl.ds(pl.multiple_of(t * hc, hc), hc)], local_sems.at[2 + k])

        @pl.when(i == 0)
        def _():
            rc = 32

            def cast(r, carry):
                rows = pl.ds(r, rc)
                for t in range(n_split):
                    wall[t, me, rows, :] = win_ref[rows, t * hc:(t + 1) * hc].astype(MM)
                return carry

            _chunks(d, rc, cast, 0)

            def cast_out(r, carry):
                st_out[pl.ds(r, rc), :] = wout_ref[pl.ds(r, rc), :].astype(MM)
                return carry

            _chunks(wout_ref.shape[0], rc, cast_out, 0)
            n_cw, n_lw = cw_ref.shape[0], lw_ref.shape[0]
            st_cp[...] = jnp.zeros_like(st_cp)
            st_cp[0:n_cw, :] = cw_ref[...]
            st_cp[n_cw:n_cw + n_lw, :] = lw_ref[...]
            for t in range(n_split):
                gather.start_own(t)
            for cp in keep_own:
                cp.start()

        @pl.when(i < nb)
        def _():
            xv = x_ref[...]
            r0 = lax.rsqrt(jnp.mean(xv * xv, axis=-1, keepdims=True) + RMS_EPS)
            xn = xv * r0 * lng_ref[...]
            xnb[pl.ds(pl.multiple_of(i * tb, tb), tb), :] = xn.astype(MM)
            xnt_ref[...] = xn.T.astype(MM)

        before = {k: [] for k in range(n_steps)}
        do = functools.partial
        for t in range(n_split):
            before[n_split + t].append(do(gather.wait_sibling, t))
            k0 = 2 * n_split + 4 * t
            before[k0] += [do(gather.on_neighbour, t, 0), do(gather.on_neighbour, t, 1)]
            before[k0 + 1].append(do(gather.wait_passed_on, t, 0))
            before[k0 + 3].append(do(gather.wait_passed_on, t, 1))
            kd = 6 * n_split + 2 * t
            before[kd].append(do(gather.on_diagonal, t))
            before[kd + 1].append(do(gather.wait_passed_on, t, 2))
        before[6 * n_split - 4] += [do(gather.start_own, a_out), do(small.start_own, 0)]
        before[6 * n_split] += [do(gather.on_neighbour, a_out, 0), do(gather.on_neighbour, a_out, 1)]
        before[n_steps - 1].append(do(gather.on_diagonal, a_out))

        for k in range(n_steps):
            @pl.when(i == nb + k)
            def _(k=k):
                for action in before[k]:
                    action()
                blk, t = order_ref[k] // n_split, order_ref[k] % n_split
                if k >= n_split:
                    keep(k).start()

                def project(r, carry):
                    rows = pl.ds(r, mc)
                    p_ref[rows, :] = _dot(xnb[rows, :], wall[t, blk]).astype(MM)
                    return carry

                _chunks(t_len, mc, project, 0)
                if k == n_steps - 1:
                    for t2 in range(n_split):
                        gather.wait_sends(t2)
                    gather.wait_sibling(a_out)
                    for j in range(3):
                        gather.wait_passed_on(a_out, j)
                    gather.wait_sends(a_out)
                    small.finish(0)
                    for cp in keep_own + [keep(kk) for kk in range(n_split, n_steps)]:
                        cp.wait()
                    load = pltpu.make_async_copy(cp_all, cp_vm, local_sems.at[n_steps + 2])
                    load.start()
                    load.wait()
                    for r, ref in enumerate(vec_refs):
                        sp_ref[r, :] = ref[...]
                    sp_ref[n_vec:n_vec + SUBLANES, :] = jnp.concatenate([cp_vm[dev] for dev in range(N_DEV)], axis=1)
                    for src, dst in ((wa_ref, wat_ref), (wi_ref, wit_ref)):
                        dst[...] = jnp.zeros_like(dst)
                        for head in range(n_heads):
                            lo = (head % per) * hd
                            dst[head // per, lo:lo + hd, lo:lo + hd] = src[head].astype(MM)

    vm = pl.BlockSpec(memory_space=pltpu.VMEM)
    hbm = pl.BlockSpec(memory_space=pl.ANY)
    grid_spec = pltpu.PrefetchScalarGridSpec(
        num_scalar_prefetch=1, grid=(nb + n_steps,),
        in_specs=[pl.BlockSpec((tb, d), lambda i, o: (jnp.minimum(i, nb - 1), 0))] + [vm] * (7 + n_vec),
        out_specs=(pl.BlockSpec((t_len, hc), lambda i, o: (0, o[jnp.maximum(i - nb, 0)])),
                   pl.BlockSpec((d, tb), lambda i, o: (0, jnp.minimum(i, nb - 1))), hbm, hbm, hbm,
                   pl.BlockSpec((SP_ROWS, d), lambda i, o: (0, 0)),
                   pl.BlockSpec((n_heads // per, tw, tw), lambda i, o: (0, 0, 0)),
                   pl.BlockSpec((n_heads // per, tw, tw), lambda i, o: (0, 0, 0))),
        scratch_shapes=[pltpu.VMEM((t_len, d), MM), pltpu.VMEM((n_split, N_DEV, d, hc), MM),
                        pltpu.VMEM(w_out.shape, MM), pltpu.VMEM(conv_pack.shape, F32),
                        pltpu.VMEM((N_DEV,) + conv_pack.shape, F32),
                        pltpu.SemaphoreType.DMA((8 * (n_split + 1),)), pltpu.SemaphoreType.DMA((8 * (n_split + 1),)),
                        pltpu.SemaphoreType.DMA((7,)), pltpu.SemaphoreType.DMA((7,)), pltpu.SemaphoreType.DMA((n_steps + 3,))])
    return pl.pallas_call(
        body, name="gather_project", grid_spec=grid_spec,
        out_shape=(jax.ShapeDtypeStruct((t_len, N_DEV * cols), MM),
                   jax.ShapeDtypeStruct((d, t_len), MM))
                  + tuple(jax.ShapeDtypeStruct((N_DEV,) + s.shape, dt) for s, dt in zip(srcs, dts))
                  + (jax.ShapeDtypeStruct((SP_ROWS, d), F32),)
                  + (jax.ShapeDtypeStruct((n_heads // per, tw, tw), MM),) * 2,
        compiler_params=_params(dimension_semantics=("arbitrary",)),
    )(_block_order(n_split), x, w_in, w_out, conv_w, lru_conv_w, ln_g, w_a, w_i, *vecs)


def _forward(x, tgt, p, wout, wa_t, wi_t, sp, ones_c, ones_l, tb):
    t_len, d = x.shape
    nb = t_len // tb
    n_tiles, tw = wa_t.shape[0], wa_t.shape[1]
    hd_c, hd_l = d // N_CONV_HEADS, d // N_LRU_HEADS
    s8 = SUBLANES

    def body(x_ref, tgt_ref, p_ref, wout_ref, wa_ref, wi_ref, sp_ref, oc_ref, ol_ref,
             h_ref, dh_ref, dhb_ref, acc_ref, yc, czs, u, pa, pi,
             rcf, rlf, ybuf, tail_z, tail_xl, hcar):
        i = pl.program_id(0)
        row = _row_iota(d)

        @pl.when(i == 0)
        def _():
            tail_z[...] = jnp.zeros_like(tail_z)
            tail_xl[...] = jnp.zeros_like(tail_xl)
            hcar[...] = jnp.zeros_like(hcar)
            acc_ref[...] = jnp.zeros_like(acc_ref)

        def spr(r):
            return sp_ref[r:r + 1, :]

        def proj(rows, seg):
            return p_ref[rows, seg * d:(seg + 1) * d].astype(F32)

        w0, w1, w2 = spr(SP_CONV_W), spr(SP_CONV_W + 1), spr(SP_CONV_W + 2)
        l0, l1, l2, l3 = spr(SP_LRU_W), spr(SP_LRU_W + 1), spr(SP_LRU_W + 2), spr(SP_LRU_W + 3)
        lb = spr(SP_LRU_B)

        def convs(r, carry):
            zp, xp = carry
            rows16 = pl.ds(r, 2 * s8)
            bg16, xl16 = proj(rows16, P_B), proj(rows16, P_XL)
            z16 = proj(rows16, P_C) * proj(rows16, P_XC)
            for j in range(2):
                rows, sub = pl.ds(r + j * s8, s8), slice(j * s8, (j + 1) * s8)
                z, xl = z16[sub], xl16[sub]
                cz = w0 * _shift_down(z, zp, 2, row) + w1 * _shift_down(z, zp, 1, row) + w2 * z
                czs[rows, :] = cz
                yc[rows, :] = bg16[sub] * cz
                u[rows, :] = (l0 * _shift_down(xl, xp, 3, row) + l1 * _shift_down(xl, xp, 2, row)
                              + l2 * _shift_down(xl, xp, 1, row) + l3 * xl + lb)
                zp, xp = z, xl
            return zp, xp

        z_last, xl_last = _chunks(tb, 2 * s8, convs, (tail_z[...], tail_xl[...]))
        tail_z[...] = z_last
        tail_xl[...] = xl_last

        ub = u[...].astype(MM)
        for k in range(n_tiles):
            sl = slice(k * tw, (k + 1) * tw)
            pa[:, sl] = _dot(ub[:, sl], wa_ref[k])
            pi[:, sl] = _dot(ub[:, sl], wi_ref[k])
        rcf[...] = _head_rstd(yc[...], oc_ref[...], hd_c)

        c8 = RG_LRU_C * _log_sigmoid(spr(SP_LAM))
        b_a, b_i = spr(SP_B_A), spr(SP_B_I)

        def lru(r, hp):
            rows = pl.ds(r, SUBLANES)
            ra = _sigmoid(pa[rows, :] + b_a)
            ii = _sigmoid(pi[rows, :] + b_i)
            pa[rows, :] = ra
            pi[rows, :] = ii
            la = ra * c8
            a = jnp.exp(la)
            mult = jnp.sqrt(_lru_input_scale_sq(la, a))
            h = _scan_fwd(a, mult * (ii * u[rows, :]), hp, row)
            h_ref[rows, :] = h
            return _bcast_row(h, SUBLANES - 1)

        hcar[...] = _chunks(tb, SUBLANES, lru, hcar[...])
        rlf[...] = _head_rstd(h_ref[...], ol_ref[...], hd_l)

        g_c, g_l = spr(SP_CONV_G), spr(SP_LRU_G)

        def gate(r, carry):
            rows = pl.ds(r, 2 * s8)
            gc, gl = proj(rows, P_GC), proj(rows, P_GL)
            ybuf[rows, 0:d] = (yc[rows, :] * rcf[rows, :] * g_c * (gc * _sigmoid(gc))).astype(MM)
            ybuf[rows, d:2 * d] = (h_ref[rows, :] * rlf[rows, :] * g_l * (gl * _sigmoid(gl))).astype(MM)
            return carry

        _chunks(tb, 2 * s8, gate, 0)

        hres = x_ref[...] + _dot(ybuf[...], wout_ref[...])
        rf = lax.rsqrt(jnp.mean(hres * hres, axis=-1, keepdims=True) + RMS_EPS)
        hn = hres * rf
        fg = spr(SP_FINAL_G)
        err = hn * fg - tgt_ref[...]
        dout = err * (1.0 / d)
        acc_ref[0:SUBLANES, :] += (err * err).reshape(tb // SUBLANES, SUBLANES, d).sum(axis=0)
        acc_ref[SUBLANES:2 * SUBLANES, :] += (dout * hn).reshape(tb // SUBLANES, SUBLANES, d).sum(axis=0)
        gd = dout * fg
        dhres = rf * (gd - hn * jnp.mean(gd * hn, axis=-1, keepdims=True))
        dh_ref[...] = dhres
        dhb_ref[...] = dhres.astype(MM)

    vm = pl.BlockSpec(memory_space=pltpu.VMEM)
    blk = lambda w: pl.BlockSpec((tb, w), lambda i: (i, 0))
    buf = pltpu.VMEM((tb, d), F32)
    car = pltpu.VMEM((SUBLANES, d), F32)
    return pl.pallas_call(
        body, name="forward", grid=(nb,),
        in_specs=[blk(d), blk(d), blk(6 * d), vm, vm, vm, vm, vm, vm],
        out_specs=(blk(d), blk(d), blk(d), pl.BlockSpec((2 * SUBLANES, d), lambda i: (0, 0))) + (blk(d),) * 5,
        out_shape=(jax.ShapeDtypeStruct((t_len, d), F32),
                   jax.ShapeDtypeStruct((t_len, d), F32),
                   jax.ShapeDtypeStruct((t_len, d), MM),
                   jax.ShapeDtypeStruct((2 * SUBLANES, d), F32))
                  + (jax.ShapeDtypeStruct((t_len, d), F32),) * 5,
        scratch_shapes=[buf] * 2 + [pltpu.VMEM((tb, 2 * d), MM), car, car, car],
        compiler_params=_params(dimension_semantics=("arbitrary",)),
    )(x, tgt, p, wout, wa_t, wi_t, sp, ones_c, ones_l)


def _backward(p, h, dh, saved, facc, wout, wa_t, wi_t, sp, ones_c, ones_l, tb):
    t_len, d = h.shape
    nb = t_len // tb
    n_tiles, tw = wa_t.shape[0], wa_t.shape[1]
    hd_c, hd_l = d // N_CONV_HEADS, d // N_LRU_HEADS
    g_rows = 2 * n_tiles * hd_l
    s8 = SUBLANES

    def body(p_ref, h_ref, hhalo_ref, dh_ref, yc, czs, u, ra_ref, ii_ref, facc_ref,
             wout_ref, wa_ref, wi_ref, sp_ref, oc_ref, ol_ref,
             dp_ref, yt_ref, slab_v, slab_g,
             hh, dy, ybuf, rcf, rlf, qc, ql, dyc_hat, dyl_hat, dpa, dpi, du, gwa_ref, gwi_ref, acc_ref,
             car_dcz, car_a, car_g, car_du):
        i = pl.program_id(0)
        blk_idx = nb - 1 - i
        row = _row_iota(d)

        @pl.when(i == 0)
        def _():
            for ref in (car_dcz, car_a, car_g, car_du, gwa_ref, gwi_ref, acc_ref):
                ref[...] = jnp.zeros_like(ref)

        def spr(r):
            return sp_ref[r:r + 1, :]

        def proj(rows, seg):
            return p_ref[rows, seg * d:(seg + 1) * d].astype(F32)

        def put(rows, seg, halves):
            dp_ref[rows, seg * d:(seg + 1) * d] = jnp.concatenate(halves, axis=0).astype(MM)

        def acc_add(group, val):
            acc_ref[group * s8:(group + 1) * s8, :] += val

        live = jnp.where(blk_idx > 0, 1.0, 0.0).astype(F32)
        hh[0:s8, :] = hhalo_ref[...] * live
        hh[s8:, :] = h_ref[...]

        dy[...] = _dot_nt(dh_ref[...].astype(MM), wout_ref[...])

        w0, w1, w2 = spr(SP_CONV_W), spr(SP_CONV_W + 1), spr(SP_CONV_W + 2)
        l0, l1, l2, l3 = spr(SP_LRU_W), spr(SP_LRU_W + 1), spr(SP_LRU_W + 2), spr(SP_LRU_W + 3)

        rcf[...] = _head_rstd(yc[...], oc_ref[...], hd_c)
        rlf[...] = _head_rstd(h_ref[...], ol_ref[...], hd_l)

        g_c, g_l = spr(SP_CONV_G), spr(SP_LRU_G)

        def gates(r, carry):
            rows = pl.ds(r, 2 * s8)
            for (seg, off_y, src, rstd, gain, q, dhat, grp) in (
                    (P_GC, 0, yc, rcf, g_c, qc, dyc_hat, A_CONV_G),
                    (P_GL, d, h_ref, rlf, g_l, ql, dyl_hat, A_LRU_G)):
                gt = proj(rows, seg)
                sg = _sigmoid(gt)
                silu = gt * sg
                yhat = src[rows, :] * rstd[rows, :]
                nrm = yhat * gain
                ybuf[rows, off_y:off_y + d] = nrm * silu
                dout = dy[rows, off_y:off_y + d]
                dnrm = dout * silu
                dp_ref[rows, seg * d:(seg + 1) * d] = (dout * nrm * (sg * (1.0 + gt * (1.0 - sg)))).astype(MM)
                dg = dnrm * yhat
                acc_add(grp, dg[0:s8] + dg[s8:])
                dh_ = dnrm * gain
                dhat[rows, :] = dh_
                q[rows, :] = dh_ * yhat
            return carry

        _chunks(tb, 2 * s8, gates, 0)

        qc[...] = _head_sums(qc[...], oc_ref[...]) * (1.0 / hd_c)
        ql[...] = _head_sums(ql[...], ol_ref[...]) * (1.0 / hd_l)
        yt_ref[...] = ybuf[...].T.astype(MM)

        c8 = RG_LRU_C * _log_sigmoid(spr(SP_LAM))

        def conv_mixer(r, dcz_n):
            rows16 = pl.ds(r, 2 * s8)
            bg16, cg16, xc16 = proj(rows16, P_B), proj(rows16, P_C), proj(rows16, P_XC)
            z16 = cg16 * xc16
            d_b, d_c, d_x = [None, None], [None, None], [None, None]
            for j in (1, 0):
                rows, sub = pl.ds(r + j * s8, s8), slice(j * s8, (j + 1) * s8)
                rstd = rcf[rows, :]
                yhat = yc[rows, :] * rstd
                dyc = rstd * (dyc_hat[rows, :] - yhat * qc[rows, :])
                d_b[j] = dyc * czs[rows, :]
                dcz = dyc * bg16[sub]
                up1, up2 = _shift_up(dcz, dcz_n, 1, row), _shift_up(dcz, dcz_n, 2, row)
                dz = w2 * dcz + w1 * up1 + w0 * up2
                d_c[j] = dz * xc16[sub]
                d_x[j] = dz * cg16[sub]
                z = z16[sub]
                acc_add(A_CONV_W, up2 * z)
                acc_add(A_CONV_W + 1, up1 * z)
                acc_add(A_CONV_W + 2, dcz * z)
                dcz_n = dcz
            put(rows16, P_B, d_b)
            put(rows16, P_C, d_c)
            put(rows16, P_XC, d_x)
            return dcz_n

        car_dcz[...] = _chunks(tb, 2 * s8, conv_mixer, car_dcz[...], reverse=True)

        def lru_mixer(r, carry):
            a_n, g_n = carry
            for j in (1, 0):
                rows = pl.ds(r + j * s8, s8)
                rstd = rlf[rows, :]
                hcur = hh[pl.ds(r + (j + 1) * s8, s8), :]
                hhat = hcur * rstd
                dh_out = rstd * (dyl_hat[rows, :] - hhat * ql[rows, :])
                ra = ra_ref[rows, :]
                la = ra * c8
                a = jnp.exp(la)
                g = _scan_bwd(_shift_up(a, a_n, 1, row), dh_out, g_n, row)
                da = g * _shift_down(hcur, hh[pl.ds(r + j * s8, s8), :], 1, row)
                ii = ii_ref[rows, :]
                uu = u[rows, :]
                mult_sq = _lru_input_scale_sq(la, a)
                inv_mult = lax.rsqrt(mult_sq)
                dmult = g * (ii * uu)
                ds = g * (mult_sq * inv_mult)
                dla = a * (da - dmult * a * inv_mult)
                acc_add(A_LAM, dla * ra)
                dpa_ = dla * c8 * ra * (1.0 - ra)
                dpi_ = ds * uu * ii * (1.0 - ii)
                acc_add(A_B_A, dpa_)
                acc_add(A_B_I, dpi_)
                dpa[rows, :] = dpa_
                dpi[rows, :] = dpi_
                du[rows, :] = ds * ii
                a_n, g_n = a, _bcast_row(g, 0)
            return a_n, g_n

        a_f, g_f = _chunks(tb, 2 * s8, lru_mixer, (car_a[...], car_g[...]), reverse=True)
        car_a[...] = a_f
        car_g[...] = g_f

        dpab = dpa[...].astype(MM)
        dpib = dpi[...].astype(MM)
        for k in range(n_tiles):
            sl = slice(k * tw, (k + 1) * tw)
            du[:, sl] += _dot_nt(dpab[:, sl], wa_ref[k]) + _dot_nt(dpib[:, sl], wi_ref[k])
            ut = u[:, sl].T.astype(MM)
            gwa_ref[k] += _dot(ut, dpab[:, sl])
            gwi_ref[k] += _dot(ut, dpib[:, sl])

        def lru_conv(r, du_n):
            rows16 = pl.ds(r, 2 * s8)
            xl16 = proj(rows16, P_XL)
            d_xl = [None, None]
            for j in (1, 0):
                rows, sub = pl.ds(r + j * s8, s8), slice(j * s8, (j + 1) * s8)
                dut = du[rows, :]
                up1, up2, up3 = (_shift_up(dut, du_n, s, row) for s in (1, 2, 3))
                d_xl[j] = l3 * dut + l2 * up1 + l1 * up2 + l0 * up3
                xl = xl16[sub]
                acc_add(A_LRU_W, up3 * xl)
                acc_add(A_LRU_W + 1, up2 * xl)
                acc_add(A_LRU_W + 2, up1 * xl)
                acc_add(A_LRU_W + 3, dut * xl)
                acc_add(A_LRU_B, dut)
                du_n = dut
            put(rows16, P_XL, d_xl)
            return du_n

        car_du[...] = _chunks(tb, 2 * s8, lru_conv, car_du[...], reverse=True)

        @pl.when(i == nb - 1)
        def _():
            def rowsum(ref, group):
                return jnp.sum(ref[group * s8:(group + 1) * s8, :], axis=0, keepdims=True)

            slab_v[...] = jnp.zeros_like(slab_v)
            loss = jnp.sum(rowsum(facc_ref, 0), axis=1, keepdims=True) * (0.5 / d)
            rows = {SL_LOSS: jnp.broadcast_to(loss, (1, d)), SL_FINAL_G: rowsum(facc_ref, 1),
                    SL_LRU_B: rowsum(acc_ref, A_LRU_B), SL_B_A: rowsum(acc_ref, A_B_A), SL_B_I: rowsum(acc_ref, A_B_I),
                    SL_LAM: rowsum(acc_ref, A_LAM), SL_CONV_G: rowsum(acc_ref, A_CONV_G), SL_LRU_G: rowsum(acc_ref, A_LRU_G)}
            for k in range(3):
                rows[SL_CONV_W + k] = rowsum(acc_ref, A_CONV_W + k)
            for k in range(4):
                rows[SL_LRU_W + k] = rowsum(acc_ref, A_LRU_W + k)
            for r, val in rows.items():
                slab_v[r:r + 1, :] = val
            head_of_lane = lax.broadcasted_iota(jnp.int32, (hd_l, tw), 1) // hd_l
            for mtx, g_ref in enumerate((gwa_ref, gwi_ref)):
                for k in range(n_tiles):
                    packed = jnp.zeros((hd_l, tw), F32)
                    for a in range(tw // hd_l):
                        packed = jnp.where(head_of_lane == a, g_ref[k, a * hd_l:(a + 1) * hd_l, :], packed)
                    slab_g[(mtx * n_tiles + k) * hd_l:(mtx * n_tiles + k + 1) * hd_l, :] = packed.astype(MM)

    vm = pl.BlockSpec(memory_space=pltpu.VMEM)
    rev = lambda w: pl.BlockSpec((tb, w), lambda i: (nb - 1 - i, 0))
    halo = lambda rows, w: pl.BlockSpec((rows, w), lambda i: (jnp.maximum((nb - 1 - i) * (tb // rows) - 1, 0), 0))
    const = lambda shape: pl.BlockSpec(shape, lambda i: (0,) * len(shape))
    buf = lambda w: pltpu.VMEM((tb, w), F32)
    car = pltpu.VMEM((SUBLANES, d), F32)
    return pl.pallas_call(
        body, name="backward", grid=(nb,),
        in_specs=[rev(6 * d), rev(d), halo(SUBLANES, d), rev(d)] + [rev(d)] * 5 + [vm, vm, vm, vm, vm, vm, vm],
        out_specs=(rev(6 * d), pl.BlockSpec((2 * d, tb), lambda i: (0, nb - 1 - i)),
                   const((SL_ROWS, d)), const((g_rows, tw))),
        out_shape=(jax.ShapeDtypeStruct((t_len, 6 * d), MM),
                   jax.ShapeDtypeStruct((2 * d, t_len), MM),
                   jax.ShapeDtypeStruct((SL_ROWS, d), F32),
                   jax.ShapeDtypeStruct((g_rows, tw), MM)),
        scratch_shapes=[pltpu.VMEM((SUBLANES + tb, d), F32), buf(2 * d), buf(2 * d)] + [buf(d)] * 9
                       + [pltpu.VMEM((n_tiles, tw, tw), F32), pltpu.VMEM((n_tiles, tw, tw), F32),
                          pltpu.VMEM((A_GROUPS * SUBLANES, d), F32), car, car, car, car],
        compiler_params=_params(dimension_semantics=("arbitrary",)),
    )(p, h, h, dh, *saved, facc, wout, wa_t, wi_t, sp, ones_c, ones_l)


def _input_grad(dp, win_all, x, dh, sp, part, tb):
    t_len, d = x.shape
    nb = t_len // tb
    cols = win_all.shape[2]
    mid = min(nb - 1, (5 * nb) // 8)
    rc = 32

    def body(dp_ref, win_ref, x_ref, dh_ref, sp_ref, part_ref, gx_ref, ln_ref, direct, passing, relayed,
             send_sems, recv_sems, local_sems, acc_ref, ln_all, ln_send, ln_recv, mine, theirs):
        i = pl.program_id(0)
        x_, y_, c_ = _mesh_pos()
        first, second = 1 - c_, c_
        nbr1 = (x_ ^ c_, y_ ^ (1 - c_), c_)
        nbr2 = (x_ ^ (1 - c_), y_ ^ c_, c_)

        def remote(src, dst, k, to):
            return pltpu.make_async_remote_copy(src_ref=src, dst_ref=dst, send_sem=send_sems.at[k], recv_sem=recv_sems.at[k],
                                                device_id=to, device_id_type=MESH)

        to_first = [remote(part_ref.at[first], direct, 0, nbr1), remote(part_ref.at[2], passing, 1, nbr1)]
        to_second = remote(theirs, relayed, 2, nbr2)

        @pl.when(i == 0)
        def _():
            acc_ref[...] = jnp.zeros_like(acc_ref)
            for cp in to_first:
                cp.start()

        dxn = _dot_nt(dp_ref[:, 0:cols], win_ref[0])
        for j in range(1, N_DEV):
            dxn += _dot_nt(dp_ref[:, j * cols:(j + 1) * cols], win_ref[j])
        xv = x_ref[...]
        r0 = lax.rsqrt(jnp.mean(xv * xv, axis=-1, keepdims=True) + RMS_EPS)
        xhat = xv * r0
        acc_ref[...] += (dxn * xhat).reshape(tb // SUBLANES, SUBLANES, d).sum(axis=0)
        dxh = dxn * sp_ref[SP_LN_G:SP_LN_G + 1, :]
        gx_ref[...] = dh_ref[...] + r0 * (dxh - xhat * jnp.mean(dxh * xhat, axis=-1, keepdims=True))

        @pl.when(i == mid)
        def _():
            to_first[1].wait_recv()
            loads = [pltpu.make_async_copy(part_ref.at[second], mine, local_sems.at[0]),
                     pltpu.make_async_copy(passing, theirs, local_sems.at[1])]
            for cp in loads:
                cp.start()
            for cp in loads:
                cp.wait()

            def add(r, carry):
                rows = pl.ds(r, rc)
                theirs[rows, :] = (mine[rows, :].astype(F32) + theirs[rows, :].astype(F32)).astype(MM)
                return carry

            _chunks(mine.shape[0], rc, add, 0)
            to_second.start()

        @pl.when(i == nb - 1)
        def _():
            to_first[0].wait_recv()
            to_second.wait_recv()
            for cp in to_first + [to_second]:
                cp.wait_send()
            ln_all[4 * x_ + 2 * y_ + c_] = jnp.broadcast_to(jnp.sum(acc_ref[...], axis=0, keepdims=True), acc_ref.shape)
            gather = _Gather(lambda a, px, py, pc: ln_all.at[4 * px + 2 * py + pc], ln_send, ln_recv)
            gather.start_own(0)
            gather.finish(0)
            total = ln_all[0]
            for dev in range(1, N_DEV):
                total = total + ln_all[dev]
            ln_ref[...] = total

    vm = pl.BlockSpec(memory_space=pltpu.VMEM)
    hbm = pl.BlockSpec(memory_space=pl.ANY)
    blk = lambda w: pl.BlockSpec((tb, w), lambda i: (i, 0))
    landed = jax.ShapeDtypeStruct(part.shape[1:], part.dtype)
    outs = pl.pallas_call(
        body, name="input_grad", grid=(nb,),
        in_specs=[blk(6 * d), vm, blk(d), blk(d), vm, hbm],
        out_specs=(blk(d), pl.BlockSpec((SUBLANES, d), lambda i: (0, 0)), hbm, hbm, hbm),
        out_shape=(jax.ShapeDtypeStruct((t_len, d), F32), jax.ShapeDtypeStruct((SUBLANES, d), F32), landed, landed, landed),
        scratch_shapes=[pltpu.SemaphoreType.DMA((3,)), pltpu.SemaphoreType.DMA((3,)), pltpu.SemaphoreType.DMA((2,)),
                        pltpu.VMEM((SUBLANES, d), F32), pltpu.VMEM((N_DEV, SUBLANES, d), F32),
                        pltpu.SemaphoreType.DMA((7,)), pltpu.SemaphoreType.DMA((7,)),
                        pltpu.VMEM(part.shape[1:], MM), pltpu.VMEM(part.shape[1:], MM)],
        compiler_params=_params(dimension_semantics=("arbitrary",)),
    )(dp, win_all, x, dh, sp, part)
    return outs[0], outs[1], (outs[2], outs[4])


_CHIP_RELATIONS = [(0, 0), (1, 0), (0, 1), (1, 1)]


def _related_block(k, core):
    x, y, _ = _mesh_pos()
    fx, fy = _CHIP_RELATIONS[k]
    return 4 * (x ^ fx) + 2 * (y ^ fy) + core


class _ChipExchange:
    def __init__(self, part_refs, land_refs, send_sems, recv_sems):
        self.part_refs, self.land_refs, self.send_sems, self.recv_sems = part_refs, land_refs, send_sems, recv_sems

    def copies(self):
        x, y, c = _mesh_pos()
        for a in range(len(self.part_refs)):
            for k in (1, 2, 3):
                fx, fy = _CHIP_RELATIONS[k]
                yield pltpu.make_async_remote_copy(
                    src_ref=self.part_refs[a].at[k - 1], dst_ref=self.land_refs[a].at[k - 1],
                    send_sem=self.send_sems.at[3 * a + k - 1], recv_sem=self.recv_sems.at[3 * a + k - 1],
                    device_id=(x ^ fx, y ^ fy, c), device_id_type=MESH)

    def start(self):
        for cp in self.copies():
            cp.start()

    def finish(self):
        for cp in self.copies():
            cp.wait_recv()
        for cp in self.copies():
            cp.wait_send()


def _weight_grad_stage1(name, blk_shape, n_split, operands, in_specs, product, riders=(), slabs=()):
    n_rows, n_cols = blk_shape
    rs = n_rows // n_split
    rc = 32
    n_in, n_ride, n_slab = len(operands), len(riders), len(slabs)
    _, _, c = _mesh_pos()
    order = jnp.stack([_related_block(k, 1 - c) for k in range(4)]
                      + [_related_block(k, c) for k in (1, 2, 3, 0)]).astype(jnp.int32)

    def body(order_ref, *refs):
        ins = refs[:n_in]
        ride_in = refs[n_in:n_in + n_ride]
        slab_in = refs[n_in + n_ride:n_in + n_ride + n_slab]
        n_op = n_in + n_ride + n_slab
        part_ref, own_ref = refs[n_op:n_op + 2]
        ride_out = refs[n_op + 2:n_op + 2 + n_ride]
        gathered = refs[n_op + 2 + n_ride:n_op + 2 + n_ride + n_slab]
        (gbuf, sendbuf, from_sib, send_sems, recv_sems, ride_send, ride_recv,
         slab_send, slab_recv, slab_local) = refs[n_op + 2 + n_ride + n_slab:]
        exchange = _ChipExchange(ride_in, ride_out, ride_send, ride_recv)
        s = pl.program_id(0)
        x, y, c = _mesh_pos()
        me = 4 * x + 2 * y + c
        gather = _BalancedGather(lambda a, px, py, pc: gathered[a].at[4 * px + 2 * py + pc], slab_send, slab_recv, slab_in)
        keep_own = [pltpu.make_async_copy(slab_in[a], gathered[a].at[me], slab_local.at[a]) for a in range(n_slab)]

        def to_sibling(k):
            return pltpu.make_async_remote_copy(
                src_ref=sendbuf.at[k], dst_ref=from_sib.at[k], send_sem=send_sems.at[k], recv_sem=recv_sems.at[k],
                device_id=(x, y, 1 - c), device_id_type=MESH)

        @pl.when(s == 0)
        def _():
            exchange.start()
            for a in range(n_slab):
                gather.start_own(a)
                keep_own[a].start()

        @pl.when(s == 5)
        def _():
            for a in range(n_slab):
                gather.on_neighbour(a, 0)
                gather.on_neighbour(a, 1)

        @pl.when(s == 7)
        def _():
            for a in range(n_slab):
                gather.on_diagonal(a)

        for h in range(n_split):
            gbuf[h * rs:(h + 1) * rs, :] = product(ins, h)

        @pl.when(s < 4)
        def _():
            def narrow(r, carry):
                sendbuf[s, pl.ds(r, rc), :] = gbuf[pl.ds(r, rc), :].astype(MM)
                return carry

            _chunks(n_rows, rc, narrow, 0)
            to_sibling(s).start()

        @pl.when(s >= 4)
        def _():
            k = jnp.where(s == 7, 0, s - 3)
            to_sibling(k).wait_recv()

            @pl.when(s < 7)
            def _():
                def add(r, carry):
                    rows = pl.ds(r, rc)
                    part_ref[0, rows, :] = (gbuf[rows, :] + from_sib[k, rows, :].astype(F32)).astype(MM)
                    return carry

                _chunks(n_rows, rc, add, 0)

            @pl.when(s == 7)
            def _():
                def add(r, carry):
                    rows = pl.ds(r, rc)
                    own_ref[rows, :] = gbuf[rows, :] + from_sib[0, rows, :].astype(F32)
                    return carry

                _chunks(n_rows, rc, add, 0)
                for kk in range(4):
                    to_sibling(kk).wait_send()
                exchange.finish()
                for a in range(n_slab):
                    gather.wait_sibling(a)
                    for j in range(3):
                        gather.wait_passed_on(a, j)
                    gather.wait_sends(a)
                    keep_own[a].wait()

    hbm = pl.BlockSpec(memory_space=pl.ANY)
    grid_spec = pltpu.PrefetchScalarGridSpec(
        num_scalar_prefetch=1, grid=(N_DEV,), in_specs=list(in_specs) + [hbm] * (n_ride + n_slab),
        out_specs=(pl.BlockSpec((1, n_rows, n_cols), lambda s, o: (jnp.clip(s - 4, 0, 2), 0, 0)),
                   pl.BlockSpec((n_rows, n_cols), lambda s, o: (0, 0))) + (hbm,) * (n_ride + n_slab),
        scratch_shapes=[pltpu.VMEM((n_rows, n_cols), F32), pltpu.VMEM((4, n_rows, n_cols), MM),
                        pltpu.VMEM((4, n_rows, n_cols), MM),
                        pltpu.SemaphoreType.DMA((4,)), pltpu.SemaphoreType.DMA((4,)),
                        pltpu.SemaphoreType.DMA((max(3 * n_ride, 1),)), pltpu.SemaphoreType.DMA((max(3 * n_ride, 1),)),
                        pltpu.SemaphoreType.DMA((max(8 * n_slab, 1),)), pltpu.SemaphoreType.DMA((max(8 * n_slab, 1),)),
                        pltpu.SemaphoreType.DMA((max(n_slab, 1),))])
    outs = pl.pallas_call(
        body, name=name, grid_spec=grid_spec,
        out_shape=(jax.ShapeDtypeStruct((3, n_rows, n_cols), MM), jax.ShapeDtypeStruct((n_rows, n_cols), F32))
                  + tuple(jax.ShapeDtypeStruct(p.shape, p.dtype) for p in riders)
                  + tuple(jax.ShapeDtypeStruct((N_DEV,) + a.shape, a.dtype) for a in slabs),
        compiler_params=_params(dimension_semantics=("arbitrary",)),
    )(order, *operands, *riders, *slabs)
    return outs[0], outs[1], outs[2:2 + n_ride], outs[2 + n_ride:]


def _weight_grad_in(xnt, dp, riders, slabs):
    d, t_len = xnt.shape
    cols = dp.shape[1] // N_DEV
    half = d // 2
    return _weight_grad_stage1(
        "weight_grad_in", (d, cols), 2, (xnt, dp),
        [pl.BlockSpec(memory_space=pltpu.VMEM), pl.BlockSpec((t_len, cols), lambda s, o: (0, o[s]))],
        lambda refs, h: _dot(refs[0][h * half:(h + 1) * half, :], refs[1][...]), riders, slabs)


def _weight_grad_out(yt, dhb):
    d2, t_len = yt.shape
    d = dhb.shape[1]
    rows = d2 // N_DEV
    return _weight_grad_stage1(
        "weight_grad_out", (rows, d), 1, (yt, dhb),
        [pl.BlockSpec((rows, t_len), lambda s, o: (o[s], 0)), pl.BlockSpec(memory_space=pltpu.VMEM)],
        lambda refs, h: _dot(refs[0][...], refs[1][...]))


def _update_shard(own, others, w, m, v, name):
    n_rows, n_cols = w.shape
    rb = min(256, n_rows)
    n_other = len(others)

    def body(own_ref, *refs):
        other_refs = refs[:n_other]
        w_ref, m_ref, v_ref, grad_ref, delta_ref, mo_ref, vo_ref = refs[n_other:]
        g = own_ref[...]
        for ref in other_refs:
            for k in range(ref.shape[0] if len(ref.shape) == 3 else 1):
                g = g + (ref[k] if len(ref.shape) == 3 else ref[...]).astype(F32)
        delta, m_new, v_new = _adamw(w_ref[...], g, m_ref[...], v_ref[...])
        grad_ref[...] = g
        delta_ref[...] = delta
        mo_ref[...] = m_new
        vo_ref[...] = v_new

    blk = pl.BlockSpec((rb, n_cols), lambda i: (i, 0))
    stacked = lambda n: pl.BlockSpec((n, rb, n_cols), lambda i: (0, i, 0))
    out = jax.ShapeDtypeStruct((n_rows, n_cols), F32)
    return pl.pallas_call(
        body, name=name, grid=(n_rows // rb,),
        in_specs=[blk] + [stacked(o.shape[0]) if o.ndim == 3 else blk for o in others] + [blk, blk, blk],
        out_specs=(blk, blk, blk, blk), out_shape=(out, out, out, out),
        compiler_params=_params(dimension_semantics=("arbitrary",)),
    )(own, *others, w, m, v)


def _small_update(gat_v, gat_g, ln_tot, vec_w, vec_m, vec_v, gates, convs):
    n_vec = len(vec_w)
    n_heads, hd, _ = gates[0].shape
    tw = gat_g.shape[2]
    s8 = SUBLANES
    per = tw // hd
    n_tiles = n_heads // per
    cc = convs[0].shape[1]
    n_in = 3 + 3 * n_vec + 12

    def body(*refs):
        gv_ref, gg_ref, ln_ref = refs[:3]
        w_refs, m_refs, v_refs = (refs[3 + j * n_vec:3 + (j + 1) * n_vec] for j in range(3))
        gate_refs = refs[3 + 3 * n_vec:3 + 3 * n_vec + 6]
        conv_refs = refs[3 + 3 * n_vec + 6:n_in]
        loss_o = refs[n_in]
        kinds = [refs[n_in + 1 + j * (n_vec + 4):n_in + 1 + (j + 1) * (n_vec + 4)] for j in range(4)]
        tv, tg = refs[n_in + 1 + 4 * (n_vec + 4):]
        x, y, c = _mesh_pos()
        me = 4 * x + 2 * y + c

        def emit(k_out, w, g, m, v):
            delta, m_new, v_new = _adamw(w, g, m, v)
            for ref, val in zip(k_out, (g, delta, m_new, v_new)):
                ref[...] = val

        total = gv_ref[0]
        for dev in range(1, N_DEV):
            total = total + gv_ref[dev]
        tv[...] = total
        tv[SL_LN_G:SL_LN_G + 1, :] = ln_ref[0:1, :]

        def sum_gates(r, carry):
            rows = pl.ds(r, 2 * s8)
            part = gg_ref[0, rows, :].astype(F32)
            for dev in range(1, N_DEV):
                part = part + gg_ref[dev, rows, :].astype(F32)
            tg[rows, :] = part
            return carry

        _chunks(tg.shape[0], 2 * s8, sum_gates, 0)
        loss_o[...] = jnp.broadcast_to(tv[SL_LOSS:SL_LOSS + 1, 0:LANES], loss_o.shape)
        for p in range(n_vec):
            w, g = w_refs[p][...], tv[SL_LN_G + p, :]
            if SL_LN_G + p == SL_LAM:
                g = g * (RG_LRU_C * jax.nn.sigmoid(-w))
            emit([k_out[p] for k_out in kinds], w, g, m_refs[p][...], v_refs[p][...])
        lanes = pl.ds(pl.multiple_of(me * cc, cc), cc)
        for j, (row0, n) in enumerate(((SL_CONV_W, 3), (SL_LRU_W, 4))):
            w_ref, m_ref, v_ref = conv_refs[3 * j:3 * j + 3]
            emit([k_out[n_vec + 2 + j] for k_out in kinds], w_ref[...], tv[row0:row0 + n, lanes], m_ref[...], v_ref[...])
        for mtx in range(2):
            w_ref, m_ref, v_ref = gate_refs[3 * mtx:3 * mtx + 3]
            for k in range(n_tiles):
                tile = tg[(mtx * n_tiles + k) * hd:(mtx * n_tiles + k + 1) * hd, :]
                for a in range(per):
                    head = k * per + a
                    g = tile[:, a * hd:(a + 1) * hd]
                    delta, m_new, v_new = _adamw(w_ref[head], g, m_ref[head], v_ref[head])
                    for k_out, val in zip(kinds, (g, delta, m_new, v_new)):
                        k_out[n_vec + mtx][head] = val

    vm = pl.BlockSpec(memory_space=pltpu.VMEM)
    like = lambda a: jax.ShapeDtypeStruct(a.shape, F32)
    per_kind = tuple(like(a) for a in vec_w) + (like(gates[0]), like(gates[3]), like(convs[0]), like(convs[3]))
    n_out = 1 + 4 * len(per_kind)
    outs = pl.pallas_call(
        body, name="small_update",
        in_specs=[vm] * n_in, out_specs=(vm,) * n_out,
        out_shape=(jax.ShapeDtypeStruct((SUBLANES, LANES), F32),) + per_kind * 4,
        scratch_shapes=[pltpu.VMEM(gat_v.shape[1:], F32), pltpu.VMEM(gat_g.shape[1:], F32)],
        compiler_params=_params(),
    )(gat_v, gat_g, ln_tot, *vec_w, *vec_m, *vec_v, *gates, *convs)
    return outs[0], [outs[1 + j * len(per_kind):1 + (j + 1) * len(per_kind)] for j in range(4)]


def _head_ones(head_dim, tw):
    lane = jnp.arange(tw) // head_dim
    return (lane[:, None] == lane[None, :]).astype(MM)


def kernel(x, ln_g, w_in, conv_w, lru_conv_w, lru_conv_b, w_a, b_a, w_i, b_i, lam, conv_out_g, lru_out_g, w_out, final_g, loss_target, m_ln_g, m_w_in, m_conv_w, m_lru_conv_w, m_lru_conv_b, m_w_a, m_b_a, m_w_i, m_b_i, m_lam, m_conv_out_g, m_lru_out_g, m_w_out, m_final_g, v_ln_g, v_w_in, v_conv_w, v_lru_conv_w, v_lru_conv_b, v_w_a, v_b_a, v_w_i, v_b_i, v_lam, v_conv_out_g, v_lru_out_g, v_w_out, v_final_g):
    _, t_len, d = x.shape
    hd_l = d // N_LRU_HEADS
    tw = min(MXU_TILE, d)
    x2, tgt2 = x[0], loss_target[0]

    small = [ln_g, lru_conv_b, b_a, b_i, lam, conv_out_g, lru_out_g, final_g]
    p, xnt, win_all, wout_all, _, sp, wa_t, wi_t = _gather_project(
        x2, w_in, w_out, conv_w, lru_conv_w, ln_g.reshape(1, d), w_a, w_i, small, min(256, t_len), tw)
    wout_full = wout_all.reshape(N_DEV * w_out.shape[0], d)
    ones_c, ones_l = _head_ones(d // N_CONV_HEADS, tw), _head_ones(hd_l, tw)

    h, dh, dhb, facc, *saved = _forward(x2, tgt2, p, wout_full, wa_t, wi_t, sp, ones_c, ones_l, min(256, t_len))
    dp, yt, slab_v, slab_g = _backward(p, h, dh, saved, facc, wout_full, wa_t, wi_t, sp, ones_c, ones_l, min(256, t_len))
    part_out, own_out, _, _ = _weight_grad_out(yt, dhb)
    part_in, own_in, (chips_out,), (gat_v, gat_g) = _weight_grad_in(xnt, dp, (part_out,), (slab_v, slab_g))
    grad_x, ln_tot, sums_in = _input_grad(dp, win_all, x2, dh, sp, part_in, min(512, t_len))
    gw_in, dw_in, mw_in, vw_in = _update_shard(own_in, sums_in, w_in, m_w_in, v_w_in, "update_w_in")
    gw_out, dw_out, mw_out, vw_out = _update_shard(own_out, (chips_out,), w_out, m_w_out, v_w_out, "update_w_out")

    loss_tile, kinds = _small_update(
        gat_v, gat_g, ln_tot, small,
        [m_ln_g, m_lru_conv_b, m_b_a, m_b_i, m_lam, m_conv_out_g, m_lru_out_g, m_final_g],
        [v_ln_g, v_lru_conv_b, v_b_a, v_b_i, v_lam, v_conv_out_g, v_lru_out_g, v_final_g],
        (w_a, m_w_a, v_w_a, w_i, m_w_i, v_w_i), (conv_w, m_conv_w, v_conv_w, lru_conv_w, m_lru_conv_w, v_lru_conv_w))

    def unpack(kind, big_in, big_out):
        vec, (wa_, wi_, cw_, lw_) = kind[:len(small)], kind[len(small):]
        return [vec[0], big_in, cw_, lw_, vec[1], wa_, vec[2], wi_, vec[3], vec[4], vec[5], vec[6], big_out, vec[7]]

    return (loss_tile[0, 0], grad_x[None], *unpack(kinds[0], gw_in, gw_out), *unpack(kinds[1], dw_in, dw_out),
            *unpack(kinds[2], mw_in, mw_out), *unpack(kinds[3], vw_in, vw_out))
```

```python
import functools

import jax
import jax.numpy as jnp
from jax import lax
from jax.experimental import pallas as pl
from jax.experimental.pallas import tpu as pltpu

F32 = jnp.float32
MM = jnp.bfloat16
MESH = pl.DeviceIdType.MESH

N_DEV = 8
N_CONV_HEADS = 8
N_LRU_HEADS = 16
RG_LRU_C = 8.0
RMS_EPS = 1e-6
ADAM_LR, ADAM_B1, ADAM_B2, ADAM_EPS, ADAM_WD, ADAM_STEP = 0.001, 0.9, 0.999, 1e-08, 0.01, 10
ADAM_BC1 = 1.0 - ADAM_B1 ** ADAM_STEP
ADAM_BC2 = 1.0 - ADAM_B2 ** ADAM_STEP

SUBLANES = 8
LANES = 128
MXU_TILE = 256
VMEM_LIMIT = 56 * 1024 * 1024

SP_LN_G, SP_LRU_B, SP_B_A, SP_B_I, SP_LAM, SP_CONV_G, SP_LRU_G, SP_FINAL_G, SP_CONV_W, SP_LRU_W = 0, 1, 2, 3, 4, 5, 6, 7, 8, 11
SP_ROWS = 16
P_B, P_C, P_XC, P_GC, P_XL, P_GL = 0, 1, 2, 3, 4, 5
A_CONV_G, A_LRU_G, A_LAM, A_B_A, A_B_I, A_CONV_W, A_LRU_W, A_LRU_B = 0, 1, 2, 3, 4, 5, 8, 12
A_GROUPS = 13
SL_LOSS, SL_LN_G, SL_LRU_B, SL_B_A, SL_B_I, SL_LAM, SL_CONV_G, SL_LRU_G, SL_FINAL_G, SL_CONV_W, SL_LRU_W = 0, 1, 2, 3, 4, 5, 6, 7, 8, 16, 24
SL_ROWS = 32


def _params(vmem=True, **kw):
    if vmem:
        kw["vmem_limit_bytes"] = VMEM_LIMIT
    return pltpu.CompilerParams(**kw)


def _dot(a, b):
    return jnp.dot(a, b, preferred_element_type=F32)


def _dot_nt(a, b):
    return lax.dot_general(a, b, (((1,), (1,)), ((), ())), preferred_element_type=F32)


def _head_sums(v, ones_tile):
    tw = ones_tile.shape[0]
    vb = v.astype(MM)
    return jnp.concatenate([_dot(vb[:, k:k + tw], ones_tile) for k in range(0, v.shape[1], tw)], axis=1)


def _head_rstd(v, ones_tile, head_dim):
    return lax.rsqrt(_head_sums(v * v, ones_tile) * (1.0 / head_dim) + RMS_EPS)


def _sigmoid(x):
    return 0.5 * jnp.tanh(0.5 * x) + 0.5


def _lru_input_scale_sq(log_a, a):
    return -jnp.tanh(log_a) * (1.0 + a * a)


def _log_sigmoid(x):
    z = jnp.exp(-jnp.abs(x))
    u = 1.0 + z
    log1p_z = jnp.where(u == 1.0, z, jnp.log(u) * (z / (u - 1.0)))
    return jnp.minimum(x, 0.0) - log1p_z


def _row_iota(d):
    return lax.broadcasted_iota(jnp.int32, (SUBLANES, d), 0)


def _shift_down(cur, prev, s, row):
    return jnp.where(row >= s, pltpu.roll(cur, s, axis=0), pltpu.roll(prev, s, axis=0))


def _shift_up(cur, nxt, s, row):
    k = SUBLANES - s
    return jnp.where(row < k, pltpu.roll(cur, k, axis=0), pltpu.roll(nxt, k, axis=0))


def _scan_fwd(a, b, h_prev, row):
    for s in (1, 2, 4):
        a_s = jnp.where(row >= s, pltpu.roll(a, s, axis=0), 1.0)
        b_s = jnp.where(row >= s, pltpu.roll(b, s, axis=0), 0.0)
        b = a * b_s + b
        a = a * a_s
    return a * h_prev + b


def _scan_bwd(a_next, b, g_next, row):
    a = a_next
    for s in (1, 2, 4):
        k = SUBLANES - s
        a_s = jnp.where(row < k, pltpu.roll(a, k, axis=0), 1.0)
        b_s = jnp.where(row < k, pltpu.roll(b, k, axis=0), 0.0)
        b = a * b_s + b
        a = a * a_s
    return a * g_next + b


def _bcast_row(v, r):
    return jnp.broadcast_to(v[r:r + 1, :], v.shape)


def _chunks(n_rows, rc, body, init, reverse=False):
    n = n_rows // rc

    def step(i, carry):
        j = (n - 1 - i) if reverse else i
        return body(pl.multiple_of(j * rc, rc), carry)

    return lax.fori_loop(0, n, step, init)


def _adamw(w, g, m, v):
    m = ADAM_B1 * m + (1.0 - ADAM_B1) * g
    v = ADAM_B2 * v + (1.0 - ADAM_B2) * (g * g)
    m_hat = m / ADAM_BC1
    v_hat = v / ADAM_BC2
    delta = -ADAM_LR * (m_hat / (jnp.sqrt(v_hat) + ADAM_EPS) + ADAM_WD * w)
    return delta, m, v


def _mesh_pos():
    return lax.axis_index("x"), lax.axis_index("y"), lax.axis_index("c")


class _Gather:
    def __init__(self, blocks_of, send_sems, recv_sems, own_src=None):
        x, y, c = _mesh_pos()
        self.c = c
        self.me, self.sibling = (x, y, c), (x, y, 1 - c)
        self.chips = [(1 - x, y), (x, 1 - y), (1 - x, 1 - y)]
        self.blocks_of, self.send_sems, self.recv_sems = blocks_of, send_sems, recv_sems
        self.own_src = own_src

    def copy(self, a, k, block, to):
        src = self.blocks_of(a, *block)
        if block is self.me and self.own_src is not None:
            src = self.own_src[a]
        return pltpu.make_async_remote_copy(
            src_ref=src, dst_ref=self.blocks_of(a, *block),
            send_sem=self.send_sems.at[a * 7 + k], recv_sem=self.recv_sems.at[a * 7 + k],
            device_id=to, device_id_type=MESH)

    def start_own(self, a):
        self.copy(a, 0, self.me, self.sibling).start()
        for j, chip in enumerate(self.chips):
            self.copy(a, 1 + j, self.me, (*chip, self.c)).start()

    def wait_sibling(self, a):
        self.copy(a, 0, self.sibling, self.me).wait_recv()

    def wait_chip_and_pass_on(self, a, j):
        block = (*self.chips[j], self.c)
        self.copy(a, 1 + j, block, self.me).wait_recv()
        self.copy(a, 4 + j, block, self.sibling).start()

    def wait_passed_on(self, a, j):
        self.copy(a, 4 + j, (*self.chips[j], 1 - self.c), self.me).wait_recv()

    def wait_sends(self, a):
        self.copy(a, 0, self.me, self.sibling).wait_send()
        for j, chip in enumerate(self.chips):
            self.copy(a, 1 + j, self.me, (*chip, self.c)).wait_send()
            self.copy(a, 4 + j, (*chip, self.c), self.sibling).wait_send()

    def finish(self, a):
        for j in range(3):
            self.wait_chip_and_pass_on(a, j)
        self.wait_sibling(a)
        for j in range(3):
            self.wait_passed_on(a, j)
        self.wait_sends(a)


class _BalancedGather:
    def __init__(self, slot, send_sems, recv_sems, own_src):
        x, y, c = _mesh_pos()
        self.c = c
        self.me, self.sibling = (x, y, c), (x, y, 1 - c)
        self.chips = [(1 - x, y), (x, 1 - y), (1 - x, 1 - y)]
        self.slot, self.send_sems, self.recv_sems, self.own_src = slot, send_sems, recv_sems, own_src

    def half(self, a, block, which):
        ref = self.slot(a, *block)
        n = ref.shape[0] // 2
        return ref.at[pl.ds(which * n, n)]

    def copy(self, a, k, src, dst, to):
        return pltpu.make_async_remote_copy(
            src_ref=src, dst_ref=dst, send_sem=self.send_sems.at[a * 8 + k], recv_sem=self.recv_sems.at[a * 8 + k],
            device_id=to, device_id_type=MESH)

    def whole(self, a, k, block, to):
        src = self.own_src[a] if block is self.me else self.slot(a, *block)
        return self.copy(a, k, src, self.slot(a, *block), to)

    def halved(self, a, k, block, which, to):
        return self.copy(a, k, self.half(a, block, which), self.half(a, block, which), to)

    def on(self, chip):
        return (*self.chips[chip], self.c)

    def start_own(self, a):
        self.whole(a, 0, self.me, self.sibling).start()
        self.whole(a, 1, self.me, self.on(0)).start()
        self.whole(a, 2, self.me, self.on(1)).start()

    def wait_sibling(self, a):
        self.whole(a, 0, self.sibling, self.me).wait_recv()

    def on_neighbour(self, a, j):
        self.whole(a, 1 + j, self.on(j), self.me).wait_recv()
        self.halved(a, 3 + j, self.on(j), j, self.on(1 - j)).start()
        self.whole(a, 5 + j, self.on(j), self.sibling).start()

    def on_diagonal(self, a):
        self.halved(a, 3, self.on(2), 0, self.me).wait_recv()
        self.halved(a, 4, self.on(2), 1, self.me).wait_recv()
        self.whole(a, 7, self.on(2), self.sibling).start()

    def wait_passed_on(self, a, j):
        self.whole(a, 5 + j, (*self.chips[j], 1 - self.c), self.me).wait_recv()

    def wait_sends(self, a):
        self.whole(a, 0, self.me, self.sibling).wait_send()
        for j in range(2):
            self.whole(a, 1 + j, self.me, self.on(j)).wait_send()
            self.halved(a, 3 + j, self.on(j), j, self.on(1 - j)).wait_send()
        for j in range(3):
            self.whole(a, 5 + j, self.on(j), self.sibling).wait_send()


def _chip_order():
    x, y, _ = _mesh_pos()
    return jnp.stack([2 * px + py for px, py in ((x, y), (1 - x, y), (x, 1 - y), (1 - x, 1 - y))]).astype(jnp.int32)


def _gather_project(x, w_in, w_out, conv_w, lru_conv_w, ln_g, w_a, w_i, vecs, tb, tw):
    t_len, d = x.shape
    nb = t_len // tb
    cols = w_in.shape[1]
    n_chips = N_DEV // 2
    n_row = 2 if t_len % 2048 == 0 else 1
    rows_per = t_len // n_row
    n_steps = n_chips * n_row
    mc = min(512, rows_per)
    conv_pack = jax.ShapeDtypeStruct((SUBLANES, conv_w.shape[1]), F32)
    n_vec = len(vecs)
    n_heads, hd, _ = w_a.shape
    per = tw // hd

    def body(order_ref, x_ref, win_ref, wout_ref, cw_ref, lw_ref, lng_ref, wa_ref, wi_ref, *refs):
        vec_refs = refs[:n_vec]
        (p_ref, xnt_ref, win_all, wout_all, cp_all, sp_ref, wat_ref, wit_ref,
         xnb, wall, st_out, st_cp, cp_vm, send_sems, recv_sems, cp_send, cp_recv, local_sems) = refs[n_vec:]
        i = pl.program_id(0)
        x_, y_, c_ = _mesh_pos()
        me = 4 * x_ + 2 * y_ + c_

        def slot(a, px, py, pc):
            if a:
                return wout_all.at[4 * px + 2 * py + pc]
            return wall.at[2 * px + py, :, pl.ds(pl.multiple_of(pc * cols, cols), cols)]

        stages = (slot(0, x_, y_, c_), st_out)
        gather = _BalancedGather(slot, send_sems, recv_sems, stages)
        small = _Gather(lambda a, px, py, pc: cp_all.at[4 * px + 2 * py + pc], cp_send, cp_recv, own_src=[st_cp])
        keep_own = [pltpu.make_async_copy(st_out, wout_all.at[me], local_sems.at[0]),
                    pltpu.make_async_copy(st_cp, cp_all.at[me], local_sems.at[1])]

        def keep(j):
            chip = order_ref[j]
            return pltpu.make_async_copy(wall.at[chip], win_all.at[chip], local_sems.at[2 + j])

        @pl.when(i == 0)
        def _():
            rc = 32
            own = stages[0]

            def cast(r, carry):
                own[pl.ds(r, rc), :] = win_ref[pl.ds(r, rc), :].astype(MM)
                return carry

            _chunks(d, rc, cast, 0)

            def cast_out(r, carry):
                st_out[pl.ds(r, rc), :] = wout_ref[pl.ds(r, rc), :].astype(MM)
                return carry

            _chunks(wout_ref.shape[0], rc, cast_out, 0)
            n_cw, n_lw = cw_ref.shape[0], lw_ref.shape[0]
            st_cp[...] = jnp.zeros_like(st_cp)
            st_cp[0:n_cw, :] = cw_ref[...]
            st_cp[n_cw:n_cw + n_lw, :] = lw_ref[...]
            gather.start_own(0)
            for cp in keep_own:
                cp.start()

        @pl.when(i < nb)
        def _():
            xv = x_ref[...]
            r0 = lax.rsqrt(jnp.mean(xv * xv, axis=-1, keepdims=True) + RMS_EPS)
            xn = xv * r0 * lng_ref[...]
            xnb[pl.ds(pl.multiple_of(i * tb, tb), tb), :] = xn.astype(MM)
            xnt_ref[...] = xn.T.astype(MM)

        before = {0: [functools.partial(gather.wait_sibling, 0)],
                  1: [functools.partial(gather.on_neighbour, 0, 0), functools.partial(gather.on_neighbour, 0, 1),
                      functools.partial(gather.wait_passed_on, 0, 0),
                      functools.partial(gather.start_own, 1), functools.partial(small.start_own, 0)],
                  2: [functools.partial(gather.wait_passed_on, 0, 1),
                      functools.partial(gather.on_neighbour, 1, 0), functools.partial(gather.on_neighbour, 1, 1)],
                  3: [functools.partial(gather.on_diagonal, 0), functools.partial(gather.wait_passed_on, 0, 2),
                      functools.partial(gather.on_diagonal, 1)]}

        for k in range(n_steps):
            @pl.when(i == nb + k)
            def _(k=k):
                j, part = k // n_row, k % n_row
                if part == 0:
                    for action in before[j]:
                        action()
                    keep(j).start()
                chip = order_ref[j]

                def project(r, carry):
                    p_ref[pl.ds(r, mc), :] = _dot(xnb[pl.ds(part * rows_per + r, mc), :], wall[chip]).astype(MM)
                    return carry

                _chunks(rows_per, mc, project, 0)
                if k == n_steps - 1:
                    gather.wait_sends(0)
                    gather.wait_sibling(1)
                    for jj in range(3):
                        gather.wait_passed_on(1, jj)
                    gather.wait_sends(1)
                    small.finish(0)
                    for cp in keep_own + [keep(jj) for jj in range(n_chips)]:
                        cp.wait()
                    load = pltpu.make_async_copy(cp_all, cp_vm, local_sems.at[2 + n_chips])
                    load.start()
                    load.wait()
                    for r, ref in enumerate(vec_refs):
                        sp_ref[r, :] = ref[...]
                    sp_ref[n_vec:n_vec + SUBLANES, :] = jnp.concatenate([cp_vm[dev] for dev in range(N_DEV)], axis=1)
                    for src, dst in ((wa_ref, wat_ref), (wi_ref, wit_ref)):
                        dst[...] = jnp.zeros_like(dst)
                        for head in range(n_heads):
                            lo = (head % per) * hd
                            dst[head // per, lo:lo + hd, lo:lo + hd] = src[head].astype(MM)

    vm = pl.BlockSpec(memory_space=pltpu.VMEM)
    hbm = pl.BlockSpec(memory_space=pl.ANY)

    def p_index(i, o):
        k = jnp.maximum(i - nb, 0)
        return k % n_row, o[k // n_row]

    grid_spec = pltpu.PrefetchScalarGridSpec(
        num_scalar_prefetch=1, grid=(nb + n_steps,),
        in_specs=[pl.BlockSpec((tb, d), lambda i, o: (jnp.minimum(i, nb - 1), 0))] + [vm] * (7 + n_vec),
        out_specs=(pl.BlockSpec((rows_per, 2 * cols), p_index),
                   pl.BlockSpec((d, tb), lambda i, o: (0, jnp.minimum(i, nb - 1))), hbm, hbm, hbm,
                   pl.BlockSpec((SP_ROWS, d), lambda i, o: (0, 0)),
                   pl.BlockSpec((n_heads // per, tw, tw), lambda i, o: (0, 0, 0)),
                   pl.BlockSpec((n_heads // per, tw, tw), lambda i, o: (0, 0, 0))),
        scratch_shapes=[pltpu.VMEM((t_len, d), MM), pltpu.VMEM((n_chips, d, 2 * cols), MM),
                        pltpu.VMEM(w_out.shape, MM), pltpu.VMEM(conv_pack.shape, F32),
                        pltpu.VMEM((N_DEV,) + conv_pack.shape, F32),
                        pltpu.SemaphoreType.DMA((16,)), pltpu.SemaphoreType.DMA((16,)),
                        pltpu.SemaphoreType.DMA((7,)), pltpu.SemaphoreType.DMA((7,)), pltpu.SemaphoreType.DMA((n_chips + 3,))])
    return pl.pallas_call(
        body, name="gather_project", grid_spec=grid_spec,
        out_shape=(jax.ShapeDtypeStruct((t_len, N_DEV * cols), MM),
                   jax.ShapeDtypeStruct((d, t_len), MM),
                   jax.ShapeDtypeStruct((n_chips, d, 2 * cols), MM),
                   jax.ShapeDtypeStruct((N_DEV,) + w_out.shape, MM),
                   jax.ShapeDtypeStruct((N_DEV,) + conv_pack.shape, F32),
                   jax.ShapeDtypeStruct((SP_ROWS, d), F32))
                  + (jax.ShapeDtypeStruct((n_heads // per, tw, tw), MM),) * 2,
        compiler_params=_params(dimension_semantics=("arbitrary",)),
    )(_chip_order(), x, w_in, w_out, conv_w, lru_conv_w, ln_g, w_a, w_i, *vecs)


def _forward(x, tgt, p, wout, wa_t, wi_t, sp, ones_c, ones_l, tb):
    t_len, d = x.shape
    nb = t_len // tb
    n_tiles, tw = wa_t.shape[0], wa_t.shape[1]
    hd_c, hd_l = d // N_CONV_HEADS, d // N_LRU_HEADS
    s8 = SUBLANES

    def body(x_ref, tgt_ref, p_ref, wout_ref, wa_ref, wi_ref, sp_ref, oc_ref, ol_ref,
             h_ref, dh_ref, dhb_ref, acc_ref, yc, czs, u, pa, pi,
             rcf, rlf, ybuf, tail_z, tail_xl, hcar):
        i = pl.program_id(0)
        row = _row_iota(d)

        @pl.when(i == 0)
        def _():
            tail_z[...] = jnp.zeros_like(tail_z)
            tail_xl[...] = jnp.zeros_like(tail_xl)
            hcar[...] = jnp.zeros_like(hcar)
            acc_ref[...] = jnp.zeros_like(acc_ref)

        def spr(r):
            return sp_ref[r:r + 1, :]

        def proj(rows, seg):
            return p_ref[rows, seg * d:(seg + 1) * d].astype(F32)

        w0, w1, w2 = spr(SP_CONV_W), spr(SP_CONV_W + 1), spr(SP_CONV_W + 2)
        l0, l1, l2, l3 = spr(SP_LRU_W), spr(SP_LRU_W + 1), spr(SP_LRU_W + 2), spr(SP_LRU_W + 3)
        lb = spr(SP_LRU_B)

        def convs(r, carry):
            zp, xp = carry
            rows16 = pl.ds(r, 2 * s8)
            bg16, xl16 = proj(rows16, P_B), proj(rows16, P_XL)
            z16 = proj(rows16, P_C) * proj(rows16, P_XC)
            for j in range(2):
                rows, sub = pl.ds(r + j * s8, s8), slice(j * s8, (j + 1) * s8)
                z, xl = z16[sub], xl16[sub]
                cz = w0 * _shift_down(z, zp, 2, row) + w1 * _shift_down(z, zp, 1, row) + w2 * z
                czs[rows, :] = cz
                yc[rows, :] = bg16[sub] * cz
                u[rows, :] = (l0 * _shift_down(xl, xp, 3, row) + l1 * _shift_down(xl, xp, 2, row)
                              + l2 * _shift_down(xl, xp, 1, row) + l3 * xl + lb)
                zp, xp = z, xl
            return zp, xp

        z_last, xl_last = _chunks(tb, 2 * s8, convs, (tail_z[...], tail_xl[...]))
        tail_z[...] = z_last
        tail_xl[...] = xl_last

        ub = u[...].astype(MM)
        for k in range(n_tiles):
            sl = slice(k * tw, (k + 1) * tw)
            pa[:, sl] = _dot(ub[:, sl], wa_ref[k])
            pi[:, sl] = _dot(ub[:, sl], wi_ref[k])
        rcf[...] = _head_rstd(yc[...], oc_ref[...], hd_c)

        c8 = RG_LRU_C * _log_sigmoid(spr(SP_LAM))
        b_a, b_i = spr(SP_B_A), spr(SP_B_I)

        def lru(r, hp):
            rows = pl.ds(r, SUBLANES)
            ra = _sigmoid(pa[rows, :] + b_a)
            ii = _sigmoid(pi[rows, :] + b_i)
            pa[rows, :] = ra
            pi[rows, :] = ii
            la = ra * c8
            a = jnp.exp(la)
            mult = jnp.sqrt(_lru_input_scale_sq(la, a))
            h = _scan_fwd(a, mult * (ii * u[rows, :]), hp, row)
            h_ref[rows, :] = h
            return _bcast_row(h, SUBLANES - 1)

        hcar[...] = _chunks(tb, SUBLANES, lru, hcar[...])
        rlf[...] = _head_rstd(h_ref[...], ol_ref[...], hd_l)

        g_c, g_l = spr(SP_CONV_G), spr(SP_LRU_G)

        def gate(r, carry):
            rows = pl.ds(r, 2 * s8)
            gc, gl = proj(rows, P_GC), proj(rows, P_GL)
            ybuf[rows, 0:d] = (yc[rows, :] * rcf[rows, :] * g_c * (gc * _sigmoid(gc))).astype(MM)
            ybuf[rows, d:2 * d] = (h_ref[rows, :] * rlf[rows, :] * g_l * (gl * _sigmoid(gl))).astype(MM)
            return carry

        _chunks(tb, 2 * s8, gate, 0)

        hres = x_ref[...] + _dot(ybuf[...], wout_ref[...])
        rf = lax.rsqrt(jnp.mean(hres * hres, axis=-1, keepdims=True) + RMS_EPS)
        hn = hres * rf
        fg = spr(SP_FINAL_G)
        err = hn * fg - tgt_ref[...]
        dout = err * (1.0 / d)
        acc_ref[0:SUBLANES, :] += (err * err).reshape(tb // SUBLANES, SUBLANES, d).sum(axis=0)
        acc_ref[SUBLANES:2 * SUBLANES, :] += (dout * hn).reshape(tb // SUBLANES, SUBLANES, d).sum(axis=0)
        gd = dout * fg
        dhres = rf * (gd - hn * jnp.mean(gd * hn, axis=-1, keepdims=True))
        dh_ref[...] = dhres
        dhb_ref[...] = dhres.astype(MM)

    vm = pl.BlockSpec(memory_space=pltpu.VMEM)
    blk = lambda w: pl.BlockSpec((tb, w), lambda i: (i, 0))
    buf = pltpu.VMEM((tb, d), F32)
    car = pltpu.VMEM((SUBLANES, d), F32)
    return pl.pallas_call(
        body, name="forward", grid=(nb,),
        in_specs=[blk(d), blk(d), blk(6 * d), vm, vm, vm, vm, vm, vm],
        out_specs=(blk(d), blk(d), blk(d), pl.BlockSpec((2 * SUBLANES, d), lambda i: (0, 0))) + (blk(d),) * 5,
        out_shape=(jax.ShapeDtypeStruct((t_len, d), F32),
                   jax.ShapeDtypeStruct((t_len, d), F32),
                   jax.ShapeDtypeStruct((t_len, d), MM),
                   jax.ShapeDtypeStruct((2 * SUBLANES, d), F32))
                  + (jax.ShapeDtypeStruct((t_len, d), F32),) * 5,
        scratch_shapes=[buf] * 2 + [pltpu.VMEM((tb, 2 * d), MM), car, car, car],
        compiler_params=_params(dimension_semantics=("arbitrary",)),
    )(x, tgt, p, wout, wa_t, wi_t, sp, ones_c, ones_l)


def _backward(p, h, dh, saved, facc, wout, wa_t, wi_t, sp, ones_c, ones_l, tb):
    t_len, d = h.shape
    nb = t_len // tb
    n_tiles, tw = wa_t.shape[0], wa_t.shape[1]
    hd_c, hd_l = d // N_CONV_HEADS, d // N_LRU_HEADS
    g_rows = 2 * n_tiles * hd_l
    s8 = SUBLANES

    def body(p_ref, h_ref, hhalo_ref, dh_ref, yc, czs, u, ra_ref, ii_ref, facc_ref,
             wout_ref, wa_ref, wi_ref, sp_ref, oc_ref, ol_ref,
             dp_ref, yt_ref, slab_v, slab_g,
             hh, dy, ybuf, rcf, rlf, qc, ql, dyc_hat, dyl_hat, dpa, dpi, du, gwa_ref, gwi_ref, acc_ref,
             car_dcz, car_a, car_g, car_du):
        i = pl.program_id(0)
        blk_idx = nb - 1 - i
        row = _row_iota(d)

        @pl.when(i == 0)
        def _():
            for ref in (car_dcz, car_a, car_g, car_du, gwa_ref, gwi_ref, acc_ref):
                ref[...] = jnp.zeros_like(ref)

        def spr(r):
            return sp_ref[r:r + 1, :]

        def proj(rows, seg):
            return p_ref[rows, seg * d:(seg + 1) * d].astype(F32)

        def put(rows, seg, halves):
            dp_ref[rows, seg * d:(seg + 1) * d] = jnp.concatenate(halves, axis=0).astype(MM)

        def acc_add(group, val):
            acc_ref[group * s8:(group + 1) * s8, :] += val

        live = jnp.where(blk_idx > 0, 1.0, 0.0).astype(F32)
        hh[0:s8, :] = hhalo_ref[...] * live
        hh[s8:, :] = h_ref[...]

        dy[...] = _dot_nt(dh_ref[...].astype(MM), wout_ref[...])

        w0, w1, w2 = spr(SP_CONV_W), spr(SP_CONV_W + 1), spr(SP_CONV_W + 2)
        l0, l1, l2, l3 = spr(SP_LRU_W), spr(SP_LRU_W + 1), spr(SP_LRU_W + 2), spr(SP_LRU_W + 3)

        rcf[...] = _head_rstd(yc[...], oc_ref[...], hd_c)
        rlf[...] = _head_rstd(h_ref[...], ol_ref[...], hd_l)

        g_c, g_l = spr(SP_CONV_G), spr(SP_LRU_G)

        def gates(r, carry):
            rows = pl.ds(r, 2 * s8)
            for (seg, off_y, src, rstd, gain, q, dhat, grp) in (
                    (P_GC, 0, yc, rcf, g_c, qc, dyc_hat, A_CONV_G),
                    (P_GL, d, h_ref, rlf, g_l, ql, dyl_hat, A_LRU_G)):
                gt = proj(rows, seg)
                sg = _sigmoid(gt)
                silu = gt * sg
                yhat = src[rows, :] * rstd[rows, :]
                nrm = yhat * gain
                ybuf[rows, off_y:off_y + d] = nrm * silu
                dout = dy[rows, off_y:off_y + d]
                dnrm = dout * silu
                dp_ref[rows, seg * d:(seg + 1) * d] = (dout * nrm * (sg * (1.0 + gt * (1.0 - sg)))).astype(MM)
                dg = dnrm * yhat
                acc_add(grp, dg[0:s8] + dg[s8:])
                dh_ = dnrm * gain
                dhat[rows, :] = dh_
                q[rows, :] = dh_ * yhat
            return carry

        _chunks(tb, 2 * s8, gates, 0)

        qc[...] = _head_sums(qc[...], oc_ref[...]) * (1.0 / hd_c)
        ql[...] = _head_sums(ql[...], ol_ref[...]) * (1.0 / hd_l)
        yt_ref[...] = ybuf[...].T.astype(MM)

        c8 = RG_LRU_C * _log_sigmoid(spr(SP_LAM))

        def conv_mixer(r, dcz_n):
            rows16 = pl.ds(r, 2 * s8)
            bg16, cg16, xc16 = proj(rows16, P_B), proj(rows16, P_C), proj(rows16, P_XC)
            z16 = cg16 * xc16
            d_b, d_c, d_x = [None, None], [None, None], [None, None]
            for j in (1, 0):
                rows, sub = pl.ds(r + j * s8, s8), slice(j * s8, (j + 1) * s8)
                rstd = rcf[rows, :]
                yhat = yc[rows, :] * rstd
                dyc = rstd * (dyc_hat[rows, :] - yhat * qc[rows, :])
                d_b[j] = dyc * czs[rows, :]
                dcz = dyc * bg16[sub]
                up1, up2 = _shift_up(dcz, dcz_n, 1, row), _shift_up(dcz, dcz_n, 2, row)
                dz = w2 * dcz + w1 * up1 + w0 * up2
                d_c[j] = dz * xc16[sub]
                d_x[j] = dz * cg16[sub]
                z = z16[sub]
                acc_add(A_CONV_W, up2 * z)
                acc_add(A_CONV_W + 1, up1 * z)
                acc_add(A_CONV_W + 2, dcz * z)
                dcz_n = dcz
            put(rows16, P_B, d_b)
            put(rows16, P_C, d_c)
            put(rows16, P_XC, d_x)
            return dcz_n

        car_dcz[...] = _chunks(tb, 2 * s8, conv_mixer, car_dcz[...], reverse=True)

        def lru_mixer(r, carry):
            a_n, g_n = carry
            for j in (1, 0):
                rows = pl.ds(r + j * s8, s8)
                rstd = rlf[rows, :]
                hcur = hh[pl.ds(r + (j + 1) * s8, s8), :]
                hhat = hcur * rstd
                dh_out = rstd * (dyl_hat[rows, :] - hhat * ql[rows, :])
                ra = ra_ref[rows, :]
                la = ra * c8
                a = jnp.exp(la)
                g = _scan_bwd(_shift_up(a, a_n, 1, row), dh_out, g_n, row)
                da = g * _shift_down(hcur, hh[pl.ds(r + j * s8, s8), :], 1, row)
                ii = ii_ref[rows, :]
                uu = u[rows, :]
                mult_sq = _lru_input_scale_sq(la, a)
                inv_mult = lax.rsqrt(mult_sq)
                dmult = g * (ii * uu)
                ds = g * (mult_sq * inv_mult)
                dla = a * (da - dmult * a * inv_mult)
                acc_add(A_LAM, dla * ra)
                dpa_ = dla * c8 * ra * (1.0 - ra)
                dpi_ = ds * uu * ii * (1.0 - ii)
                acc_add(A_B_A, dpa_)
                acc_add(A_B_I, dpi_)
                dpa[rows, :] = dpa_
                dpi[rows, :] = dpi_
                du[rows, :] = ds * ii
                a_n, g_n = a, _bcast_row(g, 0)
            return a_n, g_n

        a_f, g_f = _chunks(tb, 2 * s8, lru_mixer, (car_a[...], car_g[...]), reverse=True)
        car_a[...] = a_f
        car_g[...] = g_f

        dpab = dpa[...].astype(MM)
        dpib = dpi[...].astype(MM)
        for k in range(n_tiles):
            sl = slice(k * tw, (k + 1) * tw)
            du[:, sl] += _dot_nt(dpab[:, sl], wa_ref[k]) + _dot_nt(dpib[:, sl], wi_ref[k])
            ut = u[:, sl].T.astype(MM)
            gwa_ref[k] += _dot(ut, dpab[:, sl])
            gwi_ref[k] += _dot(ut, dpib[:, sl])

        def lru_conv(r, du_n):
            rows16 = pl.ds(r, 2 * s8)
            xl16 = proj(rows16, P_XL)
            d_xl = [None, None]
            for j in (1, 0):
                rows, sub = pl.ds(r + j * s8, s8), slice(j * s8, (j + 1) * s8)
                dut = du[rows, :]
                up1, up2, up3 = (_shift_up(dut, du_n, s, row) for s in (1, 2, 3))
                d_xl[j] = l3 * dut + l2 * up1 + l1 * up2 + l0 * up3
                xl = xl16[sub]
                acc_add(A_LRU_W, up3 * xl)
                acc_add(A_LRU_W + 1, up2 * xl)
                acc_add(A_LRU_W + 2, up1 * xl)
                acc_add(A_LRU_W + 3, dut * xl)
                acc_add(A_LRU_B, dut)
                du_n = dut
            put(rows16, P_XL, d_xl)
            return du_n

        car_du[...] = _chunks(tb, 2 * s8, lru_conv, car_du[...], reverse=True)

        @pl.when(i == nb - 1)
        def _():
            def rowsum(ref, group):
                return jnp.sum(ref[group * s8:(group + 1) * s8, :], axis=0, keepdims=True)

            slab_v[...] = jnp.zeros_like(slab_v)
            loss = jnp.sum(rowsum(facc_ref, 0), axis=1, keepdims=True) * (0.5 / d)
            rows = {SL_LOSS: jnp.broadcast_to(loss, (1, d)), SL_FINAL_G: rowsum(facc_ref, 1),
                    SL_LRU_B: rowsum(acc_ref, A_LRU_B), SL_B_A: rowsum(acc_ref, A_B_A), SL_B_I: rowsum(acc_ref, A_B_I),
                    SL_LAM: rowsum(acc_ref, A_LAM), SL_CONV_G: rowsum(acc_ref, A_CONV_G), SL_LRU_G: rowsum(acc_ref, A_LRU_G)}
            for k in range(3):
                rows[SL_CONV_W + k] = rowsum(acc_ref, A_CONV_W + k)
            for k in range(4):
                rows[SL_LRU_W + k] = rowsum(acc_ref, A_LRU_W + k)
            for r, val in rows.items():
                slab_v[r:r + 1, :] = val
            head_of_lane = lax.broadcasted_iota(jnp.int32, (hd_l, tw), 1) // hd_l
            for mtx, g_ref in enumerate((gwa_ref, gwi_ref)):
                for k in range(n_tiles):
                    packed = jnp.zeros((hd_l, tw), F32)
                    for a in range(tw // hd_l):
                        packed = jnp.where(head_of_lane == a, g_ref[k, a * hd_l:(a + 1) * hd_l, :], packed)
                    slab_g[(mtx * n_tiles + k) * hd_l:(mtx * n_tiles + k + 1) * hd_l, :] = packed.astype(MM)

    vm = pl.BlockSpec(memory_space=pltpu.VMEM)
    rev = lambda w: pl.BlockSpec((tb, w), lambda i: (nb - 1 - i, 0))
    halo = lambda rows, w: pl.BlockSpec((rows, w), lambda i: (jnp.maximum((nb - 1 - i) * (tb // rows) - 1, 0), 0))
    const = lambda shape: pl.BlockSpec(shape, lambda i: (0,) * len(shape))
    buf = lambda w: pltpu.VMEM((tb, w), F32)
    car = pltpu.VMEM((SUBLANES, d), F32)
    return pl.pallas_call(
        body, name="backward", grid=(nb,),
        in_specs=[rev(6 * d), rev(d), halo(SUBLANES, d), rev(d)] + [rev(d)] * 5 + [vm, vm, vm, vm, vm, vm, vm],
        out_specs=(rev(6 * d), pl.BlockSpec((2 * d, tb), lambda i: (0, nb - 1 - i)),
                   const((SL_ROWS, d)), const((g_rows, tw))),
        out_shape=(jax.ShapeDtypeStruct((t_len, 6 * d), MM),
                   jax.ShapeDtypeStruct((2 * d, t_len), MM),
                   jax.ShapeDtypeStruct((SL_ROWS, d), F32),
                   jax.ShapeDtypeStruct((g_rows, tw), MM)),
        scratch_shapes=[pltpu.VMEM((SUBLANES + tb, d), F32), buf(2 * d), buf(2 * d)] + [buf(d)] * 9
                       + [pltpu.VMEM((n_tiles, tw, tw), F32), pltpu.VMEM((n_tiles, tw, tw), F32),
                          pltpu.VMEM((A_GROUPS * SUBLANES, d), F32), car, car, car, car],
        compiler_params=_params(dimension_semantics=("arbitrary",)),
    )(p, h, h, dh, *saved, facc, wout, wa_t, wi_t, sp, ones_c, ones_l)


def _input_grad(dp, win_all, x, dh, sp, part, tb):
    t_len, d = x.shape
    nb = t_len // tb
    pair = win_all.shape[2]
    mid = min(nb - 1, (5 * nb) // 8)
    rc = 32

    def body(dp_ref, win_ref, x_ref, dh_ref, sp_ref, part_ref, gx_ref, ln_ref, direct, passing, relayed,
             send_sems, recv_sems, local_sems, acc_ref, ln_all, ln_send, ln_recv, mine, theirs):
        i = pl.program_id(0)
        x_, y_, c_ = _mesh_pos()
        first, second = 1 - c_, c_
        nbr1 = (x_ ^ c_, y_ ^ (1 - c_), c_)
        nbr2 = (x_ ^ (1 - c_), y_ ^ c_, c_)

        def remote(src, dst, k, to):
            return pltpu.make_async_remote_copy(src_ref=src, dst_ref=dst, send_sem=send_sems.at[k], recv_sem=recv_sems.at[k],
                                                device_id=to, device_id_type=MESH)

        to_first = [remote(part_ref.at[first], direct, 0, nbr1), remote(part_ref.at[2], passing, 1, nbr1)]
        to_second = remote(theirs, relayed, 2, nbr2)

        @pl.when(i == 0)
        def _():
            acc_ref[...] = jnp.zeros_like(acc_ref)
            for cp in to_first:
                cp.start()

        dxn = _dot_nt(dp_ref[:, 0:pair], win_ref[0])
        for j in range(1, win_ref.shape[0]):
            dxn += _dot_nt(dp_ref[:, j * pair:(j + 1) * pair], win_ref[j])
        xv = x_ref[...]
        r0 = lax.rsqrt(jnp.mean(xv * xv, axis=-1, keepdims=True) + RMS_EPS)
        xhat = xv * r0
        acc_ref[...] += (dxn * xhat).reshape(tb // SUBLANES, SUBLANES, d).sum(axis=0)
        dxh = dxn * sp_ref[SP_LN_G:SP_LN_G + 1, :]
        gx_ref[...] = dh_ref[...] + r0 * (dxh - xhat * jnp.mean(dxh * xhat, axis=-1, keepdims=True))

        @pl.when(i == mid)
        def _():
            to_first[1].wait_recv()
            loads = [pltpu.make_async_copy(part_ref.at[second], mine, local_sems.at[0]),
                     pltpu.make_async_copy(passing, theirs, local_sems.at[1])]
            for cp in loads:
                cp.start()
            for cp in loads:
                cp.wait()

            def add(r, carry):
                rows = pl.ds(r, rc)
                theirs[rows, :] = (mine[rows, :].astype(F32) + theirs[rows, :].astype(F32)).astype(MM)
                return carry

            _chunks(mine.shape[0], rc, add, 0)
            to_second.start()

        @pl.when(i == nb - 1)
        def _():
            to_first[0].wait_recv()
            to_second.wait_recv()
            for cp in to_first + [to_second]:
                cp.wait_send()
            ln_all[4 * x_ + 2 * y_ + c_] = jnp.broadcast_to(jnp.sum(acc_ref[...], axis=0, keepdims=True), acc_ref.shape)
            gather = _Gather(lambda a, px, py, pc: ln_all.at[4 * px + 2 * py + pc], ln_send, ln_recv)
            gather.start_own(0)
            gather.finish(0)
            total = ln_all[0]
            for dev in range(1, N_DEV):
                total = total + ln_all[dev]
            ln_ref[...] = total

    vm = pl.BlockSpec(memory_space=pltpu.VMEM)
    hbm = pl.BlockSpec(memory_space=pl.ANY)
    blk = lambda w: pl.BlockSpec((tb, w), lambda i: (i, 0))
    landed = jax.ShapeDtypeStruct(part.shape[1:], part.dtype)
    outs = pl.pallas_call(
        body, name="input_grad", grid=(nb,),
        in_specs=[blk(6 * d), vm, blk(d), blk(d), vm, hbm],
        out_specs=(blk(d), pl.BlockSpec((SUBLANES, d), lambda i: (0, 0)), hbm, hbm, hbm),
        out_shape=(jax.ShapeDtypeStruct((t_len, d), F32), jax.ShapeDtypeStruct((SUBLANES, d), F32), landed, landed, landed),
        scratch_shapes=[pltpu.SemaphoreType.DMA((3,)), pltpu.SemaphoreType.DMA((3,)), pltpu.SemaphoreType.DMA((2,)),
                        pltpu.VMEM((SUBLANES, d), F32), pltpu.VMEM((N_DEV, SUBLANES, d), F32),
                        pltpu.SemaphoreType.DMA((7,)), pltpu.SemaphoreType.DMA((7,)),
                        pltpu.VMEM(part.shape[1:], MM), pltpu.VMEM(part.shape[1:], MM)],
        compiler_params=_params(dimension_semantics=("arbitrary",)),
    )(dp, win_all, x, dh, sp, part)
    return outs[0], outs[1], (outs[2], outs[4])


_CHIP_RELATIONS = [(0, 0), (1, 0), (0, 1), (1, 1)]


def _related_block(k, core):
    x, y, _ = _mesh_pos()
    fx, fy = _CHIP_RELATIONS[k]
    return 4 * (x ^ fx) + 2 * (y ^ fy) + core


class _ChipExchange:
    def __init__(self, part_refs, land_refs, send_sems, recv_sems):
        self.part_refs, self.land_refs, self.send_sems, self.recv_sems = part_refs, land_refs, send_sems, recv_sems

    def copies(self):
        x, y, c = _mesh_pos()
        for a in range(len(self.part_refs)):
            for k in (1, 2, 3):
                fx, fy = _CHIP_RELATIONS[k]
                yield pltpu.make_async_remote_copy(
                    src_ref=self.part_refs[a].at[k - 1], dst_ref=self.land_refs[a].at[k - 1],
                    send_sem=self.send_sems.at[3 * a + k - 1], recv_sem=self.recv_sems.at[3 * a + k - 1],
                    device_id=(x ^ fx, y ^ fy, c), device_id_type=MESH)

    def start(self):
        for cp in self.copies():
            cp.start()

    def finish(self):
        for cp in self.copies():
            cp.wait_recv()
        for cp in self.copies():
            cp.wait_send()


def _weight_grad_stage1(name, blk_shape, n_split, operands, in_specs, product, riders=(), slabs=()):
    n_rows, n_cols = blk_shape
    rs = n_rows // n_split
    rc = 32
    n_in, n_ride, n_slab = len(operands), len(riders), len(slabs)
    _, _, c = _mesh_pos()
    order = jnp.stack([_related_block(k, 1 - c) for k in range(4)]
                      + [_related_block(k, c) for k in (1, 2, 3, 0)]).astype(jnp.int32)

    def body(order_ref, *refs):
        ins = refs[:n_in]
        ride_in = refs[n_in:n_in + n_ride]
        slab_in = refs[n_in + n_ride:n_in + n_ride + n_slab]
        n_op = n_in + n_ride + n_slab
        part_ref, own_ref = refs[n_op:n_op + 2]
        ride_out = refs[n_op + 2:n_op + 2 + n_ride]
        gathered = refs[n_op + 2 + n_ride:n_op + 2 + n_ride + n_slab]
        (gbuf, sendbuf, from_sib, send_sems, recv_sems, ride_send, ride_recv,
         slab_send, slab_recv, slab_local) = refs[n_op + 2 + n_ride + n_slab:]
        exchange = _ChipExchange(ride_in, ride_out, ride_send, ride_recv)
        s = pl.program_id(0)
        x, y, c = _mesh_pos()
        me = 4 * x + 2 * y + c
        gather = _BalancedGather(lambda a, px, py, pc: gathered[a].at[4 * px + 2 * py + pc], slab_send, slab_recv, slab_in)
        keep_own = [pltpu.make_async_copy(slab_in[a], gathered[a].at[me], slab_local.at[a]) for a in range(n_slab)]

        def to_sibling(k):
            return pltpu.make_async_remote_copy(
                src_ref=sendbuf.at[k], dst_ref=from_sib.at[k], send_sem=send_sems.at[k], recv_sem=recv_sems.at[k],
                device_id=(x, y, 1 - c), device_id_type=MESH)

        @pl.when(s == 0)
        def _():
            exchange.start()
            for a in range(n_slab):
                gather.start_own(a)
                keep_own[a].start()

        @pl.when(s == 5)
        def _():
            for a in range(n_slab):
                gather.on_neighbour(a, 0)
                gather.on_neighbour(a, 1)

        @pl.when(s == 7)
        def _():
            for a in range(n_slab):
                gather.on_diagonal(a)

        for h in range(n_split):
            gbuf[h * rs:(h + 1) * rs, :] = product(ins, h)

        @pl.when(s < 4)
        def _():
            def narrow(r, carry):
                sendbuf[s, pl.ds(r, rc), :] = gbuf[pl.ds(r, rc), :].astype(MM)
                return carry

            _chunks(n_rows, rc, narrow, 0)
            to_sibling(s).start()

        @pl.when(s >= 4)
        def _():
            k = jnp.where(s == 7, 0, s - 3)
            to_sibling(k).wait_recv()

            @pl.when(s < 7)
            def _():
                def add(r, carry):
                    rows = pl.ds(r, rc)
                    part_ref[0, rows, :] = (gbuf[rows, :] + from_sib[k, rows, :].astype(F32)).astype(MM)
                    return carry

                _chunks(n_rows, rc, add, 0)

            @pl.when(s == 7)
            def _():
                def add(r, carry):
                    rows = pl.ds(r, rc)
                    own_ref[rows, :] = gbuf[rows, :] + from_sib[0, rows, :].astype(F32)
                    return carry

                _chunks(n_rows, rc, add, 0)
                for kk in range(4):
                    to_sibling(kk).wait_send()
                exchange.finish()
                for a in range(n_slab):
                    gather.wait_sibling(a)
                    for j in range(3):
                        gather.wait_passed_on(a, j)
                    gather.wait_sends(a)
                    keep_own[a].wait()

    hbm = pl.BlockSpec(memory_space=pl.ANY)
    grid_spec = pltpu.PrefetchScalarGridSpec(
        num_scalar_prefetch=1, grid=(N_DEV,), in_specs=list(in_specs) + [hbm] * (n_ride + n_slab),
        out_specs=(pl.BlockSpec((1, n_rows, n_cols), lambda s, o: (jnp.clip(s - 4, 0, 2), 0, 0)),
                   pl.BlockSpec((n_rows, n_cols), lambda s, o: (0, 0))) + (hbm,) * (n_ride + n_slab),
        scratch_shapes=[pltpu.VMEM((n_rows, n_cols), F32), pltpu.VMEM((4, n_rows, n_cols), MM),
                        pltpu.VMEM((4, n_rows, n_cols), MM),
                        pltpu.SemaphoreType.DMA((4,)), pltpu.SemaphoreType.DMA((4,)),
                        pltpu.SemaphoreType.DMA((max(3 * n_ride, 1),)), pltpu.SemaphoreType.DMA((max(3 * n_ride, 1),)),
                        pltpu.SemaphoreType.DMA((max(8 * n_slab, 1),)), pltpu.SemaphoreType.DMA((max(8 * n_slab, 1),)),
                        pltpu.SemaphoreType.DMA((max(n_slab, 1),))])
    outs = pl.pallas_call(
        body, name=name, grid_spec=grid_spec,
        out_shape=(jax.ShapeDtypeStruct((3, n_rows, n_cols), MM), jax.ShapeDtypeStruct((n_rows, n_cols), F32))
                  + tuple(jax.ShapeDtypeStruct(p.shape, p.dtype) for p in riders)
                  + tuple(jax.ShapeDtypeStruct((N_DEV,) + a.shape, a.dtype) for a in slabs),
        compiler_params=_params(dimension_semantics=("arbitrary",)),
    )(order, *operands, *riders, *slabs)
    return outs[0], outs[1], outs[2:2 + n_ride], outs[2 + n_ride:]


def _weight_grad_in(xnt, dp, riders, slabs):
    d, t_len = xnt.shape
    cols = dp.shape[1] // N_DEV
    half = d // 2
    return _weight_grad_stage1(
        "weight_grad_in", (d, cols), 2, (xnt, dp),
        [pl.BlockSpec(memory_space=pltpu.VMEM), pl.BlockSpec((t_len, cols), lambda s, o: (0, o[s]))],
        lambda refs, h: _dot(refs[0][h * half:(h + 1) * half, :], refs[1][...]), riders, slabs)


def _weight_grad_out(yt, dhb):
    d2, t_len = yt.shape
    d = dhb.shape[1]
    rows = d2 // N_DEV
    return _weight_grad_stage1(
        "weight_grad_out", (rows, d), 1, (yt, dhb),
        [pl.BlockSpec((rows, t_len), lambda s, o: (o[s], 0)), pl.BlockSpec(memory_space=pltpu.VMEM)],
        lambda refs, h: _dot(refs[0][...], refs[1][...]))


def _update_shard(own, others, w, m, v, name):
    n_rows, n_cols = w.shape
    rb = min(256, n_rows)
    n_other = len(others)

    def body(own_ref, *refs):
        other_refs = refs[:n_other]
        w_ref, m_ref, v_ref, grad_ref, delta_ref, mo_ref, vo_ref = refs[n_other:]
        g = own_ref[...]
        for ref in other_refs:
            for k in range(ref.shape[0] if len(ref.shape) == 3 else 1):
                g = g + (ref[k] if len(ref.shape) == 3 else ref[...]).astype(F32)
        delta, m_new, v_new = _adamw(w_ref[...], g, m_ref[...], v_ref[...])
        grad_ref[...] = g
        delta_ref[...] = delta
        mo_ref[...] = m_new
        vo_ref[...] = v_new

    blk = pl.BlockSpec((rb, n_cols), lambda i: (i, 0))
    stacked = lambda n: pl.BlockSpec((n, rb, n_cols), lambda i: (0, i, 0))
    out = jax.ShapeDtypeStruct((n_rows, n_cols), F32)
    return pl.pallas_call(
        body, name=name, grid=(n_rows // rb,),
        in_specs=[blk] + [stacked(o.shape[0]) if o.ndim == 3 else blk for o in others] + [blk, blk, blk],
        out_specs=(blk, blk, blk, blk), out_shape=(out, out, out, out),
        compiler_params=_params(dimension_semantics=("arbitrary",)),
    )(own, *others, w, m, v)


def _small_update(gat_v, gat_g, ln_tot, vec_w, vec_m, vec_v, gates, convs):
    n_vec = len(vec_w)
    n_heads, hd, _ = gates[0].shape
    tw = gat_g.shape[2]
    s8 = SUBLANES
    per = tw // hd
    n_tiles = n_heads // per
    cc = convs[0].shape[1]
    n_in = 3 + 3 * n_vec + 12

    def body(*refs):
        gv_ref, gg_ref, ln_ref = refs[:3]
        w_refs, m_refs, v_refs = (refs[3 + j * n_vec:3 + (j + 1) * n_vec] for j in range(3))
        gate_refs = refs[3 + 3 * n_vec:3 + 3 * n_vec + 6]
        conv_refs = refs[3 + 3 * n_vec + 6:n_in]
        loss_o = refs[n_in]
        kinds = [refs[n_in + 1 + j * (n_vec + 4):n_in + 1 + (j + 1) * (n_vec + 4)] for j in range(4)]
        tv, tg = refs[n_in + 1 + 4 * (n_vec + 4):]
        x, y, c = _mesh_pos()
        me = 4 * x + 2 * y + c

        def emit(k_out, w, g, m, v):
            delta, m_new, v_new = _adamw(w, g, m, v)
            for ref, val in zip(k_out, (g, delta, m_new, v_new)):
                ref[...] = val

        total = gv_ref[0]
        for dev in range(1, N_DEV):
            total = total + gv_ref[dev]
        tv[...] = total
        tv[SL_LN_G:SL_LN_G + 1, :] = ln_ref[0:1, :]

        def sum_gates(r, carry):
            rows = pl.ds(r, 2 * s8)
            part = gg_ref[0, rows, :].astype(F32)
            for dev in range(1, N_DEV):
                part = part + gg_ref[dev, rows, :].astype(F32)
            tg[rows, :] = part
            return carry

        _chunks(tg.shape[0], 2 * s8, sum_gates, 0)
        loss_o[...] = jnp.broadcast_to(tv[SL_LOSS:SL_LOSS + 1, 0:LANES], loss_o.shape)
        for p in range(n_vec):
            w, g = w_refs[p][...], tv[SL_LN_G + p, :]
            if SL_LN_G + p == SL_LAM:
                g = g * (RG_LRU_C * jax.nn.sigmoid(-w))
            emit([k_out[p] for k_out in kinds], w, g, m_refs[p][...], v_refs[p][...])
        lanes = pl.ds(pl.multiple_of(me * cc, cc), cc)
        for j, (row0, n) in enumerate(((SL_CONV_W, 3), (SL_LRU_W, 4))):
            w_ref, m_ref, v_ref = conv_refs[3 * j:3 * j + 3]
            emit([k_out[n_vec + 2 + j] for k_out in kinds], w_ref[...], tv[row0:row0 + n, lanes], m_ref[...], v_ref[...])
        for mtx in range(2):
            w_ref, m_ref, v_ref = gate_refs[3 * mtx:3 * mtx + 3]
            for k in range(n_tiles):
                tile = tg[(mtx * n_tiles + k) * hd:(mtx * n_tiles + k + 1) * hd, :]
                for a in range(per):
                    head = k * per + a
                    g = tile[:, a * hd:(a + 1) * hd]
                    delta, m_new, v_new = _adamw(w_ref[head], g, m_ref[head], v_ref[head])
                    for k_out, val in zip(kinds, (g, delta, m_new, v_new)):
                        k_out[n_vec + mtx][head] = val

    vm = pl.BlockSpec(memory_space=pltpu.VMEM)
    like = lambda a: jax.ShapeDtypeStruct(a.shape, F32)
    per_kind = tuple(like(a) for a in vec_w) + (like(gates[0]), like(gates[3]), like(convs[0]), like(convs[3]))
    n_out = 1 + 4 * len(per_kind)
    outs = pl.pallas_call(
        body, name="small_update",
        in_specs=[vm] * n_in, out_specs=(vm,) * n_out,
        out_shape=(jax.ShapeDtypeStruct((SUBLANES, LANES), F32),) + per_kind * 4,
        scratch_shapes=[pltpu.VMEM(gat_v.shape[1:], F32), pltpu.VMEM(gat_g.shape[1:], F32)],
        compiler_params=_params(),
    )(gat_v, gat_g, ln_tot, *vec_w, *vec_m, *vec_v, *gates, *convs)
    return outs[0], [outs[1 + j * len(per_kind):1 + (j + 1) * len(per_kind)] for j in range(4)]


def _head_ones(head_dim, tw):
    lane = jnp.arange(tw) // head_dim
    return (lane[:, None] == lane[None, :]).astype(MM)


def kernel(x, ln_g, w_in, conv_w, lru_conv_w, lru_conv_b, w_a, b_a, w_i, b_i, lam, conv_out_g, lru_out_g, w_out, final_g, loss_target, m_ln_g, m_w_in, m_conv_w, m_lru_conv_w, m_lru_conv_b, m_w_a, m_b_a, m_w_i, m_b_i, m_lam, m_conv_out_g, m_lru_out_g, m_w_out, m_final_g, v_ln_g, v_w_in, v_conv_w, v_lru_conv_w, v_lru_conv_b, v_w_a, v_b_a, v_w_i, v_b_i, v_lam, v_conv_out_g, v_lru_out_g, v_w_out, v_final_g):
    _, t_len, d = x.shape
    hd_l = d // N_LRU_HEADS
    tw = min(MXU_TILE, d)
    x2, tgt2 = x[0], loss_target[0]

    small = [ln_g, lru_conv_b, b_a, b_i, lam, conv_out_g, lru_out_g, final_g]
    p, xnt, win_all, wout_all, _, sp, wa_t, wi_t = _gather_project(
        x2, w_in, w_out, conv_w, lru_conv_w, ln_g.reshape(1, d), w_a, w_i, small, min(256, t_len), tw)
    wout_full = wout_all.reshape(N_DEV * w_out.shape[0], d)
    ones_c, ones_l = _head_ones(d // N_CONV_HEADS, tw), _head_ones(hd_l, tw)

    h, dh, dhb, facc, *saved = _forward(x2, tgt2, p, wout_full, wa_t, wi_t, sp, ones_c, ones_l, min(256, t_len))
    dp, yt, slab_v, slab_g = _backward(p, h, dh, saved, facc, wout_full, wa_t, wi_t, sp, ones_c, ones_l, min(256, t_len))
    part_out, own_out, _, _ = _weight_grad_out(yt, dhb)
    part_in, own_in, (chips_out,), (gat_v, gat_g) = _weight_grad_in(xnt, dp, (part_out,), (slab_v, slab_g))
    grad_x, ln_tot, sums_in = _input_grad(dp, win_all, x2, dh, sp, part_in, min(512, t_len))
    gw_in, dw_in, mw_in, vw_in = _update_shard(own_in, sums_in, w_in, m_w_in, v_w_in, "update_w_in")
    gw_out, dw_out, mw_out, vw_out = _update_shard(own_out, (chips_out,), w_out, m_w_out, v_w_out, "update_w_out")

    loss_tile, kinds = _small_update(
        gat_v, gat_g, ln_tot, small,
        [m_ln_g, m_lru_conv_b, m_b_a, m_b_i, m_lam, m_conv_out_g, m_lru_out_g, m_final_g],
        [v_ln_g, v_lru_conv_b, v_b_a, v_b_i, v_lam, v_conv_out_g, v_lru_out_g, v_final_g],
        (w_a, m_w_a, v_w_a, w_i, m_w_i, v_w_i), (conv_w, m_conv_w, v_conv_w, lru_conv_w, m_lru_conv_w, v_lru_conv_w))

    def unpack(kind, big_in, big_out):
        vec, (wa_, wi_, cw_, lw_) = kind[:len(small)], kind[len(small):]
        return [vec[0], big_in, cw_, lw_, vec[1], wa_, vec[2], wi_, vec[3], vec[4], vec[5], vec[6], big_out, vec[7]]

    return (loss_tile[0, 0], grad_x[None], *unpack(kinds[0], gw_in, gw_out), *unpack(kinds[1], dw_in, dw_out),
            *unpack(kinds[2], mw_in, mw_out), *unpack(kinds[3], vw_in, vw_out))
```

```python
import jax
import jax.numpy as jnp
from jax import lax
from jax.experimental import pallas as pl
from jax.experimental.pallas import tpu as pltpu

F32 = jnp.float32
MM = jnp.bfloat16
MESH = pl.DeviceIdType.MESH

N_DEV = 8
N_CONV_HEADS = 8
N_LRU_HEADS = 16
RG_LRU_C = 8.0
RMS_EPS = 1e-6
ADAM_LR, ADAM_B1, ADAM_B2, ADAM_EPS, ADAM_WD, ADAM_STEP = 0.001, 0.9, 0.999, 1e-08, 0.01, 10
ADAM_BC1 = 1.0 - ADAM_B1 ** ADAM_STEP
ADAM_BC2 = 1.0 - ADAM_B2 ** ADAM_STEP

SUBLANES = 8
LANES = 128
MXU_TILE = 256
VMEM_LIMIT = 56 * 1024 * 1024

SP_LN_G, SP_LRU_B, SP_B_A, SP_B_I, SP_LAM, SP_CONV_G, SP_LRU_G, SP_FINAL_G, SP_CONV_W, SP_LRU_W = 0, 1, 2, 3, 4, 5, 6, 7, 8, 11
SP_ROWS = 16
P_B, P_C, P_XC, P_GC, P_XL, P_GL = 0, 1, 2, 3, 4, 5
A_CONV_G, A_LRU_G, A_LAM, A_B_A, A_B_I, A_CONV_W, A_LRU_W, A_LRU_B = 0, 1, 2, 3, 4, 5, 8, 12
A_GROUPS = 13
SL_LOSS, SL_LN_G, SL_LRU_B, SL_B_A, SL_B_I, SL_LAM, SL_CONV_G, SL_LRU_G, SL_FINAL_G, SL_CONV_W, SL_LRU_W = 0, 1, 2, 3, 4, 5, 6, 7, 8, 16, 24
SL_ROWS = 32


def _params(vmem=True, **kw):
    if vmem:
        kw["vmem_limit_bytes"] = VMEM_LIMIT
    return pltpu.CompilerParams(**kw)


def _dot(a, b):
    return jnp.dot(a, b, preferred_element_type=F32)


def _dot_nt(a, b):
    return lax.dot_general(a, b, (((1,), (1,)), ((), ())), preferred_element_type=F32)


def _head_sums(v, ones_tile):
    tw = ones_tile.shape[0]
    vb = v.astype(MM)
    return jnp.concatenate([_dot(vb[:, k:k + tw], ones_tile) for k in range(0, v.shape[1], tw)], axis=1)


def _head_rstd(v, ones_tile, head_dim):
    return lax.rsqrt(_head_sums(v * v, ones_tile) * (1.0 / head_dim) + RMS_EPS)


def _sigmoid(x):
    return 0.5 * jnp.tanh(0.5 * x) + 0.5


def _lru_input_scale_sq(log_a, a):
    return -jnp.tanh(log_a) * (1.0 + a * a)


def _log_sigmoid(x):
    z = jnp.exp(-jnp.abs(x))
    u = 1.0 + z
    log1p_z = jnp.where(u == 1.0, z, jnp.log(u) * (z / (u - 1.0)))
    return jnp.minimum(x, 0.0) - log1p_z


def _row_iota(d):
    return lax.broadcasted_iota(jnp.int32, (SUBLANES, d), 0)


def _shift_down(cur, prev, s, row):
    return jnp.where(row >= s, pltpu.roll(cur, s, axis=0), pltpu.roll(prev, s, axis=0))


def _shift_up(cur, nxt, s, row):
    k = SUBLANES - s
    return jnp.where(row < k, pltpu.roll(cur, k, axis=0), pltpu.roll(nxt, k, axis=0))


def _scan_fwd(a, b, h_prev, row):
    for s in (1, 2, 4):
        a_s = jnp.where(row >= s, pltpu.roll(a, s, axis=0), 1.0)
        b_s = jnp.where(row >= s, pltpu.roll(b, s, axis=0), 0.0)
        b = a * b_s + b
        a = a * a_s
    return a * h_prev + b


def _scan_bwd(a_next, b, g_next, row):
    a = a_next
    for s in (1, 2, 4):
        k = SUBLANES - s
        a_s = jnp.where(row < k, pltpu.roll(a, k, axis=0), 1.0)
        b_s = jnp.where(row < k, pltpu.roll(b, k, axis=0), 0.0)
        b = a * b_s + b
        a = a * a_s
    return a * g_next + b


def _bcast_row(v, r):
    return jnp.broadcast_to(v[r:r + 1, :], v.shape)


def _chunks(n_rows, rc, body, init, reverse=False):
    n = n_rows // rc

    def step(i, carry):
        j = (n - 1 - i) if reverse else i
        return body(pl.multiple_of(j * rc, rc), carry)

    return lax.fori_loop(0, n, step, init)


def _adamw(w, g, m, v):
    m = ADAM_B1 * m + (1.0 - ADAM_B1) * g
    v = ADAM_B2 * v + (1.0 - ADAM_B2) * (g * g)
    m_hat = m / ADAM_BC1
    v_hat = v / ADAM_BC2
    delta = -ADAM_LR * (m_hat / (jnp.sqrt(v_hat) + ADAM_EPS) + ADAM_WD * w)
    return delta, m, v


def _mesh_pos():
    return lax.axis_index("x"), lax.axis_index("y"), lax.axis_index("c")


class _Gather:
    def __init__(self, blocks_of, send_sems, recv_sems, own_src=None):
        x, y, c = _mesh_pos()
        self.c = c
        self.me, self.sibling = (x, y, c), (x, y, 1 - c)
        self.chips = [(1 - x, y), (x, 1 - y), (1 - x, 1 - y)]
        self.blocks_of, self.send_sems, self.recv_sems = blocks_of, send_sems, recv_sems
        self.own_src = own_src

    def copy(self, a, k, block, to):
        src = self.blocks_of(a, *block)
        if block is self.me and self.own_src is not None:
            src = self.own_src[a]
        return pltpu.make_async_remote_copy(
            src_ref=src, dst_ref=self.blocks_of(a, *block),
            send_sem=self.send_sems.at[a * 7 + k], recv_sem=self.recv_sems.at[a * 7 + k],
            device_id=to, device_id_type=MESH)

    def start_own(self, a):
        self.copy(a, 0, self.me, self.sibling).start()
        for j, chip in enumerate(self.chips):
            self.copy(a, 1 + j, self.me, (*chip, self.c)).start()

    def wait_sibling(self, a):
        self.copy(a, 0, self.sibling, self.me).wait_recv()

    def wait_chip_and_pass_on(self, a, j):
        block = (*self.chips[j], self.c)
        self.copy(a, 1 + j, block, self.me).wait_recv()
        self.copy(a, 4 + j, block, self.sibling).start()

    def wait_passed_on(self, a, j):
        self.copy(a, 4 + j, (*self.chips[j], 1 - self.c), self.me).wait_recv()

    def wait_sends(self, a):
        self.copy(a, 0, self.me, self.sibling).wait_send()
        for j, chip in enumerate(self.chips):
            self.copy(a, 1 + j, self.me, (*chip, self.c)).wait_send()
            self.copy(a, 4 + j, (*chip, self.c), self.sibling).wait_send()

    def finish(self, a):
        for j in range(3):
            self.wait_chip_and_pass_on(a, j)
        self.wait_sibling(a)
        for j in range(3):
            self.wait_passed_on(a, j)
        self.wait_sends(a)


class _BalancedGather:
    def __init__(self, slot, send_sems, recv_sems, own_src):
        x, y, c = _mesh_pos()
        self.c = c
        self.me, self.sibling = (x, y, c), (x, y, 1 - c)
        self.chips = [(1 - x, y), (x, 1 - y), (1 - x, 1 - y)]
        self.slot, self.send_sems, self.recv_sems, self.own_src = slot, send_sems, recv_sems, own_src

    def half(self, a, block, which):
        ref = self.slot(a, *block)
        n = ref.shape[0] // 2
        return ref.at[pl.ds(which * n, n)]

    def copy(self, a, k, src, dst, to):
        return pltpu.make_async_remote_copy(
            src_ref=src, dst_ref=dst, send_sem=self.send_sems.at[a * 8 + k], recv_sem=self.recv_sems.at[a * 8 + k],
            device_id=to, device_id_type=MESH)

    def whole(self, a, k, block, to):
        src = self.own_src[a] if block is self.me else self.slot(a, *block)
        return self.copy(a, k, src, self.slot(a, *block), to)

    def halved(self, a, k, block, which, to):
        return self.copy(a, k, self.half(a, block, which), self.half(a, block, which), to)

    def on(self, chip):
        return (*self.chips[chip], self.c)

    def start_own(self, a):
        self.whole(a, 0, self.me, self.sibling).start()
        self.whole(a, 1, self.me, self.on(0)).start()
        self.whole(a, 2, self.me, self.on(1)).start()

    def wait_sibling(self, a):
        self.whole(a, 0, self.sibling, self.me).wait_recv()

    def on_neighbour(self, a, j):
        self.whole(a, 1 + j, self.on(j), self.me).wait_recv()
        self.halved(a, 3 + j, self.on(j), j, self.on(1 - j)).start()
        self.whole(a, 5 + j, self.on(j), self.sibling).start()

    def on_diagonal(self, a):
        self.halved(a, 3, self.on(2), 0, self.me).wait_recv()
        self.halved(a, 4, self.on(2), 1, self.me).wait_recv()
        self.whole(a, 7, self.on(2), self.sibling).start()

    def wait_passed_on(self, a, j):
        self.whole(a, 5 + j, (*self.chips[j], 1 - self.c), self.me).wait_recv()

    def wait_sends(self, a):
        self.whole(a, 0, self.me, self.sibling).wait_send()
        for j in range(2):
            self.whole(a, 1 + j, self.me, self.on(j)).wait_send()
            self.halved(a, 3 + j, self.on(j), j, self.on(1 - j)).wait_send()
        for j in range(3):
            self.whole(a, 5 + j, self.on(j), self.sibling).wait_send()


def _block_order():
    x, y, c = _mesh_pos()
    chips = [(x, y), (1 - x, y), (x, 1 - y), (1 - x, 1 - y)]
    return jnp.stack([4 * px + 2 * py + pc for px, py in chips for pc in (c, 1 - c)]).astype(jnp.int32)


def _gather_project(x, w_in, w_out, conv_w, lru_conv_w, ln_g, w_a, w_i, vecs, tb, tw):
    t_len, d = x.shape
    nb = t_len // tb
    cols = w_in.shape[1]
    mc = min(512, t_len)
    conv_pack = jax.ShapeDtypeStruct((SUBLANES, conv_w.shape[1]), F32)
    srcs = (w_in, w_out, conv_pack)
    dts = (MM, MM, F32)
    n_vec = len(vecs)
    n_heads, hd, _ = w_a.shape
    per = tw // hd

    def body(order_ref, x_ref, win_ref, wout_ref, cw_ref, lw_ref, lng_ref, wa_ref, wi_ref, *refs):
        vec_refs = refs[:n_vec]
        (p_ref, xnt_ref, win_all, wout_all, cp_all, sp_ref, wat_ref, wit_ref,
         xnb, wall, st_out, st_cp, cp_vm, send_sems, recv_sems, cp_send, cp_recv, local_sems) = refs[n_vec:]
        i = pl.program_id(0)
        x_, y_, c_ = _mesh_pos()
        me = 4 * x_ + 2 * y_ + c_
        outs = (win_all, wout_all, cp_all)
        lands = (wall, wout_all, cp_all)
        stages = (wall.at[me], st_out, st_cp)
        gather = _BalancedGather(lambda a, px, py, pc: lands[a].at[4 * px + 2 * py + pc], send_sems, recv_sems, stages)
        small = _Gather(lambda a, px, py, pc: cp_all.at[4 * px + 2 * py + pc], cp_send, cp_recv, own_src=[st_cp])
        keep_own = [pltpu.make_async_copy(stages[a], outs[a].at[me], local_sems.at[a]) for a in range(3)]

        def keep(k):
            blk = order_ref[k]
            return pltpu.make_async_copy(wall.at[blk], win_all.at[blk], local_sems.at[2 + k])

        @pl.when(i == 0)
        def _():
            for a, src in enumerate((win_ref, wout_ref)):
                dst, rc = stages[a], 32

                def cast(r, carry, src=src, dst=dst):
                    dst[pl.ds(r, rc), :] = src[pl.ds(r, rc), :].astype(dst.dtype)
                    return carry

                _chunks(src.shape[0], rc, cast, 0)
                keep_own[a].start()
            gather.start_own(0)
            n_cw, n_lw = cw_ref.shape[0], lw_ref.shape[0]
            st_cp[...] = jnp.zeros_like(st_cp)
            st_cp[0:n_cw, :] = cw_ref[...]
            st_cp[n_cw:n_cw + n_lw, :] = lw_ref[...]
            keep_own[2].start()

        @pl.when(i < nb)
        def _():
            xv = x_ref[...]
            r0 = lax.rsqrt(jnp.mean(xv * xv, axis=-1, keepdims=True) + RMS_EPS)
            xn = xv * r0 * lng_ref[...]
            xnb[pl.ds(pl.multiple_of(i * tb, tb), tb), :] = xn.astype(MM)
            xnt_ref[...] = xn.T.astype(MM)

        for k in range(N_DEV):
            @pl.when(i == nb + k)
            def _(k=k):
                if k == 1:
                    gather.wait_sibling(0)
                elif k == 2:
                    gather.on_neighbour(0, 0)
                    gather.on_neighbour(0, 1)
                    gather.start_own(1)
                    small.start_own(0)
                elif k in (3, 5, 7):
                    gather.wait_passed_on(0, (k - 3) // 2)
                    if k == 5:
                        gather.on_neighbour(1, 0)
                        gather.on_neighbour(1, 1)
                    if k == 7:
                        gather.on_diagonal(1)
                elif k == 6:
                    gather.on_diagonal(0)
                blk = order_ref[k]
                if k:
                    keep(k).start()

                def project(r, carry):
                    rows = pl.ds(r, mc)
                    p_ref[rows, :] = _dot(xnb[rows, :], wall[blk]).astype(MM)
                    return carry

                _chunks(t_len, mc, project, 0)
                if k == N_DEV - 1:
                    gather.wait_sends(0)
                    gather.wait_sibling(1)
                    for j in range(3):
                        gather.wait_passed_on(1, j)
                    gather.wait_sends(1)
                    small.finish(0)
                    for cp in keep_own + [keep(kk) for kk in range(1, N_DEV)]:
                        cp.wait()
                    load = pltpu.make_async_copy(cp_all, cp_vm, local_sems.at[N_DEV + 2])
                    load.start()
                    load.wait()
                    for r, ref in enumerate(vec_refs):
                        sp_ref[r, :] = ref[...]
                    sp_ref[n_vec:n_vec + SUBLANES, :] = jnp.concatenate([cp_vm[dev] for dev in range(N_DEV)], axis=1)
                    for src, dst in ((wa_ref, wat_ref), (wi_ref, wit_ref)):
                        dst[...] = jnp.zeros_like(dst)
                        for head in range(n_heads):
                            lo = (head % per) * hd
                            dst[head // per, lo:lo + hd, lo:lo + hd] = src[head].astype(MM)

    vm = pl.BlockSpec(memory_space=pltpu.VMEM)
    hbm = pl.BlockSpec(memory_space=pl.ANY)
    grid_spec = pltpu.PrefetchScalarGridSpec(
        num_scalar_prefetch=1, grid=(nb + N_DEV,),
        in_specs=[pl.BlockSpec((tb, d), lambda i, o: (jnp.minimum(i, nb - 1), 0))] + [vm] * (7 + n_vec),
        out_specs=(pl.BlockSpec((t_len, cols), lambda i, o: (0, o[jnp.maximum(i - nb, 0)])),
                   pl.BlockSpec((d, tb), lambda i, o: (0, jnp.minimum(i, nb - 1))), hbm, hbm, hbm,
                   pl.BlockSpec((SP_ROWS, d), lambda i, o: (0, 0)),
                   pl.BlockSpec((n_heads // per, tw, tw), lambda i, o: (0, 0, 0)),
                   pl.BlockSpec((n_heads // per, tw, tw), lambda i, o: (0, 0, 0))),
        scratch_shapes=[pltpu.VMEM((t_len, d), MM), pltpu.VMEM((N_DEV,) + w_in.shape, MM),
                        pltpu.VMEM(w_out.shape, MM), pltpu.VMEM(conv_pack.shape, F32),
                        pltpu.VMEM((N_DEV,) + conv_pack.shape, F32),
                        pltpu.SemaphoreType.DMA((16,)), pltpu.SemaphoreType.DMA((16,)),
                        pltpu.SemaphoreType.DMA((7,)), pltpu.SemaphoreType.DMA((7,)), pltpu.SemaphoreType.DMA((N_DEV + 3,))])
    return pl.pallas_call(
        body, name="gather_project", grid_spec=grid_spec,
        out_shape=(jax.ShapeDtypeStruct((t_len, N_DEV * cols), MM),
                   jax.ShapeDtypeStruct((d, t_len), MM))
                  + tuple(jax.ShapeDtypeStruct((N_DEV,) + s.shape, dt) for s, dt in zip(srcs, dts))
                  + (jax.ShapeDtypeStruct((SP_ROWS, d), F32),)
                  + (jax.ShapeDtypeStruct((n_heads // per, tw, tw), MM),) * 2,
        compiler_params=_params(dimension_semantics=("arbitrary",)),
    )(_block_order(), x, w_in, w_out, conv_w, lru_conv_w, ln_g, w_a, w_i, *vecs)


def _forward(x, tgt, p, wout, wa_t, wi_t, sp, ones_c, ones_l, tb):
    t_len, d = x.shape
    nb = t_len // tb
    n_tiles, tw = wa_t.shape[0], wa_t.shape[1]
    hd_c, hd_l = d // N_CONV_HEADS, d // N_LRU_HEADS
    s8 = SUBLANES

    def body(x_ref, tgt_ref, p_ref, wout_ref, wa_ref, wi_ref, sp_ref, oc_ref, ol_ref,
             h_ref, dh_ref, dhb_ref, acc_ref, yc, czs, u, pa, pi,
             rcf, rlf, ybuf, tail_z, tail_xl, hcar):
        i = pl.program_id(0)
        row = _row_iota(d)

        @pl.when(i == 0)
        def _():
            tail_z[...] = jnp.zeros_like(tail_z)
            tail_xl[...] = jnp.zeros_like(tail_xl)
            hcar[...] = jnp.zeros_like(hcar)
            acc_ref[...] = jnp.zeros_like(acc_ref)

        def spr(r):
            return sp_ref[r:r + 1, :]

        def proj(rows, seg):
            return p_ref[rows, seg * d:(seg + 1) * d].astype(F32)

        w0, w1, w2 = spr(SP_CONV_W), spr(SP_CONV_W + 1), spr(SP_CONV_W + 2)
        l0, l1, l2, l3 = spr(SP_LRU_W), spr(SP_LRU_W + 1), spr(SP_LRU_W + 2), spr(SP_LRU_W + 3)
        lb = spr(SP_LRU_B)

        def convs(r, carry):
            zp, xp = carry
            rows16 = pl.ds(r, 2 * s8)
            bg16, xl16 = proj(rows16, P_B), proj(rows16, P_XL)
            z16 = proj(rows16, P_C) * proj(rows16, P_XC)
            for j in range(2):
                rows, sub = pl.ds(r + j * s8, s8), slice(j * s8, (j + 1) * s8)
                z, xl = z16[sub], xl16[sub]
                cz = w0 * _shift_down(z, zp, 2, row) + w1 * _shift_down(z, zp, 1, row) + w2 * z
                czs[rows, :] = cz
                yc[rows, :] = bg16[sub] * cz
                u[rows, :] = (l0 * _shift_down(xl, xp, 3, row) + l1 * _shift_down(xl, xp, 2, row)
                              + l2 * _shift_down(xl, xp, 1, row) + l3 * xl + lb)
                zp, xp = z, xl
            return zp, xp

        z_last, xl_last = _chunks(tb, 2 * s8, convs, (tail_z[...], tail_xl[...]))
        tail_z[...] = z_last
        tail_xl[...] = xl_last

        ub = u[...].astype(MM)
        for k in range(n_tiles):
            sl = slice(k * tw, (k + 1) * tw)
            pa[:, sl] = _dot(ub[:, sl], wa_ref[k])
            pi[:, sl] = _dot(ub[:, sl], wi_ref[k])
        rcf[...] = _head_rstd(yc[...], oc_ref[...], hd_c)

        c8 = RG_LRU_C * _log_sigmoid(spr(SP_LAM))
        b_a, b_i = spr(SP_B_A), spr(SP_B_I)

        def lru(r, hp):
            rows = pl.ds(r, SUBLANES)
            ra = _sigmoid(pa[rows, :] + b_a)
            ii = _sigmoid(pi[rows, :] + b_i)
            pa[rows, :] = ra
            pi[rows, :] = ii
            la = ra * c8
            a = jnp.exp(la)
            mult = jnp.sqrt(_lru_input_scale_sq(la, a))
            h = _scan_fwd(a, mult * (ii * u[rows, :]), hp, row)
            h_ref[rows, :] = h
            return _bcast_row(h, SUBLANES - 1)

        hcar[...] = _chunks(tb, SUBLANES, lru, hcar[...])
        rlf[...] = _head_rstd(h_ref[...], ol_ref[...], hd_l)

        g_c, g_l = spr(SP_CONV_G), spr(SP_LRU_G)

        def gate(r, carry):
            rows = pl.ds(r, 2 * s8)
            gc, gl = proj(rows, P_GC), proj(rows, P_GL)
            ybuf[rows, 0:d] = (yc[rows, :] * rcf[rows, :] * g_c * (gc * _sigmoid(gc))).astype(MM)
            ybuf[rows, d:2 * d] = (h_ref[rows, :] * rlf[rows, :] * g_l * (gl * _sigmoid(gl))).astype(MM)
            return carry

        _chunks(tb, 2 * s8, gate, 0)

        hres = x_ref[...] + _dot(ybuf[...], wout_ref[...])
        rf = lax.rsqrt(jnp.mean(hres * hres, axis=-1, keepdims=True) + RMS_EPS)
        hn = hres * rf
        fg = spr(SP_FINAL_G)
        err = hn * fg - tgt_ref[...]
        dout = err * (1.0 / d)
        acc_ref[0:SUBLANES, :] += (err * err).reshape(tb // SUBLANES, SUBLANES, d).sum(axis=0)
        acc_ref[SUBLANES:2 * SUBLANES, :] += (dout * hn).reshape(tb // SUBLANES, SUBLANES, d).sum(axis=0)
        gd = dout * fg
        dhres = rf * (gd - hn * jnp.mean(gd * hn, axis=-1, keepdims=True))
        dh_ref[...] = dhres
        dhb_ref[...] = dhres.astype(MM)

    vm = pl.BlockSpec(memory_space=pltpu.VMEM)
    blk = lambda w: pl.BlockSpec((tb, w), lambda i: (i, 0))
    buf = pltpu.VMEM((tb, d), F32)
    car = pltpu.VMEM((SUBLANES, d), F32)
    return pl.pallas_call(
        body, name="forward", grid=(nb,),
        in_specs=[blk(d), blk(d), blk(6 * d), vm, vm, vm, vm, vm, vm],
        out_specs=(blk(d), blk(d), blk(d), pl.BlockSpec((2 * SUBLANES, d), lambda i: (0, 0))) + (blk(d),) * 5,
        out_shape=(jax.ShapeDtypeStruct((t_len, d), F32),
                   jax.ShapeDtypeStruct((t_len, d), F32),
                   jax.ShapeDtypeStruct((t_len, d), MM),
                   jax.ShapeDtypeStruct((2 * SUBLANES, d), F32))
                  + (jax.ShapeDtypeStruct((t_len, d), F32),) * 5,
        scratch_shapes=[buf] * 2 + [pltpu.VMEM((tb, 2 * d), MM), car, car, car],
        compiler_params=_params(dimension_semantics=("arbitrary",)),
    )(x, tgt, p, wout, wa_t, wi_t, sp, ones_c, ones_l)


def _backward(p, h, dh, saved, facc, wout, wa_t, wi_t, sp, ones_c, ones_l, tb):
    t_len, d = h.shape
    nb = t_len // tb
    n_tiles, tw = wa_t.shape[0], wa_t.shape[1]
    hd_c, hd_l = d // N_CONV_HEADS, d // N_LRU_HEADS
    g_rows = 2 * n_tiles * hd_l
    s8 = SUBLANES

    def body(p_ref, h_ref, hhalo_ref, dh_ref, yc, czs, u, ra_ref, ii_ref, facc_ref,
             wout_ref, wa_ref, wi_ref, sp_ref, oc_ref, ol_ref,
             dp_ref, yt_ref, slab_v, slab_g,
             hh, dy, ybuf, rcf, rlf, qc, ql, dyc_hat, dyl_hat, dpa, dpi, du, gwa_ref, gwi_ref, acc_ref,
             car_dcz, car_a, car_g, car_du):
        i = pl.program_id(0)
        blk_idx = nb - 1 - i
        row = _row_iota(d)

        @pl.when(i == 0)
        def _():
            for ref in (car_dcz, car_a, car_g, car_du, gwa_ref, gwi_ref, acc_ref):
                ref[...] = jnp.zeros_like(ref)

        def spr(r):
            return sp_ref[r:r + 1, :]

        def proj(rows, seg):
            return p_ref[rows, seg * d:(seg + 1) * d].astype(F32)

        def put(rows, seg, halves):
            dp_ref[rows, seg * d:(seg + 1) * d] = jnp.concatenate(halves, axis=0).astype(MM)

        def acc_add(group, val):
            acc_ref[group * s8:(group + 1) * s8, :] += val

        live = jnp.where(blk_idx > 0, 1.0, 0.0).astype(F32)
        hh[0:s8, :] = hhalo_ref[...] * live
        hh[s8:, :] = h_ref[...]

        dy[...] = _dot_nt(dh_ref[...].astype(MM), wout_ref[...])

        w0, w1, w2 = spr(SP_CONV_W), spr(SP_CONV_W + 1), spr(SP_CONV_W + 2)
        l0, l1, l2, l3 = spr(SP_LRU_W), spr(SP_LRU_W + 1), spr(SP_LRU_W + 2), spr(SP_LRU_W + 3)

        rcf[...] = _head_rstd(yc[...], oc_ref[...], hd_c)
        rlf[...] = _head_rstd(h_ref[...], ol_ref[...], hd_l)

        g_c, g_l = spr(SP_CONV_G), spr(SP_LRU_G)

        def gates(r, carry):
            rows = pl.ds(r, 2 * s8)
            for (seg, off_y, src, rstd, gain, q, dhat, grp) in (
                    (P_GC, 0, yc, rcf, g_c, qc, dyc_hat, A_CONV_G),
                    (P_GL, d, h_ref, rlf, g_l, ql, dyl_hat, A_LRU_G)):
                gt = proj(rows, seg)
                sg = _sigmoid(gt)
                silu = gt * sg
                yhat = src[rows, :] * rstd[rows, :]
                nrm = yhat * gain
                ybuf[rows, off_y:off_y + d] = nrm * silu
                dout = dy[rows, off_y:off_y + d]
                dnrm = dout * silu
                dp_ref[rows, seg * d:(seg + 1) * d] = (dout * nrm * (sg * (1.0 + gt * (1.0 - sg)))).astype(MM)
                dg = dnrm * yhat
                acc_add(grp, dg[0:s8] + dg[s8:])
                dh_ = dnrm * gain
                dhat[rows, :] = dh_
                q[rows, :] = dh_ * yhat
            return carry

        _chunks(tb, 2 * s8, gates, 0)

        qc[...] = _head_sums(qc[...], oc_ref[...]) * (1.0 / hd_c)
        ql[...] = _head_sums(ql[...], ol_ref[...]) * (1.0 / hd_l)
        yt_ref[...] = ybuf[...].T.astype(MM)

        c8 = RG_LRU_C * _log_sigmoid(spr(SP_LAM))

        def conv_mixer(r, dcz_n):
            rows16 = pl.ds(r, 2 * s8)
            bg16, cg16, xc16 = proj(rows16, P_B), proj(rows16, P_C), proj(rows16, P_XC)
            z16 = cg16 * xc16
            d_b, d_c, d_x = [None, None], [None, None], [None, None]
            for j in (1, 0):
                rows, sub = pl.ds(r + j * s8, s8), slice(j * s8, (j + 1) * s8)
                rstd = rcf[rows, :]
                yhat = yc[rows, :] * rstd
                dyc = rstd * (dyc_hat[rows, :] - yhat * qc[rows, :])
                d_b[j] = dyc * czs[rows, :]
                dcz = dyc * bg16[sub]
                up1, up2 = _shift_up(dcz, dcz_n, 1, row), _shift_up(dcz, dcz_n, 2, row)
                dz = w2 * dcz + w1 * up1 + w0 * up2
                d_c[j] = dz * xc16[sub]
                d_x[j] = dz * cg16[sub]
                z = z16[sub]
                acc_add(A_CONV_W, up2 * z)
                acc_add(A_CONV_W + 1, up1 * z)
                acc_add(A_CONV_W + 2, dcz * z)
                dcz_n = dcz
            put(rows16, P_B, d_b)
            put(rows16, P_C, d_c)
            put(rows16, P_XC, d_x)
            return dcz_n

        car_dcz[...] = _chunks(tb, 2 * s8, conv_mixer, car_dcz[...], reverse=True)

        def lru_mixer(r, carry):
            a_n, g_n = carry
            for j in (1, 0):
                rows = pl.ds(r + j * s8, s8)
                rstd = rlf[rows, :]
                hcur = hh[pl.ds(r + (j + 1) * s8, s8), :]
                hhat = hcur * rstd
                dh_out = rstd * (dyl_hat[rows, :] - hhat * ql[rows, :])
                ra = ra_ref[rows, :]
                la = ra * c8
                a = jnp.exp(la)
                g = _scan_bwd(_shift_up(a, a_n, 1, row), dh_out, g_n, row)
                da = g * _shift_down(hcur, hh[pl.ds(r + j * s8, s8), :], 1, row)
                ii = ii_ref[rows, :]
                uu = u[rows, :]
                mult_sq = _lru_input_scale_sq(la, a)
                inv_mult = lax.rsqrt(mult_sq)
                dmult = g * (ii * uu)
                ds = g * (mult_sq * inv_mult)
                dla = a * (da - dmult * a * inv_mult)
                acc_add(A_LAM, dla * ra)
                dpa_ = dla * c8 * ra * (1.0 - ra)
                dpi_ = ds * uu * ii * (1.0 - ii)
                acc_add(A_B_A, dpa_)
                acc_add(A_B_I, dpi_)
                dpa[rows, :] = dpa_
                dpi[rows, :] = dpi_
                du[rows, :] = ds * ii
                a_n, g_n = a, _bcast_row(g, 0)
            return a_n, g_n

        a_f, g_f = _chunks(tb, 2 * s8, lru_mixer, (car_a[...], car_g[...]), reverse=True)
        car_a[...] = a_f
        car_g[...] = g_f

        dpab = dpa[...].astype(MM)
        dpib = dpi[...].astype(MM)
        for k in range(n_tiles):
            sl = slice(k * tw, (k + 1) * tw)
            du[:, sl] += _dot_nt(dpab[:, sl], wa_ref[k]) + _dot_nt(dpib[:, sl], wi_ref[k])
            ut = u[:, sl].T.astype(MM)
            gwa_ref[k] += _dot(ut, dpab[:, sl])
            gwi_ref[k] += _dot(ut, dpib[:, sl])

        def lru_conv(r, du_n):
            rows16 = pl.ds(r, 2 * s8)
            xl16 = proj(rows16, P_XL)
            d_xl = [None, None]
            for j in (1, 0):
                rows, sub = pl.ds(r + j * s8, s8), slice(j * s8, (j + 1) * s8)
                dut = du[rows, :]
                up1, up2, up3 = (_shift_up(dut, du_n, s, row) for s in (1, 2, 3))
                d_xl[j] = l3 * dut + l2 * up1 + l1 * up2 + l0 * up3
                xl = xl16[sub]
                acc_add(A_LRU_W, up3 * xl)
                acc_add(A_LRU_W + 1, up2 * xl)
                acc_add(A_LRU_W + 2, up1 * xl)
                acc_add(A_LRU_W + 3, dut * xl)
                acc_add(A_LRU_B, dut)
                du_n = dut
            put(rows16, P_XL, d_xl)
            return du_n

        car_du[...] = _chunks(tb, 2 * s8, lru_conv, car_du[...], reverse=True)

        @pl.when(i == nb - 1)
        def _():
            def rowsum(ref, group):
                return jnp.sum(ref[group * s8:(group + 1) * s8, :], axis=0, keepdims=True)

            slab_v[...] = jnp.zeros_like(slab_v)
            loss = jnp.sum(rowsum(facc_ref, 0), axis=1, keepdims=True) * (0.5 / d)
            rows = {SL_LOSS: jnp.broadcast_to(loss, (1, d)), SL_FINAL_G: rowsum(facc_ref, 1),
                    SL_LRU_B: rowsum(acc_ref, A_LRU_B), SL_B_A: rowsum(acc_ref, A_B_A), SL_B_I: rowsum(acc_ref, A_B_I),
                    SL_LAM: rowsum(acc_ref, A_LAM), SL_CONV_G: rowsum(acc_ref, A_CONV_G), SL_LRU_G: rowsum(acc_ref, A_LRU_G)}
            for k in range(3):
                rows[SL_CONV_W + k] = rowsum(acc_ref, A_CONV_W + k)
            for k in range(4):
                rows[SL_LRU_W + k] = rowsum(acc_ref, A_LRU_W + k)
            for r, val in rows.items():
                slab_v[r:r + 1, :] = val
            head_of_lane = lax.broadcasted_iota(jnp.int32, (hd_l, tw), 1) // hd_l
            for mtx, g_ref in enumerate((gwa_ref, gwi_ref)):
                for k in range(n_tiles):
                    packed = jnp.zeros((hd_l, tw), F32)
                    for a in range(tw // hd_l):
                        packed = jnp.where(head_of_lane == a, g_ref[k, a * hd_l:(a + 1) * hd_l, :], packed)
                    slab_g[(mtx * n_tiles + k) * hd_l:(mtx * n_tiles + k + 1) * hd_l, :] = packed.astype(MM)

    vm = pl.BlockSpec(memory_space=pltpu.VMEM)
    rev = lambda w: pl.BlockSpec((tb, w), lambda i: (nb - 1 - i, 0))
    halo = lambda rows, w: pl.BlockSpec((rows, w), lambda i: (jnp.maximum((nb - 1 - i) * (tb // rows) - 1, 0), 0))
    const = lambda shape: pl.BlockSpec(shape, lambda i: (0,) * len(shape))
    buf = lambda w: pltpu.VMEM((tb, w), F32)
    car = pltpu.VMEM((SUBLANES, d), F32)
    return pl.pallas_call(
        body, name="backward", grid=(nb,),
        in_specs=[rev(6 * d), rev(d), halo(SUBLANES, d), rev(d)] + [rev(d)] * 5 + [vm, vm, vm, vm, vm, vm, vm],
        out_specs=(rev(6 * d), pl.BlockSpec((2 * d, tb), lambda i: (0, nb - 1 - i)),
                   const((SL_ROWS, d)), const((g_rows, tw))),
        out_shape=(jax.ShapeDtypeStruct((t_len, 6 * d), MM),
                   jax.ShapeDtypeStruct((2 * d, t_len), MM),
                   jax.ShapeDtypeStruct((SL_ROWS, d), F32),
                   jax.ShapeDtypeStruct((g_rows, tw), MM)),
        scratch_shapes=[pltpu.VMEM((SUBLANES + tb, d), F32), buf(2 * d), buf(2 * d)] + [buf(d)] * 9
                       + [pltpu.VMEM((n_tiles, tw, tw), F32), pltpu.VMEM((n_tiles, tw, tw), F32),
                          pltpu.VMEM((A_GROUPS * SUBLANES, d), F32), car, car, car, car],
        compiler_params=_params(dimension_semantics=("arbitrary",)),
    )(p, h, h, dh, *saved, facc, wout, wa_t, wi_t, sp, ones_c, ones_l)


def _input_grad(dp, win_all, x, dh, sp, part, tb):
    t_len, d = x.shape
    nb = t_len // tb
    cols = win_all.shape[2]
    mid = min(nb - 1, (5 * nb) // 8)
    rc = 32

    def body(dp_ref, win_ref, x_ref, dh_ref, sp_ref, part_ref, gx_ref, ln_ref, direct, relayed,
             send_sems, recv_sems, local_sems, acc_ref, ln_all, ln_send, ln_recv, mine, theirs):
        i = pl.program_id(0)
        x_, y_, c_ = _mesh_pos()
        first, second = 1 - c_, c_
        nbr1 = (x_ ^ c_, y_ ^ (1 - c_), c_)
        nbr2 = (x_ ^ (1 - c_), y_ ^ c_, c_)

        def remote(src, dst, k, to):
            return pltpu.make_async_remote_copy(src_ref=src, dst_ref=dst, send_sem=send_sems.at[k], recv_sem=recv_sems.at[k],
                                                device_id=to, device_id_type=MESH)

        to_first = [remote(part_ref.at[first], direct, 0, nbr1), remote(part_ref.at[2], theirs, 1, nbr1)]
        to_second = remote(theirs, relayed, 2, nbr2)
        load_mine = pltpu.make_async_copy(part_ref.at[second], mine, local_sems.at[0])

        @pl.when(i == 0)
        def _():
            acc_ref[...] = jnp.zeros_like(acc_ref)
            for cp in to_first:
                cp.start()
            load_mine.start()

        dxn = _dot_nt(dp_ref[:, 0:cols], win_ref[0])
        for j in range(1, N_DEV):
            dxn += _dot_nt(dp_ref[:, j * cols:(j + 1) * cols], win_ref[j])
        xv = x_ref[...]
        r0 = lax.rsqrt(jnp.mean(xv * xv, axis=-1, keepdims=True) + RMS_EPS)
        xhat = xv * r0
        acc_ref[...] += (dxn * xhat).reshape(tb // SUBLANES, SUBLANES, d).sum(axis=0)
        dxh = dxn * sp_ref[SP_LN_G:SP_LN_G + 1, :]
        gx_ref[...] = dh_ref[...] + r0 * (dxh - xhat * jnp.mean(dxh * xhat, axis=-1, keepdims=True))

        @pl.when(i == mid)
        def _():
            to_first[1].wait_recv()
            load_mine.wait()

            def add(r, carry):
                rows = pl.ds(r, rc)
                theirs[rows, :] = (mine[rows, :].astype(F32) + theirs[rows, :].astype(F32)).astype(MM)
                return carry

            _chunks(mine.shape[0], rc, add, 0)
            to_second.start()

        @pl.when(i == nb - 1)
        def _():
            to_first[0].wait_recv()
            to_second.wait_recv()
            for cp in to_first + [to_second]:
                cp.wait_send()
            ln_all[4 * x_ + 2 * y_ + c_] = jnp.broadcast_to(jnp.sum(acc_ref[...], axis=0, keepdims=True), acc_ref.shape)
            gather = _Gather(lambda a, px, py, pc: ln_all.at[4 * px + 2 * py + pc], ln_send, ln_recv)
            gather.start_own(0)
            gather.finish(0)
            total = ln_all[0]
            for dev in range(1, N_DEV):
                total = total + ln_all[dev]
            ln_ref[...] = total

    vm = pl.BlockSpec(memory_space=pltpu.VMEM)
    hbm = pl.BlockSpec(memory_space=pl.ANY)
    blk = lambda w: pl.BlockSpec((tb, w), lambda i: (i, 0))
    landed = jax.ShapeDtypeStruct(part.shape[1:], part.dtype)
    outs = pl.pallas_call(
        body, name="input_grad", grid=(nb,),
        in_specs=[blk(6 * d), vm, blk(d), blk(d), vm, hbm],
        out_specs=(blk(d), pl.BlockSpec((SUBLANES, d), lambda i: (0, 0)), hbm, hbm),
        out_shape=(jax.ShapeDtypeStruct((t_len, d), F32), jax.ShapeDtypeStruct((SUBLANES, d), F32), landed, landed),
        scratch_shapes=[pltpu.SemaphoreType.DMA((3,)), pltpu.SemaphoreType.DMA((3,)), pltpu.SemaphoreType.DMA((1,)),
                        pltpu.VMEM((SUBLANES, d), F32), pltpu.VMEM((N_DEV, SUBLANES, d), F32),
                        pltpu.SemaphoreType.DMA((7,)), pltpu.SemaphoreType.DMA((7,)),
                        pltpu.VMEM(part.shape[1:], MM), pltpu.VMEM(part.shape[1:], MM)],
        compiler_params=_params(dimension_semantics=("arbitrary",)),
    )(dp, win_all, x, dh, sp, part)
    return outs[0], outs[1], (outs[2], outs[3])


_CHIP_RELATIONS = [(0, 0), (1, 0), (0, 1), (1, 1)]


def _related_block(k, core):
    x, y, _ = _mesh_pos()
    fx, fy = _CHIP_RELATIONS[k]
    return 4 * (x ^ fx) + 2 * (y ^ fy) + core


class _ChipExchange:
    def __init__(self, part_refs, land_refs, send_sems, recv_sems):
        self.part_refs, self.land_refs, self.send_sems, self.recv_sems = part_refs, land_refs, send_sems, recv_sems

    def copies(self):
        x, y, c = _mesh_pos()
        for a in range(len(self.part_refs)):
            for k in (1, 2, 3):
                fx, fy = _CHIP_RELATIONS[k]
                yield pltpu.make_async_remote_copy(
                    src_ref=self.part_refs[a].at[k - 1], dst_ref=self.land_refs[a].at[k - 1],
                    send_sem=self.send_sems.at[3 * a + k - 1], recv_sem=self.recv_sems.at[3 * a + k - 1],
                    device_id=(x ^ fx, y ^ fy, c), device_id_type=MESH)

    def start(self):
        for cp in self.copies():
            cp.start()

    def finish(self):
        for cp in self.copies():
            cp.wait_recv()
        for cp in self.copies():
            cp.wait_send()


def _weight_grad_stage1(name, blk_shape, n_split, operands, in_specs, product, riders=(), slabs=()):
    n_rows, n_cols = blk_shape
    rs = n_rows // n_split
    rc = 32
    n_in, n_ride, n_slab = len(operands), len(riders), len(slabs)
    _, _, c = _mesh_pos()
    order = jnp.stack([_related_block(k, 1 - c) for k in range(4)]
                      + [_related_block(k, c) for k in (1, 2, 3, 0)]).astype(jnp.int32)

    def body(order_ref, *refs):
        ins = refs[:n_in]
        ride_in = refs[n_in:n_in + n_ride]
        slab_in = refs[n_in + n_ride:n_in + n_ride + n_slab]
        n_op = n_in + n_ride + n_slab
        part_ref, own_ref = refs[n_op:n_op + 2]
        ride_out = refs[n_op + 2:n_op + 2 + n_ride]
        gathered = refs[n_op + 2 + n_ride:n_op + 2 + n_ride + n_slab]
        (gbuf, sendbuf, from_sib, send_sems, recv_sems, ride_send, ride_recv,
         slab_send, slab_recv, slab_local) = refs[n_op + 2 + n_ride + n_slab:]
        exchange = _ChipExchange(ride_in, ride_out, ride_send, ride_recv)
        s = pl.program_id(0)
        x, y, c = _mesh_pos()
        me = 4 * x + 2 * y + c
        gather = _BalancedGather(lambda a, px, py, pc: gathered[a].at[4 * px + 2 * py + pc], slab_send, slab_recv, slab_in)
        keep_own = [pltpu.make_async_copy(slab_in[a], gathered[a].at[me], slab_local.at[a]) for a in range(n_slab)]

        def to_sibling(k):
            return pltpu.make_async_remote_copy(
                src_ref=sendbuf.at[k], dst_ref=from_sib.at[k], send_sem=send_sems.at[k], recv_sem=recv_sems.at[k],
                device_id=(x, y, 1 - c), device_id_type=MESH)

        @pl.when(s == 0)
        def _():
            exchange.start()
            for a in range(n_slab):
                gather.start_own(a)
                keep_own[a].start()

        @pl.when(s == 5)
        def _():
            for a in range(n_slab):
                gather.on_neighbour(a, 0)
                gather.on_neighbour(a, 1)

        @pl.when(s == 7)
        def _():
            for a in range(n_slab):
                gather.on_diagonal(a)

        for h in range(n_split):
            gbuf[h * rs:(h + 1) * rs, :] = product(ins, h)

        @pl.when(s < 4)
        def _():
            def narrow(r, carry):
                sendbuf[s, pl.ds(r, rc), :] = gbuf[pl.ds(r, rc), :].astype(MM)
                return carry

            _chunks(n_rows, rc, narrow, 0)
            to_sibling(s).start()

        @pl.when(s >= 4)
        def _():
            k = jnp.where(s == 7, 0, s - 3)
            to_sibling(k).wait_recv()

            @pl.when(s < 7)
            def _():
                def add(r, carry):
                    rows = pl.ds(r, rc)
                    part_ref[0, rows, :] = (gbuf[rows, :] + from_sib[k, rows, :].astype(F32)).astype(MM)
                    return carry

                _chunks(n_rows, rc, add, 0)

            @pl.when(s == 7)
            def _():
                def add(r, carry):
                    rows = pl.ds(r, rc)
                    own_ref[rows, :] = gbuf[rows, :] + from_sib[0, rows, :].astype(F32)
                    return carry

                _chunks(n_rows, rc, add, 0)
                for kk in range(4):
                    to_sibling(kk).wait_send()
                exchange.finish()
                for a in range(n_slab):
                    gather.wait_sibling(a)
                    for j in range(3):
                        gather.wait_passed_on(a, j)
                    gather.wait_sends(a)
                    keep_own[a].wait()

    hbm = pl.BlockSpec(memory_space=pl.ANY)
    grid_spec = pltpu.PrefetchScalarGridSpec(
        num_scalar_prefetch=1, grid=(N_DEV,), in_specs=list(in_specs) + [hbm] * (n_ride + n_slab),
        out_specs=(pl.BlockSpec((1, n_rows, n_cols), lambda s, o: (jnp.clip(s - 4, 0, 2), 0, 0)),
                   pl.BlockSpec((n_rows, n_cols), lambda s, o: (0, 0))) + (hbm,) * (n_ride + n_slab),
        scratch_shapes=[pltpu.VMEM((n_rows, n_cols), F32), pltpu.VMEM((4, n_rows, n_cols), MM),
                        pltpu.VMEM((4, n_rows, n_cols), MM),
                        pltpu.SemaphoreType.DMA((4,)), pltpu.SemaphoreType.DMA((4,)),
                        pltpu.SemaphoreType.DMA((max(3 * n_ride, 1),)), pltpu.SemaphoreType.DMA((max(3 * n_ride, 1),)),
                        pltpu.SemaphoreType.DMA((max(8 * n_slab, 1),)), pltpu.SemaphoreType.DMA((max(8 * n_slab, 1),)),
                        pltpu.SemaphoreType.DMA((max(n_slab, 1),))])
    outs = pl.pallas_call(
        body, name=name, grid_spec=grid_spec,
        out_shape=(jax.ShapeDtypeStruct((3, n_rows, n_cols), MM), jax.ShapeDtypeStruct((n_rows, n_cols), F32))
                  + tuple(jax.ShapeDtypeStruct(p.shape, p.dtype) for p in riders)
                  + tuple(jax.ShapeDtypeStruct((N_DEV,) + a.shape, a.dtype) for a in slabs),
        compiler_params=_params(dimension_semantics=("arbitrary",)),
    )(order, *operands, *riders, *slabs)
    return outs[0], outs[1], outs[2:2 + n_ride], outs[2 + n_ride:]


def _weight_grad_in(xnt, dp, riders, slabs):
    d, t_len = xnt.shape
    cols = dp.shape[1] // N_DEV
    half = d // 2
    return _weight_grad_stage1(
        "weight_grad_in", (d, cols), 2, (xnt, dp),
        [pl.BlockSpec(memory_space=pltpu.VMEM), pl.BlockSpec((t_len, cols), lambda s, o: (0, o[s]))],
        lambda refs, h: _dot(refs[0][h * half:(h + 1) * half, :], refs[1][...]), riders, slabs)


def _weight_grad_out(yt, dhb):
    d2, t_len = yt.shape
    d = dhb.shape[1]
    rows = d2 // N_DEV
    return _weight_grad_stage1(
        "weight_grad_out", (rows, d), 1, (yt, dhb),
        [pl.BlockSpec((rows, t_len), lambda s, o: (o[s], 0)), pl.BlockSpec(memory_space=pltpu.VMEM)],
        lambda refs, h: _dot(refs[0][...], refs[1][...]))


def _update_shard(own, others, w, m, v, name):
    n_rows, n_cols = w.shape
    rb = min(256, n_rows)
    n_other = len(others)

    def body(own_ref, *refs):
        other_refs = refs[:n_other]
        w_ref, m_ref, v_ref, grad_ref, delta_ref, mo_ref, vo_ref = refs[n_other:]
        g = own_ref[...]
        for ref in other_refs:
            for k in range(ref.shape[0] if len(ref.shape) == 3 else 1):
                g = g + (ref[k] if len(ref.shape) == 3 else ref[...]).astype(F32)
        delta, m_new, v_new = _adamw(w_ref[...], g, m_ref[...], v_ref[...])
        grad_ref[...] = g
        delta_ref[...] = delta
        mo_ref[...] = m_new
        vo_ref[...] = v_new

    blk = pl.BlockSpec((rb, n_cols), lambda i: (i, 0))
    stacked = lambda n: pl.BlockSpec((n, rb, n_cols), lambda i: (0, i, 0))
    out = jax.ShapeDtypeStruct((n_rows, n_cols), F32)
    return pl.pallas_call(
        body, name=name, grid=(n_rows // rb,),
        in_specs=[blk] + [stacked(o.shape[0]) if o.ndim == 3 else blk for o in others] + [blk, blk, blk],
        out_specs=(blk, blk, blk, blk), out_shape=(out, out, out, out),
        compiler_params=_params(dimension_semantics=("arbitrary",)),
    )(own, *others, w, m, v)


def _small_update(gat_v, gat_g, ln_tot, vec_w, vec_m, vec_v, gates, convs):
    n_vec = len(vec_w)
    n_heads, hd, _ = gates[0].shape
    tw = gat_g.shape[2]
    s8 = SUBLANES
    per = tw // hd
    n_tiles = n_heads // per
    cc = convs[0].shape[1]
    n_in = 3 + 3 * n_vec + 12

    def body(*refs):
        gv_ref, gg_ref, ln_ref = refs[:3]
        w_refs, m_refs, v_refs = (refs[3 + j * n_vec:3 + (j + 1) * n_vec] for j in range(3))
        gate_refs = refs[3 + 3 * n_vec:3 + 3 * n_vec + 6]
        conv_refs = refs[3 + 3 * n_vec + 6:n_in]
        loss_o = refs[n_in]
        kinds = [refs[n_in + 1 + j * (n_vec + 4):n_in + 1 + (j + 1) * (n_vec + 4)] for j in range(4)]
        tv, tg = refs[n_in + 1 + 4 * (n_vec + 4):]
        x, y, c = _mesh_pos()
        me = 4 * x + 2 * y + c

        def emit(k_out, w, g, m, v):
            delta, m_new, v_new = _adamw(w, g, m, v)
            for ref, val in zip(k_out, (g, delta, m_new, v_new)):
                ref[...] = val

        total = gv_ref[0]
        for dev in range(1, N_DEV):
            total = total + gv_ref[dev]
        tv[...] = total
        tv[SL_LN_G:SL_LN_G + 1, :] = ln_ref[0:1, :]

        def sum_gates(r, carry):
            rows = pl.ds(r, 2 * s8)
            part = gg_ref[0, rows, :].astype(F32)
            for dev in range(1, N_DEV):
                part = part + gg_ref[dev, rows, :].astype(F32)
            tg[rows, :] = part
            return carry

        _chunks(tg.shape[0], 2 * s8, sum_gates, 0)
        loss_o[...] = jnp.broadcast_to(tv[SL_LOSS:SL_LOSS + 1, 0:LANES], loss_o.shape)
        for p in range(n_vec):
            w, g = w_refs[p][...], tv[SL_LN_G + p, :]
            if SL_LN_G + p == SL_LAM:
                g = g * (RG_LRU_C * jax.nn.sigmoid(-w))
            emit([k_out[p] for k_out in kinds], w, g, m_refs[p][...], v_refs[p][...])
        lanes = pl.ds(pl.multiple_of(me * cc, cc), cc)
        for j, (row0, n) in enumerate(((SL_CONV_W, 3), (SL_LRU_W, 4))):
            w_ref, m_ref, v_ref = conv_refs[3 * j:3 * j + 3]
            emit([k_out[n_vec + 2 + j] for k_out in kinds], w_ref[...], tv[row0:row0 + n, lanes], m_ref[...], v_ref[...])
        for mtx in range(2):
            w_ref, m_ref, v_ref = gate_refs[3 * mtx:3 * mtx + 3]
            for k in range(n_tiles):
                tile = tg[(mtx * n_tiles + k) * hd:(mtx * n_tiles + k + 1) * hd, :]
                for a in range(per):
                    head = k * per + a
                    g = tile[:, a * hd:(a + 1) * hd]
                    delta, m_new, v_new = _adamw(w_ref[head], g, m_ref[head], v_ref[head])
                    for k_out, val in zip(kinds, (g, delta, m_new, v_new)):
                        k_out[n_vec + mtx][head] = val

    vm = pl.BlockSpec(memory_space=pltpu.VMEM)
    like = lambda a: jax.ShapeDtypeStruct(a.shape, F32)
    per_kind = tuple(like(a) for a in vec_w) + (like(gates[0]), like(gates[3]), like(convs[0]), like(convs[3]))
    n_out = 1 + 4 * len(per_kind)
    outs = pl.pallas_call(
        body, name="small_update",
        in_specs=[vm] * n_in, out_specs=(vm,) * n_out,
        out_shape=(jax.ShapeDtypeStruct((SUBLANES, LANES), F32),) + per_kind * 4,
        scratch_shapes=[pltpu.VMEM(gat_v.shape[1:], F32), pltpu.VMEM(gat_g.shape[1:], F32)],
        compiler_params=_params(),
    )(gat_v, gat_g, ln_tot, *vec_w, *vec_m, *vec_v, *gates, *convs)
    return outs[0], [outs[1 + j * len(per_kind):1 + (j + 1) * len(per_kind)] for j in range(4)]


def _head_ones(head_dim, tw):
    lane = jnp.arange(tw) // head_dim
    return (lane[:, None] == lane[None, :]).astype(MM)


def kernel(x, ln_g, w_in, conv_w, lru_conv_w, lru_conv_b, w_a, b_a, w_i, b_i, lam, conv_out_g, lru_out_g, w_out, final_g, loss_target, m_ln_g, m_w_in, m_conv_w, m_lru_conv_w, m_lru_conv_b, m_w_a, m_b_a, m_w_i, m_b_i, m_lam, m_conv_out_g, m_lru_out_g, m_w_out, m_final_g, v_ln_g, v_w_in, v_conv_w, v_lru_conv_w, v_lru_conv_b, v_w_a, v_b_a, v_w_i, v_b_i, v_lam, v_conv_out_g, v_lru_out_g, v_w_out, v_final_g):
    _, t_len, d = x.shape
    hd_l = d // N_LRU_HEADS
    tw = min(MXU_TILE, d)
    x2, tgt2 = x[0], loss_target[0]

    small = [ln_g, lru_conv_b, b_a, b_i, lam, conv_out_g, lru_out_g, final_g]
    p, xnt, win_all, wout_all, _, sp, wa_t, wi_t = _gather_project(
        x2, w_in, w_out, conv_w, lru_conv_w, ln_g.reshape(1, d), w_a, w_i, small, min(256, t_len), tw)
    wout_full = wout_all.reshape(N_DEV * w_out.shape[0], d)
    ones_c, ones_l = _head_ones(d // N_CONV_HEADS, tw), _head_ones(hd_l, tw)

    h, dh, dhb, facc, *saved = _forward(x2, tgt2, p, wout_full, wa_t, wi_t, sp, ones_c, ones_l, min(256, t_len))
    dp, yt, slab_v, slab_g = _backward(p, h, dh, saved, facc, wout_full, wa_t, wi_t, sp, ones_c, ones_l, min(256, t_len))
    part_out, own_out, _, _ = _weight_grad_out(yt, dhb)
    part_in, own_in, (chips_out,), (gat_v, gat_g) = _weight_grad_in(xnt, dp, (part_out,), (slab_v, slab_g))
    grad_x, ln_tot, sums_in = _input_grad(dp, win_all, x2, dh, sp, part_in, min(512, t_len))
    gw_in, dw_in, mw_in, vw_in = _update_shard(own_in, sums_in, w_in, m_w_in, v_w_in, "update_w_in")
    gw_out, dw_out, mw_out, vw_out = _update_shard(own_out, (chips_out,), w_out, m_w_out, v_w_out, "update_w_out")

    loss_tile, kinds = _small_update(
        gat_v, gat_g, ln_tot, small,
        [m_ln_g, m_lru_conv_b, m_b_a, m_b_i, m_lam, m_conv_out_g, m_lru_out_g, m_final_g],
        [v_ln_g, v_lru_conv_b, v_b_a, v_b_i, v_lam, v_conv_out_g, v_lru_out_g, v_final_g],
        (w_a, m_w_a, v_w_a, w_i, m_w_i, v_w_i), (conv_w, m_conv_w, v_conv_w, lru_conv_w, m_lru_conv_w, v_lru_conv_w))

    def unpack(kind, big_in, big_out):
        vec, (wa_, wi_, cw_, lw_) = kind[:len(small)], kind[len(small):]
        return [vec[0], big_in, cw_, lw_, vec[1], wa_, vec[2], wi_, vec[3], vec[4], vec[5], vec[6], big_out, vec[7]]

    return (loss_tile[0, 0], grad_x[None], *unpack(kinds[0], gw_in, gw_out), *unpack(kinds[1], dw_in, dw_out),
            *unpack(kinds[2], mw_in, mw_out), *unpack(kinds[3], vw_in, vw_out))
```

```python
import jax
import jax.numpy as jnp
from jax import lax
from jax.experimental import pallas as pl
from jax.experimental.pallas import tpu as pltpu

F32 = jnp.float32
MM = jnp.bfloat16
MESH = pl.DeviceIdType.MESH

N_DEV = 8
N_CONV_HEADS = 8
N_LRU_HEADS = 16
RG_LRU_C = 8.0
RMS_EPS = 1e-6
ADAM_LR, ADAM_B1, ADAM_B2, ADAM_EPS, ADAM_WD, ADAM_STEP = 0.001, 0.9, 0.999, 1e-08, 0.01, 10
ADAM_BC1 = 1.0 - ADAM_B1 ** ADAM_STEP
ADAM_BC2 = 1.0 - ADAM_B2 ** ADAM_STEP

SUBLANES = 8
LANES = 128
MXU_TILE = 256
VMEM_LIMIT = 56 * 1024 * 1024

SP_LN_G, SP_LRU_B, SP_B_A, SP_B_I, SP_LAM, SP_CONV_G, SP_LRU_G, SP_FINAL_G, SP_CONV_W, SP_LRU_W = 0, 1, 2, 3, 4, 5, 6, 7, 8, 11
SP_ROWS = 16
P_B, P_C, P_XC, P_GC, P_XL, P_GL = 0, 1, 2, 3, 4, 5
A_CONV_G, A_LRU_G, A_LAM, A_B_A, A_B_I, A_CONV_W, A_LRU_W, A_LRU_B = 0, 1, 2, 3, 4, 5, 8, 12
A_GROUPS = 13
SL_LOSS, SL_LN_G, SL_LRU_B, SL_B_A, SL_B_I, SL_LAM, SL_CONV_G, SL_LRU_G, SL_FINAL_G, SL_CONV_W, SL_LRU_W = 0, 1, 2, 3, 4, 5, 6, 7, 8, 16, 24
SL_ROWS = 32


def _params(vmem=True, **kw):
    if vmem:
        kw["vmem_limit_bytes"] = VMEM_LIMIT
    return pltpu.CompilerParams(**kw)


def _dot(a, b):
    return jnp.dot(a, b, preferred_element_type=F32)


def _dot_nt(a, b):
    return lax.dot_general(a, b, (((1,), (1,)), ((), ())), preferred_element_type=F32)


def _head_sums(v, ones_tile):
    tw = ones_tile.shape[0]
    vb = v.astype(MM)
    return jnp.concatenate([_dot(vb[:, k:k + tw], ones_tile) for k in range(0, v.shape[1], tw)], axis=1)


def _head_rstd(v, ones_tile, head_dim):
    return lax.rsqrt(_head_sums(v * v, ones_tile) * (1.0 / head_dim) + RMS_EPS)


def _sigmoid(x):
    return 0.5 * jnp.tanh(0.5 * x) + 0.5


def _lru_input_scale_sq(log_a, a):
    return -jnp.tanh(log_a) * (1.0 + a * a)


def _log_sigmoid(x):
    z = jnp.exp(-jnp.abs(x))
    u = 1.0 + z
    log1p_z = jnp.where(u == 1.0, z, jnp.log(u) * (z / (u - 1.0)))
    return jnp.minimum(x, 0.0) - log1p_z


def _row_iota(d):
    return lax.broadcasted_iota(jnp.int32, (SUBLANES, d), 0)


def _shift_down(cur, prev, s, row):
    return jnp.where(row >= s, pltpu.roll(cur, s, axis=0), pltpu.roll(prev, s, axis=0))


def _shift_up(cur, nxt, s, row):
    k = SUBLANES - s
    return jnp.where(row < k, pltpu.roll(cur, k, axis=0), pltpu.roll(nxt, k, axis=0))


def _scan_fwd(a, b, h_prev, row):
    for s in (1, 2, 4):
        a_s = jnp.where(row >= s, pltpu.roll(a, s, axis=0), 1.0)
        b_s = jnp.where(row >= s, pltpu.roll(b, s, axis=0), 0.0)
        b = a * b_s + b
        a = a * a_s
    return a * h_prev + b


def _scan_bwd(a_next, b, g_next, row):
    a = a_next
    for s in (1, 2, 4):
        k = SUBLANES - s
        a_s = jnp.where(row < k, pltpu.roll(a, k, axis=0), 1.0)
        b_s = jnp.where(row < k, pltpu.roll(b, k, axis=0), 0.0)
        b = a * b_s + b
        a = a * a_s
    return a * g_next + b


def _bcast_row(v, r):
    return jnp.broadcast_to(v[r:r + 1, :], v.shape)


def _chunks(n_rows, rc, body, init, reverse=False):
    n = n_rows // rc

    def step(i, carry):
        j = (n - 1 - i) if reverse else i
        return body(pl.multiple_of(j * rc, rc), carry)

    return lax.fori_loop(0, n, step, init)


def _adamw(w, g, m, v):
    m = ADAM_B1 * m + (1.0 - ADAM_B1) * g
    v = ADAM_B2 * v + (1.0 - ADAM_B2) * (g * g)
    m_hat = m / ADAM_BC1
    v_hat = v / ADAM_BC2
    delta = -ADAM_LR * (m_hat / (jnp.sqrt(v_hat) + ADAM_EPS) + ADAM_WD * w)
    return delta, m, v


def _mesh_pos():
    return lax.axis_index("x"), lax.axis_index("y"), lax.axis_index("c")


class _Gather:
    def __init__(self, blocks_of, send_sems, recv_sems, own_src=None):
        x, y, c = _mesh_pos()
        self.c = c
        self.me, self.sibling = (x, y, c), (x, y, 1 - c)
        self.chips = [(1 - x, y), (x, 1 - y), (1 - x, 1 - y)]
        self.blocks_of, self.send_sems, self.recv_sems = blocks_of, send_sems, recv_sems
        self.own_src = own_src

    def copy(self, a, k, block, to):
        src = self.blocks_of(a, *block)
        if block is self.me and self.own_src is not None:
            src = self.own_src[a]
        return pltpu.make_async_remote_copy(
            src_ref=src, dst_ref=self.blocks_of(a, *block),
            send_sem=self.send_sems.at[a * 7 + k], recv_sem=self.recv_sems.at[a * 7 + k],
            device_id=to, device_id_type=MESH)

    def start_own(self, a):
        self.copy(a, 0, self.me, self.sibling).start()
        for j, chip in enumerate(self.chips):
            self.copy(a, 1 + j, self.me, (*chip, self.c)).start()

    def wait_sibling(self, a):
        self.copy(a, 0, self.sibling, self.me).wait_recv()

    def wait_chip_and_pass_on(self, a, j):
        block = (*self.chips[j], self.c)
        self.copy(a, 1 + j, block, self.me).wait_recv()
        self.copy(a, 4 + j, block, self.sibling).start()

    def wait_passed_on(self, a, j):
        self.copy(a, 4 + j, (*self.chips[j], 1 - self.c), self.me).wait_recv()

    def wait_sends(self, a):
        self.copy(a, 0, self.me, self.sibling).wait_send()
        for j, chip in enumerate(self.chips):
            self.copy(a, 1 + j, self.me, (*chip, self.c)).wait_send()
            self.copy(a, 4 + j, (*chip, self.c), self.sibling).wait_send()

    def finish(self, a):
        for j in range(3):
            self.wait_chip_and_pass_on(a, j)
        self.wait_sibling(a)
        for j in range(3):
            self.wait_passed_on(a, j)
        self.wait_sends(a)


class _BalancedGather:
    def __init__(self, slot, send_sems, recv_sems, own_src):
        x, y, c = _mesh_pos()
        self.c = c
        self.me, self.sibling = (x, y, c), (x, y, 1 - c)
        self.chips = [(1 - x, y), (x, 1 - y), (1 - x, 1 - y)]
        self.slot, self.send_sems, self.recv_sems, self.own_src = slot, send_sems, recv_sems, own_src

    def half(self, a, block, which):
        ref = self.slot(a, *block)
        n = ref.shape[0] // 2
        return ref.at[pl.ds(which * n, n)]

    def copy(self, a, k, src, dst, to):
        return pltpu.make_async_remote_copy(
            src_ref=src, dst_ref=dst, send_sem=self.send_sems.at[a * 8 + k], recv_sem=self.recv_sems.at[a * 8 + k],
            device_id=to, device_id_type=MESH)

    def whole(self, a, k, block, to):
        src = self.own_src[a] if block is self.me else self.slot(a, *block)
        return self.copy(a, k, src, self.slot(a, *block), to)

    def halved(self, a, k, block, which, to):
        return self.copy(a, k, self.half(a, block, which), self.half(a, block, which), to)

    def on(self, chip):
        return (*self.chips[chip], self.c)

    def start_own(self, a):
        self.whole(a, 0, self.me, self.sibling).start()
        self.whole(a, 1, self.me, self.on(0)).start()
        self.whole(a, 2, self.me, self.on(1)).start()

    def wait_sibling(self, a):
        self.whole(a, 0, self.sibling, self.me).wait_recv()

    def on_neighbour(self, a, j):
        self.whole(a, 1 + j, self.on(j), self.me).wait_recv()
        self.halved(a, 3 + j, self.on(j), j, self.on(1 - j)).start()
        self.whole(a, 5 + j, self.on(j), self.sibling).start()

    def on_diagonal(self, a):
        self.halved(a, 3, self.on(2), 0, self.me).wait_recv()
        self.halved(a, 4, self.on(2), 1, self.me).wait_recv()
        self.whole(a, 7, self.on(2), self.sibling).start()

    def wait_passed_on(self, a, j):
        self.whole(a, 5 + j, (*self.chips[j], 1 - self.c), self.me).wait_recv()

    def wait_sends(self, a):
        self.whole(a, 0, self.me, self.sibling).wait_send()
        for j in range(2):
            self.whole(a, 1 + j, self.me, self.on(j)).wait_send()
            self.halved(a, 3 + j, self.on(j), j, self.on(1 - j)).wait_send()
        for j in range(3):
            self.whole(a, 5 + j, self.on(j), self.sibling).wait_send()


def _block_order():
    x, y, c = _mesh_pos()
    chips = [(x, y), (1 - x, y), (x, 1 - y), (1 - x, 1 - y)]
    return jnp.stack([4 * px + 2 * py + pc for px, py in chips for pc in (c, 1 - c)]).astype(jnp.int32)


def _gather_project(x, w_in, w_out, conv_w, lru_conv_w, ln_g, w_a, w_i, vecs, tb, tw):
    t_len, d = x.shape
    nb = t_len // tb
    cols = w_in.shape[1]
    mc = min(512, t_len)
    conv_pack = jax.ShapeDtypeStruct((SUBLANES, conv_w.shape[1]), F32)
    srcs = (w_in, w_out, conv_pack)
    dts = (MM, MM, F32)
    n_vec = len(vecs)
    n_heads, hd, _ = w_a.shape
    per = tw // hd

    def body(order_ref, x_ref, win_ref, wout_ref, cw_ref, lw_ref, lng_ref, wa_ref, wi_ref, *refs):
        vec_refs = refs[:n_vec]
        (p_ref, xnt_ref, win_all, wout_all, cp_all, sp_ref, wat_ref, wit_ref,
         xnb, wall, st_out, st_cp, cp_vm, send_sems, recv_sems, cp_send, cp_recv, local_sems) = refs[n_vec:]
        i = pl.program_id(0)
        x_, y_, c_ = _mesh_pos()
        me = 4 * x_ + 2 * y_ + c_
        outs = (win_all, wout_all, cp_all)
        lands = (wall, wout_all, cp_all)
        stages = (wall.at[me], st_out, st_cp)
        gather = _BalancedGather(lambda a, px, py, pc: lands[a].at[4 * px + 2 * py + pc], send_sems, recv_sems, stages)
        small = _Gather(lambda a, px, py, pc: cp_all.at[4 * px + 2 * py + pc], cp_send, cp_recv, own_src=[st_cp])
        keep_own = [pltpu.make_async_copy(stages[a], outs[a].at[me], local_sems.at[a]) for a in range(3)]

        def keep(k):
            blk = order_ref[k]
            return pltpu.make_async_copy(wall.at[blk], win_all.at[blk], local_sems.at[2 + k])

        @pl.when(i == 0)
        def _():
            for a, src in enumerate((win_ref, wout_ref)):
                dst, rc = stages[a], 32

                def cast(r, carry, src=src, dst=dst):
                    dst[pl.ds(r, rc), :] = src[pl.ds(r, rc), :].astype(dst.dtype)
                    return carry

                _chunks(src.shape[0], rc, cast, 0)
                keep_own[a].start()
            gather.start_own(0)
            n_cw, n_lw = cw_ref.shape[0], lw_ref.shape[0]
            st_cp[...] = jnp.zeros_like(st_cp)
            st_cp[0:n_cw, :] = cw_ref[...]
            st_cp[n_cw:n_cw + n_lw, :] = lw_ref[...]
            keep_own[2].start()

        @pl.when(i < nb)
        def _():
            xv = x_ref[...]
            r0 = lax.rsqrt(jnp.mean(xv * xv, axis=-1, keepdims=True) + RMS_EPS)
            xn = xv * r0 * lng_ref[...]
            xnb[pl.ds(pl.multiple_of(i * tb, tb), tb), :] = xn.astype(MM)
            xnt_ref[...] = xn.T.astype(MM)

        for k in range(N_DEV):
            @pl.when(i == nb + k)
            def _(k=k):
                if k == 1:
                    gather.wait_sibling(0)
                elif k == 2:
                    gather.on_neighbour(0, 0)
                    gather.on_neighbour(0, 1)
                    gather.start_own(1)
                    small.start_own(0)
                elif k in (3, 5, 7):
                    gather.wait_passed_on(0, (k - 3) // 2)
                    if k == 5:
                        gather.on_neighbour(1, 0)
                        gather.on_neighbour(1, 1)
                    if k == 7:
                        gather.on_diagonal(1)
                elif k == 6:
                    gather.on_diagonal(0)
                blk = order_ref[k]
                if k:
                    keep(k).start()

                def project(r, carry):
                    rows = pl.ds(r, mc)
                    p_ref[rows, :] = _dot(xnb[rows, :], wall[blk]).astype(MM)
                    return carry

                _chunks(t_len, mc, project, 0)
                if k == N_DEV - 1:
                    gather.wait_sends(0)
                    gather.wait_sibling(1)
                    for j in range(3):
                        gather.wait_passed_on(1, j)
                    gather.wait_sends(1)
                    small.finish(0)
                    for cp in keep_own + [keep(kk) for kk in range(1, N_DEV)]:
                        cp.wait()
                    load = pltpu.make_async_copy(cp_all, cp_vm, local_sems.at[N_DEV + 2])
                    load.start()
                    load.wait()
                    for r, ref in enumerate(vec_refs):
                        sp_ref[r, :] = ref[...]
                    sp_ref[n_vec:n_vec + SUBLANES, :] = jnp.concatenate([cp_vm[dev] for dev in range(N_DEV)], axis=1)
                    for src, dst in ((wa_ref, wat_ref), (wi_ref, wit_ref)):
                        dst[...] = jnp.zeros_like(dst)
                        for head in range(n_heads):
                            lo = (head % per) * hd
                            dst[head // per, lo:lo + hd, lo:lo + hd] = src[head].astype(MM)

    vm = pl.BlockSpec(memory_space=pltpu.VMEM)
    hbm = pl.BlockSpec(memory_space=pl.ANY)
    grid_spec = pltpu.PrefetchScalarGridSpec(
        num_scalar_prefetch=1, grid=(nb + N_DEV,),
        in_specs=[pl.BlockSpec((tb, d), lambda i, o: (jnp.minimum(i, nb - 1), 0))] + [vm] * (7 + n_vec),
        out_specs=(pl.BlockSpec((t_len, cols), lambda i, o: (0, o[jnp.maximum(i - nb, 0)])),
                   pl.BlockSpec((d, tb), lambda i, o: (0, jnp.minimum(i, nb - 1))), hbm, hbm, hbm,
                   pl.BlockSpec((SP_ROWS, d), lambda i, o: (0, 0)),
                   pl.BlockSpec((n_heads // per, tw, tw), lambda i, o: (0, 0, 0)),
                   pl.BlockSpec((n_heads // per, tw, tw), lambda i, o: (0, 0, 0))),
        scratch_shapes=[pltpu.VMEM((t_len, d), MM), pltpu.VMEM((N_DEV,) + w_in.shape, MM),
                        pltpu.VMEM(w_out.shape, MM), pltpu.VMEM(conv_pack.shape, F32),
                        pltpu.VMEM((N_DEV,) + conv_pack.shape, F32),
                        pltpu.SemaphoreType.DMA((16,)), pltpu.SemaphoreType.DMA((16,)),
                        pltpu.SemaphoreType.DMA((7,)), pltpu.SemaphoreType.DMA((7,)), pltpu.SemaphoreType.DMA((N_DEV + 3,))])
    return pl.pallas_call(
        body, name="gather_project", grid_spec=grid_spec,
        out_shape=(jax.ShapeDtypeStruct((t_len, N_DEV * cols), MM),
                   jax.ShapeDtypeStruct((d, t_len), MM))
                  + tuple(jax.ShapeDtypeStruct((N_DEV,) + s.shape, dt) for s, dt in zip(srcs, dts))
                  + (jax.ShapeDtypeStruct((SP_ROWS, d), F32),)
                  + (jax.ShapeDtypeStruct((n_heads // per, tw, tw), MM),) * 2,
        compiler_params=_params(dimension_semantics=("arbitrary",)),
    )(_block_order(), x, w_in, w_out, conv_w, lru_conv_w, ln_g, w_a, w_i, *vecs)


def _forward(x, tgt, p, wout, wa_t, wi_t, sp, ones_c, ones_l, tb):
    t_len, d = x.shape
    nb = t_len // tb
    n_tiles, tw = wa_t.shape[0], wa_t.shape[1]
    hd_c, hd_l = d // N_CONV_HEADS, d // N_LRU_HEADS
    s8 = SUBLANES

    def body(x_ref, tgt_ref, p_ref, wout_ref, wa_ref, wi_ref, sp_ref, oc_ref, ol_ref,
             h_ref, dh_ref, dhb_ref, acc_ref, saved_ref,
             rcf, rlf, ybuf, tail_z, tail_xl, hcar):
        i = pl.program_id(0)
        row = _row_iota(d)
        yc, czs, u, pa, pi = (saved_ref.at[:, k * d:(k + 1) * d] for k in range(5))

        @pl.when(i == 0)
        def _():
            tail_z[...] = jnp.zeros_like(tail_z)
            tail_xl[...] = jnp.zeros_like(tail_xl)
            hcar[...] = jnp.zeros_like(hcar)
            acc_ref[...] = jnp.zeros_like(acc_ref)

        def spr(r):
            return sp_ref[r:r + 1, :]

        def proj(rows, seg):
            return p_ref[rows, seg * d:(seg + 1) * d].astype(F32)

        w0, w1, w2 = spr(SP_CONV_W), spr(SP_CONV_W + 1), spr(SP_CONV_W + 2)
        l0, l1, l2, l3 = spr(SP_LRU_W), spr(SP_LRU_W + 1), spr(SP_LRU_W + 2), spr(SP_LRU_W + 3)
        lb = spr(SP_LRU_B)

        def convs(r, carry):
            zp, xp = carry
            rows16 = pl.ds(r, 2 * s8)
            bg16, xl16 = proj(rows16, P_B), proj(rows16, P_XL)
            z16 = proj(rows16, P_C) * proj(rows16, P_XC)
            for j in range(2):
                rows, sub = pl.ds(r + j * s8, s8), slice(j * s8, (j + 1) * s8)
                z, xl = z16[sub], xl16[sub]
                cz = w0 * _shift_down(z, zp, 2, row) + w1 * _shift_down(z, zp, 1, row) + w2 * z
                czs[rows, :] = cz
                yc[rows, :] = bg16[sub] * cz
                u[rows, :] = (l0 * _shift_down(xl, xp, 3, row) + l1 * _shift_down(xl, xp, 2, row)
                              + l2 * _shift_down(xl, xp, 1, row) + l3 * xl + lb)
                zp, xp = z, xl
            return zp, xp

        z_last, xl_last = _chunks(tb, 2 * s8, convs, (tail_z[...], tail_xl[...]))
        tail_z[...] = z_last
        tail_xl[...] = xl_last

        ub = u[...].astype(MM)
        for k in range(n_tiles):
            sl = slice(k * tw, (k + 1) * tw)
            pa[:, sl] = _dot(ub[:, sl], wa_ref[k])
            pi[:, sl] = _dot(ub[:, sl], wi_ref[k])
        rcf[...] = _head_rstd(yc[...], oc_ref[...], hd_c)

        c8 = RG_LRU_C * _log_sigmoid(spr(SP_LAM))
        b_a, b_i = spr(SP_B_A), spr(SP_B_I)

        def lru(r, hp):
            rows = pl.ds(r, SUBLANES)
            ra = _sigmoid(pa[rows, :] + b_a)
            ii = _sigmoid(pi[rows, :] + b_i)
            pa[rows, :] = ra
            pi[rows, :] = ii
            la = ra * c8
            a = jnp.exp(la)
            mult = jnp.sqrt(_lru_input_scale_sq(la, a))
            h = _scan_fwd(a, mult * (ii * u[rows, :]), hp, row)
            h_ref[rows, :] = h
            return _bcast_row(h, SUBLANES - 1)

        hcar[...] = _chunks(tb, SUBLANES, lru, hcar[...])
        rlf[...] = _head_rstd(h_ref[...], ol_ref[...], hd_l)

        g_c, g_l = spr(SP_CONV_G), spr(SP_LRU_G)

        def gate(r, carry):
            rows = pl.ds(r, 2 * s8)
            gc, gl = proj(rows, P_GC), proj(rows, P_GL)
            ybuf[rows, 0:d] = (yc[rows, :] * rcf[rows, :] * g_c * (gc * _sigmoid(gc))).astype(MM)
            ybuf[rows, d:2 * d] = (h_ref[rows, :] * rlf[rows, :] * g_l * (gl * _sigmoid(gl))).astype(MM)
            return carry

        _chunks(tb, 2 * s8, gate, 0)

        hres = x_ref[...] + _dot(ybuf[...], wout_ref[...])
        rf = lax.rsqrt(jnp.mean(hres * hres, axis=-1, keepdims=True) + RMS_EPS)
        hn = hres * rf
        fg = spr(SP_FINAL_G)
        err = hn * fg - tgt_ref[...]
        dout = err * (1.0 / d)
        acc_ref[0:SUBLANES, :] += (err * err).reshape(tb // SUBLANES, SUBLANES, d).sum(axis=0)
        acc_ref[SUBLANES:2 * SUBLANES, :] += (dout * hn).reshape(tb // SUBLANES, SUBLANES, d).sum(axis=0)
        gd = dout * fg
        dhres = rf * (gd - hn * jnp.mean(gd * hn, axis=-1, keepdims=True))
        dh_ref[...] = dhres
        dhb_ref[...] = dhres.astype(MM)

    vm = pl.BlockSpec(memory_space=pltpu.VMEM)
    blk = lambda w: pl.BlockSpec((tb, w), lambda i: (i, 0))
    buf = pltpu.VMEM((tb, d), F32)
    car = pltpu.VMEM((SUBLANES, d), F32)
    return pl.pallas_call(
        body, name="forward", grid=(nb,),
        in_specs=[blk(d), blk(d), blk(6 * d), vm, vm, vm, vm, vm, vm],
        out_specs=(blk(d), blk(d), blk(d), pl.BlockSpec((2 * SUBLANES, d), lambda i: (0, 0)), blk(5 * d)),
        out_shape=(jax.ShapeDtypeStruct((t_len, d), F32),
                   jax.ShapeDtypeStruct((t_len, d), F32),
                   jax.ShapeDtypeStruct((t_len, d), MM),
                   jax.ShapeDtypeStruct((2 * SUBLANES, d), F32))
                  + (jax.ShapeDtypeStruct((t_len, 5 * d), F32),),
        scratch_shapes=[buf] * 2 + [pltpu.VMEM((tb, 2 * d), MM), car, car, car],
        compiler_params=_params(dimension_semantics=("arbitrary",)),
    )(x, tgt, p, wout, wa_t, wi_t, sp, ones_c, ones_l)


def _backward(p, h, dh, saved, facc, wout, wa_t, wi_t, sp, ones_c, ones_l, tb):
    t_len, d = h.shape
    nb = t_len // tb
    n_tiles, tw = wa_t.shape[0], wa_t.shape[1]
    hd_c, hd_l = d // N_CONV_HEADS, d // N_LRU_HEADS
    g_rows = 2 * n_tiles * hd_l
    s8 = SUBLANES

    def body(p_ref, h_ref, hhalo_ref, dh_ref, saved_ref, facc_ref,
             wout_ref, wa_ref, wi_ref, sp_ref, oc_ref, ol_ref,
             dp_ref, yt_ref, slab_v, slab_g,
             hh, dy, ybuf, rcf, rlf, qc, ql, dyc_hat, dyl_hat, dpa, dpi, du, gwa_ref, gwi_ref, acc_ref,
             car_dcz, car_a, car_g, car_du):
        i = pl.program_id(0)
        blk_idx = nb - 1 - i
        row = _row_iota(d)
        yc, czs, u, ra_ref, ii_ref = (saved_ref.at[:, k * d:(k + 1) * d] for k in range(5))

        @pl.when(i == 0)
        def _():
            for ref in (car_dcz, car_a, car_g, car_du, gwa_ref, gwi_ref, acc_ref):
                ref[...] = jnp.zeros_like(ref)

        def spr(r):
            return sp_ref[r:r + 1, :]

        def proj(rows, seg):
            return p_ref[rows, seg * d:(seg + 1) * d].astype(F32)

        def put(rows, seg, halves):
            dp_ref[rows, seg * d:(seg + 1) * d] = jnp.concatenate(halves, axis=0).astype(MM)

        def acc_add(group, val):
            acc_ref[group * s8:(group + 1) * s8, :] += val

        live = jnp.where(blk_idx > 0, 1.0, 0.0).astype(F32)
        hh[0:s8, :] = hhalo_ref[...] * live
        hh[s8:, :] = h_ref[...]

        dy[...] = _dot_nt(dh_ref[...].astype(MM), wout_ref[...])

        w0, w1, w2 = spr(SP_CONV_W), spr(SP_CONV_W + 1), spr(SP_CONV_W + 2)
        l0, l1, l2, l3 = spr(SP_LRU_W), spr(SP_LRU_W + 1), spr(SP_LRU_W + 2), spr(SP_LRU_W + 3)

        rcf[...] = _head_rstd(yc[...], oc_ref[...], hd_c)
        rlf[...] = _head_rstd(h_ref[...], ol_ref[...], hd_l)

        g_c, g_l = spr(SP_CONV_G), spr(SP_LRU_G)

        def gates(r, carry):
            rows = pl.ds(r, 2 * s8)
            for (seg, off_y, src, rstd, gain, q, dhat, grp) in (
                    (P_GC, 0, yc, rcf, g_c, qc, dyc_hat, A_CONV_G),
                    (P_GL, d, h_ref, rlf, g_l, ql, dyl_hat, A_LRU_G)):
                gt = proj(rows, seg)
                sg = _sigmoid(gt)
                silu = gt * sg
                yhat = src[rows, :] * rstd[rows, :]
                nrm = yhat * gain
                ybuf[rows, off_y:off_y + d] = nrm * silu
                dout = dy[rows, off_y:off_y + d]
                dnrm = dout * silu
                dp_ref[rows, seg * d:(seg + 1) * d] = (dout * nrm * (sg + silu * (1.0 - sg))).astype(MM)
                dg = dnrm * yhat
                acc_add(grp, dg[0:s8] + dg[s8:])
                dh_ = dnrm * gain
                dhat[rows, :] = dh_
                q[rows, :] = dh_ * yhat
            return carry

        _chunks(tb, 2 * s8, gates, 0)

        qc[...] = _head_sums(qc[...], oc_ref[...]) * (1.0 / hd_c)
        ql[...] = _head_sums(ql[...], ol_ref[...]) * (1.0 / hd_l)
        yt_ref[...] = ybuf[...].T.astype(MM)

        c8 = RG_LRU_C * _log_sigmoid(spr(SP_LAM))

        def conv_mixer(r, dcz_n):
            rows16 = pl.ds(r, 2 * s8)
            bg16, cg16, xc16 = proj(rows16, P_B), proj(rows16, P_C), proj(rows16, P_XC)
            z16 = cg16 * xc16
            d_b, d_c, d_x = [None, None], [None, None], [None, None]
            for j in (1, 0):
                rows, sub = pl.ds(r + j * s8, s8), slice(j * s8, (j + 1) * s8)
                rstd = rcf[rows, :]
                yhat = yc[rows, :] * rstd
                dyc = rstd * (dyc_hat[rows, :] - yhat * qc[rows, :])
                d_b[j] = dyc * czs[rows, :]
                dcz = dyc * bg16[sub]
                up1, up2 = _shift_up(dcz, dcz_n, 1, row), _shift_up(dcz, dcz_n, 2, row)
                dz = w2 * dcz + w1 * up1 + w0 * up2
                d_c[j] = dz * xc16[sub]
                d_x[j] = dz * cg16[sub]
                z = z16[sub]
                acc_add(A_CONV_W, up2 * z)
                acc_add(A_CONV_W + 1, up1 * z)
                acc_add(A_CONV_W + 2, dcz * z)
                dcz_n = dcz
            put(rows16, P_B, d_b)
            put(rows16, P_C, d_c)
            put(rows16, P_XC, d_x)
            return dcz_n

        car_dcz[...] = _chunks(tb, 2 * s8, conv_mixer, car_dcz[...], reverse=True)

        def lru_mixer(r, carry):
            a_n, g_n = carry
            for j in (1, 0):
                rows = pl.ds(r + j * s8, s8)
                rstd = rlf[rows, :]
                hcur = hh[pl.ds(r + (j + 1) * s8, s8), :]
                hhat = hcur * rstd
                dh_out = rstd * (dyl_hat[rows, :] - hhat * ql[rows, :])
                ra = ra_ref[rows, :]
                la = ra * c8
                a = jnp.exp(la)
                g = _scan_bwd(_shift_up(a, a_n, 1, row), dh_out, g_n, row)
                da = g * _shift_down(hcur, hh[pl.ds(r + j * s8, s8), :], 1, row)
                ii = ii_ref[rows, :]
                uu = u[rows, :]
                mult_sq = _lru_input_scale_sq(la, a)
                inv_mult = lax.rsqrt(mult_sq)
                dmult = g * (ii * uu)
                ds = g * (mult_sq * inv_mult)
                dla = a * (da - dmult * a * inv_mult)
                acc_add(A_LAM, dla * ra)
                dpa_ = dla * c8 * ra * (1.0 - ra)
                dpi_ = ds * uu * ii * (1.0 - ii)
                acc_add(A_B_A, dpa_)
                acc_add(A_B_I, dpi_)
                dpa[rows, :] = dpa_
                dpi[rows, :] = dpi_
                du[rows, :] = ds * ii
                a_n, g_n = a, _bcast_row(g, 0)
            return a_n, g_n

        a_f, g_f = _chunks(tb, 2 * s8, lru_mixer, (car_a[...], car_g[...]), reverse=True)
        car_a[...] = a_f
        car_g[...] = g_f

        dpab = dpa[...].astype(MM)
        dpib = dpi[...].astype(MM)
        for k in range(n_tiles):
            sl = slice(k * tw, (k + 1) * tw)
            du[:, sl] += _dot_nt(dpab[:, sl], wa_ref[k]) + _dot_nt(dpib[:, sl], wi_ref[k])
            ut = u[:, sl].T.astype(MM)
            gwa_ref[k] += _dot(ut, dpab[:, sl])
            gwi_ref[k] += _dot(ut, dpib[:, sl])

        def lru_conv(r, du_n):
            rows16 = pl.ds(r, 2 * s8)
            xl16 = proj(rows16, P_XL)
            d_xl = [None, None]
            for j in (1, 0):
                rows, sub = pl.ds(r + j * s8, s8), slice(j * s8, (j + 1) * s8)
                dut = du[rows, :]
                up1, up2, up3 = (_shift_up(dut, du_n, s, row) for s in (1, 2, 3))
                d_xl[j] = l3 * dut + l2 * up1 + l1 * up2 + l0 * up3
                xl = xl16[sub]
                acc_add(A_LRU_W, up3 * xl)
                acc_add(A_LRU_W + 1, up2 * xl)
                acc_add(A_LRU_W + 2, up1 * xl)
                acc_add(A_LRU_W + 3, dut * xl)
                acc_add(A_LRU_B, dut)
                du_n = dut
            put(rows16, P_XL, d_xl)
            return du_n

        car_du[...] = _chunks(tb, 2 * s8, lru_conv, car_du[...], reverse=True)

        @pl.when(i == nb - 1)
        def _():
            def rowsum(ref, group):
                return jnp.sum(ref[group * s8:(group + 1) * s8, :], axis=0, keepdims=True)

            slab_v[...] = jnp.zeros_like(slab_v)
            loss = jnp.sum(rowsum(facc_ref, 0), axis=1, keepdims=True) * (0.5 / d)
            rows = {SL_LOSS: jnp.broadcast_to(loss, (1, d)), SL_FINAL_G: rowsum(facc_ref, 1),
                    SL_LRU_B: rowsum(acc_ref, A_LRU_B), SL_B_A: rowsum(acc_ref, A_B_A), SL_B_I: rowsum(acc_ref, A_B_I),
                    SL_LAM: rowsum(acc_ref, A_LAM), SL_CONV_G: rowsum(acc_ref, A_CONV_G), SL_LRU_G: rowsum(acc_ref, A_LRU_G)}
            for k in range(3):
                rows[SL_CONV_W + k] = rowsum(acc_ref, A_CONV_W + k)
            for k in range(4):
                rows[SL_LRU_W + k] = rowsum(acc_ref, A_LRU_W + k)
            for r, val in rows.items():
                slab_v[r:r + 1, :] = val
            head_of_lane = lax.broadcasted_iota(jnp.int32, (hd_l, tw), 1) // hd_l
            for mtx, g_ref in enumerate((gwa_ref, gwi_ref)):
                for k in range(n_tiles):
                    packed = jnp.zeros((hd_l, tw), F32)
                    for a in range(tw // hd_l):
                        packed = jnp.where(head_of_lane == a, g_ref[k, a * hd_l:(a + 1) * hd_l, :], packed)
                    slab_g[(mtx * n_tiles + k) * hd_l:(mtx * n_tiles + k + 1) * hd_l, :] = packed.astype(MM)

    vm = pl.BlockSpec(memory_space=pltpu.VMEM)
    rev = lambda w: pl.BlockSpec((tb, w), lambda i: (nb - 1 - i, 0))
    halo = lambda rows, w: pl.BlockSpec((rows, w), lambda i: (jnp.maximum((nb - 1 - i) * (tb // rows) - 1, 0), 0))
    const = lambda shape: pl.BlockSpec(shape, lambda i: (0,) * len(shape))
    buf = lambda w: pltpu.VMEM((tb, w), F32)
    car = pltpu.VMEM((SUBLANES, d), F32)
    return pl.pallas_call(
        body, name="backward", grid=(nb,),
        in_specs=[rev(6 * d), rev(d), halo(SUBLANES, d), rev(d), rev(5 * d)] + [vm, vm, vm, vm, vm, vm, vm],
        out_specs=(rev(6 * d), pl.BlockSpec((2 * d, tb), lambda i: (0, nb - 1 - i)),
                   const((SL_ROWS, d)), const((g_rows, tw))),
        out_shape=(jax.ShapeDtypeStruct((t_len, 6 * d), MM),
                   jax.ShapeDtypeStruct((2 * d, t_len), MM),
                   jax.ShapeDtypeStruct((SL_ROWS, d), F32),
                   jax.ShapeDtypeStruct((g_rows, tw), MM)),
        scratch_shapes=[pltpu.VMEM((SUBLANES + tb, d), F32), buf(2 * d), buf(2 * d)] + [buf(d)] * 9
                       + [pltpu.VMEM((n_tiles, tw, tw), F32), pltpu.VMEM((n_tiles, tw, tw), F32),
                          pltpu.VMEM((A_GROUPS * SUBLANES, d), F32), car, car, car, car],
        compiler_params=_params(dimension_semantics=("arbitrary",)),
    )(p, h, h, dh, saved, facc, wout, wa_t, wi_t, sp, ones_c, ones_l)


def _input_grad(dp, win_all, x, dh, sp, part, tb):
    t_len, d = x.shape
    nb = t_len // tb
    cols = win_all.shape[2]
    mid = min(nb - 1, (5 * nb) // 8)
    rc = 32

    def body(dp_ref, win_ref, x_ref, dh_ref, sp_ref, part_ref, gx_ref, ln_ref, direct, relayed,
             send_sems, recv_sems, local_sems, acc_ref, ln_all, ln_send, ln_recv, mine, theirs):
        i = pl.program_id(0)
        x_, y_, c_ = _mesh_pos()
        first, second = 1 - c_, c_
        nbr1 = (x_ ^ c_, y_ ^ (1 - c_), c_)
        nbr2 = (x_ ^ (1 - c_), y_ ^ c_, c_)

        def remote(src, dst, k, to):
            return pltpu.make_async_remote_copy(src_ref=src, dst_ref=dst, send_sem=send_sems.at[k], recv_sem=recv_sems.at[k],
                                                device_id=to, device_id_type=MESH)

        to_first = [remote(part_ref.at[first], direct, 0, nbr1), remote(part_ref.at[2], theirs, 1, nbr1)]
        to_second = remote(theirs, relayed, 2, nbr2)
        load_mine = pltpu.make_async_copy(part_ref.at[second], mine, local_sems.at[0])

        @pl.when(i == 0)
        def _():
            acc_ref[...] = jnp.zeros_like(acc_ref)
            for cp in to_first:
                cp.start()
            load_mine.start()

        dxn = _dot_nt(dp_ref[:, 0:cols], win_ref[0])
        for j in range(1, N_DEV):
            dxn += _dot_nt(dp_ref[:, j * cols:(j + 1) * cols], win_ref[j])
        xv = x_ref[...]
        r0 = lax.rsqrt(jnp.mean(xv * xv, axis=-1, keepdims=True) + RMS_EPS)
        xhat = xv * r0
        acc_ref[...] += (dxn * xhat).reshape(tb // SUBLANES, SUBLANES, d).sum(axis=0)
        dxh = dxn * sp_ref[SP_LN_G:SP_LN_G + 1, :]
        gx_ref[...] = dh_ref[...] + r0 * (dxh - xhat * jnp.mean(dxh * xhat, axis=-1, keepdims=True))

        @pl.when(i == mid)
        def _():
            to_first[1].wait_recv()
            load_mine.wait()

            def add(r, carry):
                rows = pl.ds(r, rc)
                theirs[rows, :] = (mine[rows, :].astype(F32) + theirs[rows, :].astype(F32)).astype(MM)
                return carry

            _chunks(mine.shape[0], rc, add, 0)
            to_second.start()

        @pl.when(i == nb - 1)
        def _():
            to_first[0].wait_recv()
            to_second.wait_recv()
            for cp in to_first + [to_second]:
                cp.wait_send()
            ln_all[4 * x_ + 2 * y_ + c_] = jnp.broadcast_to(jnp.sum(acc_ref[...], axis=0, keepdims=True), acc_ref.shape)
            gather = _Gather(lambda a, px, py, pc: ln_all.at[4 * px + 2 * py + pc], ln_send, ln_recv)
            gather.start_own(0)
            gather.finish(0)
            total = ln_all[0]
            for dev in range(1, N_DEV):
                total = total + ln_all[dev]
            ln_ref[...] = total

    vm = pl.BlockSpec(memory_space=pltpu.VMEM)
    hbm = pl.BlockSpec(memory_space=pl.ANY)
    blk = lambda w: pl.BlockSpec((tb, w), lambda i: (i, 0))
    landed = jax.ShapeDtypeStruct(part.shape[1:], part.dtype)
    outs = pl.pallas_call(
        body, name="input_grad", grid=(nb,),
        in_specs=[blk(6 * d), vm, blk(d), blk(d), vm, hbm],
        out_specs=(blk(d), pl.BlockSpec((SUBLANES, d), lambda i: (0, 0)), hbm, hbm),
        out_shape=(jax.ShapeDtypeStruct((t_len, d), F32), jax.ShapeDtypeStruct((SUBLANES, d), F32), landed, landed),
        scratch_shapes=[pltpu.SemaphoreType.DMA((3,)), pltpu.SemaphoreType.DMA((3,)), pltpu.SemaphoreType.DMA((1,)),
                        pltpu.VMEM((SUBLANES, d), F32), pltpu.VMEM((N_DEV, SUBLANES, d), F32),
                        pltpu.SemaphoreType.DMA((7,)), pltpu.SemaphoreType.DMA((7,)),
                        pltpu.VMEM(part.shape[1:], MM), pltpu.VMEM(part.shape[1:], MM)],
        compiler_params=_params(dimension_semantics=("arbitrary",)),
    )(dp, win_all, x, dh, sp, part)
    return outs[0], outs[1], (outs[2], outs[3])


_CHIP_RELATIONS = [(0, 0), (1, 0), (0, 1), (1, 1)]


def _related_block(k, core):
    x, y, _ = _mesh_pos()
    fx, fy = _CHIP_RELATIONS[k]
    return 4 * (x ^ fx) + 2 * (y ^ fy) + core


class _ChipExchange:
    def __init__(self, part_refs, land_refs, send_sems, recv_sems):
        self.part_refs, self.land_refs, self.send_sems, self.recv_sems = part_refs, land_refs, send_sems, recv_sems

    def copies(self):
        x, y, c = _mesh_pos()
        for a in range(len(self.part_refs)):
            for k in (1, 2, 3):
                fx, fy = _CHIP_RELATIONS[k]
                yield pltpu.make_async_remote_copy(
                    src_ref=self.part_refs[a].at[k - 1], dst_ref=self.land_refs[a].at[k - 1],
                    send_sem=self.send_sems.at[3 * a + k - 1], recv_sem=self.recv_sems.at[3 * a + k - 1],
                    device_id=(x ^ fx, y ^ fy, c), device_id_type=MESH)

    def start(self):
        for cp in self.copies():
            cp.start()

    def finish(self):
        for cp in self.copies():
            cp.wait_recv()
        for cp in self.copies():
            cp.wait_send()


def _weight_grad_stage1(name, blk_shape, n_split, operands, in_specs, product, riders=(), slabs=()):
    n_rows, n_cols = blk_shape
    rs = n_rows // n_split
    rc = 32
    n_in, n_ride, n_slab = len(operands), len(riders), len(slabs)
    _, _, c = _mesh_pos()
    order = jnp.stack([_related_block(k, 1 - c) for k in range(4)]
                      + [_related_block(k, c) for k in (1, 2, 3, 0)]).astype(jnp.int32)

    def body(order_ref, *refs):
        ins = refs[:n_in]
        ride_in = refs[n_in:n_in + n_ride]
        slab_in = refs[n_in + n_ride:n_in + n_ride + n_slab]
        n_op = n_in + n_ride + n_slab
        part_ref, own_ref = refs[n_op:n_op + 2]
        ride_out = refs[n_op + 2:n_op + 2 + n_ride]
        gathered = refs[n_op + 2 + n_ride:n_op + 2 + n_ride + n_slab]
        (gbuf, sendbuf, from_sib, send_sems, recv_sems, ride_send, ride_recv,
         slab_send, slab_recv, slab_local) = refs[n_op + 2 + n_ride + n_slab:]
        exchange = _ChipExchange(ride_in, ride_out, ride_send, ride_recv)
        s = pl.program_id(0)
        x, y, c = _mesh_pos()
        me = 4 * x + 2 * y + c
        gather = _BalancedGather(lambda a, px, py, pc: gathered[a].at[4 * px + 2 * py + pc], slab_send, slab_recv, slab_in)
        keep_own = [pltpu.make_async_copy(slab_in[a], gathered[a].at[me], slab_local.at[a]) for a in range(n_slab)]

        def to_sibling(k):
            return pltpu.make_async_remote_copy(
                src_ref=sendbuf.at[k], dst_ref=from_sib.at[k], send_sem=send_sems.at[k], recv_sem=recv_sems.at[k],
                device_id=(x, y, 1 - c), device_id_type=MESH)

        @pl.when(s == 0)
        def _():
            exchange.start()
            for a in range(n_slab):
                gather.start_own(a)
                keep_own[a].start()

        @pl.when(s == 5)
        def _():
            for a in range(n_slab):
                gather.on_neighbour(a, 0)
                gather.on_neighbour(a, 1)

        @pl.when(s == 7)
        def _():
            for a in range(n_slab):
                gather.on_diagonal(a)

        for h in range(n_split):
            gbuf[h * rs:(h + 1) * rs, :] = product(ins, h)

        @pl.when(s < 4)
        def _():
            def narrow(r, carry):
                sendbuf[s, pl.ds(r, rc), :] = gbuf[pl.ds(r, rc), :].astype(MM)
                return carry

            _chunks(n_rows, rc, narrow, 0)
            to_sibling(s).start()

        @pl.when(s >= 4)
        def _():
            k = jnp.where(s == 7, 0, s - 3)
            to_sibling(k).wait_recv()

            @pl.when(s < 7)
            def _():
                def add(r, carry):
                    rows = pl.ds(r, rc)
                    part_ref[0, rows, :] = (gbuf[rows, :] + from_sib[k, rows, :].astype(F32)).astype(MM)
                    return carry

                _chunks(n_rows, rc, add, 0)

            @pl.when(s == 7)
            def _():
                def add(r, carry):
                    rows = pl.ds(r, rc)
                    own_ref[rows, :] = gbuf[rows, :] + from_sib[0, rows, :].astype(F32)
                    return carry

                _chunks(n_rows, rc, add, 0)
                for kk in range(4):
                    to_sibling(kk).wait_send()
                exchange.finish()
                for a in range(n_slab):
                    gather.wait_sibling(a)
                    for j in range(3):
                        gather.wait_passed_on(a, j)
                    gather.wait_sends(a)
                    keep_own[a].wait()

    hbm = pl.BlockSpec(memory_space=pl.ANY)
    grid_spec = pltpu.PrefetchScalarGridSpec(
        num_scalar_prefetch=1, grid=(N_DEV,), in_specs=list(in_specs) + [hbm] * (n_ride + n_slab),
        out_specs=(pl.BlockSpec((1, n_rows, n_cols), lambda s, o: (jnp.clip(s - 4, 0, 2), 0, 0)),
                   pl.BlockSpec((n_rows, n_cols), lambda s, o: (0, 0))) + (hbm,) * (n_ride + n_slab),
        scratch_shapes=[pltpu.VMEM((n_rows, n_cols), F32), pltpu.VMEM((4, n_rows, n_cols), MM),
                        pltpu.VMEM((4, n_rows, n_cols), MM),
                        pltpu.SemaphoreType.DMA((4,)), pltpu.SemaphoreType.DMA((4,)),
                        pltpu.SemaphoreType.DMA((max(3 * n_ride, 1),)), pltpu.SemaphoreType.DMA((max(3 * n_ride, 1),)),
                        pltpu.SemaphoreType.DMA((max(8 * n_slab, 1),)), pltpu.SemaphoreType.DMA((max(8 * n_slab, 1),)),
                        pltpu.SemaphoreType.DMA((max(n_slab, 1),))])
    outs = pl.pallas_call(
        body, name=name, grid_spec=grid_spec,
        out_shape=(jax.ShapeDtypeStruct((3, n_rows, n_cols), MM), jax.ShapeDtypeStruct((n_rows, n_cols), F32))
                  + tuple(jax.ShapeDtypeStruct(p.shape, p.dtype) for p in riders)
                  + tuple(jax.ShapeDtypeStruct((N_DEV,) + a.shape, a.dtype) for a in slabs),
        compiler_params=_params(dimension_semantics=("arbitrary",)),
    )(order, *operands, *riders, *slabs)
    return outs[0], outs[1], outs[2:2 + n_ride], outs[2 + n_ride:]


def _weight_grad_in(xnt, dp, riders, slabs):
    d, t_len = xnt.shape
    cols = dp.shape[1] // N_DEV
    half = d // 2
    return _weight_grad_stage1(
        "weight_grad_in", (d, cols), 2, (xnt, dp),
        [pl.BlockSpec(memory_space=pltpu.VMEM), pl.BlockSpec((t_len, cols), lambda s, o: (0, o[s]))],
        lambda refs, h: _dot(refs[0][h * half:(h + 1) * half, :], refs[1][...]), riders, slabs)


def _weight_grad_out(yt, dhb):
    d2, t_len = yt.shape
    d = dhb.shape[1]
    rows = d2 // N_DEV
    return _weight_grad_stage1(
        "weight_grad_out", (rows, d), 1, (yt, dhb),
        [pl.BlockSpec((rows, t_len), lambda s, o: (o[s], 0)), pl.BlockSpec(memory_space=pltpu.VMEM)],
        lambda refs, h: _dot(refs[0][...], refs[1][...]))


def _update_shard(own, others, w, m, v, name):
    n_rows, n_cols = w.shape
    rb = min(256, n_rows)
    n_other = len(others)

    def body(own_ref, *refs):
        other_refs = refs[:n_other]
        w_ref, m_ref, v_ref, grad_ref, delta_ref, mo_ref, vo_ref = refs[n_other:]
        g = own_ref[...]
        for ref in other_refs:
            for k in range(ref.shape[0] if len(ref.shape) == 3 else 1):
                g = g + (ref[k] if len(ref.shape) == 3 else ref[...]).astype(F32)
        delta, m_new, v_new = _adamw(w_ref[...], g, m_ref[...], v_ref[...])
        grad_ref[...] = g
        delta_ref[...] = delta
        mo_ref[...] = m_new
        vo_ref[...] = v_new

    blk = pl.BlockSpec((rb, n_cols), lambda i: (i, 0))
    stacked = lambda n: pl.BlockSpec((n, rb, n_cols), lambda i: (0, i, 0))
    out = jax.ShapeDtypeStruct((n_rows, n_cols), F32)
    return pl.pallas_call(
        body, name=name, grid=(n_rows // rb,),
        in_specs=[blk] + [stacked(o.shape[0]) if o.ndim == 3 else blk for o in others] + [blk, blk, blk],
        out_specs=(blk, blk, blk, blk), out_shape=(out, out, out, out),
        compiler_params=_params(dimension_semantics=("arbitrary",)),
    )(own, *others, w, m, v)


def _small_update(gat_v, gat_g, ln_tot, vec_w, vec_m, vec_v, gates, convs):
    n_vec = len(vec_w)
    n_heads, hd, _ = gates[0].shape
    tw = gat_g.shape[2]
    s8 = SUBLANES
    per = tw // hd
    n_tiles = n_heads // per
    cc = convs[0].shape[1]
    n_in = 3 + 3 * n_vec + 12

    def body(*refs):
        gv_ref, gg_ref, ln_ref = refs[:3]
        w_refs, m_refs, v_refs = (refs[3 + j * n_vec:3 + (j + 1) * n_vec] for j in range(3))
        gate_refs = refs[3 + 3 * n_vec:3 + 3 * n_vec + 6]
        conv_refs = refs[3 + 3 * n_vec + 6:n_in]
        loss_o = refs[n_in]
        kinds = [refs[n_in + 1 + j * (n_vec + 4):n_in + 1 + (j + 1) * (n_vec + 4)] for j in range(4)]
        tv, tg = refs[n_in + 1 + 4 * (n_vec + 4):]
        x, y, c = _mesh_pos()
        me = 4 * x + 2 * y + c

        def emit(k_out, w, g, m, v):
            delta, m_new, v_new = _adamw(w, g, m, v)
            for ref, val in zip(k_out, (g, delta, m_new, v_new)):
                ref[...] = val

        total = gv_ref[0]
        for dev in range(1, N_DEV):
            total = total + gv_ref[dev]
        tv[...] = total
        tv[SL_LN_G:SL_LN_G + 1, :] = ln_ref[0:1, :]

        def sum_gates(r, carry):
            rows = pl.ds(r, 2 * s8)
            part = gg_ref[0, rows, :].astype(F32)
            for dev in range(1, N_DEV):
                part = part + gg_ref[dev, rows, :].astype(F32)
            tg[rows, :] = part
            return carry

        _chunks(tg.shape[0], 2 * s8, sum_gates, 0)
        loss_o[...] = jnp.broadcast_to(tv[SL_LOSS:SL_LOSS + 1, 0:LANES], loss_o.shape)
        for p in range(n_vec):
            w, g = w_refs[p][...], tv[SL_LN_G + p, :]
            if SL_LN_G + p == SL_LAM:
                g = g * (RG_LRU_C * jax.nn.sigmoid(-w))
            emit([k_out[p] for k_out in kinds], w, g, m_refs[p][...], v_refs[p][...])
        lanes = pl.ds(pl.multiple_of(me * cc, cc), cc)
        for j, (row0, n) in enumerate(((SL_CONV_W, 3), (SL_LRU_W, 4))):
            w_ref, m_ref, v_ref = conv_refs[3 * j:3 * j + 3]
            emit([k_out[n_vec + 2 + j] for k_out in kinds], w_ref[...], tv[row0:row0 + n, lanes], m_ref[...], v_ref[...])
        for mtx in range(2):
            w_ref, m_ref, v_ref = gate_refs[3 * mtx:3 * mtx + 3]
            for k in range(n_tiles):
                tile = tg[(mtx * n_tiles + k) * hd:(mtx * n_tiles + k + 1) * hd, :]
                for a in range(per):
                    head = k * per + a
                    g = tile[:, a * hd:(a + 1) * hd]
                    delta, m_new, v_new = _adamw(w_ref[head], g, m_ref[head], v_ref[head])
                    for k_out, val in zip(kinds, (g, delta, m_new, v_new)):
                        k_out[n_vec + mtx][head] = val

    vm = pl.BlockSpec(memory_space=pltpu.VMEM)
    like = lambda a: jax.ShapeDtypeStruct(a.shape, F32)
    per_kind = tuple(like(a) for a in vec_w) + (like(gates[0]), like(gates[3]), like(convs[0]), like(convs[3]))
    n_out = 1 + 4 * len(per_kind)
    outs = pl.pallas_call(
        body, name="small_update",
        in_specs=[vm] * n_in, out_specs=(vm,) * n_out,
        out_shape=(jax.ShapeDtypeStruct((SUBLANES, LANES), F32),) + per_kind * 4,
        scratch_shapes=[pltpu.VMEM(gat_v.shape[1:], F32), pltpu.VMEM(gat_g.shape[1:], F32)],
        compiler_params=_params(),
    )(gat_v, gat_g, ln_tot, *vec_w, *vec_m, *vec_v, *gates, *convs)
    return outs[0], [outs[1 + j * len(per_kind):1 + (j + 1) * len(per_kind)] for j in range(4)]


def _head_ones(head_dim, tw):
    lane = jnp.arange(tw) // head_dim
    return (lane[:, None] == lane[None, :]).astype(MM)


def kernel(x, ln_g, w_in, conv_w, lru_conv_w, lru_conv_b, w_a, b_a, w_i, b_i, lam, conv_out_g, lru_out_g, w_out, final_g, loss_target, m_ln_g, m_w_in, m_conv_w, m_lru_conv_w, m_lru_conv_b, m_w_a, m_b_a, m_w_i, m_b_i, m_lam, m_conv_out_g, m_lru_out_g, m_w_out, m_final_g, v_ln_g, v_w_in, v_conv_w, v_lru_conv_w, v_lru_conv_b, v_w_a, v_b_a, v_w_i, v_b_i, v_lam, v_conv_out_g, v_lru_out_g, v_w_out, v_final_g):
    _, t_len, d = x.shape
    hd_l = d // N_LRU_HEADS
    tw = min(MXU_TILE, d)
    x2, tgt2 = x[0], loss_target[0]

    small = [ln_g, lru_conv_b, b_a, b_i, lam, conv_out_g, lru_out_g, final_g]
    p, xnt, win_all, wout_all, _, sp, wa_t, wi_t = _gather_project(
        x2, w_in, w_out, conv_w, lru_conv_w, ln_g.reshape(1, d), w_a, w_i, small, min(512, t_len), tw)
    wout_full = wout_all.reshape(N_DEV * w_out.shape[0], d)
    ones_c, ones_l = _head_ones(d // N_CONV_HEADS, tw), _head_ones(hd_l, tw)

    h, dh, dhb, facc, saved = _forward(x2, tgt2, p, wout_full, wa_t, wi_t, sp, ones_c, ones_l, min(256, t_len))
    dp, yt, slab_v, slab_g = _backward(p, h, dh, saved, facc, wout_full, wa_t, wi_t, sp, ones_c, ones_l, min(256, t_len))
    part_out, own_out, _, _ = _weight_grad_out(yt, dhb)
    part_in, own_in, (chips_out,), (gat_v, gat_g) = _weight_grad_in(xnt, dp, (part_out,), (slab_v, slab_g))
    grad_x, ln_tot, sums_in = _input_grad(dp, win_all, x2, dh, sp, part_in, min(512, t_len))
    gw_in, dw_in, mw_in, vw_in = _update_shard(own_in, sums_in, w_in, m_w_in, v_w_in, "update_w_in")
    gw_out, dw_out, mw_out, vw_out = _update_shard(own_out, (chips_out,), w_out, m_w_out, v_w_out, "update_w_out")

    loss_tile, kinds = _small_update(
        gat_v, gat_g, ln_tot, small,
        [m_ln_g, m_lru_conv_b, m_b_a, m_b_i, m_lam, m_conv_out_g, m_lru_out_g, m_final_g],
        [v_ln_g, v_lru_conv_b, v_b_a, v_b_i, v_lam, v_conv_out_g, v_lru_out_g, v_final_g],
        (w_a, m_w_a, v_w_a, w_i, m_w_i, v_w_i), (conv_w, m_conv_w, v_conv_w, lru_conv_w, m_lru_conv_w, v_lru_conv_w))

    def unpack(kind, big_in, big_out):
        vec, (wa_, wi_, cw_, lw_) = kind[:len(small)], kind[len(small):]
        return [vec[0], big_in, cw_, lw_, vec[1], wa_, vec[2], wi_, vec[3], vec[4], vec[5], vec[6], big_out, vec[7]]

    return (loss_tile[0, 0], grad_x[None], *unpack(kinds[0], gw_in, gw_out), *unpack(kinds[1], dw_in, dw_out),
            *unpack(kinds[2], mw_in, mw_out), *unpack(kinds[3], vw_in, vw_out))
```

```python
import jax
import jax.numpy as jnp
from jax import lax
from jax.experimental import pallas as pl
from jax.experimental.pallas import tpu as pltpu

F32 = jnp.float32
MM = jnp.bfloat16
MESH = pl.DeviceIdType.MESH

N_DEV = 8
N_CONV_HEADS = 8
N_LRU_HEADS = 16
RG_LRU_C = 8.0
RMS_EPS = 1e-6
ADAM_LR, ADAM_B1, ADAM_B2, ADAM_EPS, ADAM_WD, ADAM_STEP = 0.001, 0.9, 0.999, 1e-08, 0.01, 10
ADAM_BC1 = 1.0 - ADAM_B1 ** ADAM_STEP
ADAM_BC2 = 1.0 - ADAM_B2 ** ADAM_STEP

SUBLANES = 8
LANES = 128
MXU_TILE = 256
VMEM_LIMIT = 56 * 1024 * 1024

SP_LN_G, SP_LRU_B, SP_B_A, SP_B_I, SP_LAM, SP_CONV_G, SP_LRU_G, SP_FINAL_G, SP_CONV_W, SP_LRU_W = 0, 1, 2, 3, 4, 5, 6, 7, 8, 11
SP_ROWS = 16
P_B, P_C, P_XC, P_GC, P_XL, P_GL = 0, 1, 2, 3, 4, 5
A_CONV_G, A_LRU_G, A_LAM, A_B_A, A_B_I, A_CONV_W, A_LRU_W, A_LRU_B = 0, 1, 2, 3, 4, 5, 8, 12
A_GROUPS = 13
SL_LOSS, SL_LN_G, SL_LRU_B, SL_B_A, SL_B_I, SL_LAM, SL_CONV_G, SL_LRU_G, SL_FINAL_G, SL_CONV_W, SL_LRU_W = 0, 1, 2, 3, 4, 5, 6, 7, 8, 16, 24
SL_ROWS = 32


def _params(vmem=True, **kw):
    if vmem:
        kw["vmem_limit_bytes"] = VMEM_LIMIT
    return pltpu.CompilerParams(**kw)


def _dot(a, b):
    return jnp.dot(a, b, preferred_element_type=F32)


def _dot_nt(a, b):
    return lax.dot_general(a, b, (((1,), (1,)), ((), ())), preferred_element_type=F32)


def _head_sums(v, ones_tile):
    tw = ones_tile.shape[0]
    vb = v.astype(MM)
    return jnp.concatenate([_dot(vb[:, k:k + tw], ones_tile) for k in range(0, v.shape[1], tw)], axis=1)


def _head_rstd(v, ones_tile, head_dim):
    return lax.rsqrt(_head_sums(v * v, ones_tile) * (1.0 / head_dim) + RMS_EPS)


def _sigmoid(x):
    return 0.5 * jnp.tanh(0.5 * x) + 0.5


def _lru_input_scale_sq(log_a, a):
    return -jnp.tanh(log_a) * (1.0 + a * a)


def _log_sigmoid(x):
    z = jnp.exp(-jnp.abs(x))
    u = 1.0 + z
    log1p_z = jnp.where(u == 1.0, z, jnp.log(u) * (z / (u - 1.0)))
    return jnp.minimum(x, 0.0) - log1p_z


def _row_iota(d):
    return lax.broadcasted_iota(jnp.int32, (SUBLANES, d), 0)


def _shift_down(cur, prev, s, row):
    return jnp.where(row >= s, pltpu.roll(cur, s, axis=0), pltpu.roll(prev, s, axis=0))


def _shift_up(cur, nxt, s, row):
    k = SUBLANES - s
    return jnp.where(row < k, pltpu.roll(cur, k, axis=0), pltpu.roll(nxt, k, axis=0))


def _scan_fwd(a, b, h_prev, row):
    for s in (1, 2, 4):
        a_s = jnp.where(row >= s, pltpu.roll(a, s, axis=0), 1.0)
        b_s = jnp.where(row >= s, pltpu.roll(b, s, axis=0), 0.0)
        b = a * b_s + b
        a = a * a_s
    return a * h_prev + b


def _scan_bwd(a_next, b, g_next, row):
    a = a_next
    for s in (1, 2, 4):
        k = SUBLANES - s
        a_s = jnp.where(row < k, pltpu.roll(a, k, axis=0), 1.0)
        b_s = jnp.where(row < k, pltpu.roll(b, k, axis=0), 0.0)
        b = a * b_s + b
        a = a * a_s
    return a * g_next + b


def _bcast_row(v, r):
    return jnp.broadcast_to(v[r:r + 1, :], v.shape)


def _chunks(n_rows, rc, body, init, reverse=False):
    n = n_rows // rc

    def step(i, carry):
        j = (n - 1 - i) if reverse else i
        return body(pl.multiple_of(j * rc, rc), carry)

    return lax.fori_loop(0, n, step, init)


def _adamw(w, g, m, v):
    m = ADAM_B1 * m + (1.0 - ADAM_B1) * g
    v = ADAM_B2 * v + (1.0 - ADAM_B2) * (g * g)
    m_hat = m / ADAM_BC1
    v_hat = v / ADAM_BC2
    delta = -ADAM_LR * (m_hat / (jnp.sqrt(v_hat) + ADAM_EPS) + ADAM_WD * w)
    return delta, m, v


def _mesh_pos():
    return lax.axis_index("x"), lax.axis_index("y"), lax.axis_index("c")


class _Gather:
    def __init__(self, blocks_of, send_sems, recv_sems, own_src=None):
        x, y, c = _mesh_pos()
        self.c = c
        self.me, self.sibling = (x, y, c), (x, y, 1 - c)
        self.chips = [(1 - x, y), (x, 1 - y), (1 - x, 1 - y)]
        self.blocks_of, self.send_sems, self.recv_sems = blocks_of, send_sems, recv_sems
        self.own_src = own_src

    def copy(self, a, k, block, to):
        src = self.blocks_of(a, *block)
        if block is self.me and self.own_src is not None:
            src = self.own_src[a]
        return pltpu.make_async_remote_copy(
            src_ref=src, dst_ref=self.blocks_of(a, *block),
            send_sem=self.send_sems.at[a * 7 + k], recv_sem=self.recv_sems.at[a * 7 + k],
            device_id=to, device_id_type=MESH)

    def start_own(self, a):
        self.copy(a, 0, self.me, self.sibling).start()
        for j, chip in enumerate(self.chips):
            self.copy(a, 1 + j, self.me, (*chip, self.c)).start()

    def wait_sibling(self, a):
        self.copy(a, 0, self.sibling, self.me).wait_recv()

    def wait_chip_and_pass_on(self, a, j):
        block = (*self.chips[j], self.c)
        self.copy(a, 1 + j, block, self.me).wait_recv()
        self.copy(a, 4 + j, block, self.sibling).start()

    def wait_passed_on(self, a, j):
        self.copy(a, 4 + j, (*self.chips[j], 1 - self.c), self.me).wait_recv()

    def wait_sends(self, a):
        self.copy(a, 0, self.me, self.sibling).wait_send()
        for j, chip in enumerate(self.chips):
            self.copy(a, 1 + j, self.me, (*chip, self.c)).wait_send()
            self.copy(a, 4 + j, (*chip, self.c), self.sibling).wait_send()

    def finish(self, a):
        for j in range(3):
            self.wait_chip_and_pass_on(a, j)
        self.wait_sibling(a)
        for j in range(3):
            self.wait_passed_on(a, j)
        self.wait_sends(a)


class _BalancedGather:
    def __init__(self, slot, send_sems, recv_sems, own_src):
        x, y, c = _mesh_pos()
        self.c = c
        self.me, self.sibling = (x, y, c), (x, y, 1 - c)
        self.chips = [(1 - x, y), (x, 1 - y), (1 - x, 1 - y)]
        self.slot, self.send_sems, self.recv_sems, self.own_src = slot, send_sems, recv_sems, own_src

    def half(self, a, block, which):
        ref = self.slot(a, *block)
        n = ref.shape[0] // 2
        return ref.at[pl.ds(which * n, n)]

    def copy(self, a, k, src, dst, to):
        return pltpu.make_async_remote_copy(
            src_ref=src, dst_ref=dst, send_sem=self.send_sems.at[a * 8 + k], recv_sem=self.recv_sems.at[a * 8 + k],
            device_id=to, device_id_type=MESH)

    def whole(self, a, k, block, to):
        src = self.own_src[a] if block is self.me else self.slot(a, *block)
        return self.copy(a, k, src, self.slot(a, *block), to)

    def halved(self, a, k, block, which, to):
        return self.copy(a, k, self.half(a, block, which), self.half(a, block, which), to)

    def on(self, chip):
        return (*self.chips[chip], self.c)

    def start_own(self, a):
        self.whole(a, 0, self.me, self.sibling).start()
        self.whole(a, 1, self.me, self.on(0)).start()
        self.whole(a, 2, self.me, self.on(1)).start()

    def start_own_staggered(self, a):
        self.whole(a, 0, self.me, self.sibling).start()
        for core, order in ((1, (0, 1)), (0, (1, 0))):
            @pl.when(self.c == core)
            def _(order=order):
                for j in order:
                    self.whole(a, 1 + j, self.me, self.on(j)).start()

    def wait_sibling(self, a):
        self.whole(a, 0, self.sibling, self.me).wait_recv()

    def on_neighbour(self, a, j):
        self.whole(a, 1 + j, self.on(j), self.me).wait_recv()
        self.halved(a, 3 + j, self.on(j), j, self.on(1 - j)).start()
        self.whole(a, 5 + j, self.on(j), self.sibling).start()

    def on_diagonal(self, a):
        self.halved(a, 3, self.on(2), 0, self.me).wait_recv()
        self.halved(a, 4, self.on(2), 1, self.me).wait_recv()
        self.whole(a, 7, self.on(2), self.sibling).start()

    def wait_passed_on(self, a, j):
        self.whole(a, 5 + j, (*self.chips[j], 1 - self.c), self.me).wait_recv()

    def wait_sends(self, a):
        self.whole(a, 0, self.me, self.sibling).wait_send()
        for j in range(2):
            self.whole(a, 1 + j, self.me, self.on(j)).wait_send()
            self.halved(a, 3 + j, self.on(j), j, self.on(1 - j)).wait_send()
        for j in range(3):
            self.whole(a, 5 + j, self.on(j), self.sibling).wait_send()


def _block_order():
    x, y, c = _mesh_pos()
    idx = lambda chip, core: 4 * chip[0] + 2 * chip[1] + core
    own, first, second, diag = (x, y), (x ^ c, y ^ (1 - c)), (x ^ (1 - c), y ^ c), (1 - x, 1 - y)
    order = [idx(own, c), idx(own, 1 - c), idx(first, c), idx(second, 1 - c), idx(second, c), idx(first, 1 - c),
             idx(diag, c), idx(diag, 1 - c)]
    return jnp.stack(order).astype(jnp.int32)


def _gather_project(x, w_in, w_out, conv_w, lru_conv_w, ln_g, w_a, w_i, vecs, tb, tw):
    t_len, d = x.shape
    nb = t_len // tb
    cols = w_in.shape[1]
    mc = min(512, t_len)
    conv_pack = jax.ShapeDtypeStruct((SUBLANES, conv_w.shape[1]), F32)
    srcs = (w_in, w_out, conv_pack)
    dts = (MM, MM, F32)
    n_vec = len(vecs)
    n_heads, hd, _ = w_a.shape
    per = tw // hd

    def body(order_ref, x_ref, win_ref, wout_ref, cw_ref, lw_ref, lng_ref, wa_ref, wi_ref, *refs):
        vec_refs = refs[:n_vec]
        (p_ref, xnt_ref, win_all, wout_all, cp_all, sp_ref, wat_ref, wit_ref,
         xnb, wall, st_out, st_cp, cp_vm, send_sems, recv_sems, cp_send, cp_recv, local_sems) = refs[n_vec:]
        i = pl.program_id(0)
        x_, y_, c_ = _mesh_pos()
        me = 4 * x_ + 2 * y_ + c_
        outs = (win_all, wout_all, cp_all)
        lands = (wall, wout_all, cp_all)
        stages = (wall.at[me], st_out, st_cp)
        gather = _BalancedGather(lambda a, px, py, pc: lands[a].at[4 * px + 2 * py + pc], send_sems, recv_sems, stages)
        small = _Gather(lambda a, px, py, pc: cp_all.at[4 * px + 2 * py + pc], cp_send, cp_recv, own_src=[st_cp])
        keep_own = [pltpu.make_async_copy(stages[a], outs[a].at[me], local_sems.at[a]) for a in range(3)]

        def keep(k):
            blk = order_ref[k]
            return pltpu.make_async_copy(wall.at[blk], win_all.at[blk], local_sems.at[2 + k])

        @pl.when(i == 0)
        def _():
            for a, src in enumerate((win_ref, wout_ref)):
                dst, rc = stages[a], 32

                def cast(r, carry, src=src, dst=dst):
                    dst[pl.ds(r, rc), :] = src[pl.ds(r, rc), :].astype(dst.dtype)
                    return carry

                _chunks(src.shape[0], rc, cast, 0)
                keep_own[a].start()
            gather.start_own_staggered(0)
            n_cw, n_lw = cw_ref.shape[0], lw_ref.shape[0]
            st_cp[...] = jnp.zeros_like(st_cp)
            st_cp[0:n_cw, :] = cw_ref[...]
            st_cp[n_cw:n_cw + n_lw, :] = lw_ref[...]
            keep_own[2].start()

        @pl.when(i < nb)
        def _():
            xv = x_ref[...]
            r0 = lax.rsqrt(jnp.mean(xv * xv, axis=-1, keepdims=True) + RMS_EPS)
            xn = xv * r0 * lng_ref[...]
            xnb[pl.ds(pl.multiple_of(i * tb, tb), tb), :] = xn.astype(MM)
            xnt_ref[...] = xn.T.astype(MM)

        def by_core(action):
            for core, (first, second) in ((1, (0, 1)), (0, (1, 0))):
                @pl.when(c_ == core)
                def _(first=first, second=second):
                    action(first, second)

        for k in range(N_DEV):
            @pl.when(i == nb + k)
            def _(k=k):
                if k == 1:
                    gather.wait_sibling(0)
                elif k == 2:
                    by_core(lambda first, second: gather.on_neighbour(0, first))
                    gather.start_own(1)
                    small.start_own(0)
                elif k == 3:
                    by_core(lambda first, second: gather.wait_passed_on(0, second))
                elif k == 4:
                    by_core(lambda first, second: gather.on_neighbour(0, second))
                elif k == 5:
                    by_core(lambda first, second: gather.wait_passed_on(0, first))
                    gather.on_neighbour(1, 0)
                    gather.on_neighbour(1, 1)
                elif k == 6:
                    gather.on_diagonal(0)
                elif k == 7:
                    gather.wait_passed_on(0, 2)
                    gather.on_diagonal(1)
                blk = order_ref[k]
                if k:
                    keep(k).start()

                def project(r, carry):
                    rows = pl.ds(r, mc)
                    p_ref[rows, :] = _dot(xnb[rows, :], wall[blk]).astype(MM)
                    return carry

                _chunks(t_len, mc, project, 0)
                if k == N_DEV - 1:
                    gather.wait_sends(0)
                    gather.wait_sibling(1)
                    for j in range(3):
                        gather.wait_passed_on(1, j)
                    gather.wait_sends(1)
                    small.finish(0)
                    for cp in keep_own + [keep(kk) for kk in range(1, N_DEV)]:
                        cp.wait()
                    load = pltpu.make_async_copy(cp_all, cp_vm, local_sems.at[N_DEV + 2])
                    load.start()
                    load.wait()
                    for r, ref in enumerate(vec_refs):
                        sp_ref[r, :] = ref[...]
                    sp_ref[n_vec:n_vec + SUBLANES, :] = jnp.concatenate([cp_vm[dev] for dev in range(N_DEV)], axis=1)
                    for src, dst in ((wa_ref, wat_ref), (wi_ref, wit_ref)):
                        dst[...] = jnp.zeros_like(dst)
                        for head in range(n_heads):
                            lo = (head % per) * hd
                            dst[head // per, lo:lo + hd, lo:lo + hd] = src[head].astype(MM)

    vm = pl.BlockSpec(memory_space=pltpu.VMEM)
    hbm = pl.BlockSpec(memory_space=pl.ANY)
    grid_spec = pltpu.PrefetchScalarGridSpec(
        num_scalar_prefetch=1, grid=(nb + N_DEV,),
        in_specs=[pl.BlockSpec((tb, d), lambda i, o: (jnp.minimum(i, nb - 1), 0))] + [vm] * (7 + n_vec),
        out_specs=(pl.BlockSpec((t_len, cols), lambda i, o: (0, o[jnp.maximum(i - nb, 0)])),
                   pl.BlockSpec((d, tb), lambda i, o: (0, jnp.minimum(i, nb - 1))), hbm, hbm, hbm,
                   pl.BlockSpec((SP_ROWS, d), lambda i, o: (0, 0)),
                   pl.BlockSpec((n_heads // per, tw, tw), lambda i, o: (0, 0, 0)),
                   pl.BlockSpec((n_heads // per, tw, tw), lambda i, o: (0, 0, 0))),
        scratch_shapes=[pltpu.VMEM((t_len, d), MM), pltpu.VMEM((N_DEV,) + w_in.shape, MM),
                        pltpu.VMEM(w_out.shape, MM), pltpu.VMEM(conv_pack.shape, F32),
                        pltpu.VMEM((N_DEV,) + conv_pack.shape, F32),
                        pltpu.SemaphoreType.DMA((16,)), pltpu.SemaphoreType.DMA((16,)),
                        pltpu.SemaphoreType.DMA((7,)), pltpu.SemaphoreType.DMA((7,)), pltpu.SemaphoreType.DMA((N_DEV + 3,))])
    return pl.pallas_call(
        body, name="gather_project", grid_spec=grid_spec,
        out_shape=(jax.ShapeDtypeStruct((t_len, N_DEV * cols), MM),
                   jax.ShapeDtypeStruct((d, t_len), MM))
                  + tuple(jax.ShapeDtypeStruct((N_DEV,) + s.shape, dt) for s, dt in zip(srcs, dts))
                  + (jax.ShapeDtypeStruct((SP_ROWS, d), F32),)
                  + (jax.ShapeDtypeStruct((n_heads // per, tw, tw), MM),) * 2,
        compiler_params=_params(dimension_semantics=("arbitrary",)),
    )(_block_order(), x, w_in, w_out, conv_w, lru_conv_w, ln_g, w_a, w_i, *vecs)


def _forward(x, tgt, p, wout, wa_t, wi_t, sp, ones_c, ones_l, tb):
    t_len, d = x.shape
    nb = t_len // tb
    n_tiles, tw = wa_t.shape[0], wa_t.shape[1]
    hd_c, hd_l = d // N_CONV_HEADS, d // N_LRU_HEADS
    s8 = SUBLANES

    def body(x_ref, tgt_ref, p_ref, wout_ref, wa_ref, wi_ref, sp_ref, oc_ref, ol_ref,
             h_ref, dh_ref, dhb_ref, acc_ref, yc, czs, u, pa, pi,
             rcf, rlf, ybuf, tail_z, tail_xl, hcar):
        i = pl.program_id(0)
        row = _row_iota(d)

        @pl.when(i == 0)
        def _():
            tail_z[...] = jnp.zeros_like(tail_z)
            tail_xl[...] = jnp.zeros_like(tail_xl)
            hcar[...] = jnp.zeros_like(hcar)
            acc_ref[...] = jnp.zeros_like(acc_ref)

        def spr(r):
            return sp_ref[r:r + 1, :]

        def proj(rows, seg):
            return p_ref[rows, seg * d:(seg + 1) * d].astype(F32)

        w0, w1, w2 = spr(SP_CONV_W), spr(SP_CONV_W + 1), spr(SP_CONV_W + 2)
        l0, l1, l2, l3 = spr(SP_LRU_W), spr(SP_LRU_W + 1), spr(SP_LRU_W + 2), spr(SP_LRU_W + 3)
        lb = spr(SP_LRU_B)

        def convs(r, carry):
            zp, xp = carry
            rows16 = pl.ds(r, 2 * s8)
            bg16, xl16 = proj(rows16, P_B), proj(rows16, P_XL)
            z16 = proj(rows16, P_C) * proj(rows16, P_XC)
            for j in range(2):
                rows, sub = pl.ds(r + j * s8, s8), slice(j * s8, (j + 1) * s8)
                z, xl = z16[sub], xl16[sub]
                cz = w0 * _shift_down(z, zp, 2, row) + w1 * _shift_down(z, zp, 1, row) + w2 * z
                czs[rows, :] = cz
                yc[rows, :] = bg16[sub] * cz
                u[rows, :] = (l0 * _shift_down(xl, xp, 3, row) + l1 * _shift_down(xl, xp, 2, row)
                              + l2 * _shift_down(xl, xp, 1, row) + l3 * xl + lb)
                zp, xp = z, xl
            return zp, xp

        z_last, xl_last = _chunks(tb, 2 * s8, convs, (tail_z[...], tail_xl[...]))
        tail_z[...] = z_last
        tail_xl[...] = xl_last

        ub = u[...].astype(MM)
        for k in range(n_tiles):
            sl = slice(k * tw, (k + 1) * tw)
            pa[:, sl] = _dot(ub[:, sl], wa_ref[k])
            pi[:, sl] = _dot(ub[:, sl], wi_ref[k])
        rcf[...] = _head_rstd(yc[...], oc_ref[...], hd_c)

        c8 = RG_LRU_C * _log_sigmoid(spr(SP_LAM))
        b_a, b_i = spr(SP_B_A), spr(SP_B_I)

        def lru(r, hp):
            rows = pl.ds(r, SUBLANES)
            ra = _sigmoid(pa[rows, :] + b_a)
            ii = _sigmoid(pi[rows, :] + b_i)
            pa[rows, :] = ra
            pi[rows, :] = ii
            la = ra * c8
            a = jnp.exp(la)
            mult = jnp.sqrt(_lru_input_scale_sq(la, a))
            h = _scan_fwd(a, mult * (ii * u[rows, :]), hp, row)
            h_ref[rows, :] = h
            return _bcast_row(h, SUBLANES - 1)

        hcar[...] = _chunks(tb, SUBLANES, lru, hcar[...])
        rlf[...] = _head_rstd(h_ref[...], ol_ref[...], hd_l)

        g_c, g_l = spr(SP_CONV_G), spr(SP_LRU_G)

        def gate(r, carry):
            rows = pl.ds(r, 2 * s8)
            gc, gl = proj(rows, P_GC), proj(rows, P_GL)
            ybuf[rows, 0:d] = (yc[rows, :] * rcf[rows, :] * g_c * (gc * _sigmoid(gc))).astype(MM)
            ybuf[rows, d:2 * d] = (h_ref[rows, :] * rlf[rows, :] * g_l * (gl * _sigmoid(gl))).astype(MM)
            return carry

        _chunks(tb, 2 * s8, gate, 0)

        hres = x_ref[...] + _dot(ybuf[...], wout_ref[...])
        rf = lax.rsqrt(jnp.mean(hres * hres, axis=-1, keepdims=True) + RMS_EPS)
        hn = hres * rf
        fg = spr(SP_FINAL_G)
        err = hn * fg - tgt_ref[...]
        dout = err * (1.0 / d)
        acc_ref[0:SUBLANES, :] += (err * err).reshape(tb // SUBLANES, SUBLANES, d).sum(axis=0)
        acc_ref[SUBLANES:2 * SUBLANES, :] += (dout * hn).reshape(tb // SUBLANES, SUBLANES, d).sum(axis=0)
        gd = dout * fg
        dhres = rf * (gd - hn * jnp.mean(gd * hn, axis=-1, keepdims=True))
        dh_ref[...] = dhres
        dhb_ref[...] = dhres.astype(MM)

    vm = pl.BlockSpec(memory_space=pltpu.VMEM)
    blk = lambda w: pl.BlockSpec((tb, w), lambda i: (i, 0))
    buf = pltpu.VMEM((tb, d), F32)
    car = pltpu.VMEM((SUBLANES, d), F32)
    return pl.pallas_call(
        body, name="forward", grid=(nb,),
        in_specs=[blk(d), blk(d), blk(6 * d), vm, vm, vm, vm, vm, vm],
        out_specs=(blk(d), blk(d), blk(d), pl.BlockSpec((2 * SUBLANES, d), lambda i: (0, 0))) + (blk(d),) * 5,
        out_shape=(jax.ShapeDtypeStruct((t_len, d), F32),
                   jax.ShapeDtypeStruct((t_len, d), F32),
                   jax.ShapeDtypeStruct((t_len, d), MM),
                   jax.ShapeDtypeStruct((2 * SUBLANES, d), F32))
                  + (jax.ShapeDtypeStruct((t_len, d), F32),) * 5,
        scratch_shapes=[buf] * 2 + [pltpu.VMEM((tb, 2 * d), MM), car, car, car],
        compiler_params=_params(dimension_semantics=("arbitrary",)),
    )(x, tgt, p, wout, wa_t, wi_t, sp, ones_c, ones_l)


def _backward(p, h, dh, saved, facc, wout, wa_t, wi_t, sp, ones_c, ones_l, tb):
    t_len, d = h.shape
    nb = t_len // tb
    n_tiles, tw = wa_t.shape[0], wa_t.shape[1]
    hd_c, hd_l = d // N_CONV_HEADS, d // N_LRU_HEADS
    g_rows = 2 * n_tiles * hd_l
    s8 = SUBLANES

    def body(p_ref, h_ref, hhalo_ref, dh_ref, yc, czs, u, ra_ref, ii_ref, facc_ref,
             wout_ref, wa_ref, wi_ref, sp_ref, oc_ref, ol_ref,
             dp_ref, yt_ref, slab_v, slab_g,
             hh, dy, ybuf, rcf, rlf, qc, ql, dyc_hat, dyl_hat, dpa, dpi, du, gwa_ref, gwi_ref, acc_ref,
             car_dcz, car_a, car_g, car_du):
        i = pl.program_id(0)
        blk_idx = nb - 1 - i
        row = _row_iota(d)

        @pl.when(i == 0)
        def _():
            for ref in (car_dcz, car_a, car_g, car_du, gwa_ref, gwi_ref, acc_ref):
                ref[...] = jnp.zeros_like(ref)

        def spr(r):
            return sp_ref[r:r + 1, :]

        def proj(rows, seg):
            return p_ref[rows, seg * d:(seg + 1) * d].astype(F32)

        def put(rows, seg, halves):
            dp_ref[rows, seg * d:(seg + 1) * d] = jnp.concatenate(halves, axis=0).astype(MM)

        def acc_add(group, val):
            acc_ref[group * s8:(group + 1) * s8, :] += val

        live = jnp.where(blk_idx > 0, 1.0, 0.0).astype(F32)
        hh[0:s8, :] = hhalo_ref[...] * live
        hh[s8:, :] = h_ref[...]

        dy[...] = _dot_nt(dh_ref[...].astype(MM), wout_ref[...])

        w0, w1, w2 = spr(SP_CONV_W), spr(SP_CONV_W + 1), spr(SP_CONV_W + 2)
        l0, l1, l2, l3 = spr(SP_LRU_W), spr(SP_LRU_W + 1), spr(SP_LRU_W + 2), spr(SP_LRU_W + 3)

        rcf[...] = _head_rstd(yc[...], oc_ref[...], hd_c)
        rlf[...] = _head_rstd(h_ref[...], ol_ref[...], hd_l)

        g_c, g_l = spr(SP_CONV_G), spr(SP_LRU_G)

        def gates(r, carry):
            rows = pl.ds(r, 2 * s8)
            for (seg, off_y, src, rstd, gain, q, dhat, grp) in (
                    (P_GC, 0, yc, rcf, g_c, qc, dyc_hat, A_CONV_G),
                    (P_GL, d, h_ref, rlf, g_l, ql, dyl_hat, A_LRU_G)):
                gt = proj(rows, seg)
                sg = _sigmoid(gt)
                silu = gt * sg
                yhat = src[rows, :] * rstd[rows, :]
                nrm = yhat * gain
                ybuf[rows, off_y:off_y + d] = nrm * silu
                dout = dy[rows, off_y:off_y + d]
                dnrm = dout * silu
                dp_ref[rows, seg * d:(seg + 1) * d] = (dout * nrm * (sg * (1.0 + gt * (1.0 - sg)))).astype(MM)
                dg = dnrm * yhat
                acc_add(grp, dg[0:s8] + dg[s8:])
                dh_ = dnrm * gain
                dhat[rows, :] = dh_
                q[rows, :] = dh_ * yhat
            return carry

        _chunks(tb, 2 * s8, gates, 0)

        qc[...] = _head_sums(qc[...], oc_ref[...]) * (1.0 / hd_c)
        ql[...] = _head_sums(ql[...], ol_ref[...]) * (1.0 / hd_l)
        yt_ref[...] = ybuf[...].T.astype(MM)

        c8 = RG_LRU_C * _log_sigmoid(spr(SP_LAM))

        def conv_mixer(r, dcz_n):
            rows16 = pl.ds(r, 2 * s8)
            bg16, cg16, xc16 = proj(rows16, P_B), proj(rows16, P_C), proj(rows16, P_XC)
            z16 = cg16 * xc16
            d_b, d_c, d_x = [None, None], [None, None], [None, None]
            for j in (1, 0):
                rows, sub = pl.ds(r + j * s8, s8), slice(j * s8, (j + 1) * s8)
                rstd = rcf[rows, :]
                yhat = yc[rows, :] * rstd
                dyc = rstd * (dyc_hat[rows, :] - yhat * qc[rows, :])
                d_b[j] = dyc * czs[rows, :]
                dcz = dyc * bg16[sub]
                up1, up2 = _shift_up(dcz, dcz_n, 1, row), _shift_up(dcz, dcz_n, 2, row)
                dz = w2 * dcz + w1 * up1 + w0 * up2
                d_c[j] = dz * xc16[sub]
                d_x[j] = dz * cg16[sub]
                z = z16[sub]
                acc_add(A_CONV_W, up2 * z)
                acc_add(A_CONV_W + 1, up1 * z)
                acc_add(A_CONV_W + 2, dcz * z)
                dcz_n = dcz
            put(rows16, P_B, d_b)
            put(rows16, P_C, d_c)
            put(rows16, P_XC, d_x)
            return dcz_n

        car_dcz[...] = _chunks(tb, 2 * s8, conv_mixer, car_dcz[...], reverse=True)

        def lru_mixer(r, carry):
            a_n, g_n = carry
            for j in (1, 0):
                rows = pl.ds(r + j * s8, s8)
                rstd = rlf[rows, :]
                hcur = hh[pl.ds(r + (j + 1) * s8, s8), :]
                hhat = hcur * rstd
                dh_out = rstd * (dyl_hat[rows, :] - hhat * ql[rows, :])
                ra = ra_ref[rows, :]
                la = ra * c8
                a = jnp.exp(la)
                g = _scan_bwd(_shift_up(a, a_n, 1, row), dh_out, g_n, row)
                da = g * _shift_down(hcur, hh[pl.ds(r + j * s8, s8), :], 1, row)
                ii = ii_ref[rows, :]
                uu = u[rows, :]
                mult_sq = _lru_input_scale_sq(la, a)
                inv_mult = lax.rsqrt(mult_sq)
                dmult = g * (ii * uu)
                ds = g * (mult_sq * inv_mult)
                dla = a * (da - dmult * a * inv_mult)
                acc_add(A_LAM, dla * ra)
                dpa_ = dla * c8 * ra * (1.0 - ra)
                dpi_ = ds * uu * ii * (1.0 - ii)
                acc_add(A_B_A, dpa_)
                acc_add(A_B_I, dpi_)
                dpa[rows, :] = dpa_
                dpi[rows, :] = dpi_
                du[rows, :] = ds * ii
                a_n, g_n = a, _bcast_row(g, 0)
            return a_n, g_n

        a_f, g_f = _chunks(tb, 2 * s8, lru_mixer, (car_a[...], car_g[...]), reverse=True)
        car_a[...] = a_f
        car_g[...] = g_f

        dpab = dpa[...].astype(MM)
        dpib = dpi[...].astype(MM)
        for k in range(n_tiles):
            sl = slice(k * tw, (k + 1) * tw)
            du[:, sl] += _dot_nt(dpab[:, sl], wa_ref[k]) + _dot_nt(dpib[:, sl], wi_ref[k])
            ut = u[:, sl].T.astype(MM)
            gwa_ref[k] += _dot(ut, dpab[:, sl])
            gwi_ref[k] += _dot(ut, dpib[:, sl])

        def lru_conv(r, du_n):
            rows16 = pl.ds(r, 2 * s8)
            xl16 = proj(rows16, P_XL)
            d_xl = [None, None]
            for j in (1, 0):
                rows, sub = pl.ds(r + j * s8, s8), slice(j * s8, (j + 1) * s8)
                dut = du[rows, :]
                up1, up2, up3 = (_shift_up(dut, du_n, s, row) for s in (1, 2, 3))
                d_xl[j] = l3 * dut + l2 * up1 + l1 * up2 + l0 * up3
                xl = xl16[sub]
                acc_add(A_LRU_W, up3 * xl)
                acc_add(A_LRU_W + 1, up2 * xl)
                acc_add(A_LRU_W + 2, up1 * xl)
                acc_add(A_LRU_W + 3, dut * xl)
                acc_add(A_LRU_B, dut)
                du_n = dut
            put(rows16, P_XL, d_xl)
            return du_n

        car_du[...] = _chunks(tb, 2 * s8, lru_conv, car_du[...], reverse=True)

        @pl.when(i == nb - 1)
        def _():
            def rowsum(ref, group):
                return jnp.sum(ref[group * s8:(group + 1) * s8, :], axis=0, keepdims=True)

            slab_v[...] = jnp.zeros_like(slab_v)
            loss = jnp.sum(rowsum(facc_ref, 0), axis=1, keepdims=True) * (0.5 / d)
            rows = {SL_LOSS: jnp.broadcast_to(loss, (1, d)), SL_FINAL_G: rowsum(facc_ref, 1),
                    SL_LRU_B: rowsum(acc_ref, A_LRU_B), SL_B_A: rowsum(acc_ref, A_B_A), SL_B_I: rowsum(acc_ref, A_B_I),
                    SL_LAM: rowsum(acc_ref, A_LAM), SL_CONV_G: rowsum(acc_ref, A_CONV_G), SL_LRU_G: rowsum(acc_ref, A_LRU_G)}
            for k in range(3):
                rows[SL_CONV_W + k] = rowsum(acc_ref, A_CONV_W + k)
            for k in range(4):
                rows[SL_LRU_W + k] = rowsum(acc_ref, A_LRU_W + k)
            for r, val in rows.items():
                slab_v[r:r + 1, :] = val
            head_of_lane = lax.broadcasted_iota(jnp.int32, (hd_l, tw), 1) // hd_l
            for mtx, g_ref in enumerate((gwa_ref, gwi_ref)):
                for k in range(n_tiles):
                    packed = jnp.zeros((hd_l, tw), F32)
                    for a in range(tw // hd_l):
                        packed = jnp.where(head_of_lane == a, g_ref[k, a * hd_l:(a + 1) * hd_l, :], packed)
                    slab_g[(mtx * n_tiles + k) * hd_l:(mtx * n_tiles + k + 1) * hd_l, :] = packed.astype(MM)

    vm = pl.BlockSpec(memory_space=pltpu.VMEM)
    rev = lambda w: pl.BlockSpec((tb, w), lambda i: (nb - 1 - i, 0))
    halo = lambda rows, w: pl.BlockSpec((rows, w), lambda i: (jnp.maximum((nb - 1 - i) * (tb // rows) - 1, 0), 0))
    const = lambda shape: pl.BlockSpec(shape, lambda i: (0,) * len(shape))
    buf = lambda w: pltpu.VMEM((tb, w), F32)
    car = pltpu.VMEM((SUBLANES, d), F32)
    return pl.pallas_call(
        body, name="backward", grid=(nb,),
        in_specs=[rev(6 * d), rev(d), halo(SUBLANES, d), rev(d)] + [rev(d)] * 5 + [vm, vm, vm, vm, vm, vm, vm],
        out_specs=(rev(6 * d), pl.BlockSpec((2 * d, tb), lambda i: (0, nb - 1 - i)),
                   const((SL_ROWS, d)), const((g_rows, tw))),
        out_shape=(jax.ShapeDtypeStruct((t_len, 6 * d), MM),
                   jax.ShapeDtypeStruct((2 * d, t_len), MM),
                   jax.ShapeDtypeStruct((SL_ROWS, d), F32),
                   jax.ShapeDtypeStruct((g_rows, tw), MM)),
        scratch_shapes=[pltpu.VMEM((SUBLANES + tb, d), F32), buf(2 * d), buf(2 * d)] + [buf(d)] * 9
                       + [pltpu.VMEM((n_tiles, tw, tw), F32), pltpu.VMEM((n_tiles, tw, tw), F32),
                          pltpu.VMEM((A_GROUPS * SUBLANES, d), F32), car, car, car, car],
        compiler_params=_params(dimension_semantics=("arbitrary",)),
    )(p, h, h, dh, *saved, facc, wout, wa_t, wi_t, sp, ones_c, ones_l)


def _input_grad(dp, win_all, x, dh, sp, part, tb):
    t_len, d = x.shape
    nb = t_len // tb
    cols = win_all.shape[2]
    mid = min(nb - 1, (5 * nb) // 8)
    rc = 32

    def body(dp_ref, win_ref, x_ref, dh_ref, sp_ref, part_ref, gx_ref, ln_ref, direct, relayed,
             send_sems, recv_sems, local_sems, acc_ref, ln_all, ln_send, ln_recv, mine, theirs):
        i = pl.program_id(0)
        x_, y_, c_ = _mesh_pos()
        first, second = 1 - c_, c_
        nbr1 = (x_ ^ c_, y_ ^ (1 - c_), c_)
        nbr2 = (x_ ^ (1 - c_), y_ ^ c_, c_)

        def remote(src, dst, k, to):
            return pltpu.make_async_remote_copy(src_ref=src, dst_ref=dst, send_sem=send_sems.at[k], recv_sem=recv_sems.at[k],
                                                device_id=to, device_id_type=MESH)

        to_first = [remote(part_ref.at[first], direct, 0, nbr1), remote(part_ref.at[2], theirs, 1, nbr1)]
        to_second = remote(theirs, relayed, 2, nbr2)
        load_mine = pltpu.make_async_copy(part_ref.at[second], mine, local_sems.at[0])

        @pl.when(i == 0)
        def _():
            acc_ref[...] = jnp.zeros_like(acc_ref)
            for cp in to_first:
                cp.start()
            load_mine.start()

        dxn = _dot_nt(dp_ref[:, 0:cols], win_ref[0])
        for j in range(1, N_DEV):
            dxn += _dot_nt(dp_ref[:, j * cols:(j + 1) * cols], win_ref[j])
        xv = x_ref[...]
        r0 = lax.rsqrt(jnp.mean(xv * xv, axis=-1, keepdims=True) + RMS_EPS)
        xhat = xv * r0
        acc_ref[...] += (dxn * xhat).reshape(tb // SUBLANES, SUBLANES, d).sum(axis=0)
        dxh = dxn * sp_ref[SP_LN_G:SP_LN_G + 1, :]
        gx_ref[...] = dh_ref[...] + r0 * (dxh - xhat * jnp.mean(dxh * xhat, axis=-1, keepdims=True))

        @pl.when(i == mid)
        def _():
            to_first[1].wait_recv()
            load_mine.wait()

            def add(r, carry):
                rows = pl.ds(r, rc)
                theirs[rows, :] = (mine[rows, :].astype(F32) + theirs[rows, :].astype(F32)).astype(MM)
                return carry

            _chunks(mine.shape[0], rc, add, 0)
            to_second.start()

        @pl.when(i == nb - 1)
        def _():
            to_first[0].wait_recv()
            to_second.wait_recv()
            for cp in to_first + [to_second]:
                cp.wait_send()
            ln_all[4 * x_ + 2 * y_ + c_] = jnp.broadcast_to(jnp.sum(acc_ref[...], axis=0, keepdims=True), acc_ref.shape)
            gather = _Gather(lambda a, px, py, pc: ln_all.at[4 * px + 2 * py + pc], ln_send, ln_recv)
            gather.start_own(0)
            gather.finish(0)
            total = ln_all[0]
            for dev in range(1, N_DEV):
                total = total + ln_all[dev]
            ln_ref[...] = total

    vm = pl.BlockSpec(memory_space=pltpu.VMEM)
    hbm = pl.BlockSpec(memory_space=pl.ANY)
    blk = lambda w: pl.BlockSpec((tb, w), lambda i: (i, 0))
    landed = jax.ShapeDtypeStruct(part.shape[1:], part.dtype)
    outs = pl.pallas_call(
        body, name="input_grad", grid=(nb,),
        in_specs=[blk(6 * d), vm, blk(d), blk(d), vm, hbm],
        out_specs=(blk(d), pl.BlockSpec((SUBLANES, d), lambda i: (0, 0)), hbm, hbm),
        out_shape=(jax.ShapeDtypeStruct((t_len, d), F32), jax.ShapeDtypeStruct((SUBLANES, d), F32), landed, landed),
        scratch_shapes=[pltpu.SemaphoreType.DMA((3,)), pltpu.SemaphoreType.DMA((3,)), pltpu.SemaphoreType.DMA((1,)),
                        pltpu.VMEM((SUBLANES, d), F32), pltpu.VMEM((N_DEV, SUBLANES, d), F32),
                        pltpu.SemaphoreType.DMA((7,)), pltpu.SemaphoreType.DMA((7,)),
                        pltpu.VMEM(part.shape[1:], MM), pltpu.VMEM(part.shape[1:], MM)],
        compiler_params=_params(dimension_semantics=("arbitrary",)),
    )(dp, win_all, x, dh, sp, part)
    return outs[0], outs[1], (outs[2], outs[3])


_CHIP_RELATIONS = [(0, 0), (1, 0), (0, 1), (1, 1)]


def _related_block(k, core):
    x, y, _ = _mesh_pos()
    fx, fy = _CHIP_RELATIONS[k]
    return 4 * (x ^ fx) + 2 * (y ^ fy) + core


class _ChipExchange:
    def __init__(self, part_refs, land_refs, send_sems, recv_sems):
        self.part_refs, self.land_refs, self.send_sems, self.recv_sems = part_refs, land_refs, send_sems, recv_sems

    def copies(self):
        x, y, c = _mesh_pos()
        for a in range(len(self.part_refs)):
            for k in (1, 2, 3):
                fx, fy = _CHIP_RELATIONS[k]
                yield pltpu.make_async_remote_copy(
                    src_ref=self.part_refs[a].at[k - 1], dst_ref=self.land_refs[a].at[k - 1],
                    send_sem=self.send_sems.at[3 * a + k - 1], recv_sem=self.recv_sems.at[3 * a + k - 1],
                    device_id=(x ^ fx, y ^ fy, c), device_id_type=MESH)

    def start(self):
        for cp in self.copies():
            cp.start()

    def finish(self):
        for cp in self.copies():
            cp.wait_recv()
        for cp in self.copies():
            cp.wait_send()


def _weight_grad_stage1(name, blk_shape, n_split, operands, in_specs, product, riders=(), slabs=()):
    n_rows, n_cols = blk_shape
    rs = n_rows // n_split
    rc = 32
    n_in, n_ride, n_slab = len(operands), len(riders), len(slabs)
    _, _, c = _mesh_pos()
    order = jnp.stack([_related_block(k, 1 - c) for k in range(4)]
                      + [_related_block(k, c) for k in (1, 2, 3, 0)]).astype(jnp.int32)

    def body(order_ref, *refs):
        ins = refs[:n_in]
        ride_in = refs[n_in:n_in + n_ride]
        slab_in = refs[n_in + n_ride:n_in + n_ride + n_slab]
        n_op = n_in + n_ride + n_slab
        part_ref, own_ref = refs[n_op:n_op + 2]
        ride_out = refs[n_op + 2:n_op + 2 + n_ride]
        gathered = refs[n_op + 2 + n_ride:n_op + 2 + n_ride + n_slab]
        (gbuf, sendbuf, from_sib, send_sems, recv_sems, ride_send, ride_recv,
         slab_send, slab_recv, slab_local) = refs[n_op + 2 + n_ride + n_slab:]
        exchange = _ChipExchange(ride_in, ride_out, ride_send, ride_recv)
        s = pl.program_id(0)
        x, y, c = _mesh_pos()
        me = 4 * x + 2 * y + c
        gather = _BalancedGather(lambda a, px, py, pc: gathered[a].at[4 * px + 2 * py + pc], slab_send, slab_recv, slab_in)
        keep_own = [pltpu.make_async_copy(slab_in[a], gathered[a].at[me], slab_local.at[a]) for a in range(n_slab)]

        def to_sibling(k):
            return pltpu.make_async_remote_copy(
                src_ref=sendbuf.at[k], dst_ref=from_sib.at[k], send_sem=send_sems.at[k], recv_sem=recv_sems.at[k],
                device_id=(x, y, 1 - c), device_id_type=MESH)

        @pl.when(s == 0)
        def _():
            exchange.start()
            for a in range(n_slab):
                gather.start_own(a)
                keep_own[a].start()

        @pl.when(s == 5)
        def _():
            for a in range(n_slab):
                gather.on_neighbour(a, 0)
                gather.on_neighbour(a, 1)

        @pl.when(s == 7)
        def _():
            for a in range(n_slab):
                gather.on_diagonal(a)

        for h in range(n_split):
            gbuf[h * rs:(h + 1) * rs, :] = product(ins, h)

        @pl.when(s < 4)
        def _():
            def narrow(r, carry):
                sendbuf[s, pl.ds(r, rc), :] = gbuf[pl.ds(r, rc), :].astype(MM)
                return carry

            _chunks(n_rows, rc, narrow, 0)
            to_sibling(s).start()

        @pl.when(s >= 4)
        def _():
            k = jnp.where(s == 7, 0, s - 3)
            to_sibling(k).wait_recv()

            @pl.when(s < 7)
            def _():
                def add(r, carry):
                    rows = pl.ds(r, rc)
                    part_ref[0, rows, :] = (gbuf[rows, :] + from_sib[k, rows, :].astype(F32)).astype(MM)
                    return carry

                _chunks(n_rows, rc, add, 0)

            @pl.when(s == 7)
            def _():
                def add(r, carry):
                    rows = pl.ds(r, rc)
                    own_ref[rows, :] = gbuf[rows, :] + from_sib[0, rows, :].astype(F32)
                    return carry

                _chunks(n_rows, rc, add, 0)
                for kk in range(4):
                    to_sibling(kk).wait_send()
                exchange.finish()
                for a in range(n_slab):
                    gather.wait_sibling(a)
                    for j in range(3):
                        gather.wait_passed_on(a, j)
                    gather.wait_sends(a)
                    keep_own[a].wait()

    hbm = pl.BlockSpec(memory_space=pl.ANY)
    grid_spec = pltpu.PrefetchScalarGridSpec(
        num_scalar_prefetch=1, grid=(N_DEV,), in_specs=list(in_specs) + [hbm] * (n_ride + n_slab),
        out_specs=(pl.BlockSpec((1, n_rows, n_cols), lambda s, o: (jnp.clip(s - 4, 0, 2), 0, 0)),
                   pl.BlockSpec((n_rows, n_cols), lambda s, o: (0, 0))) + (hbm,) * (n_ride + n_slab),
        scratch_shapes=[pltpu.VMEM((n_rows, n_cols), F32), pltpu.VMEM((4, n_rows, n_cols), MM),
                        pltpu.VMEM((4, n_rows, n_cols), MM),
                        pltpu.SemaphoreType.DMA((4,)), pltpu.SemaphoreType.DMA((4,)),
                        pltpu.SemaphoreType.DMA((max(3 * n_ride, 1),)), pltpu.SemaphoreType.DMA((max(3 * n_ride, 1),)),
                        pltpu.SemaphoreType.DMA((max(8 * n_slab, 1),)), pltpu.SemaphoreType.DMA((max(8 * n_slab, 1),)),
                        pltpu.SemaphoreType.DMA((max(n_slab, 1),))])
    outs = pl.pallas_call(
        body, name=name, grid_spec=grid_spec,
        out_shape=(jax.ShapeDtypeStruct((3, n_rows, n_cols), MM), jax.ShapeDtypeStruct((n_rows, n_cols), F32))
                  + tuple(jax.ShapeDtypeStruct(p.shape, p.dtype) for p in riders)
                  + tuple(jax.ShapeDtypeStruct((N_DEV,) + a.shape, a.dtype) for a in slabs),
        compiler_params=_params(dimension_semantics=("arbitrary",)),
    )(order, *operands, *riders, *slabs)
    return outs[0], outs[1], outs[2:2 + n_ride], outs[2 + n_ride:]


def _weight_grad_in(xnt, dp, riders, slabs):
    d, t_len = xnt.shape
    cols = dp.shape[1] // N_DEV
    half = d // 2
    return _weight_grad_stage1(
        "weight_grad_in", (d, cols), 2, (xnt, dp),
        [pl.BlockSpec(memory_space=pltpu.VMEM), pl.BlockSpec((t_len, cols), lambda s, o: (0, o[s]))],
        lambda refs, h: _dot(refs[0][h * half:(h + 1) * half, :], refs[1][...]), riders, slabs)


def _weight_grad_out(yt, dhb):
    d2, t_len = yt.shape
    d = dhb.shape[1]
    rows = d2 // N_DEV
    return _weight_grad_stage1(
        "weight_grad_out", (rows, d), 1, (yt, dhb),
        [pl.BlockSpec((rows, t_len), lambda s, o: (o[s], 0)), pl.BlockSpec(memory_space=pltpu.VMEM)],
        lambda refs, h: _dot(refs[0][...], refs[1][...]))


def _update_shard(own, others, w, m, v, name):
    n_rows, n_cols = w.shape
    rb = min(256, n_rows)
    n_other = len(others)

    def body(own_ref, *refs):
        other_refs = refs[:n_other]
        w_ref, m_ref, v_ref, grad_ref, delta_ref, mo_ref, vo_ref = refs[n_other:]
        g = own_ref[...]
        for ref in other_refs:
            for k in range(ref.shape[0] if len(ref.shape) == 3 else 1):
                g = g + (ref[k] if len(ref.shape) == 3 else ref[...]).astype(F32)
        delta, m_new, v_new = _adamw(w_ref[...], g, m_ref[...], v_ref[...])
        grad_ref[...] = g
        delta_ref[...] = delta
        mo_ref[...] = m_new
        vo_ref[...] = v_new

    blk = pl.BlockSpec((rb, n_cols), lambda i: (i, 0))
    stacked = lambda n: pl.BlockSpec((n, rb, n_cols), lambda i: (0, i, 0))
    out = jax.ShapeDtypeStruct((n_rows, n_cols), F32)
    return pl.pallas_call(
        body, name=name, grid=(n_rows // rb,),
        in_specs=[blk] + [stacked(o.shape[0]) if o.ndim == 3 else blk for o in others] + [blk, blk, blk],
        out_specs=(blk, blk, blk, blk), out_shape=(out, out, out, out),
        compiler_params=_params(dimension_semantics=("arbitrary",)),
    )(own, *others, w, m, v)


def _small_update(gat_v, gat_g, ln_tot, vec_w, vec_m, vec_v, gates, convs):
    n_vec = len(vec_w)
    n_heads, hd, _ = gates[0].shape
    tw = gat_g.shape[2]
    s8 = SUBLANES
    per = tw // hd
    n_tiles = n_heads // per
    cc = convs[0].shape[1]
    n_in = 3 + 3 * n_vec + 12

    def body(*refs):
        gv_ref, gg_ref, ln_ref = refs[:3]
        w_refs, m_refs, v_refs = (refs[3 + j * n_vec:3 + (j + 1) * n_vec] for j in range(3))
        gate_refs = refs[3 + 3 * n_vec:3 + 3 * n_vec + 6]
        conv_refs = refs[3 + 3 * n_vec + 6:n_in]
        loss_o = refs[n_in]
        kinds = [refs[n_in + 1 + j * (n_vec + 4):n_in + 1 + (j + 1) * (n_vec + 4)] for j in range(4)]
        tv, tg = refs[n_in + 1 + 4 * (n_vec + 4):]
        x, y, c = _mesh_pos()
        me = 4 * x + 2 * y + c

        def emit(k_out, w, g, m, v):
            delta, m_new, v_new = _adamw(w, g, m, v)
            for ref, val in zip(k_out, (g, delta, m_new, v_new)):
                ref[...] = val

        total = gv_ref[0]
        for dev in range(1, N_DEV):
            total = total + gv_ref[dev]
        tv[...] = total
        tv[SL_LN_G:SL_LN_G + 1, :] = ln_ref[0:1, :]

        def sum_gates(r, carry):
            rows = pl.ds(r, 2 * s8)
            part = gg_ref[0, rows, :].astype(F32)
            for dev in range(1, N_DEV):
                part = part + gg_ref[dev, rows, :].astype(F32)
            tg[rows, :] = part
            return carry

        _chunks(tg.shape[0], 2 * s8, sum_gates, 0)
        loss_o[...] = jnp.broadcast_to(tv[SL_LOSS:SL_LOSS + 1, 0:LANES], loss_o.shape)
        for p in range(n_vec):
            w, g = w_refs[p][...], tv[SL_LN_G + p, :]
            if SL_LN_G + p == SL_LAM:
                g = g * (RG_LRU_C * jax.nn.sigmoid(-w))
            emit([k_out[p] for k_out in kinds], w, g, m_refs[p][...], v_refs[p][...])
        lanes = pl.ds(pl.multiple_of(me * cc, cc), cc)
        for j, (row0, n) in enumerate(((SL_CONV_W, 3), (SL_LRU_W, 4))):
            w_ref, m_ref, v_ref = conv_refs[3 * j:3 * j + 3]
            emit([k_out[n_vec + 2 + j] for k_out in kinds], w_ref[...], tv[row0:row0 + n, lanes], m_ref[...], v_ref[...])
        for mtx in range(2):
            w_ref, m_ref, v_ref = gate_refs[3 * mtx:3 * mtx + 3]
            for k in range(n_tiles):
                tile = tg[(mtx * n_tiles + k) * hd:(mtx * n_tiles + k + 1) * hd, :]
                for a in range(per):
                    head = k * per + a
                    g = tile[:, a * hd:(a + 1) * hd]
                    delta, m_new, v_new = _adamw(w_ref[head], g, m_ref[head], v_ref[head])
                    for k_out, val in zip(kinds, (g, delta, m_new, v_new)):
                        k_out[n_vec + mtx][head] = val

    vm = pl.BlockSpec(memory_space=pltpu.VMEM)
    like = lambda a: jax.ShapeDtypeStruct(a.shape, F32)
    per_kind = tuple(like(a) for a in vec_w) + (like(gates[0]), like(gates[3]), like(convs[0]), like(convs[3]))
    n_out = 1 + 4 * len(per_kind)
    outs = pl.pallas_call(
        body, name="small_update",
        in_specs=[vm] * n_in, out_specs=(vm,) * n_out,
        out_shape=(jax.ShapeDtypeStruct((SUBLANES, LANES), F32),) + per_kind * 4,
        scratch_shapes=[pltpu.VMEM(gat_v.shape[1:], F32), pltpu.VMEM(gat_g.shape[1:], F32)],
        compiler_params=_params(),
    )(gat_v, gat_g, ln_tot, *vec_w, *vec_m, *vec_v, *gates, *convs)
    return outs[0], [outs[1 + j * len(per_kind):1 + (j + 1) * len(per_kind)] for j in range(4)]


def _head_ones(head_dim, tw):
    lane = jnp.arange(tw) // head_dim
    return (lane[:, None] == lane[None, :]).astype(MM)


def kernel(x, ln_g, w_in, conv_w, lru_conv_w, lru_conv_b, w_a, b_a, w_i, b_i, lam, conv_out_g, lru_out_g, w_out, final_g, loss_target, m_ln_g, m_w_in, m_conv_w, m_lru_conv_w, m_lru_conv_b, m_w_a, m_b_a, m_w_i, m_b_i, m_lam, m_conv_out_g, m_lru_out_g, m_w_out, m_final_g, v_ln_g, v_w_in, v_conv_w, v_lru_conv_w, v_lru_conv_b, v_w_a, v_b_a, v_w_i, v_b_i, v_lam, v_conv_out_g, v_lru_out_g, v_w_out, v_final_g):
    _, t_len, d = x.shape
    hd_l = d // N_LRU_HEADS
    tw = min(MXU_TILE, d)
    x2, tgt2 = x[0], loss_target[0]

    small = [ln_g, lru_conv_b, b_a, b_i, lam, conv_out_g, lru_out_g, final_g]
    p, xnt, win_all, wout_all, _, sp, wa_t, wi_t = _gather_project(
        x2, w_in, w_out, conv_w, lru_conv_w, ln_g.reshape(1, d), w_a, w_i, small, min(256, t_len), tw)
    wout_full = wout_all.reshape(N_DEV * w_out.shape[0], d)
    ones_c, ones_l = _head_ones(d // N_CONV_HEADS, tw), _head_ones(hd_l, tw)

    h, dh, dhb, facc, *saved = _forward(x2, tgt2, p, wout_full, wa_t, wi_t, sp, ones_c, ones_l, min(256, t_len))
    dp, yt, slab_v, slab_g = _backward(p, h, dh, saved, facc, wout_full, wa_t, wi_t, sp, ones_c, ones_l, min(256, t_len))
    part_out, own_out, _, _ = _weight_grad_out(yt, dhb)
    part_in, own_in, (chips_out,), (gat_v, gat_g) = _weight_grad_in(xnt, dp, (part_out,), (slab_v, slab_g))
    grad_x, ln_tot, sums_in = _input_grad(dp, win_all, x2, dh, sp, part_in, min(512, t_len))
    gw_in, dw_in, mw_in, vw_in = _update_shard(own_in, sums_in, w_in, m_w_in, v_w_in, "update_w_in")
    gw_out, dw_out, mw_out, vw_out = _update_shard(own_out, (chips_out,), w_out, m_w_out, v_w_out, "update_w_out")

    loss_tile, kinds = _small_update(
        gat_v, gat_g, ln_tot, small,
        [m_ln_g, m_lru_conv_b, m_b_a, m_b_i, m_lam, m_conv_out_g, m_lru_out_g, m_final_g],
        [v_ln_g, v_lru_conv_b, v_b_a, v_b_i, v_lam, v_conv_out_g, v_lru_out_g, v_final_g],
        (w_a, m_w_a, v_w_a, w_i, m_w_i, v_w_i), (conv_w, m_conv_w, v_conv_w, lru_conv_w, m_lru_conv_w, v_lru_conv_w))

    def unpack(kind, big_in, big_out):
        vec, (wa_, wi_, cw_, lw_) = kind[:len(small)], kind[len(small):]
        return [vec[0], big_in, cw_, lw_, vec[1], wa_, vec[2], wi_, vec[3], vec[4], vec[5], vec[6], big_out, vec[7]]

    return (loss_tile[0, 0], grad_x[None], *unpack(kinds[0], gw_in, gw_out), *unpack(kinds[1], dw_in, dw_out),
            *unpack(kinds[2], mw_in, mw_out), *unpack(kinds[3], vw_in, vw_out))
```

```python
import jax
import jax.numpy as jnp
from jax import lax
from jax.experimental import pallas as pl
from jax.experimental.pallas import tpu as pltpu

F32 = jnp.float32
MM = jnp.bfloat16
MESH = pl.DeviceIdType.MESH

N_DEV = 8
N_CONV_HEADS = 8
N_LRU_HEADS = 16
RG_LRU_C = 8.0
RMS_EPS = 1e-6
ADAM_LR, ADAM_B1, ADAM_B2, ADAM_EPS, ADAM_WD, ADAM_STEP = 0.001, 0.9, 0.999, 1e-08, 0.01, 10
ADAM_BC1 = 1.0 - ADAM_B1 ** ADAM_STEP
ADAM_BC2 = 1.0 - ADAM_B2 ** ADAM_STEP

SUBLANES = 8
LANES = 128
MXU_TILE = 256
VMEM_LIMIT = 56 * 1024 * 1024

SP_LN_G, SP_LRU_B, SP_B_A, SP_B_I, SP_LAM, SP_CONV_G, SP_LRU_G, SP_FINAL_G, SP_CONV_W, SP_LRU_W = 0, 1, 2, 3, 4, 5, 6, 7, 8, 11
SP_ROWS = 16
P_B, P_C, P_XC, P_GC, P_XL, P_GL = 0, 1, 2, 3, 4, 5
A_CONV_G, A_LRU_G, A_LAM, A_B_A, A_B_I, A_CONV_W, A_LRU_W, A_LRU_B = 0, 1, 2, 3, 4, 5, 8, 12
A_GROUPS = 13
SL_LOSS, SL_LN_G, SL_LRU_B, SL_B_A, SL_B_I, SL_LAM, SL_CONV_G, SL_LRU_G, SL_FINAL_G, SL_CONV_W, SL_LRU_W = 0, 1, 2, 3, 4, 5, 6, 7, 8, 16, 24
SL_ROWS = 32


def _params(vmem=True, **kw):
    if vmem:
        kw["vmem_limit_bytes"] = VMEM_LIMIT
    return pltpu.CompilerParams(**kw)


def _dot(a, b):
    return jnp.dot(a, b, preferred_element_type=F32)


def _dot_nt(a, b):
    return lax.dot_general(a, b, (((1,), (1,)), ((), ())), preferred_element_type=F32)


def _head_sums(v, ones_tile):
    tw = ones_tile.shape[0]
    vb = v.astype(MM)
    return jnp.concatenate([_dot(vb[:, k:k + tw], ones_tile) for k in range(0, v.shape[1], tw)], axis=1)


def _head_rstd(v, ones_tile, head_dim):
    return lax.rsqrt(_head_sums(v * v, ones_tile) * (1.0 / head_dim) + RMS_EPS)


def _sigmoid(x):
    return 0.5 * jnp.tanh(0.5 * x) + 0.5


def _lru_input_scale_sq(log_a, a):
    return -jnp.tanh(log_a) * (1.0 + a * a)


def _log_sigmoid(x):
    z = jnp.exp(-jnp.abs(x))
    u = 1.0 + z
    log1p_z = jnp.where(u == 1.0, z, jnp.log(u) * (z / (u - 1.0)))
    return jnp.minimum(x, 0.0) - log1p_z


def _row_iota(d):
    return lax.broadcasted_iota(jnp.int32, (SUBLANES, d), 0)


def _shift_down(cur, prev, s, row):
    return jnp.where(row >= s, pltpu.roll(cur, s, axis=0), pltpu.roll(prev, s, axis=0))


def _shift_up(cur, nxt, s, row):
    k = SUBLANES - s
    return jnp.where(row < k, pltpu.roll(cur, k, axis=0), pltpu.roll(nxt, k, axis=0))


def _scan_fwd(a, b, h_prev, row):
    for s in (1, 2, 4):
        a_s = jnp.where(row >= s, pltpu.roll(a, s, axis=0), 1.0)
        b_s = jnp.where(row >= s, pltpu.roll(b, s, axis=0), 0.0)
        b = a * b_s + b
        a = a * a_s
    return a * h_prev + b


def _scan_bwd(a_next, b, g_next, row):
    a = a_next
    for s in (1, 2, 4):
        k = SUBLANES - s
        a_s = jnp.where(row < k, pltpu.roll(a, k, axis=0), 1.0)
        b_s = jnp.where(row < k, pltpu.roll(b, k, axis=0), 0.0)
        b = a * b_s + b
        a = a * a_s
    return a * g_next + b


def _bcast_row(v, r):
    return jnp.broadcast_to(v[r:r + 1, :], v.shape)


def _chunks(n_rows, rc, body, init, reverse=False):
    n = n_rows // rc

    def step(i, carry):
        j = (n - 1 - i) if reverse else i
        return body(pl.multiple_of(j * rc, rc), carry)

    return lax.fori_loop(0, n, step, init)


def _adamw(w, g, m, v):
    m = ADAM_B1 * m + (1.0 - ADAM_B1) * g
    v = ADAM_B2 * v + (1.0 - ADAM_B2) * (g * g)
    m_hat = m / ADAM_BC1
    v_hat = v / ADAM_BC2
    delta = -ADAM_LR * (m_hat / (jnp.sqrt(v_hat) + ADAM_EPS) + ADAM_WD * w)
    return delta, m, v


def _mesh_pos():
    return lax.axis_index("x"), lax.axis_index("y"), lax.axis_index("c")


class _Gather:
    def __init__(self, blocks_of, send_sems, recv_sems, own_src=None):
        x, y, c = _mesh_pos()
        self.c = c
        self.me, self.sibling = (x, y, c), (x, y, 1 - c)
        self.chips = [(1 - x, y), (x, 1 - y), (1 - x, 1 - y)]
        self.blocks_of, self.send_sems, self.recv_sems = blocks_of, send_sems, recv_sems
        self.own_src = own_src

    def copy(self, a, k, block, to):
        src = self.blocks_of(a, *block)
        if block is self.me and self.own_src is not None:
            src = self.own_src[a]
        return pltpu.make_async_remote_copy(
            src_ref=src, dst_ref=self.blocks_of(a, *block),
            send_sem=self.send_sems.at[a * 7 + k], recv_sem=self.recv_sems.at[a * 7 + k],
            device_id=to, device_id_type=MESH)

    def start_own(self, a):
        self.copy(a, 0, self.me, self.sibling).start()
        for j, chip in enumerate(self.chips):
            self.copy(a, 1 + j, self.me, (*chip, self.c)).start()

    def wait_sibling(self, a):
        self.copy(a, 0, self.sibling, self.me).wait_recv()

    def wait_chip_and_pass_on(self, a, j):
        block = (*self.chips[j], self.c)
        self.copy(a, 1 + j, block, self.me).wait_recv()
        self.copy(a, 4 + j, block, self.sibling).start()

    def wait_passed_on(self, a, j):
        self.copy(a, 4 + j, (*self.chips[j], 1 - self.c), self.me).wait_recv()

    def wait_sends(self, a):
        self.copy(a, 0, self.me, self.sibling).wait_send()
        for j, chip in enumerate(self.chips):
            self.copy(a, 1 + j, self.me, (*chip, self.c)).wait_send()
            self.copy(a, 4 + j, (*chip, self.c), self.sibling).wait_send()

    def finish(self, a):
        for j in range(3):
            self.wait_chip_and_pass_on(a, j)
        self.wait_sibling(a)
        for j in range(3):
            self.wait_passed_on(a, j)
        self.wait_sends(a)


class _BalancedGather:
    def __init__(self, slot, send_sems, recv_sems, own_src):
        x, y, c = _mesh_pos()
        self.c = c
        self.me, self.sibling = (x, y, c), (x, y, 1 - c)
        self.chips = [(1 - x, y), (x, 1 - y), (1 - x, 1 - y)]
        self.slot, self.send_sems, self.recv_sems, self.own_src = slot, send_sems, recv_sems, own_src

    def half(self, a, block, which):
        ref = self.slot(a, *block)
        n = ref.shape[0] // 2
        return ref.at[pl.ds(which * n, n)]

    def copy(self, a, k, src, dst, to):
        return pltpu.make_async_remote_copy(
            src_ref=src, dst_ref=dst, send_sem=self.send_sems.at[a * 8 + k], recv_sem=self.recv_sems.at[a * 8 + k],
            device_id=to, device_id_type=MESH)

    def whole(self, a, k, block, to):
        src = self.own_src[a] if block is self.me else self.slot(a, *block)
        return self.copy(a, k, src, self.slot(a, *block), to)

    def halved(self, a, k, block, which, to):
        return self.copy(a, k, self.half(a, block, which), self.half(a, block, which), to)

    def on(self, chip):
        return (*self.chips[chip], self.c)

    def start_own(self, a):
        self.whole(a, 0, self.me, self.sibling).start()
        self.whole(a, 1, self.me, self.on(0)).start()
        self.whole(a, 2, self.me, self.on(1)).start()

    def start_own_staggered(self, a):
        self.whole(a, 0, self.me, self.sibling).start()
        for core, order in ((1, (0, 1)), (0, (1, 0))):
            @pl.when(self.c == core)
            def _(order=order):
                for j in order:
                    self.whole(a, 1 + j, self.me, self.on(j)).start()

    def wait_sibling(self, a):
        self.whole(a, 0, self.sibling, self.me).wait_recv()

    def on_neighbour(self, a, j):
        self.whole(a, 1 + j, self.on(j), self.me).wait_recv()
        self.halved(a, 3 + j, self.on(j), j, self.on(1 - j)).start()
        self.whole(a, 5 + j, self.on(j), self.sibling).start()

    def on_diagonal(self, a):
        self.halved(a, 3, self.on(2), 0, self.me).wait_recv()
        self.halved(a, 4, self.on(2), 1, self.me).wait_recv()
        self.whole(a, 7, self.on(2), self.sibling).start()

    def wait_passed_on(self, a, j):
        self.whole(a, 5 + j, (*self.chips[j], 1 - self.c), self.me).wait_recv()

    def wait_sends(self, a):
        self.whole(a, 0, self.me, self.sibling).wait_send()
        for j in range(2):
            self.whole(a, 1 + j, self.me, self.on(j)).wait_send()
            self.halved(a, 3 + j, self.on(j), j, self.on(1 - j)).wait_send()
        for j in range(3):
            self.whole(a, 5 + j, self.on(j), self.sibling).wait_send()


def _block_order():
    x, y, c = _mesh_pos()
    idx = lambda chip, core: 4 * chip[0] + 2 * chip[1] + core
    own, first, second, diag = (x, y), (x ^ c, y ^ (1 - c)), (x ^ (1 - c), y ^ c), (1 - x, 1 - y)
    order = [idx(own, c), idx(own, 1 - c), idx(first, c), idx(second, 1 - c), idx(second, c), idx(first, 1 - c),
             idx(diag, c), idx(diag, 1 - c)]
    return jnp.stack(order).astype(jnp.int32)


def _gather_project(x, w_in, w_out, conv_w, lru_conv_w, ln_g, w_a, w_i, vecs, tb, tw):
    t_len, d = x.shape
    nb = t_len // tb
    cols = w_in.shape[1]
    mc = min(512, t_len)
    conv_pack = jax.ShapeDtypeStruct((SUBLANES, conv_w.shape[1]), F32)
    srcs = (w_in, w_out, conv_pack)
    dts = (MM, MM, F32)
    n_vec = len(vecs)
    n_heads, hd, _ = w_a.shape
    per = tw // hd

    def body(order_ref, x_ref, win_ref, wout_ref, cw_ref, lw_ref, lng_ref, wa_ref, wi_ref, *refs):
        vec_refs = refs[:n_vec]
        (p_ref, xnt_ref, win_all, wout_all, cp_all, sp_ref, wat_ref, wit_ref,
         xnb, wall, st_out, st_cp, cp_vm, send_sems, recv_sems, cp_send, cp_recv, local_sems) = refs[n_vec:]
        i = pl.program_id(0)
        x_, y_, c_ = _mesh_pos()
        me = 4 * x_ + 2 * y_ + c_
        outs = (win_all, wout_all, cp_all)
        lands = (wall, wout_all, cp_all)
        stages = (wall.at[me], st_out, st_cp)
        gather = _BalancedGather(lambda a, px, py, pc: lands[a].at[4 * px + 2 * py + pc], send_sems, recv_sems, stages)
        small = _Gather(lambda a, px, py, pc: cp_all.at[4 * px + 2 * py + pc], cp_send, cp_recv, own_src=[st_cp])
        keep_own = [pltpu.make_async_copy(stages[a], outs[a].at[me], local_sems.at[a]) for a in range(3)]

        def keep(k):
            blk = order_ref[k]
            return pltpu.make_async_copy(wall.at[blk], win_all.at[blk], local_sems.at[2 + k])

        @pl.when(i == 0)
        def _():
            for a, src in enumerate((win_ref, wout_ref)):
                dst, rc = stages[a], 32

                def cast(r, carry, src=src, dst=dst):
                    dst[pl.ds(r, rc), :] = src[pl.ds(r, rc), :].astype(dst.dtype)
                    return carry

                _chunks(src.shape[0], rc, cast, 0)
                keep_own[a].start()
            gather.start_own_staggered(0)
            n_cw, n_lw = cw_ref.shape[0], lw_ref.shape[0]
            st_cp[...] = jnp.zeros_like(st_cp)
            st_cp[0:n_cw, :] = cw_ref[...]
            st_cp[n_cw:n_cw + n_lw, :] = lw_ref[...]
            keep_own[2].start()

        @pl.when(i < nb)
        def _():
            xv = x_ref[...]
            r0 = lax.rsqrt(jnp.mean(xv * xv, axis=-1, keepdims=True) + RMS_EPS)
            xn = xv * r0 * lng_ref[...]
            xnb[pl.ds(pl.multiple_of(i * tb, tb), tb), :] = xn.astype(MM)
            xnt_ref[...] = xn.T.astype(MM)

        def by_core(action):
            for core, (first, second) in ((1, (0, 1)), (0, (1, 0))):
                @pl.when(c_ == core)
                def _(first=first, second=second):
                    action(first, second)

        for k in range(N_DEV):
            @pl.when(i == nb + k)
            def _(k=k):
                if k == 1:
                    gather.wait_sibling(0)
                elif k == 2:
                    by_core(lambda first, second: gather.on_neighbour(0, first))
                    gather.start_own(1)
                    small.start_own(0)
                elif k == 3:
                    by_core(lambda first, second: gather.wait_passed_on(0, second))
                elif k == 4:
                    by_core(lambda first, second: gather.on_neighbour(0, second))
                elif k == 5:
                    by_core(lambda first, second: gather.wait_passed_on(0, first))
                    gather.on_neighbour(1, 0)
                    gather.on_neighbour(1, 1)
                elif k == 6:
                    gather.on_diagonal(0)
                elif k == 7:
                    gather.wait_passed_on(0, 2)
                    gather.on_diagonal(1)
                blk = order_ref[k]
                if k:
                    keep(k).start()

                def project(r, carry):
                    rows = pl.ds(r, mc)
                    p_ref[rows, :] = _dot(xnb[rows, :], wall[blk]).astype(MM)
                    return carry

                _chunks(t_len, mc, project, 0)
                if k == N_DEV - 1:
                    gather.wait_sends(0)
                    gather.wait_sibling(1)
                    for j in range(3):
                        gather.wait_passed_on(1, j)
                    gather.wait_sends(1)
                    small.finish(0)
                    for cp in keep_own + [keep(kk) for kk in range(1, N_DEV)]:
                        cp.wait()
                    load = pltpu.make_async_copy(cp_all, cp_vm, local_sems.at[N_DEV + 2])
                    load.start()
                    load.wait()
                    for r, ref in enumerate(vec_refs):
                        sp_ref[r, :] = ref[...]
                    sp_ref[n_vec:n_vec + SUBLANES, :] = jnp.concatenate([cp_vm[dev] for dev in range(N_DEV)], axis=1)
                    for src, dst in ((wa_ref, wat_ref), (wi_ref, wit_ref)):
                        dst[...] = jnp.zeros_like(dst)
                        for head in range(n_heads):
                            lo = (head % per) * hd
                            dst[head // per, lo:lo + hd, lo:lo + hd] = src[head].astype(MM)

    vm = pl.BlockSpec(memory_space=pltpu.VMEM)
    hbm = pl.BlockSpec(memory_space=pl.ANY)
    grid_spec = pltpu.PrefetchScalarGridSpec(
        num_scalar_prefetch=1, grid=(nb + N_DEV,),
        in_specs=[pl.BlockSpec((tb, d), lambda i, o: (jnp.minimum(i, nb - 1), 0))] + [vm] * (7 + n_vec),
        out_specs=(pl.BlockSpec((t_len, cols), lambda i, o: (0, o[jnp.maximum(i - nb, 0)])),
                   pl.BlockSpec((d, tb), lambda i, o: (0, jnp.minimum(i, nb - 1))), hbm, hbm, hbm,
                   pl.BlockSpec((SP_ROWS, d), lambda i, o: (0, 0)),
                   pl.BlockSpec((n_heads // per, tw, tw), lambda i, o: (0, 0, 0)),
                   pl.BlockSpec((n_heads // per, tw, tw), lambda i, o: (0, 0, 0))),
        scratch_shapes=[pltpu.VMEM((t_len, d), MM), pltpu.VMEM((N_DEV,) + w_in.shape, MM),
                        pltpu.VMEM(w_out.shape, MM), pltpu.VMEM(conv_pack.shape, F32),
                        pltpu.VMEM((N_DEV,) + conv_pack.shape, F32),
                        pltpu.SemaphoreType.DMA((16,)), pltpu.SemaphoreType.DMA((16,)),
                        pltpu.SemaphoreType.DMA((7,)), pltpu.SemaphoreType.DMA((7,)), pltpu.SemaphoreType.DMA((N_DEV + 3,))])
    return pl.pallas_call(
        body, name="gather_project", grid_spec=grid_spec,
        out_shape=(jax.ShapeDtypeStruct((t_len, N_DEV * cols), MM),
                   jax.ShapeDtypeStruct((d, t_len), MM))
                  + tuple(jax.ShapeDtypeStruct((N_DEV,) + s.shape, dt) for s, dt in zip(srcs, dts))
                  + (jax.ShapeDtypeStruct((SP_ROWS, d), F32),)
                  + (jax.ShapeDtypeStruct((n_heads // per, tw, tw), MM),) * 2,
        compiler_params=_params(dimension_semantics=("arbitrary",)),
    )(_block_order(), x, w_in, w_out, conv_w, lru_conv_w, ln_g, w_a, w_i, *vecs)


def _forward(x, tgt, p, wout, wa_t, wi_t, sp, ones_c, ones_l, tb):
    t_len, d = x.shape
    nb = t_len // tb
    n_tiles, tw = wa_t.shape[0], wa_t.shape[1]
    hd_c, hd_l = d // N_CONV_HEADS, d // N_LRU_HEADS
    s8 = SUBLANES

    def body(x_ref, tgt_ref, p_ref, wout_ref, wa_ref, wi_ref, sp_ref, oc_ref, ol_ref,
             h_ref, dh_ref, dhb_ref, acc_ref, yc, czs, u, pa, pi,
             rcf, rlf, ybuf, tail_z, tail_xl, hcar):
        i = pl.program_id(0)
        row = _row_iota(d)

        @pl.when(i == 0)
        def _():
            tail_z[...] = jnp.zeros_like(tail_z)
            tail_xl[...] = jnp.zeros_like(tail_xl)
            hcar[...] = jnp.zeros_like(hcar)
            acc_ref[...] = jnp.zeros_like(acc_ref)

        def spr(r):
            return sp_ref[r:r + 1, :]

        def proj(rows, seg):
            return p_ref[rows, seg * d:(seg + 1) * d].astype(F32)

        w0, w1, w2 = spr(SP_CONV_W), spr(SP_CONV_W + 1), spr(SP_CONV_W + 2)
        l0, l1, l2, l3 = spr(SP_LRU_W), spr(SP_LRU_W + 1), spr(SP_LRU_W + 2), spr(SP_LRU_W + 3)
        lb = spr(SP_LRU_B)

        def convs(r, carry):
            zp, xp = carry
            rows16 = pl.ds(r, 2 * s8)
            bg16, xl16 = proj(rows16, P_B), proj(rows16, P_XL)
            z16 = proj(rows16, P_C) * proj(rows16, P_XC)
            for j in range(2):
                rows, sub = pl.ds(r + j * s8, s8), slice(j * s8, (j + 1) * s8)
                z, xl = z16[sub], xl16[sub]
                cz = w0 * _shift_down(z, zp, 2, row) + w1 * _shift_down(z, zp, 1, row) + w2 * z
                czs[rows, :] = cz
                yc[rows, :] = bg16[sub] * cz
                u[rows, :] = (l0 * _shift_down(xl, xp, 3, row) + l1 * _shift_down(xl, xp, 2, row)
                              + l2 * _shift_down(xl, xp, 1, row) + l3 * xl + lb)
                zp, xp = z, xl
            return zp, xp

        z_last, xl_last = _chunks(tb, 2 * s8, convs, (tail_z[...], tail_xl[...]))
        tail_z[...] = z_last
        tail_xl[...] = xl_last

        ub = u[...].astype(MM)
        for k in range(n_tiles):
            sl = slice(k * tw, (k + 1) * tw)
            pa[:, sl] = _dot(ub[:, sl], wa_ref[k])
            pi[:, sl] = _dot(ub[:, sl], wi_ref[k])
        rcf[...] = _head_rstd(yc[...], oc_ref[...], hd_c)

        c8 = RG_LRU_C * _log_sigmoid(spr(SP_LAM))
        b_a, b_i = spr(SP_B_A), spr(SP_B_I)

        def lru(r, hp):
            rows = pl.ds(r, SUBLANES)
            ra = _sigmoid(pa[rows, :] + b_a)
            ii = _sigmoid(pi[rows, :] + b_i)
            pa[rows, :] = ra
            pi[rows, :] = ii
            la = ra * c8
            a = jnp.exp(la)
            mult = jnp.sqrt(_lru_input_scale_sq(la, a))
            h = _scan_fwd(a, mult * (ii * u[rows, :]), hp, row)
            h_ref[rows, :] = h
            return _bcast_row(h, SUBLANES - 1)

        hcar[...] = _chunks(tb, SUBLANES, lru, hcar[...])
        rlf[...] = _head_rstd(h_ref[...], ol_ref[...], hd_l)

        g_c, g_l = spr(SP_CONV_G), spr(SP_LRU_G)

        def gate(r, carry):
            rows = pl.ds(r, 2 * s8)
            gc, gl = proj(rows, P_GC), proj(rows, P_GL)
            ybuf[rows, 0:d] = (yc[rows, :] * rcf[rows, :] * g_c * (gc * _sigmoid(gc))).astype(MM)
            ybuf[rows, d:2 * d] = (h_ref[rows, :] * rlf[rows, :] * g_l * (gl * _sigmoid(gl))).astype(MM)
            return carry

        _chunks(tb, 2 * s8, gate, 0)

        hres = x_ref[...] + _dot(ybuf[...], wout_ref[...])
        rf = lax.rsqrt(jnp.mean(hres * hres, axis=-1, keepdims=True) + RMS_EPS)
        hn = hres * rf
        fg = spr(SP_FINAL_G)
        err = hn * fg - tgt_ref[...]
        dout = err * (1.0 / d)
        acc_ref[0:SUBLANES, :] += (err * err).reshape(tb // SUBLANES, SUBLANES, d).sum(axis=0)
        acc_ref[SUBLANES:2 * SUBLANES, :] += (dout * hn).reshape(tb // SUBLANES, SUBLANES, d).sum(axis=0)
        gd = dout * fg
        dhres = rf * (gd - hn * jnp.mean(gd * hn, axis=-1, keepdims=True))
        dh_ref[...] = dhres
        dhb_ref[...] = dhres.astype(MM)

    vm = pl.BlockSpec(memory_space=pltpu.VMEM)
    blk = lambda w: pl.BlockSpec((tb, w), lambda i: (i, 0))
    buf = pltpu.VMEM((tb, d), F32)
    car = pltpu.VMEM((SUBLANES, d), F32)
    return pl.pallas_call(
        body, name="forward", grid=(nb,),
        in_specs=[blk(d), blk(d), blk(6 * d), vm, vm, vm, vm, vm, vm],
        out_specs=(blk(d), blk(d), blk(d), pl.BlockSpec((2 * SUBLANES, d), lambda i: (0, 0))) + (blk(d),) * 5,
        out_shape=(jax.ShapeDtypeStruct((t_len, d), F32),
                   jax.ShapeDtypeStruct((t_len, d), F32),
                   jax.ShapeDtypeStruct((t_len, d), MM),
                   jax.ShapeDtypeStruct((2 * SUBLANES, d), F32))
                  + (jax.ShapeDtypeStruct((t_len, d), F32),) * 5,
        scratch_shapes=[buf] * 2 + [pltpu.VMEM((tb, 2 * d), MM), car, car, car],
        compiler_params=_params(dimension_semantics=("arbitrary",)),
    )(x, tgt, p, wout, wa_t, wi_t, sp, ones_c, ones_l)


def _backward(p, h, dh, saved, facc, wout, wa_t, wi_t, sp, ones_c, ones_l, tb):
    t_len, d = h.shape
    nb = t_len // tb
    n_tiles, tw = wa_t.shape[0], wa_t.shape[1]
    hd_c, hd_l = d // N_CONV_HEADS, d // N_LRU_HEADS
    g_rows = 2 * n_tiles * hd_l
    s8 = SUBLANES

    def body(p_ref, h_ref, hhalo_ref, dh_ref, yc, czs, u, ra_ref, ii_ref, facc_ref,
             wout_ref, wa_ref, wi_ref, sp_ref, oc_ref, ol_ref,
             dp_ref, yt_ref, slab_v, slab_g,
             hh, dy, ybuf, rcf, rlf, qc, ql, dyc_hat, dyl_hat, dpa, dpi, du, gwa_ref, gwi_ref, acc_ref,
             car_dcz, car_a, car_g, car_du):
        i = pl.program_id(0)
        blk_idx = nb - 1 - i
        row = _row_iota(d)

        @pl.when(i == 0)
        def _():
            for ref in (car_dcz, car_a, car_g, car_du, gwa_ref, gwi_ref, acc_ref):
                ref[...] = jnp.zeros_like(ref)

        def spr(r):
            return sp_ref[r:r + 1, :]

        def proj(rows, seg):
            return p_ref[rows, seg * d:(seg + 1) * d].astype(F32)

        def put(rows, seg, halves):
            dp_ref[rows, seg * d:(seg + 1) * d] = jnp.concatenate(halves, axis=0).astype(MM)

        def acc_add(group, val):
            acc_ref[group * s8:(group + 1) * s8, :] += val

        live = jnp.where(blk_idx > 0, 1.0, 0.0).astype(F32)
        hh[0:s8, :] = hhalo_ref[...] * live
        hh[s8:, :] = h_ref[...]

        dy[...] = _dot_nt(dh_ref[...].astype(MM), wout_ref[...])

        w0, w1, w2 = spr(SP_CONV_W), spr(SP_CONV_W + 1), spr(SP_CONV_W + 2)
        l0, l1, l2, l3 = spr(SP_LRU_W), spr(SP_LRU_W + 1), spr(SP_LRU_W + 2), spr(SP_LRU_W + 3)

        rcf[...] = _head_rstd(yc[...], oc_ref[...], hd_c)
        rlf[...] = _head_rstd(h_ref[...], ol_ref[...], hd_l)

        g_c, g_l = spr(SP_CONV_G), spr(SP_LRU_G)

        def gates(r, carry):
            rows = pl.ds(r, 2 * s8)
            for (seg, off_y, src, rstd, gain, q, dhat, grp) in (
                    (P_GC, 0, yc, rcf, g_c, qc, dyc_hat, A_CONV_G),
                    (P_GL, d, h_ref, rlf, g_l, ql, dyl_hat, A_LRU_G)):
                gt = proj(rows, seg)
                sg = _sigmoid(gt)
                silu = gt * sg
                yhat = src[rows, :] * rstd[rows, :]
                nrm = yhat * gain
                ybuf[rows, off_y:off_y + d] = nrm * silu
                dout = dy[rows, off_y:off_y + d]
                dnrm = dout * silu
                dp_ref[rows, seg * d:(seg + 1) * d] = (dout * nrm * (sg * (1.0 + gt * (1.0 - sg)))).astype(MM)
                dg = dnrm * yhat
                acc_add(grp, dg[0:s8] + dg[s8:])
                dh_ = dnrm * gain
                dhat[rows, :] = dh_
                q[rows, :] = dh_ * yhat
            return carry

        _chunks(tb, 2 * s8, gates, 0)

        qc[...] = _head_sums(qc[...], oc_ref[...]) * (1.0 / hd_c)
        ql[...] = _head_sums(ql[...], ol_ref[...]) * (1.0 / hd_l)
        yt_ref[...] = ybuf[...].T.astype(MM)

        c8 = RG_LRU_C * _log_sigmoid(spr(SP_LAM))

        def conv_mixer(r, dcz_n):
            rows16 = pl.ds(r, 2 * s8)
            bg16, cg16, xc16 = proj(rows16, P_B), proj(rows16, P_C), proj(rows16, P_XC)
            z16 = cg16 * xc16
            d_b, d_c, d_x = [None, None], [None, None], [None, None]
            for j in (1, 0):
                rows, sub = pl.ds(r + j * s8, s8), slice(j * s8, (j + 1) * s8)
                rstd = rcf[rows, :]
                yhat = yc[rows, :] * rstd
                dyc = rstd * (dyc_hat[rows, :] - yhat * qc[rows, :])
                d_b[j] = dyc * czs[rows, :]
                dcz = dyc * bg16[sub]
                up1, up2 = _shift_up(dcz, dcz_n, 1, row), _shift_up(dcz, dcz_n, 2, row)
                dz = w2 * dcz + w1 * up1 + w0 * up2
                d_c[j] = dz * xc16[sub]
                d_x[j] = dz * cg16[sub]
                z = z16[sub]
                acc_add(A_CONV_W, up2 * z)
                acc_add(A_CONV_W + 1, up1 * z)
                acc_add(A_CONV_W + 2, dcz * z)
                dcz_n = dcz
            put(rows16, P_B, d_b)
            put(rows16, P_C, d_c)
            put(rows16, P_XC, d_x)
            return dcz_n

        car_dcz[...] = _chunks(tb, 2 * s8, conv_mixer, car_dcz[...], reverse=True)

        def lru_mixer(r, carry):
            a_n, g_n = carry
            for j in (1, 0):
                rows = pl.ds(r + j * s8, s8)
                rstd = rlf[rows, :]
                hcur = hh[pl.ds(r + (j + 1) * s8, s8), :]
                hhat = hcur * rstd
                dh_out = rstd * (dyl_hat[rows, :] - hhat * ql[rows, :])
                ra = ra_ref[rows, :]
                la = ra * c8
                a = jnp.exp(la)
                g = _scan_bwd(_shift_up(a, a_n, 1, row), dh_out, g_n, row)
                da = g * _shift_down(hcur, hh[pl.ds(r + j * s8, s8), :], 1, row)
                ii = ii_ref[rows, :]
                uu = u[rows, :]
                mult_sq = _lru_input_scale_sq(la, a)
                inv_mult = lax.rsqrt(mult_sq)
                dmult = g * (ii * uu)
                ds = g * (mult_sq * inv_mult)
                dla = a * (da - dmult * a * inv_mult)
                acc_add(A_LAM, dla * ra)
                dpa_ = dla * c8 * ra * (1.0 - ra)
                dpi_ = ds * uu * ii * (1.0 - ii)
                acc_add(A_B_A, dpa_)
                acc_add(A_B_I, dpi_)
                dpa[rows, :] = dpa_
                dpi[rows, :] = dpi_
                du[rows, :] = ds * ii
                a_n, g_n = a, _bcast_row(g, 0)
            return a_n, g_n

        a_f, g_f = _chunks(tb, 2 * s8, lru_mixer, (car_a[...], car_g[...]), reverse=True)
        car_a[...] = a_f
        car_g[...] = g_f

        dpab = dpa[...].astype(MM)
        dpib = dpi[...].astype(MM)
        for k in range(n_tiles):
            sl = slice(k * tw, (k + 1) * tw)
            du[:, sl] += _dot_nt(dpab[:, sl], wa_ref[k]) + _dot_nt(dpib[:, sl], wi_ref[k])
            ut = u[:, sl].T.astype(MM)
            gwa_ref[k] += _dot(ut, dpab[:, sl])
            gwi_ref[k] += _dot(ut, dpib[:, sl])

        def lru_conv(r, du_n):
            rows16 = pl.ds(r, 2 * s8)
            xl16 = proj(rows16, P_XL)
            d_xl = [None, None]
            for j in (1, 0):
                rows, sub = pl.ds(r + j * s8, s8), slice(j * s8, (j + 1) * s8)
                dut = du[rows, :]
                up1, up2, up3 = (_shift_up(dut, du_n, s, row) for s in (1, 2, 3))
                d_xl[j] = l3 * dut + l2 * up1 + l1 * up2 + l0 * up3
                xl = xl16[sub]
                acc_add(A_LRU_W, up3 * xl)
                acc_add(A_LRU_W + 1, up2 * xl)
                acc_add(A_LRU_W + 2, up1 * xl)
                acc_add(A_LRU_W + 3, dut * xl)
                acc_add(A_LRU_B, dut)
                du_n = dut
            put(rows16, P_XL, d_xl)
            return du_n

        car_du[...] = _chunks(tb, 2 * s8, lru_conv, car_du[...], reverse=True)

        @pl.when(i == nb - 1)
        def _():
            def rowsum(ref, group):
                return jnp.sum(ref[group * s8:(group + 1) * s8, :], axis=0, keepdims=True)

            slab_v[...] = jnp.zeros_like(slab_v)
            loss = jnp.sum(rowsum(facc_ref, 0), axis=1, keepdims=True) * (0.5 / d)
            rows = {SL_LOSS: jnp.broadcast_to(loss, (1, d)), SL_FINAL_G: rowsum(facc_ref, 1),
                    SL_LRU_B: rowsum(acc_ref, A_LRU_B), SL_B_A: rowsum(acc_ref, A_B_A), SL_B_I: rowsum(acc_ref, A_B_I),
                    SL_LAM: rowsum(acc_ref, A_LAM), SL_CONV_G: rowsum(acc_ref, A_CONV_G), SL_LRU_G: rowsum(acc_ref, A_LRU_G)}
            for k in range(3):
                rows[SL_CONV_W + k] = rowsum(acc_ref, A_CONV_W + k)
            for k in range(4):
                rows[SL_LRU_W + k] = rowsum(acc_ref, A_LRU_W + k)
            for r, val in rows.items():
                slab_v[r:r + 1, :] = val
            head_of_lane = lax.broadcasted_iota(jnp.int32, (hd_l, tw), 1) // hd_l
            for mtx, g_ref in enumerate((gwa_ref, gwi_ref)):
                for k in range(n_tiles):
                    packed = jnp.zeros((hd_l, tw), F32)
                    for a in range(tw // hd_l):
                        packed = jnp.where(head_of_lane == a, g_ref[k, a * hd_l:(a + 1) * hd_l, :], packed)
                    slab_g[(mtx * n_tiles + k) * hd_l:(mtx * n_tiles + k + 1) * hd_l, :] = packed.astype(MM)

    vm = pl.BlockSpec(memory_space=pltpu.VMEM)
    rev = lambda w: pl.BlockSpec((tb, w), lambda i: (nb - 1 - i, 0))
    halo = lambda rows, w: pl.BlockSpec((rows, w), lambda i: (jnp.maximum((nb - 1 - i) * (tb // rows) - 1, 0), 0))
    const = lambda shape: pl.BlockSpec(shape, lambda i: (0,) * len(shape))
    buf = lambda w: pltpu.VMEM((tb, w), F32)
    car = pltpu.VMEM((SUBLANES, d), F32)
    return pl.pallas_call(
        body, name="backward", grid=(nb,),
        in_specs=[rev(6 * d), rev(d), halo(SUBLANES, d), rev(d)] + [rev(d)] * 5 + [vm, vm, vm, vm, vm, vm, vm],
        out_specs=(rev(6 * d), pl.BlockSpec((2 * d, tb), lambda i: (0, nb - 1 - i)),
                   const((SL_ROWS, d)), const((g_rows, tw))),
        out_shape=(jax.ShapeDtypeStruct((t_len, 6 * d), MM),
                   jax.ShapeDtypeStruct((2 * d, t_len), MM),
                   jax.ShapeDtypeStruct((SL_ROWS, d), F32),
                   jax.ShapeDtypeStruct((g_rows, tw), MM)),
        scratch_shapes=[pltpu.VMEM((SUBLANES + tb, d), F32), buf(2 * d), buf(2 * d)] + [buf(d)] * 9
                       + [pltpu.VMEM((n_tiles, tw, tw), F32), pltpu.VMEM((n_tiles, tw, tw), F32),
                          pltpu.VMEM((A_GROUPS * SUBLANES, d), F32), car, car, car, car],
        compiler_params=_params(dimension_semantics=("arbitrary",)),
    )(p, h, h, dh, *saved, facc, wout, wa_t, wi_t, sp, ones_c, ones_l)


def _input_grad(dp, win_all, x, dh, sp, part, tb):
    t_len, d = x.shape
    nb = t_len // tb
    cols = win_all.shape[2]
    mid = min(nb - 1, (3 * nb) // 8)
    late = min(mid, nb // 4)
    rc = 32

    def body(dp_ref, win_ref, x_ref, dh_ref, sp_ref, part_ref, gx_ref, ln_ref, direct, relayed,
             send_sems, recv_sems, local_sems, acc_ref, ln_all, ln_send, ln_recv, mine, theirs):
        i = pl.program_id(0)
        x_, y_, c_ = _mesh_pos()
        first, second = 1 - c_, c_
        nbr1 = (x_ ^ c_, y_ ^ (1 - c_), c_)
        nbr2 = (x_ ^ (1 - c_), y_ ^ c_, c_)

        def remote(src, dst, k, to):
            return pltpu.make_async_remote_copy(src_ref=src, dst_ref=dst, send_sem=send_sems.at[k], recv_sem=recv_sems.at[k],
                                                device_id=to, device_id_type=MESH)

        to_first = [remote(part_ref.at[first], direct, 0, nbr1), remote(part_ref.at[2], theirs, 1, nbr1)]
        to_second = remote(theirs, relayed, 2, nbr2)
        load_mine = pltpu.make_async_copy(part_ref.at[second], mine, local_sems.at[0])

        @pl.when(i == 0)
        def _():
            acc_ref[...] = jnp.zeros_like(acc_ref)
            to_first[1].start()
            load_mine.start()

        @pl.when(i == late)
        def _():
            to_first[0].start()

        dxn = _dot_nt(dp_ref[:, 0:cols], win_ref[0])
        for j in range(1, N_DEV):
            dxn += _dot_nt(dp_ref[:, j * cols:(j + 1) * cols], win_ref[j])
        xv = x_ref[...]
        r0 = lax.rsqrt(jnp.mean(xv * xv, axis=-1, keepdims=True) + RMS_EPS)
        xhat = xv * r0
        acc_ref[...] += (dxn * xhat).reshape(tb // SUBLANES, SUBLANES, d).sum(axis=0)
        dxh = dxn * sp_ref[SP_LN_G:SP_LN_G + 1, :]
        gx_ref[...] = dh_ref[...] + r0 * (dxh - xhat * jnp.mean(dxh * xhat, axis=-1, keepdims=True))

        @pl.when(i == mid)
        def _():
            to_first[1].wait_recv()
            load_mine.wait()

            def add(r, carry):
                rows = pl.ds(r, rc)
                theirs[rows, :] = (mine[rows, :].astype(F32) + theirs[rows, :].astype(F32)).astype(MM)
                return carry

            _chunks(mine.shape[0], rc, add, 0)
            to_second.start()

        @pl.when(i == nb - 1)
        def _():
            to_first[0].wait_recv()
            to_second.wait_recv()
            for cp in to_first + [to_second]:
                cp.wait_send()
            ln_all[4 * x_ + 2 * y_ + c_] = jnp.broadcast_to(jnp.sum(acc_ref[...], axis=0, keepdims=True), acc_ref.shape)
            gather = _Gather(lambda a, px, py, pc: ln_all.at[4 * px + 2 * py + pc], ln_send, ln_recv)
            gather.start_own(0)
            gather.finish(0)
            total = ln_all[0]
            for dev in range(1, N_DEV):
                total = total + ln_all[dev]
            ln_ref[...] = total

    vm = pl.BlockSpec(memory_space=pltpu.VMEM)
    hbm = pl.BlockSpec(memory_space=pl.ANY)
    blk = lambda w: pl.BlockSpec((tb, w), lambda i: (i, 0))
    landed = jax.ShapeDtypeStruct(part.shape[1:], part.dtype)
    outs = pl.pallas_call(
        body, name="input_grad", grid=(nb,),
        in_specs=[blk(6 * d), vm, blk(d), blk(d), vm, hbm],
        out_specs=(blk(d), pl.BlockSpec((SUBLANES, d), lambda i: (0, 0)), hbm, hbm),
        out_shape=(jax.ShapeDtypeStruct((t_len, d), F32), jax.ShapeDtypeStruct((SUBLANES, d), F32), landed, landed),
        scratch_shapes=[pltpu.SemaphoreType.DMA((3,)), pltpu.SemaphoreType.DMA((3,)), pltpu.SemaphoreType.DMA((1,)),
                        pltpu.VMEM((SUBLANES, d), F32), pltpu.VMEM((N_DEV, SUBLANES, d), F32),
                        pltpu.SemaphoreType.DMA((7,)), pltpu.SemaphoreType.DMA((7,)),
                        pltpu.VMEM(part.shape[1:], MM), pltpu.VMEM(part.shape[1:], MM)],
        compiler_params=_params(dimension_semantics=("arbitrary",)),
    )(dp, win_all, x, dh, sp, part)
    return outs[0], outs[1], (outs[2], outs[3])


_CHIP_RELATIONS = [(0, 0), (1, 0), (0, 1), (1, 1)]


def _related_block(k, core):
    x, y, _ = _mesh_pos()
    fx, fy = _CHIP_RELATIONS[k]
    return 4 * (x ^ fx) + 2 * (y ^ fy) + core


class _ChipExchange:
    def __init__(self, part_refs, land_refs, send_sems, recv_sems):
        self.part_refs, self.land_refs, self.send_sems, self.recv_sems = part_refs, land_refs, send_sems, recv_sems

    def copies(self):
        x, y, c = _mesh_pos()
        for a in range(len(self.part_refs)):
            for k in (1, 2, 3):
                fx, fy = _CHIP_RELATIONS[k]
                yield pltpu.make_async_remote_copy(
                    src_ref=self.part_refs[a].at[k - 1], dst_ref=self.land_refs[a].at[k - 1],
                    send_sem=self.send_sems.at[3 * a + k - 1], recv_sem=self.recv_sems.at[3 * a + k - 1],
                    device_id=(x ^ fx, y ^ fy, c), device_id_type=MESH)

    def start(self):
        for cp in self.copies():
            cp.start()

    def finish(self):
        for cp in self.copies():
            cp.wait_recv()
        for cp in self.copies():
            cp.wait_send()


def _weight_grad_stage1(name, blk_shape, n_split, operands, in_specs, product, riders=(), slabs=()):
    n_rows, n_cols = blk_shape
    rs = n_rows // n_split
    rc = 32
    n_in, n_ride, n_slab = len(operands), len(riders), len(slabs)
    _, _, c = _mesh_pos()
    order = jnp.stack([_related_block(k, 1 - c) for k in range(4)]
                      + [_related_block(k, c) for k in (1, 2, 3, 0)]).astype(jnp.int32)

    def body(order_ref, *refs):
        ins = refs[:n_in]
        ride_in = refs[n_in:n_in + n_ride]
        slab_in = refs[n_in + n_ride:n_in + n_ride + n_slab]
        n_op = n_in + n_ride + n_slab
        part_ref, own_ref = refs[n_op:n_op + 2]
        ride_out = refs[n_op + 2:n_op + 2 + n_ride]
        gathered = refs[n_op + 2 + n_ride:n_op + 2 + n_ride + n_slab]
        (gbuf, sendbuf, from_sib, send_sems, recv_sems, ride_send, ride_recv,
         slab_send, slab_recv, slab_local) = refs[n_op + 2 + n_ride + n_slab:]
        exchange = _ChipExchange(ride_in, ride_out, ride_send, ride_recv)
        s = pl.program_id(0)
        x, y, c = _mesh_pos()
        me = 4 * x + 2 * y + c
        gather = _BalancedGather(lambda a, px, py, pc: gathered[a].at[4 * px + 2 * py + pc], slab_send, slab_recv, slab_in)
        keep_own = [pltpu.make_async_copy(slab_in[a], gathered[a].at[me], slab_local.at[a]) for a in range(n_slab)]

        def to_sibling(k):
            return pltpu.make_async_remote_copy(
                src_ref=sendbuf.at[k], dst_ref=from_sib.at[k], send_sem=send_sems.at[k], recv_sem=recv_sems.at[k],
                device_id=(x, y, 1 - c), device_id_type=MESH)

        @pl.when(s == 0)
        def _():
            exchange.start()
            for a in range(n_slab):
                gather.start_own(a)
                keep_own[a].start()

        @pl.when(s == 5)
        def _():
            for a in range(n_slab):
                gather.on_neighbour(a, 0)
                gather.on_neighbour(a, 1)

        @pl.when(s == 7)
        def _():
            for a in range(n_slab):
                gather.on_diagonal(a)

        for h in range(n_split):
            gbuf[h * rs:(h + 1) * rs, :] = product(ins, h)

        @pl.when(s < 4)
        def _():
            def narrow(r, carry):
                sendbuf[s, pl.ds(r, rc), :] = gbuf[pl.ds(r, rc), :].astype(MM)
                return carry

            _chunks(n_rows, rc, narrow, 0)
            to_sibling(s).start()

        @pl.when(s >= 4)
        def _():
            k = jnp.where(s == 7, 0, s - 3)
            to_sibling(k).wait_recv()

            @pl.when(s < 7)
            def _():
                def add(r, carry):
                    rows = pl.ds(r, rc)
                    part_ref[0, rows, :] = (gbuf[rows, :] + from_sib[k, rows, :].astype(F32)).astype(MM)
                    return carry

                _chunks(n_rows, rc, add, 0)

            @pl.when(s == 7)
            def _():
                def add(r, carry):
                    rows = pl.ds(r, rc)
                    own_ref[rows, :] = gbuf[rows, :] + from_sib[0, rows, :].astype(F32)
                    return carry

                _chunks(n_rows, rc, add, 0)
                for kk in range(4):
                    to_sibling(kk).wait_send()
                exchange.finish()
                for a in range(n_slab):
                    gather.wait_sibling(a)
                    for j in range(3):
                        gather.wait_passed_on(a, j)
                    gather.wait_sends(a)
                    keep_own[a].wait()

    hbm = pl.BlockSpec(memory_space=pl.ANY)
    grid_spec = pltpu.PrefetchScalarGridSpec(
        num_scalar_prefetch=1, grid=(N_DEV,), in_specs=list(in_specs) + [hbm] * (n_ride + n_slab),
        out_specs=(pl.BlockSpec((1, n_rows, n_cols), lambda s, o: (jnp.clip(s - 4, 0, 2), 0, 0)),
                   pl.BlockSpec((n_rows, n_cols), lambda s, o: (0, 0))) + (hbm,) * (n_ride + n_slab),
        scratch_shapes=[pltpu.VMEM((n_rows, n_cols), F32), pltpu.VMEM((4, n_rows, n_cols), MM),
                        pltpu.VMEM((4, n_rows, n_cols), MM),
                        pltpu.SemaphoreType.DMA((4,)), pltpu.SemaphoreType.DMA((4,)),
                        pltpu.SemaphoreType.DMA((max(3 * n_ride, 1),)), pltpu.SemaphoreType.DMA((max(3 * n_ride, 1),)),
                        pltpu.SemaphoreType.DMA((max(8 * n_slab, 1),)), pltpu.SemaphoreType.DMA((max(8 * n_slab, 1),)),
                        pltpu.SemaphoreType.DMA((max(n_slab, 1),))])
    outs = pl.pallas_call(
        body, name=name, grid_spec=grid_spec,
        out_shape=(jax.ShapeDtypeStruct((3, n_rows, n_cols), MM), jax.ShapeDtypeStruct((n_rows, n_cols), F32))
                  + tuple(jax.ShapeDtypeStruct(p.shape, p.dtype) for p in riders)
                  + tuple(jax.ShapeDtypeStruct((N_DEV,) + a.shape, a.dtype) for a in slabs),
        compiler_params=_params(dimension_semantics=("arbitrary",)),
    )(order, *operands, *riders, *slabs)
    return outs[0], outs[1], outs[2:2 + n_ride], outs[2 + n_ride:]


def _weight_grad_in(xnt, dp, riders, slabs):
    d, t_len = xnt.shape
    cols = dp.shape[1] // N_DEV
    half = d // 2
    return _weight_grad_stage1(
        "weight_grad_in", (d, cols), 2, (xnt, dp),
        [pl.BlockSpec(memory_space=pltpu.VMEM), pl.BlockSpec((t_len, cols), lambda s, o: (0, o[s]))],
        lambda refs, h: _dot(refs[0][h * half:(h + 1) * half, :], refs[1][...]), riders, slabs)


def _weight_grad_out(yt, dhb):
    d2, t_len = yt.shape
    d = dhb.shape[1]
    rows = d2 // N_DEV
    return _weight_grad_stage1(
        "weight_grad_out", (rows, d), 1, (yt, dhb),
        [pl.BlockSpec((rows, t_len), lambda s, o: (o[s], 0)), pl.BlockSpec(memory_space=pltpu.VMEM)],
        lambda refs, h: _dot(refs[0][...], refs[1][...]))


def _update_shard(own, others, w, m, v, name):
    n_rows, n_cols = w.shape
    rb = min(256, n_rows)
    n_other = len(others)

    def body(own_ref, *refs):
        other_refs = refs[:n_other]
        w_ref, m_ref, v_ref, grad_ref, delta_ref, mo_ref, vo_ref = refs[n_other:]
        g = own_ref[...]
        for ref in other_refs:
            for k in range(ref.shape[0] if len(ref.shape) == 3 else 1):
                g = g + (ref[k] if len(ref.shape) == 3 else ref[...]).astype(F32)
        delta, m_new, v_new = _adamw(w_ref[...], g, m_ref[...], v_ref[...])
        grad_ref[...] = g
        delta_ref[...] = delta
        mo_ref[...] = m_new
        vo_ref[...] = v_new

    blk = pl.BlockSpec((rb, n_cols), lambda i: (i, 0))
    stacked = lambda n: pl.BlockSpec((n, rb, n_cols), lambda i: (0, i, 0))
    out = jax.ShapeDtypeStruct((n_rows, n_cols), F32)
    return pl.pallas_call(
        body, name=name, grid=(n_rows // rb,),
        in_specs=[blk] + [stacked(o.shape[0]) if o.ndim == 3 else blk for o in others] + [blk, blk, blk],
        out_specs=(blk, blk, blk, blk), out_shape=(out, out, out, out),
        compiler_params=_params(dimension_semantics=("arbitrary",)),
    )(own, *others, w, m, v)


def _small_update(gat_v, gat_g, ln_tot, vec_w, vec_m, vec_v, gates, convs):
    n_vec = len(vec_w)
    n_heads, hd, _ = gates[0].shape
    tw = gat_g.shape[2]
    s8 = SUBLANES
    per = tw // hd
    n_tiles = n_heads // per
    cc = convs[0].shape[1]
    n_in = 3 + 3 * n_vec + 12

    def body(*refs):
        gv_ref, gg_ref, ln_ref = refs[:3]
        w_refs, m_refs, v_refs = (refs[3 + j * n_vec:3 + (j + 1) * n_vec] for j in range(3))
        gate_refs = refs[3 + 3 * n_vec:3 + 3 * n_vec + 6]
        conv_refs = refs[3 + 3 * n_vec + 6:n_in]
        loss_o = refs[n_in]
        kinds = [refs[n_in + 1 + j * (n_vec + 4):n_in + 1 + (j + 1) * (n_vec + 4)] for j in range(4)]
        tv, tg = refs[n_in + 1 + 4 * (n_vec + 4):]
        x, y, c = _mesh_pos()
        me = 4 * x + 2 * y + c

        def emit(k_out, w, g, m, v):
            delta, m_new, v_new = _adamw(w, g, m, v)
            for ref, val in zip(k_out, (g, delta, m_new, v_new)):
                ref[...] = val

        total = gv_ref[0]
        for dev in range(1, N_DEV):
            total = total + gv_ref[dev]
        tv[...] = total
        tv[SL_LN_G:SL_LN_G + 1, :] = ln_ref[0:1, :]

        def sum_gates(r, carry):
            rows = pl.ds(r, 2 * s8)
            part = gg_ref[0, rows, :].astype(F32)
            for dev in range(1, N_DEV):
                part = part + gg_ref[dev, rows, :].astype(F32)
            tg[rows, :] = part
            return carry

        _chunks(tg.shape[0], 2 * s8, sum_gates, 0)
        loss_o[...] = jnp.broadcast_to(tv[SL_LOSS:SL_LOSS + 1, 0:LANES], loss_o.shape)
        for p in range(n_vec):
            w, g = w_refs[p][...], tv[SL_LN_G + p, :]
            if SL_LN_G + p == SL_LAM:
                g = g * (RG_LRU_C * jax.nn.sigmoid(-w))
            emit([k_out[p] for k_out in kinds], w, g, m_refs[p][...], v_refs[p][...])
        lanes = pl.ds(pl.multiple_of(me * cc, cc), cc)
        for j, (row0, n) in enumerate(((SL_CONV_W, 3), (SL_LRU_W, 4))):
            w_ref, m_ref, v_ref = conv_refs[3 * j:3 * j + 3]
            emit([k_out[n_vec + 2 + j] for k_out in kinds], w_ref[...], tv[row0:row0 + n, lanes], m_ref[...], v_ref[...])
        for mtx in range(2):
            w_ref, m_ref, v_ref = gate_refs[3 * mtx:3 * mtx + 3]
            for k in range(n_tiles):
                tile = tg[(mtx * n_tiles + k) * hd:(mtx * n_tiles + k + 1) * hd, :]
                for a in range(per):
                    head = k * per + a
                    g = tile[:, a * hd:(a + 1) * hd]
                    delta, m_new, v_new = _adamw(w_ref[head], g, m_ref[head], v_ref[head])
                    for k_out, val in zip(kinds, (g, delta, m_new, v_new)):
                        k_out[n_vec + mtx][head] = val

    vm = pl.BlockSpec(memory_space=pltpu.VMEM)
    like = lambda a: jax.ShapeDtypeStruct(a.shape, F32)
    per_kind = tuple(like(a) for a in vec_w) + (like(gates[0]), like(gates[3]), like(convs[0]), like(convs[3]))
    n_out = 1 + 4 * len(per_kind)
    outs = pl.pallas_call(
        body, name="small_update",
        in_specs=[vm] * n_in, out_specs=(vm,) * n_out,
        out_shape=(jax.ShapeDtypeStruct((SUBLANES, LANES), F32),) + per_kind * 4,
        scratch_shapes=[pltpu.VMEM(gat_v.shape[1:], F32), pltpu.VMEM(gat_g.shape[1:], F32)],
        compiler_params=_params(),
    )(gat_v, gat_g, ln_tot, *vec_w, *vec_m, *vec_v, *gates, *convs)
    return outs[0], [outs[1 + j * len(per_kind):1 + (j + 1) * len(per_kind)] for j in range(4)]


def _head_ones(head_dim, tw):
    lane = jnp.arange(tw) // head_dim
    return (lane[:, None] == lane[None, :]).astype(MM)


def kernel(x, ln_g, w_in, conv_w, lru_conv_w, lru_conv_b, w_a, b_a, w_i, b_i, lam, conv_out_g, lru_out_g, w_out, final_g, loss_target, m_ln_g, m_w_in, m_conv_w, m_lru_conv_w, m_lru_conv_b, m_w_a, m_b_a, m_w_i, m_b_i, m_lam, m_conv_out_g, m_lru_out_g, m_w_out, m_final_g, v_ln_g, v_w_in, v_conv_w, v_lru_conv_w, v_lru_conv_b, v_w_a, v_b_a, v_w_i, v_b_i, v_lam, v_conv_out_g, v_lru_out_g, v_w_out, v_final_g):
    _, t_len, d = x.shape
    hd_l = d // N_LRU_HEADS
    tw = min(MXU_TILE, d)
    x2, tgt2 = x[0], loss_target[0]

    small = [ln_g, lru_conv_b, b_a, b_i, lam, conv_out_g, lru_out_g, final_g]
    p, xnt, win_all, wout_all, _, sp, wa_t, wi_t = _gather_project(
        x2, w_in, w_out, conv_w, lru_conv_w, ln_g.reshape(1, d), w_a, w_i, small, min(256, t_len), tw)
    wout_full = wout_all.reshape(N_DEV * w_out.shape[0], d)
    ones_c, ones_l = _head_ones(d // N_CONV_HEADS, tw), _head_ones(hd_l, tw)

    h, dh, dhb, facc, *saved = _forward(x2, tgt2, p, wout_full, wa_t, wi_t, sp, ones_c, ones_l, min(256, t_len))
    dp, yt, slab_v, slab_g = _backward(p, h, dh, saved, facc, wout_full, wa_t, wi_t, sp, ones_c, ones_l, min(256, t_len))
    part_out, own_out, _, _ = _weight_grad_out(yt, dhb)
    part_in, own_in, (chips_out,), (gat_v, gat_g) = _weight_grad_in(xnt, dp, (part_out,), (slab_v, slab_g))
    grad_x, ln_tot, sums_in = _input_grad(dp, win_all, x2, dh, sp, part_in, min(512, t_len))
    gw_in, dw_in, mw_in, vw_in = _update_shard(own_in, sums_in, w_in, m_w_in, v_w_in, "update_w_in")
    gw_out, dw_out, mw_out, vw_out = _update_shard(own_out, (chips_out,), w_out, m_w_out, v_w_out, "update_w_out")

    loss_tile, kinds = _small_update(
        gat_v, gat_g, ln_tot, small,
        [m_ln_g, m_lru_conv_b, m_b_a, m_b_i, m_lam, m_conv_out_g, m_lru_out_g, m_final_g],
        [v_ln_g, v_lru_conv_b, v_b_a, v_b_i, v_lam, v_conv_out_g, v_lru_out_g, v_final_g],
        (w_a, m_w_a, v_w_a, w_i, m_w_i, v_w_i), (conv_w, m_conv_w, v_conv_w, lru_conv_w, m_lru_conv_w, v_lru_conv_w))

    def unpack(kind, big_in, big_out):
        vec, (wa_, wi_, cw_, lw_) = kind[:len(small)], kind[len(small):]
        return [vec[0], big_in, cw_, lw_, vec[1], wa_, vec[2], wi_, vec[3], vec[4], vec[5], vec[6], big_out, vec[7]]

    return (loss_tile[0, 0], grad_x[None], *unpack(kinds[0], gw_in, gw_out), *unpack(kinds[1], dw_in, dw_out),
            *unpack(kinds[2], mw_in, mw_out), *unpack(kinds[3], vw_in, vw_out))
```

```python
import jax
import jax.numpy as jnp
from jax import lax
from jax.experimental import pallas as pl
from jax.experimental.pallas import tpu as pltpu

F32 = jnp.float32
MM = jnp.bfloat16
MESH = pl.DeviceIdType.MESH

N_DEV = 8
N_CONV_HEADS = 8
N_LRU_HEADS = 16
RG_LRU_C = 8.0
RMS_EPS = 1e-6
ADAM_LR, ADAM_B1, ADAM_B2, ADAM_EPS, ADAM_WD, ADAM_STEP = 0.001, 0.9, 0.999, 1e-08, 0.01, 10
ADAM_BC1 = 1.0 - ADAM_B1 ** ADAM_STEP
ADAM_BC2 = 1.0 - ADAM_B2 ** ADAM_STEP

SUBLANES = 8
LANES = 128
MXU_TILE = 256
VMEM_LIMIT = 56 * 1024 * 1024

SP_LN_G, SP_LRU_B, SP_B_A, SP_B_I, SP_LAM, SP_CONV_G, SP_LRU_G, SP_FINAL_G, SP_CONV_W, SP_LRU_W = 0, 1, 2, 3, 4, 5, 6, 7, 8, 11
SP_ROWS = 16
P_B, P_C, P_XC, P_GC, P_XL, P_GL = 0, 1, 2, 3, 4, 5
A_CONV_G, A_LRU_G, A_LAM, A_B_A, A_B_I, A_CONV_W, A_LRU_W, A_LRU_B = 0, 1, 2, 3, 4, 5, 8, 12
A_GROUPS = 13
SL_LOSS, SL_LN_G, SL_LRU_B, SL_B_A, SL_B_I, SL_LAM, SL_CONV_G, SL_LRU_G, SL_FINAL_G, SL_CONV_W, SL_LRU_W = 0, 1, 2, 3, 4, 5, 6, 7, 8, 16, 24
SL_ROWS = 32


def _params(vmem=True, **kw):
    if vmem:
        kw["vmem_limit_bytes"] = VMEM_LIMIT
    return pltpu.CompilerParams(**kw)


def _dot(a, b):
    return jnp.dot(a, b, preferred_element_type=F32)


def _dot_nt(a, b):
    return lax.dot_general(a, b, (((1,), (1,)), ((), ())), preferred_element_type=F32)


def _head_sums(v, ones_tile):
    tw = ones_tile.shape[0]
    vb = v.astype(MM)
    return jnp.concatenate([_dot(vb[:, k:k + tw], ones_tile) for k in range(0, v.shape[1], tw)], axis=1)


def _head_rstd(v, ones_tile, head_dim):
    return lax.rsqrt(_head_sums(v * v, ones_tile) * (1.0 / head_dim) + RMS_EPS)


def _sigmoid(x):
    return 0.5 * jnp.tanh(0.5 * x) + 0.5


def _lru_input_scale_sq(log_a, a):
    return -jnp.tanh(log_a) * (1.0 + a * a)


def _log_sigmoid(x):
    z = jnp.exp(-jnp.abs(x))
    u = 1.0 + z
    log1p_z = jnp.where(u == 1.0, z, jnp.log(u) * (z / (u - 1.0)))
    return jnp.minimum(x, 0.0) - log1p_z


def _row_iota(d):
    return lax.broadcasted_iota(jnp.int32, (SUBLANES, d), 0)


def _shift_down(cur, prev, s, row):
    return jnp.where(row >= s, pltpu.roll(cur, s, axis=0), pltpu.roll(prev, s, axis=0))


def _shift_up(cur, nxt, s, row):
    k = SUBLANES - s
    return jnp.where(row < k, pltpu.roll(cur, k, axis=0), pltpu.roll(nxt, k, axis=0))


def _scan_fwd(a, b, h_prev, row):
    for s in (1, 2, 4):
        a_s = jnp.where(row >= s, pltpu.roll(a, s, axis=0), 1.0)
        b_s = jnp.where(row >= s, pltpu.roll(b, s, axis=0), 0.0)
        b = a * b_s + b
        a = a * a_s
    return a * h_prev + b


def _scan_bwd(a_next, b, g_next, row):
    a = a_next
    for s in (1, 2, 4):
        k = SUBLANES - s
        a_s = jnp.where(row < k, pltpu.roll(a, k, axis=0), 1.0)
        b_s = jnp.where(row < k, pltpu.roll(b, k, axis=0), 0.0)
        b = a * b_s + b
        a = a * a_s
    return a * g_next + b


def _bcast_row(v, r):
    return jnp.broadcast_to(v[r:r + 1, :], v.shape)


def _chunks(n_rows, rc, body, init, reverse=False):
    n = n_rows // rc

    def step(i, carry):
        j = (n - 1 - i) if reverse else i
        return body(pl.multiple_of(j * rc, rc), carry)

    return lax.fori_loop(0, n, step, init)


def _adamw(w, g, m, v):
    m = ADAM_B1 * m + (1.0 - ADAM_B1) * g
    v = ADAM_B2 * v + (1.0 - ADAM_B2) * (g * g)
    m_hat = m / ADAM_BC1
    v_hat = v / ADAM_BC2
    delta = -ADAM_LR * (m_hat / (jnp.sqrt(v_hat) + ADAM_EPS) + ADAM_WD * w)
    return delta, m, v


def _mesh_pos():
    return lax.axis_index("x"), lax.axis_index("y"), lax.axis_index("c")


class _Gather:
    def __init__(self, blocks_of, send_sems, recv_sems, own_src=None):
        x, y, c = _mesh_pos()
        self.c = c
        self.me, self.sibling = (x, y, c), (x, y, 1 - c)
        self.chips = [(1 - x, y), (x, 1 - y), (1 - x, 1 - y)]
        self.blocks_of, self.send_sems, self.recv_sems = blocks_of, send_sems, recv_sems
        self.own_src = own_src

    def copy(self, a, k, block, to):
        src = self.blocks_of(a, *block)
        if block is self.me and self.own_src is not None:
            src = self.own_src[a]
        return pltpu.make_async_remote_copy(
            src_ref=src, dst_ref=self.blocks_of(a, *block),
            send_sem=self.send_sems.at[a * 7 + k], recv_sem=self.recv_sems.at[a * 7 + k],
            device_id=to, device_id_type=MESH)

    def start_own(self, a):
        self.copy(a, 0, self.me, self.sibling).start()
        for j, chip in enumerate(self.chips):
            self.copy(a, 1 + j, self.me, (*chip, self.c)).start()

    def wait_sibling(self, a):
        self.copy(a, 0, self.sibling, self.me).wait_recv()

    def wait_chip_and_pass_on(self, a, j):
        block = (*self.chips[j], self.c)
        self.copy(a, 1 + j, block, self.me).wait_recv()
        self.copy(a, 4 + j, block, self.sibling).start()

    def wait_passed_on(self, a, j):
        self.copy(a, 4 + j, (*self.chips[j], 1 - self.c), self.me).wait_recv()

    def wait_sends(self, a):
        self.copy(a, 0, self.me, self.sibling).wait_send()
        for j, chip in enumerate(self.chips):
            self.copy(a, 1 + j, self.me, (*chip, self.c)).wait_send()
            self.copy(a, 4 + j, (*chip, self.c), self.sibling).wait_send()

    def finish(self, a):
        for j in range(3):
            self.wait_chip_and_pass_on(a, j)
        self.wait_sibling(a)
        for j in range(3):
            self.wait_passed_on(a, j)
        self.wait_sends(a)


class _BalancedGather:
    def __init__(self, slot, send_sems, recv_sems, own_src):
        x, y, c = _mesh_pos()
        self.c = c
        self.me, self.sibling = (x, y, c), (x, y, 1 - c)
        self.chips = [(1 - x, y), (x, 1 - y), (1 - x, 1 - y)]
        self.slot, self.send_sems, self.recv_sems, self.own_src = slot, send_sems, recv_sems, own_src

    def half(self, a, block, which):
        ref = self.slot(a, *block)
        n = ref.shape[0] // 2
        return ref.at[pl.ds(which * n, n)]

    def copy(self, a, k, src, dst, to):
        return pltpu.make_async_remote_copy(
            src_ref=src, dst_ref=dst, send_sem=self.send_sems.at[a * 8 + k], recv_sem=self.recv_sems.at[a * 8 + k],
            device_id=to, device_id_type=MESH)

    def whole(self, a, k, block, to):
        src = self.own_src[a] if block is self.me else self.slot(a, *block)
        return self.copy(a, k, src, self.slot(a, *block), to)

    def halved(self, a, k, block, which, to):
        return self.copy(a, k, self.half(a, block, which), self.half(a, block, which), to)

    def on(self, chip):
        return (*self.chips[chip], self.c)

    def start_own(self, a):
        self.whole(a, 0, self.me, self.sibling).start()
        self.whole(a, 1, self.me, self.on(0)).start()
        self.whole(a, 2, self.me, self.on(1)).start()

    def start_own_staggered(self, a):
        self.whole(a, 0, self.me, self.sibling).start()
        for core, order in ((1, (0, 1)), (0, (1, 0))):
            @pl.when(self.c == core)
            def _(order=order):
                for j in order:
                    self.whole(a, 1 + j, self.me, self.on(j)).start()

    def wait_sibling(self, a):
        self.whole(a, 0, self.sibling, self.me).wait_recv()

    def on_neighbour(self, a, j):
        self.whole(a, 1 + j, self.on(j), self.me).wait_recv()
        self.halved(a, 3 + j, self.on(j), j, self.on(1 - j)).start()
        self.whole(a, 5 + j, self.on(j), self.sibling).start()

    def on_diagonal(self, a):
        self.halved(a, 3, self.on(2), 0, self.me).wait_recv()
        self.halved(a, 4, self.on(2), 1, self.me).wait_recv()
        self.whole(a, 7, self.on(2), self.sibling).start()

    def wait_passed_on(self, a, j):
        self.whole(a, 5 + j, (*self.chips[j], 1 - self.c), self.me).wait_recv()

    def wait_sends(self, a):
        self.whole(a, 0, self.me, self.sibling).wait_send()
        for j in range(2):
            self.whole(a, 1 + j, self.me, self.on(j)).wait_send()
            self.halved(a, 3 + j, self.on(j), j, self.on(1 - j)).wait_send()
        for j in range(3):
            self.whole(a, 5 + j, self.on(j), self.sibling).wait_send()


def _block_order():
    x, y, c = _mesh_pos()
    idx = lambda chip, core: 4 * chip[0] + 2 * chip[1] + core
    own, first, second, diag = (x, y), (x ^ c, y ^ (1 - c)), (x ^ (1 - c), y ^ c), (1 - x, 1 - y)
    order = [idx(own, c), idx(own, 1 - c), idx(first, c), idx(second, 1 - c), idx(second, c), idx(first, 1 - c),
             idx(diag, c), idx(diag, 1 - c)]
    return jnp.stack(order).astype(jnp.int32)


def _gather_project(x, w_in, w_out, conv_w, lru_conv_w, ln_g, w_a, w_i, vecs, tb, tw):
    t_len, d = x.shape
    nb = t_len // tb
    cols = w_in.shape[1]
    mc = min(512, t_len)
    conv_pack = jax.ShapeDtypeStruct((SUBLANES, conv_w.shape[1]), F32)
    srcs = (w_in, w_out, conv_pack)
    dts = (MM, MM, F32)
    n_vec = len(vecs)
    n_heads, hd, _ = w_a.shape
    per = tw // hd

    def body(order_ref, x_ref, win_ref, wout_ref, cw_ref, lw_ref, lng_ref, wa_ref, wi_ref, *refs):
        vec_refs = refs[:n_vec]
        (p_ref, xnt_ref, win_all, wout_all, cp_all, sp_ref, wat_ref, wit_ref,
         xnb, wall, st_out, st_cp, cp_vm, send_sems, recv_sems, cp_send, cp_recv, local_sems) = refs[n_vec:]
        i = pl.program_id(0)
        x_, y_, c_ = _mesh_pos()
        me = 4 * x_ + 2 * y_ + c_
        outs = (win_all, wout_all, cp_all)
        lands = (wall, wout_all, cp_all)
        stages = (wall.at[me], st_out, st_cp)
        gather = _BalancedGather(lambda a, px, py, pc: lands[a].at[4 * px + 2 * py + pc], send_sems, recv_sems, stages)
        small = _Gather(lambda a, px, py, pc: cp_all.at[4 * px + 2 * py + pc], cp_send, cp_recv, own_src=[st_cp])
        keep_own = [pltpu.make_async_copy(stages[a], outs[a].at[me], local_sems.at[a]) for a in range(3)]

        def keep(k):
            blk = order_ref[k]
            return pltpu.make_async_copy(wall.at[blk], win_all.at[blk], local_sems.at[2 + k])

        @pl.when(i == 0)
        def _():
            for a, src in enumerate((win_ref, wout_ref)):
                dst, rc = stages[a], 32

                def cast(r, carry, src=src, dst=dst):
                    dst[pl.ds(r, rc), :] = src[pl.ds(r, rc), :].astype(dst.dtype)
                    return carry

                _chunks(src.shape[0], rc, cast, 0)
                keep_own[a].start()
            gather.start_own_staggered(0)
            n_cw, n_lw = cw_ref.shape[0], lw_ref.shape[0]
            st_cp[...] = jnp.zeros_like(st_cp)
            st_cp[0:n_cw, :] = cw_ref[...]
            st_cp[n_cw:n_cw + n_lw, :] = lw_ref[...]
            keep_own[2].start()

        @pl.when(i < nb)
        def _():
            xv = x_ref[...]
            r0 = lax.rsqrt(jnp.mean(xv * xv, axis=-1, keepdims=True) + RMS_EPS)
            xn = xv * r0 * lng_ref[...]
            xnb[pl.ds(pl.multiple_of(i * tb, tb), tb), :] = xn.astype(MM)
            xnt_ref[...] = xn.T.astype(MM)

        def by_core(action):
            for core, (first, second) in ((1, (0, 1)), (0, (1, 0))):
                @pl.when(c_ == core)
                def _(first=first, second=second):
                    action(first, second)

        for k in range(N_DEV):
            @pl.when(i == nb + k)
            def _(k=k):
                if k == 1:
                    gather.wait_sibling(0)
                elif k == 2:
                    by_core(lambda first, second: gather.on_neighbour(0, first))
                    gather.start_own(1)
                    small.start_own(0)
                elif k == 3:
                    by_core(lambda first, second: gather.wait_passed_on(0, second))
                elif k == 4:
                    by_core(lambda first, second: gather.on_neighbour(0, second))
                elif k == 5:
                    by_core(lambda first, second: gather.wait_passed_on(0, first))
                    gather.on_neighbour(1, 0)
                    gather.on_neighbour(1, 1)
                elif k == 6:
                    gather.on_diagonal(0)
                elif k == 7:
                    gather.wait_passed_on(0, 2)
                    gather.on_diagonal(1)
                blk = order_ref[k]
                if k:
                    keep(k).start()

                def project(r, carry):
                    rows = pl.ds(r, mc)
                    p_ref[rows, :] = _dot(xnb[rows, :], wall[blk]).astype(MM)
                    return carry

                _chunks(t_len, mc, project, 0)
                if k == N_DEV - 1:
                    gather.wait_sends(0)
                    gather.wait_sibling(1)
                    for j in range(3):
                        gather.wait_passed_on(1, j)
                    gather.wait_sends(1)
                    small.finish(0)
                    for cp in keep_own + [keep(kk) for kk in range(1, N_DEV)]:
                        cp.wait()
                    load = pltpu.make_async_copy(cp_all, cp_vm, local_sems.at[N_DEV + 2])
                    load.start()
                    load.wait()
                    for r, ref in enumerate(vec_refs):
                        sp_ref[r, :] = ref[...]
                    sp_ref[n_vec:n_vec + SUBLANES, :] = jnp.concatenate([cp_vm[dev] for dev in range(N_DEV)], axis=1)
                    for src, dst in ((wa_ref, wat_ref), (wi_ref, wit_ref)):
                        dst[...] = jnp.zeros_like(dst)
                        for head in range(n_heads):
                            lo = (head % per) * hd
                            dst[head // per, lo:lo + hd, lo:lo + hd] = src[head].astype(MM)

    vm = pl.BlockSpec(memory_space=pltpu.VMEM)
    hbm = pl.BlockSpec(memory_space=pl.ANY)
    grid_spec = pltpu.PrefetchScalarGridSpec(
        num_scalar_prefetch=1, grid=(nb + N_DEV,),
        in_specs=[pl.BlockSpec((tb, d), lambda i, o: (jnp.minimum(i, nb - 1), 0))] + [vm] * (7 + n_vec),
        out_specs=(pl.BlockSpec((t_len, cols), lambda i, o: (0, o[jnp.maximum(i - nb, 0)])),
                   pl.BlockSpec((d, tb), lambda i, o: (0, jnp.minimum(i, nb - 1))), hbm, hbm, hbm,
                   pl.BlockSpec((SP_ROWS, d), lambda i, o: (0, 0)),
                   pl.BlockSpec((n_heads // per, tw, tw), lambda i, o: (0, 0, 0)),
                   pl.BlockSpec((n_heads // per, tw, tw), lambda i, o: (0, 0, 0))),
        scratch_shapes=[pltpu.VMEM((t_len, d), MM), pltpu.VMEM((N_DEV,) + w_in.shape, MM),
                        pltpu.VMEM(w_out.shape, MM), pltpu.VMEM(conv_pack.shape, F32),
                        pltpu.VMEM((N_DEV,) + conv_pack.shape, F32),
                        pltpu.SemaphoreType.DMA((16,)), pltpu.SemaphoreType.DMA((16,)),
                        pltpu.SemaphoreType.DMA((7,)), pltpu.SemaphoreType.DMA((7,)), pltpu.SemaphoreType.DMA((N_DEV + 3,))])
    return pl.pallas_call(
        body, name="gather_project", grid_spec=grid_spec,
        out_shape=(jax.ShapeDtypeStruct((t_len, N_DEV * cols), MM),
                   jax.ShapeDtypeStruct((d, t_len), MM))
                  + tuple(jax.ShapeDtypeStruct((N_DEV,) + s.shape, dt) for s, dt in zip(srcs, dts))
                  + (jax.ShapeDtypeStruct((SP_ROWS, d), F32),)
                  + (jax.ShapeDtypeStruct((n_heads // per, tw, tw), MM),) * 2,
        compiler_params=_params(dimension_semantics=("arbitrary",)),
    )(_block_order(), x, w_in, w_out, conv_w, lru_conv_w, ln_g, w_a, w_i, *vecs)


def _forward(x, tgt, p, wout, wa_t, wi_t, sp, ones_c, ones_l, tb):
    t_len, d = x.shape
    nb = t_len // tb
    n_tiles, tw = wa_t.shape[0], wa_t.shape[1]
    hd_c, hd_l = d // N_CONV_HEADS, d // N_LRU_HEADS
    s8 = SUBLANES

    def body(x_ref, tgt_ref, p_ref, wout_ref, wa_ref, wi_ref, sp_ref, oc_ref, ol_ref,
             h_ref, dh_ref, dhb_ref, acc_ref, yc, czs, u, pa, pi,
             rcf, rlf, ybuf, tail_z, tail_xl, hcar):
        i = pl.program_id(0)
        row = _row_iota(d)

        @pl.when(i == 0)
        def _():
            tail_z[...] = jnp.zeros_like(tail_z)
            tail_xl[...] = jnp.zeros_like(tail_xl)
            hcar[...] = jnp.zeros_like(hcar)
            acc_ref[...] = jnp.zeros_like(acc_ref)

        def spr(r):
            return sp_ref[r:r + 1, :]

        def proj(rows, seg):
            return p_ref[rows, seg * d:(seg + 1) * d].astype(F32)

        w0, w1, w2 = spr(SP_CONV_W), spr(SP_CONV_W + 1), spr(SP_CONV_W + 2)
        l0, l1, l2, l3 = spr(SP_LRU_W), spr(SP_LRU_W + 1), spr(SP_LRU_W + 2), spr(SP_LRU_W + 3)
        lb = spr(SP_LRU_B)

        def convs(r, carry):
            zp, xp = carry
            rows16 = pl.ds(r, 2 * s8)
            bg16, xl16 = proj(rows16, P_B), proj(rows16, P_XL)
            z16 = proj(rows16, P_C) * proj(rows16, P_XC)
            for j in range(2):
                rows, sub = pl.ds(r + j * s8, s8), slice(j * s8, (j + 1) * s8)
                z, xl = z16[sub], xl16[sub]
                cz = w0 * _shift_down(z, zp, 2, row) + w1 * _shift_down(z, zp, 1, row) + w2 * z
                czs[rows, :] = cz
                yc[rows, :] = bg16[sub] * cz
                u[rows, :] = (l0 * _shift_down(xl, xp, 3, row) + l1 * _shift_down(xl, xp, 2, row)
                              + l2 * _shift_down(xl, xp, 1, row) + l3 * xl + lb)
                zp, xp = z, xl
            return zp, xp

        z_last, xl_last = _chunks(tb, 2 * s8, convs, (tail_z[...], tail_xl[...]))
        tail_z[...] = z_last
        tail_xl[...] = xl_last

        ub = u[...].astype(MM)
        for k in range(n_tiles):
            sl = slice(k * tw, (k + 1) * tw)
            pa[:, sl] = _dot(ub[:, sl], wa_ref[k])
            pi[:, sl] = _dot(ub[:, sl], wi_ref[k])
        rcf[...] = _head_rstd(yc[...], oc_ref[...], hd_c)

        c8 = RG_LRU_C * _log_sigmoid(spr(SP_LAM))
        b_a, b_i = spr(SP_B_A), spr(SP_B_I)

        def lru(r, hp):
            rows = pl.ds(r, SUBLANES)
            ra = _sigmoid(pa[rows, :] + b_a)
            ii = _sigmoid(pi[rows, :] + b_i)
            pa[rows, :] = ra
            pi[rows, :] = ii
            la = ra * c8
            a = jnp.exp(la)
            mult = jnp.sqrt(_lru_input_scale_sq(la, a))
            h = _scan_fwd(a, mult * (ii * u[rows, :]), hp, row)
            h_ref[rows, :] = h
            return _bcast_row(h, SUBLANES - 1)

        hcar[...] = _chunks(tb, SUBLANES, lru, hcar[...])
        rlf[...] = _head_rstd(h_ref[...], ol_ref[...], hd_l)

        g_c, g_l = spr(SP_CONV_G), spr(SP_LRU_G)

        def gate(r, carry):
            rows = pl.ds(r, 2 * s8)
            gc, gl = proj(rows, P_GC), proj(rows, P_GL)
            ybuf[rows, 0:d] = (yc[rows, :] * rcf[rows, :] * g_c * (gc * _sigmoid(gc))).astype(MM)
            ybuf[rows, d:2 * d] = (h_ref[rows, :] * rlf[rows, :] * g_l * (gl * _sigmoid(gl))).astype(MM)
            return carry

        _chunks(tb, 2 * s8, gate, 0)

        hres = x_ref[...] + _dot(ybuf[...], wout_ref[...])
        rf = lax.rsqrt(jnp.mean(hres * hres, axis=-1, keepdims=True) + RMS_EPS)
        hn = hres * rf
        fg = spr(SP_FINAL_G)
        err = hn * fg - tgt_ref[...]
        dout = err * (1.0 / d)
        acc_ref[0:SUBLANES, :] += (err * err).reshape(tb // SUBLANES, SUBLANES, d).sum(axis=0)
        acc_ref[SUBLANES:2 * SUBLANES, :] += (dout * hn).reshape(tb // SUBLANES, SUBLANES, d).sum(axis=0)
        gd = dout * fg
        dhres = rf * (gd - hn * jnp.mean(gd * hn, axis=-1, keepdims=True))
        dh_ref[...] = dhres
        dhb_ref[...] = dhres.astype(MM)

    vm = pl.BlockSpec(memory_space=pltpu.VMEM)
    blk = lambda w: pl.BlockSpec((tb, w), lambda i: (i, 0))
    buf = pltpu.VMEM((tb, d), F32)
    car = pltpu.VMEM((SUBLANES, d), F32)
    return pl.pallas_call(
        body, name="forward", grid=(nb,),
        in_specs=[blk(d), blk(d), blk(6 * d), vm, vm, vm, vm, vm, vm],
        out_specs=(blk(d), blk(d), blk(d), pl.BlockSpec((2 * SUBLANES, d), lambda i: (0, 0))) + (blk(d),) * 5,
        out_shape=(jax.ShapeDtypeStruct((t_len, d), F32),
                   jax.ShapeDtypeStruct((t_len, d), F32),
                   jax.ShapeDtypeStruct((t_len, d), MM),
                   jax.ShapeDtypeStruct((2 * SUBLANES, d), F32))
                  + (jax.ShapeDtypeStruct((t_len, d), F32),) * 5,
        scratch_shapes=[buf] * 2 + [pltpu.VMEM((tb, 2 * d), MM), car, car, car],
        compiler_params=_params(dimension_semantics=("arbitrary",)),
    )(x, tgt, p, wout, wa_t, wi_t, sp, ones_c, ones_l)


def _backward(p, h, dh, saved, facc, wout, wa_t, wi_t, sp, ones_c, ones_l, tb):
    t_len, d = h.shape
    nb = t_len // tb
    n_tiles, tw = wa_t.shape[0], wa_t.shape[1]
    hd_c, hd_l = d // N_CONV_HEADS, d // N_LRU_HEADS
    g_rows = 2 * n_tiles * hd_l
    s8 = SUBLANES

    def body(p_ref, h_ref, hhalo_ref, dh_ref, yc, czs, u, ra_ref, ii_ref, facc_ref,
             wout_ref, wa_ref, wi_ref, sp_ref, oc_ref, ol_ref,
             dp_ref, yt_ref, slab_v, slab_g,
             hh, dy, ybuf, rcf, rlf, qc, ql, dyc_hat, dyl_hat, dpa, dpi, du, gwa_ref, gwi_ref, acc_ref,
             car_dcz, car_a, car_g, car_du):
        i = pl.program_id(0)
        blk_idx = nb - 1 - i
        row = _row_iota(d)

        @pl.when(i == 0)
        def _():
            for ref in (car_dcz, car_a, car_g, car_du, gwa_ref, gwi_ref, acc_ref):
                ref[...] = jnp.zeros_like(ref)

        def spr(r):
            return sp_ref[r:r + 1, :]

        def proj(rows, seg):
            return p_ref[rows, seg * d:(seg + 1) * d].astype(F32)

        def put(rows, seg, halves):
            dp_ref[rows, seg * d:(seg + 1) * d] = jnp.concatenate(halves, axis=0).astype(MM)

        def acc_add(group, val):
            acc_ref[group * s8:(group + 1) * s8, :] += val

        live = jnp.where(blk_idx > 0, 1.0, 0.0).astype(F32)
        hh[0:s8, :] = hhalo_ref[...] * live
        hh[s8:, :] = h_ref[...]

        dy[...] = _dot_nt(dh_ref[...].astype(MM), wout_ref[...])

        w0, w1, w2 = spr(SP_CONV_W), spr(SP_CONV_W + 1), spr(SP_CONV_W + 2)
        l0, l1, l2, l3 = spr(SP_LRU_W), spr(SP_LRU_W + 1), spr(SP_LRU_W + 2), spr(SP_LRU_W + 3)

        rcf[...] = _head_rstd(yc[...], oc_ref[...], hd_c)
        rlf[...] = _head_rstd(h_ref[...], ol_ref[...], hd_l)

        g_c, g_l = spr(SP_CONV_G), spr(SP_LRU_G)

        def gates(r, carry):
            rows = pl.ds(r, 2 * s8)
            for (seg, off_y, src, rstd, gain, q, dhat, grp) in (
                    (P_GC, 0, yc, rcf, g_c, qc, dyc_hat, A_CONV_G),
                    (P_GL, d, h_ref, rlf, g_l, ql, dyl_hat, A_LRU_G)):
                gt = proj(rows, seg)
                sg = _sigmoid(gt)
                silu = gt * sg
                yhat = src[rows, :] * rstd[rows, :]
                nrm = yhat * gain
                ybuf[rows, off_y:off_y + d] = nrm * silu
                dout = dy[rows, off_y:off_y + d]
                dnrm = dout * silu
                dp_ref[rows, seg * d:(seg + 1) * d] = (dout * nrm * (sg * (1.0 + gt * (1.0 - sg)))).astype(MM)
                dg = dnrm * yhat
                acc_add(grp, dg[0:s8] + dg[s8:])
                dh_ = dnrm * gain
                dhat[rows, :] = dh_
                q[rows, :] = dh_ * yhat
            return carry

        _chunks(tb, 2 * s8, gates, 0)

        qc[...] = _head_sums(qc[...], oc_ref[...]) * (1.0 / hd_c)
        ql[...] = _head_sums(ql[...], ol_ref[...]) * (1.0 / hd_l)
        yt_ref[...] = ybuf[...].T.astype(MM)

        c8 = RG_LRU_C * _log_sigmoid(spr(SP_LAM))

        def conv_mixer(r, dcz_n):
            rows16 = pl.ds(r, 2 * s8)
            bg16, cg16, xc16 = proj(rows16, P_B), proj(rows16, P_C), proj(rows16, P_XC)
            z16 = cg16 * xc16
            d_b, d_c, d_x = [None, None], [None, None], [None, None]
            for j in (1, 0):
                rows, sub = pl.ds(r + j * s8, s8), slice(j * s8, (j + 1) * s8)
                rstd = rcf[rows, :]
                yhat = yc[rows, :] * rstd
                dyc = rstd * (dyc_hat[rows, :] - yhat * qc[rows, :])
                d_b[j] = dyc * czs[rows, :]
                dcz = dyc * bg16[sub]
                up1, up2 = _shift_up(dcz, dcz_n, 1, row), _shift_up(dcz, dcz_n, 2, row)
                dz = w2 * dcz + w1 * up1 + w0 * up2
                d_c[j] = dz * xc16[sub]
                d_x[j] = dz * cg16[sub]
                z = z16[sub]
                acc_add(A_CONV_W, up2 * z)
                acc_add(A_CONV_W + 1, up1 * z)
                acc_add(A_CONV_W + 2, dcz * z)
                dcz_n = dcz
            put(rows16, P_B, d_b)
            put(rows16, P_C, d_c)
            put(rows16, P_XC, d_x)
            return dcz_n

        car_dcz[...] = _chunks(tb, 2 * s8, conv_mixer, car_dcz[...], reverse=True)

        def lru_mixer(r, carry):
            a_n, g_n = carry
            for j in (1, 0):
                rows = pl.ds(r + j * s8, s8)
                rstd = rlf[rows, :]
                hcur = hh[pl.ds(r + (j + 1) * s8, s8), :]
                hhat = hcur * rstd
                dh_out = rstd * (dyl_hat[rows, :] - hhat * ql[rows, :])
                ra = ra_ref[rows, :]
                la = ra * c8
                a = jnp.exp(la)
                g = _scan_bwd(_shift_up(a, a_n, 1, row), dh_out, g_n, row)
                da = g * _shift_down(hcur, hh[pl.ds(r + j * s8, s8), :], 1, row)
                ii = ii_ref[rows, :]
                uu = u[rows, :]
                mult_sq = _lru_input_scale_sq(la, a)
                inv_mult = lax.rsqrt(mult_sq)
                dmult = g * (ii * uu)
                ds = g * (mult_sq * inv_mult)
                dla = a * (da - dmult * a * inv_mult)
                acc_add(A_LAM, dla * ra)
                dpa_ = dla * c8 * ra * (1.0 - ra)
                dpi_ = ds * uu * ii * (1.0 - ii)
                acc_add(A_B_A, dpa_)
                acc_add(A_B_I, dpi_)
                dpa[rows, :] = dpa_
                dpi[rows, :] = dpi_
                du[rows, :] = ds * ii
                a_n, g_n = a, _bcast_row(g, 0)
            return a_n, g_n

        a_f, g_f = _chunks(tb, 2 * s8, lru_mixer, (car_a[...], car_g[...]), reverse=True)
        car_a[...] = a_f
        car_g[...] = g_f

        dpab = dpa[...].astype(MM)
        dpib = dpi[...].astype(MM)
        for k in range(n_tiles):
            sl = slice(k * tw, (k + 1) * tw)
            du[:, sl] += _dot_nt(dpab[:, sl], wa_ref[k]) + _dot_nt(dpib[:, sl], wi_ref[k])
            ut = u[:, sl].T.astype(MM)
            gwa_ref[k] += _dot(ut, dpab[:, sl])
            gwi_ref[k] += _dot(ut, dpib[:, sl])

        def lru_conv(r, du_n):
            rows16 = pl.ds(r, 2 * s8)
            xl16 = proj(rows16, P_XL)
            d_xl = [None, None]
            for j in (1, 0):
                rows, sub = pl.ds(r + j * s8, s8), slice(j * s8, (j + 1) * s8)
                dut = du[rows, :]
                up1, up2, up3 = (_shift_up(dut, du_n, s, row) for s in (1, 2, 3))
                d_xl[j] = l3 * dut + l2 * up1 + l1 * up2 + l0 * up3
                xl = xl16[sub]
                acc_add(A_LRU_W, up3 * xl)
                acc_add(A_LRU_W + 1, up2 * xl)
                acc_add(A_LRU_W + 2, up1 * xl)
                acc_add(A_LRU_W + 3, dut * xl)
                acc_add(A_LRU_B, dut)
                du_n = dut
            put(rows16, P_XL, d_xl)
            return du_n

        car_du[...] = _chunks(tb, 2 * s8, lru_conv, car_du[...], reverse=True)

        @pl.when(i == nb - 1)
        def _():
            def rowsum(ref, group):
                return jnp.sum(ref[group * s8:(group + 1) * s8, :], axis=0, keepdims=True)

            slab_v[...] = jnp.zeros_like(slab_v)
            loss = jnp.sum(rowsum(facc_ref, 0), axis=1, keepdims=True) * (0.5 / d)
            rows = {SL_LOSS: jnp.broadcast_to(loss, (1, d)), SL_FINAL_G: rowsum(facc_ref, 1),
                    SL_LRU_B: rowsum(acc_ref, A_LRU_B), SL_B_A: rowsum(acc_ref, A_B_A), SL_B_I: rowsum(acc_ref, A_B_I),
                    SL_LAM: rowsum(acc_ref, A_LAM), SL_CONV_G: rowsum(acc_ref, A_CONV_G), SL_LRU_G: rowsum(acc_ref, A_LRU_G)}
            for k in range(3):
                rows[SL_CONV_W + k] = rowsum(acc_ref, A_CONV_W + k)
            for k in range(4):
                rows[SL_LRU_W + k] = rowsum(acc_ref, A_LRU_W + k)
            for r, val in rows.items():
                slab_v[r:r + 1, :] = val
            head_of_lane = lax.broadcasted_iota(jnp.int32, (hd_l, tw), 1) // hd_l
            for mtx, g_ref in enumerate((gwa_ref, gwi_ref)):
                for k in range(n_tiles):
                    packed = jnp.zeros((hd_l, tw), F32)
                    for a in range(tw // hd_l):
                        packed = jnp.where(head_of_lane == a, g_ref[k, a * hd_l:(a + 1) * hd_l, :], packed)
                    slab_g[(mtx * n_tiles + k) * hd_l:(mtx * n_tiles + k + 1) * hd_l, :] = packed.astype(MM)

    vm = pl.BlockSpec(memory_space=pltpu.VMEM)
    rev = lambda w: pl.BlockSpec((tb, w), lambda i: (nb - 1 - i, 0))
    halo = lambda rows, w: pl.BlockSpec((rows, w), lambda i: (jnp.maximum((nb - 1 - i) * (tb // rows) - 1, 0), 0))
    const = lambda shape: pl.BlockSpec(shape, lambda i: (0,) * len(shape))
    buf = lambda w: pltpu.VMEM((tb, w), F32)
    car = pltpu.VMEM((SUBLANES, d), F32)
    return pl.pallas_call(
        body, name="backward", grid=(nb,),
        in_specs=[rev(6 * d), rev(d), halo(SUBLANES, d), rev(d)] + [rev(d)] * 5 + [vm, vm, vm, vm, vm, vm, vm],
        out_specs=(rev(6 * d), pl.BlockSpec((2 * d, tb), lambda i: (0, nb - 1 - i)),
                   const((SL_ROWS, d)), const((g_rows, tw))),
        out_shape=(jax.ShapeDtypeStruct((t_len, 6 * d), MM),
                   jax.ShapeDtypeStruct((2 * d, t_len), MM),
                   jax.ShapeDtypeStruct((SL_ROWS, d), F32),
                   jax.ShapeDtypeStruct((g_rows, tw), MM)),
        scratch_shapes=[pltpu.VMEM((SUBLANES + tb, d), F32), buf(2 * d), buf(2 * d)] + [buf(d)] * 9
                       + [pltpu.VMEM((n_tiles, tw, tw), F32), pltpu.VMEM((n_tiles, tw, tw), F32),
                          pltpu.VMEM((A_GROUPS * SUBLANES, d), F32), car, car, car, car],
        compiler_params=_params(dimension_semantics=("arbitrary",)),
    )(p, h, h, dh, *saved, facc, wout, wa_t, wi_t, sp, ones_c, ones_l)


def _input_grad(dp, win_all, x, dh, sp, part, passed, tb):
    t_len, d = x.shape
    nb = t_len // tb
    cols = win_all.shape[2]
    rc = 32

    def body(dp_ref, win_ref, x_ref, dh_ref, sp_ref, part_ref, passed_ref, gx_ref, ln_ref, direct, relayed,
             send_sems, recv_sems, local_sems, acc_ref, ln_all, ln_send, ln_recv, mine, theirs):
        i = pl.program_id(0)
        x_, y_, c_ = _mesh_pos()
        first, second = 1 - c_, c_
        nbr1 = (x_ ^ c_, y_ ^ (1 - c_), c_)
        nbr2 = (x_ ^ (1 - c_), y_ ^ c_, c_)

        def remote(src, dst, k, to):
            return pltpu.make_async_remote_copy(src_ref=src, dst_ref=dst, send_sem=send_sems.at[k], recv_sem=recv_sems.at[k],
                                                device_id=to, device_id_type=MESH)

        to_first = remote(part_ref.at[first], direct, 0, nbr1)
        to_second = remote(theirs, relayed, 1, nbr2)
        loads = [pltpu.make_async_copy(part_ref.at[second], mine, local_sems.at[0]),
                 pltpu.make_async_copy(passed_ref, theirs, local_sems.at[1])]

        @pl.when(i == 0)
        def _():
            acc_ref[...] = jnp.zeros_like(acc_ref)
            to_first.start()
            for cp in loads:
                cp.start()

        dxn = _dot_nt(dp_ref[:, 0:cols], win_ref[0])
        for j in range(1, N_DEV):
            dxn += _dot_nt(dp_ref[:, j * cols:(j + 1) * cols], win_ref[j])
        xv = x_ref[...]
        r0 = lax.rsqrt(jnp.mean(xv * xv, axis=-1, keepdims=True) + RMS_EPS)
        xhat = xv * r0
        acc_ref[...] += (dxn * xhat).reshape(tb // SUBLANES, SUBLANES, d).sum(axis=0)
        dxh = dxn * sp_ref[SP_LN_G:SP_LN_G + 1, :]
        gx_ref[...] = dh_ref[...] + r0 * (dxh - xhat * jnp.mean(dxh * xhat, axis=-1, keepdims=True))

        @pl.when(i == 0)
        def _():
            for cp in loads:
                cp.wait()

            def add(r, carry):
                rows = pl.ds(r, rc)
                theirs[rows, :] = (mine[rows, :].astype(F32) + theirs[rows, :].astype(F32)).astype(MM)
                return carry

            _chunks(mine.shape[0], rc, add, 0)
            to_second.start()

        @pl.when(i == nb - 1)
        def _():
            for cp in (to_first, to_second):
                cp.wait_recv()
                cp.wait_send()
            ln_all[4 * x_ + 2 * y_ + c_] = jnp.broadcast_to(jnp.sum(acc_ref[...], axis=0, keepdims=True), acc_ref.shape)
            gather = _Gather(lambda a, px, py, pc: ln_all.at[4 * px + 2 * py + pc], ln_send, ln_recv)
            gather.start_own(0)
            gather.finish(0)
            total = ln_all[0]
            for dev in range(1, N_DEV):
                total = total + ln_all[dev]
            ln_ref[...] = total

    vm = pl.BlockSpec(memory_space=pltpu.VMEM)
    hbm = pl.BlockSpec(memory_space=pl.ANY)
    blk = lambda w: pl.BlockSpec((tb, w), lambda i: (i, 0))
    landed = jax.ShapeDtypeStruct(part.shape[1:], part.dtype)
    outs = pl.pallas_call(
        body, name="input_grad", grid=(nb,),
        in_specs=[blk(6 * d), vm, blk(d), blk(d), vm, hbm, hbm],
        out_specs=(blk(d), pl.BlockSpec((SUBLANES, d), lambda i: (0, 0)), hbm, hbm),
        out_shape=(jax.ShapeDtypeStruct((t_len, d), F32), jax.ShapeDtypeStruct((SUBLANES, d), F32), landed, landed),
        scratch_shapes=[pltpu.SemaphoreType.DMA((2,)), pltpu.SemaphoreType.DMA((2,)), pltpu.SemaphoreType.DMA((2,)),
                        pltpu.VMEM((SUBLANES, d), F32), pltpu.VMEM((N_DEV, SUBLANES, d), F32),
                        pltpu.SemaphoreType.DMA((7,)), pltpu.SemaphoreType.DMA((7,)),
                        pltpu.VMEM(part.shape[1:], MM), pltpu.VMEM(part.shape[1:], MM)],
        compiler_params=_params(dimension_semantics=("arbitrary",)),
    )(dp, win_all, x, dh, sp, part, passed)
    return outs[0], outs[1], (outs[2], outs[3])


_CHIP_RELATIONS = [(0, 0), (1, 0), (0, 1), (1, 1)]


def _related_block(k, core):
    x, y, _ = _mesh_pos()
    fx, fy = _CHIP_RELATIONS[k]
    return 4 * (x ^ fx) + 2 * (y ^ fy) + core


class _ChipExchange:
    def __init__(self, part_refs, land_refs, send_sems, recv_sems):
        self.part_refs, self.land_refs, self.send_sems, self.recv_sems = part_refs, land_refs, send_sems, recv_sems

    def copies(self):
        x, y, c = _mesh_pos()
        for a in range(len(self.part_refs)):
            for k in (1, 2, 3):
                fx, fy = _CHIP_RELATIONS[k]
                yield pltpu.make_async_remote_copy(
                    src_ref=self.part_refs[a].at[k - 1], dst_ref=self.land_refs[a].at[k - 1],
                    send_sem=self.send_sems.at[3 * a + k - 1], recv_sem=self.recv_sems.at[3 * a + k - 1],
                    device_id=(x ^ fx, y ^ fy, c), device_id_type=MESH)

    def start(self):
        for cp in self.copies():
            cp.start()

    def finish(self):
        for cp in self.copies():
            cp.wait_recv()
        for cp in self.copies():
            cp.wait_send()


_BLOCK_ORDER = (3, 1, 2, 0)


def _weight_grad_stage1(name, blk_shape, n_split, operands, in_specs, product, riders=(), slabs=(), pass_on=False):
    n_rows, n_cols = blk_shape
    rs = n_rows // n_split
    rc = 32
    n_in, n_ride, n_slab, n_pass = len(operands), len(riders), len(slabs), int(pass_on)
    _, _, c = _mesh_pos()
    order = jnp.stack([_related_block(k, 1 - c) for k in _BLOCK_ORDER] + [_related_block(k, c) for k in _BLOCK_ORDER]
                      + [jnp.int32(k - 1) for k in _BLOCK_ORDER[:3]]).astype(jnp.int32)

    def body(order_ref, *refs):
        ins = refs[:n_in]
        ride_in = refs[n_in:n_in + n_ride]
        slab_in = refs[n_in + n_ride:n_in + n_ride + n_slab]
        n_op = n_in + n_ride + n_slab
        part_ref, own_ref = refs[n_op:n_op + 2]
        ride_out = refs[n_op + 2:n_op + 2 + n_ride]
        gathered = refs[n_op + 2 + n_ride:n_op + 2 + n_ride + n_slab]
        (gbuf, sendbuf, from_sib, send_sems, recv_sems, ride_send, ride_recv,
         slab_send, slab_recv, slab_local) = refs[n_op + 2 + n_ride + n_slab + n_pass:][:10]
        exchange = _ChipExchange(ride_in, ride_out, ride_send, ride_recv)
        s = pl.program_id(0)
        x, y, c = _mesh_pos()
        me = 4 * x + 2 * y + c
        gather = _BalancedGather(lambda a, px, py, pc: gathered[a].at[4 * px + 2 * py + pc], slab_send, slab_recv, slab_in)
        keep_own = [pltpu.make_async_copy(slab_in[a], gathered[a].at[me], slab_local.at[a]) for a in range(n_slab)]
        if pass_on:
            passed_ref = refs[n_op + 2 + n_ride + n_slab]
            pass_buf, pass_sems = refs[-2:]
            to_first = pltpu.make_async_remote_copy(
                src_ref=pass_buf, dst_ref=passed_ref, send_sem=pass_sems.at[0], recv_sem=pass_sems.at[1],
                device_id=(x ^ c, y ^ (1 - c), c), device_id_type=MESH)

        def to_sibling(k):
            return pltpu.make_async_remote_copy(
                src_ref=sendbuf.at[k], dst_ref=from_sib.at[k], send_sem=send_sems.at[k], recv_sem=recv_sems.at[k],
                device_id=(x, y, 1 - c), device_id_type=MESH)

        @pl.when(s == 0)
        def _():
            exchange.start()
            for a in range(n_slab):
                gather.start_own(a)
                keep_own[a].start()

        @pl.when(s == 5)
        def _():
            for a in range(n_slab):
                gather.on_neighbour(a, 0)
                gather.on_neighbour(a, 1)

        @pl.when(s == 7)
        def _():
            for a in range(n_slab):
                gather.on_diagonal(a)

        for h in range(n_split):
            gbuf[h * rs:(h + 1) * rs, :] = product(ins, h)

        @pl.when(s < 4)
        def _():
            def narrow(r, carry):
                sendbuf[s, pl.ds(r, rc), :] = gbuf[pl.ds(r, rc), :].astype(MM)
                return carry

            _chunks(n_rows, rc, narrow, 0)
            to_sibling(s).start()

        @pl.when(s >= 4)
        def _():
            k = s - 4
            to_sibling(k).wait_recv()

            @pl.when(s < 7)
            def _():
                def add(r, carry):
                    rows = pl.ds(r, rc)
                    part_ref[0, rows, :] = (gbuf[rows, :] + from_sib[k, rows, :].astype(F32)).astype(MM)
                    return carry

                _chunks(n_rows, rc, add, 0)

            if pass_on:
                @pl.when(s == 4 + _BLOCK_ORDER.index(3))
                def _():
                    def stage(r, carry):
                        pass_buf[pl.ds(r, rc), :] = part_ref[0, pl.ds(r, rc), :]
                        return carry

                    _chunks(n_rows, rc, stage, 0)
                    to_first.start()

            @pl.when(s == 7)
            def _():
                def add(r, carry):
                    rows = pl.ds(r, rc)
                    own_ref[rows, :] = gbuf[rows, :] + from_sib[3, rows, :].astype(F32)
                    return carry

                _chunks(n_rows, rc, add, 0)
                for kk in range(4):
                    to_sibling(kk).wait_send()
                exchange.finish()
                if pass_on:
                    to_first.wait_recv()
                    to_first.wait_send()
                for a in range(n_slab):
                    gather.wait_sibling(a)
                    for j in range(3):
                        gather.wait_passed_on(a, j)
                    gather.wait_sends(a)
                    keep_own[a].wait()

    hbm = pl.BlockSpec(memory_space=pl.ANY)
    grid_spec = pltpu.PrefetchScalarGridSpec(
        num_scalar_prefetch=1, grid=(N_DEV,), in_specs=list(in_specs) + [hbm] * (n_ride + n_slab),
        out_specs=(pl.BlockSpec((1, n_rows, n_cols), lambda s, o: (o[N_DEV + jnp.clip(s - 4, 0, 2)], 0, 0)),
                   pl.BlockSpec((n_rows, n_cols), lambda s, o: (0, 0))) + (hbm,) * (n_ride + n_slab + n_pass),
        scratch_shapes=[pltpu.VMEM((n_rows, n_cols), F32), pltpu.VMEM((4, n_rows, n_cols), MM),
                        pltpu.VMEM((4, n_rows, n_cols), MM),
                        pltpu.SemaphoreType.DMA((4,)), pltpu.SemaphoreType.DMA((4,)),
                        pltpu.SemaphoreType.DMA((max(3 * n_ride, 1),)), pltpu.SemaphoreType.DMA((max(3 * n_ride, 1),)),
                        pltpu.SemaphoreType.DMA((max(8 * n_slab, 1),)), pltpu.SemaphoreType.DMA((max(8 * n_slab, 1),)),
                        pltpu.SemaphoreType.DMA((max(n_slab, 1),))]
                       + [pltpu.VMEM((n_rows, n_cols), MM), pltpu.SemaphoreType.DMA((2,))] * n_pass)
    outs = pl.pallas_call(
        body, name=name, grid_spec=grid_spec,
        out_shape=(jax.ShapeDtypeStruct((3, n_rows, n_cols), MM), jax.ShapeDtypeStruct((n_rows, n_cols), F32))
                  + tuple(jax.ShapeDtypeStruct(p.shape, p.dtype) for p in riders)
                  + tuple(jax.ShapeDtypeStruct((N_DEV,) + a.shape, a.dtype) for a in slabs)
                  + (jax.ShapeDtypeStruct((n_rows, n_cols), MM),) * n_pass,
        compiler_params=_params(dimension_semantics=("arbitrary",)),
    )(order, *operands, *riders, *slabs)
    return outs[0], outs[1], outs[2:2 + n_ride], outs[2 + n_ride:2 + n_ride + n_slab], outs[2 + n_ride + n_slab:]


def _weight_grad_in(xnt, dp, riders, slabs):
    d, t_len = xnt.shape
    cols = dp.shape[1] // N_DEV
    half = d // 2
    return _weight_grad_stage1(
        "weight_grad_in", (d, cols), 2, (xnt, dp),
        [pl.BlockSpec(memory_space=pltpu.VMEM), pl.BlockSpec((t_len, cols), lambda s, o: (0, o[s]))],
        lambda refs, h: _dot(refs[0][h * half:(h + 1) * half, :], refs[1][...]), riders, slabs, pass_on=True)


def _weight_grad_out(yt, dhb):
    d2, t_len = yt.shape
    d = dhb.shape[1]
    rows = d2 // N_DEV
    return _weight_grad_stage1(
        "weight_grad_out", (rows, d), 1, (yt, dhb),
        [pl.BlockSpec((rows, t_len), lambda s, o: (o[s], 0)), pl.BlockSpec(memory_space=pltpu.VMEM)],
        lambda refs, h: _dot(refs[0][...], refs[1][...]))


def _update_shard(own, others, w, m, v, name):
    n_rows, n_cols = w.shape
    rb = min(256, n_rows)
    n_other = len(others)

    def body(own_ref, *refs):
        other_refs = refs[:n_other]
        w_ref, m_ref, v_ref, grad_ref, delta_ref, mo_ref, vo_ref = refs[n_other:]
        g = own_ref[...]
        for ref in other_refs:
            for k in range(ref.shape[0] if len(ref.shape) == 3 else 1):
                g = g + (ref[k] if len(ref.shape) == 3 else ref[...]).astype(F32)
        delta, m_new, v_new = _adamw(w_ref[...], g, m_ref[...], v_ref[...])
        grad_ref[...] = g
        delta_ref[...] = delta
        mo_ref[...] = m_new
        vo_ref[...] = v_new

    blk = pl.BlockSpec((rb, n_cols), lambda i: (i, 0))
    stacked = lambda n: pl.BlockSpec((n, rb, n_cols), lambda i: (0, i, 0))
    out = jax.ShapeDtypeStruct((n_rows, n_cols), F32)
    return pl.pallas_call(
        body, name=name, grid=(n_rows // rb,),
        in_specs=[blk] + [stacked(o.shape[0]) if o.ndim == 3 else blk for o in others] + [blk, blk, blk],
        out_specs=(blk, blk, blk, blk), out_shape=(out, out, out, out),
        compiler_params=_params(dimension_semantics=("arbitrary",)),
    )(own, *others, w, m, v)


def _small_update(gat_v, gat_g, ln_tot, vec_w, vec_m, vec_v, gates, convs):
    n_vec = len(vec_w)
    n_heads, hd, _ = gates[0].shape
    tw = gat_g.shape[2]
    s8 = SUBLANES
    per = tw // hd
    n_tiles = n_heads // per
    cc = convs[0].shape[1]
    n_in = 3 + 3 * n_vec + 12

    def body(*refs):
        gv_ref, gg_ref, ln_ref = refs[:3]
        w_refs, m_refs, v_refs = (refs[3 + j * n_vec:3 + (j + 1) * n_vec] for j in range(3))
        gate_refs = refs[3 + 3 * n_vec:3 + 3 * n_vec + 6]
        conv_refs = refs[3 + 3 * n_vec + 6:n_in]
        loss_o = refs[n_in]
        kinds = [refs[n_in + 1 + j * (n_vec + 4):n_in + 1 + (j + 1) * (n_vec + 4)] for j in range(4)]
        tv, tg = refs[n_in + 1 + 4 * (n_vec + 4):]
        x, y, c = _mesh_pos()
        me = 4 * x + 2 * y + c

        def emit(k_out, w, g, m, v):
            delta, m_new, v_new = _adamw(w, g, m, v)
            for ref, val in zip(k_out, (g, delta, m_new, v_new)):
                ref[...] = val

        total = gv_ref[0]
        for dev in range(1, N_DEV):
            total = total + gv_ref[dev]
        tv[...] = total
        tv[SL_LN_G:SL_LN_G + 1, :] = ln_ref[0:1, :]

        def sum_gates(r, carry):
            rows = pl.ds(r, 2 * s8)
            part = gg_ref[0, rows, :].astype(F32)
            for dev in range(1, N_DEV):
                part = part + gg_ref[dev, rows, :].astype(F32)
            tg[rows, :] = part
            return carry

        _chunks(tg.shape[0], 2 * s8, sum_gates, 0)
        loss_o[...] = jnp.broadcast_to(tv[SL_LOSS:SL_LOSS + 1, 0:LANES], loss_o.shape)
        for p in range(n_vec):
            w, g = w_refs[p][...], tv[SL_LN_G + p, :]
            if SL_LN_G + p == SL_LAM:
                g = g * (RG_LRU_C * jax.nn.sigmoid(-w))
            emit([k_out[p] for k_out in kinds], w, g, m_refs[p][...], v_refs[p][...])
        lanes = pl.ds(pl.multiple_of(me * cc, cc), cc)
        for j, (row0, n) in enumerate(((SL_CONV_W, 3), (SL_LRU_W, 4))):
            w_ref, m_ref, v_ref = conv_refs[3 * j:3 * j + 3]
            emit([k_out[n_vec + 2 + j] for k_out in kinds], w_ref[...], tv[row0:row0 + n, lanes], m_ref[...], v_ref[...])
        for mtx in range(2):
            w_ref, m_ref, v_ref = gate_refs[3 * mtx:3 * mtx + 3]
            for k in range(n_tiles):
                tile = tg[(mtx * n_tiles + k) * hd:(mtx * n_tiles + k + 1) * hd, :]
                for a in range(per):
                    head = k * per + a
                    g = tile[:, a * hd:(a + 1) * hd]
                    delta, m_new, v_new = _adamw(w_ref[head], g, m_ref[head], v_ref[head])
                    for k_out, val in zip(kinds, (g, delta, m_new, v_new)):
                        k_out[n_vec + mtx][head] = val

    vm = pl.BlockSpec(memory_space=pltpu.VMEM)
    like = lambda a: jax.ShapeDtypeStruct(a.shape, F32)
    per_kind = tuple(like(a) for a in vec_w) + (like(gates[0]), like(gates[3]), like(convs[0]), like(convs[3]))
    n_out = 1 + 4 * len(per_kind)
    outs = pl.pallas_call(
        body, name="small_update",
        in_specs=[vm] * n_in, out_specs=(vm,) * n_out,
        out_shape=(jax.ShapeDtypeStruct((SUBLANES, LANES), F32),) + per_kind * 4,
        scratch_shapes=[pltpu.VMEM(gat_v.shape[1:], F32), pltpu.VMEM(gat_g.shape[1:], F32)],
        compiler_params=_params(),
    )(gat_v, gat_g, ln_tot, *vec_w, *vec_m, *vec_v, *gates, *convs)
    return outs[0], [outs[1 + j * len(per_kind):1 + (j + 1) * len(per_kind)] for j in range(4)]


def _head_ones(head_dim, tw):
    lane = jnp.arange(tw) // head_dim
    return (lane[:, None] == lane[None, :]).astype(MM)


def kernel(x, ln_g, w_in, conv_w, lru_conv_w, lru_conv_b, w_a, b_a, w_i, b_i, lam, conv_out_g, lru_out_g, w_out, final_g, loss_target, m_ln_g, m_w_in, m_conv_w, m_lru_conv_w, m_lru_conv_b, m_w_a, m_b_a, m_w_i, m_b_i, m_lam, m_conv_out_g, m_lru_out_g, m_w_out, m_final_g, v_ln_g, v_w_in, v_conv_w, v_lru_conv_w, v_lru_conv_b, v_w_a, v_b_a, v_w_i, v_b_i, v_lam, v_conv_out_g, v_lru_out_g, v_w_out, v_final_g):
    _, t_len, d = x.shape
    hd_l = d // N_LRU_HEADS
    tw = min(MXU_TILE, d)
    x2, tgt2 = x[0], loss_target[0]

    small = [ln_g, lru_conv_b, b_a, b_i, lam, conv_out_g, lru_out_g, final_g]
    p, xnt, win_all, wout_all, _, sp, wa_t, wi_t = _gather_project(
        x2, w_in, w_out, conv_w, lru_conv_w, ln_g.reshape(1, d), w_a, w_i, small, min(256, t_len), tw)
    wout_full = wout_all.reshape(N_DEV * w_out.shape[0], d)
    ones_c, ones_l = _head_ones(d // N_CONV_HEADS, tw), _head_ones(hd_l, tw)

    h, dh, dhb, facc, *saved = _forward(x2, tgt2, p, wout_full, wa_t, wi_t, sp, ones_c, ones_l, min(256, t_len))
    dp, yt, slab_v, slab_g = _backward(p, h, dh, saved, facc, wout_full, wa_t, wi_t, sp, ones_c, ones_l, min(256, t_len))
    part_out, own_out, _, _, _ = _weight_grad_out(yt, dhb)
    part_in, own_in, (chips_out,), (gat_v, gat_g), (passed,) = _weight_grad_in(xnt, dp, (part_out,), (slab_v, slab_g))
    grad_x, ln_tot, sums_in = _input_grad(dp, win_all, x2, dh, sp, part_in, passed, min(512, t_len))
    gw_in, dw_in, mw_in, vw_in = _update_shard(own_in, sums_in, w_in, m_w_in, v_w_in, "update_w_in")
    gw_out, dw_out, mw_out, vw_out = _update_shard(own_out, (chips_out,), w_out, m_w_out, v_w_out, "update_w_out")

    loss_tile, kinds = _small_update(
        gat_v, gat_g, ln_tot, small,
        [m_ln_g, m_lru_conv_b, m_b_a, m_b_i, m_lam, m_conv_out_g, m_lru_out_g, m_final_g],
        [v_ln_g, v_lru_conv_b, v_b_a, v_b_i, v_lam, v_conv_out_g, v_lru_out_g, v_final_g],
        (w_a, m_w_a, v_w_a, w_i, m_w_i, v_w_i), (conv_w, m_conv_w, v_conv_w, lru_conv_w, m_lru_conv_w, v_lru_conv_w))

    def unpack(kind, big_in, big_out):
        vec, (wa_, wi_, cw_, lw_) = kind[:len(small)], kind[len(small):]
        return [vec[0], big_in, cw_, lw_, vec[1], wa_, vec[2], wi_, vec[3], vec[4], vec[5], vec[6], big_out, vec[7]]

    return (loss_tile[0, 0], grad_x[None], *unpack(kinds[0], gw_in, gw_out), *unpack(kinds[1], dw_in, dw_out),
            *unpack(kinds[2], mw_in, mw_out), *unpack(kinds[3], vw_in, vw_out))
```

```python
import jax
import jax.numpy as jnp
from jax import lax
from jax.experimental import pallas as pl
from jax.experimental.pallas import tpu as pltpu

F32 = jnp.float32
MM = jnp.bfloat16
MESH = pl.DeviceIdType.MESH

N_DEV = 8
N_CONV_HEADS = 8
N_LRU_HEADS = 16
RG_LRU_C = 8.0
RMS_EPS = 1e-6
ADAM_LR, ADAM_B1, ADAM_B2, ADAM_EPS, ADAM_WD, ADAM_STEP = 0.001, 0.9, 0.999, 1e-08, 0.01, 10
ADAM_BC1 = 1.0 - ADAM_B1 ** ADAM_STEP
ADAM_BC2 = 1.0 - ADAM_B2 ** ADAM_STEP

SUBLANES = 8
LANES = 128
MXU_TILE = 256
VMEM_LIMIT = 56 * 1024 * 1024

SP_LN_G, SP_LRU_B, SP_B_A, SP_B_I, SP_LAM, SP_CONV_G, SP_LRU_G, SP_FINAL_G, SP_CONV_W, SP_LRU_W = 0, 1, 2, 3, 4, 5, 6, 7, 8, 11
SP_ROWS = 16
P_B, P_C, P_XC, P_GC, P_XL, P_GL = 0, 1, 2, 3, 4, 5
A_CONV_G, A_LRU_G, A_LAM, A_B_A, A_B_I, A_CONV_W, A_LRU_W, A_LRU_B = 0, 1, 2, 3, 4, 5, 8, 12
A_GROUPS = 13
SL_LOSS, SL_LN_G, SL_LRU_B, SL_B_A, SL_B_I, SL_LAM, SL_CONV_G, SL_LRU_G, SL_FINAL_G, SL_CONV_W, SL_LRU_W = 0, 1, 2, 3, 4, 5, 6, 7, 8, 16, 24
SL_ROWS = 32


def _params(vmem=True, **kw):
    if vmem:
        kw["vmem_limit_bytes"] = VMEM_LIMIT
    return pltpu.CompilerParams(**kw)


def _dot(a, b):
    return jnp.dot(a, b, preferred_element_type=F32)


def _dot_nt(a, b):
    return lax.dot_general(a, b, (((1,), (1,)), ((), ())), preferred_element_type=F32)


def _head_sums(v, ones_tile):
    tw = ones_tile.shape[0]
    vb = v.astype(MM)
    return jnp.concatenate([_dot(vb[:, k:k + tw], ones_tile) for k in range(0, v.shape[1], tw)], axis=1)


def _head_rstd(v, ones_tile, head_dim):
    return lax.rsqrt(_head_sums(v * v, ones_tile) * (1.0 / head_dim) + RMS_EPS)


def _sigmoid(x):
    return 0.5 * jnp.tanh(0.5 * x) + 0.5


def _lru_input_scale_sq(log_a, a):
    return -jnp.tanh(log_a) * (1.0 + a * a)


def _log_sigmoid(x):
    z = jnp.exp(-jnp.abs(x))
    u = 1.0 + z
    log1p_z = jnp.where(u == 1.0, z, jnp.log(u) * (z / (u - 1.0)))
    return jnp.minimum(x, 0.0) - log1p_z


def _row_iota(d):
    return lax.broadcasted_iota(jnp.int32, (SUBLANES, d), 0)


def _shift_down(cur, prev, s, row):
    return jnp.where(row >= s, pltpu.roll(cur, s, axis=0), pltpu.roll(prev, s, axis=0))


def _shift_up(cur, nxt, s, row):
    k = SUBLANES - s
    return jnp.where(row < k, pltpu.roll(cur, k, axis=0), pltpu.roll(nxt, k, axis=0))


def _scan_fwd(a, b, h_prev, row):
    for s in (1, 2, 4):
        a_s = jnp.where(row >= s, pltpu.roll(a, s, axis=0), 1.0)
        b_s = jnp.where(row >= s, pltpu.roll(b, s, axis=0), 0.0)
        b = a * b_s + b
        a = a * a_s
    return a * h_prev + b


def _scan_bwd(a_next, b, g_next, row):
    a = a_next
    for s in (1, 2, 4):
        k = SUBLANES - s
        a_s = jnp.where(row < k, pltpu.roll(a, k, axis=0), 1.0)
        b_s = jnp.where(row < k, pltpu.roll(b, k, axis=0), 0.0)
        b = a * b_s + b
        a = a * a_s
    return a * g_next + b


def _bcast_row(v, r):
    return jnp.broadcast_to(v[r:r + 1, :], v.shape)


def _chunks(n_rows, rc, body, init, reverse=False):
    n = n_rows // rc

    def step(i, carry):
        j = (n - 1 - i) if reverse else i
        return body(pl.multiple_of(j * rc, rc), carry)

    return lax.fori_loop(0, n, step, init)


def _adamw(w, g, m, v):
    m = ADAM_B1 * m + (1.0 - ADAM_B1) * g
    v = ADAM_B2 * v + (1.0 - ADAM_B2) * (g * g)
    m_hat = m / ADAM_BC1
    v_hat = v / ADAM_BC2
    delta = -ADAM_LR * (m_hat / (jnp.sqrt(v_hat) + ADAM_EPS) + ADAM_WD * w)
    return delta, m, v


def _mesh_pos():
    return lax.axis_index("x"), lax.axis_index("y"), lax.axis_index("c")


class _Gather:
    def __init__(self, blocks_of, send_sems, recv_sems, own_src=None):
        x, y, c = _mesh_pos()
        self.c = c
        self.me, self.sibling = (x, y, c), (x, y, 1 - c)
        self.chips = [(1 - x, y), (x, 1 - y), (1 - x, 1 - y)]
        self.blocks_of, self.send_sems, self.recv_sems = blocks_of, send_sems, recv_sems
        self.own_src = own_src

    def copy(self, a, k, block, to):
        src = self.blocks_of(a, *block)
        if block is self.me and self.own_src is not None:
            src = self.own_src[a]
        return pltpu.make_async_remote_copy(
            src_ref=src, dst_ref=self.blocks_of(a, *block),
            send_sem=self.send_sems.at[a * 7 + k], recv_sem=self.recv_sems.at[a * 7 + k],
            device_id=to, device_id_type=MESH)

    def start_own(self, a):
        self.copy(a, 0, self.me, self.sibling).start()
        for j, chip in enumerate(self.chips):
            self.copy(a, 1 + j, self.me, (*chip, self.c)).start()

    def wait_sibling(self, a):
        self.copy(a, 0, self.sibling, self.me).wait_recv()

    def wait_chip_and_pass_on(self, a, j):
        block = (*self.chips[j], self.c)
        self.copy(a, 1 + j, block, self.me).wait_recv()
        self.copy(a, 4 + j, block, self.sibling).start()

    def wait_passed_on(self, a, j):
        self.copy(a, 4 + j, (*self.chips[j], 1 - self.c), self.me).wait_recv()

    def wait_sends(self, a):
        self.copy(a, 0, self.me, self.sibling).wait_send()
        for j, chip in enumerate(self.chips):
            self.copy(a, 1 + j, self.me, (*chip, self.c)).wait_send()
            self.copy(a, 4 + j, (*chip, self.c), self.sibling).wait_send()

    def finish(self, a):
        for j in range(3):
            self.wait_chip_and_pass_on(a, j)
        self.wait_sibling(a)
        for j in range(3):
            self.wait_passed_on(a, j)
        self.wait_sends(a)


class _BalancedGather:
    def __init__(self, slot, send_sems, recv_sems, own_src):
        x, y, c = _mesh_pos()
        self.c = c
        self.me, self.sibling = (x, y, c), (x, y, 1 - c)
        self.chips = [(1 - x, y), (x, 1 - y), (1 - x, 1 - y)]
        self.slot, self.send_sems, self.recv_sems, self.own_src = slot, send_sems, recv_sems, own_src

    def half(self, a, block, which):
        ref = self.slot(a, *block)
        n = ref.shape[0] // 2
        return ref.at[pl.ds(which * n, n)]

    def copy(self, a, k, src, dst, to):
        return pltpu.make_async_remote_copy(
            src_ref=src, dst_ref=dst, send_sem=self.send_sems.at[a * 8 + k], recv_sem=self.recv_sems.at[a * 8 + k],
            device_id=to, device_id_type=MESH)

    def whole(self, a, k, block, to):
        src = self.own_src[a] if block is self.me else self.slot(a, *block)
        return self.copy(a, k, src, self.slot(a, *block), to)

    def halved(self, a, k, block, which, to):
        return self.copy(a, k, self.half(a, block, which), self.half(a, block, which), to)

    def on(self, chip):
        return (*self.chips[chip], self.c)

    def start_own(self, a):
        self.whole(a, 0, self.me, self.sibling).start()
        self.whole(a, 1, self.me, self.on(0)).start()
        self.whole(a, 2, self.me, self.on(1)).start()

    def start_own_staggered(self, a):
        self.whole(a, 0, self.me, self.sibling).start()
        for core, order in ((1, (0, 1)), (0, (1, 0))):
            @pl.when(self.c == core)
            def _(order=order):
                for j in order:
                    self.whole(a, 1 + j, self.me, self.on(j)).start()

    def wait_sibling(self, a):
        self.whole(a, 0, self.sibling, self.me).wait_recv()

    def on_neighbour(self, a, j):
        self.whole(a, 1 + j, self.on(j), self.me).wait_recv()
        self.halved(a, 3 + j, self.on(j), j, self.on(1 - j)).start()
        self.whole(a, 5 + j, self.on(j), self.sibling).start()

    def on_diagonal(self, a):
        self.halved(a, 3, self.on(2), 0, self.me).wait_recv()
        self.halved(a, 4, self.on(2), 1, self.me).wait_recv()
        self.whole(a, 7, self.on(2), self.sibling).start()

    def wait_passed_on(self, a, j):
        self.whole(a, 5 + j, (*self.chips[j], 1 - self.c), self.me).wait_recv()

    def wait_sends(self, a):
        self.whole(a, 0, self.me, self.sibling).wait_send()
        for j in range(2):
            self.whole(a, 1 + j, self.me, self.on(j)).wait_send()
            self.halved(a, 3 + j, self.on(j), j, self.on(1 - j)).wait_send()
        for j in range(3):
            self.whole(a, 5 + j, self.on(j), self.sibling).wait_send()


def _block_order():
    x, y, c = _mesh_pos()
    idx = lambda chip, core: 4 * chip[0] + 2 * chip[1] + core
    own, first, second, diag = (x, y), (x ^ c, y ^ (1 - c)), (x ^ (1 - c), y ^ c), (1 - x, 1 - y)
    order = [idx(own, c), idx(own, 1 - c), idx(first, c), idx(second, 1 - c), idx(second, c), idx(first, 1 - c),
             idx(diag, c), idx(diag, 1 - c)]
    return jnp.stack(order).astype(jnp.int32)


def _gather_project(x, w_in, w_out, conv_w, lru_conv_w, ln_g, w_a, w_i, vecs, tb, tw):
    t_len, d = x.shape
    nb = t_len // tb
    cols = w_in.shape[1]
    mc = min(512, t_len)
    conv_pack = jax.ShapeDtypeStruct((SUBLANES, conv_w.shape[1]), F32)
    srcs = (w_in, w_out, conv_pack)
    dts = (MM, MM, F32)
    n_vec = len(vecs)
    n_heads, hd, _ = w_a.shape
    per = tw // hd

    def body(order_ref, x_ref, win_ref, wout_ref, cw_ref, lw_ref, lng_ref, wa_ref, wi_ref, *refs):
        vec_refs = refs[:n_vec]
        (p_ref, xnt_ref, win_all, wout_all, cp_all, sp_ref, wat_ref, wit_ref,
         xnb, wall, st_out, st_cp, cp_vm, send_sems, recv_sems, cp_send, cp_recv, local_sems) = refs[n_vec:]
        i = pl.program_id(0)
        x_, y_, c_ = _mesh_pos()
        me = 4 * x_ + 2 * y_ + c_
        outs = (win_all, wout_all, cp_all)
        lands = (wall, wout_all, cp_all)
        stages = (wall.at[me], st_out, st_cp)
        gather = _BalancedGather(lambda a, px, py, pc: lands[a].at[4 * px + 2 * py + pc], send_sems, recv_sems, stages)
        small = _Gather(lambda a, px, py, pc: cp_all.at[4 * px + 2 * py + pc], cp_send, cp_recv, own_src=[st_cp])
        keep_own = [pltpu.make_async_copy(stages[a], outs[a].at[me], local_sems.at[a]) for a in range(3)]

        def keep(k):
            blk = order_ref[k]
            return pltpu.make_async_copy(wall.at[blk], win_all.at[blk], local_sems.at[2 + k])

        @pl.when(i == 0)
        def _():
            for a, src in enumerate((win_ref, wout_ref)):
                dst, rc = stages[a], 32

                def cast(r, carry, src=src, dst=dst):
                    dst[pl.ds(r, rc), :] = src[pl.ds(r, rc), :].astype(dst.dtype)
                    return carry

                _chunks(src.shape[0], rc, cast, 0)
                keep_own[a].start()
            gather.start_own_staggered(0)
            n_cw, n_lw = cw_ref.shape[0], lw_ref.shape[0]
            st_cp[...] = jnp.zeros_like(st_cp)
            st_cp[0:n_cw, :] = cw_ref[...]
            st_cp[n_cw:n_cw + n_lw, :] = lw_ref[...]
            keep_own[2].start()

        @pl.when(i < nb)
        def _():
            xv = x_ref[...]
            r0 = lax.rsqrt(jnp.mean(xv * xv, axis=-1, keepdims=True) + RMS_EPS)
            xn = xv * r0 * lng_ref[...]
            xnb[pl.ds(pl.multiple_of(i * tb, tb), tb), :] = xn.astype(MM)
            xnt_ref[...] = xn.T.astype(MM)

        def by_core(action):
            for core, (first, second) in ((1, (0, 1)), (0, (1, 0))):
                @pl.when(c_ == core)
                def _(first=first, second=second):
                    action(first, second)

        for k in range(N_DEV):
            @pl.when(i == nb + k)
            def _(k=k):
                if k == 1:
                    gather.wait_sibling(0)
                elif k == 2:
                    by_core(lambda first, second: gather.on_neighbour(0, first))
                    gather.start_own(1)
                    small.start_own(0)
                elif k == 3:
                    by_core(lambda first, second: gather.wait_passed_on(0, second))
                elif k == 4:
                    by_core(lambda first, second: gather.on_neighbour(0, second))
                elif k == 5:
                    by_core(lambda first, second: gather.wait_passed_on(0, first))
                    gather.on_neighbour(1, 0)
                    gather.on_neighbour(1, 1)
                elif k == 6:
                    gather.on_diagonal(0)
                elif k == 7:
                    gather.wait_passed_on(0, 2)
                    gather.on_diagonal(1)
                blk = order_ref[k]
                if k:
                    keep(k).start()

                def project(r, carry):
                    rows = pl.ds(r, mc)
                    p_ref[rows, :] = _dot(xnb[rows, :], wall[blk]).astype(MM)
                    return carry

                _chunks(t_len, mc, project, 0)
                if k == N_DEV - 1:
                    gather.wait_sends(0)
                    gather.wait_sibling(1)
                    for j in range(3):
                        gather.wait_passed_on(1, j)
                    gather.wait_sends(1)
                    small.finish(0)
                    for cp in keep_own + [keep(kk) for kk in range(1, N_DEV)]:
                        cp.wait()
                    load = pltpu.make_async_copy(cp_all, cp_vm, local_sems.at[N_DEV + 2])
                    load.start()
                    load.wait()
                    for r, ref in enumerate(vec_refs):
                        sp_ref[r, :] = ref[...]
                    sp_ref[n_vec:n_vec + SUBLANES, :] = jnp.concatenate([cp_vm[dev] for dev in range(N_DEV)], axis=1)
                    for src, dst in ((wa_ref, wat_ref), (wi_ref, wit_ref)):
                        dst[...] = jnp.zeros_like(dst)
                        for head in range(n_heads):
                            lo = (head % per) * hd
                            dst[head // per, lo:lo + hd, lo:lo + hd] = src[head].astype(MM)

    vm = pl.BlockSpec(memory_space=pltpu.VMEM)
    hbm = pl.BlockSpec(memory_space=pl.ANY)
    grid_spec = pltpu.PrefetchScalarGridSpec(
        num_scalar_prefetch=1, grid=(nb + N_DEV,),
        in_specs=[pl.BlockSpec((tb, d), lambda i, o: (jnp.minimum(i, nb - 1), 0))] + [vm] * (7 + n_vec),
        out_specs=(pl.BlockSpec((t_len, cols), lambda i, o: (0, o[jnp.maximum(i - nb, 0)])),
                   pl.BlockSpec((d, tb), lambda i, o: (0, jnp.minimum(i, nb - 1))), hbm, hbm, hbm,
                   pl.BlockSpec((SP_ROWS, d), lambda i, o: (0, 0)),
                   pl.BlockSpec((n_heads // per, tw, tw), lambda i, o: (0, 0, 0)),
                   pl.BlockSpec((n_heads // per, tw, tw), lambda i, o: (0, 0, 0))),
        scratch_shapes=[pltpu.VMEM((t_len, d), MM), pltpu.VMEM((N_DEV,) + w_in.shape, MM),
                        pltpu.VMEM(w_out.shape, MM), pltpu.VMEM(conv_pack.shape, F32),
                        pltpu.VMEM((N_DEV,) + conv_pack.shape, F32),
                        pltpu.SemaphoreType.DMA((16,)), pltpu.SemaphoreType.DMA((16,)),
                        pltpu.SemaphoreType.DMA((7,)), pltpu.SemaphoreType.DMA((7,)), pltpu.SemaphoreType.DMA((N_DEV + 3,))])
    return pl.pallas_call(
        body, name="gather_project", grid_spec=grid_spec,
        out_shape=(jax.ShapeDtypeStruct((t_len, N_DEV * cols), MM),
                   jax.ShapeDtypeStruct((d, t_len), MM))
                  + tuple(jax.ShapeDtypeStruct((N_DEV,) + s.shape, dt) for s, dt in zip(srcs, dts))
                  + (jax.ShapeDtypeStruct((SP_ROWS, d), F32),)
                  + (jax.ShapeDtypeStruct((n_heads // per, tw, tw), MM),) * 2,
        compiler_params=_params(dimension_semantics=("arbitrary",)),
    )(_block_order(), x, w_in, w_out, conv_w, lru_conv_w, ln_g, w_a, w_i, *vecs)


def _forward(x, tgt, p, wout, wa_t, wi_t, sp, ones_c, ones_l, tb):
    t_len, d = x.shape
    nb = t_len // tb
    n_tiles, tw = wa_t.shape[0], wa_t.shape[1]
    hd_c, hd_l = d // N_CONV_HEADS, d // N_LRU_HEADS
    s8 = SUBLANES

    def body(x_ref, tgt_ref, p_ref, wout_ref, wa_ref, wi_ref, sp_ref, oc_ref, ol_ref,
             h_ref, dh_ref, dhb_ref, acc_ref, yc, czs, u, pa, pi,
             rcf, rlf, ybuf, tail_z, tail_xl, hcar):
        i = pl.program_id(0)
        row = _row_iota(d)

        @pl.when(i == 0)
        def _():
            tail_z[...] = jnp.zeros_like(tail_z)
            tail_xl[...] = jnp.zeros_like(tail_xl)
            hcar[...] = jnp.zeros_like(hcar)
            acc_ref[...] = jnp.zeros_like(acc_ref)

        def spr(r):
            return sp_ref[r:r + 1, :]

        def proj(rows, seg):
            return p_ref[rows, seg * d:(seg + 1) * d].astype(F32)

        w0, w1, w2 = spr(SP_CONV_W), spr(SP_CONV_W + 1), spr(SP_CONV_W + 2)
        l0, l1, l2, l3 = spr(SP_LRU_W), spr(SP_LRU_W + 1), spr(SP_LRU_W + 2), spr(SP_LRU_W + 3)
        lb = spr(SP_LRU_B)

        def convs(r, carry):
            zp, xp = carry
            rows16 = pl.ds(r, 2 * s8)
            bg16, xl16 = proj(rows16, P_B), proj(rows16, P_XL)
            z16 = proj(rows16, P_C) * proj(rows16, P_XC)
            for j in range(2):
                rows, sub = pl.ds(r + j * s8, s8), slice(j * s8, (j + 1) * s8)
                z, xl = z16[sub], xl16[sub]
                cz = w0 * _shift_down(z, zp, 2, row) + w1 * _shift_down(z, zp, 1, row) + w2 * z
                czs[rows, :] = cz
                yc[rows, :] = bg16[sub] * cz
                u[rows, :] = (l0 * _shift_down(xl, xp, 3, row) + l1 * _shift_down(xl, xp, 2, row)
                              + l2 * _shift_down(xl, xp, 1, row) + l3 * xl + lb)
                zp, xp = z, xl
            return zp, xp

        z_last, xl_last = _chunks(tb, 2 * s8, convs, (tail_z[...], tail_xl[...]))
        tail_z[...] = z_last
        tail_xl[...] = xl_last

        ub = u[...].astype(MM)
        for k in range(n_tiles):
            sl = slice(k * tw, (k + 1) * tw)
            pa[:, sl] = _dot(ub[:, sl], wa_ref[k])
            pi[:, sl] = _dot(ub[:, sl], wi_ref[k])
        rcf[...] = _head_rstd(yc[...], oc_ref[...], hd_c)

        c8 = RG_LRU_C * _log_sigmoid(spr(SP_LAM))
        b_a, b_i = spr(SP_B_A), spr(SP_B_I)

        def lru(r, hp):
            rows = pl.ds(r, SUBLANES)
            ra = _sigmoid(pa[rows, :] + b_a)
            ii = _sigmoid(pi[rows, :] + b_i)
            pa[rows, :] = ra
            pi[rows, :] = ii
            la = ra * c8
            a = jnp.exp(la)
            mult = jnp.sqrt(_lru_input_scale_sq(la, a))
            h = _scan_fwd(a, mult * (ii * u[rows, :]), hp, row)
            h_ref[rows, :] = h
            return _bcast_row(h, SUBLANES - 1)

        hcar[...] = _chunks(tb, SUBLANES, lru, hcar[...])
        rlf[...] = _head_rstd(h_ref[...], ol_ref[...], hd_l)

        g_c, g_l = spr(SP_CONV_G), spr(SP_LRU_G)

        def gate(r, carry):
            rows = pl.ds(r, 2 * s8)
            gc, gl = proj(rows, P_GC), proj(rows, P_GL)
            ybuf[rows, 0:d] = (yc[rows, :] * rcf[rows, :] * g_c * (gc * _sigmoid(gc))).astype(MM)
            ybuf[rows, d:2 * d] = (h_ref[rows, :] * rlf[rows, :] * g_l * (gl * _sigmoid(gl))).astype(MM)
            return carry

        _chunks(tb, 2 * s8, gate, 0)

        hres = x_ref[...] + _dot(ybuf[...], wout_ref[...])
        rf = lax.rsqrt(jnp.mean(hres * hres, axis=-1, keepdims=True) + RMS_EPS)
        hn = hres * rf
        fg = spr(SP_FINAL_G)
        err = hn * fg - tgt_ref[...]
        dout = err * (1.0 / d)
        acc_ref[0:SUBLANES, :] += (err * err).reshape(tb // SUBLANES, SUBLANES, d).sum(axis=0)
        acc_ref[SUBLANES:2 * SUBLANES, :] += (dout * hn).reshape(tb // SUBLANES, SUBLANES, d).sum(axis=0)
        gd = dout * fg
        dhres = rf * (gd - hn * jnp.mean(gd * hn, axis=-1, keepdims=True))
        dh_ref[...] = dhres
        dhb_ref[...] = dhres.astype(MM)

    vm = pl.BlockSpec(memory_space=pltpu.VMEM)
    blk = lambda w: pl.BlockSpec((tb, w), lambda i: (i, 0))
    buf = pltpu.VMEM((tb, d), F32)
    car = pltpu.VMEM((SUBLANES, d), F32)
    return pl.pallas_call(
        body, name="forward", grid=(nb,),
        in_specs=[blk(d), blk(d), blk(6 * d), vm, vm, vm, vm, vm, vm],
        out_specs=(blk(d), blk(d), blk(d), pl.BlockSpec((2 * SUBLANES, d), lambda i: (0, 0))) + (blk(d),) * 5,
        out_shape=(jax.ShapeDtypeStruct((t_len, d), F32),
                   jax.ShapeDtypeStruct((t_len, d), F32),
                   jax.ShapeDtypeStruct((t_len, d), MM),
                   jax.ShapeDtypeStruct((2 * SUBLANES, d), F32))
                  + (jax.ShapeDtypeStruct((t_len, d), F32),) * 5,
        scratch_shapes=[buf] * 2 + [pltpu.VMEM((tb, 2 * d), MM), car, car, car],
        compiler_params=_params(dimension_semantics=("arbitrary",)),
    )(x, tgt, p, wout, wa_t, wi_t, sp, ones_c, ones_l)


def _backward(p, h, dh, saved, facc, wout, wa_t, wi_t, sp, ones_c, ones_l, tb):
    t_len, d = h.shape
    nb = t_len // tb
    n_tiles, tw = wa_t.shape[0], wa_t.shape[1]
    hd_c, hd_l = d // N_CONV_HEADS, d // N_LRU_HEADS
    g_rows = 2 * n_tiles * hd_l
    s8 = SUBLANES

    def body(p_ref, h_ref, hhalo_ref, dh_ref, yc, czs, u, ra_ref, ii_ref, facc_ref,
             wout_ref, wa_ref, wi_ref, sp_ref, oc_ref, ol_ref,
             dp_ref, yt_ref, slab_v, slab_g,
             hh, dy, ybuf, rcf, rlf, qc, ql, dyc_hat, dyl_hat, dpa, dpi, du, gwa_ref, gwi_ref, acc_ref,
             car_dcz, car_a, car_g, car_du):
        i = pl.program_id(0)
        blk_idx = nb - 1 - i
        row = _row_iota(d)

        @pl.when(i == 0)
        def _():
            for ref in (car_dcz, car_a, car_g, car_du, gwa_ref, gwi_ref, acc_ref):
                ref[...] = jnp.zeros_like(ref)

        def spr(r):
            return sp_ref[r:r + 1, :]

        def proj(rows, seg):
            return p_ref[rows, seg * d:(seg + 1) * d].astype(F32)

        def put(rows, seg, halves):
            dp_ref[rows, seg * d:(seg + 1) * d] = jnp.concatenate(halves, axis=0).astype(MM)

        def acc_add(group, val):
            acc_ref[group * s8:(group + 1) * s8, :] += val

        live = jnp.where(blk_idx > 0, 1.0, 0.0).astype(F32)
        hh[0:s8, :] = hhalo_ref[...] * live
        hh[s8:, :] = h_ref[...]

        dy[...] = _dot_nt(dh_ref[...].astype(MM), wout_ref[...])

        w0, w1, w2 = spr(SP_CONV_W), spr(SP_CONV_W + 1), spr(SP_CONV_W + 2)
        l0, l1, l2, l3 = spr(SP_LRU_W), spr(SP_LRU_W + 1), spr(SP_LRU_W + 2), spr(SP_LRU_W + 3)

        rcf[...] = _head_rstd(yc[...], oc_ref[...], hd_c)
        rlf[...] = _head_rstd(h_ref[...], ol_ref[...], hd_l)

        g_c, g_l = spr(SP_CONV_G), spr(SP_LRU_G)

        def gates(r, carry):
            rows = pl.ds(r, 2 * s8)
            for (seg, off_y, src, rstd, gain, q, dhat, grp) in (
                    (P_GC, 0, yc, rcf, g_c, qc, dyc_hat, A_CONV_G),
                    (P_GL, d, h_ref, rlf, g_l, ql, dyl_hat, A_LRU_G)):
                gt = proj(rows, seg)
                sg = _sigmoid(gt)
                silu = gt * sg
                yhat = src[rows, :] * rstd[rows, :]
                nrm = yhat * gain
                ybuf[rows, off_y:off_y + d] = nrm * silu
                dout = dy[rows, off_y:off_y + d]
                dnrm = dout * silu
                dp_ref[rows, seg * d:(seg + 1) * d] = (dout * nrm * (sg * (1.0 + gt * (1.0 - sg)))).astype(MM)
                dg = dnrm * yhat
                acc_add(grp, dg[0:s8] + dg[s8:])
                dh_ = dnrm * gain
                dhat[rows, :] = dh_
                q[rows, :] = dh_ * yhat
            return carry

        _chunks(tb, 2 * s8, gates, 0)

        qc[...] = _head_sums(qc[...], oc_ref[...]) * (1.0 / hd_c)
        ql[...] = _head_sums(ql[...], ol_ref[...]) * (1.0 / hd_l)
        yt_ref[...] = ybuf[...].T.astype(MM)

        c8 = RG_LRU_C * _log_sigmoid(spr(SP_LAM))

        def conv_mixer(r, dcz_n):
            rows16 = pl.ds(r, 2 * s8)
            bg16, cg16, xc16 = proj(rows16, P_B), proj(rows16, P_C), proj(rows16, P_XC)
            z16 = cg16 * xc16
            d_b, d_c, d_x = [None, None], [None, None], [None, None]
            for j in (1, 0):
                rows, sub = pl.ds(r + j * s8, s8), slice(j * s8, (j + 1) * s8)
                rstd = rcf[rows, :]
                yhat = yc[rows, :] * rstd
                dyc = rstd * (dyc_hat[rows, :] - yhat * qc[rows, :])
                d_b[j] = dyc * czs[rows, :]
                dcz = dyc * bg16[sub]
                up1, up2 = _shift_up(dcz, dcz_n, 1, row), _shift_up(dcz, dcz_n, 2, row)
                dz = w2 * dcz + w1 * up1 + w0 * up2
                d_c[j] = dz * xc16[sub]
                d_x[j] = dz * cg16[sub]
                z = z16[sub]
                acc_add(A_CONV_W, up2 * z)
                acc_add(A_CONV_W + 1, up1 * z)
                acc_add(A_CONV_W + 2, dcz * z)
                dcz_n = dcz
            put(rows16, P_B, d_b)
            put(rows16, P_C, d_c)
            put(rows16, P_XC, d_x)
            return dcz_n

        car_dcz[...] = _chunks(tb, 2 * s8, conv_mixer, car_dcz[...], reverse=True)

        def lru_mixer(r, carry):
            a_n, g_n = carry
            for j in (1, 0):
                rows = pl.ds(r + j * s8, s8)
                rstd = rlf[rows, :]
                hcur = hh[pl.ds(r + (j + 1) * s8, s8), :]
                hhat = hcur * rstd
                dh_out = rstd * (dyl_hat[rows, :] - hhat * ql[rows, :])
                ra = ra_ref[rows, :]
                la = ra * c8
                a = jnp.exp(la)
                g = _scan_bwd(_shift_up(a, a_n, 1, row), dh_out, g_n, row)
                da = g * _shift_down(hcur, hh[pl.ds(r + j * s8, s8), :], 1, row)
                ii = ii_ref[rows, :]
                uu = u[rows, :]
                mult_sq = _lru_input_scale_sq(la, a)
                inv_mult = lax.rsqrt(mult_sq)
                dmult = g * (ii * uu)
                ds = g * (mult_sq * inv_mult)
                dla = a * (da - dmult * a * inv_mult)
                acc_add(A_LAM, dla * ra)
                dpa_ = dla * c8 * ra * (1.0 - ra)
                dpi_ = ds * uu * ii * (1.0 - ii)
                acc_add(A_B_A, dpa_)
                acc_add(A_B_I, dpi_)
                dpa[rows, :] = dpa_
                dpi[rows, :] = dpi_
                du[rows, :] = ds * ii
                a_n, g_n = a, _bcast_row(g, 0)
            return a_n, g_n

        a_f, g_f = _chunks(tb, 2 * s8, lru_mixer, (car_a[...], car_g[...]), reverse=True)
        car_a[...] = a_f
        car_g[...] = g_f

        dpab = dpa[...].astype(MM)
        dpib = dpi[...].astype(MM)
        for k in range(n_tiles):
            sl = slice(k * tw, (k + 1) * tw)
            du[:, sl] += _dot_nt(dpab[:, sl], wa_ref[k]) + _dot_nt(dpib[:, sl], wi_ref[k])
            ut = u[:, sl].T.astype(MM)
            gwa_ref[k] += _dot(ut, dpab[:, sl])
            gwi_ref[k] += _dot(ut, dpib[:, sl])

        def lru_conv(r, du_n):
            rows16 = pl.ds(r, 2 * s8)
            xl16 = proj(rows16, P_XL)
            d_xl = [None, None]
            for j in (1, 0):
                rows, sub = pl.ds(r + j * s8, s8), slice(j * s8, (j + 1) * s8)
                dut = du[rows, :]
                up1, up2, up3 = (_shift_up(dut, du_n, s, row) for s in (1, 2, 3))
                d_xl[j] = l3 * dut + l2 * up1 + l1 * up2 + l0 * up3
                xl = xl16[sub]
                acc_add(A_LRU_W, up3 * xl)
                acc_add(A_LRU_W + 1, up2 * xl)
                acc_add(A_LRU_W + 2, up1 * xl)
                acc_add(A_LRU_W + 3, dut * xl)
                acc_add(A_LRU_B, dut)
                du_n = dut
            put(rows16, P_XL, d_xl)
            return du_n

        car_du[...] = _chunks(tb, 2 * s8, lru_conv, car_du[...], reverse=True)

        @pl.when(i == nb - 1)
        def _():
            def rowsum(ref, group):
                return jnp.sum(ref[group * s8:(group + 1) * s8, :], axis=0, keepdims=True)

            slab_v[...] = jnp.zeros_like(slab_v)
            loss = jnp.sum(rowsum(facc_ref, 0), axis=1, keepdims=True) * (0.5 / d)
            rows = {SL_LOSS: jnp.broadcast_to(loss, (1, d)), SL_FINAL_G: rowsum(facc_ref, 1),
                    SL_LRU_B: rowsum(acc_ref, A_LRU_B), SL_B_A: rowsum(acc_ref, A_B_A), SL_B_I: rowsum(acc_ref, A_B_I),
                    SL_LAM: rowsum(acc_ref, A_LAM), SL_CONV_G: rowsum(acc_ref, A_CONV_G), SL_LRU_G: rowsum(acc_ref, A_LRU_G)}
            for k in range(3):
                rows[SL_CONV_W + k] = rowsum(acc_ref, A_CONV_W + k)
            for k in range(4):
                rows[SL_LRU_W + k] = rowsum(acc_ref, A_LRU_W + k)
            for r, val in rows.items():
                slab_v[r:r + 1, :] = val
            head_of_lane = lax.broadcasted_iota(jnp.int32, (hd_l, tw), 1) // hd_l
            for mtx, g_ref in enumerate((gwa_ref, gwi_ref)):
                for k in range(n_tiles):
                    packed = jnp.zeros((hd_l, tw), F32)
                    for a in range(tw // hd_l):
                        packed = jnp.where(head_of_lane == a, g_ref[k, a * hd_l:(a + 1) * hd_l, :], packed)
                    slab_g[(mtx * n_tiles + k) * hd_l:(mtx * n_tiles + k + 1) * hd_l, :] = packed.astype(MM)

    vm = pl.BlockSpec(memory_space=pltpu.VMEM)
    rev = lambda w: pl.BlockSpec((tb, w), lambda i: (nb - 1 - i, 0))
    halo = lambda rows, w: pl.BlockSpec((rows, w), lambda i: (jnp.maximum((nb - 1 - i) * (tb // rows) - 1, 0), 0))
    const = lambda shape: pl.BlockSpec(shape, lambda i: (0,) * len(shape))
    buf = lambda w: pltpu.VMEM((tb, w), F32)
    car = pltpu.VMEM((SUBLANES, d), F32)
    return pl.pallas_call(
        body, name="backward", grid=(nb,),
        in_specs=[rev(6 * d), rev(d), halo(SUBLANES, d), rev(d)] + [rev(d)] * 5 + [vm, vm, vm, vm, vm, vm, vm],
        out_specs=(rev(6 * d), pl.BlockSpec((2 * d, tb), lambda i: (0, nb - 1 - i)),
                   const((SL_ROWS, d)), const((g_rows, tw))),
        out_shape=(jax.ShapeDtypeStruct((t_len, 6 * d), MM),
                   jax.ShapeDtypeStruct((2 * d, t_len), MM),
                   jax.ShapeDtypeStruct((SL_ROWS, d), F32),
                   jax.ShapeDtypeStruct((g_rows, tw), MM)),
        scratch_shapes=[pltpu.VMEM((SUBLANES + tb, d), F32), buf(2 * d), buf(2 * d)] + [buf(d)] * 9
                       + [pltpu.VMEM((n_tiles, tw, tw), F32), pltpu.VMEM((n_tiles, tw, tw), F32),
                          pltpu.VMEM((A_GROUPS * SUBLANES, d), F32), car, car, car, car],
        compiler_params=_params(dimension_semantics=("arbitrary",)),
    )(p, h, h, dh, *saved, facc, wout, wa_t, wi_t, sp, ones_c, ones_l)


def _input_grad(dp, win_all, x, dh, sp, part, tb):
    t_len, d = x.shape
    nb = t_len // tb
    cols = win_all.shape[2]
    mid = min(nb - 1, (3 * nb) // 8)
    late = min(mid, nb // 4)
    rc = 32

    def body(dp_ref, win_ref, x_ref, dh_ref, sp_ref, part_ref, gx_ref, ln_ref, direct, relayed,
             send_sems, recv_sems, local_sems, acc_ref, ln_all, ln_send, ln_recv, mine, theirs):
        i = pl.program_id(0)
        x_, y_, c_ = _mesh_pos()
        first, second = 1 - c_, c_
        nbr1 = (x_ ^ c_, y_ ^ (1 - c_), c_)
        nbr2 = (x_ ^ (1 - c_), y_ ^ c_, c_)

        def remote(src, dst, k, to):
            return pltpu.make_async_remote_copy(src_ref=src, dst_ref=dst, send_sem=send_sems.at[k], recv_sem=recv_sems.at[k],
                                                device_id=to, device_id_type=MESH)

        to_first = [remote(part_ref.at[first], direct, 0, nbr1), remote(part_ref.at[2], theirs, 1, nbr1)]
        to_second = remote(theirs, relayed, 2, nbr2)
        load_mine = pltpu.make_async_copy(part_ref.at[second], mine, local_sems.at[0])

        @pl.when(i == 0)
        def _():
            acc_ref[...] = jnp.zeros_like(acc_ref)
            to_first[1].start()
            load_mine.start()

        @pl.when(i == late)
        def _():
            to_first[0].start()

        dxt = _dot_nt(win_ref[0], dp_ref[:, 0:cols])
        for j in range(1, N_DEV):
            dxt += _dot_nt(win_ref[j], dp_ref[:, j * cols:(j + 1) * cols])
        dxn = dxt.T
        xv = x_ref[...]
        r0 = lax.rsqrt(jnp.mean(xv * xv, axis=-1, keepdims=True) + RMS_EPS)
        xhat = xv * r0
        acc_ref[...] += (dxn * xhat).reshape(tb // SUBLANES, SUBLANES, d).sum(axis=0)
        dxh = dxn * sp_ref[SP_LN_G:SP_LN_G + 1, :]
        gx_ref[...] = dh_ref[...] + r0 * (dxh - xhat * jnp.mean(dxh * xhat, axis=-1, keepdims=True))

        @pl.when(i == mid)
        def _():
            to_first[1].wait_recv()
            load_mine.wait()

            def add(r, carry):
                rows = pl.ds(r, rc)
                theirs[rows, :] = (mine[rows, :].astype(F32) + theirs[rows, :].astype(F32)).astype(MM)
                return carry

            _chunks(mine.shape[0], rc, add, 0)
            to_second.start()

        @pl.when(i == nb - 1)
        def _():
            to_first[0].wait_recv()
            to_second.wait_recv()
            for cp in to_first + [to_second]:
                cp.wait_send()
            ln_all[4 * x_ + 2 * y_ + c_] = jnp.broadcast_to(jnp.sum(acc_ref[...], axis=0, keepdims=True), acc_ref.shape)
            gather = _Gather(lambda a, px, py, pc: ln_all.at[4 * px + 2 * py + pc], ln_send, ln_recv)
            gather.start_own(0)
            gather.finish(0)
            total = ln_all[0]
            for dev in range(1, N_DEV):
                total = total + ln_all[dev]
            ln_ref[...] = total

    vm = pl.BlockSpec(memory_space=pltpu.VMEM)
    hbm = pl.BlockSpec(memory_space=pl.ANY)
    blk = lambda w: pl.BlockSpec((tb, w), lambda i: (i, 0))
    landed = jax.ShapeDtypeStruct(part.shape[1:], part.dtype)
    outs = pl.pallas_call(
        body, name="input_grad", grid=(nb,),
        in_specs=[blk(6 * d), vm, blk(d), blk(d), vm, hbm],
        out_specs=(blk(d), pl.BlockSpec((SUBLANES, d), lambda i: (0, 0)), hbm, hbm),
        out_shape=(jax.ShapeDtypeStruct((t_len, d), F32), jax.ShapeDtypeStruct((SUBLANES, d), F32), landed, landed),
        scratch_shapes=[pltpu.SemaphoreType.DMA((3,)), pltpu.SemaphoreType.DMA((3,)), pltpu.SemaphoreType.DMA((1,)),
                        pltpu.VMEM((SUBLANES, d), F32), pltpu.VMEM((N_DEV, SUBLANES, d), F32),
                        pltpu.SemaphoreType.DMA((7,)), pltpu.SemaphoreType.DMA((7,)),
                        pltpu.VMEM(part.shape[1:], MM), pltpu.VMEM(part.shape[1:], MM)],
        compiler_params=_params(dimension_semantics=("arbitrary",)),
    )(dp, win_all, x, dh, sp, part)
    return outs[0], outs[1], (outs[2], outs[3])


_CHIP_RELATIONS = [(0, 0), (1, 0), (0, 1), (1, 1)]


def _related_block(k, core):
    x, y, _ = _mesh_pos()
    fx, fy = _CHIP_RELATIONS[k]
    return 4 * (x ^ fx) + 2 * (y ^ fy) + core


class _ChipExchange:
    def __init__(self, part_refs, land_refs, send_sems, recv_sems):
        self.part_refs, self.land_refs, self.send_sems, self.recv_sems = part_refs, land_refs, send_sems, recv_sems

    def copies(self):
        x, y, c = _mesh_pos()
        for a in range(len(self.part_refs)):
            for k in (1, 2, 3):
                fx, fy = _CHIP_RELATIONS[k]
                yield pltpu.make_async_remote_copy(
                    src_ref=self.part_refs[a].at[k - 1], dst_ref=self.land_refs[a].at[k - 1],
                    send_sem=self.send_sems.at[3 * a + k - 1], recv_sem=self.recv_sems.at[3 * a + k - 1],
                    device_id=(x ^ fx, y ^ fy, c), device_id_type=MESH)

    def start(self):
        for cp in self.copies():
            cp.start()

    def finish(self):
        for cp in self.copies():
            cp.wait_recv()
        for cp in self.copies():
            cp.wait_send()


def _weight_grad_stage1(name, blk_shape, n_split, operands, in_specs, product, riders=(), slabs=()):
    n_rows, n_cols = blk_shape
    rs = n_rows // n_split
    rc = 32
    n_in, n_ride, n_slab = len(operands), len(riders), len(slabs)
    _, _, c = _mesh_pos()
    order = jnp.stack([_related_block(k, 1 - c) for k in range(4)]
                      + [_related_block(k, c) for k in (1, 2, 3, 0)]).astype(jnp.int32)

    def body(order_ref, *refs):
        ins = refs[:n_in]
        ride_in = refs[n_in:n_in + n_ride]
        slab_in = refs[n_in + n_ride:n_in + n_ride + n_slab]
        n_op = n_in + n_ride + n_slab
        part_ref, own_ref = refs[n_op:n_op + 2]
        ride_out = refs[n_op + 2:n_op + 2 + n_ride]
        gathered = refs[n_op + 2 + n_ride:n_op + 2 + n_ride + n_slab]
        (gbuf, sendbuf, from_sib, send_sems, recv_sems, ride_send, ride_recv,
         slab_send, slab_recv, slab_local) = refs[n_op + 2 + n_ride + n_slab:]
        exchange = _ChipExchange(ride_in, ride_out, ride_send, ride_recv)
        s = pl.program_id(0)
        x, y, c = _mesh_pos()
        me = 4 * x + 2 * y + c
        gather = _BalancedGather(lambda a, px, py, pc: gathered[a].at[4 * px + 2 * py + pc], slab_send, slab_recv, slab_in)
        keep_own = [pltpu.make_async_copy(slab_in[a], gathered[a].at[me], slab_local.at[a]) for a in range(n_slab)]

        def to_sibling(k):
            return pltpu.make_async_remote_copy(
                src_ref=sendbuf.at[k], dst_ref=from_sib.at[k], send_sem=send_sems.at[k], recv_sem=recv_sems.at[k],
                device_id=(x, y, 1 - c), device_id_type=MESH)

        @pl.when(s == 0)
        def _():
            exchange.start()
            for a in range(n_slab):
                gather.start_own(a)
                keep_own[a].start()

        @pl.when(s == 5)
        def _():
            for a in range(n_slab):
                gather.on_neighbour(a, 0)
                gather.on_neighbour(a, 1)

        @pl.when(s == 7)
        def _():
            for a in range(n_slab):
                gather.on_diagonal(a)

        for h in range(n_split):
            gbuf[h * rs:(h + 1) * rs, :] = product(ins, h)

        @pl.when(s < 4)
        def _():
            def narrow(r, carry):
                sendbuf[s, pl.ds(r, rc), :] = gbuf[pl.ds(r, rc), :].astype(MM)
                return carry

            _chunks(n_rows, rc, narrow, 0)
            to_sibling(s).start()

        @pl.when(s >= 4)
        def _():
            k = jnp.where(s == 7, 0, s - 3)
            to_sibling(k).wait_recv()

            @pl.when(s < 7)
            def _():
                def add(r, carry):
                    rows = pl.ds(r, rc)
                    part_ref[0, rows, :] = (gbuf[rows, :] + from_sib[k, rows, :].astype(F32)).astype(MM)
                    return carry

                _chunks(n_rows, rc, add, 0)

            @pl.when(s == 7)
            def _():
                def add(r, carry):
                    rows = pl.ds(r, rc)
                    own_ref[rows, :] = gbuf[rows, :] + from_sib[0, rows, :].astype(F32)
                    return carry

                _chunks(n_rows, rc, add, 0)
                for kk in range(4):
                    to_sibling(kk).wait_send()
                exchange.finish()
                for a in range(n_slab):
                    gather.wait_sibling(a)
                    for j in range(3):
                        gather.wait_passed_on(a, j)
                    gather.wait_sends(a)
                    keep_own[a].wait()

    hbm = pl.BlockSpec(memory_space=pl.ANY)
    grid_spec = pltpu.PrefetchScalarGridSpec(
        num_scalar_prefetch=1, grid=(N_DEV,), in_specs=list(in_specs) + [hbm] * (n_ride + n_slab),
        out_specs=(pl.BlockSpec((1, n_rows, n_cols), lambda s, o: (jnp.clip(s - 4, 0, 2), 0, 0)),
                   pl.BlockSpec((n_rows, n_cols), lambda s, o: (0, 0))) + (hbm,) * (n_ride + n_slab),
        scratch_shapes=[pltpu.VMEM((n_rows, n_cols), F32), pltpu.VMEM((4, n_rows, n_cols), MM),
                        pltpu.VMEM((4, n_rows, n_cols), MM),
                        pltpu.SemaphoreType.DMA((4,)), pltpu.SemaphoreType.DMA((4,)),
                        pltpu.SemaphoreType.DMA((max(3 * n_ride, 1),)), pltpu.SemaphoreType.DMA((max(3 * n_ride, 1),)),
                        pltpu.SemaphoreType.DMA((max(8 * n_slab, 1),)), pltpu.SemaphoreType.DMA((max(8 * n_slab, 1),)),
                        pltpu.SemaphoreType.DMA((max(n_slab, 1),))])
    outs = pl.pallas_call(
        body, name=name, grid_spec=grid_spec,
        out_shape=(jax.ShapeDtypeStruct((3, n_rows, n_cols), MM), jax.ShapeDtypeStruct((n_rows, n_cols), F32))
                  + tuple(jax.ShapeDtypeStruct(p.shape, p.dtype) for p in riders)
                  + tuple(jax.ShapeDtypeStruct((N_DEV,) + a.shape, a.dtype) for a in slabs),
        compiler_params=_params(dimension_semantics=("arbitrary",)),
    )(order, *operands, *riders, *slabs)
    return outs[0], outs[1], outs[2:2 + n_ride], outs[2 + n_ride:]


def _weight_grad_in(xnt, dp, riders, slabs):
    d, t_len = xnt.shape
    cols = dp.shape[1] // N_DEV
    half = d // 2
    return _weight_grad_stage1(
        "weight_grad_in", (d, cols), 2, (xnt, dp),
        [pl.BlockSpec(memory_space=pltpu.VMEM), pl.BlockSpec((t_len, cols), lambda s, o: (0, o[s]))],
        lambda refs, h: _dot(refs[0][h * half:(h + 1) * half, :], refs[1][...]), riders, slabs)


def _weight_grad_out(yt, dhb):
    d2, t_len = yt.shape
    d = dhb.shape[1]
    rows = d2 // N_DEV
    return _weight_grad_stage1(
        "weight_grad_out", (rows, d), 1, (yt, dhb),
        [pl.BlockSpec((rows, t_len), lambda s, o: (o[s], 0)), pl.BlockSpec(memory_space=pltpu.VMEM)],
        lambda refs, h: _dot(refs[0][...], refs[1][...]))


def _update_shard(own, others, w, m, v, name):
    n_rows, n_cols = w.shape
    rb = min(256, n_rows)
    n_other = len(others)

    def body(own_ref, *refs):
        other_refs = refs[:n_other]
        w_ref, m_ref, v_ref, grad_ref, delta_ref, mo_ref, vo_ref = refs[n_other:]
        g = own_ref[...]
        for ref in other_refs:
            for k in range(ref.shape[0] if len(ref.shape) == 3 else 1):
                g = g + (ref[k] if len(ref.shape) == 3 else ref[...]).astype(F32)
        delta, m_new, v_new = _adamw(w_ref[...], g, m_ref[...], v_ref[...])
        grad_ref[...] = g
        delta_ref[...] = delta
        mo_ref[...] = m_new
        vo_ref[...] = v_new

    blk = pl.BlockSpec((rb, n_cols), lambda i: (i, 0))
    stacked = lambda n: pl.BlockSpec((n, rb, n_cols), lambda i: (0, i, 0))
    out = jax.ShapeDtypeStruct((n_rows, n_cols), F32)
    return pl.pallas_call(
        body, name=name, grid=(n_rows // rb,),
        in_specs=[blk] + [stacked(o.shape[0]) if o.ndim == 3 else blk for o in others] + [blk, blk, blk],
        out_specs=(blk, blk, blk, blk), out_shape=(out, out, out, out),
        compiler_params=_params(dimension_semantics=("arbitrary",)),
    )(own, *others, w, m, v)


def _small_update(gat_v, gat_g, ln_tot, vec_w, vec_m, vec_v, gates, convs):
    n_vec = len(vec_w)
    n_heads, hd, _ = gates[0].shape
    tw = gat_g.shape[2]
    s8 = SUBLANES
    per = tw // hd
    n_tiles = n_heads // per
    cc = convs[0].shape[1]
    n_in = 3 + 3 * n_vec + 12

    def body(*refs):
        gv_ref, gg_ref, ln_ref = refs[:3]
        w_refs, m_refs, v_refs = (refs[3 + j * n_vec:3 + (j + 1) * n_vec] for j in range(3))
        gate_refs = refs[3 + 3 * n_vec:3 + 3 * n_vec + 6]
        conv_refs = refs[3 + 3 * n_vec + 6:n_in]
        loss_o = refs[n_in]
        kinds = [refs[n_in + 1 + j * (n_vec + 4):n_in + 1 + (j + 1) * (n_vec + 4)] for j in range(4)]
        tv, tg = refs[n_in + 1 + 4 * (n_vec + 4):]
        x, y, c = _mesh_pos()
        me = 4 * x + 2 * y + c

        def emit(k_out, w, g, m, v):
            delta, m_new, v_new = _adamw(w, g, m, v)
            for ref, val in zip(k_out, (g, delta, m_new, v_new)):
                ref[...] = val

        total = gv_ref[0]
        for dev in range(1, N_DEV):
            total = total + gv_ref[dev]
        tv[...] = total
        tv[SL_LN_G:SL_LN_G + 1, :] = ln_ref[0:1, :]

        def sum_gates(r, carry):
            rows = pl.ds(r, 2 * s8)
            part = gg_ref[0, rows, :].astype(F32)
            for dev in range(1, N_DEV):
                part = part + gg_ref[dev, rows, :].astype(F32)
            tg[rows, :] = part
            return carry

        _chunks(tg.shape[0], 2 * s8, sum_gates, 0)
        loss_o[...] = jnp.broadcast_to(tv[SL_LOSS:SL_LOSS + 1, 0:LANES], loss_o.shape)
        for p in range(n_vec):
            w, g = w_refs[p][...], tv[SL_LN_G + p, :]
            if SL_LN_G + p == SL_LAM:
                g = g * (RG_LRU_C * jax.nn.sigmoid(-w))
            emit([k_out[p] for k_out in kinds], w, g, m_refs[p][...], v_refs[p][...])
        lanes = pl.ds(pl.multiple_of(me * cc, cc), cc)
        for j, (row0, n) in enumerate(((SL_CONV_W, 3), (SL_LRU_W, 4))):
            w_ref, m_ref, v_ref = conv_refs[3 * j:3 * j + 3]
            emit([k_out[n_vec + 2 + j] for k_out in kinds], w_ref[...], tv[row0:row0 + n, lanes], m_ref[...], v_ref[...])
        for mtx in range(2):
            w_ref, m_ref, v_ref = gate_refs[3 * mtx:3 * mtx + 3]
            for k in range(n_tiles):
                tile = tg[(mtx * n_tiles + k) * hd:(mtx * n_tiles + k + 1) * hd, :]
                for a in range(per):
                    head = k * per + a
                    g = tile[:, a * hd:(a + 1) * hd]
                    delta, m_new, v_new = _adamw(w_ref[head], g, m_ref[head], v_ref[head])
                    for k_out, val in zip(kinds, (g, delta, m_new, v_new)):
                        k_out[n_vec + mtx][head] = val

    vm = pl.BlockSpec(memory_space=pltpu.VMEM)
    like = lambda a: jax.ShapeDtypeStruct(a.shape, F32)
    per_kind = tuple(like(a) for a in vec_w) + (like(gates[0]), like(gates[3]), like(convs[0]), like(convs[3]))
    n_out = 1 + 4 * len(per_kind)
    outs = pl.pallas_call(
        body, name="small_update",
        in_specs=[vm] * n_in, out_specs=(vm,) * n_out,
        out_shape=(jax.ShapeDtypeStruct((SUBLANES, LANES), F32),) + per_kind * 4,
        scratch_shapes=[pltpu.VMEM(gat_v.shape[1:], F32), pltpu.VMEM(gat_g.shape[1:], F32)],
        compiler_params=_params(),
    )(gat_v, gat_g, ln_tot, *vec_w, *vec_m, *vec_v, *gates, *convs)
    return outs[0], [outs[1 + j * len(per_kind):1 + (j + 1) * len(per_kind)] for j in range(4)]


def _head_ones(head_dim, tw):
    lane = jnp.arange(tw) // head_dim
    return (lane[:, None] == lane[None, :]).astype(MM)


def kernel(x, ln_g, w_in, conv_w, lru_conv_w, lru_conv_b, w_a, b_a, w_i, b_i, lam, conv_out_g, lru_out_g, w_out, final_g, loss_target, m_ln_g, m_w_in, m_conv_w, m_lru_conv_w, m_lru_conv_b, m_w_a, m_b_a, m_w_i, m_b_i, m_lam, m_conv_out_g, m_lru_out_g, m_w_out, m_final_g, v_ln_g, v_w_in, v_conv_w, v_lru_conv_w, v_lru_conv_b, v_w_a, v_b_a, v_w_i, v_b_i, v_lam, v_conv_out_g, v_lru_out_g, v_w_out, v_final_g):
    _, t_len, d = x.shape
    hd_l = d // N_LRU_HEADS
    tw = min(MXU_TILE, d)
    x2, tgt2 = x[0], loss_target[0]

    small = [ln_g, lru_conv_b, b_a, b_i, lam, conv_out_g, lru_out_g, final_g]
    p, xnt, win_all, wout_all, _, sp, wa_t, wi_t = _gather_project(
        x2, w_in, w_out, conv_w, lru_conv_w, ln_g.reshape(1, d), w_a, w_i, small, min(256, t_len), tw)
    wout_full = wout_all.reshape(N_DEV * w_out.shape[0], d)
    ones_c, ones_l = _head_ones(d // N_CONV_HEADS, tw), _head_ones(hd_l, tw)

    h, dh, dhb, facc, *saved = _forward(x2, tgt2, p, wout_full, wa_t, wi_t, sp, ones_c, ones_l, min(256, t_len))
    dp, yt, slab_v, slab_g = _backward(p, h, dh, saved, facc, wout_full, wa_t, wi_t, sp, ones_c, ones_l, min(256, t_len))
    part_out, own_out, _, _ = _weight_grad_out(yt, dhb)
    part_in, own_in, (chips_out,), (gat_v, gat_g) = _weight_grad_in(xnt, dp, (part_out,), (slab_v, slab_g))
    grad_x, ln_tot, sums_in = _input_grad(dp, win_all, x2, dh, sp, part_in, min(512, t_len))
    gw_in, dw_in, mw_in, vw_in = _update_shard(own_in, sums_in, w_in, m_w_in, v_w_in, "update_w_in")
    gw_out, dw_out, mw_out, vw_out = _update_shard(own_out, (chips_out,), w_out, m_w_out, v_w_out, "update_w_out")

    loss_tile, kinds = _small_update(
        gat_v, gat_g, ln_tot, small,
        [m_ln_g, m_lru_conv_b, m_b_a, m_b_i, m_lam, m_conv_out_g, m_lru_out_g, m_final_g],
        [v_ln_g, v_lru_conv_b, v_b_a, v_b_i, v_lam, v_conv_out_g, v_lru_out_g, v_final_g],
        (w_a, m_w_a, v_w_a, w_i, m_w_i, v_w_i), (conv_w, m_conv_w, v_conv_w, lru_conv_w, m_lru_conv_w, v_lru_conv_w))

    def unpack(kind, big_in, big_out):
        vec, (wa_, wi_, cw_, lw_) = kind[:len(small)], kind[len(small):]
        return [vec[0], big_in, cw_, lw_, vec[1], wa_, vec[2], wi_, vec[3], vec[4], vec[5], vec[6], big_out, vec[7]]

    return (loss_tile[0, 0], grad_x[None], *unpack(kinds[0], gw_in, gw_out), *unpack(kinds[1], dw_in, dw_out),
            *unpack(kinds[2], mw_in, mw_out), *unpack(kinds[3], vw_in, vw_out))
```

```python
import jax
import jax.numpy as jnp
from jax import lax
from jax.experimental import pallas as pl
from jax.experimental.pallas import tpu as pltpu

F32 = jnp.float32
MM = jnp.bfloat16
MESH = pl.DeviceIdType.MESH

N_DEV = 8
N_CONV_HEADS = 8
N_LRU_HEADS = 16
RG_LRU_C = 8.0
RMS_EPS = 1e-6
ADAM_LR, ADAM_B1, ADAM_B2, ADAM_EPS, ADAM_WD, ADAM_STEP = 0.001, 0.9, 0.999, 1e-08, 0.01, 10
ADAM_BC1 = 1.0 - ADAM_B1 ** ADAM_STEP
ADAM_BC2 = 1.0 - ADAM_B2 ** ADAM_STEP

SUBLANES = 8
LANES = 128
MXU_TILE = 256
VMEM_LIMIT = 56 * 1024 * 1024

SP_LN_G, SP_LRU_B, SP_B_A, SP_B_I, SP_LAM, SP_CONV_G, SP_LRU_G, SP_FINAL_G, SP_CONV_W, SP_LRU_W = 0, 1, 2, 3, 4, 5, 6, 7, 8, 11
SP_ROWS = 16
P_B, P_C, P_XC, P_GC, P_XL, P_GL = 0, 1, 2, 3, 4, 5
A_CONV_G, A_LRU_G, A_LAM, A_B_A, A_B_I, A_CONV_W, A_LRU_W, A_LRU_B = 0, 1, 2, 3, 4, 5, 8, 12
A_GROUPS = 13
SL_LOSS, SL_LN_G, SL_LRU_B, SL_B_A, SL_B_I, SL_LAM, SL_CONV_G, SL_LRU_G, SL_FINAL_G, SL_CONV_W, SL_LRU_W = 0, 1, 2, 3, 4, 5, 6, 7, 8, 16, 24
SL_ROWS = 32


def _params(vmem=True, **kw):
    if vmem:
        kw["vmem_limit_bytes"] = VMEM_LIMIT
    return pltpu.CompilerParams(**kw)


def _dot(a, b):
    return jnp.dot(a, b, preferred_element_type=F32)


def _dot_nt(a, b):
    return lax.dot_general(a, b, (((1,), (1,)), ((), ())), preferred_element_type=F32)


def _head_sums(v, ones_tile):
    tw = ones_tile.shape[0]
    vb = v.astype(MM)
    return jnp.concatenate([_dot(vb[:, k:k + tw], ones_tile) for k in range(0, v.shape[1], tw)], axis=1)


def _head_rstd(v, ones_tile, head_dim):
    return lax.rsqrt(_head_sums(v * v, ones_tile) * (1.0 / head_dim) + RMS_EPS)


def _sigmoid(x):
    return 0.5 * jnp.tanh(0.5 * x) + 0.5


def _lru_input_scale_sq(log_a, a):
    return -jnp.tanh(log_a) * (1.0 + a * a)


def _log_sigmoid(x):
    z = jnp.exp(-jnp.abs(x))
    u = 1.0 + z
    log1p_z = jnp.where(u == 1.0, z, jnp.log(u) * (z / (u - 1.0)))
    return jnp.minimum(x, 0.0) - log1p_z


def _row_iota(d):
    return lax.broadcasted_iota(jnp.int32, (SUBLANES, d), 0)


def _shift_down(cur, prev, s, row):
    return jnp.where(row >= s, pltpu.roll(cur, s, axis=0), pltpu.roll(prev, s, axis=0))


def _shift_up(cur, nxt, s, row):
    k = SUBLANES - s
    return jnp.where(row < k, pltpu.roll(cur, k, axis=0), pltpu.roll(nxt, k, axis=0))


def _scan_fwd(a, b, h_prev, row):
    for s in (1, 2, 4):
        a_s = jnp.where(row >= s, pltpu.roll(a, s, axis=0), 1.0)
        b_s = jnp.where(row >= s, pltpu.roll(b, s, axis=0), 0.0)
        b = a * b_s + b
        a = a * a_s
    return a * h_prev + b


def _scan_bwd(a_next, b, g_next, row):
    a = a_next
    for s in (1, 2, 4):
        k = SUBLANES - s
        a_s = jnp.where(row < k, pltpu.roll(a, k, axis=0), 1.0)
        b_s = jnp.where(row < k, pltpu.roll(b, k, axis=0), 0.0)
        b = a * b_s + b
        a = a * a_s
    return a * g_next + b


def _bcast_row(v, r):
    return jnp.broadcast_to(v[r:r + 1, :], v.shape)


def _chunks(n_rows, rc, body, init, reverse=False):
    n = n_rows // rc

    def step(i, carry):
        j = (n - 1 - i) if reverse else i
        return body(pl.multiple_of(j * rc, rc), carry)

    return lax.fori_loop(0, n, step, init)


def _adamw(w, g, m, v):
    m = ADAM_B1 * m + (1.0 - ADAM_B1) * g
    v = ADAM_B2 * v + (1.0 - ADAM_B2) * (g * g)
    m_hat = m / ADAM_BC1
    v_hat = v / ADAM_BC2
    delta = -ADAM_LR * (m_hat / (jnp.sqrt(v_hat) + ADAM_EPS) + ADAM_WD * w)
    return delta, m, v


def _mesh_pos():
    return lax.axis_index("x"), lax.axis_index("y"), lax.axis_index("c")


class _Gather:
    def __init__(self, blocks_of, send_sems, recv_sems, own_src=None):
        x, y, c = _mesh_pos()
        self.c = c
        self.me, self.sibling = (x, y, c), (x, y, 1 - c)
        self.chips = [(1 - x, y), (x, 1 - y), (1 - x, 1 - y)]
        self.blocks_of, self.send_sems, self.recv_sems = blocks_of, send_sems, recv_sems
        self.own_src = own_src

    def copy(self, a, k, block, to):
        src = self.blocks_of(a, *block)
        if block is self.me and self.own_src is not None:
            src = self.own_src[a]
        return pltpu.make_async_remote_copy(
            src_ref=src, dst_ref=self.blocks_of(a, *block),
            send_sem=self.send_sems.at[a * 7 + k], recv_sem=self.recv_sems.at[a * 7 + k],
            device_id=to, device_id_type=MESH)

    def start_own(self, a):
        self.copy(a, 0, self.me, self.sibling).start()
        for j, chip in enumerate(self.chips):
            self.copy(a, 1 + j, self.me, (*chip, self.c)).start()

    def wait_sibling(self, a):
        self.copy(a, 0, self.sibling, self.me).wait_recv()

    def wait_chip_and_pass_on(self, a, j):
        block = (*self.chips[j], self.c)
        self.copy(a, 1 + j, block, self.me).wait_recv()
        self.copy(a, 4 + j, block, self.sibling).start()

    def wait_passed_on(self, a, j):
        self.copy(a, 4 + j, (*self.chips[j], 1 - self.c), self.me).wait_recv()

    def wait_sends(self, a):
        self.copy(a, 0, self.me, self.sibling).wait_send()
        for j, chip in enumerate(self.chips):
            self.copy(a, 1 + j, self.me, (*chip, self.c)).wait_send()
            self.copy(a, 4 + j, (*chip, self.c), self.sibling).wait_send()

    def finish(self, a):
        for j in range(3):
            self.wait_chip_and_pass_on(a, j)
        self.wait_sibling(a)
        for j in range(3):
            self.wait_passed_on(a, j)
        self.wait_sends(a)


class _BalancedGather:
    def __init__(self, slot, send_sems, recv_sems, own_src):
        x, y, c = _mesh_pos()
        self.c = c
        self.me, self.sibling = (x, y, c), (x, y, 1 - c)
        self.chips = [(1 - x, y), (x, 1 - y), (1 - x, 1 - y)]
        self.slot, self.send_sems, self.recv_sems, self.own_src = slot, send_sems, recv_sems, own_src

    def half(self, a, block, which):
        ref = self.slot(a, *block)
        n = ref.shape[0] // 2
        return ref.at[pl.ds(which * n, n)]

    def copy(self, a, k, src, dst, to):
        return pltpu.make_async_remote_copy(
            src_ref=src, dst_ref=dst, send_sem=self.send_sems.at[a * 8 + k], recv_sem=self.recv_sems.at[a * 8 + k],
            device_id=to, device_id_type=MESH)

    def whole(self, a, k, block, to):
        src = self.own_src[a] if block is self.me else self.slot(a, *block)
        return self.copy(a, k, src, self.slot(a, *block), to)

    def halved(self, a, k, block, which, to):
        return self.copy(a, k, self.half(a, block, which), self.half(a, block, which), to)

    def on(self, chip):
        return (*self.chips[chip], self.c)

    def start_own(self, a):
        self.whole(a, 0, self.me, self.sibling).start()
        self.whole(a, 1, self.me, self.on(0)).start()
        self.whole(a, 2, self.me, self.on(1)).start()

    def start_own_staggered(self, a):
        self.whole(a, 0, self.me, self.sibling).start()
        for core, order in ((1, (0, 1)), (0, (1, 0))):
            @pl.when(self.c == core)
            def _(order=order):
                for j in order:
                    self.whole(a, 1 + j, self.me, self.on(j)).start()

    def wait_sibling(self, a):
        self.whole(a, 0, self.sibling, self.me).wait_recv()

    def on_neighbour(self, a, j):
        self.whole(a, 1 + j, self.on(j), self.me).wait_recv()
        self.halved(a, 3 + j, self.on(j), j, self.on(1 - j)).start()
        self.whole(a, 5 + j, self.on(j), self.sibling).start()

    def on_diagonal(self, a):
        self.halved(a, 3, self.on(2), 0, self.me).wait_recv()
        self.halved(a, 4, self.on(2), 1, self.me).wait_recv()
        self.whole(a, 7, self.on(2), self.sibling).start()

    def wait_passed_on(self, a, j):
        self.whole(a, 5 + j, (*self.chips[j], 1 - self.c), self.me).wait_recv()

    def wait_sends(self, a):
        self.whole(a, 0, self.me, self.sibling).wait_send()
        for j in range(2):
            self.whole(a, 1 + j, self.me, self.on(j)).wait_send()
            self.halved(a, 3 + j, self.on(j), j, self.on(1 - j)).wait_send()
        for j in range(3):
            self.whole(a, 5 + j, self.on(j), self.sibling).wait_send()


def _block_order():
    x, y, c = _mesh_pos()
    idx = lambda chip, core: 4 * chip[0] + 2 * chip[1] + core
    own, first, second, diag = (x, y), (x ^ c, y ^ (1 - c)), (x ^ (1 - c), y ^ c), (1 - x, 1 - y)
    order = [idx(own, c), idx(own, 1 - c), idx(first, c), idx(second, 1 - c), idx(second, c), idx(first, 1 - c),
             idx(diag, c), idx(diag, 1 - c)]
    return jnp.stack(order).astype(jnp.int32)


def _gather_project(x, w_in, w_out, conv_w, lru_conv_w, ln_g, w_a, w_i, vecs, tb, tw):
    t_len, d = x.shape
    nb = t_len // tb
    cols = w_in.shape[1]
    mc = min(512, t_len)
    conv_pack = jax.ShapeDtypeStruct((SUBLANES, conv_w.shape[1]), F32)
    srcs = (w_in, w_out, conv_pack)
    dts = (MM, MM, F32)
    n_vec = len(vecs)
    n_heads, hd, _ = w_a.shape
    per = tw // hd

    def body(order_ref, x_ref, win_ref, wout_ref, cw_ref, lw_ref, lng_ref, wa_ref, wi_ref, *refs):
        vec_refs = refs[:n_vec]
        (p_ref, xnt_ref, win_all, wout_all, cp_all, sp_ref, wat_ref, wit_ref, wint_ref,
         xnb, wall, st_out, st_cp, cp_vm, send_sems, recv_sems, cp_send, cp_recv, local_sems) = refs[n_vec:]
        i = pl.program_id(0)
        x_, y_, c_ = _mesh_pos()
        me = 4 * x_ + 2 * y_ + c_
        outs = (win_all, wout_all, cp_all)
        lands = (wall, wout_all, cp_all)
        stages = (wall.at[me], st_out, st_cp)
        gather = _BalancedGather(lambda a, px, py, pc: lands[a].at[4 * px + 2 * py + pc], send_sems, recv_sems, stages)
        small = _Gather(lambda a, px, py, pc: cp_all.at[4 * px + 2 * py + pc], cp_send, cp_recv, own_src=[st_cp])
        keep_own = [pltpu.make_async_copy(stages[a], outs[a].at[me], local_sems.at[a]) for a in range(3)]

        def keep(k):
            blk = order_ref[k]
            return pltpu.make_async_copy(wall.at[blk], win_all.at[blk], local_sems.at[2 + k])

        @pl.when(i == 0)
        def _():
            for a, src in enumerate((win_ref, wout_ref)):
                dst, rc = stages[a], 32

                def cast(r, carry, src=src, dst=dst):
                    dst[pl.ds(r, rc), :] = src[pl.ds(r, rc), :].astype(dst.dtype)
                    return carry

                _chunks(src.shape[0], rc, cast, 0)
                keep_own[a].start()
            gather.start_own_staggered(0)
            n_cw, n_lw = cw_ref.shape[0], lw_ref.shape[0]
            st_cp[...] = jnp.zeros_like(st_cp)
            st_cp[0:n_cw, :] = cw_ref[...]
            st_cp[n_cw:n_cw + n_lw, :] = lw_ref[...]
            keep_own[2].start()

        @pl.when(i < nb)
        def _():
            xv = x_ref[...]
            r0 = lax.rsqrt(jnp.mean(xv * xv, axis=-1, keepdims=True) + RMS_EPS)
            xn = xv * r0 * lng_ref[...]
            xnb[pl.ds(pl.multiple_of(i * tb, tb), tb), :] = xn.astype(MM)
            xnt_ref[...] = xn.T.astype(MM)

        def by_core(action):
            for core, (first, second) in ((1, (0, 1)), (0, (1, 0))):
                @pl.when(c_ == core)
                def _(first=first, second=second):
                    action(first, second)

        for k in range(N_DEV):
            @pl.when(i == nb + k)
            def _(k=k):
                if k == 1:
                    gather.wait_sibling(0)
                elif k == 2:
                    by_core(lambda first, second: gather.on_neighbour(0, first))
                    gather.start_own(1)
                    small.start_own(0)
                elif k == 3:
                    by_core(lambda first, second: gather.wait_passed_on(0, second))
                elif k == 4:
                    by_core(lambda first, second: gather.on_neighbour(0, second))
                elif k == 5:
                    by_core(lambda first, second: gather.wait_passed_on(0, first))
                    gather.on_neighbour(1, 0)
                    gather.on_neighbour(1, 1)
                elif k == 6:
                    gather.on_diagonal(0)
                elif k == 7:
                    gather.wait_passed_on(0, 2)
                    gather.on_diagonal(1)
                blk = order_ref[k]
                if k:
                    keep(k).start()

                def project(r, carry):
                    rows = pl.ds(r, mc)
                    p_ref[rows, :] = _dot(xnb[rows, :], wall[blk]).astype(MM)
                    return carry

                _chunks(t_len, mc, project, 0)
                for r in range(0, d, mc):
                    wint_ref[0, :, r:r + mc] = wall[blk, r:r + mc, :].astype(F32).T.astype(MM)
                if k == N_DEV - 1:
                    gather.wait_sends(0)
                    gather.wait_sibling(1)
                    for j in range(3):
                        gather.wait_passed_on(1, j)
                    gather.wait_sends(1)
                    small.finish(0)
                    for cp in keep_own + [keep(kk) for kk in range(1, N_DEV)]:
                        cp.wait()
                    load = pltpu.make_async_copy(cp_all, cp_vm, local_sems.at[N_DEV + 2])
                    load.start()
                    load.wait()
                    for r, ref in enumerate(vec_refs):
                        sp_ref[r, :] = ref[...]
                    sp_ref[n_vec:n_vec + SUBLANES, :] = jnp.concatenate([cp_vm[dev] for dev in range(N_DEV)], axis=1)
                    for src, dst in ((wa_ref, wat_ref), (wi_ref, wit_ref)):
                        dst[...] = jnp.zeros_like(dst)
                        for head in range(n_heads):
                            lo = (head % per) * hd
                            dst[head // per, lo:lo + hd, lo:lo + hd] = src[head].astype(MM)

    vm = pl.BlockSpec(memory_space=pltpu.VMEM)
    hbm = pl.BlockSpec(memory_space=pl.ANY)
    grid_spec = pltpu.PrefetchScalarGridSpec(
        num_scalar_prefetch=1, grid=(nb + N_DEV,),
        in_specs=[pl.BlockSpec((tb, d), lambda i, o: (jnp.minimum(i, nb - 1), 0))] + [vm] * (7 + n_vec),
        out_specs=(pl.BlockSpec((t_len, cols), lambda i, o: (0, o[jnp.maximum(i - nb, 0)])),
                   pl.BlockSpec((d, tb), lambda i, o: (0, jnp.minimum(i, nb - 1))), hbm, hbm, hbm,
                   pl.BlockSpec((SP_ROWS, d), lambda i, o: (0, 0)),
                   pl.BlockSpec((n_heads // per, tw, tw), lambda i, o: (0, 0, 0)),
                   pl.BlockSpec((n_heads // per, tw, tw), lambda i, o: (0, 0, 0)),
                   pl.BlockSpec((1, cols, d), lambda i, o: (o[jnp.maximum(i - nb, 0)], 0, 0))),
        scratch_shapes=[pltpu.VMEM((t_len, d), MM), pltpu.VMEM((N_DEV,) + w_in.shape, MM),
                        pltpu.VMEM(w_out.shape, MM), pltpu.VMEM(conv_pack.shape, F32),
                        pltpu.VMEM((N_DEV,) + conv_pack.shape, F32),
                        pltpu.SemaphoreType.DMA((16,)), pltpu.SemaphoreType.DMA((16,)),
                        pltpu.SemaphoreType.DMA((7,)), pltpu.SemaphoreType.DMA((7,)), pltpu.SemaphoreType.DMA((N_DEV + 3,))])
    return pl.pallas_call(
        body, name="gather_project", grid_spec=grid_spec,
        out_shape=(jax.ShapeDtypeStruct((t_len, N_DEV * cols), MM),
                   jax.ShapeDtypeStruct((d, t_len), MM))
                  + tuple(jax.ShapeDtypeStruct((N_DEV,) + s.shape, dt) for s, dt in zip(srcs, dts))
                  + (jax.ShapeDtypeStruct((SP_ROWS, d), F32),)
                  + (jax.ShapeDtypeStruct((n_heads // per, tw, tw), MM),) * 2
                  + (jax.ShapeDtypeStruct((N_DEV, cols, d), MM),),
        compiler_params=_params(dimension_semantics=("arbitrary",)),
    )(_block_order(), x, w_in, w_out, conv_w, lru_conv_w, ln_g, w_a, w_i, *vecs)


def _forward(x, tgt, p, wout, wa_t, wi_t, sp, ones_c, ones_l, tb):
    t_len, d = x.shape
    nb = t_len // tb
    n_tiles, tw = wa_t.shape[0], wa_t.shape[1]
    hd_c, hd_l = d // N_CONV_HEADS, d // N_LRU_HEADS
    s8 = SUBLANES

    def body(x_ref, tgt_ref, p_ref, wout_ref, wa_ref, wi_ref, sp_ref, oc_ref, ol_ref,
             h_ref, dh_ref, dhb_ref, acc_ref, yc, czs, u, pa, pi,
             rcf, rlf, ybuf, tail_z, tail_xl, hcar):
        i = pl.program_id(0)
        row = _row_iota(d)

        @pl.when(i == 0)
        def _():
            tail_z[...] = jnp.zeros_like(tail_z)
            tail_xl[...] = jnp.zeros_like(tail_xl)
            hcar[...] = jnp.zeros_like(hcar)
            acc_ref[...] = jnp.zeros_like(acc_ref)

        def spr(r):
            return sp_ref[r:r + 1, :]

        def proj(rows, seg):
            return p_ref[rows, seg * d:(seg + 1) * d].astype(F32)

        w0, w1, w2 = spr(SP_CONV_W), spr(SP_CONV_W + 1), spr(SP_CONV_W + 2)
        l0, l1, l2, l3 = spr(SP_LRU_W), spr(SP_LRU_W + 1), spr(SP_LRU_W + 2), spr(SP_LRU_W + 3)
        lb = spr(SP_LRU_B)

        def convs(r, carry):
            zp, xp = carry
            rows16 = pl.ds(r, 2 * s8)
            bg16, xl16 = proj(rows16, P_B), proj(rows16, P_XL)
            z16 = proj(rows16, P_C) * proj(rows16, P_XC)
            for j in range(2):
                rows, sub = pl.ds(r + j * s8, s8), slice(j * s8, (j + 1) * s8)
                z, xl = z16[sub], xl16[sub]
                cz = w0 * _shift_down(z, zp, 2, row) + w1 * _shift_down(z, zp, 1, row) + w2 * z
                czs[rows, :] = cz
                yc[rows, :] = bg16[sub] * cz
                u[rows, :] = (l0 * _shift_down(xl, xp, 3, row) + l1 * _shift_down(xl, xp, 2, row)
                              + l2 * _shift_down(xl, xp, 1, row) + l3 * xl + lb)
                zp, xp = z, xl
            return zp, xp

        z_last, xl_last = _chunks(tb, 2 * s8, convs, (tail_z[...], tail_xl[...]))
        tail_z[...] = z_last
        tail_xl[...] = xl_last

        ub = u[...].astype(MM)
        for k in range(n_tiles):
            sl = slice(k * tw, (k + 1) * tw)
            pa[:, sl] = _dot(ub[:, sl], wa_ref[k])
            pi[:, sl] = _dot(ub[:, sl], wi_ref[k])
        rcf[...] = _head_rstd(yc[...], oc_ref[...], hd_c)

        c8 = RG_LRU_C * _log_sigmoid(spr(SP_LAM))
        b_a, b_i = spr(SP_B_A), spr(SP_B_I)

        def lru(r, hp):
            rows = pl.ds(r, SUBLANES)
            ra = _sigmoid(pa[rows, :] + b_a)
            ii = _sigmoid(pi[rows, :] + b_i)
            pa[rows, :] = ra
            pi[rows, :] = ii
            la = ra * c8
            a = jnp.exp(la)
            mult = jnp.sqrt(_lru_input_scale_sq(la, a))
            h = _scan_fwd(a, mult * (ii * u[rows, :]), hp, row)
            h_ref[rows, :] = h
            return _bcast_row(h, SUBLANES - 1)

        hcar[...] = _chunks(tb, SUBLANES, lru, hcar[...])
        rlf[...] = _head_rstd(h_ref[...], ol_ref[...], hd_l)

        g_c, g_l = spr(SP_CONV_G), spr(SP_LRU_G)

        def gate(r, carry):
            rows = pl.ds(r, 2 * s8)
            gc, gl = proj(rows, P_GC), proj(rows, P_GL)
            ybuf[rows, 0:d] = (yc[rows, :] * rcf[rows, :] * g_c * (gc * _sigmoid(gc))).astype(MM)
            ybuf[rows, d:2 * d] = (h_ref[rows, :] * rlf[rows, :] * g_l * (gl * _sigmoid(gl))).astype(MM)
            return carry

        _chunks(tb, 2 * s8, gate, 0)

        hres = x_ref[...] + _dot(ybuf[...], wout_ref[...])
        rf = lax.rsqrt(jnp.mean(hres * hres, axis=-1, keepdims=True) + RMS_EPS)
        hn = hres * rf
        fg = spr(SP_FINAL_G)
        err = hn * fg - tgt_ref[...]
        dout = err * (1.0 / d)
        acc_ref[0:SUBLANES, :] += (err * err).reshape(tb // SUBLANES, SUBLANES, d).sum(axis=0)
        acc_ref[SUBLANES:2 * SUBLANES, :] += (dout * hn).reshape(tb // SUBLANES, SUBLANES, d).sum(axis=0)
        gd = dout * fg
        dhres = rf * (gd - hn * jnp.mean(gd * hn, axis=-1, keepdims=True))
        dh_ref[...] = dhres
        dhb_ref[...] = dhres.astype(MM)

    vm = pl.BlockSpec(memory_space=pltpu.VMEM)
    blk = lambda w: pl.BlockSpec((tb, w), lambda i: (i, 0))
    buf = pltpu.VMEM((tb, d), F32)
    car = pltpu.VMEM((SUBLANES, d), F32)
    return pl.pallas_call(
        body, name="forward", grid=(nb,),
        in_specs=[blk(d), blk(d), blk(6 * d), vm, vm, vm, vm, vm, vm],
        out_specs=(blk(d), blk(d), blk(d), pl.BlockSpec((2 * SUBLANES, d), lambda i: (0, 0))) + (blk(d),) * 5,
        out_shape=(jax.ShapeDtypeStruct((t_len, d), F32),
                   jax.ShapeDtypeStruct((t_len, d), F32),
                   jax.ShapeDtypeStruct((t_len, d), MM),
                   jax.ShapeDtypeStruct((2 * SUBLANES, d), F32))
                  + (jax.ShapeDtypeStruct((t_len, d), F32),) * 5,
        scratch_shapes=[buf] * 2 + [pltpu.VMEM((tb, 2 * d), MM), car, car, car],
        compiler_params=_params(dimension_semantics=("arbitrary",)),
    )(x, tgt, p, wout, wa_t, wi_t, sp, ones_c, ones_l)


def _backward(p, h, dh, saved, facc, wout, wa_t, wi_t, sp, ones_c, ones_l, tb):
    t_len, d = h.shape
    nb = t_len // tb
    n_tiles, tw = wa_t.shape[0], wa_t.shape[1]
    hd_c, hd_l = d // N_CONV_HEADS, d // N_LRU_HEADS
    g_rows = 2 * n_tiles * hd_l
    s8 = SUBLANES

    def body(p_ref, h_ref, hhalo_ref, dh_ref, yc, czs, u, ra_ref, ii_ref, facc_ref,
             wout_ref, wa_ref, wi_ref, sp_ref, oc_ref, ol_ref,
             dp_ref, yt_ref, slab_v, slab_g,
             hh, dy, ybuf, rcf, rlf, qc, ql, dyc_hat, dyl_hat, dpa, dpi, du, gwa_ref, gwi_ref, acc_ref,
             car_dcz, car_a, car_g, car_du):
        i = pl.program_id(0)
        blk_idx = nb - 1 - i
        row = _row_iota(d)

        @pl.when(i == 0)
        def _():
            for ref in (car_dcz, car_a, car_g, car_du, gwa_ref, gwi_ref, acc_ref):
                ref[...] = jnp.zeros_like(ref)

        def spr(r):
            return sp_ref[r:r + 1, :]

        def proj(rows, seg):
            return p_ref[rows, seg * d:(seg + 1) * d].astype(F32)

        def put(rows, seg, halves):
            dp_ref[rows, seg * d:(seg + 1) * d] = jnp.concatenate(halves, axis=0).astype(MM)

        def acc_add(group, val):
            acc_ref[group * s8:(group + 1) * s8, :] += val

        live = jnp.where(blk_idx > 0, 1.0, 0.0).astype(F32)
        hh[0:s8, :] = hhalo_ref[...] * live
        hh[s8:, :] = h_ref[...]

        dy[...] = _dot_nt(dh_ref[...].astype(MM), wout_ref[...])

        w0, w1, w2 = spr(SP_CONV_W), spr(SP_CONV_W + 1), spr(SP_CONV_W + 2)
        l0, l1, l2, l3 = spr(SP_LRU_W), spr(SP_LRU_W + 1), spr(SP_LRU_W + 2), spr(SP_LRU_W + 3)

        rcf[...] = _head_rstd(yc[...], oc_ref[...], hd_c)
        rlf[...] = _head_rstd(h_ref[...], ol_ref[...], hd_l)

        g_c, g_l = spr(SP_CONV_G), spr(SP_LRU_G)

        def gates(r, carry):
            rows = pl.ds(r, 2 * s8)
            for (seg, off_y, src, rstd, gain, q, dhat, grp) in (
                    (P_GC, 0, yc, rcf, g_c, qc, dyc_hat, A_CONV_G),
                    (P_GL, d, h_ref, rlf, g_l, ql, dyl_hat, A_LRU_G)):
                gt = proj(rows, seg)
                sg = _sigmoid(gt)
                silu = gt * sg
                yhat = src[rows, :] * rstd[rows, :]
                nrm = yhat * gain
                ybuf[rows, off_y:off_y + d] = nrm * silu
                dout = dy[rows, off_y:off_y + d]
                dnrm = dout * silu
                dp_ref[rows, seg * d:(seg + 1) * d] = (dout * nrm * (sg * (1.0 + gt * (1.0 - sg)))).astype(MM)
                dg = dnrm * yhat
                acc_add(grp, dg[0:s8] + dg[s8:])
                dh_ = dnrm * gain
                dhat[rows, :] = dh_
                q[rows, :] = dh_ * yhat
            return carry

        _chunks(tb, 2 * s8, gates, 0)

        qc[...] = _head_sums(qc[...], oc_ref[...]) * (1.0 / hd_c)
        ql[...] = _head_sums(ql[...], ol_ref[...]) * (1.0 / hd_l)
        yt_ref[...] = ybuf[...].T.astype(MM)

        c8 = RG_LRU_C * _log_sigmoid(spr(SP_LAM))

        def conv_mixer(r, dcz_n):
            rows16 = pl.ds(r, 2 * s8)
            bg16, cg16, xc16 = proj(rows16, P_B), proj(rows16, P_C), proj(rows16, P_XC)
            z16 = cg16 * xc16
            d_b, d_c, d_x = [None, None], [None, None], [None, None]
            for j in (1, 0):
                rows, sub = pl.ds(r + j * s8, s8), slice(j * s8, (j + 1) * s8)
                rstd = rcf[rows, :]
                yhat = yc[rows, :] * rstd
                dyc = rstd * (dyc_hat[rows, :] - yhat * qc[rows, :])
                d_b[j] = dyc * czs[rows, :]
                dcz = dyc * bg16[sub]
                up1, up2 = _shift_up(dcz, dcz_n, 1, row), _shift_up(dcz, dcz_n, 2, row)
                dz = w2 * dcz + w1 * up1 + w0 * up2
                d_c[j] = dz * xc16[sub]
                d_x[j] = dz * cg16[sub]
                z = z16[sub]
                acc_add(A_CONV_W, up2 * z)
                acc_add(A_CONV_W + 1, up1 * z)
                acc_add(A_CONV_W + 2, dcz * z)
                dcz_n = dcz
            put(rows16, P_B, d_b)
            put(rows16, P_C, d_c)
            put(rows16, P_XC, d_x)
            return dcz_n

        car_dcz[...] = _chunks(tb, 2 * s8, conv_mixer, car_dcz[...], reverse=True)

        def lru_mixer(r, carry):
            a_n, g_n = carry
            for j in (1, 0):
                rows = pl.ds(r + j * s8, s8)
                rstd = rlf[rows, :]
                hcur = hh[pl.ds(r + (j + 1) * s8, s8), :]
                hhat = hcur * rstd
                dh_out = rstd * (dyl_hat[rows, :] - hhat * ql[rows, :])
                ra = ra_ref[rows, :]
                la = ra * c8
                a = jnp.exp(la)
                g = _scan_bwd(_shift_up(a, a_n, 1, row), dh_out, g_n, row)
                da = g * _shift_down(hcur, hh[pl.ds(r + j * s8, s8), :], 1, row)
                ii = ii_ref[rows, :]
                uu = u[rows, :]
                mult_sq = _lru_input_scale_sq(la, a)
                inv_mult = lax.rsqrt(mult_sq)
                dmult = g * (ii * uu)
                ds = g * (mult_sq * inv_mult)
                dla = a * (da - dmult * a * inv_mult)
                acc_add(A_LAM, dla * ra)
                dpa_ = dla * c8 * ra * (1.0 - ra)
                dpi_ = ds * uu * ii * (1.0 - ii)
                acc_add(A_B_A, dpa_)
                acc_add(A_B_I, dpi_)
                dpa[rows, :] = dpa_
                dpi[rows, :] = dpi_
                du[rows, :] = ds * ii
                a_n, g_n = a, _bcast_row(g, 0)
            return a_n, g_n

        a_f, g_f = _chunks(tb, 2 * s8, lru_mixer, (car_a[...], car_g[...]), reverse=True)
        car_a[...] = a_f
        car_g[...] = g_f

        dpab = dpa[...].astype(MM)
        dpib = dpi[...].astype(MM)
        for k in range(n_tiles):
            sl = slice(k * tw, (k + 1) * tw)
            du[:, sl] += _dot_nt(dpab[:, sl], wa_ref[k]) + _dot_nt(dpib[:, sl], wi_ref[k])
            ut = u[:, sl].T.astype(MM)
            gwa_ref[k] += _dot(ut, dpab[:, sl])
            gwi_ref[k] += _dot(ut, dpib[:, sl])

        def lru_conv(r, du_n):
            rows16 = pl.ds(r, 2 * s8)
            xl16 = proj(rows16, P_XL)
            d_xl = [None, None]
            for j in (1, 0):
                rows, sub = pl.ds(r + j * s8, s8), slice(j * s8, (j + 1) * s8)
                dut = du[rows, :]
                up1, up2, up3 = (_shift_up(dut, du_n, s, row) for s in (1, 2, 3))
                d_xl[j] = l3 * dut + l2 * up1 + l1 * up2 + l0 * up3
                xl = xl16[sub]
                acc_add(A_LRU_W, up3 * xl)
                acc_add(A_LRU_W + 1, up2 * xl)
                acc_add(A_LRU_W + 2, up1 * xl)
                acc_add(A_LRU_W + 3, dut * xl)
                acc_add(A_LRU_B, dut)
                du_n = dut
            put(rows16, P_XL, d_xl)
            return du_n

        car_du[...] = _chunks(tb, 2 * s8, lru_conv, car_du[...], reverse=True)

        @pl.when(i == nb - 1)
        def _():
            def rowsum(ref, group):
                return jnp.sum(ref[group * s8:(group + 1) * s8, :], axis=0, keepdims=True)

            slab_v[...] = jnp.zeros_like(slab_v)
            loss = jnp.sum(rowsum(facc_ref, 0), axis=1, keepdims=True) * (0.5 / d)
            rows = {SL_LOSS: jnp.broadcast_to(loss, (1, d)), SL_FINAL_G: rowsum(facc_ref, 1),
                    SL_LRU_B: rowsum(acc_ref, A_LRU_B), SL_B_A: rowsum(acc_ref, A_B_A), SL_B_I: rowsum(acc_ref, A_B_I),
                    SL_LAM: rowsum(acc_ref, A_LAM), SL_CONV_G: rowsum(acc_ref, A_CONV_G), SL_LRU_G: rowsum(acc_ref, A_LRU_G)}
            for k in range(3):
                rows[SL_CONV_W + k] = rowsum(acc_ref, A_CONV_W + k)
            for k in range(4):
                rows[SL_LRU_W + k] = rowsum(acc_ref, A_LRU_W + k)
            for r, val in rows.items():
                slab_v[r:r + 1, :] = val
            head_of_lane = lax.broadcasted_iota(jnp.int32, (hd_l, tw), 1) // hd_l
            for mtx, g_ref in enumerate((gwa_ref, gwi_ref)):
                for k in range(n_tiles):
                    packed = jnp.zeros((hd_l, tw), F32)
                    for a in range(tw // hd_l):
                        packed = jnp.where(head_of_lane == a, g_ref[k, a * hd_l:(a + 1) * hd_l, :], packed)
                    slab_g[(mtx * n_tiles + k) * hd_l:(mtx * n_tiles + k + 1) * hd_l, :] = packed.astype(MM)

    vm = pl.BlockSpec(memory_space=pltpu.VMEM)
    rev = lambda w: pl.BlockSpec((tb, w), lambda i: (nb - 1 - i, 0))
    halo = lambda rows, w: pl.BlockSpec((rows, w), lambda i: (jnp.maximum((nb - 1 - i) * (tb // rows) - 1, 0), 0))
    const = lambda shape: pl.BlockSpec(shape, lambda i: (0,) * len(shape))
    buf = lambda w: pltpu.VMEM((tb, w), F32)
    car = pltpu.VMEM((SUBLANES, d), F32)
    return pl.pallas_call(
        body, name="backward", grid=(nb,),
        in_specs=[rev(6 * d), rev(d), halo(SUBLANES, d), rev(d)] + [rev(d)] * 5 + [vm, vm, vm, vm, vm, vm, vm],
        out_specs=(rev(6 * d), pl.BlockSpec((2 * d, tb), lambda i: (0, nb - 1 - i)),
                   const((SL_ROWS, d)), const((g_rows, tw))),
        out_shape=(jax.ShapeDtypeStruct((t_len, 6 * d), MM),
                   jax.ShapeDtypeStruct((2 * d, t_len), MM),
                   jax.ShapeDtypeStruct((SL_ROWS, d), F32),
                   jax.ShapeDtypeStruct((g_rows, tw), MM)),
        scratch_shapes=[pltpu.VMEM((SUBLANES + tb, d), F32), buf(2 * d), buf(2 * d)] + [buf(d)] * 9
                       + [pltpu.VMEM((n_tiles, tw, tw), F32), pltpu.VMEM((n_tiles, tw, tw), F32),
                          pltpu.VMEM((A_GROUPS * SUBLANES, d), F32), car, car, car, car],
        compiler_params=_params(dimension_semantics=("arbitrary",)),
    )(p, h, h, dh, *saved, facc, wout, wa_t, wi_t, sp, ones_c, ones_l)


def _input_grad(dp, win_all, x, dh, sp, part, tb):
    t_len, d = x.shape
    nb = t_len // tb
    cols = win_all.shape[1]
    mid =min(nb - 1, (3 * nb) // 8)
    late = min(mid, nb // 4)
    rc = 32

    def body(dp_ref, win_ref, x_ref, dh_ref, sp_ref, part_ref, gx_ref, ln_ref, direct, relayed,
             send_sems, recv_sems, local_sems, acc_ref, ln_all, ln_send, ln_recv, mine, theirs):
        i = pl.program_id(0)
        x_, y_, c_ = _mesh_pos()
        first, second = 1 - c_, c_
        nbr1 = (x_ ^ c_, y_ ^ (1 - c_), c_)
        nbr2 = (x_ ^ (1 - c_), y_ ^ c_, c_)

        def remote(src, dst, k, to):
            return pltpu.make_async_remote_copy(src_ref=src, dst_ref=dst, send_sem=send_sems.at[k], recv_sem=recv_sems.at[k],
                                                device_id=to, device_id_type=MESH)

        to_first = [remote(part_ref.at[first], direct, 0, nbr1), remote(part_ref.at[2], theirs, 1, nbr1)]
        to_second = remote(theirs, relayed, 2, nbr2)
        load_mine = pltpu.make_async_copy(part_ref.at[second], mine, local_sems.at[0])

        @pl.when(i == 0)
        def _():
            acc_ref[...] = jnp.zeros_like(acc_ref)
            to_first[1].start()
            load_mine.start()

        @pl.when(i == late)
        def _():
            to_first[0].start()

        dxn = _dot(dp_ref[:, 0:cols], win_ref[0])
        for j in range(1, N_DEV):
            dxn += _dot(dp_ref[:, j * cols:(j + 1) * cols], win_ref[j])
        xv = x_ref[...]
        r0 = lax.rsqrt(jnp.mean(xv * xv, axis=-1, keepdims=True) + RMS_EPS)
        xhat = xv * r0
        acc_ref[...] += (dxn * xhat).reshape(tb // SUBLANES, SUBLANES, d).sum(axis=0)
        dxh = dxn * sp_ref[SP_LN_G:SP_LN_G + 1, :]
        gx_ref[...] = dh_ref[...] + r0 * (dxh - xhat * jnp.mean(dxh * xhat, axis=-1, keepdims=True))

        @pl.when(i == mid)
        def _():
            to_first[1].wait_recv()
            load_mine.wait()

            def add(r, carry):
                rows = pl.ds(r, rc)
                theirs[rows, :] = (mine[rows, :].astype(F32) + theirs[rows, :].astype(F32)).astype(MM)
                return carry

            _chunks(mine.shape[0], rc, add, 0)
            to_second.start()

        @pl.when(i == nb - 1)
        def _():
            to_first[0].wait_recv()
            to_second.wait_recv()
            for cp in to_first + [to_second]:
                cp.wait_send()
            ln_all[4 * x_ + 2 * y_ + c_] = jnp.broadcast_to(jnp.sum(acc_ref[...], axis=0, keepdims=True), acc_ref.shape)
            gather = _Gather(lambda a, px, py, pc: ln_all.at[4 * px + 2 * py + pc], ln_send, ln_recv)
            gather.start_own(0)
            gather.finish(0)
            total = ln_all[0]
            for dev in range(1, N_DEV):
                total = total + ln_all[dev]
            ln_ref[...] = total

    vm = pl.BlockSpec(memory_space=pltpu.VMEM)
    hbm = pl.BlockSpec(memory_space=pl.ANY)
    blk = lambda w: pl.BlockSpec((tb, w), lambda i: (i, 0))
    landed = jax.ShapeDtypeStruct(part.shape[1:], part.dtype)
    outs = pl.pallas_call(
        body, name="input_grad", grid=(nb,),
        in_specs=[blk(6 * d), vm, blk(d), blk(d), vm, hbm],
        out_specs=(blk(d), pl.BlockSpec((SUBLANES, d), lambda i: (0, 0)), hbm, hbm),
        out_shape=(jax.ShapeDtypeStruct((t_len, d), F32), jax.ShapeDtypeStruct((SUBLANES, d), F32), landed, landed),
        scratch_shapes=[pltpu.SemaphoreType.DMA((3,)), pltpu.SemaphoreType.DMA((3,)), pltpu.SemaphoreType.DMA((1,)),
                        pltpu.VMEM((SUBLANES, d), F32), pltpu.VMEM((N_DEV, SUBLANES, d), F32),
                        pltpu.SemaphoreType.DMA((7,)), pltpu.SemaphoreType.DMA((7,)),
                        pltpu.VMEM(part.shape[1:], MM), pltpu.VMEM(part.shape[1:], MM)],
        compiler_params=_params(dimension_semantics=("arbitrary",)),
    )(dp, win_all, x, dh, sp, part)
    return outs[0], outs[1], (outs[2], outs[3])


_CHIP_RELATIONS = [(0, 0), (1, 0), (0, 1), (1, 1)]


def _related_block(k, core):
    x, y, _ = _mesh_pos()
    fx, fy = _CHIP_RELATIONS[k]
    return 4 * (x ^ fx) + 2 * (y ^ fy) + core


class _ChipExchange:
    def __init__(self, part_refs, land_refs, send_sems, recv_sems):
        self.part_refs, self.land_refs, self.send_sems, self.recv_sems = part_refs, land_refs, send_sems, recv_sems

    def copies(self):
        x, y, c = _mesh_pos()
        for a in range(len(self.part_refs)):
            for k in (1, 2, 3):
                fx, fy = _CHIP_RELATIONS[k]
                yield pltpu.make_async_remote_copy(
                    src_ref=self.part_refs[a].at[k - 1], dst_ref=self.land_refs[a].at[k - 1],
                    send_sem=self.send_sems.at[3 * a + k - 1], recv_sem=self.recv_sems.at[3 * a + k - 1],
                    device_id=(x ^ fx, y ^ fy, c), device_id_type=MESH)

    def start(self):
        for cp in self.copies():
            cp.start()

    def finish(self):
        for cp in self.copies():
            cp.wait_recv()
        for cp in self.copies():
            cp.wait_send()


def _weight_grad_stage1(name, blk_shape, n_split, operands, in_specs, product, riders=(), slabs=()):
    n_rows, n_cols = blk_shape
    rs = n_rows // n_split
    rc = 32
    n_in, n_ride, n_slab = len(operands), len(riders), len(slabs)
    _, _, c = _mesh_pos()
    order = jnp.stack([_related_block(k, 1 - c) for k in range(4)]
                      + [_related_block(k, c) for k in (1, 2, 3, 0)]).astype(jnp.int32)

    def body(order_ref, *refs):
        ins = refs[:n_in]
        ride_in = refs[n_in:n_in + n_ride]
        slab_in = refs[n_in + n_ride:n_in + n_ride + n_slab]
        n_op = n_in + n_ride + n_slab
        part_ref, own_ref = refs[n_op:n_op + 2]
        ride_out = refs[n_op + 2:n_op + 2 + n_ride]
        gathered = refs[n_op + 2 + n_ride:n_op + 2 + n_ride + n_slab]
        (gbuf, sendbuf, from_sib, send_sems, recv_sems, ride_send, ride_recv,
         slab_send, slab_recv, slab_local) = refs[n_op + 2 + n_ride + n_slab:]
        exchange = _ChipExchange(ride_in, ride_out, ride_send, ride_recv)
        s = pl.program_id(0)
        x, y, c = _mesh_pos()
        me = 4 * x + 2 * y + c
        gather = _BalancedGather(lambda a, px, py, pc: gathered[a].at[4 * px + 2 * py + pc], slab_send, slab_recv, slab_in)
        keep_own = [pltpu.make_async_copy(slab_in[a], gathered[a].at[me], slab_local.at[a]) for a in range(n_slab)]

        def to_sibling(k):
            return pltpu.make_async_remote_copy(
                src_ref=sendbuf.at[k], dst_ref=from_sib.at[k], send_sem=send_sems.at[k], recv_sem=recv_sems.at[k],
                device_id=(x, y, 1 - c), device_id_type=MESH)

        @pl.when(s == 0)
        def _():
            exchange.start()
            for a in range(n_slab):
                gather.start_own(a)
                keep_own[a].start()

        @pl.when(s == 5)
        def _():
            for a in range(n_slab):
                gather.on_neighbour(a, 0)
                gather.on_neighbour(a, 1)

        @pl.when(s == 7)
        def _():
            for a in range(n_slab):
                gather.on_diagonal(a)

        for h in range(n_split):
            gbuf[h * rs:(h + 1) * rs, :] = product(ins, h)

        @pl.when(s < 4)
        def _():
            def narrow(r, carry):
                sendbuf[s, pl.ds(r, rc), :] = gbuf[pl.ds(r, rc), :].astype(MM)
                return carry

            _chunks(n_rows, rc, narrow, 0)
            to_sibling(s).start()

        @pl.when(s >= 4)
        def _():
            k = jnp.where(s == 7, 0, s - 3)
            to_sibling(k).wait_recv()

            @pl.when(s < 7)
            def _():
                def add(r, carry):
                    rows = pl.ds(r, rc)
                    part_ref[0, rows, :] = (gbuf[rows, :] + from_sib[k, rows, :].astype(F32)).astype(MM)
                    return carry

                _chunks(n_rows, rc, add, 0)

            @pl.when(s == 7)
            def _():
                def add(r, carry):
                    rows = pl.ds(r, rc)
                    own_ref[rows, :] = gbuf[rows, :] + from_sib[0, rows, :].astype(F32)
                    return carry

                _chunks(n_rows, rc, add, 0)
                for kk in range(4):
                    to_sibling(kk).wait_send()
                exchange.finish()
                for a in range(n_slab):
                    gather.wait_sibling(a)
                    for j in range(3):
                        gather.wait_passed_on(a, j)
                    gather.wait_sends(a)
                    keep_own[a].wait()

    hbm = pl.BlockSpec(memory_space=pl.ANY)
    grid_spec = pltpu.PrefetchScalarGridSpec(
        num_scalar_prefetch=1, grid=(N_DEV,), in_specs=list(in_specs) + [hbm] * (n_ride + n_slab),
        out_specs=(pl.BlockSpec((1, n_rows, n_cols), lambda s, o: (jnp.clip(s - 4, 0, 2), 0, 0)),
                   pl.BlockSpec((n_rows, n_cols), lambda s, o: (0, 0))) + (hbm,) * (n_ride + n_slab),
        scratch_shapes=[pltpu.VMEM((n_rows, n_cols), F32), pltpu.VMEM((4, n_rows, n_cols), MM),
                        pltpu.VMEM((4, n_rows, n_cols), MM),
                        pltpu.SemaphoreType.DMA((4,)), pltpu.SemaphoreType.DMA((4,)),
                        pltpu.SemaphoreType.DMA((max(3 * n_ride, 1),)), pltpu.SemaphoreType.DMA((max(3 * n_ride, 1),)),
                        pltpu.SemaphoreType.DMA((max(8 * n_slab, 1),)), pltpu.SemaphoreType.DMA((max(8 * n_slab, 1),)),
                        pltpu.SemaphoreType.DMA((max(n_slab, 1),))])
    outs = pl.pallas_call(
        body, name=name, grid_spec=grid_spec,
        out_shape=(jax.ShapeDtypeStruct((3, n_rows, n_cols), MM), jax.ShapeDtypeStruct((n_rows, n_cols), F32))
                  + tuple(jax.ShapeDtypeStruct(p.shape, p.dtype) for p in riders)
                  + tuple(jax.ShapeDtypeStruct((N_DEV,) + a.shape, a.dtype) for a in slabs),
        compiler_params=_params(dimension_semantics=("arbitrary",)),
    )(order, *operands, *riders, *slabs)
    return outs[0], outs[1], outs[2:2 + n_ride], outs[2 + n_ride:]


def _weight_grad_in(xnt, dp, riders, slabs):
    d, t_len = xnt.shape
    cols = dp.shape[1] // N_DEV
    half = d // 2
    return _weight_grad_stage1(
        "weight_grad_in", (d, cols), 2, (xnt, dp),
        [pl.BlockSpec(memory_space=pltpu.VMEM), pl.BlockSpec((t_len, cols), lambda s, o: (0, o[s]))],
        lambda refs, h: _dot(refs[0][h * half:(h + 1) * half, :], refs[1][...]), riders, slabs)


def _weight_grad_out(yt, dhb):
    d2, t_len = yt.shape
    d = dhb.shape[1]
    rows = d2 // N_DEV
    return _weight_grad_stage1(
        "weight_grad_out", (rows, d), 1, (yt, dhb),
        [pl.BlockSpec((rows, t_len), lambda s, o: (o[s], 0)), pl.BlockSpec(memory_space=pltpu.VMEM)],
        lambda refs, h: _dot(refs[0][...], refs[1][...]))


def _update_shard(own, others, w, m, v, name):
    n_rows, n_cols = w.shape
    rb = min(256, n_rows)
    n_other = len(others)

    def body(own_ref, *refs):
        other_refs = refs[:n_other]
        w_ref, m_ref, v_ref, grad_ref, delta_ref, mo_ref, vo_ref = refs[n_other:]
        g = own_ref[...]
        for ref in other_refs:
            for k in range(ref.shape[0] if len(ref.shape) == 3 else 1):
                g = g + (ref[k] if len(ref.shape) == 3 else ref[...]).astype(F32)
        delta, m_new, v_new = _adamw(w_ref[...], g, m_ref[...], v_ref[...])
        grad_ref[...] = g
        delta_ref[...] = delta
        mo_ref[...] = m_new
        vo_ref[...] = v_new

    blk = pl.BlockSpec((rb, n_cols), lambda i: (i, 0))
    stacked = lambda n: pl.BlockSpec((n, rb, n_cols), lambda i: (0, i, 0))
    out = jax.ShapeDtypeStruct((n_rows, n_cols), F32)
    return pl.pallas_call(
        body, name=name, grid=(n_rows // rb,),
        in_specs=[blk] + [stacked(o.shape[0]) if o.ndim == 3 else blk for o in others] + [blk, blk, blk],
        out_specs=(blk, blk, blk, blk), out_shape=(out, out, out, out),
        compiler_params=_params(dimension_semantics=("arbitrary",)),
    )(own, *others, w, m, v)


def _small_update(gat_v, gat_g, ln_tot, vec_w, vec_m, vec_v, gates, convs):
    n_vec = len(vec_w)
    n_heads, hd, _ = gates[0].shape
    tw = gat_g.shape[2]
    s8 = SUBLANES
    per = tw // hd
    n_tiles = n_heads // per
    cc = convs[0].shape[1]
    n_in = 3 + 3 * n_vec + 12

    def body(*refs):
        gv_ref, gg_ref, ln_ref = refs[:3]
        w_refs, m_refs, v_refs = (refs[3 + j * n_vec:3 + (j + 1) * n_vec] for j in range(3))
        gate_refs = refs[3 + 3 * n_vec:3 + 3 * n_vec + 6]
        conv_refs = refs[3 + 3 * n_vec + 6:n_in]
        loss_o = refs[n_in]
        kinds = [refs[n_in + 1 + j * (n_vec + 4):n_in + 1 + (j + 1) * (n_vec + 4)] for j in range(4)]
        tv, tg = refs[n_in + 1 + 4 * (n_vec + 4):]
        x, y, c = _mesh_pos()
        me = 4 * x + 2 * y + c

        def emit(k_out, w, g, m, v):
            delta, m_new, v_new = _adamw(w, g, m, v)
            for ref, val in zip(k_out, (g, delta, m_new, v_new)):
                ref[...] = val

        total = gv_ref[0]
        for dev in range(1, N_DEV):
            total = total + gv_ref[dev]
        tv[...] = total
        tv[SL_LN_G:SL_LN_G + 1, :] = ln_ref[0:1, :]

        def sum_gates(r, carry):
            rows = pl.ds(r, 2 * s8)
            part = gg_ref[0, rows, :].astype(F32)
            for dev in range(1, N_DEV):
                part = part + gg_ref[dev, rows, :].astype(F32)
            tg[rows, :] = part
            return carry

        _chunks(tg.shape[0], 2 * s8, sum_gates, 0)
        loss_o[...] = jnp.broadcast_to(tv[SL_LOSS:SL_LOSS + 1, 0:LANES], loss_o.shape)
        for p in range(n_vec):
            w, g = w_refs[p][...], tv[SL_LN_G + p, :]
            if SL_LN_G + p == SL_LAM:
                g = g * (RG_LRU_C * jax.nn.sigmoid(-w))
            emit([k_out[p] for k_out in kinds], w, g, m_refs[p][...], v_refs[p][...])
        lanes = pl.ds(pl.multiple_of(me * cc, cc), cc)
        for j, (row0, n) in enumerate(((SL_CONV_W, 3), (SL_LRU_W, 4))):
            w_ref, m_ref, v_ref = conv_refs[3 * j:3 * j + 3]
            emit([k_out[n_vec + 2 + j] for k_out in kinds], w_ref[...], tv[row0:row0 + n, lanes], m_ref[...], v_ref[...])
        for mtx in range(2):
            w_ref, m_ref, v_ref = gate_refs[3 * mtx:3 * mtx + 3]
            for k in range(n_tiles):
                tile = tg[(mtx * n_tiles + k) * hd:(mtx * n_tiles + k + 1) * hd, :]
                for a in range(per):
                    head = k * per + a
                    g = tile[:, a * hd:(a + 1) * hd]
                    delta, m_new, v_new = _adamw(w_ref[head], g, m_ref[head], v_ref[head])
                    for k_out, val in zip(kinds, (g, delta, m_new, v_new)):
                        k_out[n_vec + mtx][head] = val

    vm = pl.BlockSpec(memory_space=pltpu.VMEM)
    like = lambda a: jax.ShapeDtypeStruct(a.shape, F32)
    per_kind = tuple(like(a) for a in vec_w) + (like(gates[0]), like(gates[3]), like(convs[0]), like(convs[3]))
    n_out = 1 + 4 * len(per_kind)
    outs = pl.pallas_call(
        body, name="small_update",
        in_specs=[vm] * n_in, out_specs=(vm,) * n_out,
        out_shape=(jax.ShapeDtypeStruct((SUBLANES, LANES), F32),) + per_kind * 4,
        scratch_shapes=[pltpu.VMEM(gat_v.shape[1:], F32), pltpu.VMEM(gat_g.shape[1:], F32)],
        compiler_params=_params(),
    )(gat_v, gat_g, ln_tot, *vec_w, *vec_m, *vec_v, *gates, *convs)
    return outs[0], [outs[1 + j * len(per_kind):1 + (j + 1) * len(per_kind)] for j in range(4)]


def _head_ones(head_dim, tw):
    lane = jnp.arange(tw) // head_dim
    return (lane[:, None] == lane[None, :]).astype(MM)


def kernel(x, ln_g, w_in, conv_w, lru_conv_w, lru_conv_b, w_a, b_a, w_i, b_i, lam, conv_out_g, lru_out_g, w_out, final_g, loss_target, m_ln_g, m_w_in, m_conv_w, m_lru_conv_w, m_lru_conv_b, m_w_a, m_b_a, m_w_i, m_b_i, m_lam, m_conv_out_g, m_lru_out_g, m_w_out, m_final_g, v_ln_g, v_w_in, v_conv_w, v_lru_conv_w, v_lru_conv_b, v_w_a, v_b_a, v_w_i, v_b_i, v_lam, v_conv_out_g, v_lru_out_g, v_w_out, v_final_g):
    _, t_len, d = x.shape
    hd_l = d // N_LRU_HEADS
    tw = min(MXU_TILE, d)
    x2, tgt2 = x[0], loss_target[0]

    small = [ln_g, lru_conv_b, b_a, b_i, lam, conv_out_g, lru_out_g, final_g]
    p, xnt, _, wout_all, _, sp, wa_t, wi_t, win_all = _gather_project(
        x2, w_in, w_out, conv_w, lru_conv_w, ln_g.reshape(1, d), w_a, w_i, small, min(256, t_len), tw)
    wout_full = wout_all.reshape(N_DEV * w_out.shape[0], d)
    ones_c, ones_l = _head_ones(d // N_CONV_HEADS, tw), _head_ones(hd_l, tw)

    h, dh, dhb, facc, *saved = _forward(x2, tgt2, p, wout_full, wa_t, wi_t, sp, ones_c, ones_l, min(256, t_len))
    dp, yt, slab_v, slab_g = _backward(p, h, dh, saved, facc, wout_full, wa_t, wi_t, sp, ones_c, ones_l, min(256, t_len))
    part_out, own_out, _, _ = _weight_grad_out(yt, dhb)
    part_in, own_in, (chips_out,), (gat_v, gat_g) = _weight_grad_in(xnt, dp, (part_out,), (slab_v, slab_g))
    grad_x, ln_tot, sums_in = _input_grad(dp, win_all, x2, dh, sp, part_in, min(512, t_len))
    gw_in, dw_in, mw_in, vw_in = _update_shard(own_in, sums_in, w_in, m_w_in, v_w_in, "update_w_in")
    gw_out, dw_out, mw_out, vw_out = _update_shard(own_out, (chips_out,), w_out, m_w_out, v_w_out, "update_w_out")

    loss_tile, kinds = _small_update(
        gat_v, gat_g, ln_tot, small,
        [m_ln_g, m_lru_conv_b, m_b_a, m_b_i, m_lam, m_conv_out_g, m_lru_out_g, m_final_g],
        [v_ln_g, v_lru_conv_b, v_b_a, v_b_i, v_lam, v_conv_out_g, v_lru_out_g, v_final_g],
        (w_a, m_w_a, v_w_a, w_i, m_w_i, v_w_i), (conv_w, m_conv_w, v_conv_w, lru_conv_w, m_lru_conv_w, v_lru_conv_w))

    def unpack(kind, big_in, big_out):
        vec, (wa_, wi_, cw_, lw_) = kind[:len(small)], kind[len(small):]
        return [vec[0], big_in, cw_, lw_, vec[1], wa_, vec[2], wi_, vec[3], vec[4], vec[5], vec[6], big_out, vec[7]]

    return (loss_tile[0, 0], grad_x[None], *unpack(kinds[0], gw_in, gw_out), *unpack(kinds[1], dw_in, dw_out),
            *unpack(kinds[2], mw_in, mw_out), *unpack(kinds[3], vw_in, vw_out))
```

```python
import jax
import jax.numpy as jnp
from jax import lax
from jax.experimental import pallas as pl
from jax.experimental.pallas import tpu as pltpu

F32 = jnp.float32
MM = jnp.bfloat16
MESH = pl.DeviceIdType.MESH

N_DEV = 8
N_CONV_HEADS = 8
N_LRU_HEADS = 16
RG_LRU_C = 8.0
RMS_EPS = 1e-6
ADAM_LR, ADAM_B1, ADAM_B2, ADAM_EPS, ADAM_WD, ADAM_STEP = 0.001, 0.9, 0.999, 1e-08, 0.01, 10
ADAM_BC1 = 1.0 - ADAM_B1 ** ADAM_STEP
ADAM_BC2 = 1.0 - ADAM_B2 ** ADAM_STEP

SUBLANES = 8
LANES = 128
MXU_TILE = 256
VMEM_LIMIT = 56 * 1024 * 1024

SP_LN_G, SP_LRU_B, SP_B_A, SP_B_I, SP_LAM, SP_CONV_G, SP_LRU_G, SP_FINAL_G, SP_CONV_W, SP_LRU_W = 0, 1, 2, 3, 4, 5, 6, 7, 8, 11
SP_ROWS = 16
P_B, P_C, P_XC, P_GC, P_XL, P_GL = 0, 1, 2, 3, 4, 5
A_CONV_G, A_LRU_G, A_LAM, A_B_A, A_B_I, A_CONV_W, A_LRU_W, A_LRU_B = 0, 1, 2, 3, 4, 5, 8, 12
A_GROUPS = 13
SL_LOSS, SL_LN_G, SL_LRU_B, SL_B_A, SL_B_I, SL_LAM, SL_CONV_G, SL_LRU_G, SL_FINAL_G, SL_CONV_W, SL_LRU_W = 0, 1, 2, 3, 4, 5, 6, 7, 8, 16, 24
SL_ROWS = 32


def _params(vmem=True, **kw):
    if vmem:
        kw["vmem_limit_bytes"] = VMEM_LIMIT
    return pltpu.CompilerParams(**kw)


def _dot(a, b):
    return jnp.dot(a, b, preferred_element_type=F32)


def _dot_nt(a, b):
    return lax.dot_general(a, b, (((1,), (1,)), ((), ())), preferred_element_type=F32)


def _head_sums(v, ones_tile):
    tw = ones_tile.shape[0]
    vb = v.astype(MM)
    return jnp.concatenate([_dot(vb[:, k:k + tw], ones_tile) for k in range(0, v.shape[1], tw)], axis=1)


def _head_rstd(v, ones_tile, head_dim):
    return lax.rsqrt(_head_sums(v * v, ones_tile) * (1.0 / head_dim) + RMS_EPS)


def _sigmoid(x):
    return 0.5 * jnp.tanh(0.5 * x) + 0.5


def _lru_input_scale_sq(log_a, a):
    return -jnp.tanh(log_a) * (1.0 + a * a)


def _log_sigmoid(x):
    z = jnp.exp(-jnp.abs(x))
    u = 1.0 + z
    log1p_z = jnp.where(u == 1.0, z, jnp.log(u) * (z / (u - 1.0)))
    return jnp.minimum(x, 0.0) - log1p_z


def _row_iota(d):
    return lax.broadcasted_iota(jnp.int32, (SUBLANES, d), 0)


def _shift_down(cur, prev, s, row):
    return jnp.where(row >= s, pltpu.roll(cur, s, axis=0), pltpu.roll(prev, s, axis=0))


def _shift_up(cur, nxt, s, row):
    k = SUBLANES - s
    return jnp.where(row < k, pltpu.roll(cur, k, axis=0), pltpu.roll(nxt, k, axis=0))


def _scan_fwd(a, b, h_prev, row):
    for s in (1, 2, 4):
        a_s = jnp.where(row >= s, pltpu.roll(a, s, axis=0), 1.0)
        b_s = jnp.where(row >= s, pltpu.roll(b, s, axis=0), 0.0)
        b = a * b_s + b
        a = a * a_s
    return a * h_prev + b


def _scan_bwd(a_next, b, g_next, row):
    a = a_next
    for s in (1, 2, 4):
        k = SUBLANES - s
        a_s = jnp.where(row < k, pltpu.roll(a, k, axis=0), 1.0)
        b_s = jnp.where(row < k, pltpu.roll(b, k, axis=0), 0.0)
        b = a * b_s + b
        a = a * a_s
    return a * g_next + b


def _bcast_row(v, r):
    return jnp.broadcast_to(v[r:r + 1, :], v.shape)


def _chunks(n_rows, rc, body, init, reverse=False):
    n = n_rows // rc

    def step(i, carry):
        j = (n - 1 - i) if reverse else i
        return body(pl.multiple_of(j * rc, rc), carry)

    return lax.fori_loop(0, n, step, init)


def _adamw(w, g, m, v):
    m = ADAM_B1 * m + (1.0 - ADAM_B1) * g
    v = ADAM_B2 * v + (1.0 - ADAM_B2) * (g * g)
    m_hat = m / ADAM_BC1
    v_hat = v / ADAM_BC2
    delta = -ADAM_LR * (m_hat / (jnp.sqrt(v_hat) + ADAM_EPS) + ADAM_WD * w)
    return delta, m, v


def _mesh_pos():
    return lax.axis_index("x"), lax.axis_index("y"), lax.axis_index("c")


class _Gather:
    def __init__(self, blocks_of, send_sems, recv_sems, own_src=None):
        x, y, c = _mesh_pos()
        self.c = c
        self.me, self.sibling = (x, y, c), (x, y, 1 - c)
        self.chips = [(1 - x, y), (x, 1 - y), (1 - x, 1 - y)]
        self.blocks_of, self.send_sems, self.recv_sems = blocks_of, send_sems, recv_sems
        self.own_src = own_src

    def copy(self, a, k, block, to):
        src = self.blocks_of(a, *block)
        if block is self.me and self.own_src is not None:
            src = self.own_src[a]
        return pltpu.make_async_remote_copy(
            src_ref=src, dst_ref=self.blocks_of(a, *block),
            send_sem=self.send_sems.at[a * 7 + k], recv_sem=self.recv_sems.at[a * 7 + k],
            device_id=to, device_id_type=MESH)

    def start_own(self, a):
        self.copy(a, 0, self.me, self.sibling).start()
        for j, chip in enumerate(self.chips):
            self.copy(a, 1 + j, self.me, (*chip, self.c)).start()

    def wait_sibling(self, a):
        self.copy(a, 0, self.sibling, self.me).wait_recv()

    def wait_chip_and_pass_on(self, a, j):
        block = (*self.chips[j], self.c)
        self.copy(a, 1 + j, block, self.me).wait_recv()
        self.copy(a, 4 + j, block, self.sibling).start()

    def wait_passed_on(self, a, j):
        self.copy(a, 4 + j, (*self.chips[j], 1 - self.c), self.me).wait_recv()

    def wait_sends(self, a):
        self.copy(a, 0, self.me, self.sibling).wait_send()
        for j, chip in enumerate(self.chips):
            self.copy(a, 1 + j, self.me, (*chip, self.c)).wait_send()
            self.copy(a, 4 + j, (*chip, self.c), self.sibling).wait_send()

    def finish(self, a):
        for j in range(3):
            self.wait_chip_and_pass_on(a, j)
        self.wait_sibling(a)
        for j in range(3):
            self.wait_passed_on(a, j)
        self.wait_sends(a)


class _BalancedGather:
    def __init__(self, slot, send_sems, recv_sems, own_src):
        x, y, c = _mesh_pos()
        self.c = c
        self.me, self.sibling = (x, y, c), (x, y, 1 - c)
        self.chips = [(1 - x, y), (x, 1 - y), (1 - x, 1 - y)]
        self.slot, self.send_sems, self.recv_sems, self.own_src = slot, send_sems, recv_sems, own_src

    def half(self, a, block, which):
        ref = self.slot(a, *block)
        n = ref.shape[0] // 2
        return ref.at[pl.ds(which * n, n)]

    def copy(self, a, k, src, dst, to):
        return pltpu.make_async_remote_copy(
            src_ref=src, dst_ref=dst, send_sem=self.send_sems.at[a * 8 + k], recv_sem=self.recv_sems.at[a * 8 + k],
            device_id=to, device_id_type=MESH)

    def whole(self, a, k, block, to):
        src = self.own_src[a] if block is self.me else self.slot(a, *block)
        return self.copy(a, k, src, self.slot(a, *block), to)

    def halved(self, a, k, block, which, to):
        return self.copy(a, k, self.half(a, block, which), self.half(a, block, which), to)

    def on(self, chip):
        return (*self.chips[chip], self.c)

    def start_own(self, a):
        self.whole(a, 0, self.me, self.sibling).start()
        self.whole(a, 1, self.me, self.on(0)).start()
        self.whole(a, 2, self.me, self.on(1)).start()

    def start_own_staggered(self, a):
        self.whole(a, 0, self.me, self.sibling).start()
        for core, order in ((1, (0, 1)), (0, (1, 0))):
            @pl.when(self.c == core)
            def _(order=order):
                for j in order:
                    self.whole(a, 1 + j, self.me, self.on(j)).start()

    def wait_sibling(self, a):
        self.whole(a, 0, self.sibling, self.me).wait_recv()

    def on_neighbour(self, a, j):
        self.whole(a, 1 + j, self.on(j), self.me).wait_recv()
        self.halved(a, 3 + j, self.on(j), j, self.on(1 - j)).start()
        self.whole(a, 5 + j, self.on(j), self.sibling).start()

    def on_diagonal(self, a):
        self.halved(a, 3, self.on(2), 0, self.me).wait_recv()
        self.halved(a, 4, self.on(2), 1, self.me).wait_recv()
        self.whole(a, 7, self.on(2), self.sibling).start()

    def wait_passed_on(self, a, j):
        self.whole(a, 5 + j, (*self.chips[j], 1 - self.c), self.me).wait_recv()

    def wait_sends(self, a):
        self.whole(a, 0, self.me, self.sibling).wait_send()
        for j in range(2):
            self.whole(a, 1 + j, self.me, self.on(j)).wait_send()
            self.halved(a, 3 + j, self.on(j), j, self.on(1 - j)).wait_send()
        for j in range(3):
            self.whole(a, 5 + j, self.on(j), self.sibling).wait_send()


def _block_order():
    x, y, c = _mesh_pos()
    idx = lambda chip, core: 4 * chip[0] + 2 * chip[1] + core
    own, first, second, diag = (x, y), (x ^ c, y ^ (1 - c)), (x ^ (1 - c), y ^ c), (1 - x, 1 - y)
    order = [idx(own, c), idx(own, 1 - c), idx(first, c), idx(second, 1 - c), idx(second, c), idx(first, 1 - c),
             idx(diag, c), idx(diag, 1 - c)]
    return jnp.stack(order).astype(jnp.int32)


def _gather_project(x, w_in, w_out, conv_w, lru_conv_w, ln_g, w_a, w_i, vecs, tb, tw):
    t_len, d = x.shape
    nb = t_len // tb
    cols = w_in.shape[1]
    mc = min(512, t_len)
    conv_pack = jax.ShapeDtypeStruct((SUBLANES, conv_w.shape[1]), F32)
    srcs = (w_in, w_out, conv_pack)
    dts = (MM, MM, F32)
    n_vec = len(vecs)
    n_heads, hd, _ = w_a.shape
    per = tw // hd

    def body(order_ref, x_ref, win_ref, wout_ref, cw_ref, lw_ref, lng_ref, wa_ref, wi_ref, *refs):
        vec_refs = refs[:n_vec]
        (p_ref, xnt_ref, win_all, wout_all, cp_all, sp_ref, wat_ref, wit_ref,
         xnb, wall, st_out, st_cp, cp_vm, send_sems, recv_sems, cp_send, cp_recv, local_sems) = refs[n_vec:]
        i = pl.program_id(0)
        x_, y_, c_ = _mesh_pos()
        me = 4 * x_ + 2 * y_ + c_
        outs = (win_all, wout_all, cp_all)
        lands = (wall, wout_all, cp_all)
        stages = (wall.at[me], st_out, st_cp)
        gather = _BalancedGather(lambda a, px, py, pc: lands[a].at[4 * px + 2 * py + pc], send_sems, recv_sems, stages)
        small = _Gather(lambda a, px, py, pc: cp_all.at[4 * px + 2 * py + pc], cp_send, cp_recv, own_src=[st_cp])
        keep_own = [pltpu.make_async_copy(stages[a], outs[a].at[me], local_sems.at[a]) for a in range(3)]

        def keep(k):
            blk = order_ref[k]
            return pltpu.make_async_copy(wall.at[blk], win_all.at[blk], local_sems.at[2 + k])

        @pl.when(i == 0)
        def _():
            for a, src in enumerate((win_ref, wout_ref)):
                dst, rc = stages[a], 32

                def cast(r, carry, src=src, dst=dst):
                    dst[pl.ds(r, rc), :] = src[pl.ds(r, rc), :].astype(dst.dtype)
                    return carry

                _chunks(src.shape[0], rc, cast, 0)
                keep_own[a].start()
            gather.start_own_staggered(0)
            n_cw, n_lw = cw_ref.shape[0], lw_ref.shape[0]
            st_cp[...] = jnp.zeros_like(st_cp)
            st_cp[0:n_cw, :] = cw_ref[...]
            st_cp[n_cw:n_cw + n_lw, :] = lw_ref[...]
            keep_own[2].start()

        @pl.when(i < nb)
        def _():
            xv = x_ref[...]
            r0 = lax.rsqrt(jnp.mean(xv * xv, axis=-1, keepdims=True) + RMS_EPS)
            xn = xv * r0 * lng_ref[...]
            xnb[pl.ds(pl.multiple_of(i * tb, tb), tb), :] = xn.astype(MM)
            xnt_ref[...] = xn.T.astype(MM)

        def by_core(action):
            for core, (first, second) in ((1, (0, 1)), (0, (1, 0))):
                @pl.when(c_ == core)
                def _(first=first, second=second):
                    action(first, second)

        for k in range(N_DEV):
            @pl.when(i == nb + k)
            def _(k=k):
                if k == 1:
                    gather.wait_sibling(0)
                elif k == 2:
                    by_core(lambda first, second: gather.on_neighbour(0, first))
                    gather.start_own(1)
                    small.start_own(0)
                elif k == 3:
                    by_core(lambda first, second: gather.wait_passed_on(0, second))
                elif k == 4:
                    by_core(lambda first, second: gather.on_neighbour(0, second))
                elif k == 5:
                    by_core(lambda first, second: gather.wait_passed_on(0, first))
                    gather.on_neighbour(1, 0)
                    gather.on_neighbour(1, 1)
                elif k == 6:
                    gather.on_diagonal(0)
                elif k == 7:
                    gather.wait_passed_on(0, 2)
                    gather.on_diagonal(1)
                blk = order_ref[k]
                if k:
                    keep(k).start()

                def project(r, carry):
                    rows = pl.ds(r, mc)
                    p_ref[rows, :] = _dot(xnb[rows, :], wall[blk]).astype(MM)
                    return carry

                _chunks(t_len, mc, project, 0)
                if k == N_DEV - 1:
                    gather.wait_sends(0)
                    gather.wait_sibling(1)
                    for j in range(3):
                        gather.wait_passed_on(1, j)
                    gather.wait_sends(1)
                    small.finish(0)
                    for cp in keep_own + [keep(kk) for kk in range(1, N_DEV)]:
                        cp.wait()
                    load = pltpu.make_async_copy(cp_all, cp_vm, local_sems.at[N_DEV + 2])
                    load.start()
                    load.wait()
                    for r, ref in enumerate(vec_refs):
                        sp_ref[r, :] = ref[...]
                    sp_ref[n_vec:n_vec + SUBLANES, :] = jnp.concatenate([cp_vm[dev] for dev in range(N_DEV)], axis=1)
                    for src, dst in ((wa_ref, wat_ref), (wi_ref, wit_ref)):
                        dst[...] = jnp.zeros_like(dst)
                        for head in range(n_heads):
                            lo = (head % per) * hd
                            dst[head // per, lo:lo + hd, lo:lo + hd] = src[head].astype(MM)

    vm = pl.BlockSpec(memory_space=pltpu.VMEM)
    hbm = pl.BlockSpec(memory_space=pl.ANY)
    grid_spec = pltpu.PrefetchScalarGridSpec(
        num_scalar_prefetch=1, grid=(nb + N_DEV,),
        in_specs=[pl.BlockSpec((tb, d), lambda i, o: (jnp.minimum(i, nb - 1), 0))] + [vm] * (7 + n_vec),
        out_specs=(pl.BlockSpec((t_len, cols), lambda i, o: (0, o[jnp.maximum(i - nb, 0)])),
                   pl.BlockSpec((d, tb), lambda i, o: (0, jnp.minimum(i, nb - 1))), hbm, hbm, hbm,
                   pl.BlockSpec((SP_ROWS, d), lambda i, o: (0, 0)),
                   pl.BlockSpec((n_heads // per, tw, tw), lambda i, o: (0, 0, 0)),
                   pl.BlockSpec((n_heads // per, tw, tw), lambda i, o: (0, 0, 0))),
        scratch_shapes=[pltpu.VMEM((t_len, d), MM), pltpu.VMEM((N_DEV,) + w_in.shape, MM),
                        pltpu.VMEM(w_out.shape, MM), pltpu.VMEM(conv_pack.shape, F32),
                        pltpu.VMEM((N_DEV,) + conv_pack.shape, F32),
                        pltpu.SemaphoreType.DMA((16,)), pltpu.SemaphoreType.DMA((16,)),
                        pltpu.SemaphoreType.DMA((7,)), pltpu.SemaphoreType.DMA((7,)), pltpu.SemaphoreType.DMA((N_DEV + 3,))])
    return pl.pallas_call(
        body, name="gather_project", grid_spec=grid_spec,
        out_shape=(jax.ShapeDtypeStruct((t_len, N_DEV * cols), MM),
                   jax.ShapeDtypeStruct((d, t_len), MM))
                  + tuple(jax.ShapeDtypeStruct((N_DEV,) + s.shape, dt) for s, dt in zip(srcs, dts))
                  + (jax.ShapeDtypeStruct((SP_ROWS, d), F32),)
                  + (jax.ShapeDtypeStruct((n_heads // per, tw, tw), MM),) * 2,
        compiler_params=_params(dimension_semantics=("arbitrary",)),
    )(_block_order(), x, w_in, w_out, conv_w, lru_conv_w, ln_g, w_a, w_i, *vecs)


def _forward(x, tgt, p, wout, wa_t, wi_t, sp, ones_c, ones_l, tb):
    t_len, d = x.shape
    nb = t_len // tb
    n_tiles, tw = wa_t.shape[0], wa_t.shape[1]
    hd_c, hd_l = d // N_CONV_HEADS, d // N_LRU_HEADS
    s8 = SUBLANES

    def body(x_ref, tgt_ref, p_ref, wout_hbm, wa_ref, wi_ref, sp_ref, oc_ref, ol_ref,
             h_ref, dh_ref, dhb_ref, acc_ref, yc, czs, u, pa, pi,
             rcf, rlf, ybuf, tail_z, tail_xl, hcar, wout_ref, wout_sem):
        i = pl.program_id(0)
        row = _row_iota(d)
        load_wout = pltpu.make_async_copy(wout_hbm, wout_ref, wout_sem.at[0])

        @pl.when(i == 0)
        def _():
            load_wout.start()
            tail_z[...] = jnp.zeros_like(tail_z)
            tail_xl[...] = jnp.zeros_like(tail_xl)
            hcar[...] = jnp.zeros_like(hcar)
            acc_ref[...] = jnp.zeros_like(acc_ref)

        def spr(r):
            return sp_ref[r:r + 1, :]

        def proj(rows, seg):
            return p_ref[rows, seg * d:(seg + 1) * d].astype(F32)

        w0, w1, w2 = spr(SP_CONV_W), spr(SP_CONV_W + 1), spr(SP_CONV_W + 2)
        l0, l1, l2, l3 = spr(SP_LRU_W), spr(SP_LRU_W + 1), spr(SP_LRU_W + 2), spr(SP_LRU_W + 3)
        lb = spr(SP_LRU_B)

        def convs(r, carry):
            zp, xp = carry
            rows16 = pl.ds(r, 2 * s8)
            bg16, xl16 = proj(rows16, P_B), proj(rows16, P_XL)
            z16 = proj(rows16, P_C) * proj(rows16, P_XC)
            for j in range(2):
                rows, sub = pl.ds(r + j * s8, s8), slice(j * s8, (j + 1) * s8)
                z, xl = z16[sub], xl16[sub]
                cz = w0 * _shift_down(z, zp, 2, row) + w1 * _shift_down(z, zp, 1, row) + w2 * z
                czs[rows, :] = cz
                yc[rows, :] = bg16[sub] * cz
                u[rows, :] = (l0 * _shift_down(xl, xp, 3, row) + l1 * _shift_down(xl, xp, 2, row)
                              + l2 * _shift_down(xl, xp, 1, row) + l3 * xl + lb)
                zp, xp = z, xl
            return zp, xp

        z_last, xl_last = _chunks(tb, 2 * s8, convs, (tail_z[...], tail_xl[...]))
        tail_z[...] = z_last
        tail_xl[...] = xl_last

        ub = u[...].astype(MM)
        for k in range(n_tiles):
            sl = slice(k * tw, (k + 1) * tw)
            pa[:, sl] = _dot(ub[:, sl], wa_ref[k])
            pi[:, sl] = _dot(ub[:, sl], wi_ref[k])
        rcf[...] = _head_rstd(yc[...], oc_ref[...], hd_c)

        c8 = RG_LRU_C * _log_sigmoid(spr(SP_LAM))
        b_a, b_i = spr(SP_B_A), spr(SP_B_I)

        def lru(r, hp):
            rows = pl.ds(r, SUBLANES)
            ra = _sigmoid(pa[rows, :] + b_a)
            ii = _sigmoid(pi[rows, :] + b_i)
            pa[rows, :] = ra
            pi[rows, :] = ii
            la = ra * c8
            a = jnp.exp(la)
            mult = jnp.sqrt(_lru_input_scale_sq(la, a))
            h = _scan_fwd(a, mult * (ii * u[rows, :]), hp, row)
            h_ref[rows, :] = h
            return _bcast_row(h, SUBLANES - 1)

        hcar[...] = _chunks(tb, SUBLANES, lru, hcar[...])
        rlf[...] = _head_rstd(h_ref[...], ol_ref[...], hd_l)

        g_c, g_l = spr(SP_CONV_G), spr(SP_LRU_G)

        def gate(r, carry):
            rows = pl.ds(r, 2 * s8)
            gc, gl = proj(rows, P_GC), proj(rows, P_GL)
            ybuf[rows, 0:d] = (yc[rows, :] * rcf[rows, :] * g_c * (gc * _sigmoid(gc))).astype(MM)
            ybuf[rows, d:2 * d] = (h_ref[rows, :] * rlf[rows, :] * g_l * (gl * _sigmoid(gl))).astype(MM)
            return carry

        _chunks(tb, 2 * s8, gate, 0)

        pl.when(i == 0)(load_wout.wait)
        hres = x_ref[...] + _dot(ybuf[...], wout_ref[...])
        rf = lax.rsqrt(jnp.mean(hres * hres, axis=-1, keepdims=True) + RMS_EPS)
        hn = hres * rf
        fg = spr(SP_FINAL_G)
        err = hn * fg - tgt_ref[...]
        dout = err * (1.0 / d)
        acc_ref[0:SUBLANES, :] += (err * err).reshape(tb // SUBLANES, SUBLANES, d).sum(axis=0)
        acc_ref[SUBLANES:2 * SUBLANES, :] += (dout * hn).reshape(tb // SUBLANES, SUBLANES, d).sum(axis=0)
        gd = dout * fg
        dhres = rf * (gd - hn * jnp.mean(gd * hn, axis=-1, keepdims=True))
        dh_ref[...] = dhres
        dhb_ref[...] = dhres.astype(MM)

    vm = pl.BlockSpec(memory_space=pltpu.VMEM)
    blk = lambda w: pl.BlockSpec((tb, w), lambda i: (i, 0))
    buf = pltpu.VMEM((tb, d), F32)
    car = pltpu.VMEM((SUBLANES, d), F32)
    return pl.pallas_call(
        body, name="forward", grid=(nb,),
        in_specs=[blk(d), blk(d), blk(6 * d), pl.BlockSpec(memory_space=pl.ANY), vm, vm, vm, vm, vm],
        out_specs=(blk(d), blk(d), blk(d), pl.BlockSpec((2 * SUBLANES, d), lambda i: (0, 0))) + (blk(d),) * 5,
        out_shape=(jax.ShapeDtypeStruct((t_len, d), F32),
                   jax.ShapeDtypeStruct((t_len, d), F32),
                   jax.ShapeDtypeStruct((t_len, d), MM),
                   jax.ShapeDtypeStruct((2 * SUBLANES, d), F32))
                  + (jax.ShapeDtypeStruct((t_len, d), F32),) * 5,
        scratch_shapes=[buf] * 2 + [pltpu.VMEM((tb, 2 * d), MM), car, car, car,
                                    pltpu.VMEM(wout.shape, wout.dtype), pltpu.SemaphoreType.DMA((1,))],
        compiler_params=_params(dimension_semantics=("arbitrary",)),
    )(x, tgt, p, wout, wa_t, wi_t, sp, ones_c, ones_l)


def _backward(p, h, dh, saved, facc, wout, wa_t, wi_t, sp, ones_c, ones_l, tb):
    t_len, d = h.shape
    nb = t_len // tb
    n_tiles, tw = wa_t.shape[0], wa_t.shape[1]
    hd_c, hd_l = d // N_CONV_HEADS, d // N_LRU_HEADS
    g_rows = 2 * n_tiles * hd_l
    s8 = SUBLANES

    def body(p_ref, h_ref, hhalo_ref, dh_ref, yc, czs, u, ra_ref, ii_ref, facc_ref,
             wout_ref, wa_ref, wi_ref, sp_ref, oc_ref, ol_ref,
             dp_ref, yt_ref, slab_v, slab_g,
             hh, dy, ybuf, rcf, rlf, qc, ql, dyc_hat, dyl_hat, dpa, dpi, du, gwa_ref, gwi_ref, acc_ref,
             car_dcz, car_a, car_g, car_du):
        i = pl.program_id(0)
        blk_idx = nb - 1 - i
        row = _row_iota(d)

        @pl.when(i == 0)
        def _():
            for ref in (car_dcz, car_a, car_g, car_du, gwa_ref, gwi_ref, acc_ref):
                ref[...] = jnp.zeros_like(ref)

        def spr(r):
            return sp_ref[r:r + 1, :]

        def proj(rows, seg):
            return p_ref[rows, seg * d:(seg + 1) * d].astype(F32)

        def put(rows, seg, halves):
            dp_ref[rows, seg * d:(seg + 1) * d] = jnp.concatenate(halves, axis=0).astype(MM)

        def acc_add(group, val):
            acc_ref[group * s8:(group + 1) * s8, :] += val

        live = jnp.where(blk_idx > 0, 1.0, 0.0).astype(F32)
        hh[0:s8, :] = hhalo_ref[...] * live
        hh[s8:, :] = h_ref[...]

        dy[...] = _dot_nt(dh_ref[...].astype(MM), wout_ref[...])

        w0, w1, w2 = spr(SP_CONV_W), spr(SP_CONV_W + 1), spr(SP_CONV_W + 2)
        l0, l1, l2, l3 = spr(SP_LRU_W), spr(SP_LRU_W + 1), spr(SP_LRU_W + 2), spr(SP_LRU_W + 3)

        rcf[...] = _head_rstd(yc[...], oc_ref[...], hd_c)
        rlf[...] = _head_rstd(h_ref[...], ol_ref[...], hd_l)

        g_c, g_l = spr(SP_CONV_G), spr(SP_LRU_G)

        def gates(r, carry):
            rows = pl.ds(r, 2 * s8)
            for (seg, off_y, src, rstd, gain, q, dhat, grp) in (
                    (P_GC, 0, yc, rcf, g_c, qc, dyc_hat, A_CONV_G),
                    (P_GL, d, h_ref, rlf, g_l, ql, dyl_hat, A_LRU_G)):
                gt = proj(rows, seg)
                sg = _sigmoid(gt)
                silu = gt * sg
                yhat = src[rows, :] * rstd[rows, :]
                nrm = yhat * gain
                ybuf[rows, off_y:off_y + d] = nrm * silu
                dout = dy[rows, off_y:off_y + d]
                dnrm = dout * silu
                dp_ref[rows, seg * d:(seg + 1) * d] = (dout * nrm * (sg * (1.0 + gt * (1.0 - sg)))).astype(MM)
                dg = dnrm * yhat
                acc_add(grp, dg[0:s8] + dg[s8:])
                dh_ = dnrm * gain
                dhat[rows, :] = dh_
                q[rows, :] = dh_ * yhat
            return carry

        _chunks(tb, 2 * s8, gates, 0)

        qc[...] = _head_sums(qc[...], oc_ref[...]) * (1.0 / hd_c)
        ql[...] = _head_sums(ql[...], ol_ref[...]) * (1.0 / hd_l)
        yt_ref[...] = ybuf[...].T.astype(MM)

        c8 = RG_LRU_C * _log_sigmoid(spr(SP_LAM))

        def conv_mixer(r, dcz_n):
            rows16 = pl.ds(r, 2 * s8)
            bg16, cg16, xc16 = proj(rows16, P_B), proj(rows16, P_C), proj(rows16, P_XC)
            z16 = cg16 * xc16
            d_b, d_c, d_x = [None, None], [None, None], [None, None]
            for j in (1, 0):
                rows, sub = pl.ds(r + j * s8, s8), slice(j * s8, (j + 1) * s8)
                rstd = rcf[rows, :]
                yhat = yc[rows, :] * rstd
                dyc = rstd * (dyc_hat[rows, :] - yhat * qc[rows, :])
                d_b[j] = dyc * czs[rows, :]
                dcz = dyc * bg16[sub]
                up1, up2 = _shift_up(dcz, dcz_n, 1, row), _shift_up(dcz, dcz_n, 2, row)
                dz = w2 * dcz + w1 * up1 + w0 * up2
                d_c[j] = dz * xc16[sub]
                d_x[j] = dz * cg16[sub]
                z = z16[sub]
                acc_add(A_CONV_W, up2 * z)
                acc_add(A_CONV_W + 1, up1 * z)
                acc_add(A_CONV_W + 2, dcz * z)
                dcz_n = dcz
            put(rows16, P_B, d_b)
            put(rows16, P_C, d_c)
            put(rows16, P_XC, d_x)
            return dcz_n

        car_dcz[...] = _chunks(tb, 2 * s8, conv_mixer, car_dcz[...], reverse=True)

        def lru_mixer(r, carry):
            a_n, g_n = carry
            for j in (1, 0):
                rows = pl.ds(r + j * s8, s8)
                rstd = rlf[rows, :]
                hcur = hh[pl.ds(r + (j + 1) * s8, s8), :]
                hhat = hcur * rstd
                dh_out = rstd * (dyl_hat[rows, :] - hhat * ql[rows, :])
                ra = ra_ref[rows, :]
                la = ra * c8
                a = jnp.exp(la)
                g = _scan_bwd(_shift_up(a, a_n, 1, row), dh_out, g_n, row)
                da = g * _shift_down(hcur, hh[pl.ds(r + j * s8, s8), :], 1, row)
                ii = ii_ref[rows, :]
                uu = u[rows, :]
                mult_sq = _lru_input_scale_sq(la, a)
                inv_mult = lax.rsqrt(mult_sq)
                dmult = g * (ii * uu)
                ds = g * (mult_sq * inv_mult)
                dla = a * (da - dmult * a * inv_mult)
                acc_add(A_LAM, dla * ra)
                dpa_ = dla * c8 * ra * (1.0 - ra)
                dpi_ = ds * uu * ii * (1.0 - ii)
                acc_add(A_B_A, dpa_)
                acc_add(A_B_I, dpi_)
                dpa[rows, :] = dpa_
                dpi[rows, :] = dpi_
                du[rows, :] = ds * ii
                a_n, g_n = a, _bcast_row(g, 0)
            return a_n, g_n

        a_f, g_f = _chunks(tb, 2 * s8, lru_mixer, (car_a[...], car_g[...]), reverse=True)
        car_a[...] = a_f
        car_g[...] = g_f

        dpab = dpa[...].astype(MM)
        dpib = dpi[...].astype(MM)
        for k in range(n_tiles):
            sl = slice(k * tw, (k + 1) * tw)
            du[:, sl] += _dot_nt(dpab[:, sl], wa_ref[k]) + _dot_nt(dpib[:, sl], wi_ref[k])
            ut = u[:, sl].T.astype(MM)
            gwa_ref[k] += _dot(ut, dpab[:, sl])
            gwi_ref[k] += _dot(ut, dpib[:, sl])

        def lru_conv(r, du_n):
            rows16 = pl.ds(r, 2 * s8)
            xl16 = proj(rows16, P_XL)
            d_xl = [None, None]
            for j in (1, 0):
                rows, sub = pl.ds(r + j * s8, s8), slice(j * s8, (j + 1) * s8)
                dut = du[rows, :]
                up1, up2, up3 = (_shift_up(dut, du_n, s, row) for s in (1, 2, 3))
                d_xl[j] = l3 * dut + l2 * up1 + l1 * up2 + l0 * up3
                xl = xl16[sub]
                acc_add(A_LRU_W, up3 * xl)
                acc_add(A_LRU_W + 1, up2 * xl)
                acc_add(A_LRU_W + 2, up1 * xl)
                acc_add(A_LRU_W + 3, dut * xl)
                acc_add(A_LRU_B, dut)
                du_n = dut
            put(rows16, P_XL, d_xl)
            return du_n

        car_du[...] = _chunks(tb, 2 * s8, lru_conv, car_du[...], reverse=True)

        @pl.when(i == nb - 1)
        def _():
            def rowsum(ref, group):
                return jnp.sum(ref[group * s8:(group + 1) * s8, :], axis=0, keepdims=True)

            slab_v[...] = jnp.zeros_like(slab_v)
            loss = jnp.sum(rowsum(facc_ref, 0), axis=1, keepdims=True) * (0.5 / d)
            rows = {SL_LOSS: jnp.broadcast_to(loss, (1, d)), SL_FINAL_G: rowsum(facc_ref, 1),
                    SL_LRU_B: rowsum(acc_ref, A_LRU_B), SL_B_A: rowsum(acc_ref, A_B_A), SL_B_I: rowsum(acc_ref, A_B_I),
                    SL_LAM: rowsum(acc_ref, A_LAM), SL_CONV_G: rowsum(acc_ref, A_CONV_G), SL_LRU_G: rowsum(acc_ref, A_LRU_G)}
            for k in range(3):
                rows[SL_CONV_W + k] = rowsum(acc_ref, A_CONV_W + k)
            for k in range(4):
                rows[SL_LRU_W + k] = rowsum(acc_ref, A_LRU_W + k)
            for r, val in rows.items():
                slab_v[r:r + 1, :] = val
            head_of_lane = lax.broadcasted_iota(jnp.int32, (hd_l, tw), 1) // hd_l
            for mtx, g_ref in enumerate((gwa_ref, gwi_ref)):
                for k in range(n_tiles):
                    packed = jnp.zeros((hd_l, tw), F32)
                    for a in range(tw // hd_l):
                        packed = jnp.where(head_of_lane == a, g_ref[k, a * hd_l:(a + 1) * hd_l, :], packed)
                    slab_g[(mtx * n_tiles + k) * hd_l:(mtx * n_tiles + k + 1) * hd_l, :] = packed.astype(MM)

    vm = pl.BlockSpec(memory_space=pltpu.VMEM)
    rev = lambda w: pl.BlockSpec((tb, w), lambda i: (nb - 1 - i, 0))
    halo = lambda rows, w: pl.BlockSpec((rows, w), lambda i: (jnp.maximum((nb - 1 - i) * (tb // rows) - 1, 0), 0))
    const = lambda shape: pl.BlockSpec(shape, lambda i: (0,) * len(shape))
    buf = lambda w: pltpu.VMEM((tb, w), F32)
    car = pltpu.VMEM((SUBLANES, d), F32)
    return pl.pallas_call(
        body, name="backward", grid=(nb,),
        in_specs=[rev(6 * d), rev(d), halo(SUBLANES, d), rev(d)] + [rev(d)] * 5 + [vm, vm, vm, vm, vm, vm, vm],
        out_specs=(rev(6 * d), pl.BlockSpec((2 * d, tb), lambda i: (0, nb - 1 - i)),
                   const((SL_ROWS, d)), const((g_rows, tw))),
        out_shape=(jax.ShapeDtypeStruct((t_len, 6 * d), MM),
                   jax.ShapeDtypeStruct((2 * d, t_len), MM),
                   jax.ShapeDtypeStruct((SL_ROWS, d), F32),
                   jax.ShapeDtypeStruct((g_rows, tw), MM)),
        scratch_shapes=[pltpu.VMEM((SUBLANES + tb, d), F32), buf(2 * d), buf(2 * d)] + [buf(d)] * 9
                       + [pltpu.VMEM((n_tiles, tw, tw), F32), pltpu.VMEM((n_tiles, tw, tw), F32),
                          pltpu.VMEM((A_GROUPS * SUBLANES, d), F32), car, car, car, car],
        compiler_params=_params(dimension_semantics=("arbitrary",)),
    )(p, h, h, dh, *saved, facc, wout, wa_t, wi_t, sp, ones_c, ones_l)


def _input_grad(dp, win_all, x, dh, sp, part, tb):
    t_len, d = x.shape
    nb = t_len // tb
    cols = win_all.shape[2]
    mid = min(nb - 1, (3 * nb) // 8)
    late = min(mid, nb // 4)
    rc = 32

    def body(dp_ref, win_hbm, x_ref, dh_ref, sp_ref, part_ref, gx_ref, ln_ref, direct, relayed,
             send_sems, recv_sems, local_sems, acc_ref, ln_all, ln_send, ln_recv, mine, theirs, win_ref, win_sems):
        i = pl.program_id(0)
        load_win = [pltpu.make_async_copy(win_hbm.at[j], win_ref.at[j], win_sems.at[j]) for j in range(N_DEV)]
        x_, y_, c_ = _mesh_pos()
        first, second = 1 - c_, c_
        nbr1 = (x_ ^ c_, y_ ^ (1 - c_), c_)
        nbr2 = (x_ ^ (1 - c_), y_ ^ c_, c_)

        def remote(src, dst, k, to):
            return pltpu.make_async_remote_copy(src_ref=src, dst_ref=dst, send_sem=send_sems.at[k], recv_sem=recv_sems.at[k],
                                                device_id=to, device_id_type=MESH)

        to_first = [remote(part_ref.at[first], direct, 0, nbr1), remote(part_ref.at[2], theirs, 1, nbr1)]
        to_second = remote(theirs, relayed, 2, nbr2)
        load_mine = pltpu.make_async_copy(part_ref.at[second], mine, local_sems.at[0])

        @pl.when(i == 0)
        def _():
            for cp in load_win:
                cp.start()
            acc_ref[...] = jnp.zeros_like(acc_ref)
            to_first[1].start()
            load_mine.start()

        @pl.when(i == late)
        def _():
            to_first[0].start()

        dxn = None
        for j in range(N_DEV):
            pl.when(i == 0)(load_win[j].wait)
            term = _dot_nt(dp_ref[:, j * cols:(j + 1) * cols], win_ref[j])
            dxn = term if j == 0 else dxn + term
        xv = x_ref[...]
        r0 = lax.rsqrt(jnp.mean(xv * xv, axis=-1, keepdims=True) + RMS_EPS)
        xhat = xv * r0
        acc_ref[...] += (dxn * xhat).reshape(tb // SUBLANES, SUBLANES, d).sum(axis=0)
        dxh = dxn * sp_ref[SP_LN_G:SP_LN_G + 1, :]
        gx_ref[...] = dh_ref[...] + r0 * (dxh - xhat * jnp.mean(dxh * xhat, axis=-1, keepdims=True))

        @pl.when(i == mid)
        def _():
            to_first[1].wait_recv()
            load_mine.wait()

            def add(r, carry):
                rows = pl.ds(r, rc)
                theirs[rows, :] = (mine[rows, :].astype(F32) + theirs[rows, :].astype(F32)).astype(MM)
                return carry

            _chunks(mine.shape[0], rc, add, 0)
            to_second.start()

        @pl.when(i == nb - 1)
        def _():
            to_first[0].wait_recv()
            to_second.wait_recv()
            for cp in to_first + [to_second]:
                cp.wait_send()
            ln_all[4 * x_ + 2 * y_ + c_] = jnp.broadcast_to(jnp.sum(acc_ref[...], axis=0, keepdims=True), acc_ref.shape)
            gather = _Gather(lambda a, px, py, pc: ln_all.at[4 * px + 2 * py + pc], ln_send, ln_recv)
            gather.start_own(0)
            gather.finish(0)
            total = ln_all[0]
            for dev in range(1, N_DEV):
                total = total + ln_all[dev]
            ln_ref[...] = total

    vm = pl.BlockSpec(memory_space=pltpu.VMEM)
    hbm = pl.BlockSpec(memory_space=pl.ANY)
    blk = lambda w: pl.BlockSpec((tb, w), lambda i: (i, 0))
    landed = jax.ShapeDtypeStruct(part.shape[1:], part.dtype)
    outs = pl.pallas_call(
        body, name="input_grad", grid=(nb,),
        in_specs=[blk(6 * d), hbm, blk(d), blk(d), vm, hbm],
        out_specs=(blk(d), pl.BlockSpec((SUBLANES, d), lambda i: (0, 0)), hbm, hbm),
        out_shape=(jax.ShapeDtypeStruct((t_len, d), F32), jax.ShapeDtypeStruct((SUBLANES, d), F32), landed, landed),
        scratch_shapes=[pltpu.SemaphoreType.DMA((3,)), pltpu.SemaphoreType.DMA((3,)), pltpu.SemaphoreType.DMA((1,)),
                        pltpu.VMEM((SUBLANES, d), F32), pltpu.VMEM((N_DEV, SUBLANES, d), F32),
                        pltpu.SemaphoreType.DMA((7,)), pltpu.SemaphoreType.DMA((7,)),
                        pltpu.VMEM(part.shape[1:], MM), pltpu.VMEM(part.shape[1:], MM),
                        pltpu.VMEM(win_all.shape, MM), pltpu.SemaphoreType.DMA((N_DEV,))],
        compiler_params=_params(dimension_semantics=("arbitrary",)),
    )(dp, win_all, x, dh, sp, part)
    return outs[0], outs[1], (outs[2], outs[3])


_CHIP_RELATIONS = [(0, 0), (1, 0), (0, 1), (1, 1)]


def _related_block(k, core):
    x, y, _ = _mesh_pos()
    fx, fy = _CHIP_RELATIONS[k]
    return 4 * (x ^ fx) + 2 * (y ^ fy) + core


class _ChipExchange:
    def __init__(self, part_refs, land_refs, send_sems, recv_sems):
        self.part_refs, self.land_refs, self.send_sems, self.recv_sems = part_refs, land_refs, send_sems, recv_sems

    def copies(self):
        x, y, c = _mesh_pos()
        for a in range(len(self.part_refs)):
            for k in (1, 2, 3):
                fx, fy = _CHIP_RELATIONS[k]
                yield pltpu.make_async_remote_copy(
                    src_ref=self.part_refs[a].at[k - 1], dst_ref=self.land_refs[a].at[k - 1],
                    send_sem=self.send_sems.at[3 * a + k - 1], recv_sem=self.recv_sems.at[3 * a + k - 1],
                    device_id=(x ^ fx, y ^ fy, c), device_id_type=MESH)

    def start(self):
        for cp in self.copies():
            cp.start()

    def finish(self):
        for cp in self.copies():
            cp.wait_recv()
        for cp in self.copies():
            cp.wait_send()


def _weight_grad_stage1(name, blk_shape, n_split, operands, in_specs, product, riders=(), slabs=()):
    n_rows, n_cols = blk_shape
    rs = n_rows // n_split
    rc = 32
    n_in, n_ride, n_slab = len(operands), len(riders), len(slabs)
    _, _, c = _mesh_pos()
    order = jnp.stack([_related_block(k, 1 - c) for k in range(4)]
                      + [_related_block(k, c) for k in (1, 2, 3, 0)]).astype(jnp.int32)

    def body(order_ref, *refs):
        ins = refs[:n_in]
        ride_in = refs[n_in:n_in + n_ride]
        slab_in = refs[n_in + n_ride:n_in + n_ride + n_slab]
        n_op = n_in + n_ride + n_slab
        part_ref, own_ref = refs[n_op:n_op + 2]
        ride_out = refs[n_op + 2:n_op + 2 + n_ride]
        gathered = refs[n_op + 2 + n_ride:n_op + 2 + n_ride + n_slab]
        (gbuf, sendbuf, from_sib, send_sems, recv_sems, ride_send, ride_recv,
         slab_send, slab_recv, slab_local) = refs[n_op + 2 + n_ride + n_slab:]
        exchange = _ChipExchange(ride_in, ride_out, ride_send, ride_recv)
        s = pl.program_id(0)
        x, y, c = _mesh_pos()
        me = 4 * x + 2 * y + c
        gather = _BalancedGather(lambda a, px, py, pc: gathered[a].at[4 * px + 2 * py + pc], slab_send, slab_recv, slab_in)
        keep_own = [pltpu.make_async_copy(slab_in[a], gathered[a].at[me], slab_local.at[a]) for a in range(n_slab)]

        def to_sibling(k):
            return pltpu.make_async_remote_copy(
                src_ref=sendbuf.at[k], dst_ref=from_sib.at[k], send_sem=send_sems.at[k], recv_sem=recv_sems.at[k],
                device_id=(x, y, 1 - c), device_id_type=MESH)

        @pl.when(s == 0)
        def _():
            exchange.start()
            for a in range(n_slab):
                gather.start_own(a)
                keep_own[a].start()

        @pl.when(s == 5)
        def _():
            for a in range(n_slab):
                gather.on_neighbour(a, 0)
                gather.on_neighbour(a, 1)

        @pl.when(s == 7)
        def _():
            for a in range(n_slab):
                gather.on_diagonal(a)

        for h in range(n_split):
            gbuf[h * rs:(h + 1) * rs, :] = product(ins, h)

        @pl.when(s < 4)
        def _():
            def narrow(r, carry):
                sendbuf[s, pl.ds(r, rc), :] = gbuf[pl.ds(r, rc), :].astype(MM)
                return carry

            _chunks(n_rows, rc, narrow, 0)
            to_sibling(s).start()

        @pl.when(s >= 4)
        def _():
            k = jnp.where(s == 7, 0, s - 3)
            to_sibling(k).wait_recv()

            @pl.when(s < 7)
            def _():
                def add(r, carry):
                    rows = pl.ds(r, rc)
                    part_ref[0, rows, :] = (gbuf[rows, :] + from_sib[k, rows, :].astype(F32)).astype(MM)
                    return carry

                _chunks(n_rows, rc, add, 0)

            @pl.when(s == 7)
            def _():
                def add(r, carry):
                    rows = pl.ds(r, rc)
                    own_ref[rows, :] = gbuf[rows, :] + from_sib[0, rows, :].astype(F32)
                    return carry

                _chunks(n_rows, rc, add, 0)
                for kk in range(4):
                    to_sibling(kk).wait_send()
                exchange.finish()
                for a in range(n_slab):
                    gather.wait_sibling(a)
                    for j in range(3):
                        gather.wait_passed_on(a, j)
                    gather.wait_sends(a)
                    keep_own[a].wait()

    hbm = pl.BlockSpec(memory_space=pl.ANY)
    grid_spec = pltpu.PrefetchScalarGridSpec(
        num_scalar_prefetch=1, grid=(N_DEV,), in_specs=list(in_specs) + [hbm] * (n_ride + n_slab),
        out_specs=(pl.BlockSpec((1, n_rows, n_cols), lambda s, o: (jnp.clip(s - 4, 0, 2), 0, 0)),
                   pl.BlockSpec((n_rows, n_cols), lambda s, o: (0, 0))) + (hbm,) * (n_ride + n_slab),
        scratch_shapes=[pltpu.VMEM((n_rows, n_cols), F32), pltpu.VMEM((4, n_rows, n_cols), MM),
                        pltpu.VMEM((4, n_rows, n_cols), MM),
                        pltpu.SemaphoreType.DMA((4,)), pltpu.SemaphoreType.DMA((4,)),
                        pltpu.SemaphoreType.DMA((max(3 * n_ride, 1),)), pltpu.SemaphoreType.DMA((max(3 * n_ride, 1),)),
                        pltpu.SemaphoreType.DMA((max(8 * n_slab, 1),)), pltpu.SemaphoreType.DMA((max(8 * n_slab, 1),)),
                        pltpu.SemaphoreType.DMA((max(n_slab, 1),))])
    outs = pl.pallas_call(
        body, name=name, grid_spec=grid_spec,
        out_shape=(jax.ShapeDtypeStruct((3, n_rows, n_cols), MM), jax.ShapeDtypeStruct((n_rows, n_cols), F32))
                  + tuple(jax.ShapeDtypeStruct(p.shape, p.dtype) for p in riders)
                  + tuple(jax.ShapeDtypeStruct((N_DEV,) + a.shape, a.dtype) for a in slabs),
        compiler_params=_params(dimension_semantics=("arbitrary",)),
    )(order, *operands, *riders, *slabs)
    return outs[0], outs[1], outs[2:2 + n_ride], outs[2 + n_ride:]


def _weight_grad_in(xnt, dp, riders, slabs):
    d, t_len = xnt.shape
    cols = dp.shape[1] // N_DEV
    half = d // 2
    return _weight_grad_stage1(
        "weight_grad_in", (d, cols), 2, (xnt, dp),
        [pl.BlockSpec(memory_space=pltpu.VMEM), pl.BlockSpec((t_len, cols), lambda s, o: (0, o[s]))],
        lambda refs, h: _dot(refs[0][h * half:(h + 1) * half, :], refs[1][...]), riders, slabs)


def _weight_grad_out(yt, dhb):
    d2, t_len = yt.shape
    d = dhb.shape[1]
    rows = d2 // N_DEV
    return _weight_grad_stage1(
        "weight_grad_out", (rows, d), 1, (yt, dhb),
        [pl.BlockSpec((rows, t_len), lambda s, o: (o[s], 0)), pl.BlockSpec(memory_space=pltpu.VMEM)],
        lambda refs, h: _dot(refs[0][...], refs[1][...]))


def _update_shard(own, others, w, m, v, name):
    n_rows, n_cols = w.shape
    rb = min(256, n_rows)
    n_other = len(others)

    def body(own_ref, *refs):
        other_refs = refs[:n_other]
        w_ref, m_ref, v_ref, grad_ref, delta_ref, mo_ref, vo_ref = refs[n_other:]
        g = own_ref[...]
        for ref in other_refs:
            for k in range(ref.shape[0] if len(ref.shape) == 3 else 1):
                g = g + (ref[k] if len(ref.shape) == 3 else ref[...]).astype(F32)
        delta, m_new, v_new = _adamw(w_ref[...], g, m_ref[...], v_ref[...])
        grad_ref[...] = g
        delta_ref[...] = delta
        mo_ref[...] = m_new
        vo_ref[...] = v_new

    blk = pl.BlockSpec((rb, n_cols), lambda i: (i, 0))
    stacked = lambda n: pl.BlockSpec((n, rb, n_cols), lambda i: (0, i, 0))
    out = jax.ShapeDtypeStruct((n_rows, n_cols), F32)
    return pl.pallas_call(
        body, name=name, grid=(n_rows // rb,),
        in_specs=[blk] + [stacked(o.shape[0]) if o.ndim == 3 else blk for o in others] + [blk, blk, blk],
        out_specs=(blk, blk, blk, blk), out_shape=(out, out, out, out),
        compiler_params=_params(dimension_semantics=("arbitrary",)),
    )(own, *others, w, m, v)


def _small_update(gat_v, gat_g, ln_tot, vec_w, vec_m, vec_v, gates, convs):
    n_vec = len(vec_w)
    n_heads, hd, _ = gates[0].shape
    tw = gat_g.shape[2]
    s8 = SUBLANES
    per = tw // hd
    n_tiles = n_heads // per
    cc = convs[0].shape[1]
    n_in = 3 + 3 * n_vec + 12

    def body(*refs):
        gv_ref, gg_ref, ln_ref = refs[:3]
        w_refs, m_refs, v_refs = (refs[3 + j * n_vec:3 + (j + 1) * n_vec] for j in range(3))
        gate_refs = refs[3 + 3 * n_vec:3 + 3 * n_vec + 6]
        conv_refs = refs[3 + 3 * n_vec + 6:n_in]
        loss_o = refs[n_in]
        kinds = [refs[n_in + 1 + j * (n_vec + 4):n_in + 1 + (j + 1) * (n_vec + 4)] for j in range(4)]
        tv, tg = refs[n_in + 1 + 4 * (n_vec + 4):]
        x, y, c = _mesh_pos()
        me = 4 * x + 2 * y + c

        def emit(k_out, w, g, m, v):
            delta, m_new, v_new = _adamw(w, g, m, v)
            for ref, val in zip(k_out, (g, delta, m_new, v_new)):
                ref[...] = val

        total = gv_ref[0]
        for dev in range(1, N_DEV):
            total = total + gv_ref[dev]
        tv[...] = total
        tv[SL_LN_G:SL_LN_G + 1, :] = ln_ref[0:1, :]

        def sum_gates(r, carry):
            rows = pl.ds(r, 2 * s8)
            part = gg_ref[0, rows, :].astype(F32)
            for dev in range(1, N_DEV):
                part = part + gg_ref[dev, rows, :].astype(F32)
            tg[rows, :] = part
            return carry

        _chunks(tg.shape[0], 2 * s8, sum_gates, 0)
        loss_o[...] = jnp.broadcast_to(tv[SL_LOSS:SL_LOSS + 1, 0:LANES], loss_o.shape)
        for p in range(n_vec):
            w, g = w_refs[p][...], tv[SL_LN_G + p, :]
            if SL_LN_G + p == SL_LAM:
                g = g * (RG_LRU_C * jax.nn.sigmoid(-w))
            emit([k_out[p] for k_out in kinds], w, g, m_refs[p][...], v_refs[p][...])
        lanes = pl.ds(pl.multiple_of(me * cc, cc), cc)
        for j, (row0, n) in enumerate(((SL_CONV_W, 3), (SL_LRU_W, 4))):
            w_ref, m_ref, v_ref = conv_refs[3 * j:3 * j + 3]
            emit([k_out[n_vec + 2 + j] for k_out in kinds], w_ref[...], tv[row0:row0 + n, lanes], m_ref[...], v_ref[...])
        for mtx in range(2):
            w_ref, m_ref, v_ref = gate_refs[3 * mtx:3 * mtx + 3]
            for k in range(n_tiles):
                tile = tg[(mtx * n_tiles + k) * hd:(mtx * n_tiles + k + 1) * hd, :]
                for a in range(per):
                    head = k * per + a
                    g = tile[:, a * hd:(a + 1) * hd]
                    delta, m_new, v_new = _adamw(w_ref[head], g, m_ref[head], v_ref[head])
                    for k_out, val in zip(kinds, (g, delta, m_new, v_new)):
                        k_out[n_vec + mtx][head] = val

    vm = pl.BlockSpec(memory_space=pltpu.VMEM)
    like = lambda a: jax.ShapeDtypeStruct(a.shape, F32)
    per_kind = tuple(like(a) for a in vec_w) + (like(gates[0]), like(gates[3]), like(convs[0]), like(convs[3]))
    n_out = 1 + 4 * len(per_kind)
    outs = pl.pallas_call(
        body, name="small_update",
        in_specs=[vm] * n_in, out_specs=(vm,) * n_out,
        out_shape=(jax.ShapeDtypeStruct((SUBLANES, LANES), F32),) + per_kind * 4,
        scratch_shapes=[pltpu.VMEM(gat_v.shape[1:], F32), pltpu.VMEM(gat_g.shape[1:], F32)],
        compiler_params=_params(),
    )(gat_v, gat_g, ln_tot, *vec_w, *vec_m, *vec_v, *gates, *convs)
    return outs[0], [outs[1 + j * len(per_kind):1 + (j + 1) * len(per_kind)] for j in range(4)]


def _head_ones(head_dim, tw):
    lane = jnp.arange(tw) // head_dim
    return (lane[:, None] == lane[None, :]).astype(MM)


def kernel(x, ln_g, w_in, conv_w, lru_conv_w, lru_conv_b, w_a, b_a, w_i, b_i, lam, conv_out_g, lru_out_g, w_out, final_g, loss_target, m_ln_g, m_w_in, m_conv_w, m_lru_conv_w, m_lru_conv_b, m_w_a, m_b_a, m_w_i, m_b_i, m_lam, m_conv_out_g, m_lru_out_g, m_w_out, m_final_g, v_ln_g, v_w_in, v_conv_w, v_lru_conv_w, v_lru_conv_b, v_w_a, v_b_a, v_w_i, v_b_i, v_lam, v_conv_out_g, v_lru_out_g, v_w_out, v_final_g):
    _, t_len, d = x.shape
    hd_l = d // N_LRU_HEADS
    tw = min(MXU_TILE, d)
    x2, tgt2 = x[0], loss_target[0]

    small = [ln_g, lru_conv_b, b_a, b_i, lam, conv_out_g, lru_out_g, final_g]
    p, xnt, win_all, wout_all, _, sp, wa_t, wi_t = _gather_project(
        x2, w_in, w_out, conv_w, lru_conv_w, ln_g.reshape(1, d), w_a, w_i, small, min(256, t_len), tw)
    wout_full = wout_all.reshape(N_DEV * w_out.shape[0], d)
    ones_c, ones_l = _head_ones(d // N_CONV_HEADS, tw), _head_ones(hd_l, tw)

    h, dh, dhb, facc, *saved = _forward(x2, tgt2, p, wout_full, wa_t, wi_t, sp, ones_c, ones_l, min(256, t_len))
    dp, yt, slab_v, slab_g = _backward(p, h, dh, saved, facc, wout_full, wa_t, wi_t, sp, ones_c, ones_l, min(256, t_len))
    part_out, own_out, _, _ = _weight_grad_out(yt, dhb)
    part_in, own_in, (chips_out,), (gat_v, gat_g) = _weight_grad_in(xnt, dp, (part_out,), (slab_v, slab_g))
    grad_x, ln_tot, sums_in = _input_grad(dp, win_all, x2, dh, sp, part_in, min(512, t_len))
    gw_in, dw_in, mw_in, vw_in = _update_shard(own_in, sums_in, w_in, m_w_in, v_w_in, "update_w_in")
    gw_out, dw_out, mw_out, vw_out = _update_shard(own_out, (chips_out,), w_out, m_w_out, v_w_out, "update_w_out")

    loss_tile, kinds = _small_update(
        gat_v, gat_g, ln_tot, small,
        [m_ln_g, m_lru_conv_b, m_b_a, m_b_i, m_lam, m_conv_out_g, m_lru_out_g, m_final_g],
        [v_ln_g, v_lru_conv_b, v_b_a, v_b_i, v_lam, v_conv_out_g, v_lru_out_g, v_final_g],
        (w_a, m_w_a, v_w_a, w_i, m_w_i, v_w_i), (conv_w, m_conv_w, v_conv_w, lru_conv_w, m_lru_conv_w, v_lru_conv_w))

    def unpack(kind, big_in, big_out):
        vec, (wa_, wi_, cw_, lw_) = kind[:len(small)], kind[len(small):]
        return [vec[0], big_in, cw_, lw_, vec[1], wa_, vec[2], wi_, vec[3], vec[4], vec[5], vec[6], big_out, vec[7]]

    return (loss_tile[0, 0], grad_x[None], *unpack(kinds[0], gw_in, gw_out), *unpack(kinds[1], dw_in, dw_out),
            *unpack(kinds[2], mw_in, mw_out), *unpack(kinds[3], vw_in, vw_out))
```

```python
import jax
import jax.numpy as jnp
from jax import lax
from jax.experimental import pallas as pl
from jax.experimental.pallas import tpu as pltpu

F32 = jnp.float32
MM = jnp.bfloat16
MESH = pl.DeviceIdType.MESH

N_DEV = 8
N_CONV_HEADS = 8
N_LRU_HEADS = 16
RG_LRU_C = 8.0
RMS_EPS = 1e-6
ADAM_LR, ADAM_B1, ADAM_B2, ADAM_EPS, ADAM_WD, ADAM_STEP = 0.001, 0.9, 0.999, 1e-08, 0.01, 10
ADAM_BC1 = 1.0 - ADAM_B1 ** ADAM_STEP
ADAM_BC2 = 1.0 - ADAM_B2 ** ADAM_STEP

SUBLANES = 8
LANES = 128
MXU_TILE = 256
VMEM_LIMIT = 56 * 1024 * 1024

SP_LN_G, SP_LRU_B, SP_B_A, SP_B_I, SP_LAM, SP_CONV_G, SP_LRU_G, SP_FINAL_G, SP_CONV_W, SP_LRU_W = 0, 1, 2, 3, 4, 5, 6, 7, 8, 11
SP_ROWS = 16
P_B, P_C, P_XC, P_GC, P_XL, P_GL = 0, 1, 2, 3, 4, 5
A_CONV_G, A_LRU_G, A_LAM, A_B_A, A_B_I, A_CONV_W, A_LRU_W, A_LRU_B = 0, 1, 2, 3, 4, 5, 8, 12
A_GROUPS = 13
SL_LOSS, SL_LN_G, SL_LRU_B, SL_B_A, SL_B_I, SL_LAM, SL_CONV_G, SL_LRU_G, SL_FINAL_G, SL_CONV_W, SL_LRU_W = 0, 1, 2, 3, 4, 5, 6, 7, 8, 16, 24
SL_ROWS = 32


def _params(vmem=True, **kw):
    if vmem:
        kw["vmem_limit_bytes"] = VMEM_LIMIT
    return pltpu.CompilerParams(**kw)


def _dot(a, b):
    return jnp.dot(a, b, preferred_element_type=F32)


def _dot_nt(a, b):
    return lax.dot_general(a, b, (((1,), (1,)), ((), ())), preferred_element_type=F32)


def _head_sums(v, ones_tile):
    tw = ones_tile.shape[0]
    vb = v.astype(MM)
    return jnp.concatenate([_dot(vb[:, k:k + tw], ones_tile) for k in range(0, v.shape[1], tw)], axis=1)


def _head_rstd(v, ones_tile, head_dim):
    return lax.rsqrt(_head_sums(v * v, ones_tile) * (1.0 / head_dim) + RMS_EPS)


def _sigmoid(x):
    return 0.5 * jnp.tanh(0.5 * x) + 0.5


def _lru_input_scale_sq(log_a, a):
    return -jnp.tanh(log_a) * (1.0 + a * a)


def _log_sigmoid(x):
    z = jnp.exp(-jnp.abs(x))
    u = 1.0 + z
    log1p_z = jnp.where(u == 1.0, z, jnp.log(u) * (z / (u - 1.0)))
    return jnp.minimum(x, 0.0) - log1p_z


def _row_iota(d):
    return lax.broadcasted_iota(jnp.int32, (SUBLANES, d), 0)


def _shift_down(cur, prev, s, row):
    return jnp.where(row >= s, pltpu.roll(cur, s, axis=0), pltpu.roll(prev, s, axis=0))


def _shift_up(cur, nxt, s, row):
    k = SUBLANES - s
    return jnp.where(row < k, pltpu.roll(cur, k, axis=0), pltpu.roll(nxt, k, axis=0))


def _scan_fwd(a, b, h_prev, row):
    for s in (1, 2, 4):
        a_s = jnp.where(row >= s, pltpu.roll(a, s, axis=0), 1.0)
        b_s = jnp.where(row >= s, pltpu.roll(b, s, axis=0), 0.0)
        b = a * b_s + b
        a = a * a_s
    return a * h_prev + b


def _scan_bwd(a_next, b, g_next, row):
    a = a_next
    for s in (1, 2, 4):
        k = SUBLANES - s
        a_s = jnp.where(row < k, pltpu.roll(a, k, axis=0), 1.0)
        b_s = jnp.where(row < k, pltpu.roll(b, k, axis=0), 0.0)
        b = a * b_s + b
        a = a * a_s
    return a * g_next + b


def _bcast_row(v, r):
    return jnp.broadcast_to(v[r:r + 1, :], v.shape)


def _chunks(n_rows, rc, body, init, reverse=False):
    n = n_rows // rc

    def step(i, carry):
        j = (n - 1 - i) if reverse else i
        return body(pl.multiple_of(j * rc, rc), carry)

    return lax.fori_loop(0, n, step, init)


def _adamw(w, g, m, v):
    m = ADAM_B1 * m + (1.0 - ADAM_B1) * g
    v = ADAM_B2 * v + (1.0 - ADAM_B2) * (g * g)
    m_hat = m / ADAM_BC1
    v_hat = v / ADAM_BC2
    delta = -ADAM_LR * (m_hat / (jnp.sqrt(v_hat) + ADAM_EPS) + ADAM_WD * w)
    return delta, m, v


def _mesh_pos():
    return lax.axis_index("x"), lax.axis_index("y"), lax.axis_index("c")


class _Gather:
    def __init__(self, blocks_of, send_sems, recv_sems, own_src=None):
        x, y, c = _mesh_pos()
        self.c = c
        self.me, self.sibling = (x, y, c), (x, y, 1 - c)
        self.chips = [(1 - x, y), (x, 1 - y), (1 - x, 1 - y)]
        self.blocks_of, self.send_sems, self.recv_sems = blocks_of, send_sems, recv_sems
        self.own_src = own_src

    def copy(self, a, k, block, to):
        src = self.blocks_of(a, *block)
        if block is self.me and self.own_src is not None:
            src = self.own_src[a]
        return pltpu.make_async_remote_copy(
            src_ref=src, dst_ref=self.blocks_of(a, *block),
            send_sem=self.send_sems.at[a * 7 + k], recv_sem=self.recv_sems.at[a * 7 + k],
            device_id=to, device_id_type=MESH)

    def start_own(self, a):
        self.copy(a, 0, self.me, self.sibling).start()
        for j, chip in enumerate(self.chips):
            self.copy(a, 1 + j, self.me, (*chip, self.c)).start()

    def wait_sibling(self, a):
        self.copy(a, 0, self.sibling, self.me).wait_recv()

    def wait_chip_and_pass_on(self, a, j):
        block = (*self.chips[j], self.c)
        self.copy(a, 1 + j, block, self.me).wait_recv()
        self.copy(a, 4 + j, block, self.sibling).start()

    def wait_passed_on(self, a, j):
        self.copy(a, 4 + j, (*self.chips[j], 1 - self.c), self.me).wait_recv()

    def wait_sends(self, a):
        self.copy(a, 0, self.me, self.sibling).wait_send()
        for j, chip in enumerate(self.chips):
            self.copy(a, 1 + j, self.me, (*chip, self.c)).wait_send()
            self.copy(a, 4 + j, (*chip, self.c), self.sibling).wait_send()

    def finish(self, a):
        for j in range(3):
            self.wait_chip_and_pass_on(a, j)
        self.wait_sibling(a)
        for j in range(3):
            self.wait_passed_on(a, j)
        self.wait_sends(a)


class _BalancedGather:
    def __init__(self, slot, send_sems, recv_sems, own_src):
        x, y, c = _mesh_pos()
        self.c = c
        self.me, self.sibling = (x, y, c), (x, y, 1 - c)
        self.chips = [(1 - x, y), (x, 1 - y), (1 - x, 1 - y)]
        self.slot, self.send_sems, self.recv_sems, self.own_src = slot, send_sems, recv_sems, own_src

    def half(self, a, block, which):
        ref = self.slot(a, *block)
        n = ref.shape[0] // 2
        return ref.at[pl.ds(which * n, n)]

    def copy(self, a, k, src, dst, to):
        return pltpu.make_async_remote_copy(
            src_ref=src, dst_ref=dst, send_sem=self.send_sems.at[a * 8 + k], recv_sem=self.recv_sems.at[a * 8 + k],
            device_id=to, device_id_type=MESH)

    def whole(self, a, k, block, to):
        src = self.own_src[a] if block is self.me else self.slot(a, *block)
        return self.copy(a, k, src, self.slot(a, *block), to)

    def halved(self, a, k, block, which, to):
        return self.copy(a, k, self.half(a, block, which), self.half(a, block, which), to)

    def on(self, chip):
        return (*self.chips[chip], self.c)

    def start_own(self, a):
        self.whole(a, 0, self.me, self.sibling).start()
        self.whole(a, 1, self.me, self.on(0)).start()
        self.whole(a, 2, self.me, self.on(1)).start()

    def start_own_staggered(self, a):
        self.whole(a, 0, self.me, self.sibling).start()
        for core, order in ((1, (0, 1)), (0, (1, 0))):
            @pl.when(self.c == core)
            def _(order=order):
                for j in order:
                    self.whole(a, 1 + j, self.me, self.on(j)).start()

    def wait_sibling(self, a):
        self.whole(a, 0, self.sibling, self.me).wait_recv()

    def on_neighbour(self, a, j):
        self.whole(a, 1 + j, self.on(j), self.me).wait_recv()
        self.halved(a, 3 + j, self.on(j), j, self.on(1 - j)).start()
        self.whole(a, 5 + j, self.on(j), self.sibling).start()

    def on_diagonal(self, a):
        self.halved(a, 3, self.on(2), 0, self.me).wait_recv()
        self.halved(a, 4, self.on(2), 1, self.me).wait_recv()
        self.whole(a, 7, self.on(2), self.sibling).start()

    def wait_passed_on(self, a, j):
        self.whole(a, 5 + j, (*self.chips[j], 1 - self.c), self.me).wait_recv()

    def wait_sends(self, a):
        self.whole(a, 0, self.me, self.sibling).wait_send()
        for j in range(2):
            self.whole(a, 1 + j, self.me, self.on(j)).wait_send()
            self.halved(a, 3 + j, self.on(j), j, self.on(1 - j)).wait_send()
        for j in range(3):
            self.whole(a, 5 + j, self.on(j), self.sibling).wait_send()


def _block_order():
    x, y, c = _mesh_pos()
    idx = lambda chip, core: 4 * chip[0] + 2 * chip[1] + core
    own, first, second, diag = (x, y), (x ^ c, y ^ (1 - c)), (x ^ (1 - c), y ^ c), (1 - x, 1 - y)
    order = [idx(own, c), idx(own, 1 - c), idx(first, c), idx(second, 1 - c), idx(second, c), idx(first, 1 - c),
             idx(diag, c), idx(diag, 1 - c)]
    return jnp.stack(order).astype(jnp.int32)


def _gather_project(x, w_in, w_out, conv_w, lru_conv_w, ln_g, w_a, w_i, vecs, tb, tw):
    t_len, d = x.shape
    nb = t_len // tb
    cols = w_in.shape[1]
    mc = min(512, t_len)
    conv_pack = jax.ShapeDtypeStruct((SUBLANES, conv_w.shape[1]), F32)
    srcs = (w_in, w_out, conv_pack)
    dts = (MM, MM, F32)
    n_vec = len(vecs)
    n_heads, hd, _ = w_a.shape
    per = tw // hd

    def body(order_ref, x_ref, win_ref, wout_ref, cw_ref, lw_ref, lng_ref, wa_ref, wi_ref, *refs):
        vec_refs = refs[:n_vec]
        (p_ref, xnt_ref, win_all, wout_all, cp_all, sp_ref, wat_ref, wit_ref,
         xnb, wall, st_out, st_cp, cp_vm, send_sems, recv_sems, cp_send, cp_recv, local_sems) = refs[n_vec:]
        i = pl.program_id(0)
        x_, y_, c_ = _mesh_pos()
        me = 4 * x_ + 2 * y_ + c_
        outs = (win_all, wout_all, cp_all)
        lands = (wall, wout_all, cp_all)
        stages = (wall.at[me], st_out, st_cp)
        gather = _BalancedGather(lambda a, px, py, pc: lands[a].at[4 * px + 2 * py + pc], send_sems, recv_sems, stages)
        small = _Gather(lambda a, px, py, pc: cp_all.at[4 * px + 2 * py + pc], cp_send, cp_recv, own_src=[st_cp])
        keep_own = [pltpu.make_async_copy(stages[a], outs[a].at[me], local_sems.at[a]) for a in range(3)]

        def keep(k):
            blk = order_ref[k]
            return pltpu.make_async_copy(wall.at[blk], win_all.at[blk], local_sems.at[2 + k])

        @pl.when(i == 0)
        def _():
            for a, src in enumerate((win_ref, wout_ref)):
                dst, rc = stages[a], 32

                def cast(r, carry, src=src, dst=dst):
                    dst[pl.ds(r, rc), :] = src[pl.ds(r, rc), :].astype(dst.dtype)
                    return carry

                _chunks(src.shape[0], rc, cast, 0)
                keep_own[a].start()
            gather.start_own_staggered(0)
            n_cw, n_lw = cw_ref.shape[0], lw_ref.shape[0]
            st_cp[...] = jnp.zeros_like(st_cp)
            st_cp[0:n_cw, :] = cw_ref[...]
            st_cp[n_cw:n_cw + n_lw, :] = lw_ref[...]
            keep_own[2].start()

        @pl.when(i < nb)
        def _():
            xv = x_ref[...]
            r0 = lax.rsqrt(jnp.mean(xv * xv, axis=-1, keepdims=True) + RMS_EPS)
            xn = xv * r0 * lng_ref[...]
            xnb[pl.ds(pl.multiple_of(i * tb, tb), tb), :] = xn.astype(MM)
            xnt_ref[...] = xn.T.astype(MM)

        def by_core(action):
            for core, (first, second) in ((1, (0, 1)), (0, (1, 0))):
                @pl.when(c_ == core)
                def _(first=first, second=second):
                    action(first, second)

        for k in range(N_DEV):
            @pl.when(i == nb + k)
            def _(k=k):
                if k == 1:
                    gather.wait_sibling(0)
                elif k == 2:
                    by_core(lambda first, second: gather.on_neighbour(0, first))
                    gather.start_own(1)
                    small.start_own(0)
                elif k == 3:
                    by_core(lambda first, second: gather.wait_passed_on(0, second))
                elif k == 4:
                    by_core(lambda first, second: gather.on_neighbour(0, second))
                elif k == 5:
                    by_core(lambda first, second: gather.wait_passed_on(0, first))
                    gather.on_neighbour(1, 0)
                    gather.on_neighbour(1, 1)
                elif k == 6:
                    gather.on_diagonal(0)
                elif k == 7:
                    gather.wait_passed_on(0, 2)
                    gather.on_diagonal(1)
                blk = order_ref[k]
                if k:
                    keep(k).start()

                def project(r, carry):
                    rows = pl.ds(r, mc)
                    p_ref[rows, :] = _dot(xnb[rows, :], wall[blk]).astype(MM)
                    return carry

                _chunks(t_len, mc, project, 0)
                if k == N_DEV - 1:
                    gather.wait_sends(0)
                    gather.wait_sibling(1)
                    for j in range(3):
                        gather.wait_passed_on(1, j)
                    gather.wait_sends(1)
                    small.finish(0)
                    for cp in keep_own + [keep(kk) for kk in range(1, N_DEV)]:
                        cp.wait()
                    load = pltpu.make_async_copy(cp_all, cp_vm, local_sems.at[N_DEV + 2])
                    load.start()
                    load.wait()
                    for r, ref in enumerate(vec_refs):
                        sp_ref[r, :] = ref[...]
                    sp_ref[n_vec:n_vec + SUBLANES, :] = jnp.concatenate([cp_vm[dev] for dev in range(N_DEV)], axis=1)
                    for src, dst in ((wa_ref, wat_ref), (wi_ref, wit_ref)):
                        dst[...] = jnp.zeros_like(dst)
                        for head in range(n_heads):
                            lo = (head % per) * hd
                            dst[head // per, lo:lo + hd, lo:lo + hd] = src[head].astype(MM)

    vm = pl.BlockSpec(memory_space=pltpu.VMEM)
    hbm = pl.BlockSpec(memory_space=pl.ANY)
    grid_spec = pltpu.PrefetchScalarGridSpec(
        num_scalar_prefetch=1, grid=(nb + N_DEV,),
        in_specs=[pl.BlockSpec((tb, d), lambda i, o: (jnp.minimum(i, nb - 1), 0))] + [vm] * (7 + n_vec),
        out_specs=(pl.BlockSpec((t_len, cols), lambda i, o: (0, o[jnp.maximum(i - nb, 0)])),
                   pl.BlockSpec((d, tb), lambda i, o: (0, jnp.minimum(i, nb - 1))), hbm, hbm, hbm,
                   pl.BlockSpec((SP_ROWS, d), lambda i, o: (0, 0)),
                   pl.BlockSpec((n_heads // per, tw, tw), lambda i, o: (0, 0, 0)),
                   pl.BlockSpec((n_heads // per, tw, tw), lambda i, o: (0, 0, 0))),
        scratch_shapes=[pltpu.VMEM((t_len, d), MM), pltpu.VMEM((N_DEV,) + w_in.shape, MM),
                        pltpu.VMEM(w_out.shape, MM), pltpu.VMEM(conv_pack.shape, F32),
                        pltpu.VMEM((N_DEV,) + conv_pack.shape, F32),
                        pltpu.SemaphoreType.DMA((16,)), pltpu.SemaphoreType.DMA((16,)),
                        pltpu.SemaphoreType.DMA((7,)), pltpu.SemaphoreType.DMA((7,)), pltpu.SemaphoreType.DMA((N_DEV + 3,))])
    return pl.pallas_call(
        body, name="gather_project", grid_spec=grid_spec,
        out_shape=(jax.ShapeDtypeStruct((t_len, N_DEV * cols), MM),
                   jax.ShapeDtypeStruct((d, t_len), MM))
                  + tuple(jax.ShapeDtypeStruct((N_DEV,) + s.shape, dt) for s, dt in zip(srcs, dts))
                  + (jax.ShapeDtypeStruct((SP_ROWS, d), F32),)
                  + (jax.ShapeDtypeStruct((n_heads // per, tw, tw), MM),) * 2,
        compiler_params=_params(dimension_semantics=("arbitrary",)),
    )(_block_order(), x, w_in, w_out, conv_w, lru_conv_w, ln_g, w_a, w_i, *vecs)


def _forward(x, tgt, p, wout, wa_t, wi_t, sp, ones_c, ones_l, tb):
    t_len, d = x.shape
    nb = t_len // tb
    n_tiles, tw = wa_t.shape[0], wa_t.shape[1]
    hd_c, hd_l = d // N_CONV_HEADS, d // N_LRU_HEADS
    s8 = SUBLANES

    def body(x_ref, tgt_ref, p_ref, wout_hbm, wa_ref, wi_ref, sp_ref, oc_ref, ol_ref,
             h_ref, dh_ref, dhb_ref, acc_ref, yc, czs, u, pa, pi,
             rcf, rlf, ybuf, tail_z, tail_xl, hcar, wout_ref, wout_sem):
        i = pl.program_id(0)
        row = _row_iota(d)
        load_wout = pltpu.make_async_copy(wout_hbm, wout_ref, wout_sem.at[0])

        @pl.when(i == 0)
        def _():
            load_wout.start()
            tail_z[...] = jnp.zeros_like(tail_z)
            tail_xl[...] = jnp.zeros_like(tail_xl)
            hcar[...] = jnp.zeros_like(hcar)
            acc_ref[...] = jnp.zeros_like(acc_ref)

        def spr(r):
            return sp_ref[r:r + 1, :]

        def proj(rows, seg):
            return p_ref[rows, seg * d:(seg + 1) * d].astype(F32)

        w0, w1, w2 = spr(SP_CONV_W), spr(SP_CONV_W + 1), spr(SP_CONV_W + 2)
        l0, l1, l2, l3 = spr(SP_LRU_W), spr(SP_LRU_W + 1), spr(SP_LRU_W + 2), spr(SP_LRU_W + 3)
        lb = spr(SP_LRU_B)

        def convs(r, carry):
            zp, xp = carry
            rows16 = pl.ds(r, 2 * s8)
            bg16, xl16 = proj(rows16, P_B), proj(rows16, P_XL)
            z16 = proj(rows16, P_C) * proj(rows16, P_XC)
            for j in range(2):
                rows, sub = pl.ds(r + j * s8, s8), slice(j * s8, (j + 1) * s8)
                z, xl = z16[sub], xl16[sub]
                cz = w0 * _shift_down(z, zp, 2, row) + w1 * _shift_down(z, zp, 1, row) + w2 * z
                czs[rows, :] = cz
                yc[rows, :] = bg16[sub] * cz
                u[rows, :] = (l0 * _shift_down(xl, xp, 3, row) + l1 * _shift_down(xl, xp, 2, row)
                              + l2 * _shift_down(xl, xp, 1, row) + l3 * xl + lb)
                zp, xp = z, xl
            return zp, xp

        z_last, xl_last = _chunks(tb, 2 * s8, convs, (tail_z[...], tail_xl[...]))
        tail_z[...] = z_last
        tail_xl[...] = xl_last

        ub = u[...].astype(MM)
        for k in range(n_tiles):
            sl = slice(k * tw, (k + 1) * tw)
            pa[:, sl] = _dot(ub[:, sl], wa_ref[k])
            pi[:, sl] = _dot(ub[:, sl], wi_ref[k])
        rcf[...] = _head_rstd(yc[...], oc_ref[...], hd_c)

        c8 = RG_LRU_C * _log_sigmoid(spr(SP_LAM))
        b_a, b_i = spr(SP_B_A), spr(SP_B_I)

        def lru(r, hp):
            rows = pl.ds(r, SUBLANES)
            ra = _sigmoid(pa[rows, :] + b_a)
            ii = _sigmoid(pi[rows, :] + b_i)
            pa[rows, :] = ra
            pi[rows, :] = ii
            la = ra * c8
            a = jnp.exp(la)
            mult = jnp.sqrt(_lru_input_scale_sq(la, a))
            h = _scan_fwd(a, mult * (ii * u[rows, :]), hp, row)
            h_ref[rows, :] = h
            return _bcast_row(h, SUBLANES - 1)

        hcar[...] = _chunks(tb, SUBLANES, lru, hcar[...])
        rlf[...] = _head_rstd(h_ref[...], ol_ref[...], hd_l)

        g_c, g_l = spr(SP_CONV_G), spr(SP_LRU_G)

        def gate(r, carry):
            rows = pl.ds(r, 2 * s8)
            gc, gl = proj(rows, P_GC), proj(rows, P_GL)
            ybuf[rows, 0:d] = (yc[rows, :] * rcf[rows, :] * g_c * (gc * _sigmoid(gc))).astype(MM)
            ybuf[rows, d:2 * d] = (h_ref[rows, :] * rlf[rows, :] * g_l * (gl * _sigmoid(gl))).astype(MM)
            return carry

        _chunks(tb, 2 * s8, gate, 0)

        pl.when(i == 0)(load_wout.wait)
        hres = x_ref[...] + _dot(ybuf[...], wout_ref[...])
        rf = lax.rsqrt(jnp.mean(hres * hres, axis=-1, keepdims=True) + RMS_EPS)
        hn = hres * rf
        fg = spr(SP_FINAL_G)
        err = hn * fg - tgt_ref[...]
        dout = err * (1.0 / d)
        acc_ref[0:SUBLANES, :] += (err * err).reshape(tb // SUBLANES, SUBLANES, d).sum(axis=0)
        acc_ref[SUBLANES:2 * SUBLANES, :] += (dout * hn).reshape(tb // SUBLANES, SUBLANES, d).sum(axis=0)
        gd = dout * fg
        dhres = rf * (gd - hn * jnp.mean(gd * hn, axis=-1, keepdims=True))
        dh_ref[...] = dhres
        dhb_ref[...] = dhres.astype(MM)

    vm = pl.BlockSpec(memory_space=pltpu.VMEM)
    blk = lambda w: pl.BlockSpec((tb, w), lambda i: (i, 0))
    buf = pltpu.VMEM((tb, d), F32)
    car = pltpu.VMEM((SUBLANES, d), F32)
    return pl.pallas_call(
        body, name="forward", grid=(nb,),
        in_specs=[blk(d), blk(d), blk(6 * d), pl.BlockSpec(memory_space=pl.ANY), vm, vm, vm, vm, vm],
        out_specs=(blk(d), blk(d), blk(d), pl.BlockSpec((2 * SUBLANES, d), lambda i: (0, 0))) + (blk(d),) * 5,
        out_shape=(jax.ShapeDtypeStruct((t_len, d), F32),
                   jax.ShapeDtypeStruct((t_len, d), F32),
                   jax.ShapeDtypeStruct((t_len, d), MM),
                   jax.ShapeDtypeStruct((2 * SUBLANES, d), F32))
                  + (jax.ShapeDtypeStruct((t_len, d), F32),) * 5,
        scratch_shapes=[buf] * 2 + [pltpu.VMEM((tb, 2 * d), MM), car, car, car,
                                    pltpu.VMEM(wout.shape, wout.dtype), pltpu.SemaphoreType.DMA((1,))],
        compiler_params=_params(dimension_semantics=("arbitrary",)),
    )(x, tgt, p, wout, wa_t, wi_t, sp, ones_c, ones_l)


def _backward(p, h, dh, saved, facc, wout, wa_t, wi_t, sp, ones_c, ones_l, tb):
    t_len, d = h.shape
    nb = t_len // tb
    n_tiles, tw = wa_t.shape[0], wa_t.shape[1]
    hd_c, hd_l = d // N_CONV_HEADS, d // N_LRU_HEADS
    g_rows = 2 * n_tiles * hd_l
    s8 = SUBLANES

    def body(p_ref, h_ref, hhalo_ref, dh_ref, yc, czs, u, ra_ref, ii_ref, facc_ref,
             wout_ref, wa_ref, wi_ref, sp_ref, oc_ref, ol_ref,
             dp_ref, yt_ref, slab_v, slab_g,
             hh, dy, ybuf, rcf, rlf, qc, ql, dyc_hat, dyl_hat, dpa, dpi, du, gwa_ref, gwi_ref, acc_ref,
             car_dcz, car_a, car_g, car_du):
        i = pl.program_id(0)
        blk_idx = nb - 1 - i
        row = _row_iota(d)

        @pl.when(i == 0)
        def _():
            for ref in (car_dcz, car_a, car_g, car_du, gwa_ref, gwi_ref, acc_ref):
                ref[...] = jnp.zeros_like(ref)

        def spr(r):
            return sp_ref[r:r + 1, :]

        def proj(rows, seg):
            return p_ref[rows, seg * d:(seg + 1) * d].astype(F32)

        def put(rows, seg, halves):
            dp_ref[rows, seg * d:(seg + 1) * d] = jnp.concatenate(halves, axis=0).astype(MM)

        def acc_add(group, val):
            acc_ref[group * s8:(group + 1) * s8, :] += val

        live = jnp.where(blk_idx > 0, 1.0, 0.0).astype(F32)
        hh[0:s8, :] = hhalo_ref[...] * live
        hh[s8:, :] = h_ref[...]

        dy[...] = _dot_nt(dh_ref[...].astype(MM), wout_ref[...])

        w0, w1, w2 = spr(SP_CONV_W), spr(SP_CONV_W + 1), spr(SP_CONV_W + 2)
        l0, l1, l2, l3 = spr(SP_LRU_W), spr(SP_LRU_W + 1), spr(SP_LRU_W + 2), spr(SP_LRU_W + 3)

        rcf[...] = _head_rstd(yc[...], oc_ref[...], hd_c)
        rlf[...] = _head_rstd(h_ref[...], ol_ref[...], hd_l)

        g_c, g_l = spr(SP_CONV_G), spr(SP_LRU_G)

        def gates(r, carry):
            rows = pl.ds(r, 2 * s8)
            for (seg, off_y, src, rstd, gain, q, dhat, grp) in (
                    (P_GC, 0, yc, rcf, g_c, qc, dyc_hat, A_CONV_G),
                    (P_GL, d, h_ref, rlf, g_l, ql, dyl_hat, A_LRU_G)):
                gt = proj(rows, seg)
                sg = _sigmoid(gt)
                silu = gt * sg
                yhat = src[rows, :] * rstd[rows, :]
                nrm = yhat * gain
                ybuf[rows, off_y:off_y + d] = nrm * silu
                dout = dy[rows, off_y:off_y + d]
                dnrm = dout * silu
                dp_ref[rows, seg * d:(seg + 1) * d] = (dout * nrm * (sg * (1.0 + gt * (1.0 - sg)))).astype(MM)
                dg = dnrm * yhat
                acc_add(grp, dg[0:s8] + dg[s8:])
                dh_ = dnrm * gain
                dhat[rows, :] = dh_
                q[rows, :] = dh_ * yhat
            return carry

        _chunks(tb, 2 * s8, gates, 0)

        qc[...] = _head_sums(qc[...], oc_ref[...]) * (1.0 / hd_c)
        ql[...] = _head_sums(ql[...], ol_ref[...]) * (1.0 / hd_l)
        yt_ref[...] = ybuf[...].T.astype(MM)

        c8 = RG_LRU_C * _log_sigmoid(spr(SP_LAM))

        def conv_mixer(r, dcz_n):
            rows16 = pl.ds(r, 2 * s8)
            bg16, cg16, xc16 = proj(rows16, P_B), proj(rows16, P_C), proj(rows16, P_XC)
            z16 = cg16 * xc16
            d_b, d_c, d_x = [None, None], [None, None], [None, None]
            for j in (1, 0):
                rows, sub = pl.ds(r + j * s8, s8), slice(j * s8, (j + 1) * s8)
                rstd = rcf[rows, :]
                yhat = yc[rows, :] * rstd
                dyc = rstd * (dyc_hat[rows, :] - yhat * qc[rows, :])
                d_b[j] = dyc * czs[rows, :]
                dcz = dyc * bg16[sub]
                up1, up2 = _shift_up(dcz, dcz_n, 1, row), _shift_up(dcz, dcz_n, 2, row)
                dz = w2 * dcz + w1 * up1 + w0 * up2
                d_c[j] = dz * xc16[sub]
                d_x[j] = dz * cg16[sub]
                z = z16[sub]
                acc_add(A_CONV_W, up2 * z)
                acc_add(A_CONV_W + 1, up1 * z)
                acc_add(A_CONV_W + 2, dcz * z)
                dcz_n = dcz
            put(rows16, P_B, d_b)
            put(rows16, P_C, d_c)
            put(rows16, P_XC, d_x)
            return dcz_n

        car_dcz[...] = _chunks(tb, 2 * s8, conv_mixer, car_dcz[...], reverse=True)

        def lru_mixer(r, carry):
            a_n, g_n = carry
            for j in (1, 0):
                rows = pl.ds(r + j * s8, s8)
                rstd = rlf[rows, :]
                hcur = hh[pl.ds(r + (j + 1) * s8, s8), :]
                hhat = hcur * rstd
                dh_out = rstd * (dyl_hat[rows, :] - hhat * ql[rows, :])
                ra = ra_ref[rows, :]
                la = ra * c8
                a = jnp.exp(la)
                g = _scan_bwd(_shift_up(a, a_n, 1, row), dh_out, g_n, row)
                da = g * _shift_down(hcur, hh[pl.ds(r + j * s8, s8), :], 1, row)
                ii = ii_ref[rows, :]
                uu = u[rows, :]
                mult_sq = _lru_input_scale_sq(la, a)
                inv_mult = lax.rsqrt(mult_sq)
                dmult = g * (ii * uu)
                ds = g * (mult_sq * inv_mult)
                dla = a * (da - dmult * a * inv_mult)
                acc_add(A_LAM, dla * ra)
                dpa_ = dla * c8 * ra * (1.0 - ra)
                dpi_ = ds * uu * ii * (1.0 - ii)
                acc_add(A_B_A, dpa_)
                acc_add(A_B_I, dpi_)
                dpa[rows, :] = dpa_
                dpi[rows, :] = dpi_
                du[rows, :] = ds * ii
                a_n, g_n = a, _bcast_row(g, 0)
            return a_n, g_n

        a_f, g_f = _chunks(tb, 2 * s8, lru_mixer, (car_a[...], car_g[...]), reverse=True)
        car_a[...] = a_f
        car_g[...] = g_f

        dpab = dpa[...].astype(MM)
        dpib = dpi[...].astype(MM)
        for k in range(n_tiles):
            sl = slice(k * tw, (k + 1) * tw)
            du[:, sl] += _dot_nt(dpab[:, sl], wa_ref[k]) + _dot_nt(dpib[:, sl], wi_ref[k])
            ut = u[:, sl].T.astype(MM)
            gwa_ref[k] += _dot(ut, dpab[:, sl])
            gwi_ref[k] += _dot(ut, dpib[:, sl])

        def lru_conv(r, du_n):
            rows16 = pl.ds(r, 2 * s8)
            xl16 = proj(rows16, P_XL)
            d_xl = [None, None]
            for j in (1, 0):
                rows, sub = pl.ds(r + j * s8, s8), slice(j * s8, (j + 1) * s8)
                dut = du[rows, :]
                up1, up2, up3 = (_shift_up(dut, du_n, s, row) for s in (1, 2, 3))
                d_xl[j] = l3 * dut + l2 * up1 + l1 * up2 + l0 * up3
                xl = xl16[sub]
                acc_add(A_LRU_W, up3 * xl)
                acc_add(A_LRU_W + 1, up2 * xl)
                acc_add(A_LRU_W + 2, up1 * xl)
                acc_add(A_LRU_W + 3, dut * xl)
                acc_add(A_LRU_B, dut)
                du_n = dut
            put(rows16, P_XL, d_xl)
            return du_n

        car_du[...] = _chunks(tb, 2 * s8, lru_conv, car_du[...], reverse=True)

        @pl.when(i == nb - 1)
        def _():
            def rowsum(ref, group):
                return jnp.sum(ref[group * s8:(group + 1) * s8, :], axis=0, keepdims=True)

            slab_v[...] = jnp.zeros_like(slab_v)
            loss = jnp.sum(rowsum(facc_ref, 0), axis=1, keepdims=True) * (0.5 / d)
            rows = {SL_LOSS: jnp.broadcast_to(loss, (1, d)), SL_FINAL_G: rowsum(facc_ref, 1),
                    SL_LRU_B: rowsum(acc_ref, A_LRU_B), SL_B_A: rowsum(acc_ref, A_B_A), SL_B_I: rowsum(acc_ref, A_B_I),
                    SL_LAM: rowsum(acc_ref, A_LAM), SL_CONV_G: rowsum(acc_ref, A_CONV_G), SL_LRU_G: rowsum(acc_ref, A_LRU_G)}
            for k in range(3):
                rows[SL_CONV_W + k] = rowsum(acc_ref, A_CONV_W + k)
            for k in range(4):
                rows[SL_LRU_W + k] = rowsum(acc_ref, A_LRU_W + k)
            for r, val in rows.items():
                slab_v[r:r + 1, :] = val
            head_of_lane = lax.broadcasted_iota(jnp.int32, (hd_l, tw), 1) // hd_l
            for mtx, g_ref in enumerate((gwa_ref, gwi_ref)):
                for k in range(n_tiles):
                    packed = jnp.zeros((hd_l, tw), F32)
                    for a in range(tw // hd_l):
                        packed = jnp.where(head_of_lane == a, g_ref[k, a * hd_l:(a + 1) * hd_l, :], packed)
                    slab_g[(mtx * n_tiles + k) * hd_l:(mtx * n_tiles + k + 1) * hd_l, :] = packed.astype(MM)

    vm = pl.BlockSpec(memory_space=pltpu.VMEM)
    rev = lambda w: pl.BlockSpec((tb, w), lambda i: (nb - 1 - i, 0))
    halo = lambda rows, w: pl.BlockSpec((rows, w), lambda i: (jnp.maximum((nb - 1 - i) * (tb // rows) - 1, 0), 0))
    const = lambda shape: pl.BlockSpec(shape, lambda i: (0,) * len(shape))
    buf = lambda w: pltpu.VMEM((tb, w), F32)
    car = pltpu.VMEM((SUBLANES, d), F32)
    return pl.pallas_call(
        body, name="backward", grid=(nb,),
        in_specs=[rev(6 * d), rev(d), halo(SUBLANES, d), rev(d)] + [rev(d)] * 5 + [vm, vm, vm, vm, vm, vm, vm],
        out_specs=(rev(6 * d), pl.BlockSpec((2 * d, tb), lambda i: (0, nb - 1 - i)),
                   const((SL_ROWS, d)), const((g_rows, tw))),
        out_shape=(jax.ShapeDtypeStruct((t_len, 6 * d), MM),
                   jax.ShapeDtypeStruct((2 * d, t_len), MM),
                   jax.ShapeDtypeStruct((SL_ROWS, d), F32),
                   jax.ShapeDtypeStruct((g_rows, tw), MM)),
        scratch_shapes=[pltpu.VMEM((SUBLANES + tb, d), F32), buf(2 * d), buf(2 * d)] + [buf(d)] * 9
                       + [pltpu.VMEM((n_tiles, tw, tw), F32), pltpu.VMEM((n_tiles, tw, tw), F32),
                          pltpu.VMEM((A_GROUPS * SUBLANES, d), F32), car, car, car, car],
        compiler_params=_params(dimension_semantics=("arbitrary",)),
    )(p, h, h, dh, *saved, facc, wout, wa_t, wi_t, sp, ones_c, ones_l)


def _input_grad(dp, win_all, x, dh, sp, part, tb):
    t_len, d = x.shape
    nb = t_len // tb
    cols = win_all.shape[2]
    mid = min(nb - 1, (3 * nb) // 8)
    late = min(mid, nb // 4)
    rc = 32

    def body(dp_ref, win_ref, x_ref, dh_ref, sp_ref, part_ref, gx_ref, ln_ref, direct, relayed,
             send_sems, recv_sems, local_sems, acc_ref, ln_all, ln_send, ln_recv, mine, theirs):
        i = pl.program_id(0)
        x_, y_, c_ = _mesh_pos()
        first, second = 1 - c_, c_
        nbr1 = (x_ ^ c_, y_ ^ (1 - c_), c_)
        nbr2 = (x_ ^ (1 - c_), y_ ^ c_, c_)

        def remote(src, dst, k, to):
            return pltpu.make_async_remote_copy(src_ref=src, dst_ref=dst, send_sem=send_sems.at[k], recv_sem=recv_sems.at[k],
                                                device_id=to, device_id_type=MESH)

        to_first = [remote(part_ref.at[first], direct, 0, nbr1), remote(part_ref.at[2], theirs, 1, nbr1)]
        to_second = remote(theirs, relayed, 2, nbr2)
        load_mine = pltpu.make_async_copy(part_ref.at[second], mine, local_sems.at[0])

        @pl.when(i == 0)
        def _():
            acc_ref[...] = jnp.zeros_like(acc_ref)
            to_first[1].start()
            load_mine.start()

        @pl.when(i == late)
        def _():
            to_first[0].start()

        dxn = _dot_nt(dp_ref[:, 0:cols], win_ref[0])
        for j in range(1, N_DEV):
            dxn += _dot_nt(dp_ref[:, j * cols:(j + 1) * cols], win_ref[j])
        xv = x_ref[...]
        r0 = lax.rsqrt(jnp.mean(xv * xv, axis=-1, keepdims=True) + RMS_EPS)
        xhat = xv * r0
        acc_ref[...] += (dxn * xhat).reshape(tb // SUBLANES, SUBLANES, d).sum(axis=0)
        dxh = dxn * sp_ref[SP_LN_G:SP_LN_G + 1, :]
        gx_ref[...] = dh_ref[...] + r0 * (dxh - xhat * jnp.mean(dxh * xhat, axis=-1, keepdims=True))

        @pl.when(i == mid)
        def _():
            to_first[1].wait_recv()
            load_mine.wait()

            def add(r, carry):
                rows = pl.ds(r, rc)
                theirs[rows, :] = (mine[rows, :].astype(F32) + theirs[rows, :].astype(F32)).astype(MM)
                return carry

            _chunks(mine.shape[0], rc, add, 0)
            to_second.start()

        @pl.when(i == nb - 1)
        def _():
            to_first[0].wait_recv()
            to_second.wait_recv()
            for cp in to_first + [to_second]:
                cp.wait_send()
            ln_all[4 * x_ + 2 * y_ + c_] = jnp.broadcast_to(jnp.sum(acc_ref[...], axis=0, keepdims=True), acc_ref.shape)
            gather = _Gather(lambda a, px, py, pc: ln_all.at[4 * px + 2 * py + pc], ln_send, ln_recv)
            gather.start_own(0)
            gather.finish(0)
            total = ln_all[0]
            for dev in range(1, N_DEV):
                total = total + ln_all[dev]
            ln_ref[...] = total

    vm = pl.BlockSpec(memory_space=pltpu.VMEM)
    hbm = pl.BlockSpec(memory_space=pl.ANY)
    blk = lambda w: pl.BlockSpec((tb, w), lambda i: (i, 0))
    landed = jax.ShapeDtypeStruct(part.shape[1:], part.dtype)
    outs = pl.pallas_call(
        body, name="input_grad", grid=(nb,),
        in_specs=[blk(6 * d), vm, blk(d), blk(d), vm, hbm],
        out_specs=(blk(d), pl.BlockSpec((SUBLANES, d), lambda i: (0, 0)), hbm, hbm),
        out_shape=(jax.ShapeDtypeStruct((t_len, d), F32), jax.ShapeDtypeStruct((SUBLANES, d), F32), landed, landed),
        scratch_shapes=[pltpu.SemaphoreType.DMA((3,)), pltpu.SemaphoreType.DMA((3,)), pltpu.SemaphoreType.DMA((1,)),
                        pltpu.VMEM((SUBLANES, d), F32), pltpu.VMEM((N_DEV, SUBLANES, d), F32),
                        pltpu.SemaphoreType.DMA((7,)), pltpu.SemaphoreType.DMA((7,)),
                        pltpu.VMEM(part.shape[1:], MM), pltpu.VMEM(part.shape[1:], MM)],
        compiler_params=_params(dimension_semantics=("arbitrary",)),
    )(dp, win_all, x, dh, sp, part)
    return outs[0], outs[1], (outs[2], outs[3])


_CHIP_RELATIONS = [(0, 0), (1, 0), (0, 1), (1, 1)]


def _related_block(k, core):
    x, y, _ = _mesh_pos()
    fx, fy = _CHIP_RELATIONS[k]
    return 4 * (x ^ fx) + 2 * (y ^ fy) + core


class _ChipExchange:
    def __init__(self, part_refs, land_refs, send_sems, recv_sems):
        self.part_refs, self.land_refs, self.send_sems, self.recv_sems = part_refs, land_refs, send_sems, recv_sems

    def copies(self):
        x, y, c = _mesh_pos()
        for a in range(len(self.part_refs)):
            for k in (1, 2, 3):
                fx, fy = _CHIP_RELATIONS[k]
                yield pltpu.make_async_remote_copy(
                    src_ref=self.part_refs[a].at[k - 1], dst_ref=self.land_refs[a].at[k - 1],
                    send_sem=self.send_sems.at[3 * a + k - 1], recv_sem=self.recv_sems.at[3 * a + k - 1],
                    device_id=(x ^ fx, y ^ fy, c), device_id_type=MESH)

    def start(self):
        for cp in self.copies():
            cp.start()

    def finish(self):
        for cp in self.copies():
            cp.wait_recv()
        for cp in self.copies():
            cp.wait_send()


def _weight_grad_stage1(name, blk_shape, n_split, operands, in_specs, product, riders=(), slabs=()):
    n_rows, n_cols = blk_shape
    rs = n_rows // n_split
    rc = 32
    n_in, n_ride, n_slab = len(operands), len(riders), len(slabs)
    _, _, c = _mesh_pos()
    order = jnp.stack([_related_block(k, 1 - c) for k in range(4)]
                      + [_related_block(k, c) for k in (1, 2, 3, 0)]).astype(jnp.int32)

    def body(order_ref, *refs):
        ins = refs[:n_in]
        ride_in = refs[n_in:n_in + n_ride]
        slab_in = refs[n_in + n_ride:n_in + n_ride + n_slab]
        n_op = n_in + n_ride + n_slab
        part_ref, own_ref = refs[n_op:n_op + 2]
        ride_out = refs[n_op + 2:n_op + 2 + n_ride]
        gathered = refs[n_op + 2 + n_ride:n_op + 2 + n_ride + n_slab]
        (gbuf, sendbuf, from_sib, send_sems, recv_sems, ride_send, ride_recv,
         slab_send, slab_recv, slab_local) = refs[n_op + 2 + n_ride + n_slab:]
        exchange = _ChipExchange(ride_in, ride_out, ride_send, ride_recv)
        s = pl.program_id(0)
        x, y, c = _mesh_pos()
        me = 4 * x + 2 * y + c
        gather = _BalancedGather(lambda a, px, py, pc: gathered[a].at[4 * px + 2 * py + pc], slab_send, slab_recv, slab_in)
        keep_own = [pltpu.make_async_copy(slab_in[a], gathered[a].at[me], slab_local.at[a]) for a in range(n_slab)]

        def to_sibling(k):
            return pltpu.make_async_remote_copy(
                src_ref=sendbuf.at[k], dst_ref=from_sib.at[k], send_sem=send_sems.at[k], recv_sem=recv_sems.at[k],
                device_id=(x, y, 1 - c), device_id_type=MESH)

        @pl.when(s == 0)
        def _():
            exchange.start()
            for a in range(n_slab):
                gather.start_own(a)
                keep_own[a].start()

        @pl.when(s == 5)
        def _():
            for a in range(n_slab):
                gather.on_neighbour(a, 0)
                gather.on_neighbour(a, 1)

        @pl.when(s == 7)
        def _():
            for a in range(n_slab):
                gather.on_diagonal(a)

        for h in range(n_split):
            gbuf[h * rs:(h + 1) * rs, :] = product(ins, h)

        @pl.when(s < 4)
        def _():
            def narrow(r, carry):
                sendbuf[s, pl.ds(r, rc), :] = gbuf[pl.ds(r, rc), :].astype(MM)
                return carry

            _chunks(n_rows, rc, narrow, 0)
            to_sibling(s).start()

        @pl.when(s >= 4)
        def _():
            k = jnp.where(s == 7, 0, s - 3)
            to_sibling(k).wait_recv()

            @pl.when(s < 7)
            def _():
                def add(r, carry):
                    rows = pl.ds(r, rc)
                    part_ref[0, rows, :] = (gbuf[rows, :] + from_sib[k, rows, :].astype(F32)).astype(MM)
                    return carry

                _chunks(n_rows, rc, add, 0)

            @pl.when(s == 7)
            def _():
                def add(r, carry):
                    rows = pl.ds(r, rc)
                    own_ref[rows, :] = gbuf[rows, :] + from_sib[0, rows, :].astype(F32)
                    return carry

                _chunks(n_rows, rc, add, 0)
                for kk in range(4):
                    to_sibling(kk).wait_send()
                exchange.finish()
                for a in range(n_slab):
                    gather.wait_sibling(a)
                    for j in range(3):
                        gather.wait_passed_on(a, j)
                    gather.wait_sends(a)
                    keep_own[a].wait()

    hbm = pl.BlockSpec(memory_space=pl.ANY)
    grid_spec = pltpu.PrefetchScalarGridSpec(
        num_scalar_prefetch=1, grid=(N_DEV,), in_specs=list(in_specs) + [hbm] * (n_ride + n_slab),
        out_specs=(pl.BlockSpec((1, n_rows, n_cols), lambda s, o: (jnp.clip(s - 4, 0, 2), 0, 0)),
                   pl.BlockSpec((n_rows, n_cols), lambda s, o: (0, 0))) + (hbm,) * (n_ride + n_slab),
        scratch_shapes=[pltpu.VMEM((n_rows, n_cols), F32), pltpu.VMEM((4, n_rows, n_cols), MM),
                        pltpu.VMEM((4, n_rows, n_cols), MM),
                        pltpu.SemaphoreType.DMA((4,)), pltpu.SemaphoreType.DMA((4,)),
                        pltpu.SemaphoreType.DMA((max(3 * n_ride, 1),)), pltpu.SemaphoreType.DMA((max(3 * n_ride, 1),)),
                        pltpu.SemaphoreType.DMA((max(8 * n_slab, 1),)), pltpu.SemaphoreType.DMA((max(8 * n_slab, 1),)),
                        pltpu.SemaphoreType.DMA((max(n_slab, 1),))])
    outs = pl.pallas_call(
        body, name=name, grid_spec=grid_spec,
        out_shape=(jax.ShapeDtypeStruct((3, n_rows, n_cols), MM), jax.ShapeDtypeStruct((n_rows, n_cols), F32))
                  + tuple(jax.ShapeDtypeStruct(p.shape, p.dtype) for p in riders)
                  + tuple(jax.ShapeDtypeStruct((N_DEV,) + a.shape, a.dtype) for a in slabs),
        compiler_params=_params(dimension_semantics=("arbitrary",)),
    )(order, *operands, *riders, *slabs)
    return outs[0], outs[1], outs[2:2 + n_ride], outs[2 + n_ride:]


def _weight_grad_in(xnt, dp, riders, slabs):
    d, t_len = xnt.shape
    cols = dp.shape[1] // N_DEV
    half = d // 2
    return _weight_grad_stage1(
        "weight_grad_in", (d, cols), 2, (xnt, dp),
        [pl.BlockSpec(memory_space=pltpu.VMEM), pl.BlockSpec((t_len, cols), lambda s, o: (0, o[s]))],
        lambda refs, h: _dot(refs[0][h * half:(h + 1) * half, :], refs[1][...]), riders, slabs)


def _weight_grad_out(yt, dhb):
    d2, t_len = yt.shape
    d = dhb.shape[1]
    rows = d2 // N_DEV
    return _weight_grad_stage1(
        "weight_grad_out", (rows, d), 1, (yt, dhb),
        [pl.BlockSpec((rows, t_len), lambda s, o: (o[s], 0)), pl.BlockSpec(memory_space=pltpu.VMEM)],
        lambda refs, h: _dot(refs[0][...], refs[1][...]))


def _update_shard(own, others, w, m, v, name):
    n_rows, n_cols = w.shape
    rb = min(256, n_rows)
    n_other = len(others)

    def body(own_ref, *refs):
        other_refs = refs[:n_other]
        w_ref, m_ref, v_ref, grad_ref, delta_ref, mo_ref, vo_ref = refs[n_other:]
        g = own_ref[...]
        for ref in other_refs:
            for k in range(ref.shape[0] if len(ref.shape) == 3 else 1):
                g = g + (ref[k] if len(ref.shape) == 3 else ref[...]).astype(F32)
        delta, m_new, v_new = _adamw(w_ref[...], g, m_ref[...], v_ref[...])
        grad_ref[...] = g
        delta_ref[...] = delta
        mo_ref[...] = m_new
        vo_ref[...] = v_new

    blk = pl.BlockSpec((rb, n_cols), lambda i: (i, 0))
    stacked = lambda n: pl.BlockSpec((n, rb, n_cols), lambda i: (0, i, 0))
    out = jax.ShapeDtypeStruct((n_rows, n_cols), F32)
    return pl.pallas_call(
        body, name=name, grid=(n_rows // rb,),
        in_specs=[blk] + [stacked(o.shape[0]) if o.ndim == 3 else blk for o in others] + [blk, blk, blk],
        out_specs=(blk, blk, blk, blk), out_shape=(out, out, out, out),
        compiler_params=_params(dimension_semantics=("arbitrary",)),
    )(own, *others, w, m, v)


def _small_update(gat_v, gat_g, ln_tot, vec_w, vec_m, vec_v, gates, convs):
    n_vec = len(vec_w)
    n_heads, hd, _ = gates[0].shape
    tw = gat_g.shape[2]
    s8 = SUBLANES
    per = tw // hd
    n_tiles = n_heads // per
    cc = convs[0].shape[1]
    n_in = 3 + 3 * n_vec + 12

    def body(*refs):
        gv_ref, gg_ref, ln_ref = refs[:3]
        w_refs, m_refs, v_refs = (refs[3 + j * n_vec:3 + (j + 1) * n_vec] for j in range(3))
        gate_refs = refs[3 + 3 * n_vec:3 + 3 * n_vec + 6]
        conv_refs = refs[3 + 3 * n_vec + 6:n_in]
        loss_o = refs[n_in]
        kinds = [refs[n_in + 1 + j * (n_vec + 4):n_in + 1 + (j + 1) * (n_vec + 4)] for j in range(4)]
        tv, tg = refs[n_in + 1 + 4 * (n_vec + 4):]
        x, y, c = _mesh_pos()
        me = 4 * x + 2 * y + c

        def emit(k_out, w, g, m, v):
            delta, m_new, v_new = _adamw(w, g, m, v)
            for ref, val in zip(k_out, (g, delta, m_new, v_new)):
                ref[...] = val

        total = gv_ref[0]
        for dev in range(1, N_DEV):
            total = total + gv_ref[dev]
        tv[...] = total
        tv[SL_LN_G:SL_LN_G + 1, :] = ln_ref[0:1, :]

        def sum_gates(r, carry):
            rows = pl.ds(r, 2 * s8)
            part = gg_ref[0, rows, :].astype(F32)
            for dev in range(1, N_DEV):
                part = part + gg_ref[dev, rows, :].astype(F32)
            tg[rows, :] = part
            return carry

        _chunks(tg.shape[0], 2 * s8, sum_gates, 0)
        loss_o[...] = jnp.broadcast_to(tv[SL_LOSS:SL_LOSS + 1, 0:LANES], loss_o.shape)
        for p in range(n_vec):
            w, g = w_refs[p][...], tv[SL_LN_G + p, :]
            if SL_LN_G + p == SL_LAM:
                g = g * (RG_LRU_C * jax.nn.sigmoid(-w))
            emit([k_out[p] for k_out in kinds], w, g, m_refs[p][...], v_refs[p][...])
        lanes = pl.ds(pl.multiple_of(me * cc, cc), cc)
        for j, (row0, n) in enumerate(((SL_CONV_W, 3), (SL_LRU_W, 4))):
            w_ref, m_ref, v_ref = conv_refs[3 * j:3 * j + 3]
            emit([k_out[n_vec + 2 + j] for k_out in kinds], w_ref[...], tv[row0:row0 + n, lanes], m_ref[...], v_ref[...])
        for mtx in range(2):
            w_ref, m_ref, v_ref = gate_refs[3 * mtx:3 * mtx + 3]
            for k in range(n_tiles):
                tile = tg[(mtx * n_tiles + k) * hd:(mtx * n_tiles + k + 1) * hd, :]
                for a in range(per):
                    head = k * per + a
                    g = tile[:, a * hd:(a + 1) * hd]
                    delta, m_new, v_new = _adamw(w_ref[head], g, m_ref[head], v_ref[head])
                    for k_out, val in zip(kinds, (g, delta, m_new, v_new)):
                        k_out[n_vec + mtx][head] = val

    vm = pl.BlockSpec(memory_space=pltpu.VMEM)
    like = lambda a: jax.ShapeDtypeStruct(a.shape, F32)
    per_kind = tuple(like(a) for a in vec_w) + (like(gates[0]), like(gates[3]), like(convs[0]), like(convs[3]))
    n_out = 1 + 4 * len(per_kind)
    outs = pl.pallas_call(
        body, name="small_update",
        in_specs=[vm] * n_in, out_specs=(vm,) * n_out,
        out_shape=(jax.ShapeDtypeStruct((SUBLANES, LANES), F32),) + per_kind * 4,
        scratch_shapes=[pltpu.VMEM(gat_v.shape[1:], F32), pltpu.VMEM(gat_g.shape[1:], F32)],
        compiler_params=_params(),
    )(gat_v, gat_g, ln_tot, *vec_w, *vec_m, *vec_v, *gates, *convs)
    return outs[0], [outs[1 + j * len(per_kind):1 + (j + 1) * len(per_kind)] for j in range(4)]


def _head_ones(head_dim, tw):
    lane = jnp.arange(tw) // head_dim
    return (lane[:, None] == lane[None, :]).astype(MM)


def kernel(x, ln_g, w_in, conv_w, lru_conv_w, lru_conv_b, w_a, b_a, w_i, b_i, lam, conv_out_g, lru_out_g, w_out, final_g, loss_target, m_ln_g, m_w_in, m_conv_w, m_lru_conv_w, m_lru_conv_b, m_w_a, m_b_a, m_w_i, m_b_i, m_lam, m_conv_out_g, m_lru_out_g, m_w_out, m_final_g, v_ln_g, v_w_in, v_conv_w, v_lru_conv_w, v_lru_conv_b, v_w_a, v_b_a, v_w_i, v_b_i, v_lam, v_conv_out_g, v_lru_out_g, v_w_out, v_final_g):
    _, t_len, d = x.shape
    hd_l = d // N_LRU_HEADS
    tw = min(MXU_TILE, d)
    x2, tgt2 = x[0], loss_target[0]

    small = [ln_g, lru_conv_b, b_a, b_i, lam, conv_out_g, lru_out_g, final_g]
    p, xnt, win_all, wout_all, _, sp, wa_t, wi_t = _gather_project(
        x2, w_in, w_out, conv_w, lru_conv_w, ln_g.reshape(1, d), w_a, w_i, small, min(256, t_len), tw)
    wout_full = wout_all.reshape(N_DEV * w_out.shape[0], d)
    ones_c, ones_l = _head_ones(d // N_CONV_HEADS, tw), _head_ones(hd_l, tw)

    h, dh, dhb, facc, *saved = _forward(x2, tgt2, p, wout_full, wa_t, wi_t, sp, ones_c, ones_l, min(256, t_len))
    dp, yt, slab_v, slab_g = _backward(p, h, dh, saved, facc, wout_full, wa_t, wi_t, sp, ones_c, ones_l, min(256, t_len))
    part_out, own_out, _, _ = _weight_grad_out(yt, dhb)
    part_in, own_in, (chips_out,), (gat_v, gat_g) = _weight_grad_in(xnt, dp, (part_out,), (slab_v, slab_g))
    grad_x, ln_tot, sums_in = _input_grad(dp, win_all, x2, dh, sp, part_in, min(512, t_len))
    gw_in, dw_in, mw_in, vw_in = _update_shard(own_in, sums_in, w_in, m_w_in, v_w_in, "update_w_in")
    gw_out, dw_out, mw_out, vw_out = _update_shard(own_out, (chips_out,), w_out, m_w_out, v_w_out, "update_w_out")

    loss_tile, kinds = _small_update(
        gat_v, gat_g, ln_tot, small,
        [m_ln_g, m_lru_conv_b, m_b_a, m_b_i, m_lam, m_conv_out_g, m_lru_out_g, m_final_g],
        [v_ln_g, v_lru_conv_b, v_b_a, v_b_i, v_lam, v_conv_out_g, v_lru_out_g, v_final_g],
        (w_a, m_w_a, v_w_a, w_i, m_w_i, v_w_i), (conv_w, m_conv_w, v_conv_w, lru_conv_w, m_lru_conv_w, v_lru_conv_w))

    def unpack(kind, big_in, big_out):
        vec, (wa_, wi_, cw_, lw_) = kind[:len(small)], kind[len(small):]
        return [vec[0], big_in, cw_, lw_, vec[1], wa_, vec[2], wi_, vec[3], vec[4], vec[5], vec[6], big_out, vec[7]]

    return (loss_tile[0, 0], grad_x[None], *unpack(kinds[0], gw_in, gw_out), *unpack(kinds[1], dw_in, dw_out),
            *unpack(kinds[2], mw_in, mw_out), *unpack(kinds[3], vw_in, vw_out))
```
